```python
import jax, jax.numpy as jnp
from jax import lax
import numpy as np

D_MODEL = 1024
BATCH = 8
SEQ = 8192
DEPTH = 1

CHUNK = 64
D_MIX = D_MODEL
LRU_WIDTH = D_MIX // 2
LRU_HEADS = 8
LRU_HEAD_DIM = LRU_WIDTH // LRU_HEADS
CONV_WIDTH = 4
LRU_C = 8.0
POOL_WIDTH = D_MIX - LRU_WIDTH
POOL_WINDOWS = (2, 4, 8, 16)
POOL_GROUPS = len(POOL_WINDOWS)
POOL_GROUP_DIM = POOL_WIDTH // POOL_GROUPS
D_IN = 2 * LRU_WIDTH + POOL_WIDTH
D_FF = ((8 * D_MODEL // 3 + 255) // 256) * 256
EPS = 1e-6

kernel_name = "hybrid_rglru_multiscale_pool_swiglu"


def rmsnorm(x, g):
    xf = x.astype(jnp.float32)
    return xf * lax.rsqrt(jnp.mean(xf * xf, axis=-1, keepdims=True) + EPS) * g.astype(jnp.float32)


def causal_depthwise_conv(x, w, b):
    S = x.shape[1]
    xp = jnp.pad(x, ((0, 0), (CONV_WIDTH - 1, 0), (0, 0)))
    out = b
    for k in range(CONV_WIDTH):
        out = out + xp[:, k:k + S, :] * w[k]
    return out


def rg_lru(x, wa, ba, wx, bx, lam):
    B, S, _ = x.shape
    xh = x.reshape(B, S, LRU_HEADS, LRU_HEAD_DIM)
    r = jax.nn.sigmoid(jnp.einsum('bshi,hij->bshj', xh, wa).reshape(B, S, LRU_WIDTH) + ba)
    i = jax.nn.sigmoid(jnp.einsum('bshi,hij->bshj', xh, wx).reshape(B, S, LRU_WIDTH) + bx)
    log_a = -LRU_C * r * jax.nn.softplus(-lam)
    a = jnp.exp(log_a)
    mult = jnp.sqrt(jnp.maximum(-jnp.expm1(2.0 * log_a), 1e-12))
    b = mult * (i * x)

    def combine(left, right):
        a1, b1 = left
        a2, b2 = right
        return a1 * a2, a2 * b1 + b2

    _, h = lax.associative_scan(combine, (a, b), axis=1)
    return h


def trailing_mean(x, w):
    S = x.shape[1]
    cs = jnp.cumsum(jnp.pad(x, ((0, 0), (w, 0), (0, 0))), axis=1)
    window_sum = cs[:, w:, :] - cs[:, :S, :]
    count = jnp.minimum(jnp.arange(1, S + 1), w).astype(jnp.float32)[None, :, None]
    return window_sum / count


def pool_mixer(u, pool_w, pool_b, pool_scale):
    B, S, _ = u.shape
    ug = u.reshape(B, S, POOL_GROUPS, POOL_GROUP_DIM)
    pooled = jnp.stack(
        [trailing_mean(ug[:, :, g, :], w) - ug[:, :, g, :] for g, w in enumerate(POOL_WINDOWS)],
        axis=2)
    y = jnp.einsum('bsgc,gcd->bsgd', pooled, pool_w).reshape(B, S, POOL_WIDTH) + pool_b
    return y * pool_scale


def _fwd_setup_inputs(seed: int = 0) -> dict:
    key = jax.random.key(seed)
    ks = jax.random.split(key, 24)
    f32 = jnp.float32
    nrm = lambda k, shape, fan_in: jax.random.normal(k, shape, f32) * (fan_in ** -0.5)
    gain = lambda k, shape: 1.0 + 0.02 * jax.random.normal(k, shape, f32)
    small = lambda k, shape: 0.01 * jax.random.normal(k, shape, f32)
    u = jax.random.uniform(ks[8], (DEPTH, LRU_WIDTH), f32, minval=0.9, maxval=0.999)
    a0 = u ** (1.0 / LRU_C)
    lam = jnp.log(a0) - jnp.log1p(-a0)
    return {
        "x": jax.random.normal(ks[0], (BATCH, SEQ, D_MODEL), f32),
        "norm_mix_g": gain(ks[1], (DEPTH, D_MODEL)),
        "w_in": nrm(ks[2], (DEPTH, D_MODEL, D_IN), D_MODEL),
        "conv_w": nrm(ks[3], (DEPTH, CONV_WIDTH, LRU_WIDTH), CONV_WIDTH),
        "conv_b": small(ks[4], (DEPTH, LRU_WIDTH)),
        "gate_a_w": nrm(ks[5], (DEPTH, LRU_HEADS, LRU_HEAD_DIM, LRU_HEAD_DIM), LRU_HEAD_DIM),
        "gate_a_b": small(ks[6], (DEPTH, LRU_WIDTH)),
        "gate_x_w": nrm(ks[7], (DEPTH, LRU_HEADS, LRU_HEAD_DIM, LRU_HEAD_DIM), LRU_HEAD_DIM),
        "gate_x_b": small(ks[9], (DEPTH, LRU_WIDTH)),
        "lru_lambda": lam,
        "pool_w": nrm(ks[10], (DEPTH, POOL_GROUPS, POOL_GROUP_DIM, POOL_GROUP_DIM), POOL_GROUP_DIM),
        "pool_b": small(ks[11], (DEPTH, POOL_WIDTH)),
        "pool_scale": gain(ks[12], (DEPTH, POOL_WIDTH)),
        "norm_lru_g": gain(ks[13], (DEPTH, LRU_WIDTH)),
        "norm_pool_g": gain(ks[14], (DEPTH, POOL_WIDTH)),
        "w_out": nrm(ks[15], (DEPTH, D_MIX, D_MODEL), D_MIX),
        "norm_ffn_g": gain(ks[16], (DEPTH, D_MODEL)),
        "ffn_w1": nrm(ks[17], (DEPTH, D_MODEL, D_FF), D_MODEL),
        "ffn_w3": nrm(ks[18], (DEPTH, D_MODEL, D_FF), D_MODEL),
        "ffn_w2": nrm(ks[19], (DEPTH, D_FF, D_MODEL), D_FF),
        "final_norm_g": gain(ks[20], (D_MODEL,)),
    }


def _fwd_reference(x, norm_mix_g, w_in, conv_w, conv_b, gate_a_w, gate_a_b, gate_x_w, gate_x_b,
              lru_lambda, pool_w, pool_b, pool_scale, norm_lru_g, norm_pool_g, w_out,
              norm_ffn_g, ffn_w1, ffn_w3, ffn_w2, final_norm_g):
    out_dtype = x.dtype
    h_res = x.astype(jnp.float32)
    for l in range(DEPTH):
        h = rmsnorm(h_res, norm_mix_g[l])
        u = h @ w_in[l].astype(jnp.float32)
        u_lru = u[..., :LRU_WIDTH]
        u_gate = u[..., LRU_WIDTH:2 * LRU_WIDTH]
        u_pool = u[..., 2 * LRU_WIDTH:]
        xc = causal_depthwise_conv(u_lru, conv_w[l], conv_b[l])
        y_lru = rg_lru(xc, gate_a_w[l], gate_a_b[l], gate_x_w[l], gate_x_b[l], lru_lambda[l])
        y_lru = y_lru * jax.nn.gelu(u_gate, approximate=True)
        y_pool = pool_mixer(u_pool, pool_w[l], pool_b[l], pool_scale[l])
        y = jnp.concatenate([rmsnorm(y_lru, norm_lru_g[l]), rmsnorm(y_pool, norm_pool_g[l])], axis=-1)
        h_res = h_res + y @ w_out[l].astype(jnp.float32)
        h = rmsnorm(h_res, norm_ffn_g[l])
        ff = jax.nn.silu(h @ ffn_w1[l].astype(jnp.float32)) * (h @ ffn_w3[l].astype(jnp.float32))
        h_res = h_res + ff @ ffn_w2[l].astype(jnp.float32)
    return rmsnorm(h_res, final_norm_g).astype(out_dtype)


import jax as _jax
import jax.numpy as _jnp

TWIN_FORMAT = 'train_step'
FWD_PARAMS = ['x', 'norm_mix_g', 'w_in', 'conv_w', 'conv_b', 'gate_a_w', 'gate_a_b', 'gate_x_w', 'gate_x_b', 'lru_lambda', 'pool_w', 'pool_b', 'pool_scale', 'norm_lru_g', 'norm_pool_g', 'w_out', 'norm_ffn_g', 'ffn_w1', 'ffn_w3', 'ffn_w2', 'final_norm_g']
TWIN_WEIGHTS = ['norm_mix_g', 'w_in', 'conv_w', 'conv_b', 'gate_a_w', 'gate_a_b', 'gate_x_w', 'gate_x_b', 'lru_lambda', 'pool_w', 'pool_b', 'pool_scale', 'norm_lru_g', 'norm_pool_g', 'w_out', 'norm_ffn_g', 'ffn_w1', 'ffn_w3', 'ffn_w2', 'final_norm_g']
TWIN_DIFF_INPUT = 'x'
TWIN_INPUTS = ['x', 'norm_mix_g', 'w_in', 'conv_w', 'conv_b', 'gate_a_w', 'gate_a_b', 'gate_x_w', 'gate_x_b', 'lru_lambda', 'pool_w', 'pool_b', 'pool_scale', 'norm_lru_g', 'norm_pool_g', 'w_out', 'norm_ffn_g', 'ffn_w1', 'ffn_w3', 'ffn_w2', 'final_norm_g', 'loss_target', 'm_norm_mix_g', 'm_w_in', 'm_conv_w', 'm_conv_b', 'm_gate_a_w', 'm_gate_a_b', 'm_gate_x_w', 'm_gate_x_b', 'm_lru_lambda', 'm_pool_w', 'm_pool_b', 'm_pool_scale', 'm_norm_lru_g', 'm_norm_pool_g', 'm_w_out', 'm_norm_ffn_g', 'm_ffn_w1', 'm_ffn_w3', 'm_ffn_w2', 'm_final_norm_g', 'v_norm_mix_g', 'v_w_in', 'v_conv_w', 'v_conv_b', 'v_gate_a_w', 'v_gate_a_b', 'v_gate_x_w', 'v_gate_x_b', 'v_lru_lambda', 'v_pool_w', 'v_pool_b', 'v_pool_scale', 'v_norm_lru_g', 'v_norm_pool_g', 'v_w_out', 'v_norm_ffn_g', 'v_ffn_w1', 'v_ffn_w3', 'v_ffn_w2', 'v_final_norm_g']
TWIN_OUTPUTS = ['loss', 'grad_x', 'grad_norm_mix_g', 'grad_w_in', 'grad_conv_w', 'grad_conv_b', 'grad_gate_a_w', 'grad_gate_a_b', 'grad_gate_x_w', 'grad_gate_x_b', 'grad_lru_lambda', 'grad_pool_w', 'grad_pool_b', 'grad_pool_scale', 'grad_norm_lru_g', 'grad_norm_pool_g', 'grad_w_out', 'grad_norm_ffn_g', 'grad_ffn_w1', 'grad_ffn_w3', 'grad_ffn_w2', 'grad_final_norm_g', 'delta_norm_mix_g', 'delta_w_in', 'delta_conv_w', 'delta_conv_b', 'delta_gate_a_w', 'delta_gate_a_b', 'delta_gate_x_w', 'delta_gate_x_b', 'delta_lru_lambda', 'delta_pool_w', 'delta_pool_b', 'delta_pool_scale', 'delta_norm_lru_g', 'delta_norm_pool_g', 'delta_w_out', 'delta_norm_ffn_g', 'delta_ffn_w1', 'delta_ffn_w3', 'delta_ffn_w2', 'delta_final_norm_g', 'new_m_norm_mix_g', 'new_m_w_in', 'new_m_conv_w', 'new_m_conv_b', 'new_m_gate_a_w', 'new_m_gate_a_b', 'new_m_gate_x_w', 'new_m_gate_x_b', 'new_m_lru_lambda', 'new_m_pool_w', 'new_m_pool_b', 'new_m_pool_scale', 'new_m_norm_lru_g', 'new_m_norm_pool_g', 'new_m_w_out', 'new_m_norm_ffn_g', 'new_m_ffn_w1', 'new_m_ffn_w3', 'new_m_ffn_w2', 'new_m_final_norm_g', 'new_v_norm_mix_g', 'new_v_w_in', 'new_v_conv_w', 'new_v_conv_b', 'new_v_gate_a_w', 'new_v_gate_a_b', 'new_v_gate_x_w', 'new_v_gate_x_b', 'new_v_lru_lambda', 'new_v_pool_w', 'new_v_pool_b', 'new_v_pool_scale', 'new_v_norm_lru_g', 'new_v_norm_pool_g', 'new_v_w_out', 'new_v_norm_ffn_g', 'new_v_ffn_w1', 'new_v_ffn_w3', 'new_v_ffn_w2', 'new_v_final_norm_g']
TWIN_LEAF_KINDS = {'loss': 'loss', 'grad_x': 'grad_x', 'grad_norm_mix_g': 'grad_w', 'grad_w_in': 'grad_w', 'grad_conv_w': 'grad_w', 'grad_conv_b': 'grad_w', 'grad_gate_a_w': 'grad_w', 'grad_gate_a_b': 'grad_w', 'grad_gate_x_w': 'grad_w', 'grad_gate_x_b': 'grad_w', 'grad_lru_lambda': 'grad_w', 'grad_pool_w': 'grad_w', 'grad_pool_b': 'grad_w', 'grad_pool_scale': 'grad_w', 'grad_norm_lru_g': 'grad_w', 'grad_norm_pool_g': 'grad_w', 'grad_w_out': 'grad_w', 'grad_norm_ffn_g': 'grad_w', 'grad_ffn_w1': 'grad_w', 'grad_ffn_w3': 'grad_w', 'grad_ffn_w2': 'grad_w', 'grad_final_norm_g': 'grad_w', 'delta_norm_mix_g': 'delta_w', 'delta_w_in': 'delta_w', 'delta_conv_w': 'delta_w', 'delta_conv_b': 'delta_w', 'delta_gate_a_w': 'delta_w', 'delta_gate_a_b': 'delta_w', 'delta_gate_x_w': 'delta_w', 'delta_gate_x_b': 'delta_w', 'delta_lru_lambda': 'delta_w', 'delta_pool_w': 'delta_w', 'delta_pool_b': 'delta_w', 'delta_pool_scale': 'delta_w', 'delta_norm_lru_g': 'delta_w', 'delta_norm_pool_g': 'delta_w', 'delta_w_out': 'delta_w', 'delta_norm_ffn_g': 'delta_w', 'delta_ffn_w1': 'delta_w', 'delta_ffn_w3': 'delta_w', 'delta_ffn_w2': 'delta_w', 'delta_final_norm_g': 'delta_w', 'new_m_norm_mix_g': 'new_m', 'new_m_w_in': 'new_m', 'new_m_conv_w': 'new_m', 'new_m_conv_b': 'new_m', 'new_m_gate_a_w': 'new_m', 'new_m_gate_a_b': 'new_m', 'new_m_gate_x_w': 'new_m', 'new_m_gate_x_b': 'new_m', 'new_m_lru_lambda': 'new_m', 'new_m_pool_w': 'new_m', 'new_m_pool_b': 'new_m', 'new_m_pool_scale': 'new_m', 'new_m_norm_lru_g': 'new_m', 'new_m_norm_pool_g': 'new_m', 'new_m_w_out': 'new_m', 'new_m_norm_ffn_g': 'new_m', 'new_m_ffn_w1': 'new_m', 'new_m_ffn_w3': 'new_m', 'new_m_ffn_w2': 'new_m', 'new_m_final_norm_g': 'new_m', 'new_v_norm_mix_g': 'new_v', 'new_v_w_in': 'new_v', 'new_v_conv_w': 'new_v', 'new_v_conv_b': 'new_v', 'new_v_gate_a_w': 'new_v', 'new_v_gate_a_b': 'new_v', 'new_v_gate_x_w': 'new_v', 'new_v_gate_x_b': 'new_v', 'new_v_lru_lambda': 'new_v', 'new_v_pool_w': 'new_v', 'new_v_pool_b': 'new_v', 'new_v_pool_scale': 'new_v', 'new_v_norm_lru_g': 'new_v', 'new_v_norm_pool_g': 'new_v', 'new_v_w_out': 'new_v', 'new_v_norm_ffn_g': 'new_v', 'new_v_ffn_w1': 'new_v', 'new_v_ffn_w3': 'new_v', 'new_v_ffn_w2': 'new_v', 'new_v_final_norm_g': 'new_v'}


def _forward(args):
    return _fwd_reference(*[args[k] for k in FWD_PARAMS])


def _output_shape():
    def fwd():
        inp = _fwd_setup_inputs(0)
        return _fwd_reference(*[inp[k] for k in FWD_PARAMS])
    out = _jax.eval_shape(fwd)
    return out.shape, out.dtype

N_MICROBATCH = 1
ADAM_LR = 0.001
ADAM_B1 = 0.9
ADAM_B2 = 0.999
ADAM_EPS = 1e-08
ADAM_WD = 0.01
ADAM_STEP = 10
PER_EXAMPLE_BATCH_AXIS = {'x': 0, 'loss_target': 0}
SHARED_INPUTS = []
_WEIGHT_DTYPES = {'norm_mix_g': _jnp.float32, 'w_in': _jnp.float32, 'conv_w': _jnp.float32, 'conv_b': _jnp.float32, 'gate_a_w': _jnp.float32, 'gate_a_b': _jnp.float32, 'gate_x_w': _jnp.float32, 'gate_x_b': _jnp.float32, 'lru_lambda': _jnp.float32, 'pool_w': _jnp.float32, 'pool_b': _jnp.float32, 'pool_scale': _jnp.float32, 'norm_lru_g': _jnp.float32, 'norm_pool_g': _jnp.float32, 'w_out': _jnp.float32, 'norm_ffn_g': _jnp.float32, 'ffn_w1': _jnp.float32, 'ffn_w3': _jnp.float32, 'ffn_w2': _jnp.float32, 'final_norm_g': _jnp.float32}
MOMENT_SCALE = {'norm_mix_g': 2.318801e-01, 'w_in': 1.949170e-01, 'conv_w': 2.102766e-01, 'conv_b': 2.061732e+00, 'gate_a_w': 6.764250e-02, 'gate_a_b': 5.653477e-02, 'gate_x_w': 1.226029e-01, 'gate_x_b': 6.713023e-02, 'lru_lambda': 9.818805e-02, 'pool_w': 1.928289e-01, 'pool_b': 1.159232e+00, 'pool_scale': 1.909367e-01, 'norm_lru_g': 1.894650e-01, 'norm_pool_g': 1.915818e-01, 'w_out': 1.956449e-01, 'norm_ffn_g': 1.415238e-01, 'ffn_w1': 6.085022e-02, 'ffn_w3': 5.912117e-02, 'ffn_w2': 9.788785e-02, 'final_norm_g': 6.406648e+01}


def _to_microbatches(a, axis):
    t = _jnp.moveaxis(a, axis, 0)
    t = t.reshape((N_MICROBATCH, t.shape[0] // N_MICROBATCH) + t.shape[1:])
    return _jnp.moveaxis(t, 1, axis + 1)


def setup_inputs(seed: int = 0) -> dict:
    inp = _fwd_setup_inputs(seed)
    key = _jax.random.fold_in(_jax.random.key(seed), 7919)
    shape, _ = _output_shape()
    out = dict(inp)
    out["loss_target"] = _jax.random.normal(_jax.random.fold_in(key, 0), shape, _jnp.float32)
    for i, name in enumerate(TWIN_WEIGHTS):
        w = inp[name].astype(_jnp.float32)
        if MOMENT_SCALE is None:
            s = _jnp.sqrt(_jnp.mean(_jnp.square(w)) + 1e-30)
        else:
            s = MOMENT_SCALE[name]
        km, kv = _jax.random.split(_jax.random.fold_in(key, i + 1))
        out[name] = w
        out["m_" + name] = s * _jax.random.normal(km, w.shape, _jnp.float32)
        out["v_" + name] = (s * s) * _jax.random.uniform(kv, w.shape, _jnp.float32, 0.5, 1.5)
    if N_MICROBATCH > 1:
        for name, axis in PER_EXAMPLE_BATCH_AXIS.items():
            out[name] = _to_microbatches(out[name], axis)
    return {'x': out['x'], 'norm_mix_g': out['norm_mix_g'], 'w_in': out['w_in'], 'conv_w': out['conv_w'], 'conv_b': out['conv_b'], 'gate_a_w': out['gate_a_w'], 'gate_a_b': out['gate_a_b'], 'gate_x_w': out['gate_x_w'], 'gate_x_b': out['gate_x_b'], 'lru_lambda': out['lru_lambda'], 'pool_w': out['pool_w'], 'pool_b': out['pool_b'], 'pool_scale': out['pool_scale'], 'norm_lru_g': out['norm_lru_g'], 'norm_pool_g': out['norm_pool_g'], 'w_out': out['w_out'], 'norm_ffn_g': out['norm_ffn_g'], 'ffn_w1': out['ffn_w1'], 'ffn_w3': out['ffn_w3'], 'ffn_w2': out['ffn_w2'], 'final_norm_g': out['final_norm_g'], 'loss_target': out['loss_target'], 'm_norm_mix_g': out['m_norm_mix_g'], 'm_w_in': out['m_w_in'], 'm_conv_w': out['m_conv_w'], 'm_conv_b': out['m_conv_b'], 'm_gate_a_w': out['m_gate_a_w'], 'm_gate_a_b': out['m_gate_a_b'], 'm_gate_x_w': out['m_gate_x_w'], 'm_gate_x_b': out['m_gate_x_b'], 'm_lru_lambda': out['m_lru_lambda'], 'm_pool_w': out['m_pool_w'], 'm_pool_b': out['m_pool_b'], 'm_pool_scale': out['m_pool_scale'], 'm_norm_lru_g': out['m_norm_lru_g'], 'm_norm_pool_g': out['m_norm_pool_g'], 'm_w_out': out['m_w_out'], 'm_norm_ffn_g': out['m_norm_ffn_g'], 'm_ffn_w1': out['m_ffn_w1'], 'm_ffn_w3': out['m_ffn_w3'], 'm_ffn_w2': out['m_ffn_w2'], 'm_final_norm_g': out['m_final_norm_g'], 'v_norm_mix_g': out['v_norm_mix_g'], 'v_w_in': out['v_w_in'], 'v_conv_w': out['v_conv_w'], 'v_conv_b': out['v_conv_b'], 'v_gate_a_w': out['v_gate_a_w'], 'v_gate_a_b': out['v_gate_a_b'], 'v_gate_x_w': out['v_gate_x_w'], 'v_gate_x_b': out['v_gate_x_b'], 'v_lru_lambda': out['v_lru_lambda'], 'v_pool_w': out['v_pool_w'], 'v_pool_b': out['v_pool_b'], 'v_pool_scale': out['v_pool_scale'], 'v_norm_lru_g': out['v_norm_lru_g'], 'v_norm_pool_g': out['v_norm_pool_g'], 'v_w_out': out['v_w_out'], 'v_norm_ffn_g': out['v_norm_ffn_g'], 'v_ffn_w1': out['v_ffn_w1'], 'v_ffn_w3': out['v_ffn_w3'], 'v_ffn_w2': out['v_ffn_w2'], 'v_final_norm_g': out['v_final_norm_g']}


def _loss(weights, diff, rest, loss_target):
    with _jax.named_scope("forward"):
        args = {**rest, TWIN_DIFF_INPUT: diff, **{k: w.astype(_WEIGHT_DTYPES[k]) for k, w in weights.items()}}
        y = _forward(args)
    with _jax.named_scope("loss_head"):
        err = _jnp.square(y.astype(_jnp.float32) - loss_target)
        return 0.5 * _jnp.sum(_jnp.mean(err, axis=-1)) if err.ndim else 0.5 * err


def _adamw(w, g, m, v):
    m = ADAM_B1 * m + (1.0 - ADAM_B1) * g
    v = ADAM_B2 * v + (1.0 - ADAM_B2) * _jnp.square(g)
    m_hat = m / (1.0 - ADAM_B1 ** ADAM_STEP)
    v_hat = v / (1.0 - ADAM_B2 ** ADAM_STEP)
    delta = -ADAM_LR * (m_hat / (_jnp.sqrt(v_hat) + ADAM_EPS) + ADAM_WD * w)
    return delta, m, v


def reference(x, norm_mix_g, w_in, conv_w, conv_b, gate_a_w, gate_a_b, gate_x_w, gate_x_b, lru_lambda, pool_w, pool_b, pool_scale, norm_lru_g, norm_pool_g, w_out, norm_ffn_g, ffn_w1, ffn_w3, ffn_w2, final_norm_g, loss_target, m_norm_mix_g, m_w_in, m_conv_w, m_conv_b, m_gate_a_w, m_gate_a_b, m_gate_x_w, m_gate_x_b, m_lru_lambda, m_pool_w, m_pool_b, m_pool_scale, m_norm_lru_g, m_norm_pool_g, m_w_out, m_norm_ffn_g, m_ffn_w1, m_ffn_w3, m_ffn_w2, m_final_norm_g, v_norm_mix_g, v_w_in, v_conv_w, v_conv_b, v_gate_a_w, v_gate_a_b, v_gate_x_w, v_gate_x_b, v_lru_lambda, v_pool_w, v_pool_b, v_pool_scale, v_norm_lru_g, v_norm_pool_g, v_w_out, v_norm_ffn_g, v_ffn_w1, v_ffn_w3, v_ffn_w2, v_final_norm_g):
    given = dict(x=x, norm_mix_g=norm_mix_g, w_in=w_in, conv_w=conv_w, conv_b=conv_b, gate_a_w=gate_a_w, gate_a_b=gate_a_b, gate_x_w=gate_x_w, gate_x_b=gate_x_b, lru_lambda=lru_lambda, pool_w=pool_w, pool_b=pool_b, pool_scale=pool_scale, norm_lru_g=norm_lru_g, norm_pool_g=norm_pool_g, w_out=w_out, norm_ffn_g=norm_ffn_g, ffn_w1=ffn_w1, ffn_w3=ffn_w3, ffn_w2=ffn_w2, final_norm_g=final_norm_g, loss_target=loss_target, m_norm_mix_g=m_norm_mix_g, m_w_in=m_w_in, m_conv_w=m_conv_w, m_conv_b=m_conv_b, m_gate_a_w=m_gate_a_w, m_gate_a_b=m_gate_a_b, m_gate_x_w=m_gate_x_w, m_gate_x_b=m_gate_x_b, m_lru_lambda=m_lru_lambda, m_pool_w=m_pool_w, m_pool_b=m_pool_b, m_pool_scale=m_pool_scale, m_norm_lru_g=m_norm_lru_g, m_norm_pool_g=m_norm_pool_g, m_w_out=m_w_out, m_norm_ffn_g=m_norm_ffn_g, m_ffn_w1=m_ffn_w1, m_ffn_w3=m_ffn_w3, m_ffn_w2=m_ffn_w2, m_final_norm_g=m_final_norm_g, v_norm_mix_g=v_norm_mix_g, v_w_in=v_w_in, v_conv_w=v_conv_w, v_conv_b=v_conv_b, v_gate_a_w=v_gate_a_w, v_gate_a_b=v_gate_a_b, v_gate_x_w=v_gate_x_w, v_gate_x_b=v_gate_x_b, v_lru_lambda=v_lru_lambda, v_pool_w=v_pool_w, v_pool_b=v_pool_b, v_pool_scale=v_pool_scale, v_norm_lru_g=v_norm_lru_g, v_norm_pool_g=v_norm_pool_g, v_w_out=v_w_out, v_norm_ffn_g=v_norm_ffn_g, v_ffn_w1=v_ffn_w1, v_ffn_w3=v_ffn_w3, v_ffn_w2=v_ffn_w2, v_final_norm_g=v_final_norm_g)
    weights = {n: given[n] for n in TWIN_WEIGHTS}
    shared = {n: given[n] for n in SHARED_INPUTS}
    per_example = {n: given[n] for n in ['x']}
    grad_fn = _jax.value_and_grad(_loss, argnums=(0, 1))

    def one_microbatch(ex, loss_target):
        ex = dict(ex)
        diff = ex.pop(TWIN_DIFF_INPUT)
        return grad_fn(weights, diff, {**shared, **ex}, loss_target)

    if N_MICROBATCH == 1:
        loss, (grad_w, grad_x) = one_microbatch(per_example, given["loss_target"])
    else:
        def body(carry, xs):
            loss_sum, grad_sum = carry
            l_k, (gw_k, gx_k) = one_microbatch(xs[0], xs[1])
            with _jax.named_scope("update"):
                return (loss_sum + l_k, _jax.tree.map(_jnp.add, grad_sum, gw_k)), gx_k

        init = (_jnp.zeros((), _jnp.float32), _jax.tree.map(_jnp.zeros_like, weights))
        (loss, grad_w), grad_x = _jax.lax.scan(body, init, (per_example, given["loss_target"]))
    with _jax.named_scope("update"):
        delta_w, new_m, new_v = {}, {}, {}
        for n in TWIN_WEIGHTS:
            delta_w[n], new_m[n], new_v[n] = _adamw(weights[n], grad_w[n], given["m_" + n], given["v_" + n])
    return (loss, grad_x, *[grad_w[n] for n in TWIN_WEIGHTS], *[delta_w[n] for n in TWIN_WEIGHTS],
            *[new_m[n] for n in TWIN_WEIGHTS], *[new_v[n] for n in TWIN_WEIGHTS])
```

```python
import functools
import math

import jax
import jax.numpy as jnp
from jax import lax
from jax.experimental import pallas as pl
from jax.experimental.pallas import tpu as pltpu

F32 = jnp.float32
BF16 = jnp.bfloat16
SDS = jax.ShapeDtypeStruct
MESH = pl.DeviceIdType.MESH

EPS = 1e-6
LRU_C = 8.0
CONV_WIDTH = 4
POOL_WINDOWS = (2, 4, 8, 16)
HALO = 16
LANES = 128
SUBLANES = 8
GATE_BLOCK = 256
N_CHIPS = 4

ADAM_LR = 0.001
ADAM_B1 = 0.9
ADAM_B2 = 0.999
ADAM_EPS = 1e-08
ADAM_WD = 0.01
ADAM_STEP = 10

TM_PROJ = 512
TM_MIX = 256
TM_FFN = 512
TM_WGRAD = 1024
VMEM_LIMIT = 56 * 1024 * 1024

SLAB_W = 512
ROW_CONV_B, ROW_CONV_W, ROW_BA, ROW_BX, ROW_LAM, ROW_PB, ROW_PS, ROW_GL, ROW_GP = 0, 1, 5, 6, 7, 8, 9, 10, 11
ROW_GA, ROW_GX, ROW_PW = 16, 80, 144
ROW_MIX, ROW_FFN, ROW_FIN = 272, 274, 276
MIX_SLAB_ROWS = 272
SLAB_ROWS = 288


def _cp(sem=None, **kw):
    if sem is not None:
        kw["dimension_semantics"] = sem
    return pltpu.CompilerParams(vmem_limit_bytes=VMEM_LIMIT, **kw)


def _const_spec(shape):
    nd = len(shape)
    return pl.BlockSpec(shape, lambda *_: (0,) * nd, pipeline_mode=pl.Buffered(1))


def _sigmoid(x):
    return 1.0 / (1.0 + jnp.exp(-x))


def _dot(a, b):
    return jnp.dot(a, b, preferred_element_type=F32)


def _dot_nt(a, b):
    return lax.dot_general(a, b, (((1,), (1,)), ((), ())), preferred_element_type=F32)


def _dot_tn(a, b):
    return lax.dot_general(a, b, (((0,), (0,)), ((), ())), preferred_element_type=F32)


def _colsum8(v):
    m, c = v.shape
    return v.reshape(m // SUBLANES, SUBLANES, c).sum(axis=0)


def _rowmean(v):
    return jnp.mean(v, axis=-1, keepdims=True)


def _rms_bwd(dy, xhat, r, g):
    dxh = dy * g
    return r * (dxh - xhat * _rowmean(dxh * xhat))


def _softplus_neg(lam):
    z = -lam
    e = jnp.exp(-jnp.abs(z))
    u = 1.0 + e
    d = u - 1.0
    log1p = jnp.where(d == 0.0, e, jnp.log(u) * (e / jnp.where(d == 0.0, 1.0, d)))
    return jnp.maximum(z, 0.0) + log1p


def _neg_expm1(z):
    series = -(z * (1.0 + z * (0.5 + z * (1.0 / 6.0 + z * (1.0 / 24.0)))))
    return jnp.where(z > -0.03, series, 1.0 - jnp.exp(z))


_GELU_C = math.sqrt(2.0 / math.pi)
_GELU_K = 0.044715


def _gelu_parts(x):
    x2 = x * x
    th = jnp.tanh(_GELU_C * (x + _GELU_K * x2 * x))
    ge = 0.5 * x * (1.0 + th)
    dge = 0.5 * (1.0 + th) + 0.5 * x * (1.0 - th * th) * (_GELU_C * (1.0 + 3.0 * _GELU_K * x2))
    return ge, dge


def _shift_down(halo, tile, k):
    if k == 0:
        return tile
    ext = jnp.concatenate([halo, tile], axis=0)
    n = tile.shape[0]
    h = halo.shape[0]
    return ext[h - k:h - k + n]


def _shift_up(tile, nxt, k):
    if k == 0:
        return tile
    ext = jnp.concatenate([tile, nxt], axis=0)
    return ext[k:k + tile.shape[0]]


def _build_gate_blocks(ga_ref, gx_ref, gw_ref):
    hd = ga_ref.shape[0]
    per = GATE_BLOCK // hd
    lane = lax.broadcasted_iota(jnp.int32, (hd, GATE_BLOCK), 1)
    for b in range(gw_ref.shape[0]):
        for src, off in ((ga_ref, 0), (gx_ref, GATE_BLOCK)):
            blk = src[:, b * GATE_BLOCK:(b + 1) * GATE_BLOCK]
            for hh in range(per):
                m = (lane >= hh * hd) & (lane < (hh + 1) * hd)
                gw_ref[b, hh * hd:(hh + 1) * hd, off:off + GATE_BLOCK] = jnp.where(m, blk, 0.0).astype(BF16)


def _scan_level1(a, b, reverse):
    m, c = a.shape
    a3 = a.reshape(m // SUBLANES, SUBLANES, c)
    b3 = b.reshape(m // SUBLANES, SUBLANES, c)
    row = lax.broadcasted_iota(jnp.int32, a3.shape, 1)
    for s in (1, 2, 4):
        sh = (SUBLANES - s) if reverse else s
        a_sh = pltpu.roll(a3, sh, 1)
        b_sh = pltpu.roll(b3, sh, 1)
        ok = (row < SUBLANES - s) if reverse else (row >= s)
        b3 = jnp.where(ok, a3 * b_sh + b3, b3)
        a3 = jnp.where(ok, a3 * a_sh, a3)
    return a3.reshape(m, c), b3.reshape(m, c)


def _scan_level2(a_ref, b_ref, out_ref, carry, reverse):
    m, c = a_ref.shape
    ng = m // SUBLANES

    def step(g, cr):
        gi = (ng - 1 - g) if reverse else g
        off = pl.multiple_of(gi * SUBLANES, SUBLANES)
        h = b_ref[pl.ds(off, SUBLANES), :] + a_ref[pl.ds(off, SUBLANES), :] * cr
        out_ref[pl.ds(off, SUBLANES), :] = h
        edge = h[0:1, :] if reverse else h[SUBLANES - 1:SUBLANES, :]
        return jnp.broadcast_to(edge, (SUBLANES, c))

    return lax.fori_loop(0, ng, step, carry, unroll=4)


def _mixer_recompute(u, hal, t0, cw, cb, gw_ref, ba, bx, lam, pw_ref, pb, ps):
    tm = u.shape[0]
    lw = cb.shape[1]
    u_l, u_g, u_p = u[:, :lw], u[:, lw:2 * lw], u[:, 2 * lw:]
    hal_l, hal_p = hal[:, :lw], hal[:, 2 * lw:]
    taps = [_shift_down(hal_l, u_l, CONV_WIDTH - 1 - k) for k in range(CONV_WIDTH)]
    xc = cb
    for k in range(CONV_WIDTH):
        xc = xc + taps[k] * cw[k:k + 1, :]
    xcb = xc.astype(BF16)
    nb = lw // GATE_BLOCK
    gs = [_dot(xcb[:, b * GATE_BLOCK:(b + 1) * GATE_BLOCK], gw_ref[b]) for b in range(nb)]
    r = _sigmoid(jnp.concatenate([g[:, :GATE_BLOCK] for g in gs], axis=1) + ba)
    ig = _sigmoid(jnp.concatenate([g[:, GATE_BLOCK:] for g in gs], axis=1) + bx)
    sp = _softplus_neg(lam)
    la = (-LRU_C * r) * sp
    a = jnp.exp(la)
    m2raw = _neg_expm1(2.0 * la)
    mult = jnp.sqrt(jnp.maximum(m2raw, 1e-12))
    ge, dge = _gelu_parts(u_g)
    row = lax.broadcasted_iota(jnp.int32, (tm, LANES), 0) + t0
    pooled, invs, zs = [], [], []
    for gi, w in enumerate(POOL_WINDOWS):
        e = jnp.concatenate([hal_p[:, gi * LANES:(gi + 1) * LANES], u_p[:, gi * LANES:(gi + 1) * LANES]], axis=0)
        s = e
        k = 1
        while k < w:
            s = s + pltpu.roll(s, k, 0)
            k *= 2
        inv = 1.0 / jnp.minimum(row + 1, w).astype(F32)
        pg = s[HALO:] * inv - e[HALO:]
        pooled.append(pg)
        invs.append(inv)
        zs.append(_dot(pg.astype(BF16), pw_ref[:, gi * LANES:(gi + 1) * LANES].astype(BF16)))
    z = jnp.concatenate(zs, axis=1) + pb
    y_pool = z * ps
    return dict(u_l=u_l, u_g=u_g, taps=taps, xc=xc, xcb=xcb, r=r, ig=ig, sp=sp, la=la, a=a, m2raw=m2raw,
                mult=mult, ge=ge, dge=dge, pooled=pooled, invs=invs, z=z, y_pool=y_pool)


def _inproj(x, g_mix, w_in):
    s, d = x.shape
    n = w_in.shape[1]
    tm = min(TM_PROJ, s)

    def body(x_ref, g_ref, w_ref, u_ref):
        xv = x_ref[...]
        r = lax.rsqrt(_rowmean(xv * xv) + EPS)
        u_ref[...] = _dot((xv * r * g_ref[...]).astype(BF16), w_ref[...])

    return pl.pallas_call(
        body, grid=(s // tm,), name="inproj",
        in_specs=[pl.BlockSpec((tm, d), lambda i: (i, 0)), _const_spec((1, d)), _const_spec((d, n))],
        out_specs=pl.BlockSpec((tm, n), lambda i: (i, 0)),
        out_shape=SDS((s, n), F32), compiler_params=_cp(("parallel",)))(x, g_mix, w_in)


def _mixer_fwd(u, x, sp_, w_out):
    s, din = u.shape
    d = x.shape[1]
    lw = din // 3
    tm = min(TM_MIX, s)
    nb = lw // GATE_BLOCK

    def body(u_ref, halo_ref, x_ref, cw_ref, cb_ref, ga_ref, gx_ref, ba_ref, bx_ref, lam_ref, pw_ref, pb_ref,
             ps_ref, gl_ref, gp_ref, wout_ref, h_ref, yn_ref, hres_ref, gw_s, a_s, b_s, carry_s):
        i = pl.program_id(0)

        @pl.when(i == 0)
        def _():
            _build_gate_blocks(ga_ref, gx_ref, gw_s)
            carry_s[...] = jnp.zeros_like(carry_s)

        uv = u_ref[...]
        hal = jnp.where(i > 0, halo_ref[...], 0.0)
        f = _mixer_recompute(uv, hal, i * tm, cw_ref[...], cb_ref[...], gw_s, ba_ref[...], bx_ref[...],
                             lam_ref[...], pw_ref, pb_ref[...], ps_ref[...])
        bb = f["mult"] * (f["ig"] * f["xc"])
        a1, b1 = _scan_level1(f["a"], bb, reverse=False)
        a_s[...] = a1
        b_s[...] = b1
        carry_s[...] = _scan_level2(a_s, b_s, h_ref, carry_s[...], reverse=False)
        y_lru = h_ref[...] * f["ge"]
        rl = lax.rsqrt(_rowmean(y_lru * y_lru) + EPS)
        yp = f["y_pool"]
        rp = lax.rsqrt(_rowmean(yp * yp) + EPS)
        yn = jnp.concatenate([y_lru * rl * gl_ref[...], yp * rp * gp_ref[...]], axis=1).astype(BF16)
        yn_ref[...] = yn
        hres_ref[...] = x_ref[...] + _dot(yn, wout_ref[...])

    small = [sp_[k] for k in ("conv_w", "conv_b", "gate_a_w", "gate_x_w", "gate_a_b", "gate_x_b", "lru_lambda",
                              "pool_w", "pool_b", "pool_scale", "norm_lru_g", "norm_pool_g")]
    return pl.pallas_call(
        body, grid=(s // tm,), name="mixer_fwd",
        in_specs=[pl.BlockSpec((tm, din), lambda i: (i, 0)),
                  pl.BlockSpec((HALO, din), lambda i: (jnp.maximum(i * (tm // HALO) - 1, 0), 0)),
                  pl.BlockSpec((tm, d), lambda i: (i, 0))]
        + [_const_spec(a.shape) for a in small] + [_const_spec(w_out.shape)],
        out_specs=[pl.BlockSpec((tm, lw), lambda i: (i, 0)), pl.BlockSpec((tm, d), lambda i: (i, 0)),
                   pl.BlockSpec((tm, d), lambda i: (i, 0))],
        out_shape=[SDS((s, lw), F32), SDS((s, d), BF16), SDS((s, d), F32)],
        scratch_shapes=[pltpu.VMEM((nb, GATE_BLOCK, 2 * GATE_BLOCK), BF16), pltpu.VMEM((tm, lw), F32),
                        pltpu.VMEM((tm, lw), F32), pltpu.VMEM((SUBLANES, lw), F32)],
        compiler_params=_cp(("arbitrary",)))(u, u, x, *small, w_out)


def _ffn_fwd(hres1, target, g_ffn, g_fin, w1, w3, w2):
    s, d = hres1.shape
    nj, _, fc = w1.shape
    tm = min(TM_FFN, s)

    def body(h_ref, t_ref, gf_ref, gn_ref, w1_ref, w3_ref, w2_ref,
             a1_ref, a3_ref, h2_ref, dh_ref, dhb_ref, loss_ref, dgn_ref, acc_s):
        i, j = pl.program_id(0), pl.program_id(1)

        @pl.when((i == 0) & (j == 0))
        def _():
            loss_ref[...] = jnp.zeros_like(loss_ref)
            dgn_ref[...] = jnp.zeros_like(dgn_ref)

        @pl.when(j == 0)
        def _():
            hv = h_ref[...]
            r = lax.rsqrt(_rowmean(hv * hv) + EPS)
            h2_ref[...] = (hv * r * gf_ref[...]).astype(BF16)

        h2 = h2_ref[...]
        a1 = _dot(h2, w1_ref[0])
        a3 = _dot(h2, w3_ref[0])
        a1_ref[0] = a1.astype(BF16)
        a3_ref[0] = a3.astype(BF16)
        part = _dot(((a1 * _sigmoid(a1)) * a3).astype(BF16), w2_ref[0])

        @pl.when(j == 0)
        def _():
            acc_s[...] = part

        @pl.when(j > 0)
        def _():
            acc_s[...] += part

        @pl.when(j == nj - 1)
        def _():
            hr2 = h_ref[...] + acc_s[...]
            r2 = lax.rsqrt(_rowmean(hr2 * hr2) + EPS)
            xh = hr2 * r2
            gn = gn_ref[...]
            diff = xh * gn - t_ref[...]
            tot = jnp.sum(jnp.sum(diff * diff, axis=1, keepdims=True), axis=0, keepdims=True)
            loss_ref[...] += tot * (0.5 / d)
            dout = diff * (1.0 / d)
            dgn_ref[...] += _colsum8(dout * xh)
            dh = _rms_bwd(dout, xh, r2, gn)
            dh_ref[...] = dh
            dhb_ref[...] = dh.astype(BF16)

    return pl.pallas_call(
        body, grid=(s // tm, nj), name="ffn_fwd",
        in_specs=[pl.BlockSpec((tm, d), lambda i, j: (i, 0)), pl.BlockSpec((tm, d), lambda i, j: (i, 0)),
                  _const_spec((1, d)), _const_spec((1, d)),
                  pl.BlockSpec((1, d, fc), lambda i, j: (j, 0, 0)), pl.BlockSpec((1, d, fc), lambda i, j: (j, 0, 0)),
                  pl.BlockSpec((1, fc, d), lambda i, j: (j, 0, 0))],
        out_specs=[pl.BlockSpec((1, tm, fc), lambda i, j: (j, i, 0)), pl.BlockSpec((1, tm, fc), lambda i, j: (j, i, 0)),
                   pl.BlockSpec((tm, d), lambda i, j: (i, 0)), pl.BlockSpec((tm, d), lambda i, j: (i, 0)),
                   pl.BlockSpec((tm, d), lambda i, j: (i, 0)),
                   pl.BlockSpec((SUBLANES, LANES), lambda i, j: (0, 0)),
                   pl.BlockSpec((SUBLANES, d), lambda i, j: (0, 0))],
        out_shape=[SDS((nj, s, fc), BF16), SDS((nj, s, fc), BF16), SDS((s, d), BF16), SDS((s, d), F32),
                   SDS((s, d), BF16), SDS((SUBLANES, LANES), F32), SDS((SUBLANES, d), F32)],
        scratch_shapes=[pltpu.VMEM((tm, d), F32)],
        compiler_params=_cp(("arbitrary", "arbitrary")))(hres1, target, g_ffn, g_fin, w1, w3, w2)


def _ffn_bwd_act(dh, dhb, a1, a3, hres1, g_ffn, w1, w3, w2):
    s, d = hres1.shape
    nj, _, fc = a1.shape
    tm = min(TM_FFN, s)

    def body(dh_ref, dhb_ref, a1_ref, a3_ref, h_ref, gf_ref, w1_ref, w3_ref, w2_ref,
             da1_ref, da3_ref, dhr_ref, dgf_ref, acc_s):
        i, j = pl.program_id(0), pl.program_id(1)

        @pl.when((i == 0) & (j == 0))
        def _():
            dgf_ref[...] = jnp.zeros_like(dgf_ref)

        dff = _dot_nt(dhb_ref[...], w2_ref[0])
        a1v = a1_ref[0].astype(F32)
        a3v = a3_ref[0].astype(F32)
        sg = _sigmoid(a1v)
        silu = a1v * sg
        da1 = (dff * a3v * (sg * (1.0 + a1v * (1.0 - sg)))).astype(BF16)
        da3 = (dff * silu).astype(BF16)
        da1_ref[0] = da1
        da3_ref[0] = da3
        part = _dot_nt(da1, w1_ref[0]) + _dot_nt(da3, w3_ref[0])

        @pl.when(j == 0)
        def _():
            acc_s[...] = part

        @pl.when(j > 0)
        def _():
            acc_s[...] += part

        @pl.when(j == nj - 1)
        def _():
            hv = h_ref[...]
            r = lax.rsqrt(_rowmean(hv * hv) + EPS)
            xh = hv * r
            dh2 = acc_s[...]
            dgf_ref[...] += _colsum8(dh2 * xh)
            dhr_ref[...] = dh_ref[...] + _rms_bwd(dh2, xh, r, gf_ref[...])

    return pl.pallas_call(
        body, grid=(s // tm, nj), name="ffn_bwd_act",
        in_specs=[pl.BlockSpec((tm, d), lambda i, j: (i, 0)), pl.BlockSpec((tm, d), lambda i, j: (i, 0)),
                  pl.BlockSpec((1, tm, fc), lambda i, j: (j, i, 0)), pl.BlockSpec((1, tm, fc), lambda i, j: (j, i, 0)),
                  pl.BlockSpec((tm, d), lambda i, j: (i, 0)), _const_spec((1, d)),
                  pl.BlockSpec((1, d, fc), lambda i, j: (j, 0, 0)), pl.BlockSpec((1, d, fc), lambda i, j: (j, 0, 0)),
                  pl.BlockSpec((1, fc, d), lambda i, j: (j, 0, 0))],
        out_specs=[pl.BlockSpec((1, tm, fc), lambda i, j: (j, i, 0)), pl.BlockSpec((1, tm, fc), lambda i, j: (j, i, 0)),
                   pl.BlockSpec((tm, d), lambda i, j: (i, 0)), pl.BlockSpec((SUBLANES, d), lambda i, j: (0, 0))],
        out_shape=[SDS((nj, s, fc), BF16), SDS((nj, s, fc), BF16), SDS((s, d), F32), SDS((SUBLANES, d), F32)],
        scratch_shapes=[pltpu.VMEM((tm, d), F32)],
        compiler_params=_cp(("arbitrary", "arbitrary")))(dh, dhb, a1, a3, hres1, g_ffn, w1, w3, w2)


def _ffn_wgrad(h2, dhb, a1, a3, da1, da3):
    s, d = h2.shape
    _, _, fc = a1.shape
    tm = min(TM_WGRAD, s)

    def body(h2_ref, dhb_ref, a1_ref, a3_ref, da1_ref, da3_ref, dw1_ref, dw3_ref, dw2_ref):
        i = pl.program_id(1)

        @pl.when(i == 0)
        def _():
            dw1_ref[...] = jnp.zeros_like(dw1_ref)
            dw3_ref[...] = jnp.zeros_like(dw3_ref)
            dw2_ref[...] = jnp.zeros_like(dw2_ref)

        h2v = h2_ref[...]
        a1v = a1_ref[0].astype(F32)
        ff = ((a1v * _sigmoid(a1v)) * a3_ref[0].astype(F32)).astype(BF16)
        dw1_ref[0] += _dot_tn(h2v, da1_ref[0])
        dw3_ref[0] += _dot_tn(h2v, da3_ref[0])
        dw2_ref[0] += _dot_tn(ff, dhb_ref[...])

    return pl.pallas_call(
        body, grid=(N_CHIPS, s // tm), name="ffn_wgrad",
        in_specs=[pl.BlockSpec((tm, d), lambda j, i: (i, 0)), pl.BlockSpec((tm, d), lambda j, i: (i, 0))]
        + [pl.BlockSpec((1, tm, fc), lambda j, i: (j, i, 0))] * 4,
        out_specs=[pl.BlockSpec((1, d, fc), lambda j, i: (j, 0, 0)), pl.BlockSpec((1, d, fc), lambda j, i: (j, 0, 0)),
                   pl.BlockSpec((1, fc, d), lambda j, i: (j, 0, 0))],
        out_shape=[SDS((N_CHIPS, d, fc), F32), SDS((N_CHIPS, d, fc), F32), SDS((N_CHIPS, fc, d), F32)],
        compiler_params=_cp(("parallel", "arbitrary")))(h2, dhb, a1, a3, da1, da3)


def _mixer_bwd(u, h, dhres1, sp_, w_out):
    s, din = u.shape
    d = dhres1.shape[1]
    lw = din // 3
    tm = min(TM_MIX, s)
    nt = s // tm
    nb = lw // GATE_BLOCK
    hd = sp_["gate_a_w"].shape[0]

    def body(u_ref, halo_ref, h_ref, hhalo_ref, dhr_ref, cw_ref, cb_ref, ga_ref, gx_ref, ba_ref, bx_ref, lam_ref,
             pw_ref, pb_ref, ps_ref, gl_ref, gp_ref, wout_ref, du_ref, slab_ref,
             gw_s, a_s, b_s, e_s, ecarry_s, dxc_s, q_s, vec_s, cwacc_s, dgw_s, dpw_s):
        i = pl.program_id(0)
        tile = nt - 1 - i

        @pl.when(i == 0)
        def _():
            _build_gate_blocks(ga_ref, gx_ref, gw_s)
            for ref in (ecarry_s, dxc_s, q_s, vec_s, cwacc_s, dgw_s, dpw_s):
                ref[...] = jnp.zeros_like(ref)

        uv = u_ref[...]
        hal = jnp.where(tile > 0, halo_ref[...], 0.0)
        cw = cw_ref[...]
        lam = lam_ref[...]
        ps = ps_ref[...]
        f = _mixer_recompute(uv, hal, tile * tm, cw, cb_ref[...], gw_s, ba_ref[...], bx_ref[...], lam, pw_ref,
                             pb_ref[...], ps)
        hv = h_ref[...]
        h_prev = _shift_down(jnp.where(tile > 0, hhalo_ref[...], 0.0), hv, 1)
        y_lru = hv * f["ge"]
        rl = lax.rsqrt(_rowmean(y_lru * y_lru) + EPS)
        yp = f["y_pool"]
        rp = lax.rsqrt(_rowmean(yp * yp) + EPS)
        xh_l = y_lru * rl
        xh_p = yp * rp

        dyn = _dot_nt(dhr_ref[...].astype(BF16), wout_ref[...])
        d_nl, d_np = dyn[:, :lw], dyn[:, lw:]
        vec = {}
        vec[ROW_GL] = _colsum8(d_nl * xh_l)
        vec[ROW_GP] = _colsum8(d_np * xh_p)
        d_ylru = _rms_bwd(d_nl, xh_l, rl, gl_ref[...])
        d_ypool = _rms_bwd(d_np, xh_p, rp, gp_ref[...])

        vec[ROW_PS] = _colsum8(d_ypool * f["z"])
        dz = d_ypool * ps
        vec[ROW_PB] = _colsum8(dz)
        dzb = dz.astype(BF16)
        dup = []
        for gi, w in enumerate(POOL_WINDOWS):
            sl = slice(gi * LANES, (gi + 1) * LANES)
            dpw_s[:, sl] += _dot_tn(f["pooled"][gi].astype(BF16), dzb[:, sl])
            dpool = _dot_nt(dzb[:, sl], pw_ref[:, sl].astype(BF16))
            q = dpool * f["invs"][gi]
            e = jnp.concatenate([q, q_s[:, sl]], axis=0)
            k = 1
            while k < w:
                e = e + pltpu.roll(e, tm + HALO - k, 0)
                k *= 2
            dup.append(e[:tm] - dpool)
            q_s[:, sl] = q[:HALO]

        d_hout = d_ylru * f["ge"]
        d_ug = d_ylru * hv * f["dge"]
        a = f["a"]
        a1, b1 = _scan_level1(a, a * d_hout, reverse=True)
        a_s[...] = a1
        b_s[...] = b1
        e_next = ecarry_s[...]
        ecarry_s[...] = _scan_level2(a_s, b_s, e_s, e_next, reverse=True)
        sv = d_hout + _shift_up(e_s[...], e_next, 1)
        d_a = sv * h_prev
        mult, ig, xc, r = f["mult"], f["ig"], f["xc"], f["r"]
        d_mult = sv * (ig * xc)
        d_ig = sv * mult * xc
        d_xc = sv * mult * ig
        d_la = d_a * a + jnp.where(f["m2raw"] > 1e-12, d_mult * (-(a * a) / mult), 0.0)
        d_r = d_la * (-LRU_C * f["sp"])
        vec[ROW_LAM] = _colsum8(d_la * (-LRU_C * r))
        d_pr = d_r * r * (1.0 - r)
        d_pi = d_ig * ig * (1.0 - ig)
        vec[ROW_BA] = _colsum8(d_pr)
        vec[ROW_BX] = _colsum8(d_pi)
        dxc_parts = []
        for b in range(nb):
            sl = slice(b * GATE_BLOCK, (b + 1) * GATE_BLOCK)
            rhs = jnp.concatenate([d_pr[:, sl], d_pi[:, sl]], axis=1).astype(BF16)
            dgw_s[b] += _dot_tn(f["xcb"][:, sl], rhs)
            dxc_parts.append(_dot_nt(rhs, gw_s[b]))
        d_xc = d_xc + jnp.concatenate(dxc_parts, axis=1)
        vec[ROW_CONV_B] = _colsum8(d_xc)
        dxc_next = dxc_s[...]
        d_ul = None
        for k in range(CONV_WIDTH):
            cwacc_s[k * SUBLANES:(k + 1) * SUBLANES, :] += _colsum8(d_xc * f["taps"][k])
            term = _shift_up(d_xc, dxc_next, CONV_WIDTH - 1 - k) * cw[k:k + 1, :]
            d_ul = term if d_ul is None else d_ul + term
        dxc_s[...] = d_xc[:SUBLANES]
        for row, val in vec.items():
            vec_s[row * SUBLANES:(row + 1) * SUBLANES, :] += val
        du_ref[...] = jnp.concatenate([d_ul, d_ug] + dup, axis=1).astype(BF16)

        @pl.when(i == nt - 1)
        def _():
            rows = []
            for row in range(ROW_GA):
                if row in (ROW_CONV_W, ROW_CONV_W + 1, ROW_CONV_W + 2, ROW_CONV_W + 3):
                    k = row - ROW_CONV_W
                    v = jnp.sum(cwacc_s[k * SUBLANES:(k + 1) * SUBLANES, :], axis=0, keepdims=True)
                elif row <= ROW_GP:
                    v = jnp.sum(vec_s[row * SUBLANES:(row + 1) * SUBLANES, :], axis=0, keepdims=True)
                    if row == ROW_LAM:
                        v = v * (-1.0 / (1.0 + jnp.exp(lam)))
                else:
                    v = jnp.zeros((1, lw), F32)
                rows.append(v)
            slab_ref[0:ROW_GA, :] = jnp.concatenate(rows, axis=0)
            lane = lax.broadcasted_iota(jnp.int32, (hd, GATE_BLOCK), 1)
            for b in range(nb):
                for off, row0 in ((0, ROW_GA), (GATE_BLOCK, ROW_GX)):
                    acc = jnp.zeros((hd, GATE_BLOCK), F32)
                    for hh in range(GATE_BLOCK // hd):
                        m = (lane >= hh * hd) & (lane < (hh + 1) * hd)
                        acc = acc + jnp.where(m, dgw_s[b, hh * hd:(hh + 1) * hd, off:off + GATE_BLOCK], 0.0)
                    slab_ref[row0:row0 + hd, b * GATE_BLOCK:(b + 1) * GATE_BLOCK] = acc
            slab_ref[ROW_PW:ROW_PW + LANES, :] = dpw_s[...]

    small = [sp_[k] for k in ("conv_w", "conv_b", "gate_a_w", "gate_x_w", "gate_a_b", "gate_x_b", "lru_lambda",
                              "pool_w", "pool_b", "pool_scale", "norm_lru_g", "norm_pool_g")]
    rev = lambda i: nt - 1 - i
    return pl.pallas_call(
        body, grid=(nt,), name="mixer_bwd",
        in_specs=[pl.BlockSpec((tm, din), lambda i: (rev(i), 0)),
                  pl.BlockSpec((HALO, din), lambda i: (jnp.maximum(rev(i) * (tm // HALO) - 1, 0), 0)),
                  pl.BlockSpec((tm, lw), lambda i: (rev(i), 0)),
                  pl.BlockSpec((SUBLANES, lw), lambda i: (jnp.maximum(rev(i) * (tm // SUBLANES) - 1, 0), 0)),
                  pl.BlockSpec((tm, d), lambda i: (rev(i), 0))]
        + [_const_spec(a.shape) for a in small] + [_const_spec(w_out.shape)],
        out_specs=[pl.BlockSpec((tm, din), lambda i: (rev(i), 0)),
                   pl.BlockSpec((MIX_SLAB_ROWS, SLAB_W), lambda i: (0, 0))],
        out_shape=[SDS((s, din), BF16), SDS((MIX_SLAB_ROWS, SLAB_W), F32)],
        scratch_shapes=[pltpu.VMEM((nb, GATE_BLOCK, 2 * GATE_BLOCK), BF16),
                        pltpu.VMEM((tm, lw), F32), pltpu.VMEM((tm, lw), F32), pltpu.VMEM((tm, lw), F32),
                        pltpu.VMEM((SUBLANES, lw), F32), pltpu.VMEM((SUBLANES, lw), F32),
                        pltpu.VMEM((HALO, lw), F32), pltpu.VMEM((ROW_GA * SUBLANES, lw), F32),
                        pltpu.VMEM((CONV_WIDTH * SUBLANES, lw), F32),
                        pltpu.VMEM((nb, GATE_BLOCK, 2 * GATE_BLOCK), F32), pltpu.VMEM((LANES, lw), F32)],
        compiler_params=_cp(("arbitrary",)))(u, u, h, h, dhres1, *small, w_out)


def _inproj_bwd(x, du, dhres1, yn, g_mix, w_in):
    s, d = x.shape
    n = w_in.shape[1]
    nc = n // N_CHIPS
    tm = min(TM_PROJ, s)

    def body(x_ref, du_ref, dhr_ref, yn_ref, g_ref, w_ref, gx_ref, dwin_ref, dwout_ref, dg_ref):
        i = pl.program_id(0)

        @pl.when(i == 0)
        def _():
            dwin_ref[...] = jnp.zeros_like(dwin_ref)
            dwout_ref[...] = jnp.zeros_like(dwout_ref)
            dg_ref[...] = jnp.zeros_like(dg_ref)

        xv = x_ref[...]
        g = g_ref[...]
        r = lax.rsqrt(_rowmean(xv * xv) + EPS)
        xh = xv * r
        h1 = (xh * g).astype(BF16)
        duv = du_ref[...]
        dh1 = _dot_nt(duv, w_ref[...])
        dg_ref[...] += _colsum8(dh1 * xh)
        dhr = dhr_ref[...]
        gx_ref[...] = dhr + _rms_bwd(dh1, xh, r, g)
        for jj in range(N_CHIPS):
            dwin_ref[jj] += _dot_tn(h1, duv[:, jj * nc:(jj + 1) * nc])
        dwout_ref[...] += _dot_tn(yn_ref[...], dhr.astype(BF16))

    return pl.pallas_call(
        body, grid=(s // tm,), name="inproj_bwd",
        in_specs=[pl.BlockSpec((tm, d), lambda i: (i, 0)), pl.BlockSpec((tm, n), lambda i: (i, 0)),
                  pl.BlockSpec((tm, d), lambda i: (i, 0)), pl.BlockSpec((tm, d), lambda i: (i, 0)),
                  _const_spec((1, d)), _const_spec((d, n))],
        out_specs=[pl.BlockSpec((tm, d), lambda i: (i, 0)), pl.BlockSpec((N_CHIPS, d, nc), lambda i: (0, 0, 0)),
                   pl.BlockSpec((d, d), lambda i: (0, 0)), pl.BlockSpec((SUBLANES, d), lambda i: (0, 0))],
        out_shape=[SDS((s, d), F32), SDS((N_CHIPS, d, nc), F32), SDS((d, d), F32), SDS((SUBLANES, d), F32)],
        compiler_params=_cp(("arbitrary",)))(x, du, dhres1, yn, g_mix, w_in)


def _place():
    x, y, c = lax.axis_index("x"), lax.axis_index("y"), lax.axis_index("c")
    return x, y, c


def _other_chips(x, y):
    return [(1 - x, y), (x, 1 - y), (1 - x, 1 - y)]


ANY = pl.BlockSpec(memory_space=pl.ANY)
VMEM_SPEC = pl.BlockSpec(memory_space=pltpu.VMEM)

_GATHERED = {"w_in": "cols", "w_out": "major", "ffn_w1": "major", "ffn_w3": "major", "ffn_w2": "major"}
_BIG = ("w_in", "w_out", "ffn_w1", "ffn_w3", "ffn_w2")


def _gather_weights(shards, conv_w):
    n = len(shards)
    full_shapes = []
    for name, sh in zip(_BIG, shards):
        r, cdim = sh.shape
        if _GATHERED[name] == "cols":
            assert cdim % LANES == 0
            full_shapes.append((r, cdim * N_CHIPS))
        else:
            full_shapes.append((N_CHIPS, r, cdim))

    def region(ref, name, sh, jj, cc):
        r, cdim = sh
        rows = pl.ds(0, r) if cc is None else pl.ds(pl.multiple_of(cc * (r // 2), 16), r // 2)
        if _GATHERED[name] == "cols":
            return ref.at[rows, pl.ds(pl.multiple_of(jj * cdim, LANES), cdim)]
        return ref.at[jj, rows, :]

    def staged(ref, sh, cc):
        r = sh[0]
        return ref.at[pl.ds(pl.multiple_of(cc * (r // 2), 16), r // 2), :]

    def body(*refs):
        ins, cw_in = refs[:n], refs[n]
        outs, cw_out = refs[n + 1:2 * n + 1], refs[2 * n + 1]
        stage = refs[2 * n + 2:3 * n + 2]
        cw_stage, lsem, ssem, rsem, fssem, frsem, cssem, crsem = refs[3 * n + 2:]
        x, y, c = _place()
        j = 2 * x + y
        chips = _other_chips(x, y)
        for w in range(n):
            stage[w][...] = ins[w][...].astype(BF16)
        cw_stage[...] = jnp.zeros_like(cw_stage)
        cw_stage[0:CONV_WIDTH, :] = cw_in[...]
        shs = [s_.shape for s_ in shards]
        local = [pltpu.make_async_copy(stage[w], region(outs[w], _BIG[w], shs[w], j, None), lsem.at[w])
                 for w in range(n)]
        local.append(pltpu.make_async_copy(cw_stage, cw_out.at[j], lsem.at[n]))
        for cp in local:
            cp.start()
        sends = []
        for k, (px, py) in enumerate(chips):
            for w in range(n):
                sends.append(pltpu.make_async_remote_copy(
                    src_ref=staged(stage[w], shs[w], c), dst_ref=region(outs[w], _BIG[w], shs[w], j, c),
                    send_sem=ssem.at[k * n + w], recv_sem=rsem.at[k * n + w], device_id=(px, py, c),
                    device_id_type=MESH))
            sends.append(pltpu.make_async_remote_copy(
                src_ref=cw_stage, dst_ref=cw_out.at[j], send_sem=cssem.at[k], recv_sem=crsem.at[k],
                device_id=(px, py, c), device_id_type=MESH))
        for cp in sends:
            cp.start()
        fwd = []
        for k, (px, py) in enumerate(chips):
            jk = 2 * px + py
            for w in range(n):
                reg = region(outs[w], _BIG[w], shs[w], jk, c)
                pltpu.make_async_remote_copy(src_ref=reg, dst_ref=reg, send_sem=ssem.at[k * n + w],
                                             recv_sem=rsem.at[k * n + w], device_id=(px, py, c),
                                             device_id_type=MESH).wait_recv()
                cp = pltpu.make_async_remote_copy(src_ref=reg, dst_ref=reg, send_sem=fssem.at[k * n + w],
                                                  recv_sem=frsem.at[k * n + w], device_id=(x, y, 1 - c),
                                                  device_id_type=MESH)
                cp.start()
                fwd.append(cp)
            pltpu.make_async_remote_copy(src_ref=cw_stage, dst_ref=cw_out.at[jk], send_sem=cssem.at[k],
                                         recv_sem=crsem.at[k], device_id=(px, py, c),
                                         device_id_type=MESH).wait_recv()
        for k, (px, py) in enumerate(chips):
            jk = 2 * px + py
            for w in range(n):
                reg = region(outs[w], _BIG[w], shs[w], jk, 1 - c)
                pltpu.make_async_remote_copy(src_ref=reg, dst_ref=reg, send_sem=fssem.at[k * n + w],
                                             recv_sem=frsem.at[k * n + w], device_id=(x, y, 1 - c),
                                             device_id_type=MESH).wait_recv()
        for cp in sends + fwd:
            cp.wait_send()
        for cp in local:
            cp.wait()

    nsem = 3 * n
    return pl.pallas_call(
        body, name="gather_weights",
        in_specs=[VMEM_SPEC] * (n + 1), out_specs=[ANY] * (n + 1),
        out_shape=[SDS(fs, BF16) for fs in full_shapes] + [SDS((N_CHIPS, SUBLANES, LANES), F32)],
        scratch_shapes=[pltpu.VMEM(s_.shape, BF16) for s_ in shards] + [pltpu.VMEM((SUBLANES, LANES), F32)]
        + [pltpu.SemaphoreType.DMA((n + 1,))] + [pltpu.SemaphoreType.DMA((nsem,))] * 4
        + [pltpu.SemaphoreType.DMA((3,))] * 2,
        compiler_params=_cp())(*shards, conv_w)


def _halves_to_sibling(arrs):
    n = len(arrs)

    def body(*refs):
        ins, outs, ssem, rsem = refs[:n], refs[n:2 * n], refs[2 * n], refs[2 * n + 1]
        x, y, c = _place()
        cps = []
        for w in range(n):
            hr = arrs[w].shape[1] // 2
            src = ins[w].at[:, pl.ds(pl.multiple_of((1 - c) * hr, SUBLANES), hr), :]
            cps.append(pltpu.make_async_remote_copy(src_ref=src, dst_ref=outs[w], send_sem=ssem.at[w],
                                                    recv_sem=rsem.at[w], device_id=(x, y, 1 - c),
                                                    device_id_type=MESH))
        for cp in cps:
            cp.start()
        for cp in cps:
            cp.wait()

    return pl.pallas_call(
        body, name="rs_sibling", in_specs=[ANY] * n, out_specs=[ANY] * n,
        out_shape=[SDS((a.shape[0], a.shape[1] // 2, a.shape[2]), F32) for a in arrs],
        scratch_shapes=[pltpu.SemaphoreType.DMA((n,))] * 2, compiler_params=_cp())(*arrs)


def _exchange_chips(parts):
    n = len(parts)

    def body(*refs):
        ins, outs, lsem, ssem, rsem = refs[:n], refs[n:2 * n], refs[2 * n], refs[2 * n + 1], refs[2 * n + 2]
        x, y, c = _place()
        j = 2 * x + y
        cps = [pltpu.make_async_copy(ins[w].at[j], outs[w].at[j], lsem.at[w]) for w in range(n)]
        for k, (px, py) in enumerate(_other_chips(x, y)):
            jk = 2 * px + py
            for w in range(n):
                cps.append(pltpu.make_async_remote_copy(
                    src_ref=ins[w].at[jk], dst_ref=outs[w].at[j], send_sem=ssem.at[k * n + w],
                    recv_sem=rsem.at[k * n + w], device_id=(px, py, c), device_id_type=MESH))
        for cp in cps:
            cp.start()
        for cp in cps:
            cp.wait()

    return pl.pallas_call(
        body, name="rs_chips", in_specs=[ANY] * n, out_specs=[ANY] * n,
        out_shape=[SDS(p.shape, F32) for p in parts],
        scratch_shapes=[pltpu.SemaphoreType.DMA((n,))] + [pltpu.SemaphoreType.DMA((3 * n,))] * 2,
        compiler_params=_cp())(*parts)


def _swap_final_halves(halves):
    n = len(halves)

    def body(*refs):
        ins, outs, lsem, ssem, rsem = refs[:n], refs[n:2 * n], refs[2 * n], refs[2 * n + 1], refs[2 * n + 2]
        x, y, c = _place()
        cps = []
        for w in range(n):
            cps.append(pltpu.make_async_copy(ins[w], outs[w].at[c], lsem.at[w]))
            cps.append(pltpu.make_async_remote_copy(src_ref=ins[w], dst_ref=outs[w].at[c], send_sem=ssem.at[w],
                                                    recv_sem=rsem.at[w], device_id=(x, y, 1 - c),
                                                    device_id_type=MESH))
        for cp in cps:
            cp.start()
        for cp in cps:
            cp.wait()

    return pl.pallas_call(
        body, name="rs_swap", in_specs=[ANY] * n, out_specs=[ANY] * n,
        out_shape=[SDS((2,) + h.shape, F32) for h in halves],
        scratch_shapes=[pltpu.SemaphoreType.DMA((n,))] * 3, compiler_params=_cp())(*halves)


def _row_tile(rows, cols, n_arrays):
    budget = 24 * 1024 * 1024 // (2 * 4 * n_arrays * cols)
    best = SUBLANES
    for t in range(SUBLANES, rows + 1, SUBLANES):
        if rows % t == 0 and t <= budget:
            best = t
    return best


def _add_own_half(full, recv, name):
    nsh, rows, cols = full.shape
    hr = rows // 2
    t = _row_tile(hr, cols, 3)
    nt = hr // t

    def body(c_ref, a_ref, b_ref, o_ref):
        o_ref[...] = a_ref[...] + b_ref[...]

    cidx = jnp.reshape(lax.axis_index("c"), (1,)).astype(jnp.int32)
    return pl.pallas_call(
        body, name=name,
        grid_spec=pltpu.PrefetchScalarGridSpec(
            num_scalar_prefetch=1, grid=(nsh, nt),
            in_specs=[pl.BlockSpec((1, t, cols), lambda s_, i, c_ref: (s_, c_ref[0] * nt + i, 0)),
                      pl.BlockSpec((1, t, cols), lambda s_, i, c_ref: (s_, i, 0))],
            out_specs=pl.BlockSpec((1, t, cols), lambda s_, i, c_ref: (s_, i, 0))),
        out_shape=SDS((nsh, hr, cols), F32), compiler_params=_cp(("parallel", "parallel")))(cidx, full, recv)


def _sum_chips(slabs, name):
    nsh, hr, cols = slabs.shape
    t = _row_tile(hr, cols, 5)

    def body(a_ref, o_ref):
        o_ref[...] = ((a_ref[0] + a_ref[1]) + a_ref[2]) + a_ref[3]

    return pl.pallas_call(
        body, name=name, grid=(hr // t,),
        in_specs=[pl.BlockSpec((nsh, t, cols), lambda i: (0, i, 0))],
        out_specs=pl.BlockSpec((t, cols), lambda i: (i, 0)),
        out_shape=SDS((hr, cols), F32), compiler_params=_cp(("parallel",)))(slabs)


def _adamw_math(w, g, m, v):
    m = ADAM_B1 * m + (1.0 - ADAM_B1) * g
    v = ADAM_B2 * v + (1.0 - ADAM_B2) * (g * g)
    m_hat = m / (1.0 - ADAM_B1 ** ADAM_STEP)
    v_hat = v / (1.0 - ADAM_B2 ** ADAM_STEP)
    delta = -ADAM_LR * (m_hat / (jnp.sqrt(v_hat) + ADAM_EPS) + ADAM_WD * w)
    return delta, m, v


def _adamw_big(w, g, m, v, name):
    rows, cols = w.shape
    t = _row_tile(rows, cols, 7)

    def body(w_ref, g_ref, m_ref, v_ref, d_ref, mo_ref, vo_ref):
        d_ref[...], mo_ref[...], vo_ref[...] = _adamw_math(w_ref[...], g_ref[...], m_ref[...], v_ref[...])

    spec = pl.BlockSpec((t, cols), lambda i: (i, 0))
    return pl.pallas_call(
        body, name=name, grid=(rows // t,), in_specs=[spec] * 4, out_specs=[spec] * 3,
        out_shape=[SDS((rows, cols), F32)] * 3, compiler_params=_cp(("parallel",)))(w, g, m, v)


def _allreduce_small(mix_slab, dg_mix, dg_ffn, dg_fin):
    half = SLAB_ROWS // 2

    def body(ms_ref, gm_ref, gf_ref, gn_ref, out_ref, loc_s, sib_s, chip_s, r2_s, fin_s, sems):
        x, y, c = _place()
        j = 2 * x + y
        rows = []
        for ref in (gm_ref, gf_ref, gn_ref):
            v = jnp.sum(ref[...], axis=0, keepdims=True)
            rows += [v[:, :SLAB_W], v[:, SLAB_W:]]
        rows.append(jnp.zeros((SLAB_ROWS - ROW_FIN - 2, SLAB_W), F32))
        loc_s[0:MIX_SLAB_ROWS, :] = ms_ref[...]
        loc_s[MIX_SLAB_ROWS:SLAB_ROWS, :] = jnp.concatenate(rows, axis=0)
        sib = (x, y, 1 - c)
        cp = pltpu.make_async_remote_copy(src_ref=loc_s, dst_ref=sib_s, send_sem=sems.at[0], recv_sem=sems.at[1],
                                          device_id=sib, device_id_type=MESH)
        cp.start()
        cp.wait()
        chip_s[...] = loc_s[...] + sib_s[...]
        mine = chip_s.at[pl.ds(pl.multiple_of(c * half, SUBLANES), half), :]
        r2_s[j] = chip_s[pl.ds(pl.multiple_of(c * half, SUBLANES), half), :]
        cps = []
        for k, (px, py) in enumerate(_other_chips(x, y)):
            cps.append(pltpu.make_async_remote_copy(src_ref=mine, dst_ref=r2_s.at[j], send_sem=sems.at[2 + k],
                                                    recv_sem=sems.at[5 + k], device_id=(px, py, c),
                                                    device_id_type=MESH))
        for cp in cps:
            cp.start()
        for cp in cps:
            cp.wait()
        fin_s[...] = ((r2_s[0] + r2_s[1]) + r2_s[2]) + r2_s[3]
        dst = out_ref.at[pl.ds(pl.multiple_of(c * half, SUBLANES), half), :]
        out_ref[pl.ds(pl.multiple_of(c * half, SUBLANES), half), :] = fin_s[...]
        cp = pltpu.make_async_remote_copy(src_ref=fin_s, dst_ref=dst, send_sem=sems.at[8], recv_sem=sems.at[9],
                                          device_id=sib, device_id_type=MESH)
        cp.start()
        cp.wait()

    return pl.pallas_call(
        body, name="allreduce_small", in_specs=[VMEM_SPEC] * 4, out_specs=VMEM_SPEC,
        out_shape=SDS((SLAB_ROWS, SLAB_W), F32),
        scratch_shapes=[pltpu.VMEM((SLAB_ROWS, SLAB_W), F32)] * 3 + [pltpu.VMEM((N_CHIPS, half, SLAB_W), F32),
                                                                       pltpu.VMEM((half, SLAB_W), F32),
                                                                       pltpu.SemaphoreType.DMA((10,))],
        compiler_params=_cp())(mix_slab, dg_mix, dg_ffn, dg_fin)


_SMALL_ROWS = (("conv_b", ROW_CONV_B), ("gate_a_b", ROW_BA), ("gate_x_b", ROW_BX), ("lru_lambda", ROW_LAM),
               ("pool_b", ROW_PB), ("pool_scale", ROW_PS), ("norm_lru_g", ROW_GL), ("norm_pool_g", ROW_GP))
_WIDE_ROWS = (("norm_mix_g", ROW_MIX), ("norm_ffn_g", ROW_FFN), ("final_norm_g", ROW_FIN))
_BLOCK_ROWS = (("gate_a_w", ROW_GA), ("gate_x_w", ROW_GX), ("pool_w", ROW_PW))
_SMALL_ORDER = tuple(n for n, _ in _SMALL_ROWS) + tuple(n for n, _ in _WIDE_ROWS) + tuple(
    n for n, _ in _BLOCK_ROWS) + ("conv_w",)


def _adamw_small(slab, wmv):
    names = _SMALL_ORDER
    flat = [a for nme in names for a in wmv[nme]]
    nin = len(flat)

    def body(*refs):
        slab_ref, j_ref = refs[0], refs[1]
        ins = refs[2:2 + nin]
        outs = refs[2 + nin:]
        grads = {}
        for nme, row in _SMALL_ROWS:
            grads[nme] = slab_ref[row:row + 1, :]
        for nme, row in _WIDE_ROWS:
            grads[nme] = jnp.concatenate([slab_ref[row:row + 1, :], slab_ref[row + 1:row + 2, :]], axis=1)
        for nme, row in _BLOCK_ROWS:
            grads[nme] = slab_ref[row:row + wmv[nme][0].shape[0], :]
        full = slab_ref[ROW_CONV_W:ROW_CONV_W + CONV_WIDTH, :]
        jv = j_ref[0]
        g = jnp.zeros((CONV_WIDTH, LANES), F32)
        for jj in range(N_CHIPS):
            g = jnp.where(jv == jj, full[:, jj * LANES:(jj + 1) * LANES], g)
        grads["conv_w"] = g
        for idx, nme in enumerate(names):
            w_ref, m_ref, v_ref = ins[3 * idx:3 * idx + 3]
            g = grads[nme]
            delta, m, v = _adamw_math(w_ref[...], g, m_ref[...], v_ref[...])
            outs[4 * idx][...] = g
            outs[4 * idx + 1][...] = delta
            outs[4 * idx + 2][...] = m
            outs[4 * idx + 3][...] = v

    x, y, _ = _place()
    jidx = jnp.reshape(2 * x + y, (1,)).astype(jnp.int32)
    out_shape = [SDS(wmv[nme][0].shape, F32) for nme in names for _ in range(4)]
    res = pl.pallas_call(
        body, name="adamw_small",
        in_specs=[VMEM_SPEC, pl.BlockSpec(memory_space=pltpu.SMEM)] + [VMEM_SPEC] * nin,
        out_specs=[VMEM_SPEC] * len(out_shape), out_shape=out_shape, compiler_params=_cp())(slab, jidx, *flat)
    return {nme: tuple(res[4 * idx:4 * idx + 4]) for idx, nme in enumerate(names)}


def _local_step(x, target, full, sp_):
    u = _inproj(x, sp_["norm_mix_g"], full["w_in"])
    h, yn, hres1 = _mixer_fwd(u, x, sp_, full["w_out"])
    a1, a3, h2, dh, dhb, loss8, dg_fin = _ffn_fwd(hres1, target, sp_["norm_ffn_g"], sp_["final_norm_g"],
                                                  full["ffn_w1"], full["ffn_w3"], full["ffn_w2"])
    da1, da3, dhres1, dg_ffn = _ffn_bwd_act(dh, dhb, a1, a3, hres1, sp_["norm_ffn_g"], full["ffn_w1"],
                                            full["ffn_w3"], full["ffn_w2"])
    dw1, dw3, dw2 = _ffn_wgrad(h2, dhb, a1, a3, da1, da3)
    du, mix_slab = _mixer_bwd(u, h, dhres1, sp_, full["w_out"])
    gx, dwin, dwout, dg_mix = _inproj_bwd(x, du, dhres1, yn, sp_["norm_mix_g"], full["w_in"])
    d = x.shape[1]
    big = {"w_in": dwin, "w_out": dwout.reshape(N_CHIPS, d // N_CHIPS, d), "ffn_w1": dw1, "ffn_w3": dw3,
           "ffn_w2": dw2}
    return loss8[0, 0], gx, big, (mix_slab, dg_mix, dg_ffn, dg_fin)


def _to_compact(w):
    h, i, j = w.shape
    return jnp.transpose(w, (1, 0, 2)).reshape(i, h * j)


def _from_compact(w, h):
    i, hj = w.shape
    return jnp.transpose(w.reshape(i, h, hj // h), (1, 0, 2))


_SMALL_LAYOUT = {
    "gate_a_w": (lambda a: _to_compact(a[0]), lambda a: _from_compact(a, 8)[None]),
    "gate_x_w": (lambda a: _to_compact(a[0]), lambda a: _from_compact(a, 8)[None]),
    "pool_w": (lambda a: _to_compact(a[0]), lambda a: _from_compact(a, 4)[None]),
    "conv_w": (lambda a: a[0], lambda a: a[None]),
    "final_norm_g": (lambda a: a[None], lambda a: a[0]),
}

_WEIGHTS = ("norm_mix_g", "w_in", "conv_w", "conv_b", "gate_a_w", "gate_a_b", "gate_x_w", "gate_x_b", "lru_lambda",
            "pool_w", "pool_b", "pool_scale", "norm_lru_g", "norm_pool_g", "w_out", "norm_ffn_g", "ffn_w1",
            "ffn_w3", "ffn_w2", "final_norm_g")


def kernel(x, norm_mix_g, w_in, conv_w, conv_b, gate_a_w, gate_a_b, gate_x_w, gate_x_b, lru_lambda, pool_w, pool_b, pool_scale, norm_lru_g, norm_pool_g, w_out, norm_ffn_g, ffn_w1, ffn_w3, ffn_w2, final_norm_g, loss_target, m_norm_mix_g, m_w_in, m_conv_w, m_conv_b, m_gate_a_w, m_gate_a_b, m_gate_x_w, m_gate_x_b, m_lru_lambda, m_pool_w, m_pool_b, m_pool_scale, m_norm_lru_g, m_norm_pool_g, m_w_out, m_norm_ffn_g, m_ffn_w1, m_ffn_w3, m_ffn_w2, m_final_norm_g, v_norm_mix_g, v_w_in, v_conv_w, v_conv_b, v_gate_a_w, v_gate_a_b, v_gate_x_w, v_gate_x_b, v_lru_lambda, v_pool_w, v_pool_b, v_pool_scale, v_norm_lru_g, v_norm_pool_g, v_w_out, v_norm_ffn_g, v_ffn_w1, v_ffn_w3, v_ffn_w2, v_final_norm_g):
    loc = locals()
    w = {n: loc[n] for n in _WEIGHTS}
    m = {n: loc["m_" + n] for n in _WEIGHTS}
    v = {n: loc["v_" + n] for n in _WEIGHTS}

    def lay(nme, a):
        return _SMALL_LAYOUT[nme][0](a) if nme in _SMALL_LAYOUT else a

    def unlay(nme, a):
        return _SMALL_LAYOUT[nme][1](a) if nme in _SMALL_LAYOUT else a

    gathered = _gather_weights([w[n][0] for n in _BIG], w["conv_w"][0])
    full = dict(zip(_BIG, gathered[:-1]))
    full["w_out"] = full["w_out"].reshape(w_out.shape[2], w_out.shape[2])
    cw_all = gathered[-1]
    sp_ = {n: lay(n, w[n]) for n in _SMALL_ORDER}
    sp_["conv_w"] = jnp.transpose(cw_all[:, :CONV_WIDTH, :], (1, 0, 2)).reshape(CONV_WIDTH, N_CHIPS * LANES)

    loss, gx, big, small = _local_step(x[0], loss_target[0], full, sp_)
    loss = lax.psum(loss, ("x", "y", "c"))

    names = list(_BIG)
    recv = _halves_to_sibling([big[n] for n in names])
    parts = [_add_own_half(big[n], r, "add_half_" + n) for n, r in zip(names, recv)]
    slabs = _exchange_chips(parts)
    halves = [_sum_chips(s_, "sum_chips_" + n) for n, s_ in zip(names, slabs)]
    grads = _swap_final_halves(halves)
    out = {}
    for n, g in zip(names, grads):
        shp = w[n].shape[1:]
        g2 = g.reshape(shp)
        delta, nm, nv = _adamw_big(w[n][0], g2, m[n][0], v[n][0], "adamw_" + n)
        out[n] = (g2[None], delta[None], nm[None], nv[None])
    slab = _allreduce_small(*small)
    wmv = {n: (lay(n, w[n]), lay(n, m[n]), lay(n, v[n])) for n in _SMALL_ORDER}
    res = _adamw_small(slab, wmv)
    for n in _SMALL_ORDER:
        out[n] = tuple(unlay(n, a) for a in res[n])
    return (loss, gx[None]) + tuple(out[n][k] for k in range(4) for n in _WEIGHTS)
```

```python
import functools
import math

import jax
import jax.numpy as jnp
from jax import lax
from jax.experimental import pallas as pl
from jax.experimental.pallas import tpu as pltpu

F32 = jnp.float32
BF16 = jnp.bfloat16
SDS = jax.ShapeDtypeStruct
MESH = pl.DeviceIdType.MESH

EPS = 1e-6
LRU_C = 8.0
CONV_WIDTH = 4
POOL_WINDOWS = (2, 4, 8, 16)
HALO = 16
LANES = 128
SUBLANES = 8
GATE_BLOCK = 256
N_CHIPS = 4

ADAM_LR = 0.001
ADAM_B1 = 0.9
ADAM_B2 = 0.999
ADAM_EPS = 1e-08
ADAM_WD = 0.01
ADAM_STEP = 10

TM_PROJ = 512
TM_MIX = 256
TM_FFN = 512
TM_WGRAD = 1024
VMEM_LIMIT = 56 * 1024 * 1024

SLAB_W = 512
ROW_CONV_B, ROW_CONV_W, ROW_BA, ROW_BX, ROW_LAM, ROW_PB, ROW_PS, ROW_GL, ROW_GP = 0, 1, 5, 6, 7, 8, 9, 10, 11
ROW_GA, ROW_GX, ROW_PW = 16, 80, 144
ROW_MIX, ROW_FFN, ROW_FIN = 272, 274, 276
MIX_SLAB_ROWS = 272
SLAB_ROWS = 288


def _cp(sem=None, **kw):
    if sem is not None:
        kw["dimension_semantics"] = sem
    return pltpu.CompilerParams(vmem_limit_bytes=VMEM_LIMIT, **kw)


def _const_spec(shape):
    nd = len(shape)
    return pl.BlockSpec(shape, lambda *_: (0,) * nd, pipeline_mode=pl.Buffered(1))


def _sigmoid(x):
    return 1.0 / (1.0 + jnp.exp(-x))


def _dot(a, b):
    return jnp.dot(a, b, preferred_element_type=F32)


def _dot_nt(a, b):
    return lax.dot_general(a, b, (((1,), (1,)), ((), ())), preferred_element_type=F32)


def _dot_tn(a, b):
    return lax.dot_general(a, b, (((0,), (0,)), ((), ())), preferred_element_type=F32)


def _colsum8(v):
    m, c = v.shape
    return v.reshape(m // SUBLANES, SUBLANES, c).sum(axis=0)


def _rowmean(v):
    return jnp.mean(v, axis=-1, keepdims=True)


def _rms_bwd(dy, xhat, r, g):
    dxh = dy * g
    return r * (dxh - xhat * _rowmean(dxh * xhat))


def _softplus_neg(lam):
    z = -lam
    e = jnp.exp(-jnp.abs(z))
    u = 1.0 + e
    d = u - 1.0
    log1p = jnp.where(d == 0.0, e, jnp.log(u) * (e / jnp.where(d == 0.0, 1.0, d)))
    return jnp.maximum(z, 0.0) + log1p


def _neg_expm1(z):
    series = -(z * (1.0 + z * (0.5 + z * (1.0 / 6.0 + z * (1.0 / 24.0)))))
    return jnp.where(z > -0.03, series, 1.0 - jnp.exp(z))


_GELU_C = math.sqrt(2.0 / math.pi)
_GELU_K = 0.044715


def _gelu_parts(x):
    x2 = x * x
    th = jnp.tanh(_GELU_C * (x + _GELU_K * x2 * x))
    ge = 0.5 * x * (1.0 + th)
    dge = 0.5 * (1.0 + th) + 0.5 * x * (1.0 - th * th) * (_GELU_C * (1.0 + 3.0 * _GELU_K * x2))
    return ge, dge


def _shift_down(halo, tile, k):
    if k == 0:
        return tile
    ext = jnp.concatenate([halo, tile], axis=0)
    n = tile.shape[0]
    h = halo.shape[0]
    return ext[h - k:h - k + n]


def _shift_up(tile, nxt, k):
    if k == 0:
        return tile
    ext = jnp.concatenate([tile, nxt], axis=0)
    return ext[k:k + tile.shape[0]]


def _build_gate_blocks(ga_ref, gx_ref, gw_ref):
    hd = ga_ref.shape[0]
    per = GATE_BLOCK // hd
    lane = lax.broadcasted_iota(jnp.int32, (hd, GATE_BLOCK), 1)
    for b in range(gw_ref.shape[0]):
        for src, off in ((ga_ref, 0), (gx_ref, GATE_BLOCK)):
            blk = src[:, b * GATE_BLOCK:(b + 1) * GATE_BLOCK]
            for hh in range(per):
                m = (lane >= hh * hd) & (lane < (hh + 1) * hd)
                gw_ref[b, hh * hd:(hh + 1) * hd, off:off + GATE_BLOCK] = jnp.where(m, blk, 0.0).astype(BF16)


def _scan_level1(a, b, reverse):
    m, c = a.shape
    a3 = a.reshape(m // SUBLANES, SUBLANES, c)
    b3 = b.reshape(m // SUBLANES, SUBLANES, c)
    row = lax.broadcasted_iota(jnp.int32, a3.shape, 1)
    for s in (1, 2, 4):
        sh = (SUBLANES - s) if reverse else s
        a_sh = pltpu.roll(a3, sh, 1)
        b_sh = pltpu.roll(b3, sh, 1)
        ok = (row < SUBLANES - s) if reverse else (row >= s)
        b3 = jnp.where(ok, a3 * b_sh + b3, b3)
        a3 = jnp.where(ok, a3 * a_sh, a3)
    return a3.reshape(m, c), b3.reshape(m, c)


def _scan_level2(a_ref, b_ref, out_ref, carry, reverse):
    m, c = a_ref.shape
    ng = m // SUBLANES

    def step(g, cr):
        gi = (ng - 1 - g) if reverse else g
        off = pl.multiple_of(gi * SUBLANES, SUBLANES)
        h = b_ref[pl.ds(off, SUBLANES), :] + a_ref[pl.ds(off, SUBLANES), :] * cr
        out_ref[pl.ds(off, SUBLANES), :] = h
        edge = h[0:1, :] if reverse else h[SUBLANES - 1:SUBLANES, :]
        return jnp.broadcast_to(edge, (SUBLANES, c))

    return lax.fori_loop(0, ng, step, carry, unroll=4)


def _mixer_recompute(u, hal, t0, cw, cb, gw_ref, ba, bx, lam, pw_ref, pb, ps):
    tm = u.shape[0]
    lw = cb.shape[1]
    u_l, u_g, u_p = u[:, :lw], u[:, lw:2 * lw], u[:, 2 * lw:]
    hal_l, hal_p = hal[:, :lw], hal[:, 2 * lw:]
    taps = [_shift_down(hal_l, u_l, CONV_WIDTH - 1 - k) for k in range(CONV_WIDTH)]
    xc = cb
    for k in range(CONV_WIDTH):
        xc = xc + taps[k] * cw[k:k + 1, :]
    xcb = xc.astype(BF16)
    nb = lw // GATE_BLOCK
    gs = [_dot(xcb[:, b * GATE_BLOCK:(b + 1) * GATE_BLOCK], gw_ref[b]) for b in range(nb)]
    r = _sigmoid(jnp.concatenate([g[:, :GATE_BLOCK] for g in gs], axis=1) + ba)
    ig = _sigmoid(jnp.concatenate([g[:, GATE_BLOCK:] for g in gs], axis=1) + bx)
    sp = _softplus_neg(lam)
    la = (-LRU_C * r) * sp
    a = jnp.exp(la)
    m2raw = _neg_expm1(2.0 * la)
    mult = jnp.sqrt(jnp.maximum(m2raw, 1e-12))
    ge, dge = _gelu_parts(u_g)
    row = lax.broadcasted_iota(jnp.int32, (tm, LANES), 0) + t0
    pooled, invs, zs = [], [], []
    for gi, w in enumerate(POOL_WINDOWS):
        e = jnp.concatenate([hal_p[:, gi * LANES:(gi + 1) * LANES], u_p[:, gi * LANES:(gi + 1) * LANES]], axis=0)
        s = e
        k = 1
        while k < w:
            s = s + pltpu.roll(s, k, 0)
            k *= 2
        inv = 1.0 / jnp.minimum(row + 1, w).astype(F32)
        pg = s[HALO:] * inv - e[HALO:]
        pooled.append(pg)
        invs.append(inv)
        zs.append(_dot(pg.astype(BF16), pw_ref[:, gi * LANES:(gi + 1) * LANES].astype(BF16)))
    z = jnp.concatenate(zs, axis=1) + pb
    y_pool = z * ps
    return dict(u_l=u_l, u_g=u_g, taps=taps, xc=xc, xcb=xcb, r=r, ig=ig, sp=sp, la=la, a=a, m2raw=m2raw,
                mult=mult, ge=ge, dge=dge, pooled=pooled, invs=invs, z=z, y_pool=y_pool)


ANY = pl.BlockSpec(memory_space=pl.ANY)
VMEM_SPEC = pl.BlockSpec(memory_space=pltpu.VMEM)


class _Hosted:
    def __init__(self, ins, out_shapes, sems, start, finish, mid=None, aliases=None):
        self.ins, self.out_shapes, self.sems = list(ins), list(out_shapes), list(sems)
        self.start, self.mid, self.finish = start, mid, finish
        self.aliases = dict(aliases or {})


def _call(body, hosted, stage_preds, *, name, grid, in_specs, out_specs, out_shape, scratch_shapes, args, sem):
    hosted = list(hosted or [])
    n_in, n_out, n_scr = len(in_specs), len(out_specs), len(scratch_shapes)
    c_in = [a for h in hosted for a in h.ins]
    c_out = [o for h in hosted for o in h.out_shapes]
    c_sem = [pltpu.SemaphoreType.DMA((k,)) for h in hosted for k in h.sems]

    def full(*refs):
        p = 0
        parts = []
        for cnt in (n_in, len(c_in), n_out, len(c_out), n_scr, len(c_sem)):
            parts.append(refs[p:p + cnt])
            p += cnt
        hi, ci, ho, co, hs, cs = parts
        per = []
        a = b = c_ = 0
        for h in hosted:
            per.append((h, ci[a:a + len(h.ins)], co[b:b + len(h.out_shapes)], cs[c_:c_ + len(h.sems)]))
            a, b, c_ = a + len(h.ins), b + len(h.out_shapes), c_ + len(h.sems)
        first = mid = last = None
        if hosted and grid:
            first, mid, last = stage_preds()

        def run(fn, pred, i_, o_, s_):
            if fn is None:
                return
            if pred is None:
                fn(i_, o_, s_)
            else:
                pl.when(pred)(functools.partial(fn, i_, o_, s_))

        for h, i_, o_, s_ in per:
            run(h.start, first, i_, o_, s_)
        body(*hi, *ho, *hs)
        for h, i_, o_, s_ in per:
            run(h.mid, mid, i_, o_, s_)
        for h, i_, o_, s_ in per:
            run(h.finish, last, i_, o_, s_)

    aliases = {}
    a = b = 0
    for h in hosted:
        for k, v in h.aliases.items():
            aliases[n_in + a + k] = n_out + b + v
        a, b = a + len(h.ins), b + len(h.out_shapes)
    res = pl.pallas_call(
        full, name=name, grid=grid, in_specs=list(in_specs) + [ANY] * len(c_in),
        out_specs=list(out_specs) + [ANY] * len(c_out), out_shape=list(out_shape) + c_out,
        scratch_shapes=list(scratch_shapes) + c_sem, input_output_aliases=aliases,
        compiler_params=_cp(sem))(*args, *c_in)
    res = list(res)
    outs = []
    p = n_out
    for h in hosted:
        outs.append(res[p:p + len(h.out_shapes)])
        p += len(h.out_shapes)
    return res[:n_out], outs


def _inproj(x, g_mix, w_in):
    s, d = x.shape
    n = w_in.shape[1]
    tm = min(TM_PROJ, s)

    def body(x_ref, g_ref, w_ref, u_ref):
        xv = x_ref[...]
        r = lax.rsqrt(_rowmean(xv * xv) + EPS)
        u_ref[...] = _dot((xv * r * g_ref[...]).astype(BF16), w_ref[...])

    return pl.pallas_call(
        body, grid=(s // tm,), name="inproj",
        in_specs=[pl.BlockSpec((tm, d), lambda i: (i, 0)), _const_spec((1, d)), _const_spec((d, n))],
        out_specs=pl.BlockSpec((tm, n), lambda i: (i, 0)),
        out_shape=SDS((s, n), F32), compiler_params=_cp(("parallel",)))(x, g_mix, w_in)


def _mixer_fwd(u, x, sp_, w_out, hosted=None):
    s, din = u.shape
    d = x.shape[1]
    lw = din // 3
    tm = min(TM_MIX, s)
    nb = lw // GATE_BLOCK

    def body(u_ref, halo_ref, x_ref, cw_ref, cb_ref, ga_ref, gx_ref, ba_ref, bx_ref, lam_ref, pw_ref, pb_ref,
             ps_ref, gl_ref, gp_ref, wout_ref, h_ref, yn_ref, hres_ref, gw_s, a_s, b_s, carry_s):
        i = pl.program_id(0)

        @pl.when(i == 0)
        def _():
            _build_gate_blocks(ga_ref, gx_ref, gw_s)
            carry_s[...] = jnp.zeros_like(carry_s)

        uv = u_ref[...]
        hal = jnp.where(i > 0, halo_ref[...], 0.0)
        f = _mixer_recompute(uv, hal, i * tm, cw_ref[...], cb_ref[...], gw_s, ba_ref[...], bx_ref[...],
                             lam_ref[...], pw_ref, pb_ref[...], ps_ref[...])
        bb = f["mult"] * (f["ig"] * f["xc"])
        a1, b1 = _scan_level1(f["a"], bb, reverse=False)
        a_s[...] = a1
        b_s[...] = b1
        carry_s[...] = _scan_level2(a_s, b_s, h_ref, carry_s[...], reverse=False)
        y_lru = h_ref[...] * f["ge"]
        rl = lax.rsqrt(_rowmean(y_lru * y_lru) + EPS)
        yp = f["y_pool"]
        rp = lax.rsqrt(_rowmean(yp * yp) + EPS)
        yn = jnp.concatenate([y_lru * rl * gl_ref[...], yp * rp * gp_ref[...]], axis=1).astype(BF16)
        yn_ref[...] = yn
        hres_ref[...] = x_ref[...] + _dot(yn, wout_ref[...])

    small = [sp_[k] for k in ("conv_w", "conv_b", "gate_a_w", "gate_x_w", "gate_a_b", "gate_x_b", "lru_lambda",
                              "pool_w", "pool_b", "pool_scale", "norm_lru_g", "norm_pool_g")]
    nt = s // tm

    def stages():
        i = pl.program_id(0)
        return i == 0, i == max(nt - 3, 0), i == nt - 1

    return _call(
        body, hosted, stages, grid=(nt,), name="mixer_fwd",
        in_specs=[pl.BlockSpec((tm, din), lambda i: (i, 0)),
                  pl.BlockSpec((HALO, din), lambda i: (jnp.maximum(i * (tm // HALO) - 1, 0), 0)),
                  pl.BlockSpec((tm, d), lambda i: (i, 0))]
        + [_const_spec(a.shape) for a in small] + [_const_spec(w_out.shape)],
        out_specs=[pl.BlockSpec((tm, lw), lambda i: (i, 0)), pl.BlockSpec((tm, d), lambda i: (i, 0)),
                   pl.BlockSpec((tm, d), lambda i: (i, 0))],
        out_shape=[SDS((s, lw), F32), SDS((s, d), BF16), SDS((s, d), F32)],
        scratch_shapes=[pltpu.VMEM((nb, GATE_BLOCK, 2 * GATE_BLOCK), BF16), pltpu.VMEM((tm, lw), F32),
                        pltpu.VMEM((tm, lw), F32), pltpu.VMEM((SUBLANES, lw), F32)],
        args=(u, u, x, *small, w_out), sem=("arbitrary",))


def _ffn_fwd(hres1, target, g_ffn, g_fin, w1, w3, w2):
    s, d = hres1.shape
    nj, _, fc = w1.shape
    tm = min(TM_FFN, s)

    def body(h_ref, t_ref, gf_ref, gn_ref, w1_ref, w3_ref, w2_ref,
             a1_ref, a3_ref, h2_ref, dh_ref, dhb_ref, loss_ref, dgn_ref, acc_s):
        i, j = pl.program_id(0), pl.program_id(1)

        @pl.when((i == 0) & (j == 0))
        def _():
            loss_ref[...] = jnp.zeros_like(loss_ref)
            dgn_ref[...] = jnp.zeros_like(dgn_ref)

        @pl.when(j == 0)
        def _():
            hv = h_ref[...]
            r = lax.rsqrt(_rowmean(hv * hv) + EPS)
            h2_ref[...] = (hv * r * gf_ref[...]).astype(BF16)

        h2 = h2_ref[...]
        a1 = _dot(h2, w1_ref[0])
        a3 = _dot(h2, w3_ref[0])
        a1_ref[0] = a1.astype(BF16)
        a3_ref[0] = a3.astype(BF16)
        part = _dot(((a1 * _sigmoid(a1)) * a3).astype(BF16), w2_ref[0])

        @pl.when(j == 0)
        def _():
            acc_s[...] = part

        @pl.when(j > 0)
        def _():
            acc_s[...] += part

        @pl.when(j == nj - 1)
        def _():
            hr2 = h_ref[...] + acc_s[...]
            r2 = lax.rsqrt(_rowmean(hr2 * hr2) + EPS)
            xh = hr2 * r2
            gn = gn_ref[...]
            diff = xh * gn - t_ref[...]
            tot = jnp.sum(jnp.sum(diff * diff, axis=1, keepdims=True), axis=0, keepdims=True)
            loss_ref[...] += tot * (0.5 / d)
            dout = diff * (1.0 / d)
            dgn_ref[...] += _colsum8(dout * xh)
            dh = _rms_bwd(dout, xh, r2, gn)
            dh_ref[...] = dh
            dhb_ref[...] = dh.astype(BF16)

    return pl.pallas_call(
        body, grid=(s // tm, nj), name="ffn_fwd",
        in_specs=[pl.BlockSpec((tm, d), lambda i, j: (i, 0)), pl.BlockSpec((tm, d), lambda i, j: (i, 0)),
                  _const_spec((1, d)), _const_spec((1, d)),
                  pl.BlockSpec((1, d, fc), lambda i, j: (j, 0, 0)), pl.BlockSpec((1, d, fc), lambda i, j: (j, 0, 0)),
                  pl.BlockSpec((1, fc, d), lambda i, j: (j, 0, 0))],
        out_specs=[pl.BlockSpec((1, tm, fc), lambda i, j: (j, i, 0)), pl.BlockSpec((1, tm, fc), lambda i, j: (j, i, 0)),
                   pl.BlockSpec((tm, d), lambda i, j: (i, 0)), pl.BlockSpec((tm, d), lambda i, j: (i, 0)),
                   pl.BlockSpec((tm, d), lambda i, j: (i, 0)),
                   pl.BlockSpec((SUBLANES, LANES), lambda i, j: (0, 0)),
                   pl.BlockSpec((SUBLANES, d), lambda i, j: (0, 0))],
        out_shape=[SDS((nj, s, fc), BF16), SDS((nj, s, fc), BF16), SDS((s, d), BF16), SDS((s, d), F32),
                   SDS((s, d), BF16), SDS((SUBLANES, LANES), F32), SDS((SUBLANES, d), F32)],
        scratch_shapes=[pltpu.VMEM((tm, d), F32)],
        compiler_params=_cp(("arbitrary", "arbitrary")))(hres1, target, g_ffn, g_fin, w1, w3, w2)


def _ffn_bwd_act(dh, dhb, a1, a3, hres1, g_ffn, w1, w3, w2):
    s, d = hres1.shape
    nj, _, fc = a1.shape
    tm = min(TM_FFN, s)

    def body(dh_ref, dhb_ref, a1_ref, a3_ref, h_ref, gf_ref, w1_ref, w3_ref, w2_ref,
             da1_ref, da3_ref, dhr_ref, dgf_ref, acc_s):
        i, j = pl.program_id(0), pl.program_id(1)

        @pl.when((i == 0) & (j == 0))
        def _():
            dgf_ref[...] = jnp.zeros_like(dgf_ref)

        dff = _dot_nt(dhb_ref[...], w2_ref[0])
        a1v = a1_ref[0].astype(F32)
        a3v = a3_ref[0].astype(F32)
        sg = _sigmoid(a1v)
        silu = a1v * sg
        da1 = (dff * a3v * (sg * (1.0 + a1v * (1.0 - sg)))).astype(BF16)
        da3 = (dff * silu).astype(BF16)
        da1_ref[0] = da1
        da3_ref[0] = da3
        part = _dot_nt(da1, w1_ref[0]) + _dot_nt(da3, w3_ref[0])

        @pl.when(j == 0)
        def _():
            acc_s[...] = part

        @pl.when(j > 0)
        def _():
            acc_s[...] += part

        @pl.when(j == nj - 1)
        def _():
            hv = h_ref[...]
            r = lax.rsqrt(_rowmean(hv * hv) + EPS)
            xh = hv * r
            dh2 = acc_s[...]
            dgf_ref[...] += _colsum8(dh2 * xh)
            dhr_ref[...] = dh_ref[...] + _rms_bwd(dh2, xh, r, gf_ref[...])

    return pl.pallas_call(
        body, grid=(s // tm, nj), name="ffn_bwd_act",
        in_specs=[pl.BlockSpec((tm, d), lambda i, j: (i, 0)), pl.BlockSpec((tm, d), lambda i, j: (i, 0)),
                  pl.BlockSpec((1, tm, fc), lambda i, j: (j, i, 0)), pl.BlockSpec((1, tm, fc), lambda i, j: (j, i, 0)),
                  pl.BlockSpec((tm, d), lambda i, j: (i, 0)), _const_spec((1, d)),
                  pl.BlockSpec((1, d, fc), lambda i, j: (j, 0, 0)), pl.BlockSpec((1, d, fc), lambda i, j: (j, 0, 0)),
                  pl.BlockSpec((1, fc, d), lambda i, j: (j, 0, 0))],
        out_specs=[pl.BlockSpec((1, tm, fc), lambda i, j: (j, i, 0)), pl.BlockSpec((1, tm, fc), lambda i, j: (j, i, 0)),
                   pl.BlockSpec((tm, d), lambda i, j: (i, 0)), pl.BlockSpec((SUBLANES, d), lambda i, j: (0, 0))],
        out_shape=[SDS((nj, s, fc), BF16), SDS((nj, s, fc), BF16), SDS((s, d), F32), SDS((SUBLANES, d), F32)],
        scratch_shapes=[pltpu.VMEM((tm, d), F32)],
        compiler_params=_cp(("arbitrary", "arbitrary")))(dh, dhb, a1, a3, hres1, g_ffn, w1, w3, w2)


def _ffn_wgrad(h2, dhb, a1, a3, da1, da3):
    s, d = h2.shape
    _, _, fc = a1.shape
    tm = min(TM_WGRAD, s)

    def body(h2_ref, dhb_ref, a1_ref, a3_ref, da1_ref, da3_ref, dw1_ref, dw3_ref, dw2_ref):
        i = pl.program_id(1)

        @pl.when(i == 0)
        def _():
            dw1_ref[...] = jnp.zeros_like(dw1_ref)
            dw3_ref[...] = jnp.zeros_like(dw3_ref)
            dw2_ref[...] = jnp.zeros_like(dw2_ref)

        h2v = h2_ref[...]
        a1v = a1_ref[0].astype(F32)
        ff = ((a1v * _sigmoid(a1v)) * a3_ref[0].astype(F32)).astype(BF16)
        dw1_ref[0] += _dot_tn(h2v, da1_ref[0])
        dw3_ref[0] += _dot_tn(h2v, da3_ref[0])
        dw2_ref[0] += _dot_tn(ff, dhb_ref[...])

    return pl.pallas_call(
        body, grid=(N_CHIPS, s // tm), name="ffn_wgrad",
        in_specs=[pl.BlockSpec((tm, d), lambda j, i: (i, 0)), pl.BlockSpec((tm, d), lambda j, i: (i, 0))]
        + [pl.BlockSpec((1, tm, fc), lambda j, i: (j, i, 0))] * 4,
        out_specs=[pl.BlockSpec((1, d, fc), lambda j, i: (j, 0, 0)), pl.BlockSpec((1, d, fc), lambda j, i: (j, 0, 0)),
                   pl.BlockSpec((1, fc, d), lambda j, i: (j, 0, 0))],
        out_shape=[SDS((N_CHIPS, d, fc), F32), SDS((N_CHIPS, d, fc), F32), SDS((N_CHIPS, fc, d), F32)],
        compiler_params=_cp(("parallel", "arbitrary")))(h2, dhb, a1, a3, da1, da3)


def _mixer_bwd(u, h, dhres1, sp_, w_out, hosted=None):
    s, din = u.shape
    d = dhres1.shape[1]
    lw = din // 3
    tm = min(TM_MIX, s)
    nt = s // tm
    nb = lw // GATE_BLOCK
    hd = sp_["gate_a_w"].shape[0]

    def body(u_ref, halo_ref, h_ref, hhalo_ref, dhr_ref, cw_ref, cb_ref, ga_ref, gx_ref, ba_ref, bx_ref, lam_ref,
             pw_ref, pb_ref, ps_ref, gl_ref, gp_ref, wout_ref, du_ref, slab_ref,
             gw_s, a_s, b_s, e_s, ecarry_s, dxc_s, q_s, vec_s, cwacc_s, dgw_s, dpw_s):
        i = pl.program_id(0)
        tile = nt - 1 - i

        @pl.when(i == 0)
        def _():
            _build_gate_blocks(ga_ref, gx_ref, gw_s)
            for ref in (ecarry_s, dxc_s, q_s, vec_s, cwacc_s, dgw_s, dpw_s):
                ref[...] = jnp.zeros_like(ref)

        uv = u_ref[...]
        hal = jnp.where(tile > 0, halo_ref[...], 0.0)
        cw = cw_ref[...]
        lam = lam_ref[...]
        ps = ps_ref[...]
        f = _mixer_recompute(uv, hal, tile * tm, cw, cb_ref[...], gw_s, ba_ref[...], bx_ref[...], lam, pw_ref,
                             pb_ref[...], ps)
        hv = h_ref[...]
        h_prev = _shift_down(jnp.where(tile > 0, hhalo_ref[...], 0.0), hv, 1)
        y_lru = hv * f["ge"]
        rl = lax.rsqrt(_rowmean(y_lru * y_lru) + EPS)
        yp = f["y_pool"]
        rp = lax.rsqrt(_rowmean(yp * yp) + EPS)
        xh_l = y_lru * rl
        xh_p = yp * rp

        dyn = _dot_nt(dhr_ref[...].astype(BF16), wout_ref[...])
        d_nl, d_np = dyn[:, :lw], dyn[:, lw:]
        vec = {}
        vec[ROW_GL] = _colsum8(d_nl * xh_l)
        vec[ROW_GP] = _colsum8(d_np * xh_p)
        d_ylru = _rms_bwd(d_nl, xh_l, rl, gl_ref[...])
        d_ypool = _rms_bwd(d_np, xh_p, rp, gp_ref[...])

        vec[ROW_PS] = _colsum8(d_ypool * f["z"])
        dz = d_ypool * ps
        vec[ROW_PB] = _colsum8(dz)
        dzb = dz.astype(BF16)
        dup = []
        for gi, w in enumerate(POOL_WINDOWS):
            sl = slice(gi * LANES, (gi + 1) * LANES)
            dpw_s[:, sl] += _dot_tn(f["pooled"][gi].astype(BF16), dzb[:, sl])
            dpool = _dot_nt(dzb[:, sl], pw_ref[:, sl].astype(BF16))
            q = dpool * f["invs"][gi]
            e = jnp.concatenate([q, q_s[:, sl]], axis=0)
            k = 1
            while k < w:
                e = e + pltpu.roll(e, tm + HALO - k, 0)
                k *= 2
            dup.append(e[:tm] - dpool)
            q_s[:, sl] = q[:HALO]

        d_hout = d_ylru * f["ge"]
        d_ug = d_ylru * hv * f["dge"]
        a = f["a"]
        a1, b1 = _scan_level1(a, a * d_hout, reverse=True)
        a_s[...] = a1
        b_s[...] = b1
        e_next = ecarry_s[...]
        ecarry_s[...] = _scan_level2(a_s, b_s, e_s, e_next, reverse=True)
        sv = d_hout + _shift_up(e_s[...], e_next, 1)
        d_a = sv * h_prev
        mult, ig, xc, r = f["mult"], f["ig"], f["xc"], f["r"]
        d_mult = sv * (ig * xc)
        d_ig = sv * mult * xc
        d_xc = sv * mult * ig
        d_la = d_a * a + jnp.where(f["m2raw"] > 1e-12, d_mult * (-(a * a) / mult), 0.0)
        d_r = d_la * (-LRU_C * f["sp"])
        vec[ROW_LAM] = _colsum8(d_la * (-LRU_C * r))
        d_pr = d_r * r * (1.0 - r)
        d_pi = d_ig * ig * (1.0 - ig)
        vec[ROW_BA] = _colsum8(d_pr)
        vec[ROW_BX] = _colsum8(d_pi)
        dxc_parts = []
        for b in range(nb):
            sl = slice(b * GATE_BLOCK, (b + 1) * GATE_BLOCK)
            rhs = jnp.concatenate([d_pr[:, sl], d_pi[:, sl]], axis=1).astype(BF16)
            dgw_s[b] += _dot_tn(f["xcb"][:, sl], rhs)
            dxc_parts.append(_dot_nt(rhs, gw_s[b]))
        d_xc = d_xc + jnp.concatenate(dxc_parts, axis=1)
        vec[ROW_CONV_B] = _colsum8(d_xc)
        dxc_next = dxc_s[...]
        d_ul = None
        for k in range(CONV_WIDTH):
            cwacc_s[k * SUBLANES:(k + 1) * SUBLANES, :] += _colsum8(d_xc * f["taps"][k])
            term = _shift_up(d_xc, dxc_next, CONV_WIDTH - 1 - k) * cw[k:k + 1, :]
            d_ul = term if d_ul is None else d_ul + term
        dxc_s[...] = d_xc[:SUBLANES]
        for row, val in vec.items():
            vec_s[row * SUBLANES:(row + 1) * SUBLANES, :] += val
        du_ref[...] = jnp.concatenate([d_ul, d_ug] + dup, axis=1).astype(BF16)

        @pl.when(i == nt - 1)
        def _():
            rows = []
            for row in range(ROW_GA):
                if row in (ROW_CONV_W, ROW_CONV_W + 1, ROW_CONV_W + 2, ROW_CONV_W + 3):
                    k = row - ROW_CONV_W
                    v = jnp.sum(cwacc_s[k * SUBLANES:(k + 1) * SUBLANES, :], axis=0, keepdims=True)
                elif row <= ROW_GP:
                    v = jnp.sum(vec_s[row * SUBLANES:(row + 1) * SUBLANES, :], axis=0, keepdims=True)
                    if row == ROW_LAM:
                        v = v * (-1.0 / (1.0 + jnp.exp(lam)))
                else:
                    v = jnp.zeros((1, lw), F32)
                rows.append(v)
            slab_ref[0:ROW_GA, :] = jnp.concatenate(rows, axis=0)
            lane = lax.broadcasted_iota(jnp.int32, (hd, GATE_BLOCK), 1)
            for b in range(nb):
                for off, row0 in ((0, ROW_GA), (GATE_BLOCK, ROW_GX)):
                    acc = jnp.zeros((hd, GATE_BLOCK), F32)
                    for hh in range(GATE_BLOCK // hd):
                        m = (lane >= hh * hd) & (lane < (hh + 1) * hd)
                        acc = acc + jnp.where(m, dgw_s[b, hh * hd:(hh + 1) * hd, off:off + GATE_BLOCK], 0.0)
                    slab_ref[row0:row0 + hd, b * GATE_BLOCK:(b + 1) * GATE_BLOCK] = acc
            slab_ref[ROW_PW:ROW_PW + LANES, :] = dpw_s[...]

    small = [sp_[k] for k in ("conv_w", "conv_b", "gate_a_w", "gate_x_w", "gate_a_b", "gate_x_b", "lru_lambda",
                              "pool_w", "pool_b", "pool_scale", "norm_lru_g", "norm_pool_g")]
    rev = lambda i: nt - 1 - i

    def stages():
        i = pl.program_id(0)
        return i == 0, i == max(nt - 3, 0), i == nt - 1

    return _call(
        body, hosted, stages, grid=(nt,), name="mixer_bwd",
        in_specs=[pl.BlockSpec((tm, din), lambda i: (rev(i), 0)),
                  pl.BlockSpec((HALO, din), lambda i: (jnp.maximum(rev(i) * (tm // HALO) - 1, 0), 0)),
                  pl.BlockSpec((tm, lw), lambda i: (rev(i), 0)),
                  pl.BlockSpec((SUBLANES, lw), lambda i: (jnp.maximum(rev(i) * (tm // SUBLANES) - 1, 0), 0)),
                  pl.BlockSpec((tm, d), lambda i: (rev(i), 0))]
        + [_const_spec(a.shape) for a in small] + [_const_spec(w_out.shape)],
        out_specs=[pl.BlockSpec((tm, din), lambda i: (rev(i), 0)),
                   pl.BlockSpec((MIX_SLAB_ROWS, SLAB_W), lambda i: (0, 0))],
        out_shape=[SDS((s, din), BF16), SDS((MIX_SLAB_ROWS, SLAB_W), F32)],
        scratch_shapes=[pltpu.VMEM((nb, GATE_BLOCK, 2 * GATE_BLOCK), BF16),
                        pltpu.VMEM((tm, lw), F32), pltpu.VMEM((tm, lw), F32), pltpu.VMEM((tm, lw), F32),
                        pltpu.VMEM((SUBLANES, lw), F32), pltpu.VMEM((SUBLANES, lw), F32),
                        pltpu.VMEM((HALO, lw), F32), pltpu.VMEM((ROW_GA * SUBLANES, lw), F32),
                        pltpu.VMEM((CONV_WIDTH * SUBLANES, lw), F32),
                        pltpu.VMEM((nb, GATE_BLOCK, 2 * GATE_BLOCK), F32), pltpu.VMEM((LANES, lw), F32)],
        args=(u, u, h, h, dhres1, *small, w_out), sem=("arbitrary",))


def _inproj_bwd(x, du, dhres1, yn, g_mix, w_in, hosted=None):
    s, d = x.shape
    n = w_in.shape[1]
    nc = n // N_CHIPS
    tm = min(TM_PROJ, s)
    nt = s // tm

    def body(x_ref, du_ref, dhr_ref, yn_ref, g_ref, w_ref, gx_ref, dwin_ref, dwout_ref, dg_ref):
        i = pl.program_id(0)

        @pl.when(i == 0)
        def _():
            dwin_ref[...] = jnp.zeros_like(dwin_ref)
            dwout_ref[...] = jnp.zeros_like(dwout_ref)
            dg_ref[...] = jnp.zeros_like(dg_ref)

        xv = x_ref[...]
        g = g_ref[...]
        r = lax.rsqrt(_rowmean(xv * xv) + EPS)
        xh = xv * r
        h1 = (xh * g).astype(BF16)
        duv = du_ref[...]
        dh1 = _dot_nt(duv, w_ref[...])
        dg_ref[...] += _colsum8(dh1 * xh)
        dhr = dhr_ref[...]
        gx_ref[...] = dhr + _rms_bwd(dh1, xh, r, g)
        for jj in range(N_CHIPS):
            dwin_ref[jj] += _dot_tn(h1, duv[:, jj * nc:(jj + 1) * nc])
        dwout_ref[...] += _dot_tn(yn_ref[...], dhr.astype(BF16))

    def stages():
        i = pl.program_id(0)
        return i == 0, i == max(nt - 3, 0), i == nt - 1

    return _call(
        body, hosted, stages, grid=(nt,), name="inproj_bwd",
        in_specs=[pl.BlockSpec((tm, d), lambda i: (i, 0)), pl.BlockSpec((tm, n), lambda i: (i, 0)),
                  pl.BlockSpec((tm, d), lambda i: (i, 0)), pl.BlockSpec((tm, d), lambda i: (i, 0)),
                  _const_spec((1, d)), _const_spec((d, n))],
        out_specs=[pl.BlockSpec((tm, d), lambda i: (i, 0)), pl.BlockSpec((N_CHIPS, d, nc), lambda i: (0, 0, 0)),
                   pl.BlockSpec((d, d), lambda i: (0, 0)), pl.BlockSpec((SUBLANES, d), lambda i: (0, 0))],
        out_shape=[SDS((s, d), F32), SDS((N_CHIPS, d, nc), F32), SDS((d, d), F32), SDS((SUBLANES, d), F32)],
        scratch_shapes=[], args=(x, du, dhres1, yn, g_mix, w_in), sem=("arbitrary",))


def _place():
    x, y, c = lax.axis_index("x"), lax.axis_index("y"), lax.axis_index("c")
    return x, y, c


def _other_chips(x, y):
    return [(1 - x, y), (x, 1 - y), (1 - x, 1 - y)]


ANY = pl.BlockSpec(memory_space=pl.ANY)
VMEM_SPEC = pl.BlockSpec(memory_space=pltpu.VMEM)

_GATHERED = {"w_in": "cols", "w_out": "major", "ffn_w1": "major", "ffn_w3": "major", "ffn_w2": "major"}
_BIG = ("w_in", "w_out", "ffn_w1", "ffn_w3", "ffn_w2")


def _gather_weights(shards, conv_w, n_remote):
    n = len(shards)
    full_shapes = []
    for name, sh in zip(_BIG, shards):
        r, cdim = sh.shape
        if _GATHERED[name] == "cols":
            assert cdim % LANES == 0
            full_shapes.append((r, cdim * N_CHIPS))
        else:
            full_shapes.append((N_CHIPS, r, cdim))

    def region(ref, name, sh, jj, cc):
        r, cdim = sh
        rows = pl.ds(0, r) if cc is None else pl.ds(pl.multiple_of(cc * (r // 2), 16), r // 2)
        if _GATHERED[name] == "cols":
            return ref.at[rows, pl.ds(pl.multiple_of(jj * cdim, LANES), cdim)]
        return ref.at[jj, rows, :]

    def staged(ref, sh, cc):
        r = sh[0]
        return ref.at[pl.ds(pl.multiple_of(cc * (r // 2), 16), r // 2), :]

    def body(*refs):
        ins, cw_in = refs[:n], refs[n]
        outs, cw_out = refs[n + 1:2 * n + 1], refs[2 * n + 1]
        stage = refs[2 * n + 2:3 * n + 2]
        cw_stage, lsem, ssem, rsem, fssem, frsem, cssem, crsem = refs[3 * n + 2:]
        x, y, c = _place()
        j = 2 * x + y
        chips = _other_chips(x, y)
        for w in range(n):
            stage[w][...] = ins[w][...].astype(BF16)
        cw_stage[...] = jnp.zeros_like(cw_stage)
        cw_stage[0:CONV_WIDTH, :] = cw_in[...]
        shs = [s_.shape for s_ in shards]
        local = [pltpu.make_async_copy(stage[w], region(outs[w], _BIG[w], shs[w], j, None), lsem.at[w])
                 for w in range(n)]
        local.append(pltpu.make_async_copy(cw_stage, cw_out.at[j], lsem.at[n]))
        for cp in local:
            cp.start()
        sends = []
        for k, (px, py) in enumerate(chips):
            for w in range(n_remote):
                sends.append(pltpu.make_async_remote_copy(
                    src_ref=staged(stage[w], shs[w], c), dst_ref=region(outs[w], _BIG[w], shs[w], j, c),
                    send_sem=ssem.at[k * n + w], recv_sem=rsem.at[k * n + w], device_id=(px, py, c),
                    device_id_type=MESH))
            sends.append(pltpu.make_async_remote_copy(
                src_ref=cw_stage, dst_ref=cw_out.at[j], send_sem=cssem.at[k], recv_sem=crsem.at[k],
                device_id=(px, py, c), device_id_type=MESH))
        for cp in sends:
            cp.start()
        fwd = []
        for k, (px, py) in enumerate(chips):
            jk = 2 * px + py
            for w in range(n_remote):
                reg = region(outs[w], _BIG[w], shs[w], jk, c)
                pltpu.make_async_remote_copy(src_ref=reg, dst_ref=reg, send_sem=ssem.at[k * n + w],
                                             recv_sem=rsem.at[k * n + w], device_id=(px, py, c),
                                             device_id_type=MESH).wait_recv()
                cp = pltpu.make_async_remote_copy(src_ref=reg, dst_ref=reg, send_sem=fssem.at[k * n + w],
                                                  recv_sem=frsem.at[k * n + w], device_id=(x, y, 1 - c),
                                                  device_id_type=MESH)
                cp.start()
                fwd.append(cp)
            pltpu.make_async_remote_copy(src_ref=cw_stage, dst_ref=cw_out.at[jk], send_sem=cssem.at[k],
                                         recv_sem=crsem.at[k], device_id=(px, py, c),
                                         device_id_type=MESH).wait_recv()
        for k, (px, py) in enumerate(chips):
            jk = 2 * px + py
            for w in range(n_remote):
                reg = region(outs[w], _BIG[w], shs[w], jk, 1 - c)
                pltpu.make_async_remote_copy(src_ref=reg, dst_ref=reg, send_sem=fssem.at[k * n + w],
                                             recv_sem=frsem.at[k * n + w], device_id=(x, y, 1 - c),
                                             device_id_type=MESH).wait_recv()
        for cp in sends + fwd:
            cp.wait_send()
        for cp in local:
            cp.wait()

    nsem = 3 * n
    return pl.pallas_call(
        body, name="gather_first",
        in_specs=[VMEM_SPEC] * (n + 1), out_specs=[ANY] * (n + 1),
        out_shape=[SDS(fs, BF16) for fs in full_shapes] + [SDS((N_CHIPS, SUBLANES, LANES), F32)],
        scratch_shapes=[pltpu.VMEM(s_.shape, BF16) for s_ in shards] + [pltpu.VMEM((SUBLANES, LANES), F32)]
        + [pltpu.SemaphoreType.DMA((n + 1,))] + [pltpu.SemaphoreType.DMA((nsem,))] * 4
        + [pltpu.SemaphoreType.DMA((3,))] * 2,
        compiler_params=_cp())(*shards, conv_w)


def _start_all(make):
    def f(ins, outs, sems):
        for cp in make(ins, outs, sems):
            cp.start()
    return f


def _wait_all(make):
    def f(ins, outs, sems):
        for cp in make(ins, outs, sems):
            cp.wait()
    return f


def _ffn_gather_hosted(arrs):
    n = len(arrs)

    def make(outs, sems):
        ssem, rsem, fs, fr = sems
        x, y, c = _place()
        j = 2 * x + y

        def reg(w, jj, cc):
            hr = arrs[w].shape[1] // 2
            return outs[w].at[jj, pl.ds(pl.multiple_of(cc * hr, 16), hr), :]

        def rc(w, jj, cc, s_sem, r_sem, dev):
            return pltpu.make_async_remote_copy(src_ref=reg(w, jj, cc), dst_ref=reg(w, jj, cc), send_sem=s_sem,
                                                recv_sem=r_sem, device_id=dev, device_id_type=MESH)

        sends, recvs, fwds, frecvs = [], [], [], []
        for k, (px, py) in enumerate(_other_chips(x, y)):
            jk = 2 * px + py
            for w in range(n):
                q = k * n + w
                sends.append(rc(w, j, c, ssem.at[q], rsem.at[q], (px, py, c)))
                recvs.append(rc(w, jk, c, ssem.at[q], rsem.at[q], (px, py, c)))
                fwds.append(rc(w, jk, c, fs.at[q], fr.at[q], (x, y, 1 - c)))
                frecvs.append(rc(w, jk, 1 - c, fs.at[q], fr.at[q], (x, y, 1 - c)))
        return sends, recvs, fwds, frecvs

    def start(ins, outs, sems):
        for cp in make(outs, sems)[0]:
            cp.start()

    def mid(ins, outs, sems):
        _, recvs, fwds, _ = make(outs, sems)
        for r, f in zip(recvs, fwds):
            r.wait_recv()
            f.start()

    def finish(ins, outs, sems):
        sends, _, fwds, frecvs = make(outs, sems)
        for r in frecvs:
            r.wait_recv()
        for cp in sends + fwds:
            cp.wait_send()

    return _Hosted(arrs, [SDS(a.shape, a.dtype) for a in arrs], [3 * n] * 4, start, finish, mid=mid,
                   aliases={w: w for w in range(n)})


def _rs_sibling_hosted(arrs):
    n = len(arrs)

    def make(ins, outs, sems):
        x, y, c = _place()
        cps = []
        for w in range(n):
            hr = arrs[w].shape[1] // 2
            src = ins[w].at[:, pl.ds(pl.multiple_of((1 - c) * hr, SUBLANES), hr), :]
            cps.append(pltpu.make_async_remote_copy(src_ref=src, dst_ref=outs[w], send_sem=sems[0].at[w],
                                                    recv_sem=sems[1].at[w], device_id=(x, y, 1 - c),
                                                    device_id_type=MESH))
        return cps

    return _Hosted(arrs, [SDS((a.shape[0], a.shape[1] // 2, a.shape[2]), F32) for a in arrs], [n, n],
                   _start_all(make), _wait_all(make))


def _rs_chips_hosted(parts):
    n = len(parts)

    def make(ins, outs, sems):
        x, y, c = _place()
        j = 2 * x + y
        cps = []
        for k, (px, py) in enumerate(_other_chips(x, y)):
            jk = 2 * px + py
            for w in range(n):
                cps.append(pltpu.make_async_remote_copy(
                    src_ref=ins[w].at[jk], dst_ref=outs[w].at[j], send_sem=sems[0].at[k * n + w],
                    recv_sem=sems[1].at[k * n + w], device_id=(px, py, c), device_id_type=MESH))
        return cps

    return _Hosted(parts, [SDS(p.shape, p.dtype) for p in parts], [3 * n, 3 * n], _start_all(make), _wait_all(make))


def _rs_swap_hosted(halves):
    n = len(halves)

    def make(ins, outs, sems):
        x, y, c = _place()
        return [pltpu.make_async_remote_copy(src_ref=ins[w], dst_ref=outs[w], send_sem=sems[0].at[w],
                                             recv_sem=sems[1].at[w], device_id=(x, y, 1 - c), device_id_type=MESH)
                for w in range(n)]

    return _Hosted(halves, [SDS(h.shape, F32) for h in halves], [n, n], _start_all(make), _wait_all(make))


def _run_comm(hosted, name):
    return _call(lambda: None, hosted, None, name=name, grid=(), in_specs=[], out_specs=[], out_shape=[],
                 scratch_shapes=[], args=(), sem=None)[1]


def _row_tile(rows, cols, n_arrays):
    budget = 24 * 1024 * 1024 // (2 * 4 * n_arrays * cols)
    best = SUBLANES
    for t in range(SUBLANES, rows + 1, SUBLANES):
        if rows % t == 0 and t <= budget:
            best = t
    return best


def _place_index(which):
    x, y, c = _place()
    v = c if which == "c" else 2 * x + y
    return jnp.reshape(v, (1,)).astype(jnp.int32)


def _add_own_half(full, recv, name):
    nsh, rows, cols = full.shape
    hr = rows // 2
    t = _row_tile(hr, cols, 4)
    nt = hr // t

    def body(c_ref, a_ref, b_ref, o_ref, ob_ref):
        v = a_ref[...] + b_ref[...]
        o_ref[...] = v
        ob_ref[...] = v.astype(BF16)

    half = pl.BlockSpec((1, t, cols), lambda s_, i, c_ref: (s_, i, 0))
    return pl.pallas_call(
        body, name=name,
        grid_spec=pltpu.PrefetchScalarGridSpec(
            num_scalar_prefetch=1, grid=(nsh, nt),
            in_specs=[pl.BlockSpec((1, t, cols), lambda s_, i, c_ref: (s_, c_ref[0] * nt + i, 0)), half],
            out_specs=[half, half]),
        out_shape=[SDS((nsh, hr, cols), F32), SDS((nsh, hr, cols), BF16)],
        compiler_params=_cp(("parallel", "parallel")))(_place_index("c"), full, recv)


def _sum_chips(own, recv, name):
    nsh, hr, cols = own.shape
    t = _row_tile(hr, cols, 6)

    def body(j_ref, own_ref, *rest):
        r_refs, o_ref = rest[:nsh], rest[nsh]
        j = j_ref[0]
        mine = own_ref[0]
        parts = [jnp.where(j == k, mine, r_refs[k][0].astype(F32)) for k in range(nsh)]
        o_ref[...] = ((parts[0] + parts[1]) + parts[2]) + parts[3]

    def other(k):
        return pl.BlockSpec((1, t, cols), lambda i, j_ref: (jnp.where(j_ref[0] == k, (k + 1) % nsh, k), i, 0))

    return pl.pallas_call(
        body, name=name,
        grid_spec=pltpu.PrefetchScalarGridSpec(
            num_scalar_prefetch=1, grid=(hr // t,),
            in_specs=[pl.BlockSpec((1, t, cols), lambda i, j_ref: (j_ref[0], i, 0))]
            + [other(k) for k in range(nsh)],
            out_specs=pl.BlockSpec((t, cols), lambda i, j_ref: (i, 0))),
        out_shape=SDS((hr, cols), F32), compiler_params=_cp(("parallel",)))(_place_index("j"), own, *([recv] * nsh))


def _adamw_math(w, g, m, v):
    m = ADAM_B1 * m + (1.0 - ADAM_B1) * g
    v = ADAM_B2 * v + (1.0 - ADAM_B2) * (g * g)
    m_hat = m / (1.0 - ADAM_B1 ** ADAM_STEP)
    v_hat = v / (1.0 - ADAM_B2 ** ADAM_STEP)
    delta = -ADAM_LR * (m_hat / (jnp.sqrt(v_hat) + ADAM_EPS) + ADAM_WD * w)
    return delta, m, v


def _adamw_big(w, g_own, g_sib, m, v, name):
    rows, cols = w.shape
    hr = rows // 2
    t = _row_tile(hr, cols, 9)
    nth = hr // t

    def body(c_ref, w_ref, go_ref, gs_ref, m_ref, v_ref, g_ref, d_ref, mo_ref, vo_ref):
        own = (pl.program_id(0) // nth) == c_ref[0]
        g = jnp.where(own, go_ref[...], gs_ref[...])
        g_ref[...] = g
        d_ref[...], mo_ref[...], vo_ref[...] = _adamw_math(w_ref[...], g, m_ref[...], v_ref[...])

    spec = pl.BlockSpec((t, cols), lambda i, c_ref: (i, 0))
    hspec = pl.BlockSpec((t, cols), lambda i, c_ref: (i % nth, 0))
    return pl.pallas_call(
        body, name=name,
        grid_spec=pltpu.PrefetchScalarGridSpec(
            num_scalar_prefetch=1, grid=(2 * nth,), in_specs=[spec, hspec, hspec, spec, spec],
            out_specs=[spec] * 4),
        out_shape=[SDS((rows, cols), F32)] * 4,
        compiler_params=_cp(("parallel",)))(_place_index("c"), w, g_own, g_sib, m, v)


def _allreduce_small(mix_slab, dg_mix, dg_ffn, dg_fin):
    half = SLAB_ROWS // 2

    def body(ms_ref, gm_ref, gf_ref, gn_ref, out_ref, loc_s, sib_s, chip_s, r2_s, fin_s, sems):
        x, y, c = _place()
        j = 2 * x + y
        rows = []
        for ref in (gm_ref, gf_ref, gn_ref):
            v = jnp.sum(ref[...], axis=0, keepdims=True)
            rows += [v[:, :SLAB_W], v[:, SLAB_W:]]
        rows.append(jnp.zeros((SLAB_ROWS - ROW_FIN - 2, SLAB_W), F32))
        loc_s[0:MIX_SLAB_ROWS, :] = ms_ref[...]
        loc_s[MIX_SLAB_ROWS:SLAB_ROWS, :] = jnp.concatenate(rows, axis=0)
        sib = (x, y, 1 - c)
        cp = pltpu.make_async_remote_copy(src_ref=loc_s, dst_ref=sib_s, send_sem=sems.at[0], recv_sem=sems.at[1],
                                          device_id=sib, device_id_type=MESH)
        cp.start()
        cp.wait()
        chip_s[...] = loc_s[...] + sib_s[...]
        mine = chip_s.at[pl.ds(pl.multiple_of(c * half, SUBLANES), half), :]
        r2_s[j] = chip_s[pl.ds(pl.multiple_of(c * half, SUBLANES), half), :]
        cps = []
        for k, (px, py) in enumerate(_other_chips(x, y)):
            cps.append(pltpu.make_async_remote_copy(src_ref=mine, dst_ref=r2_s.at[j], send_sem=sems.at[2 + k],
                                                    recv_sem=sems.at[5 + k], device_id=(px, py, c),
                                                    device_id_type=MESH))
        for cp in cps:
            cp.start()
        for cp in cps:
            cp.wait()
        fin_s[...] = ((r2_s[0] + r2_s[1]) + r2_s[2]) + r2_s[3]
        dst = out_ref.at[pl.ds(pl.multiple_of(c * half, SUBLANES), half), :]
        out_ref[pl.ds(pl.multiple_of(c * half, SUBLANES), half), :] = fin_s[...]
        cp = pltpu.make_async_remote_copy(src_ref=fin_s, dst_ref=dst, send_sem=sems.at[8], recv_sem=sems.at[9],
                                          device_id=sib, device_id_type=MESH)
        cp.start()
        cp.wait()

    return pl.pallas_call(
        body, name="allreduce_small", in_specs=[VMEM_SPEC] * 4, out_specs=VMEM_SPEC,
        out_shape=SDS((SLAB_ROWS, SLAB_W), F32),
        scratch_shapes=[pltpu.VMEM((SLAB_ROWS, SLAB_W), F32)] * 3 + [pltpu.VMEM((N_CHIPS, half, SLAB_W), F32),
                                                                       pltpu.VMEM((half, SLAB_W), F32),
                                                                       pltpu.SemaphoreType.DMA((10,))],
        compiler_params=_cp())(mix_slab, dg_mix, dg_ffn, dg_fin)


_SMALL_ROWS = (("conv_b", ROW_CONV_B), ("gate_a_b", ROW_BA), ("gate_x_b", ROW_BX), ("lru_lambda", ROW_LAM),
               ("pool_b", ROW_PB), ("pool_scale", ROW_PS), ("norm_lru_g", ROW_GL), ("norm_pool_g", ROW_GP))
_WIDE_ROWS = (("norm_mix_g", ROW_MIX), ("norm_ffn_g", ROW_FFN), ("final_norm_g", ROW_FIN))
_BLOCK_ROWS = (("gate_a_w", ROW_GA), ("gate_x_w", ROW_GX), ("pool_w", ROW_PW))
_SMALL_ORDER = tuple(n for n, _ in _SMALL_ROWS) + tuple(n for n, _ in _WIDE_ROWS) + tuple(
    n for n, _ in _BLOCK_ROWS) + ("conv_w",)


def _adamw_small(slab, wmv):
    names = _SMALL_ORDER
    flat = [a for nme in names for a in wmv[nme]]
    nin = len(flat)

    def body(*refs):
        slab_ref, j_ref = refs[0], refs[1]
        ins = refs[2:2 + nin]
        outs = refs[2 + nin:]
        grads = {}
        for nme, row in _SMALL_ROWS:
            grads[nme] = slab_ref[row:row + 1, :]
        for nme, row in _WIDE_ROWS:
            grads[nme] = jnp.concatenate([slab_ref[row:row + 1, :], slab_ref[row + 1:row + 2, :]], axis=1)
        for nme, row in _BLOCK_ROWS:
            grads[nme] = slab_ref[row:row + wmv[nme][0].shape[0], :]
        full = slab_ref[ROW_CONV_W:ROW_CONV_W + CONV_WIDTH, :]
        jv = j_ref[0]
        g = jnp.zeros((CONV_WIDTH, LANES), F32)
        for jj in range(N_CHIPS):
            g = jnp.where(jv == jj, full[:, jj * LANES:(jj + 1) * LANES], g)
        grads["conv_w"] = g
        for idx, nme in enumerate(names):
            w_ref, m_ref, v_ref = ins[3 * idx:3 * idx + 3]
            g = grads[nme]
            delta, m, v = _adamw_math(w_ref[...], g, m_ref[...], v_ref[...])
            outs[4 * idx][...] = g
            outs[4 * idx + 1][...] = delta
            outs[4 * idx + 2][...] = m
            outs[4 * idx + 3][...] = v

    x, y, _ = _place()
    jidx = jnp.reshape(2 * x + y, (1,)).astype(jnp.int32)
    out_shape = [SDS(wmv[nme][0].shape, F32) for nme in names for _ in range(4)]
    res = pl.pallas_call(
        body, name="adamw_small",
        in_specs=[VMEM_SPEC, pl.BlockSpec(memory_space=pltpu.SMEM)] + [VMEM_SPEC] * nin,
        out_specs=[VMEM_SPEC] * len(out_shape), out_shape=out_shape, compiler_params=_cp())(slab, jidx, *flat)
    return {nme: tuple(res[4 * idx:4 * idx + 4]) for idx, nme in enumerate(names)}


_FFN = ("ffn_w1", "ffn_w3", "ffn_w2")


def _local_step(x, target, full, sp_, distributed):
    u = _inproj(x, sp_["norm_mix_g"], full["w_in"])
    gather = [_ffn_gather_hosted([full[n] for n in _FFN])] if distributed else None
    (h, yn, hres1), got = _mixer_fwd(u, x, sp_, full["w_out"], gather)
    w1, w3, w2 = got[0] if distributed else [full[n] for n in _FFN]
    a1, a3, h2, dh, dhb, loss8, dg_fin = _ffn_fwd(hres1, target, sp_["norm_ffn_g"], sp_["final_norm_g"], w1, w3, w2)
    da1, da3, dhres1, dg_ffn = _ffn_bwd_act(dh, dhb, a1, a3, hres1, sp_["norm_ffn_g"], w1, w3, w2)
    dws = list(_ffn_wgrad(h2, dhb, a1, a3, da1, da3))
    rs1 = [_rs_sibling_hosted(dws)] if distributed else None
    (du, mix_slab), got = _mixer_bwd(u, h, dhres1, sp_, full["w_out"], rs1)
    rs2 = None
    if distributed:
        pairs = [_add_own_half(a, r, "add_half_" + n) for n, a, r in zip(_FFN, dws, got[0])]
        rs2 = [_rs_chips_hosted([pb for _, pb in pairs])]
    (gx, dwin, dwout, dg_mix), got = _inproj_bwd(x, du, dhres1, yn, sp_["norm_mix_g"], full["w_in"], rs2)
    d = x.shape[1]
    big = {"w_in": dwin, "w_out": dwout.reshape(N_CHIPS, d // N_CHIPS, d)}
    for k, n in enumerate(_FFN):
        big[n] = (pairs[k][0], got[0][k]) if distributed else dws[k]
    return loss8[0, 0], gx, big, (mix_slab, dg_mix, dg_ffn, dg_fin)


def _to_compact(w):
    h, i, j = w.shape
    return jnp.transpose(w, (1, 0, 2)).reshape(i, h * j)


def _from_compact(w, h):
    i, hj = w.shape
    return jnp.transpose(w.reshape(i, h, hj // h), (1, 0, 2))


_SMALL_LAYOUT = {
    "gate_a_w": (lambda a: _to_compact(a[0]), lambda a: _from_compact(a, 8)[None]),
    "gate_x_w": (lambda a: _to_compact(a[0]), lambda a: _from_compact(a, 8)[None]),
    "pool_w": (lambda a: _to_compact(a[0]), lambda a: _from_compact(a, 4)[None]),
    "conv_w": (lambda a: a[0], lambda a: a[None]),
    "final_norm_g": (lambda a: a[None], lambda a: a[0]),
}

_WEIGHTS = ("norm_mix_g", "w_in", "conv_w", "conv_b", "gate_a_w", "gate_a_b", "gate_x_w", "gate_x_b", "lru_lambda",
            "pool_w", "pool_b", "pool_scale", "norm_lru_g", "norm_pool_g", "w_out", "norm_ffn_g", "ffn_w1",
            "ffn_w3", "ffn_w2", "final_norm_g")


def kernel(x, norm_mix_g, w_in, conv_w, conv_b, gate_a_w, gate_a_b, gate_x_w, gate_x_b, lru_lambda, pool_w, pool_b, pool_scale, norm_lru_g, norm_pool_g, w_out, norm_ffn_g, ffn_w1, ffn_w3, ffn_w2, final_norm_g, loss_target, m_norm_mix_g, m_w_in, m_conv_w, m_conv_b, m_gate_a_w, m_gate_a_b, m_gate_x_w, m_gate_x_b, m_lru_lambda, m_pool_w, m_pool_b, m_pool_scale, m_norm_lru_g, m_norm_pool_g, m_w_out, m_norm_ffn_g, m_ffn_w1, m_ffn_w3, m_ffn_w2, m_final_norm_g, v_norm_mix_g, v_w_in, v_conv_w, v_conv_b, v_gate_a_w, v_gate_a_b, v_gate_x_w, v_gate_x_b, v_lru_lambda, v_pool_w, v_pool_b, v_pool_scale, v_norm_lru_g, v_norm_pool_g, v_w_out, v_norm_ffn_g, v_ffn_w1, v_ffn_w3, v_ffn_w2, v_final_norm_g):
    loc = locals()
    w = {n: loc[n] for n in _WEIGHTS}
    m = {n: loc["m_" + n] for n in _WEIGHTS}
    v = {n: loc["v_" + n] for n in _WEIGHTS}

    def lay(nme, a):
        return _SMALL_LAYOUT[nme][0](a) if nme in _SMALL_LAYOUT else a

    def unlay(nme, a):
        return _SMALL_LAYOUT[nme][1](a) if nme in _SMALL_LAYOUT else a

    gathered = _gather_weights([w[n][0] for n in _BIG], w["conv_w"][0], n_remote=2)
    full = dict(zip(_BIG, gathered[:-1]))
    full["w_out"] = full["w_out"].reshape(w_out.shape[2], w_out.shape[2])
    cw_all = gathered[-1]
    sp_ = {n: lay(n, w[n]) for n in _SMALL_ORDER}
    sp_["conv_w"] = jnp.transpose(cw_all[:, :CONV_WIDTH, :], (1, 0, 2)).reshape(CONV_WIDTH, N_CHIPS * LANES)

    loss, gx, big, small = _local_step(x[0], loss_target[0], full, sp_, distributed=True)
    loss = lax.psum(loss, ("x", "y", "c"))

    late = ("w_in", "w_out")
    fin = {n: _sum_chips(big[n][0], big[n][1], "sum_chips_" + n) for n in _FFN}
    recv1, swapped = _run_comm([_rs_sibling_hosted([big[n] for n in late]),
                                _rs_swap_hosted([fin[n] for n in _FFN])], "tail_sibling")
    sib = dict(zip(_FFN, swapped))
    pairs = [_add_own_half(big[n], r, "add_half_" + n) for n, r in zip(late, recv1)]
    recv2, = _run_comm([_rs_chips_hosted([pb for _, pb in pairs])], "tail_chips")
    for n, (p, _), r in zip(late, pairs, recv2):
        fin[n] = _sum_chips(p, r, "sum_chips_" + n)
    swapped, = _run_comm([_rs_swap_hosted([fin[n] for n in late])], "tail_swap")
    sib.update(zip(late, swapped))
    out = {}
    for n in _BIG:
        g, delta, nm, nv = _adamw_big(w[n][0], fin[n], sib[n], m[n][0], v[n][0], "adamw_" + n)
        out[n] = (g[None], delta[None], nm[None], nv[None])
    slab = _allreduce_small(*small)
    wmv = {n: (lay(n, w[n]), lay(n, m[n]), lay(n, v[n])) for n in _SMALL_ORDER}
    res = _adamw_small(slab, wmv)
    for n in _SMALL_ORDER:
        out[n] = tuple(unlay(n, a) for a in res[n])
    return (loss, gx[None]) + tuple(out[n][k] for k in range(4) for n in _WEIGHTS)
```

```python
import functools
import math

import jax
import jax.numpy as jnp
from jax import lax
from jax.experimental import pallas as pl
from jax.experimental.pallas import tpu as pltpu

F32 = jnp.float32
BF16 = jnp.bfloat16
SDS = jax.ShapeDtypeStruct
MESH = pl.DeviceIdType.MESH

EPS = 1e-6
LRU_C = 8.0
CONV_WIDTH = 4
POOL_WINDOWS = (2, 4, 8, 16)
HALO = 16
LANES = 128
SUBLANES = 8
GATE_BLOCK = 256
N_CHIPS = 4

ADAM_LR = 0.001
ADAM_B1 = 0.9
ADAM_B2 = 0.999
ADAM_EPS = 1e-08
ADAM_WD = 0.01
ADAM_STEP = 10

TM_PROJ = 512
TM_MIX = 256
TM_FFN = 512
TM_WGRAD = 1024
TM_FFN_UP = 1024
TM_FFN_DOWN = 512
FFN_ROW_CHUNKS = 2
VMEM_LIMIT = 56 * 1024 * 1024

SLAB_W = 512
ROW_CONV_B, ROW_CONV_W, ROW_BA, ROW_BX, ROW_LAM, ROW_PB, ROW_PS, ROW_GL, ROW_GP = 0, 1, 5, 6, 7, 8, 9, 10, 11
ROW_GA, ROW_GX, ROW_PW = 16, 80, 144
ROW_MIX, ROW_FFN, ROW_FIN, ROW_LOSS = 272, 274, 276, 278
MIX_SLAB_ROWS = 272
SLAB_ROWS = 288


def _cp(sem=None, **kw):
    if sem is not None:
        kw["dimension_semantics"] = sem
    return pltpu.CompilerParams(vmem_limit_bytes=VMEM_LIMIT, **kw)


def _const_spec(shape):
    nd = len(shape)
    return pl.BlockSpec(shape, lambda *_: (0,) * nd, pipeline_mode=pl.Buffered(1))


def _sigmoid(x):
    return 1.0 / (1.0 + jnp.exp(-x))


def _dot(a, b):
    return jnp.dot(a, b, preferred_element_type=F32)


def _dot_nt(a, b):
    return lax.dot_general(a, b, (((1,), (1,)), ((), ())), preferred_element_type=F32)


def _dot_tn(a, b):
    return lax.dot_general(a, b, (((0,), (0,)), ((), ())), preferred_element_type=F32)


def _colsum8(v):
    m, c = v.shape
    return v.reshape(m // SUBLANES, SUBLANES, c).sum(axis=0)


def _rowmean(v):
    return jnp.mean(v, axis=-1, keepdims=True)


def _rms_bwd(dy, xhat, r, g):
    dxh = dy * g
    return r * (dxh - xhat * _rowmean(dxh * xhat))


def _softplus_neg(lam):
    z = -lam
    e = jnp.exp(-jnp.abs(z))
    u = 1.0 + e
    d = u - 1.0
    log1p = jnp.where(d == 0.0, e, jnp.log(u) * (e / jnp.where(d == 0.0, 1.0, d)))
    return jnp.maximum(z, 0.0) + log1p


def _neg_expm1(z):
    series = -(z * (1.0 + z * (0.5 + z * (1.0 / 6.0 + z * (1.0 / 24.0)))))
    return jnp.where(z > -0.03, series, 1.0 - jnp.exp(z))


_GELU_C = math.sqrt(2.0 / math.pi)
_GELU_K = 0.044715


def _gelu_parts(x):
    x2 = x * x
    th = jnp.tanh(_GELU_C * (x + _GELU_K * x2 * x))
    ge = 0.5 * x * (1.0 + th)
    dge = 0.5 * (1.0 + th) + 0.5 * x * (1.0 - th * th) * (_GELU_C * (1.0 + 3.0 * _GELU_K * x2))
    return ge, dge


def _shift_down(halo, tile, k):
    if k == 0:
        return tile
    ext = jnp.concatenate([halo, tile], axis=0)
    n = tile.shape[0]
    h = halo.shape[0]
    return ext[h - k:h - k + n]


def _shift_up(tile, nxt, k):
    if k == 0:
        return tile
    ext = jnp.concatenate([tile, nxt], axis=0)
    return ext[k:k + tile.shape[0]]


def _build_gate_blocks(ga_ref, gx_ref, gw_ref):
    hd = ga_ref.shape[0]
    per = GATE_BLOCK // hd
    lane = lax.broadcasted_iota(jnp.int32, (hd, GATE_BLOCK), 1)
    for b in range(gw_ref.shape[0]):
        for src, off in ((ga_ref, 0), (gx_ref, GATE_BLOCK)):
            blk = src[:, b * GATE_BLOCK:(b + 1) * GATE_BLOCK]
            for hh in range(per):
                m = (lane >= hh * hd) & (lane < (hh + 1) * hd)
                gw_ref[b, hh * hd:(hh + 1) * hd, off:off + GATE_BLOCK] = jnp.where(m, blk, 0.0).astype(BF16)


def _scan_level1(a, b, reverse):
    m, c = a.shape
    a3 = a.reshape(m // SUBLANES, SUBLANES, c)
    b3 = b.reshape(m // SUBLANES, SUBLANES, c)
    row = lax.broadcasted_iota(jnp.int32, a3.shape, 1)
    for s in (1, 2, 4):
        sh = (SUBLANES - s) if reverse else s
        a_sh = pltpu.roll(a3, sh, 1)
        b_sh = pltpu.roll(b3, sh, 1)
        ok = (row < SUBLANES - s) if reverse else (row >= s)
        b3 = jnp.where(ok, a3 * b_sh + b3, b3)
        a3 = jnp.where(ok, a3 * a_sh, a3)
    return a3.reshape(m, c), b3.reshape(m, c)


def _scan_level2(a_ref, b_ref, out_ref, carry, reverse):
    m, c = a_ref.shape
    ng = m // SUBLANES

    def step(g, cr):
        gi = (ng - 1 - g) if reverse else g
        off = pl.multiple_of(gi * SUBLANES, SUBLANES)
        h = b_ref[pl.ds(off, SUBLANES), :] + a_ref[pl.ds(off, SUBLANES), :] * cr
        out_ref[pl.ds(off, SUBLANES), :] = h
        edge = h[0:1, :] if reverse else h[SUBLANES - 1:SUBLANES, :]
        return jnp.broadcast_to(edge, (SUBLANES, c))

    return lax.fori_loop(0, ng, step, carry, unroll=4)


def _mixer_recompute(u, hal, t0, cw, cb, gw_ref, ba, bx, lam, pw_ref, pb, ps):
    tm = u.shape[0]
    lw = cb.shape[1]
    u_l, u_g, u_p = u[:, :lw], u[:, lw:2 * lw], u[:, 2 * lw:]
    hal_l, hal_p = hal[:, :lw], hal[:, 2 * lw:]
    taps = [_shift_down(hal_l, u_l, CONV_WIDTH - 1 - k) for k in range(CONV_WIDTH)]
    xc = cb
    for k in range(CONV_WIDTH):
        xc = xc + taps[k] * cw[k:k + 1, :]
    xcb = xc.astype(BF16)
    nb = lw // GATE_BLOCK
    gs = [_dot(xcb[:, b * GATE_BLOCK:(b + 1) * GATE_BLOCK], gw_ref[b]) for b in range(nb)]
    r = _sigmoid(jnp.concatenate([g[:, :GATE_BLOCK] for g in gs], axis=1) + ba)
    ig = _sigmoid(jnp.concatenate([g[:, GATE_BLOCK:] for g in gs], axis=1) + bx)
    sp = _softplus_neg(lam)
    la = (-LRU_C * r) * sp
    a = jnp.exp(la)
    m2raw = _neg_expm1(2.0 * la)
    mult = jnp.sqrt(jnp.maximum(m2raw, 1e-12))
    ge, dge = _gelu_parts(u_g)
    row = lax.broadcasted_iota(jnp.int32, (tm, LANES), 0) + t0
    pooled, invs, zs = [], [], []
    for gi, w in enumerate(POOL_WINDOWS):
        e = jnp.concatenate([hal_p[:, gi * LANES:(gi + 1) * LANES], u_p[:, gi * LANES:(gi + 1) * LANES]], axis=0)
        s = e
        k = 1
        while k < w:
            s = s + pltpu.roll(s, k, 0)
            k *= 2
        inv = 1.0 / jnp.minimum(row + 1, w).astype(F32)
        pg = s[HALO:] * inv - e[HALO:]
        pooled.append(pg)
        invs.append(inv)
        zs.append(_dot(pg.astype(BF16), pw_ref[:, gi * LANES:(gi + 1) * LANES].astype(BF16)))
    z = jnp.concatenate(zs, axis=1) + pb
    y_pool = z * ps
    return dict(u_l=u_l, u_g=u_g, taps=taps, xc=xc, xcb=xcb, r=r, ig=ig, sp=sp, la=la, a=a, m2raw=m2raw,
                mult=mult, ge=ge, dge=dge, pooled=pooled, invs=invs, z=z, y_pool=y_pool)


ANY = pl.BlockSpec(memory_space=pl.ANY)
VMEM_SPEC = pl.BlockSpec(memory_space=pltpu.VMEM)


class _Hosted:
    def __init__(self, ins, out_shapes, sems, start, finish, mid=None, aliases=None):
        self.ins, self.out_shapes, self.sems = list(ins), list(out_shapes), list(sems)
        self.start, self.mid, self.finish = start, mid, finish
        self.aliases = dict(aliases or {})


def _call(body, hosted, stage_preds, *, name, grid, in_specs, out_specs, out_shape, scratch_shapes, args, sem):
    hosted = list(hosted or [])
    n_in, n_out, n_scr = len(in_specs), len(out_specs), len(scratch_shapes)
    c_in = [a for h in hosted for a in h.ins]
    c_out = [o for h in hosted for o in h.out_shapes]
    c_sem = [pltpu.SemaphoreType.DMA((k,)) for h in hosted for k in h.sems]

    def full(*refs):
        p = 0
        parts = []
        for cnt in (n_in, len(c_in), n_out, len(c_out), n_scr, len(c_sem)):
            parts.append(refs[p:p + cnt])
            p += cnt
        hi, ci, ho, co, hs, cs = parts
        per = []
        a = b = c_ = 0
        for h in hosted:
            per.append((h, ci[a:a + len(h.ins)], co[b:b + len(h.out_shapes)], cs[c_:c_ + len(h.sems)]))
            a, b, c_ = a + len(h.ins), b + len(h.out_shapes), c_ + len(h.sems)
        first = mid = last = None
        if hosted and grid:
            first, mid, last = stage_preds()

        def run(fn, pred, i_, o_, s_):
            if fn is None:
                return
            if pred is None:
                fn(i_, o_, s_)
            else:
                pl.when(pred)(functools.partial(fn, i_, o_, s_))

        for h, i_, o_, s_ in per:
            run(h.start, first, i_, o_, s_)
        body(*hi, *ho, *hs)
        for h, i_, o_, s_ in per:
            run(h.mid, mid, i_, o_, s_)
        for h, i_, o_, s_ in per:
            run(h.finish, last, i_, o_, s_)

    aliases = {}
    a = b = 0
    for h in hosted:
        for k, v in h.aliases.items():
            aliases[n_in + a + k] = n_out + b + v
        a, b = a + len(h.ins), b + len(h.out_shapes)
    res = pl.pallas_call(
        full, name=name, grid=grid, in_specs=list(in_specs) + [ANY] * len(c_in),
        out_specs=list(out_specs) + [ANY] * len(c_out), out_shape=list(out_shape) + c_out,
        scratch_shapes=list(scratch_shapes) + c_sem, input_output_aliases=aliases,
        compiler_params=_cp(sem))(*args, *c_in)
    res = list(res)
    outs = []
    p = n_out
    for h in hosted:
        outs.append(res[p:p + len(h.out_shapes)])
        p += len(h.out_shapes)
    return res[:n_out], outs


def _inproj(x, g_mix, w_in):
    s, d = x.shape
    n = w_in.shape[1]
    tm = min(TM_PROJ, s)

    def body(x_ref, g_ref, w_ref, u_ref):
        xv = x_ref[...]
        r = lax.rsqrt(_rowmean(xv * xv) + EPS)
        u_ref[...] = _dot((xv * r * g_ref[...]).astype(BF16), w_ref[...])

    return pl.pallas_call(
        body, grid=(s // tm,), name="inproj",
        in_specs=[pl.BlockSpec((tm, d), lambda i: (i, 0)), _const_spec((1, d)), _const_spec((d, n))],
        out_specs=pl.BlockSpec((tm, n), lambda i: (i, 0)),
        out_shape=SDS((s, n), F32), compiler_params=_cp(("parallel",)))(x, g_mix, w_in)


def _mixer_fwd(u, x, sp_, w_out, hosted=None):
    s, din = u.shape
    d = x.shape[1]
    lw = din // 3
    tm = min(TM_MIX, s)
    nb = lw // GATE_BLOCK

    def body(u_ref, halo_ref, x_ref, cw_ref, cb_ref, ga_ref, gx_ref, ba_ref, bx_ref, lam_ref, pw_ref, pb_ref,
             ps_ref, gl_ref, gp_ref, wout_ref, h_ref, yn_ref, hres_ref, gw_s, a_s, b_s, carry_s):
        i = pl.program_id(0)

        @pl.when(i == 0)
        def _():
            _build_gate_blocks(ga_ref, gx_ref, gw_s)
            carry_s[...] = jnp.zeros_like(carry_s)

        uv = u_ref[...]
        hal = jnp.where(i > 0, halo_ref[...], 0.0)
        f = _mixer_recompute(uv, hal, i * tm, cw_ref[...], cb_ref[...], gw_s, ba_ref[...], bx_ref[...],
                             lam_ref[...], pw_ref, pb_ref[...], ps_ref[...])
        bb = f["mult"] * (f["ig"] * f["xc"])
        a1, b1 = _scan_level1(f["a"], bb, reverse=False)
        a_s[...] = a1
        b_s[...] = b1
        carry_s[...] = _scan_level2(a_s, b_s, h_ref, carry_s[...], reverse=False)
        y_lru = h_ref[...] * f["ge"]
        rl = lax.rsqrt(_rowmean(y_lru * y_lru) + EPS)
        yp = f["y_pool"]
        rp = lax.rsqrt(_rowmean(yp * yp) + EPS)
        yn = jnp.concatenate([y_lru * rl * gl_ref[...], yp * rp * gp_ref[...]], axis=1).astype(BF16)
        yn_ref[...] = yn
        hres_ref[...] = x_ref[...] + _dot(yn, wout_ref[...])

    small = [sp_[k] for k in ("conv_w", "conv_b", "gate_a_w", "gate_x_w", "gate_a_b", "gate_x_b", "lru_lambda",
                              "pool_w", "pool_b", "pool_scale", "norm_lru_g", "norm_pool_g")]
    nt = s // tm

    def stages():
        i = pl.program_id(0)
        return i == 0, i == max(nt - 3, 0), i == nt - 1

    return _call(
        body, hosted, stages, grid=(nt,), name="mixer_fwd",
        in_specs=[pl.BlockSpec((tm, din), lambda i: (i, 0)),
                  pl.BlockSpec((HALO, din), lambda i: (jnp.maximum(i * (tm // HALO) - 1, 0), 0)),
                  pl.BlockSpec((tm, d), lambda i: (i, 0))]
        + [_const_spec(a.shape) for a in small] + [_const_spec(w_out.shape)],
        out_specs=[pl.BlockSpec((tm, lw), lambda i: (i, 0)), pl.BlockSpec((tm, d), lambda i: (i, 0)),
                   pl.BlockSpec((tm, d), lambda i: (i, 0))],
        out_shape=[SDS((s, lw), F32), SDS((s, d), BF16), SDS((s, d), F32)],
        scratch_shapes=[pltpu.VMEM((nb, GATE_BLOCK, 2 * GATE_BLOCK), BF16), pltpu.VMEM((tm, lw), F32),
                        pltpu.VMEM((tm, lw), F32), pltpu.VMEM((SUBLANES, lw), F32)],
        args=(u, u, x, *small, w_out), sem=("arbitrary",))


def _ffn_fwd(hres1, target, g_ffn, g_fin, w1, w3, w2):
    s, d = hres1.shape
    nj, _, fc = w1.shape
    tm = min(TM_FFN, s)

    def body(h_ref, t_ref, gf_ref, gn_ref, w1_ref, w3_ref, w2_ref,
             a1_ref, a3_ref, h2_ref, dh_ref, dhb_ref, loss_ref, dgn_ref, acc_s):
        i, j = pl.program_id(0), pl.program_id(1)

        @pl.when((i == 0) & (j == 0))
        def _():
            loss_ref[...] = jnp.zeros_like(loss_ref)
            dgn_ref[...] = jnp.zeros_like(dgn_ref)

        @pl.when(j == 0)
        def _():
            hv = h_ref[...]
            r = lax.rsqrt(_rowmean(hv * hv) + EPS)
            h2_ref[...] = (hv * r * gf_ref[...]).astype(BF16)

        h2 = h2_ref[...]
        a1 = _dot(h2, w1_ref[0])
        a3 = _dot(h2, w3_ref[0])
        a1_ref[0] = a1.astype(BF16)
        a3_ref[0] = a3.astype(BF16)
        part = _dot(((a1 * _sigmoid(a1)) * a3).astype(BF16), w2_ref[0])

        @pl.when(j == 0)
        def _():
            acc_s[...] = part

        @pl.when(j > 0)
        def _():
            acc_s[...] += part

        @pl.when(j == nj - 1)
        def _():
            hr2 = h_ref[...] + acc_s[...]
            r2 = lax.rsqrt(_rowmean(hr2 * hr2) + EPS)
            xh = hr2 * r2
            gn = gn_ref[...]
            diff = xh * gn - t_ref[...]
            tot = jnp.sum(jnp.sum(diff * diff, axis=1, keepdims=True), axis=0, keepdims=True)
            loss_ref[...] += tot * (0.5 / d)
            dout = diff * (1.0 / d)
            dgn_ref[...] += _colsum8(dout * xh)
            dh = _rms_bwd(dout, xh, r2, gn)
            dh_ref[...] = dh
            dhb_ref[...] = dh.astype(BF16)

    return pl.pallas_call(
        body, grid=(s // tm, nj), name="ffn_fwd",
        in_specs=[pl.BlockSpec((tm, d), lambda i, j: (i, 0)), pl.BlockSpec((tm, d), lambda i, j: (i, 0)),
                  _const_spec((1, d)), _const_spec((1, d)),
                  pl.BlockSpec((1, d, fc), lambda i, j: (j, 0, 0)), pl.BlockSpec((1, d, fc), lambda i, j: (j, 0, 0)),
                  pl.BlockSpec((1, fc, d), lambda i, j: (j, 0, 0))],
        out_specs=[pl.BlockSpec((1, tm, fc), lambda i, j: (j, i, 0)), pl.BlockSpec((1, tm, fc), lambda i, j: (j, i, 0)),
                   pl.BlockSpec((tm, d), lambda i, j: (i, 0)), pl.BlockSpec((tm, d), lambda i, j: (i, 0)),
                   pl.BlockSpec((tm, d), lambda i, j: (i, 0)),
                   pl.BlockSpec((SUBLANES, LANES), lambda i, j: (0, 0)),
                   pl.BlockSpec((SUBLANES, d), lambda i, j: (0, 0))],
        out_shape=[SDS((nj, s, fc), BF16), SDS((nj, s, fc), BF16), SDS((s, d), BF16), SDS((s, d), F32),
                   SDS((s, d), BF16), SDS((SUBLANES, LANES), F32), SDS((SUBLANES, d), F32)],
        scratch_shapes=[pltpu.VMEM((tm, d), F32)],
        compiler_params=_cp(("arbitrary", "arbitrary")))(hres1, target, g_ffn, g_fin, w1, w3, w2)


def _ffn_bwd_act(dh, dhb, a1, a3, hres1, g_ffn, w1, w3, w2):
    s, d = hres1.shape
    nj, _, fc = a1.shape
    tm = min(TM_FFN, s)

    def body(dh_ref, dhb_ref, a1_ref, a3_ref, h_ref, gf_ref, w1_ref, w3_ref, w2_ref,
             da1_ref, da3_ref, dhr_ref, dgf_ref, acc_s):
        i, j = pl.program_id(0), pl.program_id(1)

        @pl.when((i == 0) & (j == 0))
        def _():
            dgf_ref[...] = jnp.zeros_like(dgf_ref)

        @pl.when(j == 0)
        def _():
            acc_s[...] = jnp.zeros_like(acc_s)

        rc = tm // FFN_ROW_CHUNKS
        for q in range(FFN_ROW_CHUNKS):
            rows = slice(q * rc, (q + 1) * rc)
            dff = _dot_nt(dhb_ref[rows, :], w2_ref[0])
            a1v = a1_ref[0, rows, :].astype(F32)
            a3v = a3_ref[0, rows, :].astype(F32)
            sg = _sigmoid(a1v)
            silu = a1v * sg
            da1 = (dff * a3v * (sg * (1.0 + a1v * (1.0 - sg)))).astype(BF16)
            da3 = (dff * silu).astype(BF16)
            da1_ref[0, rows, :] = da1
            da3_ref[0, rows, :] = da3
            acc_s[rows, :] += _dot_nt(da1, w1_ref[0]) + _dot_nt(da3, w3_ref[0])

        @pl.when(j == nj - 1)
        def _():
            hv = h_ref[...]
            r = lax.rsqrt(_rowmean(hv * hv) + EPS)
            xh = hv * r
            dh2 = acc_s[...]
            dgf_ref[...] += _colsum8(dh2 * xh)
            dhr_ref[...] = dh_ref[...] + _rms_bwd(dh2, xh, r, gf_ref[...])

    return pl.pallas_call(
        body, grid=(s // tm, nj), name="ffn_bwd_act",
        in_specs=[pl.BlockSpec((tm, d), lambda i, j: (i, 0)), pl.BlockSpec((tm, d), lambda i, j: (i, 0)),
                  pl.BlockSpec((1, tm, fc), lambda i, j: (j, i, 0)), pl.BlockSpec((1, tm, fc), lambda i, j: (j, i, 0)),
                  pl.BlockSpec((tm, d), lambda i, j: (i, 0)), _const_spec((1, d)),
                  pl.BlockSpec((1, d, fc), lambda i, j: (j, 0, 0)), pl.BlockSpec((1, d, fc), lambda i, j: (j, 0, 0)),
                  pl.BlockSpec((1, fc, d), lambda i, j: (j, 0, 0))],
        out_specs=[pl.BlockSpec((1, tm, fc), lambda i, j: (j, i, 0)), pl.BlockSpec((1, tm, fc), lambda i, j: (j, i, 0)),
                   pl.BlockSpec((tm, d), lambda i, j: (i, 0)), pl.BlockSpec((SUBLANES, d), lambda i, j: (0, 0))],
        out_shape=[SDS((nj, s, fc), BF16), SDS((nj, s, fc), BF16), SDS((s, d), F32), SDS((SUBLANES, d), F32)],
        scratch_shapes=[pltpu.VMEM((tm, d), F32)],
        compiler_params=_cp(("arbitrary", "arbitrary")))(dh, dhb, a1, a3, hres1, g_ffn, w1, w3, w2)


def _ffn_wgrad(h2, dhb, a1, a3, da1, da3):
    s, d = h2.shape
    _, _, fc = a1.shape
    tm = min(TM_WGRAD, s)

    def body(h2_ref, dhb_ref, a1_ref, a3_ref, da1_ref, da3_ref, dw1_ref, dw3_ref, dw2_ref):
        i = pl.program_id(1)

        @pl.when(i == 0)
        def _():
            dw1_ref[...] = jnp.zeros_like(dw1_ref)
            dw3_ref[...] = jnp.zeros_like(dw3_ref)
            dw2_ref[...] = jnp.zeros_like(dw2_ref)

        h2v = h2_ref[...]
        a1v = a1_ref[0].astype(F32)
        ff = ((a1v * _sigmoid(a1v)) * a3_ref[0].astype(F32)).astype(BF16)
        dw1_ref[0] += _dot_tn(h2v, da1_ref[0])
        dw3_ref[0] += _dot_tn(h2v, da3_ref[0])
        dw2_ref[0] += _dot_tn(ff, dhb_ref[...])

    return pl.pallas_call(
        body, grid=(N_CHIPS, s // tm), name="ffn_wgrad",
        in_specs=[pl.BlockSpec((tm, d), lambda j, i: (i, 0)), pl.BlockSpec((tm, d), lambda j, i: (i, 0))]
        + [pl.BlockSpec((1, tm, fc), lambda j, i: (j, i, 0))] * 4,
        out_specs=[pl.BlockSpec((1, d, fc), lambda j, i: (j, 0, 0)), pl.BlockSpec((1, d, fc), lambda j, i: (j, 0, 0)),
                   pl.BlockSpec((1, fc, d), lambda j, i: (j, 0, 0))],
        out_shape=[SDS((N_CHIPS, d, fc), F32), SDS((N_CHIPS, d, fc), F32), SDS((N_CHIPS, fc, d), F32)],
        compiler_params=_cp(("parallel", "arbitrary")))(h2, dhb, a1, a3, da1, da3)


def _row_chunks(tm):
    rc = tm // FFN_ROW_CHUNKS
    return [slice(q * rc, (q + 1) * rc) for q in range(FFN_ROW_CHUNKS)]


def _ffn_up(hres1, g_ffn, w1, w3):
    s, d = hres1.shape
    nj, _, fc = w1.shape
    tm = min(TM_FFN_UP, s)

    def body(h_ref, gf_ref, w1_ref, w3_ref, h2_ref, a1_ref, a3_ref, ff_ref):
        @pl.when(pl.program_id(1) == 0)
        def _():
            hv = h_ref[...]
            r = lax.rsqrt(_rowmean(hv * hv) + EPS)
            h2_ref[...] = (hv * r * gf_ref[...]).astype(BF16)

        for rows in _row_chunks(tm):
            h2 = h2_ref[rows, :]
            a1 = _dot(h2, w1_ref[0])
            a3 = _dot(h2, w3_ref[0])
            a1_ref[0, rows, :] = a1.astype(BF16)
            a3_ref[0, rows, :] = a3.astype(BF16)
            ff_ref[0, rows, :] = ((a1 * _sigmoid(a1)) * a3).astype(BF16)

    wspec = pl.BlockSpec((1, d, fc), lambda i, j: (j, 0, 0))
    aspec = pl.BlockSpec((1, tm, fc), lambda i, j: (j, i, 0))
    return pl.pallas_call(
        body, grid=(s // tm, nj), name="ffn_up",
        in_specs=[pl.BlockSpec((tm, d), lambda i, j: (i, 0)), _const_spec((1, d)), wspec, wspec],
        out_specs=[pl.BlockSpec((tm, d), lambda i, j: (i, 0)), aspec, aspec, aspec],
        out_shape=[SDS((s, d), BF16)] + [SDS((nj, s, fc), BF16)] * 3,
        compiler_params=_cp(("parallel", "arbitrary")))(hres1, g_ffn, w1, w3)


def _ffn_down(ff, hres1, target, g_fin, w2):
    s, d = hres1.shape
    nj, _, fc = ff.shape
    tm = min(TM_FFN_DOWN, s)

    def body(ff_ref, h_ref, t_ref, gn_ref, w2_ref, dh_ref, dhb_ref, loss_ref, dgn_ref, acc_s):
        i, j = pl.program_id(0), pl.program_id(1)

        @pl.when((i == 0) & (j == 0))
        def _():
            loss_ref[...] = jnp.zeros_like(loss_ref)
            dgn_ref[...] = jnp.zeros_like(dgn_ref)

        @pl.when(j == 0)
        def _():
            acc_s[...] = jnp.zeros_like(acc_s)

        acc_s[...] += _dot(ff_ref[0], w2_ref[0])

        @pl.when(j == nj - 1)
        def _():
            hr2 = h_ref[...] + acc_s[...]
            r2 = lax.rsqrt(_rowmean(hr2 * hr2) + EPS)
            xh = hr2 * r2
            gn = gn_ref[...]
            diff = xh * gn - t_ref[...]
            tot = jnp.sum(jnp.sum(diff * diff, axis=1, keepdims=True), axis=0, keepdims=True)
            loss_ref[...] += tot * (0.5 / d)
            dout = diff * (1.0 / d)
            dgn_ref[...] += _colsum8(dout * xh)
            dh = _rms_bwd(dout, xh, r2, gn)
            dh_ref[...] = dh
            dhb_ref[...] = dh.astype(BF16)

    tile = pl.BlockSpec((tm, d), lambda i, j: (i, 0))
    return pl.pallas_call(
        body, grid=(s // tm, nj), name="ffn_down",
        in_specs=[pl.BlockSpec((1, tm, fc), lambda i, j: (j, i, 0)), tile, tile, _const_spec((1, d)),
                  pl.BlockSpec((1, fc, d), lambda i, j: (j, 0, 0))],
        out_specs=[tile, tile, pl.BlockSpec((SUBLANES, LANES), lambda i, j: (0, 0)),
                   pl.BlockSpec((SUBLANES, d), lambda i, j: (0, 0))],
        out_shape=[SDS((s, d), F32), SDS((s, d), BF16), SDS((SUBLANES, LANES), F32), SDS((SUBLANES, d), F32)],
        scratch_shapes=[pltpu.VMEM((tm, d), F32)],
        compiler_params=_cp(("arbitrary", "arbitrary")))(ff, hres1, target, g_fin, w2)


def _ffn_bwd_gate(dhb, a1, a3, w2):
    s, d = dhb.shape
    nj, _, fc = a1.shape
    tm = min(TM_FFN_UP, s)

    def body(dhb_ref, a1_ref, a3_ref, w2_ref, da1_ref, da3_ref):
        for rows in _row_chunks(tm):
            dff = _dot_nt(dhb_ref[rows, :], w2_ref[0])
            a1v = a1_ref[0, rows, :].astype(F32)
            sg = _sigmoid(a1v)
            silu = a1v * sg
            da1_ref[0, rows, :] = (dff * a3_ref[0, rows, :].astype(F32) * (sg * (1.0 + (a1v - silu)))).astype(BF16)
            da3_ref[0, rows, :] = (dff * silu).astype(BF16)

    aspec = pl.BlockSpec((1, tm, fc), lambda i, j: (j, i, 0))
    return pl.pallas_call(
        body, grid=(s // tm, nj), name="ffn_bwd_gate",
        in_specs=[pl.BlockSpec((tm, d), lambda i, j: (i, 0)), aspec, aspec,
                  pl.BlockSpec((1, fc, d), lambda i, j: (j, 0, 0))],
        out_specs=[aspec, aspec], out_shape=[SDS((nj, s, fc), BF16)] * 2,
        compiler_params=_cp(("parallel", "arbitrary")))(dhb, a1, a3, w2)


def _ffn_bwd_down(da1, da3, dh, hres1, g_ffn, w1, w3, hosted=None):
    s, d = hres1.shape
    nj, _, fc = da1.shape
    tm = min(TM_FFN_DOWN, s)
    nt = s // tm

    def body(da1_ref, da3_ref, dh_ref, h_ref, gf_ref, w1_ref, w3_ref, dhr_ref, dgf_ref, acc_s):
        i, j = pl.program_id(0), pl.program_id(1)

        @pl.when((i == 0) & (j == 0))
        def _():
            dgf_ref[...] = jnp.zeros_like(dgf_ref)

        @pl.when(j == 0)
        def _():
            acc_s[...] = jnp.zeros_like(acc_s)

        acc_s[...] += _dot_nt(da1_ref[0], w1_ref[0]) + _dot_nt(da3_ref[0], w3_ref[0])

        @pl.when(j == nj - 1)
        def _():
            hv = h_ref[...]
            r = lax.rsqrt(_rowmean(hv * hv) + EPS)
            xh = hv * r
            dh2 = acc_s[...]
            dgf_ref[...] += _colsum8(dh2 * xh)
            dhr_ref[...] = dh_ref[...] + _rms_bwd(dh2, xh, r, gf_ref[...])

    tile = pl.BlockSpec((tm, d), lambda i, j: (i, 0))
    aspec = pl.BlockSpec((1, tm, fc), lambda i, j: (j, i, 0))
    wspec = pl.BlockSpec((1, d, fc), lambda i, j: (j, 0, 0))
    def stages():
        i, j = pl.program_id(0), pl.program_id(1)
        return (i == 0) & (j == 0), (i == max(nt - 2, 0)) & (j == 0), (i == nt - 1) & (j == nj - 1)

    return _call(
        body, hosted, stages, grid=(nt, nj), name="ffn_bwd_down",
        in_specs=[aspec, aspec, tile, tile, _const_spec((1, d)), wspec, wspec],
        out_specs=[tile, pl.BlockSpec((SUBLANES, d), lambda i, j: (0, 0))],
        out_shape=[SDS((s, d), F32), SDS((SUBLANES, d), F32)],
        scratch_shapes=[pltpu.VMEM((tm, d), F32)],
        args=(da1, da3, dh, hres1, g_ffn, w1, w3), sem=("arbitrary", "arbitrary"))


def _ffn_wgrad2(h2, dhb, ff, da1, da3):
    s, d = h2.shape
    _, _, fc = ff.shape
    tm = min(TM_WGRAD, s)

    def body(h2_ref, dhb_ref, ff_ref, da1_ref, da3_ref, dw1_ref, dw3_ref, dw2_ref):
        @pl.when(pl.program_id(1) == 0)
        def _():
            dw1_ref[...] = jnp.zeros_like(dw1_ref)
            dw3_ref[...] = jnp.zeros_like(dw3_ref)
            dw2_ref[...] = jnp.zeros_like(dw2_ref)

        h2v = h2_ref[...]
        dw1_ref[0] += _dot_tn(h2v, da1_ref[0])
        dw3_ref[0] += _dot_tn(h2v, da3_ref[0])
        dw2_ref[0] += _dot_tn(ff_ref[0], dhb_ref[...])

    return pl.pallas_call(
        body, grid=(N_CHIPS, s // tm), name="ffn_wgrad",
        in_specs=[pl.BlockSpec((tm, d), lambda j, i: (i, 0)), pl.BlockSpec((tm, d), lambda j, i: (i, 0))]
        + [pl.BlockSpec((1, tm, fc), lambda j, i: (j, i, 0))] * 3,
        out_specs=[pl.BlockSpec((1, d, fc), lambda j, i: (j, 0, 0)), pl.BlockSpec((1, d, fc), lambda j, i: (j, 0, 0)),
                   pl.BlockSpec((1, fc, d), lambda j, i: (j, 0, 0))],
        out_shape=[SDS((N_CHIPS, d, fc), F32), SDS((N_CHIPS, d, fc), F32), SDS((N_CHIPS, fc, d), F32)],
        compiler_params=_cp(("parallel", "arbitrary")))(h2, dhb, ff, da1, da3)


def _mixer_bwd(u, h, dhres1, sp_, w_out, hosted=None):
    s, din = u.shape
    d = dhres1.shape[1]
    lw = din // 3
    tm = min(TM_MIX, s)
    nt = s // tm
    nb = lw // GATE_BLOCK
    hd = sp_["gate_a_w"].shape[0]

    def body(u_ref, halo_ref, h_ref, hhalo_ref, dhr_ref, cw_ref, cb_ref, ga_ref, gx_ref, ba_ref, bx_ref, lam_ref,
             pw_ref, pb_ref, ps_ref, gl_ref, gp_ref, wout_ref, du_ref, slab_ref,
             gw_s, a_s, b_s, e_s, ecarry_s, dxc_s, q_s, vec_s, cwacc_s, dgw_s, dpw_s):
        i = pl.program_id(0)
        tile = nt - 1 - i

        @pl.when(i == 0)
        def _():
            _build_gate_blocks(ga_ref, gx_ref, gw_s)
            for ref in (ecarry_s, dxc_s, q_s, vec_s, cwacc_s, dgw_s, dpw_s):
                ref[...] = jnp.zeros_like(ref)

        uv = u_ref[...]
        hal = jnp.where(tile > 0, halo_ref[...], 0.0)
        cw = cw_ref[...]
        lam = lam_ref[...]
        ps = ps_ref[...]
        f = _mixer_recompute(uv, hal, tile * tm, cw, cb_ref[...], gw_s, ba_ref[...], bx_ref[...], lam, pw_ref,
                             pb_ref[...], ps)
        hv = h_ref[...]
        h_prev = _shift_down(jnp.where(tile > 0, hhalo_ref[...], 0.0), hv, 1)
        y_lru = hv * f["ge"]
        rl = lax.rsqrt(_rowmean(y_lru * y_lru) + EPS)
        yp = f["y_pool"]
        rp = lax.rsqrt(_rowmean(yp * yp) + EPS)
        xh_l = y_lru * rl
        xh_p = yp * rp

        dyn = _dot_nt(dhr_ref[...].astype(BF16), wout_ref[...])
        d_nl, d_np = dyn[:, :lw], dyn[:, lw:]
        vec = {}
        vec[ROW_GL] = _colsum8(d_nl * xh_l)
        vec[ROW_GP] = _colsum8(d_np * xh_p)
        d_ylru = _rms_bwd(d_nl, xh_l, rl, gl_ref[...])
        d_ypool = _rms_bwd(d_np, xh_p, rp, gp_ref[...])

        vec[ROW_PS] = _colsum8(d_ypool * f["z"])
        dz = d_ypool * ps
        vec[ROW_PB] = _colsum8(dz)
        dzb = dz.astype(BF16)
        dup = []
        for gi, w in enumerate(POOL_WINDOWS):
            sl = slice(gi * LANES, (gi + 1) * LANES)
            dpw_s[:, sl] += _dot_tn(f["pooled"][gi].astype(BF16), dzb[:, sl])
            dpool = _dot_nt(dzb[:, sl], pw_ref[:, sl].astype(BF16))
            q = dpool * f["invs"][gi]
            e = jnp.concatenate([q, q_s[:, sl]], axis=0)
            k = 1
            while k < w:
                e = e + pltpu.roll(e, tm + HALO - k, 0)
                k *= 2
            dup.append(e[:tm] - dpool)
            q_s[:, sl] = q[:HALO]

        d_hout = d_ylru * f["ge"]
        d_ug = d_ylru * hv * f["dge"]
        a = f["a"]
        a1, b1 = _scan_level1(a, a * d_hout, reverse=True)
        a_s[...] = a1
        b_s[...] = b1
        e_next = ecarry_s[...]
        ecarry_s[...] = _scan_level2(a_s, b_s, e_s, e_next, reverse=True)
        sv = d_hout + _shift_up(e_s[...], e_next, 1)
        d_a = sv * h_prev
        mult, ig, xc, r = f["mult"], f["ig"], f["xc"], f["r"]
        d_mult = sv * (ig * xc)
        d_ig = sv * mult * xc
        d_xc = sv * mult * ig
        d_la = d_a * a + jnp.where(f["m2raw"] > 1e-12, d_mult * (-(a * a) / mult), 0.0)
        d_r = d_la * (-LRU_C * f["sp"])
        vec[ROW_LAM] = _colsum8(d_la * (-LRU_C * r))
        d_pr = d_r * r * (1.0 - r)
        d_pi = d_ig * ig * (1.0 - ig)
        vec[ROW_BA] = _colsum8(d_pr)
        vec[ROW_BX] = _colsum8(d_pi)
        dxc_parts = []
        for b in range(nb):
            sl = slice(b * GATE_BLOCK, (b + 1) * GATE_BLOCK)
            rhs = jnp.concatenate([d_pr[:, sl], d_pi[:, sl]], axis=1).astype(BF16)
            dgw_s[b] += _dot_tn(f["xcb"][:, sl], rhs)
            dxc_parts.append(_dot_nt(rhs, gw_s[b]))
        d_xc = d_xc + jnp.concatenate(dxc_parts, axis=1)
        vec[ROW_CONV_B] = _colsum8(d_xc)
        dxc_next = dxc_s[...]
        d_ul = None
        for k in range(CONV_WIDTH):
            cwacc_s[k * SUBLANES:(k + 1) * SUBLANES, :] += _colsum8(d_xc * f["taps"][k])
            term = _shift_up(d_xc, dxc_next, CONV_WIDTH - 1 - k) * cw[k:k + 1, :]
            d_ul = term if d_ul is None else d_ul + term
        dxc_s[...] = d_xc[:SUBLANES]
        for row, val in vec.items():
            vec_s[row * SUBLANES:(row + 1) * SUBLANES, :] += val
        du_ref[...] = jnp.concatenate([d_ul, d_ug] + dup, axis=1).astype(BF16)

        @pl.when(i == nt - 1)
        def _():
            rows = []
            for row in range(ROW_GA):
                if row in (ROW_CONV_W, ROW_CONV_W + 1, ROW_CONV_W + 2, ROW_CONV_W + 3):
                    k = row - ROW_CONV_W
                    v = jnp.sum(cwacc_s[k * SUBLANES:(k + 1) * SUBLANES, :], axis=0, keepdims=True)
                elif row <= ROW_GP:
                    v = jnp.sum(vec_s[row * SUBLANES:(row + 1) * SUBLANES, :], axis=0, keepdims=True)
                    if row == ROW_LAM:
                        v = v * (-1.0 / (1.0 + jnp.exp(lam)))
                else:
                    v = jnp.zeros((1, lw), F32)
                rows.append(v)
            slab_ref[0:ROW_GA, :] = jnp.concatenate(rows, axis=0)
            lane = lax.broadcasted_iota(jnp.int32, (hd, GATE_BLOCK), 1)
            for b in range(nb):
                for off, row0 in ((0, ROW_GA), (GATE_BLOCK, ROW_GX)):
                    acc = jnp.zeros((hd, GATE_BLOCK), F32)
                    for hh in range(GATE_BLOCK // hd):
                        m = (lane >= hh * hd) & (lane < (hh + 1) * hd)
                        acc = acc + jnp.where(m, dgw_s[b, hh * hd:(hh + 1) * hd, off:off + GATE_BLOCK], 0.0)
                    slab_ref[row0:row0 + hd, b * GATE_BLOCK:(b + 1) * GATE_BLOCK] = acc
            slab_ref[ROW_PW:ROW_PW + LANES, :] = dpw_s[...]

    small = [sp_[k] for k in ("conv_w", "conv_b", "gate_a_w", "gate_x_w", "gate_a_b", "gate_x_b", "lru_lambda",
                              "pool_w", "pool_b", "pool_scale", "norm_lru_g", "norm_pool_g")]
    rev = lambda i: nt - 1 - i

    def stages():
        i = pl.program_id(0)
        return i == 0, i == max(nt - 3, 0), i == nt - 1

    return _call(
        body, hosted, stages, grid=(nt,), name="mixer_bwd",
        in_specs=[pl.BlockSpec((tm, din), lambda i: (rev(i), 0)),
                  pl.BlockSpec((HALO, din), lambda i: (jnp.maximum(rev(i) * (tm // HALO) - 1, 0), 0)),
                  pl.BlockSpec((tm, lw), lambda i: (rev(i), 0)),
                  pl.BlockSpec((SUBLANES, lw), lambda i: (jnp.maximum(rev(i) * (tm // SUBLANES) - 1, 0), 0)),
                  pl.BlockSpec((tm, d), lambda i: (rev(i), 0))]
        + [_const_spec(a.shape) for a in small] + [_const_spec(w_out.shape)],
        out_specs=[pl.BlockSpec((tm, din), lambda i: (rev(i), 0)),
                   pl.BlockSpec((MIX_SLAB_ROWS, SLAB_W), lambda i: (0, 0))],
        out_shape=[SDS((s, din), BF16), SDS((MIX_SLAB_ROWS, SLAB_W), F32)],
        scratch_shapes=[pltpu.VMEM((nb, GATE_BLOCK, 2 * GATE_BLOCK), BF16),
                        pltpu.VMEM((tm, lw), F32), pltpu.VMEM((tm, lw), F32), pltpu.VMEM((tm, lw), F32),
                        pltpu.VMEM((SUBLANES, lw), F32), pltpu.VMEM((SUBLANES, lw), F32),
                        pltpu.VMEM((HALO, lw), F32), pltpu.VMEM((ROW_GA * SUBLANES, lw), F32),
                        pltpu.VMEM((CONV_WIDTH * SUBLANES, lw), F32),
                        pltpu.VMEM((nb, GATE_BLOCK, 2 * GATE_BLOCK), F32), pltpu.VMEM((LANES, lw), F32)],
        args=(u, u, h, h, dhres1, *small, w_out), sem=("arbitrary",))


def _inproj_bwd(x, du, dhres1, yn, g_mix, w_in, hosted=None):
    s, d = x.shape
    n = w_in.shape[1]
    nc = n // N_CHIPS
    tm = min(TM_PROJ, s)
    nt = s // tm

    def body(x_ref, du_ref, dhr_ref, yn_ref, g_ref, w_ref, gx_ref, dwin_ref, dwout_ref, dg_ref):
        i = pl.program_id(0)

        @pl.when(i == 0)
        def _():
            dwin_ref[...] = jnp.zeros_like(dwin_ref)
            dwout_ref[...] = jnp.zeros_like(dwout_ref)
            dg_ref[...] = jnp.zeros_like(dg_ref)

        xv = x_ref[...]
        g = g_ref[...]
        r = lax.rsqrt(_rowmean(xv * xv) + EPS)
        xh = xv * r
        h1 = (xh * g).astype(BF16)
        duv = du_ref[...]
        dh1 = _dot_nt(duv, w_ref[...])
        dg_ref[...] += _colsum8(dh1 * xh)
        dhr = dhr_ref[...]
        gx_ref[...] = dhr + _rms_bwd(dh1, xh, r, g)
        for jj in range(N_CHIPS):
            dwin_ref[jj] += _dot_tn(h1, duv[:, jj * nc:(jj + 1) * nc])
        dwout_ref[...] += _dot_tn(yn_ref[...], dhr.astype(BF16))

    def stages():
        i = pl.program_id(0)
        return i == 0, i == max(nt - 3, 0), i == nt - 1

    return _call(
        body, hosted, stages, grid=(nt,), name="inproj_bwd",
        in_specs=[pl.BlockSpec((tm, d), lambda i: (i, 0)), pl.BlockSpec((tm, n), lambda i: (i, 0)),
                  pl.BlockSpec((tm, d), lambda i: (i, 0)), pl.BlockSpec((tm, d), lambda i: (i, 0)),
                  _const_spec((1, d)), _const_spec((d, n))],
        out_specs=[pl.BlockSpec((tm, d), lambda i: (i, 0)), pl.BlockSpec((N_CHIPS, d, nc), lambda i: (0, 0, 0)),
                   pl.BlockSpec((d, d), lambda i: (0, 0)), pl.BlockSpec((SUBLANES, d), lambda i: (0, 0))],
        out_shape=[SDS((s, d), F32), SDS((N_CHIPS, d, nc), F32), SDS((d, d), F32), SDS((SUBLANES, d), F32)],
        scratch_shapes=[], args=(x, du, dhres1, yn, g_mix, w_in), sem=("arbitrary",))


def _place():
    x, y, c = lax.axis_index("x"), lax.axis_index("y"), lax.axis_index("c")
    return x, y, c


def _other_chips(x, y):
    return [(1 - x, y), (x, 1 - y), (1 - x, 1 - y)]


ANY = pl.BlockSpec(memory_space=pl.ANY)
VMEM_SPEC = pl.BlockSpec(memory_space=pltpu.VMEM)

_GATHERED = {"w_in": "cols", "w_out": "major", "ffn_w1": "major", "ffn_w3": "major", "ffn_w2": "major"}
_BIG = ("w_in", "w_out", "ffn_w1", "ffn_w3", "ffn_w2")


def _gather_weights(shards, conv_w, n_remote):
    n = len(shards)
    full_shapes = []
    for name, sh in zip(_BIG, shards):
        r, cdim = sh.shape
        if _GATHERED[name] == "cols":
            assert cdim % LANES == 0
            full_shapes.append((r, cdim * N_CHIPS))
        else:
            full_shapes.append((N_CHIPS, r, cdim))

    def region(ref, name, sh, jj, cc):
        r, cdim = sh
        rows = pl.ds(0, r) if cc is None else pl.ds(pl.multiple_of(cc * (r // 2), 16), r // 2)
        if _GATHERED[name] == "cols":
            return ref.at[rows, pl.ds(pl.multiple_of(jj * cdim, LANES), cdim)]
        return ref.at[jj, rows, :]

    def staged(ref, sh, cc):
        r = sh[0]
        return ref.at[pl.ds(pl.multiple_of(cc * (r // 2), 16), r // 2), :]

    def body(*refs):
        ins, cw_in = refs[:n], refs[n]
        outs, cw_out = refs[n + 1:2 * n + 1], refs[2 * n + 1]
        stage = refs[2 * n + 2:3 * n + 2]
        cw_stage, lsem, ssem, rsem, fssem, frsem, cssem, crsem = refs[3 * n + 2:]
        x, y, c = _place()
        j = 2 * x + y
        chips = _other_chips(x, y)
        for w in range(n):
            stage[w][...] = ins[w][...].astype(BF16)
        cw_stage[...] = jnp.zeros_like(cw_stage)
        cw_stage[0:CONV_WIDTH, :] = cw_in[...]
        shs = [s_.shape for s_ in shards]
        local = [pltpu.make_async_copy(stage[w], region(outs[w], _BIG[w], shs[w], j, None), lsem.at[w])
                 for w in range(n)]
        local.append(pltpu.make_async_copy(cw_stage, cw_out.at[j], lsem.at[n]))
        for cp in local:
            cp.start()
        sends = []
        for k, (px, py) in enumerate(chips):
            for w in range(n_remote):
                sends.append(pltpu.make_async_remote_copy(
                    src_ref=staged(stage[w], shs[w], c), dst_ref=region(outs[w], _BIG[w], shs[w], j, c),
                    send_sem=ssem.at[k * n + w], recv_sem=rsem.at[k * n + w], device_id=(px, py, c),
                    device_id_type=MESH))
            sends.append(pltpu.make_async_remote_copy(
                src_ref=cw_stage, dst_ref=cw_out.at[j], send_sem=cssem.at[k], recv_sem=crsem.at[k],
                device_id=(px, py, c), device_id_type=MESH))
        for cp in sends:
            cp.start()
        fwd = []
        for k, (px, py) in enumerate(chips):
            jk = 2 * px + py
            for w in range(n_remote):
                reg = region(outs[w], _BIG[w], shs[w], jk, c)
                pltpu.make_async_remote_copy(src_ref=reg, dst_ref=reg, send_sem=ssem.at[k * n + w],
                                             recv_sem=rsem.at[k * n + w], device_id=(px, py, c),
                                             device_id_type=MESH).wait_recv()
                cp = pltpu.make_async_remote_copy(src_ref=reg, dst_ref=reg, send_sem=fssem.at[k * n + w],
                                                  recv_sem=frsem.at[k * n + w], device_id=(x, y, 1 - c),
                                                  device_id_type=MESH)
                cp.start()
                fwd.append(cp)
            pltpu.make_async_remote_copy(src_ref=cw_stage, dst_ref=cw_out.at[jk], send_sem=cssem.at[k],
                                         recv_sem=crsem.at[k], device_id=(px, py, c),
                                         device_id_type=MESH).wait_recv()
        for k, (px, py) in enumerate(chips):
            jk = 2 * px + py
            for w in range(n_remote):
                reg = region(outs[w], _BIG[w], shs[w], jk, 1 - c)
                pltpu.make_async_remote_copy(src_ref=reg, dst_ref=reg, send_sem=fssem.at[k * n + w],
                                             recv_sem=frsem.at[k * n + w], device_id=(x, y, 1 - c),
                                             device_id_type=MESH).wait_recv()
        for cp in sends + fwd:
            cp.wait_send()
        for cp in local:
            cp.wait()

    nsem = 3 * n
    return pl.pallas_call(
        body, name="gather_first",
        in_specs=[VMEM_SPEC] * (n + 1), out_specs=[ANY] * (n + 1),
        out_shape=[SDS(fs, BF16) for fs in full_shapes] + [SDS((N_CHIPS, SUBLANES, LANES), F32)],
        scratch_shapes=[pltpu.VMEM(s_.shape, BF16) for s_ in shards] + [pltpu.VMEM((SUBLANES, LANES), F32)]
        + [pltpu.SemaphoreType.DMA((n + 1,))] + [pltpu.SemaphoreType.DMA((nsem,))] * 4
        + [pltpu.SemaphoreType.DMA((3,))] * 2,
        compiler_params=_cp())(*shards, conv_w)


def _start_all(make):
    def f(ins, outs, sems):
        for cp in make(ins, outs, sems):
            cp.start()
    return f


def _wait_all(make):
    def f(ins, outs, sems):
        for cp in make(ins, outs, sems):
            cp.wait()
    return f


def _ffn_gather_hosted(arrs):
    n = len(arrs)

    def make(outs, sems):
        ssem, rsem, fs, fr = sems
        x, y, c = _place()
        j = 2 * x + y

        def reg(w, jj, cc):
            hr = arrs[w].shape[1] // 2
            return outs[w].at[jj, pl.ds(pl.multiple_of(cc * hr, 16), hr), :]

        def rc(w, jj, cc, s_sem, r_sem, dev):
            return pltpu.make_async_remote_copy(src_ref=reg(w, jj, cc), dst_ref=reg(w, jj, cc), send_sem=s_sem,
                                                recv_sem=r_sem, device_id=dev, device_id_type=MESH)

        sends, recvs, fwds, frecvs = [], [], [], []
        for k, (px, py) in enumerate(_other_chips(x, y)):
            jk = 2 * px + py
            for w in range(n):
                q = k * n + w
                sends.append(rc(w, j, c, ssem.at[q], rsem.at[q], (px, py, c)))
                recvs.append(rc(w, jk, c, ssem.at[q], rsem.at[q], (px, py, c)))
                fwds.append(rc(w, jk, c, fs.at[q], fr.at[q], (x, y, 1 - c)))
                frecvs.append(rc(w, jk, 1 - c, fs.at[q], fr.at[q], (x, y, 1 - c)))
        return sends, recvs, fwds, frecvs

    def start(ins, outs, sems):
        for cp in make(outs, sems)[0]:
            cp.start()

    def mid(ins, outs, sems):
        _, recvs, fwds, _ = make(outs, sems)
        for r, f in zip(recvs, fwds):
            r.wait_recv()
            f.start()

    def finish(ins, outs, sems):
        sends, _, fwds, frecvs = make(outs, sems)
        for r in frecvs:
            r.wait_recv()
        for cp in sends + fwds:
            cp.wait_send()

    return _Hosted(arrs, [SDS(a.shape, a.dtype) for a in arrs], [3 * n] * 4, start, finish, mid=mid,
                   aliases={w: w for w in range(n)})


def _rs_sibling_hosted(arrs):
    n = len(arrs)

    def make(ins, outs, sems):
        x, y, c = _place()
        cps = []
        for w in range(n):
            hr = arrs[w].shape[1] // 2
            src = ins[w].at[:, pl.ds(pl.multiple_of((1 - c) * hr, SUBLANES), hr), :]
            cps.append(pltpu.make_async_remote_copy(src_ref=src, dst_ref=outs[w], send_sem=sems[0].at[w],
                                                    recv_sem=sems[1].at[w], device_id=(x, y, 1 - c),
                                                    device_id_type=MESH))
        return cps

    return _Hosted(arrs, [SDS((a.shape[0], a.shape[1] // 2, a.shape[2]), F32) for a in arrs], [n, n],
                   _start_all(make), _wait_all(make))


def _rs_chips_hosted(parts):
    n = len(parts)

    def make(ins, outs, sems):
        x, y, c = _place()
        j = 2 * x + y
        cps = []
        for k, (px, py) in enumerate(_other_chips(x, y)):
            jk = 2 * px + py
            for w in range(n):
                cps.append(pltpu.make_async_remote_copy(
                    src_ref=ins[w].at[jk], dst_ref=outs[w].at[j], send_sem=sems[0].at[k * n + w],
                    recv_sem=sems[1].at[k * n + w], device_id=(px, py, c), device_id_type=MESH))
        return cps

    return _Hosted(parts, [SDS(p.shape, p.dtype) for p in parts], [3 * n, 3 * n], _start_all(make), _wait_all(make))


def _rs_swap_hosted(halves):
    n = len(halves)

    def make(ins, outs, sems):
        x, y, c = _place()
        return [pltpu.make_async_remote_copy(src_ref=ins[w], dst_ref=outs[w], send_sem=sems[0].at[w],
                                             recv_sem=sems[1].at[w], device_id=(x, y, 1 - c), device_id_type=MESH)
                for w in range(n)]

    return _Hosted(halves, [SDS(h.shape, F32) for h in halves], [n, n], _start_all(make), _wait_all(make))


def _run_comm(hosted, name):
    return _call(lambda: None, hosted, None, name=name, grid=(), in_specs=[], out_specs=[], out_shape=[],
                 scratch_shapes=[], args=(), sem=None)[1]


def _row_tile(rows, cols, n_arrays):
    budget = 24 * 1024 * 1024 // (2 * 4 * n_arrays * cols)
    best = SUBLANES
    for t in range(SUBLANES, rows + 1, SUBLANES):
        if rows % t == 0 and t <= budget:
            best = t
    return best


def _place_index(which):
    x, y, c = _place()
    v = c if which == "c" else 2 * x + y
    return jnp.reshape(v, (1,)).astype(jnp.int32)


def _add_own_half(full, recv, name):
    nsh, rows, cols = full.shape
    hr = rows // 2
    t = _row_tile(hr, cols, 4)
    nt = hr // t

    def body(c_ref, a_ref, b_ref, o_ref, ob_ref):
        v = a_ref[...] + b_ref[...]
        o_ref[...] = v
        ob_ref[...] = v.astype(BF16)

    half = pl.BlockSpec((1, t, cols), lambda s_, i, c_ref: (s_, i, 0))
    return pl.pallas_call(
        body, name=name,
        grid_spec=pltpu.PrefetchScalarGridSpec(
            num_scalar_prefetch=1, grid=(nsh, nt),
            in_specs=[pl.BlockSpec((1, t, cols), lambda s_, i, c_ref: (s_, c_ref[0] * nt + i, 0)), half],
            out_specs=[half, half]),
        out_shape=[SDS((nsh, hr, cols), F32), SDS((nsh, hr, cols), BF16)],
        compiler_params=_cp(("parallel", "parallel")))(_place_index("c"), full, recv)


def _sum_chips(own, recv, name):
    nsh, hr, cols = own.shape
    t = _row_tile(hr, cols, 6)

    def body(j_ref, own_ref, *rest):
        r_refs, o_ref = rest[:nsh], rest[nsh]
        j = j_ref[0]
        mine = own_ref[0]
        parts = [jnp.where(j == k, mine, r_refs[k][0].astype(F32)) for k in range(nsh)]
        o_ref[...] = ((parts[0] + parts[1]) + parts[2]) + parts[3]

    def other(k):
        return pl.BlockSpec((1, t, cols), lambda i, j_ref: (jnp.where(j_ref[0] == k, (k + 1) % nsh, k), i, 0))

    return pl.pallas_call(
        body, name=name,
        grid_spec=pltpu.PrefetchScalarGridSpec(
            num_scalar_prefetch=1, grid=(hr // t,),
            in_specs=[pl.BlockSpec((1, t, cols), lambda i, j_ref: (j_ref[0], i, 0))]
            + [other(k) for k in range(nsh)],
            out_specs=pl.BlockSpec((t, cols), lambda i, j_ref: (i, 0))),
        out_shape=SDS((hr, cols), F32), compiler_params=_cp(("parallel",)))(_place_index("j"), own, *([recv] * nsh))


def _adamw_math(w, g, m, v):
    m = ADAM_B1 * m + (1.0 - ADAM_B1) * g
    v = ADAM_B2 * v + (1.0 - ADAM_B2) * (g * g)
    m_hat = m / (1.0 - ADAM_B1 ** ADAM_STEP)
    v_hat = v / (1.0 - ADAM_B2 ** ADAM_STEP)
    delta = -ADAM_LR * (m_hat / (jnp.sqrt(v_hat) + ADAM_EPS) + ADAM_WD * w)
    return delta, m, v


def _adamw_big(w, g_own, g_sib, m, v, name):
    _, rows, cols = w.shape
    hr = rows // 2
    t = _row_tile(hr, cols, 9)
    nth = hr // t

    def body(c_ref, w_ref, go_ref, gs_ref, m_ref, v_ref, g_ref, d_ref, mo_ref, vo_ref):
        own = (pl.program_id(0) // nth) == c_ref[0]
        g = jnp.where(own, go_ref[...], gs_ref[...])
        g_ref[0] = g
        d_ref[0], mo_ref[0], vo_ref[0] = _adamw_math(w_ref[0], g, m_ref[0], v_ref[0])

    spec = pl.BlockSpec((1, t, cols), lambda i, c_ref: (0, i, 0))
    hspec = pl.BlockSpec((t, cols), lambda i, c_ref: (i % nth, 0))
    return pl.pallas_call(
        body, name=name,
        grid_spec=pltpu.PrefetchScalarGridSpec(
            num_scalar_prefetch=1, grid=(2 * nth,), in_specs=[spec, hspec, hspec, spec, spec],
            out_specs=[spec] * 4),
        out_shape=[SDS((1, rows, cols), F32)] * 4,
        compiler_params=_cp(("parallel",)))(_place_index("c"), w, g_own, g_sib, m, v)


def _allreduce_small(mix_slab, dg_mix, dg_ffn, dg_fin, loss8):
    half = SLAB_ROWS // 2

    def body(ms_ref, gm_ref, gf_ref, gn_ref, loss_ref, out_ref, loc_s, sib_s, chip_s, r2_s, fin_s, sems):
        x, y, c = _place()
        j = 2 * x + y
        rows = []
        for ref in (gm_ref, gf_ref, gn_ref):
            v = jnp.sum(ref[...], axis=0, keepdims=True)
            rows += [v[:, :SLAB_W], v[:, SLAB_W:]]
        rows.append(jnp.concatenate([loss_ref[0:1, :]] * (SLAB_W // LANES), axis=1))
        rows.append(jnp.zeros((SLAB_ROWS - ROW_LOSS - 1, SLAB_W), F32))
        loc_s[0:MIX_SLAB_ROWS, :] = ms_ref[...]
        loc_s[MIX_SLAB_ROWS:SLAB_ROWS, :] = jnp.concatenate(rows, axis=0)
        sib = (x, y, 1 - c)
        cp = pltpu.make_async_remote_copy(src_ref=loc_s, dst_ref=sib_s, send_sem=sems.at[0], recv_sem=sems.at[1],
                                          device_id=sib, device_id_type=MESH)
        cp.start()
        cp.wait()
        chip_s[...] = loc_s[...] + sib_s[...]
        mine = chip_s.at[pl.ds(pl.multiple_of(c * half, SUBLANES), half), :]
        r2_s[j] = chip_s[pl.ds(pl.multiple_of(c * half, SUBLANES), half), :]
        cps = []
        for k, (px, py) in enumerate(_other_chips(x, y)):
            cps.append(pltpu.make_async_remote_copy(src_ref=mine, dst_ref=r2_s.at[j], send_sem=sems.at[2 + k],
                                                    recv_sem=sems.at[5 + k], device_id=(px, py, c),
                                                    device_id_type=MESH))
        for cp in cps:
            cp.start()
        for cp in cps:
            cp.wait()
        fin_s[...] = ((r2_s[0] + r2_s[1]) + r2_s[2]) + r2_s[3]
        dst = out_ref.at[pl.ds(pl.multiple_of(c * half, SUBLANES), half), :]
        out_ref[pl.ds(pl.multiple_of(c * half, SUBLANES), half), :] = fin_s[...]
        cp = pltpu.make_async_remote_copy(src_ref=fin_s, dst_ref=dst, send_sem=sems.at[8], recv_sem=sems.at[9],
                                          device_id=sib, device_id_type=MESH)
        cp.start()
        cp.wait()

    return pl.pallas_call(
        body, name="allreduce_small", in_specs=[VMEM_SPEC] * 5, out_specs=VMEM_SPEC,
        out_shape=SDS((SLAB_ROWS, SLAB_W), F32),
        scratch_shapes=[pltpu.VMEM((SLAB_ROWS, SLAB_W), F32)] * 3 + [pltpu.VMEM((N_CHIPS, half, SLAB_W), F32),
                                                                       pltpu.VMEM((half, SLAB_W), F32),
                                                                       pltpu.SemaphoreType.DMA((10,))],
        compiler_params=_cp())(mix_slab, dg_mix, dg_ffn, dg_fin, loss8)


_SMALL_ROWS = (("conv_b", ROW_CONV_B), ("gate_a_b", ROW_BA), ("gate_x_b", ROW_BX), ("lru_lambda", ROW_LAM),
               ("pool_b", ROW_PB), ("pool_scale", ROW_PS), ("norm_lru_g", ROW_GL), ("norm_pool_g", ROW_GP))
_WIDE_ROWS = (("norm_mix_g", ROW_MIX), ("norm_ffn_g", ROW_FFN), ("final_norm_g", ROW_FIN))
_BLOCK_ROWS = (("gate_a_w", ROW_GA), ("gate_x_w", ROW_GX), ("pool_w", ROW_PW))
_SMALL_ORDER = tuple(n for n, _ in _SMALL_ROWS) + tuple(n for n, _ in _WIDE_ROWS) + tuple(
    n for n, _ in _BLOCK_ROWS) + ("conv_w",)


def _adamw_small(slab, wmv):
    names = _SMALL_ORDER
    flat = [a for nme in names for a in wmv[nme]]
    nin = len(flat)

    def body(*refs):
        slab_ref, j_ref = refs[0], refs[1]
        ins = refs[2:2 + nin]
        outs = refs[2 + nin:]
        grads = {}
        for nme, row in _SMALL_ROWS:
            grads[nme] = slab_ref[row:row + 1, :]
        for nme, row in _WIDE_ROWS:
            grads[nme] = jnp.concatenate([slab_ref[row:row + 1, :], slab_ref[row + 1:row + 2, :]], axis=1)
        for nme, row in _BLOCK_ROWS:
            grads[nme] = slab_ref[row:row + wmv[nme][0].shape[0], :]
        full = slab_ref[ROW_CONV_W:ROW_CONV_W + CONV_WIDTH, :]
        jv = j_ref[0]
        g = jnp.zeros((CONV_WIDTH, LANES), F32)
        for jj in range(N_CHIPS):
            g = jnp.where(jv == jj, full[:, jj * LANES:(jj + 1) * LANES], g)
        grads["conv_w"] = g
        for idx, nme in enumerate(names):
            w_ref, m_ref, v_ref = ins[3 * idx:3 * idx + 3]
            g = grads[nme]
            delta, m, v = _adamw_math(w_ref[...], g, m_ref[...], v_ref[...])
            outs[4 * idx][...] = g
            outs[4 * idx + 1][...] = delta
            outs[4 * idx + 2][...] = m
            outs[4 * idx + 3][...] = v

    x, y, _ = _place()
    jidx = jnp.reshape(2 * x + y, (1,)).astype(jnp.int32)
    out_shape = [SDS(wmv[nme][0].shape, F32) for nme in names for _ in range(4)]
    res = pl.pallas_call(
        body, name="adamw_small",
        in_specs=[VMEM_SPEC, pl.BlockSpec(memory_space=pltpu.SMEM)] + [VMEM_SPEC] * nin,
        out_specs=[VMEM_SPEC] * len(out_shape), out_shape=out_shape, compiler_params=_cp())(slab, jidx, *flat)
    return {nme: tuple(res[4 * idx:4 * idx + 4]) for idx, nme in enumerate(names)}


_FFN = ("ffn_w1", "ffn_w3", "ffn_w2")


def _local_step(x, target, full, sp_, distributed):
    u = _inproj(x, sp_["norm_mix_g"], full["w_in"])
    gather = [_ffn_gather_hosted([full[n] for n in _FFN])] if distributed else None
    (h, yn, hres1), got = _mixer_fwd(u, x, sp_, full["w_out"], gather)
    w1, w3, w2 = got[0] if distributed else [full[n] for n in _FFN]
    h2, a1, a3, ff = _ffn_up(hres1, sp_["norm_ffn_g"], w1, w3)
    dh, dhb, loss8, dg_fin = _ffn_down(ff, hres1, target, sp_["final_norm_g"], w2)
    da1, da3 = _ffn_bwd_gate(dhb, a1, a3, w2)
    dws = list(_ffn_wgrad2(h2, dhb, ff, da1, da3))
    rs1 = [_rs_sibling_hosted(dws)] if distributed else None
    (dhres1, dg_ffn), got = _ffn_bwd_down(da1, da3, dh, hres1, sp_["norm_ffn_g"], w1, w3, rs1)
    rs2 = None
    if distributed:
        pairs = [_add_own_half(a, r, "add_half_" + n) for n, a, r in zip(_FFN, dws, got[0])]
        rs2 = [_rs_chips_hosted([pb for _, pb in pairs])]
    (du, mix_slab), got = _mixer_bwd(u, h, dhres1, sp_, full["w_out"], rs2)
    (gx, dwin, dwout, dg_mix), _ = _inproj_bwd(x, du, dhres1, yn, sp_["norm_mix_g"], full["w_in"])
    d = x.shape[1]
    big = {"w_in": dwin, "w_out": dwout.reshape(N_CHIPS, d // N_CHIPS, d)}
    for k, n in enumerate(_FFN):
        big[n] = (pairs[k][0], got[0][k]) if distributed else dws[k]
    return gx, big, (mix_slab, dg_mix, dg_ffn, dg_fin, loss8)


def _to_compact(w):
    h, i, j = w.shape
    return jnp.transpose(w, (1, 0, 2)).reshape(i, h * j)


def _from_compact(w, h):
    i, hj = w.shape
    return jnp.transpose(w.reshape(i, h, hj // h), (1, 0, 2))


_SMALL_LAYOUT = {
    "gate_a_w": (lambda a: _to_compact(a[0]), lambda a: _from_compact(a, 8)[None]),
    "gate_x_w": (lambda a: _to_compact(a[0]), lambda a: _from_compact(a, 8)[None]),
    "pool_w": (lambda a: _to_compact(a[0]), lambda a: _from_compact(a, 4)[None]),
    "conv_w": (lambda a: a[0], lambda a: a[None]),
    "final_norm_g": (lambda a: a[None], lambda a: a[0]),
}

_WEIGHTS = ("norm_mix_g", "w_in", "conv_w", "conv_b", "gate_a_w", "gate_a_b", "gate_x_w", "gate_x_b", "lru_lambda",
            "pool_w", "pool_b", "pool_scale", "norm_lru_g", "norm_pool_g", "w_out", "norm_ffn_g", "ffn_w1",
            "ffn_w3", "ffn_w2", "final_norm_g")


def kernel(x, norm_mix_g, w_in, conv_w, conv_b, gate_a_w, gate_a_b, gate_x_w, gate_x_b, lru_lambda, pool_w, pool_b, pool_scale, norm_lru_g, norm_pool_g, w_out, norm_ffn_g, ffn_w1, ffn_w3, ffn_w2, final_norm_g, loss_target, m_norm_mix_g, m_w_in, m_conv_w, m_conv_b, m_gate_a_w, m_gate_a_b, m_gate_x_w, m_gate_x_b, m_lru_lambda, m_pool_w, m_pool_b, m_pool_scale, m_norm_lru_g, m_norm_pool_g, m_w_out, m_norm_ffn_g, m_ffn_w1, m_ffn_w3, m_ffn_w2, m_final_norm_g, v_norm_mix_g, v_w_in, v_conv_w, v_conv_b, v_gate_a_w, v_gate_a_b, v_gate_x_w, v_gate_x_b, v_lru_lambda, v_pool_w, v_pool_b, v_pool_scale, v_norm_lru_g, v_norm_pool_g, v_w_out, v_norm_ffn_g, v_ffn_w1, v_ffn_w3, v_ffn_w2, v_final_norm_g):
    loc = locals()
    w = {n: loc[n] for n in _WEIGHTS}
    m = {n: loc["m_" + n] for n in _WEIGHTS}
    v = {n: loc["v_" + n] for n in _WEIGHTS}

    def lay(nme, a):
        return _SMALL_LAYOUT[nme][0](a) if nme in _SMALL_LAYOUT else a

    def unlay(nme, a):
        return _SMALL_LAYOUT[nme][1](a) if nme in _SMALL_LAYOUT else a

    gathered = _gather_weights([w[n][0] for n in _BIG], w["conv_w"][0], n_remote=2)
    full = dict(zip(_BIG, gathered[:-1]))
    full["w_out"] = full["w_out"].reshape(w_out.shape[2], w_out.shape[2])
    cw_all = gathered[-1]
    sp_ = {n: lay(n, w[n]) for n in _SMALL_ORDER}
    sp_["conv_w"] = jnp.transpose(cw_all[:, :CONV_WIDTH, :], (1, 0, 2)).reshape(CONV_WIDTH, N_CHIPS * LANES)

    gx, big, small = _local_step(x[0], loss_target[0], full, sp_, distributed=True)

    late = ("w_in", "w_out")
    fin = {n: _sum_chips(big[n][0], big[n][1], "sum_chips_" + n) for n in _FFN}
    recv1, swapped = _run_comm([_rs_sibling_hosted([big[n] for n in late]),
                                _rs_swap_hosted([fin[n] for n in _FFN])], "tail_sibling")
    sib = dict(zip(_FFN, swapped))
    pairs = [_add_own_half(big[n], r, "add_half_" + n) for n, r in zip(late, recv1)]
    recv2, = _run_comm([_rs_chips_hosted([pb for _, pb in pairs])], "tail_chips")
    for n, (p, _), r in zip(late, pairs, recv2):
        fin[n] = _sum_chips(p, r, "sum_chips_" + n)
    swapped, = _run_comm([_rs_swap_hosted([fin[n] for n in late])], "tail_swap")
    sib.update(zip(late, swapped))
    out = {}
    for n in _BIG:
        out[n] = tuple(_adamw_big(w[n], fin[n], sib[n], m[n], v[n], "adamw_" + n))
    slab = _allreduce_small(*small)
    loss = slab[ROW_LOSS, 0]
    wmv = {n: (lay(n, w[n]), lay(n, m[n]), lay(n, v[n])) for n in _SMALL_ORDER}
    res = _adamw_small(slab, wmv)
    for n in _SMALL_ORDER:
        out[n] = tuple(unlay(n, a) for a in res[n])
    return (loss, gx[None]) + tuple(out[n][k] for k in range(4) for n in _WEIGHTS)
```

```python
import functools
import math

import jax
import jax.numpy as jnp
from jax import lax
from jax.experimental import pallas as pl
from jax.experimental.pallas import tpu as pltpu

F32 = jnp.float32
BF16 = jnp.bfloat16
SDS = jax.ShapeDtypeStruct
MESH = pl.DeviceIdType.MESH

EPS = 1e-6
LRU_C = 8.0
CONV_WIDTH = 4
POOL_WINDOWS = (2, 4, 8, 16)
HALO = 16
LANES = 128
SUBLANES = 8
GATE_BLOCK = 256
N_CHIPS = 4

ADAM_LR = 0.001
ADAM_B1 = 0.9
ADAM_B2 = 0.999
ADAM_EPS = 1e-08
ADAM_WD = 0.01
ADAM_STEP = 10

TM_PROJ = 512
TM_MIX = 256
TM_FFN = 512
TM_WGRAD = 1024
TM_FFN_UP = 1024
TM_FFN_DOWN = 512
FFN_ROW_CHUNKS = 2
VMEM_LIMIT = 56 * 1024 * 1024

SLAB_W = 512
ROW_CONV_B, ROW_CONV_W, ROW_BA, ROW_BX, ROW_LAM, ROW_PB, ROW_PS, ROW_GL, ROW_GP = 0, 1, 5, 6, 7, 8, 9, 10, 11
ROW_GA, ROW_GX, ROW_PW = 16, 80, 144
ROW_MIX, ROW_FFN, ROW_FIN, ROW_LOSS = 272, 274, 276, 278
MIX_SLAB_ROWS = 272
SLAB_ROWS = 288


def _cp(sem=None, **kw):
    if sem is not None:
        kw["dimension_semantics"] = sem
    return pltpu.CompilerParams(vmem_limit_bytes=VMEM_LIMIT, **kw)


def _const_spec(shape):
    nd = len(shape)
    return pl.BlockSpec(shape, lambda *_: (0,) * nd, pipeline_mode=pl.Buffered(1))


def _sigmoid(x):
    return 1.0 / (1.0 + jnp.exp(-x))


def _dot(a, b):
    return jnp.dot(a, b, preferred_element_type=F32)


def _dot_nt(a, b):
    return lax.dot_general(a, b, (((1,), (1,)), ((), ())), preferred_element_type=F32)


def _dot_tn(a, b):
    return lax.dot_general(a, b, (((0,), (0,)), ((), ())), preferred_element_type=F32)


def _colsum8(v):
    m, c = v.shape
    return v.reshape(m // SUBLANES, SUBLANES, c).sum(axis=0)


def _rowmean(v):
    return jnp.mean(v, axis=-1, keepdims=True)


def _rms_bwd(dy, xhat, r, g):
    dxh = dy * g
    return r * (dxh - xhat * _rowmean(dxh * xhat))


def _softplus_neg(lam):
    z = -lam
    e = jnp.exp(-jnp.abs(z))
    u = 1.0 + e
    d = u - 1.0
    log1p = jnp.where(d == 0.0, e, jnp.log(u) * (e / jnp.where(d == 0.0, 1.0, d)))
    return jnp.maximum(z, 0.0) + log1p


def _neg_expm1(z):
    series = -(z * (1.0 + z * (0.5 + z * (1.0 / 6.0 + z * (1.0 / 24.0)))))
    return jnp.where(z > -0.03, series, 1.0 - jnp.exp(z))


_GELU_C = math.sqrt(2.0 / math.pi)
_GELU_K = 0.044715


def _gelu_parts(x):
    x2 = x * x
    th = jnp.tanh(_GELU_C * (x + _GELU_K * x2 * x))
    ge = 0.5 * x * (1.0 + th)
    dge = 0.5 * (1.0 + th) + 0.5 * x * (1.0 - th * th) * (_GELU_C * (1.0 + 3.0 * _GELU_K * x2))
    return ge, dge


def _shift_down(halo, tile, k):
    if k == 0:
        return tile
    ext = jnp.concatenate([halo, tile], axis=0)
    n = tile.shape[0]
    h = halo.shape[0]
    return ext[h - k:h - k + n]


def _shift_up(tile, nxt, k):
    if k == 0:
        return tile
    ext = jnp.concatenate([tile, nxt], axis=0)
    return ext[k:k + tile.shape[0]]


def _build_gate_blocks(ga_ref, gx_ref, gw_ref):
    hd = ga_ref.shape[0]
    per = GATE_BLOCK // hd
    lane = lax.broadcasted_iota(jnp.int32, (hd, GATE_BLOCK), 1)
    for b in range(gw_ref.shape[0]):
        for src, off in ((ga_ref, 0), (gx_ref, GATE_BLOCK)):
            blk = src[:, b * GATE_BLOCK:(b + 1) * GATE_BLOCK]
            for hh in range(per):
                m = (lane >= hh * hd) & (lane < (hh + 1) * hd)
                gw_ref[b, hh * hd:(hh + 1) * hd, off:off + GATE_BLOCK] = jnp.where(m, blk, 0.0).astype(BF16)


def _scan_level1(a, b, reverse):
    m, c = a.shape
    a3 = a.reshape(m // SUBLANES, SUBLANES, c)
    b3 = b.reshape(m // SUBLANES, SUBLANES, c)
    row = lax.broadcasted_iota(jnp.int32, a3.shape, 1)
    for s in (1, 2, 4):
        sh = (SUBLANES - s) if reverse else s
        a_sh = pltpu.roll(a3, sh, 1)
        b_sh = pltpu.roll(b3, sh, 1)
        ok = (row < SUBLANES - s) if reverse else (row >= s)
        b3 = jnp.where(ok, a3 * b_sh + b3, b3)
        a3 = jnp.where(ok, a3 * a_sh, a3)
    return a3.reshape(m, c), b3.reshape(m, c)


def _scan_level2(a_ref, b_ref, out_ref, carry, reverse):
    m, c = a_ref.shape
    ng = m // SUBLANES

    def step(g, cr):
        gi = (ng - 1 - g) if reverse else g
        off = pl.multiple_of(gi * SUBLANES, SUBLANES)
        h = b_ref[pl.ds(off, SUBLANES), :] + a_ref[pl.ds(off, SUBLANES), :] * cr
        out_ref[pl.ds(off, SUBLANES), :] = h
        edge = h[0:1, :] if reverse else h[SUBLANES - 1:SUBLANES, :]
        return jnp.broadcast_to(edge, (SUBLANES, c))

    return lax.fori_loop(0, ng, step, carry, unroll=4)


def _mixer_recompute(u, hal, t0, cw, cb, gw_ref, ba, bx, lam, pw_ref, pb, ps):
    tm = u.shape[0]
    lw = cb.shape[1]
    u_l, u_g, u_p = u[:, :lw], u[:, lw:2 * lw], u[:, 2 * lw:]
    hal_l, hal_p = hal[:, :lw], hal[:, 2 * lw:]
    taps = [_shift_down(hal_l, u_l, CONV_WIDTH - 1 - k) for k in range(CONV_WIDTH)]
    xc = cb
    for k in range(CONV_WIDTH):
        xc = xc + taps[k] * cw[k:k + 1, :]
    xcb = xc.astype(BF16)
    nb = lw // GATE_BLOCK
    gs = [_dot(xcb[:, b * GATE_BLOCK:(b + 1) * GATE_BLOCK], gw_ref[b]) for b in range(nb)]
    r = _sigmoid(jnp.concatenate([g[:, :GATE_BLOCK] for g in gs], axis=1) + ba)
    ig = _sigmoid(jnp.concatenate([g[:, GATE_BLOCK:] for g in gs], axis=1) + bx)
    sp = _softplus_neg(lam)
    la = (-LRU_C * r) * sp
    a = jnp.exp(la)
    m2raw = _neg_expm1(2.0 * la)
    mult = jnp.sqrt(jnp.maximum(m2raw, 1e-12))
    ge, dge = _gelu_parts(u_g)
    row = lax.broadcasted_iota(jnp.int32, (tm, LANES), 0) + t0
    pooled, invs, zs = [], [], []
    for gi, w in enumerate(POOL_WINDOWS):
        e = jnp.concatenate([hal_p[:, gi * LANES:(gi + 1) * LANES], u_p[:, gi * LANES:(gi + 1) * LANES]], axis=0)
        s = e
        k = 1
        while k < w:
            s = s + pltpu.roll(s, k, 0)
            k *= 2
        inv = 1.0 / jnp.minimum(row + 1, w).astype(F32)
        pg = s[HALO:] * inv - e[HALO:]
        pooled.append(pg)
        invs.append(inv)
        zs.append(_dot(pg.astype(BF16), pw_ref[:, gi * LANES:(gi + 1) * LANES].astype(BF16)))
    z = jnp.concatenate(zs, axis=1) + pb
    y_pool = z * ps
    return dict(u_l=u_l, u_g=u_g, taps=taps, xc=xc, xcb=xcb, r=r, ig=ig, sp=sp, la=la, a=a, m2raw=m2raw,
                mult=mult, ge=ge, dge=dge, pooled=pooled, invs=invs, z=z, y_pool=y_pool)


ANY = pl.BlockSpec(memory_space=pl.ANY)
VMEM_SPEC = pl.BlockSpec(memory_space=pltpu.VMEM)


class _Hosted:
    def __init__(self, ins, out_shapes, sems, start, finish, mid=None, aliases=None):
        self.ins, self.out_shapes, self.sems = list(ins), list(out_shapes), list(sems)
        self.start, self.mid, self.finish = start, mid, finish
        self.aliases = dict(aliases or {})


def _call(body, hosted, stage_preds, *, name, grid, in_specs, out_specs, out_shape, scratch_shapes, args, sem):
    hosted = list(hosted or [])
    n_in, n_out, n_scr = len(in_specs), len(out_specs), len(scratch_shapes)
    c_in = [a for h in hosted for a in h.ins]
    c_out = [o for h in hosted for o in h.out_shapes]
    c_sem = [pltpu.SemaphoreType.DMA((k,)) for h in hosted for k in h.sems]

    def full(*refs):
        p = 0
        parts = []
        for cnt in (n_in, len(c_in), n_out, len(c_out), n_scr, len(c_sem)):
            parts.append(refs[p:p + cnt])
            p += cnt
        hi, ci, ho, co, hs, cs = parts
        per = []
        a = b = c_ = 0
        for h in hosted:
            per.append((h, ci[a:a + len(h.ins)], co[b:b + len(h.out_shapes)], cs[c_:c_ + len(h.sems)]))
            a, b, c_ = a + len(h.ins), b + len(h.out_shapes), c_ + len(h.sems)
        first = mid = last = None
        if hosted and grid:
            first, mid, last = stage_preds()

        def run(fn, pred, i_, o_, s_):
            if fn is None:
                return
            if pred is None:
                fn(i_, o_, s_)
            else:
                pl.when(pred)(functools.partial(fn, i_, o_, s_))

        for h, i_, o_, s_ in per:
            run(h.start, first, i_, o_, s_)
        body(*hi, *ho, *hs)
        for h, i_, o_, s_ in per:
            run(h.mid, mid, i_, o_, s_)
        for h, i_, o_, s_ in per:
            run(h.finish, last, i_, o_, s_)

    aliases = {}
    a = b = 0
    for h in hosted:
        for k, v in h.aliases.items():
            aliases[n_in + a + k] = n_out + b + v
        a, b = a + len(h.ins), b + len(h.out_shapes)
    res = pl.pallas_call(
        full, name=name, grid=grid, in_specs=list(in_specs) + [ANY] * len(c_in),
        out_specs=list(out_specs) + [ANY] * len(c_out), out_shape=list(out_shape) + c_out,
        scratch_shapes=list(scratch_shapes) + c_sem, input_output_aliases=aliases,
        compiler_params=_cp(sem))(*args, *c_in)
    res = list(res)
    outs = []
    p = n_out
    for h in hosted:
        outs.append(res[p:p + len(h.out_shapes)])
        p += len(h.out_shapes)
    return res[:n_out], outs


def _inproj(x, g_mix, w_in):
    s, d = x.shape
    n = w_in.shape[1]
    tm = min(TM_PROJ, s)

    def body(x_ref, g_ref, w_ref, u_ref):
        xv = x_ref[...]
        r = lax.rsqrt(_rowmean(xv * xv) + EPS)
        u_ref[...] = _dot((xv * r * g_ref[...]).astype(BF16), w_ref[...])

    return pl.pallas_call(
        body, grid=(s // tm,), name="inproj",
        in_specs=[pl.BlockSpec((tm, d), lambda i: (i, 0)), _const_spec((1, d)), _const_spec((d, n))],
        out_specs=pl.BlockSpec((tm, n), lambda i: (i, 0)),
        out_shape=SDS((s, n), F32), compiler_params=_cp(("parallel",)))(x, g_mix, w_in)


def _mixer_fwd(u, x, sp_, w_out, hosted=None):
    s, din = u.shape
    d = x.shape[1]
    lw = din // 3
    tm = min(TM_MIX, s)
    nb = lw // GATE_BLOCK

    def body(u_ref, halo_ref, x_ref, cw_ref, cb_ref, ga_ref, gx_ref, ba_ref, bx_ref, lam_ref, pw_ref, pb_ref,
             ps_ref, gl_ref, gp_ref, wout_ref, h_ref, yn_ref, hres_ref, gw_s, a_s, b_s, carry_s):
        i = pl.program_id(0)

        @pl.when(i == 0)
        def _():
            _build_gate_blocks(ga_ref, gx_ref, gw_s)
            carry_s[...] = jnp.zeros_like(carry_s)

        uv = u_ref[...]
        hal = jnp.where(i > 0, halo_ref[...], 0.0)
        f = _mixer_recompute(uv, hal, i * tm, cw_ref[...], cb_ref[...], gw_s, ba_ref[...], bx_ref[...],
                             lam_ref[...], pw_ref, pb_ref[...], ps_ref[...])
        bb = f["mult"] * (f["ig"] * f["xc"])
        a1, b1 = _scan_level1(f["a"], bb, reverse=False)
        a_s[...] = a1
        b_s[...] = b1
        carry_s[...] = _scan_level2(a_s, b_s, h_ref, carry_s[...], reverse=False)
        y_lru = h_ref[...] * f["ge"]
        rl = lax.rsqrt(_rowmean(y_lru * y_lru) + EPS)
        yp = f["y_pool"]
        rp = lax.rsqrt(_rowmean(yp * yp) + EPS)
        yn = jnp.concatenate([y_lru * rl * gl_ref[...], yp * rp * gp_ref[...]], axis=1).astype(BF16)
        yn_ref[...] = yn
        hres_ref[...] = x_ref[...] + _dot(yn, wout_ref[...])

    small = [sp_[k] for k in ("conv_w", "conv_b", "gate_a_w", "gate_x_w", "gate_a_b", "gate_x_b", "lru_lambda",
                              "pool_w", "pool_b", "pool_scale", "norm_lru_g", "norm_pool_g")]
    nt = s // tm

    def stages():
        i = pl.program_id(0)
        return i == 0, i == max(nt - 3, 0), i == nt - 1

    return _call(
        body, hosted, stages, grid=(nt,), name="mixer_fwd",
        in_specs=[pl.BlockSpec((tm, din), lambda i: (i, 0)),
                  pl.BlockSpec((HALO, din), lambda i: (jnp.maximum(i * (tm // HALO) - 1, 0), 0)),
                  pl.BlockSpec((tm, d), lambda i: (i, 0))]
        + [_const_spec(a.shape) for a in small] + [_const_spec(w_out.shape)],
        out_specs=[pl.BlockSpec((tm, lw), lambda i: (i, 0)), pl.BlockSpec((tm, d), lambda i: (i, 0)),
                   pl.BlockSpec((tm, d), lambda i: (i, 0))],
        out_shape=[SDS((s, lw), F32), SDS((s, d), BF16), SDS((s, d), F32)],
        scratch_shapes=[pltpu.VMEM((nb, GATE_BLOCK, 2 * GATE_BLOCK), BF16), pltpu.VMEM((tm, lw), F32),
                        pltpu.VMEM((tm, lw), F32), pltpu.VMEM((SUBLANES, lw), F32)],
        args=(u, u, x, *small, w_out), sem=("arbitrary",))


def _ffn_fwd(hres1, target, g_ffn, g_fin, w1, w3, w2):
    s, d = hres1.shape
    nj, _, fc = w1.shape
    tm = min(TM_FFN, s)

    def body(h_ref, t_ref, gf_ref, gn_ref, w1_ref, w3_ref, w2_ref,
             a1_ref, a3_ref, h2_ref, dh_ref, dhb_ref, loss_ref, dgn_ref, acc_s):
        i, j = pl.program_id(0), pl.program_id(1)

        @pl.when((i == 0) & (j == 0))
        def _():
            loss_ref[...] = jnp.zeros_like(loss_ref)
            dgn_ref[...] = jnp.zeros_like(dgn_ref)

        @pl.when(j == 0)
        def _():
            hv = h_ref[...]
            r = lax.rsqrt(_rowmean(hv * hv) + EPS)
            h2_ref[...] = (hv * r * gf_ref[...]).astype(BF16)

        h2 = h2_ref[...]
        a1 = _dot(h2, w1_ref[0])
        a3 = _dot(h2, w3_ref[0])
        a1_ref[0] = a1.astype(BF16)
        a3_ref[0] = a3.astype(BF16)
        part = _dot(((a1 * _sigmoid(a1)) * a3).astype(BF16), w2_ref[0])

        @pl.when(j == 0)
        def _():
            acc_s[...] = part

        @pl.when(j > 0)
        def _():
            acc_s[...] += part

        @pl.when(j == nj - 1)
        def _():
            hr2 = h_ref[...] + acc_s[...]
            r2 = lax.rsqrt(_rowmean(hr2 * hr2) + EPS)
            xh = hr2 * r2
            gn = gn_ref[...]
            diff = xh * gn - t_ref[...]
            tot = jnp.sum(jnp.sum(diff * diff, axis=1, keepdims=True), axis=0, keepdims=True)
            loss_ref[...] += tot * (0.5 / d)
            dout = diff * (1.0 / d)
            dgn_ref[...] += _colsum8(dout * xh)
            dh = _rms_bwd(dout, xh, r2, gn)
            dh_ref[...] = dh
            dhb_ref[...] = dh.astype(BF16)

    return pl.pallas_call(
        body, grid=(s // tm, nj), name="ffn_fwd",
        in_specs=[pl.BlockSpec((tm, d), lambda i, j: (i, 0)), pl.BlockSpec((tm, d), lambda i, j: (i, 0)),
                  _const_spec((1, d)), _const_spec((1, d)),
                  pl.BlockSpec((1, d, fc), lambda i, j: (j, 0, 0)), pl.BlockSpec((1, d, fc), lambda i, j: (j, 0, 0)),
                  pl.BlockSpec((1, fc, d), lambda i, j: (j, 0, 0))],
        out_specs=[pl.BlockSpec((1, tm, fc), lambda i, j: (j, i, 0)), pl.BlockSpec((1, tm, fc), lambda i, j: (j, i, 0)),
                   pl.BlockSpec((tm, d), lambda i, j: (i, 0)), pl.BlockSpec((tm, d), lambda i, j: (i, 0)),
                   pl.BlockSpec((tm, d), lambda i, j: (i, 0)),
                   pl.BlockSpec((SUBLANES, LANES), lambda i, j: (0, 0)),
                   pl.BlockSpec((SUBLANES, d), lambda i, j: (0, 0))],
        out_shape=[SDS((nj, s, fc), BF16), SDS((nj, s, fc), BF16), SDS((s, d), BF16), SDS((s, d), F32),
                   SDS((s, d), BF16), SDS((SUBLANES, LANES), F32), SDS((SUBLANES, d), F32)],
        scratch_shapes=[pltpu.VMEM((tm, d), F32)],
        compiler_params=_cp(("arbitrary", "arbitrary")))(hres1, target, g_ffn, g_fin, w1, w3, w2)


def _ffn_bwd_act(dh, dhb, a1, a3, hres1, g_ffn, w1, w3, w2):
    s, d = hres1.shape
    nj, _, fc = a1.shape
    tm = min(TM_FFN, s)

    def body(dh_ref, dhb_ref, a1_ref, a3_ref, h_ref, gf_ref, w1_ref, w3_ref, w2_ref,
             da1_ref, da3_ref, dhr_ref, dgf_ref, acc_s):
        i, j = pl.program_id(0), pl.program_id(1)

        @pl.when((i == 0) & (j == 0))
        def _():
            dgf_ref[...] = jnp.zeros_like(dgf_ref)

        @pl.when(j == 0)
        def _():
            acc_s[...] = jnp.zeros_like(acc_s)

        rc = tm // FFN_ROW_CHUNKS
        for q in range(FFN_ROW_CHUNKS):
            rows = slice(q * rc, (q + 1) * rc)
            dff = _dot_nt(dhb_ref[rows, :], w2_ref[0])
            a1v = a1_ref[0, rows, :].astype(F32)
            a3v = a3_ref[0, rows, :].astype(F32)
            sg = _sigmoid(a1v)
            silu = a1v * sg
            da1 = (dff * a3v * (sg * (1.0 + a1v * (1.0 - sg)))).astype(BF16)
            da3 = (dff * silu).astype(BF16)
            da1_ref[0, rows, :] = da1
            da3_ref[0, rows, :] = da3
            acc_s[rows, :] += _dot_nt(da1, w1_ref[0]) + _dot_nt(da3, w3_ref[0])

        @pl.when(j == nj - 1)
        def _():
            hv = h_ref[...]
            r = lax.rsqrt(_rowmean(hv * hv) + EPS)
            xh = hv * r
            dh2 = acc_s[...]
            dgf_ref[...] += _colsum8(dh2 * xh)
            dhr_ref[...] = dh_ref[...] + _rms_bwd(dh2, xh, r, gf_ref[...])

    return pl.pallas_call(
        body, grid=(s // tm, nj), name="ffn_bwd_act",
        in_specs=[pl.BlockSpec((tm, d), lambda i, j: (i, 0)), pl.BlockSpec((tm, d), lambda i, j: (i, 0)),
                  pl.BlockSpec((1, tm, fc), lambda i, j: (j, i, 0)), pl.BlockSpec((1, tm, fc), lambda i, j: (j, i, 0)),
                  pl.BlockSpec((tm, d), lambda i, j: (i, 0)), _const_spec((1, d)),
                  pl.BlockSpec((1, d, fc), lambda i, j: (j, 0, 0)), pl.BlockSpec((1, d, fc), lambda i, j: (j, 0, 0)),
                  pl.BlockSpec((1, fc, d), lambda i, j: (j, 0, 0))],
        out_specs=[pl.BlockSpec((1, tm, fc), lambda i, j: (j, i, 0)), pl.BlockSpec((1, tm, fc), lambda i, j: (j, i, 0)),
                   pl.BlockSpec((tm, d), lambda i, j: (i, 0)), pl.BlockSpec((SUBLANES, d), lambda i, j: (0, 0))],
        out_shape=[SDS((nj, s, fc), BF16), SDS((nj, s, fc), BF16), SDS((s, d), F32), SDS((SUBLANES, d), F32)],
        scratch_shapes=[pltpu.VMEM((tm, d), F32)],
        compiler_params=_cp(("arbitrary", "arbitrary")))(dh, dhb, a1, a3, hres1, g_ffn, w1, w3, w2)


def _ffn_wgrad(h2, dhb, a1, a3, da1, da3):
    s, d = h2.shape
    _, _, fc = a1.shape
    tm = min(TM_WGRAD, s)

    def body(h2_ref, dhb_ref, a1_ref, a3_ref, da1_ref, da3_ref, dw1_ref, dw3_ref, dw2_ref):
        i = pl.program_id(1)

        @pl.when(i == 0)
        def _():
            dw1_ref[...] = jnp.zeros_like(dw1_ref)
            dw3_ref[...] = jnp.zeros_like(dw3_ref)
            dw2_ref[...] = jnp.zeros_like(dw2_ref)

        h2v = h2_ref[...]
        a1v = a1_ref[0].astype(F32)
        ff = ((a1v * _sigmoid(a1v)) * a3_ref[0].astype(F32)).astype(BF16)
        dw1_ref[0] += _dot_tn(h2v, da1_ref[0])
        dw3_ref[0] += _dot_tn(h2v, da3_ref[0])
        dw2_ref[0] += _dot_tn(ff, dhb_ref[...])

    return pl.pallas_call(
        body, grid=(N_CHIPS, s // tm), name="ffn_wgrad",
        in_specs=[pl.BlockSpec((tm, d), lambda j, i: (i, 0)), pl.BlockSpec((tm, d), lambda j, i: (i, 0))]
        + [pl.BlockSpec((1, tm, fc), lambda j, i: (j, i, 0))] * 4,
        out_specs=[pl.BlockSpec((1, d, fc), lambda j, i: (j, 0, 0)), pl.BlockSpec((1, d, fc), lambda j, i: (j, 0, 0)),
                   pl.BlockSpec((1, fc, d), lambda j, i: (j, 0, 0))],
        out_shape=[SDS((N_CHIPS, d, fc), F32), SDS((N_CHIPS, d, fc), F32), SDS((N_CHIPS, fc, d), F32)],
        compiler_params=_cp(("parallel", "arbitrary")))(h2, dhb, a1, a3, da1, da3)


def _row_chunks(tm):
    rc = tm // FFN_ROW_CHUNKS
    return [slice(q * rc, (q + 1) * rc) for q in range(FFN_ROW_CHUNKS)]


def _ffn_up(hres1, g_ffn, w1, w3):
    s, d = hres1.shape
    nj, _, fc = w1.shape
    tm = min(TM_FFN_UP, s)

    def body(h_ref, gf_ref, w1_ref, w3_ref, h2_ref, a1_ref, a3_ref, ff_ref):
        @pl.when(pl.program_id(1) == 0)
        def _():
            hv = h_ref[...]
            r = lax.rsqrt(_rowmean(hv * hv) + EPS)
            h2_ref[...] = (hv * r * gf_ref[...]).astype(BF16)

        j = pl.program_id(1)
        for rows in _row_chunks(tm):
            h2 = h2_ref[rows, :]
            a1 = _dot(h2, w1_ref[j])
            a3 = _dot(h2, w3_ref[j])
            a1_ref[0, rows, :] = a1.astype(BF16)
            a3_ref[0, rows, :] = a3.astype(BF16)
            ff_ref[0, rows, :] = ((a1 * _sigmoid(a1)) * a3).astype(BF16)

    wspec = _const_spec(w1.shape)
    aspec = pl.BlockSpec((1, tm, fc), lambda i, j: (j, i, 0))
    return pl.pallas_call(
        body, grid=(s // tm, nj), name="ffn_up",
        in_specs=[pl.BlockSpec((tm, d), lambda i, j: (i, 0)), _const_spec((1, d)), wspec, wspec],
        out_specs=[pl.BlockSpec((tm, d), lambda i, j: (i, 0)), aspec, aspec, aspec],
        out_shape=[SDS((s, d), BF16)] + [SDS((nj, s, fc), BF16)] * 3,
        compiler_params=_cp(("parallel", "arbitrary")))(hres1, g_ffn, w1, w3)


def _ffn_down(ff, hres1, target, g_fin, w2):
    s, d = hres1.shape
    nj, _, fc = ff.shape
    tm = min(TM_FFN_DOWN, s)

    def body(ff_ref, h_ref, t_ref, gn_ref, w2_ref, dh_ref, dhb_ref, loss_ref, dgn_ref, acc_s):
        i, j = pl.program_id(0), pl.program_id(1)

        @pl.when((i == 0) & (j == 0))
        def _():
            loss_ref[...] = jnp.zeros_like(loss_ref)
            dgn_ref[...] = jnp.zeros_like(dgn_ref)

        @pl.when(j == 0)
        def _():
            acc_s[...] = jnp.zeros_like(acc_s)

        acc_s[...] += _dot(ff_ref[0], w2_ref[j])

        @pl.when(j == nj - 1)
        def _():
            hr2 = h_ref[...] + acc_s[...]
            r2 = lax.rsqrt(_rowmean(hr2 * hr2) + EPS)
            xh = hr2 * r2
            gn = gn_ref[...]
            diff = xh * gn - t_ref[...]
            tot = jnp.sum(jnp.sum(diff * diff, axis=1, keepdims=True), axis=0, keepdims=True)
            loss_ref[...] += tot * (0.5 / d)
            dout = diff * (1.0 / d)
            dgn_ref[...] += _colsum8(dout * xh)
            dh = _rms_bwd(dout, xh, r2, gn)
            dh_ref[...] = dh
            dhb_ref[...] = dh.astype(BF16)

    tile = pl.BlockSpec((tm, d), lambda i, j: (i, 0))
    return pl.pallas_call(
        body, grid=(s // tm, nj), name="ffn_down",
        in_specs=[pl.BlockSpec((1, tm, fc), lambda i, j: (j, i, 0)), tile, tile, _const_spec((1, d)),
                  _const_spec(w2.shape)],
        out_specs=[tile, tile, pl.BlockSpec((SUBLANES, LANES), lambda i, j: (0, 0)),
                   pl.BlockSpec((SUBLANES, d), lambda i, j: (0, 0))],
        out_shape=[SDS((s, d), F32), SDS((s, d), BF16), SDS((SUBLANES, LANES), F32), SDS((SUBLANES, d), F32)],
        scratch_shapes=[pltpu.VMEM((tm, d), F32)],
        compiler_params=_cp(("arbitrary", "arbitrary")))(ff, hres1, target, g_fin, w2)


def _ffn_bwd_gate(dhb, a1, a3, w2):
    s, d = dhb.shape
    nj, _, fc = a1.shape
    tm = min(TM_FFN_UP, s)

    def body(dhb_ref, a1_ref, a3_ref, w2_ref, da1_ref, da3_ref):
        j = pl.program_id(1)
        for rows in _row_chunks(tm):
            dff = _dot_nt(dhb_ref[rows, :], w2_ref[j])
            a1v = a1_ref[0, rows, :].astype(F32)
            sg = _sigmoid(a1v)
            silu = a1v * sg
            da1_ref[0, rows, :] = (dff * a3_ref[0, rows, :].astype(F32) * (sg * (1.0 + (a1v - silu)))).astype(BF16)
            da3_ref[0, rows, :] = (dff * silu).astype(BF16)

    aspec = pl.BlockSpec((1, tm, fc), lambda i, j: (j, i, 0))
    return pl.pallas_call(
        body, grid=(s // tm, nj), name="ffn_bwd_gate",
        in_specs=[pl.BlockSpec((tm, d), lambda i, j: (i, 0)), aspec, aspec, _const_spec(w2.shape)],
        out_specs=[aspec, aspec], out_shape=[SDS((nj, s, fc), BF16)] * 2,
        compiler_params=_cp(("parallel", "arbitrary")))(dhb, a1, a3, w2)


def _ffn_bwd_down(da1, da3, dh, hres1, g_ffn, w1, w3, hosted=None):
    s, d = hres1.shape
    nj, _, fc = da1.shape
    tm = min(TM_FFN_DOWN, s)
    nt = s // tm

    def body(da1_ref, da3_ref, dh_ref, h_ref, gf_ref, w1_ref, w3_ref, dhr_ref, dgf_ref, acc_s):
        i, j = pl.program_id(0), pl.program_id(1)

        @pl.when((i == 0) & (j == 0))
        def _():
            dgf_ref[...] = jnp.zeros_like(dgf_ref)

        @pl.when(j == 0)
        def _():
            acc_s[...] = jnp.zeros_like(acc_s)

        acc_s[...] += _dot_nt(da1_ref[0], w1_ref[j]) + _dot_nt(da3_ref[0], w3_ref[j])

        @pl.when(j == nj - 1)
        def _():
            hv = h_ref[...]
            r = lax.rsqrt(_rowmean(hv * hv) + EPS)
            xh = hv * r
            dh2 = acc_s[...]
            dgf_ref[...] += _colsum8(dh2 * xh)
            dhr_ref[...] = dh_ref[...] + _rms_bwd(dh2, xh, r, gf_ref[...])

    tile = pl.BlockSpec((tm, d), lambda i, j: (i, 0))
    aspec = pl.BlockSpec((1, tm, fc), lambda i, j: (j, i, 0))
    wspec = _const_spec(w1.shape)

    def stages():
        i, j = pl.program_id(0), pl.program_id(1)
        return (i == 0) & (j == 0), (i == max(nt - 2, 0)) & (j == 0), (i == nt - 1) & (j == nj - 1)

    return _call(
        body, hosted, stages, grid=(nt, nj), name="ffn_bwd_down",
        in_specs=[aspec, aspec, tile, tile, _const_spec((1, d)), wspec, wspec],
        out_specs=[tile, pl.BlockSpec((SUBLANES, d), lambda i, j: (0, 0))],
        out_shape=[SDS((s, d), F32), SDS((SUBLANES, d), F32)],
        scratch_shapes=[pltpu.VMEM((tm, d), F32)],
        args=(da1, da3, dh, hres1, g_ffn, w1, w3), sem=("arbitrary", "arbitrary"))


def _ffn_wgrad2(h2, dhb, ff, da1, da3):
    s, d = h2.shape
    _, _, fc = ff.shape
    tm = min(TM_WGRAD, s)

    def body(h2_ref, dhb_ref, ff_ref, da1_ref, da3_ref, dw1_ref, dw3_ref, dw2_ref):
        @pl.when(pl.program_id(1) == 0)
        def _():
            dw1_ref[...] = jnp.zeros_like(dw1_ref)
            dw3_ref[...] = jnp.zeros_like(dw3_ref)
            dw2_ref[...] = jnp.zeros_like(dw2_ref)

        h2v = h2_ref[...]
        dw1_ref[0] += _dot_tn(h2v, da1_ref[0])
        dw3_ref[0] += _dot_tn(h2v, da3_ref[0])
        dw2_ref[0] += _dot_tn(ff_ref[0], dhb_ref[...])

    return pl.pallas_call(
        body, grid=(N_CHIPS, s // tm), name="ffn_wgrad",
        in_specs=[pl.BlockSpec((tm, d), lambda j, i: (i, 0)), pl.BlockSpec((tm, d), lambda j, i: (i, 0))]
        + [pl.BlockSpec((1, tm, fc), lambda j, i: (j, i, 0))] * 3,
        out_specs=[pl.BlockSpec((1, d, fc), lambda j, i: (j, 0, 0)), pl.BlockSpec((1, d, fc), lambda j, i: (j, 0, 0)),
                   pl.BlockSpec((1, fc, d), lambda j, i: (j, 0, 0))],
        out_shape=[SDS((N_CHIPS, d, fc), F32), SDS((N_CHIPS, d, fc), F32), SDS((N_CHIPS, fc, d), F32)],
        compiler_params=_cp(("parallel", "arbitrary")))(h2, dhb, ff, da1, da3)


def _mixer_bwd(u, h, dhres1, sp_, w_out, hosted=None):
    s, din = u.shape
    d = dhres1.shape[1]
    lw = din // 3
    tm = min(TM_MIX, s)
    nt = s // tm
    nb = lw // GATE_BLOCK
    hd = sp_["gate_a_w"].shape[0]

    def body(u_ref, halo_ref, h_ref, hhalo_ref, dhr_ref, cw_ref, cb_ref, ga_ref, gx_ref, ba_ref, bx_ref, lam_ref,
             pw_ref, pb_ref, ps_ref, gl_ref, gp_ref, wout_ref, du_ref, slab_ref,
             gw_s, a_s, b_s, e_s, ecarry_s, dxc_s, q_s, vec_s, cwacc_s, dgw_s, dpw_s):
        i = pl.program_id(0)
        tile = nt - 1 - i

        @pl.when(i == 0)
        def _():
            _build_gate_blocks(ga_ref, gx_ref, gw_s)
            for ref in (ecarry_s, dxc_s, q_s, vec_s, cwacc_s, dgw_s, dpw_s):
                ref[...] = jnp.zeros_like(ref)

        uv = u_ref[...]
        hal = jnp.where(tile > 0, halo_ref[...], 0.0)
        cw = cw_ref[...]
        lam = lam_ref[...]
        ps = ps_ref[...]
        f = _mixer_recompute(uv, hal, tile * tm, cw, cb_ref[...], gw_s, ba_ref[...], bx_ref[...], lam, pw_ref,
                             pb_ref[...], ps)
        hv = h_ref[...]
        h_prev = _shift_down(jnp.where(tile > 0, hhalo_ref[...], 0.0), hv, 1)
        y_lru = hv * f["ge"]
        rl = lax.rsqrt(_rowmean(y_lru * y_lru) + EPS)
        yp = f["y_pool"]
        rp = lax.rsqrt(_rowmean(yp * yp) + EPS)
        xh_l = y_lru * rl
        xh_p = yp * rp

        dyn = _dot_nt(dhr_ref[...].astype(BF16), wout_ref[...])
        d_nl, d_np = dyn[:, :lw], dyn[:, lw:]
        vec = {}
        vec[ROW_GL] = _colsum8(d_nl * xh_l)
        vec[ROW_GP] = _colsum8(d_np * xh_p)
        d_ylru = _rms_bwd(d_nl, xh_l, rl, gl_ref[...])
        d_ypool = _rms_bwd(d_np, xh_p, rp, gp_ref[...])

        vec[ROW_PS] = _colsum8(d_ypool * f["z"])
        dz = d_ypool * ps
        vec[ROW_PB] = _colsum8(dz)
        dzb = dz.astype(BF16)
        dup = []
        for gi, w in enumerate(POOL_WINDOWS):
            sl = slice(gi * LANES, (gi + 1) * LANES)
            dpw_s[:, sl] += _dot_tn(f["pooled"][gi].astype(BF16), dzb[:, sl])
            dpool = _dot_nt(dzb[:, sl], pw_ref[:, sl].astype(BF16))
            q = dpool * f["invs"][gi]
            e = jnp.concatenate([q, q_s[:, sl]], axis=0)
            k = 1
            while k < w:
                e = e + pltpu.roll(e, tm + HALO - k, 0)
                k *= 2
            dup.append(e[:tm] - dpool)
            q_s[:, sl] = q[:HALO]

        d_hout = d_ylru * f["ge"]
        d_ug = d_ylru * hv * f["dge"]
        a = f["a"]
        a1, b1 = _scan_level1(a, a * d_hout, reverse=True)
        a_s[...] = a1
        b_s[...] = b1
        e_next = ecarry_s[...]
        ecarry_s[...] = _scan_level2(a_s, b_s, e_s, e_next, reverse=True)
        sv = d_hout + _shift_up(e_s[...], e_next, 1)
        d_a = sv * h_prev
        mult, ig, xc, r = f["mult"], f["ig"], f["xc"], f["r"]
        d_mult = sv * (ig * xc)
        d_ig = sv * mult * xc
        d_xc = sv * mult * ig
        d_la = d_a * a + jnp.where(f["m2raw"] > 1e-12, d_mult * (-(a * a) / mult), 0.0)
        d_r = d_la * (-LRU_C * f["sp"])
        vec[ROW_LAM] = _colsum8(d_la * (-LRU_C * r))
        d_pr = d_r * r * (1.0 - r)
        d_pi = d_ig * ig * (1.0 - ig)
        vec[ROW_BA] = _colsum8(d_pr)
        vec[ROW_BX] = _colsum8(d_pi)
        dxc_parts = []
        for b in range(nb):
            sl = slice(b * GATE_BLOCK, (b + 1) * GATE_BLOCK)
            rhs = jnp.concatenate([d_pr[:, sl], d_pi[:, sl]], axis=1).astype(BF16)
            dgw_s[b] += _dot_tn(f["xcb"][:, sl], rhs)
            dxc_parts.append(_dot_nt(rhs, gw_s[b]))
        d_xc = d_xc + jnp.concatenate(dxc_parts, axis=1)
        vec[ROW_CONV_B] = _colsum8(d_xc)
        dxc_next = dxc_s[...]
        d_ul = None
        for k in range(CONV_WIDTH):
            cwacc_s[k * SUBLANES:(k + 1) * SUBLANES, :] += _colsum8(d_xc * f["taps"][k])
            term = _shift_up(d_xc, dxc_next, CONV_WIDTH - 1 - k) * cw[k:k + 1, :]
            d_ul = term if d_ul is None else d_ul + term
        dxc_s[...] = d_xc[:SUBLANES]
        for row, val in vec.items():
            vec_s[row * SUBLANES:(row + 1) * SUBLANES, :] += val
        du_ref[...] = jnp.concatenate([d_ul, d_ug] + dup, axis=1).astype(BF16)

        @pl.when(i == nt - 1)
        def _():
            rows = []
            for row in range(ROW_GA):
                if row in (ROW_CONV_W, ROW_CONV_W + 1, ROW_CONV_W + 2, ROW_CONV_W + 3):
                    k = row - ROW_CONV_W
                    v = jnp.sum(cwacc_s[k * SUBLANES:(k + 1) * SUBLANES, :], axis=0, keepdims=True)
                elif row <= ROW_GP:
                    v = jnp.sum(vec_s[row * SUBLANES:(row + 1) * SUBLANES, :], axis=0, keepdims=True)
                    if row == ROW_LAM:
                        v = v * (-1.0 / (1.0 + jnp.exp(lam)))
                else:
                    v = jnp.zeros((1, lw), F32)
                rows.append(v)
            slab_ref[0:ROW_GA, :] = jnp.concatenate(rows, axis=0)
            lane = lax.broadcasted_iota(jnp.int32, (hd, GATE_BLOCK), 1)
            for b in range(nb):
                for off, row0 in ((0, ROW_GA), (GATE_BLOCK, ROW_GX)):
                    acc = jnp.zeros((hd, GATE_BLOCK), F32)
                    for hh in range(GATE_BLOCK // hd):
                        m = (lane >= hh * hd) & (lane < (hh + 1) * hd)
                        acc = acc + jnp.where(m, dgw_s[b, hh * hd:(hh + 1) * hd, off:off + GATE_BLOCK], 0.0)
                    slab_ref[row0:row0 + hd, b * GATE_BLOCK:(b + 1) * GATE_BLOCK] = acc
            slab_ref[ROW_PW:ROW_PW + LANES, :] = dpw_s[...]

    small = [sp_[k] for k in ("conv_w", "conv_b", "gate_a_w", "gate_x_w", "gate_a_b", "gate_x_b", "lru_lambda",
                              "pool_w", "pool_b", "pool_scale", "norm_lru_g", "norm_pool_g")]
    rev = lambda i: nt - 1 - i

    def stages():
        i = pl.program_id(0)
        return i == 0, i == max(nt - 3, 0), i == nt - 1

    return _call(
        body, hosted, stages, grid=(nt,), name="mixer_bwd",
        in_specs=[pl.BlockSpec((tm, din), lambda i: (rev(i), 0)),
                  pl.BlockSpec((HALO, din), lambda i: (jnp.maximum(rev(i) * (tm // HALO) - 1, 0), 0)),
                  pl.BlockSpec((tm, lw), lambda i: (rev(i), 0)),
                  pl.BlockSpec((SUBLANES, lw), lambda i: (jnp.maximum(rev(i) * (tm // SUBLANES) - 1, 0), 0)),
                  pl.BlockSpec((tm, d), lambda i: (rev(i), 0))]
        + [_const_spec(a.shape) for a in small] + [_const_spec(w_out.shape)],
        out_specs=[pl.BlockSpec((tm, din), lambda i: (rev(i), 0)),
                   pl.BlockSpec((MIX_SLAB_ROWS, SLAB_W), lambda i: (0, 0))],
        out_shape=[SDS((s, din), BF16), SDS((MIX_SLAB_ROWS, SLAB_W), F32)],
        scratch_shapes=[pltpu.VMEM((nb, GATE_BLOCK, 2 * GATE_BLOCK), BF16),
                        pltpu.VMEM((tm, lw), F32), pltpu.VMEM((tm, lw), F32), pltpu.VMEM((tm, lw), F32),
                        pltpu.VMEM((SUBLANES, lw), F32), pltpu.VMEM((SUBLANES, lw), F32),
                        pltpu.VMEM((HALO, lw), F32), pltpu.VMEM((ROW_GA * SUBLANES, lw), F32),
                        pltpu.VMEM((CONV_WIDTH * SUBLANES, lw), F32),
                        pltpu.VMEM((nb, GATE_BLOCK, 2 * GATE_BLOCK), F32), pltpu.VMEM((LANES, lw), F32)],
        args=(u, u, h, h, dhres1, *small, w_out), sem=("arbitrary",))


def _inproj_bwd(x, du, dhres1, yn, g_mix, w_in, hosted=None):
    s, d = x.shape
    n = w_in.shape[1]
    nc = n // N_CHIPS
    tm = min(TM_PROJ, s)
    nt = s // tm

    def body(x_ref, du_ref, dhr_ref, yn_ref, g_ref, w_ref, gx_ref, dwin_ref, dwout_ref, dg_ref):
        i = pl.program_id(0)

        @pl.when(i == 0)
        def _():
            dwin_ref[...] = jnp.zeros_like(dwin_ref)
            dwout_ref[...] = jnp.zeros_like(dwout_ref)
            dg_ref[...] = jnp.zeros_like(dg_ref)

        xv = x_ref[...]
        g = g_ref[...]
        r = lax.rsqrt(_rowmean(xv * xv) + EPS)
        xh = xv * r
        h1 = (xh * g).astype(BF16)
        duv = du_ref[...]
        dh1 = _dot_nt(duv, w_ref[...])
        dg_ref[...] += _colsum8(dh1 * xh)
        dhr = dhr_ref[...]
        gx_ref[...] = dhr + _rms_bwd(dh1, xh, r, g)
        for jj in range(N_CHIPS):
            dwin_ref[jj] += _dot_tn(h1, duv[:, jj * nc:(jj + 1) * nc])
        dwout_ref[...] += _dot_tn(yn_ref[...], dhr.astype(BF16))

    def stages():
        i = pl.program_id(0)
        return i == 0, i == max(nt - 3, 0), i == nt - 1

    return _call(
        body, hosted, stages, grid=(nt,), name="inproj_bwd",
        in_specs=[pl.BlockSpec((tm, d), lambda i: (i, 0)), pl.BlockSpec((tm, n), lambda i: (i, 0)),
                  pl.BlockSpec((tm, d), lambda i: (i, 0)), pl.BlockSpec((tm, d), lambda i: (i, 0)),
                  _const_spec((1, d)), _const_spec((d, n))],
        out_specs=[pl.BlockSpec((tm, d), lambda i: (i, 0)), pl.BlockSpec((N_CHIPS, d, nc), lambda i: (0, 0, 0)),
                   pl.BlockSpec((d, d), lambda i: (0, 0)), pl.BlockSpec((SUBLANES, d), lambda i: (0, 0))],
        out_shape=[SDS((s, d), F32), SDS((N_CHIPS, d, nc), F32), SDS((d, d), F32), SDS((SUBLANES, d), F32)],
        scratch_shapes=[], args=(x, du, dhres1, yn, g_mix, w_in), sem=("arbitrary",))


def _place():
    x, y, c = lax.axis_index("x"), lax.axis_index("y"), lax.axis_index("c")
    return x, y, c


def _other_chips(x, y):
    return [(1 - x, y), (x, 1 - y), (1 - x, 1 - y)]


ANY = pl.BlockSpec(memory_space=pl.ANY)
VMEM_SPEC = pl.BlockSpec(memory_space=pltpu.VMEM)

_GATHERED = {"w_in": "cols", "w_out": "major", "ffn_w1": "major", "ffn_w3": "major", "ffn_w2": "major"}
_BIG = ("w_in", "w_out", "ffn_w1", "ffn_w3", "ffn_w2")


def _gather_weights(shards, conv_w, n_remote):
    n = len(shards)
    full_shapes = []
    for name, sh in zip(_BIG, shards):
        r, cdim = sh.shape
        if _GATHERED[name] == "cols":
            assert cdim % LANES == 0
            full_shapes.append((r, cdim * N_CHIPS))
        else:
            full_shapes.append((N_CHIPS, r, cdim))

    def region(ref, name, sh, jj, cc):
        r, cdim = sh
        rows = pl.ds(0, r) if cc is None else pl.ds(pl.multiple_of(cc * (r // 2), 16), r // 2)
        if _GATHERED[name] == "cols":
            return ref.at[rows, pl.ds(pl.multiple_of(jj * cdim, LANES), cdim)]
        return ref.at[jj, rows, :]

    def staged(ref, sh, cc):
        r = sh[0]
        return ref.at[pl.ds(pl.multiple_of(cc * (r // 2), 16), r // 2), :]

    def body(*refs):
        ins, cw_in = refs[:n], refs[n]
        outs, cw_out = refs[n + 1:2 * n + 1], refs[2 * n + 1]
        stage = refs[2 * n + 2:3 * n + 2]
        cw_stage, lsem, ssem, rsem, fssem, frsem, cssem, crsem = refs[3 * n + 2:]
        x, y, c = _place()
        j = 2 * x + y
        chips = _other_chips(x, y)
        for w in range(n):
            stage[w][...] = ins[w][...].astype(BF16)
        cw_stage[...] = jnp.zeros_like(cw_stage)
        cw_stage[0:CONV_WIDTH, :] = cw_in[...]
        shs = [s_.shape for s_ in shards]
        local = [pltpu.make_async_copy(stage[w], region(outs[w], _BIG[w], shs[w], j, None), lsem.at[w])
                 for w in range(n)]
        local.append(pltpu.make_async_copy(cw_stage, cw_out.at[j], lsem.at[n]))
        for cp in local:
            cp.start()
        sends = []
        for k, (px, py) in enumerate(chips):
            for w in range(n_remote):
                sends.append(pltpu.make_async_remote_copy(
                    src_ref=staged(stage[w], shs[w], c), dst_ref=region(outs[w], _BIG[w], shs[w], j, c),
                    send_sem=ssem.at[k * n + w], recv_sem=rsem.at[k * n + w], device_id=(px, py, c),
                    device_id_type=MESH))
            sends.append(pltpu.make_async_remote_copy(
                src_ref=cw_stage, dst_ref=cw_out.at[j], send_sem=cssem.at[k], recv_sem=crsem.at[k],
                device_id=(px, py, c), device_id_type=MESH))
        for cp in sends:
            cp.start()
        fwd = []
        for k, (px, py) in enumerate(chips):
            jk = 2 * px + py
            for w in range(n_remote):
                reg = region(outs[w], _BIG[w], shs[w], jk, c)
                pltpu.make_async_remote_copy(src_ref=reg, dst_ref=reg, send_sem=ssem.at[k * n + w],
                                             recv_sem=rsem.at[k * n + w], device_id=(px, py, c),
                                             device_id_type=MESH).wait_recv()
                cp = pltpu.make_async_remote_copy(src_ref=reg, dst_ref=reg, send_sem=fssem.at[k * n + w],
                                                  recv_sem=frsem.at[k * n + w], device_id=(x, y, 1 - c),
                                                  device_id_type=MESH)
                cp.start()
                fwd.append(cp)
            pltpu.make_async_remote_copy(src_ref=cw_stage, dst_ref=cw_out.at[jk], send_sem=cssem.at[k],
                                         recv_sem=crsem.at[k], device_id=(px, py, c),
                                         device_id_type=MESH).wait_recv()
        for k, (px, py) in enumerate(chips):
            jk = 2 * px + py
            for w in range(n_remote):
                reg = region(outs[w], _BIG[w], shs[w], jk, 1 - c)
                pltpu.make_async_remote_copy(src_ref=reg, dst_ref=reg, send_sem=fssem.at[k * n + w],
                                             recv_sem=frsem.at[k * n + w], device_id=(x, y, 1 - c),
                                             device_id_type=MESH).wait_recv()
        for cp in sends + fwd:
            cp.wait_send()
        for cp in local:
            cp.wait()

    nsem = 3 * n
    return pl.pallas_call(
        body, name="gather_first",
        in_specs=[VMEM_SPEC] * (n + 1), out_specs=[ANY] * (n + 1),
        out_shape=[SDS(fs, BF16) for fs in full_shapes] + [SDS((N_CHIPS, SUBLANES, LANES), F32)],
        scratch_shapes=[pltpu.VMEM(s_.shape, BF16) for s_ in shards] + [pltpu.VMEM((SUBLANES, LANES), F32)]
        + [pltpu.SemaphoreType.DMA((n + 1,))] + [pltpu.SemaphoreType.DMA((nsem,))] * 4
        + [pltpu.SemaphoreType.DMA((3,))] * 2,
        compiler_params=_cp())(*shards, conv_w)


def _start_all(make):
    def f(ins, outs, sems):
        for cp in make(ins, outs, sems):
            cp.start()
    return f


def _wait_all(make):
    def f(ins, outs, sems):
        for cp in make(ins, outs, sems):
            cp.wait()
    return f


def _ffn_gather_hosted(arrs):
    n = len(arrs)

    def make(outs, sems):
        ssem, rsem, fs, fr = sems
        x, y, c = _place()
        j = 2 * x + y

        def reg(w, jj, cc):
            hr = arrs[w].shape[1] // 2
            return outs[w].at[jj, pl.ds(pl.multiple_of(cc * hr, 16), hr), :]

        def rc(w, jj, cc, s_sem, r_sem, dev):
            return pltpu.make_async_remote_copy(src_ref=reg(w, jj, cc), dst_ref=reg(w, jj, cc), send_sem=s_sem,
                                                recv_sem=r_sem, device_id=dev, device_id_type=MESH)

        sends, recvs, fwds, frecvs = [], [], [], []
        for k, (px, py) in enumerate(_other_chips(x, y)):
            jk = 2 * px + py
            for w in range(n):
                q = k * n + w
                sends.append(rc(w, j, c, ssem.at[q], rsem.at[q], (px, py, c)))
                recvs.append(rc(w, jk, c, ssem.at[q], rsem.at[q], (px, py, c)))
                fwds.append(rc(w, jk, c, fs.at[q], fr.at[q], (x, y, 1 - c)))
                frecvs.append(rc(w, jk, 1 - c, fs.at[q], fr.at[q], (x, y, 1 - c)))
        return sends, recvs, fwds, frecvs

    def start(ins, outs, sems):
        for cp in make(outs, sems)[0]:
            cp.start()

    def mid(ins, outs, sems):
        _, recvs, fwds, _ = make(outs, sems)
        for r, f in zip(recvs, fwds):
            r.wait_recv()
            f.start()

    def finish(ins, outs, sems):
        sends, _, fwds, frecvs = make(outs, sems)
        for r in frecvs:
            r.wait_recv()
        for cp in sends + fwds:
            cp.wait_send()

    return _Hosted(arrs, [SDS(a.shape, a.dtype) for a in arrs], [3 * n] * 4, start, finish, mid=mid,
                   aliases={w: w for w in range(n)})


def _rs_sibling_hosted(arrs):
    n = len(arrs)

    def make(ins, outs, sems):
        x, y, c = _place()
        cps = []
        for w in range(n):
            hr = arrs[w].shape[1] // 2
            src = ins[w].at[:, pl.ds(pl.multiple_of((1 - c) * hr, SUBLANES), hr), :]
            cps.append(pltpu.make_async_remote_copy(src_ref=src, dst_ref=outs[w], send_sem=sems[0].at[w],
                                                    recv_sem=sems[1].at[w], device_id=(x, y, 1 - c),
                                                    device_id_type=MESH))
        return cps

    return _Hosted(arrs, [SDS((a.shape[0], a.shape[1] // 2, a.shape[2]), F32) for a in arrs], [n, n],
                   _start_all(make), _wait_all(make))


def _rs_chips_hosted(parts):
    n = len(parts)

    def make(ins, outs, sems):
        x, y, c = _place()
        j = 2 * x + y
        cps = []
        for k, (px, py) in enumerate(_other_chips(x, y)):
            jk = 2 * px + py
            for w in range(n):
                cps.append(pltpu.make_async_remote_copy(
                    src_ref=ins[w].at[jk], dst_ref=outs[w].at[j], send_sem=sems[0].at[k * n + w],
                    recv_sem=sems[1].at[k * n + w], device_id=(px, py, c), device_id_type=MESH))
        return cps

    return _Hosted(parts, [SDS(p.shape, p.dtype) for p in parts], [3 * n, 3 * n], _start_all(make), _wait_all(make))


def _rs_swap_hosted(halves):
    n = len(halves)

    def make(ins, outs, sems):
        x, y, c = _place()
        return [pltpu.make_async_remote_copy(src_ref=ins[w], dst_ref=outs[w], send_sem=sems[0].at[w],
                                             recv_sem=sems[1].at[w], device_id=(x, y, 1 - c), device_id_type=MESH)
                for w in range(n)]

    return _Hosted(halves, [SDS(h.shape, F32) for h in halves], [n, n], _start_all(make), _wait_all(make))


def _run_comm(hosted, name):
    return _call(lambda: None, hosted, None, name=name, grid=(), in_specs=[], out_specs=[], out_shape=[],
                 scratch_shapes=[], args=(), sem=None)[1]


def _row_tile(rows, cols, n_arrays):
    budget = 24 * 1024 * 1024 // (2 * 4 * n_arrays * cols)
    best = SUBLANES
    for t in range(SUBLANES, rows + 1, SUBLANES):
        if rows % t == 0 and t <= budget:
            best = t
    return best


def _place_index(which):
    x, y, c = _place()
    v = c if which == "c" else 2 * x + y
    return jnp.reshape(v, (1,)).astype(jnp.int32)


def _add_own_half(full, recv, name):
    nsh, rows, cols = full.shape
    hr = rows // 2
    t = _row_tile(hr, cols, 4)
    nt = hr // t

    def body(c_ref, a_ref, b_ref, o_ref, ob_ref):
        v = a_ref[...] + b_ref[...]
        o_ref[...] = v
        ob_ref[...] = v.astype(BF16)

    half = pl.BlockSpec((1, t, cols), lambda s_, i, c_ref: (s_, i, 0))
    return pl.pallas_call(
        body, name=name,
        grid_spec=pltpu.PrefetchScalarGridSpec(
            num_scalar_prefetch=1, grid=(nsh, nt),
            in_specs=[pl.BlockSpec((1, t, cols), lambda s_, i, c_ref: (s_, c_ref[0] * nt + i, 0)), half],
            out_specs=[half, half]),
        out_shape=[SDS((nsh, hr, cols), F32), SDS((nsh, hr, cols), BF16)],
        compiler_params=_cp(("parallel", "parallel")))(_place_index("c"), full, recv)


def _sum_chips(own, recv, name):
    nsh, hr, cols = own.shape
    t = _row_tile(hr, cols, 6)

    def body(j_ref, own_ref, *rest):
        r_refs, o_ref = rest[:nsh], rest[nsh]
        j = j_ref[0]
        mine = own_ref[0]
        parts = [jnp.where(j == k, mine, r_refs[k][0].astype(F32)) for k in range(nsh)]
        o_ref[...] = ((parts[0] + parts[1]) + parts[2]) + parts[3]

    def other(k):
        return pl.BlockSpec((1, t, cols), lambda i, j_ref: (jnp.where(j_ref[0] == k, (k + 1) % nsh, k), i, 0))

    return pl.pallas_call(
        body, name=name,
        grid_spec=pltpu.PrefetchScalarGridSpec(
            num_scalar_prefetch=1, grid=(hr // t,),
            in_specs=[pl.BlockSpec((1, t, cols), lambda i, j_ref: (j_ref[0], i, 0))]
            + [other(k) for k in range(nsh)],
            out_specs=pl.BlockSpec((t, cols), lambda i, j_ref: (i, 0))),
        out_shape=SDS((hr, cols), F32), compiler_params=_cp(("parallel",)))(_place_index("j"), own, *([recv] * nsh))


def _adamw_math(w, g, m, v):
    m = ADAM_B1 * m + (1.0 - ADAM_B1) * g
    v = ADAM_B2 * v + (1.0 - ADAM_B2) * (g * g)
    m_hat = m / (1.0 - ADAM_B1 ** ADAM_STEP)
    v_hat = v / (1.0 - ADAM_B2 ** ADAM_STEP)
    delta = -ADAM_LR * (m_hat / (jnp.sqrt(v_hat) + ADAM_EPS) + ADAM_WD * w)
    return delta, m, v


def _adamw_big(w, g_own, g_sib, m, v, name):
    _, rows, cols = w.shape
    hr = rows // 2
    t = _row_tile(hr, cols, 9)
    nth = hr // t

    def body(c_ref, w_ref, go_ref, gs_ref, m_ref, v_ref, g_ref, d_ref, mo_ref, vo_ref):
        own = (pl.program_id(0) // nth) == c_ref[0]
        g = jnp.where(own, go_ref[...], gs_ref[...])
        g_ref[0] = g
        d_ref[0], mo_ref[0], vo_ref[0] = _adamw_math(w_ref[0], g, m_ref[0], v_ref[0])

    spec = pl.BlockSpec((1, t, cols), lambda i, c_ref: (0, i, 0))
    hspec = pl.BlockSpec((t, cols), lambda i, c_ref: (i % nth, 0))
    return pl.pallas_call(
        body, name=name,
        grid_spec=pltpu.PrefetchScalarGridSpec(
            num_scalar_prefetch=1, grid=(2 * nth,), in_specs=[spec, hspec, hspec, spec, spec],
            out_specs=[spec] * 4),
        out_shape=[SDS((1, rows, cols), F32)] * 4,
        compiler_params=_cp(("parallel",)))(_place_index("c"), w, g_own, g_sib, m, v)


def _allreduce_small(mix_slab, dg_mix, dg_ffn, dg_fin, loss8):
    half = SLAB_ROWS // 2

    def body(ms_ref, gm_ref, gf_ref, gn_ref, loss_ref, out_ref, loc_s, sib_s, chip_s, r2_s, fin_s, sems):
        x, y, c = _place()
        j = 2 * x + y
        rows = []
        for ref in (gm_ref, gf_ref, gn_ref):
            v = jnp.sum(ref[...], axis=0, keepdims=True)
            rows += [v[:, :SLAB_W], v[:, SLAB_W:]]
        rows.append(jnp.concatenate([loss_ref[0:1, :]] * (SLAB_W // LANES), axis=1))
        rows.append(jnp.zeros((SLAB_ROWS - ROW_LOSS - 1, SLAB_W), F32))
        loc_s[0:MIX_SLAB_ROWS, :] = ms_ref[...]
        loc_s[MIX_SLAB_ROWS:SLAB_ROWS, :] = jnp.concatenate(rows, axis=0)
        sib = (x, y, 1 - c)
        cp = pltpu.make_async_remote_copy(src_ref=loc_s, dst_ref=sib_s, send_sem=sems.at[0], recv_sem=sems.at[1],
                                          device_id=sib, device_id_type=MESH)
        cp.start()
        cp.wait()
        chip_s[...] = loc_s[...] + sib_s[...]
        mine = chip_s.at[pl.ds(pl.multiple_of(c * half, SUBLANES), half), :]
        r2_s[j] = chip_s[pl.ds(pl.multiple_of(c * half, SUBLANES), half), :]
        cps = []
        for k, (px, py) in enumerate(_other_chips(x, y)):
            cps.append(pltpu.make_async_remote_copy(src_ref=mine, dst_ref=r2_s.at[j], send_sem=sems.at[2 + k],
                                                    recv_sem=sems.at[5 + k], device_id=(px, py, c),
                                                    device_id_type=MESH))
        for cp in cps:
            cp.start()
        for cp in cps:
            cp.wait()
        fin_s[...] = ((r2_s[0] + r2_s[1]) + r2_s[2]) + r2_s[3]
        dst = out_ref.at[pl.ds(pl.multiple_of(c * half, SUBLANES), half), :]
        out_ref[pl.ds(pl.multiple_of(c * half, SUBLANES), half), :] = fin_s[...]
        cp = pltpu.make_async_remote_copy(src_ref=fin_s, dst_ref=dst, send_sem=sems.at[8], recv_sem=sems.at[9],
                                          device_id=sib, device_id_type=MESH)
        cp.start()
        cp.wait()

    return pl.pallas_call(
        body, name="allreduce_small", in_specs=[VMEM_SPEC] * 5, out_specs=VMEM_SPEC,
        out_shape=SDS((SLAB_ROWS, SLAB_W), F32),
        scratch_shapes=[pltpu.VMEM((SLAB_ROWS, SLAB_W), F32)] * 3 + [pltpu.VMEM((N_CHIPS, half, SLAB_W), F32),
                                                                       pltpu.VMEM((half, SLAB_W), F32),
                                                                       pltpu.SemaphoreType.DMA((10,))],
        compiler_params=_cp())(mix_slab, dg_mix, dg_ffn, dg_fin, loss8)


_SMALL_ROWS = (("conv_b", ROW_CONV_B), ("gate_a_b", ROW_BA), ("gate_x_b", ROW_BX), ("lru_lambda", ROW_LAM),
               ("pool_b", ROW_PB), ("pool_scale", ROW_PS), ("norm_lru_g", ROW_GL), ("norm_pool_g", ROW_GP))
_WIDE_ROWS = (("norm_mix_g", ROW_MIX), ("norm_ffn_g", ROW_FFN), ("final_norm_g", ROW_FIN))
_BLOCK_ROWS = (("gate_a_w", ROW_GA), ("gate_x_w", ROW_GX), ("pool_w", ROW_PW))
_SMALL_ORDER = tuple(n for n, _ in _SMALL_ROWS) + tuple(n for n, _ in _WIDE_ROWS) + tuple(
    n for n, _ in _BLOCK_ROWS) + ("conv_w",)


def _adamw_small(slab, wmv):
    names = _SMALL_ORDER
    flat = [a for nme in names for a in wmv[nme]]
    nin = len(flat)

    def body(*refs):
        slab_ref, j_ref = refs[0], refs[1]
        ins = refs[2:2 + nin]
        outs = refs[2 + nin:]
        grads = {}
        for nme, row in _SMALL_ROWS:
            grads[nme] = slab_ref[row:row + 1, :]
        for nme, row in _WIDE_ROWS:
            grads[nme] = jnp.concatenate([slab_ref[row:row + 1, :], slab_ref[row + 1:row + 2, :]], axis=1)
        for nme, row in _BLOCK_ROWS:
            grads[nme] = slab_ref[row:row + wmv[nme][0].shape[0], :]
        full = slab_ref[ROW_CONV_W:ROW_CONV_W + CONV_WIDTH, :]
        jv = j_ref[0]
        g = jnp.zeros((CONV_WIDTH, LANES), F32)
        for jj in range(N_CHIPS):
            g = jnp.where(jv == jj, full[:, jj * LANES:(jj + 1) * LANES], g)
        grads["conv_w"] = g
        for idx, nme in enumerate(names):
            w_ref, m_ref, v_ref = ins[3 * idx:3 * idx + 3]
            g = grads[nme]
            delta, m, v = _adamw_math(w_ref[...], g, m_ref[...], v_ref[...])
            outs[4 * idx][...] = g
            outs[4 * idx + 1][...] = delta
            outs[4 * idx + 2][...] = m
            outs[4 * idx + 3][...] = v

    x, y, _ = _place()
    jidx = jnp.reshape(2 * x + y, (1,)).astype(jnp.int32)
    out_shape = [SDS(wmv[nme][0].shape, F32) for nme in names for _ in range(4)]
    res = pl.pallas_call(
        body, name="adamw_small",
        in_specs=[VMEM_SPEC, pl.BlockSpec(memory_space=pltpu.SMEM)] + [VMEM_SPEC] * nin,
        out_specs=[VMEM_SPEC] * len(out_shape), out_shape=out_shape, compiler_params=_cp())(slab, jidx, *flat)
    return {nme: tuple(res[4 * idx:4 * idx + 4]) for idx, nme in enumerate(names)}


_FFN = ("ffn_w1", "ffn_w3", "ffn_w2")


def _local_step(x, target, full, sp_, distributed):
    u = _inproj(x, sp_["norm_mix_g"], full["w_in"])
    gather = [_ffn_gather_hosted([full[n] for n in _FFN])] if distributed else None
    (h, yn, hres1), got = _mixer_fwd(u, x, sp_, full["w_out"], gather)
    w1, w3, w2 = got[0] if distributed else [full[n] for n in _FFN]
    h2, a1, a3, ff = _ffn_up(hres1, sp_["norm_ffn_g"], w1, w3)
    dh, dhb, loss8, dg_fin = _ffn_down(ff, hres1, target, sp_["final_norm_g"], w2)
    da1, da3 = _ffn_bwd_gate(dhb, a1, a3, w2)
    dws = list(_ffn_wgrad2(h2, dhb, ff, da1, da3))
    rs1 = [_rs_sibling_hosted(dws)] if distributed else None
    (dhres1, dg_ffn), got = _ffn_bwd_down(da1, da3, dh, hres1, sp_["norm_ffn_g"], w1, w3, rs1)
    rs2 = None
    if distributed:
        pairs = [_add_own_half(a, r, "add_half_" + n) for n, a, r in zip(_FFN, dws, got[0])]
        rs2 = [_rs_chips_hosted([pb for _, pb in pairs])]
    (du, mix_slab), got = _mixer_bwd(u, h, dhres1, sp_, full["w_out"], rs2)
    (gx, dwin, dwout, dg_mix), _ = _inproj_bwd(x, du, dhres1, yn, sp_["norm_mix_g"], full["w_in"])
    d = x.shape[1]
    big = {"w_in": dwin, "w_out": dwout.reshape(N_CHIPS, d // N_CHIPS, d)}
    for k, n in enumerate(_FFN):
        big[n] = (pairs[k][0], got[0][k]) if distributed else dws[k]
    return gx, big, (mix_slab, dg_mix, dg_ffn, dg_fin, loss8)


def _to_compact(w):
    h, i, j = w.shape
    return jnp.transpose(w, (1, 0, 2)).reshape(i, h * j)


def _from_compact(w, h):
    i, hj = w.shape
    return jnp.transpose(w.reshape(i, h, hj // h), (1, 0, 2))


_SMALL_LAYOUT = {
    "gate_a_w": (lambda a: _to_compact(a[0]), lambda a: _from_compact(a, 8)[None]),
    "gate_x_w": (lambda a: _to_compact(a[0]), lambda a: _from_compact(a, 8)[None]),
    "pool_w": (lambda a: _to_compact(a[0]), lambda a: _from_compact(a, 4)[None]),
    "conv_w": (lambda a: a[0], lambda a: a[None]),
    "final_norm_g": (lambda a: a[None], lambda a: a[0]),
}

_WEIGHTS = ("norm_mix_g", "w_in", "conv_w", "conv_b", "gate_a_w", "gate_a_b", "gate_x_w", "gate_x_b", "lru_lambda",
            "pool_w", "pool_b", "pool_scale", "norm_lru_g", "norm_pool_g", "w_out", "norm_ffn_g", "ffn_w1",
            "ffn_w3", "ffn_w2", "final_norm_g")


def kernel(x, norm_mix_g, w_in, conv_w, conv_b, gate_a_w, gate_a_b, gate_x_w, gate_x_b, lru_lambda, pool_w, pool_b, pool_scale, norm_lru_g, norm_pool_g, w_out, norm_ffn_g, ffn_w1, ffn_w3, ffn_w2, final_norm_g, loss_target, m_norm_mix_g, m_w_in, m_conv_w, m_conv_b, m_gate_a_w, m_gate_a_b, m_gate_x_w, m_gate_x_b, m_lru_lambda, m_pool_w, m_pool_b, m_pool_scale, m_norm_lru_g, m_norm_pool_g, m_w_out, m_norm_ffn_g, m_ffn_w1, m_ffn_w3, m_ffn_w2, m_final_norm_g, v_norm_mix_g, v_w_in, v_conv_w, v_conv_b, v_gate_a_w, v_gate_a_b, v_gate_x_w, v_gate_x_b, v_lru_lambda, v_pool_w, v_pool_b, v_pool_scale, v_norm_lru_g, v_norm_pool_g, v_w_out, v_norm_ffn_g, v_ffn_w1, v_ffn_w3, v_ffn_w2, v_final_norm_g):
    loc = locals()
    w = {n: loc[n] for n in _WEIGHTS}
    m = {n: loc["m_" + n] for n in _WEIGHTS}
    v = {n: loc["v_" + n] for n in _WEIGHTS}

    def lay(nme, a):
        return _SMALL_LAYOUT[nme][0](a) if nme in _SMALL_LAYOUT else a

    def unlay(nme, a):
        return _SMALL_LAYOUT[nme][1](a) if nme in _SMALL_LAYOUT else a

    gathered = _gather_weights([w[n][0] for n in _BIG], w["conv_w"][0], n_remote=2)
    full = dict(zip(_BIG, gathered[:-1]))
    full["w_out"] = full["w_out"].reshape(w_out.shape[2], w_out.shape[2])
    cw_all = gathered[-1]
    sp_ = {n: lay(n, w[n]) for n in _SMALL_ORDER}
    sp_["conv_w"] = jnp.transpose(cw_all[:, :CONV_WIDTH, :], (1, 0, 2)).reshape(CONV_WIDTH, N_CHIPS * LANES)

    gx, big, small = _local_step(x[0], loss_target[0], full, sp_, distributed=True)

    late = ("w_in", "w_out")
    fin = {n: _sum_chips(big[n][0], big[n][1], "sum_chips_" + n) for n in _FFN}
    recv1, swapped = _run_comm([_rs_sibling_hosted([big[n] for n in late]),
                                _rs_swap_hosted([fin[n] for n in _FFN])], "tail_sibling")
    sib = dict(zip(_FFN, swapped))
    pairs = [_add_own_half(big[n], r, "add_half_" + n) for n, r in zip(late, recv1)]
    recv2, = _run_comm([_rs_chips_hosted([pb for _, pb in pairs])], "tail_chips")
    for n, (p, _), r in zip(late, pairs, recv2):
        fin[n] = _sum_chips(p, r, "sum_chips_" + n)
    swapped, = _run_comm([_rs_swap_hosted([fin[n] for n in late])], "tail_swap")
    sib.update(zip(late, swapped))
    out = {}
    for n in _BIG:
        out[n] = tuple(_adamw_big(w[n], fin[n], sib[n], m[n], v[n], "adamw_" + n))
    slab = _allreduce_small(*small)
    loss = slab[ROW_LOSS, 0]
    wmv = {n: (lay(n, w[n]), lay(n, m[n]), lay(n, v[n])) for n in _SMALL_ORDER}
    res = _adamw_small(slab, wmv)
    for n in _SMALL_ORDER:
        out[n] = tuple(unlay(n, a) for a in res[n])
    return (loss, gx[None]) + tuple(out[n][k] for k in range(4) for n in _WEIGHTS)
```

```python
import functools
import math

import jax
import jax.numpy as jnp
from jax import lax
from jax.experimental import pallas as pl
from jax.experimental.pallas import tpu as pltpu

F32 = jnp.float32
BF16 = jnp.bfloat16
SDS = jax.ShapeDtypeStruct
MESH = pl.DeviceIdType.MESH

EPS = 1e-6
LRU_C = 8.0
CONV_WIDTH = 4
POOL_WINDOWS = (2, 4, 8, 16)
HALO = 16
LANES = 128
SUBLANES = 8
GATE_BLOCK = 256
N_CHIPS = 4

ADAM_LR = 0.001
ADAM_B1 = 0.9
ADAM_B2 = 0.999
ADAM_EPS = 1e-08
ADAM_WD = 0.01
ADAM_STEP = 10

TM_PROJ = 512
TM_MIX = 256
TM_FFN = 512
TM_WGRAD = 1024
TM_FFN_UP = 1024
TM_FFN_DOWN = 512
FFN_ROW_CHUNKS = 2
VMEM_LIMIT = 56 * 1024 * 1024

SLAB_W = 512
ROW_CONV_B, ROW_CONV_W, ROW_BA, ROW_BX, ROW_LAM, ROW_PB, ROW_PS, ROW_GL, ROW_GP = 0, 1, 5, 6, 7, 8, 9, 10, 11
ROW_GA, ROW_GX, ROW_PW = 16, 80, 144
ROW_MIX, ROW_FFN, ROW_FIN, ROW_LOSS = 272, 274, 276, 278
MIX_SLAB_ROWS = 272
SLAB_ROWS = 288


def _cp(sem=None, **kw):
    if sem is not None:
        kw["dimension_semantics"] = sem
    return pltpu.CompilerParams(vmem_limit_bytes=VMEM_LIMIT, **kw)


def _const_spec(shape):
    nd = len(shape)
    return pl.BlockSpec(shape, lambda *_: (0,) * nd, pipeline_mode=pl.Buffered(1))


def _sigmoid(x):
    return 1.0 / (1.0 + jnp.exp(-x))


def _dot(a, b):
    return jnp.dot(a, b, preferred_element_type=F32)


def _dot_nt(a, b):
    return lax.dot_general(a, b, (((1,), (1,)), ((), ())), preferred_element_type=F32)


def _dot_tn(a, b):
    return lax.dot_general(a, b, (((0,), (0,)), ((), ())), preferred_element_type=F32)


def _colsum8(v):
    m, c = v.shape
    return v.reshape(m // SUBLANES, SUBLANES, c).sum(axis=0)


def _rowmean(v):
    return jnp.mean(v, axis=-1, keepdims=True)


def _rms_bwd(dy, xhat, r, g):
    dxh = dy * g
    return r * (dxh - xhat * _rowmean(dxh * xhat))


def _softplus_neg(lam):
    z = -lam
    e = jnp.exp(-jnp.abs(z))
    u = 1.0 + e
    d = u - 1.0
    log1p = jnp.where(d == 0.0, e, jnp.log(u) * (e / jnp.where(d == 0.0, 1.0, d)))
    return jnp.maximum(z, 0.0) + log1p


def _neg_expm1(z):
    series = -(z * (1.0 + z * (0.5 + z * (1.0 / 6.0 + z * (1.0 / 24.0)))))
    return jnp.where(z > -0.03, series, 1.0 - jnp.exp(z))


_GELU_C = math.sqrt(2.0 / math.pi)
_GELU_K = 0.044715


def _gelu_parts(x):
    x2 = x * x
    th = jnp.tanh(_GELU_C * (x + _GELU_K * x2 * x))
    ge = 0.5 * x * (1.0 + th)
    dge = 0.5 * (1.0 + th) + 0.5 * x * (1.0 - th * th) * (_GELU_C * (1.0 + 3.0 * _GELU_K * x2))
    return ge, dge


def _shift_down(halo, tile, k):
    if k == 0:
        return tile
    ext = jnp.concatenate([halo, tile], axis=0)
    n = tile.shape[0]
    h = halo.shape[0]
    return ext[h - k:h - k + n]


def _shift_up(tile, nxt, k):
    if k == 0:
        return tile
    ext = jnp.concatenate([tile, nxt], axis=0)
    return ext[k:k + tile.shape[0]]


def _build_gate_blocks(ga_ref, gx_ref, gw_ref):
    hd = ga_ref.shape[0]
    per = GATE_BLOCK // hd
    lane = lax.broadcasted_iota(jnp.int32, (hd, GATE_BLOCK), 1)
    for b in range(gw_ref.shape[0]):
        for src, off in ((ga_ref, 0), (gx_ref, GATE_BLOCK)):
            blk = src[:, b * GATE_BLOCK:(b + 1) * GATE_BLOCK]
            for hh in range(per):
                m = (lane >= hh * hd) & (lane < (hh + 1) * hd)
                gw_ref[b, hh * hd:(hh + 1) * hd, off:off + GATE_BLOCK] = jnp.where(m, blk, 0.0).astype(BF16)


def _scan_level1(a, b, reverse):
    m, c = a.shape
    a3 = a.reshape(m // SUBLANES, SUBLANES, c)
    b3 = b.reshape(m // SUBLANES, SUBLANES, c)
    row = lax.broadcasted_iota(jnp.int32, a3.shape, 1)
    for s in (1, 2, 4):
        sh = (SUBLANES - s) if reverse else s
        a_sh = pltpu.roll(a3, sh, 1)
        b_sh = pltpu.roll(b3, sh, 1)
        ok = (row < SUBLANES - s) if reverse else (row >= s)
        b3 = jnp.where(ok, a3 * b_sh + b3, b3)
        a3 = jnp.where(ok, a3 * a_sh, a3)
    return a3.reshape(m, c), b3.reshape(m, c)


def _scan_level2(a_ref, b_ref, out_ref, carry, reverse):
    m, c = a_ref.shape
    ng = m // SUBLANES

    def step(g, cr):
        gi = (ng - 1 - g) if reverse else g
        off = pl.multiple_of(gi * SUBLANES, SUBLANES)
        h = b_ref[pl.ds(off, SUBLANES), :] + a_ref[pl.ds(off, SUBLANES), :] * cr
        out_ref[pl.ds(off, SUBLANES), :] = h
        edge = h[0:1, :] if reverse else h[SUBLANES - 1:SUBLANES, :]
        return jnp.broadcast_to(edge, (SUBLANES, c))

    return lax.fori_loop(0, ng, step, carry, unroll=4)


def _mixer_recompute(u, hal, t0, cw, cb, gw_ref, ba, bx, lam, pw_ref, pb, ps):
    tm = u.shape[0]
    lw = cb.shape[1]
    u_l, u_g, u_p = u[:, :lw], u[:, lw:2 * lw], u[:, 2 * lw:]
    hal_l, hal_p = hal[:, :lw], hal[:, 2 * lw:]
    taps = [_shift_down(hal_l, u_l, CONV_WIDTH - 1 - k) for k in range(CONV_WIDTH)]
    xc = cb
    for k in range(CONV_WIDTH):
        xc = xc + taps[k] * cw[k:k + 1, :]
    xcb = xc.astype(BF16)
    nb = lw // GATE_BLOCK
    gs = [_dot(xcb[:, b * GATE_BLOCK:(b + 1) * GATE_BLOCK], gw_ref[b]) for b in range(nb)]
    r = _sigmoid(jnp.concatenate([g[:, :GATE_BLOCK] for g in gs], axis=1) + ba)
    ig = _sigmoid(jnp.concatenate([g[:, GATE_BLOCK:] for g in gs], axis=1) + bx)
    sp = _softplus_neg(lam)
    la = (-LRU_C * r) * sp
    a = jnp.exp(la)
    m2raw = _neg_expm1(2.0 * la)
    mult = jnp.sqrt(jnp.maximum(m2raw, 1e-12))
    ge, dge = _gelu_parts(u_g)
    row = lax.broadcasted_iota(jnp.int32, (tm, LANES), 0) + t0
    pooled, invs, zs = [], [], []
    for gi, w in enumerate(POOL_WINDOWS):
        e = jnp.concatenate([hal_p[:, gi * LANES:(gi + 1) * LANES], u_p[:, gi * LANES:(gi + 1) * LANES]], axis=0)
        s = e
        k = 1
        while k < w:
            s = s + pltpu.roll(s, k, 0)
            k *= 2
        inv = 1.0 / jnp.minimum(row + 1, w).astype(F32)
        pg = s[HALO:] * inv - e[HALO:]
        pooled.append(pg)
        invs.append(inv)
        zs.append(_dot(pg.astype(BF16), pw_ref[:, gi * LANES:(gi + 1) * LANES].astype(BF16)))
    z = jnp.concatenate(zs, axis=1) + pb
    y_pool = z * ps
    return dict(u_l=u_l, u_g=u_g, taps=taps, xc=xc, xcb=xcb, r=r, ig=ig, sp=sp, la=la, a=a, m2raw=m2raw,
                mult=mult, ge=ge, dge=dge, pooled=pooled, invs=invs, z=z, y_pool=y_pool)


ANY = pl.BlockSpec(memory_space=pl.ANY)
VMEM_SPEC = pl.BlockSpec(memory_space=pltpu.VMEM)


class _Hosted:
    def __init__(self, ins, out_shapes, sems, start, finish, mid=None, aliases=None):
        self.ins, self.out_shapes, self.sems = list(ins), list(out_shapes), list(sems)
        self.start, self.mid, self.finish = start, mid, finish
        self.aliases = dict(aliases or {})


def _call(body, hosted, stage_preds, *, name, grid, in_specs, out_specs, out_shape, scratch_shapes, args, sem):
    hosted = list(hosted or [])
    n_in, n_out, n_scr = len(in_specs), len(out_specs), len(scratch_shapes)
    c_in = [a for h in hosted for a in h.ins]
    c_out = [o for h in hosted for o in h.out_shapes]
    c_sem = [pltpu.SemaphoreType.DMA((k,)) for h in hosted for k in h.sems]

    def full(*refs):
        p = 0
        parts = []
        for cnt in (n_in, len(c_in), n_out, len(c_out), n_scr, len(c_sem)):
            parts.append(refs[p:p + cnt])
            p += cnt
        hi, ci, ho, co, hs, cs = parts
        per = []
        a = b = c_ = 0
        for h in hosted:
            per.append((h, ci[a:a + len(h.ins)], co[b:b + len(h.out_shapes)], cs[c_:c_ + len(h.sems)]))
            a, b, c_ = a + len(h.ins), b + len(h.out_shapes), c_ + len(h.sems)
        first = mid = last = None
        if hosted and grid:
            first, mid, last = stage_preds()

        def run(fn, pred, i_, o_, s_):
            if fn is None:
                return
            if pred is None:
                fn(i_, o_, s_)
            else:
                pl.when(pred)(functools.partial(fn, i_, o_, s_))

        for h, i_, o_, s_ in per:
            run(h.start, first, i_, o_, s_)
        body(*hi, *ho, *hs)
        for h, i_, o_, s_ in per:
            run(h.mid, mid, i_, o_, s_)
        for h, i_, o_, s_ in per:
            run(h.finish, last, i_, o_, s_)

    aliases = {}
    a = b = 0
    for h in hosted:
        for k, v in h.aliases.items():
            aliases[n_in + a + k] = n_out + b + v
        a, b = a + len(h.ins), b + len(h.out_shapes)
    res = pl.pallas_call(
        full, name=name, grid=grid, in_specs=list(in_specs) + [ANY] * len(c_in),
        out_specs=list(out_specs) + [ANY] * len(c_out), out_shape=list(out_shape) + c_out,
        scratch_shapes=list(scratch_shapes) + c_sem, input_output_aliases=aliases,
        compiler_params=_cp(sem))(*args, *c_in)
    res = list(res)
    outs = []
    p = n_out
    for h in hosted:
        outs.append(res[p:p + len(h.out_shapes)])
        p += len(h.out_shapes)
    return res[:n_out], outs


def _inproj(x, g_mix, w_in):
    s, d = x.shape
    n = w_in.shape[1]
    tm = min(TM_PROJ, s)

    def body(x_ref, g_ref, w_ref, u_ref):
        xv = x_ref[...]
        r = lax.rsqrt(_rowmean(xv * xv) + EPS)
        u_ref[...] = _dot((xv * r * g_ref[...]).astype(BF16), w_ref[...])

    return pl.pallas_call(
        body, grid=(s // tm,), name="inproj",
        in_specs=[pl.BlockSpec((tm, d), lambda i: (i, 0)), _const_spec((1, d)), _const_spec((d, n))],
        out_specs=pl.BlockSpec((tm, n), lambda i: (i, 0)),
        out_shape=SDS((s, n), F32), compiler_params=_cp(("parallel",)))(x, g_mix, w_in)


def _mixer_fwd(u, x, sp_, w_out, hosted=None):
    s, din = u.shape
    d = x.shape[1]
    lw = din // 3
    tm = min(TM_MIX, s)
    nb = lw // GATE_BLOCK

    def body(u_ref, halo_ref, x_ref, cw_ref, cb_ref, ga_ref, gx_ref, ba_ref, bx_ref, lam_ref, pw_ref, pb_ref,
             ps_ref, gl_ref, gp_ref, wout_ref, h_ref, yn_ref, hres_ref, gw_s, a_s, b_s, carry_s):
        i = pl.program_id(0)

        @pl.when(i == 0)
        def _():
            _build_gate_blocks(ga_ref, gx_ref, gw_s)
            carry_s[...] = jnp.zeros_like(carry_s)

        uv = u_ref[...]
        hal = jnp.where(i > 0, halo_ref[...], 0.0)
        f = _mixer_recompute(uv, hal, i * tm, cw_ref[...], cb_ref[...], gw_s, ba_ref[...], bx_ref[...],
                             lam_ref[...], pw_ref, pb_ref[...], ps_ref[...])
        bb = f["mult"] * (f["ig"] * f["xc"])
        a1, b1 = _scan_level1(f["a"], bb, reverse=False)
        a_s[...] = a1
        b_s[...] = b1
        carry_s[...] = _scan_level2(a_s, b_s, h_ref, carry_s[...], reverse=False)
        y_lru = h_ref[...] * f["ge"]
        rl = lax.rsqrt(_rowmean(y_lru * y_lru) + EPS)
        yp = f["y_pool"]
        rp = lax.rsqrt(_rowmean(yp * yp) + EPS)
        yn = jnp.concatenate([y_lru * rl * gl_ref[...], yp * rp * gp_ref[...]], axis=1).astype(BF16)
        yn_ref[...] = yn
        hres_ref[...] = x_ref[...] + _dot(yn, wout_ref[...])

    small = [sp_[k] for k in ("conv_w", "conv_b", "gate_a_w", "gate_x_w", "gate_a_b", "gate_x_b", "lru_lambda",
                              "pool_w", "pool_b", "pool_scale", "norm_lru_g", "norm_pool_g")]
    nt = s // tm

    def stages():
        i = pl.program_id(0)
        return i == 0, i == max(nt - 3, 0), i == nt - 1

    return _call(
        body, hosted, stages, grid=(nt,), name="mixer_fwd",
        in_specs=[pl.BlockSpec((tm, din), lambda i: (i, 0)),
                  pl.BlockSpec((HALO, din), lambda i: (jnp.maximum(i * (tm // HALO) - 1, 0), 0)),
                  pl.BlockSpec((tm, d), lambda i: (i, 0))]
        + [_const_spec(a.shape) for a in small] + [_const_spec(w_out.shape)],
        out_specs=[pl.BlockSpec((tm, lw), lambda i: (i, 0)), pl.BlockSpec((tm, d), lambda i: (i, 0)),
                   pl.BlockSpec((tm, d), lambda i: (i, 0))],
        out_shape=[SDS((s, lw), F32), SDS((s, d), BF16), SDS((s, d), F32)],
        scratch_shapes=[pltpu.VMEM((nb, GATE_BLOCK, 2 * GATE_BLOCK), BF16), pltpu.VMEM((tm, lw), F32),
                        pltpu.VMEM((tm, lw), F32), pltpu.VMEM((SUBLANES, lw), F32)],
        args=(u, u, x, *small, w_out), sem=("arbitrary",))


def _ffn_fwd(hres1, target, g_ffn, g_fin, w1, w3, w2):
    s, d = hres1.shape
    nj, _, fc = w1.shape
    tm = min(TM_FFN, s)

    def body(h_ref, t_ref, gf_ref, gn_ref, w1_ref, w3_ref, w2_ref,
             a1_ref, a3_ref, h2_ref, dh_ref, dhb_ref, loss_ref, dgn_ref, acc_s):
        i, j = pl.program_id(0), pl.program_id(1)

        @pl.when((i == 0) & (j == 0))
        def _():
            loss_ref[...] = jnp.zeros_like(loss_ref)
            dgn_ref[...] = jnp.zeros_like(dgn_ref)

        @pl.when(j == 0)
        def _():
            hv = h_ref[...]
            r = lax.rsqrt(_rowmean(hv * hv) + EPS)
            h2_ref[...] = (hv * r * gf_ref[...]).astype(BF16)

        h2 = h2_ref[...]
        a1 = _dot(h2, w1_ref[0])
        a3 = _dot(h2, w3_ref[0])
        a1_ref[0] = a1.astype(BF16)
        a3_ref[0] = a3.astype(BF16)
        part = _dot(((a1 * _sigmoid(a1)) * a3).astype(BF16), w2_ref[0])

        @pl.when(j == 0)
        def _():
            acc_s[...] = part

        @pl.when(j > 0)
        def _():
            acc_s[...] += part

        @pl.when(j == nj - 1)
        def _():
            hr2 = h_ref[...] + acc_s[...]
            r2 = lax.rsqrt(_rowmean(hr2 * hr2) + EPS)
            xh = hr2 * r2
            gn = gn_ref[...]
            diff = xh * gn - t_ref[...]
            tot = jnp.sum(jnp.sum(diff * diff, axis=1, keepdims=True), axis=0, keepdims=True)
            loss_ref[...] += tot * (0.5 / d)
            dout = diff * (1.0 / d)
            dgn_ref[...] += _colsum8(dout * xh)
            dh = _rms_bwd(dout, xh, r2, gn)
            dh_ref[...] = dh
            dhb_ref[...] = dh.astype(BF16)

    return pl.pallas_call(
        body, grid=(s // tm, nj), name="ffn_fwd",
        in_specs=[pl.BlockSpec((tm, d), lambda i, j: (i, 0)), pl.BlockSpec((tm, d), lambda i, j: (i, 0)),
                  _const_spec((1, d)), _const_spec((1, d)),
                  pl.BlockSpec((1, d, fc), lambda i, j: (j, 0, 0)), pl.BlockSpec((1, d, fc), lambda i, j: (j, 0, 0)),
                  pl.BlockSpec((1, fc, d), lambda i, j: (j, 0, 0))],
        out_specs=[pl.BlockSpec((1, tm, fc), lambda i, j: (j, i, 0)), pl.BlockSpec((1, tm, fc), lambda i, j: (j, i, 0)),
                   pl.BlockSpec((tm, d), lambda i, j: (i, 0)), pl.BlockSpec((tm, d), lambda i, j: (i, 0)),
                   pl.BlockSpec((tm, d), lambda i, j: (i, 0)),
                   pl.BlockSpec((SUBLANES, LANES), lambda i, j: (0, 0)),
                   pl.BlockSpec((SUBLANES, d), lambda i, j: (0, 0))],
        out_shape=[SDS((nj, s, fc), BF16), SDS((nj, s, fc), BF16), SDS((s, d), BF16), SDS((s, d), F32),
                   SDS((s, d), BF16), SDS((SUBLANES, LANES), F32), SDS((SUBLANES, d), F32)],
        scratch_shapes=[pltpu.VMEM((tm, d), F32)],
        compiler_params=_cp(("arbitrary", "arbitrary")))(hres1, target, g_ffn, g_fin, w1, w3, w2)


def _ffn_bwd_act(dh, dhb, a1, a3, hres1, g_ffn, w1, w3, w2):
    s, d = hres1.shape
    nj, _, fc = a1.shape
    tm = min(TM_FFN, s)

    def body(dh_ref, dhb_ref, a1_ref, a3_ref, h_ref, gf_ref, w1_ref, w3_ref, w2_ref,
             da1_ref, da3_ref, dhr_ref, dgf_ref, acc_s):
        i, j = pl.program_id(0), pl.program_id(1)

        @pl.when((i == 0) & (j == 0))
        def _():
            dgf_ref[...] = jnp.zeros_like(dgf_ref)

        @pl.when(j == 0)
        def _():
            acc_s[...] = jnp.zeros_like(acc_s)

        rc = tm // FFN_ROW_CHUNKS
        for q in range(FFN_ROW_CHUNKS):
            rows = slice(q * rc, (q + 1) * rc)
            dff = _dot_nt(dhb_ref[rows, :], w2_ref[0])
            a1v = a1_ref[0, rows, :].astype(F32)
            a3v = a3_ref[0, rows, :].astype(F32)
            sg = _sigmoid(a1v)
            silu = a1v * sg
            da1 = (dff * a3v * (sg * (1.0 + a1v * (1.0 - sg)))).astype(BF16)
            da3 = (dff * silu).astype(BF16)
            da1_ref[0, rows, :] = da1
            da3_ref[0, rows, :] = da3
            acc_s[rows, :] += _dot_nt(da1, w1_ref[0]) + _dot_nt(da3, w3_ref[0])

        @pl.when(j == nj - 1)
        def _():
            hv = h_ref[...]
            r = lax.rsqrt(_rowmean(hv * hv) + EPS)
            xh = hv * r
            dh2 = acc_s[...]
            dgf_ref[...] += _colsum8(dh2 * xh)
            dhr_ref[...] = dh_ref[...] + _rms_bwd(dh2, xh, r, gf_ref[...])

    return pl.pallas_call(
        body, grid=(s // tm, nj), name="ffn_bwd_act",
        in_specs=[pl.BlockSpec((tm, d), lambda i, j: (i, 0)), pl.BlockSpec((tm, d), lambda i, j: (i, 0)),
                  pl.BlockSpec((1, tm, fc), lambda i, j: (j, i, 0)), pl.BlockSpec((1, tm, fc), lambda i, j: (j, i, 0)),
                  pl.BlockSpec((tm, d), lambda i, j: (i, 0)), _const_spec((1, d)),
                  pl.BlockSpec((1, d, fc), lambda i, j: (j, 0, 0)), pl.BlockSpec((1, d, fc), lambda i, j: (j, 0, 0)),
                  pl.BlockSpec((1, fc, d), lambda i, j: (j, 0, 0))],
        out_specs=[pl.BlockSpec((1, tm, fc), lambda i, j: (j, i, 0)), pl.BlockSpec((1, tm, fc), lambda i, j: (j, i, 0)),
                   pl.BlockSpec((tm, d), lambda i, j: (i, 0)), pl.BlockSpec((SUBLANES, d), lambda i, j: (0, 0))],
        out_shape=[SDS((nj, s, fc), BF16), SDS((nj, s, fc), BF16), SDS((s, d), F32), SDS((SUBLANES, d), F32)],
        scratch_shapes=[pltpu.VMEM((tm, d), F32)],
        compiler_params=_cp(("arbitrary", "arbitrary")))(dh, dhb, a1, a3, hres1, g_ffn, w1, w3, w2)


def _ffn_wgrad(h2, dhb, a1, a3, da1, da3):
    s, d = h2.shape
    _, _, fc = a1.shape
    tm = min(TM_WGRAD, s)

    def body(h2_ref, dhb_ref, a1_ref, a3_ref, da1_ref, da3_ref, dw1_ref, dw3_ref, dw2_ref):
        i = pl.program_id(1)

        @pl.when(i == 0)
        def _():
            dw1_ref[...] = jnp.zeros_like(dw1_ref)
            dw3_ref[...] = jnp.zeros_like(dw3_ref)
            dw2_ref[...] = jnp.zeros_like(dw2_ref)

        h2v = h2_ref[...]
        a1v = a1_ref[0].astype(F32)
        ff = ((a1v * _sigmoid(a1v)) * a3_ref[0].astype(F32)).astype(BF16)
        dw1_ref[0] += _dot_tn(h2v, da1_ref[0])
        dw3_ref[0] += _dot_tn(h2v, da3_ref[0])
        dw2_ref[0] += _dot_tn(ff, dhb_ref[...])

    return pl.pallas_call(
        body, grid=(N_CHIPS, s // tm), name="ffn_wgrad",
        in_specs=[pl.BlockSpec((tm, d), lambda j, i: (i, 0)), pl.BlockSpec((tm, d), lambda j, i: (i, 0))]
        + [pl.BlockSpec((1, tm, fc), lambda j, i: (j, i, 0))] * 4,
        out_specs=[pl.BlockSpec((1, d, fc), lambda j, i: (j, 0, 0)), pl.BlockSpec((1, d, fc), lambda j, i: (j, 0, 0)),
                   pl.BlockSpec((1, fc, d), lambda j, i: (j, 0, 0))],
        out_shape=[SDS((N_CHIPS, d, fc), F32), SDS((N_CHIPS, d, fc), F32), SDS((N_CHIPS, fc, d), F32)],
        compiler_params=_cp(("parallel", "arbitrary")))(h2, dhb, a1, a3, da1, da3)


def _row_chunks(tm):
    rc = tm // FFN_ROW_CHUNKS
    return [slice(q * rc, (q + 1) * rc) for q in range(FFN_ROW_CHUNKS)]


def _ffn_up(hres1, g_ffn, w1, w3):
    s, d = hres1.shape
    nj, _, fc = w1.shape
    tm = min(TM_FFN_UP, s)

    def body(h_ref, gf_ref, w1_ref, w3_ref, h2_ref, a1_ref, a3_ref, ff_ref):
        @pl.when(pl.program_id(1) == 0)
        def _():
            hv = h_ref[...]
            r = lax.rsqrt(_rowmean(hv * hv) + EPS)
            h2_ref[...] = (hv * r * gf_ref[...]).astype(BF16)

        j = pl.program_id(1)
        for rows in _row_chunks(tm):
            h2 = h2_ref[rows, :]
            a1 = _dot(h2, w1_ref[j])
            a3 = _dot(h2, w3_ref[j])
            a1_ref[0, rows, :] = a1.astype(BF16)
            a3_ref[0, rows, :] = a3.astype(BF16)
            ff_ref[0, rows, :] = ((a1 * _sigmoid(a1)) * a3).astype(BF16)

    wspec = _const_spec(w1.shape)
    aspec = pl.BlockSpec((1, tm, fc), lambda i, j: (j, i, 0))
    return pl.pallas_call(
        body, grid=(s // tm, nj), name="ffn_up",
        in_specs=[pl.BlockSpec((tm, d), lambda i, j: (i, 0)), _const_spec((1, d)), wspec, wspec],
        out_specs=[pl.BlockSpec((tm, d), lambda i, j: (i, 0)), aspec, aspec, aspec],
        out_shape=[SDS((s, d), BF16)] + [SDS((nj, s, fc), BF16)] * 3,
        compiler_params=_cp(("parallel", "arbitrary")))(hres1, g_ffn, w1, w3)


def _ffn_down(ff, hres1, target, g_fin, w2):
    s, d = hres1.shape
    nj, _, fc = ff.shape
    tm = min(TM_FFN_DOWN, s)

    def body(ff_ref, h_ref, t_ref, gn_ref, w2_ref, dh_ref, dhb_ref, loss_ref, dgn_ref):
        @pl.when(pl.program_id(0) == 0)
        def _():
            loss_ref[...] = jnp.zeros_like(loss_ref)
            dgn_ref[...] = jnp.zeros_like(dgn_ref)

        gn = gn_ref[...]
        for rows in _row_chunks(tm):
            acc = _dot(ff_ref[0, rows, :], w2_ref[0])
            for j in range(1, nj):
                acc = acc + _dot(ff_ref[j, rows, :], w2_ref[j])
            hr2 = h_ref[rows, :] + acc
            r2 = lax.rsqrt(_rowmean(hr2 * hr2) + EPS)
            xh = hr2 * r2
            diff = xh * gn - t_ref[rows, :]
            tot = jnp.sum(jnp.sum(diff * diff, axis=1, keepdims=True), axis=0, keepdims=True)
            loss_ref[...] += tot * (0.5 / d)
            dout = diff * (1.0 / d)
            dgn_ref[...] += _colsum8(dout * xh)
            dh = _rms_bwd(dout, xh, r2, gn)
            dh_ref[rows, :] = dh
            dhb_ref[rows, :] = dh.astype(BF16)

    tile = pl.BlockSpec((tm, d), lambda i: (i, 0))
    return pl.pallas_call(
        body, grid=(s // tm,), name="ffn_down",
        in_specs=[pl.BlockSpec((nj, tm, fc), lambda i: (0, i, 0)), tile, tile, _const_spec((1, d)),
                  _const_spec(w2.shape)],
        out_specs=[tile, tile, pl.BlockSpec((SUBLANES, LANES), lambda i: (0, 0)),
                   pl.BlockSpec((SUBLANES, d), lambda i: (0, 0))],
        out_shape=[SDS((s, d), F32), SDS((s, d), BF16), SDS((SUBLANES, LANES), F32), SDS((SUBLANES, d), F32)],
        compiler_params=_cp(("arbitrary",)))(ff, hres1, target, g_fin, w2)


def _ffn_bwd_gate(dhb, a1, a3, w2):
    s, d = dhb.shape
    nj, _, fc = a1.shape
    tm = min(TM_FFN_UP, s)

    def body(dhb_ref, a1_ref, a3_ref, w2_ref, da1_ref, da3_ref):
        j = pl.program_id(1)
        for rows in _row_chunks(tm):
            dff = _dot_nt(dhb_ref[rows, :], w2_ref[j])
            a1v = a1_ref[0, rows, :].astype(F32)
            sg = _sigmoid(a1v)
            silu = a1v * sg
            da1_ref[0, rows, :] = (dff * a3_ref[0, rows, :].astype(F32) * (sg * (1.0 + (a1v - silu)))).astype(BF16)
            da3_ref[0, rows, :] = (dff * silu).astype(BF16)

    aspec = pl.BlockSpec((1, tm, fc), lambda i, j: (j, i, 0))
    return pl.pallas_call(
        body, grid=(s // tm, nj), name="ffn_bwd_gate",
        in_specs=[pl.BlockSpec((tm, d), lambda i, j: (i, 0)), aspec, aspec, _const_spec(w2.shape)],
        out_specs=[aspec, aspec], out_shape=[SDS((nj, s, fc), BF16)] * 2,
        compiler_params=_cp(("parallel", "arbitrary")))(dhb, a1, a3, w2)


def _ffn_bwd_down(da1, da3, dh, hres1, g_ffn, w1, w3, hosted=None):
    s, d = hres1.shape
    nj, _, fc = da1.shape
    tm = min(TM_FFN_DOWN, s)
    nt = s // tm

    def body(da1_ref, da3_ref, dh_ref, h_ref, gf_ref, w1_ref, w3_ref, dhr_ref, dgf_ref):
        @pl.when(pl.program_id(0) == 0)
        def _():
            dgf_ref[...] = jnp.zeros_like(dgf_ref)

        gf = gf_ref[...]
        for rows in _row_chunks(tm):
            dh2 = None
            for j in range(nj):
                part = _dot_nt(da1_ref[j, rows, :], w1_ref[j]) + _dot_nt(da3_ref[j, rows, :], w3_ref[j])
                dh2 = part if dh2 is None else dh2 + part
            hv = h_ref[rows, :]
            r = lax.rsqrt(_rowmean(hv * hv) + EPS)
            xh = hv * r
            dgf_ref[...] += _colsum8(dh2 * xh)
            dhr_ref[rows, :] = dh_ref[rows, :] + _rms_bwd(dh2, xh, r, gf)

    tile = pl.BlockSpec((tm, d), lambda i: (i, 0))
    aspec = pl.BlockSpec((nj, tm, fc), lambda i: (0, i, 0))
    wspec = _const_spec(w1.shape)

    def stages():
        i = pl.program_id(0)
        return i == 0, i == max(nt - 2, 0), i == nt - 1

    return _call(
        body, hosted, stages, grid=(nt,), name="ffn_bwd_down",
        in_specs=[aspec, aspec, tile, tile, _const_spec((1, d)), wspec, wspec],
        out_specs=[tile, pl.BlockSpec((SUBLANES, d), lambda i: (0, 0))],
        out_shape=[SDS((s, d), F32), SDS((SUBLANES, d), F32)],
        scratch_shapes=[], args=(da1, da3, dh, hres1, g_ffn, w1, w3), sem=("arbitrary",))


def _ffn_wgrad2(h2, dhb, ff, da1, da3):
    s, d = h2.shape
    _, _, fc = ff.shape
    tm = min(TM_WGRAD, s)

    def body(h2_ref, dhb_ref, ff_ref, da1_ref, da3_ref, dw1_ref, dw3_ref, dw2_ref):
        @pl.when(pl.program_id(1) == 0)
        def _():
            dw1_ref[...] = jnp.zeros_like(dw1_ref)
            dw3_ref[...] = jnp.zeros_like(dw3_ref)
            dw2_ref[...] = jnp.zeros_like(dw2_ref)

        h2v = h2_ref[...]
        dw1_ref[0] += _dot_tn(h2v, da1_ref[0])
        dw3_ref[0] += _dot_tn(h2v, da3_ref[0])
        dw2_ref[0] += _dot_tn(ff_ref[0], dhb_ref[...])

    return pl.pallas_call(
        body, grid=(N_CHIPS, s // tm), name="ffn_wgrad",
        in_specs=[pl.BlockSpec((tm, d), lambda j, i: (i, 0)), pl.BlockSpec((tm, d), lambda j, i: (i, 0))]
        + [pl.BlockSpec((1, tm, fc), lambda j, i: (j, i, 0))] * 3,
        out_specs=[pl.BlockSpec((1, d, fc), lambda j, i: (j, 0, 0)), pl.BlockSpec((1, d, fc), lambda j, i: (j, 0, 0)),
                   pl.BlockSpec((1, fc, d), lambda j, i: (j, 0, 0))],
        out_shape=[SDS((N_CHIPS, d, fc), F32), SDS((N_CHIPS, d, fc), F32), SDS((N_CHIPS, fc, d), F32)],
        compiler_params=_cp(("parallel", "arbitrary")))(h2, dhb, ff, da1, da3)


def _mixer_bwd(u, h, dhres1, sp_, w_out, hosted=None):
    s, din = u.shape
    d = dhres1.shape[1]
    lw = din // 3
    tm = min(TM_MIX, s)
    nt = s // tm
    nb = lw // GATE_BLOCK
    hd = sp_["gate_a_w"].shape[0]

    def body(u_ref, halo_ref, h_ref, hhalo_ref, dhr_ref, cw_ref, cb_ref, ga_ref, gx_ref, ba_ref, bx_ref, lam_ref,
             pw_ref, pb_ref, ps_ref, gl_ref, gp_ref, wout_ref, du_ref, slab_ref,
             gw_s, a_s, b_s, e_s, ecarry_s, dxc_s, q_s, vec_s, cwacc_s, dgw_s, dpw_s):
        i = pl.program_id(0)
        tile = nt - 1 - i

        @pl.when(i == 0)
        def _():
            _build_gate_blocks(ga_ref, gx_ref, gw_s)
            for ref in (ecarry_s, dxc_s, q_s, vec_s, cwacc_s, dgw_s, dpw_s):
                ref[...] = jnp.zeros_like(ref)

        uv = u_ref[...]
        hal = jnp.where(tile > 0, halo_ref[...], 0.0)
        cw = cw_ref[...]
        lam = lam_ref[...]
        ps = ps_ref[...]
        f = _mixer_recompute(uv, hal, tile * tm, cw, cb_ref[...], gw_s, ba_ref[...], bx_ref[...], lam, pw_ref,
                             pb_ref[...], ps)
        hv = h_ref[...]
        h_prev = _shift_down(jnp.where(tile > 0, hhalo_ref[...], 0.0), hv, 1)
        y_lru = hv * f["ge"]
        rl = lax.rsqrt(_rowmean(y_lru * y_lru) + EPS)
        yp = f["y_pool"]
        rp = lax.rsqrt(_rowmean(yp * yp) + EPS)
        xh_l = y_lru * rl
        xh_p = yp * rp

        dyn = _dot_nt(dhr_ref[...].astype(BF16), wout_ref[...])
        d_nl, d_np = dyn[:, :lw], dyn[:, lw:]
        vec = {}
        vec[ROW_GL] = _colsum8(d_nl * xh_l)
        vec[ROW_GP] = _colsum8(d_np * xh_p)
        d_ylru = _rms_bwd(d_nl, xh_l, rl, gl_ref[...])
        d_ypool = _rms_bwd(d_np, xh_p, rp, gp_ref[...])

        vec[ROW_PS] = _colsum8(d_ypool * f["z"])
        dz = d_ypool * ps
        vec[ROW_PB] = _colsum8(dz)
        dzb = dz.astype(BF16)
        dup = []
        for gi, w in enumerate(POOL_WINDOWS):
            sl = slice(gi * LANES, (gi + 1) * LANES)
            dpw_s[:, sl] += _dot_tn(f["pooled"][gi].astype(BF16), dzb[:, sl])
            dpool = _dot_nt(dzb[:, sl], pw_ref[:, sl].astype(BF16))
            q = dpool * f["invs"][gi]
            e = jnp.concatenate([q, q_s[:, sl]], axis=0)
            k = 1
            while k < w:
                e = e + pltpu.roll(e, tm + HALO - k, 0)
                k *= 2
            dup.append(e[:tm] - dpool)
            q_s[:, sl] = q[:HALO]

        d_hout = d_ylru * f["ge"]
        d_ug = d_ylru * hv * f["dge"]
        a = f["a"]
        a1, b1 = _scan_level1(a, a * d_hout, reverse=True)
        a_s[...] = a1
        b_s[...] = b1
        e_next = ecarry_s[...]
        ecarry_s[...] = _scan_level2(a_s, b_s, e_s, e_next, reverse=True)
        sv = d_hout + _shift_up(e_s[...], e_next, 1)
        d_a = sv * h_prev
        mult, ig, xc, r = f["mult"], f["ig"], f["xc"], f["r"]
        d_mult = sv * (ig * xc)
        d_ig = sv * mult * xc
        d_xc = sv * mult * ig
        d_la = d_a * a + jnp.where(f["m2raw"] > 1e-12, d_mult * (-(a * a) / mult), 0.0)
        d_r = d_la * (-LRU_C * f["sp"])
        vec[ROW_LAM] = _colsum8(d_la * (-LRU_C * r))
        d_pr = d_r * r * (1.0 - r)
        d_pi = d_ig * ig * (1.0 - ig)
        vec[ROW_BA] = _colsum8(d_pr)
        vec[ROW_BX] = _colsum8(d_pi)
        dxc_parts = []
        for b in range(nb):
            sl = slice(b * GATE_BLOCK, (b + 1) * GATE_BLOCK)
            rhs = jnp.concatenate([d_pr[:, sl], d_pi[:, sl]], axis=1).astype(BF16)
            dgw_s[b] += _dot_tn(f["xcb"][:, sl], rhs)
            dxc_parts.append(_dot_nt(rhs, gw_s[b]))
        d_xc = d_xc + jnp.concatenate(dxc_parts, axis=1)
        vec[ROW_CONV_B] = _colsum8(d_xc)
        dxc_next = dxc_s[...]
        d_ul = None
        for k in range(CONV_WIDTH):
            cwacc_s[k * SUBLANES:(k + 1) * SUBLANES, :] += _colsum8(d_xc * f["taps"][k])
            term = _shift_up(d_xc, dxc_next, CONV_WIDTH - 1 - k) * cw[k:k + 1, :]
            d_ul = term if d_ul is None else d_ul + term
        dxc_s[...] = d_xc[:SUBLANES]
        for row, val in vec.items():
            vec_s[row * SUBLANES:(row + 1) * SUBLANES, :] += val
        du_ref[...] = jnp.concatenate([d_ul, d_ug] + dup, axis=1).astype(BF16)

        @pl.when(i == nt - 1)
        def _():
            rows = []
            for row in range(ROW_GA):
                if row in (ROW_CONV_W, ROW_CONV_W + 1, ROW_CONV_W + 2, ROW_CONV_W + 3):
                    k = row - ROW_CONV_W
                    v = jnp.sum(cwacc_s[k * SUBLANES:(k + 1) * SUBLANES, :], axis=0, keepdims=True)
                elif row <= ROW_GP:
                    v = jnp.sum(vec_s[row * SUBLANES:(row + 1) * SUBLANES, :], axis=0, keepdims=True)
                    if row == ROW_LAM:
                        v = v * (-1.0 / (1.0 + jnp.exp(lam)))
                else:
                    v = jnp.zeros((1, lw), F32)
                rows.append(v)
            slab_ref[0:ROW_GA, :] = jnp.concatenate(rows, axis=0)
            lane = lax.broadcasted_iota(jnp.int32, (hd, GATE_BLOCK), 1)
            for b in range(nb):
                for off, row0 in ((0, ROW_GA), (GATE_BLOCK, ROW_GX)):
                    acc = jnp.zeros((hd, GATE_BLOCK), F32)
                    for hh in range(GATE_BLOCK // hd):
                        m = (lane >= hh * hd) & (lane < (hh + 1) * hd)
                        acc = acc + jnp.where(m, dgw_s[b, hh * hd:(hh + 1) * hd, off:off + GATE_BLOCK], 0.0)
                    slab_ref[row0:row0 + hd, b * GATE_BLOCK:(b + 1) * GATE_BLOCK] = acc
            slab_ref[ROW_PW:ROW_PW + LANES, :] = dpw_s[...]

    small = [sp_[k] for k in ("conv_w", "conv_b", "gate_a_w", "gate_x_w", "gate_a_b", "gate_x_b", "lru_lambda",
                              "pool_w", "pool_b", "pool_scale", "norm_lru_g", "norm_pool_g")]
    rev = lambda i: nt - 1 - i

    def stages():
        i = pl.program_id(0)
        return i == 0, i == max(nt - 3, 0), i == nt - 1

    return _call(
        body, hosted, stages, grid=(nt,), name="mixer_bwd",
        in_specs=[pl.BlockSpec((tm, din), lambda i: (rev(i), 0)),
                  pl.BlockSpec((HALO, din), lambda i: (jnp.maximum(rev(i) * (tm // HALO) - 1, 0), 0)),
                  pl.BlockSpec((tm, lw), lambda i: (rev(i), 0)),
                  pl.BlockSpec((SUBLANES, lw), lambda i: (jnp.maximum(rev(i) * (tm // SUBLANES) - 1, 0), 0)),
                  pl.BlockSpec((tm, d), lambda i: (rev(i), 0))]
        + [_const_spec(a.shape) for a in small] + [_const_spec(w_out.shape)],
        out_specs=[pl.BlockSpec((tm, din), lambda i: (rev(i), 0)),
                   pl.BlockSpec((MIX_SLAB_ROWS, SLAB_W), lambda i: (0, 0))],
        out_shape=[SDS((s, din), BF16), SDS((MIX_SLAB_ROWS, SLAB_W), F32)],
        scratch_shapes=[pltpu.VMEM((nb, GATE_BLOCK, 2 * GATE_BLOCK), BF16),
                        pltpu.VMEM((tm, lw), F32), pltpu.VMEM((tm, lw), F32), pltpu.VMEM((tm, lw), F32),
                        pltpu.VMEM((SUBLANES, lw), F32), pltpu.VMEM((SUBLANES, lw), F32),
                        pltpu.VMEM((HALO, lw), F32), pltpu.VMEM((ROW_GA * SUBLANES, lw), F32),
                        pltpu.VMEM((CONV_WIDTH * SUBLANES, lw), F32),
                        pltpu.VMEM((nb, GATE_BLOCK, 2 * GATE_BLOCK), F32), pltpu.VMEM((LANES, lw), F32)],
        args=(u, u, h, h, dhres1, *small, w_out), sem=("arbitrary",))


def _inproj_bwd(x, du, dhres1, yn, g_mix, w_in, hosted=None):
    s, d = x.shape
    n = w_in.shape[1]
    nc = n // N_CHIPS
    tm = min(TM_PROJ, s)
    nt = s // tm

    def body(x_ref, du_ref, dhr_ref, yn_ref, g_ref, w_ref, gx_ref, dwin_ref, dwout_ref, dg_ref):
        i = pl.program_id(0)

        @pl.when(i == 0)
        def _():
            dwin_ref[...] = jnp.zeros_like(dwin_ref)
            dwout_ref[...] = jnp.zeros_like(dwout_ref)
            dg_ref[...] = jnp.zeros_like(dg_ref)

        xv = x_ref[...]
        g = g_ref[...]
        r = lax.rsqrt(_rowmean(xv * xv) + EPS)
        xh = xv * r
        h1 = (xh * g).astype(BF16)
        duv = du_ref[...]
        dh1 = _dot_nt(duv, w_ref[...])
        dg_ref[...] += _colsum8(dh1 * xh)
        dhr = dhr_ref[...]
        gx_ref[...] = dhr + _rms_bwd(dh1, xh, r, g)
        for jj in range(N_CHIPS):
            dwin_ref[jj] += _dot_tn(h1, duv[:, jj * nc:(jj + 1) * nc])
        dwout_ref[...] += _dot_tn(yn_ref[...], dhr.astype(BF16))

    def stages():
        i = pl.program_id(0)
        return i == 0, i == max(nt - 3, 0), i == nt - 1

    return _call(
        body, hosted, stages, grid=(nt,), name="inproj_bwd",
        in_specs=[pl.BlockSpec((tm, d), lambda i: (i, 0)), pl.BlockSpec((tm, n), lambda i: (i, 0)),
                  pl.BlockSpec((tm, d), lambda i: (i, 0)), pl.BlockSpec((tm, d), lambda i: (i, 0)),
                  _const_spec((1, d)), _const_spec((d, n))],
        out_specs=[pl.BlockSpec((tm, d), lambda i: (i, 0)), pl.BlockSpec((N_CHIPS, d, nc), lambda i: (0, 0, 0)),
                   pl.BlockSpec((d, d), lambda i: (0, 0)), pl.BlockSpec((SUBLANES, d), lambda i: (0, 0))],
        out_shape=[SDS((s, d), F32), SDS((N_CHIPS, d, nc), F32), SDS((d, d), F32), SDS((SUBLANES, d), F32)],
        scratch_shapes=[], args=(x, du, dhres1, yn, g_mix, w_in), sem=("arbitrary",))


def _place():
    x, y, c = lax.axis_index("x"), lax.axis_index("y"), lax.axis_index("c")
    return x, y, c


def _other_chips(x, y):
    return [(1 - x, y), (x, 1 - y), (1 - x, 1 - y)]


ANY = pl.BlockSpec(memory_space=pl.ANY)
VMEM_SPEC = pl.BlockSpec(memory_space=pltpu.VMEM)

_GATHERED = {"w_in": "cols", "w_out": "major", "ffn_w1": "major", "ffn_w3": "major", "ffn_w2": "major"}
_BIG = ("w_in", "w_out", "ffn_w1", "ffn_w3", "ffn_w2")


def _gather_weights(shards, conv_w, n_remote):
    n = len(shards)
    full_shapes = []
    for name, sh in zip(_BIG, shards):
        r, cdim = sh.shape
        if _GATHERED[name] == "cols":
            assert cdim % LANES == 0
            full_shapes.append((r, cdim * N_CHIPS))
        else:
            full_shapes.append((N_CHIPS, r, cdim))

    def region(ref, name, sh, jj, cc):
        r, cdim = sh
        rows = pl.ds(0, r) if cc is None else pl.ds(pl.multiple_of(cc * (r // 2), 16), r // 2)
        if _GATHERED[name] == "cols":
            return ref.at[rows, pl.ds(pl.multiple_of(jj * cdim, LANES), cdim)]
        return ref.at[jj, rows, :]

    def staged(ref, sh, cc):
        r = sh[0]
        return ref.at[pl.ds(pl.multiple_of(cc * (r // 2), 16), r // 2), :]

    def body(*refs):
        ins, cw_in = refs[:n], refs[n]
        outs, cw_out = refs[n + 1:2 * n + 1], refs[2 * n + 1]
        stage = refs[2 * n + 2:3 * n + 2]
        cw_stage, lsem, ssem, rsem, fssem, frsem, cssem, crsem = refs[3 * n + 2:]
        x, y, c = _place()
        j = 2 * x + y
        chips = _other_chips(x, y)
        for w in range(n):
            stage[w][...] = ins[w][...].astype(BF16)
        cw_stage[...] = jnp.zeros_like(cw_stage)
        cw_stage[0:CONV_WIDTH, :] = cw_in[...]
        shs = [s_.shape for s_ in shards]
        local = [pltpu.make_async_copy(stage[w], region(outs[w], _BIG[w], shs[w], j, None), lsem.at[w])
                 for w in range(n)]
        local.append(pltpu.make_async_copy(cw_stage, cw_out.at[j], lsem.at[n]))
        for cp in local:
            cp.start()
        sends = []
        for k, (px, py) in enumerate(chips):
            for w in range(n_remote):
                sends.append(pltpu.make_async_remote_copy(
                    src_ref=staged(stage[w], shs[w], c), dst_ref=region(outs[w], _BIG[w], shs[w], j, c),
                    send_sem=ssem.at[k * n + w], recv_sem=rsem.at[k * n + w], device_id=(px, py, c),
                    device_id_type=MESH))
            sends.append(pltpu.make_async_remote_copy(
                src_ref=cw_stage, dst_ref=cw_out.at[j], send_sem=cssem.at[k], recv_sem=crsem.at[k],
                device_id=(px, py, c), device_id_type=MESH))
        for cp in sends:
            cp.start()
        fwd = []
        for k, (px, py) in enumerate(chips):
            jk = 2 * px + py
            for w in range(n_remote):
                reg = region(outs[w], _BIG[w], shs[w], jk, c)
                pltpu.make_async_remote_copy(src_ref=reg, dst_ref=reg, send_sem=ssem.at[k * n + w],
                                             recv_sem=rsem.at[k * n + w], device_id=(px, py, c),
                                             device_id_type=MESH).wait_recv()
                cp = pltpu.make_async_remote_copy(src_ref=reg, dst_ref=reg, send_sem=fssem.at[k * n + w],
                                                  recv_sem=frsem.at[k * n + w], device_id=(x, y, 1 - c),
                                                  device_id_type=MESH)
                cp.start()
                fwd.append(cp)
            pltpu.make_async_remote_copy(src_ref=cw_stage, dst_ref=cw_out.at[jk], send_sem=cssem.at[k],
                                         recv_sem=crsem.at[k], device_id=(px, py, c),
                                         device_id_type=MESH).wait_recv()
        for k, (px, py) in enumerate(chips):
            jk = 2 * px + py
            for w in range(n_remote):
                reg = region(outs[w], _BIG[w], shs[w], jk, 1 - c)
                pltpu.make_async_remote_copy(src_ref=reg, dst_ref=reg, send_sem=fssem.at[k * n + w],
                                             recv_sem=frsem.at[k * n + w], device_id=(x, y, 1 - c),
                                             device_id_type=MESH).wait_recv()
        for cp in sends + fwd:
            cp.wait_send()
        for cp in local:
            cp.wait()

    nsem = 3 * n
    return pl.pallas_call(
        body, name="gather_first",
        in_specs=[VMEM_SPEC] * (n + 1), out_specs=[ANY] * (n + 1),
        out_shape=[SDS(fs, BF16) for fs in full_shapes] + [SDS((N_CHIPS, SUBLANES, LANES), F32)],
        scratch_shapes=[pltpu.VMEM(s_.shape, BF16) for s_ in shards] + [pltpu.VMEM((SUBLANES, LANES), F32)]
        + [pltpu.SemaphoreType.DMA((n + 1,))] + [pltpu.SemaphoreType.DMA((nsem,))] * 4
        + [pltpu.SemaphoreType.DMA((3,))] * 2,
        compiler_params=_cp())(*shards, conv_w)


def _start_all(make):
    def f(ins, outs, sems):
        for cp in make(ins, outs, sems):
            cp.start()
    return f


def _wait_all(make):
    def f(ins, outs, sems):
        for cp in make(ins, outs, sems):
            cp.wait()
    return f


def _ffn_gather_hosted(arrs):
    n = len(arrs)

    def make(outs, sems):
        ssem, rsem, fs, fr = sems
        x, y, c = _place()
        j = 2 * x + y

        def reg(w, jj, cc):
            hr = arrs[w].shape[1] // 2
            return outs[w].at[jj, pl.ds(pl.multiple_of(cc * hr, 16), hr), :]

        def rc(w, jj, cc, s_sem, r_sem, dev):
            return pltpu.make_async_remote_copy(src_ref=reg(w, jj, cc), dst_ref=reg(w, jj, cc), send_sem=s_sem,
                                                recv_sem=r_sem, device_id=dev, device_id_type=MESH)

        sends, recvs, fwds, frecvs = [], [], [], []
        for k, (px, py) in enumerate(_other_chips(x, y)):
            jk = 2 * px + py
            for w in range(n):
                q = k * n + w
                sends.append(rc(w, j, c, ssem.at[q], rsem.at[q], (px, py, c)))
                recvs.append(rc(w, jk, c, ssem.at[q], rsem.at[q], (px, py, c)))
                fwds.append(rc(w, jk, c, fs.at[q], fr.at[q], (x, y, 1 - c)))
                frecvs.append(rc(w, jk, 1 - c, fs.at[q], fr.at[q], (x, y, 1 - c)))
        return sends, recvs, fwds, frecvs

    def start(ins, outs, sems):
        for cp in make(outs, sems)[0]:
            cp.start()

    def mid(ins, outs, sems):
        _, recvs, fwds, _ = make(outs, sems)
        for r, f in zip(recvs, fwds):
            r.wait_recv()
            f.start()

    def finish(ins, outs, sems):
        sends, _, fwds, frecvs = make(outs, sems)
        for r in frecvs:
            r.wait_recv()
        for cp in sends + fwds:
            cp.wait_send()

    return _Hosted(arrs, [SDS(a.shape, a.dtype) for a in arrs], [3 * n] * 4, start, finish, mid=mid,
                   aliases={w: w for w in range(n)})


def _rs_sibling_hosted(arrs):
    n = len(arrs)

    def make(ins, outs, sems):
        x, y, c = _place()
        cps = []
        for w in range(n):
            hr = arrs[w].shape[1] // 2
            src = ins[w].at[:, pl.ds(pl.multiple_of((1 - c) * hr, SUBLANES), hr), :]
            cps.append(pltpu.make_async_remote_copy(src_ref=src, dst_ref=outs[w], send_sem=sems[0].at[w],
                                                    recv_sem=sems[1].at[w], device_id=(x, y, 1 - c),
                                                    device_id_type=MESH))
        return cps

    return _Hosted(arrs, [SDS((a.shape[0], a.shape[1] // 2, a.shape[2]), F32) for a in arrs], [n, n],
                   _start_all(make), _wait_all(make))


def _rs_chips_hosted(parts):
    n = len(parts)

    def make(ins, outs, sems):
        x, y, c = _place()
        j = 2 * x + y
        cps = []
        for k, (px, py) in enumerate(_other_chips(x, y)):
            jk = 2 * px + py
            for w in range(n):
                cps.append(pltpu.make_async_remote_copy(
                    src_ref=ins[w].at[jk], dst_ref=outs[w].at[j], send_sem=sems[0].at[k * n + w],
                    recv_sem=sems[1].at[k * n + w], device_id=(px, py, c), device_id_type=MESH))
        return cps

    return _Hosted(parts, [SDS(p.shape, p.dtype) for p in parts], [3 * n, 3 * n], _start_all(make), _wait_all(make))


def _rs_swap_hosted(halves):
    n = len(halves)

    def make(ins, outs, sems):
        x, y, c = _place()
        return [pltpu.make_async_remote_copy(src_ref=ins[w], dst_ref=outs[w], send_sem=sems[0].at[w],
                                             recv_sem=sems[1].at[w], device_id=(x, y, 1 - c), device_id_type=MESH)
                for w in range(n)]

    return _Hosted(halves, [SDS(h.shape, F32) for h in halves], [n, n], _start_all(make), _wait_all(make))


def _run_comm(hosted, name):
    return _call(lambda: None, hosted, None, name=name, grid=(), in_specs=[], out_specs=[], out_shape=[],
                 scratch_shapes=[], args=(), sem=None)[1]


def _row_tile(rows, cols, n_arrays):
    budget = 24 * 1024 * 1024 // (2 * 4 * n_arrays * cols)
    best = SUBLANES
    for t in range(SUBLANES, rows + 1, SUBLANES):
        if rows % t == 0 and t <= budget:
            best = t
    return best


def _place_index(which):
    x, y, c = _place()
    v = c if which == "c" else 2 * x + y
    return jnp.reshape(v, (1,)).astype(jnp.int32)


def _add_own_half(full, recv, name):
    nsh, rows, cols = full.shape
    hr = rows // 2
    t = _row_tile(hr, cols, 4)
    nt = hr // t

    def body(c_ref, a_ref, b_ref, o_ref, ob_ref):
        v = a_ref[...] + b_ref[...]
        o_ref[...] = v
        ob_ref[...] = v.astype(BF16)

    half = pl.BlockSpec((1, t, cols), lambda s_, i, c_ref: (s_, i, 0))
    return pl.pallas_call(
        body, name=name,
        grid_spec=pltpu.PrefetchScalarGridSpec(
            num_scalar_prefetch=1, grid=(nsh, nt),
            in_specs=[pl.BlockSpec((1, t, cols), lambda s_, i, c_ref: (s_, c_ref[0] * nt + i, 0)), half],
            out_specs=[half, half]),
        out_shape=[SDS((nsh, hr, cols), F32), SDS((nsh, hr, cols), BF16)],
        compiler_params=_cp(("parallel", "parallel")))(_place_index("c"), full, recv)


def _sum_chips(own, recv, name):
    nsh, hr, cols = own.shape
    t = _row_tile(hr, cols, 6)

    def body(j_ref, own_ref, *rest):
        r_refs, o_ref = rest[:nsh], rest[nsh]
        j = j_ref[0]
        mine = own_ref[0]
        parts = [jnp.where(j == k, mine, r_refs[k][0].astype(F32)) for k in range(nsh)]
        o_ref[...] = ((parts[0] + parts[1]) + parts[2]) + parts[3]

    def other(k):
        return pl.BlockSpec((1, t, cols), lambda i, j_ref: (jnp.where(j_ref[0] == k, (k + 1) % nsh, k), i, 0))

    return pl.pallas_call(
        body, name=name,
        grid_spec=pltpu.PrefetchScalarGridSpec(
            num_scalar_prefetch=1, grid=(hr // t,),
            in_specs=[pl.BlockSpec((1, t, cols), lambda i, j_ref: (j_ref[0], i, 0))]
            + [other(k) for k in range(nsh)],
            out_specs=pl.BlockSpec((t, cols), lambda i, j_ref: (i, 0))),
        out_shape=SDS((hr, cols), F32), compiler_params=_cp(("parallel",)))(_place_index("j"), own, *([recv] * nsh))


def _adamw_math(w, g, m, v):
    m = ADAM_B1 * m + (1.0 - ADAM_B1) * g
    v = ADAM_B2 * v + (1.0 - ADAM_B2) * (g * g)
    m_hat = m / (1.0 - ADAM_B1 ** ADAM_STEP)
    v_hat = v / (1.0 - ADAM_B2 ** ADAM_STEP)
    delta = -ADAM_LR * (m_hat / (jnp.sqrt(v_hat) + ADAM_EPS) + ADAM_WD * w)
    return delta, m, v


def _adamw_big(w, g_own, g_sib, m, v, name):
    _, rows, cols = w.shape
    hr = rows // 2
    t = _row_tile(hr, cols, 9)
    nth = hr // t

    def body(c_ref, w_ref, go_ref, gs_ref, m_ref, v_ref, g_ref, d_ref, mo_ref, vo_ref):
        own = (pl.program_id(0) // nth) == c_ref[0]
        g = jnp.where(own, go_ref[...], gs_ref[...])
        g_ref[0] = g
        d_ref[0], mo_ref[0], vo_ref[0] = _adamw_math(w_ref[0], g, m_ref[0], v_ref[0])

    spec = pl.BlockSpec((1, t, cols), lambda i, c_ref: (0, i, 0))
    hspec = pl.BlockSpec((t, cols), lambda i, c_ref: (i % nth, 0))
    return pl.pallas_call(
        body, name=name,
        grid_spec=pltpu.PrefetchScalarGridSpec(
            num_scalar_prefetch=1, grid=(2 * nth,), in_specs=[spec, hspec, hspec, spec, spec],
            out_specs=[spec] * 4),
        out_shape=[SDS((1, rows, cols), F32)] * 4,
        compiler_params=_cp(("parallel",)))(_place_index("c"), w, g_own, g_sib, m, v)


def _allreduce_small(mix_slab, dg_mix, dg_ffn, dg_fin, loss8):
    half = SLAB_ROWS // 2

    def body(ms_ref, gm_ref, gf_ref, gn_ref, loss_ref, out_ref, loc_s, sib_s, chip_s, r2_s, fin_s, sems):
        x, y, c = _place()
        j = 2 * x + y
        rows = []
        for ref in (gm_ref, gf_ref, gn_ref):
            v = jnp.sum(ref[...], axis=0, keepdims=True)
            rows += [v[:, :SLAB_W], v[:, SLAB_W:]]
        rows.append(jnp.concatenate([loss_ref[0:1, :]] * (SLAB_W // LANES), axis=1))
        rows.append(jnp.zeros((SLAB_ROWS - ROW_LOSS - 1, SLAB_W), F32))
        loc_s[0:MIX_SLAB_ROWS, :] = ms_ref[...]
        loc_s[MIX_SLAB_ROWS:SLAB_ROWS, :] = jnp.concatenate(rows, axis=0)
        sib = (x, y, 1 - c)
        cp = pltpu.make_async_remote_copy(src_ref=loc_s, dst_ref=sib_s, send_sem=sems.at[0], recv_sem=sems.at[1],
                                          device_id=sib, device_id_type=MESH)
        cp.start()
        cp.wait()
        chip_s[...] = loc_s[...] + sib_s[...]
        mine = chip_s.at[pl.ds(pl.multiple_of(c * half, SUBLANES), half), :]
        r2_s[j] = chip_s[pl.ds(pl.multiple_of(c * half, SUBLANES), half), :]
        cps = []
        for k, (px, py) in enumerate(_other_chips(x, y)):
            cps.append(pltpu.make_async_remote_copy(src_ref=mine, dst_ref=r2_s.at[j], send_sem=sems.at[2 + k],
                                                    recv_sem=sems.at[5 + k], device_id=(px, py, c),
                                                    device_id_type=MESH))
        for cp in cps:
            cp.start()
        for cp in cps:
            cp.wait()
        fin_s[...] = ((r2_s[0] + r2_s[1]) + r2_s[2]) + r2_s[3]
        dst = out_ref.at[pl.ds(pl.multiple_of(c * half, SUBLANES), half), :]
        out_ref[pl.ds(pl.multiple_of(c * half, SUBLANES), half), :] = fin_s[...]
        cp = pltpu.make_async_remote_copy(src_ref=fin_s, dst_ref=dst, send_sem=sems.at[8], recv_sem=sems.at[9],
                                          device_id=sib, device_id_type=MESH)
        cp.start()
        cp.wait()

    return pl.pallas_call(
        body, name="allreduce_small", in_specs=[VMEM_SPEC] * 5, out_specs=VMEM_SPEC,
        out_shape=SDS((SLAB_ROWS, SLAB_W), F32),
        scratch_shapes=[pltpu.VMEM((SLAB_ROWS, SLAB_W), F32)] * 3 + [pltpu.VMEM((N_CHIPS, half, SLAB_W), F32),
                                                                       pltpu.VMEM((half, SLAB_W), F32),
                                                                       pltpu.SemaphoreType.DMA((10,))],
        compiler_params=_cp())(mix_slab, dg_mix, dg_ffn, dg_fin, loss8)


_SMALL_ROWS = (("conv_b", ROW_CONV_B), ("gate_a_b", ROW_BA), ("gate_x_b", ROW_BX), ("lru_lambda", ROW_LAM),
               ("pool_b", ROW_PB), ("pool_scale", ROW_PS), ("norm_lru_g", ROW_GL), ("norm_pool_g", ROW_GP))
_WIDE_ROWS = (("norm_mix_g", ROW_MIX), ("norm_ffn_g", ROW_FFN), ("final_norm_g", ROW_FIN))
_BLOCK_ROWS = (("gate_a_w", ROW_GA), ("gate_x_w", ROW_GX), ("pool_w", ROW_PW))
_SMALL_ORDER = tuple(n for n, _ in _SMALL_ROWS) + tuple(n for n, _ in _WIDE_ROWS) + tuple(
    n for n, _ in _BLOCK_ROWS) + ("conv_w",)


def _adamw_small(slab, wmv):
    names = _SMALL_ORDER
    flat = [a for nme in names for a in wmv[nme]]
    nin = len(flat)

    def body(*refs):
        slab_ref, j_ref = refs[0], refs[1]
        ins = refs[2:2 + nin]
        outs = refs[2 + nin:]
        grads = {}
        for nme, row in _SMALL_ROWS:
            grads[nme] = slab_ref[row:row + 1, :]
        for nme, row in _WIDE_ROWS:
            grads[nme] = jnp.concatenate([slab_ref[row:row + 1, :], slab_ref[row + 1:row + 2, :]], axis=1)
        for nme, row in _BLOCK_ROWS:
            grads[nme] = slab_ref[row:row + wmv[nme][0].shape[0], :]
        full = slab_ref[ROW_CONV_W:ROW_CONV_W + CONV_WIDTH, :]
        jv = j_ref[0]
        g = jnp.zeros((CONV_WIDTH, LANES), F32)
        for jj in range(N_CHIPS):
            g = jnp.where(jv == jj, full[:, jj * LANES:(jj + 1) * LANES], g)
        grads["conv_w"] = g
        for idx, nme in enumerate(names):
            w_ref, m_ref, v_ref = ins[3 * idx:3 * idx + 3]
            g = grads[nme]
            delta, m, v = _adamw_math(w_ref[...], g, m_ref[...], v_ref[...])
            outs[4 * idx][...] = g
            outs[4 * idx + 1][...] = delta
            outs[4 * idx + 2][...] = m
            outs[4 * idx + 3][...] = v

    x, y, _ = _place()
    jidx = jnp.reshape(2 * x + y, (1,)).astype(jnp.int32)
    out_shape = [SDS(wmv[nme][0].shape, F32) for nme in names for _ in range(4)]
    res = pl.pallas_call(
        body, name="adamw_small",
        in_specs=[VMEM_SPEC, pl.BlockSpec(memory_space=pltpu.SMEM)] + [VMEM_SPEC] * nin,
        out_specs=[VMEM_SPEC] * len(out_shape), out_shape=out_shape, compiler_params=_cp())(slab, jidx, *flat)
    return {nme: tuple(res[4 * idx:4 * idx + 4]) for idx, nme in enumerate(names)}


_FFN = ("ffn_w1", "ffn_w3", "ffn_w2")


def _local_step(x, target, full, sp_, distributed):
    u = _inproj(x, sp_["norm_mix_g"], full["w_in"])
    gather = [_ffn_gather_hosted([full[n] for n in _FFN])] if distributed else None
    (h, yn, hres1), got = _mixer_fwd(u, x, sp_, full["w_out"], gather)
    w1, w3, w2 = got[0] if distributed else [full[n] for n in _FFN]
    h2, a1, a3, ff = _ffn_up(hres1, sp_["norm_ffn_g"], w1, w3)
    dh, dhb, loss8, dg_fin = _ffn_down(ff, hres1, target, sp_["final_norm_g"], w2)
    da1, da3 = _ffn_bwd_gate(dhb, a1, a3, w2)
    dws = list(_ffn_wgrad2(h2, dhb, ff, da1, da3))
    rs1 = [_rs_sibling_hosted(dws)] if distributed else None
    (dhres1, dg_ffn), got = _ffn_bwd_down(da1, da3, dh, hres1, sp_["norm_ffn_g"], w1, w3, rs1)
    rs2 = None
    if distributed:
        pairs = [_add_own_half(a, r, "add_half_" + n) for n, a, r in zip(_FFN, dws, got[0])]
        rs2 = [_rs_chips_hosted([pb for _, pb in pairs])]
    (du, mix_slab), got = _mixer_bwd(u, h, dhres1, sp_, full["w_out"], rs2)
    (gx, dwin, dwout, dg_mix), _ = _inproj_bwd(x, du, dhres1, yn, sp_["norm_mix_g"], full["w_in"])
    d = x.shape[1]
    big = {"w_in": dwin, "w_out": dwout.reshape(N_CHIPS, d // N_CHIPS, d)}
    for k, n in enumerate(_FFN):
        big[n] = (pairs[k][0], got[0][k]) if distributed else dws[k]
    return gx, big, (mix_slab, dg_mix, dg_ffn, dg_fin, loss8)


def _to_compact(w):
    h, i, j = w.shape
    return jnp.transpose(w, (1, 0, 2)).reshape(i, h * j)


def _from_compact(w, h):
    i, hj = w.shape
    return jnp.transpose(w.reshape(i, h, hj // h), (1, 0, 2))


_SMALL_LAYOUT = {
    "gate_a_w": (lambda a: _to_compact(a[0]), lambda a: _from_compact(a, 8)[None]),
    "gate_x_w": (lambda a: _to_compact(a[0]), lambda a: _from_compact(a, 8)[None]),
    "pool_w": (lambda a: _to_compact(a[0]), lambda a: _from_compact(a, 4)[None]),
    "conv_w": (lambda a: a[0], lambda a: a[None]),
    "final_norm_g": (lambda a: a[None], lambda a: a[0]),
}

_WEIGHTS = ("norm_mix_g", "w_in", "conv_w", "conv_b", "gate_a_w", "gate_a_b", "gate_x_w", "gate_x_b", "lru_lambda",
            "pool_w", "pool_b", "pool_scale", "norm_lru_g", "norm_pool_g", "w_out", "norm_ffn_g", "ffn_w1",
            "ffn_w3", "ffn_w2", "final_norm_g")


def kernel(x, norm_mix_g, w_in, conv_w, conv_b, gate_a_w, gate_a_b, gate_x_w, gate_x_b, lru_lambda, pool_w, pool_b, pool_scale, norm_lru_g, norm_pool_g, w_out, norm_ffn_g, ffn_w1, ffn_w3, ffn_w2, final_norm_g, loss_target, m_norm_mix_g, m_w_in, m_conv_w, m_conv_b, m_gate_a_w, m_gate_a_b, m_gate_x_w, m_gate_x_b, m_lru_lambda, m_pool_w, m_pool_b, m_pool_scale, m_norm_lru_g, m_norm_pool_g, m_w_out, m_norm_ffn_g, m_ffn_w1, m_ffn_w3, m_ffn_w2, m_final_norm_g, v_norm_mix_g, v_w_in, v_conv_w, v_conv_b, v_gate_a_w, v_gate_a_b, v_gate_x_w, v_gate_x_b, v_lru_lambda, v_pool_w, v_pool_b, v_pool_scale, v_norm_lru_g, v_norm_pool_g, v_w_out, v_norm_ffn_g, v_ffn_w1, v_ffn_w3, v_ffn_w2, v_final_norm_g):
    loc = locals()
    w = {n: loc[n] for n in _WEIGHTS}
    m = {n: loc["m_" + n] for n in _WEIGHTS}
    v = {n: loc["v_" + n] for n in _WEIGHTS}

    def lay(nme, a):
        return _SMALL_LAYOUT[nme][0](a) if nme in _SMALL_LAYOUT else a

    def unlay(nme, a):
        return _SMALL_LAYOUT[nme][1](a) if nme in _SMALL_LAYOUT else a

    gathered = _gather_weights([w[n][0] for n in _BIG], w["conv_w"][0], n_remote=2)
    full = dict(zip(_BIG, gathered[:-1]))
    full["w_out"] = full["w_out"].reshape(w_out.shape[2], w_out.shape[2])
    cw_all = gathered[-1]
    sp_ = {n: lay(n, w[n]) for n in _SMALL_ORDER}
    sp_["conv_w"] = jnp.transpose(cw_all[:, :CONV_WIDTH, :], (1, 0, 2)).reshape(CONV_WIDTH, N_CHIPS * LANES)

    gx, big, small = _local_step(x[0], loss_target[0], full, sp_, distributed=True)

    late = ("w_in", "w_out")
    fin = {n: _sum_chips(big[n][0], big[n][1], "sum_chips_" + n) for n in _FFN}
    recv1, swapped = _run_comm([_rs_sibling_hosted([big[n] for n in late]),
                                _rs_swap_hosted([fin[n] for n in _FFN])], "tail_sibling")
    sib = dict(zip(_FFN, swapped))
    pairs = [_add_own_half(big[n], r, "add_half_" + n) for n, r in zip(late, recv1)]
    recv2, = _run_comm([_rs_chips_hosted([pb for _, pb in pairs])], "tail_chips")
    for n, (p, _), r in zip(late, pairs, recv2):
        fin[n] = _sum_chips(p, r, "sum_chips_" + n)
    swapped, = _run_comm([_rs_swap_hosted([fin[n] for n in late])], "tail_swap")
    sib.update(zip(late, swapped))
    out = {}
    for n in _BIG:
        out[n] = tuple(_adamw_big(w[n], fin[n], sib[n], m[n], v[n], "adamw_" + n))
    slab = _allreduce_small(*small)
    loss = slab[ROW_LOSS, 0]
    wmv = {n: (lay(n, w[n]), lay(n, m[n]), lay(n, v[n])) for n in _SMALL_ORDER}
    res = _adamw_small(slab, wmv)
    for n in _SMALL_ORDER:
        out[n] = tuple(unlay(n, a) for a in res[n])
    return (loss, gx[None]) + tuple(out[n][k] for k in range(4) for n in _WEIGHTS)
```

```python
import functools
import math

import jax
import jax.numpy as jnp
from jax import lax
from jax.experimental import pallas as pl
from jax.experimental.pallas import tpu as pltpu

F32 = jnp.float32
BF16 = jnp.bfloat16
SDS = jax.ShapeDtypeStruct
MESH = pl.DeviceIdType.MESH

EPS = 1e-6
LRU_C = 8.0
CONV_WIDTH = 4
POOL_WINDOWS = (2, 4, 8, 16)
HALO = 16
LANES = 128
SUBLANES = 8
GATE_BLOCK = 256
N_CHIPS = 4

ADAM_LR = 0.001
ADAM_B1 = 0.9
ADAM_B2 = 0.999
ADAM_EPS = 1e-08
ADAM_WD = 0.01
ADAM_STEP = 10

TM_PROJ = 512
TM_MIX = 256
TM_FFN = 512
TM_WGRAD = 1024
TM_FFN_UP = 1024
TM_FFN_DOWN = 512
MIX_SAVED = ("xc", "r", "ig", "a", "m2raw", "ge", "dge")
FFN_ROW_CHUNKS = 2
VMEM_LIMIT = 56 * 1024 * 1024

SLAB_W = 512
ROW_CONV_B, ROW_CONV_W, ROW_BA, ROW_BX, ROW_LAM, ROW_PB, ROW_PS, ROW_GL, ROW_GP = 0, 1, 5, 6, 7, 8, 9, 10, 11
ROW_GA, ROW_GX, ROW_PW = 16, 80, 144
ROW_MIX, ROW_FFN, ROW_FIN, ROW_LOSS = 272, 274, 276, 278
MIX_SLAB_ROWS = 272
SLAB_ROWS = 288


def _cp(sem=None, **kw):
    if sem is not None:
        kw["dimension_semantics"] = sem
    return pltpu.CompilerParams(vmem_limit_bytes=VMEM_LIMIT, **kw)


def _const_spec(shape):
    nd = len(shape)
    return pl.BlockSpec(shape, lambda *_: (0,) * nd, pipeline_mode=pl.Buffered(1))


def _sigmoid(x):
    return 1.0 / (1.0 + jnp.exp(-x))


def _dot(a, b):
    return jnp.dot(a, b, preferred_element_type=F32)


def _dot_nt(a, b):
    return lax.dot_general(a, b, (((1,), (1,)), ((), ())), preferred_element_type=F32)


def _dot_tn(a, b):
    return lax.dot_general(a, b, (((0,), (0,)), ((), ())), preferred_element_type=F32)


def _colsum8(v):
    m, c = v.shape
    return v.reshape(m // SUBLANES, SUBLANES, c).sum(axis=0)


def _rowmean(v):
    return jnp.mean(v, axis=-1, keepdims=True)


def _rms_bwd(dy, xhat, r, g):
    dxh = dy * g
    return r * (dxh - xhat * _rowmean(dxh * xhat))


def _softplus_neg(lam):
    z = -lam
    e = jnp.exp(-jnp.abs(z))
    u = 1.0 + e
    d = u - 1.0
    log1p = jnp.where(d == 0.0, e, jnp.log(u) * (e / jnp.where(d == 0.0, 1.0, d)))
    return jnp.maximum(z, 0.0) + log1p


def _neg_expm1(z):
    series = -(z * (1.0 + z * (0.5 + z * (1.0 / 6.0 + z * (1.0 / 24.0)))))
    return jnp.where(z > -0.03, series, 1.0 - jnp.exp(z))


_GELU_C = math.sqrt(2.0 / math.pi)
_GELU_K = 0.044715


def _gelu_parts(x):
    x2 = x * x
    th = jnp.tanh(_GELU_C * (x + _GELU_K * x2 * x))
    ge = 0.5 * x * (1.0 + th)
    dge = 0.5 * (1.0 + th) + 0.5 * x * (1.0 - th * th) * (_GELU_C * (1.0 + 3.0 * _GELU_K * x2))
    return ge, dge


def _shift_down(halo, tile, k):
    if k == 0:
        return tile
    ext = jnp.concatenate([halo, tile], axis=0)
    n = tile.shape[0]
    h = halo.shape[0]
    return ext[h - k:h - k + n]


def _shift_up(tile, nxt, k):
    if k == 0:
        return tile
    ext = jnp.concatenate([tile, nxt], axis=0)
    return ext[k:k + tile.shape[0]]


def _build_gate_blocks(ga_ref, gx_ref, gw_ref):
    hd = ga_ref.shape[0]
    per = GATE_BLOCK // hd
    lane = lax.broadcasted_iota(jnp.int32, (hd, GATE_BLOCK), 1)
    for b in range(gw_ref.shape[0]):
        for src, off in ((ga_ref, 0), (gx_ref, GATE_BLOCK)):
            blk = src[:, b * GATE_BLOCK:(b + 1) * GATE_BLOCK]
            for hh in range(per):
                m = (lane >= hh * hd) & (lane < (hh + 1) * hd)
                gw_ref[b, hh * hd:(hh + 1) * hd, off:off + GATE_BLOCK] = jnp.where(m, blk, 0.0).astype(BF16)


def _scan_level1(a, b, reverse):
    m, c = a.shape
    a3 = a.reshape(m // SUBLANES, SUBLANES, c)
    b3 = b.reshape(m // SUBLANES, SUBLANES, c)
    row = lax.broadcasted_iota(jnp.int32, a3.shape, 1)
    for s in (1, 2, 4):
        sh = (SUBLANES - s) if reverse else s
        a_sh = pltpu.roll(a3, sh, 1)
        b_sh = pltpu.roll(b3, sh, 1)
        ok = (row < SUBLANES - s) if reverse else (row >= s)
        b3 = jnp.where(ok, a3 * b_sh + b3, b3)
        a3 = jnp.where(ok, a3 * a_sh, a3)
    return a3.reshape(m, c), b3.reshape(m, c)


def _scan_level2(a_ref, b_ref, out_ref, carry, reverse):
    m, c = a_ref.shape
    ng = m // SUBLANES

    def step(g, cr):
        gi = (ng - 1 - g) if reverse else g
        off = pl.multiple_of(gi * SUBLANES, SUBLANES)
        h = b_ref[pl.ds(off, SUBLANES), :] + a_ref[pl.ds(off, SUBLANES), :] * cr
        out_ref[pl.ds(off, SUBLANES), :] = h
        edge = h[0:1, :] if reverse else h[SUBLANES - 1:SUBLANES, :]
        return jnp.broadcast_to(edge, (SUBLANES, c))

    return lax.fori_loop(0, ng, step, carry, unroll=4)


def _mixer_recompute(u, hal, t0, cw, cb, gw_ref, ba, bx, lam, pw_ref, pb, ps):
    tm = u.shape[0]
    lw = cb.shape[1]
    u_l, u_g, u_p = u[:, :lw], u[:, lw:2 * lw], u[:, 2 * lw:]
    hal_l, hal_p = hal[:, :lw], hal[:, 2 * lw:]
    taps = [_shift_down(hal_l, u_l, CONV_WIDTH - 1 - k) for k in range(CONV_WIDTH)]
    xc = cb
    for k in range(CONV_WIDTH):
        xc = xc + taps[k] * cw[k:k + 1, :]
    xcb = xc.astype(BF16)
    nb = lw // GATE_BLOCK
    gs = [_dot(xcb[:, b * GATE_BLOCK:(b + 1) * GATE_BLOCK], gw_ref[b]) for b in range(nb)]
    r = _sigmoid(jnp.concatenate([g[:, :GATE_BLOCK] for g in gs], axis=1) + ba)
    ig = _sigmoid(jnp.concatenate([g[:, GATE_BLOCK:] for g in gs], axis=1) + bx)
    sp = _softplus_neg(lam)
    la = (-LRU_C * r) * sp
    a = jnp.exp(la)
    m2raw = _neg_expm1(2.0 * la)
    mult = jnp.sqrt(jnp.maximum(m2raw, 1e-12))
    ge, dge = _gelu_parts(u_g)
    row = lax.broadcasted_iota(jnp.int32, (tm, LANES), 0) + t0
    pooled, invs, zs = [], [], []
    for gi, w in enumerate(POOL_WINDOWS):
        e = jnp.concatenate([hal_p[:, gi * LANES:(gi + 1) * LANES], u_p[:, gi * LANES:(gi + 1) * LANES]], axis=0)
        s = e
        k = 1
        while k < w:
            s = s + pltpu.roll(s, k, 0)
            k *= 2
        inv = 1.0 / jnp.minimum(row + 1, w).astype(F32)
        pg = s[HALO:] * inv - e[HALO:]
        pooled.append(pg)
        invs.append(inv)
        zs.append(_dot(pg.astype(BF16), pw_ref[:, gi * LANES:(gi + 1) * LANES].astype(BF16)))
    z = jnp.concatenate(zs, axis=1) + pb
    y_pool = z * ps
    return dict(u_l=u_l, u_g=u_g, taps=taps, xc=xc, xcb=xcb, r=r, ig=ig, sp=sp, la=la, a=a, m2raw=m2raw,
                mult=mult, ge=ge, dge=dge, pooled=pooled, invs=invs, z=z, y_pool=y_pool)


ANY = pl.BlockSpec(memory_space=pl.ANY)
VMEM_SPEC = pl.BlockSpec(memory_space=pltpu.VMEM)


class _Hosted:
    def __init__(self, ins, out_shapes, sems, start, finish, mid=None, aliases=None):
        self.ins, self.out_shapes, self.sems = list(ins), list(out_shapes), list(sems)
        self.start, self.mid, self.finish = start, mid, finish
        self.aliases = dict(aliases or {})


def _call(body, hosted, stage_preds, *, name, grid, in_specs, out_specs, out_shape, scratch_shapes, args, sem):
    hosted = list(hosted or [])
    n_in, n_out, n_scr = len(in_specs), len(out_specs), len(scratch_shapes)
    c_in = [a for h in hosted for a in h.ins]
    c_out = [o for h in hosted for o in h.out_shapes]
    c_sem = [pltpu.SemaphoreType.DMA((k,)) for h in hosted for k in h.sems]

    def full(*refs):
        p = 0
        parts = []
        for cnt in (n_in, len(c_in), n_out, len(c_out), n_scr, len(c_sem)):
            parts.append(refs[p:p + cnt])
            p += cnt
        hi, ci, ho, co, hs, cs = parts
        per = []
        a = b = c_ = 0
        for h in hosted:
            per.append((h, ci[a:a + len(h.ins)], co[b:b + len(h.out_shapes)], cs[c_:c_ + len(h.sems)]))
            a, b, c_ = a + len(h.ins), b + len(h.out_shapes), c_ + len(h.sems)
        first = mid = last = None
        if hosted and grid:
            first, mid, last = stage_preds()

        def run(fn, pred, i_, o_, s_):
            if fn is None:
                return
            if pred is None:
                fn(i_, o_, s_)
            else:
                pl.when(pred)(functools.partial(fn, i_, o_, s_))

        for h, i_, o_, s_ in per:
            run(h.start, first, i_, o_, s_)
        body(*hi, *ho, *hs)
        for h, i_, o_, s_ in per:
            run(h.mid, mid, i_, o_, s_)
        for h, i_, o_, s_ in per:
            run(h.finish, last, i_, o_, s_)

    aliases = {}
    a = b = 0
    for h in hosted:
        for k, v in h.aliases.items():
            aliases[n_in + a + k] = n_out + b + v
        a, b = a + len(h.ins), b + len(h.out_shapes)
    res = pl.pallas_call(
        full, name=name, grid=grid, in_specs=list(in_specs) + [ANY] * len(c_in),
        out_specs=list(out_specs) + [ANY] * len(c_out), out_shape=list(out_shape) + c_out,
        scratch_shapes=list(scratch_shapes) + c_sem, input_output_aliases=aliases,
        compiler_params=_cp(sem))(*args, *c_in)
    res = list(res)
    outs = []
    p = n_out
    for h in hosted:
        outs.append(res[p:p + len(h.out_shapes)])
        p += len(h.out_shapes)
    return res[:n_out], outs


def _inproj(x, g_mix, w_in, hosted=None):
    s, d = x.shape
    n = w_in.shape[1]
    tm = min(TM_PROJ, s)
    nt = s // tm

    def body(x_ref, g_ref, w_ref, u_ref):
        xv = x_ref[...]
        r = lax.rsqrt(_rowmean(xv * xv) + EPS)
        u_ref[...] = _dot((xv * r * g_ref[...]).astype(BF16), w_ref[...])

    def stages():
        i = pl.program_id(0)
        return i == 0, i == max(nt - 3, 0), i == nt - 1

    return _call(
        body, hosted, stages, grid=(nt,), name="inproj",
        in_specs=[pl.BlockSpec((tm, d), lambda i: (i, 0)), _const_spec((1, d)), _const_spec((d, n))],
        out_specs=[pl.BlockSpec((tm, n), lambda i: (i, 0))], out_shape=[SDS((s, n), F32)], scratch_shapes=[],
        args=(x, g_mix, w_in), sem=("arbitrary",))


def _mixer_fwd(u, x, sp_, w_out, hosted=None):
    s, din = u.shape
    d = x.shape[1]
    lw = din // 3
    tm = min(TM_MIX, s)
    nb = lw // GATE_BLOCK

    def body(u_ref, halo_ref, x_ref, cw_ref, cb_ref, ga_ref, gx_ref, ba_ref, bx_ref, lam_ref, pw_ref, pb_ref,
             ps_ref, gl_ref, gp_ref, wout_ref, h_ref, yn_ref, hres_ref, saved_ref, pooled_ref,
             gw_s, a_s, b_s, carry_s):
        i = pl.program_id(0)

        @pl.when(i == 0)
        def _():
            _build_gate_blocks(ga_ref, gx_ref, gw_s)
            carry_s[...] = jnp.zeros_like(carry_s)

        uv = u_ref[...]
        hal = jnp.where(i > 0, halo_ref[...], 0.0)
        f = _mixer_recompute(uv, hal, i * tm, cw_ref[...], cb_ref[...], gw_s, ba_ref[...], bx_ref[...],
                             lam_ref[...], pw_ref, pb_ref[...], ps_ref[...])
        for k, name in enumerate(MIX_SAVED):
            saved_ref[k] = f[name]
        pooled_ref[...] = jnp.concatenate(f["pooled"], axis=1).astype(BF16)
        bb = f["mult"] * (f["ig"] * f["xc"])
        a1, b1 = _scan_level1(f["a"], bb, reverse=False)
        a_s[...] = a1
        b_s[...] = b1
        carry_s[...] = _scan_level2(a_s, b_s, h_ref, carry_s[...], reverse=False)
        y_lru = h_ref[...] * f["ge"]
        rl = lax.rsqrt(_rowmean(y_lru * y_lru) + EPS)
        yp = f["y_pool"]
        rp = lax.rsqrt(_rowmean(yp * yp) + EPS)
        yn = jnp.concatenate([y_lru * rl * gl_ref[...], yp * rp * gp_ref[...]], axis=1).astype(BF16)
        yn_ref[...] = yn
        hres_ref[...] = x_ref[...] + _dot(yn, wout_ref[...])

    small = [sp_[k] for k in ("conv_w", "conv_b", "gate_a_w", "gate_x_w", "gate_a_b", "gate_x_b", "lru_lambda",
                              "pool_w", "pool_b", "pool_scale", "norm_lru_g", "norm_pool_g")]
    nt = s // tm

    def stages():
        i = pl.program_id(0)
        return i == 0, i == max(nt - 3, 0), i == nt - 1

    return _call(
        body, hosted, stages, grid=(nt,), name="mixer_fwd",
        in_specs=[pl.BlockSpec((tm, din), lambda i: (i, 0)),
                  pl.BlockSpec((HALO, din), lambda i: (jnp.maximum(i * (tm // HALO) - 1, 0), 0)),
                  pl.BlockSpec((tm, d), lambda i: (i, 0))]
        + [_const_spec(a.shape) for a in small] + [_const_spec(w_out.shape)],
        out_specs=[pl.BlockSpec((tm, lw), lambda i: (i, 0)), pl.BlockSpec((tm, d), lambda i: (i, 0)),
                   pl.BlockSpec((tm, d), lambda i: (i, 0)),
                   pl.BlockSpec((len(MIX_SAVED), tm, lw), lambda i: (0, i, 0)),
                   pl.BlockSpec((tm, lw), lambda i: (i, 0))],
        out_shape=[SDS((s, lw), F32), SDS((s, d), BF16), SDS((s, d), F32), SDS((len(MIX_SAVED), s, lw), F32),
                   SDS((s, lw), BF16)],
        scratch_shapes=[pltpu.VMEM((nb, GATE_BLOCK, 2 * GATE_BLOCK), BF16), pltpu.VMEM((tm, lw), F32),
                        pltpu.VMEM((tm, lw), F32), pltpu.VMEM((SUBLANES, lw), F32)],
        args=(u, u, x, *small, w_out), sem=("arbitrary",))


def _ffn_fwd(hres1, target, g_ffn, g_fin, w1, w3, w2):
    s, d = hres1.shape
    nj, _, fc = w1.shape
    tm = min(TM_FFN, s)

    def body(h_ref, t_ref, gf_ref, gn_ref, w1_ref, w3_ref, w2_ref,
             a1_ref, a3_ref, h2_ref, dh_ref, dhb_ref, loss_ref, dgn_ref, acc_s):
        i, j = pl.program_id(0), pl.program_id(1)

        @pl.when((i == 0) & (j == 0))
        def _():
            loss_ref[...] = jnp.zeros_like(loss_ref)
            dgn_ref[...] = jnp.zeros_like(dgn_ref)

        @pl.when(j == 0)
        def _():
            hv = h_ref[...]
            r = lax.rsqrt(_rowmean(hv * hv) + EPS)
            h2_ref[...] = (hv * r * gf_ref[...]).astype(BF16)

        h2 = h2_ref[...]
        a1 = _dot(h2, w1_ref[0])
        a3 = _dot(h2, w3_ref[0])
        a1_ref[0] = a1.astype(BF16)
        a3_ref[0] = a3.astype(BF16)
        part = _dot(((a1 * _sigmoid(a1)) * a3).astype(BF16), w2_ref[0])

        @pl.when(j == 0)
        def _():
            acc_s[...] = part

        @pl.when(j > 0)
        def _():
            acc_s[...] += part

        @pl.when(j == nj - 1)
        def _():
            hr2 = h_ref[...] + acc_s[...]
            r2 = lax.rsqrt(_rowmean(hr2 * hr2) + EPS)
            xh = hr2 * r2
            gn = gn_ref[...]
            diff = xh * gn - t_ref[...]
            tot = jnp.sum(jnp.sum(diff * diff, axis=1, keepdims=True), axis=0, keepdims=True)
            loss_ref[...] += tot * (0.5 / d)
            dout = diff * (1.0 / d)
            dgn_ref[...] += _colsum8(dout * xh)
            dh = _rms_bwd(dout, xh, r2, gn)
            dh_ref[...] = dh
            dhb_ref[...] = dh.astype(BF16)

    return pl.pallas_call(
        body, grid=(s // tm, nj), name="ffn_fwd",
        in_specs=[pl.BlockSpec((tm, d), lambda i, j: (i, 0)), pl.BlockSpec((tm, d), lambda i, j: (i, 0)),
                  _const_spec((1, d)), _const_spec((1, d)),
                  pl.BlockSpec((1, d, fc), lambda i, j: (j, 0, 0)), pl.BlockSpec((1, d, fc), lambda i, j: (j, 0, 0)),
                  pl.BlockSpec((1, fc, d), lambda i, j: (j, 0, 0))],
        out_specs=[pl.BlockSpec((1, tm, fc), lambda i, j: (j, i, 0)), pl.BlockSpec((1, tm, fc), lambda i, j: (j, i, 0)),
                   pl.BlockSpec((tm, d), lambda i, j: (i, 0)), pl.BlockSpec((tm, d), lambda i, j: (i, 0)),
                   pl.BlockSpec((tm, d), lambda i, j: (i, 0)),
                   pl.BlockSpec((SUBLANES, LANES), lambda i, j: (0, 0)),
                   pl.BlockSpec((SUBLANES, d), lambda i, j: (0, 0))],
        out_shape=[SDS((nj, s, fc), BF16), SDS((nj, s, fc), BF16), SDS((s, d), BF16), SDS((s, d), F32),
                   SDS((s, d), BF16), SDS((SUBLANES, LANES), F32), SDS((SUBLANES, d), F32)],
        scratch_shapes=[pltpu.VMEM((tm, d), F32)],
        compiler_params=_cp(("arbitrary", "arbitrary")))(hres1, target, g_ffn, g_fin, w1, w3, w2)


def _ffn_bwd_act(dh, dhb, a1, a3, hres1, g_ffn, w1, w3, w2):
    s, d = hres1.shape
    nj, _, fc = a1.shape
    tm = min(TM_FFN, s)

    def body(dh_ref, dhb_ref, a1_ref, a3_ref, h_ref, gf_ref, w1_ref, w3_ref, w2_ref,
             da1_ref, da3_ref, dhr_ref, dgf_ref, acc_s):
        i, j = pl.program_id(0), pl.program_id(1)

        @pl.when((i == 0) & (j == 0))
        def _():
            dgf_ref[...] = jnp.zeros_like(dgf_ref)

        @pl.when(j == 0)
        def _():
            acc_s[...] = jnp.zeros_like(acc_s)

        rc = tm // FFN_ROW_CHUNKS
        for q in range(FFN_ROW_CHUNKS):
            rows = slice(q * rc, (q + 1) * rc)
            dff = _dot_nt(dhb_ref[rows, :], w2_ref[0])
            a1v = a1_ref[0, rows, :].astype(F32)
            a3v = a3_ref[0, rows, :].astype(F32)
            sg = _sigmoid(a1v)
            silu = a1v * sg
            da1 = (dff * a3v * (sg * (1.0 + a1v * (1.0 - sg)))).astype(BF16)
            da3 = (dff * silu).astype(BF16)
            da1_ref[0, rows, :] = da1
            da3_ref[0, rows, :] = da3
            acc_s[rows, :] += _dot_nt(da1, w1_ref[0]) + _dot_nt(da3, w3_ref[0])

        @pl.when(j == nj - 1)
        def _():
            hv = h_ref[...]
            r = lax.rsqrt(_rowmean(hv * hv) + EPS)
            xh = hv * r
            dh2 = acc_s[...]
            dgf_ref[...] += _colsum8(dh2 * xh)
            dhr_ref[...] = dh_ref[...] + _rms_bwd(dh2, xh, r, gf_ref[...])

    return pl.pallas_call(
        body, grid=(s // tm, nj), name="ffn_bwd_act",
        in_specs=[pl.BlockSpec((tm, d), lambda i, j: (i, 0)), pl.BlockSpec((tm, d), lambda i, j: (i, 0)),
                  pl.BlockSpec((1, tm, fc), lambda i, j: (j, i, 0)), pl.BlockSpec((1, tm, fc), lambda i, j: (j, i, 0)),
                  pl.BlockSpec((tm, d), lambda i, j: (i, 0)), _const_spec((1, d)),
                  pl.BlockSpec((1, d, fc), lambda i, j: (j, 0, 0)), pl.BlockSpec((1, d, fc), lambda i, j: (j, 0, 0)),
                  pl.BlockSpec((1, fc, d), lambda i, j: (j, 0, 0))],
        out_specs=[pl.BlockSpec((1, tm, fc), lambda i, j: (j, i, 0)), pl.BlockSpec((1, tm, fc), lambda i, j: (j, i, 0)),
                   pl.BlockSpec((tm, d), lambda i, j: (i, 0)), pl.BlockSpec((SUBLANES, d), lambda i, j: (0, 0))],
        out_shape=[SDS((nj, s, fc), BF16), SDS((nj, s, fc), BF16), SDS((s, d), F32), SDS((SUBLANES, d), F32)],
        scratch_shapes=[pltpu.VMEM((tm, d), F32)],
        compiler_params=_cp(("arbitrary", "arbitrary")))(dh, dhb, a1, a3, hres1, g_ffn, w1, w3, w2)


def _ffn_wgrad(h2, dhb, a1, a3, da1, da3):
    s, d = h2.shape
    _, _, fc = a1.shape
    tm = min(TM_WGRAD, s)

    def body(h2_ref, dhb_ref, a1_ref, a3_ref, da1_ref, da3_ref, dw1_ref, dw3_ref, dw2_ref):
        i = pl.program_id(1)

        @pl.when(i == 0)
        def _():
            dw1_ref[...] = jnp.zeros_like(dw1_ref)
            dw3_ref[...] = jnp.zeros_like(dw3_ref)
            dw2_ref[...] = jnp.zeros_like(dw2_ref)

        h2v = h2_ref[...]
        a1v = a1_ref[0].astype(F32)
        ff = ((a1v * _sigmoid(a1v)) * a3_ref[0].astype(F32)).astype(BF16)
        dw1_ref[0] += _dot_tn(h2v, da1_ref[0])
        dw3_ref[0] += _dot_tn(h2v, da3_ref[0])
        dw2_ref[0] += _dot_tn(ff, dhb_ref[...])

    return pl.pallas_call(
        body, grid=(N_CHIPS, s // tm), name="ffn_wgrad",
        in_specs=[pl.BlockSpec((tm, d), lambda j, i: (i, 0)), pl.BlockSpec((tm, d), lambda j, i: (i, 0))]
        + [pl.BlockSpec((1, tm, fc), lambda j, i: (j, i, 0))] * 4,
        out_specs=[pl.BlockSpec((1, d, fc), lambda j, i: (j, 0, 0)), pl.BlockSpec((1, d, fc), lambda j, i: (j, 0, 0)),
                   pl.BlockSpec((1, fc, d), lambda j, i: (j, 0, 0))],
        out_shape=[SDS((N_CHIPS, d, fc), F32), SDS((N_CHIPS, d, fc), F32), SDS((N_CHIPS, fc, d), F32)],
        compiler_params=_cp(("parallel", "arbitrary")))(h2, dhb, a1, a3, da1, da3)


def _row_chunks(tm):
    rc = tm // FFN_ROW_CHUNKS
    return [slice(q * rc, (q + 1) * rc) for q in range(FFN_ROW_CHUNKS)]


def _ffn_up(hres1, g_ffn, w1, w3):
    s, d = hres1.shape
    nj, _, fc = w1.shape
    tm = min(TM_FFN_UP, s)

    def body(h_ref, gf_ref, w1_ref, w3_ref, h2_ref, a1_ref, a3_ref, ff_ref):
        @pl.when(pl.program_id(1) == 0)
        def _():
            hv = h_ref[...]
            r = lax.rsqrt(_rowmean(hv * hv) + EPS)
            h2_ref[...] = (hv * r * gf_ref[...]).astype(BF16)

        j = pl.program_id(1)
        for rows in _row_chunks(tm):
            h2 = h2_ref[rows, :]
            a1 = _dot(h2, w1_ref[j])
            a3 = _dot(h2, w3_ref[j])
            a1_ref[0, rows, :] = a1.astype(BF16)
            a3_ref[0, rows, :] = a3.astype(BF16)
            ff_ref[0, rows, :] = ((a1 * _sigmoid(a1)) * a3).astype(BF16)

    wspec = _const_spec(w1.shape)
    aspec = pl.BlockSpec((1, tm, fc), lambda i, j: (j, i, 0))
    return pl.pallas_call(
        body, grid=(s // tm, nj), name="ffn_up",
        in_specs=[pl.BlockSpec((tm, d), lambda i, j: (i, 0)), _const_spec((1, d)), wspec, wspec],
        out_specs=[pl.BlockSpec((tm, d), lambda i, j: (i, 0)), aspec, aspec, aspec],
        out_shape=[SDS((s, d), BF16)] + [SDS((nj, s, fc), BF16)] * 3,
        compiler_params=_cp(("parallel", "arbitrary")))(hres1, g_ffn, w1, w3)


def _ffn_down(ff, hres1, target, g_fin, w2):
    s, d = hres1.shape
    nj, _, fc = ff.shape
    tm = min(TM_FFN_DOWN, s)

    def body(ff_ref, h_ref, t_ref, gn_ref, w2_ref, dh_ref, dhb_ref, loss_ref, dgn_ref):
        @pl.when(pl.program_id(0) == 0)
        def _():
            loss_ref[...] = jnp.zeros_like(loss_ref)
            dgn_ref[...] = jnp.zeros_like(dgn_ref)

        gn = gn_ref[...]
        for rows in _row_chunks(tm):
            acc = _dot(ff_ref[0, rows, :], w2_ref[0])
            for j in range(1, nj):
                acc = acc + _dot(ff_ref[j, rows, :], w2_ref[j])
            hr2 = h_ref[rows, :] + acc
            r2 = lax.rsqrt(_rowmean(hr2 * hr2) + EPS)
            xh = hr2 * r2
            diff = xh * gn - t_ref[rows, :]
            tot = jnp.sum(jnp.sum(diff * diff, axis=1, keepdims=True), axis=0, keepdims=True)
            loss_ref[...] += tot * (0.5 / d)
            dout = diff * (1.0 / d)
            dgn_ref[...] += _colsum8(dout * xh)
            dh = _rms_bwd(dout, xh, r2, gn)
            dh_ref[rows, :] = dh
            dhb_ref[rows, :] = dh.astype(BF16)

    tile = pl.BlockSpec((tm, d), lambda i: (i, 0))
    return pl.pallas_call(
        body, grid=(s // tm,), name="ffn_down",
        in_specs=[pl.BlockSpec((nj, tm, fc), lambda i: (0, i, 0)), tile, tile, _const_spec((1, d)),
                  _const_spec(w2.shape)],
        out_specs=[tile, tile, pl.BlockSpec((SUBLANES, LANES), lambda i: (0, 0)),
                   pl.BlockSpec((SUBLANES, d), lambda i: (0, 0))],
        out_shape=[SDS((s, d), F32), SDS((s, d), BF16), SDS((SUBLANES, LANES), F32), SDS((SUBLANES, d), F32)],
        compiler_params=_cp(("arbitrary",)))(ff, hres1, target, g_fin, w2)


def _ffn_bwd_gate(dhb, a1, a3, w2):
    s, d = dhb.shape
    nj, _, fc = a1.shape
    tm = min(TM_FFN_UP, s)

    def body(dhb_ref, a1_ref, a3_ref, w2_ref, da1_ref, da3_ref):
        j = pl.program_id(1)
        for rows in _row_chunks(tm):
            dff = _dot_nt(dhb_ref[rows, :], w2_ref[j])
            a1v = a1_ref[0, rows, :].astype(F32)
            sg = _sigmoid(a1v)
            silu = a1v * sg
            da1_ref[0, rows, :] = (dff * a3_ref[0, rows, :].astype(F32) * (sg * (1.0 + (a1v - silu)))).astype(BF16)
            da3_ref[0, rows, :] = (dff * silu).astype(BF16)

    aspec = pl.BlockSpec((1, tm, fc), lambda i, j: (j, i, 0))
    return pl.pallas_call(
        body, grid=(s // tm, nj), name="ffn_bwd_gate",
        in_specs=[pl.BlockSpec((tm, d), lambda i, j: (i, 0)), aspec, aspec, _const_spec(w2.shape)],
        out_specs=[aspec, aspec], out_shape=[SDS((nj, s, fc), BF16)] * 2,
        compiler_params=_cp(("parallel", "arbitrary")))(dhb, a1, a3, w2)


def _ffn_bwd_down(da1, da3, dh, hres1, g_ffn, w1, w3, hosted=None):
    s, d = hres1.shape
    nj, _, fc = da1.shape
    tm = min(TM_FFN_DOWN, s)
    nt = s // tm

    def body(da1_ref, da3_ref, dh_ref, h_ref, gf_ref, w1_ref, w3_ref, dhr_ref, dgf_ref):
        @pl.when(pl.program_id(0) == 0)
        def _():
            dgf_ref[...] = jnp.zeros_like(dgf_ref)

        gf = gf_ref[...]
        for rows in _row_chunks(tm):
            dh2 = None
            for j in range(nj):
                part = _dot_nt(da1_ref[j, rows, :], w1_ref[j]) + _dot_nt(da3_ref[j, rows, :], w3_ref[j])
                dh2 = part if dh2 is None else dh2 + part
            hv = h_ref[rows, :]
            r = lax.rsqrt(_rowmean(hv * hv) + EPS)
            xh = hv * r
            dgf_ref[...] += _colsum8(dh2 * xh)
            dhr_ref[rows, :] = dh_ref[rows, :] + _rms_bwd(dh2, xh, r, gf)

    tile = pl.BlockSpec((tm, d), lambda i: (i, 0))
    aspec = pl.BlockSpec((nj, tm, fc), lambda i: (0, i, 0))
    wspec = _const_spec(w1.shape)

    def stages():
        i = pl.program_id(0)
        return i == 0, i == max(nt - 2, 0), i == nt - 1

    return _call(
        body, hosted, stages, grid=(nt,), name="ffn_bwd_down",
        in_specs=[aspec, aspec, tile, tile, _const_spec((1, d)), wspec, wspec],
        out_specs=[tile, pl.BlockSpec((SUBLANES, d), lambda i: (0, 0))],
        out_shape=[SDS((s, d), F32), SDS((SUBLANES, d), F32)],
        scratch_shapes=[], args=(da1, da3, dh, hres1, g_ffn, w1, w3), sem=("arbitrary",))


def _ffn_wgrad2(h2, dhb, ff, da1, da3):
    s, d = h2.shape
    _, _, fc = ff.shape
    tm = min(TM_WGRAD, s)

    def body(h2_ref, dhb_ref, ff_ref, da1_ref, da3_ref, dw1_ref, dw3_ref, dw2_ref):
        @pl.when(pl.program_id(1) == 0)
        def _():
            dw1_ref[...] = jnp.zeros_like(dw1_ref)
            dw3_ref[...] = jnp.zeros_like(dw3_ref)
            dw2_ref[...] = jnp.zeros_like(dw2_ref)

        h2v = h2_ref[...]
        dw1_ref[0] += _dot_tn(h2v, da1_ref[0])
        dw3_ref[0] += _dot_tn(h2v, da3_ref[0])
        dw2_ref[0] += _dot_tn(ff_ref[0], dhb_ref[...])

    return pl.pallas_call(
        body, grid=(N_CHIPS, s // tm), name="ffn_wgrad",
        in_specs=[pl.BlockSpec((tm, d), lambda j, i: (i, 0)), pl.BlockSpec((tm, d), lambda j, i: (i, 0))]
        + [pl.BlockSpec((1, tm, fc), lambda j, i: (j, i, 0))] * 3,
        out_specs=[pl.BlockSpec((1, d, fc), lambda j, i: (j, 0, 0)), pl.BlockSpec((1, d, fc), lambda j, i: (j, 0, 0)),
                   pl.BlockSpec((1, fc, d), lambda j, i: (j, 0, 0))],
        out_shape=[SDS((N_CHIPS, d, fc), F32), SDS((N_CHIPS, d, fc), F32), SDS((N_CHIPS, fc, d), F32)],
        compiler_params=_cp(("parallel", "arbitrary")))(h2, dhb, ff, da1, da3)


def _mixer_bwd(u, saved, pooled, h, dhres1, sp_, w_out, hosted=None):
    s, din = u.shape
    d = dhres1.shape[1]
    lw = din // 3
    tm = min(TM_MIX, s)
    nt = s // tm
    nb = lw // GATE_BLOCK
    hd = sp_["gate_a_w"].shape[0]

    def body(ul_ref, saved_ref, pooled_ref, h_ref, hhalo_ref, dhr_ref, cw_ref, cb_ref, ga_ref, gx_ref, ba_ref,
             bx_ref, lam_ref, pw_ref, pb_ref, ps_ref, gl_ref, gp_ref, wout_ref, du_ref, slab_ref,
             gw_s, a_s, b_s, e_s, ecarry_s, dxc_s, q_s, vec_s, cwacc_s, dgw_s, dpw_s):
        i = pl.program_id(0)
        tile = nt - 1 - i

        @pl.when(i == 0)
        def _():
            _build_gate_blocks(ga_ref, gx_ref, gw_s)
            for ref in (ecarry_s, dxc_s, q_s, vec_s, cwacc_s, dgw_s, dpw_s):
                ref[...] = jnp.zeros_like(ref)

        cw = cw_ref[...]
        lam = lam_ref[...]
        ps = ps_ref[...]
        f = {name: saved_ref[k] for k, name in enumerate(MIX_SAVED)}
        f["mult"] = jnp.sqrt(jnp.maximum(f["m2raw"], 1e-12))
        f["sp"] = _softplus_neg(lam)
        f["xcb"] = f["xc"].astype(BF16)
        pooled = pooled_ref[...]
        row = lax.broadcasted_iota(jnp.int32, (tm, LANES), 0) + tile * tm
        f["invs"] = [1.0 / jnp.minimum(row + 1, w).astype(F32) for w in POOL_WINDOWS]
        f["z"] = jnp.concatenate(
            [_dot(pooled[:, g * LANES:(g + 1) * LANES], pw_ref[:, g * LANES:(g + 1) * LANES].astype(BF16))
             for g in range(len(POOL_WINDOWS))], axis=1) + pb_ref[...]
        f["y_pool"] = f["z"] * ps
        u_l = ul_ref[...]
        hv = h_ref[...]
        h_prev = _shift_down(jnp.where(tile > 0, hhalo_ref[...], 0.0), hv, 1)
        y_lru = hv * f["ge"]
        rl = lax.rsqrt(_rowmean(y_lru * y_lru) + EPS)
        yp = f["y_pool"]
        rp = lax.rsqrt(_rowmean(yp * yp) + EPS)
        xh_l = y_lru * rl
        xh_p = yp * rp

        dyn = _dot_nt(dhr_ref[...].astype(BF16), wout_ref[...])
        d_nl, d_np = dyn[:, :lw], dyn[:, lw:]
        vec = {}
        vec[ROW_GL] = _colsum8(d_nl * xh_l)
        vec[ROW_GP] = _colsum8(d_np * xh_p)
        d_ylru = _rms_bwd(d_nl, xh_l, rl, gl_ref[...])
        d_ypool = _rms_bwd(d_np, xh_p, rp, gp_ref[...])

        vec[ROW_PS] = _colsum8(d_ypool * f["z"])
        dz = d_ypool * ps
        vec[ROW_PB] = _colsum8(dz)
        dzb = dz.astype(BF16)
        dup = []
        for gi, w in enumerate(POOL_WINDOWS):
            sl = slice(gi * LANES, (gi + 1) * LANES)
            dpw_s[:, sl] += _dot_tn(pooled[:, sl], dzb[:, sl])
            dpool = _dot_nt(dzb[:, sl], pw_ref[:, sl].astype(BF16))
            q = dpool * f["invs"][gi]
            e = jnp.concatenate([q, q_s[:, sl]], axis=0)
            k = 1
            while k < w:
                e = e + pltpu.roll(e, tm + HALO - k, 0)
                k *= 2
            dup.append(e[:tm] - dpool)
            q_s[:, sl] = q[:HALO]

        d_hout = d_ylru * f["ge"]
        d_ug = d_ylru * hv * f["dge"]
        a = f["a"]
        a1, b1 = _scan_level1(a, a * d_hout, reverse=True)
        a_s[...] = a1
        b_s[...] = b1
        e_next = ecarry_s[...]
        ecarry_s[...] = _scan_level2(a_s, b_s, e_s, e_next, reverse=True)
        sv = d_hout + _shift_up(e_s[...], e_next, 1)
        d_a = sv * h_prev
        mult, ig, xc, r = f["mult"], f["ig"], f["xc"], f["r"]
        d_mult = sv * (ig * xc)
        d_ig = sv * mult * xc
        d_xc = sv * mult * ig
        d_la = d_a * a + jnp.where(f["m2raw"] > 1e-12, d_mult * (-(a * a) / mult), 0.0)
        d_r = d_la * (-LRU_C * f["sp"])
        vec[ROW_LAM] = _colsum8(d_la * (-LRU_C * r))
        d_pr = d_r * r * (1.0 - r)
        d_pi = d_ig * ig * (1.0 - ig)
        vec[ROW_BA] = _colsum8(d_pr)
        vec[ROW_BX] = _colsum8(d_pi)
        dxc_parts = []
        for b in range(nb):
            sl = slice(b * GATE_BLOCK, (b + 1) * GATE_BLOCK)
            rhs = jnp.concatenate([d_pr[:, sl], d_pi[:, sl]], axis=1).astype(BF16)
            dgw_s[b] += _dot_tn(f["xcb"][:, sl], rhs)
            dxc_parts.append(_dot_nt(rhs, gw_s[b]))
        d_xc = d_xc + jnp.concatenate(dxc_parts, axis=1)
        vec[ROW_CONV_B] = _colsum8(d_xc)
        dxc_next = dxc_s[...]
        d_ul = None
        for k in range(CONV_WIDTH):
            ahead = _shift_up(d_xc, dxc_next, CONV_WIDTH - 1 - k)
            cwacc_s[k * SUBLANES:(k + 1) * SUBLANES, :] += _colsum8(ahead * u_l)
            term = ahead * cw[k:k + 1, :]
            d_ul = term if d_ul is None else d_ul + term
        dxc_s[...] = d_xc[:SUBLANES]
        for row, val in vec.items():
            vec_s[row * SUBLANES:(row + 1) * SUBLANES, :] += val
        du_ref[...] = jnp.concatenate([d_ul, d_ug] + dup, axis=1).astype(BF16)

        @pl.when(i == nt - 1)
        def _():
            rows = []
            for row in range(ROW_GA):
                if row in (ROW_CONV_W, ROW_CONV_W + 1, ROW_CONV_W + 2, ROW_CONV_W + 3):
                    k = row - ROW_CONV_W
                    v = jnp.sum(cwacc_s[k * SUBLANES:(k + 1) * SUBLANES, :], axis=0, keepdims=True)
                elif row <= ROW_GP:
                    v = jnp.sum(vec_s[row * SUBLANES:(row + 1) * SUBLANES, :], axis=0, keepdims=True)
                    if row == ROW_LAM:
                        v = v * (-1.0 / (1.0 + jnp.exp(lam)))
                else:
                    v = jnp.zeros((1, lw), F32)
                rows.append(v)
            slab_ref[0:ROW_GA, :] = jnp.concatenate(rows, axis=0)
            lane = lax.broadcasted_iota(jnp.int32, (hd, GATE_BLOCK), 1)
            for b in range(nb):
                for off, row0 in ((0, ROW_GA), (GATE_BLOCK, ROW_GX)):
                    acc = jnp.zeros((hd, GATE_BLOCK), F32)
                    for hh in range(GATE_BLOCK // hd):
                        m = (lane >= hh * hd) & (lane < (hh + 1) * hd)
                        acc = acc + jnp.where(m, dgw_s[b, hh * hd:(hh + 1) * hd, off:off + GATE_BLOCK], 0.0)
                    slab_ref[row0:row0 + hd, b * GATE_BLOCK:(b + 1) * GATE_BLOCK] = acc
            slab_ref[ROW_PW:ROW_PW + LANES, :] = dpw_s[...]

    small = [sp_[k] for k in ("conv_w", "conv_b", "gate_a_w", "gate_x_w", "gate_a_b", "gate_x_b", "lru_lambda",
                              "pool_w", "pool_b", "pool_scale", "norm_lru_g", "norm_pool_g")]
    rev = lambda i: nt - 1 - i

    def stages():
        i = pl.program_id(0)
        return i == 0, i == max(nt - 3, 0), i == nt - 1

    return _call(
        body, hosted, stages, grid=(nt,), name="mixer_bwd",
        in_specs=[pl.BlockSpec((tm, lw), lambda i: (rev(i), 0)),
                  pl.BlockSpec((len(MIX_SAVED), tm, lw), lambda i: (0, rev(i), 0)),
                  pl.BlockSpec((tm, lw), lambda i: (rev(i), 0)),
                  pl.BlockSpec((tm, lw), lambda i: (rev(i), 0)),
                  pl.BlockSpec((SUBLANES, lw), lambda i: (jnp.maximum(rev(i) * (tm // SUBLANES) - 1, 0), 0)),
                  pl.BlockSpec((tm, d), lambda i: (rev(i), 0))]
        + [_const_spec(a.shape) for a in small] + [_const_spec(w_out.shape)],
        out_specs=[pl.BlockSpec((tm, din), lambda i: (rev(i), 0)),
                   pl.BlockSpec((MIX_SLAB_ROWS, SLAB_W), lambda i: (0, 0))],
        out_shape=[SDS((s, din), BF16), SDS((MIX_SLAB_ROWS, SLAB_W), F32)],
        scratch_shapes=[pltpu.VMEM((nb, GATE_BLOCK, 2 * GATE_BLOCK), BF16),
                        pltpu.VMEM((tm, lw), F32), pltpu.VMEM((tm, lw), F32), pltpu.VMEM((tm, lw), F32),
                        pltpu.VMEM((SUBLANES, lw), F32), pltpu.VMEM((SUBLANES, lw), F32),
                        pltpu.VMEM((HALO, lw), F32), pltpu.VMEM((ROW_GA * SUBLANES, lw), F32),
                        pltpu.VMEM((CONV_WIDTH * SUBLANES, lw), F32),
                        pltpu.VMEM((nb, GATE_BLOCK, 2 * GATE_BLOCK), F32), pltpu.VMEM((LANES, lw), F32)],
        args=(u, saved, pooled, h, h, dhres1, *small, w_out), sem=("arbitrary",))


def _inproj_bwd(x, du, dhres1, yn, g_mix, w_in, hosted=None):
    s, d = x.shape
    n = w_in.shape[1]
    nc = n // N_CHIPS
    tm = min(TM_PROJ, s)
    nt = s // tm

    def body(x_ref, du_ref, dhr_ref, yn_ref, g_ref, w_ref, gx_ref, dwin_ref, dwout_ref, dg_ref):
        i = pl.program_id(0)

        @pl.when(i == 0)
        def _():
            dwin_ref[...] = jnp.zeros_like(dwin_ref)
            dwout_ref[...] = jnp.zeros_like(dwout_ref)
            dg_ref[...] = jnp.zeros_like(dg_ref)

        xv = x_ref[...]
        g = g_ref[...]
        r = lax.rsqrt(_rowmean(xv * xv) + EPS)
        xh = xv * r
        h1 = (xh * g).astype(BF16)
        duv = du_ref[...]
        dh1 = _dot_nt(duv, w_ref[...])
        dg_ref[...] += _colsum8(dh1 * xh)
        dhr = dhr_ref[...]
        gx_ref[...] = dhr + _rms_bwd(dh1, xh, r, g)
        for jj in range(N_CHIPS):
            dwin_ref[jj] += _dot_tn(h1, duv[:, jj * nc:(jj + 1) * nc])
        dwout_ref[...] += _dot_tn(yn_ref[...], dhr.astype(BF16))

    def stages():
        i = pl.program_id(0)
        return i == 0, i == max(nt - 3, 0), i == nt - 1

    return _call(
        body, hosted, stages, grid=(nt,), name="inproj_bwd",
        in_specs=[pl.BlockSpec((tm, d), lambda i: (i, 0)), pl.BlockSpec((tm, n), lambda i: (i, 0)),
                  pl.BlockSpec((tm, d), lambda i: (i, 0)), pl.BlockSpec((tm, d), lambda i: (i, 0)),
                  _const_spec((1, d)), _const_spec((d, n))],
        out_specs=[pl.BlockSpec((tm, d), lambda i: (i, 0)), pl.BlockSpec((N_CHIPS, d, nc), lambda i: (0, 0, 0)),
                   pl.BlockSpec((d, d), lambda i: (0, 0)), pl.BlockSpec((SUBLANES, d), lambda i: (0, 0))],
        out_shape=[SDS((s, d), F32), SDS((N_CHIPS, d, nc), F32), SDS((d, d), F32), SDS((SUBLANES, d), F32)],
        scratch_shapes=[], args=(x, du, dhres1, yn, g_mix, w_in), sem=("arbitrary",))


def _place():
    x, y, c = lax.axis_index("x"), lax.axis_index("y"), lax.axis_index("c")
    return x, y, c


def _other_chips(x, y):
    return [(1 - x, y), (x, 1 - y), (1 - x, 1 - y)]


ANY = pl.BlockSpec(memory_space=pl.ANY)
VMEM_SPEC = pl.BlockSpec(memory_space=pltpu.VMEM)

_GATHERED = {"w_in": "cols", "w_out": "major", "ffn_w1": "major", "ffn_w3": "major", "ffn_w2": "major"}
_BIG = ("w_in", "w_out", "ffn_w1", "ffn_w3", "ffn_w2")


def _gather_weights(shards, conv_w, n_remote):
    n = len(shards)
    full_shapes = []
    for name, sh in zip(_BIG, shards):
        r, cdim = sh.shape
        if _GATHERED[name] == "cols":
            assert cdim % LANES == 0
            full_shapes.append((r, cdim * N_CHIPS))
        else:
            full_shapes.append((N_CHIPS, r, cdim))

    def region(ref, name, sh, jj, cc):
        r, cdim = sh
        rows = pl.ds(0, r) if cc is None else pl.ds(pl.multiple_of(cc * (r // 2), 16), r // 2)
        if _GATHERED[name] == "cols":
            return ref.at[rows, pl.ds(pl.multiple_of(jj * cdim, LANES), cdim)]
        return ref.at[jj, rows, :]

    def staged(ref, sh, cc):
        r = sh[0]
        return ref.at[pl.ds(pl.multiple_of(cc * (r // 2), 16), r // 2), :]

    def body(*refs):
        ins, cw_in = refs[:n], refs[n]
        outs, cw_out = refs[n + 1:2 * n + 1], refs[2 * n + 1]
        stage = refs[2 * n + 2:3 * n + 2]
        cw_stage, lsem, ssem, rsem, fssem, frsem, cssem, crsem = refs[3 * n + 2:]
        x, y, c = _place()
        j = 2 * x + y
        chips = _other_chips(x, y)
        for w in range(n):
            stage[w][...] = ins[w][...].astype(BF16)
        cw_stage[...] = jnp.zeros_like(cw_stage)
        cw_stage[0:CONV_WIDTH, :] = cw_in[...]
        shs = [s_.shape for s_ in shards]
        local = [pltpu.make_async_copy(stage[w], region(outs[w], _BIG[w], shs[w], j, None), lsem.at[w])
                 for w in range(n)]
        local.append(pltpu.make_async_copy(cw_stage, cw_out.at[j], lsem.at[n]))
        for cp in local:
            cp.start()
        sends = []
        for k, (px, py) in enumerate(chips):
            for w in range(n_remote):
                sends.append(pltpu.make_async_remote_copy(
                    src_ref=staged(stage[w], shs[w], c), dst_ref=region(outs[w], _BIG[w], shs[w], j, c),
                    send_sem=ssem.at[k * n + w], recv_sem=rsem.at[k * n + w], device_id=(px, py, c),
                    device_id_type=MESH))
            sends.append(pltpu.make_async_remote_copy(
                src_ref=cw_stage, dst_ref=cw_out.at[j], send_sem=cssem.at[k], recv_sem=crsem.at[k],
                device_id=(px, py, c), device_id_type=MESH))
        for cp in sends:
            cp.start()
        fwd = []
        for k, (px, py) in enumerate(chips):
            jk = 2 * px + py
            for w in range(n_remote):
                reg = region(outs[w], _BIG[w], shs[w], jk, c)
                pltpu.make_async_remote_copy(src_ref=reg, dst_ref=reg, send_sem=ssem.at[k * n + w],
                                             recv_sem=rsem.at[k * n + w], device_id=(px, py, c),
                                             device_id_type=MESH).wait_recv()
                cp = pltpu.make_async_remote_copy(src_ref=reg, dst_ref=reg, send_sem=fssem.at[k * n + w],
                                                  recv_sem=frsem.at[k * n + w], device_id=(x, y, 1 - c),
                                                  device_id_type=MESH)
                cp.start()
                fwd.append(cp)
            pltpu.make_async_remote_copy(src_ref=cw_stage, dst_ref=cw_out.at[jk], send_sem=cssem.at[k],
                                         recv_sem=crsem.at[k], device_id=(px, py, c),
                                         device_id_type=MESH).wait_recv()
        for k, (px, py) in enumerate(chips):
            jk = 2 * px + py
            for w in range(n_remote):
                reg = region(outs[w], _BIG[w], shs[w], jk, 1 - c)
                pltpu.make_async_remote_copy(src_ref=reg, dst_ref=reg, send_sem=fssem.at[k * n + w],
                                             recv_sem=frsem.at[k * n + w], device_id=(x, y, 1 - c),
                                             device_id_type=MESH).wait_recv()
        for cp in sends + fwd:
            cp.wait_send()
        for cp in local:
            cp.wait()

    nsem = 3 * n
    return pl.pallas_call(
        body, name="gather_first",
        in_specs=[VMEM_SPEC] * (n + 1), out_specs=[ANY] * (n + 1),
        out_shape=[SDS(fs, BF16) for fs in full_shapes] + [SDS((N_CHIPS, SUBLANES, LANES), F32)],
        scratch_shapes=[pltpu.VMEM(s_.shape, BF16) for s_ in shards] + [pltpu.VMEM((SUBLANES, LANES), F32)]
        + [pltpu.SemaphoreType.DMA((n + 1,))] + [pltpu.SemaphoreType.DMA((nsem,))] * 4
        + [pltpu.SemaphoreType.DMA((3,))] * 2,
        compiler_params=_cp())(*shards, conv_w)


def _start_all(make):
    def f(ins, outs, sems):
        for cp in make(ins, outs, sems):
            cp.start()
    return f


def _wait_all(make):
    def f(ins, outs, sems):
        for cp in make(ins, outs, sems):
            cp.wait()
    return f


def _ffn_gather_hosted(arrs):
    n = len(arrs)

    def make(outs, sems):
        ssem, rsem, fs, fr = sems
        x, y, c = _place()
        j = 2 * x + y

        def reg(w, jj, cc):
            hr = arrs[w].shape[1] // 2
            return outs[w].at[jj, pl.ds(pl.multiple_of(cc * hr, 16), hr), :]

        def rc(w, jj, cc, s_sem, r_sem, dev):
            return pltpu.make_async_remote_copy(src_ref=reg(w, jj, cc), dst_ref=reg(w, jj, cc), send_sem=s_sem,
                                                recv_sem=r_sem, device_id=dev, device_id_type=MESH)

        sends, recvs, fwds, frecvs = [], [], [], []
        for k, (px, py) in enumerate(_other_chips(x, y)):
            jk = 2 * px + py
            for w in range(n):
                q = k * n + w
                sends.append(rc(w, j, c, ssem.at[q], rsem.at[q], (px, py, c)))
                recvs.append(rc(w, jk, c, ssem.at[q], rsem.at[q], (px, py, c)))
                fwds.append(rc(w, jk, c, fs.at[q], fr.at[q], (x, y, 1 - c)))
                frecvs.append(rc(w, jk, 1 - c, fs.at[q], fr.at[q], (x, y, 1 - c)))
        return sends, recvs, fwds, frecvs

    def start(ins, outs, sems):
        for cp in make(outs, sems)[0]:
            cp.start()

    def mid(ins, outs, sems):
        _, recvs, fwds, _ = make(outs, sems)
        for r, f in zip(recvs, fwds):
            r.wait_recv()
            f.start()

    def finish(ins, outs, sems):
        sends, _, fwds, frecvs = make(outs, sems)
        for r in frecvs:
            r.wait_recv()
        for cp in sends + fwds:
            cp.wait_send()

    return _Hosted(arrs, [SDS(a.shape, a.dtype) for a in arrs], [3 * n] * 4, start, finish, mid=mid,
                   aliases={w: w for w in range(n)})


def _rs_sibling_hosted(arrs):
    n = len(arrs)

    def make(ins, outs, sems):
        x, y, c = _place()
        cps = []
        for w in range(n):
            hr = arrs[w].shape[1] // 2
            src = ins[w].at[:, pl.ds(pl.multiple_of((1 - c) * hr, SUBLANES), hr), :]
            cps.append(pltpu.make_async_remote_copy(src_ref=src, dst_ref=outs[w], send_sem=sems[0].at[w],
                                                    recv_sem=sems[1].at[w], device_id=(x, y, 1 - c),
                                                    device_id_type=MESH))
        return cps

    return _Hosted(arrs, [SDS((a.shape[0], a.shape[1] // 2, a.shape[2]), F32) for a in arrs], [n, n],
                   _start_all(make), _wait_all(make))


def _rs_chips_hosted(parts):
    n = len(parts)

    def make(ins, outs, sems):
        x, y, c = _place()
        j = 2 * x + y
        cps = []
        for k, (px, py) in enumerate(_other_chips(x, y)):
            jk = 2 * px + py
            for w in range(n):
                cps.append(pltpu.make_async_remote_copy(
                    src_ref=ins[w].at[jk], dst_ref=outs[w].at[j], send_sem=sems[0].at[k * n + w],
                    recv_sem=sems[1].at[k * n + w], device_id=(px, py, c), device_id_type=MESH))
        return cps

    return _Hosted(parts, [SDS(p.shape, p.dtype) for p in parts], [3 * n, 3 * n], _start_all(make), _wait_all(make))


def _rs_swap_hosted(halves):
    n = len(halves)

    def make(ins, outs, sems):
        x, y, c = _place()
        return [pltpu.make_async_remote_copy(src_ref=ins[w], dst_ref=outs[w], send_sem=sems[0].at[w],
                                             recv_sem=sems[1].at[w], device_id=(x, y, 1 - c), device_id_type=MESH)
                for w in range(n)]

    return _Hosted(halves, [SDS(h.shape, F32) for h in halves], [n, n], _start_all(make), _wait_all(make))


def _run_comm(hosted, name):
    return _call(lambda: None, hosted, None, name=name, grid=(), in_specs=[], out_specs=[], out_shape=[],
                 scratch_shapes=[], args=(), sem=None)[1]


def _row_tile(rows, cols, n_arrays):
    budget = 24 * 1024 * 1024 // (2 * 4 * n_arrays * cols)
    best = SUBLANES
    for t in range(SUBLANES, rows + 1, SUBLANES):
        if rows % t == 0 and t <= budget:
            best = t
    return best


def _place_index(which):
    x, y, c = _place()
    v = c if which == "c" else 2 * x + y
    return jnp.reshape(v, (1,)).astype(jnp.int32)


def _add_own_half(full, recv, name):
    nsh, rows, cols = full.shape
    hr = rows // 2
    t = _row_tile(hr, cols, 4)
    nt = hr // t

    def body(c_ref, a_ref, b_ref, o_ref, ob_ref):
        v = a_ref[...] + b_ref[...]
        o_ref[...] = v
        ob_ref[...] = v.astype(BF16)

    half = pl.BlockSpec((1, t, cols), lambda s_, i, c_ref: (s_, i, 0))
    return pl.pallas_call(
        body, name=name,
        grid_spec=pltpu.PrefetchScalarGridSpec(
            num_scalar_prefetch=1, grid=(nsh, nt),
            in_specs=[pl.BlockSpec((1, t, cols), lambda s_, i, c_ref: (s_, c_ref[0] * nt + i, 0)), half],
            out_specs=[half, half]),
        out_shape=[SDS((nsh, hr, cols), F32), SDS((nsh, hr, cols), BF16)],
        compiler_params=_cp(("parallel", "parallel")))(_place_index("c"), full, recv)


def _sum_chips(own, recv, name):
    nsh, hr, cols = own.shape
    t = _row_tile(hr, cols, 6)

    def body(j_ref, own_ref, *rest):
        r_refs, o_ref = rest[:nsh], rest[nsh]
        j = j_ref[0]
        mine = own_ref[0]
        parts = [jnp.where(j == k, mine, r_refs[k][0].astype(F32)) for k in range(nsh)]
        o_ref[...] = ((parts[0] + parts[1]) + parts[2]) + parts[3]

    def other(k):
        return pl.BlockSpec((1, t, cols), lambda i, j_ref: (jnp.where(j_ref[0] == k, (k + 1) % nsh, k), i, 0))

    return pl.pallas_call(
        body, name=name,
        grid_spec=pltpu.PrefetchScalarGridSpec(
            num_scalar_prefetch=1, grid=(hr // t,),
            in_specs=[pl.BlockSpec((1, t, cols), lambda i, j_ref: (j_ref[0], i, 0))]
            + [other(k) for k in range(nsh)],
            out_specs=pl.BlockSpec((t, cols), lambda i, j_ref: (i, 0))),
        out_shape=SDS((hr, cols), F32), compiler_params=_cp(("parallel",)))(_place_index("j"), own, *([recv] * nsh))


def _adamw_math(w, g, m, v):
    m = ADAM_B1 * m + (1.0 - ADAM_B1) * g
    v = ADAM_B2 * v + (1.0 - ADAM_B2) * (g * g)
    m_hat = m / (1.0 - ADAM_B1 ** ADAM_STEP)
    v_hat = v / (1.0 - ADAM_B2 ** ADAM_STEP)
    delta = -ADAM_LR * (m_hat / (jnp.sqrt(v_hat) + ADAM_EPS) + ADAM_WD * w)
    return delta, m, v


def _adamw_big(w, g_own, g_sib, m, v, name):
    _, rows, cols = w.shape
    hr = rows // 2
    t = _row_tile(hr, cols, 9)
    nth = hr // t

    def body(c_ref, w_ref, go_ref, gs_ref, m_ref, v_ref, g_ref, d_ref, mo_ref, vo_ref):
        own = (pl.program_id(0) // nth) == c_ref[0]
        g = jnp.where(own, go_ref[...], gs_ref[...])
        g_ref[0] = g
        d_ref[0], mo_ref[0], vo_ref[0] = _adamw_math(w_ref[0], g, m_ref[0], v_ref[0])

    spec = pl.BlockSpec((1, t, cols), lambda i, c_ref: (0, i, 0))
    hspec = pl.BlockSpec((t, cols), lambda i, c_ref: (i % nth, 0))
    return pl.pallas_call(
        body, name=name,
        grid_spec=pltpu.PrefetchScalarGridSpec(
            num_scalar_prefetch=1, grid=(2 * nth,), in_specs=[spec, hspec, hspec, spec, spec],
            out_specs=[spec] * 4),
        out_shape=[SDS((1, rows, cols), F32)] * 4,
        compiler_params=_cp(("parallel",)))(_place_index("c"), w, g_own, g_sib, m, v)


def _allreduce_small(mix_slab, dg_mix, dg_ffn, dg_fin, loss8):
    half = SLAB_ROWS // 2

    def body(ms_ref, gm_ref, gf_ref, gn_ref, loss_ref, out_ref, loc_s, sib_s, chip_s, r2_s, fin_s, sems):
        x, y, c = _place()
        j = 2 * x + y
        rows = []
        for ref in (gm_ref, gf_ref, gn_ref):
            v = jnp.sum(ref[...], axis=0, keepdims=True)
            rows += [v[:, :SLAB_W], v[:, SLAB_W:]]
        rows.append(jnp.concatenate([loss_ref[0:1, :]] * (SLAB_W // LANES), axis=1))
        rows.append(jnp.zeros((SLAB_ROWS - ROW_LOSS - 1, SLAB_W), F32))
        loc_s[0:MIX_SLAB_ROWS, :] = ms_ref[...]
        loc_s[MIX_SLAB_ROWS:SLAB_ROWS, :] = jnp.concatenate(rows, axis=0)
        sib = (x, y, 1 - c)
        cp = pltpu.make_async_remote_copy(src_ref=loc_s, dst_ref=sib_s, send_sem=sems.at[0], recv_sem=sems.at[1],
                                          device_id=sib, device_id_type=MESH)
        cp.start()
        cp.wait()
        chip_s[...] = loc_s[...] + sib_s[...]
        mine = chip_s.at[pl.ds(pl.multiple_of(c * half, SUBLANES), half), :]
        r2_s[j] = chip_s[pl.ds(pl.multiple_of(c * half, SUBLANES), half), :]
        cps = []
        for k, (px, py) in enumerate(_other_chips(x, y)):
            cps.append(pltpu.make_async_remote_copy(src_ref=mine, dst_ref=r2_s.at[j], send_sem=sems.at[2 + k],
                                                    recv_sem=sems.at[5 + k], device_id=(px, py, c),
                                                    device_id_type=MESH))
        for cp in cps:
            cp.start()
        for cp in cps:
            cp.wait()
        fin_s[...] = ((r2_s[0] + r2_s[1]) + r2_s[2]) + r2_s[3]
        dst = out_ref.at[pl.ds(pl.multiple_of(c * half, SUBLANES), half), :]
        out_ref[pl.ds(pl.multiple_of(c * half, SUBLANES), half), :] = fin_s[...]
        cp = pltpu.make_async_remote_copy(src_ref=fin_s, dst_ref=dst, send_sem=sems.at[8], recv_sem=sems.at[9],
                                          device_id=sib, device_id_type=MESH)
        cp.start()
        cp.wait()

    return pl.pallas_call(
        body, name="allreduce_small", in_specs=[VMEM_SPEC] * 5, out_specs=VMEM_SPEC,
        out_shape=SDS((SLAB_ROWS, SLAB_W), F32),
        scratch_shapes=[pltpu.VMEM((SLAB_ROWS, SLAB_W), F32)] * 3 + [pltpu.VMEM((N_CHIPS, half, SLAB_W), F32),
                                                                       pltpu.VMEM((half, SLAB_W), F32),
                                                                       pltpu.SemaphoreType.DMA((10,))],
        compiler_params=_cp())(mix_slab, dg_mix, dg_ffn, dg_fin, loss8)


_SMALL_ROWS = (("conv_b", ROW_CONV_B), ("gate_a_b", ROW_BA), ("gate_x_b", ROW_BX), ("lru_lambda", ROW_LAM),
               ("pool_b", ROW_PB), ("pool_scale", ROW_PS), ("norm_lru_g", ROW_GL), ("norm_pool_g", ROW_GP))
_WIDE_ROWS = (("norm_mix_g", ROW_MIX), ("norm_ffn_g", ROW_FFN), ("final_norm_g", ROW_FIN))
_BLOCK_ROWS = (("gate_a_w", ROW_GA), ("gate_x_w", ROW_GX), ("pool_w", ROW_PW))
_SMALL_ORDER = tuple(n for n, _ in _SMALL_ROWS) + tuple(n for n, _ in _WIDE_ROWS) + tuple(
    n for n, _ in _BLOCK_ROWS) + ("conv_w",)


def _adamw_small(slab, wmv):
    names = _SMALL_ORDER
    flat = [a for nme in names for a in wmv[nme]]
    nin = len(flat)

    def body(*refs):
        slab_ref, j_ref = refs[0], refs[1]
        ins = refs[2:2 + nin]
        outs = refs[2 + nin:]
        grads = {}
        for nme, row in _SMALL_ROWS:
            grads[nme] = slab_ref[row:row + 1, :]
        for nme, row in _WIDE_ROWS:
            grads[nme] = jnp.concatenate([slab_ref[row:row + 1, :], slab_ref[row + 1:row + 2, :]], axis=1)
        for nme, row in _BLOCK_ROWS:
            grads[nme] = slab_ref[row:row + wmv[nme][0].shape[0], :]
        full = slab_ref[ROW_CONV_W:ROW_CONV_W + CONV_WIDTH, :]
        jv = j_ref[0]
        g = jnp.zeros((CONV_WIDTH, LANES), F32)
        for jj in range(N_CHIPS):
            g = jnp.where(jv == jj, full[:, jj * LANES:(jj + 1) * LANES], g)
        grads["conv_w"] = g
        for idx, nme in enumerate(names):
            w_ref, m_ref, v_ref = ins[3 * idx:3 * idx + 3]
            g = grads[nme]
            delta, m, v = _adamw_math(w_ref[...], g, m_ref[...], v_ref[...])
            outs[4 * idx][...] = g
            outs[4 * idx + 1][...] = delta
            outs[4 * idx + 2][...] = m
            outs[4 * idx + 3][...] = v

    x, y, _ = _place()
    jidx = jnp.reshape(2 * x + y, (1,)).astype(jnp.int32)
    out_shape = [SDS(wmv[nme][0].shape, F32) for nme in names for _ in range(4)]
    res = pl.pallas_call(
        body, name="adamw_small",
        in_specs=[VMEM_SPEC, pl.BlockSpec(memory_space=pltpu.SMEM)] + [VMEM_SPEC] * nin,
        out_specs=[VMEM_SPEC] * len(out_shape), out_shape=out_shape, compiler_params=_cp())(slab, jidx, *flat)
    return {nme: tuple(res[4 * idx:4 * idx + 4]) for idx, nme in enumerate(names)}


_FFN = ("ffn_w1", "ffn_w3", "ffn_w2")


def _local_step(x, target, full, sp_, distributed):
    d = x.shape[1]
    (u,), got = _inproj(x, sp_["norm_mix_g"], full["w_in"],
                        [_ffn_gather_hosted([full["w_out"]])] if distributed else None)
    w_out = (got[0][0] if distributed else full["w_out"]).reshape(d, d)
    gather = [_ffn_gather_hosted([full[n] for n in _FFN])] if distributed else None
    (h, yn, hres1, saved, pooled), got = _mixer_fwd(u, x, sp_, w_out, gather)
    w1, w3, w2 = got[0] if distributed else [full[n] for n in _FFN]
    h2, a1, a3, ff = _ffn_up(hres1, sp_["norm_ffn_g"], w1, w3)
    dh, dhb, loss8, dg_fin = _ffn_down(ff, hres1, target, sp_["final_norm_g"], w2)
    da1, da3 = _ffn_bwd_gate(dhb, a1, a3, w2)
    dws = list(_ffn_wgrad2(h2, dhb, ff, da1, da3))
    rs1 = [_rs_sibling_hosted(dws)] if distributed else None
    (dhres1, dg_ffn), got = _ffn_bwd_down(da1, da3, dh, hres1, sp_["norm_ffn_g"], w1, w3, rs1)
    rs2 = None
    if distributed:
        pairs = [_add_own_half(a, r, "add_half_" + n) for n, a, r in zip(_FFN, dws, got[0])]
        rs2 = [_rs_chips_hosted([pb for _, pb in pairs])]
    (du, mix_slab), got = _mixer_bwd(u, saved, pooled, h, dhres1, sp_, w_out, rs2)
    (gx, dwin, dwout, dg_mix), _ = _inproj_bwd(x, du, dhres1, yn, sp_["norm_mix_g"], full["w_in"])
    big = {"w_in": dwin, "w_out": dwout.reshape(N_CHIPS, d // N_CHIPS, d)}
    for k, n in enumerate(_FFN):
        big[n] = (pairs[k][0], got[0][k]) if distributed else dws[k]
    return gx, big, (mix_slab, dg_mix, dg_ffn, dg_fin, loss8)


def _to_compact(w):
    h, i, j = w.shape
    return jnp.transpose(w, (1, 0, 2)).reshape(i, h * j)


def _from_compact(w, h):
    i, hj = w.shape
    return jnp.transpose(w.reshape(i, h, hj // h), (1, 0, 2))


_SMALL_LAYOUT = {
    "gate_a_w": (lambda a: _to_compact(a[0]), lambda a: _from_compact(a, 8)[None]),
    "gate_x_w": (lambda a: _to_compact(a[0]), lambda a: _from_compact(a, 8)[None]),
    "pool_w": (lambda a: _to_compact(a[0]), lambda a: _from_compact(a, 4)[None]),
    "conv_w": (lambda a: a[0], lambda a: a[None]),
    "final_norm_g": (lambda a: a[None], lambda a: a[0]),
}

_WEIGHTS = ("norm_mix_g", "w_in", "conv_w", "conv_b", "gate_a_w", "gate_a_b", "gate_x_w", "gate_x_b", "lru_lambda",
            "pool_w", "pool_b", "pool_scale", "norm_lru_g", "norm_pool_g", "w_out", "norm_ffn_g", "ffn_w1",
            "ffn_w3", "ffn_w2", "final_norm_g")


def kernel(x, norm_mix_g, w_in, conv_w, conv_b, gate_a_w, gate_a_b, gate_x_w, gate_x_b, lru_lambda, pool_w, pool_b, pool_scale, norm_lru_g, norm_pool_g, w_out, norm_ffn_g, ffn_w1, ffn_w3, ffn_w2, final_norm_g, loss_target, m_norm_mix_g, m_w_in, m_conv_w, m_conv_b, m_gate_a_w, m_gate_a_b, m_gate_x_w, m_gate_x_b, m_lru_lambda, m_pool_w, m_pool_b, m_pool_scale, m_norm_lru_g, m_norm_pool_g, m_w_out, m_norm_ffn_g, m_ffn_w1, m_ffn_w3, m_ffn_w2, m_final_norm_g, v_norm_mix_g, v_w_in, v_conv_w, v_conv_b, v_gate_a_w, v_gate_a_b, v_gate_x_w, v_gate_x_b, v_lru_lambda, v_pool_w, v_pool_b, v_pool_scale, v_norm_lru_g, v_norm_pool_g, v_w_out, v_norm_ffn_g, v_ffn_w1, v_ffn_w3, v_ffn_w2, v_final_norm_g):
    loc = locals()
    w = {n: loc[n] for n in _WEIGHTS}
    m = {n: loc["m_" + n] for n in _WEIGHTS}
    v = {n: loc["v_" + n] for n in _WEIGHTS}

    def lay(nme, a):
        return _SMALL_LAYOUT[nme][0](a) if nme in _SMALL_LAYOUT else a

    def unlay(nme, a):
        return _SMALL_LAYOUT[nme][1](a) if nme in _SMALL_LAYOUT else a

    gathered = _gather_weights([w[n][0] for n in _BIG], w["conv_w"][0], n_remote=1)
    full = dict(zip(_BIG, gathered[:-1]))
    cw_all = gathered[-1]
    sp_ = {n: lay(n, w[n]) for n in _SMALL_ORDER}
    sp_["conv_w"] = jnp.transpose(cw_all[:, :CONV_WIDTH, :], (1, 0, 2)).reshape(CONV_WIDTH, N_CHIPS * LANES)

    gx, big, small = _local_step(x[0], loss_target[0], full, sp_, distributed=True)

    late = ("w_in", "w_out")
    fin = {n: _sum_chips(big[n][0], big[n][1], "sum_chips_" + n) for n in _FFN}
    recv1, swapped = _run_comm([_rs_sibling_hosted([big[n] for n in late]),
                                _rs_swap_hosted([fin[n] for n in _FFN])], "tail_sibling")
    sib = dict(zip(_FFN, swapped))
    pairs = [_add_own_half(big[n], r, "add_half_" + n) for n, r in zip(late, recv1)]
    recv2, = _run_comm([_rs_chips_hosted([pb for _, pb in pairs])], "tail_chips")
    for n, (p, _), r in zip(late, pairs, recv2):
        fin[n] = _sum_chips(p, r, "sum_chips_" + n)
    swapped, = _run_comm([_rs_swap_hosted([fin[n] for n in late])], "tail_swap")
    sib.update(zip(late, swapped))
    out = {}
    for n in _BIG:
        out[n] = tuple(_adamw_big(w[n], fin[n], sib[n], m[n], v[n], "adamw_" + n))
    slab = _allreduce_small(*small)
    loss = slab[ROW_LOSS, 0]
    wmv = {n: (lay(n, w[n]), lay(n, m[n]), lay(n, v[n])) for n in _SMALL_ORDER}
    res = _adamw_small(slab, wmv)
    for n in _SMALL_ORDER:
        out[n] = tuple(unlay(n, a) for a in res[n])
    return (loss, gx[None]) + tuple(out[n][k] for k in range(4) for n in _WEIGHTS)
```

```python
import functools
import math

import jax
import jax.numpy as jnp
from jax import lax
from jax.experimental import pallas as pl
from jax.experimental.pallas import tpu as pltpu

F32 = jnp.float32
BF16 = jnp.bfloat16
SDS = jax.ShapeDtypeStruct
MESH = pl.DeviceIdType.MESH

EPS = 1e-6
LRU_C = 8.0
CONV_WIDTH = 4
POOL_WINDOWS = (2, 4, 8, 16)
HALO = 16
LANES = 128
SUBLANES = 8
GATE_BLOCK = 256
N_CHIPS = 4

ADAM_LR = 0.001
ADAM_B1 = 0.9
ADAM_B2 = 0.999
ADAM_EPS = 1e-08
ADAM_WD = 0.01
ADAM_STEP = 10

TM_PROJ = 512
TM_MIX = 256
TM_FFN = 512
TM_WGRAD = 1024
TM_FFN_UP = 1024
TM_FFN_DOWN = 512
MIX_SAVED = ("xc", "r", "ig", "a", "m2raw", "ge", "dge")
FFN_ROW_CHUNKS = 2
VMEM_LIMIT = 56 * 1024 * 1024

SLAB_W = 512
ROW_CONV_B, ROW_CONV_W, ROW_BA, ROW_BX, ROW_LAM, ROW_PB, ROW_PS, ROW_GL, ROW_GP = 0, 1, 5, 6, 7, 8, 9, 10, 11
ROW_GA, ROW_GX, ROW_PW = 16, 80, 144
ROW_MIX, ROW_FFN, ROW_FIN, ROW_LOSS = 272, 274, 276, 278
MIX_SLAB_ROWS = 272
SLAB_ROWS = 288


def _cp(sem=None, **kw):
    if sem is not None:
        kw["dimension_semantics"] = sem
    return pltpu.CompilerParams(vmem_limit_bytes=VMEM_LIMIT, **kw)


def _const_spec(shape):
    nd = len(shape)
    return pl.BlockSpec(shape, lambda *_: (0,) * nd, pipeline_mode=pl.Buffered(1))


def _sigmoid(x):
    return 1.0 / (1.0 + jnp.exp(-x))


def _dot(a, b):
    return jnp.dot(a, b, preferred_element_type=F32)


def _dot_nt(a, b):
    return lax.dot_general(a, b, (((1,), (1,)), ((), ())), preferred_element_type=F32)


def _dot_tn(a, b):
    return lax.dot_general(a, b, (((0,), (0,)), ((), ())), preferred_element_type=F32)


def _colsum8(v):
    m, c = v.shape
    return v.reshape(m // SUBLANES, SUBLANES, c).sum(axis=0)


def _rowmean(v):
    return jnp.mean(v, axis=-1, keepdims=True)


def _rms_bwd(dy, xhat, r, g):
    dxh = dy * g
    return r * (dxh - xhat * _rowmean(dxh * xhat))


def _softplus_neg(lam):
    z = -lam
    e = jnp.exp(-jnp.abs(z))
    u = 1.0 + e
    d = u - 1.0
    log1p = jnp.where(d == 0.0, e, jnp.log(u) * (e / jnp.where(d == 0.0, 1.0, d)))
    return jnp.maximum(z, 0.0) + log1p


def _neg_expm1(z):
    series = -(z * (1.0 + z * (0.5 + z * (1.0 / 6.0 + z * (1.0 / 24.0)))))
    return jnp.where(z > -0.03, series, 1.0 - jnp.exp(z))


_GELU_C = math.sqrt(2.0 / math.pi)
_GELU_K = 0.044715


def _gelu_parts(x):
    x2 = x * x
    th = jnp.tanh(_GELU_C * (x + _GELU_K * x2 * x))
    ge = 0.5 * x * (1.0 + th)
    dge = 0.5 * (1.0 + th) + 0.5 * x * (1.0 - th * th) * (_GELU_C * (1.0 + 3.0 * _GELU_K * x2))
    return ge, dge


def _shift_down(halo, tile, k):
    if k == 0:
        return tile
    ext = jnp.concatenate([halo, tile], axis=0)
    n = tile.shape[0]
    h = halo.shape[0]
    return ext[h - k:h - k + n]


def _shift_up(tile, nxt, k):
    if k == 0:
        return tile
    ext = jnp.concatenate([tile, nxt], axis=0)
    return ext[k:k + tile.shape[0]]


def _build_gate_blocks(ga_ref, gx_ref, gw_ref):
    hd = ga_ref.shape[0]
    per = GATE_BLOCK // hd
    lane = lax.broadcasted_iota(jnp.int32, (hd, GATE_BLOCK), 1)
    for b in range(gw_ref.shape[0]):
        for src, off in ((ga_ref, 0), (gx_ref, GATE_BLOCK)):
            blk = src[:, b * GATE_BLOCK:(b + 1) * GATE_BLOCK]
            for hh in range(per):
                m = (lane >= hh * hd) & (lane < (hh + 1) * hd)
                gw_ref[b, hh * hd:(hh + 1) * hd, off:off + GATE_BLOCK] = jnp.where(m, blk, 0.0).astype(BF16)


def _scan_level1(a, b, reverse):
    m, c = a.shape
    a3 = a.reshape(m // SUBLANES, SUBLANES, c)
    b3 = b.reshape(m // SUBLANES, SUBLANES, c)
    row = lax.broadcasted_iota(jnp.int32, a3.shape, 1)
    for s in (1, 2, 4):
        sh = (SUBLANES - s) if reverse else s
        a_sh = pltpu.roll(a3, sh, 1)
        b_sh = pltpu.roll(b3, sh, 1)
        ok = (row < SUBLANES - s) if reverse else (row >= s)
        b3 = jnp.where(ok, a3 * b_sh + b3, b3)
        a3 = jnp.where(ok, a3 * a_sh, a3)
    return a3.reshape(m, c), b3.reshape(m, c)


def _scan_level2(a_ref, b_ref, out_ref, carry, reverse):
    m, c = a_ref.shape
    ng = m // SUBLANES

    def step(g, cr):
        gi = (ng - 1 - g) if reverse else g
        off = pl.multiple_of(gi * SUBLANES, SUBLANES)
        h = b_ref[pl.ds(off, SUBLANES), :] + a_ref[pl.ds(off, SUBLANES), :] * cr
        out_ref[pl.ds(off, SUBLANES), :] = h
        edge = h[0:1, :] if reverse else h[SUBLANES - 1:SUBLANES, :]
        return jnp.broadcast_to(edge, (SUBLANES, c))

    return lax.fori_loop(0, ng, step, carry, unroll=4)


def _mixer_recompute(u, hal, t0, cw, cb, gw_ref, ba, bx, lam, pw_ref, pb, ps):
    tm = u.shape[0]
    lw = cb.shape[1]
    u_l, u_g, u_p = u[:, :lw], u[:, lw:2 * lw], u[:, 2 * lw:]
    hal_l, hal_p = hal[:, :lw], hal[:, 2 * lw:]
    taps = [_shift_down(hal_l, u_l, CONV_WIDTH - 1 - k) for k in range(CONV_WIDTH)]
    xc = cb
    for k in range(CONV_WIDTH):
        xc = xc + taps[k] * cw[k:k + 1, :]
    xcb = xc.astype(BF16)
    nb = lw // GATE_BLOCK
    gs = [_dot(xcb[:, b * GATE_BLOCK:(b + 1) * GATE_BLOCK], gw_ref[b]) for b in range(nb)]
    r = _sigmoid(jnp.concatenate([g[:, :GATE_BLOCK] for g in gs], axis=1) + ba)
    ig = _sigmoid(jnp.concatenate([g[:, GATE_BLOCK:] for g in gs], axis=1) + bx)
    sp = _softplus_neg(lam)
    la = (-LRU_C * r) * sp
    a = jnp.exp(la)
    m2raw = _neg_expm1(2.0 * la)
    mult = jnp.sqrt(jnp.maximum(m2raw, 1e-12))
    ge, dge = _gelu_parts(u_g)
    row = lax.broadcasted_iota(jnp.int32, (tm, LANES), 0) + t0
    pooled, invs, zs = [], [], []
    for gi, w in enumerate(POOL_WINDOWS):
        e = jnp.concatenate([hal_p[:, gi * LANES:(gi + 1) * LANES], u_p[:, gi * LANES:(gi + 1) * LANES]], axis=0)
        s = e
        k = 1
        while k < w:
            s = s + pltpu.roll(s, k, 0)
            k *= 2
        inv = 1.0 / jnp.minimum(row + 1, w).astype(F32)
        pg = s[HALO:] * inv - e[HALO:]
        pooled.append(pg)
        invs.append(inv)
        zs.append(_dot(pg.astype(BF16), pw_ref[:, gi * LANES:(gi + 1) * LANES].astype(BF16)))
    z = jnp.concatenate(zs, axis=1) + pb
    y_pool = z * ps
    return dict(u_l=u_l, u_g=u_g, taps=taps, xc=xc, xcb=xcb, r=r, ig=ig, sp=sp, la=la, a=a, m2raw=m2raw,
                mult=mult, ge=ge, dge=dge, pooled=pooled, invs=invs, z=z, y_pool=y_pool)


ANY = pl.BlockSpec(memory_space=pl.ANY)
VMEM_SPEC = pl.BlockSpec(memory_space=pltpu.VMEM)


class _Hosted:
    def __init__(self, ins, out_shapes, sems, start, finish, mid=None, aliases=None):
        self.ins, self.out_shapes, self.sems = list(ins), list(out_shapes), list(sems)
        self.start, self.mid, self.finish = start, mid, finish
        self.aliases = dict(aliases or {})


def _call(body, hosted, stage_preds, *, name, grid, in_specs, out_specs, out_shape, scratch_shapes, args, sem):
    hosted = list(hosted or [])
    n_in, n_out, n_scr = len(in_specs), len(out_specs), len(scratch_shapes)
    c_in = [a for h in hosted for a in h.ins]
    c_out = [o for h in hosted for o in h.out_shapes]
    c_sem = [pltpu.SemaphoreType.DMA((k,)) for h in hosted for k in h.sems]

    def full(*refs):
        p = 0
        parts = []
        for cnt in (n_in, len(c_in), n_out, len(c_out), n_scr, len(c_sem)):
            parts.append(refs[p:p + cnt])
            p += cnt
        hi, ci, ho, co, hs, cs = parts
        per = []
        a = b = c_ = 0
        for h in hosted:
            per.append((h, ci[a:a + len(h.ins)], co[b:b + len(h.out_shapes)], cs[c_:c_ + len(h.sems)]))
            a, b, c_ = a + len(h.ins), b + len(h.out_shapes), c_ + len(h.sems)
        first = mid = last = None
        if hosted and grid:
            first, mid, last = stage_preds()

        def run(fn, pred, i_, o_, s_):
            if fn is None:
                return
            if pred is None:
                fn(i_, o_, s_)
            else:
                pl.when(pred)(functools.partial(fn, i_, o_, s_))

        for h, i_, o_, s_ in per:
            run(h.start, first, i_, o_, s_)
        body(*hi, *ho, *hs)
        for h, i_, o_, s_ in per:
            run(h.mid, mid, i_, o_, s_)
        for h, i_, o_, s_ in per:
            run(h.finish, last, i_, o_, s_)

    aliases = {}
    a = b = 0
    for h in hosted:
        for k, v in h.aliases.items():
            aliases[n_in + a + k] = n_out + b + v
        a, b = a + len(h.ins), b + len(h.out_shapes)
    res = pl.pallas_call(
        full, name=name, grid=grid, in_specs=list(in_specs) + [ANY] * len(c_in),
        out_specs=list(out_specs) + [ANY] * len(c_out), out_shape=list(out_shape) + c_out,
        scratch_shapes=list(scratch_shapes) + c_sem, input_output_aliases=aliases,
        compiler_params=_cp(sem))(*args, *c_in)
    res = list(res)
    outs = []
    p = n_out
    for h in hosted:
        outs.append(res[p:p + len(h.out_shapes)])
        p += len(h.out_shapes)
    return res[:n_out], outs


def _inproj(x, g_mix, w_in, hosted=None):
    s, d = x.shape
    n = w_in.shape[1]
    tm = min(TM_PROJ, s)
    nt = s // tm

    def body(x_ref, g_ref, w_ref, u_ref):
        xv = x_ref[...]
        r = lax.rsqrt(_rowmean(xv * xv) + EPS)
        u_ref[...] = _dot((xv * r * g_ref[...]).astype(BF16), w_ref[...])

    def stages():
        i = pl.program_id(0)
        return i == 0, i == max(nt - 3, 0), i == nt - 1

    return _call(
        body, hosted, stages, grid=(nt,), name="inproj",
        in_specs=[pl.BlockSpec((tm, d), lambda i: (i, 0)), _const_spec((1, d)), _const_spec((d, n))],
        out_specs=[pl.BlockSpec((tm, n), lambda i: (i, 0))], out_shape=[SDS((s, n), F32)], scratch_shapes=[],
        args=(x, g_mix, w_in), sem=("arbitrary",))


def _mixer_fwd(u, x, sp_, w_out, hosted=None):
    s, din = u.shape
    d = x.shape[1]
    lw = din // 3
    tm = min(TM_MIX, s)
    nb = lw // GATE_BLOCK

    def body(u_ref, halo_ref, x_ref, cw_ref, cb_ref, ga_ref, gx_ref, ba_ref, bx_ref, lam_ref, pw_ref, pb_ref,
             ps_ref, gl_ref, gp_ref, wout_ref, h_ref, yn_ref, hres_ref, saved_ref, pooled_ref,
             gw_s, a_s, b_s, carry_s):
        i = pl.program_id(0)

        @pl.when(i == 0)
        def _():
            _build_gate_blocks(ga_ref, gx_ref, gw_s)
            carry_s[...] = jnp.zeros_like(carry_s)

        uv = u_ref[...]
        hal = jnp.where(i > 0, halo_ref[...], 0.0)
        f = _mixer_recompute(uv, hal, i * tm, cw_ref[...], cb_ref[...], gw_s, ba_ref[...], bx_ref[...],
                             lam_ref[...], pw_ref, pb_ref[...], ps_ref[...])
        for k, name in enumerate(MIX_SAVED):
            saved_ref[k] = f[name]
        pooled_ref[...] = jnp.concatenate(f["pooled"], axis=1).astype(BF16)
        bb = f["mult"] * (f["ig"] * f["xc"])
        a1, b1 = _scan_level1(f["a"], bb, reverse=False)
        a_s[...] = a1
        b_s[...] = b1
        carry_s[...] = _scan_level2(a_s, b_s, h_ref, carry_s[...], reverse=False)
        y_lru = h_ref[...] * f["ge"]
        rl = lax.rsqrt(_rowmean(y_lru * y_lru) + EPS)
        yp = f["y_pool"]
        rp = lax.rsqrt(_rowmean(yp * yp) + EPS)
        yn = jnp.concatenate([y_lru * rl * gl_ref[...], yp * rp * gp_ref[...]], axis=1).astype(BF16)
        yn_ref[...] = yn
        hres_ref[...] = x_ref[...] + _dot(yn, wout_ref[...])

    small = [sp_[k] for k in ("conv_w", "conv_b", "gate_a_w", "gate_x_w", "gate_a_b", "gate_x_b", "lru_lambda",
                              "pool_w", "pool_b", "pool_scale", "norm_lru_g", "norm_pool_g")]
    nt = s // tm

    def stages():
        i = pl.program_id(0)
        return i == 0, i == max(nt - 3, 0), i == nt - 1

    return _call(
        body, hosted, stages, grid=(nt,), name="mixer_fwd",
        in_specs=[pl.BlockSpec((tm, din), lambda i: (i, 0)),
                  pl.BlockSpec((HALO, din), lambda i: (jnp.maximum(i * (tm // HALO) - 1, 0), 0)),
                  pl.BlockSpec((tm, d), lambda i: (i, 0))]
        + [_const_spec(a.shape) for a in small] + [_const_spec(w_out.shape)],
        out_specs=[pl.BlockSpec((tm, lw), lambda i: (i, 0)), pl.BlockSpec((tm, d), lambda i: (i, 0)),
                   pl.BlockSpec((tm, d), lambda i: (i, 0)),
                   pl.BlockSpec((len(MIX_SAVED), tm, lw), lambda i: (0, i, 0)),
                   pl.BlockSpec((tm, lw), lambda i: (i, 0))],
        out_shape=[SDS((s, lw), F32), SDS((s, d), BF16), SDS((s, d), F32), SDS((len(MIX_SAVED), s, lw), F32),
                   SDS((s, lw), BF16)],
        scratch_shapes=[pltpu.VMEM((nb, GATE_BLOCK, 2 * GATE_BLOCK), BF16), pltpu.VMEM((tm, lw), F32),
                        pltpu.VMEM((tm, lw), F32), pltpu.VMEM((SUBLANES, lw), F32)],
        args=(u, u, x, *small, w_out), sem=("arbitrary",))


def _ffn_fwd(hres1, target, g_ffn, g_fin, w1, w3, w2):
    s, d = hres1.shape
    nj, _, fc = w1.shape
    tm = min(TM_FFN, s)

    def body(h_ref, t_ref, gf_ref, gn_ref, w1_ref, w3_ref, w2_ref,
             a1_ref, a3_ref, h2_ref, dh_ref, dhb_ref, loss_ref, dgn_ref, acc_s):
        i, j = pl.program_id(0), pl.program_id(1)

        @pl.when((i == 0) & (j == 0))
        def _():
            loss_ref[...] = jnp.zeros_like(loss_ref)
            dgn_ref[...] = jnp.zeros_like(dgn_ref)

        @pl.when(j == 0)
        def _():
            hv = h_ref[...]
            r = lax.rsqrt(_rowmean(hv * hv) + EPS)
            h2_ref[...] = (hv * r * gf_ref[...]).astype(BF16)

        h2 = h2_ref[...]
        a1 = _dot(h2, w1_ref[0])
        a3 = _dot(h2, w3_ref[0])
        a1_ref[0] = a1.astype(BF16)
        a3_ref[0] = a3.astype(BF16)
        part = _dot(((a1 * _sigmoid(a1)) * a3).astype(BF16), w2_ref[0])

        @pl.when(j == 0)
        def _():
            acc_s[...] = part

        @pl.when(j > 0)
        def _():
            acc_s[...] += part

        @pl.when(j == nj - 1)
        def _():
            hr2 = h_ref[...] + acc_s[...]
            r2 = lax.rsqrt(_rowmean(hr2 * hr2) + EPS)
            xh = hr2 * r2
            gn = gn_ref[...]
            diff = xh * gn - t_ref[...]
            tot = jnp.sum(jnp.sum(diff * diff, axis=1, keepdims=True), axis=0, keepdims=True)
            loss_ref[...] += tot * (0.5 / d)
            dout = diff * (1.0 / d)
            dgn_ref[...] += _colsum8(dout * xh)
            dh = _rms_bwd(dout, xh, r2, gn)
            dh_ref[...] = dh
            dhb_ref[...] = dh.astype(BF16)

    return pl.pallas_call(
        body, grid=(s // tm, nj), name="ffn_fwd",
        in_specs=[pl.BlockSpec((tm, d), lambda i, j: (i, 0)), pl.BlockSpec((tm, d), lambda i, j: (i, 0)),
                  _const_spec((1, d)), _const_spec((1, d)),
                  pl.BlockSpec((1, d, fc), lambda i, j: (j, 0, 0)), pl.BlockSpec((1, d, fc), lambda i, j: (j, 0, 0)),
                  pl.BlockSpec((1, fc, d), lambda i, j: (j, 0, 0))],
        out_specs=[pl.BlockSpec((1, tm, fc), lambda i, j: (j, i, 0)), pl.BlockSpec((1, tm, fc), lambda i, j: (j, i, 0)),
                   pl.BlockSpec((tm, d), lambda i, j: (i, 0)), pl.BlockSpec((tm, d), lambda i, j: (i, 0)),
                   pl.BlockSpec((tm, d), lambda i, j: (i, 0)),
                   pl.BlockSpec((SUBLANES, LANES), lambda i, j: (0, 0)),
                   pl.BlockSpec((SUBLANES, d), lambda i, j: (0, 0))],
        out_shape=[SDS((nj, s, fc), BF16), SDS((nj, s, fc), BF16), SDS((s, d), BF16), SDS((s, d), F32),
                   SDS((s, d), BF16), SDS((SUBLANES, LANES), F32), SDS((SUBLANES, d), F32)],
        scratch_shapes=[pltpu.VMEM((tm, d), F32)],
        compiler_params=_cp(("arbitrary", "arbitrary")))(hres1, target, g_ffn, g_fin, w1, w3, w2)


def _ffn_bwd_act(dh, dhb, a1, a3, hres1, g_ffn, w1, w3, w2):
    s, d = hres1.shape
    nj, _, fc = a1.shape
    tm = min(TM_FFN, s)

    def body(dh_ref, dhb_ref, a1_ref, a3_ref, h_ref, gf_ref, w1_ref, w3_ref, w2_ref,
             da1_ref, da3_ref, dhr_ref, dgf_ref, acc_s):
        i, j = pl.program_id(0), pl.program_id(1)

        @pl.when((i == 0) & (j == 0))
        def _():
            dgf_ref[...] = jnp.zeros_like(dgf_ref)

        @pl.when(j == 0)
        def _():
            acc_s[...] = jnp.zeros_like(acc_s)

        rc = tm // FFN_ROW_CHUNKS
        for q in range(FFN_ROW_CHUNKS):
            rows = slice(q * rc, (q + 1) * rc)
            dff = _dot_nt(dhb_ref[rows, :], w2_ref[0])
            a1v = a1_ref[0, rows, :].astype(F32)
            a3v = a3_ref[0, rows, :].astype(F32)
            sg = _sigmoid(a1v)
            silu = a1v * sg
            da1 = (dff * a3v * (sg * (1.0 + a1v * (1.0 - sg)))).astype(BF16)
            da3 = (dff * silu).astype(BF16)
            da1_ref[0, rows, :] = da1
            da3_ref[0, rows, :] = da3
            acc_s[rows, :] += _dot_nt(da1, w1_ref[0]) + _dot_nt(da3, w3_ref[0])

        @pl.when(j == nj - 1)
        def _():
            hv = h_ref[...]
            r = lax.rsqrt(_rowmean(hv * hv) + EPS)
            xh = hv * r
            dh2 = acc_s[...]
            dgf_ref[...] += _colsum8(dh2 * xh)
            dhr_ref[...] = dh_ref[...] + _rms_bwd(dh2, xh, r, gf_ref[...])

    return pl.pallas_call(
        body, grid=(s // tm, nj), name="ffn_bwd_act",
        in_specs=[pl.BlockSpec((tm, d), lambda i, j: (i, 0)), pl.BlockSpec((tm, d), lambda i, j: (i, 0)),
                  pl.BlockSpec((1, tm, fc), lambda i, j: (j, i, 0)), pl.BlockSpec((1, tm, fc), lambda i, j: (j, i, 0)),
                  pl.BlockSpec((tm, d), lambda i, j: (i, 0)), _const_spec((1, d)),
                  pl.BlockSpec((1, d, fc), lambda i, j: (j, 0, 0)), pl.BlockSpec((1, d, fc), lambda i, j: (j, 0, 0)),
                  pl.BlockSpec((1, fc, d), lambda i, j: (j, 0, 0))],
        out_specs=[pl.BlockSpec((1, tm, fc), lambda i, j: (j, i, 0)), pl.BlockSpec((1, tm, fc), lambda i, j: (j, i, 0)),
                   pl.BlockSpec((tm, d), lambda i, j: (i, 0)), pl.BlockSpec((SUBLANES, d), lambda i, j: (0, 0))],
        out_shape=[SDS((nj, s, fc), BF16), SDS((nj, s, fc), BF16), SDS((s, d), F32), SDS((SUBLANES, d), F32)],
        scratch_shapes=[pltpu.VMEM((tm, d), F32)],
        compiler_params=_cp(("arbitrary", "arbitrary")))(dh, dhb, a1, a3, hres1, g_ffn, w1, w3, w2)


def _ffn_wgrad(h2, dhb, a1, a3, da1, da3):
    s, d = h2.shape
    _, _, fc = a1.shape
    tm = min(TM_WGRAD, s)

    def body(h2_ref, dhb_ref, a1_ref, a3_ref, da1_ref, da3_ref, dw1_ref, dw3_ref, dw2_ref):
        i = pl.program_id(1)

        @pl.when(i == 0)
        def _():
            dw1_ref[...] = jnp.zeros_like(dw1_ref)
            dw3_ref[...] = jnp.zeros_like(dw3_ref)
            dw2_ref[...] = jnp.zeros_like(dw2_ref)

        h2v = h2_ref[...]
        a1v = a1_ref[0].astype(F32)
        ff = ((a1v * _sigmoid(a1v)) * a3_ref[0].astype(F32)).astype(BF16)
        dw1_ref[0] += _dot_tn(h2v, da1_ref[0])
        dw3_ref[0] += _dot_tn(h2v, da3_ref[0])
        dw2_ref[0] += _dot_tn(ff, dhb_ref[...])

    return pl.pallas_call(
        body, grid=(N_CHIPS, s // tm), name="ffn_wgrad",
        in_specs=[pl.BlockSpec((tm, d), lambda j, i: (i, 0)), pl.BlockSpec((tm, d), lambda j, i: (i, 0))]
        + [pl.BlockSpec((1, tm, fc), lambda j, i: (j, i, 0))] * 4,
        out_specs=[pl.BlockSpec((1, d, fc), lambda j, i: (j, 0, 0)), pl.BlockSpec((1, d, fc), lambda j, i: (j, 0, 0)),
                   pl.BlockSpec((1, fc, d), lambda j, i: (j, 0, 0))],
        out_shape=[SDS((N_CHIPS, d, fc), F32), SDS((N_CHIPS, d, fc), F32), SDS((N_CHIPS, fc, d), F32)],
        compiler_params=_cp(("parallel", "arbitrary")))(h2, dhb, a1, a3, da1, da3)


def _row_chunks(tm):
    rc = tm // FFN_ROW_CHUNKS
    return [slice(q * rc, (q + 1) * rc) for q in range(FFN_ROW_CHUNKS)]


def _ffn_up(hres1, g_ffn, w1, w3):
    s, d = hres1.shape
    nj, fc, _ = w1.shape
    tm = min(TM_FFN_UP, s)

    def body(h_ref, gf_ref, w1_ref, w3_ref, h2_ref, a1_ref, a3_ref, ff_ref):
        @pl.when(pl.program_id(1) == 0)
        def _():
            hv = h_ref[...]
            r = lax.rsqrt(_rowmean(hv * hv) + EPS)
            h2_ref[...] = (hv * r * gf_ref[...]).astype(BF16)

        j = pl.program_id(1)
        for rows in _row_chunks(tm):
            h2 = h2_ref[rows, :]
            a1 = _dot_nt(h2, w1_ref[j])
            a3 = _dot_nt(h2, w3_ref[j])
            a1_ref[0, rows, :] = a1.astype(BF16)
            a3_ref[0, rows, :] = a3.astype(BF16)
            ff_ref[0, rows, :] = ((a1 * _sigmoid(a1)) * a3).astype(BF16)

    wspec = _const_spec(w1.shape)
    aspec = pl.BlockSpec((1, tm, fc), lambda i, j: (j, i, 0))
    return pl.pallas_call(
        body, grid=(s // tm, nj), name="ffn_up",
        in_specs=[pl.BlockSpec((tm, d), lambda i, j: (i, 0)), _const_spec((1, d)), wspec, wspec],
        out_specs=[pl.BlockSpec((tm, d), lambda i, j: (i, 0)), aspec, aspec, aspec],
        out_shape=[SDS((s, d), BF16)] + [SDS((nj, s, fc), BF16)] * 3,
        compiler_params=_cp(("parallel", "arbitrary")))(hres1, g_ffn, w1, w3)


def _ffn_down(ff, hres1, target, g_fin, w2):
    s, d = hres1.shape
    nj, _, fc = ff.shape
    tm = min(TM_FFN_DOWN, s)

    def body(ff_ref, h_ref, t_ref, gn_ref, w2_ref, dh_ref, dhb_ref, loss_ref, dgn_ref):
        @pl.when(pl.program_id(0) == 0)
        def _():
            loss_ref[...] = jnp.zeros_like(loss_ref)
            dgn_ref[...] = jnp.zeros_like(dgn_ref)

        gn = gn_ref[...]
        for rows in _row_chunks(tm):
            acc = _dot(ff_ref[0, rows, :], w2_ref[0])
            for j in range(1, nj):
                acc = acc + _dot(ff_ref[j, rows, :], w2_ref[j])
            hr2 = h_ref[rows, :] + acc
            r2 = lax.rsqrt(_rowmean(hr2 * hr2) + EPS)
            xh = hr2 * r2
            diff = xh * gn - t_ref[rows, :]
            tot = jnp.sum(jnp.sum(diff * diff, axis=1, keepdims=True), axis=0, keepdims=True)
            loss_ref[...] += tot * (0.5 / d)
            dout = diff * (1.0 / d)
            dgn_ref[...] += _colsum8(dout * xh)
            dh = _rms_bwd(dout, xh, r2, gn)
            dh_ref[rows, :] = dh
            dhb_ref[rows, :] = dh.astype(BF16)

    tile = pl.BlockSpec((tm, d), lambda i: (i, 0))
    return pl.pallas_call(
        body, grid=(s // tm,), name="ffn_down",
        in_specs=[pl.BlockSpec((nj, tm, fc), lambda i: (0, i, 0)), tile, tile, _const_spec((1, d)),
                  _const_spec(w2.shape)],
        out_specs=[tile, tile, pl.BlockSpec((SUBLANES, LANES), lambda i: (0, 0)),
                   pl.BlockSpec((SUBLANES, d), lambda i: (0, 0))],
        out_shape=[SDS((s, d), F32), SDS((s, d), BF16), SDS((SUBLANES, LANES), F32), SDS((SUBLANES, d), F32)],
        compiler_params=_cp(("arbitrary",)))(ff, hres1, target, g_fin, w2)


def _ffn_bwd_gate(dhb, a1, a3, w2):
    s, d = dhb.shape
    nj, _, fc = a1.shape
    tm = min(TM_FFN_UP, s)

    def body(dhb_ref, a1_ref, a3_ref, w2_ref, da1_ref, da3_ref):
        j = pl.program_id(1)
        for rows in _row_chunks(tm):
            dff = _dot_nt(dhb_ref[rows, :], w2_ref[j])
            a1v = a1_ref[0, rows, :].astype(F32)
            sg = _sigmoid(a1v)
            silu = a1v * sg
            da1_ref[0, rows, :] = (dff * a3_ref[0, rows, :].astype(F32) * (sg * (1.0 + (a1v - silu)))).astype(BF16)
            da3_ref[0, rows, :] = (dff * silu).astype(BF16)

    aspec = pl.BlockSpec((1, tm, fc), lambda i, j: (j, i, 0))
    return pl.pallas_call(
        body, grid=(s // tm, nj), name="ffn_bwd_gate",
        in_specs=[pl.BlockSpec((tm, d), lambda i, j: (i, 0)), aspec, aspec, _const_spec(w2.shape)],
        out_specs=[aspec, aspec], out_shape=[SDS((nj, s, fc), BF16)] * 2,
        compiler_params=_cp(("parallel", "arbitrary")))(dhb, a1, a3, w2)


def _ffn_bwd_down(da1, da3, dh, hres1, g_ffn, w1, w3, hosted=None):
    s, d = hres1.shape
    nj, _, fc = da1.shape
    tm = min(TM_FFN_DOWN, s)
    nt = s // tm

    def body(da1_ref, da3_ref, dh_ref, h_ref, gf_ref, w1_ref, w3_ref, dhr_ref, dgf_ref):
        @pl.when(pl.program_id(0) == 0)
        def _():
            dgf_ref[...] = jnp.zeros_like(dgf_ref)

        gf = gf_ref[...]
        for rows in _row_chunks(tm):
            dh2 = None
            for j in range(nj):
                part = _dot(da1_ref[j, rows, :], w1_ref[j]) + _dot(da3_ref[j, rows, :], w3_ref[j])
                dh2 = part if dh2 is None else dh2 + part
            hv = h_ref[rows, :]
            r = lax.rsqrt(_rowmean(hv * hv) + EPS)
            xh = hv * r
            dgf_ref[...] += _colsum8(dh2 * xh)
            dhr_ref[rows, :] = dh_ref[rows, :] + _rms_bwd(dh2, xh, r, gf)

    tile = pl.BlockSpec((tm, d), lambda i: (i, 0))
    aspec = pl.BlockSpec((nj, tm, fc), lambda i: (0, i, 0))
    wspec = _const_spec(w1.shape)

    def stages():
        i = pl.program_id(0)
        return i == 0, i == max(nt - 2, 0), i == nt - 1

    return _call(
        body, hosted, stages, grid=(nt,), name="ffn_bwd_down",
        in_specs=[aspec, aspec, tile, tile, _const_spec((1, d)), wspec, wspec],
        out_specs=[tile, pl.BlockSpec((SUBLANES, d), lambda i: (0, 0))],
        out_shape=[SDS((s, d), F32), SDS((SUBLANES, d), F32)],
        scratch_shapes=[], args=(da1, da3, dh, hres1, g_ffn, w1, w3), sem=("arbitrary",))


def _ffn_wgrad2(h2, dhb, ff, da1, da3):
    s, d = h2.shape
    _, _, fc = ff.shape
    tm = min(TM_WGRAD, s)

    def body(h2_ref, dhb_ref, ff_ref, da1_ref, da3_ref, dw1_ref, dw3_ref, dw2_ref):
        @pl.when(pl.program_id(1) == 0)
        def _():
            dw1_ref[...] = jnp.zeros_like(dw1_ref)
            dw3_ref[...] = jnp.zeros_like(dw3_ref)
            dw2_ref[...] = jnp.zeros_like(dw2_ref)

        h2v = h2_ref[...]
        dw1_ref[0] += _dot_tn(da1_ref[0], h2v)
        dw3_ref[0] += _dot_tn(da3_ref[0], h2v)
        dw2_ref[0] += _dot_tn(ff_ref[0], dhb_ref[...])

    wspec = pl.BlockSpec((1, fc, d), lambda j, i: (j, 0, 0))
    return pl.pallas_call(
        body, grid=(N_CHIPS, s // tm), name="ffn_wgrad",
        in_specs=[pl.BlockSpec((tm, d), lambda j, i: (i, 0)), pl.BlockSpec((tm, d), lambda j, i: (i, 0))]
        + [pl.BlockSpec((1, tm, fc), lambda j, i: (j, i, 0))] * 3,
        out_specs=[wspec] * 3, out_shape=[SDS((N_CHIPS, fc, d), F32)] * 3,
        compiler_params=_cp(("parallel", "arbitrary")))(h2, dhb, ff, da1, da3)


def _mixer_bwd(u, saved, pooled, h, dhres1, sp_, w_out, hosted=None):
    s, din = u.shape
    d = dhres1.shape[1]
    lw = din // 3
    tm = min(TM_MIX, s)
    nt = s // tm
    nb = lw // GATE_BLOCK
    hd = sp_["gate_a_w"].shape[0]

    def body(ul_ref, saved_ref, pooled_ref, h_ref, hhalo_ref, dhr_ref, cw_ref, cb_ref, ga_ref, gx_ref, ba_ref,
             bx_ref, lam_ref, pw_ref, pb_ref, ps_ref, gl_ref, gp_ref, wout_ref, du_ref, slab_ref,
             gw_s, a_s, b_s, e_s, ecarry_s, dxc_s, q_s, vec_s, cwacc_s, dgw_s, dpw_s):
        i = pl.program_id(0)
        tile = nt - 1 - i

        @pl.when(i == 0)
        def _():
            _build_gate_blocks(ga_ref, gx_ref, gw_s)
            for ref in (ecarry_s, dxc_s, q_s, vec_s, cwacc_s, dgw_s, dpw_s):
                ref[...] = jnp.zeros_like(ref)

        cw = cw_ref[...]
        lam = lam_ref[...]
        ps = ps_ref[...]
        f = {name: saved_ref[k] for k, name in enumerate(MIX_SAVED)}
        f["mult"] = jnp.sqrt(jnp.maximum(f["m2raw"], 1e-12))
        f["sp"] = _softplus_neg(lam)
        f["xcb"] = f["xc"].astype(BF16)
        pooled = pooled_ref[...]
        row = lax.broadcasted_iota(jnp.int32, (tm, LANES), 0) + tile * tm
        f["invs"] = [1.0 / jnp.minimum(row + 1, w).astype(F32) for w in POOL_WINDOWS]
        f["z"] = jnp.concatenate(
            [_dot(pooled[:, g * LANES:(g + 1) * LANES], pw_ref[:, g * LANES:(g + 1) * LANES].astype(BF16))
             for g in range(len(POOL_WINDOWS))], axis=1) + pb_ref[...]
        f["y_pool"] = f["z"] * ps
        u_l = ul_ref[...]
        hv = h_ref[...]
        h_prev = _shift_down(jnp.where(tile > 0, hhalo_ref[...], 0.0), hv, 1)
        y_lru = hv * f["ge"]
        rl = lax.rsqrt(_rowmean(y_lru * y_lru) + EPS)
        yp = f["y_pool"]
        rp = lax.rsqrt(_rowmean(yp * yp) + EPS)
        xh_l = y_lru * rl
        xh_p = yp * rp

        dyn = _dot_nt(dhr_ref[...].astype(BF16), wout_ref[...])
        d_nl, d_np = dyn[:, :lw], dyn[:, lw:]
        vec = {}
        vec[ROW_GL] = _colsum8(d_nl * xh_l)
        vec[ROW_GP] = _colsum8(d_np * xh_p)
        d_ylru = _rms_bwd(d_nl, xh_l, rl, gl_ref[...])
        d_ypool = _rms_bwd(d_np, xh_p, rp, gp_ref[...])

        vec[ROW_PS] = _colsum8(d_ypool * f["z"])
        dz = d_ypool * ps
        vec[ROW_PB] = _colsum8(dz)
        dzb = dz.astype(BF16)
        dup = []
        for gi, w in enumerate(POOL_WINDOWS):
            sl = slice(gi * LANES, (gi + 1) * LANES)
            dpw_s[:, sl] += _dot_tn(pooled[:, sl], dzb[:, sl])
            dpool = _dot_nt(dzb[:, sl], pw_ref[:, sl].astype(BF16))
            q = dpool * f["invs"][gi]
            e = jnp.concatenate([q, q_s[:, sl]], axis=0)
            k = 1
            while k < w:
                e = e + pltpu.roll(e, tm + HALO - k, 0)
                k *= 2
            dup.append(e[:tm] - dpool)
            q_s[:, sl] = q[:HALO]

        d_hout = d_ylru * f["ge"]
        d_ug = d_ylru * hv * f["dge"]
        a = f["a"]
        a1, b1 = _scan_level1(a, a * d_hout, reverse=True)
        a_s[...] = a1
        b_s[...] = b1
        e_next = ecarry_s[...]
        ecarry_s[...] = _scan_level2(a_s, b_s, e_s, e_next, reverse=True)
        sv = d_hout + _shift_up(e_s[...], e_next, 1)
        d_a = sv * h_prev
        mult, ig, xc, r = f["mult"], f["ig"], f["xc"], f["r"]
        d_mult = sv * (ig * xc)
        d_ig = sv * mult * xc
        d_xc = sv * mult * ig
        d_la = d_a * a + jnp.where(f["m2raw"] > 1e-12, d_mult * (-(a * a) / mult), 0.0)
        d_r = d_la * (-LRU_C * f["sp"])
        vec[ROW_LAM] = _colsum8(d_la * (-LRU_C * r))
        d_pr = d_r * r * (1.0 - r)
        d_pi = d_ig * ig * (1.0 - ig)
        vec[ROW_BA] = _colsum8(d_pr)
        vec[ROW_BX] = _colsum8(d_pi)
        dxc_parts = []
        for b in range(nb):
            sl = slice(b * GATE_BLOCK, (b + 1) * GATE_BLOCK)
            rhs = jnp.concatenate([d_pr[:, sl], d_pi[:, sl]], axis=1).astype(BF16)
            dgw_s[b] += _dot_tn(f["xcb"][:, sl], rhs)
            dxc_parts.append(_dot_nt(rhs, gw_s[b]))
        d_xc = d_xc + jnp.concatenate(dxc_parts, axis=1)
        vec[ROW_CONV_B] = _colsum8(d_xc)
        dxc_next = dxc_s[...]
        d_ul = None
        for k in range(CONV_WIDTH):
            ahead = _shift_up(d_xc, dxc_next, CONV_WIDTH - 1 - k)
            cwacc_s[k * SUBLANES:(k + 1) * SUBLANES, :] += _colsum8(ahead * u_l)
            term = ahead * cw[k:k + 1, :]
            d_ul = term if d_ul is None else d_ul + term
        dxc_s[...] = d_xc[:SUBLANES]
        for row, val in vec.items():
            vec_s[row * SUBLANES:(row + 1) * SUBLANES, :] += val
        du_ref[...] = jnp.concatenate([d_ul, d_ug] + dup, axis=1).astype(BF16)

        @pl.when(i == nt - 1)
        def _():
            rows = []
            for row in range(ROW_GA):
                if row in (ROW_CONV_W, ROW_CONV_W + 1, ROW_CONV_W + 2, ROW_CONV_W + 3):
                    k = row - ROW_CONV_W
                    v = jnp.sum(cwacc_s[k * SUBLANES:(k + 1) * SUBLANES, :], axis=0, keepdims=True)
                elif row <= ROW_GP:
                    v = jnp.sum(vec_s[row * SUBLANES:(row + 1) * SUBLANES, :], axis=0, keepdims=True)
                    if row == ROW_LAM:
                        v = v * (-1.0 / (1.0 + jnp.exp(lam)))
                else:
                    v = jnp.zeros((1, lw), F32)
                rows.append(v)
            slab_ref[0:ROW_GA, :] = jnp.concatenate(rows, axis=0)
            lane = lax.broadcasted_iota(jnp.int32, (hd, GATE_BLOCK), 1)
            for b in range(nb):
                for off, row0 in ((0, ROW_GA), (GATE_BLOCK, ROW_GX)):
                    acc = jnp.zeros((hd, GATE_BLOCK), F32)
                    for hh in range(GATE_BLOCK // hd):
                        m = (lane >= hh * hd) & (lane < (hh + 1) * hd)
                        acc = acc + jnp.where(m, dgw_s[b, hh * hd:(hh + 1) * hd, off:off + GATE_BLOCK], 0.0)
                    slab_ref[row0:row0 + hd, b * GATE_BLOCK:(b + 1) * GATE_BLOCK] = acc
            slab_ref[ROW_PW:ROW_PW + LANES, :] = dpw_s[...]

    small = [sp_[k] for k in ("conv_w", "conv_b", "gate_a_w", "gate_x_w", "gate_a_b", "gate_x_b", "lru_lambda",
                              "pool_w", "pool_b", "pool_scale", "norm_lru_g", "norm_pool_g")]
    rev = lambda i: nt - 1 - i

    def stages():
        i = pl.program_id(0)
        return i == 0, i == max(nt - 3, 0), i == nt - 1

    return _call(
        body, hosted, stages, grid=(nt,), name="mixer_bwd",
        in_specs=[pl.BlockSpec((tm, lw), lambda i: (rev(i), 0)),
                  pl.BlockSpec((len(MIX_SAVED), tm, lw), lambda i: (0, rev(i), 0)),
                  pl.BlockSpec((tm, lw), lambda i: (rev(i), 0)),
                  pl.BlockSpec((tm, lw), lambda i: (rev(i), 0)),
                  pl.BlockSpec((SUBLANES, lw), lambda i: (jnp.maximum(rev(i) * (tm // SUBLANES) - 1, 0), 0)),
                  pl.BlockSpec((tm, d), lambda i: (rev(i), 0))]
        + [_const_spec(a.shape) for a in small] + [_const_spec(w_out.shape)],
        out_specs=[pl.BlockSpec((tm, din), lambda i: (rev(i), 0)),
                   pl.BlockSpec((MIX_SLAB_ROWS, SLAB_W), lambda i: (0, 0))],
        out_shape=[SDS((s, din), BF16), SDS((MIX_SLAB_ROWS, SLAB_W), F32)],
        scratch_shapes=[pltpu.VMEM((nb, GATE_BLOCK, 2 * GATE_BLOCK), BF16),
                        pltpu.VMEM((tm, lw), F32), pltpu.VMEM((tm, lw), F32), pltpu.VMEM((tm, lw), F32),
                        pltpu.VMEM((SUBLANES, lw), F32), pltpu.VMEM((SUBLANES, lw), F32),
                        pltpu.VMEM((HALO, lw), F32), pltpu.VMEM((ROW_GA * SUBLANES, lw), F32),
                        pltpu.VMEM((CONV_WIDTH * SUBLANES, lw), F32),
                        pltpu.VMEM((nb, GATE_BLOCK, 2 * GATE_BLOCK), F32), pltpu.VMEM((LANES, lw), F32)],
        args=(u, saved, pooled, h, h, dhres1, *small, w_out), sem=("arbitrary",))


def _inproj_bwd(x, du, dhres1, yn, g_mix, w_in, hosted=None):
    s, d = x.shape
    n = w_in.shape[1]
    nc = n // N_CHIPS
    tm = min(TM_PROJ, s)
    nt = s // tm

    def body(x_ref, du_ref, dhr_ref, yn_ref, g_ref, w_ref, gx_ref, dwin_ref, dwout_ref, dg_ref):
        i = pl.program_id(0)

        @pl.when(i == 0)
        def _():
            dwin_ref[...] = jnp.zeros_like(dwin_ref)
            dwout_ref[...] = jnp.zeros_like(dwout_ref)
            dg_ref[...] = jnp.zeros_like(dg_ref)

        xv = x_ref[...]
        g = g_ref[...]
        r = lax.rsqrt(_rowmean(xv * xv) + EPS)
        xh = xv * r
        h1 = (xh * g).astype(BF16)
        duv = du_ref[...]
        dh1 = _dot_nt(duv, w_ref[...])
        dg_ref[...] += _colsum8(dh1 * xh)
        dhr = dhr_ref[...]
        gx_ref[...] = dhr + _rms_bwd(dh1, xh, r, g)
        for jj in range(N_CHIPS):
            dwin_ref[jj] += _dot_tn(h1, duv[:, jj * nc:(jj + 1) * nc])
        dwout_ref[...] += _dot_tn(yn_ref[...], dhr.astype(BF16))

    def stages():
        i = pl.program_id(0)
        return i == 0, i == max(nt - 3, 0), i == nt - 1

    return _call(
        body, hosted, stages, grid=(nt,), name="inproj_bwd",
        in_specs=[pl.BlockSpec((tm, d), lambda i: (i, 0)), pl.BlockSpec((tm, n), lambda i: (i, 0)),
                  pl.BlockSpec((tm, d), lambda i: (i, 0)), pl.BlockSpec((tm, d), lambda i: (i, 0)),
                  _const_spec((1, d)), _const_spec((d, n))],
        out_specs=[pl.BlockSpec((tm, d), lambda i: (i, 0)), pl.BlockSpec((N_CHIPS, d, nc), lambda i: (0, 0, 0)),
                   pl.BlockSpec((d, d), lambda i: (0, 0)), pl.BlockSpec((SUBLANES, d), lambda i: (0, 0))],
        out_shape=[SDS((s, d), F32), SDS((N_CHIPS, d, nc), F32), SDS((d, d), F32), SDS((SUBLANES, d), F32)],
        scratch_shapes=[], args=(x, du, dhres1, yn, g_mix, w_in), sem=("arbitrary",))


def _place():
    x, y, c = lax.axis_index("x"), lax.axis_index("y"), lax.axis_index("c")
    return x, y, c


def _other_chips(x, y):
    return [(1 - x, y), (x, 1 - y), (1 - x, 1 - y)]


ANY = pl.BlockSpec(memory_space=pl.ANY)
VMEM_SPEC = pl.BlockSpec(memory_space=pltpu.VMEM)

_GATHERED = {"w_in": "cols", "w_out": "major", "ffn_w1": "major", "ffn_w3": "major", "ffn_w2": "major"}
_BIG = ("w_in", "w_out", "ffn_w1", "ffn_w3", "ffn_w2")


def _gather_weights(shards, conv_w, n_remote):
    n = len(shards)
    full_shapes = []
    for name, sh in zip(_BIG, shards):
        r, cdim = sh.shape
        if _GATHERED[name] == "cols":
            assert cdim % LANES == 0
            full_shapes.append((r, cdim * N_CHIPS))
        else:
            full_shapes.append((N_CHIPS, r, cdim))

    def region(ref, name, sh, jj, cc):
        r, cdim = sh
        rows = pl.ds(0, r) if cc is None else pl.ds(pl.multiple_of(cc * (r // 2), 16), r // 2)
        if _GATHERED[name] == "cols":
            return ref.at[rows, pl.ds(pl.multiple_of(jj * cdim, LANES), cdim)]
        return ref.at[jj, rows, :]

    def staged(ref, sh, cc):
        r = sh[0]
        return ref.at[pl.ds(pl.multiple_of(cc * (r // 2), 16), r // 2), :]

    def body(*refs):
        ins, cw_in = refs[:n], refs[n]
        outs, cw_out = refs[n + 1:2 * n + 1], refs[2 * n + 1]
        stage = refs[2 * n + 2:3 * n + 2]
        cw_stage, lsem, ssem, rsem, fssem, frsem, cssem, crsem = refs[3 * n + 2:]
        x, y, c = _place()
        j = 2 * x + y
        chips = _other_chips(x, y)
        for w in range(n):
            stage[w][...] = ins[w][...].astype(BF16)
        cw_stage[...] = jnp.zeros_like(cw_stage)
        cw_stage[0:CONV_WIDTH, :] = cw_in[...]
        shs = [s_.shape for s_ in shards]
        local = [pltpu.make_async_copy(stage[w], region(outs[w], _BIG[w], shs[w], j, None), lsem.at[w])
                 for w in range(n)]
        local.append(pltpu.make_async_copy(cw_stage, cw_out.at[j], lsem.at[n]))
        for cp in local:
            cp.start()
        sends = []
        for k, (px, py) in enumerate(chips):
            for w in range(n_remote):
                sends.append(pltpu.make_async_remote_copy(
                    src_ref=staged(stage[w], shs[w], c), dst_ref=region(outs[w], _BIG[w], shs[w], j, c),
                    send_sem=ssem.at[k * n + w], recv_sem=rsem.at[k * n + w], device_id=(px, py, c),
                    device_id_type=MESH))
            sends.append(pltpu.make_async_remote_copy(
                src_ref=cw_stage, dst_ref=cw_out.at[j], send_sem=cssem.at[k], recv_sem=crsem.at[k],
                device_id=(px, py, c), device_id_type=MESH))
        for cp in sends:
            cp.start()
        fwd = []
        for k, (px, py) in enumerate(chips):
            jk = 2 * px + py
            for w in range(n_remote):
                reg = region(outs[w], _BIG[w], shs[w], jk, c)
                pltpu.make_async_remote_copy(src_ref=reg, dst_ref=reg, send_sem=ssem.at[k * n + w],
                                             recv_sem=rsem.at[k * n + w], device_id=(px, py, c),
                                             device_id_type=MESH).wait_recv()
                cp = pltpu.make_async_remote_copy(src_ref=reg, dst_ref=reg, send_sem=fssem.at[k * n + w],
                                                  recv_sem=frsem.at[k * n + w], device_id=(x, y, 1 - c),
                                                  device_id_type=MESH)
                cp.start()
                fwd.append(cp)
            pltpu.make_async_remote_copy(src_ref=cw_stage, dst_ref=cw_out.at[jk], send_sem=cssem.at[k],
                                         recv_sem=crsem.at[k], device_id=(px, py, c),
                                         device_id_type=MESH).wait_recv()
        for k, (px, py) in enumerate(chips):
            jk = 2 * px + py
            for w in range(n_remote):
                reg = region(outs[w], _BIG[w], shs[w], jk, 1 - c)
                pltpu.make_async_remote_copy(src_ref=reg, dst_ref=reg, send_sem=fssem.at[k * n + w],
                                             recv_sem=frsem.at[k * n + w], device_id=(x, y, 1 - c),
                                             device_id_type=MESH).wait_recv()
        for cp in sends + fwd:
            cp.wait_send()
        for cp in local:
            cp.wait()

    nsem = 3 * n
    return pl.pallas_call(
        body, name="gather_first",
        in_specs=[VMEM_SPEC] * (n + 1), out_specs=[ANY] * (n + 1),
        out_shape=[SDS(fs, BF16) for fs in full_shapes] + [SDS((N_CHIPS, SUBLANES, LANES), F32)],
        scratch_shapes=[pltpu.VMEM(s_.shape, BF16) for s_ in shards] + [pltpu.VMEM((SUBLANES, LANES), F32)]
        + [pltpu.SemaphoreType.DMA((n + 1,))] + [pltpu.SemaphoreType.DMA((nsem,))] * 4
        + [pltpu.SemaphoreType.DMA((3,))] * 2,
        compiler_params=_cp())(*shards, conv_w)


def _start_all(make):
    def f(ins, outs, sems):
        for cp in make(ins, outs, sems):
            cp.start()
    return f


def _wait_all(make):
    def f(ins, outs, sems):
        for cp in make(ins, outs, sems):
            cp.wait()
    return f


def _ffn_gather_hosted(arrs):
    n = len(arrs)

    def make(outs, sems):
        ssem, rsem, fs, fr = sems
        x, y, c = _place()
        j = 2 * x + y

        def reg(w, jj, cc):
            hr = arrs[w].shape[1] // 2
            return outs[w].at[jj, pl.ds(pl.multiple_of(cc * hr, 16), hr), :]

        def rc(w, jj, cc, s_sem, r_sem, dev):
            return pltpu.make_async_remote_copy(src_ref=reg(w, jj, cc), dst_ref=reg(w, jj, cc), send_sem=s_sem,
                                                recv_sem=r_sem, device_id=dev, device_id_type=MESH)

        sends, recvs, fwds, frecvs = [], [], [], []
        for k, (px, py) in enumerate(_other_chips(x, y)):
            jk = 2 * px + py
            for w in range(n):
                q = k * n + w
                sends.append(rc(w, j, c, ssem.at[q], rsem.at[q], (px, py, c)))
                recvs.append(rc(w, jk, c, ssem.at[q], rsem.at[q], (px, py, c)))
                fwds.append(rc(w, jk, c, fs.at[q], fr.at[q], (x, y, 1 - c)))
                frecvs.append(rc(w, jk, 1 - c, fs.at[q], fr.at[q], (x, y, 1 - c)))
        return sends, recvs, fwds, frecvs

    def start(ins, outs, sems):
        for cp in make(outs, sems)[0]:
            cp.start()

    def mid(ins, outs, sems):
        _, recvs, fwds, _ = make(outs, sems)
        for r, f in zip(recvs, fwds):
            r.wait_recv()
            f.start()

    def finish(ins, outs, sems):
        sends, _, fwds, frecvs = make(outs, sems)
        for r in frecvs:
            r.wait_recv()
        for cp in sends + fwds:
            cp.wait_send()

    return _Hosted(arrs, [SDS(a.shape, a.dtype) for a in arrs], [3 * n] * 4, start, finish, mid=mid,
                   aliases={w: w for w in range(n)})


def _rs_sibling_hosted(arrs):
    n = len(arrs)

    def make(ins, outs, sems):
        x, y, c = _place()
        cps = []
        for w in range(n):
            hr = arrs[w].shape[1] // 2
            src = ins[w].at[:, pl.ds(pl.multiple_of((1 - c) * hr, SUBLANES), hr), :]
            cps.append(pltpu.make_async_remote_copy(src_ref=src, dst_ref=outs[w], send_sem=sems[0].at[w],
                                                    recv_sem=sems[1].at[w], device_id=(x, y, 1 - c),
                                                    device_id_type=MESH))
        return cps

    return _Hosted(arrs, [SDS((a.shape[0], a.shape[1] // 2, a.shape[2]), F32) for a in arrs], [n, n],
                   _start_all(make), _wait_all(make))


def _rs_chips_hosted(parts):
    n = len(parts)

    def make(ins, outs, sems):
        x, y, c = _place()
        j = 2 * x + y
        cps = []
        for k, (px, py) in enumerate(_other_chips(x, y)):
            jk = 2 * px + py
            for w in range(n):
                cps.append(pltpu.make_async_remote_copy(
                    src_ref=ins[w].at[jk], dst_ref=outs[w].at[j], send_sem=sems[0].at[k * n + w],
                    recv_sem=sems[1].at[k * n + w], device_id=(px, py, c), device_id_type=MESH))
        return cps

    return _Hosted(parts, [SDS(p.shape, p.dtype) for p in parts], [3 * n, 3 * n], _start_all(make), _wait_all(make))


def _rs_swap_hosted(halves):
    n = len(halves)

    def make(ins, outs, sems):
        x, y, c = _place()
        return [pltpu.make_async_remote_copy(src_ref=ins[w], dst_ref=outs[w], send_sem=sems[0].at[w],
                                             recv_sem=sems[1].at[w], device_id=(x, y, 1 - c), device_id_type=MESH)
                for w in range(n)]

    return _Hosted(halves, [SDS(h.shape, F32) for h in halves], [n, n], _start_all(make), _wait_all(make))


def _run_comm(hosted, name):
    return _call(lambda: None, hosted, None, name=name, grid=(), in_specs=[], out_specs=[], out_shape=[],
                 scratch_shapes=[], args=(), sem=None)[1]


def _row_tile(rows, cols, n_arrays):
    budget = 24 * 1024 * 1024 // (2 * 4 * n_arrays * cols)
    best = SUBLANES
    for t in range(SUBLANES, rows + 1, SUBLANES):
        if rows % t == 0 and t <= budget:
            best = t
    return best


def _place_index(which):
    x, y, c = _place()
    v = c if which == "c" else 2 * x + y
    return jnp.reshape(v, (1,)).astype(jnp.int32)


def _add_own_half(full, recv, name):
    nsh, rows, cols = full.shape
    hr = rows // 2
    t = _row_tile(hr, cols, 4)
    nt = hr // t

    def body(c_ref, a_ref, b_ref, o_ref, ob_ref):
        v = a_ref[...] + b_ref[...]
        o_ref[...] = v
        ob_ref[...] = v.astype(BF16)

    half = pl.BlockSpec((1, t, cols), lambda s_, i, c_ref: (s_, i, 0))
    return pl.pallas_call(
        body, name=name,
        grid_spec=pltpu.PrefetchScalarGridSpec(
            num_scalar_prefetch=1, grid=(nsh, nt),
            in_specs=[pl.BlockSpec((1, t, cols), lambda s_, i, c_ref: (s_, c_ref[0] * nt + i, 0)), half],
            out_specs=[half, half]),
        out_shape=[SDS((nsh, hr, cols), F32), SDS((nsh, hr, cols), BF16)],
        compiler_params=_cp(("parallel", "parallel")))(_place_index("c"), full, recv)


def _sum_chips(own, recv, name):
    nsh, hr, cols = own.shape
    t = _row_tile(hr, cols, 6)

    def body(j_ref, own_ref, *rest):
        r_refs, o_ref = rest[:nsh], rest[nsh]
        j = j_ref[0]
        mine = own_ref[0]
        parts = [jnp.where(j == k, mine, r_refs[k][0].astype(F32)) for k in range(nsh)]
        o_ref[...] = ((parts[0] + parts[1]) + parts[2]) + parts[3]

    def other(k):
        return pl.BlockSpec((1, t, cols), lambda i, j_ref: (jnp.where(j_ref[0] == k, (k + 1) % nsh, k), i, 0))

    return pl.pallas_call(
        body, name=name,
        grid_spec=pltpu.PrefetchScalarGridSpec(
            num_scalar_prefetch=1, grid=(hr // t,),
            in_specs=[pl.BlockSpec((1, t, cols), lambda i, j_ref: (j_ref[0], i, 0))]
            + [other(k) for k in range(nsh)],
            out_specs=pl.BlockSpec((t, cols), lambda i, j_ref: (i, 0))),
        out_shape=SDS((hr, cols), F32), compiler_params=_cp(("parallel",)))(_place_index("j"), own, *([recv] * nsh))


def _adamw_math(w, g, m, v):
    m = ADAM_B1 * m + (1.0 - ADAM_B1) * g
    v = ADAM_B2 * v + (1.0 - ADAM_B2) * (g * g)
    m_hat = m / (1.0 - ADAM_B1 ** ADAM_STEP)
    v_hat = v / (1.0 - ADAM_B2 ** ADAM_STEP)
    delta = -ADAM_LR * (m_hat / (jnp.sqrt(v_hat) + ADAM_EPS) + ADAM_WD * w)
    return delta, m, v


def _adamw_big(w, g_own, g_sib, m, v, name):
    _, rows, cols = w.shape
    hr = rows // 2
    t = _row_tile(hr, cols, 9)
    nth = hr // t

    def body(c_ref, w_ref, go_ref, gs_ref, m_ref, v_ref, g_ref, d_ref, mo_ref, vo_ref):
        own = (pl.program_id(0) // nth) == c_ref[0]
        g = jnp.where(own, go_ref[...], gs_ref[...])
        g_ref[0] = g
        d_ref[0], mo_ref[0], vo_ref[0] = _adamw_math(w_ref[0], g, m_ref[0], v_ref[0])

    spec = pl.BlockSpec((1, t, cols), lambda i, c_ref: (0, i, 0))
    hspec = pl.BlockSpec((t, cols), lambda i, c_ref: (i % nth, 0))
    return pl.pallas_call(
        body, name=name,
        grid_spec=pltpu.PrefetchScalarGridSpec(
            num_scalar_prefetch=1, grid=(2 * nth,), in_specs=[spec, hspec, hspec, spec, spec],
            out_specs=[spec] * 4),
        out_shape=[SDS((1, rows, cols), F32)] * 4,
        compiler_params=_cp(("parallel",)))(_place_index("c"), w, g_own, g_sib, m, v)


def _allreduce_small(mix_slab, dg_mix, dg_ffn, dg_fin, loss8):
    half = SLAB_ROWS // 2

    def body(ms_ref, gm_ref, gf_ref, gn_ref, loss_ref, out_ref, loc_s, sib_s, chip_s, r2_s, fin_s, sems):
        x, y, c = _place()
        j = 2 * x + y
        rows = []
        for ref in (gm_ref, gf_ref, gn_ref):
            v = jnp.sum(ref[...], axis=0, keepdims=True)
            rows += [v[:, :SLAB_W], v[:, SLAB_W:]]
        rows.append(jnp.concatenate([loss_ref[0:1, :]] * (SLAB_W // LANES), axis=1))
        rows.append(jnp.zeros((SLAB_ROWS - ROW_LOSS - 1, SLAB_W), F32))
        loc_s[0:MIX_SLAB_ROWS, :] = ms_ref[...]
        loc_s[MIX_SLAB_ROWS:SLAB_ROWS, :] = jnp.concatenate(rows, axis=0)
        sib = (x, y, 1 - c)
        cp = pltpu.make_async_remote_copy(src_ref=loc_s, dst_ref=sib_s, send_sem=sems.at[0], recv_sem=sems.at[1],
                                          device_id=sib, device_id_type=MESH)
        cp.start()
        cp.wait()
        chip_s[...] = loc_s[...] + sib_s[...]
        mine = chip_s.at[pl.ds(pl.multiple_of(c * half, SUBLANES), half), :]
        r2_s[j] = chip_s[pl.ds(pl.multiple_of(c * half, SUBLANES), half), :]
        cps = []
        for k, (px, py) in enumerate(_other_chips(x, y)):
            cps.append(pltpu.make_async_remote_copy(src_ref=mine, dst_ref=r2_s.at[j], send_sem=sems.at[2 + k],
                                                    recv_sem=sems.at[5 + k], device_id=(px, py, c),
                                                    device_id_type=MESH))
        for cp in cps:
            cp.start()
        for cp in cps:
            cp.wait()
        fin_s[...] = ((r2_s[0] + r2_s[1]) + r2_s[2]) + r2_s[3]
        dst = out_ref.at[pl.ds(pl.multiple_of(c * half, SUBLANES), half), :]
        out_ref[pl.ds(pl.multiple_of(c * half, SUBLANES), half), :] = fin_s[...]
        cp = pltpu.make_async_remote_copy(src_ref=fin_s, dst_ref=dst, send_sem=sems.at[8], recv_sem=sems.at[9],
                                          device_id=sib, device_id_type=MESH)
        cp.start()
        cp.wait()

    return pl.pallas_call(
        body, name="allreduce_small", in_specs=[VMEM_SPEC] * 5, out_specs=VMEM_SPEC,
        out_shape=SDS((SLAB_ROWS, SLAB_W), F32),
        scratch_shapes=[pltpu.VMEM((SLAB_ROWS, SLAB_W), F32)] * 3 + [pltpu.VMEM((N_CHIPS, half, SLAB_W), F32),
                                                                       pltpu.VMEM((half, SLAB_W), F32),
                                                                       pltpu.SemaphoreType.DMA((10,))],
        compiler_params=_cp())(mix_slab, dg_mix, dg_ffn, dg_fin, loss8)


_SMALL_ROWS = (("conv_b", ROW_CONV_B), ("gate_a_b", ROW_BA), ("gate_x_b", ROW_BX), ("lru_lambda", ROW_LAM),
               ("pool_b", ROW_PB), ("pool_scale", ROW_PS), ("norm_lru_g", ROW_GL), ("norm_pool_g", ROW_GP))
_WIDE_ROWS = (("norm_mix_g", ROW_MIX), ("norm_ffn_g", ROW_FFN), ("final_norm_g", ROW_FIN))
_BLOCK_ROWS = (("gate_a_w", ROW_GA), ("gate_x_w", ROW_GX), ("pool_w", ROW_PW))
_SMALL_ORDER = tuple(n for n, _ in _SMALL_ROWS) + tuple(n for n, _ in _WIDE_ROWS) + tuple(
    n for n, _ in _BLOCK_ROWS) + ("conv_w",)


def _adamw_small(slab, wmv):
    names = _SMALL_ORDER
    flat = [a for nme in names for a in wmv[nme]]
    nin = len(flat)

    def body(*refs):
        slab_ref, j_ref = refs[0], refs[1]
        ins = refs[2:2 + nin]
        outs = refs[2 + nin:]
        grads = {}
        for nme, row in _SMALL_ROWS:
            grads[nme] = slab_ref[row:row + 1, :]
        for nme, row in _WIDE_ROWS:
            grads[nme] = jnp.concatenate([slab_ref[row:row + 1, :], slab_ref[row + 1:row + 2, :]], axis=1)
        for nme, row in _BLOCK_ROWS:
            grads[nme] = slab_ref[row:row + wmv[nme][0].shape[0], :]
        full = slab_ref[ROW_CONV_W:ROW_CONV_W + CONV_WIDTH, :]
        jv = j_ref[0]
        g = jnp.zeros((CONV_WIDTH, LANES), F32)
        for jj in range(N_CHIPS):
            g = jnp.where(jv == jj, full[:, jj * LANES:(jj + 1) * LANES], g)
        grads["conv_w"] = g
        for idx, nme in enumerate(names):
            w_ref, m_ref, v_ref = ins[3 * idx:3 * idx + 3]
            g = grads[nme]
            delta, m, v = _adamw_math(w_ref[...], g, m_ref[...], v_ref[...])
            outs[4 * idx][...] = g
            outs[4 * idx + 1][...] = delta
            outs[4 * idx + 2][...] = m
            outs[4 * idx + 3][...] = v

    x, y, _ = _place()
    jidx = jnp.reshape(2 * x + y, (1,)).astype(jnp.int32)
    out_shape = [SDS(wmv[nme][0].shape, F32) for nme in names for _ in range(4)]
    res = pl.pallas_call(
        body, name="adamw_small",
        in_specs=[VMEM_SPEC, pl.BlockSpec(memory_space=pltpu.SMEM)] + [VMEM_SPEC] * nin,
        out_specs=[VMEM_SPEC] * len(out_shape), out_shape=out_shape, compiler_params=_cp())(slab, jidx, *flat)
    return {nme: tuple(res[4 * idx:4 * idx + 4]) for idx, nme in enumerate(names)}


_FFN = ("ffn_w1", "ffn_w3", "ffn_w2")
_TRANSPOSED = ("ffn_w1", "ffn_w3")


def _local_step(x, target, full, sp_, distributed):
    d = x.shape[1]
    (u,), got = _inproj(x, sp_["norm_mix_g"], full["w_in"],
                        [_ffn_gather_hosted([full["w_out"]])] if distributed else None)
    w_out = (got[0][0] if distributed else full["w_out"]).reshape(d, d)
    gather = [_ffn_gather_hosted([full[n] for n in _FFN])] if distributed else None
    (h, yn, hres1, saved, pooled), got = _mixer_fwd(u, x, sp_, w_out, gather)
    w1, w3, w2 = got[0] if distributed else [full[n] for n in _FFN]
    h2, a1, a3, ff = _ffn_up(hres1, sp_["norm_ffn_g"], w1, w3)
    dh, dhb, loss8, dg_fin = _ffn_down(ff, hres1, target, sp_["final_norm_g"], w2)
    da1, da3 = _ffn_bwd_gate(dhb, a1, a3, w2)
    dws = list(_ffn_wgrad2(h2, dhb, ff, da1, da3))
    rs1 = [_rs_sibling_hosted(dws)] if distributed else None
    (dhres1, dg_ffn), got = _ffn_bwd_down(da1, da3, dh, hres1, sp_["norm_ffn_g"], w1, w3, rs1)
    rs2 = None
    if distributed:
        pairs = [_add_own_half(a, r, "add_half_" + n) for n, a, r in zip(_FFN, dws, got[0])]
        rs2 = [_rs_chips_hosted([pb for _, pb in pairs])]
    (du, mix_slab), got = _mixer_bwd(u, saved, pooled, h, dhres1, sp_, w_out, rs2)
    (gx, dwin, dwout, dg_mix), _ = _inproj_bwd(x, du, dhres1, yn, sp_["norm_mix_g"], full["w_in"])
    big = {"w_in": dwin, "w_out": dwout.reshape(N_CHIPS, d // N_CHIPS, d)}
    for k, n in enumerate(_FFN):
        big[n] = (pairs[k][0], got[0][k]) if distributed else dws[k]
    return gx, big, (mix_slab, dg_mix, dg_ffn, dg_fin, loss8)


def _to_compact(w):
    h, i, j = w.shape
    return jnp.transpose(w, (1, 0, 2)).reshape(i, h * j)


def _from_compact(w, h):
    i, hj = w.shape
    return jnp.transpose(w.reshape(i, h, hj // h), (1, 0, 2))


_SMALL_LAYOUT = {
    "gate_a_w": (lambda a: _to_compact(a[0]), lambda a: _from_compact(a, 8)[None]),
    "gate_x_w": (lambda a: _to_compact(a[0]), lambda a: _from_compact(a, 8)[None]),
    "pool_w": (lambda a: _to_compact(a[0]), lambda a: _from_compact(a, 4)[None]),
    "conv_w": (lambda a: a[0], lambda a: a[None]),
    "final_norm_g": (lambda a: a[None], lambda a: a[0]),
}

_WEIGHTS = ("norm_mix_g", "w_in", "conv_w", "conv_b", "gate_a_w", "gate_a_b", "gate_x_w", "gate_x_b", "lru_lambda",
            "pool_w", "pool_b", "pool_scale", "norm_lru_g", "norm_pool_g", "w_out", "norm_ffn_g", "ffn_w1",
            "ffn_w3", "ffn_w2", "final_norm_g")


def kernel(x, norm_mix_g, w_in, conv_w, conv_b, gate_a_w, gate_a_b, gate_x_w, gate_x_b, lru_lambda, pool_w, pool_b, pool_scale, norm_lru_g, norm_pool_g, w_out, norm_ffn_g, ffn_w1, ffn_w3, ffn_w2, final_norm_g, loss_target, m_norm_mix_g, m_w_in, m_conv_w, m_conv_b, m_gate_a_w, m_gate_a_b, m_gate_x_w, m_gate_x_b, m_lru_lambda, m_pool_w, m_pool_b, m_pool_scale, m_norm_lru_g, m_norm_pool_g, m_w_out, m_norm_ffn_g, m_ffn_w1, m_ffn_w3, m_ffn_w2, m_final_norm_g, v_norm_mix_g, v_w_in, v_conv_w, v_conv_b, v_gate_a_w, v_gate_a_b, v_gate_x_w, v_gate_x_b, v_lru_lambda, v_pool_w, v_pool_b, v_pool_scale, v_norm_lru_g, v_norm_pool_g, v_w_out, v_norm_ffn_g, v_ffn_w1, v_ffn_w3, v_ffn_w2, v_final_norm_g):
    loc = locals()
    w = {n: loc[n] for n in _WEIGHTS}
    m = {n: loc["m_" + n] for n in _WEIGHTS}
    v = {n: loc["v_" + n] for n in _WEIGHTS}

    def lay(nme, a):
        return _SMALL_LAYOUT[nme][0](a) if nme in _SMALL_LAYOUT else a

    def unlay(nme, a):
        return _SMALL_LAYOUT[nme][1](a) if nme in _SMALL_LAYOUT else a

    for group in (w, m, v):
        for n in _TRANSPOSED:
            group[n] = jnp.transpose(group[n], (0, 2, 1))

    gathered = _gather_weights([w[n][0] for n in _BIG], w["conv_w"][0], n_remote=1)
    full = dict(zip(_BIG, gathered[:-1]))
    cw_all = gathered[-1]
    sp_ = {n: lay(n, w[n]) for n in _SMALL_ORDER}
    sp_["conv_w"] = jnp.transpose(cw_all[:, :CONV_WIDTH, :], (1, 0, 2)).reshape(CONV_WIDTH, N_CHIPS * LANES)

    gx, big, small = _local_step(x[0], loss_target[0], full, sp_, distributed=True)

    late = ("w_in", "w_out")
    fin = {n: _sum_chips(big[n][0], big[n][1], "sum_chips_" + n) for n in _FFN}
    recv1, swapped = _run_comm([_rs_sibling_hosted([big[n] for n in late]),
                                _rs_swap_hosted([fin[n] for n in _FFN])], "tail_sibling")
    sib = dict(zip(_FFN, swapped))
    pairs = [_add_own_half(big[n], r, "add_half_" + n) for n, r in zip(late, recv1)]
    recv2, = _run_comm([_rs_chips_hosted([pb for _, pb in pairs])], "tail_chips")
    for n, (p, _), r in zip(late, pairs, recv2):
        fin[n] = _sum_chips(p, r, "sum_chips_" + n)
    swapped, = _run_comm([_rs_swap_hosted([fin[n] for n in late])], "tail_swap")
    sib.update(zip(late, swapped))
    out = {}
    for n in _BIG:
        out[n] = tuple(_adamw_big(w[n], fin[n], sib[n], m[n], v[n], "adamw_" + n))
        if n in _TRANSPOSED:
            out[n] = tuple(jnp.transpose(a, (0, 2, 1)) for a in out[n])
    slab = _allreduce_small(*small)
    loss = slab[ROW_LOSS, 0]
    wmv = {n: (lay(n, w[n]), lay(n, m[n]), lay(n, v[n])) for n in _SMALL_ORDER}
    res = _adamw_small(slab, wmv)
    for n in _SMALL_ORDER:
        out[n] = tuple(unlay(n, a) for a in res[n])
    return (loss, gx[None]) + tuple(out[n][k] for k in range(4) for n in _WEIGHTS)
```

```python
import functools
import math

import jax
import jax.numpy as jnp
from jax import lax
from jax.experimental import pallas as pl
from jax.experimental.pallas import tpu as pltpu

F32 = jnp.float32
BF16 = jnp.bfloat16
SDS = jax.ShapeDtypeStruct
MESH = pl.DeviceIdType.MESH

EPS = 1e-6
LRU_C = 8.0
CONV_WIDTH = 4
POOL_WINDOWS = (2, 4, 8, 16)
HALO = 16
LANES = 128
SUBLANES = 8
GATE_BLOCK = 256
N_CHIPS = 4

ADAM_LR = 0.001
ADAM_B1 = 0.9
ADAM_B2 = 0.999
ADAM_EPS = 1e-08
ADAM_WD = 0.01
ADAM_STEP = 10

TM_PROJ = 512
TM_MIX = 256
TM_FFN = 512
TM_WGRAD = 1024
TM_FFN_UP = 1024
TM_FFN_DOWN = 512
MIX_SAVED = ("xc", "r", "ig", "a", "m2raw", "ge", "dge")
FFN_ROW_CHUNKS = 2
VMEM_LIMIT = 56 * 1024 * 1024

SLAB_W = 512
ROW_CONV_B, ROW_CONV_W, ROW_BA, ROW_BX, ROW_LAM, ROW_PB, ROW_PS, ROW_GL, ROW_GP = 0, 1, 5, 6, 7, 8, 9, 10, 11
ROW_GA, ROW_GX, ROW_PW = 16, 80, 144
ROW_MIX, ROW_FFN, ROW_FIN, ROW_LOSS = 272, 274, 276, 278
MIX_SLAB_ROWS = 272
SLAB_ROWS = 288


def _cp(sem=None, **kw):
    if sem is not None:
        kw["dimension_semantics"] = sem
    return pltpu.CompilerParams(vmem_limit_bytes=VMEM_LIMIT, **kw)


def _const_spec(shape):
    nd = len(shape)
    return pl.BlockSpec(shape, lambda *_: (0,) * nd, pipeline_mode=pl.Buffered(1))


def _sigmoid(x):
    return 1.0 / (1.0 + jnp.exp(-x))


def _dot(a, b):
    return jnp.dot(a, b, preferred_element_type=F32)


def _dot_nt(a, b):
    return lax.dot_general(a, b, (((1,), (1,)), ((), ())), preferred_element_type=F32)


def _dot_tn(a, b):
    return lax.dot_general(a, b, (((0,), (0,)), ((), ())), preferred_element_type=F32)


def _colsum8(v):
    m, c = v.shape
    return v.reshape(m // SUBLANES, SUBLANES, c).sum(axis=0)


def _rowmean(v):
    return jnp.mean(v, axis=-1, keepdims=True)


def _rms_bwd(dy, xhat, r, g):
    dxh = dy * g
    return r * (dxh - xhat * _rowmean(dxh * xhat))


def _softplus_neg(lam):
    z = -lam
    e = jnp.exp(-jnp.abs(z))
    u = 1.0 + e
    d = u - 1.0
    log1p = jnp.where(d == 0.0, e, jnp.log(u) * (e / jnp.where(d == 0.0, 1.0, d)))
    return jnp.maximum(z, 0.0) + log1p


def _neg_expm1(z):
    series = -(z * (1.0 + z * (0.5 + z * (1.0 / 6.0 + z * (1.0 / 24.0)))))
    return jnp.where(z > -0.03, series, 1.0 - jnp.exp(z))


_GELU_C = math.sqrt(2.0 / math.pi)
_GELU_K = 0.044715


def _gelu_parts(x):
    x2 = x * x
    th = jnp.tanh(_GELU_C * (x + _GELU_K * x2 * x))
    ge = 0.5 * x * (1.0 + th)
    dge = 0.5 * (1.0 + th) + 0.5 * x * (1.0 - th * th) * (_GELU_C * (1.0 + 3.0 * _GELU_K * x2))
    return ge, dge


def _shift_down(halo, tile, k):
    if k == 0:
        return tile
    ext = jnp.concatenate([halo, tile], axis=0)
    n = tile.shape[0]
    h = halo.shape[0]
    return ext[h - k:h - k + n]


def _shift_up(tile, nxt, k):
    if k == 0:
        return tile
    ext = jnp.concatenate([tile, nxt], axis=0)
    return ext[k:k + tile.shape[0]]


def _build_gate_blocks(ga_ref, gx_ref, gw_ref):
    hd = ga_ref.shape[0]
    per = GATE_BLOCK // hd
    lane = lax.broadcasted_iota(jnp.int32, (hd, GATE_BLOCK), 1)
    for b in range(gw_ref.shape[0]):
        for src, off in ((ga_ref, 0), (gx_ref, GATE_BLOCK)):
            blk = src[:, b * GATE_BLOCK:(b + 1) * GATE_BLOCK]
            for hh in range(per):
                m = (lane >= hh * hd) & (lane < (hh + 1) * hd)
                gw_ref[b, hh * hd:(hh + 1) * hd, off:off + GATE_BLOCK] = jnp.where(m, blk, 0.0).astype(BF16)


def _scan_level1(a, b, reverse):
    m, c = a.shape
    a3 = a.reshape(m // SUBLANES, SUBLANES, c)
    b3 = b.reshape(m // SUBLANES, SUBLANES, c)
    row = lax.broadcasted_iota(jnp.int32, a3.shape, 1)
    for s in (1, 2, 4):
        sh = (SUBLANES - s) if reverse else s
        a_sh = pltpu.roll(a3, sh, 1)
        b_sh = pltpu.roll(b3, sh, 1)
        ok = (row < SUBLANES - s) if reverse else (row >= s)
        b3 = jnp.where(ok, a3 * b_sh + b3, b3)
        a3 = jnp.where(ok, a3 * a_sh, a3)
    return a3.reshape(m, c), b3.reshape(m, c)


def _scan_level2(a_ref, b_ref, out_ref, carry, reverse):
    m, c = a_ref.shape
    ng = m // SUBLANES

    def step(g, cr):
        gi = (ng - 1 - g) if reverse else g
        off = pl.multiple_of(gi * SUBLANES, SUBLANES)
        h = b_ref[pl.ds(off, SUBLANES), :] + a_ref[pl.ds(off, SUBLANES), :] * cr
        out_ref[pl.ds(off, SUBLANES), :] = h
        edge = h[0:1, :] if reverse else h[SUBLANES - 1:SUBLANES, :]
        return jnp.broadcast_to(edge, (SUBLANES, c))

    return lax.fori_loop(0, ng, step, carry, unroll=4)


def _mixer_recompute(u, hal, t0, cw, cb, gw_ref, ba, bx, lam, pw_ref, pb, ps):
    tm = u.shape[0]
    lw = cb.shape[1]
    u_l, u_g, u_p = u[:, :lw], u[:, lw:2 * lw], u[:, 2 * lw:]
    hal_l, hal_p = hal[:, :lw], hal[:, 2 * lw:]
    taps = [_shift_down(hal_l, u_l, CONV_WIDTH - 1 - k) for k in range(CONV_WIDTH)]
    xc = cb
    for k in range(CONV_WIDTH):
        xc = xc + taps[k] * cw[k:k + 1, :]
    xcb = xc.astype(BF16)
    nb = lw // GATE_BLOCK
    gs = [_dot(xcb[:, b * GATE_BLOCK:(b + 1) * GATE_BLOCK], gw_ref[b]) for b in range(nb)]
    r = _sigmoid(jnp.concatenate([g[:, :GATE_BLOCK] for g in gs], axis=1) + ba)
    ig = _sigmoid(jnp.concatenate([g[:, GATE_BLOCK:] for g in gs], axis=1) + bx)
    sp = _softplus_neg(lam)
    la = (-LRU_C * r) * sp
    a = jnp.exp(la)
    m2raw = _neg_expm1(2.0 * la)
    mult = jnp.sqrt(jnp.maximum(m2raw, 1e-12))
    ge, dge = _gelu_parts(u_g)
    row = lax.broadcasted_iota(jnp.int32, (tm, LANES), 0) + t0
    pooled, invs, zs = [], [], []
    for gi, w in enumerate(POOL_WINDOWS):
        e = jnp.concatenate([hal_p[:, gi * LANES:(gi + 1) * LANES], u_p[:, gi * LANES:(gi + 1) * LANES]], axis=0)
        s = e
        k = 1
        while k < w:
            s = s + pltpu.roll(s, k, 0)
            k *= 2
        inv = 1.0 / jnp.minimum(row + 1, w).astype(F32)
        pg = s[HALO:] * inv - e[HALO:]
        pooled.append(pg)
        invs.append(inv)
        zs.append(_dot(pg.astype(BF16), pw_ref[:, gi * LANES:(gi + 1) * LANES].astype(BF16)))
    z = jnp.concatenate(zs, axis=1) + pb
    y_pool = z * ps
    return dict(u_l=u_l, u_g=u_g, taps=taps, xc=xc, xcb=xcb, r=r, ig=ig, sp=sp, la=la, a=a, m2raw=m2raw,
                mult=mult, ge=ge, dge=dge, pooled=pooled, invs=invs, z=z, y_pool=y_pool)


ANY = pl.BlockSpec(memory_space=pl.ANY)
VMEM_SPEC = pl.BlockSpec(memory_space=pltpu.VMEM)


class _Hosted:
    def __init__(self, ins, out_shapes, sems, start, finish, mid=None, aliases=None):
        self.ins, self.out_shapes, self.sems = list(ins), list(out_shapes), list(sems)
        self.start, self.mid, self.finish = start, mid, finish
        self.aliases = dict(aliases or {})


def _call(body, hosted, stage_preds, *, name, grid, in_specs, out_specs, out_shape, scratch_shapes, args, sem):
    hosted = list(hosted or [])
    n_in, n_out, n_scr = len(in_specs), len(out_specs), len(scratch_shapes)
    c_in = [a for h in hosted for a in h.ins]
    c_out = [o for h in hosted for o in h.out_shapes]
    c_sem = [pltpu.SemaphoreType.DMA((k,)) for h in hosted for k in h.sems]

    def full(*refs):
        p = 0
        parts = []
        for cnt in (n_in, len(c_in), n_out, len(c_out), n_scr, len(c_sem)):
            parts.append(refs[p:p + cnt])
            p += cnt
        hi, ci, ho, co, hs, cs = parts
        per = []
        a = b = c_ = 0
        for h in hosted:
            per.append((h, ci[a:a + len(h.ins)], co[b:b + len(h.out_shapes)], cs[c_:c_ + len(h.sems)]))
            a, b, c_ = a + len(h.ins), b + len(h.out_shapes), c_ + len(h.sems)
        first = mid = last = None
        if hosted and grid:
            first, mid, last = stage_preds()

        def run(fn, pred, i_, o_, s_):
            if fn is None:
                return
            if pred is None:
                fn(i_, o_, s_)
            else:
                pl.when(pred)(functools.partial(fn, i_, o_, s_))

        for h, i_, o_, s_ in per:
            run(h.start, first, i_, o_, s_)
        body(*hi, *ho, *hs)
        for h, i_, o_, s_ in per:
            run(h.mid, mid, i_, o_, s_)
        for h, i_, o_, s_ in per:
            run(h.finish, last, i_, o_, s_)

    aliases = {}
    a = b = 0
    for h in hosted:
        for k, v in h.aliases.items():
            aliases[n_in + a + k] = n_out + b + v
        a, b = a + len(h.ins), b + len(h.out_shapes)
    res = pl.pallas_call(
        full, name=name, grid=grid, in_specs=list(in_specs) + [ANY] * len(c_in),
        out_specs=list(out_specs) + [ANY] * len(c_out), out_shape=list(out_shape) + c_out,
        scratch_shapes=list(scratch_shapes) + c_sem, input_output_aliases=aliases,
        compiler_params=_cp(sem))(*args, *c_in)
    res = list(res)
    outs = []
    p = n_out
    for h in hosted:
        outs.append(res[p:p + len(h.out_shapes)])
        p += len(h.out_shapes)
    return res[:n_out], outs


def _inproj(x, g_mix, w_in, hosted=None):
    s, d = x.shape
    n = w_in.shape[1]
    tm = min(TM_PROJ, s)
    nt = s // tm

    def body(x_ref, g_ref, w_ref, u_ref):
        xv = x_ref[...]
        r = lax.rsqrt(_rowmean(xv * xv) + EPS)
        u_ref[...] = _dot((xv * r * g_ref[...]).astype(BF16), w_ref[...])

    def stages():
        i = pl.program_id(0)
        return i == 0, i == max(nt - 3, 0), i == nt - 1

    return _call(
        body, hosted, stages, grid=(nt,), name="inproj",
        in_specs=[pl.BlockSpec((tm, d), lambda i: (i, 0)), _const_spec((1, d)), _const_spec((d, n))],
        out_specs=[pl.BlockSpec((tm, n), lambda i: (i, 0))], out_shape=[SDS((s, n), F32)], scratch_shapes=[],
        args=(x, g_mix, w_in), sem=("arbitrary",))


def _mixer_fwd(u, x, sp_, w_out, hosted=None):
    s, din = u.shape
    d = x.shape[1]
    lw = din // 3
    tm = min(TM_MIX, s)
    nb = lw // GATE_BLOCK

    def body(u_ref, halo_ref, x_ref, cw_ref, cb_ref, ga_ref, gx_ref, ba_ref, bx_ref, lam_ref, pw_ref, pb_ref,
             ps_ref, gl_ref, gp_ref, wout_ref, h_ref, yn_ref, hres_ref, saved_ref, pooled_ref,
             gw_s, a_s, b_s, carry_s):
        i = pl.program_id(0)

        @pl.when(i == 0)
        def _():
            _build_gate_blocks(ga_ref, gx_ref, gw_s)
            carry_s[...] = jnp.zeros_like(carry_s)

        uv = u_ref[...]
        hal = jnp.where(i > 0, halo_ref[...], 0.0)
        f = _mixer_recompute(uv, hal, i * tm, cw_ref[...], cb_ref[...], gw_s, ba_ref[...], bx_ref[...],
                             lam_ref[...], pw_ref, pb_ref[...], ps_ref[...])
        for k, name in enumerate(MIX_SAVED):
            saved_ref[k] = f[name]
        pooled_ref[...] = jnp.concatenate(f["pooled"], axis=1).astype(BF16)
        bb = f["mult"] * (f["ig"] * f["xc"])
        a1, b1 = _scan_level1(f["a"], bb, reverse=False)
        a_s[...] = a1
        b_s[...] = b1
        carry_s[...] = _scan_level2(a_s, b_s, h_ref, carry_s[...], reverse=False)
        y_lru = h_ref[...] * f["ge"]
        rl = lax.rsqrt(_rowmean(y_lru * y_lru) + EPS)
        yp = f["y_pool"]
        rp = lax.rsqrt(_rowmean(yp * yp) + EPS)
        yn = jnp.concatenate([y_lru * rl * gl_ref[...], yp * rp * gp_ref[...]], axis=1).astype(BF16)
        yn_ref[...] = yn
        hres_ref[...] = x_ref[...] + _dot(yn, wout_ref[...])

    small = [sp_[k] for k in ("conv_w", "conv_b", "gate_a_w", "gate_x_w", "gate_a_b", "gate_x_b", "lru_lambda",
                              "pool_w", "pool_b", "pool_scale", "norm_lru_g", "norm_pool_g")]
    nt = s // tm

    def stages():
        i = pl.program_id(0)
        return i == 0, i == max(nt - 3, 0), i == nt - 1

    return _call(
        body, hosted, stages, grid=(nt,), name="mixer_fwd",
        in_specs=[pl.BlockSpec((tm, din), lambda i: (i, 0)),
                  pl.BlockSpec((HALO, din), lambda i: (jnp.maximum(i * (tm // HALO) - 1, 0), 0)),
                  pl.BlockSpec((tm, d), lambda i: (i, 0))]
        + [_const_spec(a.shape) for a in small] + [_const_spec(w_out.shape)],
        out_specs=[pl.BlockSpec((tm, lw), lambda i: (i, 0)), pl.BlockSpec((tm, d), lambda i: (i, 0)),
                   pl.BlockSpec((tm, d), lambda i: (i, 0)),
                   pl.BlockSpec((len(MIX_SAVED), tm, lw), lambda i: (0, i, 0)),
                   pl.BlockSpec((tm, lw), lambda i: (i, 0))],
        out_shape=[SDS((s, lw), F32), SDS((s, d), BF16), SDS((s, d), F32), SDS((len(MIX_SAVED), s, lw), F32),
                   SDS((s, lw), BF16)],
        scratch_shapes=[pltpu.VMEM((nb, GATE_BLOCK, 2 * GATE_BLOCK), BF16), pltpu.VMEM((tm, lw), F32),
                        pltpu.VMEM((tm, lw), F32), pltpu.VMEM((SUBLANES, lw), F32)],
        args=(u, u, x, *small, w_out), sem=("arbitrary",))


def _ffn_fwd(hres1, target, g_ffn, g_fin, w1, w3, w2):
    s, d = hres1.shape
    nj, _, fc = w1.shape
    tm = min(TM_FFN, s)

    def body(h_ref, t_ref, gf_ref, gn_ref, w1_ref, w3_ref, w2_ref,
             a1_ref, a3_ref, h2_ref, dh_ref, dhb_ref, loss_ref, dgn_ref, acc_s):
        i, j = pl.program_id(0), pl.program_id(1)

        @pl.when((i == 0) & (j == 0))
        def _():
            loss_ref[...] = jnp.zeros_like(loss_ref)
            dgn_ref[...] = jnp.zeros_like(dgn_ref)

        @pl.when(j == 0)
        def _():
            hv = h_ref[...]
            r = lax.rsqrt(_rowmean(hv * hv) + EPS)
            h2_ref[...] = (hv * r * gf_ref[...]).astype(BF16)

        h2 = h2_ref[...]
        a1 = _dot(h2, w1_ref[0])
        a3 = _dot(h2, w3_ref[0])
        a1_ref[0] = a1.astype(BF16)
        a3_ref[0] = a3.astype(BF16)
        part = _dot(((a1 * _sigmoid(a1)) * a3).astype(BF16), w2_ref[0])

        @pl.when(j == 0)
        def _():
            acc_s[...] = part

        @pl.when(j > 0)
        def _():
            acc_s[...] += part

        @pl.when(j == nj - 1)
        def _():
            hr2 = h_ref[...] + acc_s[...]
            r2 = lax.rsqrt(_rowmean(hr2 * hr2) + EPS)
            xh = hr2 * r2
            gn = gn_ref[...]
            diff = xh * gn - t_ref[...]
            tot = jnp.sum(jnp.sum(diff * diff, axis=1, keepdims=True), axis=0, keepdims=True)
            loss_ref[...] += tot * (0.5 / d)
            dout = diff * (1.0 / d)
            dgn_ref[...] += _colsum8(dout * xh)
            dh = _rms_bwd(dout, xh, r2, gn)
            dh_ref[...] = dh
            dhb_ref[...] = dh.astype(BF16)

    return pl.pallas_call(
        body, grid=(s // tm, nj), name="ffn_fwd",
        in_specs=[pl.BlockSpec((tm, d), lambda i, j: (i, 0)), pl.BlockSpec((tm, d), lambda i, j: (i, 0)),
                  _const_spec((1, d)), _const_spec((1, d)),
                  pl.BlockSpec((1, d, fc), lambda i, j: (j, 0, 0)), pl.BlockSpec((1, d, fc), lambda i, j: (j, 0, 0)),
                  pl.BlockSpec((1, fc, d), lambda i, j: (j, 0, 0))],
        out_specs=[pl.BlockSpec((1, tm, fc), lambda i, j: (j, i, 0)), pl.BlockSpec((1, tm, fc), lambda i, j: (j, i, 0)),
                   pl.BlockSpec((tm, d), lambda i, j: (i, 0)), pl.BlockSpec((tm, d), lambda i, j: (i, 0)),
                   pl.BlockSpec((tm, d), lambda i, j: (i, 0)),
                   pl.BlockSpec((SUBLANES, LANES), lambda i, j: (0, 0)),
                   pl.BlockSpec((SUBLANES, d), lambda i, j: (0, 0))],
        out_shape=[SDS((nj, s, fc), BF16), SDS((nj, s, fc), BF16), SDS((s, d), BF16), SDS((s, d), F32),
                   SDS((s, d), BF16), SDS((SUBLANES, LANES), F32), SDS((SUBLANES, d), F32)],
        scratch_shapes=[pltpu.VMEM((tm, d), F32)],
        compiler_params=_cp(("arbitrary", "arbitrary")))(hres1, target, g_ffn, g_fin, w1, w3, w2)


def _ffn_bwd_act(dh, dhb, a1, a3, hres1, g_ffn, w1, w3, w2):
    s, d = hres1.shape
    nj, _, fc = a1.shape
    tm = min(TM_FFN, s)

    def body(dh_ref, dhb_ref, a1_ref, a3_ref, h_ref, gf_ref, w1_ref, w3_ref, w2_ref,
             da1_ref, da3_ref, dhr_ref, dgf_ref, acc_s):
        i, j = pl.program_id(0), pl.program_id(1)

        @pl.when((i == 0) & (j == 0))
        def _():
            dgf_ref[...] = jnp.zeros_like(dgf_ref)

        @pl.when(j == 0)
        def _():
            acc_s[...] = jnp.zeros_like(acc_s)

        rc = tm // FFN_ROW_CHUNKS
        for q in range(FFN_ROW_CHUNKS):
            rows = slice(q * rc, (q + 1) * rc)
            dff = _dot_nt(dhb_ref[rows, :], w2_ref[0])
            a1v = a1_ref[0, rows, :].astype(F32)
            a3v = a3_ref[0, rows, :].astype(F32)
            sg = _sigmoid(a1v)
            silu = a1v * sg
            da1 = (dff * a3v * (sg * (1.0 + a1v * (1.0 - sg)))).astype(BF16)
            da3 = (dff * silu).astype(BF16)
            da1_ref[0, rows, :] = da1
            da3_ref[0, rows, :] = da3
            acc_s[rows, :] += _dot_nt(da1, w1_ref[0]) + _dot_nt(da3, w3_ref[0])

        @pl.when(j == nj - 1)
        def _():
            hv = h_ref[...]
            r = lax.rsqrt(_rowmean(hv * hv) + EPS)
            xh = hv * r
            dh2 = acc_s[...]
            dgf_ref[...] += _colsum8(dh2 * xh)
            dhr_ref[...] = dh_ref[...] + _rms_bwd(dh2, xh, r, gf_ref[...])

    return pl.pallas_call(
        body, grid=(s // tm, nj), name="ffn_bwd_act",
        in_specs=[pl.BlockSpec((tm, d), lambda i, j: (i, 0)), pl.BlockSpec((tm, d), lambda i, j: (i, 0)),
                  pl.BlockSpec((1, tm, fc), lambda i, j: (j, i, 0)), pl.BlockSpec((1, tm, fc), lambda i, j: (j, i, 0)),
                  pl.BlockSpec((tm, d), lambda i, j: (i, 0)), _const_spec((1, d)),
                  pl.BlockSpec((1, d, fc), lambda i, j: (j, 0, 0)), pl.BlockSpec((1, d, fc), lambda i, j: (j, 0, 0)),
                  pl.BlockSpec((1, fc, d), lambda i, j: (j, 0, 0))],
        out_specs=[pl.BlockSpec((1, tm, fc), lambda i, j: (j, i, 0)), pl.BlockSpec((1, tm, fc), lambda i, j: (j, i, 0)),
                   pl.BlockSpec((tm, d), lambda i, j: (i, 0)), pl.BlockSpec((SUBLANES, d), lambda i, j: (0, 0))],
        out_shape=[SDS((nj, s, fc), BF16), SDS((nj, s, fc), BF16), SDS((s, d), F32), SDS((SUBLANES, d), F32)],
        scratch_shapes=[pltpu.VMEM((tm, d), F32)],
        compiler_params=_cp(("arbitrary", "arbitrary")))(dh, dhb, a1, a3, hres1, g_ffn, w1, w3, w2)


def _ffn_wgrad(h2, dhb, a1, a3, da1, da3):
    s, d = h2.shape
    _, _, fc = a1.shape
    tm = min(TM_WGRAD, s)

    def body(h2_ref, dhb_ref, a1_ref, a3_ref, da1_ref, da3_ref, dw1_ref, dw3_ref, dw2_ref):
        i = pl.program_id(1)

        @pl.when(i == 0)
        def _():
            dw1_ref[...] = jnp.zeros_like(dw1_ref)
            dw3_ref[...] = jnp.zeros_like(dw3_ref)
            dw2_ref[...] = jnp.zeros_like(dw2_ref)

        h2v = h2_ref[...]
        a1v = a1_ref[0].astype(F32)
        ff = ((a1v * _sigmoid(a1v)) * a3_ref[0].astype(F32)).astype(BF16)
        dw1_ref[0] += _dot_tn(h2v, da1_ref[0])
        dw3_ref[0] += _dot_tn(h2v, da3_ref[0])
        dw2_ref[0] += _dot_tn(ff, dhb_ref[...])

    return pl.pallas_call(
        body, grid=(N_CHIPS, s // tm), name="ffn_wgrad",
        in_specs=[pl.BlockSpec((tm, d), lambda j, i: (i, 0)), pl.BlockSpec((tm, d), lambda j, i: (i, 0))]
        + [pl.BlockSpec((1, tm, fc), lambda j, i: (j, i, 0))] * 4,
        out_specs=[pl.BlockSpec((1, d, fc), lambda j, i: (j, 0, 0)), pl.BlockSpec((1, d, fc), lambda j, i: (j, 0, 0)),
                   pl.BlockSpec((1, fc, d), lambda j, i: (j, 0, 0))],
        out_shape=[SDS((N_CHIPS, d, fc), F32), SDS((N_CHIPS, d, fc), F32), SDS((N_CHIPS, fc, d), F32)],
        compiler_params=_cp(("parallel", "arbitrary")))(h2, dhb, a1, a3, da1, da3)


def _row_chunks(tm):
    rc = tm // FFN_ROW_CHUNKS
    return [slice(q * rc, (q + 1) * rc) for q in range(FFN_ROW_CHUNKS)]


def _ffn_up(hres1, g_ffn, w1, w3):
    s, d = hres1.shape
    nj, fc, _ = w1.shape
    tm = min(TM_FFN_DOWN, s)

    def body(h_ref, gf_ref, w1_ref, w3_ref, h2_ref, a1_ref, a3_ref, ff_ref):
        hv = h_ref[...]
        r = lax.rsqrt(_rowmean(hv * hv) + EPS)
        h2_ref[...] = (hv * r * gf_ref[...]).astype(BF16)
        h2 = h2_ref[...]
        for j in range(nj):
            a1 = _dot_nt(h2, w1_ref[j])
            a3 = _dot_nt(h2, w3_ref[j])
            a1_ref[j] = a1.astype(BF16)
            a3_ref[j] = a3.astype(BF16)
            ff_ref[j] = ((a1 * _sigmoid(a1)) * a3).astype(BF16)

    wspec = _const_spec(w1.shape)
    aspec = pl.BlockSpec((nj, tm, fc), lambda i: (0, i, 0))
    return pl.pallas_call(
        body, grid=(s // tm,), name="ffn_up",
        in_specs=[pl.BlockSpec((tm, d), lambda i: (i, 0)), _const_spec((1, d)), wspec, wspec],
        out_specs=[pl.BlockSpec((tm, d), lambda i: (i, 0)), aspec, aspec, aspec],
        out_shape=[SDS((s, d), BF16)] + [SDS((nj, s, fc), BF16)] * 3,
        compiler_params=_cp(("parallel",)))(hres1, g_ffn, w1, w3)


def _ffn_down(ff, hres1, target, g_fin, w2):
    s, d = hres1.shape
    nj, _, fc = ff.shape
    tm = min(TM_FFN_DOWN, s)

    def body(ff_ref, h_ref, t_ref, gn_ref, w2_ref, dh_ref, dhb_ref, loss_ref, dgn_ref):
        @pl.when(pl.program_id(0) == 0)
        def _():
            loss_ref[...] = jnp.zeros_like(loss_ref)
            dgn_ref[...] = jnp.zeros_like(dgn_ref)

        gn = gn_ref[...]
        for rows in _row_chunks(tm):
            acc = _dot(ff_ref[0, rows, :], w2_ref[0])
            for j in range(1, nj):
                acc = acc + _dot(ff_ref[j, rows, :], w2_ref[j])
            hr2 = h_ref[rows, :] + acc
            r2 = lax.rsqrt(_rowmean(hr2 * hr2) + EPS)
            xh = hr2 * r2
            diff = xh * gn - t_ref[rows, :]
            tot = jnp.sum(jnp.sum(diff * diff, axis=1, keepdims=True), axis=0, keepdims=True)
            loss_ref[...] += tot * (0.5 / d)
            dout = diff * (1.0 / d)
            dgn_ref[...] += _colsum8(dout * xh)
            dh = _rms_bwd(dout, xh, r2, gn)
            dh_ref[rows, :] = dh
            dhb_ref[rows, :] = dh.astype(BF16)

    tile = pl.BlockSpec((tm, d), lambda i: (i, 0))
    return pl.pallas_call(
        body, grid=(s // tm,), name="ffn_down",
        in_specs=[pl.BlockSpec((nj, tm, fc), lambda i: (0, i, 0)), tile, tile, _const_spec((1, d)),
                  _const_spec(w2.shape)],
        out_specs=[tile, tile, pl.BlockSpec((SUBLANES, LANES), lambda i: (0, 0)),
                   pl.BlockSpec((SUBLANES, d), lambda i: (0, 0))],
        out_shape=[SDS((s, d), F32), SDS((s, d), BF16), SDS((SUBLANES, LANES), F32), SDS((SUBLANES, d), F32)],
        compiler_params=_cp(("arbitrary",)))(ff, hres1, target, g_fin, w2)


def _ffn_bwd_gate(dhb, a1, a3, w2):
    s, d = dhb.shape
    nj, _, fc = a1.shape
    tm = min(TM_FFN_DOWN, s)

    def body(dhb_ref, a1_ref, a3_ref, w2_ref, da1_ref, da3_ref):
        for j in range(nj):
            for rows in _row_chunks(tm):
                dff = _dot_nt(dhb_ref[rows, :], w2_ref[j])
                a1v = a1_ref[j, rows, :].astype(F32)
                sg = _sigmoid(a1v)
                silu = a1v * sg
                da1_ref[j, rows, :] = (dff * a3_ref[j, rows, :].astype(F32)
                                       * (sg * (1.0 + (a1v - silu)))).astype(BF16)
                da3_ref[j, rows, :] = (dff * silu).astype(BF16)

    aspec = pl.BlockSpec((nj, tm, fc), lambda i: (0, i, 0))
    return pl.pallas_call(
        body, grid=(s // tm,), name="ffn_bwd_gate",
        in_specs=[pl.BlockSpec((tm, d), lambda i: (i, 0)), aspec, aspec, _const_spec(w2.shape)],
        out_specs=[aspec, aspec], out_shape=[SDS((nj, s, fc), BF16)] * 2,
        compiler_params=_cp(("parallel",)))(dhb, a1, a3, w2)


def _ffn_bwd_down(da1, da3, dh, hres1, g_ffn, w1, w3, hosted=None):
    s, d = hres1.shape
    nj, _, fc = da1.shape
    tm = min(TM_FFN_DOWN, s)
    nt = s // tm

    def body(da1_ref, da3_ref, dh_ref, h_ref, gf_ref, w1_ref, w3_ref, dhr_ref, dgf_ref):
        @pl.when(pl.program_id(0) == 0)
        def _():
            dgf_ref[...] = jnp.zeros_like(dgf_ref)

        gf = gf_ref[...]
        for rows in _row_chunks(tm):
            dh2 = None
            for j in range(nj):
                part = _dot(da1_ref[j, rows, :], w1_ref[j]) + _dot(da3_ref[j, rows, :], w3_ref[j])
                dh2 = part if dh2 is None else dh2 + part
            hv = h_ref[rows, :]
            r = lax.rsqrt(_rowmean(hv * hv) + EPS)
            xh = hv * r
            dgf_ref[...] += _colsum8(dh2 * xh)
            dhr_ref[rows, :] = dh_ref[rows, :] + _rms_bwd(dh2, xh, r, gf)

    tile = pl.BlockSpec((tm, d), lambda i: (i, 0))
    aspec = pl.BlockSpec((nj, tm, fc), lambda i: (0, i, 0))
    wspec = _const_spec(w1.shape)

    def stages():
        i = pl.program_id(0)
        return i == 0, i == max(nt - 2, 0), i == nt - 1

    return _call(
        body, hosted, stages, grid=(nt,), name="ffn_bwd_down",
        in_specs=[aspec, aspec, tile, tile, _const_spec((1, d)), wspec, wspec],
        out_specs=[tile, pl.BlockSpec((SUBLANES, d), lambda i: (0, 0))],
        out_shape=[SDS((s, d), F32), SDS((SUBLANES, d), F32)],
        scratch_shapes=[], args=(da1, da3, dh, hres1, g_ffn, w1, w3), sem=("arbitrary",))


def _ffn_wgrad2(h2, dhb, ff, da1, da3):
    s, d = h2.shape
    _, _, fc = ff.shape
    tm = min(TM_WGRAD, s)

    def body(h2_ref, dhb_ref, ff_ref, da1_ref, da3_ref, dw1_ref, dw3_ref, dw2_ref):
        @pl.when(pl.program_id(1) == 0)
        def _():
            dw1_ref[...] = jnp.zeros_like(dw1_ref)
            dw3_ref[...] = jnp.zeros_like(dw3_ref)
            dw2_ref[...] = jnp.zeros_like(dw2_ref)

        h2v = h2_ref[...]
        dw1_ref[0] += _dot_tn(da1_ref[0], h2v)
        dw3_ref[0] += _dot_tn(da3_ref[0], h2v)
        dw2_ref[0] += _dot_tn(ff_ref[0], dhb_ref[...])

    wspec = pl.BlockSpec((1, fc, d), lambda j, i: (j, 0, 0))
    return pl.pallas_call(
        body, grid=(N_CHIPS, s // tm), name="ffn_wgrad",
        in_specs=[pl.BlockSpec((tm, d), lambda j, i: (i, 0)), pl.BlockSpec((tm, d), lambda j, i: (i, 0))]
        + [pl.BlockSpec((1, tm, fc), lambda j, i: (j, i, 0))] * 3,
        out_specs=[wspec] * 3, out_shape=[SDS((N_CHIPS, fc, d), F32)] * 3,
        compiler_params=_cp(("parallel", "arbitrary")))(h2, dhb, ff, da1, da3)


def _mixer_bwd(u, saved, pooled, h, dhres1, sp_, w_out, hosted=None):
    s, din = u.shape
    d = dhres1.shape[1]
    lw = din // 3
    tm = min(TM_MIX, s)
    nt = s // tm
    nb = lw // GATE_BLOCK
    hd = sp_["gate_a_w"].shape[0]

    def body(ul_ref, saved_ref, pooled_ref, h_ref, hhalo_ref, dhr_ref, cw_ref, cb_ref, ga_ref, gx_ref, ba_ref,
             bx_ref, lam_ref, pw_ref, pb_ref, ps_ref, gl_ref, gp_ref, wout_ref, du_ref, slab_ref,
             gw_s, a_s, b_s, e_s, ecarry_s, dxc_s, q_s, vec_s, cwacc_s, dgw_s, dpw_s):
        i = pl.program_id(0)
        tile = nt - 1 - i

        @pl.when(i == 0)
        def _():
            _build_gate_blocks(ga_ref, gx_ref, gw_s)
            for ref in (ecarry_s, dxc_s, q_s, vec_s, cwacc_s, dgw_s, dpw_s):
                ref[...] = jnp.zeros_like(ref)

        cw = cw_ref[...]
        lam = lam_ref[...]
        ps = ps_ref[...]
        f = {name: saved_ref[k] for k, name in enumerate(MIX_SAVED)}
        f["mult"] = jnp.sqrt(jnp.maximum(f["m2raw"], 1e-12))
        f["sp"] = _softplus_neg(lam)
        f["xcb"] = f["xc"].astype(BF16)
        pooled = pooled_ref[...]
        row = lax.broadcasted_iota(jnp.int32, (tm, LANES), 0) + tile * tm
        f["invs"] = [1.0 / jnp.minimum(row + 1, w).astype(F32) for w in POOL_WINDOWS]
        f["z"] = jnp.concatenate(
            [_dot(pooled[:, g * LANES:(g + 1) * LANES], pw_ref[:, g * LANES:(g + 1) * LANES].astype(BF16))
             for g in range(len(POOL_WINDOWS))], axis=1) + pb_ref[...]
        f["y_pool"] = f["z"] * ps
        u_l = ul_ref[...]
        hv = h_ref[...]
        h_prev = _shift_down(jnp.where(tile > 0, hhalo_ref[...], 0.0), hv, 1)
        y_lru = hv * f["ge"]
        rl = lax.rsqrt(_rowmean(y_lru * y_lru) + EPS)
        yp = f["y_pool"]
        rp = lax.rsqrt(_rowmean(yp * yp) + EPS)
        xh_l = y_lru * rl
        xh_p = yp * rp

        dyn = _dot_nt(dhr_ref[...].astype(BF16), wout_ref[...])
        d_nl, d_np = dyn[:, :lw], dyn[:, lw:]
        vec = {}
        vec[ROW_GL] = _colsum8(d_nl * xh_l)
        vec[ROW_GP] = _colsum8(d_np * xh_p)
        d_ylru = _rms_bwd(d_nl, xh_l, rl, gl_ref[...])
        d_ypool = _rms_bwd(d_np, xh_p, rp, gp_ref[...])

        vec[ROW_PS] = _colsum8(d_ypool * f["z"])
        dz = d_ypool * ps
        vec[ROW_PB] = _colsum8(dz)
        dzb = dz.astype(BF16)
        dup = []
        for gi, w in enumerate(POOL_WINDOWS):
            sl = slice(gi * LANES, (gi + 1) * LANES)
            dpw_s[:, sl] += _dot_tn(pooled[:, sl], dzb[:, sl])
            dpool = _dot_nt(dzb[:, sl], pw_ref[:, sl].astype(BF16))
            q = dpool * f["invs"][gi]
            e = jnp.concatenate([q, q_s[:, sl]], axis=0)
            k = 1
            while k < w:
                e = e + pltpu.roll(e, tm + HALO - k, 0)
                k *= 2
            dup.append(e[:tm] - dpool)
            q_s[:, sl] = q[:HALO]

        d_hout = d_ylru * f["ge"]
        d_ug = d_ylru * hv * f["dge"]
        a = f["a"]
        a1, b1 = _scan_level1(a, a * d_hout, reverse=True)
        a_s[...] = a1
        b_s[...] = b1
        e_next = ecarry_s[...]
        ecarry_s[...] = _scan_level2(a_s, b_s, e_s, e_next, reverse=True)
        sv = d_hout + _shift_up(e_s[...], e_next, 1)
        d_a = sv * h_prev
        mult, ig, xc, r = f["mult"], f["ig"], f["xc"], f["r"]
        d_mult = sv * (ig * xc)
        d_ig = sv * mult * xc
        d_xc = sv * mult * ig
        d_la = d_a * a + jnp.where(f["m2raw"] > 1e-12, d_mult * (-(a * a) / mult), 0.0)
        d_r = d_la * (-LRU_C * f["sp"])
        vec[ROW_LAM] = _colsum8(d_la * (-LRU_C * r))
        d_pr = d_r * r * (1.0 - r)
        d_pi = d_ig * ig * (1.0 - ig)
        vec[ROW_BA] = _colsum8(d_pr)
        vec[ROW_BX] = _colsum8(d_pi)
        dxc_parts = []
        for b in range(nb):
            sl = slice(b * GATE_BLOCK, (b + 1) * GATE_BLOCK)
            rhs = jnp.concatenate([d_pr[:, sl], d_pi[:, sl]], axis=1).astype(BF16)
            dgw_s[b] += _dot_tn(f["xcb"][:, sl], rhs)
            dxc_parts.append(_dot_nt(rhs, gw_s[b]))
        d_xc = d_xc + jnp.concatenate(dxc_parts, axis=1)
        vec[ROW_CONV_B] = _colsum8(d_xc)
        dxc_next = dxc_s[...]
        d_ul = None
        for k in range(CONV_WIDTH):
            ahead = _shift_up(d_xc, dxc_next, CONV_WIDTH - 1 - k)
            cwacc_s[k * SUBLANES:(k + 1) * SUBLANES, :] += _colsum8(ahead * u_l)
            term = ahead * cw[k:k + 1, :]
            d_ul = term if d_ul is None else d_ul + term
        dxc_s[...] = d_xc[:SUBLANES]
        for row, val in vec.items():
            vec_s[row * SUBLANES:(row + 1) * SUBLANES, :] += val
        du_ref[...] = jnp.concatenate([d_ul, d_ug] + dup, axis=1).astype(BF16)

        @pl.when(i == nt - 1)
        def _():
            rows = []
            for row in range(ROW_GA):
                if row in (ROW_CONV_W, ROW_CONV_W + 1, ROW_CONV_W + 2, ROW_CONV_W + 3):
                    k = row - ROW_CONV_W
                    v = jnp.sum(cwacc_s[k * SUBLANES:(k + 1) * SUBLANES, :], axis=0, keepdims=True)
                elif row <= ROW_GP:
                    v = jnp.sum(vec_s[row * SUBLANES:(row + 1) * SUBLANES, :], axis=0, keepdims=True)
                    if row == ROW_LAM:
                        v = v * (-1.0 / (1.0 + jnp.exp(lam)))
                else:
                    v = jnp.zeros((1, lw), F32)
                rows.append(v)
            slab_ref[0:ROW_GA, :] = jnp.concatenate(rows, axis=0)
            lane = lax.broadcasted_iota(jnp.int32, (hd, GATE_BLOCK), 1)
            for b in range(nb):
                for off, row0 in ((0, ROW_GA), (GATE_BLOCK, ROW_GX)):
                    acc = jnp.zeros((hd, GATE_BLOCK), F32)
                    for hh in range(GATE_BLOCK // hd):
                        m = (lane >= hh * hd) & (lane < (hh + 1) * hd)
                        acc = acc + jnp.where(m, dgw_s[b, hh * hd:(hh + 1) * hd, off:off + GATE_BLOCK], 0.0)
                    slab_ref[row0:row0 + hd, b * GATE_BLOCK:(b + 1) * GATE_BLOCK] = acc
            slab_ref[ROW_PW:ROW_PW + LANES, :] = dpw_s[...]

    small = [sp_[k] for k in ("conv_w", "conv_b", "gate_a_w", "gate_x_w", "gate_a_b", "gate_x_b", "lru_lambda",
                              "pool_w", "pool_b", "pool_scale", "norm_lru_g", "norm_pool_g")]
    rev = lambda i: nt - 1 - i

    def stages():
        i = pl.program_id(0)
        return i == 0, i == max(nt - 3, 0), i == nt - 1

    return _call(
        body, hosted, stages, grid=(nt,), name="mixer_bwd",
        in_specs=[pl.BlockSpec((tm, lw), lambda i: (rev(i), 0)),
                  pl.BlockSpec((len(MIX_SAVED), tm, lw), lambda i: (0, rev(i), 0)),
                  pl.BlockSpec((tm, lw), lambda i: (rev(i), 0)),
                  pl.BlockSpec((tm, lw), lambda i: (rev(i), 0)),
                  pl.BlockSpec((SUBLANES, lw), lambda i: (jnp.maximum(rev(i) * (tm // SUBLANES) - 1, 0), 0)),
                  pl.BlockSpec((tm, d), lambda i: (rev(i), 0))]
        + [_const_spec(a.shape) for a in small] + [_const_spec(w_out.shape)],
        out_specs=[pl.BlockSpec((tm, din), lambda i: (rev(i), 0)),
                   pl.BlockSpec((MIX_SLAB_ROWS, SLAB_W), lambda i: (0, 0))],
        out_shape=[SDS((s, din), BF16), SDS((MIX_SLAB_ROWS, SLAB_W), F32)],
        scratch_shapes=[pltpu.VMEM((nb, GATE_BLOCK, 2 * GATE_BLOCK), BF16),
                        pltpu.VMEM((tm, lw), F32), pltpu.VMEM((tm, lw), F32), pltpu.VMEM((tm, lw), F32),
                        pltpu.VMEM((SUBLANES, lw), F32), pltpu.VMEM((SUBLANES, lw), F32),
                        pltpu.VMEM((HALO, lw), F32), pltpu.VMEM((ROW_GA * SUBLANES, lw), F32),
                        pltpu.VMEM((CONV_WIDTH * SUBLANES, lw), F32),
                        pltpu.VMEM((nb, GATE_BLOCK, 2 * GATE_BLOCK), F32), pltpu.VMEM((LANES, lw), F32)],
        args=(u, saved, pooled, h, h, dhres1, *small, w_out), sem=("arbitrary",))


def _inproj_bwd(x, du, dhres1, yn, g_mix, w_in, hosted=None):
    s, d = x.shape
    n = w_in.shape[1]
    nc = n // N_CHIPS
    tm = min(TM_PROJ, s)
    nt = s // tm

    def body(x_ref, du_ref, dhr_ref, yn_ref, g_ref, w_ref, gx_ref, dwin_ref, dwout_ref, dg_ref):
        i = pl.program_id(0)

        @pl.when(i == 0)
        def _():
            dwin_ref[...] = jnp.zeros_like(dwin_ref)
            dwout_ref[...] = jnp.zeros_like(dwout_ref)
            dg_ref[...] = jnp.zeros_like(dg_ref)

        xv = x_ref[...]
        g = g_ref[...]
        r = lax.rsqrt(_rowmean(xv * xv) + EPS)
        xh = xv * r
        h1 = (xh * g).astype(BF16)
        duv = du_ref[...]
        dh1 = _dot_nt(duv, w_ref[...])
        dg_ref[...] += _colsum8(dh1 * xh)
        dhr = dhr_ref[...]
        gx_ref[...] = dhr + _rms_bwd(dh1, xh, r, g)
        for jj in range(N_CHIPS):
            dwin_ref[jj] += _dot_tn(h1, duv[:, jj * nc:(jj + 1) * nc])
        dwout_ref[...] += _dot_tn(yn_ref[...], dhr.astype(BF16))

    def stages():
        i = pl.program_id(0)
        return i == 0, i == max(nt - 3, 0), i == nt - 1

    return _call(
        body, hosted, stages, grid=(nt,), name="inproj_bwd",
        in_specs=[pl.BlockSpec((tm, d), lambda i: (i, 0)), pl.BlockSpec((tm, n), lambda i: (i, 0)),
                  pl.BlockSpec((tm, d), lambda i: (i, 0)), pl.BlockSpec((tm, d), lambda i: (i, 0)),
                  _const_spec((1, d)), _const_spec((d, n))],
        out_specs=[pl.BlockSpec((tm, d), lambda i: (i, 0)), pl.BlockSpec((N_CHIPS, d, nc), lambda i: (0, 0, 0)),
                   pl.BlockSpec((d, d), lambda i: (0, 0)), pl.BlockSpec((SUBLANES, d), lambda i: (0, 0))],
        out_shape=[SDS((s, d), F32), SDS((N_CHIPS, d, nc), F32), SDS((d, d), F32), SDS((SUBLANES, d), F32)],
        scratch_shapes=[], args=(x, du, dhres1, yn, g_mix, w_in), sem=("arbitrary",))


def _place():
    x, y, c = lax.axis_index("x"), lax.axis_index("y"), lax.axis_index("c")
    return x, y, c


def _other_chips(x, y):
    return [(1 - x, y), (x, 1 - y), (1 - x, 1 - y)]


ANY = pl.BlockSpec(memory_space=pl.ANY)
VMEM_SPEC = pl.BlockSpec(memory_space=pltpu.VMEM)

_GATHERED = {"w_in": "cols", "w_out": "major", "ffn_w1": "major", "ffn_w3": "major", "ffn_w2": "major"}
_BIG = ("w_in", "w_out", "ffn_w1", "ffn_w3", "ffn_w2")


def _gather_weights(shards, conv_w, n_remote):
    n = len(shards)
    full_shapes = []
    for name, sh in zip(_BIG, shards):
        r, cdim = sh.shape
        if _GATHERED[name] == "cols":
            assert cdim % LANES == 0
            full_shapes.append((r, cdim * N_CHIPS))
        else:
            full_shapes.append((N_CHIPS, r, cdim))

    def region(ref, name, sh, jj, cc):
        r, cdim = sh
        rows = pl.ds(0, r) if cc is None else pl.ds(pl.multiple_of(cc * (r // 2), 16), r // 2)
        if _GATHERED[name] == "cols":
            return ref.at[rows, pl.ds(pl.multiple_of(jj * cdim, LANES), cdim)]
        return ref.at[jj, rows, :]

    def staged(ref, sh, cc):
        r = sh[0]
        return ref.at[pl.ds(pl.multiple_of(cc * (r // 2), 16), r // 2), :]

    def body(*refs):
        ins, cw_in = refs[:n], refs[n]
        outs, cw_out = refs[n + 1:2 * n + 1], refs[2 * n + 1]
        stage = refs[2 * n + 2:3 * n + 2]
        cw_stage, lsem, ssem, rsem, fssem, frsem, cssem, crsem = refs[3 * n + 2:]
        x, y, c = _place()
        j = 2 * x + y
        chips = _other_chips(x, y)
        for w in range(n):
            stage[w][...] = ins[w][...].astype(BF16)
        cw_stage[...] = jnp.zeros_like(cw_stage)
        cw_stage[0:CONV_WIDTH, :] = cw_in[...]
        shs = [s_.shape for s_ in shards]
        local = [pltpu.make_async_copy(stage[w], region(outs[w], _BIG[w], shs[w], j, None), lsem.at[w])
                 for w in range(n)]
        local.append(pltpu.make_async_copy(cw_stage, cw_out.at[j], lsem.at[n]))
        for cp in local:
            cp.start()
        sends = []
        for k, (px, py) in enumerate(chips):
            for w in range(n_remote):
                sends.append(pltpu.make_async_remote_copy(
                    src_ref=staged(stage[w], shs[w], c), dst_ref=region(outs[w], _BIG[w], shs[w], j, c),
                    send_sem=ssem.at[k * n + w], recv_sem=rsem.at[k * n + w], device_id=(px, py, c),
                    device_id_type=MESH))
            sends.append(pltpu.make_async_remote_copy(
                src_ref=cw_stage, dst_ref=cw_out.at[j], send_sem=cssem.at[k], recv_sem=crsem.at[k],
                device_id=(px, py, c), device_id_type=MESH))
        for cp in sends:
            cp.start()
        fwd = []
        for k, (px, py) in enumerate(chips):
            jk = 2 * px + py
            for w in range(n_remote):
                reg = region(outs[w], _BIG[w], shs[w], jk, c)
                pltpu.make_async_remote_copy(src_ref=reg, dst_ref=reg, send_sem=ssem.at[k * n + w],
                                             recv_sem=rsem.at[k * n + w], device_id=(px, py, c),
                                             device_id_type=MESH).wait_recv()
                cp = pltpu.make_async_remote_copy(src_ref=reg, dst_ref=reg, send_sem=fssem.at[k * n + w],
                                                  recv_sem=frsem.at[k * n + w], device_id=(x, y, 1 - c),
                                                  device_id_type=MESH)
                cp.start()
                fwd.append(cp)
            pltpu.make_async_remote_copy(src_ref=cw_stage, dst_ref=cw_out.at[jk], send_sem=cssem.at[k],
                                         recv_sem=crsem.at[k], device_id=(px, py, c),
                                         device_id_type=MESH).wait_recv()
        for k, (px, py) in enumerate(chips):
            jk = 2 * px + py
            for w in range(n_remote):
                reg = region(outs[w], _BIG[w], shs[w], jk, 1 - c)
                pltpu.make_async_remote_copy(src_ref=reg, dst_ref=reg, send_sem=fssem.at[k * n + w],
                                             recv_sem=frsem.at[k * n + w], device_id=(x, y, 1 - c),
                                             device_id_type=MESH).wait_recv()
        for cp in sends + fwd:
            cp.wait_send()
        for cp in local:
            cp.wait()

    nsem = 3 * n
    return pl.pallas_call(
        body, name="gather_first",
        in_specs=[VMEM_SPEC] * (n + 1), out_specs=[ANY] * (n + 1),
        out_shape=[SDS(fs, BF16) for fs in full_shapes] + [SDS((N_CHIPS, SUBLANES, LANES), F32)],
        scratch_shapes=[pltpu.VMEM(s_.shape, BF16) for s_ in shards] + [pltpu.VMEM((SUBLANES, LANES), F32)]
        + [pltpu.SemaphoreType.DMA((n + 1,))] + [pltpu.SemaphoreType.DMA((nsem,))] * 4
        + [pltpu.SemaphoreType.DMA((3,))] * 2,
        compiler_params=_cp())(*shards, conv_w)


def _start_all(make):
    def f(ins, outs, sems):
        for cp in make(ins, outs, sems):
            cp.start()
    return f


def _wait_all(make):
    def f(ins, outs, sems):
        for cp in make(ins, outs, sems):
            cp.wait()
    return f


def _ffn_gather_hosted(arrs):
    n = len(arrs)

    def make(outs, sems):
        ssem, rsem, fs, fr = sems
        x, y, c = _place()
        j = 2 * x + y

        def reg(w, jj, cc):
            hr = arrs[w].shape[1] // 2
            return outs[w].at[jj, pl.ds(pl.multiple_of(cc * hr, 16), hr), :]

        def rc(w, jj, cc, s_sem, r_sem, dev):
            return pltpu.make_async_remote_copy(src_ref=reg(w, jj, cc), dst_ref=reg(w, jj, cc), send_sem=s_sem,
                                                recv_sem=r_sem, device_id=dev, device_id_type=MESH)

        sends, recvs, fwds, frecvs = [], [], [], []
        for k, (px, py) in enumerate(_other_chips(x, y)):
            jk = 2 * px + py
            for w in range(n):
                q = k * n + w
                sends.append(rc(w, j, c, ssem.at[q], rsem.at[q], (px, py, c)))
                recvs.append(rc(w, jk, c, ssem.at[q], rsem.at[q], (px, py, c)))
                fwds.append(rc(w, jk, c, fs.at[q], fr.at[q], (x, y, 1 - c)))
                frecvs.append(rc(w, jk, 1 - c, fs.at[q], fr.at[q], (x, y, 1 - c)))
        return sends, recvs, fwds, frecvs

    def start(ins, outs, sems):
        for cp in make(outs, sems)[0]:
            cp.start()

    def mid(ins, outs, sems):
        _, recvs, fwds, _ = make(outs, sems)
        for r, f in zip(recvs, fwds):
            r.wait_recv()
            f.start()

    def finish(ins, outs, sems):
        sends, _, fwds, frecvs = make(outs, sems)
        for r in frecvs:
            r.wait_recv()
        for cp in sends + fwds:
            cp.wait_send()

    return _Hosted(arrs, [SDS(a.shape, a.dtype) for a in arrs], [3 * n] * 4, start, finish, mid=mid,
                   aliases={w: w for w in range(n)})


def _rs_sibling_hosted(arrs):
    n = len(arrs)

    def make(ins, outs, sems):
        x, y, c = _place()
        cps = []
        for w in range(n):
            hr = arrs[w].shape[1] // 2
            src = ins[w].at[:, pl.ds(pl.multiple_of((1 - c) * hr, SUBLANES), hr), :]
            cps.append(pltpu.make_async_remote_copy(src_ref=src, dst_ref=outs[w], send_sem=sems[0].at[w],
                                                    recv_sem=sems[1].at[w], device_id=(x, y, 1 - c),
                                                    device_id_type=MESH))
        return cps

    return _Hosted(arrs, [SDS((a.shape[0], a.shape[1] // 2, a.shape[2]), F32) for a in arrs], [n, n],
                   _start_all(make), _wait_all(make))


def _rs_chips_hosted(parts):
    n = len(parts)

    def make(ins, outs, sems):
        x, y, c = _place()
        j = 2 * x + y
        cps = []
        for k, (px, py) in enumerate(_other_chips(x, y)):
            jk = 2 * px + py
            for w in range(n):
                cps.append(pltpu.make_async_remote_copy(
                    src_ref=ins[w].at[jk], dst_ref=outs[w].at[j], send_sem=sems[0].at[k * n + w],
                    recv_sem=sems[1].at[k * n + w], device_id=(px, py, c), device_id_type=MESH))
        return cps

    return _Hosted(parts, [SDS(p.shape, p.dtype) for p in parts], [3 * n, 3 * n], _start_all(make), _wait_all(make))


def _rs_swap_hosted(halves):
    n = len(halves)

    def make(ins, outs, sems):
        x, y, c = _place()
        return [pltpu.make_async_remote_copy(src_ref=ins[w], dst_ref=outs[w], send_sem=sems[0].at[w],
                                             recv_sem=sems[1].at[w], device_id=(x, y, 1 - c), device_id_type=MESH)
                for w in range(n)]

    return _Hosted(halves, [SDS(h.shape, F32) for h in halves], [n, n], _start_all(make), _wait_all(make))


def _run_comm(hosted, name):
    return _call(lambda: None, hosted, None, name=name, grid=(), in_specs=[], out_specs=[], out_shape=[],
                 scratch_shapes=[], args=(), sem=None)[1]


def _row_tile(rows, cols, n_arrays):
    budget = 24 * 1024 * 1024 // (2 * 4 * n_arrays * cols)
    best = SUBLANES
    for t in range(SUBLANES, rows + 1, SUBLANES):
        if rows % t == 0 and t <= budget:
            best = t
    return best


def _place_index(which):
    x, y, c = _place()
    v = c if which == "c" else 2 * x + y
    return jnp.reshape(v, (1,)).astype(jnp.int32)


def _add_own_half(full, recv, name):
    nsh, rows, cols = full.shape
    hr = rows // 2
    t = _row_tile(hr, cols, 4)
    nt = hr // t

    def body(c_ref, a_ref, b_ref, o_ref, ob_ref):
        v = a_ref[...] + b_ref[...]
        o_ref[...] = v
        ob_ref[...] = v.astype(BF16)

    half = pl.BlockSpec((1, t, cols), lambda s_, i, c_ref: (s_, i, 0))
    return pl.pallas_call(
        body, name=name,
        grid_spec=pltpu.PrefetchScalarGridSpec(
            num_scalar_prefetch=1, grid=(nsh, nt),
            in_specs=[pl.BlockSpec((1, t, cols), lambda s_, i, c_ref: (s_, c_ref[0] * nt + i, 0)), half],
            out_specs=[half, half]),
        out_shape=[SDS((nsh, hr, cols), F32), SDS((nsh, hr, cols), BF16)],
        compiler_params=_cp(("parallel", "parallel")))(_place_index("c"), full, recv)


def _sum_chips(own, recv, name):
    nsh, hr, cols = own.shape
    t = _row_tile(hr, cols, 6)

    def body(j_ref, own_ref, *rest):
        r_refs, o_ref = rest[:nsh], rest[nsh]
        j = j_ref[0]
        mine = own_ref[0]
        parts = [jnp.where(j == k, mine, r_refs[k][0].astype(F32)) for k in range(nsh)]
        o_ref[...] = ((parts[0] + parts[1]) + parts[2]) + parts[3]

    def other(k):
        return pl.BlockSpec((1, t, cols), lambda i, j_ref: (jnp.where(j_ref[0] == k, (k + 1) % nsh, k), i, 0))

    return pl.pallas_call(
        body, name=name,
        grid_spec=pltpu.PrefetchScalarGridSpec(
            num_scalar_prefetch=1, grid=(hr // t,),
            in_specs=[pl.BlockSpec((1, t, cols), lambda i, j_ref: (j_ref[0], i, 0))]
            + [other(k) for k in range(nsh)],
            out_specs=pl.BlockSpec((t, cols), lambda i, j_ref: (i, 0))),
        out_shape=SDS((hr, cols), F32), compiler_params=_cp(("parallel",)))(_place_index("j"), own, *([recv] * nsh))


def _adamw_math(w, g, m, v):
    m = ADAM_B1 * m + (1.0 - ADAM_B1) * g
    v = ADAM_B2 * v + (1.0 - ADAM_B2) * (g * g)
    m_hat = m / (1.0 - ADAM_B1 ** ADAM_STEP)
    v_hat = v / (1.0 - ADAM_B2 ** ADAM_STEP)
    delta = -ADAM_LR * (m_hat / (jnp.sqrt(v_hat) + ADAM_EPS) + ADAM_WD * w)
    return delta, m, v


def _adamw_big(w, g_own, g_sib, m, v, name):
    _, rows, cols = w.shape
    hr = rows // 2
    t = _row_tile(hr, cols, 9)
    nth = hr // t

    def body(c_ref, w_ref, go_ref, gs_ref, m_ref, v_ref, g_ref, d_ref, mo_ref, vo_ref):
        own = (pl.program_id(0) // nth) == c_ref[0]
        g = jnp.where(own, go_ref[...], gs_ref[...])
        g_ref[0] = g
        d_ref[0], mo_ref[0], vo_ref[0] = _adamw_math(w_ref[0], g, m_ref[0], v_ref[0])

    spec = pl.BlockSpec((1, t, cols), lambda i, c_ref: (0, i, 0))
    hspec = pl.BlockSpec((t, cols), lambda i, c_ref: (i % nth, 0))
    return pl.pallas_call(
        body, name=name,
        grid_spec=pltpu.PrefetchScalarGridSpec(
            num_scalar_prefetch=1, grid=(2 * nth,), in_specs=[spec, hspec, hspec, spec, spec],
            out_specs=[spec] * 4),
        out_shape=[SDS((1, rows, cols), F32)] * 4,
        compiler_params=_cp(("parallel",)))(_place_index("c"), w, g_own, g_sib, m, v)


def _allreduce_small(mix_slab, dg_mix, dg_ffn, dg_fin, loss8):
    half = SLAB_ROWS // 2

    def body(ms_ref, gm_ref, gf_ref, gn_ref, loss_ref, out_ref, loc_s, sib_s, chip_s, r2_s, fin_s, sems):
        x, y, c = _place()
        j = 2 * x + y
        rows = []
        for ref in (gm_ref, gf_ref, gn_ref):
            v = jnp.sum(ref[...], axis=0, keepdims=True)
            rows += [v[:, :SLAB_W], v[:, SLAB_W:]]
        rows.append(jnp.concatenate([loss_ref[0:1, :]] * (SLAB_W // LANES), axis=1))
        rows.append(jnp.zeros((SLAB_ROWS - ROW_LOSS - 1, SLAB_W), F32))
        loc_s[0:MIX_SLAB_ROWS, :] = ms_ref[...]
        loc_s[MIX_SLAB_ROWS:SLAB_ROWS, :] = jnp.concatenate(rows, axis=0)
        sib = (x, y, 1 - c)
        cp = pltpu.make_async_remote_copy(src_ref=loc_s, dst_ref=sib_s, send_sem=sems.at[0], recv_sem=sems.at[1],
                                          device_id=sib, device_id_type=MESH)
        cp.start()
        cp.wait()
        chip_s[...] = loc_s[...] + sib_s[...]
        mine = chip_s.at[pl.ds(pl.multiple_of(c * half, SUBLANES), half), :]
        r2_s[j] = chip_s[pl.ds(pl.multiple_of(c * half, SUBLANES), half), :]
        cps = []
        for k, (px, py) in enumerate(_other_chips(x, y)):
            cps.append(pltpu.make_async_remote_copy(src_ref=mine, dst_ref=r2_s.at[j], send_sem=sems.at[2 + k],
                                                    recv_sem=sems.at[5 + k], device_id=(px, py, c),
                                                    device_id_type=MESH))
        for cp in cps:
            cp.start()
        for cp in cps:
            cp.wait()
        fin_s[...] = ((r2_s[0] + r2_s[1]) + r2_s[2]) + r2_s[3]
        dst = out_ref.at[pl.ds(pl.multiple_of(c * half, SUBLANES), half), :]
        out_ref[pl.ds(pl.multiple_of(c * half, SUBLANES), half), :] = fin_s[...]
        cp = pltpu.make_async_remote_copy(src_ref=fin_s, dst_ref=dst, send_sem=sems.at[8], recv_sem=sems.at[9],
                                          device_id=sib, device_id_type=MESH)
        cp.start()
        cp.wait()

    return pl.pallas_call(
        body, name="allreduce_small", in_specs=[VMEM_SPEC] * 5, out_specs=VMEM_SPEC,
        out_shape=SDS((SLAB_ROWS, SLAB_W), F32),
        scratch_shapes=[pltpu.VMEM((SLAB_ROWS, SLAB_W), F32)] * 3 + [pltpu.VMEM((N_CHIPS, half, SLAB_W), F32),
                                                                       pltpu.VMEM((half, SLAB_W), F32),
                                                                       pltpu.SemaphoreType.DMA((10,))],
        compiler_params=_cp())(mix_slab, dg_mix, dg_ffn, dg_fin, loss8)


_SMALL_ROWS = (("conv_b", ROW_CONV_B), ("gate_a_b", ROW_BA), ("gate_x_b", ROW_BX), ("lru_lambda", ROW_LAM),
               ("pool_b", ROW_PB), ("pool_scale", ROW_PS), ("norm_lru_g", ROW_GL), ("norm_pool_g", ROW_GP))
_WIDE_ROWS = (("norm_mix_g", ROW_MIX), ("norm_ffn_g", ROW_FFN), ("final_norm_g", ROW_FIN))
_BLOCK_ROWS = (("gate_a_w", ROW_GA), ("gate_x_w", ROW_GX), ("pool_w", ROW_PW))
_SMALL_ORDER = tuple(n for n, _ in _SMALL_ROWS) + tuple(n for n, _ in _WIDE_ROWS) + tuple(
    n for n, _ in _BLOCK_ROWS) + ("conv_w",)


def _adamw_small(slab, wmv):
    names = _SMALL_ORDER
    flat = [a for nme in names for a in wmv[nme]]
    nin = len(flat)

    def body(*refs):
        slab_ref, j_ref = refs[0], refs[1]
        ins = refs[2:2 + nin]
        outs = refs[2 + nin:]
        grads = {}
        for nme, row in _SMALL_ROWS:
            grads[nme] = slab_ref[row:row + 1, :]
        for nme, row in _WIDE_ROWS:
            grads[nme] = jnp.concatenate([slab_ref[row:row + 1, :], slab_ref[row + 1:row + 2, :]], axis=1)
        for nme, row in _BLOCK_ROWS:
            grads[nme] = slab_ref[row:row + wmv[nme][0].shape[0], :]
        full = slab_ref[ROW_CONV_W:ROW_CONV_W + CONV_WIDTH, :]
        jv = j_ref[0]
        g = jnp.zeros((CONV_WIDTH, LANES), F32)
        for jj in range(N_CHIPS):
            g = jnp.where(jv == jj, full[:, jj * LANES:(jj + 1) * LANES], g)
        grads["conv_w"] = g
        for idx, nme in enumerate(names):
            w_ref, m_ref, v_ref = ins[3 * idx:3 * idx + 3]
            g = grads[nme]
            delta, m, v = _adamw_math(w_ref[...], g, m_ref[...], v_ref[...])
            outs[4 * idx][...] = g
            outs[4 * idx + 1][...] = delta
            outs[4 * idx + 2][...] = m
            outs[4 * idx + 3][...] = v

    x, y, _ = _place()
    jidx = jnp.reshape(2 * x + y, (1,)).astype(jnp.int32)
    out_shape = [SDS(wmv[nme][0].shape, F32) for nme in names for _ in range(4)]
    res = pl.pallas_call(
        body, name="adamw_small",
        in_specs=[VMEM_SPEC, pl.BlockSpec(memory_space=pltpu.SMEM)] + [VMEM_SPEC] * nin,
        out_specs=[VMEM_SPEC] * len(out_shape), out_shape=out_shape, compiler_params=_cp())(slab, jidx, *flat)
    return {nme: tuple(res[4 * idx:4 * idx + 4]) for idx, nme in enumerate(names)}


_FFN = ("ffn_w1", "ffn_w3", "ffn_w2")
_TRANSPOSED = ("ffn_w1", "ffn_w3")


def _local_step(x, target, full, sp_, distributed):
    d = x.shape[1]
    (u,), got = _inproj(x, sp_["norm_mix_g"], full["w_in"],
                        [_ffn_gather_hosted([full["w_out"]])] if distributed else None)
    w_out = (got[0][0] if distributed else full["w_out"]).reshape(d, d)
    gather = [_ffn_gather_hosted([full[n] for n in _FFN])] if distributed else None
    (h, yn, hres1, saved, pooled), got = _mixer_fwd(u, x, sp_, w_out, gather)
    w1, w3, w2 = got[0] if distributed else [full[n] for n in _FFN]
    h2, a1, a3, ff = _ffn_up(hres1, sp_["norm_ffn_g"], w1, w3)
    dh, dhb, loss8, dg_fin = _ffn_down(ff, hres1, target, sp_["final_norm_g"], w2)
    da1, da3 = _ffn_bwd_gate(dhb, a1, a3, w2)
    dws = list(_ffn_wgrad2(h2, dhb, ff, da1, da3))
    rs1 = [_rs_sibling_hosted(dws)] if distributed else None
    (dhres1, dg_ffn), got = _ffn_bwd_down(da1, da3, dh, hres1, sp_["norm_ffn_g"], w1, w3, rs1)
    rs2 = None
    if distributed:
        pairs = [_add_own_half(a, r, "add_half_" + n) for n, a, r in zip(_FFN, dws, got[0])]
        rs2 = [_rs_chips_hosted([pb for _, pb in pairs])]
    (du, mix_slab), got = _mixer_bwd(u, saved, pooled, h, dhres1, sp_, w_out, rs2)
    (gx, dwin, dwout, dg_mix), _ = _inproj_bwd(x, du, dhres1, yn, sp_["norm_mix_g"], full["w_in"])
    big = {"w_in": dwin, "w_out": dwout.reshape(N_CHIPS, d // N_CHIPS, d)}
    for k, n in enumerate(_FFN):
        big[n] = (pairs[k][0], got[0][k]) if distributed else dws[k]
    return gx, big, (mix_slab, dg_mix, dg_ffn, dg_fin, loss8)


def _to_compact(w):
    h, i, j = w.shape
    return jnp.transpose(w, (1, 0, 2)).reshape(i, h * j)


def _from_compact(w, h):
    i, hj = w.shape
    return jnp.transpose(w.reshape(i, h, hj // h), (1, 0, 2))


_SMALL_LAYOUT = {
    "gate_a_w": (lambda a: _to_compact(a[0]), lambda a: _from_compact(a, 8)[None]),
    "gate_x_w": (lambda a: _to_compact(a[0]), lambda a: _from_compact(a, 8)[None]),
    "pool_w": (lambda a: _to_compact(a[0]), lambda a: _from_compact(a, 4)[None]),
    "conv_w": (lambda a: a[0], lambda a: a[None]),
    "final_norm_g": (lambda a: a[None], lambda a: a[0]),
}

_WEIGHTS = ("norm_mix_g", "w_in", "conv_w", "conv_b", "gate_a_w", "gate_a_b", "gate_x_w", "gate_x_b", "lru_lambda",
            "pool_w", "pool_b", "pool_scale", "norm_lru_g", "norm_pool_g", "w_out", "norm_ffn_g", "ffn_w1",
            "ffn_w3", "ffn_w2", "final_norm_g")


def kernel(x, norm_mix_g, w_in, conv_w, conv_b, gate_a_w, gate_a_b, gate_x_w, gate_x_b, lru_lambda, pool_w, pool_b, pool_scale, norm_lru_g, norm_pool_g, w_out, norm_ffn_g, ffn_w1, ffn_w3, ffn_w2, final_norm_g, loss_target, m_norm_mix_g, m_w_in, m_conv_w, m_conv_b, m_gate_a_w, m_gate_a_b, m_gate_x_w, m_gate_x_b, m_lru_lambda, m_pool_w, m_pool_b, m_pool_scale, m_norm_lru_g, m_norm_pool_g, m_w_out, m_norm_ffn_g, m_ffn_w1, m_ffn_w3, m_ffn_w2, m_final_norm_g, v_norm_mix_g, v_w_in, v_conv_w, v_conv_b, v_gate_a_w, v_gate_a_b, v_gate_x_w, v_gate_x_b, v_lru_lambda, v_pool_w, v_pool_b, v_pool_scale, v_norm_lru_g, v_norm_pool_g, v_w_out, v_norm_ffn_g, v_ffn_w1, v_ffn_w3, v_ffn_w2, v_final_norm_g):
    loc = locals()
    w = {n: loc[n] for n in _WEIGHTS}
    m = {n: loc["m_" + n] for n in _WEIGHTS}
    v = {n: loc["v_" + n] for n in _WEIGHTS}

    def lay(nme, a):
        return _SMALL_LAYOUT[nme][0](a) if nme in _SMALL_LAYOUT else a

    def unlay(nme, a):
        return _SMALL_LAYOUT[nme][1](a) if nme in _SMALL_LAYOUT else a

    for group in (w, m, v):
        for n in _TRANSPOSED:
            group[n] = jnp.transpose(group[n], (0, 2, 1))

    gathered = _gather_weights([w[n][0] for n in _BIG], w["conv_w"][0], n_remote=1)
    full = dict(zip(_BIG, gathered[:-1]))
    cw_all = gathered[-1]
    sp_ = {n: lay(n, w[n]) for n in _SMALL_ORDER}
    sp_["conv_w"] = jnp.transpose(cw_all[:, :CONV_WIDTH, :], (1, 0, 2)).reshape(CONV_WIDTH, N_CHIPS * LANES)

    gx, big, small = _local_step(x[0], loss_target[0], full, sp_, distributed=True)

    late = ("w_in", "w_out")
    fin = {n: _sum_chips(big[n][0], big[n][1], "sum_chips_" + n) for n in _FFN}
    recv1, swapped = _run_comm([_rs_sibling_hosted([big[n] for n in late]),
                                _rs_swap_hosted([fin[n] for n in _FFN])], "tail_sibling")
    sib = dict(zip(_FFN, swapped))
    pairs = [_add_own_half(big[n], r, "add_half_" + n) for n, r in zip(late, recv1)]
    recv2, = _run_comm([_rs_chips_hosted([pb for _, pb in pairs])], "tail_chips")
    for n, (p, _), r in zip(late, pairs, recv2):
        fin[n] = _sum_chips(p, r, "sum_chips_" + n)
    swapped, = _run_comm([_rs_swap_hosted([fin[n] for n in late])], "tail_swap")
    sib.update(zip(late, swapped))
    out = {}
    for n in _BIG:
        out[n] = tuple(_adamw_big(w[n], fin[n], sib[n], m[n], v[n], "adamw_" + n))
        if n in _TRANSPOSED:
            out[n] = tuple(jnp.transpose(a, (0, 2, 1)) for a in out[n])
    slab = _allreduce_small(*small)
    loss = slab[ROW_LOSS, 0]
    wmv = {n: (lay(n, w[n]), lay(n, m[n]), lay(n, v[n])) for n in _SMALL_ORDER}
    res = _adamw_small(slab, wmv)
    for n in _SMALL_ORDER:
        out[n] = tuple(unlay(n, a) for a in res[n])
    return (loss, gx[None]) + tuple(out[n][k] for k in range(4) for n in _WEIGHTS)
```

```python
import functools
import math

import jax
import jax.numpy as jnp
from jax import lax
from jax.experimental import pallas as pl
from jax.experimental.pallas import tpu as pltpu

F32 = jnp.float32
BF16 = jnp.bfloat16
SDS = jax.ShapeDtypeStruct
MESH = pl.DeviceIdType.MESH

EPS = 1e-6
LRU_C = 8.0
CONV_WIDTH = 4
POOL_WINDOWS = (2, 4, 8, 16)
HALO = 16
LANES = 128
SUBLANES = 8
GATE_BLOCK = 256
N_CHIPS = 4

ADAM_LR = 0.001
ADAM_B1 = 0.9
ADAM_B2 = 0.999
ADAM_EPS = 1e-08
ADAM_WD = 0.01
ADAM_STEP = 10

TM_PROJ = 512
TM_MIX = 512
TM_FFN = 512
TM_WGRAD = 1024
TM_FFN_UP = 1024
TM_FFN_DOWN = 512
MIX_SAVED = ("xc", "r", "ig", "a", "m2raw", "ge", "dge")
FFN_ROW_CHUNKS = 2
VMEM_LIMIT = 56 * 1024 * 1024

SLAB_W = 512
ROW_CONV_B, ROW_CONV_W, ROW_BA, ROW_BX, ROW_LAM, ROW_PB, ROW_PS, ROW_GL, ROW_GP = 0, 1, 5, 6, 7, 8, 9, 10, 11
ROW_GA, ROW_GX, ROW_PW = 16, 80, 144
ROW_MIX, ROW_FFN, ROW_FIN, ROW_LOSS = 272, 274, 276, 278
MIX_SLAB_ROWS = 272
SLAB_ROWS = 288


def _cp(sem=None, **kw):
    if sem is not None:
        kw["dimension_semantics"] = sem
    return pltpu.CompilerParams(vmem_limit_bytes=VMEM_LIMIT, **kw)


def _const_spec(shape):
    nd = len(shape)
    return pl.BlockSpec(shape, lambda *_: (0,) * nd, pipeline_mode=pl.Buffered(1))


def _sigmoid(x):
    return 1.0 / (1.0 + jnp.exp(-x))


def _dot(a, b):
    return jnp.dot(a, b, preferred_element_type=F32)


def _dot_nt(a, b):
    return lax.dot_general(a, b, (((1,), (1,)), ((), ())), preferred_element_type=F32)


def _dot_tn(a, b):
    return lax.dot_general(a, b, (((0,), (0,)), ((), ())), preferred_element_type=F32)


def _colsum8(v):
    m, c = v.shape
    return v.reshape(m // SUBLANES, SUBLANES, c).sum(axis=0)


def _rowmean(v):
    return jnp.mean(v, axis=-1, keepdims=True)


def _rms_bwd(dy, xhat, r, g):
    dxh = dy * g
    return r * (dxh - xhat * _rowmean(dxh * xhat))


def _softplus_neg(lam):
    z = -lam
    e = jnp.exp(-jnp.abs(z))
    u = 1.0 + e
    d = u - 1.0
    log1p = jnp.where(d == 0.0, e, jnp.log(u) * (e / jnp.where(d == 0.0, 1.0, d)))
    return jnp.maximum(z, 0.0) + log1p


def _neg_expm1(z):
    series = -(z * (1.0 + z * (0.5 + z * (1.0 / 6.0 + z * (1.0 / 24.0)))))
    return jnp.where(z > -0.03, series, 1.0 - jnp.exp(z))


_GELU_C = math.sqrt(2.0 / math.pi)
_GELU_K = 0.044715


def _gelu_parts(x):
    x2 = x * x
    th = jnp.tanh(_GELU_C * (x + _GELU_K * x2 * x))
    ge = 0.5 * x * (1.0 + th)
    dge = 0.5 * (1.0 + th) + 0.5 * x * (1.0 - th * th) * (_GELU_C * (1.0 + 3.0 * _GELU_K * x2))
    return ge, dge


def _shift_down(halo, tile, k):
    if k == 0:
        return tile
    ext = jnp.concatenate([halo, tile], axis=0)
    n = tile.shape[0]
    h = halo.shape[0]
    return ext[h - k:h - k + n]


def _shift_up(tile, nxt, k):
    if k == 0:
        return tile
    ext = jnp.concatenate([tile, nxt], axis=0)
    return ext[k:k + tile.shape[0]]


def _build_gate_blocks(ga_ref, gx_ref, gw_ref):
    hd = ga_ref.shape[1]
    per = GATE_BLOCK // hd
    zero = jnp.zeros((hd, hd), F32)
    for b in range(gw_ref.shape[0]):
        for src, off in ((ga_ref, 0), (gx_ref, GATE_BLOCK)):
            for hh in range(per):
                row = jnp.concatenate([zero] * hh + [src[b * per + hh]] + [zero] * (per - 1 - hh), axis=1)
                gw_ref[b, hh * hd:(hh + 1) * hd, off:off + GATE_BLOCK] = row.astype(BF16)


def _scan_level1(a, b, reverse):
    m, c = a.shape
    a3 = a.reshape(m // SUBLANES, SUBLANES, c)
    b3 = b.reshape(m // SUBLANES, SUBLANES, c)
    row = lax.broadcasted_iota(jnp.int32, a3.shape, 1)
    for s in (1, 2, 4):
        sh = (SUBLANES - s) if reverse else s
        a_sh = pltpu.roll(a3, sh, 1)
        b_sh = pltpu.roll(b3, sh, 1)
        ok = (row < SUBLANES - s) if reverse else (row >= s)
        b3 = jnp.where(ok, a3 * b_sh + b3, b3)
        a3 = jnp.where(ok, a3 * a_sh, a3)
    return a3.reshape(m, c), b3.reshape(m, c)


def _scan_level2(a_ref, b_ref, out_ref, carry, reverse):
    m, c = a_ref.shape
    ng = m // SUBLANES

    def step(g, cr):
        gi = (ng - 1 - g) if reverse else g
        off = pl.multiple_of(gi * SUBLANES, SUBLANES)
        h = b_ref[pl.ds(off, SUBLANES), :] + a_ref[pl.ds(off, SUBLANES), :] * cr
        out_ref[pl.ds(off, SUBLANES), :] = h
        edge = h[0:1, :] if reverse else h[SUBLANES - 1:SUBLANES, :]
        return jnp.broadcast_to(edge, (SUBLANES, c))

    return lax.fori_loop(0, ng, step, carry, unroll=4)


def _mixer_recompute(u, hal, t0, cw, cb, gw_ref, ba, bx, lam, pw_ref, pb, ps):
    tm = u.shape[0]
    lw = cb.shape[1]
    u_l, u_g, u_p = u[:, :lw], u[:, lw:2 * lw], u[:, 2 * lw:]
    hal_l, hal_p = hal[:, :lw], hal[:, 2 * lw:]
    taps = [_shift_down(hal_l, u_l, CONV_WIDTH - 1 - k) for k in range(CONV_WIDTH)]
    xc = cb
    for k in range(CONV_WIDTH):
        xc = xc + taps[k] * cw[k:k + 1, :]
    xcb = xc.astype(BF16)
    nb = lw // GATE_BLOCK
    gs = [_dot(xcb[:, b * GATE_BLOCK:(b + 1) * GATE_BLOCK], gw_ref[b]) for b in range(nb)]
    r = _sigmoid(jnp.concatenate([g[:, :GATE_BLOCK] for g in gs], axis=1) + ba)
    ig = _sigmoid(jnp.concatenate([g[:, GATE_BLOCK:] for g in gs], axis=1) + bx)
    sp = _softplus_neg(lam)
    la = (-LRU_C * r) * sp
    a = jnp.exp(la)
    m2raw = _neg_expm1(2.0 * la)
    mult = jnp.sqrt(jnp.maximum(m2raw, 1e-12))
    ge, dge = _gelu_parts(u_g)
    row = lax.broadcasted_iota(jnp.int32, (tm, LANES), 0) + t0
    pooled, invs, zs = [], [], []
    for gi, w in enumerate(POOL_WINDOWS):
        e = jnp.concatenate([hal_p[:, gi * LANES:(gi + 1) * LANES], u_p[:, gi * LANES:(gi + 1) * LANES]], axis=0)
        s = e
        k = 1
        while k < w:
            s = s + pltpu.roll(s, k, 0)
            k *= 2
        inv = 1.0 / jnp.minimum(row + 1, w).astype(F32)
        pg = s[HALO:] * inv - e[HALO:]
        pooled.append(pg)
        invs.append(inv)
        zs.append(_dot(pg.astype(BF16), pw_ref[gi].astype(BF16)))
    z = jnp.concatenate(zs, axis=1) + pb
    y_pool = z * ps
    return dict(u_l=u_l, u_g=u_g, taps=taps, xc=xc, xcb=xcb, r=r, ig=ig, sp=sp, la=la, a=a, m2raw=m2raw,
                mult=mult, ge=ge, dge=dge, pooled=pooled, invs=invs, z=z, y_pool=y_pool)


ANY = pl.BlockSpec(memory_space=pl.ANY)
VMEM_SPEC = pl.BlockSpec(memory_space=pltpu.VMEM)


class _Hosted:
    def __init__(self, ins, out_shapes, sems, start, finish, mid=None, aliases=None):
        self.ins, self.out_shapes, self.sems = list(ins), list(out_shapes), list(sems)
        self.start, self.mid, self.finish = start, mid, finish
        self.aliases = dict(aliases or {})


def _call(body, hosted, stage_preds, *, name, grid, in_specs, out_specs, out_shape, scratch_shapes, args, sem):
    hosted = list(hosted or [])
    n_in, n_out, n_scr = len(in_specs), len(out_specs), len(scratch_shapes)
    c_in = [a for h in hosted for a in h.ins]
    c_out = [o for h in hosted for o in h.out_shapes]
    c_sem = [pltpu.SemaphoreType.DMA((k,)) for h in hosted for k in h.sems]

    def full(*refs):
        p = 0
        parts = []
        for cnt in (n_in, len(c_in), n_out, len(c_out), n_scr, len(c_sem)):
            parts.append(refs[p:p + cnt])
            p += cnt
        hi, ci, ho, co, hs, cs = parts
        per = []
        a = b = c_ = 0
        for h in hosted:
            per.append((h, ci[a:a + len(h.ins)], co[b:b + len(h.out_shapes)], cs[c_:c_ + len(h.sems)]))
            a, b, c_ = a + len(h.ins), b + len(h.out_shapes), c_ + len(h.sems)
        first = mid = last = None
        if hosted and grid:
            first, mid, last = stage_preds()

        def run(fn, pred, i_, o_, s_):
            if fn is None:
                return
            if pred is None:
                fn(i_, o_, s_)
            else:
                pl.when(pred)(functools.partial(fn, i_, o_, s_))

        for h, i_, o_, s_ in per:
            run(h.start, first, i_, o_, s_)
        body(*hi, *ho, *hs)
        for h, i_, o_, s_ in per:
            run(h.mid, mid, i_, o_, s_)
        for h, i_, o_, s_ in per:
            run(h.finish, last, i_, o_, s_)

    aliases = {}
    a = b = 0
    for h in hosted:
        for k, v in h.aliases.items():
            aliases[n_in + a + k] = n_out + b + v
        a, b = a + len(h.ins), b + len(h.out_shapes)
    res = pl.pallas_call(
        full, name=name, grid=grid, in_specs=list(in_specs) + [ANY] * len(c_in),
        out_specs=list(out_specs) + [ANY] * len(c_out), out_shape=list(out_shape) + c_out,
        scratch_shapes=list(scratch_shapes) + c_sem, input_output_aliases=aliases,
        compiler_params=_cp(sem))(*args, *c_in)
    res = list(res)
    outs = []
    p = n_out
    for h in hosted:
        outs.append(res[p:p + len(h.out_shapes)])
        p += len(h.out_shapes)
    return res[:n_out], outs


def _inproj(x, g_mix, w_in, hosted=None):
    s, d = x.shape
    n = w_in.shape[1]
    tm = min(TM_PROJ, s)
    nt = s // tm

    def body(x_ref, g_ref, w_ref, u_ref):
        xv = x_ref[...]
        r = lax.rsqrt(_rowmean(xv * xv) + EPS)
        u_ref[...] = _dot((xv * r * g_ref[...]).astype(BF16), w_ref[...])

    def stages():
        i = pl.program_id(0)
        return i == 0, i == max(nt - 3, 0), i == nt - 1

    return _call(
        body, hosted, stages, grid=(nt,), name="inproj",
        in_specs=[pl.BlockSpec((tm, d), lambda i: (i, 0)), _const_spec((1, d)), _const_spec((d, n))],
        out_specs=[pl.BlockSpec((tm, n), lambda i: (i, 0))], out_shape=[SDS((s, n), F32)], scratch_shapes=[],
        args=(x, g_mix, w_in), sem=("arbitrary",))


def _mixer_fwd(u, x, sp_, w_out, hosted=None):
    s, din = u.shape
    d = x.shape[1]
    lw = din // 3
    tm = min(TM_MIX, s)
    nb = lw // GATE_BLOCK

    def body(u_ref, halo_ref, x_ref, cw_ref, cb_ref, ga_ref, gx_ref, ba_ref, bx_ref, lam_ref, pw_ref, pb_ref,
             ps_ref, gl_ref, gp_ref, wout_ref, h_ref, yn_ref, hres_ref, saved_ref, pooled_ref,
             gw_s, a_s, b_s, carry_s):
        i = pl.program_id(0)

        @pl.when(i == 0)
        def _():
            _build_gate_blocks(ga_ref, gx_ref, gw_s)
            carry_s[...] = jnp.zeros_like(carry_s)

        uv = u_ref[...]
        hal = jnp.where(i > 0, halo_ref[...], 0.0)
        f = _mixer_recompute(uv, hal, i * tm, cw_ref[...], cb_ref[...], gw_s, ba_ref[...], bx_ref[...],
                             lam_ref[...], pw_ref, pb_ref[...], ps_ref[...])
        for k, name in enumerate(MIX_SAVED):
            saved_ref[k] = f[name]
        pooled_ref[...] = jnp.concatenate(f["pooled"], axis=1).astype(BF16)
        bb = f["mult"] * (f["ig"] * f["xc"])
        a1, b1 = _scan_level1(f["a"], bb, reverse=False)
        a_s[...] = a1
        b_s[...] = b1
        carry_s[...] = _scan_level2(a_s, b_s, h_ref, carry_s[...], reverse=False)
        y_lru = h_ref[...] * f["ge"]
        rl = lax.rsqrt(_rowmean(y_lru * y_lru) + EPS)
        yp = f["y_pool"]
        rp = lax.rsqrt(_rowmean(yp * yp) + EPS)
        yn = jnp.concatenate([y_lru * rl * gl_ref[...], yp * rp * gp_ref[...]], axis=1).astype(BF16)
        yn_ref[...] = yn
        hres_ref[...] = x_ref[...] + _dot(yn, wout_ref[...])

    small = [sp_[k] for k in ("conv_w", "conv_b", "gate_a_w", "gate_x_w", "gate_a_b", "gate_x_b", "lru_lambda",
                              "pool_w", "pool_b", "pool_scale", "norm_lru_g", "norm_pool_g")]
    nt = s // tm

    def stages():
        i = pl.program_id(0)
        return i == 0, i == max(nt - 3, 0), i == nt - 1

    return _call(
        body, hosted, stages, grid=(nt,), name="mixer_fwd",
        in_specs=[pl.BlockSpec((tm, din), lambda i: (i, 0)),
                  pl.BlockSpec((HALO, din), lambda i: (jnp.maximum(i * (tm // HALO) - 1, 0), 0)),
                  pl.BlockSpec((tm, d), lambda i: (i, 0))]
        + [_const_spec(a.shape) for a in small] + [_const_spec(w_out.shape)],
        out_specs=[pl.BlockSpec((tm, lw), lambda i: (i, 0)), pl.BlockSpec((tm, d), lambda i: (i, 0)),
                   pl.BlockSpec((tm, d), lambda i: (i, 0)),
                   pl.BlockSpec((len(MIX_SAVED), tm, lw), lambda i: (0, i, 0)),
                   pl.BlockSpec((tm, lw), lambda i: (i, 0))],
        out_shape=[SDS((s, lw), F32), SDS((s, d), BF16), SDS((s, d), F32), SDS((len(MIX_SAVED), s, lw), F32),
                   SDS((s, lw), BF16)],
        scratch_shapes=[pltpu.VMEM((nb, GATE_BLOCK, 2 * GATE_BLOCK), BF16), pltpu.VMEM((tm, lw), F32),
                        pltpu.VMEM((tm, lw), F32), pltpu.VMEM((SUBLANES, lw), F32)],
        args=(u, u, x, *small, w_out), sem=("arbitrary",))


def _ffn_fwd(hres1, target, g_ffn, g_fin, w1, w3, w2):
    s, d = hres1.shape
    nj, _, fc = w1.shape
    tm = min(TM_FFN, s)

    def body(h_ref, t_ref, gf_ref, gn_ref, w1_ref, w3_ref, w2_ref,
             a1_ref, a3_ref, h2_ref, dh_ref, dhb_ref, loss_ref, dgn_ref, acc_s):
        i, j = pl.program_id(0), pl.program_id(1)

        @pl.when((i == 0) & (j == 0))
        def _():
            loss_ref[...] = jnp.zeros_like(loss_ref)
            dgn_ref[...] = jnp.zeros_like(dgn_ref)

        @pl.when(j == 0)
        def _():
            hv = h_ref[...]
            r = lax.rsqrt(_rowmean(hv * hv) + EPS)
            h2_ref[...] = (hv * r * gf_ref[...]).astype(BF16)

        h2 = h2_ref[...]
        a1 = _dot(h2, w1_ref[0])
        a3 = _dot(h2, w3_ref[0])
        a1_ref[0] = a1.astype(BF16)
        a3_ref[0] = a3.astype(BF16)
        part = _dot(((a1 * _sigmoid(a1)) * a3).astype(BF16), w2_ref[0])

        @pl.when(j == 0)
        def _():
            acc_s[...] = part

        @pl.when(j > 0)
        def _():
            acc_s[...] += part

        @pl.when(j == nj - 1)
        def _():
            hr2 = h_ref[...] + acc_s[...]
            r2 = lax.rsqrt(_rowmean(hr2 * hr2) + EPS)
            xh = hr2 * r2
            gn = gn_ref[...]
            diff = xh * gn - t_ref[...]
            tot = jnp.sum(jnp.sum(diff * diff, axis=1, keepdims=True), axis=0, keepdims=True)
            loss_ref[...] += tot * (0.5 / d)
            dout = diff * (1.0 / d)
            dgn_ref[...] += _colsum8(dout * xh)
            dh = _rms_bwd(dout, xh, r2, gn)
            dh_ref[...] = dh
            dhb_ref[...] = dh.astype(BF16)

    return pl.pallas_call(
        body, grid=(s // tm, nj), name="ffn_fwd",
        in_specs=[pl.BlockSpec((tm, d), lambda i, j: (i, 0)), pl.BlockSpec((tm, d), lambda i, j: (i, 0)),
                  _const_spec((1, d)), _const_spec((1, d)),
                  pl.BlockSpec((1, d, fc), lambda i, j: (j, 0, 0)), pl.BlockSpec((1, d, fc), lambda i, j: (j, 0, 0)),
                  pl.BlockSpec((1, fc, d), lambda i, j: (j, 0, 0))],
        out_specs=[pl.BlockSpec((1, tm, fc), lambda i, j: (j, i, 0)), pl.BlockSpec((1, tm, fc), lambda i, j: (j, i, 0)),
                   pl.BlockSpec((tm, d), lambda i, j: (i, 0)), pl.BlockSpec((tm, d), lambda i, j: (i, 0)),
                   pl.BlockSpec((tm, d), lambda i, j: (i, 0)),
                   pl.BlockSpec((SUBLANES, LANES), lambda i, j: (0, 0)),
                   pl.BlockSpec((SUBLANES, d), lambda i, j: (0, 0))],
        out_shape=[SDS((nj, s, fc), BF16), SDS((nj, s, fc), BF16), SDS((s, d), BF16), SDS((s, d), F32),
                   SDS((s, d), BF16), SDS((SUBLANES, LANES), F32), SDS((SUBLANES, d), F32)],
        scratch_shapes=[pltpu.VMEM((tm, d), F32)],
        compiler_params=_cp(("arbitrary", "arbitrary")))(hres1, target, g_ffn, g_fin, w1, w3, w2)


def _ffn_bwd_act(dh, dhb, a1, a3, hres1, g_ffn, w1, w3, w2):
    s, d = hres1.shape
    nj, _, fc = a1.shape
    tm = min(TM_FFN, s)

    def body(dh_ref, dhb_ref, a1_ref, a3_ref, h_ref, gf_ref, w1_ref, w3_ref, w2_ref,
             da1_ref, da3_ref, dhr_ref, dgf_ref, acc_s):
        i, j = pl.program_id(0), pl.program_id(1)

        @pl.when((i == 0) & (j == 0))
        def _():
            dgf_ref[...] = jnp.zeros_like(dgf_ref)

        @pl.when(j == 0)
        def _():
            acc_s[...] = jnp.zeros_like(acc_s)

        rc = tm // FFN_ROW_CHUNKS
        for q in range(FFN_ROW_CHUNKS):
            rows = slice(q * rc, (q + 1) * rc)
            dff = _dot_nt(dhb_ref[rows, :], w2_ref[0])
            a1v = a1_ref[0, rows, :].astype(F32)
            a3v = a3_ref[0, rows, :].astype(F32)
            sg = _sigmoid(a1v)
            silu = a1v * sg
            da1 = (dff * a3v * (sg * (1.0 + a1v * (1.0 - sg)))).astype(BF16)
            da3 = (dff * silu).astype(BF16)
            da1_ref[0, rows, :] = da1
            da3_ref[0, rows, :] = da3
            acc_s[rows, :] += _dot_nt(da1, w1_ref[0]) + _dot_nt(da3, w3_ref[0])

        @pl.when(j == nj - 1)
        def _():
            hv = h_ref[...]
            r = lax.rsqrt(_rowmean(hv * hv) + EPS)
            xh = hv * r
            dh2 = acc_s[...]
            dgf_ref[...] += _colsum8(dh2 * xh)
            dhr_ref[...] = dh_ref[...] + _rms_bwd(dh2, xh, r, gf_ref[...])

    return pl.pallas_call(
        body, grid=(s // tm, nj), name="ffn_bwd_act",
        in_specs=[pl.BlockSpec((tm, d), lambda i, j: (i, 0)), pl.BlockSpec((tm, d), lambda i, j: (i, 0)),
                  pl.BlockSpec((1, tm, fc), lambda i, j: (j, i, 0)), pl.BlockSpec((1, tm, fc), lambda i, j: (j, i, 0)),
                  pl.BlockSpec((tm, d), lambda i, j: (i, 0)), _const_spec((1, d)),
                  pl.BlockSpec((1, d, fc), lambda i, j: (j, 0, 0)), pl.BlockSpec((1, d, fc), lambda i, j: (j, 0, 0)),
                  pl.BlockSpec((1, fc, d), lambda i, j: (j, 0, 0))],
        out_specs=[pl.BlockSpec((1, tm, fc), lambda i, j: (j, i, 0)), pl.BlockSpec((1, tm, fc), lambda i, j: (j, i, 0)),
                   pl.BlockSpec((tm, d), lambda i, j: (i, 0)), pl.BlockSpec((SUBLANES, d), lambda i, j: (0, 0))],
        out_shape=[SDS((nj, s, fc), BF16), SDS((nj, s, fc), BF16), SDS((s, d), F32), SDS((SUBLANES, d), F32)],
        scratch_shapes=[pltpu.VMEM((tm, d), F32)],
        compiler_params=_cp(("arbitrary", "arbitrary")))(dh, dhb, a1, a3, hres1, g_ffn, w1, w3, w2)


def _ffn_wgrad(h2, dhb, a1, a3, da1, da3):
    s, d = h2.shape
    _, _, fc = a1.shape
    tm = min(TM_WGRAD, s)

    def body(h2_ref, dhb_ref, a1_ref, a3_ref, da1_ref, da3_ref, dw1_ref, dw3_ref, dw2_ref):
        i = pl.program_id(1)

        @pl.when(i == 0)
        def _():
            dw1_ref[...] = jnp.zeros_like(dw1_ref)
            dw3_ref[...] = jnp.zeros_like(dw3_ref)
            dw2_ref[...] = jnp.zeros_like(dw2_ref)

        h2v = h2_ref[...]
        a1v = a1_ref[0].astype(F32)
        ff = ((a1v * _sigmoid(a1v)) * a3_ref[0].astype(F32)).astype(BF16)
        dw1_ref[0] += _dot_tn(h2v, da1_ref[0])
        dw3_ref[0] += _dot_tn(h2v, da3_ref[0])
        dw2_ref[0] += _dot_tn(ff, dhb_ref[...])

    return pl.pallas_call(
        body, grid=(N_CHIPS, s // tm), name="ffn_wgrad",
        in_specs=[pl.BlockSpec((tm, d), lambda j, i: (i, 0)), pl.BlockSpec((tm, d), lambda j, i: (i, 0))]
        + [pl.BlockSpec((1, tm, fc), lambda j, i: (j, i, 0))] * 4,
        out_specs=[pl.BlockSpec((1, d, fc), lambda j, i: (j, 0, 0)), pl.BlockSpec((1, d, fc), lambda j, i: (j, 0, 0)),
                   pl.BlockSpec((1, fc, d), lambda j, i: (j, 0, 0))],
        out_shape=[SDS((N_CHIPS, d, fc), F32), SDS((N_CHIPS, d, fc), F32), SDS((N_CHIPS, fc, d), F32)],
        compiler_params=_cp(("parallel", "arbitrary")))(h2, dhb, a1, a3, da1, da3)


def _row_chunks(tm):
    rc = tm // FFN_ROW_CHUNKS
    return [slice(q * rc, (q + 1) * rc) for q in range(FFN_ROW_CHUNKS)]


def _ffn_up(hres1, g_ffn, w1, w3):
    s, d = hres1.shape
    nj, fc, _ = w1.shape
    tm = min(TM_FFN_DOWN, s)

    def body(h_ref, gf_ref, w1_ref, w3_ref, h2_ref, a1_ref, a3_ref, ff_ref):
        hv = h_ref[...]
        r = lax.rsqrt(_rowmean(hv * hv) + EPS)
        h2_ref[...] = (hv * r * gf_ref[...]).astype(BF16)
        h2 = h2_ref[...]
        for j in range(nj):
            a1 = _dot_nt(h2, w1_ref[j])
            a3 = _dot_nt(h2, w3_ref[j])
            a1_ref[j] = a1.astype(BF16)
            a3_ref[j] = a3.astype(BF16)
            ff_ref[j] = ((a1 * _sigmoid(a1)) * a3).astype(BF16)

    wspec = _const_spec(w1.shape)
    aspec = pl.BlockSpec((nj, tm, fc), lambda i: (0, i, 0))
    return pl.pallas_call(
        body, grid=(s // tm,), name="ffn_up",
        in_specs=[pl.BlockSpec((tm, d), lambda i: (i, 0)), _const_spec((1, d)), wspec, wspec],
        out_specs=[pl.BlockSpec((tm, d), lambda i: (i, 0)), aspec, aspec, aspec],
        out_shape=[SDS((s, d), BF16)] + [SDS((nj, s, fc), BF16)] * 3,
        compiler_params=_cp(("parallel",)))(hres1, g_ffn, w1, w3)


def _ffn_down(ff, hres1, target, g_fin, w2):
    s, d = hres1.shape
    nj, _, fc = ff.shape
    tm = min(TM_FFN_DOWN, s)

    def body(ff_ref, h_ref, t_ref, gn_ref, w2_ref, dh_ref, dhb_ref, loss_ref, dgn_ref):
        @pl.when(pl.program_id(0) == 0)
        def _():
            loss_ref[...] = jnp.zeros_like(loss_ref)
            dgn_ref[...] = jnp.zeros_like(dgn_ref)

        gn = gn_ref[...]
        for rows in _row_chunks(tm):
            acc = _dot(ff_ref[0, rows, :], w2_ref[0])
            for j in range(1, nj):
                acc = acc + _dot(ff_ref[j, rows, :], w2_ref[j])
            hr2 = h_ref[rows, :] + acc
            r2 = lax.rsqrt(_rowmean(hr2 * hr2) + EPS)
            xh = hr2 * r2
            diff = xh * gn - t_ref[rows, :]
            tot = jnp.sum(jnp.sum(diff * diff, axis=1, keepdims=True), axis=0, keepdims=True)
            loss_ref[...] += tot * (0.5 / d)
            dout = diff * (1.0 / d)
            dgn_ref[...] += _colsum8(dout * xh)
            dh = _rms_bwd(dout, xh, r2, gn)
            dh_ref[rows, :] = dh
            dhb_ref[rows, :] = dh.astype(BF16)

    tile = pl.BlockSpec((tm, d), lambda i: (i, 0))
    return pl.pallas_call(
        body, grid=(s // tm,), name="ffn_down",
        in_specs=[pl.BlockSpec((nj, tm, fc), lambda i: (0, i, 0)), tile, tile, _const_spec((1, d)),
                  _const_spec(w2.shape)],
        out_specs=[tile, tile, pl.BlockSpec((SUBLANES, LANES), lambda i: (0, 0)),
                   pl.BlockSpec((SUBLANES, d), lambda i: (0, 0))],
        out_shape=[SDS((s, d), F32), SDS((s, d), BF16), SDS((SUBLANES, LANES), F32), SDS((SUBLANES, d), F32)],
        compiler_params=_cp(("arbitrary",)))(ff, hres1, target, g_fin, w2)


def _ffn_bwd_gate(dhb, a1, a3, w2):
    s, d = dhb.shape
    nj, _, fc = a1.shape
    tm = min(TM_FFN_DOWN, s)

    def body(dhb_ref, a1_ref, a3_ref, w2_ref, da1_ref, da3_ref):
        for j in range(nj):
            for rows in _row_chunks(tm):
                dff = _dot_nt(dhb_ref[rows, :], w2_ref[j])
                a1v = a1_ref[j, rows, :].astype(F32)
                sg = _sigmoid(a1v)
                silu = a1v * sg
                da1_ref[j, rows, :] = (dff * a3_ref[j, rows, :].astype(F32)
                                       * (sg * (1.0 + (a1v - silu)))).astype(BF16)
                da3_ref[j, rows, :] = (dff * silu).astype(BF16)

    aspec = pl.BlockSpec((nj, tm, fc), lambda i: (0, i, 0))
    return pl.pallas_call(
        body, grid=(s // tm,), name="ffn_bwd_gate",
        in_specs=[pl.BlockSpec((tm, d), lambda i: (i, 0)), aspec, aspec, _const_spec(w2.shape)],
        out_specs=[aspec, aspec], out_shape=[SDS((nj, s, fc), BF16)] * 2,
        compiler_params=_cp(("parallel",)))(dhb, a1, a3, w2)


def _ffn_bwd_down(da1, da3, dh, hres1, g_ffn, w1, w3, hosted=None):
    s, d = hres1.shape
    nj, _, fc = da1.shape
    tm = min(TM_FFN_DOWN, s)
    nt = s // tm

    def body(da1_ref, da3_ref, dh_ref, h_ref, gf_ref, w1_ref, w3_ref, dhr_ref, dgf_ref):
        @pl.when(pl.program_id(0) == 0)
        def _():
            dgf_ref[...] = jnp.zeros_like(dgf_ref)

        gf = gf_ref[...]
        for rows in _row_chunks(tm):
            dh2 = None
            for j in range(nj):
                part = _dot(da1_ref[j, rows, :], w1_ref[j]) + _dot(da3_ref[j, rows, :], w3_ref[j])
                dh2 = part if dh2 is None else dh2 + part
            hv = h_ref[rows, :]
            r = lax.rsqrt(_rowmean(hv * hv) + EPS)
            xh = hv * r
            dgf_ref[...] += _colsum8(dh2 * xh)
            dhr_ref[rows, :] = dh_ref[rows, :] + _rms_bwd(dh2, xh, r, gf)

    tile = pl.BlockSpec((tm, d), lambda i: (i, 0))
    aspec = pl.BlockSpec((nj, tm, fc), lambda i: (0, i, 0))
    wspec = _const_spec(w1.shape)

    def stages():
        i = pl.program_id(0)
        return i == 0, i == max(nt - 2, 0), i == nt - 1

    return _call(
        body, hosted, stages, grid=(nt,), name="ffn_bwd_down",
        in_specs=[aspec, aspec, tile, tile, _const_spec((1, d)), wspec, wspec],
        out_specs=[tile, pl.BlockSpec((SUBLANES, d), lambda i: (0, 0))],
        out_shape=[SDS((s, d), F32), SDS((SUBLANES, d), F32)],
        scratch_shapes=[], args=(da1, da3, dh, hres1, g_ffn, w1, w3), sem=("arbitrary",))


def _ffn_wgrad2(h2, dhb, ff, da1, da3):
    s, d = h2.shape
    _, _, fc = ff.shape
    tm = min(TM_WGRAD, s)

    def body(h2_ref, dhb_ref, ff_ref, da1_ref, da3_ref, dw1_ref, dw3_ref, dw2_ref):
        @pl.when(pl.program_id(1) == 0)
        def _():
            dw1_ref[...] = jnp.zeros_like(dw1_ref)
            dw3_ref[...] = jnp.zeros_like(dw3_ref)
            dw2_ref[...] = jnp.zeros_like(dw2_ref)

        h2v = h2_ref[...]
        dw1_ref[0] += _dot_tn(da1_ref[0], h2v)
        dw3_ref[0] += _dot_tn(da3_ref[0], h2v)
        dw2_ref[0] += _dot_tn(ff_ref[0], dhb_ref[...])

    wspec = pl.BlockSpec((1, fc, d), lambda j, i: (j, 0, 0))
    return pl.pallas_call(
        body, grid=(N_CHIPS, s // tm), name="ffn_wgrad",
        in_specs=[pl.BlockSpec((tm, d), lambda j, i: (i, 0)), pl.BlockSpec((tm, d), lambda j, i: (i, 0))]
        + [pl.BlockSpec((1, tm, fc), lambda j, i: (j, i, 0))] * 3,
        out_specs=[wspec] * 3, out_shape=[SDS((N_CHIPS, fc, d), F32)] * 3,
        compiler_params=_cp(("parallel", "arbitrary")))(h2, dhb, ff, da1, da3)


def _mixer_bwd(u, saved, pooled, h, dhres1, sp_, w_out, hosted=None):
    s, din = u.shape
    d = dhres1.shape[1]
    lw = din // 3
    tm = min(TM_MIX, s)
    nt = s // tm
    nb = lw // GATE_BLOCK
    hd = sp_["gate_a_w"].shape[1]

    def body(ul_ref, saved_ref, pooled_ref, h_ref, hhalo_ref, dhr_ref, cw_ref, cb_ref, ga_ref, gx_ref, ba_ref,
             bx_ref, lam_ref, pw_ref, pb_ref, ps_ref, gl_ref, gp_ref, wout_ref, du_ref, slab_ref,
             gw_s, a_s, b_s, e_s, ecarry_s, dxc_s, q_s, vec_s, cwacc_s, dgw_s, dpw_s):
        i = pl.program_id(0)
        tile = nt - 1 - i

        @pl.when(i == 0)
        def _():
            _build_gate_blocks(ga_ref, gx_ref, gw_s)
            for ref in (ecarry_s, dxc_s, q_s, vec_s, cwacc_s, dgw_s, dpw_s):
                ref[...] = jnp.zeros_like(ref)

        cw = cw_ref[...]
        lam = lam_ref[...]
        ps = ps_ref[...]
        f = {name: saved_ref[k] for k, name in enumerate(MIX_SAVED)}
        f["mult"] = jnp.sqrt(jnp.maximum(f["m2raw"], 1e-12))
        f["sp"] = _softplus_neg(lam)
        f["xcb"] = f["xc"].astype(BF16)
        pooled = pooled_ref[...]
        row = lax.broadcasted_iota(jnp.int32, (tm, LANES), 0) + tile * tm
        f["invs"] = [1.0 / jnp.minimum(row + 1, w).astype(F32) for w in POOL_WINDOWS]
        f["z"] = jnp.concatenate(
            [_dot(pooled[:, g * LANES:(g + 1) * LANES], pw_ref[g].astype(BF16))
             for g in range(len(POOL_WINDOWS))], axis=1) + pb_ref[...]
        f["y_pool"] = f["z"] * ps
        u_l = ul_ref[...]
        hv = h_ref[...]
        h_prev = _shift_down(jnp.where(tile > 0, hhalo_ref[...], 0.0), hv, 1)
        y_lru = hv * f["ge"]
        rl = lax.rsqrt(_rowmean(y_lru * y_lru) + EPS)
        yp = f["y_pool"]
        rp = lax.rsqrt(_rowmean(yp * yp) + EPS)
        xh_l = y_lru * rl
        xh_p = yp * rp

        dyn = _dot_nt(dhr_ref[...].astype(BF16), wout_ref[...])
        d_nl, d_np = dyn[:, :lw], dyn[:, lw:]
        vec = {}
        vec[ROW_GL] = _colsum8(d_nl * xh_l)
        vec[ROW_GP] = _colsum8(d_np * xh_p)
        d_ylru = _rms_bwd(d_nl, xh_l, rl, gl_ref[...])
        d_ypool = _rms_bwd(d_np, xh_p, rp, gp_ref[...])

        vec[ROW_PS] = _colsum8(d_ypool * f["z"])
        dz = d_ypool * ps
        vec[ROW_PB] = _colsum8(dz)
        dzb = dz.astype(BF16)
        dup = []
        for gi, w in enumerate(POOL_WINDOWS):
            sl = slice(gi * LANES, (gi + 1) * LANES)
            dpw_s[:, sl] += _dot_tn(pooled[:, sl], dzb[:, sl])
            dpool = _dot_nt(dzb[:, sl], pw_ref[gi].astype(BF16))
            q = dpool * f["invs"][gi]
            e = jnp.concatenate([q, q_s[:, sl]], axis=0)
            k = 1
            while k < w:
                e = e + pltpu.roll(e, tm + HALO - k, 0)
                k *= 2
            dup.append(e[:tm] - dpool)
            q_s[:, sl] = q[:HALO]

        d_hout = d_ylru * f["ge"]
        d_ug = d_ylru * hv * f["dge"]
        a = f["a"]
        a1, b1 = _scan_level1(a, a * d_hout, reverse=True)
        a_s[...] = a1
        b_s[...] = b1
        e_next = ecarry_s[...]
        ecarry_s[...] = _scan_level2(a_s, b_s, e_s, e_next, reverse=True)
        sv = d_hout + _shift_up(e_s[...], e_next, 1)
        d_a = sv * h_prev
        mult, ig, xc, r = f["mult"], f["ig"], f["xc"], f["r"]
        d_mult = sv * (ig * xc)
        d_ig = sv * mult * xc
        d_xc = sv * mult * ig
        d_la = d_a * a + jnp.where(f["m2raw"] > 1e-12, d_mult * (-(a * a) / mult), 0.0)
        d_r = d_la * (-LRU_C * f["sp"])
        vec[ROW_LAM] = _colsum8(d_la * (-LRU_C * r))
        d_pr = d_r * r * (1.0 - r)
        d_pi = d_ig * ig * (1.0 - ig)
        vec[ROW_BA] = _colsum8(d_pr)
        vec[ROW_BX] = _colsum8(d_pi)
        dxc_parts = []
        for b in range(nb):
            sl = slice(b * GATE_BLOCK, (b + 1) * GATE_BLOCK)
            rhs = jnp.concatenate([d_pr[:, sl], d_pi[:, sl]], axis=1).astype(BF16)
            dgw_s[b] += _dot_tn(f["xcb"][:, sl], rhs)
            dxc_parts.append(_dot_nt(rhs, gw_s[b]))
        d_xc = d_xc + jnp.concatenate(dxc_parts, axis=1)
        vec[ROW_CONV_B] = _colsum8(d_xc)
        dxc_next = dxc_s[...]
        d_ul = None
        for k in range(CONV_WIDTH):
            ahead = _shift_up(d_xc, dxc_next, CONV_WIDTH - 1 - k)
            cwacc_s[k * SUBLANES:(k + 1) * SUBLANES, :] += _colsum8(ahead * u_l)
            term = ahead * cw[k:k + 1, :]
            d_ul = term if d_ul is None else d_ul + term
        dxc_s[...] = d_xc[:SUBLANES]
        for row, val in vec.items():
            vec_s[row * SUBLANES:(row + 1) * SUBLANES, :] += val
        du_ref[...] = jnp.concatenate([d_ul, d_ug] + dup, axis=1).astype(BF16)

        @pl.when(i == nt - 1)
        def _():
            rows = []
            for row in range(ROW_GA):
                if row in (ROW_CONV_W, ROW_CONV_W + 1, ROW_CONV_W + 2, ROW_CONV_W + 3):
                    k = row - ROW_CONV_W
                    v = jnp.sum(cwacc_s[k * SUBLANES:(k + 1) * SUBLANES, :], axis=0, keepdims=True)
                elif row <= ROW_GP:
                    v = jnp.sum(vec_s[row * SUBLANES:(row + 1) * SUBLANES, :], axis=0, keepdims=True)
                    if row == ROW_LAM:
                        v = v * (-1.0 / (1.0 + jnp.exp(lam)))
                else:
                    v = jnp.zeros((1, lw), F32)
                rows.append(v)
            slab_ref[0:ROW_GA, :] = jnp.concatenate(rows, axis=0)
            lane = lax.broadcasted_iota(jnp.int32, (hd, GATE_BLOCK), 1)
            for b in range(nb):
                for off, row0 in ((0, ROW_GA), (GATE_BLOCK, ROW_GX)):
                    acc = jnp.zeros((hd, GATE_BLOCK), F32)
                    for hh in range(GATE_BLOCK // hd):
                        m = (lane >= hh * hd) & (lane < (hh + 1) * hd)
                        acc = acc + jnp.where(m, dgw_s[b, hh * hd:(hh + 1) * hd, off:off + GATE_BLOCK], 0.0)
                    slab_ref[row0:row0 + hd, b * GATE_BLOCK:(b + 1) * GATE_BLOCK] = acc
            slab_ref[ROW_PW:ROW_PW + LANES, :] = dpw_s[...]

    small = [sp_[k] for k in ("conv_w", "conv_b", "gate_a_w", "gate_x_w", "gate_a_b", "gate_x_b", "lru_lambda",
                              "pool_w", "pool_b", "pool_scale", "norm_lru_g", "norm_pool_g")]
    rev = lambda i: nt - 1 - i

    def stages():
        i = pl.program_id(0)
        return i == 0, i == max(nt - 3, 0), i == nt - 1

    return _call(
        body, hosted, stages, grid=(nt,), name="mixer_bwd",
        in_specs=[pl.BlockSpec((tm, lw), lambda i: (rev(i), 0)),
                  pl.BlockSpec((len(MIX_SAVED), tm, lw), lambda i: (0, rev(i), 0)),
                  pl.BlockSpec((tm, lw), lambda i: (rev(i), 0)),
                  pl.BlockSpec((tm, lw), lambda i: (rev(i), 0)),
                  pl.BlockSpec((SUBLANES, lw), lambda i: (jnp.maximum(rev(i) * (tm // SUBLANES) - 1, 0), 0)),
                  pl.BlockSpec((tm, d), lambda i: (rev(i), 0))]
        + [_const_spec(a.shape) for a in small] + [_const_spec(w_out.shape)],
        out_specs=[pl.BlockSpec((tm, din), lambda i: (rev(i), 0)),
                   pl.BlockSpec((MIX_SLAB_ROWS, SLAB_W), lambda i: (0, 0))],
        out_shape=[SDS((s, din), BF16), SDS((MIX_SLAB_ROWS, SLAB_W), F32)],
        scratch_shapes=[pltpu.VMEM((nb, GATE_BLOCK, 2 * GATE_BLOCK), BF16),
                        pltpu.VMEM((tm, lw), F32), pltpu.VMEM((tm, lw), F32), pltpu.VMEM((tm, lw), F32),
                        pltpu.VMEM((SUBLANES, lw), F32), pltpu.VMEM((SUBLANES, lw), F32),
                        pltpu.VMEM((HALO, lw), F32), pltpu.VMEM((ROW_GA * SUBLANES, lw), F32),
                        pltpu.VMEM((CONV_WIDTH * SUBLANES, lw), F32),
                        pltpu.VMEM((nb, GATE_BLOCK, 2 * GATE_BLOCK), F32), pltpu.VMEM((LANES, lw), F32)],
        args=(u, saved, pooled, h, h, dhres1, *small, w_out), sem=("arbitrary",))


def _inproj_bwd(x, du, dhres1, yn, g_mix, w_in, hosted=None):
    s, d = x.shape
    n = w_in.shape[1]
    nc = n // N_CHIPS
    tm = min(TM_PROJ, s)
    nt = s // tm

    def body(x_ref, du_ref, dhr_ref, yn_ref, g_ref, w_ref, gx_ref, dwin_ref, dwout_ref, dg_ref):
        i = pl.program_id(0)

        @pl.when(i == 0)
        def _():
            dwin_ref[...] = jnp.zeros_like(dwin_ref)
            dwout_ref[...] = jnp.zeros_like(dwout_ref)
            dg_ref[...] = jnp.zeros_like(dg_ref)

        xv = x_ref[...]
        g = g_ref[...]
        r = lax.rsqrt(_rowmean(xv * xv) + EPS)
        xh = xv * r
        h1 = (xh * g).astype(BF16)
        duv = du_ref[...]
        dh1 = _dot_nt(duv, w_ref[...])
        dg_ref[...] += _colsum8(dh1 * xh)
        dhr = dhr_ref[...]
        gx_ref[...] = dhr + _rms_bwd(dh1, xh, r, g)
        for jj in range(N_CHIPS):
            dwin_ref[jj] += _dot_tn(h1, duv[:, jj * nc:(jj + 1) * nc])
        dwout_ref[...] += _dot_tn(yn_ref[...], dhr.astype(BF16))

    def stages():
        i = pl.program_id(0)
        return i == 0, i == max(nt - 3, 0), i == nt - 1

    return _call(
        body, hosted, stages, grid=(nt,), name="inproj_bwd",
        in_specs=[pl.BlockSpec((tm, d), lambda i: (i, 0)), pl.BlockSpec((tm, n), lambda i: (i, 0)),
                  pl.BlockSpec((tm, d), lambda i: (i, 0)), pl.BlockSpec((tm, d), lambda i: (i, 0)),
                  _const_spec((1, d)), _const_spec((d, n))],
        out_specs=[pl.BlockSpec((tm, d), lambda i: (i, 0)), pl.BlockSpec((N_CHIPS, d, nc), lambda i: (0, 0, 0)),
                   pl.BlockSpec((d, d), lambda i: (0, 0)), pl.BlockSpec((SUBLANES, d), lambda i: (0, 0))],
        out_shape=[SDS((s, d), F32), SDS((N_CHIPS, d, nc), F32), SDS((d, d), F32), SDS((SUBLANES, d), F32)],
        scratch_shapes=[], args=(x, du, dhres1, yn, g_mix, w_in), sem=("arbitrary",))


def _place():
    x, y, c = lax.axis_index("x"), lax.axis_index("y"), lax.axis_index("c")
    return x, y, c


def _other_chips(x, y):
    return [(1 - x, y), (x, 1 - y), (1 - x, 1 - y)]


ANY = pl.BlockSpec(memory_space=pl.ANY)
VMEM_SPEC = pl.BlockSpec(memory_space=pltpu.VMEM)

_GATHERED = {"w_in": "cols", "w_out": "major", "ffn_w1": "major", "ffn_w3": "major", "ffn_w2": "major"}
_BIG = ("w_in", "w_out", "ffn_w1", "ffn_w3", "ffn_w2")


def _gather_weights(shards, conv_w, n_remote):
    n = len(shards)
    full_shapes = []
    for name, sh in zip(_BIG, shards):
        r, cdim = sh.shape
        if _GATHERED[name] == "cols":
            assert cdim % LANES == 0
            full_shapes.append((r, cdim * N_CHIPS))
        else:
            full_shapes.append((N_CHIPS, r, cdim))

    def region(ref, name, sh, jj, cc):
        r, cdim = sh
        rows = pl.ds(0, r) if cc is None else pl.ds(pl.multiple_of(cc * (r // 2), 16), r // 2)
        if _GATHERED[name] == "cols":
            return ref.at[rows, pl.ds(pl.multiple_of(jj * cdim, LANES), cdim)]
        return ref.at[jj, rows, :]

    def staged(ref, sh, cc):
        r = sh[0]
        return ref.at[pl.ds(pl.multiple_of(cc * (r // 2), 16), r // 2), :]

    def body(*refs):
        ins, cw_in = refs[:n], refs[n]
        outs, cw_out = refs[n + 1:2 * n + 1], refs[2 * n + 1]
        stage = refs[2 * n + 2:3 * n + 2]
        cw_stage, lsem, ssem, rsem, fssem, frsem, cssem, crsem = refs[3 * n + 2:]
        x, y, c = _place()
        j = 2 * x + y
        chips = _other_chips(x, y)
        for w in range(n):
            stage[w][...] = ins[w][...].astype(BF16)
        cw_stage[...] = jnp.zeros_like(cw_stage)
        cw_stage[0:CONV_WIDTH, :] = cw_in[...]
        shs = [s_.shape for s_ in shards]
        local = [pltpu.make_async_copy(stage[w], region(outs[w], _BIG[w], shs[w], j, None), lsem.at[w])
                 for w in range(n)]
        local.append(pltpu.make_async_copy(cw_stage, cw_out.at[j], lsem.at[n]))
        for cp in local:
            cp.start()
        sends = []
        for k, (px, py) in enumerate(chips):
            for w in range(n_remote):
                sends.append(pltpu.make_async_remote_copy(
                    src_ref=staged(stage[w], shs[w], c), dst_ref=region(outs[w], _BIG[w], shs[w], j, c),
                    send_sem=ssem.at[k * n + w], recv_sem=rsem.at[k * n + w], device_id=(px, py, c),
                    device_id_type=MESH))
            sends.append(pltpu.make_async_remote_copy(
                src_ref=cw_stage, dst_ref=cw_out.at[j], send_sem=cssem.at[k], recv_sem=crsem.at[k],
                device_id=(px, py, c), device_id_type=MESH))
        for cp in sends:
            cp.start()
        fwd = []
        for k, (px, py) in enumerate(chips):
            jk = 2 * px + py
            for w in range(n_remote):
                reg = region(outs[w], _BIG[w], shs[w], jk, c)
                pltpu.make_async_remote_copy(src_ref=reg, dst_ref=reg, send_sem=ssem.at[k * n + w],
                                             recv_sem=rsem.at[k * n + w], device_id=(px, py, c),
                                             device_id_type=MESH).wait_recv()
                cp = pltpu.make_async_remote_copy(src_ref=reg, dst_ref=reg, send_sem=fssem.at[k * n + w],
                                                  recv_sem=frsem.at[k * n + w], device_id=(x, y, 1 - c),
                                                  device_id_type=MESH)
                cp.start()
                fwd.append(cp)
            pltpu.make_async_remote_copy(src_ref=cw_stage, dst_ref=cw_out.at[jk], send_sem=cssem.at[k],
                                         recv_sem=crsem.at[k], device_id=(px, py, c),
                                         device_id_type=MESH).wait_recv()
        for k, (px, py) in enumerate(chips):
            jk = 2 * px + py
            for w in range(n_remote):
                reg = region(outs[w], _BIG[w], shs[w], jk, 1 - c)
                pltpu.make_async_remote_copy(src_ref=reg, dst_ref=reg, send_sem=fssem.at[k * n + w],
                                             recv_sem=frsem.at[k * n + w], device_id=(x, y, 1 - c),
                                             device_id_type=MESH).wait_recv()
        for cp in sends + fwd:
            cp.wait_send()
        for cp in local:
            cp.wait()

    nsem = 3 * n
    return pl.pallas_call(
        body, name="gather_first",
        in_specs=[VMEM_SPEC] * (n + 1), out_specs=[ANY] * (n + 1),
        out_shape=[SDS(fs, BF16) for fs in full_shapes] + [SDS((N_CHIPS, SUBLANES, LANES), F32)],
        scratch_shapes=[pltpu.VMEM(s_.shape, BF16) for s_ in shards] + [pltpu.VMEM((SUBLANES, LANES), F32)]
        + [pltpu.SemaphoreType.DMA((n + 1,))] + [pltpu.SemaphoreType.DMA((nsem,))] * 4
        + [pltpu.SemaphoreType.DMA((3,))] * 2,
        compiler_params=_cp())(*shards, conv_w)


def _start_all(make):
    def f(ins, outs, sems):
        for cp in make(ins, outs, sems):
            cp.start()
    return f


def _wait_all(make):
    def f(ins, outs, sems):
        for cp in make(ins, outs, sems):
            cp.wait()
    return f


def _ffn_gather_hosted(arrs):
    n = len(arrs)

    def make(outs, sems):
        ssem, rsem, fs, fr = sems
        x, y, c = _place()
        j = 2 * x + y

        def reg(w, jj, cc):
            hr = arrs[w].shape[1] // 2
            return outs[w].at[jj, pl.ds(pl.multiple_of(cc * hr, 16), hr), :]

        def rc(w, jj, cc, s_sem, r_sem, dev):
            return pltpu.make_async_remote_copy(src_ref=reg(w, jj, cc), dst_ref=reg(w, jj, cc), send_sem=s_sem,
                                                recv_sem=r_sem, device_id=dev, device_id_type=MESH)

        sends, recvs, fwds, frecvs = [], [], [], []
        for k, (px, py) in enumerate(_other_chips(x, y)):
            jk = 2 * px + py
            for w in range(n):
                q = k * n + w
                sends.append(rc(w, j, c, ssem.at[q], rsem.at[q], (px, py, c)))
                recvs.append(rc(w, jk, c, ssem.at[q], rsem.at[q], (px, py, c)))
                fwds.append(rc(w, jk, c, fs.at[q], fr.at[q], (x, y, 1 - c)))
                frecvs.append(rc(w, jk, 1 - c, fs.at[q], fr.at[q], (x, y, 1 - c)))
        return sends, recvs, fwds, frecvs

    def start(ins, outs, sems):
        for cp in make(outs, sems)[0]:
            cp.start()

    def mid(ins, outs, sems):
        _, recvs, fwds, _ = make(outs, sems)
        for r, f in zip(recvs, fwds):
            r.wait_recv()
            f.start()

    def finish(ins, outs, sems):
        sends, _, fwds, frecvs = make(outs, sems)
        for r in frecvs:
            r.wait_recv()
        for cp in sends + fwds:
            cp.wait_send()

    return _Hosted(arrs, [SDS(a.shape, a.dtype) for a in arrs], [3 * n] * 4, start, finish, mid=mid,
                   aliases={w: w for w in range(n)})


def _rs_sibling_hosted(arrs):
    n = len(arrs)

    def make(ins, outs, sems):
        x, y, c = _place()
        cps = []
        for w in range(n):
            hr = arrs[w].shape[1] // 2
            src = ins[w].at[:, pl.ds(pl.multiple_of((1 - c) * hr, SUBLANES), hr), :]
            cps.append(pltpu.make_async_remote_copy(src_ref=src, dst_ref=outs[w], send_sem=sems[0].at[w],
                                                    recv_sem=sems[1].at[w], device_id=(x, y, 1 - c),
                                                    device_id_type=MESH))
        return cps

    return _Hosted(arrs, [SDS((a.shape[0], a.shape[1] // 2, a.shape[2]), F32) for a in arrs], [n, n],
                   _start_all(make), _wait_all(make))


def _rs_chips_hosted(parts):
    n = len(parts)

    def make(ins, outs, sems):
        x, y, c = _place()
        j = 2 * x + y
        cps = []
        for k, (px, py) in enumerate(_other_chips(x, y)):
            jk = 2 * px + py
            for w in range(n):
                cps.append(pltpu.make_async_remote_copy(
                    src_ref=ins[w].at[jk], dst_ref=outs[w].at[j], send_sem=sems[0].at[k * n + w],
                    recv_sem=sems[1].at[k * n + w], device_id=(px, py, c), device_id_type=MESH))
        return cps

    return _Hosted(parts, [SDS(p.shape, p.dtype) for p in parts], [3 * n, 3 * n], _start_all(make), _wait_all(make))


def _rs_swap_hosted(halves):
    n = len(halves)

    def make(ins, outs, sems):
        x, y, c = _place()
        return [pltpu.make_async_remote_copy(src_ref=ins[w], dst_ref=outs[w], send_sem=sems[0].at[w],
                                             recv_sem=sems[1].at[w], device_id=(x, y, 1 - c), device_id_type=MESH)
                for w in range(n)]

    return _Hosted(halves, [SDS(h.shape, F32) for h in halves], [n, n], _start_all(make), _wait_all(make))


def _run_comm(hosted, name):
    return _call(lambda: None, hosted, None, name=name, grid=(), in_specs=[], out_specs=[], out_shape=[],
                 scratch_shapes=[], args=(), sem=None)[1]


def _row_tile(rows, cols, n_arrays):
    budget = 24 * 1024 * 1024 // (2 * 4 * n_arrays * cols)
    best = SUBLANES
    for t in range(SUBLANES, rows + 1, SUBLANES):
        if rows % t == 0 and t <= budget:
            best = t
    return best


def _place_index(which):
    x, y, c = _place()
    v = c if which == "c" else 2 * x + y
    return jnp.reshape(v, (1,)).astype(jnp.int32)


def _add_own_half(full, recv, name):
    nsh, rows, cols = full.shape
    hr = rows // 2
    t = _row_tile(hr, cols, 4)
    nt = hr // t

    def body(c_ref, a_ref, b_ref, o_ref, ob_ref):
        v = a_ref[...] + b_ref[...]
        o_ref[...] = v
        ob_ref[...] = v.astype(BF16)

    half = pl.BlockSpec((1, t, cols), lambda s_, i, c_ref: (s_, i, 0))
    return pl.pallas_call(
        body, name=name,
        grid_spec=pltpu.PrefetchScalarGridSpec(
            num_scalar_prefetch=1, grid=(nsh, nt),
            in_specs=[pl.BlockSpec((1, t, cols), lambda s_, i, c_ref: (s_, c_ref[0] * nt + i, 0)), half],
            out_specs=[half, half]),
        out_shape=[SDS((nsh, hr, cols), F32), SDS((nsh, hr, cols), BF16)],
        compiler_params=_cp(("parallel", "parallel")))(_place_index("c"), full, recv)


def _sum_chips(own, recv, name):
    nsh, hr, cols = own.shape
    t = _row_tile(hr, cols, 6)

    def body(j_ref, own_ref, *rest):
        r_refs, o_ref = rest[:nsh], rest[nsh]
        j = j_ref[0]
        mine = own_ref[0]
        parts = [jnp.where(j == k, mine, r_refs[k][0].astype(F32)) for k in range(nsh)]
        o_ref[...] = ((parts[0] + parts[1]) + parts[2]) + parts[3]

    def other(k):
        return pl.BlockSpec((1, t, cols), lambda i, j_ref: (jnp.where(j_ref[0] == k, (k + 1) % nsh, k), i, 0))

    return pl.pallas_call(
        body, name=name,
        grid_spec=pltpu.PrefetchScalarGridSpec(
            num_scalar_prefetch=1, grid=(hr // t,),
            in_specs=[pl.BlockSpec((1, t, cols), lambda i, j_ref: (j_ref[0], i, 0))]
            + [other(k) for k in range(nsh)],
            out_specs=pl.BlockSpec((t, cols), lambda i, j_ref: (i, 0))),
        out_shape=SDS((hr, cols), F32), compiler_params=_cp(("parallel",)))(_place_index("j"), own, *([recv] * nsh))


def _adamw_math(w, g, m, v):
    m = ADAM_B1 * m + (1.0 - ADAM_B1) * g
    v = ADAM_B2 * v + (1.0 - ADAM_B2) * (g * g)
    m_hat = m / (1.0 - ADAM_B1 ** ADAM_STEP)
    v_hat = v / (1.0 - ADAM_B2 ** ADAM_STEP)
    delta = -ADAM_LR * (m_hat / (jnp.sqrt(v_hat) + ADAM_EPS) + ADAM_WD * w)
    return delta, m, v


def _adamw_big(w, g_own, g_sib, m, v, name):
    _, rows, cols = w.shape
    hr = rows // 2
    t = _row_tile(hr, cols, 9)
    nth = hr // t

    def body(c_ref, w_ref, go_ref, gs_ref, m_ref, v_ref, g_ref, d_ref, mo_ref, vo_ref):
        own = (pl.program_id(0) // nth) == c_ref[0]
        g = jnp.where(own, go_ref[...], gs_ref[...])
        g_ref[0] = g
        d_ref[0], mo_ref[0], vo_ref[0] = _adamw_math(w_ref[0], g, m_ref[0], v_ref[0])

    spec = pl.BlockSpec((1, t, cols), lambda i, c_ref: (0, i, 0))
    hspec = pl.BlockSpec((t, cols), lambda i, c_ref: (i % nth, 0))
    return pl.pallas_call(
        body, name=name,
        grid_spec=pltpu.PrefetchScalarGridSpec(
            num_scalar_prefetch=1, grid=(2 * nth,), in_specs=[spec, hspec, hspec, spec, spec],
            out_specs=[spec] * 4),
        out_shape=[SDS((1, rows, cols), F32)] * 4,
        compiler_params=_cp(("parallel",)))(_place_index("c"), w, g_own, g_sib, m, v)


def _allreduce_small(mix_slab, dg_mix, dg_ffn, dg_fin, loss8):
    half = SLAB_ROWS // 2

    def body(ms_ref, gm_ref, gf_ref, gn_ref, loss_ref, out_ref, loc_s, sib_s, chip_s, r2_s, fin_s, sems):
        x, y, c = _place()
        j = 2 * x + y
        rows = []
        for ref in (gm_ref, gf_ref, gn_ref):
            v = jnp.sum(ref[...], axis=0, keepdims=True)
            rows += [v[:, :SLAB_W], v[:, SLAB_W:]]
        rows.append(jnp.concatenate([loss_ref[0:1, :]] * (SLAB_W // LANES), axis=1))
        rows.append(jnp.zeros((SLAB_ROWS - ROW_LOSS - 1, SLAB_W), F32))
        loc_s[0:MIX_SLAB_ROWS, :] = ms_ref[...]
        loc_s[MIX_SLAB_ROWS:SLAB_ROWS, :] = jnp.concatenate(rows, axis=0)
        sib = (x, y, 1 - c)
        cp = pltpu.make_async_remote_copy(src_ref=loc_s, dst_ref=sib_s, send_sem=sems.at[0], recv_sem=sems.at[1],
                                          device_id=sib, device_id_type=MESH)
        cp.start()
        cp.wait()
        chip_s[...] = loc_s[...] + sib_s[...]
        mine = chip_s.at[pl.ds(pl.multiple_of(c * half, SUBLANES), half), :]
        r2_s[j] = chip_s[pl.ds(pl.multiple_of(c * half, SUBLANES), half), :]
        cps = []
        for k, (px, py) in enumerate(_other_chips(x, y)):
            cps.append(pltpu.make_async_remote_copy(src_ref=mine, dst_ref=r2_s.at[j], send_sem=sems.at[2 + k],
                                                    recv_sem=sems.at[5 + k], device_id=(px, py, c),
                                                    device_id_type=MESH))
        for cp in cps:
            cp.start()
        for cp in cps:
            cp.wait()
        fin_s[...] = ((r2_s[0] + r2_s[1]) + r2_s[2]) + r2_s[3]
        dst = out_ref.at[pl.ds(pl.multiple_of(c * half, SUBLANES), half), :]
        out_ref[pl.ds(pl.multiple_of(c * half, SUBLANES), half), :] = fin_s[...]
        cp = pltpu.make_async_remote_copy(src_ref=fin_s, dst_ref=dst, send_sem=sems.at[8], recv_sem=sems.at[9],
                                          device_id=sib, device_id_type=MESH)
        cp.start()
        cp.wait()

    return pl.pallas_call(
        body, name="allreduce_small", in_specs=[VMEM_SPEC] * 5, out_specs=VMEM_SPEC,
        out_shape=SDS((SLAB_ROWS, SLAB_W), F32),
        scratch_shapes=[pltpu.VMEM((SLAB_ROWS, SLAB_W), F32)] * 3 + [pltpu.VMEM((N_CHIPS, half, SLAB_W), F32),
                                                                       pltpu.VMEM((half, SLAB_W), F32),
                                                                       pltpu.SemaphoreType.DMA((10,))],
        compiler_params=_cp())(mix_slab, dg_mix, dg_ffn, dg_fin, loss8)


_SMALL_ROWS = (("conv_b", ROW_CONV_B), ("gate_a_b", ROW_BA), ("gate_x_b", ROW_BX), ("lru_lambda", ROW_LAM),
               ("pool_b", ROW_PB), ("pool_scale", ROW_PS), ("norm_lru_g", ROW_GL), ("norm_pool_g", ROW_GP))
_WIDE_ROWS = (("norm_mix_g", ROW_MIX), ("norm_ffn_g", ROW_FFN), ("final_norm_g", ROW_FIN))
_BLOCK_ROWS = (("gate_a_w", ROW_GA), ("gate_x_w", ROW_GX), ("pool_w", ROW_PW))
_SMALL_ORDER = tuple(n for n, _ in _SMALL_ROWS) + tuple(n for n, _ in _WIDE_ROWS) + tuple(
    n for n, _ in _BLOCK_ROWS) + ("conv_w",)


def _adamw_small(slab, wmv):
    names = _SMALL_ORDER
    flat = [a for nme in names for a in wmv[nme]]
    nin = len(flat)

    def body(*refs):
        slab_ref, j_ref = refs[0], refs[1]
        ins = refs[2:2 + nin]
        outs = refs[2 + nin:]
        grads = {}
        for nme, row in _SMALL_ROWS:
            grads[nme] = slab_ref[row:row + 1, :]
        for nme, row in _WIDE_ROWS:
            grads[nme] = jnp.concatenate([slab_ref[row:row + 1, :], slab_ref[row + 1:row + 2, :]], axis=1)
        full = slab_ref[ROW_CONV_W:ROW_CONV_W + CONV_WIDTH, :]
        jv = j_ref[0]
        g = jnp.zeros((CONV_WIDTH, LANES), F32)
        for jj in range(N_CHIPS):
            g = jnp.where(jv == jj, full[:, jj * LANES:(jj + 1) * LANES], g)
        grads["conv_w"] = g
        block_rows = dict(_BLOCK_ROWS)
        for idx, nme in enumerate(names):
            w_ref, m_ref, v_ref = ins[3 * idx:3 * idx + 3]
            if nme in block_rows:
                nblk, r, c = w_ref.shape
                parts = [(b, slab_ref[block_rows[nme]:block_rows[nme] + r, b * c:(b + 1) * c]) for b in range(nblk)]
            else:
                parts = [(Ellipsis, grads[nme])]
            for b, g in parts:
                delta, m, v = _adamw_math(w_ref[b], g, m_ref[b], v_ref[b])
                outs[4 * idx][b] = g
                outs[4 * idx + 1][b] = delta
                outs[4 * idx + 2][b] = m
                outs[4 * idx + 3][b] = v

    x, y, _ = _place()
    jidx = jnp.reshape(2 * x + y, (1,)).astype(jnp.int32)
    out_shape = [SDS(wmv[nme][0].shape, F32) for nme in names for _ in range(4)]
    res = pl.pallas_call(
        body, name="adamw_small",
        in_specs=[VMEM_SPEC, pl.BlockSpec(memory_space=pltpu.SMEM)] + [VMEM_SPEC] * nin,
        out_specs=[VMEM_SPEC] * len(out_shape), out_shape=out_shape, compiler_params=_cp())(slab, jidx, *flat)
    return {nme: tuple(res[4 * idx:4 * idx + 4]) for idx, nme in enumerate(names)}


_FFN = ("ffn_w1", "ffn_w3", "ffn_w2")
_TRANSPOSED = ("ffn_w1", "ffn_w3")


def _local_step(x, target, full, sp_, distributed):
    d = x.shape[1]
    (u,), got = _inproj(x, sp_["norm_mix_g"], full["w_in"],
                        [_ffn_gather_hosted([full["w_out"]])] if distributed else None)
    w_out = (got[0][0] if distributed else full["w_out"]).reshape(d, d)
    gather = [_ffn_gather_hosted([full[n] for n in _FFN])] if distributed else None
    (h, yn, hres1, saved, pooled), got = _mixer_fwd(u, x, sp_, w_out, gather)
    w1, w3, w2 = got[0] if distributed else [full[n] for n in _FFN]
    h2, a1, a3, ff = _ffn_up(hres1, sp_["norm_ffn_g"], w1, w3)
    dh, dhb, loss8, dg_fin = _ffn_down(ff, hres1, target, sp_["final_norm_g"], w2)
    da1, da3 = _ffn_bwd_gate(dhb, a1, a3, w2)
    dws = list(_ffn_wgrad2(h2, dhb, ff, da1, da3))
    rs1 = [_rs_sibling_hosted(dws)] if distributed else None
    (dhres1, dg_ffn), got = _ffn_bwd_down(da1, da3, dh, hres1, sp_["norm_ffn_g"], w1, w3, rs1)
    rs2 = None
    if distributed:
        pairs = [_add_own_half(a, r, "add_half_" + n) for n, a, r in zip(_FFN, dws, got[0])]
        rs2 = [_rs_chips_hosted([pb for _, pb in pairs])]
    (du, mix_slab), got = _mixer_bwd(u, saved, pooled, h, dhres1, sp_, w_out, rs2)
    (gx, dwin, dwout, dg_mix), _ = _inproj_bwd(x, du, dhres1, yn, sp_["norm_mix_g"], full["w_in"])
    big = {"w_in": dwin, "w_out": dwout.reshape(N_CHIPS, d // N_CHIPS, d)}
    for k, n in enumerate(_FFN):
        big[n] = (pairs[k][0], got[0][k]) if distributed else dws[k]
    return gx, big, (mix_slab, dg_mix, dg_ffn, dg_fin, loss8)


def _to_compact(w):
    h, i, j = w.shape
    return jnp.transpose(w, (1, 0, 2)).reshape(i, h * j)


def _from_compact(w, h):
    i, hj = w.shape
    return jnp.transpose(w.reshape(i, h, hj // h), (1, 0, 2))


_SMALL_LAYOUT = {
    "gate_a_w": (lambda a: a[0], lambda a: a[None]),
    "gate_x_w": (lambda a: a[0], lambda a: a[None]),
    "pool_w": (lambda a: a[0], lambda a: a[None]),
    "conv_w": (lambda a: a[0], lambda a: a[None]),
    "final_norm_g": (lambda a: a[None], lambda a: a[0]),
}

_WEIGHTS = ("norm_mix_g", "w_in", "conv_w", "conv_b", "gate_a_w", "gate_a_b", "gate_x_w", "gate_x_b", "lru_lambda",
            "pool_w", "pool_b", "pool_scale", "norm_lru_g", "norm_pool_g", "w_out", "norm_ffn_g", "ffn_w1",
            "ffn_w3", "ffn_w2", "final_norm_g")


def kernel(x, norm_mix_g, w_in, conv_w, conv_b, gate_a_w, gate_a_b, gate_x_w, gate_x_b, lru_lambda, pool_w, pool_b, pool_scale, norm_lru_g, norm_pool_g, w_out, norm_ffn_g, ffn_w1, ffn_w3, ffn_w2, final_norm_g, loss_target, m_norm_mix_g, m_w_in, m_conv_w, m_conv_b, m_gate_a_w, m_gate_a_b, m_gate_x_w, m_gate_x_b, m_lru_lambda, m_pool_w, m_pool_b, m_pool_scale, m_norm_lru_g, m_norm_pool_g, m_w_out, m_norm_ffn_g, m_ffn_w1, m_ffn_w3, m_ffn_w2, m_final_norm_g, v_norm_mix_g, v_w_in, v_conv_w, v_conv_b, v_gate_a_w, v_gate_a_b, v_gate_x_w, v_gate_x_b, v_lru_lambda, v_pool_w, v_pool_b, v_pool_scale, v_norm_lru_g, v_norm_pool_g, v_w_out, v_norm_ffn_g, v_ffn_w1, v_ffn_w3, v_ffn_w2, v_final_norm_g):
    loc = locals()
    w = {n: loc[n] for n in _WEIGHTS}
    m = {n: loc["m_" + n] for n in _WEIGHTS}
    v = {n: loc["v_" + n] for n in _WEIGHTS}

    def lay(nme, a):
        return _SMALL_LAYOUT[nme][0](a) if nme in _SMALL_LAYOUT else a

    def unlay(nme, a):
        return _SMALL_LAYOUT[nme][1](a) if nme in _SMALL_LAYOUT else a

    for group in (w, m, v):
        for n in _TRANSPOSED:
            group[n] = jnp.transpose(group[n], (0, 2, 1))

    gathered = _gather_weights([w[n][0] for n in _BIG], w["conv_w"][0], n_remote=1)
    full = dict(zip(_BIG, gathered[:-1]))
    cw_all = gathered[-1]
    sp_ = {n: lay(n, w[n]) for n in _SMALL_ORDER}
    sp_["conv_w"] = jnp.transpose(cw_all[:, :CONV_WIDTH, :], (1, 0, 2)).reshape(CONV_WIDTH, N_CHIPS * LANES)

    gx, big, small = _local_step(x[0], loss_target[0], full, sp_, distributed=True)

    late = ("w_in", "w_out")
    fin = {n: _sum_chips(big[n][0], big[n][1], "sum_chips_" + n) for n in _FFN}
    recv1, swapped = _run_comm([_rs_sibling_hosted([big[n] for n in late]),
                                _rs_swap_hosted([fin[n] for n in _FFN])], "tail_sibling")
    sib = dict(zip(_FFN, swapped))
    pairs = [_add_own_half(big[n], r, "add_half_" + n) for n, r in zip(late, recv1)]
    recv2, = _run_comm([_rs_chips_hosted([pb for _, pb in pairs])], "tail_chips")
    for n, (p, _), r in zip(late, pairs, recv2):
        fin[n] = _sum_chips(p, r, "sum_chips_" + n)
    swapped, = _run_comm([_rs_swap_hosted([fin[n] for n in late])], "tail_swap")
    sib.update(zip(late, swapped))
    out = {}
    for n in _BIG:
        out[n] = tuple(_adamw_big(w[n], fin[n], sib[n], m[n], v[n], "adamw_" + n))
        if n in _TRANSPOSED:
            out[n] = tuple(jnp.transpose(a, (0, 2, 1)) for a in out[n])
    slab = _allreduce_small(*small)
    loss = slab[ROW_LOSS, 0]
    wmv = {n: (lay(n, w[n]), lay(n, m[n]), lay(n, v[n])) for n in _SMALL_ORDER}
    res = _adamw_small(slab, wmv)
    for n in _SMALL_ORDER:
        out[n] = tuple(unlay(n, a) for a in res[n])
    return (loss, gx[None]) + tuple(out[n][k] for k in range(4) for n in _WEIGHTS)
```

```python
import functools
import math

import jax
import jax.numpy as jnp
from jax import lax
from jax.experimental import pallas as pl
from jax.experimental.pallas import tpu as pltpu

F32 = jnp.float32
BF16 = jnp.bfloat16
SDS = jax.ShapeDtypeStruct
MESH = pl.DeviceIdType.MESH

EPS = 1e-6
LRU_C = 8.0
CONV_WIDTH = 4
POOL_WINDOWS = (2, 4, 8, 16)
HALO = 16
LANES = 128
SUBLANES = 8
GATE_BLOCK = 256
N_CHIPS = 4

ADAM_LR = 0.001
ADAM_B1 = 0.9
ADAM_B2 = 0.999
ADAM_EPS = 1e-08
ADAM_WD = 0.01
ADAM_STEP = 10

TM_PROJ = 512
TM_MIX = 512
TM_FFN = 512
TM_WGRAD = 1024
TM_FFN_UP = 1024
TM_FFN_DOWN = 512
MIX_SAVED = ("xc", "r", "ig", "a", "m2raw", "ge", "dge")
FFN_ROW_CHUNKS = 2
VMEM_LIMIT = 56 * 1024 * 1024

SLAB_W = 512
ROW_CONV_B, ROW_CONV_W, ROW_BA, ROW_BX, ROW_LAM, ROW_PB, ROW_PS, ROW_GL, ROW_GP = 0, 1, 5, 6, 7, 8, 9, 10, 11
ROW_GA, ROW_GX, ROW_PW = 16, 80, 144
ROW_MIX, ROW_FFN, ROW_FIN, ROW_LOSS = 272, 274, 276, 278
MIX_SLAB_ROWS = 272
SLAB_ROWS = 288


def _cp(sem=None, **kw):
    if sem is not None:
        kw["dimension_semantics"] = sem
    return pltpu.CompilerParams(vmem_limit_bytes=VMEM_LIMIT, **kw)


def _const_spec(shape):
    nd = len(shape)
    return pl.BlockSpec(shape, lambda *_: (0,) * nd, pipeline_mode=pl.Buffered(1))


def _sigmoid(x):
    return 1.0 / (1.0 + jnp.exp(-x))


def _dot(a, b):
    return jnp.dot(a, b, preferred_element_type=F32)


def _dot_nt(a, b):
    return lax.dot_general(a, b, (((1,), (1,)), ((), ())), preferred_element_type=F32)


def _dot_tn(a, b):
    return lax.dot_general(a, b, (((0,), (0,)), ((), ())), preferred_element_type=F32)


def _colsum8(v):
    m, c = v.shape
    return v.reshape(m // SUBLANES, SUBLANES, c).sum(axis=0)


def _rowmean(v):
    return jnp.mean(v, axis=-1, keepdims=True)


def _rms_bwd(dy, xhat, r, g):
    dxh = dy * g
    return r * (dxh - xhat * _rowmean(dxh * xhat))


def _softplus_neg(lam):
    z = -lam
    e = jnp.exp(-jnp.abs(z))
    u = 1.0 + e
    d = u - 1.0
    log1p = jnp.where(d == 0.0, e, jnp.log(u) * (e / jnp.where(d == 0.0, 1.0, d)))
    return jnp.maximum(z, 0.0) + log1p


def _neg_expm1(z):
    series = -(z * (1.0 + z * (0.5 + z * (1.0 / 6.0 + z * (1.0 / 24.0)))))
    return jnp.where(z > -0.03, series, 1.0 - jnp.exp(z))


_GELU_C = math.sqrt(2.0 / math.pi)
_GELU_K = 0.044715


def _gelu_parts(x):
    x2 = x * x
    th = jnp.tanh(_GELU_C * (x + _GELU_K * x2 * x))
    ge = 0.5 * x * (1.0 + th)
    dge = 0.5 * (1.0 + th) + 0.5 * x * (1.0 - th * th) * (_GELU_C * (1.0 + 3.0 * _GELU_K * x2))
    return ge, dge


def _shift_down(halo, tile, k):
    if k == 0:
        return tile
    ext = jnp.concatenate([halo, tile], axis=0)
    n = tile.shape[0]
    h = halo.shape[0]
    return ext[h - k:h - k + n]


def _shift_up(tile, nxt, k):
    if k == 0:
        return tile
    ext = jnp.concatenate([tile, nxt], axis=0)
    return ext[k:k + tile.shape[0]]


def _build_gate_blocks(ga_ref, gx_ref, gw_ref):
    hd = ga_ref.shape[1]
    per = GATE_BLOCK // hd
    zero = jnp.zeros((hd, hd), F32)
    for b in range(gw_ref.shape[0]):
        for src, off in ((ga_ref, 0), (gx_ref, GATE_BLOCK)):
            for hh in range(per):
                row = jnp.concatenate([zero] * hh + [src[b * per + hh]] + [zero] * (per - 1 - hh), axis=1)
                gw_ref[b, hh * hd:(hh + 1) * hd, off:off + GATE_BLOCK] = row.astype(BF16)


def _scan_level1(a, b, reverse):
    m, c = a.shape
    a3 = a.reshape(m // SUBLANES, SUBLANES, c)
    b3 = b.reshape(m // SUBLANES, SUBLANES, c)
    row = lax.broadcasted_iota(jnp.int32, a3.shape, 1)
    for s in (1, 2, 4):
        sh = (SUBLANES - s) if reverse else s
        a_sh = pltpu.roll(a3, sh, 1)
        b_sh = pltpu.roll(b3, sh, 1)
        ok = (row < SUBLANES - s) if reverse else (row >= s)
        b3 = jnp.where(ok, a3 * b_sh + b3, b3)
        a3 = jnp.where(ok, a3 * a_sh, a3)
    return a3.reshape(m, c), b3.reshape(m, c)


def _scan_level2(a_ref, b_ref, out_ref, carry, reverse):
    m, c = a_ref.shape
    ng = m // SUBLANES

    def step(g, cr):
        gi = (ng - 1 - g) if reverse else g
        off = pl.multiple_of(gi * SUBLANES, SUBLANES)
        h = b_ref[pl.ds(off, SUBLANES), :] + a_ref[pl.ds(off, SUBLANES), :] * cr
        out_ref[pl.ds(off, SUBLANES), :] = h
        edge = h[0:1, :] if reverse else h[SUBLANES - 1:SUBLANES, :]
        return jnp.broadcast_to(edge, (SUBLANES, c))

    return lax.fori_loop(0, ng, step, carry, unroll=4)


def _mixer_recompute(u, hal, t0, cw, cb, gw_ref, ba, bx, lam, pw_ref, pb, ps):
    tm = u.shape[0]
    lw = cb.shape[1]
    u_l, u_g, u_p = u[:, :lw], u[:, lw:2 * lw], u[:, 2 * lw:]
    hal_l, hal_p = hal[:, :lw], hal[:, 2 * lw:]
    taps = [_shift_down(hal_l, u_l, CONV_WIDTH - 1 - k) for k in range(CONV_WIDTH)]
    xc = cb
    for k in range(CONV_WIDTH):
        xc = xc + taps[k] * cw[k:k + 1, :]
    xcb = xc.astype(BF16)
    nb = lw // GATE_BLOCK
    gs = [_dot(xcb[:, b * GATE_BLOCK:(b + 1) * GATE_BLOCK], gw_ref[b]) for b in range(nb)]
    r = _sigmoid(jnp.concatenate([g[:, :GATE_BLOCK] for g in gs], axis=1) + ba)
    ig = _sigmoid(jnp.concatenate([g[:, GATE_BLOCK:] for g in gs], axis=1) + bx)
    sp = _softplus_neg(lam)
    la = (-LRU_C * r) * sp
    a = jnp.exp(la)
    m2raw = _neg_expm1(2.0 * la)
    mult = jnp.sqrt(jnp.maximum(m2raw, 1e-12))
    ge, dge = _gelu_parts(u_g)
    row = lax.broadcasted_iota(jnp.int32, (tm, LANES), 0) + t0
    pooled, invs, zs = [], [], []
    for gi, w in enumerate(POOL_WINDOWS):
        e = jnp.concatenate([hal_p[:, gi * LANES:(gi + 1) * LANES], u_p[:, gi * LANES:(gi + 1) * LANES]], axis=0)
        s = e
        k = 1
        while k < w:
            s = s + pltpu.roll(s, k, 0)
            k *= 2
        inv = 1.0 / jnp.minimum(row + 1, w).astype(F32)
        pg = s[HALO:] * inv - e[HALO:]
        pooled.append(pg)
        invs.append(inv)
        zs.append(_dot(pg.astype(BF16), pw_ref[gi].astype(BF16)))
    z = jnp.concatenate(zs, axis=1) + pb
    y_pool = z * ps
    return dict(u_l=u_l, u_g=u_g, taps=taps, xc=xc, xcb=xcb, r=r, ig=ig, sp=sp, la=la, a=a, m2raw=m2raw,
                mult=mult, ge=ge, dge=dge, pooled=pooled, invs=invs, z=z, y_pool=y_pool)


ANY = pl.BlockSpec(memory_space=pl.ANY)
VMEM_SPEC = pl.BlockSpec(memory_space=pltpu.VMEM)


class _Hosted:
    def __init__(self, ins, out_shapes, sems, start, finish, mid=None, aliases=None):
        self.ins, self.out_shapes, self.sems = list(ins), list(out_shapes), list(sems)
        self.start, self.mid, self.finish = start, mid, finish
        self.aliases = dict(aliases or {})


def _call(body, hosted, stage_preds, *, name, grid, in_specs, out_specs, out_shape, scratch_shapes, args, sem):
    hosted = list(hosted or [])
    n_in, n_out, n_scr = len(in_specs), len(out_specs), len(scratch_shapes)
    c_in = [a for h in hosted for a in h.ins]
    c_out = [o for h in hosted for o in h.out_shapes]
    c_sem = [pltpu.SemaphoreType.DMA((k,)) for h in hosted for k in h.sems]

    def full(*refs):
        p = 0
        parts = []
        for cnt in (n_in, len(c_in), n_out, len(c_out), n_scr, len(c_sem)):
            parts.append(refs[p:p + cnt])
            p += cnt
        hi, ci, ho, co, hs, cs = parts
        per = []
        a = b = c_ = 0
        for h in hosted:
            per.append((h, ci[a:a + len(h.ins)], co[b:b + len(h.out_shapes)], cs[c_:c_ + len(h.sems)]))
            a, b, c_ = a + len(h.ins), b + len(h.out_shapes), c_ + len(h.sems)
        first = mid = last = None
        if hosted and grid:
            first, mid, last = stage_preds()

        def run(fn, pred, i_, o_, s_):
            if fn is None:
                return
            if pred is None:
                fn(i_, o_, s_)
            else:
                pl.when(pred)(functools.partial(fn, i_, o_, s_))

        for h, i_, o_, s_ in per:
            run(h.start, first, i_, o_, s_)
        body(*hi, *ho, *hs)
        for h, i_, o_, s_ in per:
            run(h.mid, mid, i_, o_, s_)
        for h, i_, o_, s_ in per:
            run(h.finish, last, i_, o_, s_)

    aliases = {}
    a = b = 0
    for h in hosted:
        for k, v in h.aliases.items():
            aliases[n_in + a + k] = n_out + b + v
        a, b = a + len(h.ins), b + len(h.out_shapes)
    res = pl.pallas_call(
        full, name=name, grid=grid, in_specs=list(in_specs) + [ANY] * len(c_in),
        out_specs=list(out_specs) + [ANY] * len(c_out), out_shape=list(out_shape) + c_out,
        scratch_shapes=list(scratch_shapes) + c_sem, input_output_aliases=aliases,
        compiler_params=_cp(sem))(*args, *c_in)
    res = list(res)
    outs = []
    p = n_out
    for h in hosted:
        outs.append(res[p:p + len(h.out_shapes)])
        p += len(h.out_shapes)
    return res[:n_out], outs


def _inproj(x, g_mix, w_in, hosted=None):
    s, d = x.shape
    n = w_in.shape[1]
    tm = min(TM_PROJ, s)
    nt = s // tm

    def body(x_ref, g_ref, w_ref, u_ref):
        xv = x_ref[...]
        r = lax.rsqrt(_rowmean(xv * xv) + EPS)
        u_ref[...] = _dot((xv * r * g_ref[...]).astype(BF16), w_ref[...])

    def stages():
        i = pl.program_id(0)
        return i == 0, i == max(nt - 3, 0), i == nt - 1

    return _call(
        body, hosted, stages, grid=(nt,), name="inproj",
        in_specs=[pl.BlockSpec((tm, d), lambda i: (i, 0)), _const_spec((1, d)), _const_spec((d, n))],
        out_specs=[pl.BlockSpec((tm, n), lambda i: (i, 0))], out_shape=[SDS((s, n), F32)], scratch_shapes=[],
        args=(x, g_mix, w_in), sem=("arbitrary",))


def _mixer_fwd(u, x, sp_, w_out, hosted=None):
    s, din = u.shape
    d = x.shape[1]
    lw = din // 3
    tm = min(TM_MIX, s)
    nb = lw // GATE_BLOCK

    def body(u_ref, halo_ref, x_ref, cw_ref, cb_ref, ga_ref, gx_ref, ba_ref, bx_ref, lam_ref, pw_ref, pb_ref,
             ps_ref, gl_ref, gp_ref, wout_ref, h_ref, yn_ref, hres_ref, saved_ref, pooled_ref,
             gw_s, a_s, b_s, carry_s):
        i = pl.program_id(0)

        @pl.when(i == 0)
        def _():
            _build_gate_blocks(ga_ref, gx_ref, gw_s)
            carry_s[...] = jnp.zeros_like(carry_s)

        uv = u_ref[...]
        hal = jnp.where(i > 0, halo_ref[...], 0.0)
        f = _mixer_recompute(uv, hal, i * tm, cw_ref[...], cb_ref[...], gw_s, ba_ref[...], bx_ref[...],
                             lam_ref[...], pw_ref, pb_ref[...], ps_ref[...])
        for k, name in enumerate(MIX_SAVED):
            saved_ref[k] = f[name]
        pooled_ref[...] = jnp.concatenate(f["pooled"], axis=1).astype(BF16)
        bb = f["mult"] * (f["ig"] * f["xc"])
        a1, b1 = _scan_level1(f["a"], bb, reverse=False)
        a_s[...] = a1
        b_s[...] = b1
        carry_s[...] = _scan_level2(a_s, b_s, h_ref, carry_s[...], reverse=False)
        y_lru = h_ref[...] * f["ge"]
        rl = lax.rsqrt(_rowmean(y_lru * y_lru) + EPS)
        yp = f["y_pool"]
        rp = lax.rsqrt(_rowmean(yp * yp) + EPS)
        yn = jnp.concatenate([y_lru * rl * gl_ref[...], yp * rp * gp_ref[...]], axis=1).astype(BF16)
        yn_ref[...] = yn
        hres_ref[...] = x_ref[...] + _dot(yn, wout_ref[...])

    small = [sp_[k] for k in ("conv_w", "conv_b", "gate_a_w", "gate_x_w", "gate_a_b", "gate_x_b", "lru_lambda",
                              "pool_w", "pool_b", "pool_scale", "norm_lru_g", "norm_pool_g")]
    nt = s // tm

    def stages():
        i = pl.program_id(0)
        return i == 0, i == max(nt - 3, 0), i == nt - 1

    return _call(
        body, hosted, stages, grid=(nt,), name="mixer_fwd",
        in_specs=[pl.BlockSpec((tm, din), lambda i: (i, 0)),
                  pl.BlockSpec((HALO, din), lambda i: (jnp.maximum(i * (tm // HALO) - 1, 0), 0)),
                  pl.BlockSpec((tm, d), lambda i: (i, 0))]
        + [_const_spec(a.shape) for a in small] + [_const_spec(w_out.shape)],
        out_specs=[pl.BlockSpec((tm, lw), lambda i: (i, 0)), pl.BlockSpec((tm, d), lambda i: (i, 0)),
                   pl.BlockSpec((tm, d), lambda i: (i, 0)),
                   pl.BlockSpec((len(MIX_SAVED), tm, lw), lambda i: (0, i, 0)),
                   pl.BlockSpec((tm, lw), lambda i: (i, 0))],
        out_shape=[SDS((s, lw), F32), SDS((s, d), BF16), SDS((s, d), F32), SDS((len(MIX_SAVED), s, lw), F32),
                   SDS((s, lw), BF16)],
        scratch_shapes=[pltpu.VMEM((nb, GATE_BLOCK, 2 * GATE_BLOCK), BF16), pltpu.VMEM((tm, lw), F32),
                        pltpu.VMEM((tm, lw), F32), pltpu.VMEM((SUBLANES, lw), F32)],
        args=(u, u, x, *small, w_out), sem=("arbitrary",))


def _ffn_fwd(hres1, target, g_ffn, g_fin, w1, w3, w2):
    s, d = hres1.shape
    nj, _, fc = w1.shape
    tm = min(TM_FFN, s)

    def body(h_ref, t_ref, gf_ref, gn_ref, w1_ref, w3_ref, w2_ref,
             a1_ref, a3_ref, h2_ref, dh_ref, dhb_ref, loss_ref, dgn_ref, acc_s):
        i, j = pl.program_id(0), pl.program_id(1)

        @pl.when((i == 0) & (j == 0))
        def _():
            loss_ref[...] = jnp.zeros_like(loss_ref)
            dgn_ref[...] = jnp.zeros_like(dgn_ref)

        @pl.when(j == 0)
        def _():
            hv = h_ref[...]
            r = lax.rsqrt(_rowmean(hv * hv) + EPS)
            h2_ref[...] = (hv * r * gf_ref[...]).astype(BF16)

        h2 = h2_ref[...]
        a1 = _dot(h2, w1_ref[0])
        a3 = _dot(h2, w3_ref[0])
        a1_ref[0] = a1.astype(BF16)
        a3_ref[0] = a3.astype(BF16)
        part = _dot(((a1 * _sigmoid(a1)) * a3).astype(BF16), w2_ref[0])

        @pl.when(j == 0)
        def _():
            acc_s[...] = part

        @pl.when(j > 0)
        def _():
            acc_s[...] += part

        @pl.when(j == nj - 1)
        def _():
            hr2 = h_ref[...] + acc_s[...]
            r2 = lax.rsqrt(_rowmean(hr2 * hr2) + EPS)
            xh = hr2 * r2
            gn = gn_ref[...]
            diff = xh * gn - t_ref[...]
            tot = jnp.sum(jnp.sum(diff * diff, axis=1, keepdims=True), axis=0, keepdims=True)
            loss_ref[...] += tot * (0.5 / d)
            dout = diff * (1.0 / d)
            dgn_ref[...] += _colsum8(dout * xh)
            dh = _rms_bwd(dout, xh, r2, gn)
            dh_ref[...] = dh
            dhb_ref[...] = dh.astype(BF16)

    return pl.pallas_call(
        body, grid=(s // tm, nj), name="ffn_fwd",
        in_specs=[pl.BlockSpec((tm, d), lambda i, j: (i, 0)), pl.BlockSpec((tm, d), lambda i, j: (i, 0)),
                  _const_spec((1, d)), _const_spec((1, d)),
                  pl.BlockSpec((1, d, fc), lambda i, j: (j, 0, 0)), pl.BlockSpec((1, d, fc), lambda i, j: (j, 0, 0)),
                  pl.BlockSpec((1, fc, d), lambda i, j: (j, 0, 0))],
        out_specs=[pl.BlockSpec((1, tm, fc), lambda i, j: (j, i, 0)), pl.BlockSpec((1, tm, fc), lambda i, j: (j, i, 0)),
                   pl.BlockSpec((tm, d), lambda i, j: (i, 0)), pl.BlockSpec((tm, d), lambda i, j: (i, 0)),
                   pl.BlockSpec((tm, d), lambda i, j: (i, 0)),
                   pl.BlockSpec((SUBLANES, LANES), lambda i, j: (0, 0)),
                   pl.BlockSpec((SUBLANES, d), lambda i, j: (0, 0))],
        out_shape=[SDS((nj, s, fc), BF16), SDS((nj, s, fc), BF16), SDS((s, d), BF16), SDS((s, d), F32),
                   SDS((s, d), BF16), SDS((SUBLANES, LANES), F32), SDS((SUBLANES, d), F32)],
        scratch_shapes=[pltpu.VMEM((tm, d), F32)],
        compiler_params=_cp(("arbitrary", "arbitrary")))(hres1, target, g_ffn, g_fin, w1, w3, w2)


def _ffn_bwd_act(dh, dhb, a1, a3, hres1, g_ffn, w1, w3, w2):
    s, d = hres1.shape
    nj, _, fc = a1.shape
    tm = min(TM_FFN, s)

    def body(dh_ref, dhb_ref, a1_ref, a3_ref, h_ref, gf_ref, w1_ref, w3_ref, w2_ref,
             da1_ref, da3_ref, dhr_ref, dgf_ref, acc_s):
        i, j = pl.program_id(0), pl.program_id(1)

        @pl.when((i == 0) & (j == 0))
        def _():
            dgf_ref[...] = jnp.zeros_like(dgf_ref)

        @pl.when(j == 0)
        def _():
            acc_s[...] = jnp.zeros_like(acc_s)

        rc = tm // FFN_ROW_CHUNKS
        for q in range(FFN_ROW_CHUNKS):
            rows = slice(q * rc, (q + 1) * rc)
            dff = _dot_nt(dhb_ref[rows, :], w2_ref[0])
            a1v = a1_ref[0, rows, :].astype(F32)
            a3v = a3_ref[0, rows, :].astype(F32)
            sg = _sigmoid(a1v)
            silu = a1v * sg
            da1 = (dff * a3v * (sg * (1.0 + a1v * (1.0 - sg)))).astype(BF16)
            da3 = (dff * silu).astype(BF16)
            da1_ref[0, rows, :] = da1
            da3_ref[0, rows, :] = da3
            acc_s[rows, :] += _dot_nt(da1, w1_ref[0]) + _dot_nt(da3, w3_ref[0])

        @pl.when(j == nj - 1)
        def _():
            hv = h_ref[...]
            r = lax.rsqrt(_rowmean(hv * hv) + EPS)
            xh = hv * r
            dh2 = acc_s[...]
            dgf_ref[...] += _colsum8(dh2 * xh)
            dhr_ref[...] = dh_ref[...] + _rms_bwd(dh2, xh, r, gf_ref[...])

    return pl.pallas_call(
        body, grid=(s // tm, nj), name="ffn_bwd_act",
        in_specs=[pl.BlockSpec((tm, d), lambda i, j: (i, 0)), pl.BlockSpec((tm, d), lambda i, j: (i, 0)),
                  pl.BlockSpec((1, tm, fc), lambda i, j: (j, i, 0)), pl.BlockSpec((1, tm, fc), lambda i, j: (j, i, 0)),
                  pl.BlockSpec((tm, d), lambda i, j: (i, 0)), _const_spec((1, d)),
                  pl.BlockSpec((1, d, fc), lambda i, j: (j, 0, 0)), pl.BlockSpec((1, d, fc), lambda i, j: (j, 0, 0)),
                  pl.BlockSpec((1, fc, d), lambda i, j: (j, 0, 0))],
        out_specs=[pl.BlockSpec((1, tm, fc), lambda i, j: (j, i, 0)), pl.BlockSpec((1, tm, fc), lambda i, j: (j, i, 0)),
                   pl.BlockSpec((tm, d), lambda i, j: (i, 0)), pl.BlockSpec((SUBLANES, d), lambda i, j: (0, 0))],
        out_shape=[SDS((nj, s, fc), BF16), SDS((nj, s, fc), BF16), SDS((s, d), F32), SDS((SUBLANES, d), F32)],
        scratch_shapes=[pltpu.VMEM((tm, d), F32)],
        compiler_params=_cp(("arbitrary", "arbitrary")))(dh, dhb, a1, a3, hres1, g_ffn, w1, w3, w2)


def _ffn_wgrad(h2, dhb, a1, a3, da1, da3):
    s, d = h2.shape
    _, _, fc = a1.shape
    tm = min(TM_WGRAD, s)

    def body(h2_ref, dhb_ref, a1_ref, a3_ref, da1_ref, da3_ref, dw1_ref, dw3_ref, dw2_ref):
        i = pl.program_id(1)

        @pl.when(i == 0)
        def _():
            dw1_ref[...] = jnp.zeros_like(dw1_ref)
            dw3_ref[...] = jnp.zeros_like(dw3_ref)
            dw2_ref[...] = jnp.zeros_like(dw2_ref)

        h2v = h2_ref[...]
        a1v = a1_ref[0].astype(F32)
        ff = ((a1v * _sigmoid(a1v)) * a3_ref[0].astype(F32)).astype(BF16)
        dw1_ref[0] += _dot_tn(h2v, da1_ref[0])
        dw3_ref[0] += _dot_tn(h2v, da3_ref[0])
        dw2_ref[0] += _dot_tn(ff, dhb_ref[...])

    return pl.pallas_call(
        body, grid=(N_CHIPS, s // tm), name="ffn_wgrad",
        in_specs=[pl.BlockSpec((tm, d), lambda j, i: (i, 0)), pl.BlockSpec((tm, d), lambda j, i: (i, 0))]
        + [pl.BlockSpec((1, tm, fc), lambda j, i: (j, i, 0))] * 4,
        out_specs=[pl.BlockSpec((1, d, fc), lambda j, i: (j, 0, 0)), pl.BlockSpec((1, d, fc), lambda j, i: (j, 0, 0)),
                   pl.BlockSpec((1, fc, d), lambda j, i: (j, 0, 0))],
        out_shape=[SDS((N_CHIPS, d, fc), F32), SDS((N_CHIPS, d, fc), F32), SDS((N_CHIPS, fc, d), F32)],
        compiler_params=_cp(("parallel", "arbitrary")))(h2, dhb, a1, a3, da1, da3)


def _row_chunks(tm):
    rc = tm // FFN_ROW_CHUNKS
    return [slice(q * rc, (q + 1) * rc) for q in range(FFN_ROW_CHUNKS)]


def _ffn_up(hres1, g_ffn, w1, w3):
    s, d = hres1.shape
    nj, fc, _ = w1.shape
    tm = min(TM_FFN_DOWN, s)

    def body(h_ref, gf_ref, w1_ref, w3_ref, h2_ref, a1_ref, a3_ref, ff_ref):
        hv = h_ref[...]
        r = lax.rsqrt(_rowmean(hv * hv) + EPS)
        h2_ref[...] = (hv * r * gf_ref[...]).astype(BF16)
        h2 = h2_ref[...]
        for j in range(nj):
            a1 = _dot_nt(h2, w1_ref[j])
            a3 = _dot_nt(h2, w3_ref[j])
            a1_ref[j] = a1.astype(BF16)
            a3_ref[j] = a3.astype(BF16)
            ff_ref[j] = ((a1 * _sigmoid(a1)) * a3).astype(BF16)

    wspec = _const_spec(w1.shape)
    aspec = pl.BlockSpec((nj, tm, fc), lambda i: (0, i, 0))
    return pl.pallas_call(
        body, grid=(s // tm,), name="ffn_up",
        in_specs=[pl.BlockSpec((tm, d), lambda i: (i, 0)), _const_spec((1, d)), wspec, wspec],
        out_specs=[pl.BlockSpec((tm, d), lambda i: (i, 0)), aspec, aspec, aspec],
        out_shape=[SDS((s, d), BF16)] + [SDS((nj, s, fc), BF16)] * 3,
        compiler_params=_cp(("parallel",)))(hres1, g_ffn, w1, w3)


def _ffn_down(ff, hres1, target, g_fin, w2):
    s, d = hres1.shape
    nj, _, fc = ff.shape
    tm = min(TM_FFN_DOWN, s)

    def body(ff_ref, h_ref, t_ref, gn_ref, w2_ref, dh_ref, dhb_ref, loss_ref, dgn_ref):
        @pl.when(pl.program_id(0) == 0)
        def _():
            loss_ref[...] = jnp.zeros_like(loss_ref)
            dgn_ref[...] = jnp.zeros_like(dgn_ref)

        gn = gn_ref[...]
        for rows in _row_chunks(tm):
            acc = _dot(ff_ref[0, rows, :], w2_ref[0])
            for j in range(1, nj):
                acc = acc + _dot(ff_ref[j, rows, :], w2_ref[j])
            hr2 = h_ref[rows, :] + acc
            r2 = lax.rsqrt(_rowmean(hr2 * hr2) + EPS)
            xh = hr2 * r2
            diff = xh * gn - t_ref[rows, :]
            tot = jnp.sum(jnp.sum(diff * diff, axis=1, keepdims=True), axis=0, keepdims=True)
            loss_ref[...] += tot * (0.5 / d)
            dout = diff * (1.0 / d)
            dgn_ref[...] += _colsum8(dout * xh)
            dh = _rms_bwd(dout, xh, r2, gn)
            dh_ref[rows, :] = dh
            dhb_ref[rows, :] = dh.astype(BF16)

    tile = pl.BlockSpec((tm, d), lambda i: (i, 0))
    return pl.pallas_call(
        body, grid=(s // tm,), name="ffn_down",
        in_specs=[pl.BlockSpec((nj, tm, fc), lambda i: (0, i, 0)), tile, tile, _const_spec((1, d)),
                  _const_spec(w2.shape)],
        out_specs=[tile, tile, pl.BlockSpec((SUBLANES, LANES), lambda i: (0, 0)),
                   pl.BlockSpec((SUBLANES, d), lambda i: (0, 0))],
        out_shape=[SDS((s, d), F32), SDS((s, d), BF16), SDS((SUBLANES, LANES), F32), SDS((SUBLANES, d), F32)],
        compiler_params=_cp(("arbitrary",)))(ff, hres1, target, g_fin, w2)


def _ffn_bwd_gate(dhb, a1, a3, w2):
    s, d = dhb.shape
    nj, _, fc = a1.shape
    tm = min(TM_FFN_DOWN, s)

    def body(dhb_ref, a1_ref, a3_ref, w2_ref, da1_ref, da3_ref):
        for j in range(nj):
            for rows in _row_chunks(tm):
                dff = _dot_nt(dhb_ref[rows, :], w2_ref[j])
                a1v = a1_ref[j, rows, :].astype(F32)
                sg = _sigmoid(a1v)
                silu = a1v * sg
                da1_ref[j, rows, :] = (dff * a3_ref[j, rows, :].astype(F32)
                                       * (sg * (1.0 + (a1v - silu)))).astype(BF16)
                da3_ref[j, rows, :] = (dff * silu).astype(BF16)

    aspec = pl.BlockSpec((nj, tm, fc), lambda i: (0, i, 0))
    return pl.pallas_call(
        body, grid=(s // tm,), name="ffn_bwd_gate",
        in_specs=[pl.BlockSpec((tm, d), lambda i: (i, 0)), aspec, aspec, _const_spec(w2.shape)],
        out_specs=[aspec, aspec], out_shape=[SDS((nj, s, fc), BF16)] * 2,
        compiler_params=_cp(("parallel",)))(dhb, a1, a3, w2)


def _ffn_bwd_down(da1, da3, dh, hres1, g_ffn, w1, w3, hosted=None):
    s, d = hres1.shape
    nj, _, fc = da1.shape
    tm = min(TM_FFN_DOWN, s)
    nt = s // tm

    def body(da1_ref, da3_ref, dh_ref, h_ref, gf_ref, w1_ref, w3_ref, dhr_ref, dgf_ref):
        @pl.when(pl.program_id(0) == 0)
        def _():
            dgf_ref[...] = jnp.zeros_like(dgf_ref)

        gf = gf_ref[...]
        for rows in _row_chunks(tm):
            dh2 = None
            for j in range(nj):
                part = _dot(da1_ref[j, rows, :], w1_ref[j]) + _dot(da3_ref[j, rows, :], w3_ref[j])
                dh2 = part if dh2 is None else dh2 + part
            hv = h_ref[rows, :]
            r = lax.rsqrt(_rowmean(hv * hv) + EPS)
            xh = hv * r
            dgf_ref[...] += _colsum8(dh2 * xh)
            dhr_ref[rows, :] = dh_ref[rows, :] + _rms_bwd(dh2, xh, r, gf)

    tile = pl.BlockSpec((tm, d), lambda i: (i, 0))
    aspec = pl.BlockSpec((nj, tm, fc), lambda i: (0, i, 0))
    wspec = _const_spec(w1.shape)

    def stages():
        i = pl.program_id(0)
        return i == 0, i == max(nt - 2, 0), i == nt - 1

    return _call(
        body, hosted, stages, grid=(nt,), name="ffn_bwd_down",
        in_specs=[aspec, aspec, tile, tile, _const_spec((1, d)), wspec, wspec],
        out_specs=[tile, pl.BlockSpec((SUBLANES, d), lambda i: (0, 0))],
        out_shape=[SDS((s, d), F32), SDS((SUBLANES, d), F32)],
        scratch_shapes=[], args=(da1, da3, dh, hres1, g_ffn, w1, w3), sem=("arbitrary",))


def _ffn_wgrad2(h2, dhb, ff, da1, da3):
    s, d = h2.shape
    _, _, fc = ff.shape
    tm = min(TM_WGRAD, s)

    def body(h2_ref, dhb_ref, ff_ref, da1_ref, da3_ref, dw1_ref, dw3_ref, dw2_ref):
        @pl.when(pl.program_id(1) == 0)
        def _():
            dw1_ref[...] = jnp.zeros_like(dw1_ref)
            dw3_ref[...] = jnp.zeros_like(dw3_ref)
            dw2_ref[...] = jnp.zeros_like(dw2_ref)

        h2v = h2_ref[...]
        dw1_ref[0] += _dot_tn(da1_ref[0], h2v)
        dw3_ref[0] += _dot_tn(da3_ref[0], h2v)
        dw2_ref[0] += _dot_tn(ff_ref[0], dhb_ref[...])

    wspec = pl.BlockSpec((1, fc, d), lambda j, i: (j, 0, 0))
    return pl.pallas_call(
        body, grid=(N_CHIPS, s // tm), name="ffn_wgrad",
        in_specs=[pl.BlockSpec((tm, d), lambda j, i: (i, 0)), pl.BlockSpec((tm, d), lambda j, i: (i, 0))]
        + [pl.BlockSpec((1, tm, fc), lambda j, i: (j, i, 0))] * 3,
        out_specs=[wspec] * 3, out_shape=[SDS((N_CHIPS, fc, d), F32)] * 3,
        compiler_params=_cp(("parallel", "arbitrary")))(h2, dhb, ff, da1, da3)


def _mixer_bwd(u, saved, pooled, h, dhres1, sp_, w_out, hosted=None):
    s, din = u.shape
    d = dhres1.shape[1]
    lw = din // 3
    tm = min(TM_MIX, s)
    nt = s // tm
    nb = lw // GATE_BLOCK
    hd = sp_["gate_a_w"].shape[1]

    def body(ul_ref, saved_ref, pooled_ref, h_ref, hhalo_ref, dhr_ref, cw_ref, cb_ref, ga_ref, gx_ref, ba_ref,
             bx_ref, lam_ref, pw_ref, pb_ref, ps_ref, gl_ref, gp_ref, wout_ref, du_ref, slab_ref,
             gw_s, a_s, b_s, e_s, ecarry_s, dxc_s, q_s, vec_s, cwacc_s, dgw_s, dpw_s):
        i = pl.program_id(0)
        tile = nt - 1 - i

        @pl.when(i == 0)
        def _():
            _build_gate_blocks(ga_ref, gx_ref, gw_s)
            for ref in (ecarry_s, dxc_s, q_s, vec_s, cwacc_s, dgw_s, dpw_s):
                ref[...] = jnp.zeros_like(ref)

        cw = cw_ref[...]
        lam = lam_ref[...]
        ps = ps_ref[...]
        f = {name: saved_ref[k] for k, name in enumerate(MIX_SAVED)}
        f["mult"] = jnp.sqrt(jnp.maximum(f["m2raw"], 1e-12))
        f["sp"] = _softplus_neg(lam)
        f["xcb"] = f["xc"].astype(BF16)
        pooled = pooled_ref[...]
        row = lax.broadcasted_iota(jnp.int32, (tm, LANES), 0) + tile * tm
        f["invs"] = [1.0 / jnp.minimum(row + 1, w).astype(F32) for w in POOL_WINDOWS]
        f["z"] = jnp.concatenate(
            [_dot(pooled[:, g * LANES:(g + 1) * LANES], pw_ref[g].astype(BF16))
             for g in range(len(POOL_WINDOWS))], axis=1) + pb_ref[...]
        f["y_pool"] = f["z"] * ps
        u_l = ul_ref[...]
        hv = h_ref[...]
        h_prev = _shift_down(jnp.where(tile > 0, hhalo_ref[...], 0.0), hv, 1)
        y_lru = hv * f["ge"]
        rl = lax.rsqrt(_rowmean(y_lru * y_lru) + EPS)
        yp = f["y_pool"]
        rp = lax.rsqrt(_rowmean(yp * yp) + EPS)
        xh_l = y_lru * rl
        xh_p = yp * rp

        dyn = _dot_nt(dhr_ref[...].astype(BF16), wout_ref[...])
        d_nl, d_np = dyn[:, :lw], dyn[:, lw:]
        vec = {}
        vec[ROW_GL] = _colsum8(d_nl * xh_l)
        vec[ROW_GP] = _colsum8(d_np * xh_p)
        d_ylru = _rms_bwd(d_nl, xh_l, rl, gl_ref[...])
        d_ypool = _rms_bwd(d_np, xh_p, rp, gp_ref[...])

        vec[ROW_PS] = _colsum8(d_ypool * f["z"])
        dz = d_ypool * ps
        vec[ROW_PB] = _colsum8(dz)
        dzb = dz.astype(BF16)
        dup = []
        for gi, w in enumerate(POOL_WINDOWS):
            sl = slice(gi * LANES, (gi + 1) * LANES)
            dpw_s[:, sl] += _dot_tn(pooled[:, sl], dzb[:, sl])
            dpool = _dot_nt(dzb[:, sl], pw_ref[gi].astype(BF16))
            q = dpool * f["invs"][gi]
            e = jnp.concatenate([q, q_s[:, sl]], axis=0)
            k = 1
            while k < w:
                e = e + pltpu.roll(e, tm + HALO - k, 0)
                k *= 2
            dup.append(e[:tm] - dpool)
            q_s[:, sl] = q[:HALO]

        d_hout = d_ylru * f["ge"]
        d_ug = d_ylru * hv * f["dge"]
        a = f["a"]
        a1, b1 = _scan_level1(a, a * d_hout, reverse=True)
        a_s[...] = a1
        b_s[...] = b1
        e_next = ecarry_s[...]
        ecarry_s[...] = _scan_level2(a_s, b_s, e_s, e_next, reverse=True)
        sv = d_hout + _shift_up(e_s[...], e_next, 1)
        d_a = sv * h_prev
        mult, ig, xc, r = f["mult"], f["ig"], f["xc"], f["r"]
        d_mult = sv * (ig * xc)
        d_ig = sv * mult * xc
        d_xc = sv * mult * ig
        d_la = d_a * a + jnp.where(f["m2raw"] > 1e-12, d_mult * (-(a * a) / mult), 0.0)
        d_r = d_la * (-LRU_C * f["sp"])
        vec[ROW_LAM] = _colsum8(d_la * (-LRU_C * r))
        d_pr = d_r * r * (1.0 - r)
        d_pi = d_ig * ig * (1.0 - ig)
        vec[ROW_BA] = _colsum8(d_pr)
        vec[ROW_BX] = _colsum8(d_pi)
        dxc_parts = []
        for b in range(nb):
            sl = slice(b * GATE_BLOCK, (b + 1) * GATE_BLOCK)
            rhs = jnp.concatenate([d_pr[:, sl], d_pi[:, sl]], axis=1).astype(BF16)
            dgw_s[b] += _dot_tn(f["xcb"][:, sl], rhs)
            dxc_parts.append(_dot_nt(rhs, gw_s[b]))
        d_xc = d_xc + jnp.concatenate(dxc_parts, axis=1)
        vec[ROW_CONV_B] = _colsum8(d_xc)
        dxc_next = dxc_s[...]
        d_ul = None
        for k in range(CONV_WIDTH):
            ahead = _shift_up(d_xc, dxc_next, CONV_WIDTH - 1 - k)
            cwacc_s[k * SUBLANES:(k + 1) * SUBLANES, :] += _colsum8(ahead * u_l)
            term = ahead * cw[k:k + 1, :]
            d_ul = term if d_ul is None else d_ul + term
        dxc_s[...] = d_xc[:SUBLANES]
        for row, val in vec.items():
            vec_s[row * SUBLANES:(row + 1) * SUBLANES, :] += val
        du_ref[...] = jnp.concatenate([d_ul, d_ug] + dup, axis=1).astype(BF16)

        @pl.when(i == nt - 1)
        def _():
            rows = []
            for row in range(ROW_GA):
                if row in (ROW_CONV_W, ROW_CONV_W + 1, ROW_CONV_W + 2, ROW_CONV_W + 3):
                    k = row - ROW_CONV_W
                    v = jnp.sum(cwacc_s[k * SUBLANES:(k + 1) * SUBLANES, :], axis=0, keepdims=True)
                elif row <= ROW_GP:
                    v = jnp.sum(vec_s[row * SUBLANES:(row + 1) * SUBLANES, :], axis=0, keepdims=True)
                    if row == ROW_LAM:
                        v = v * (-1.0 / (1.0 + jnp.exp(lam)))
                else:
                    v = jnp.zeros((1, lw), F32)
                rows.append(v)
            slab_ref[0:ROW_GA, :] = jnp.concatenate(rows, axis=0)
            lane = lax.broadcasted_iota(jnp.int32, (hd, GATE_BLOCK), 1)
            for b in range(nb):
                for off, row0 in ((0, ROW_GA), (GATE_BLOCK, ROW_GX)):
                    acc = jnp.zeros((hd, GATE_BLOCK), F32)
                    for hh in range(GATE_BLOCK // hd):
                        m = (lane >= hh * hd) & (lane < (hh + 1) * hd)
                        acc = acc + jnp.where(m, dgw_s[b, hh * hd:(hh + 1) * hd, off:off + GATE_BLOCK], 0.0)
                    slab_ref[row0:row0 + hd, b * GATE_BLOCK:(b + 1) * GATE_BLOCK] = acc
            slab_ref[ROW_PW:ROW_PW + LANES, :] = dpw_s[...]

    small = [sp_[k] for k in ("conv_w", "conv_b", "gate_a_w", "gate_x_w", "gate_a_b", "gate_x_b", "lru_lambda",
                              "pool_w", "pool_b", "pool_scale", "norm_lru_g", "norm_pool_g")]
    rev = lambda i: nt - 1 - i

    def stages():
        i = pl.program_id(0)
        return i == 0, i == max(nt - 3, 0), i == nt - 1

    return _call(
        body, hosted, stages, grid=(nt,), name="mixer_bwd",
        in_specs=[pl.BlockSpec((tm, lw), lambda i: (rev(i), 0)),
                  pl.BlockSpec((len(MIX_SAVED), tm, lw), lambda i: (0, rev(i), 0)),
                  pl.BlockSpec((tm, lw), lambda i: (rev(i), 0)),
                  pl.BlockSpec((tm, lw), lambda i: (rev(i), 0)),
                  pl.BlockSpec((SUBLANES, lw), lambda i: (jnp.maximum(rev(i) * (tm // SUBLANES) - 1, 0), 0)),
                  pl.BlockSpec((tm, d), lambda i: (rev(i), 0))]
        + [_const_spec(a.shape) for a in small] + [_const_spec(w_out.shape)],
        out_specs=[pl.BlockSpec((tm, din), lambda i: (rev(i), 0)),
                   pl.BlockSpec((MIX_SLAB_ROWS, SLAB_W), lambda i: (0, 0))],
        out_shape=[SDS((s, din), BF16), SDS((MIX_SLAB_ROWS, SLAB_W), F32)],
        scratch_shapes=[pltpu.VMEM((nb, GATE_BLOCK, 2 * GATE_BLOCK), BF16),
                        pltpu.VMEM((tm, lw), F32), pltpu.VMEM((tm, lw), F32), pltpu.VMEM((tm, lw), F32),
                        pltpu.VMEM((SUBLANES, lw), F32), pltpu.VMEM((SUBLANES, lw), F32),
                        pltpu.VMEM((HALO, lw), F32), pltpu.VMEM((ROW_GA * SUBLANES, lw), F32),
                        pltpu.VMEM((CONV_WIDTH * SUBLANES, lw), F32),
                        pltpu.VMEM((nb, GATE_BLOCK, 2 * GATE_BLOCK), F32), pltpu.VMEM((LANES, lw), F32)],
        args=(u, saved, pooled, h, h, dhres1, *small, w_out), sem=("arbitrary",))


def _inproj_bwd(x, du, dhres1, yn, g_mix, w_in, hosted=None):
    s, d = x.shape
    n = w_in.shape[1]
    nc = n // N_CHIPS
    tm = min(TM_PROJ, s)
    nt = s // tm

    def body(x_ref, du_ref, dhr_ref, yn_ref, g_ref, w_ref, gx_ref, dwin_ref, dwout_ref, dg_ref):
        i = pl.program_id(0)

        @pl.when(i == 0)
        def _():
            dwin_ref[...] = jnp.zeros_like(dwin_ref)
            dwout_ref[...] = jnp.zeros_like(dwout_ref)
            dg_ref[...] = jnp.zeros_like(dg_ref)

        xv = x_ref[...]
        g = g_ref[...]
        r = lax.rsqrt(_rowmean(xv * xv) + EPS)
        xh = xv * r
        h1 = (xh * g).astype(BF16)
        duv = du_ref[...]
        dh1 = _dot_nt(duv, w_ref[...])
        dg_ref[...] += _colsum8(dh1 * xh)
        dhr = dhr_ref[...]
        gx_ref[...] = dhr + _rms_bwd(dh1, xh, r, g)
        for jj in range(N_CHIPS):
            dwin_ref[jj] += _dot_tn(h1, duv[:, jj * nc:(jj + 1) * nc])
        dwout_ref[...] += _dot_tn(yn_ref[...], dhr.astype(BF16))

    def stages():
        i = pl.program_id(0)
        return i == 0, i == max(nt - 3, 0), i == nt - 1

    return _call(
        body, hosted, stages, grid=(nt,), name="inproj_bwd",
        in_specs=[pl.BlockSpec((tm, d), lambda i: (i, 0)), pl.BlockSpec((tm, n), lambda i: (i, 0)),
                  pl.BlockSpec((tm, d), lambda i: (i, 0)), pl.BlockSpec((tm, d), lambda i: (i, 0)),
                  _const_spec((1, d)), _const_spec((d, n))],
        out_specs=[pl.BlockSpec((tm, d), lambda i: (i, 0)), pl.BlockSpec((N_CHIPS, d, nc), lambda i: (0, 0, 0)),
                   pl.BlockSpec((d, d), lambda i: (0, 0)), pl.BlockSpec((SUBLANES, d), lambda i: (0, 0))],
        out_shape=[SDS((s, d), F32), SDS((N_CHIPS, d, nc), F32), SDS((d, d), F32), SDS((SUBLANES, d), F32)],
        scratch_shapes=[], args=(x, du, dhres1, yn, g_mix, w_in), sem=("arbitrary",))


def _place():
    x, y, c = lax.axis_index("x"), lax.axis_index("y"), lax.axis_index("c")
    return x, y, c


def _other_chips(x, y):
    return [(1 - x, y), (x, 1 - y), (1 - x, 1 - y)]


ANY = pl.BlockSpec(memory_space=pl.ANY)
VMEM_SPEC = pl.BlockSpec(memory_space=pltpu.VMEM)

_GATHERED = {"w_in": "cols", "w_out": "major", "ffn_w1": "major", "ffn_w3": "major", "ffn_w2": "major"}
_BIG = ("w_in", "w_out", "ffn_w1", "ffn_w3", "ffn_w2")


def _gather_weights(shards, conv_w, n_remote):
    n = len(shards)
    full_shapes = []
    for name, sh in zip(_BIG, shards):
        r, cdim = sh.shape
        if _GATHERED[name] == "cols":
            assert cdim % LANES == 0
            full_shapes.append((r, cdim * N_CHIPS))
        else:
            full_shapes.append((N_CHIPS, r, cdim))

    def region(ref, name, sh, jj, cc):
        r, cdim = sh
        rows = pl.ds(0, r) if cc is None else pl.ds(pl.multiple_of(cc * (r // 2), 16), r // 2)
        if _GATHERED[name] == "cols":
            return ref.at[rows, pl.ds(pl.multiple_of(jj * cdim, LANES), cdim)]
        return ref.at[jj, rows, :]

    def staged(ref, sh, cc):
        r = sh[0]
        return ref.at[pl.ds(pl.multiple_of(cc * (r // 2), 16), r // 2), :]

    def body(*refs):
        ins, cw_in = refs[:n], refs[n]
        outs, cw_out = refs[n + 1:2 * n + 1], refs[2 * n + 1]
        stage = refs[2 * n + 2:3 * n + 2]
        cw_stage, lsem, ssem, rsem, fssem, frsem, cssem, crsem = refs[3 * n + 2:]
        x, y, c = _place()
        j = 2 * x + y
        chips = _other_chips(x, y)
        for w in range(n):
            stage[w][...] = ins[w][...].astype(BF16)
        cw_stage[...] = jnp.zeros_like(cw_stage)
        cw_stage[0:CONV_WIDTH, :] = cw_in[...]
        shs = [s_.shape for s_ in shards]
        local = [pltpu.make_async_copy(stage[w], region(outs[w], _BIG[w], shs[w], j, None), lsem.at[w])
                 for w in range(n)]
        local.append(pltpu.make_async_copy(cw_stage, cw_out.at[j], lsem.at[n]))
        for cp in local:
            cp.start()
        sends = []
        for k, (px, py) in enumerate(chips):
            for w in range(n_remote):
                sends.append(pltpu.make_async_remote_copy(
                    src_ref=staged(stage[w], shs[w], c), dst_ref=region(outs[w], _BIG[w], shs[w], j, c),
                    send_sem=ssem.at[k * n + w], recv_sem=rsem.at[k * n + w], device_id=(px, py, c),
                    device_id_type=MESH))
            sends.append(pltpu.make_async_remote_copy(
                src_ref=cw_stage, dst_ref=cw_out.at[j], send_sem=cssem.at[k], recv_sem=crsem.at[k],
                device_id=(px, py, c), device_id_type=MESH))
        for cp in sends:
            cp.start()
        fwd = []
        for k, (px, py) in enumerate(chips):
            jk = 2 * px + py
            for w in range(n_remote):
                reg = region(outs[w], _BIG[w], shs[w], jk, c)
                pltpu.make_async_remote_copy(src_ref=reg, dst_ref=reg, send_sem=ssem.at[k * n + w],
                                             recv_sem=rsem.at[k * n + w], device_id=(px, py, c),
                                             device_id_type=MESH).wait_recv()
                cp = pltpu.make_async_remote_copy(src_ref=reg, dst_ref=reg, send_sem=fssem.at[k * n + w],
                                                  recv_sem=frsem.at[k * n + w], device_id=(x, y, 1 - c),
                                                  device_id_type=MESH)
                cp.start()
                fwd.append(cp)
            pltpu.make_async_remote_copy(src_ref=cw_stage, dst_ref=cw_out.at[jk], send_sem=cssem.at[k],
                                         recv_sem=crsem.at[k], device_id=(px, py, c),
                                         device_id_type=MESH).wait_recv()
        for k, (px, py) in enumerate(chips):
            jk = 2 * px + py
            for w in range(n_remote):
                reg = region(outs[w], _BIG[w], shs[w], jk, 1 - c)
                pltpu.make_async_remote_copy(src_ref=reg, dst_ref=reg, send_sem=fssem.at[k * n + w],
                                             recv_sem=frsem.at[k * n + w], device_id=(x, y, 1 - c),
                                             device_id_type=MESH).wait_recv()
        for cp in sends + fwd:
            cp.wait_send()
        for cp in local:
            cp.wait()

    nsem = 3 * n
    return pl.pallas_call(
        body, name="gather_first",
        in_specs=[VMEM_SPEC] * (n + 1), out_specs=[ANY] * (n + 1),
        out_shape=[SDS(fs, BF16) for fs in full_shapes] + [SDS((N_CHIPS, SUBLANES, LANES), F32)],
        scratch_shapes=[pltpu.VMEM(s_.shape, BF16) for s_ in shards] + [pltpu.VMEM((SUBLANES, LANES), F32)]
        + [pltpu.SemaphoreType.DMA((n + 1,))] + [pltpu.SemaphoreType.DMA((nsem,))] * 4
        + [pltpu.SemaphoreType.DMA((3,))] * 2,
        compiler_params=_cp())(*shards, conv_w)


def _start_all(make):
    def f(ins, outs, sems):
        for cp in make(ins, outs, sems):
            cp.start()
    return f


def _wait_all(make):
    def f(ins, outs, sems):
        for cp in make(ins, outs, sems):
            cp.wait()
    return f


def _ffn_gather_hosted(arrs):
    n = len(arrs)

    def make(outs, sems):
        ssem, rsem, fs, fr = sems
        x, y, c = _place()
        j = 2 * x + y

        def reg(w, jj, cc):
            hr = arrs[w].shape[1] // 2
            return outs[w].at[jj, pl.ds(pl.multiple_of(cc * hr, 16), hr), :]

        def rc(w, jj, cc, s_sem, r_sem, dev):
            return pltpu.make_async_remote_copy(src_ref=reg(w, jj, cc), dst_ref=reg(w, jj, cc), send_sem=s_sem,
                                                recv_sem=r_sem, device_id=dev, device_id_type=MESH)

        sends, recvs, fwds, frecvs = [], [], [], []
        for k, (px, py) in enumerate(_other_chips(x, y)):
            jk = 2 * px + py
            for w in range(n):
                q = k * n + w
                sends.append(rc(w, j, c, ssem.at[q], rsem.at[q], (px, py, c)))
                recvs.append(rc(w, jk, c, ssem.at[q], rsem.at[q], (px, py, c)))
                fwds.append(rc(w, jk, c, fs.at[q], fr.at[q], (x, y, 1 - c)))
                frecvs.append(rc(w, jk, 1 - c, fs.at[q], fr.at[q], (x, y, 1 - c)))
        return sends, recvs, fwds, frecvs

    def start(ins, outs, sems):
        for cp in make(outs, sems)[0]:
            cp.start()

    def mid(ins, outs, sems):
        _, recvs, fwds, _ = make(outs, sems)
        for r, f in zip(recvs, fwds):
            r.wait_recv()
            f.start()

    def finish(ins, outs, sems):
        sends, _, fwds, frecvs = make(outs, sems)
        for r in frecvs:
            r.wait_recv()
        for cp in sends + fwds:
            cp.wait_send()

    return _Hosted(arrs, [SDS(a.shape, a.dtype) for a in arrs], [3 * n] * 4, start, finish, mid=mid,
                   aliases={w: w for w in range(n)})


def _rs_sibling_hosted(arrs):
    n = len(arrs)

    def make(ins, outs, sems):
        x, y, c = _place()
        cps = []
        for w in range(n):
            hr = arrs[w].shape[1] // 2
            src = ins[w].at[:, pl.ds(pl.multiple_of((1 - c) * hr, SUBLANES), hr), :]
            cps.append(pltpu.make_async_remote_copy(src_ref=src, dst_ref=outs[w], send_sem=sems[0].at[w],
                                                    recv_sem=sems[1].at[w], device_id=(x, y, 1 - c),
                                                    device_id_type=MESH))
        return cps

    return _Hosted(arrs, [SDS((a.shape[0], a.shape[1] // 2, a.shape[2]), F32) for a in arrs], [n, n],
                   _start_all(make), _wait_all(make))


def _rs_chips_hosted(parts):
    n = len(parts)

    def make(ins, outs, sems):
        x, y, c = _place()
        j = 2 * x + y
        cps = []
        for k, (px, py) in enumerate(_other_chips(x, y)):
            jk = 2 * px + py
            for w in range(n):
                cps.append(pltpu.make_async_remote_copy(
                    src_ref=ins[w].at[jk], dst_ref=outs[w].at[j], send_sem=sems[0].at[k * n + w],
                    recv_sem=sems[1].at[k * n + w], device_id=(px, py, c), device_id_type=MESH))
        return cps

    return _Hosted(parts, [SDS(p.shape, p.dtype) for p in parts], [3 * n, 3 * n], _start_all(make), _wait_all(make))


def _rs_swap_hosted(halves):
    n = len(halves)

    def make(ins, outs, sems):
        x, y, c = _place()
        return [pltpu.make_async_remote_copy(src_ref=ins[w], dst_ref=outs[w], send_sem=sems[0].at[w],
                                             recv_sem=sems[1].at[w], device_id=(x, y, 1 - c), device_id_type=MESH)
                for w in range(n)]

    return _Hosted(halves, [SDS(h.shape, F32) for h in halves], [n, n], _start_all(make), _wait_all(make))


HBM_SPEC = pl.BlockSpec(memory_space=pltpu.HBM)
SEM_SPEC = pl.BlockSpec(memory_space=pltpu.SEMAPHORE)
_EFFECT = pltpu.SideEffectType.DATAFLOW_SIDE_EFFECTING


def _split_start(h, name):
    n_in, n_out, ns = len(h.ins), len(h.out_shapes), len(h.sems)
    ins = [pltpu.with_memory_space_constraint(a, pltpu.HBM) for a in h.ins]
    lands = [pltpu.with_memory_space_constraint(lax.empty(o.shape, o.dtype), pltpu.HBM) for o in h.out_shapes]

    def body(*refs):
        i_refs, l_refs = refs[:n_in], refs[n_in:n_in + n_out]
        s_refs = refs[n_in + n_out:n_in + n_out + ns]
        token = refs[-1]
        h.start(i_refs, l_refs, s_refs)
        token[...] = jnp.zeros_like(token)

    res = pl.pallas_call(
        body, name=name, in_specs=[HBM_SPEC] * (n_in + n_out),
        out_specs=[SEM_SPEC] * ns + [HBM_SPEC] * (n_in + n_out) + [VMEM_SPEC],
        out_shape=[pltpu.SemaphoreType.DMA((k,)) for k in h.sems]
        + [pltpu.HBM(a.shape, a.dtype) for a in h.ins] + [pltpu.HBM(o.shape, o.dtype) for o in h.out_shapes]
        + [SDS((SUBLANES, LANES), F32)],
        input_output_aliases={k: ns + k for k in range(n_in + n_out)},
        compiler_params=pltpu.CompilerParams(has_side_effects=_EFFECT))(*ins, *lands)
    return list(res[:-1]), res[-1]


def _split_wait(h, state, after, name):
    n_in, n_out, ns = len(h.ins), len(h.out_shapes), len(h.sems)
    sems, bufs = state[:ns], state[ns:]

    def body(*refs):
        i_refs, l_refs = refs[:n_in], refs[n_in:n_in + n_out]
        s_refs = refs[n_in + n_out:n_in + n_out + ns]
        h.finish(i_refs, l_refs, s_refs)

    res = pl.pallas_call(
        body, name=name, in_specs=[HBM_SPEC] * (n_in + n_out) + [SEM_SPEC] * ns + [ANY],
        out_specs=[HBM_SPEC] * (n_in + n_out),
        out_shape=[pltpu.HBM(b.shape, b.dtype) for b in bufs],
        input_output_aliases={k: k for k in range(n_in + n_out)},
        compiler_params=pltpu.CompilerParams(has_side_effects=_EFFECT))(*bufs, *sems, after)
    return list(res[n_in:])


def _run_comm(hosted, name):
    return _call(lambda: None, hosted, None, name=name, grid=(), in_specs=[], out_specs=[], out_shape=[],
                 scratch_shapes=[], args=(), sem=None)[1]


def _row_tile(rows, cols, n_arrays):
    budget = 24 * 1024 * 1024 // (2 * 4 * n_arrays * cols)
    best = SUBLANES
    for t in range(SUBLANES, rows + 1, SUBLANES):
        if rows % t == 0 and t <= budget:
            best = t
    return best


def _place_index(which):
    x, y, c = _place()
    v = c if which == "c" else 2 * x + y
    return jnp.reshape(v, (1,)).astype(jnp.int32)


def _add_own_half(full, recv, name):
    nsh, rows, cols = full.shape
    hr = rows // 2
    t = _row_tile(hr, cols, 4)
    nt = hr // t

    def body(c_ref, a_ref, b_ref, o_ref, ob_ref):
        v = a_ref[...] + b_ref[...]
        o_ref[...] = v
        ob_ref[...] = v.astype(BF16)

    half = pl.BlockSpec((1, t, cols), lambda s_, i, c_ref: (s_, i, 0))
    return pl.pallas_call(
        body, name=name,
        grid_spec=pltpu.PrefetchScalarGridSpec(
            num_scalar_prefetch=1, grid=(nsh, nt),
            in_specs=[pl.BlockSpec((1, t, cols), lambda s_, i, c_ref: (s_, c_ref[0] * nt + i, 0)), half],
            out_specs=[half, half]),
        out_shape=[SDS((nsh, hr, cols), F32), SDS((nsh, hr, cols), BF16)],
        compiler_params=_cp(("parallel", "parallel")))(_place_index("c"), full, recv)


def _sum_chips(own, recv, name):
    nsh, hr, cols = own.shape
    t = _row_tile(hr, cols, 6)

    def body(j_ref, own_ref, *rest):
        r_refs, o_ref = rest[:nsh], rest[nsh]
        j = j_ref[0]
        mine = own_ref[0]
        parts = [jnp.where(j == k, mine, r_refs[k][0].astype(F32)) for k in range(nsh)]
        o_ref[...] = ((parts[0] + parts[1]) + parts[2]) + parts[3]

    def other(k):
        return pl.BlockSpec((1, t, cols), lambda i, j_ref: (jnp.where(j_ref[0] == k, (k + 1) % nsh, k), i, 0))

    return pl.pallas_call(
        body, name=name,
        grid_spec=pltpu.PrefetchScalarGridSpec(
            num_scalar_prefetch=1, grid=(hr // t,),
            in_specs=[pl.BlockSpec((1, t, cols), lambda i, j_ref: (j_ref[0], i, 0))]
            + [other(k) for k in range(nsh)],
            out_specs=pl.BlockSpec((t, cols), lambda i, j_ref: (i, 0))),
        out_shape=SDS((hr, cols), F32), compiler_params=_cp(("parallel",)))(_place_index("j"), own, *([recv] * nsh))


def _adamw_math(w, g, m, v):
    m = ADAM_B1 * m + (1.0 - ADAM_B1) * g
    v = ADAM_B2 * v + (1.0 - ADAM_B2) * (g * g)
    m_hat = m / (1.0 - ADAM_B1 ** ADAM_STEP)
    v_hat = v / (1.0 - ADAM_B2 ** ADAM_STEP)
    delta = -ADAM_LR * (m_hat / (jnp.sqrt(v_hat) + ADAM_EPS) + ADAM_WD * w)
    return delta, m, v


def _adamw_big(w, g_own, g_sib, m, v, name, token=None):
    _, rows, cols = w.shape
    hr = rows // 2
    t = _row_tile(hr, cols, 9)
    nth = hr // t
    if token is None:
        token = jnp.zeros((SUBLANES, LANES), F32)

    def body(c_ref, w_ref, go_ref, gs_ref, m_ref, v_ref, tok_ref, g_ref, d_ref, mo_ref, vo_ref):
        own = (pl.program_id(0) // nth) == c_ref[0]
        g = jnp.where(own, go_ref[...], gs_ref[...]) + tok_ref[0:1, 0:1]
        g_ref[0] = g
        d_ref[0], mo_ref[0], vo_ref[0] = _adamw_math(w_ref[0], g, m_ref[0], v_ref[0])

    spec = pl.BlockSpec((1, t, cols), lambda i, c_ref: (0, i, 0))
    hspec = pl.BlockSpec((t, cols), lambda i, c_ref: (i % nth, 0))
    tspec = pl.BlockSpec((SUBLANES, LANES), lambda i, c_ref: (0, 0))
    return pl.pallas_call(
        body, name=name,
        grid_spec=pltpu.PrefetchScalarGridSpec(
            num_scalar_prefetch=1, grid=(2 * nth,), in_specs=[spec, hspec, hspec, spec, spec, tspec],
            out_specs=[spec] * 4),
        out_shape=[SDS((1, rows, cols), F32)] * 4,
        compiler_params=_cp(("parallel",)))(_place_index("c"), w, g_own, g_sib, m, v, token)


def _allreduce_small(mix_slab, dg_mix, dg_ffn, dg_fin, loss8):
    half = SLAB_ROWS // 2

    def body(ms_ref, gm_ref, gf_ref, gn_ref, loss_ref, out_ref, loc_s, sib_s, chip_s, r2_s, fin_s, sems):
        x, y, c = _place()
        j = 2 * x + y
        rows = []
        for ref in (gm_ref, gf_ref, gn_ref):
            v = jnp.sum(ref[...], axis=0, keepdims=True)
            rows += [v[:, :SLAB_W], v[:, SLAB_W:]]
        rows.append(jnp.concatenate([loss_ref[0:1, :]] * (SLAB_W // LANES), axis=1))
        rows.append(jnp.zeros((SLAB_ROWS - ROW_LOSS - 1, SLAB_W), F32))
        loc_s[0:MIX_SLAB_ROWS, :] = ms_ref[...]
        loc_s[MIX_SLAB_ROWS:SLAB_ROWS, :] = jnp.concatenate(rows, axis=0)
        sib = (x, y, 1 - c)
        cp = pltpu.make_async_remote_copy(src_ref=loc_s, dst_ref=sib_s, send_sem=sems.at[0], recv_sem=sems.at[1],
                                          device_id=sib, device_id_type=MESH)
        cp.start()
        cp.wait()
        chip_s[...] = loc_s[...] + sib_s[...]
        mine = chip_s.at[pl.ds(pl.multiple_of(c * half, SUBLANES), half), :]
        r2_s[j] = chip_s[pl.ds(pl.multiple_of(c * half, SUBLANES), half), :]
        cps = []
        for k, (px, py) in enumerate(_other_chips(x, y)):
            cps.append(pltpu.make_async_remote_copy(src_ref=mine, dst_ref=r2_s.at[j], send_sem=sems.at[2 + k],
                                                    recv_sem=sems.at[5 + k], device_id=(px, py, c),
                                                    device_id_type=MESH))
        for cp in cps:
            cp.start()
        for cp in cps:
            cp.wait()
        fin_s[...] = ((r2_s[0] + r2_s[1]) + r2_s[2]) + r2_s[3]
        dst = out_ref.at[pl.ds(pl.multiple_of(c * half, SUBLANES), half), :]
        out_ref[pl.ds(pl.multiple_of(c * half, SUBLANES), half), :] = fin_s[...]
        cp = pltpu.make_async_remote_copy(src_ref=fin_s, dst_ref=dst, send_sem=sems.at[8], recv_sem=sems.at[9],
                                          device_id=sib, device_id_type=MESH)
        cp.start()
        cp.wait()

    return pl.pallas_call(
        body, name="allreduce_small", in_specs=[VMEM_SPEC] * 5, out_specs=VMEM_SPEC,
        out_shape=SDS((SLAB_ROWS, SLAB_W), F32),
        scratch_shapes=[pltpu.VMEM((SLAB_ROWS, SLAB_W), F32)] * 3 + [pltpu.VMEM((N_CHIPS, half, SLAB_W), F32),
                                                                       pltpu.VMEM((half, SLAB_W), F32),
                                                                       pltpu.SemaphoreType.DMA((10,))],
        compiler_params=_cp())(mix_slab, dg_mix, dg_ffn, dg_fin, loss8)


_SMALL_ROWS = (("conv_b", ROW_CONV_B), ("gate_a_b", ROW_BA), ("gate_x_b", ROW_BX), ("lru_lambda", ROW_LAM),
               ("pool_b", ROW_PB), ("pool_scale", ROW_PS), ("norm_lru_g", ROW_GL), ("norm_pool_g", ROW_GP))
_WIDE_ROWS = (("norm_mix_g", ROW_MIX), ("norm_ffn_g", ROW_FFN), ("final_norm_g", ROW_FIN))
_BLOCK_ROWS = (("gate_a_w", ROW_GA), ("gate_x_w", ROW_GX), ("pool_w", ROW_PW))
_SMALL_ORDER = tuple(n for n, _ in _SMALL_ROWS) + tuple(n for n, _ in _WIDE_ROWS) + tuple(
    n for n, _ in _BLOCK_ROWS) + ("conv_w",)


def _adamw_small(slab, wmv):
    names = _SMALL_ORDER
    flat = [a for nme in names for a in wmv[nme]]
    nin = len(flat)

    def body(*refs):
        slab_ref, j_ref = refs[0], refs[1]
        ins = refs[2:2 + nin]
        outs = refs[2 + nin:]
        grads = {}
        for nme, row in _SMALL_ROWS:
            grads[nme] = slab_ref[row:row + 1, :]
        for nme, row in _WIDE_ROWS:
            grads[nme] = jnp.concatenate([slab_ref[row:row + 1, :], slab_ref[row + 1:row + 2, :]], axis=1)
        full = slab_ref[ROW_CONV_W:ROW_CONV_W + CONV_WIDTH, :]
        jv = j_ref[0]
        g = jnp.zeros((CONV_WIDTH, LANES), F32)
        for jj in range(N_CHIPS):
            g = jnp.where(jv == jj, full[:, jj * LANES:(jj + 1) * LANES], g)
        grads["conv_w"] = g
        block_rows = dict(_BLOCK_ROWS)
        for idx, nme in enumerate(names):
            w_ref, m_ref, v_ref = ins[3 * idx:3 * idx + 3]
            if nme in block_rows:
                nblk, r, c = w_ref.shape
                parts = [(b, slab_ref[block_rows[nme]:block_rows[nme] + r, b * c:(b + 1) * c]) for b in range(nblk)]
            else:
                parts = [(Ellipsis, grads[nme])]
            for b, g in parts:
                delta, m, v = _adamw_math(w_ref[b], g, m_ref[b], v_ref[b])
                outs[4 * idx][b] = g
                outs[4 * idx + 1][b] = delta
                outs[4 * idx + 2][b] = m
                outs[4 * idx + 3][b] = v

    x, y, _ = _place()
    jidx = jnp.reshape(2 * x + y, (1,)).astype(jnp.int32)
    out_shape = [SDS(wmv[nme][0].shape, F32) for nme in names for _ in range(4)]
    res = pl.pallas_call(
        body, name="adamw_small",
        in_specs=[VMEM_SPEC, pl.BlockSpec(memory_space=pltpu.SMEM)] + [VMEM_SPEC] * nin,
        out_specs=[VMEM_SPEC] * len(out_shape), out_shape=out_shape, compiler_params=_cp())(slab, jidx, *flat)
    return {nme: tuple(res[4 * idx:4 * idx + 4]) for idx, nme in enumerate(names)}


_FFN = ("ffn_w1", "ffn_w3", "ffn_w2")
_TRANSPOSED = ("ffn_w1", "ffn_w3")


def _local_step(x, target, full, sp_, distributed):
    d = x.shape[1]
    (u,), got = _inproj(x, sp_["norm_mix_g"], full["w_in"],
                        [_ffn_gather_hosted([full["w_out"]])] if distributed else None)
    w_out = (got[0][0] if distributed else full["w_out"]).reshape(d, d)
    gather = [_ffn_gather_hosted([full[n] for n in _FFN])] if distributed else None
    (h, yn, hres1, saved, pooled), got = _mixer_fwd(u, x, sp_, w_out, gather)
    w1, w3, w2 = got[0] if distributed else [full[n] for n in _FFN]
    h2, a1, a3, ff = _ffn_up(hres1, sp_["norm_ffn_g"], w1, w3)
    dh, dhb, loss8, dg_fin = _ffn_down(ff, hres1, target, sp_["final_norm_g"], w2)
    da1, da3 = _ffn_bwd_gate(dhb, a1, a3, w2)
    dws = list(_ffn_wgrad2(h2, dhb, ff, da1, da3))
    rs1 = [_rs_sibling_hosted(dws)] if distributed else None
    (dhres1, dg_ffn), got = _ffn_bwd_down(da1, da3, dh, hres1, sp_["norm_ffn_g"], w1, w3, rs1)
    rs2 = None
    if distributed:
        pairs = [_add_own_half(a, r, "add_half_" + n) for n, a, r in zip(_FFN, dws, got[0])]
        rs2 = [_rs_chips_hosted([pb for _, pb in pairs])]
    (du, mix_slab), got = _mixer_bwd(u, saved, pooled, h, dhres1, sp_, w_out, rs2)
    g_mix = sp_["norm_mix_g"]
    if distributed:
        fin = [_sum_chips(pairs[k][0], got[0][k], "sum_chips_" + n) for k, n in enumerate(_FFN)]
        swap = _rs_swap_hosted(fin)
        state, token = _split_start(swap, "ffn_swap_start")
        g_mix = g_mix + token[0:1, 0:1]
    (gx, dwin, dwout, dg_mix), _ = _inproj_bwd(x, du, dhres1, yn, g_mix, full["w_in"])
    if distributed:
        sib = _split_wait(swap, state, dg_mix, "ffn_swap_wait")
    big = {"w_in": dwin, "w_out": dwout.reshape(N_CHIPS, d // N_CHIPS, d)}
    for k, n in enumerate(_FFN):
        big[n] = (fin[k], sib[k]) if distributed else dws[k]
    return gx, big, (mix_slab, dg_mix, dg_ffn, dg_fin, loss8)


def _to_compact(w):
    h, i, j = w.shape
    return jnp.transpose(w, (1, 0, 2)).reshape(i, h * j)


def _from_compact(w, h):
    i, hj = w.shape
    return jnp.transpose(w.reshape(i, h, hj // h), (1, 0, 2))


_SMALL_LAYOUT = {
    "gate_a_w": (lambda a: a[0], lambda a: a[None]),
    "gate_x_w": (lambda a: a[0], lambda a: a[None]),
    "pool_w": (lambda a: a[0], lambda a: a[None]),
    "conv_w": (lambda a: a[0], lambda a: a[None]),
    "final_norm_g": (lambda a: a[None], lambda a: a[0]),
}

_WEIGHTS = ("norm_mix_g", "w_in", "conv_w", "conv_b", "gate_a_w", "gate_a_b", "gate_x_w", "gate_x_b", "lru_lambda",
            "pool_w", "pool_b", "pool_scale", "norm_lru_g", "norm_pool_g", "w_out", "norm_ffn_g", "ffn_w1",
            "ffn_w3", "ffn_w2", "final_norm_g")


def kernel(x, norm_mix_g, w_in, conv_w, conv_b, gate_a_w, gate_a_b, gate_x_w, gate_x_b, lru_lambda, pool_w, pool_b, pool_scale, norm_lru_g, norm_pool_g, w_out, norm_ffn_g, ffn_w1, ffn_w3, ffn_w2, final_norm_g, loss_target, m_norm_mix_g, m_w_in, m_conv_w, m_conv_b, m_gate_a_w, m_gate_a_b, m_gate_x_w, m_gate_x_b, m_lru_lambda, m_pool_w, m_pool_b, m_pool_scale, m_norm_lru_g, m_norm_pool_g, m_w_out, m_norm_ffn_g, m_ffn_w1, m_ffn_w3, m_ffn_w2, m_final_norm_g, v_norm_mix_g, v_w_in, v_conv_w, v_conv_b, v_gate_a_w, v_gate_a_b, v_gate_x_w, v_gate_x_b, v_lru_lambda, v_pool_w, v_pool_b, v_pool_scale, v_norm_lru_g, v_norm_pool_g, v_w_out, v_norm_ffn_g, v_ffn_w1, v_ffn_w3, v_ffn_w2, v_final_norm_g):
    loc = locals()
    w = {n: loc[n] for n in _WEIGHTS}
    m = {n: loc["m_" + n] for n in _WEIGHTS}
    v = {n: loc["v_" + n] for n in _WEIGHTS}

    def lay(nme, a):
        return _SMALL_LAYOUT[nme][0](a) if nme in _SMALL_LAYOUT else a

    def unlay(nme, a):
        return _SMALL_LAYOUT[nme][1](a) if nme in _SMALL_LAYOUT else a

    for group in (w, m, v):
        for n in _TRANSPOSED:
            group[n] = jnp.transpose(group[n], (0, 2, 1))

    gathered = _gather_weights([w[n][0] for n in _BIG], w["conv_w"][0], n_remote=1)
    full = dict(zip(_BIG, gathered[:-1]))
    cw_all = gathered[-1]
    sp_ = {n: lay(n, w[n]) for n in _SMALL_ORDER}
    sp_["conv_w"] = jnp.transpose(cw_all[:, :CONV_WIDTH, :], (1, 0, 2)).reshape(CONV_WIDTH, N_CHIPS * LANES)

    gx, big, small = _local_step(x[0], loss_target[0], full, sp_, distributed=True)

    late = ("w_in", "w_out")
    fin = {n: big[n][0] for n in _FFN}
    sib = {n: big[n][1] for n in _FFN}
    recv1, = _run_comm([_rs_sibling_hosted([big[n] for n in late])], "tail_sibling")
    pairs = [_add_own_half(big[n], r, "add_half_" + n) for n, r in zip(late, recv1)]
    chips = _rs_chips_hosted([pb for _, pb in pairs])
    state, token = _split_start(chips, "tail_chips_start")
    out = {}
    for n in _FFN:
        out[n] = tuple(_adamw_big(w[n], fin[n], sib[n], m[n], v[n], "adamw_" + n, token))
    recv2 = _split_wait(chips, state, out[_FFN[-1]][1], "tail_chips_wait")
    for n, (p, _), r in zip(late, pairs, recv2):
        fin[n] = _sum_chips(p, r, "sum_chips_" + n)
    swapped, = _run_comm([_rs_swap_hosted([fin[n] for n in late])], "tail_swap")
    sib.update(zip(late, swapped))
    for n in late:
        out[n] = tuple(_adamw_big(w[n], fin[n], sib[n], m[n], v[n], "adamw_" + n))
    for n in _TRANSPOSED:
        out[n] = tuple(jnp.transpose(a, (0, 2, 1)) for a in out[n])
    slab = _allreduce_small(*small)
    loss = slab[ROW_LOSS, 0]
    wmv = {n: (lay(n, w[n]), lay(n, m[n]), lay(n, v[n])) for n in _SMALL_ORDER}
    res = _adamw_small(slab, wmv)
    for n in _SMALL_ORDER:
        out[n] = tuple(unlay(n, a) for a in res[n])
    return (loss, gx[None]) + tuple(out[n][k] for k in range(4) for n in _WEIGHTS)
```

```python
import functools
import math

import jax
import jax.numpy as jnp
from jax import lax
from jax.experimental import pallas as pl
from jax.experimental.pallas import tpu as pltpu

F32 = jnp.float32
BF16 = jnp.bfloat16
SDS = jax.ShapeDtypeStruct
MESH = pl.DeviceIdType.MESH

EPS = 1e-6
LRU_C = 8.0
CONV_WIDTH = 4
POOL_WINDOWS = (2, 4, 8, 16)
HALO = 16
LANES = 128
SUBLANES = 8
GATE_BLOCK = 256
N_CHIPS = 4

ADAM_LR = 0.001
ADAM_B1 = 0.9
ADAM_B2 = 0.999
ADAM_EPS = 1e-08
ADAM_WD = 0.01
ADAM_STEP = 10

TM_PROJ = 512
TM_MIX = 512
TM_FFN = 512
TM_WGRAD = 1024
TM_FFN_UP = 1024
TM_FFN_DOWN = 512
MIX_SAVED = ("xc", "r", "ig", "a", "m2raw", "ge", "dge")
FFN_ROW_CHUNKS = 2
VMEM_LIMIT = 56 * 1024 * 1024

SLAB_W = 512
ROW_CONV_B, ROW_CONV_W, ROW_BA, ROW_BX, ROW_LAM, ROW_PB, ROW_PS, ROW_GL, ROW_GP = 0, 1, 5, 6, 7, 8, 9, 10, 11
ROW_GA, ROW_GX, ROW_PW = 16, 80, 144
ROW_MIX, ROW_FFN, ROW_FIN, ROW_LOSS = 272, 274, 276, 278
MIX_SLAB_ROWS = 272
SLAB_ROWS = 288


def _cp(sem=None, **kw):
    if sem is not None:
        kw["dimension_semantics"] = sem
    return pltpu.CompilerParams(vmem_limit_bytes=VMEM_LIMIT, **kw)


def _const_spec(shape):
    nd = len(shape)
    return pl.BlockSpec(shape, lambda *_: (0,) * nd, pipeline_mode=pl.Buffered(1))


def _sigmoid(x):
    return 1.0 / (1.0 + jnp.exp(-x))


def _dot(a, b):
    return jnp.dot(a, b, preferred_element_type=F32)


def _dot_nt(a, b):
    return lax.dot_general(a, b, (((1,), (1,)), ((), ())), preferred_element_type=F32)


def _dot_tn(a, b):
    return lax.dot_general(a, b, (((0,), (0,)), ((), ())), preferred_element_type=F32)


def _colsum8(v):
    m, c = v.shape
    return v.reshape(m // SUBLANES, SUBLANES, c).sum(axis=0)


def _rowmean(v):
    return jnp.mean(v, axis=-1, keepdims=True)


def _rms_bwd(dy, xhat, r, g):
    dxh = dy * g
    return r * (dxh - xhat * _rowmean(dxh * xhat))


def _softplus_neg(lam):
    z = -lam
    e = jnp.exp(-jnp.abs(z))
    u = 1.0 + e
    d = u - 1.0
    log1p = jnp.where(d == 0.0, e, jnp.log(u) * (e / jnp.where(d == 0.0, 1.0, d)))
    return jnp.maximum(z, 0.0) + log1p


def _neg_expm1(z):
    series = -(z * (1.0 + z * (0.5 + z * (1.0 / 6.0 + z * (1.0 / 24.0)))))
    return jnp.where(z > -0.03, series, 1.0 - jnp.exp(z))


_GELU_C = math.sqrt(2.0 / math.pi)
_GELU_K = 0.044715


def _gelu_parts(x):
    x2 = x * x
    th = jnp.tanh(_GELU_C * (x + _GELU_K * x2 * x))
    ge = 0.5 * x * (1.0 + th)
    dge = 0.5 * (1.0 + th) + 0.5 * x * (1.0 - th * th) * (_GELU_C * (1.0 + 3.0 * _GELU_K * x2))
    return ge, dge


def _shift_down(halo, tile, k):
    if k == 0:
        return tile
    ext = jnp.concatenate([halo, tile], axis=0)
    n = tile.shape[0]
    h = halo.shape[0]
    return ext[h - k:h - k + n]


def _shift_up(tile, nxt, k):
    if k == 0:
        return tile
    ext = jnp.concatenate([tile, nxt], axis=0)
    return ext[k:k + tile.shape[0]]


def _build_gate_blocks(ga_ref, gx_ref, gw_ref):
    hd = ga_ref.shape[1]
    per = GATE_BLOCK // hd
    zero = jnp.zeros((hd, hd), F32)
    for b in range(gw_ref.shape[0]):
        for src, off in ((ga_ref, 0), (gx_ref, GATE_BLOCK)):
            for hh in range(per):
                row = jnp.concatenate([zero] * hh + [src[b * per + hh]] + [zero] * (per - 1 - hh), axis=1)
                gw_ref[b, hh * hd:(hh + 1) * hd, off:off + GATE_BLOCK] = row.astype(BF16)


def _scan_level1(a, b, reverse):
    m, c = a.shape
    a3 = a.reshape(m // SUBLANES, SUBLANES, c)
    b3 = b.reshape(m // SUBLANES, SUBLANES, c)
    row = lax.broadcasted_iota(jnp.int32, a3.shape, 1)
    for s in (1, 2, 4):
        sh = (SUBLANES - s) if reverse else s
        a_sh = pltpu.roll(a3, sh, 1)
        b_sh = pltpu.roll(b3, sh, 1)
        ok = (row < SUBLANES - s) if reverse else (row >= s)
        b3 = jnp.where(ok, a3 * b_sh + b3, b3)
        a3 = jnp.where(ok, a3 * a_sh, a3)
    return a3.reshape(m, c), b3.reshape(m, c)


def _scan_level2(a_ref, b_ref, out_ref, carry, reverse):
    m, c = a_ref.shape
    ng = m // SUBLANES

    def step(g, cr):
        gi = (ng - 1 - g) if reverse else g
        off = pl.multiple_of(gi * SUBLANES, SUBLANES)
        h = b_ref[pl.ds(off, SUBLANES), :] + a_ref[pl.ds(off, SUBLANES), :] * cr
        out_ref[pl.ds(off, SUBLANES), :] = h
        edge = h[0:1, :] if reverse else h[SUBLANES - 1:SUBLANES, :]
        return jnp.broadcast_to(edge, (SUBLANES, c))

    return lax.fori_loop(0, ng, step, carry, unroll=4)


def _mixer_recompute(u, hal, t0, cw, cb, gw_ref, ba, bx, lam, pw_ref, pb, ps):
    tm = u.shape[0]
    lw = cb.shape[1]
    u_l, u_g, u_p = u[:, :lw], u[:, lw:2 * lw], u[:, 2 * lw:]
    hal_l, hal_p = hal[:, :lw], hal[:, 2 * lw:]
    taps = [_shift_down(hal_l, u_l, CONV_WIDTH - 1 - k) for k in range(CONV_WIDTH)]
    xc = cb
    for k in range(CONV_WIDTH):
        xc = xc + taps[k] * cw[k:k + 1, :]
    xcb = xc.astype(BF16)
    nb = lw // GATE_BLOCK
    gs = [_dot(xcb[:, b * GATE_BLOCK:(b + 1) * GATE_BLOCK], gw_ref[b]) for b in range(nb)]
    r = _sigmoid(jnp.concatenate([g[:, :GATE_BLOCK] for g in gs], axis=1) + ba)
    ig = _sigmoid(jnp.concatenate([g[:, GATE_BLOCK:] for g in gs], axis=1) + bx)
    sp = _softplus_neg(lam)
    la = (-LRU_C * r) * sp
    a = jnp.exp(la)
    m2raw = _neg_expm1(2.0 * la)
    mult = jnp.sqrt(jnp.maximum(m2raw, 1e-12))
    ge, dge = _gelu_parts(u_g)
    row = lax.broadcasted_iota(jnp.int32, (tm, LANES), 0) + t0
    pooled, invs, zs = [], [], []
    for gi, w in enumerate(POOL_WINDOWS):
        e = jnp.concatenate([hal_p[:, gi * LANES:(gi + 1) * LANES], u_p[:, gi * LANES:(gi + 1) * LANES]], axis=0)
        s = e
        k = 1
        while k < w:
            s = s + pltpu.roll(s, k, 0)
            k *= 2
        inv = 1.0 / jnp.minimum(row + 1, w).astype(F32)
        pg = s[HALO:] * inv - e[HALO:]
        pooled.append(pg)
        invs.append(inv)
        zs.append(_dot(pg.astype(BF16), pw_ref[gi].astype(BF16)))
    z = jnp.concatenate(zs, axis=1) + pb
    y_pool = z * ps
    return dict(u_l=u_l, u_g=u_g, taps=taps, xc=xc, xcb=xcb, r=r, ig=ig, sp=sp, la=la, a=a, m2raw=m2raw,
                mult=mult, ge=ge, dge=dge, pooled=pooled, invs=invs, z=z, y_pool=y_pool)


ANY = pl.BlockSpec(memory_space=pl.ANY)
VMEM_SPEC = pl.BlockSpec(memory_space=pltpu.VMEM)


class _Hosted:
    def __init__(self, ins, out_shapes, sems, start, finish, mid=None, aliases=None):
        self.ins, self.out_shapes, self.sems = list(ins), list(out_shapes), list(sems)
        self.start, self.mid, self.finish = start, mid, finish
        self.aliases = dict(aliases or {})


def _call(body, hosted, stage_preds, *, name, grid, in_specs, out_specs, out_shape, scratch_shapes, args, sem):
    hosted = list(hosted or [])
    n_in, n_out, n_scr = len(in_specs), len(out_specs), len(scratch_shapes)
    c_in = [a for h in hosted for a in h.ins]
    c_out = [o for h in hosted for o in h.out_shapes]
    c_sem = [pltpu.SemaphoreType.DMA((k,)) for h in hosted for k in h.sems]

    def full(*refs):
        p = 0
        parts = []
        for cnt in (n_in, len(c_in), n_out, len(c_out), n_scr, len(c_sem)):
            parts.append(refs[p:p + cnt])
            p += cnt
        hi, ci, ho, co, hs, cs = parts
        per = []
        a = b = c_ = 0
        for h in hosted:
            per.append((h, ci[a:a + len(h.ins)], co[b:b + len(h.out_shapes)], cs[c_:c_ + len(h.sems)]))
            a, b, c_ = a + len(h.ins), b + len(h.out_shapes), c_ + len(h.sems)
        first = mid = last = None
        if hosted and grid:
            first, mid, last = stage_preds()

        def run(fn, pred, i_, o_, s_):
            if fn is None:
                return
            if pred is None:
                fn(i_, o_, s_)
            else:
                pl.when(pred)(functools.partial(fn, i_, o_, s_))

        for h, i_, o_, s_ in per:
            run(h.start, first, i_, o_, s_)
        body(*hi, *ho, *hs)
        for h, i_, o_, s_ in per:
            run(h.mid, mid, i_, o_, s_)
        for h, i_, o_, s_ in per:
            run(h.finish, last, i_, o_, s_)

    aliases = {}
    a = b = 0
    for h in hosted:
        for k, v in h.aliases.items():
            aliases[n_in + a + k] = n_out + b + v
        a, b = a + len(h.ins), b + len(h.out_shapes)
    res = pl.pallas_call(
        full, name=name, grid=grid, in_specs=list(in_specs) + [ANY] * len(c_in),
        out_specs=list(out_specs) + [ANY] * len(c_out), out_shape=list(out_shape) + c_out,
        scratch_shapes=list(scratch_shapes) + c_sem, input_output_aliases=aliases,
        compiler_params=_cp(sem))(*args, *c_in)
    res = list(res)
    outs = []
    p = n_out
    for h in hosted:
        outs.append(res[p:p + len(h.out_shapes)])
        p += len(h.out_shapes)
    return res[:n_out], outs


def _inproj(x, g_mix, w_in, hosted=None):
    s, d = x.shape
    n = w_in.shape[1]
    tm = min(TM_PROJ, s)
    nt = s // tm

    def body(x_ref, g_ref, w_ref, u_ref):
        xv = x_ref[...]
        r = lax.rsqrt(_rowmean(xv * xv) + EPS)
        u_ref[...] = _dot((xv * r * g_ref[...]).astype(BF16), w_ref[...])

    def stages():
        i = pl.program_id(0)
        return i == 0, i == max(nt - 3, 0), i == nt - 1

    return _call(
        body, hosted, stages, grid=(nt,), name="inproj",
        in_specs=[pl.BlockSpec((tm, d), lambda i: (i, 0)), _const_spec((1, d)), _const_spec((d, n))],
        out_specs=[pl.BlockSpec((tm, n), lambda i: (i, 0))], out_shape=[SDS((s, n), F32)], scratch_shapes=[],
        args=(x, g_mix, w_in), sem=("arbitrary",))


def _mixer_fwd(u, x, sp_, w_out, hosted=None):
    s, din = u.shape
    d = x.shape[1]
    lw = din // 3
    tm = min(TM_MIX, s)
    nb = lw // GATE_BLOCK

    def body(u_ref, halo_ref, x_ref, cw_ref, cb_ref, ga_ref, gx_ref, ba_ref, bx_ref, lam_ref, pw_ref, pb_ref,
             ps_ref, gl_ref, gp_ref, wout_ref, h_ref, yn_ref, hres_ref, saved_ref, pooled_ref,
             gw_s, a_s, b_s, carry_s):
        i = pl.program_id(0)

        @pl.when(i == 0)
        def _():
            _build_gate_blocks(ga_ref, gx_ref, gw_s)
            carry_s[...] = jnp.zeros_like(carry_s)

        uv = u_ref[...]
        hal = jnp.where(i > 0, halo_ref[...], 0.0)
        f = _mixer_recompute(uv, hal, i * tm, cw_ref[...], cb_ref[...], gw_s, ba_ref[...], bx_ref[...],
                             lam_ref[...], pw_ref, pb_ref[...], ps_ref[...])
        for k, name in enumerate(MIX_SAVED):
            saved_ref[k] = f[name]
        pooled_ref[...] = jnp.concatenate(f["pooled"], axis=1).astype(BF16)
        bb = f["mult"] * (f["ig"] * f["xc"])
        a1, b1 = _scan_level1(f["a"], bb, reverse=False)
        a_s[...] = a1
        b_s[...] = b1
        carry_s[...] = _scan_level2(a_s, b_s, h_ref, carry_s[...], reverse=False)
        y_lru = h_ref[...] * f["ge"]
        rl = lax.rsqrt(_rowmean(y_lru * y_lru) + EPS)
        yp = f["y_pool"]
        rp = lax.rsqrt(_rowmean(yp * yp) + EPS)
        yn = jnp.concatenate([y_lru * rl * gl_ref[...], yp * rp * gp_ref[...]], axis=1).astype(BF16)
        yn_ref[...] = yn
        hres_ref[...] = x_ref[...] + _dot(yn, wout_ref[...])

    small = [sp_[k] for k in ("conv_w", "conv_b", "gate_a_w", "gate_x_w", "gate_a_b", "gate_x_b", "lru_lambda",
                              "pool_w", "pool_b", "pool_scale", "norm_lru_g", "norm_pool_g")]
    nt = s // tm

    def stages():
        i = pl.program_id(0)
        return i == 0, i == max(nt - 3, 0), i == nt - 1

    return _call(
        body, hosted, stages, grid=(nt,), name="mixer_fwd",
        in_specs=[pl.BlockSpec((tm, din), lambda i: (i, 0)),
                  pl.BlockSpec((HALO, din), lambda i: (jnp.maximum(i * (tm // HALO) - 1, 0), 0)),
                  pl.BlockSpec((tm, d), lambda i: (i, 0))]
        + [_const_spec(a.shape) for a in small] + [_const_spec(w_out.shape)],
        out_specs=[pl.BlockSpec((tm, lw), lambda i: (i, 0)), pl.BlockSpec((tm, d), lambda i: (i, 0)),
                   pl.BlockSpec((tm, d), lambda i: (i, 0)),
                   pl.BlockSpec((len(MIX_SAVED), tm, lw), lambda i: (0, i, 0)),
                   pl.BlockSpec((tm, lw), lambda i: (i, 0))],
        out_shape=[SDS((s, lw), F32), SDS((s, d), BF16), SDS((s, d), F32), SDS((len(MIX_SAVED), s, lw), F32),
                   SDS((s, lw), BF16)],
        scratch_shapes=[pltpu.VMEM((nb, GATE_BLOCK, 2 * GATE_BLOCK), BF16), pltpu.VMEM((tm, lw), F32),
                        pltpu.VMEM((tm, lw), F32), pltpu.VMEM((SUBLANES, lw), F32)],
        args=(u, u, x, *small, w_out), sem=("arbitrary",))


def _ffn_fwd(hres1, target, g_ffn, g_fin, w1, w3, w2):
    s, d = hres1.shape
    nj, _, fc = w1.shape
    tm = min(TM_FFN, s)

    def body(h_ref, t_ref, gf_ref, gn_ref, w1_ref, w3_ref, w2_ref,
             a1_ref, a3_ref, h2_ref, dh_ref, dhb_ref, loss_ref, dgn_ref, acc_s):
        i, j = pl.program_id(0), pl.program_id(1)

        @pl.when((i == 0) & (j == 0))
        def _():
            loss_ref[...] = jnp.zeros_like(loss_ref)
            dgn_ref[...] = jnp.zeros_like(dgn_ref)

        @pl.when(j == 0)
        def _():
            hv = h_ref[...]
            r = lax.rsqrt(_rowmean(hv * hv) + EPS)
            h2_ref[...] = (hv * r * gf_ref[...]).astype(BF16)

        h2 = h2_ref[...]
        a1 = _dot(h2, w1_ref[0])
        a3 = _dot(h2, w3_ref[0])
        a1_ref[0] = a1.astype(BF16)
        a3_ref[0] = a3.astype(BF16)
        part = _dot(((a1 * _sigmoid(a1)) * a3).astype(BF16), w2_ref[0])

        @pl.when(j == 0)
        def _():
            acc_s[...] = part

        @pl.when(j > 0)
        def _():
            acc_s[...] += part

        @pl.when(j == nj - 1)
        def _():
            hr2 = h_ref[...] + acc_s[...]
            r2 = lax.rsqrt(_rowmean(hr2 * hr2) + EPS)
            xh = hr2 * r2
            gn = gn_ref[...]
            diff = xh * gn - t_ref[...]
            tot = jnp.sum(jnp.sum(diff * diff, axis=1, keepdims=True), axis=0, keepdims=True)
            loss_ref[...] += tot * (0.5 / d)
            dout = diff * (1.0 / d)
            dgn_ref[...] += _colsum8(dout * xh)
            dh = _rms_bwd(dout, xh, r2, gn)
            dh_ref[...] = dh
            dhb_ref[...] = dh.astype(BF16)

    return pl.pallas_call(
        body, grid=(s // tm, nj), name="ffn_fwd",
        in_specs=[pl.BlockSpec((tm, d), lambda i, j: (i, 0)), pl.BlockSpec((tm, d), lambda i, j: (i, 0)),
                  _const_spec((1, d)), _const_spec((1, d)),
                  pl.BlockSpec((1, d, fc), lambda i, j: (j, 0, 0)), pl.BlockSpec((1, d, fc), lambda i, j: (j, 0, 0)),
                  pl.BlockSpec((1, fc, d), lambda i, j: (j, 0, 0))],
        out_specs=[pl.BlockSpec((1, tm, fc), lambda i, j: (j, i, 0)), pl.BlockSpec((1, tm, fc), lambda i, j: (j, i, 0)),
                   pl.BlockSpec((tm, d), lambda i, j: (i, 0)), pl.BlockSpec((tm, d), lambda i, j: (i, 0)),
                   pl.BlockSpec((tm, d), lambda i, j: (i, 0)),
                   pl.BlockSpec((SUBLANES, LANES), lambda i, j: (0, 0)),
                   pl.BlockSpec((SUBLANES, d), lambda i, j: (0, 0))],
        out_shape=[SDS((nj, s, fc), BF16), SDS((nj, s, fc), BF16), SDS((s, d), BF16), SDS((s, d), F32),
                   SDS((s, d), BF16), SDS((SUBLANES, LANES), F32), SDS((SUBLANES, d), F32)],
        scratch_shapes=[pltpu.VMEM((tm, d), F32)],
        compiler_params=_cp(("arbitrary", "arbitrary")))(hres1, target, g_ffn, g_fin, w1, w3, w2)


def _ffn_bwd_act(dh, dhb, a1, a3, hres1, g_ffn, w1, w3, w2):
    s, d = hres1.shape
    nj, _, fc = a1.shape
    tm = min(TM_FFN, s)

    def body(dh_ref, dhb_ref, a1_ref, a3_ref, h_ref, gf_ref, w1_ref, w3_ref, w2_ref,
             da1_ref, da3_ref, dhr_ref, dgf_ref, acc_s):
        i, j = pl.program_id(0), pl.program_id(1)

        @pl.when((i == 0) & (j == 0))
        def _():
            dgf_ref[...] = jnp.zeros_like(dgf_ref)

        @pl.when(j == 0)
        def _():
            acc_s[...] = jnp.zeros_like(acc_s)

        rc = tm // FFN_ROW_CHUNKS
        for q in range(FFN_ROW_CHUNKS):
            rows = slice(q * rc, (q + 1) * rc)
            dff = _dot_nt(dhb_ref[rows, :], w2_ref[0])
            a1v = a1_ref[0, rows, :].astype(F32)
            a3v = a3_ref[0, rows, :].astype(F32)
            sg = _sigmoid(a1v)
            silu = a1v * sg
            da1 = (dff * a3v * (sg * (1.0 + a1v * (1.0 - sg)))).astype(BF16)
            da3 = (dff * silu).astype(BF16)
            da1_ref[0, rows, :] = da1
            da3_ref[0, rows, :] = da3
            acc_s[rows, :] += _dot_nt(da1, w1_ref[0]) + _dot_nt(da3, w3_ref[0])

        @pl.when(j == nj - 1)
        def _():
            hv = h_ref[...]
            r = lax.rsqrt(_rowmean(hv * hv) + EPS)
            xh = hv * r
            dh2 = acc_s[...]
            dgf_ref[...] += _colsum8(dh2 * xh)
            dhr_ref[...] = dh_ref[...] + _rms_bwd(dh2, xh, r, gf_ref[...])

    return pl.pallas_call(
        body, grid=(s // tm, nj), name="ffn_bwd_act",
        in_specs=[pl.BlockSpec((tm, d), lambda i, j: (i, 0)), pl.BlockSpec((tm, d), lambda i, j: (i, 0)),
                  pl.BlockSpec((1, tm, fc), lambda i, j: (j, i, 0)), pl.BlockSpec((1, tm, fc), lambda i, j: (j, i, 0)),
                  pl.BlockSpec((tm, d), lambda i, j: (i, 0)), _const_spec((1, d)),
                  pl.BlockSpec((1, d, fc), lambda i, j: (j, 0, 0)), pl.BlockSpec((1, d, fc), lambda i, j: (j, 0, 0)),
                  pl.BlockSpec((1, fc, d), lambda i, j: (j, 0, 0))],
        out_specs=[pl.BlockSpec((1, tm, fc), lambda i, j: (j, i, 0)), pl.BlockSpec((1, tm, fc), lambda i, j: (j, i, 0)),
                   pl.BlockSpec((tm, d), lambda i, j: (i, 0)), pl.BlockSpec((SUBLANES, d), lambda i, j: (0, 0))],
        out_shape=[SDS((nj, s, fc), BF16), SDS((nj, s, fc), BF16), SDS((s, d), F32), SDS((SUBLANES, d), F32)],
        scratch_shapes=[pltpu.VMEM((tm, d), F32)],
        compiler_params=_cp(("arbitrary", "arbitrary")))(dh, dhb, a1, a3, hres1, g_ffn, w1, w3, w2)


def _ffn_wgrad(h2, dhb, a1, a3, da1, da3):
    s, d = h2.shape
    _, _, fc = a1.shape
    tm = min(TM_WGRAD, s)

    def body(h2_ref, dhb_ref, a1_ref, a3_ref, da1_ref, da3_ref, dw1_ref, dw3_ref, dw2_ref):
        i = pl.program_id(1)

        @pl.when(i == 0)
        def _():
            dw1_ref[...] = jnp.zeros_like(dw1_ref)
            dw3_ref[...] = jnp.zeros_like(dw3_ref)
            dw2_ref[...] = jnp.zeros_like(dw2_ref)

        h2v = h2_ref[...]
        a1v = a1_ref[0].astype(F32)
        ff = ((a1v * _sigmoid(a1v)) * a3_ref[0].astype(F32)).astype(BF16)
        dw1_ref[0] += _dot_tn(h2v, da1_ref[0])
        dw3_ref[0] += _dot_tn(h2v, da3_ref[0])
        dw2_ref[0] += _dot_tn(ff, dhb_ref[...])

    return pl.pallas_call(
        body, grid=(N_CHIPS, s // tm), name="ffn_wgrad",
        in_specs=[pl.BlockSpec((tm, d), lambda j, i: (i, 0)), pl.BlockSpec((tm, d), lambda j, i: (i, 0))]
        + [pl.BlockSpec((1, tm, fc), lambda j, i: (j, i, 0))] * 4,
        out_specs=[pl.BlockSpec((1, d, fc), lambda j, i: (j, 0, 0)), pl.BlockSpec((1, d, fc), lambda j, i: (j, 0, 0)),
                   pl.BlockSpec((1, fc, d), lambda j, i: (j, 0, 0))],
        out_shape=[SDS((N_CHIPS, d, fc), F32), SDS((N_CHIPS, d, fc), F32), SDS((N_CHIPS, fc, d), F32)],
        compiler_params=_cp(("parallel", "arbitrary")))(h2, dhb, a1, a3, da1, da3)


def _row_chunks(tm):
    rc = tm // FFN_ROW_CHUNKS
    return [slice(q * rc, (q + 1) * rc) for q in range(FFN_ROW_CHUNKS)]


def _ffn_up(hres1, g_ffn, w1, w3):
    s, d = hres1.shape
    nj, fc, _ = w1.shape
    tm = min(TM_FFN_DOWN, s)

    def body(h_ref, gf_ref, w1_ref, w3_ref, h2_ref, a1_ref, a3_ref, ff_ref):
        hv = h_ref[...]
        r = lax.rsqrt(_rowmean(hv * hv) + EPS)
        h2_ref[...] = (hv * r * gf_ref[...]).astype(BF16)
        h2 = h2_ref[...]
        for j in range(nj):
            a1 = _dot_nt(h2, w1_ref[j])
            a3 = _dot_nt(h2, w3_ref[j])
            a1_ref[j] = a1.astype(BF16)
            a3_ref[j] = a3.astype(BF16)
            ff_ref[j] = ((a1 * _sigmoid(a1)) * a3).astype(BF16)

    wspec = _const_spec(w1.shape)
    aspec = pl.BlockSpec((nj, tm, fc), lambda i: (0, i, 0))
    return pl.pallas_call(
        body, grid=(s // tm,), name="ffn_up",
        in_specs=[pl.BlockSpec((tm, d), lambda i: (i, 0)), _const_spec((1, d)), wspec, wspec],
        out_specs=[pl.BlockSpec((tm, d), lambda i: (i, 0)), aspec, aspec, aspec],
        out_shape=[SDS((s, d), BF16)] + [SDS((nj, s, fc), BF16)] * 3,
        compiler_params=_cp(("parallel",)))(hres1, g_ffn, w1, w3)


def _ffn_down(ff, hres1, target, g_fin, w2):
    s, d = hres1.shape
    nj, _, fc = ff.shape
    tm = min(TM_FFN_DOWN, s)

    def body(ff_ref, h_ref, t_ref, gn_ref, w2_ref, dh_ref, dhb_ref, loss_ref, dgn_ref):
        @pl.when(pl.program_id(0) == 0)
        def _():
            loss_ref[...] = jnp.zeros_like(loss_ref)
            dgn_ref[...] = jnp.zeros_like(dgn_ref)

        gn = gn_ref[...]
        for rows in _row_chunks(tm):
            acc = _dot(ff_ref[0, rows, :], w2_ref[0])
            for j in range(1, nj):
                acc = acc + _dot(ff_ref[j, rows, :], w2_ref[j])
            hr2 = h_ref[rows, :] + acc
            r2 = lax.rsqrt(_rowmean(hr2 * hr2) + EPS)
            xh = hr2 * r2
            diff = xh * gn - t_ref[rows, :]
            tot = jnp.sum(jnp.sum(diff * diff, axis=1, keepdims=True), axis=0, keepdims=True)
            loss_ref[...] += tot * (0.5 / d)
            dout = diff * (1.0 / d)
            dgn_ref[...] += _colsum8(dout * xh)
            dh = _rms_bwd(dout, xh, r2, gn)
            dh_ref[rows, :] = dh
            dhb_ref[rows, :] = dh.astype(BF16)

    tile = pl.BlockSpec((tm, d), lambda i: (i, 0))
    return pl.pallas_call(
        body, grid=(s // tm,), name="ffn_down",
        in_specs=[pl.BlockSpec((nj, tm, fc), lambda i: (0, i, 0)), tile, tile, _const_spec((1, d)),
                  _const_spec(w2.shape)],
        out_specs=[tile, tile, pl.BlockSpec((SUBLANES, LANES), lambda i: (0, 0)),
                   pl.BlockSpec((SUBLANES, d), lambda i: (0, 0))],
        out_shape=[SDS((s, d), F32), SDS((s, d), BF16), SDS((SUBLANES, LANES), F32), SDS((SUBLANES, d), F32)],
        compiler_params=_cp(("arbitrary",)))(ff, hres1, target, g_fin, w2)


def _ffn_bwd_gate(dhb, a1, a3, w2):
    s, d = dhb.shape
    nj, _, fc = a1.shape
    tm = min(TM_FFN_DOWN, s)

    def body(dhb_ref, a1_ref, a3_ref, w2_ref, da1_ref, da3_ref):
        for j in range(nj):
            for rows in _row_chunks(tm):
                dff = _dot_nt(dhb_ref[rows, :], w2_ref[j])
                a1v = a1_ref[j, rows, :].astype(F32)
                sg = _sigmoid(a1v)
                silu = a1v * sg
                da1_ref[j, rows, :] = (dff * a3_ref[j, rows, :].astype(F32)
                                       * (sg * (1.0 + (a1v - silu)))).astype(BF16)
                da3_ref[j, rows, :] = (dff * silu).astype(BF16)

    aspec = pl.BlockSpec((nj, tm, fc), lambda i: (0, i, 0))
    return pl.pallas_call(
        body, grid=(s // tm,), name="ffn_bwd_gate",
        in_specs=[pl.BlockSpec((tm, d), lambda i: (i, 0)), aspec, aspec, _const_spec(w2.shape)],
        out_specs=[aspec, aspec], out_shape=[SDS((nj, s, fc), BF16)] * 2,
        compiler_params=_cp(("parallel",)))(dhb, a1, a3, w2)


def _ffn_bwd_down(da1, da3, dh, hres1, g_ffn, w1, w3, hosted=None):
    s, d = hres1.shape
    nj, _, fc = da1.shape
    tm = min(TM_FFN_DOWN, s)
    nt = s // tm

    def body(da1_ref, da3_ref, dh_ref, h_ref, gf_ref, w1_ref, w3_ref, dhr_ref, dgf_ref):
        @pl.when(pl.program_id(0) == 0)
        def _():
            dgf_ref[...] = jnp.zeros_like(dgf_ref)

        gf = gf_ref[...]
        for rows in _row_chunks(tm):
            dh2 = None
            for j in range(nj):
                part = _dot(da1_ref[j, rows, :], w1_ref[j]) + _dot(da3_ref[j, rows, :], w3_ref[j])
                dh2 = part if dh2 is None else dh2 + part
            hv = h_ref[rows, :]
            r = lax.rsqrt(_rowmean(hv * hv) + EPS)
            xh = hv * r
            dgf_ref[...] += _colsum8(dh2 * xh)
            dhr_ref[rows, :] = dh_ref[rows, :] + _rms_bwd(dh2, xh, r, gf)

    tile = pl.BlockSpec((tm, d), lambda i: (i, 0))
    aspec = pl.BlockSpec((nj, tm, fc), lambda i: (0, i, 0))
    wspec = _const_spec(w1.shape)

    def stages():
        i = pl.program_id(0)
        return i == 0, i == max(nt - 2, 0), i == nt - 1

    return _call(
        body, hosted, stages, grid=(nt,), name="ffn_bwd_down",
        in_specs=[aspec, aspec, tile, tile, _const_spec((1, d)), wspec, wspec],
        out_specs=[tile, pl.BlockSpec((SUBLANES, d), lambda i: (0, 0))],
        out_shape=[SDS((s, d), F32), SDS((SUBLANES, d), F32)],
        scratch_shapes=[], args=(da1, da3, dh, hres1, g_ffn, w1, w3), sem=("arbitrary",))


def _ffn_wgrad2(h2, dhb, ff, da1, da3):
    s, d = h2.shape
    _, _, fc = ff.shape
    tm = min(TM_WGRAD, s)

    def body(h2_ref, dhb_ref, ff_ref, da1_ref, da3_ref, dw1_ref, dw3_ref, dw2_ref):
        @pl.when(pl.program_id(1) == 0)
        def _():
            dw1_ref[...] = jnp.zeros_like(dw1_ref)
            dw3_ref[...] = jnp.zeros_like(dw3_ref)
            dw2_ref[...] = jnp.zeros_like(dw2_ref)

        h2v = h2_ref[...]
        dw1_ref[0] += _dot_tn(da1_ref[0], h2v)
        dw3_ref[0] += _dot_tn(da3_ref[0], h2v)
        dw2_ref[0] += _dot_tn(ff_ref[0], dhb_ref[...])

    wspec = pl.BlockSpec((1, fc, d), lambda j, i: (j, 0, 0))
    return pl.pallas_call(
        body, grid=(N_CHIPS, s // tm), name="ffn_wgrad",
        in_specs=[pl.BlockSpec((tm, d), lambda j, i: (i, 0)), pl.BlockSpec((tm, d), lambda j, i: (i, 0))]
        + [pl.BlockSpec((1, tm, fc), lambda j, i: (j, i, 0))] * 3,
        out_specs=[wspec] * 3, out_shape=[SDS((N_CHIPS, fc, d), F32)] * 3,
        compiler_params=_cp(("parallel", "arbitrary")))(h2, dhb, ff, da1, da3)


def _mixer_bwd(u, saved, pooled, h, dhres1, sp_, w_out, hosted=None):
    s, din = u.shape
    d = dhres1.shape[1]
    lw = din // 3
    tm = min(TM_MIX, s)
    nt = s // tm
    nb = lw // GATE_BLOCK
    hd = sp_["gate_a_w"].shape[1]

    def body(ul_ref, saved_ref, pooled_ref, h_ref, hhalo_ref, dhr_ref, cw_ref, cb_ref, ga_ref, gx_ref, ba_ref,
             bx_ref, lam_ref, pw_ref, pb_ref, ps_ref, gl_ref, gp_ref, wout_ref, du_ref, slab_ref,
             gw_s, a_s, b_s, e_s, ecarry_s, dxc_s, q_s, vec_s, cwacc_s, dgw_s, dpw_s):
        i = pl.program_id(0)
        tile = nt - 1 - i

        @pl.when(i == 0)
        def _():
            _build_gate_blocks(ga_ref, gx_ref, gw_s)
            for ref in (ecarry_s, dxc_s, q_s, vec_s, cwacc_s, dgw_s, dpw_s):
                ref[...] = jnp.zeros_like(ref)

        cw = cw_ref[...]
        lam = lam_ref[...]
        ps = ps_ref[...]
        f = {name: saved_ref[k] for k, name in enumerate(MIX_SAVED)}
        f["mult"] = jnp.sqrt(jnp.maximum(f["m2raw"], 1e-12))
        f["sp"] = _softplus_neg(lam)
        f["xcb"] = f["xc"].astype(BF16)
        pooled = pooled_ref[...]
        row = lax.broadcasted_iota(jnp.int32, (tm, LANES), 0) + tile * tm
        f["invs"] = [1.0 / jnp.minimum(row + 1, w).astype(F32) for w in POOL_WINDOWS]
        f["z"] = jnp.concatenate(
            [_dot(pooled[:, g * LANES:(g + 1) * LANES], pw_ref[g].astype(BF16))
             for g in range(len(POOL_WINDOWS))], axis=1) + pb_ref[...]
        f["y_pool"] = f["z"] * ps
        u_l = ul_ref[...]
        hv = h_ref[...]
        h_prev = _shift_down(jnp.where(tile > 0, hhalo_ref[...], 0.0), hv, 1)
        y_lru = hv * f["ge"]
        rl = lax.rsqrt(_rowmean(y_lru * y_lru) + EPS)
        yp = f["y_pool"]
        rp = lax.rsqrt(_rowmean(yp * yp) + EPS)
        xh_l = y_lru * rl
        xh_p = yp * rp

        dyn = _dot_nt(dhr_ref[...].astype(BF16), wout_ref[...])
        d_nl, d_np = dyn[:, :lw], dyn[:, lw:]
        vec = {}
        vec[ROW_GL] = _colsum8(d_nl * xh_l)
        vec[ROW_GP] = _colsum8(d_np * xh_p)
        d_ylru = _rms_bwd(d_nl, xh_l, rl, gl_ref[...])
        d_ypool = _rms_bwd(d_np, xh_p, rp, gp_ref[...])

        vec[ROW_PS] = _colsum8(d_ypool * f["z"])
        dz = d_ypool * ps
        vec[ROW_PB] = _colsum8(dz)
        dzb = dz.astype(BF16)
        dup = []
        for gi, w in enumerate(POOL_WINDOWS):
            sl = slice(gi * LANES, (gi + 1) * LANES)
            dpw_s[:, sl] += _dot_tn(pooled[:, sl], dzb[:, sl])
            dpool = _dot_nt(dzb[:, sl], pw_ref[gi].astype(BF16))
            q = dpool * f["invs"][gi]
            e = jnp.concatenate([q, q_s[:, sl]], axis=0)
            k = 1
            while k < w:
                e = e + pltpu.roll(e, tm + HALO - k, 0)
                k *= 2
            dup.append(e[:tm] - dpool)
            q_s[:, sl] = q[:HALO]

        d_hout = d_ylru * f["ge"]
        d_ug = d_ylru * hv * f["dge"]
        a = f["a"]
        a1, b1 = _scan_level1(a, a * d_hout, reverse=True)
        a_s[...] = a1
        b_s[...] = b1
        e_next = ecarry_s[...]
        ecarry_s[...] = _scan_level2(a_s, b_s, e_s, e_next, reverse=True)
        sv = d_hout + _shift_up(e_s[...], e_next, 1)
        d_a = sv * h_prev
        mult, ig, xc, r = f["mult"], f["ig"], f["xc"], f["r"]
        d_mult = sv * (ig * xc)
        d_ig = sv * mult * xc
        d_xc = sv * mult * ig
        d_la = d_a * a + jnp.where(f["m2raw"] > 1e-12, d_mult * (-(a * a) / mult), 0.0)
        d_r = d_la * (-LRU_C * f["sp"])
        vec[ROW_LAM] = _colsum8(d_la * (-LRU_C * r))
        d_pr = d_r * r * (1.0 - r)
        d_pi = d_ig * ig * (1.0 - ig)
        vec[ROW_BA] = _colsum8(d_pr)
        vec[ROW_BX] = _colsum8(d_pi)
        dxc_parts = []
        for b in range(nb):
            sl = slice(b * GATE_BLOCK, (b + 1) * GATE_BLOCK)
            rhs = jnp.concatenate([d_pr[:, sl], d_pi[:, sl]], axis=1).astype(BF16)
            dgw_s[b] += _dot_tn(f["xcb"][:, sl], rhs)
            dxc_parts.append(_dot_nt(rhs, gw_s[b]))
        d_xc = d_xc + jnp.concatenate(dxc_parts, axis=1)
        vec[ROW_CONV_B] = _colsum8(d_xc)
        dxc_next = dxc_s[...]
        d_ul = None
        for k in range(CONV_WIDTH):
            ahead = _shift_up(d_xc, dxc_next, CONV_WIDTH - 1 - k)
            cwacc_s[k * SUBLANES:(k + 1) * SUBLANES, :] += _colsum8(ahead * u_l)
            term = ahead * cw[k:k + 1, :]
            d_ul = term if d_ul is None else d_ul + term
        dxc_s[...] = d_xc[:SUBLANES]
        for row, val in vec.items():
            vec_s[row * SUBLANES:(row + 1) * SUBLANES, :] += val
        du_ref[...] = jnp.concatenate([d_ul, d_ug] + dup, axis=1).astype(BF16)

        @pl.when(i == nt - 1)
        def _():
            rows = []
            for row in range(ROW_GA):
                if row in (ROW_CONV_W, ROW_CONV_W + 1, ROW_CONV_W + 2, ROW_CONV_W + 3):
                    k = row - ROW_CONV_W
                    v = jnp.sum(cwacc_s[k * SUBLANES:(k + 1) * SUBLANES, :], axis=0, keepdims=True)
                elif row <= ROW_GP:
                    v = jnp.sum(vec_s[row * SUBLANES:(row + 1) * SUBLANES, :], axis=0, keepdims=True)
                    if row == ROW_LAM:
                        v = v * (-1.0 / (1.0 + jnp.exp(lam)))
                else:
                    v = jnp.zeros((1, lw), F32)
                rows.append(v)
            slab_ref[0:ROW_GA, :] = jnp.concatenate(rows, axis=0)
            lane = lax.broadcasted_iota(jnp.int32, (hd, GATE_BLOCK), 1)
            for b in range(nb):
                for off, row0 in ((0, ROW_GA), (GATE_BLOCK, ROW_GX)):
                    acc = jnp.zeros((hd, GATE_BLOCK), F32)
                    for hh in range(GATE_BLOCK // hd):
                        m = (lane >= hh * hd) & (lane < (hh + 1) * hd)
                        acc = acc + jnp.where(m, dgw_s[b, hh * hd:(hh + 1) * hd, off:off + GATE_BLOCK], 0.0)
                    slab_ref[row0:row0 + hd, b * GATE_BLOCK:(b + 1) * GATE_BLOCK] = acc
            slab_ref[ROW_PW:ROW_PW + LANES, :] = dpw_s[...]

    small = [sp_[k] for k in ("conv_w", "conv_b", "gate_a_w", "gate_x_w", "gate_a_b", "gate_x_b", "lru_lambda",
                              "pool_w", "pool_b", "pool_scale", "norm_lru_g", "norm_pool_g")]
    rev = lambda i: nt - 1 - i

    def stages():
        i = pl.program_id(0)
        return i == 0, i == max(nt - 3, 0), i == nt - 1

    return _call(
        body, hosted, stages, grid=(nt,), name="mixer_bwd",
        in_specs=[pl.BlockSpec((tm, lw), lambda i: (rev(i), 0)),
                  pl.BlockSpec((len(MIX_SAVED), tm, lw), lambda i: (0, rev(i), 0)),
                  pl.BlockSpec((tm, lw), lambda i: (rev(i), 0)),
                  pl.BlockSpec((tm, lw), lambda i: (rev(i), 0)),
                  pl.BlockSpec((SUBLANES, lw), lambda i: (jnp.maximum(rev(i) * (tm // SUBLANES) - 1, 0), 0)),
                  pl.BlockSpec((tm, d), lambda i: (rev(i), 0))]
        + [_const_spec(a.shape) for a in small] + [_const_spec(w_out.shape)],
        out_specs=[pl.BlockSpec((tm, din), lambda i: (rev(i), 0)),
                   pl.BlockSpec((MIX_SLAB_ROWS, SLAB_W), lambda i: (0, 0))],
        out_shape=[SDS((s, din), BF16), SDS((MIX_SLAB_ROWS, SLAB_W), F32)],
        scratch_shapes=[pltpu.VMEM((nb, GATE_BLOCK, 2 * GATE_BLOCK), BF16),
                        pltpu.VMEM((tm, lw), F32), pltpu.VMEM((tm, lw), F32), pltpu.VMEM((tm, lw), F32),
                        pltpu.VMEM((SUBLANES, lw), F32), pltpu.VMEM((SUBLANES, lw), F32),
                        pltpu.VMEM((HALO, lw), F32), pltpu.VMEM((ROW_GA * SUBLANES, lw), F32),
                        pltpu.VMEM((CONV_WIDTH * SUBLANES, lw), F32),
                        pltpu.VMEM((nb, GATE_BLOCK, 2 * GATE_BLOCK), F32), pltpu.VMEM((LANES, lw), F32)],
        args=(u, saved, pooled, h, h, dhres1, *small, w_out), sem=("arbitrary",))


def _inproj_bwd(x, du, dhres1, yn, g_mix, w_in, hosted=None):
    s, d = x.shape
    n = w_in.shape[1]
    nc = n // N_CHIPS
    tm = min(TM_PROJ, s)
    nt = s // tm

    def body(x_ref, du_ref, dhr_ref, yn_ref, g_ref, w_ref, gx_ref, dwin_ref, dwout_ref, dg_ref):
        i = pl.program_id(0)

        @pl.when(i == 0)
        def _():
            dwin_ref[...] = jnp.zeros_like(dwin_ref)
            dwout_ref[...] = jnp.zeros_like(dwout_ref)
            dg_ref[...] = jnp.zeros_like(dg_ref)

        xv = x_ref[...]
        g = g_ref[...]
        r = lax.rsqrt(_rowmean(xv * xv) + EPS)
        xh = xv * r
        h1 = (xh * g).astype(BF16)
        duv = du_ref[...]
        dh1 = _dot_nt(duv, w_ref[...])
        dg_ref[...] += _colsum8(dh1 * xh)
        dhr = dhr_ref[...]
        gx_ref[...] = dhr + _rms_bwd(dh1, xh, r, g)
        for jj in range(N_CHIPS):
            dwin_ref[jj] += _dot_tn(h1, duv[:, jj * nc:(jj + 1) * nc])
        dwout_ref[...] += _dot_tn(yn_ref[...], dhr.astype(BF16))

    def stages():
        i = pl.program_id(0)
        return i == 0, i == max(nt - 3, 0), i == nt - 1

    return _call(
        body, hosted, stages, grid=(nt,), name="inproj_bwd",
        in_specs=[pl.BlockSpec((tm, d), lambda i: (i, 0)), pl.BlockSpec((tm, n), lambda i: (i, 0)),
                  pl.BlockSpec((tm, d), lambda i: (i, 0)), pl.BlockSpec((tm, d), lambda i: (i, 0)),
                  _const_spec((1, d)), _const_spec((d, n))],
        out_specs=[pl.BlockSpec((tm, d), lambda i: (i, 0)), pl.BlockSpec((N_CHIPS, d, nc), lambda i: (0, 0, 0)),
                   pl.BlockSpec((d, d), lambda i: (0, 0)), pl.BlockSpec((SUBLANES, d), lambda i: (0, 0))],
        out_shape=[SDS((s, d), F32), SDS((N_CHIPS, d, nc), F32), SDS((d, d), F32), SDS((SUBLANES, d), F32)],
        scratch_shapes=[], args=(x, du, dhres1, yn, g_mix, w_in), sem=("arbitrary",))


def _place():
    x, y, c = lax.axis_index("x"), lax.axis_index("y"), lax.axis_index("c")
    return x, y, c


def _other_chips(x, y):
    return [(1 - x, y), (x, 1 - y), (1 - x, 1 - y)]


ANY = pl.BlockSpec(memory_space=pl.ANY)
VMEM_SPEC = pl.BlockSpec(memory_space=pltpu.VMEM)

_GATHERED = {"w_in": "cols", "w_out": "major", "ffn_w1": "major", "ffn_w3": "major", "ffn_w2": "major"}
_BIG = ("w_in", "w_out", "ffn_w1", "ffn_w3", "ffn_w2")


def _gather_weights(shards, conv_w, n_remote):
    n = len(shards)
    full_shapes = []
    for name, sh in zip(_BIG, shards):
        r, cdim = sh.shape
        if _GATHERED[name] == "cols":
            assert cdim % LANES == 0
            full_shapes.append((r, cdim * N_CHIPS))
        else:
            full_shapes.append((N_CHIPS, r, cdim))

    def region(ref, name, sh, jj, cc):
        r, cdim = sh
        rows = pl.ds(0, r) if cc is None else pl.ds(pl.multiple_of(cc * (r // 2), 16), r // 2)
        if _GATHERED[name] == "cols":
            return ref.at[rows, pl.ds(pl.multiple_of(jj * cdim, LANES), cdim)]
        return ref.at[jj, rows, :]

    def staged(ref, sh, cc):
        r = sh[0]
        return ref.at[pl.ds(pl.multiple_of(cc * (r // 2), 16), r // 2), :]

    def body(*refs):
        ins, cw_in = refs[:n], refs[n]
        outs, cw_out = refs[n + 1:2 * n + 1], refs[2 * n + 1]
        stage = refs[2 * n + 2:3 * n + 2]
        cw_stage, lsem, ssem, rsem, fssem, frsem, cssem, crsem = refs[3 * n + 2:]
        x, y, c = _place()
        j = 2 * x + y
        chips = _other_chips(x, y)
        for w in range(n):
            stage[w][...] = ins[w][...].astype(BF16)
        cw_stage[...] = jnp.zeros_like(cw_stage)
        cw_stage[0:CONV_WIDTH, :] = cw_in[...]
        shs = [s_.shape for s_ in shards]
        local = [pltpu.make_async_copy(stage[w], region(outs[w], _BIG[w], shs[w], j, None), lsem.at[w])
                 for w in range(n)]
        local.append(pltpu.make_async_copy(cw_stage, cw_out.at[j], lsem.at[n]))
        for cp in local:
            cp.start()
        sends = []
        for k, (px, py) in enumerate(chips):
            for w in range(n_remote):
                sends.append(pltpu.make_async_remote_copy(
                    src_ref=staged(stage[w], shs[w], c), dst_ref=region(outs[w], _BIG[w], shs[w], j, c),
                    send_sem=ssem.at[k * n + w], recv_sem=rsem.at[k * n + w], device_id=(px, py, c),
                    device_id_type=MESH))
            sends.append(pltpu.make_async_remote_copy(
                src_ref=cw_stage, dst_ref=cw_out.at[j], send_sem=cssem.at[k], recv_sem=crsem.at[k],
                device_id=(px, py, c), device_id_type=MESH))
        for cp in sends:
            cp.start()
        fwd = []
        for k, (px, py) in enumerate(chips):
            jk = 2 * px + py
            for w in range(n_remote):
                reg = region(outs[w], _BIG[w], shs[w], jk, c)
                pltpu.make_async_remote_copy(src_ref=reg, dst_ref=reg, send_sem=ssem.at[k * n + w],
                                             recv_sem=rsem.at[k * n + w], device_id=(px, py, c),
                                             device_id_type=MESH).wait_recv()
                cp = pltpu.make_async_remote_copy(src_ref=reg, dst_ref=reg, send_sem=fssem.at[k * n + w],
                                                  recv_sem=frsem.at[k * n + w], device_id=(x, y, 1 - c),
                                                  device_id_type=MESH)
                cp.start()
                fwd.append(cp)
            pltpu.make_async_remote_copy(src_ref=cw_stage, dst_ref=cw_out.at[jk], send_sem=cssem.at[k],
                                         recv_sem=crsem.at[k], device_id=(px, py, c),
                                         device_id_type=MESH).wait_recv()
        for k, (px, py) in enumerate(chips):
            jk = 2 * px + py
            for w in range(n_remote):
                reg = region(outs[w], _BIG[w], shs[w], jk, 1 - c)
                pltpu.make_async_remote_copy(src_ref=reg, dst_ref=reg, send_sem=fssem.at[k * n + w],
                                             recv_sem=frsem.at[k * n + w], device_id=(x, y, 1 - c),
                                             device_id_type=MESH).wait_recv()
        for cp in sends + fwd:
            cp.wait_send()
        for cp in local:
            cp.wait()

    nsem = 3 * n
    return pl.pallas_call(
        body, name="gather_first",
        in_specs=[VMEM_SPEC] * (n + 1), out_specs=[ANY] * (n + 1),
        out_shape=[SDS(fs, BF16) for fs in full_shapes] + [SDS((N_CHIPS, SUBLANES, LANES), F32)],
        scratch_shapes=[pltpu.VMEM(s_.shape, BF16) for s_ in shards] + [pltpu.VMEM((SUBLANES, LANES), F32)]
        + [pltpu.SemaphoreType.DMA((n + 1,))] + [pltpu.SemaphoreType.DMA((nsem,))] * 4
        + [pltpu.SemaphoreType.DMA((3,))] * 2,
        compiler_params=_cp())(*shards, conv_w)


def _start_all(make):
    def f(ins, outs, sems):
        for cp in make(ins, outs, sems):
            cp.start()
    return f


def _wait_all(make):
    def f(ins, outs, sems):
        for cp in make(ins, outs, sems):
            cp.wait()
    return f


def _ffn_gather_hosted(arrs):
    n = len(arrs)

    def make(outs, sems):
        ssem, rsem, fs, fr = sems
        x, y, c = _place()
        j = 2 * x + y

        def reg(w, jj, cc):
            hr = arrs[w].shape[1] // 2
            return outs[w].at[jj, pl.ds(pl.multiple_of(cc * hr, 16), hr), :]

        def rc(w, jj, cc, s_sem, r_sem, dev):
            return pltpu.make_async_remote_copy(src_ref=reg(w, jj, cc), dst_ref=reg(w, jj, cc), send_sem=s_sem,
                                                recv_sem=r_sem, device_id=dev, device_id_type=MESH)

        sends, recvs, fwds, frecvs = [], [], [], []
        for k, (px, py) in enumerate(_other_chips(x, y)):
            jk = 2 * px + py
            for w in range(n):
                q = k * n + w
                sends.append(rc(w, j, c, ssem.at[q], rsem.at[q], (px, py, c)))
                recvs.append(rc(w, jk, c, ssem.at[q], rsem.at[q], (px, py, c)))
                fwds.append(rc(w, jk, c, fs.at[q], fr.at[q], (x, y, 1 - c)))
                frecvs.append(rc(w, jk, 1 - c, fs.at[q], fr.at[q], (x, y, 1 - c)))
        return sends, recvs, fwds, frecvs

    def start(ins, outs, sems):
        for cp in make(outs, sems)[0]:
            cp.start()

    def mid(ins, outs, sems):
        _, recvs, fwds, _ = make(outs, sems)
        for r, f in zip(recvs, fwds):
            r.wait_recv()
            f.start()

    def finish(ins, outs, sems):
        sends, _, fwds, frecvs = make(outs, sems)
        for r in frecvs:
            r.wait_recv()
        for cp in sends + fwds:
            cp.wait_send()

    return _Hosted(arrs, [SDS(a.shape, a.dtype) for a in arrs], [3 * n] * 4, start, finish, mid=mid,
                   aliases={w: w for w in range(n)})


def _rs_sibling_hosted(arrs):
    n = len(arrs)

    def make(ins, outs, sems):
        x, y, c = _place()
        cps = []
        for w in range(n):
            hr = arrs[w].shape[1] // 2
            src = ins[w].at[:, pl.ds(pl.multiple_of((1 - c) * hr, SUBLANES), hr), :]
            cps.append(pltpu.make_async_remote_copy(src_ref=src, dst_ref=outs[w], send_sem=sems[0].at[w],
                                                    recv_sem=sems[1].at[w], device_id=(x, y, 1 - c),
                                                    device_id_type=MESH))
        return cps

    return _Hosted(arrs, [SDS((a.shape[0], a.shape[1] // 2, a.shape[2]), F32) for a in arrs], [n, n],
                   _start_all(make), _wait_all(make))


def _rs_chips_hosted(parts):
    n = len(parts)

    def make(ins, outs, sems):
        x, y, c = _place()
        j = 2 * x + y
        cps = []
        for k, (px, py) in enumerate(_other_chips(x, y)):
            jk = 2 * px + py
            for w in range(n):
                cps.append(pltpu.make_async_remote_copy(
                    src_ref=ins[w].at[jk], dst_ref=outs[w].at[j], send_sem=sems[0].at[k * n + w],
                    recv_sem=sems[1].at[k * n + w], device_id=(px, py, c), device_id_type=MESH))
        return cps

    return _Hosted(parts, [SDS(p.shape, p.dtype) for p in parts], [3 * n, 3 * n], _start_all(make), _wait_all(make))


def _rs_swap_hosted(halves):
    n = len(halves)

    def make(ins, outs, sems):
        x, y, c = _place()
        return [pltpu.make_async_remote_copy(src_ref=ins[w], dst_ref=outs[w], send_sem=sems[0].at[w],
                                             recv_sem=sems[1].at[w], device_id=(x, y, 1 - c), device_id_type=MESH)
                for w in range(n)]

    return _Hosted(halves, [SDS(h.shape, F32) for h in halves], [n, n], _start_all(make), _wait_all(make))


HBM_SPEC = pl.BlockSpec(memory_space=pltpu.HBM)
SEM_SPEC = pl.BlockSpec(memory_space=pltpu.SEMAPHORE)
_EFFECT = pltpu.SideEffectType.DATAFLOW_SIDE_EFFECTING


def _split_start(h, name):
    n_in, n_out, ns = len(h.ins), len(h.out_shapes), len(h.sems)
    ins = [pltpu.with_memory_space_constraint(a, pltpu.HBM) for a in h.ins]
    lands = [pltpu.with_memory_space_constraint(lax.empty(o.shape, o.dtype), pltpu.HBM) for o in h.out_shapes]

    def body(*refs):
        i_refs, l_refs = refs[:n_in], refs[n_in:n_in + n_out]
        s_refs = refs[n_in + n_out:n_in + n_out + ns]
        token = refs[-1]
        h.start(i_refs, l_refs, s_refs)
        token[...] = jnp.zeros_like(token)

    res = pl.pallas_call(
        body, name=name, in_specs=[HBM_SPEC] * (n_in + n_out),
        out_specs=[SEM_SPEC] * ns + [HBM_SPEC] * (n_in + n_out) + [VMEM_SPEC],
        out_shape=[pltpu.SemaphoreType.DMA((k,)) for k in h.sems]
        + [pltpu.HBM(a.shape, a.dtype) for a in h.ins] + [pltpu.HBM(o.shape, o.dtype) for o in h.out_shapes]
        + [SDS((SUBLANES, LANES), F32)],
        input_output_aliases={k: ns + k for k in range(n_in + n_out)},
        compiler_params=pltpu.CompilerParams(has_side_effects=_EFFECT))(*ins, *lands)
    return list(res[:-1]), res[-1]


def _split_wait(h, state, after, name):
    n_in, n_out, ns = len(h.ins), len(h.out_shapes), len(h.sems)
    sems, bufs = state[:ns], state[ns:]

    def body(*refs):
        i_refs, l_refs = refs[:n_in], refs[n_in:n_in + n_out]
        s_refs = refs[n_in + n_out:n_in + n_out + ns]
        h.finish(i_refs, l_refs, s_refs)

    res = pl.pallas_call(
        body, name=name, in_specs=[HBM_SPEC] * (n_in + n_out) + [SEM_SPEC] * ns + [ANY],
        out_specs=[HBM_SPEC] * (n_in + n_out),
        out_shape=[pltpu.HBM(b.shape, b.dtype) for b in bufs],
        input_output_aliases={k: k for k in range(n_in + n_out)},
        compiler_params=pltpu.CompilerParams(has_side_effects=_EFFECT))(*bufs, *sems, after)
    return list(res[n_in:])


def _run_comm(hosted, name):
    return _call(lambda: None, hosted, None, name=name, grid=(), in_specs=[], out_specs=[], out_shape=[],
                 scratch_shapes=[], args=(), sem=None)[1]


def _row_tile(rows, cols, n_arrays):
    budget = 24 * 1024 * 1024 // (2 * 4 * n_arrays * cols)
    best = SUBLANES
    for t in range(SUBLANES, rows + 1, SUBLANES):
        if rows % t == 0 and t <= budget:
            best = t
    return best


def _place_index(which):
    x, y, c = _place()
    v = c if which == "c" else 2 * x + y
    return jnp.reshape(v, (1,)).astype(jnp.int32)


def _add_own_half(full, recv, name, wire=BF16):
    nsh, rows, cols = full.shape
    hr = rows // 2
    t = _row_tile(hr, cols, 4)
    nt = hr // t

    def body(c_ref, a_ref, b_ref, o_ref, ob_ref):
        v = a_ref[...] + b_ref[...]
        o_ref[...] = v
        ob_ref[...] = v.astype(wire)

    half = pl.BlockSpec((1, t, cols), lambda s_, i, c_ref: (s_, i, 0))
    return pl.pallas_call(
        body, name=name,
        grid_spec=pltpu.PrefetchScalarGridSpec(
            num_scalar_prefetch=1, grid=(nsh, nt),
            in_specs=[pl.BlockSpec((1, t, cols), lambda s_, i, c_ref: (s_, c_ref[0] * nt + i, 0)), half],
            out_specs=[half, half]),
        out_shape=[SDS((nsh, hr, cols), F32), SDS((nsh, hr, cols), wire)],
        compiler_params=_cp(("parallel", "parallel")))(_place_index("c"), full, recv)


def _sum_chips(own, recv, name):
    nsh, hr, cols = own.shape
    t = _row_tile(hr, cols, 6)

    def body(j_ref, own_ref, *rest):
        r_refs, o_ref = rest[:nsh], rest[nsh]
        j = j_ref[0]
        mine = own_ref[0]
        parts = [jnp.where(j == k, mine, r_refs[k][0].astype(F32)) for k in range(nsh)]
        o_ref[...] = ((parts[0] + parts[1]) + parts[2]) + parts[3]

    def other(k):
        return pl.BlockSpec((1, t, cols), lambda i, j_ref: (jnp.where(j_ref[0] == k, (k + 1) % nsh, k), i, 0))

    return pl.pallas_call(
        body, name=name,
        grid_spec=pltpu.PrefetchScalarGridSpec(
            num_scalar_prefetch=1, grid=(hr // t,),
            in_specs=[pl.BlockSpec((1, t, cols), lambda i, j_ref: (j_ref[0], i, 0))]
            + [other(k) for k in range(nsh)],
            out_specs=pl.BlockSpec((t, cols), lambda i, j_ref: (i, 0))),
        out_shape=SDS((hr, cols), F32), compiler_params=_cp(("parallel",)))(_place_index("j"), own, *([recv] * nsh))


def _adamw_math(w, g, m, v):
    m = ADAM_B1 * m + (1.0 - ADAM_B1) * g
    v = ADAM_B2 * v + (1.0 - ADAM_B2) * (g * g)
    m_hat = m / (1.0 - ADAM_B1 ** ADAM_STEP)
    v_hat = v / (1.0 - ADAM_B2 ** ADAM_STEP)
    delta = -ADAM_LR * (m_hat / (jnp.sqrt(v_hat) + ADAM_EPS) + ADAM_WD * w)
    return delta, m, v


def _adamw_big(w, g_own, g_sib, m, v, name, token=None):
    _, rows, cols = w.shape
    hr = rows // 2
    t = _row_tile(hr, cols, 9)
    nth = hr // t
    if token is None:
        token = jnp.zeros((SUBLANES, LANES), F32)

    def body(c_ref, w_ref, go_ref, gs_ref, m_ref, v_ref, tok_ref, g_ref, d_ref, mo_ref, vo_ref):
        own = (pl.program_id(0) // nth) == c_ref[0]
        g = jnp.where(own, go_ref[...], gs_ref[...]) + tok_ref[0:1, 0:1]
        g_ref[0] = g
        d_ref[0], mo_ref[0], vo_ref[0] = _adamw_math(w_ref[0], g, m_ref[0], v_ref[0])

    spec = pl.BlockSpec((1, t, cols), lambda i, c_ref: (0, i, 0))
    hspec = pl.BlockSpec((t, cols), lambda i, c_ref: (i % nth, 0))
    tspec = pl.BlockSpec((SUBLANES, LANES), lambda i, c_ref: (0, 0))
    return pl.pallas_call(
        body, name=name,
        grid_spec=pltpu.PrefetchScalarGridSpec(
            num_scalar_prefetch=1, grid=(2 * nth,), in_specs=[spec, hspec, hspec, spec, spec, tspec],
            out_specs=[spec] * 4),
        out_shape=[SDS((1, rows, cols), F32)] * 4,
        compiler_params=_cp(("parallel",)))(_place_index("c"), w, g_own, g_sib, m, v, token)


def _build_slab(mix_slab, dg_mix, dg_ffn, dg_fin, loss8):
    def body(ms_ref, gm_ref, gf_ref, gn_ref, loss_ref, out_ref):
        rows = []
        for ref in (gm_ref, gf_ref, gn_ref):
            v = jnp.sum(ref[...], axis=0, keepdims=True)
            rows += [v[:, :SLAB_W], v[:, SLAB_W:]]
        rows.append(jnp.concatenate([loss_ref[0:1, :]] * (SLAB_W // LANES), axis=1))
        rows.append(jnp.zeros((SLAB_ROWS - ROW_LOSS - 1, SLAB_W), F32))
        tail = jnp.concatenate(rows, axis=0)
        for k in range(N_CHIPS):
            out_ref[k, 0:MIX_SLAB_ROWS, :] = ms_ref[...]
            out_ref[k, MIX_SLAB_ROWS:SLAB_ROWS, :] = tail

    return pl.pallas_call(
        body, name="build_slab", in_specs=[VMEM_SPEC] * 5, out_specs=VMEM_SPEC,
        out_shape=SDS((N_CHIPS, SLAB_ROWS, SLAB_W), F32),
        compiler_params=_cp())(mix_slab, dg_mix, dg_ffn, dg_fin, loss8)


_SMALL_ROWS = (("conv_b", ROW_CONV_B), ("gate_a_b", ROW_BA), ("gate_x_b", ROW_BX), ("lru_lambda", ROW_LAM),
               ("pool_b", ROW_PB), ("pool_scale", ROW_PS), ("norm_lru_g", ROW_GL), ("norm_pool_g", ROW_GP))
_WIDE_ROWS = (("norm_mix_g", ROW_MIX), ("norm_ffn_g", ROW_FFN), ("final_norm_g", ROW_FIN))
_BLOCK_ROWS = (("gate_a_w", ROW_GA), ("gate_x_w", ROW_GX), ("pool_w", ROW_PW))
_SMALL_ORDER = tuple(n for n, _ in _SMALL_ROWS) + tuple(n for n, _ in _WIDE_ROWS) + tuple(
    n for n, _ in _BLOCK_ROWS) + ("conv_w",)


def _adamw_small(slab_own, slab_sib, wmv):
    names = _SMALL_ORDER
    flat = [a for nme in names for a in wmv[nme]]
    nin = len(flat)

    def body(*refs):
        own_ref, sib_ref, j_ref = refs[0], refs[1], refs[2]
        ins = refs[3:3 + nin]
        outs = refs[3 + nin:-1]
        first = j_ref[1] == 0
        slab_ref = jnp.concatenate([jnp.where(first, own_ref[...], sib_ref[...]),
                                    jnp.where(first, sib_ref[...], own_ref[...])], axis=0)
        refs[-1][...] = jnp.broadcast_to(slab_ref[ROW_LOSS:ROW_LOSS + 1, 0:LANES], (SUBLANES, LANES))
        grads = {}
        for nme, row in _SMALL_ROWS:
            grads[nme] = slab_ref[row:row + 1, :]
        for nme, row in _WIDE_ROWS:
            grads[nme] = jnp.concatenate([slab_ref[row:row + 1, :], slab_ref[row + 1:row + 2, :]], axis=1)
        full = slab_ref[ROW_CONV_W:ROW_CONV_W + CONV_WIDTH, :]
        jv = j_ref[0]
        g = jnp.zeros((CONV_WIDTH, LANES), F32)
        for jj in range(N_CHIPS):
            g = jnp.where(jv == jj, full[:, jj * LANES:(jj + 1) * LANES], g)
        grads["conv_w"] = g
        block_rows = dict(_BLOCK_ROWS)
        for idx, nme in enumerate(names):
            w_ref, m_ref, v_ref = ins[3 * idx:3 * idx + 3]
            if nme in block_rows:
                nblk, r, c = w_ref.shape
                parts = [(b, slab_ref[block_rows[nme]:block_rows[nme] + r, b * c:(b + 1) * c]) for b in range(nblk)]
            else:
                parts = [(Ellipsis, grads[nme])]
            for b, g in parts:
                delta, m, v = _adamw_math(w_ref[b], g, m_ref[b], v_ref[b])
                outs[4 * idx][b] = g
                outs[4 * idx + 1][b] = delta
                outs[4 * idx + 2][b] = m
                outs[4 * idx + 3][b] = v

    place = jnp.concatenate([_place_index("j"), _place_index("c")])
    out_shape = [SDS(wmv[nme][0].shape, F32) for nme in names for _ in range(4)] + [SDS((SUBLANES, LANES), F32)]
    res = pl.pallas_call(
        body, name="adamw_small",
        in_specs=[VMEM_SPEC, VMEM_SPEC, pl.BlockSpec(memory_space=pltpu.SMEM)] + [VMEM_SPEC] * nin,
        out_specs=[VMEM_SPEC] * len(out_shape), out_shape=out_shape,
        compiler_params=_cp())(slab_own, slab_sib, place, *flat)
    return {nme: tuple(res[4 * idx:4 * idx + 4]) for idx, nme in enumerate(names)}, res[-1]


_FFN = ("ffn_w1", "ffn_w3", "ffn_w2")
_TRANSPOSED = ("ffn_w1", "ffn_w3")


def _local_step(x, target, full, sp_, distributed):
    d = x.shape[1]
    (u,), got = _inproj(x, sp_["norm_mix_g"], full["w_in"],
                        [_ffn_gather_hosted([full["w_out"]])] if distributed else None)
    w_out = (got[0][0] if distributed else full["w_out"]).reshape(d, d)
    gather = [_ffn_gather_hosted([full[n] for n in _FFN])] if distributed else None
    (h, yn, hres1, saved, pooled), got = _mixer_fwd(u, x, sp_, w_out, gather)
    w1, w3, w2 = got[0] if distributed else [full[n] for n in _FFN]
    h2, a1, a3, ff = _ffn_up(hres1, sp_["norm_ffn_g"], w1, w3)
    dh, dhb, loss8, dg_fin = _ffn_down(ff, hres1, target, sp_["final_norm_g"], w2)
    da1, da3 = _ffn_bwd_gate(dhb, a1, a3, w2)
    dws = list(_ffn_wgrad2(h2, dhb, ff, da1, da3))
    rs1 = [_rs_sibling_hosted(dws)] if distributed else None
    (dhres1, dg_ffn), got = _ffn_bwd_down(da1, da3, dh, hres1, sp_["norm_ffn_g"], w1, w3, rs1)
    rs2 = None
    if distributed:
        pairs = [_add_own_half(a, r, "add_half_" + n) for n, a, r in zip(_FFN, dws, got[0])]
        rs2 = [_rs_chips_hosted([pb for _, pb in pairs])]
    (du, mix_slab), got = _mixer_bwd(u, saved, pooled, h, dhres1, sp_, w_out, rs2)
    g_mix = sp_["norm_mix_g"]
    if distributed:
        fin = [_sum_chips(pairs[k][0], got[0][k], "sum_chips_" + n) for k, n in enumerate(_FFN)]
        swap = _rs_swap_hosted(fin)
        state, token = _split_start(swap, "ffn_swap_start")
        g_mix = g_mix + token[0:1, 0:1]
    (gx, dwin, dwout, dg_mix), _ = _inproj_bwd(x, du, dhres1, yn, g_mix, full["w_in"])
    if distributed:
        sib = _split_wait(swap, state, dg_mix, "ffn_swap_wait")
    big = {"w_in": dwin, "w_out": dwout.reshape(N_CHIPS, d // N_CHIPS, d)}
    for k, n in enumerate(_FFN):
        big[n] = (fin[k], sib[k]) if distributed else dws[k]
    return gx, big, (mix_slab, dg_mix, dg_ffn, dg_fin, loss8)


def _to_compact(w):
    h, i, j = w.shape
    return jnp.transpose(w, (1, 0, 2)).reshape(i, h * j)


def _from_compact(w, h):
    i, hj = w.shape
    return jnp.transpose(w.reshape(i, h, hj // h), (1, 0, 2))


_SMALL_LAYOUT = {
    "gate_a_w": (lambda a: a[0], lambda a: a[None]),
    "gate_x_w": (lambda a: a[0], lambda a: a[None]),
    "pool_w": (lambda a: a[0], lambda a: a[None]),
    "conv_w": (lambda a: a[0], lambda a: a[None]),
    "final_norm_g": (lambda a: a[None], lambda a: a[0]),
}

_WEIGHTS = ("norm_mix_g", "w_in", "conv_w", "conv_b", "gate_a_w", "gate_a_b", "gate_x_w", "gate_x_b", "lru_lambda",
            "pool_w", "pool_b", "pool_scale", "norm_lru_g", "norm_pool_g", "w_out", "norm_ffn_g", "ffn_w1",
            "ffn_w3", "ffn_w2", "final_norm_g")


def kernel(x, norm_mix_g, w_in, conv_w, conv_b, gate_a_w, gate_a_b, gate_x_w, gate_x_b, lru_lambda, pool_w, pool_b, pool_scale, norm_lru_g, norm_pool_g, w_out, norm_ffn_g, ffn_w1, ffn_w3, ffn_w2, final_norm_g, loss_target, m_norm_mix_g, m_w_in, m_conv_w, m_conv_b, m_gate_a_w, m_gate_a_b, m_gate_x_w, m_gate_x_b, m_lru_lambda, m_pool_w, m_pool_b, m_pool_scale, m_norm_lru_g, m_norm_pool_g, m_w_out, m_norm_ffn_g, m_ffn_w1, m_ffn_w3, m_ffn_w2, m_final_norm_g, v_norm_mix_g, v_w_in, v_conv_w, v_conv_b, v_gate_a_w, v_gate_a_b, v_gate_x_w, v_gate_x_b, v_lru_lambda, v_pool_w, v_pool_b, v_pool_scale, v_norm_lru_g, v_norm_pool_g, v_w_out, v_norm_ffn_g, v_ffn_w1, v_ffn_w3, v_ffn_w2, v_final_norm_g):
    loc = locals()
    w = {n: loc[n] for n in _WEIGHTS}
    m = {n: loc["m_" + n] for n in _WEIGHTS}
    v = {n: loc["v_" + n] for n in _WEIGHTS}

    def lay(nme, a):
        return _SMALL_LAYOUT[nme][0](a) if nme in _SMALL_LAYOUT else a

    def unlay(nme, a):
        return _SMALL_LAYOUT[nme][1](a) if nme in _SMALL_LAYOUT else a

    for group in (w, m, v):
        for n in _TRANSPOSED:
            group[n] = jnp.transpose(group[n], (0, 2, 1))

    gathered = _gather_weights([w[n][0] for n in _BIG], w["conv_w"][0], n_remote=1)
    full = dict(zip(_BIG, gathered[:-1]))
    cw_all = gathered[-1]
    sp_ = {n: lay(n, w[n]) for n in _SMALL_ORDER}
    sp_["conv_w"] = jnp.transpose(cw_all[:, :CONV_WIDTH, :], (1, 0, 2)).reshape(CONV_WIDTH, N_CHIPS * LANES)

    gx, big, small = _local_step(x[0], loss_target[0], full, sp_, distributed=True)

    late = ("w_in", "w_out", "slab")
    big["slab"] = _build_slab(*small)
    fin = {n: big[n][0] for n in _FFN}
    sib = {n: big[n][1] for n in _FFN}
    recv1, = _run_comm([_rs_sibling_hosted([big[n] for n in late])], "tail_sibling")
    pairs = [_add_own_half(big[n], r, "add_half_" + n, F32 if n == "slab" else BF16) for n, r in zip(late, recv1)]
    chips = _rs_chips_hosted([pb for _, pb in pairs])
    state, token = _split_start(chips, "tail_chips_start")
    out = {}
    for n in _FFN:
        out[n] = tuple(_adamw_big(w[n], fin[n], sib[n], m[n], v[n], "adamw_" + n, token))
    recv2 = _split_wait(chips, state, out[_FFN[-1]][1], "tail_chips_wait")
    for n, (p, _), r in zip(late, pairs, recv2):
        fin[n] = _sum_chips(p, r, "sum_chips_" + n)
    swapped, = _run_comm([_rs_swap_hosted([fin[n] for n in late])], "tail_swap")
    sib.update(zip(late, swapped))
    for n in late[:2]:
        out[n] = tuple(_adamw_big(w[n], fin[n], sib[n], m[n], v[n], "adamw_" + n))
    for n in _TRANSPOSED:
        out[n] = tuple(jnp.transpose(a, (0, 2, 1)) for a in out[n])
    wmv = {n: (lay(n, w[n]), lay(n, m[n]), lay(n, v[n])) for n in _SMALL_ORDER}
    res, loss = _adamw_small(fin["slab"], sib["slab"], wmv)
    for n in _SMALL_ORDER:
        out[n] = tuple(unlay(n, a) for a in res[n])
    return (loss[0, 0], gx[None]) + tuple(out[n][k] for k in range(4) for n in _WEIGHTS)
```

```python
import functools
import math

import jax
import jax.numpy as jnp
from jax import lax
from jax.experimental import pallas as pl
from jax.experimental.pallas import tpu as pltpu

F32 = jnp.float32
BF16 = jnp.bfloat16
SDS = jax.ShapeDtypeStruct
MESH = pl.DeviceIdType.MESH

EPS = 1e-6
LRU_C = 8.0
CONV_WIDTH = 4
POOL_WINDOWS = (2, 4, 8, 16)
HALO = 16
LANES = 128
SUBLANES = 8
GATE_BLOCK = 256
N_CHIPS = 4

ADAM_LR = 0.001
ADAM_B1 = 0.9
ADAM_B2 = 0.999
ADAM_EPS = 1e-08
ADAM_WD = 0.01
ADAM_STEP = 10

TM_PROJ = 512
TM_MIX = 512
TM_FFN = 512
TM_WGRAD = 1024
MIX_SAVED = ("xc", "r", "ig", "a", "m2raw", "ge", "dge")
FFN_ROW_CHUNKS = 2
VMEM_LIMIT = 56 * 1024 * 1024

SLAB_W = 512
ROW_CONV_B, ROW_CONV_W, ROW_BA, ROW_BX, ROW_LAM, ROW_PB, ROW_PS, ROW_GL, ROW_GP = 0, 1, 5, 6, 7, 8, 9, 10, 11
ROW_GA, ROW_GX, ROW_PW = 16, 80, 144
ROW_MIX, ROW_FFN, ROW_FIN, ROW_LOSS = 272, 274, 276, 278
MIX_SLAB_ROWS = 272
SLAB_ROWS = 288


def _cp(sem=None, **kw):
    if sem is not None:
        kw["dimension_semantics"] = sem
    return pltpu.CompilerParams(vmem_limit_bytes=VMEM_LIMIT, **kw)


def _const_spec(shape):
    nd = len(shape)
    return pl.BlockSpec(shape, lambda *_: (0,) * nd, pipeline_mode=pl.Buffered(1))


def _sigmoid(x):
    return 1.0 / (1.0 + jnp.exp(-x))


def _dot(a, b):
    return jnp.dot(a, b, preferred_element_type=F32)


def _dot_nt(a, b):
    return lax.dot_general(a, b, (((1,), (1,)), ((), ())), preferred_element_type=F32)


def _dot_tn(a, b):
    return lax.dot_general(a, b, (((0,), (0,)), ((), ())), preferred_element_type=F32)


def _colsum8(v):
    m, c = v.shape
    return v.reshape(m // SUBLANES, SUBLANES, c).sum(axis=0)


def _rowmean(v):
    return jnp.mean(v, axis=-1, keepdims=True)


def _rms_bwd(dy, xhat, r, g):
    dxh = dy * g
    return r * (dxh - xhat * _rowmean(dxh * xhat))


def _softplus_neg(lam):
    z = -lam
    e = jnp.exp(-jnp.abs(z))
    u = 1.0 + e
    d = u - 1.0
    log1p = jnp.where(d == 0.0, e, jnp.log(u) * (e / jnp.where(d == 0.0, 1.0, d)))
    return jnp.maximum(z, 0.0) + log1p


def _neg_expm1(z):
    series = -(z * (1.0 + z * (0.5 + z * (1.0 / 6.0 + z * (1.0 / 24.0)))))
    return jnp.where(z > -0.03, series, 1.0 - jnp.exp(z))


_GELU_C = math.sqrt(2.0 / math.pi)
_GELU_K = 0.044715


def _gelu_parts(x):
    x2 = x * x
    th = jnp.tanh(_GELU_C * (x + _GELU_K * x2 * x))
    ge = 0.5 * x * (1.0 + th)
    dge = 0.5 * (1.0 + th) + 0.5 * x * (1.0 - th * th) * (_GELU_C * (1.0 + 3.0 * _GELU_K * x2))
    return ge, dge


def _shift_down(halo, tile, k):
    if k == 0:
        return tile
    ext = jnp.concatenate([halo, tile], axis=0)
    n = tile.shape[0]
    h = halo.shape[0]
    return ext[h - k:h - k + n]


def _shift_up(tile, nxt, k):
    if k == 0:
        return tile
    ext = jnp.concatenate([tile, nxt], axis=0)
    return ext[k:k + tile.shape[0]]


def _build_gate_blocks(ga_ref, gx_ref, gw_ref):
    hd = ga_ref.shape[1]
    per = GATE_BLOCK // hd
    zero = jnp.zeros((hd, hd), F32)
    for b in range(gw_ref.shape[0]):
        for src, off in ((ga_ref, 0), (gx_ref, GATE_BLOCK)):
            for hh in range(per):
                row = jnp.concatenate([zero] * hh + [src[b * per + hh]] + [zero] * (per - 1 - hh), axis=1)
                gw_ref[b, hh * hd:(hh + 1) * hd, off:off + GATE_BLOCK] = row.astype(BF16)


def _scan_level1(a, b, reverse):
    m, c = a.shape
    a3 = a.reshape(m // SUBLANES, SUBLANES, c)
    b3 = b.reshape(m // SUBLANES, SUBLANES, c)
    row = lax.broadcasted_iota(jnp.int32, a3.shape, 1)
    for s in (1, 2, 4):
        sh = (SUBLANES - s) if reverse else s
        a_sh = pltpu.roll(a3, sh, 1)
        b_sh = pltpu.roll(b3, sh, 1)
        ok = (row < SUBLANES - s) if reverse else (row >= s)
        b3 = jnp.where(ok, a3 * b_sh + b3, b3)
        a3 = jnp.where(ok, a3 * a_sh, a3)
    return a3.reshape(m, c), b3.reshape(m, c)


def _scan_level2(a_ref, b_ref, out_ref, carry, reverse):
    m, c = a_ref.shape
    ng = m // SUBLANES

    def step(g, cr):
        gi = (ng - 1 - g) if reverse else g
        off = pl.multiple_of(gi * SUBLANES, SUBLANES)
        h = b_ref[pl.ds(off, SUBLANES), :] + a_ref[pl.ds(off, SUBLANES), :] * cr
        out_ref[pl.ds(off, SUBLANES), :] = h
        edge = h[0:1, :] if reverse else h[SUBLANES - 1:SUBLANES, :]
        return jnp.broadcast_to(edge, (SUBLANES, c))

    return lax.fori_loop(0, ng, step, carry, unroll=4)


def _mixer_recompute(u, hal, t0, cw, cb, gw_ref, ba, bx, lam, pw_ref, pb, ps):
    tm = u.shape[0]
    lw = cb.shape[1]
    u_l, u_g, u_p = u[:, :lw], u[:, lw:2 * lw], u[:, 2 * lw:]
    hal_l, hal_p = hal[:, :lw], hal[:, 2 * lw:]
    taps = [_shift_down(hal_l, u_l, CONV_WIDTH - 1 - k) for k in range(CONV_WIDTH)]
    xc = cb
    for k in range(CONV_WIDTH):
        xc = xc + taps[k] * cw[k:k + 1, :]
    xcb = xc.astype(BF16)
    nb = lw // GATE_BLOCK
    gs = [_dot(xcb[:, b * GATE_BLOCK:(b + 1) * GATE_BLOCK], gw_ref[b]) for b in range(nb)]
    r = _sigmoid(jnp.concatenate([g[:, :GATE_BLOCK] for g in gs], axis=1) + ba)
    ig = _sigmoid(jnp.concatenate([g[:, GATE_BLOCK:] for g in gs], axis=1) + bx)
    sp = _softplus_neg(lam)
    la = (-LRU_C * r) * sp
    a = jnp.exp(la)
    m2raw = _neg_expm1(2.0 * la)
    mult = jnp.sqrt(jnp.maximum(m2raw, 1e-12))
    ge, dge = _gelu_parts(u_g)
    row = lax.broadcasted_iota(jnp.int32, (tm, LANES), 0) + t0
    pooled, invs, zs = [], [], []
    for gi, w in enumerate(POOL_WINDOWS):
        e = jnp.concatenate([hal_p[:, gi * LANES:(gi + 1) * LANES], u_p[:, gi * LANES:(gi + 1) * LANES]], axis=0)
        s = e
        k = 1
        while k < w:
            s = s + pltpu.roll(s, k, 0)
            k *= 2
        inv = 1.0 / jnp.minimum(row + 1, w).astype(F32)
        pg = s[HALO:] * inv - e[HALO:]
        pooled.append(pg)
        invs.append(inv)
        zs.append(_dot(pg.astype(BF16), pw_ref[gi].astype(BF16)))
    z = jnp.concatenate(zs, axis=1) + pb
    y_pool = z * ps
    return dict(u_l=u_l, u_g=u_g, taps=taps, xc=xc, xcb=xcb, r=r, ig=ig, sp=sp, la=la, a=a, m2raw=m2raw,
                mult=mult, ge=ge, dge=dge, pooled=pooled, invs=invs, z=z, y_pool=y_pool)


ANY = pl.BlockSpec(memory_space=pl.ANY)
VMEM_SPEC = pl.BlockSpec(memory_space=pltpu.VMEM)


class _Hosted:
    def __init__(self, ins, out_shapes, sems, start, finish, mid=None, aliases=None):
        self.ins, self.out_shapes, self.sems = list(ins), list(out_shapes), list(sems)
        self.start, self.mid, self.finish = start, mid, finish
        self.aliases = dict(aliases or {})


def _call(body, hosted, stage_preds, *, name, grid, in_specs, out_specs, out_shape, scratch_shapes, args, sem):
    hosted = list(hosted or [])
    n_in, n_out, n_scr = len(in_specs), len(out_specs), len(scratch_shapes)
    c_in = [a for h in hosted for a in h.ins]
    c_out = [o for h in hosted for o in h.out_shapes]
    c_sem = [pltpu.SemaphoreType.DMA((k,)) for h in hosted for k in h.sems]

    def full(*refs):
        p = 0
        parts = []
        for cnt in (n_in, len(c_in), n_out, len(c_out), n_scr, len(c_sem)):
            parts.append(refs[p:p + cnt])
            p += cnt
        hi, ci, ho, co, hs, cs = parts
        per = []
        a = b = c_ = 0
        for h in hosted:
            per.append((h, ci[a:a + len(h.ins)], co[b:b + len(h.out_shapes)], cs[c_:c_ + len(h.sems)]))
            a, b, c_ = a + len(h.ins), b + len(h.out_shapes), c_ + len(h.sems)
        first = mid = last = None
        if hosted and grid:
            first, mid, last = stage_preds()

        def run(fn, pred, i_, o_, s_):
            if fn is None:
                return
            if pred is None:
                fn(i_, o_, s_)
            else:
                pl.when(pred)(functools.partial(fn, i_, o_, s_))

        for h, i_, o_, s_ in per:
            run(h.start, first, i_, o_, s_)
        body(*hi, *ho, *hs)
        for h, i_, o_, s_ in per:
            run(h.mid, mid, i_, o_, s_)
        for h, i_, o_, s_ in per:
            run(h.finish, last, i_, o_, s_)

    aliases = {}
    a = b = 0
    for h in hosted:
        for k, v in h.aliases.items():
            aliases[n_in + a + k] = n_out + b + v
        a, b = a + len(h.ins), b + len(h.out_shapes)
    res = pl.pallas_call(
        full, name=name, grid=grid, in_specs=list(in_specs) + [ANY] * len(c_in),
        out_specs=list(out_specs) + [ANY] * len(c_out), out_shape=list(out_shape) + c_out,
        scratch_shapes=list(scratch_shapes) + c_sem, input_output_aliases=aliases,
        compiler_params=_cp(sem))(*args, *c_in)
    res = list(res)
    outs = []
    p = n_out
    for h in hosted:
        outs.append(res[p:p + len(h.out_shapes)])
        p += len(h.out_shapes)
    return res[:n_out], outs


def _inproj(x, g_mix, w_in, hosted=None):
    s, d = x.shape
    n = w_in.shape[1]
    tm = min(TM_PROJ, s)
    nt = s // tm

    def body(x_ref, g_ref, w_ref, u_ref):
        xv = x_ref[...]
        r = lax.rsqrt(_rowmean(xv * xv) + EPS)
        u_ref[...] = _dot((xv * r * g_ref[...]).astype(BF16), w_ref[...])

    def stages():
        i = pl.program_id(0)
        return i == 0, i == max(nt - 3, 0), i == nt - 1

    return _call(
        body, hosted, stages, grid=(nt,), name="inproj",
        in_specs=[pl.BlockSpec((tm, d), lambda i: (i, 0)), _const_spec((1, d)), _const_spec((d, n))],
        out_specs=[pl.BlockSpec((tm, n), lambda i: (i, 0))], out_shape=[SDS((s, n), F32)], scratch_shapes=[],
        args=(x, g_mix, w_in), sem=("arbitrary",))


def _mixer_fwd(u, x, sp_, w_out, hosted=None):
    s, din = u.shape
    d = x.shape[1]
    lw = din // 3
    tm = min(TM_MIX, s)
    nb = lw // GATE_BLOCK

    def body(u_ref, halo_ref, x_ref, cw_ref, cb_ref, ga_ref, gx_ref, ba_ref, bx_ref, lam_ref, pw_ref, pb_ref,
             ps_ref, gl_ref, gp_ref, wout_ref, h_ref, yn_ref, hres_ref, saved_ref, pooled_ref,
             gw_s, a_s, b_s, carry_s):
        i = pl.program_id(0)

        @pl.when(i == 0)
        def _():
            _build_gate_blocks(ga_ref, gx_ref, gw_s)
            carry_s[...] = jnp.zeros_like(carry_s)

        uv = u_ref[...]
        hal = jnp.where(i > 0, halo_ref[...], 0.0)
        f = _mixer_recompute(uv, hal, i * tm, cw_ref[...], cb_ref[...], gw_s, ba_ref[...], bx_ref[...],
                             lam_ref[...], pw_ref, pb_ref[...], ps_ref[...])
        for k, name in enumerate(MIX_SAVED):
            saved_ref[k] = f[name]
        pooled_ref[...] = jnp.concatenate(f["pooled"], axis=1).astype(BF16)
        bb = f["mult"] * (f["ig"] * f["xc"])
        a1, b1 = _scan_level1(f["a"], bb, reverse=False)
        a_s[...] = a1
        b_s[...] = b1
        carry_s[...] = _scan_level2(a_s, b_s, h_ref, carry_s[...], reverse=False)
        y_lru = h_ref[...] * f["ge"]
        rl = lax.rsqrt(_rowmean(y_lru * y_lru) + EPS)
        yp = f["y_pool"]
        rp = lax.rsqrt(_rowmean(yp * yp) + EPS)
        yn = jnp.concatenate([y_lru * rl * gl_ref[...], yp * rp * gp_ref[...]], axis=1).astype(BF16)
        yn_ref[...] = yn
        hres_ref[...] = x_ref[...] + _dot(yn, wout_ref[...])

    small = [sp_[k] for k in ("conv_w", "conv_b", "gate_a_w", "gate_x_w", "gate_a_b", "gate_x_b", "lru_lambda",
                              "pool_w", "pool_b", "pool_scale", "norm_lru_g", "norm_pool_g")]
    nt = s // tm

    def stages():
        i = pl.program_id(0)
        return i == 0, i == max(nt - 3, 0), i == nt - 1

    return _call(
        body, hosted, stages, grid=(nt,), name="mixer_fwd",
        in_specs=[pl.BlockSpec((tm, din), lambda i: (i, 0)),
                  pl.BlockSpec((HALO, din), lambda i: (jnp.maximum(i * (tm // HALO) - 1, 0), 0)),
                  pl.BlockSpec((tm, d), lambda i: (i, 0))]
        + [_const_spec(a.shape) for a in small] + [_const_spec(w_out.shape)],
        out_specs=[pl.BlockSpec((tm, lw), lambda i: (i, 0)), pl.BlockSpec((tm, d), lambda i: (i, 0)),
                   pl.BlockSpec((tm, d), lambda i: (i, 0)),
                   pl.BlockSpec((len(MIX_SAVED), tm, lw), lambda i: (0, i, 0)),
                   pl.BlockSpec((tm, lw), lambda i: (i, 0))],
        out_shape=[SDS((s, lw), F32), SDS((s, d), BF16), SDS((s, d), F32), SDS((len(MIX_SAVED), s, lw), F32),
                   SDS((s, lw), BF16)],
        scratch_shapes=[pltpu.VMEM((nb, GATE_BLOCK, 2 * GATE_BLOCK), BF16), pltpu.VMEM((tm, lw), F32),
                        pltpu.VMEM((tm, lw), F32), pltpu.VMEM((SUBLANES, lw), F32)],
        args=(u, u, x, *small, w_out), sem=("arbitrary",))


def _row_chunks(tm):
    rc = tm // FFN_ROW_CHUNKS
    return [slice(q * rc, (q + 1) * rc) for q in range(FFN_ROW_CHUNKS)]


def _ffn_up(hres1, g_ffn, w1, w3):
    s, d = hres1.shape
    nj, fc, _ = w1.shape
    tm = min(TM_FFN, s)

    def body(h_ref, gf_ref, w1_ref, w3_ref, h2_ref, a1_ref, a3_ref, ff_ref):
        hv = h_ref[...]
        r = lax.rsqrt(_rowmean(hv * hv) + EPS)
        h2_ref[...] = (hv * r * gf_ref[...]).astype(BF16)
        h2 = h2_ref[...]
        for j in range(nj):
            a1 = _dot_nt(h2, w1_ref[j])
            a3 = _dot_nt(h2, w3_ref[j])
            a1_ref[j] = a1.astype(BF16)
            a3_ref[j] = a3.astype(BF16)
            ff_ref[j] = ((a1 * _sigmoid(a1)) * a3).astype(BF16)

    wspec = _const_spec(w1.shape)
    aspec = pl.BlockSpec((nj, tm, fc), lambda i: (0, i, 0))
    return pl.pallas_call(
        body, grid=(s // tm,), name="ffn_up",
        in_specs=[pl.BlockSpec((tm, d), lambda i: (i, 0)), _const_spec((1, d)), wspec, wspec],
        out_specs=[pl.BlockSpec((tm, d), lambda i: (i, 0)), aspec, aspec, aspec],
        out_shape=[SDS((s, d), BF16)] + [SDS((nj, s, fc), BF16)] * 3,
        compiler_params=_cp(("parallel",)))(hres1, g_ffn, w1, w3)


def _ffn_down(ff, hres1, target, g_fin, w2):
    s, d = hres1.shape
    nj, _, fc = ff.shape
    tm = min(TM_FFN, s)

    def body(ff_ref, h_ref, t_ref, gn_ref, w2_ref, dh_ref, dhb_ref, loss_ref, dgn_ref):
        @pl.when(pl.program_id(0) == 0)
        def _():
            loss_ref[...] = jnp.zeros_like(loss_ref)
            dgn_ref[...] = jnp.zeros_like(dgn_ref)

        gn = gn_ref[...]
        for rows in _row_chunks(tm):
            acc = _dot(ff_ref[0, rows, :], w2_ref[0])
            for j in range(1, nj):
                acc = acc + _dot(ff_ref[j, rows, :], w2_ref[j])
            hr2 = h_ref[rows, :] + acc
            r2 = lax.rsqrt(_rowmean(hr2 * hr2) + EPS)
            xh = hr2 * r2
            diff = xh * gn - t_ref[rows, :]
            tot = jnp.sum(jnp.sum(diff * diff, axis=1, keepdims=True), axis=0, keepdims=True)
            loss_ref[...] += tot * (0.5 / d)
            dout = diff * (1.0 / d)
            dgn_ref[...] += _colsum8(dout * xh)
            dh = _rms_bwd(dout, xh, r2, gn)
            dh_ref[rows, :] = dh
            dhb_ref[rows, :] = dh.astype(BF16)

    tile = pl.BlockSpec((tm, d), lambda i: (i, 0))
    return pl.pallas_call(
        body, grid=(s // tm,), name="ffn_down",
        in_specs=[pl.BlockSpec((nj, tm, fc), lambda i: (0, i, 0)), tile, tile, _const_spec((1, d)),
                  _const_spec(w2.shape)],
        out_specs=[tile, tile, pl.BlockSpec((SUBLANES, LANES), lambda i: (0, 0)),
                   pl.BlockSpec((SUBLANES, d), lambda i: (0, 0))],
        out_shape=[SDS((s, d), F32), SDS((s, d), BF16), SDS((SUBLANES, LANES), F32), SDS((SUBLANES, d), F32)],
        compiler_params=_cp(("arbitrary",)))(ff, hres1, target, g_fin, w2)


def _ffn_bwd_gate(dhb, a1, a3, w2):
    s, d = dhb.shape
    nj, _, fc = a1.shape
    tm = min(TM_FFN, s)

    def body(dhb_ref, a1_ref, a3_ref, w2_ref, da1_ref, da3_ref):
        for j in range(nj):
            for rows in _row_chunks(tm):
                dff = _dot_nt(dhb_ref[rows, :], w2_ref[j])
                a1v = a1_ref[j, rows, :].astype(F32)
                sg = _sigmoid(a1v)
                silu = a1v * sg
                da1_ref[j, rows, :] = (dff * a3_ref[j, rows, :].astype(F32)
                                       * (sg * (1.0 + (a1v - silu)))).astype(BF16)
                da3_ref[j, rows, :] = (dff * silu).astype(BF16)

    aspec = pl.BlockSpec((nj, tm, fc), lambda i: (0, i, 0))
    return pl.pallas_call(
        body, grid=(s // tm,), name="ffn_bwd_gate",
        in_specs=[pl.BlockSpec((tm, d), lambda i: (i, 0)), aspec, aspec, _const_spec(w2.shape)],
        out_specs=[aspec, aspec], out_shape=[SDS((nj, s, fc), BF16)] * 2,
        compiler_params=_cp(("parallel",)))(dhb, a1, a3, w2)


def _ffn_bwd_down(da1, da3, dh, hres1, g_ffn, w1, w3, hosted=None):
    s, d = hres1.shape
    nj, _, fc = da1.shape
    tm = min(TM_FFN, s)
    nt = s // tm

    def body(da1_ref, da3_ref, dh_ref, h_ref, gf_ref, w1_ref, w3_ref, dhr_ref, dgf_ref):
        @pl.when(pl.program_id(0) == 0)
        def _():
            dgf_ref[...] = jnp.zeros_like(dgf_ref)

        gf = gf_ref[...]
        for rows in _row_chunks(tm):
            dh2 = None
            for j in range(nj):
                part = _dot(da1_ref[j, rows, :], w1_ref[j]) + _dot(da3_ref[j, rows, :], w3_ref[j])
                dh2 = part if dh2 is None else dh2 + part
            hv = h_ref[rows, :]
            r = lax.rsqrt(_rowmean(hv * hv) + EPS)
            xh = hv * r
            dgf_ref[...] += _colsum8(dh2 * xh)
            dhr_ref[rows, :] = dh_ref[rows, :] + _rms_bwd(dh2, xh, r, gf)

    tile = pl.BlockSpec((tm, d), lambda i: (i, 0))
    aspec = pl.BlockSpec((nj, tm, fc), lambda i: (0, i, 0))
    wspec = _const_spec(w1.shape)

    def stages():
        i = pl.program_id(0)
        return i == 0, i == max(nt - 2, 0), i == nt - 1

    return _call(
        body, hosted, stages, grid=(nt,), name="ffn_bwd_down",
        in_specs=[aspec, aspec, tile, tile, _const_spec((1, d)), wspec, wspec],
        out_specs=[tile, pl.BlockSpec((SUBLANES, d), lambda i: (0, 0))],
        out_shape=[SDS((s, d), F32), SDS((SUBLANES, d), F32)],
        scratch_shapes=[], args=(da1, da3, dh, hres1, g_ffn, w1, w3), sem=("arbitrary",))


def _ffn_wgrad(h2, dhb, ff, da1, da3):
    s, d = h2.shape
    _, _, fc = ff.shape
    tm = min(TM_WGRAD, s)

    def body(h2_ref, dhb_ref, ff_ref, da1_ref, da3_ref, dw1_ref, dw3_ref, dw2_ref):
        @pl.when(pl.program_id(1) == 0)
        def _():
            dw1_ref[...] = jnp.zeros_like(dw1_ref)
            dw3_ref[...] = jnp.zeros_like(dw3_ref)
            dw2_ref[...] = jnp.zeros_like(dw2_ref)

        h2v = h2_ref[...]
        dw1_ref[0] += _dot_tn(da1_ref[0], h2v)
        dw3_ref[0] += _dot_tn(da3_ref[0], h2v)
        dw2_ref[0] += _dot_tn(ff_ref[0], dhb_ref[...])

    wspec = pl.BlockSpec((1, fc, d), lambda j, i: (j, 0, 0))
    return pl.pallas_call(
        body, grid=(N_CHIPS, s // tm), name="ffn_wgrad",
        in_specs=[pl.BlockSpec((tm, d), lambda j, i: (i, 0)), pl.BlockSpec((tm, d), lambda j, i: (i, 0))]
        + [pl.BlockSpec((1, tm, fc), lambda j, i: (j, i, 0))] * 3,
        out_specs=[wspec] * 3, out_shape=[SDS((N_CHIPS, fc, d), F32)] * 3,
        compiler_params=_cp(("parallel", "arbitrary")))(h2, dhb, ff, da1, da3)


def _mixer_bwd(u, saved, pooled, h, dhres1, sp_, w_out, hosted=None):
    s, din = u.shape
    d = dhres1.shape[1]
    lw = din // 3
    tm = min(TM_MIX, s)
    nt = s // tm
    nb = lw // GATE_BLOCK
    hd = sp_["gate_a_w"].shape[1]

    def body(ul_ref, saved_ref, pooled_ref, h_ref, hhalo_ref, dhr_ref, cw_ref, cb_ref, ga_ref, gx_ref, ba_ref,
             bx_ref, lam_ref, pw_ref, pb_ref, ps_ref, gl_ref, gp_ref, wout_ref, du_ref, slab_ref,
             gw_s, a_s, b_s, e_s, ecarry_s, dxc_s, q_s, vec_s, cwacc_s, dgw_s, dpw_s):
        i = pl.program_id(0)
        tile = nt - 1 - i

        @pl.when(i == 0)
        def _():
            _build_gate_blocks(ga_ref, gx_ref, gw_s)
            for ref in (ecarry_s, dxc_s, q_s, vec_s, cwacc_s, dgw_s, dpw_s):
                ref[...] = jnp.zeros_like(ref)

        cw = cw_ref[...]
        lam = lam_ref[...]
        ps = ps_ref[...]
        f = {name: saved_ref[k] for k, name in enumerate(MIX_SAVED)}
        f["mult"] = jnp.sqrt(jnp.maximum(f["m2raw"], 1e-12))
        f["sp"] = _softplus_neg(lam)
        f["xcb"] = f["xc"].astype(BF16)
        pooled = pooled_ref[...]
        row = lax.broadcasted_iota(jnp.int32, (tm, LANES), 0) + tile * tm
        f["invs"] = [1.0 / jnp.minimum(row + 1, w).astype(F32) for w in POOL_WINDOWS]
        f["z"] = jnp.concatenate(
            [_dot(pooled[:, g * LANES:(g + 1) * LANES], pw_ref[g].astype(BF16))
             for g in range(len(POOL_WINDOWS))], axis=1) + pb_ref[...]
        f["y_pool"] = f["z"] * ps
        u_l = ul_ref[...]
        hv = h_ref[...]
        h_prev = _shift_down(jnp.where(tile > 0, hhalo_ref[...], 0.0), hv, 1)
        y_lru = hv * f["ge"]
        rl = lax.rsqrt(_rowmean(y_lru * y_lru) + EPS)
        yp = f["y_pool"]
        rp = lax.rsqrt(_rowmean(yp * yp) + EPS)
        xh_l = y_lru * rl
        xh_p = yp * rp

        dyn = _dot_nt(dhr_ref[...].astype(BF16), wout_ref[...])
        d_nl, d_np = dyn[:, :lw], dyn[:, lw:]
        vec = {}
        vec[ROW_GL] = _colsum8(d_nl * xh_l)
        vec[ROW_GP] = _colsum8(d_np * xh_p)
        d_ylru = _rms_bwd(d_nl, xh_l, rl, gl_ref[...])
        d_ypool = _rms_bwd(d_np, xh_p, rp, gp_ref[...])

        vec[ROW_PS] = _colsum8(d_ypool * f["z"])
        dz = d_ypool * ps
        vec[ROW_PB] = _colsum8(dz)
        dzb = dz.astype(BF16)
        dup = []
        for gi, w in enumerate(POOL_WINDOWS):
            sl = slice(gi * LANES, (gi + 1) * LANES)
            dpw_s[:, sl] += _dot_tn(pooled[:, sl], dzb[:, sl])
            dpool = _dot_nt(dzb[:, sl], pw_ref[gi].astype(BF16))
            q = dpool * f["invs"][gi]
            e = jnp.concatenate([q, q_s[:, sl]], axis=0)
            k = 1
            while k < w:
                e = e + pltpu.roll(e, tm + HALO - k, 0)
                k *= 2
            dup.append(e[:tm] - dpool)
            q_s[:, sl] = q[:HALO]

        d_hout = d_ylru * f["ge"]
        d_ug = d_ylru * hv * f["dge"]
        a = f["a"]
        a1, b1 = _scan_level1(a, a * d_hout, reverse=True)
        a_s[...] = a1
        b_s[...] = b1
        e_next = ecarry_s[...]
        ecarry_s[...] = _scan_level2(a_s, b_s, e_s, e_next, reverse=True)
        sv = d_hout + _shift_up(e_s[...], e_next, 1)
        d_a = sv * h_prev
        mult, ig, xc, r = f["mult"], f["ig"], f["xc"], f["r"]
        d_mult = sv * (ig * xc)
        d_ig = sv * mult * xc
        d_xc = sv * mult * ig
        d_la = d_a * a + jnp.where(f["m2raw"] > 1e-12, d_mult * (-(a * a) / mult), 0.0)
        d_r = d_la * (-LRU_C * f["sp"])
        vec[ROW_LAM] = _colsum8(d_la * (-LRU_C * r))
        d_pr = d_r * r * (1.0 - r)
        d_pi = d_ig * ig * (1.0 - ig)
        vec[ROW_BA] = _colsum8(d_pr)
        vec[ROW_BX] = _colsum8(d_pi)
        dxc_parts = []
        for b in range(nb):
            sl = slice(b * GATE_BLOCK, (b + 1) * GATE_BLOCK)
            rhs = jnp.concatenate([d_pr[:, sl], d_pi[:, sl]], axis=1).astype(BF16)
            dgw_s[b] += _dot_tn(f["xcb"][:, sl], rhs)
            dxc_parts.append(_dot_nt(rhs, gw_s[b]))
        d_xc = d_xc + jnp.concatenate(dxc_parts, axis=1)
        vec[ROW_CONV_B] = _colsum8(d_xc)
        dxc_next = dxc_s[...]
        d_ul = None
        for k in range(CONV_WIDTH):
            ahead = _shift_up(d_xc, dxc_next, CONV_WIDTH - 1 - k)
            cwacc_s[k * SUBLANES:(k + 1) * SUBLANES, :] += _colsum8(ahead * u_l)
            term = ahead * cw[k:k + 1, :]
            d_ul = term if d_ul is None else d_ul + term
        dxc_s[...] = d_xc[:SUBLANES]
        for row, val in vec.items():
            vec_s[row * SUBLANES:(row + 1) * SUBLANES, :] += val
        du_ref[...] = jnp.concatenate([d_ul, d_ug] + dup, axis=1).astype(BF16)

        @pl.when(i == nt - 1)
        def _():
            rows = []
            for row in range(ROW_GA):
                if row in (ROW_CONV_W, ROW_CONV_W + 1, ROW_CONV_W + 2, ROW_CONV_W + 3):
                    k = row - ROW_CONV_W
                    v = jnp.sum(cwacc_s[k * SUBLANES:(k + 1) * SUBLANES, :], axis=0, keepdims=True)
                elif row <= ROW_GP:
                    v = jnp.sum(vec_s[row * SUBLANES:(row + 1) * SUBLANES, :], axis=0, keepdims=True)
                    if row == ROW_LAM:
                        v = v * (-1.0 / (1.0 + jnp.exp(lam)))
                else:
                    v = jnp.zeros((1, lw), F32)
                rows.append(v)
            slab_ref[0:ROW_GA, :] = jnp.concatenate(rows, axis=0)
            lane = lax.broadcasted_iota(jnp.int32, (hd, GATE_BLOCK), 1)
            for b in range(nb):
                for off, row0 in ((0, ROW_GA), (GATE_BLOCK, ROW_GX)):
                    acc = jnp.zeros((hd, GATE_BLOCK), F32)
                    for hh in range(GATE_BLOCK // hd):
                        m = (lane >= hh * hd) & (lane < (hh + 1) * hd)
                        acc = acc + jnp.where(m, dgw_s[b, hh * hd:(hh + 1) * hd, off:off + GATE_BLOCK], 0.0)
                    slab_ref[row0:row0 + hd, b * GATE_BLOCK:(b + 1) * GATE_BLOCK] = acc
            slab_ref[ROW_PW:ROW_PW + LANES, :] = dpw_s[...]

    small = [sp_[k] for k in ("conv_w", "conv_b", "gate_a_w", "gate_x_w", "gate_a_b", "gate_x_b", "lru_lambda",
                              "pool_w", "pool_b", "pool_scale", "norm_lru_g", "norm_pool_g")]
    rev = lambda i: nt - 1 - i

    def stages():
        i = pl.program_id(0)
        return i == 0, i == max(nt - 3, 0), i == nt - 1

    return _call(
        body, hosted, stages, grid=(nt,), name="mixer_bwd",
        in_specs=[pl.BlockSpec((tm, lw), lambda i: (rev(i), 0)),
                  pl.BlockSpec((len(MIX_SAVED), tm, lw), lambda i: (0, rev(i), 0)),
                  pl.BlockSpec((tm, lw), lambda i: (rev(i), 0)),
                  pl.BlockSpec((tm, lw), lambda i: (rev(i), 0)),
                  pl.BlockSpec((SUBLANES, lw), lambda i: (jnp.maximum(rev(i) * (tm // SUBLANES) - 1, 0), 0)),
                  pl.BlockSpec((tm, d), lambda i: (rev(i), 0))]
        + [_const_spec(a.shape) for a in small] + [_const_spec(w_out.shape)],
        out_specs=[pl.BlockSpec((tm, din), lambda i: (rev(i), 0)),
                   pl.BlockSpec((MIX_SLAB_ROWS, SLAB_W), lambda i: (0, 0))],
        out_shape=[SDS((s, din), BF16), SDS((MIX_SLAB_ROWS, SLAB_W), F32)],
        scratch_shapes=[pltpu.VMEM((nb, GATE_BLOCK, 2 * GATE_BLOCK), BF16),
                        pltpu.VMEM((tm, lw), F32), pltpu.VMEM((tm, lw), F32), pltpu.VMEM((tm, lw), F32),
                        pltpu.VMEM((SUBLANES, lw), F32), pltpu.VMEM((SUBLANES, lw), F32),
                        pltpu.VMEM((HALO, lw), F32), pltpu.VMEM((ROW_GA * SUBLANES, lw), F32),
                        pltpu.VMEM((CONV_WIDTH * SUBLANES, lw), F32),
                        pltpu.VMEM((nb, GATE_BLOCK, 2 * GATE_BLOCK), F32), pltpu.VMEM((LANES, lw), F32)],
        args=(u, saved, pooled, h, h, dhres1, *small, w_out), sem=("arbitrary",))


def _inproj_bwd(x, du, dhres1, yn, g_mix, w_in, hosted=None):
    s, d = x.shape
    n = w_in.shape[1]
    nc = n // N_CHIPS
    tm = min(TM_PROJ, s)
    nt = s // tm

    def body(x_ref, du_ref, dhr_ref, yn_ref, g_ref, w_ref, gx_ref, dwin_ref, dwout_ref, dg_ref):
        i = pl.program_id(0)

        @pl.when(i == 0)
        def _():
            dwin_ref[...] = jnp.zeros_like(dwin_ref)
            dwout_ref[...] = jnp.zeros_like(dwout_ref)
            dg_ref[...] = jnp.zeros_like(dg_ref)

        xv = x_ref[...]
        g = g_ref[...]
        r = lax.rsqrt(_rowmean(xv * xv) + EPS)
        xh = xv * r
        h1 = (xh * g).astype(BF16)
        duv = du_ref[...]
        dh1 = _dot_nt(duv, w_ref[...])
        dg_ref[...] += _colsum8(dh1 * xh)
        dhr = dhr_ref[...]
        gx_ref[...] = dhr + _rms_bwd(dh1, xh, r, g)
        for jj in range(N_CHIPS):
            dwin_ref[jj] += _dot_tn(h1, duv[:, jj * nc:(jj + 1) * nc])
        dwout_ref[...] += _dot_tn(yn_ref[...], dhr.astype(BF16))

    def stages():
        i = pl.program_id(0)
        return i == 0, i == max(nt - 3, 0), i == nt - 1

    return _call(
        body, hosted, stages, grid=(nt,), name="inproj_bwd",
        in_specs=[pl.BlockSpec((tm, d), lambda i: (i, 0)), pl.BlockSpec((tm, n), lambda i: (i, 0)),
                  pl.BlockSpec((tm, d), lambda i: (i, 0)), pl.BlockSpec((tm, d), lambda i: (i, 0)),
                  _const_spec((1, d)), _const_spec((d, n))],
        out_specs=[pl.BlockSpec((tm, d), lambda i: (i, 0)), pl.BlockSpec((N_CHIPS, d, nc), lambda i: (0, 0, 0)),
                   pl.BlockSpec((d, d), lambda i: (0, 0)), pl.BlockSpec((SUBLANES, d), lambda i: (0, 0))],
        out_shape=[SDS((s, d), F32), SDS((N_CHIPS, d, nc), F32), SDS((d, d), F32), SDS((SUBLANES, d), F32)],
        scratch_shapes=[], args=(x, du, dhres1, yn, g_mix, w_in), sem=("arbitrary",))


def _place():
    x, y, c = lax.axis_index("x"), lax.axis_index("y"), lax.axis_index("c")
    return x, y, c


def _other_chips(x, y):
    return [(1 - x, y), (x, 1 - y), (1 - x, 1 - y)]


ANY = pl.BlockSpec(memory_space=pl.ANY)
VMEM_SPEC = pl.BlockSpec(memory_space=pltpu.VMEM)

_GATHERED = {"w_in": "cols", "w_out": "major", "ffn_w1": "major", "ffn_w3": "major", "ffn_w2": "major"}
_BIG = ("w_in", "w_out", "ffn_w1", "ffn_w3", "ffn_w2")


def _gather_weights(shards, conv_w, n_remote):
    n = len(shards)
    full_shapes = []
    for name, sh in zip(_BIG, shards):
        r, cdim = sh.shape
        if _GATHERED[name] == "cols":
            assert cdim % LANES == 0
            full_shapes.append((r, cdim * N_CHIPS))
        else:
            full_shapes.append((N_CHIPS, r, cdim))

    def region(ref, name, sh, jj, cc):
        r, cdim = sh
        rows = pl.ds(0, r) if cc is None else pl.ds(pl.multiple_of(cc * (r // 2), 16), r // 2)
        if _GATHERED[name] == "cols":
            return ref.at[rows, pl.ds(pl.multiple_of(jj * cdim, LANES), cdim)]
        return ref.at[jj, rows, :]

    def staged(ref, sh, cc):
        r = sh[0]
        return ref.at[pl.ds(pl.multiple_of(cc * (r // 2), 16), r // 2), :]

    def body(*refs):
        ins, cw_in = refs[:n], refs[n]
        outs, cw_out = refs[n + 1:2 * n + 1], refs[2 * n + 1]
        stage = refs[2 * n + 2:3 * n + 2]
        cw_stage, lsem, ssem, rsem, fssem, frsem, cssem, crsem = refs[3 * n + 2:]
        x, y, c = _place()
        j = 2 * x + y
        chips = _other_chips(x, y)
        for w in range(n_remote):
            stage[w][...] = ins[w][...].astype(BF16)
        cw_stage[...] = jnp.zeros_like(cw_stage)
        cw_stage[0:CONV_WIDTH, :] = cw_in[...]
        shs = [s_.shape for s_ in shards]
        local = [pltpu.make_async_copy(stage[w], region(outs[w], _BIG[w], shs[w], j, None), lsem.at[w])
                 for w in range(n)]
        local.append(pltpu.make_async_copy(cw_stage, cw_out.at[j], lsem.at[n]))
        sends = []
        for k, (px, py) in enumerate(chips):
            for w in range(n_remote):
                sends.append(pltpu.make_async_remote_copy(
                    src_ref=staged(stage[w], shs[w], c), dst_ref=region(outs[w], _BIG[w], shs[w], j, c),
                    send_sem=ssem.at[k * n + w], recv_sem=rsem.at[k * n + w], device_id=(px, py, c),
                    device_id_type=MESH))
            sends.append(pltpu.make_async_remote_copy(
                src_ref=cw_stage, dst_ref=cw_out.at[j], send_sem=cssem.at[k], recv_sem=crsem.at[k],
                device_id=(px, py, c), device_id_type=MESH))
        for cp in sends:
            cp.start()
        for w in range(n_remote, n):
            stage[w][...] = ins[w][...].astype(BF16)
        for cp in local:
            cp.start()
        fwd = []
        for k, (px, py) in enumerate(chips):
            jk = 2 * px + py
            for w in range(n_remote):
                reg = region(outs[w], _BIG[w], shs[w], jk, c)
                pltpu.make_async_remote_copy(src_ref=reg, dst_ref=reg, send_sem=ssem.at[k * n + w],
                                             recv_sem=rsem.at[k * n + w], device_id=(px, py, c),
                                             device_id_type=MESH).wait_recv()
                cp = pltpu.make_async_remote_copy(src_ref=reg, dst_ref=reg, send_sem=fssem.at[k * n + w],
                                                  recv_sem=frsem.at[k * n + w], device_id=(x, y, 1 - c),
                                                  device_id_type=MESH)
                cp.start()
                fwd.append(cp)
            pltpu.make_async_remote_copy(src_ref=cw_stage, dst_ref=cw_out.at[jk], send_sem=cssem.at[k],
                                         recv_sem=crsem.at[k], device_id=(px, py, c),
                                         device_id_type=MESH).wait_recv()
        for k, (px, py) in enumerate(chips):
            jk = 2 * px + py
            for w in range(n_remote):
                reg = region(outs[w], _BIG[w], shs[w], jk, 1 - c)
                pltpu.make_async_remote_copy(src_ref=reg, dst_ref=reg, send_sem=fssem.at[k * n + w],
                                             recv_sem=frsem.at[k * n + w], device_id=(x, y, 1 - c),
                                             device_id_type=MESH).wait_recv()
        for cp in sends + fwd:
            cp.wait_send()
        for cp in local:
            cp.wait()

    nsem = 3 * n
    return pl.pallas_call(
        body, name="gather_first",
        in_specs=[VMEM_SPEC] * (n + 1), out_specs=[ANY] * (n + 1),
        out_shape=[SDS(fs, BF16) for fs in full_shapes] + [SDS((N_CHIPS, SUBLANES, LANES), F32)],
        scratch_shapes=[pltpu.VMEM(s_.shape, BF16) for s_ in shards] + [pltpu.VMEM((SUBLANES, LANES), F32)]
        + [pltpu.SemaphoreType.DMA((n + 1,))] + [pltpu.SemaphoreType.DMA((nsem,))] * 4
        + [pltpu.SemaphoreType.DMA((3,))] * 2,
        compiler_params=_cp())(*shards, conv_w)


def _start_all(make):
    def f(ins, outs, sems):
        for cp in make(ins, outs, sems):
            cp.start()
    return f


def _wait_all(make):
    def f(ins, outs, sems):
        for cp in make(ins, outs, sems):
            cp.wait()
    return f


def _ffn_gather_hosted(arrs):
    n = len(arrs)

    def make(outs, sems):
        ssem, rsem, fs, fr = sems
        x, y, c = _place()
        j = 2 * x + y

        def reg(w, jj, cc):
            hr = arrs[w].shape[1] // 2
            return outs[w].at[jj, pl.ds(pl.multiple_of(cc * hr, 16), hr), :]

        def rc(w, jj, cc, s_sem, r_sem, dev):
            return pltpu.make_async_remote_copy(src_ref=reg(w, jj, cc), dst_ref=reg(w, jj, cc), send_sem=s_sem,
                                                recv_sem=r_sem, device_id=dev, device_id_type=MESH)

        sends, recvs, fwds, frecvs = [], [], [], []
        for k, (px, py) in enumerate(_other_chips(x, y)):
            jk = 2 * px + py
            for w in range(n):
                q = k * n + w
                sends.append(rc(w, j, c, ssem.at[q], rsem.at[q], (px, py, c)))
                recvs.append(rc(w, jk, c, ssem.at[q], rsem.at[q], (px, py, c)))
                fwds.append(rc(w, jk, c, fs.at[q], fr.at[q], (x, y, 1 - c)))
                frecvs.append(rc(w, jk, 1 - c, fs.at[q], fr.at[q], (x, y, 1 - c)))
        return sends, recvs, fwds, frecvs

    def start(ins, outs, sems):
        for cp in make(outs, sems)[0]:
            cp.start()

    def mid(ins, outs, sems):
        _, recvs, fwds, _ = make(outs, sems)
        for r, f in zip(recvs, fwds):
            r.wait_recv()
            f.start()

    def finish(ins, outs, sems):
        sends, _, fwds, frecvs = make(outs, sems)
        for r in frecvs:
            r.wait_recv()
        for cp in sends + fwds:
            cp.wait_send()

    return _Hosted(arrs, [SDS(a.shape, a.dtype) for a in arrs], [3 * n] * 4, start, finish, mid=mid,
                   aliases={w: w for w in range(n)})


def _rs_sibling_hosted(arrs):
    n = len(arrs)

    def make(ins, outs, sems):
        x, y, c = _place()
        cps = []
        for w in range(n):
            hr = arrs[w].shape[1] // 2
            src = ins[w].at[:, pl.ds(pl.multiple_of((1 - c) * hr, SUBLANES), hr), :]
            cps.append(pltpu.make_async_remote_copy(src_ref=src, dst_ref=outs[w], send_sem=sems[0].at[w],
                                                    recv_sem=sems[1].at[w], device_id=(x, y, 1 - c),
                                                    device_id_type=MESH))
        return cps

    return _Hosted(arrs, [SDS((a.shape[0], a.shape[1] // 2, a.shape[2]), F32) for a in arrs], [n, n],
                   _start_all(make), _wait_all(make))


def _rs_chips_hosted(parts):
    n = len(parts)

    def make(ins, outs, sems):
        x, y, c = _place()
        j = 2 * x + y
        cps = []
        for k, (px, py) in enumerate(_other_chips(x, y)):
            jk = 2 * px + py
            for w in range(n):
                cps.append(pltpu.make_async_remote_copy(
                    src_ref=ins[w].at[jk], dst_ref=outs[w].at[j], send_sem=sems[0].at[k * n + w],
                    recv_sem=sems[1].at[k * n + w], device_id=(px, py, c), device_id_type=MESH))
        return cps

    return _Hosted(parts, [SDS(p.shape, p.dtype) for p in parts], [3 * n, 3 * n], _start_all(make), _wait_all(make))


def _rs_swap_hosted(halves):
    n = len(halves)

    def make(ins, outs, sems):
        x, y, c = _place()
        return [pltpu.make_async_remote_copy(src_ref=ins[w], dst_ref=outs[w], send_sem=sems[0].at[w],
                                             recv_sem=sems[1].at[w], device_id=(x, y, 1 - c), device_id_type=MESH)
                for w in range(n)]

    return _Hosted(halves, [SDS(h.shape, F32) for h in halves], [n, n], _start_all(make), _wait_all(make))


HBM_SPEC = pl.BlockSpec(memory_space=pltpu.HBM)
SEM_SPEC = pl.BlockSpec(memory_space=pltpu.SEMAPHORE)
_EFFECT = pltpu.SideEffectType.DATAFLOW_SIDE_EFFECTING


def _split_start(h, name):
    n_in, n_out, ns = len(h.ins), len(h.out_shapes), len(h.sems)
    ins = [pltpu.with_memory_space_constraint(a, pltpu.HBM) for a in h.ins]
    lands = [pltpu.with_memory_space_constraint(lax.empty(o.shape, o.dtype), pltpu.HBM) for o in h.out_shapes]

    def body(*refs):
        i_refs, l_refs = refs[:n_in], refs[n_in:n_in + n_out]
        s_refs = refs[n_in + n_out:n_in + n_out + ns]
        token = refs[-1]
        h.start(i_refs, l_refs, s_refs)
        token[...] = jnp.zeros_like(token)

    res = pl.pallas_call(
        body, name=name, in_specs=[HBM_SPEC] * (n_in + n_out),
        out_specs=[SEM_SPEC] * ns + [HBM_SPEC] * (n_in + n_out) + [VMEM_SPEC],
        out_shape=[pltpu.SemaphoreType.DMA((k,)) for k in h.sems]
        + [pltpu.HBM(a.shape, a.dtype) for a in h.ins] + [pltpu.HBM(o.shape, o.dtype) for o in h.out_shapes]
        + [SDS((SUBLANES, LANES), F32)],
        input_output_aliases={k: ns + k for k in range(n_in + n_out)},
        compiler_params=pltpu.CompilerParams(has_side_effects=_EFFECT))(*ins, *lands)
    return list(res[:-1]), res[-1]


def _split_wait(h, state, after, name):
    n_in, n_out, ns = len(h.ins), len(h.out_shapes), len(h.sems)
    sems, bufs = state[:ns], state[ns:]

    def body(*refs):
        i_refs, l_refs = refs[:n_in], refs[n_in:n_in + n_out]
        s_refs = refs[n_in + n_out:n_in + n_out + ns]
        h.finish(i_refs, l_refs, s_refs)

    res = pl.pallas_call(
        body, name=name, in_specs=[HBM_SPEC] * (n_in + n_out) + [SEM_SPEC] * ns + [ANY],
        out_specs=[HBM_SPEC] * (n_in + n_out),
        out_shape=[pltpu.HBM(b.shape, b.dtype) for b in bufs],
        input_output_aliases={k: k for k in range(n_in + n_out)},
        compiler_params=pltpu.CompilerParams(has_side_effects=_EFFECT))(*bufs, *sems, after)
    return list(res[n_in:])


def _run_comm(hosted, name):
    return _call(lambda: None, hosted, None, name=name, grid=(), in_specs=[], out_specs=[], out_shape=[],
                 scratch_shapes=[], args=(), sem=None)[1]


def _row_tile(rows, cols, n_arrays):
    budget = 24 * 1024 * 1024 // (2 * 4 * n_arrays * cols)
    best = SUBLANES
    for t in range(SUBLANES, rows + 1, SUBLANES):
        if rows % t == 0 and t <= budget:
            best = t
    return best


def _place_index(which):
    x, y, c = _place()
    v = c if which == "c" else 2 * x + y
    return jnp.reshape(v, (1,)).astype(jnp.int32)


def _add_own_half(full, recv, name, wire=BF16):
    nsh, rows, cols = full.shape
    hr = rows // 2
    t = _row_tile(hr, cols, 4)
    nt = hr // t

    def body(c_ref, a_ref, b_ref, o_ref, ob_ref):
        v = a_ref[...] + b_ref[...]
        o_ref[...] = v
        ob_ref[...] = v.astype(wire)

    half = pl.BlockSpec((1, t, cols), lambda s_, i, c_ref: (s_, i, 0))
    return pl.pallas_call(
        body, name=name,
        grid_spec=pltpu.PrefetchScalarGridSpec(
            num_scalar_prefetch=1, grid=(nsh, nt),
            in_specs=[pl.BlockSpec((1, t, cols), lambda s_, i, c_ref: (s_, c_ref[0] * nt + i, 0)), half],
            out_specs=[half, half]),
        out_shape=[SDS((nsh, hr, cols), F32), SDS((nsh, hr, cols), wire)],
        compiler_params=_cp(("parallel", "parallel")))(_place_index("c"), full, recv)


def _sum_chips(own, recv, name):
    nsh, hr, cols = own.shape
    t = _row_tile(hr, cols, 6)

    def body(j_ref, own_ref, *rest):
        r_refs, o_ref = rest[:nsh], rest[nsh]
        j = j_ref[0]
        mine = own_ref[0]
        parts = [jnp.where(j == k, mine, r_refs[k][0].astype(F32)) for k in range(nsh)]
        o_ref[...] = ((parts[0] + parts[1]) + parts[2]) + parts[3]

    def other(k):
        return pl.BlockSpec((1, t, cols), lambda i, j_ref: (jnp.where(j_ref[0] == k, (k + 1) % nsh, k), i, 0))

    return pl.pallas_call(
        body, name=name,
        grid_spec=pltpu.PrefetchScalarGridSpec(
            num_scalar_prefetch=1, grid=(hr // t,),
            in_specs=[pl.BlockSpec((1, t, cols), lambda i, j_ref: (j_ref[0], i, 0))]
            + [other(k) for k in range(nsh)],
            out_specs=pl.BlockSpec((t, cols), lambda i, j_ref: (i, 0))),
        out_shape=SDS((hr, cols), F32), compiler_params=_cp(("parallel",)))(_place_index("j"), own, *([recv] * nsh))


def _adamw_math(w, g, m, v):
    m = ADAM_B1 * m + (1.0 - ADAM_B1) * g
    v = ADAM_B2 * v + (1.0 - ADAM_B2) * (g * g)
    m_hat = m / (1.0 - ADAM_B1 ** ADAM_STEP)
    v_hat = v / (1.0 - ADAM_B2 ** ADAM_STEP)
    delta = -ADAM_LR * (m_hat / (jnp.sqrt(v_hat) + ADAM_EPS) + ADAM_WD * w)
    return delta, m, v


def _adamw_big(w, g_own, g_sib, m, v, name, token=None):
    _, rows, cols = w.shape
    hr = rows // 2
    t = _row_tile(hr, cols, 9)
    nth = hr // t
    if token is None:
        token = jnp.zeros((SUBLANES, LANES), F32)

    def body(c_ref, w_ref, go_ref, gs_ref, m_ref, v_ref, tok_ref, g_ref, d_ref, mo_ref, vo_ref):
        own = (pl.program_id(0) // nth) == c_ref[0]
        g = jnp.where(own, go_ref[...], gs_ref[...]) + tok_ref[0:1, 0:1]
        g_ref[0] = g
        d_ref[0], mo_ref[0], vo_ref[0] = _adamw_math(w_ref[0], g, m_ref[0], v_ref[0])

    spec = pl.BlockSpec((1, t, cols), lambda i, c_ref: (0, i, 0))
    hspec = pl.BlockSpec((t, cols), lambda i, c_ref: (i % nth, 0))
    tspec = pl.BlockSpec((SUBLANES, LANES), lambda i, c_ref: (0, 0))
    return pl.pallas_call(
        body, name=name,
        grid_spec=pltpu.PrefetchScalarGridSpec(
            num_scalar_prefetch=1, grid=(2 * nth,), in_specs=[spec, hspec, hspec, spec, spec, tspec],
            out_specs=[spec] * 4),
        out_shape=[SDS((1, rows, cols), F32)] * 4,
        compiler_params=_cp(("parallel",)))(_place_index("c"), w, g_own, g_sib, m, v, token)


def _build_slab(mix_slab, dg_mix, dg_ffn, dg_fin, loss8):
    def body(ms_ref, gm_ref, gf_ref, gn_ref, loss_ref, out_ref):
        rows = []
        for ref in (gm_ref, gf_ref, gn_ref):
            v = jnp.sum(ref[...], axis=0, keepdims=True)
            rows += [v[:, :SLAB_W], v[:, SLAB_W:]]
        rows.append(jnp.concatenate([loss_ref[0:1, :]] * (SLAB_W // LANES), axis=1))
        rows.append(jnp.zeros((SLAB_ROWS - ROW_LOSS - 1, SLAB_W), F32))
        tail = jnp.concatenate(rows, axis=0)
        for k in range(N_CHIPS):
            out_ref[k, 0:MIX_SLAB_ROWS, :] = ms_ref[...]
            out_ref[k, MIX_SLAB_ROWS:SLAB_ROWS, :] = tail

    return pl.pallas_call(
        body, name="build_slab", in_specs=[VMEM_SPEC] * 5, out_specs=VMEM_SPEC,
        out_shape=SDS((N_CHIPS, SLAB_ROWS, SLAB_W), F32),
        compiler_params=_cp())(mix_slab, dg_mix, dg_ffn, dg_fin, loss8)


_SMALL_ROWS = (("conv_b", ROW_CONV_B), ("gate_a_b", ROW_BA), ("gate_x_b", ROW_BX), ("lru_lambda", ROW_LAM),
               ("pool_b", ROW_PB), ("pool_scale", ROW_PS), ("norm_lru_g", ROW_GL), ("norm_pool_g", ROW_GP))
_WIDE_ROWS = (("norm_mix_g", ROW_MIX), ("norm_ffn_g", ROW_FFN), ("final_norm_g", ROW_FIN))
_BLOCK_ROWS = (("gate_a_w", ROW_GA), ("gate_x_w", ROW_GX), ("pool_w", ROW_PW))
_SMALL_ORDER = tuple(n for n, _ in _SMALL_ROWS) + tuple(n for n, _ in _WIDE_ROWS) + tuple(
    n for n, _ in _BLOCK_ROWS) + ("conv_w",)


def _adamw_small(slab_own, slab_sib, wmv):
    names = _SMALL_ORDER
    flat = [a for nme in names for a in wmv[nme]]
    nin = len(flat)

    def body(*refs):
        own_ref, sib_ref, j_ref = refs[0], refs[1], refs[2]
        ins = refs[3:3 + nin]
        outs = refs[3 + nin:-1]
        first = j_ref[1] == 0
        slab_ref = jnp.concatenate([jnp.where(first, own_ref[...], sib_ref[...]),
                                    jnp.where(first, sib_ref[...], own_ref[...])], axis=0)
        refs[-1][...] = jnp.broadcast_to(slab_ref[ROW_LOSS:ROW_LOSS + 1, 0:LANES], (SUBLANES, LANES))
        grads = {}
        for nme, row in _SMALL_ROWS:
            grads[nme] = slab_ref[row:row + 1, :]
        for nme, row in _WIDE_ROWS:
            grads[nme] = jnp.concatenate([slab_ref[row:row + 1, :], slab_ref[row + 1:row + 2, :]], axis=1)
        full = slab_ref[ROW_CONV_W:ROW_CONV_W + CONV_WIDTH, :]
        jv = j_ref[0]
        g = jnp.zeros((CONV_WIDTH, LANES), F32)
        for jj in range(N_CHIPS):
            g = jnp.where(jv == jj, full[:, jj * LANES:(jj + 1) * LANES], g)
        grads["conv_w"] = g
        block_rows = dict(_BLOCK_ROWS)
        for idx, nme in enumerate(names):
            w_ref, m_ref, v_ref = ins[3 * idx:3 * idx + 3]
            if nme in block_rows:
                nblk, r, c = w_ref.shape
                parts = [(b, slab_ref[block_rows[nme]:block_rows[nme] + r, b * c:(b + 1) * c]) for b in range(nblk)]
            else:
                parts = [(Ellipsis, grads[nme])]
            for b, g in parts:
                delta, m, v = _adamw_math(w_ref[b], g, m_ref[b], v_ref[b])
                outs[4 * idx][b] = g
                outs[4 * idx + 1][b] = delta
                outs[4 * idx + 2][b] = m
                outs[4 * idx + 3][b] = v

    place = jnp.concatenate([_place_index("j"), _place_index("c")])
    out_shape = [SDS(wmv[nme][0].shape, F32) for nme in names for _ in range(4)] + [SDS((SUBLANES, LANES), F32)]
    res = pl.pallas_call(
        body, name="adamw_small",
        in_specs=[VMEM_SPEC, VMEM_SPEC, pl.BlockSpec(memory_space=pltpu.SMEM)] + [VMEM_SPEC] * nin,
        out_specs=[VMEM_SPEC] * len(out_shape), out_shape=out_shape,
        compiler_params=_cp())(slab_own, slab_sib, place, *flat)
    return {nme: tuple(res[4 * idx:4 * idx + 4]) for idx, nme in enumerate(names)}, res[-1]


_FFN = ("ffn_w1", "ffn_w3", "ffn_w2")
_TRANSPOSED = ("ffn_w1", "ffn_w3")


def _local_step(x, target, full, sp_, distributed):
    d = x.shape[1]
    (u,), got = _inproj(x, sp_["norm_mix_g"], full["w_in"],
                        [_ffn_gather_hosted([full["w_out"]])] if distributed else None)
    w_out = (got[0][0] if distributed else full["w_out"]).reshape(d, d)
    gather = [_ffn_gather_hosted([full[n] for n in _FFN])] if distributed else None
    (h, yn, hres1, saved, pooled), got = _mixer_fwd(u, x, sp_, w_out, gather)
    w1, w3, w2 = got[0] if distributed else [full[n] for n in _FFN]
    h2, a1, a3, ff = _ffn_up(hres1, sp_["norm_ffn_g"], w1, w3)
    dh, dhb, loss8, dg_fin = _ffn_down(ff, hres1, target, sp_["final_norm_g"], w2)
    da1, da3 = _ffn_bwd_gate(dhb, a1, a3, w2)
    dws = list(_ffn_wgrad(h2, dhb, ff, da1, da3))
    rs1 = [_rs_sibling_hosted(dws)] if distributed else None
    (dhres1, dg_ffn), got = _ffn_bwd_down(da1, da3, dh, hres1, sp_["norm_ffn_g"], w1, w3, rs1)
    rs2 = None
    if distributed:
        pairs = [_add_own_half(a, r, "add_half_" + n) for n, a, r in zip(_FFN, dws, got[0])]
        rs2 = [_rs_chips_hosted([pb for _, pb in pairs])]
    (du, mix_slab), got = _mixer_bwd(u, saved, pooled, h, dhres1, sp_, w_out, rs2)
    g_mix = sp_["norm_mix_g"]
    if distributed:
        fin = [_sum_chips(pairs[k][0], got[0][k], "sum_chips_" + n) for k, n in enumerate(_FFN)]
        swap = _rs_swap_hosted(fin)
        state, token = _split_start(swap, "ffn_swap_start")
        g_mix = g_mix + token[0:1, 0:1]
    (gx, dwin, dwout, dg_mix), _ = _inproj_bwd(x, du, dhres1, yn, g_mix, full["w_in"])
    if distributed:
        sib = _split_wait(swap, state, dg_mix, "ffn_swap_wait")
    big = {"w_in": dwin, "w_out": dwout.reshape(N_CHIPS, d // N_CHIPS, d)}
    for k, n in enumerate(_FFN):
        big[n] = (fin[k], sib[k]) if distributed else dws[k]
    return gx, big, (mix_slab, dg_mix, dg_ffn, dg_fin, loss8)


_SMALL_LAYOUT = {
    "gate_a_w": (lambda a: a[0], lambda a: a[None]),
    "gate_x_w": (lambda a: a[0], lambda a: a[None]),
    "pool_w": (lambda a: a[0], lambda a: a[None]),
    "conv_w": (lambda a: a[0], lambda a: a[None]),
    "final_norm_g": (lambda a: a[None], lambda a: a[0]),
}

_WEIGHTS = ("norm_mix_g", "w_in", "conv_w", "conv_b", "gate_a_w", "gate_a_b", "gate_x_w", "gate_x_b", "lru_lambda",
            "pool_w", "pool_b", "pool_scale", "norm_lru_g", "norm_pool_g", "w_out", "norm_ffn_g", "ffn_w1",
            "ffn_w3", "ffn_w2", "final_norm_g")


def kernel(x, norm_mix_g, w_in, conv_w, conv_b, gate_a_w, gate_a_b, gate_x_w, gate_x_b, lru_lambda, pool_w, pool_b, pool_scale, norm_lru_g, norm_pool_g, w_out, norm_ffn_g, ffn_w1, ffn_w3, ffn_w2, final_norm_g, loss_target, m_norm_mix_g, m_w_in, m_conv_w, m_conv_b, m_gate_a_w, m_gate_a_b, m_gate_x_w, m_gate_x_b, m_lru_lambda, m_pool_w, m_pool_b, m_pool_scale, m_norm_lru_g, m_norm_pool_g, m_w_out, m_norm_ffn_g, m_ffn_w1, m_ffn_w3, m_ffn_w2, m_final_norm_g, v_norm_mix_g, v_w_in, v_conv_w, v_conv_b, v_gate_a_w, v_gate_a_b, v_gate_x_w, v_gate_x_b, v_lru_lambda, v_pool_w, v_pool_b, v_pool_scale, v_norm_lru_g, v_norm_pool_g, v_w_out, v_norm_ffn_g, v_ffn_w1, v_ffn_w3, v_ffn_w2, v_final_norm_g):
    loc = locals()
    w = {n: loc[n] for n in _WEIGHTS}
    m = {n: loc["m_" + n] for n in _WEIGHTS}
    v = {n: loc["v_" + n] for n in _WEIGHTS}

    def lay(nme, a):
        return _SMALL_LAYOUT[nme][0](a) if nme in _SMALL_LAYOUT else a

    def unlay(nme, a):
        return _SMALL_LAYOUT[nme][1](a) if nme in _SMALL_LAYOUT else a

    for group in (w, m, v):
        for n in _TRANSPOSED:
            group[n] = jnp.transpose(group[n], (0, 2, 1))

    gathered = _gather_weights([w[n][0] for n in _BIG], w["conv_w"][0], n_remote=1)
    full = dict(zip(_BIG, gathered[:-1]))
    cw_all = gathered[-1]
    sp_ = {n: lay(n, w[n]) for n in _SMALL_ORDER}
    sp_["conv_w"] = jnp.transpose(cw_all[:, :CONV_WIDTH, :], (1, 0, 2)).reshape(CONV_WIDTH, N_CHIPS * LANES)

    gx, big, small = _local_step(x[0], loss_target[0], full, sp_, distributed=True)

    late = ("w_in", "w_out", "slab")
    big["slab"] = _build_slab(*small)
    fin = {n: big[n][0] for n in _FFN}
    sib = {n: big[n][1] for n in _FFN}
    recv1, = _run_comm([_rs_sibling_hosted([big[n] for n in late])], "tail_sibling")
    pairs = [_add_own_half(big[n], r, "add_half_" + n, F32 if n == "slab" else BF16) for n, r in zip(late, recv1)]
    chips = _rs_chips_hosted([pb for _, pb in pairs])
    state, token = _split_start(chips, "tail_chips_start")
    out = {}
    for n in _FFN:
        out[n] = tuple(_adamw_big(w[n], fin[n], sib[n], m[n], v[n], "adamw_" + n, token))
    recv2 = _split_wait(chips, state, out[_FFN[-1]][1], "tail_chips_wait")
    for n, (p, _), r in zip(late, pairs, recv2):
        fin[n] = _sum_chips(p, r, "sum_chips_" + n)
    swapped, = _run_comm([_rs_swap_hosted([fin[n] for n in late])], "tail_swap")
    sib.update(zip(late, swapped))
    for n in late[:2]:
        out[n] = tuple(_adamw_big(w[n], fin[n], sib[n], m[n], v[n], "adamw_" + n))
    for n in _TRANSPOSED:
        out[n] = tuple(jnp.transpose(a, (0, 2, 1)) for a in out[n])
    wmv = {n: (lay(n, w[n]), lay(n, m[n]), lay(n, v[n])) for n in _SMALL_ORDER}
    res, loss = _adamw_small(fin["slab"], sib["slab"], wmv)
    for n in _SMALL_ORDER:
        out[n] = tuple(unlay(n, a) for a in res[n])
    return (loss[0, 0], gx[None]) + tuple(out[n][k] for k in range(4) for n in _WEIGHTS)
```

```python
import functools
import math

import jax
import jax.numpy as jnp
from jax import lax
from jax.experimental import pallas as pl
from jax.experimental.pallas import tpu as pltpu

F32 = jnp.float32
BF16 = jnp.bfloat16
SDS = jax.ShapeDtypeStruct
MESH = pl.DeviceIdType.MESH

EPS = 1e-6
LRU_C = 8.0
CONV_WIDTH = 4
POOL_WINDOWS = (2, 4, 8, 16)
HALO = 16
LANES = 128
SUBLANES = 8
GATE_BLOCK = 256
N_CHIPS = 4

ADAM_LR = 0.001
ADAM_B1 = 0.9
ADAM_B2 = 0.999
ADAM_EPS = 1e-08
ADAM_WD = 0.01
ADAM_STEP = 10

TM_PROJ = 512
TM_MIX = 512
TM_FFN = 512
TM_WGRAD = 2048
MIX_SAVED = ("xc", "r", "ig", "a", "m2raw", "ge", "dge")
FFN_ROW_CHUNKS = 2
VMEM_LIMIT = 56 * 1024 * 1024

SLAB_W = 512
ROW_CONV_B, ROW_CONV_W, ROW_BA, ROW_BX, ROW_LAM, ROW_PB, ROW_PS, ROW_GL, ROW_GP = 0, 1, 5, 6, 7, 8, 9, 10, 11
ROW_GA, ROW_GX, ROW_PW = 16, 80, 144
ROW_MIX, ROW_FFN, ROW_FIN, ROW_LOSS = 272, 274, 276, 278
MIX_SLAB_ROWS = 272
SLAB_ROWS = 288


def _cp(sem=None, **kw):
    if sem is not None:
        kw["dimension_semantics"] = sem
    return pltpu.CompilerParams(vmem_limit_bytes=VMEM_LIMIT, **kw)


def _const_spec(shape):
    nd = len(shape)
    return pl.BlockSpec(shape, lambda *_: (0,) * nd, pipeline_mode=pl.Buffered(1))


def _sigmoid(x):
    return 1.0 / (1.0 + jnp.exp(-x))


def _dot(a, b):
    return jnp.dot(a, b, preferred_element_type=F32)


def _dot_nt(a, b):
    return lax.dot_general(a, b, (((1,), (1,)), ((), ())), preferred_element_type=F32)


def _dot_tn(a, b):
    return lax.dot_general(a, b, (((0,), (0,)), ((), ())), preferred_element_type=F32)


def _colsum8(v):
    m, c = v.shape
    return v.reshape(m // SUBLANES, SUBLANES, c).sum(axis=0)


def _rowmean(v):
    return jnp.mean(v, axis=-1, keepdims=True)


def _rms_bwd(dy, xhat, r, g):
    dxh = dy * g
    return r * (dxh - xhat * _rowmean(dxh * xhat))


def _softplus_neg(lam):
    z = -lam
    e = jnp.exp(-jnp.abs(z))
    u = 1.0 + e
    d = u - 1.0
    log1p = jnp.where(d == 0.0, e, jnp.log(u) * (e / jnp.where(d == 0.0, 1.0, d)))
    return jnp.maximum(z, 0.0) + log1p


def _neg_expm1(z):
    series = -(z * (1.0 + z * (0.5 + z * (1.0 / 6.0 + z * (1.0 / 24.0)))))
    return jnp.where(z > -0.03, series, 1.0 - jnp.exp(z))


_GELU_C = math.sqrt(2.0 / math.pi)
_GELU_K = 0.044715


def _gelu_parts(x):
    x2 = x * x
    th = jnp.tanh(_GELU_C * (x + _GELU_K * x2 * x))
    ge = 0.5 * x * (1.0 + th)
    dge = 0.5 * (1.0 + th) + 0.5 * x * (1.0 - th * th) * (_GELU_C * (1.0 + 3.0 * _GELU_K * x2))
    return ge, dge


def _shift_down(halo, tile, k):
    if k == 0:
        return tile
    ext = jnp.concatenate([halo, tile], axis=0)
    n = tile.shape[0]
    h = halo.shape[0]
    return ext[h - k:h - k + n]


def _shift_up(tile, nxt, k):
    if k == 0:
        return tile
    ext = jnp.concatenate([tile, nxt], axis=0)
    return ext[k:k + tile.shape[0]]


def _build_gate_blocks(ga_ref, gx_ref, gw_ref):
    hd = ga_ref.shape[1]
    per = GATE_BLOCK // hd
    zero = jnp.zeros((hd, hd), F32)
    for b in range(gw_ref.shape[0]):
        for src, off in ((ga_ref, 0), (gx_ref, GATE_BLOCK)):
            for hh in range(per):
                row = jnp.concatenate([zero] * hh + [src[b * per + hh]] + [zero] * (per - 1 - hh), axis=1)
                gw_ref[b, hh * hd:(hh + 1) * hd, off:off + GATE_BLOCK] = row.astype(BF16)


def _scan_level1(a, b, reverse):
    m, c = a.shape
    a3 = a.reshape(m // SUBLANES, SUBLANES, c)
    b3 = b.reshape(m // SUBLANES, SUBLANES, c)
    row = lax.broadcasted_iota(jnp.int32, a3.shape, 1)
    for s in (1, 2, 4):
        sh = (SUBLANES - s) if reverse else s
        a_sh = pltpu.roll(a3, sh, 1)
        b_sh = pltpu.roll(b3, sh, 1)
        ok = (row < SUBLANES - s) if reverse else (row >= s)
        b3 = jnp.where(ok, a3 * b_sh + b3, b3)
        a3 = jnp.where(ok, a3 * a_sh, a3)
    return a3.reshape(m, c), b3.reshape(m, c)


def _scan_level2(a_ref, b_ref, out_ref, carry, reverse):
    m, c = a_ref.shape
    ng = m // SUBLANES

    def step(g, cr):
        gi = (ng - 1 - g) if reverse else g
        off = pl.multiple_of(gi * SUBLANES, SUBLANES)
        h = b_ref[pl.ds(off, SUBLANES), :] + a_ref[pl.ds(off, SUBLANES), :] * cr
        out_ref[pl.ds(off, SUBLANES), :] = h
        edge = h[0:1, :] if reverse else h[SUBLANES - 1:SUBLANES, :]
        return jnp.broadcast_to(edge, (SUBLANES, c))

    return lax.fori_loop(0, ng, step, carry, unroll=4)


def _mixer_recompute(u, hal, t0, cw, cb, gw_ref, ba, bx, lam, pw_ref, pb, ps):
    tm = u.shape[0]
    lw = cb.shape[1]
    u_l, u_g, u_p = u[:, :lw], u[:, lw:2 * lw], u[:, 2 * lw:]
    hal_l, hal_p = hal[:, :lw], hal[:, 2 * lw:]
    taps = [_shift_down(hal_l, u_l, CONV_WIDTH - 1 - k) for k in range(CONV_WIDTH)]
    xc = cb
    for k in range(CONV_WIDTH):
        xc = xc + taps[k] * cw[k:k + 1, :]
    xcb = xc.astype(BF16)
    nb = lw // GATE_BLOCK
    gs = [_dot(xcb[:, b * GATE_BLOCK:(b + 1) * GATE_BLOCK], gw_ref[b]) for b in range(nb)]
    r = _sigmoid(jnp.concatenate([g[:, :GATE_BLOCK] for g in gs], axis=1) + ba)
    ig = _sigmoid(jnp.concatenate([g[:, GATE_BLOCK:] for g in gs], axis=1) + bx)
    sp = _softplus_neg(lam)
    la = (-LRU_C * r) * sp
    a = jnp.exp(la)
    m2raw = _neg_expm1(2.0 * la)
    mult = jnp.sqrt(jnp.maximum(m2raw, 1e-12))
    ge, dge = _gelu_parts(u_g)
    row = lax.broadcasted_iota(jnp.int32, (tm, LANES), 0) + t0
    pooled, invs, zs = [], [], []
    for gi, w in enumerate(POOL_WINDOWS):
        e = jnp.concatenate([hal_p[:, gi * LANES:(gi + 1) * LANES], u_p[:, gi * LANES:(gi + 1) * LANES]], axis=0)
        s = e
        k = 1
        while k < w:
            s = s + pltpu.roll(s, k, 0)
            k *= 2
        inv = 1.0 / jnp.minimum(row + 1, w).astype(F32)
        pg = s[HALO:] * inv - e[HALO:]
        pooled.append(pg)
        invs.append(inv)
        zs.append(_dot(pg.astype(BF16), pw_ref[gi].astype(BF16)))
    z = jnp.concatenate(zs, axis=1) + pb
    y_pool = z * ps
    return dict(u_l=u_l, u_g=u_g, taps=taps, xc=xc, xcb=xcb, r=r, ig=ig, sp=sp, la=la, a=a, m2raw=m2raw,
                mult=mult, ge=ge, dge=dge, pooled=pooled, invs=invs, z=z, y_pool=y_pool)


ANY = pl.BlockSpec(memory_space=pl.ANY)
VMEM_SPEC = pl.BlockSpec(memory_space=pltpu.VMEM)


class _Hosted:
    def __init__(self, ins, out_shapes, sems, start, finish, mid=None, aliases=None):
        self.ins, self.out_shapes, self.sems = list(ins), list(out_shapes), list(sems)
        self.start, self.mid, self.finish = start, mid, finish
        self.aliases = dict(aliases or {})


def _call(body, hosted, stage_preds, *, name, grid, in_specs, out_specs, out_shape, scratch_shapes, args, sem):
    hosted = list(hosted or [])
    n_in, n_out, n_scr = len(in_specs), len(out_specs), len(scratch_shapes)
    c_in = [a for h in hosted for a in h.ins]
    c_out = [o for h in hosted for o in h.out_shapes]
    c_sem = [pltpu.SemaphoreType.DMA((k,)) for h in hosted for k in h.sems]

    def full(*refs):
        p = 0
        parts = []
        for cnt in (n_in, len(c_in), n_out, len(c_out), n_scr, len(c_sem)):
            parts.append(refs[p:p + cnt])
            p += cnt
        hi, ci, ho, co, hs, cs = parts
        per = []
        a = b = c_ = 0
        for h in hosted:
            per.append((h, ci[a:a + len(h.ins)], co[b:b + len(h.out_shapes)], cs[c_:c_ + len(h.sems)]))
            a, b, c_ = a + len(h.ins), b + len(h.out_shapes), c_ + len(h.sems)
        first = mid = last = None
        if hosted and grid:
            first, mid, last = stage_preds()

        def run(fn, pred, i_, o_, s_):
            if fn is None:
                return
            if pred is None:
                fn(i_, o_, s_)
            else:
                pl.when(pred)(functools.partial(fn, i_, o_, s_))

        for h, i_, o_, s_ in per:
            run(h.start, first, i_, o_, s_)
        body(*hi, *ho, *hs)
        for h, i_, o_, s_ in per:
            run(h.mid, mid, i_, o_, s_)
        for h, i_, o_, s_ in per:
            run(h.finish, last, i_, o_, s_)

    aliases = {}
    a = b = 0
    for h in hosted:
        for k, v in h.aliases.items():
            aliases[n_in + a + k] = n_out + b + v
        a, b = a + len(h.ins), b + len(h.out_shapes)
    res = pl.pallas_call(
        full, name=name, grid=grid, in_specs=list(in_specs) + [ANY] * len(c_in),
        out_specs=list(out_specs) + [ANY] * len(c_out), out_shape=list(out_shape) + c_out,
        scratch_shapes=list(scratch_shapes) + c_sem, input_output_aliases=aliases,
        compiler_params=_cp(sem))(*args, *c_in)
    res = list(res)
    outs = []
    p = n_out
    for h in hosted:
        outs.append(res[p:p + len(h.out_shapes)])
        p += len(h.out_shapes)
    return res[:n_out], outs


def _inproj(x, g_mix, w_in, hosted=None):
    s, d = x.shape
    n = w_in.shape[1]
    tm = min(TM_PROJ, s)
    nt = s // tm

    def body(x_ref, g_ref, w_ref, u_ref):
        xv = x_ref[...]
        r = lax.rsqrt(_rowmean(xv * xv) + EPS)
        u_ref[...] = _dot((xv * r * g_ref[...]).astype(BF16), w_ref[...])

    def stages():
        i = pl.program_id(0)
        return i == 0, i == max(nt - 3, 0), i == nt - 1

    return _call(
        body, hosted, stages, grid=(nt,), name="inproj",
        in_specs=[pl.BlockSpec((tm, d), lambda i: (i, 0)), _const_spec((1, d)), _const_spec((d, n))],
        out_specs=[pl.BlockSpec((tm, n), lambda i: (i, 0))], out_shape=[SDS((s, n), F32)], scratch_shapes=[],
        args=(x, g_mix, w_in), sem=("arbitrary",))


def _mixer_fwd(u, x, sp_, w_out, hosted=None):
    s, din = u.shape
    d = x.shape[1]
    lw = din // 3
    tm = min(TM_MIX, s)
    nb = lw // GATE_BLOCK

    def body(u_ref, halo_ref, x_ref, cw_ref, cb_ref, ga_ref, gx_ref, ba_ref, bx_ref, lam_ref, pw_ref, pb_ref,
             ps_ref, gl_ref, gp_ref, wout_ref, h_ref, yn_ref, hres_ref, saved_ref, pooled_ref,
             gw_s, a_s, b_s, carry_s):
        i = pl.program_id(0)

        @pl.when(i == 0)
        def _():
            _build_gate_blocks(ga_ref, gx_ref, gw_s)
            carry_s[...] = jnp.zeros_like(carry_s)

        uv = u_ref[...]
        hal = jnp.where(i > 0, halo_ref[...], 0.0)
        f = _mixer_recompute(uv, hal, i * tm, cw_ref[...], cb_ref[...], gw_s, ba_ref[...], bx_ref[...],
                             lam_ref[...], pw_ref, pb_ref[...], ps_ref[...])
        for k, name in enumerate(MIX_SAVED):
            saved_ref[k] = f[name]
        pooled_ref[...] = jnp.concatenate(f["pooled"], axis=1).astype(BF16)
        bb = f["mult"] * (f["ig"] * f["xc"])
        a1, b1 = _scan_level1(f["a"], bb, reverse=False)
        a_s[...] = a1
        b_s[...] = b1
        carry_s[...] = _scan_level2(a_s, b_s, h_ref, carry_s[...], reverse=False)
        y_lru = h_ref[...] * f["ge"]
        rl = lax.rsqrt(_rowmean(y_lru * y_lru) + EPS)
        yp = f["y_pool"]
        rp = lax.rsqrt(_rowmean(yp * yp) + EPS)
        yn = jnp.concatenate([y_lru * rl * gl_ref[...], yp * rp * gp_ref[...]], axis=1).astype(BF16)
        yn_ref[...] = yn
        hres_ref[...] = x_ref[...] + _dot(yn, wout_ref[...])

    small = [sp_[k] for k in ("conv_w", "conv_b", "gate_a_w", "gate_x_w", "gate_a_b", "gate_x_b", "lru_lambda",
                              "pool_w", "pool_b", "pool_scale", "norm_lru_g", "norm_pool_g")]
    nt = s // tm

    def stages():
        i = pl.program_id(0)
        return i == 0, i == max(nt - 3, 0), i == nt - 1

    return _call(
        body, hosted, stages, grid=(nt,), name="mixer_fwd",
        in_specs=[pl.BlockSpec((tm, din), lambda i: (i, 0)),
                  pl.BlockSpec((HALO, din), lambda i: (jnp.maximum(i * (tm // HALO) - 1, 0), 0)),
                  pl.BlockSpec((tm, d), lambda i: (i, 0))]
        + [_const_spec(a.shape) for a in small] + [_const_spec(w_out.shape)],
        out_specs=[pl.BlockSpec((tm, lw), lambda i: (i, 0)), pl.BlockSpec((tm, d), lambda i: (i, 0)),
                   pl.BlockSpec((tm, d), lambda i: (i, 0)),
                   pl.BlockSpec((len(MIX_SAVED), tm, lw), lambda i: (0, i, 0)),
                   pl.BlockSpec((tm, lw), lambda i: (i, 0))],
        out_shape=[SDS((s, lw), F32), SDS((s, d), BF16), SDS((s, d), F32), SDS((len(MIX_SAVED), s, lw), F32),
                   SDS((s, lw), BF16)],
        scratch_shapes=[pltpu.VMEM((nb, GATE_BLOCK, 2 * GATE_BLOCK), BF16), pltpu.VMEM((tm, lw), F32),
                        pltpu.VMEM((tm, lw), F32), pltpu.VMEM((SUBLANES, lw), F32)],
        args=(u, u, x, *small, w_out), sem=("arbitrary",))


def _row_chunks(tm):
    rc = tm // FFN_ROW_CHUNKS
    return [slice(q * rc, (q + 1) * rc) for q in range(FFN_ROW_CHUNKS)]


def _ffn_up(hres1, g_ffn, w1, w3):
    s, d = hres1.shape
    nj, fc, _ = w1.shape
    tm = min(TM_FFN, s)

    def body(h_ref, gf_ref, w1_ref, w3_ref, h2_ref, a1_ref, a3_ref, ff_ref):
        hv = h_ref[...]
        r = lax.rsqrt(_rowmean(hv * hv) + EPS)
        h2_ref[...] = (hv * r * gf_ref[...]).astype(BF16)
        h2 = h2_ref[...]
        for j in range(nj):
            a1 = _dot_nt(h2, w1_ref[j])
            a3 = _dot_nt(h2, w3_ref[j])
            a1_ref[j] = a1.astype(BF16)
            a3_ref[j] = a3.astype(BF16)
            ff_ref[j] = ((a1 * _sigmoid(a1)) * a3).astype(BF16)

    wspec = _const_spec(w1.shape)
    aspec = pl.BlockSpec((nj, tm, fc), lambda i: (0, i, 0))
    return pl.pallas_call(
        body, grid=(s // tm,), name="ffn_up",
        in_specs=[pl.BlockSpec((tm, d), lambda i: (i, 0)), _const_spec((1, d)), wspec, wspec],
        out_specs=[pl.BlockSpec((tm, d), lambda i: (i, 0)), aspec, aspec, aspec],
        out_shape=[SDS((s, d), BF16)] + [SDS((nj, s, fc), BF16)] * 3,
        compiler_params=_cp(("parallel",)))(hres1, g_ffn, w1, w3)


def _ffn_down(ff, hres1, target, g_fin, w2):
    s, d = hres1.shape
    nj, _, fc = ff.shape
    tm = min(TM_FFN, s)

    def body(ff_ref, h_ref, t_ref, gn_ref, w2_ref, dh_ref, dhb_ref, loss_ref, dgn_ref):
        @pl.when(pl.program_id(0) == 0)
        def _():
            loss_ref[...] = jnp.zeros_like(loss_ref)
            dgn_ref[...] = jnp.zeros_like(dgn_ref)

        gn = gn_ref[...]
        for rows in _row_chunks(tm):
            acc = _dot(ff_ref[0, rows, :], w2_ref[0])
            for j in range(1, nj):
                acc = acc + _dot(ff_ref[j, rows, :], w2_ref[j])
            hr2 = h_ref[rows, :] + acc
            r2 = lax.rsqrt(_rowmean(hr2 * hr2) + EPS)
            xh = hr2 * r2
            diff = xh * gn - t_ref[rows, :]
            tot = jnp.sum(jnp.sum(diff * diff, axis=1, keepdims=True), axis=0, keepdims=True)
            loss_ref[...] += tot * (0.5 / d)
            dout = diff * (1.0 / d)
            dgn_ref[...] += _colsum8(dout * xh)
            dh = _rms_bwd(dout, xh, r2, gn)
            dh_ref[rows, :] = dh
            dhb_ref[rows, :] = dh.astype(BF16)

    tile = pl.BlockSpec((tm, d), lambda i: (i, 0))
    return pl.pallas_call(
        body, grid=(s // tm,), name="ffn_down",
        in_specs=[pl.BlockSpec((nj, tm, fc), lambda i: (0, i, 0)), tile, tile, _const_spec((1, d)),
                  _const_spec(w2.shape)],
        out_specs=[tile, tile, pl.BlockSpec((SUBLANES, LANES), lambda i: (0, 0)),
                   pl.BlockSpec((SUBLANES, d), lambda i: (0, 0))],
        out_shape=[SDS((s, d), F32), SDS((s, d), BF16), SDS((SUBLANES, LANES), F32), SDS((SUBLANES, d), F32)],
        compiler_params=_cp(("arbitrary",)))(ff, hres1, target, g_fin, w2)


def _ffn_bwd_gate(dhb, a1, a3, w2):
    s, d = dhb.shape
    nj, _, fc = a1.shape
    tm = min(TM_FFN, s)

    def body(dhb_ref, a1_ref, a3_ref, w2_ref, da1_ref, da3_ref):
        for j in range(nj):
            for rows in _row_chunks(tm):
                dff = _dot_nt(dhb_ref[rows, :], w2_ref[j])
                a1v = a1_ref[j, rows, :].astype(F32)
                sg = _sigmoid(a1v)
                silu = a1v * sg
                da1_ref[j, rows, :] = (dff * a3_ref[j, rows, :].astype(F32)
                                       * (sg * (1.0 + (a1v - silu)))).astype(BF16)
                da3_ref[j, rows, :] = (dff * silu).astype(BF16)

    aspec = pl.BlockSpec((nj, tm, fc), lambda i: (0, i, 0))
    return pl.pallas_call(
        body, grid=(s // tm,), name="ffn_bwd_gate",
        in_specs=[pl.BlockSpec((tm, d), lambda i: (i, 0)), aspec, aspec, _const_spec(w2.shape)],
        out_specs=[aspec, aspec], out_shape=[SDS((nj, s, fc), BF16)] * 2,
        compiler_params=_cp(("parallel",)))(dhb, a1, a3, w2)


def _ffn_bwd_down(da1, da3, dh, hres1, g_ffn, w1, w3, hosted=None):
    s, d = hres1.shape
    nj, _, fc = da1.shape
    tm = min(TM_FFN, s)
    nt = s // tm

    def body(da1_ref, da3_ref, dh_ref, h_ref, gf_ref, w1_ref, w3_ref, dhr_ref, dgf_ref):
        @pl.when(pl.program_id(0) == 0)
        def _():
            dgf_ref[...] = jnp.zeros_like(dgf_ref)

        gf = gf_ref[...]
        for rows in _row_chunks(tm):
            dh2 = None
            for j in range(nj):
                part = _dot(da1_ref[j, rows, :], w1_ref[j]) + _dot(da3_ref[j, rows, :], w3_ref[j])
                dh2 = part if dh2 is None else dh2 + part
            hv = h_ref[rows, :]
            r = lax.rsqrt(_rowmean(hv * hv) + EPS)
            xh = hv * r
            dgf_ref[...] += _colsum8(dh2 * xh)
            dhr_ref[rows, :] = dh_ref[rows, :] + _rms_bwd(dh2, xh, r, gf)

    tile = pl.BlockSpec((tm, d), lambda i: (i, 0))
    aspec = pl.BlockSpec((nj, tm, fc), lambda i: (0, i, 0))
    wspec = _const_spec(w1.shape)

    def stages():
        i = pl.program_id(0)
        return i == 0, i == max(nt - 2, 0), i == nt - 1

    return _call(
        body, hosted, stages, grid=(nt,), name="ffn_bwd_down",
        in_specs=[aspec, aspec, tile, tile, _const_spec((1, d)), wspec, wspec],
        out_specs=[tile, pl.BlockSpec((SUBLANES, d), lambda i: (0, 0))],
        out_shape=[SDS((s, d), F32), SDS((SUBLANES, d), F32)],
        scratch_shapes=[], args=(da1, da3, dh, hres1, g_ffn, w1, w3), sem=("arbitrary",))


def _ffn_wgrad(h2, dhb, ff, da1, da3):
    s, d = h2.shape
    _, _, fc = ff.shape
    tm = min(TM_WGRAD, s)

    def body(h2_ref, dhb_ref, ff_ref, da1_ref, da3_ref, dw1_ref, dw3_ref, dw2_ref):
        @pl.when(pl.program_id(1) == 0)
        def _():
            dw1_ref[...] = jnp.zeros_like(dw1_ref)
            dw3_ref[...] = jnp.zeros_like(dw3_ref)
            dw2_ref[...] = jnp.zeros_like(dw2_ref)

        h2v = h2_ref[...]
        dw1_ref[0] += _dot_tn(da1_ref[0], h2v)
        dw3_ref[0] += _dot_tn(da3_ref[0], h2v)
        dw2_ref[0] += _dot_tn(ff_ref[0], dhb_ref[...])

    wspec = pl.BlockSpec((1, fc, d), lambda j, i: (j, 0, 0))
    return pl.pallas_call(
        body, grid=(N_CHIPS, s // tm), name="ffn_wgrad",
        in_specs=[pl.BlockSpec((tm, d), lambda j, i: (i, 0)), pl.BlockSpec((tm, d), lambda j, i: (i, 0))]
        + [pl.BlockSpec((1, tm, fc), lambda j, i: (j, i, 0))] * 3,
        out_specs=[wspec] * 3, out_shape=[SDS((N_CHIPS, fc, d), F32)] * 3,
        compiler_params=_cp(("parallel", "arbitrary")))(h2, dhb, ff, da1, da3)


def _mixer_bwd(u, saved, pooled, h, dhres1, sp_, w_out, hosted=None):
    s, din = u.shape
    d = dhres1.shape[1]
    lw = din // 3
    tm = min(TM_MIX, s)
    nt = s // tm
    nb = lw // GATE_BLOCK
    hd = sp_["gate_a_w"].shape[1]

    def body(ul_ref, saved_ref, pooled_ref, h_ref, hhalo_ref, dhr_ref, cw_ref, cb_ref, ga_ref, gx_ref, ba_ref,
             bx_ref, lam_ref, pw_ref, pb_ref, ps_ref, gl_ref, gp_ref, wout_ref, du_ref, slab_ref,
             gw_s, a_s, b_s, e_s, ecarry_s, dxc_s, q_s, vec_s, cwacc_s, dgw_s, dpw_s):
        i = pl.program_id(0)
        tile = nt - 1 - i

        @pl.when(i == 0)
        def _():
            _build_gate_blocks(ga_ref, gx_ref, gw_s)
            for ref in (ecarry_s, dxc_s, q_s, vec_s, cwacc_s, dgw_s, dpw_s):
                ref[...] = jnp.zeros_like(ref)

        cw = cw_ref[...]
        lam = lam_ref[...]
        ps = ps_ref[...]
        f = {name: saved_ref[k] for k, name in enumerate(MIX_SAVED)}
        f["mult"] = jnp.sqrt(jnp.maximum(f["m2raw"], 1e-12))
        f["sp"] = _softplus_neg(lam)
        f["xcb"] = f["xc"].astype(BF16)
        pooled = pooled_ref[...]
        row = lax.broadcasted_iota(jnp.int32, (tm, LANES), 0) + tile * tm
        f["invs"] = [1.0 / jnp.minimum(row + 1, w).astype(F32) for w in POOL_WINDOWS]
        f["z"] = jnp.concatenate(
            [_dot(pooled[:, g * LANES:(g + 1) * LANES], pw_ref[g].astype(BF16))
             for g in range(len(POOL_WINDOWS))], axis=1) + pb_ref[...]
        f["y_pool"] = f["z"] * ps
        u_l = ul_ref[...]
        hv = h_ref[...]
        h_prev = _shift_down(jnp.where(tile > 0, hhalo_ref[...], 0.0), hv, 1)
        y_lru = hv * f["ge"]
        rl = lax.rsqrt(_rowmean(y_lru * y_lru) + EPS)
        yp = f["y_pool"]
        rp = lax.rsqrt(_rowmean(yp * yp) + EPS)
        xh_l = y_lru * rl
        xh_p = yp * rp

        dyn = _dot_nt(dhr_ref[...].astype(BF16), wout_ref[...])
        d_nl, d_np = dyn[:, :lw], dyn[:, lw:]
        vec = {}
        vec[ROW_GL] = _colsum8(d_nl * xh_l)
        vec[ROW_GP] = _colsum8(d_np * xh_p)
        d_ylru = _rms_bwd(d_nl, xh_l, rl, gl_ref[...])
        d_ypool = _rms_bwd(d_np, xh_p, rp, gp_ref[...])

        vec[ROW_PS] = _colsum8(d_ypool * f["z"])
        dz = d_ypool * ps
        vec[ROW_PB] = _colsum8(dz)
        dzb = dz.astype(BF16)
        dup = []
        for gi, w in enumerate(POOL_WINDOWS):
            sl = slice(gi * LANES, (gi + 1) * LANES)
            dpw_s[:, sl] += _dot_tn(pooled[:, sl], dzb[:, sl])
            dpool = _dot_nt(dzb[:, sl], pw_ref[gi].astype(BF16))
            q = dpool * f["invs"][gi]
            e = jnp.concatenate([q, q_s[:, sl]], axis=0)
            k = 1
            while k < w:
                e = e + pltpu.roll(e, tm + HALO - k, 0)
                k *= 2
            dup.append(e[:tm] - dpool)
            q_s[:, sl] = q[:HALO]

        d_hout = d_ylru * f["ge"]
        d_ug = d_ylru * hv * f["dge"]
        a = f["a"]
        a1, b1 = _scan_level1(a, a * d_hout, reverse=True)
        a_s[...] = a1
        b_s[...] = b1
        e_next = ecarry_s[...]
        ecarry_s[...] = _scan_level2(a_s, b_s, e_s, e_next, reverse=True)
        sv = d_hout + _shift_up(e_s[...], e_next, 1)
        d_a = sv * h_prev
        mult, ig, xc, r = f["mult"], f["ig"], f["xc"], f["r"]
        d_mult = sv * (ig * xc)
        d_ig = sv * mult * xc
        d_xc = sv * mult * ig
        d_la = d_a * a + jnp.where(f["m2raw"] > 1e-12, d_mult * (-(a * a) / mult), 0.0)
        d_r = d_la * (-LRU_C * f["sp"])
        vec[ROW_LAM] = _colsum8(d_la * (-LRU_C * r))
        d_pr = d_r * r * (1.0 - r)
        d_pi = d_ig * ig * (1.0 - ig)
        vec[ROW_BA] = _colsum8(d_pr)
        vec[ROW_BX] = _colsum8(d_pi)
        dxc_parts = []
        for b in range(nb):
            sl = slice(b * GATE_BLOCK, (b + 1) * GATE_BLOCK)
            rhs = jnp.concatenate([d_pr[:, sl], d_pi[:, sl]], axis=1).astype(BF16)
            dgw_s[b] += _dot_tn(f["xcb"][:, sl], rhs)
            dxc_parts.append(_dot_nt(rhs, gw_s[b]))
        d_xc = d_xc + jnp.concatenate(dxc_parts, axis=1)
        vec[ROW_CONV_B] = _colsum8(d_xc)
        dxc_next = dxc_s[...]
        d_ul = None
        for k in range(CONV_WIDTH):
            ahead = _shift_up(d_xc, dxc_next, CONV_WIDTH - 1 - k)
            cwacc_s[k * SUBLANES:(k + 1) * SUBLANES, :] += _colsum8(ahead * u_l)
            term = ahead * cw[k:k + 1, :]
            d_ul = term if d_ul is None else d_ul + term
        dxc_s[...] = d_xc[:SUBLANES]
        for row, val in vec.items():
            vec_s[row * SUBLANES:(row + 1) * SUBLANES, :] += val
        du_ref[...] = jnp.concatenate([d_ul, d_ug] + dup, axis=1).astype(BF16)

        @pl.when(i == nt - 1)
        def _():
            rows = []
            for row in range(ROW_GA):
                if row in (ROW_CONV_W, ROW_CONV_W + 1, ROW_CONV_W + 2, ROW_CONV_W + 3):
                    k = row - ROW_CONV_W
                    v = jnp.sum(cwacc_s[k * SUBLANES:(k + 1) * SUBLANES, :], axis=0, keepdims=True)
                elif row <= ROW_GP:
                    v = jnp.sum(vec_s[row * SUBLANES:(row + 1) * SUBLANES, :], axis=0, keepdims=True)
                    if row == ROW_LAM:
                        v = v * (-1.0 / (1.0 + jnp.exp(lam)))
                else:
                    v = jnp.zeros((1, lw), F32)
                rows.append(v)
            slab_ref[0:ROW_GA, :] = jnp.concatenate(rows, axis=0)
            lane = lax.broadcasted_iota(jnp.int32, (hd, GATE_BLOCK), 1)
            for b in range(nb):
                for off, row0 in ((0, ROW_GA), (GATE_BLOCK, ROW_GX)):
                    acc = jnp.zeros((hd, GATE_BLOCK), F32)
                    for hh in range(GATE_BLOCK // hd):
                        m = (lane >= hh * hd) & (lane < (hh + 1) * hd)
                        acc = acc + jnp.where(m, dgw_s[b, hh * hd:(hh + 1) * hd, off:off + GATE_BLOCK], 0.0)
                    slab_ref[row0:row0 + hd, b * GATE_BLOCK:(b + 1) * GATE_BLOCK] = acc
            slab_ref[ROW_PW:ROW_PW + LANES, :] = dpw_s[...]

    small = [sp_[k] for k in ("conv_w", "conv_b", "gate_a_w", "gate_x_w", "gate_a_b", "gate_x_b", "lru_lambda",
                              "pool_w", "pool_b", "pool_scale", "norm_lru_g", "norm_pool_g")]
    rev = lambda i: nt - 1 - i

    def stages():
        i = pl.program_id(0)
        return i == 0, i == max(nt - 3, 0), i == nt - 1

    return _call(
        body, hosted, stages, grid=(nt,), name="mixer_bwd",
        in_specs=[pl.BlockSpec((tm, lw), lambda i: (rev(i), 0)),
                  pl.BlockSpec((len(MIX_SAVED), tm, lw), lambda i: (0, rev(i), 0)),
                  pl.BlockSpec((tm, lw), lambda i: (rev(i), 0)),
                  pl.BlockSpec((tm, lw), lambda i: (rev(i), 0)),
                  pl.BlockSpec((SUBLANES, lw), lambda i: (jnp.maximum(rev(i) * (tm // SUBLANES) - 1, 0), 0)),
                  pl.BlockSpec((tm, d), lambda i: (rev(i), 0))]
        + [_const_spec(a.shape) for a in small] + [_const_spec(w_out.shape)],
        out_specs=[pl.BlockSpec((tm, din), lambda i: (rev(i), 0)),
                   pl.BlockSpec((MIX_SLAB_ROWS, SLAB_W), lambda i: (0, 0))],
        out_shape=[SDS((s, din), BF16), SDS((MIX_SLAB_ROWS, SLAB_W), F32)],
        scratch_shapes=[pltpu.VMEM((nb, GATE_BLOCK, 2 * GATE_BLOCK), BF16),
                        pltpu.VMEM((tm, lw), F32), pltpu.VMEM((tm, lw), F32), pltpu.VMEM((tm, lw), F32),
                        pltpu.VMEM((SUBLANES, lw), F32), pltpu.VMEM((SUBLANES, lw), F32),
                        pltpu.VMEM((HALO, lw), F32), pltpu.VMEM((ROW_GA * SUBLANES, lw), F32),
                        pltpu.VMEM((CONV_WIDTH * SUBLANES, lw), F32),
                        pltpu.VMEM((nb, GATE_BLOCK, 2 * GATE_BLOCK), F32), pltpu.VMEM((LANES, lw), F32)],
        args=(u, saved, pooled, h, h, dhres1, *small, w_out), sem=("arbitrary",))


def _inproj_bwd(x, du, dhres1, yn, g_mix, w_in, hosted=None):
    s, d = x.shape
    n = w_in.shape[1]
    nc = n // N_CHIPS
    tm = min(TM_PROJ, s)
    nt = s // tm

    def body(x_ref, du_ref, dhr_ref, yn_ref, g_ref, w_ref, gx_ref, dwin_ref, dwout_ref, dg_ref):
        i = pl.program_id(0)

        @pl.when(i == 0)
        def _():
            dwin_ref[...] = jnp.zeros_like(dwin_ref)
            dwout_ref[...] = jnp.zeros_like(dwout_ref)
            dg_ref[...] = jnp.zeros_like(dg_ref)

        xv = x_ref[...]
        g = g_ref[...]
        r = lax.rsqrt(_rowmean(xv * xv) + EPS)
        xh = xv * r
        h1 = (xh * g).astype(BF16)
        duv = du_ref[...]
        dh1 = _dot_nt(duv, w_ref[...])
        dg_ref[...] += _colsum8(dh1 * xh)
        dhr = dhr_ref[...]
        gx_ref[...] = dhr + _rms_bwd(dh1, xh, r, g)
        for jj in range(N_CHIPS):
            dwin_ref[jj] += _dot_tn(h1, duv[:, jj * nc:(jj + 1) * nc])
        dwout_ref[...] += _dot_tn(yn_ref[...], dhr.astype(BF16))

    def stages():
        i = pl.program_id(0)
        return i == 0, i == max(nt - 3, 0), i == nt - 1

    return _call(
        body, hosted, stages, grid=(nt,), name="inproj_bwd",
        in_specs=[pl.BlockSpec((tm, d), lambda i: (i, 0)), pl.BlockSpec((tm, n), lambda i: (i, 0)),
                  pl.BlockSpec((tm, d), lambda i: (i, 0)), pl.BlockSpec((tm, d), lambda i: (i, 0)),
                  _const_spec((1, d)), _const_spec((d, n))],
        out_specs=[pl.BlockSpec((tm, d), lambda i: (i, 0)), pl.BlockSpec((N_CHIPS, d, nc), lambda i: (0, 0, 0)),
                   pl.BlockSpec((d, d), lambda i: (0, 0)), pl.BlockSpec((SUBLANES, d), lambda i: (0, 0))],
        out_shape=[SDS((s, d), F32), SDS((N_CHIPS, d, nc), F32), SDS((d, d), F32), SDS((SUBLANES, d), F32)],
        scratch_shapes=[], args=(x, du, dhres1, yn, g_mix, w_in), sem=("arbitrary",))


def _place():
    x, y, c = lax.axis_index("x"), lax.axis_index("y"), lax.axis_index("c")
    return x, y, c


def _other_chips(x, y):
    return [(1 - x, y), (x, 1 - y), (1 - x, 1 - y)]


ANY = pl.BlockSpec(memory_space=pl.ANY)
VMEM_SPEC = pl.BlockSpec(memory_space=pltpu.VMEM)

_GATHERED = {"w_in": "cols", "w_out": "major", "ffn_w1": "major", "ffn_w3": "major", "ffn_w2": "major"}
_BIG = ("w_in", "w_out", "ffn_w1", "ffn_w3", "ffn_w2")


def _gather_weights(shards, conv_w, n_remote):
    n = len(shards)
    full_shapes = []
    for name, sh in zip(_BIG, shards):
        r, cdim = sh.shape
        if _GATHERED[name] == "cols":
            assert cdim % LANES == 0
            full_shapes.append((r, cdim * N_CHIPS))
        else:
            full_shapes.append((N_CHIPS, r, cdim))

    def region(ref, name, sh, jj, cc):
        r, cdim = sh
        rows = pl.ds(0, r) if cc is None else pl.ds(pl.multiple_of(cc * (r // 2), 16), r // 2)
        if _GATHERED[name] == "cols":
            return ref.at[rows, pl.ds(pl.multiple_of(jj * cdim, LANES), cdim)]
        return ref.at[jj, rows, :]

    def staged(ref, sh, cc):
        r = sh[0]
        return ref.at[pl.ds(pl.multiple_of(cc * (r // 2), 16), r // 2), :]

    def body(*refs):
        ins, cw_in = refs[:n], refs[n]
        outs, cw_out = refs[n + 1:2 * n + 1], refs[2 * n + 1]
        stage = refs[2 * n + 2:3 * n + 2]
        cw_stage, lsem, ssem, rsem, fssem, frsem, cssem, crsem = refs[3 * n + 2:]
        x, y, c = _place()
        j = 2 * x + y
        chips = _other_chips(x, y)
        for w in range(n_remote):
            stage[w][...] = ins[w][...].astype(BF16)
        cw_stage[...] = jnp.zeros_like(cw_stage)
        cw_stage[0:CONV_WIDTH, :] = cw_in[...]
        shs = [s_.shape for s_ in shards]
        local = [pltpu.make_async_copy(stage[w], region(outs[w], _BIG[w], shs[w], j, None), lsem.at[w])
                 for w in range(n)]
        local.append(pltpu.make_async_copy(cw_stage, cw_out.at[j], lsem.at[n]))
        sends = []
        for k, (px, py) in enumerate(chips):
            for w in range(n_remote):
                sends.append(pltpu.make_async_remote_copy(
                    src_ref=staged(stage[w], shs[w], c), dst_ref=region(outs[w], _BIG[w], shs[w], j, c),
                    send_sem=ssem.at[k * n + w], recv_sem=rsem.at[k * n + w], device_id=(px, py, c),
                    device_id_type=MESH))
            sends.append(pltpu.make_async_remote_copy(
                src_ref=cw_stage, dst_ref=cw_out.at[j], send_sem=cssem.at[k], recv_sem=crsem.at[k],
                device_id=(px, py, c), device_id_type=MESH))
        for cp in sends:
            cp.start()
        for w in range(n_remote, n):
            stage[w][...] = ins[w][...].astype(BF16)
        for cp in local:
            cp.start()
        fwd = []
        for k, (px, py) in enumerate(chips):
            jk = 2 * px + py
            for w in range(n_remote):
                reg = region(outs[w], _BIG[w], shs[w], jk, c)
                pltpu.make_async_remote_copy(src_ref=reg, dst_ref=reg, send_sem=ssem.at[k * n + w],
                                             recv_sem=rsem.at[k * n + w], device_id=(px, py, c),
                                             device_id_type=MESH).wait_recv()
                cp = pltpu.make_async_remote_copy(src_ref=reg, dst_ref=reg, send_sem=fssem.at[k * n + w],
                                                  recv_sem=frsem.at[k * n + w], device_id=(x, y, 1 - c),
                                                  device_id_type=MESH)
                cp.start()
                fwd.append(cp)
            pltpu.make_async_remote_copy(src_ref=cw_stage, dst_ref=cw_out.at[jk], send_sem=cssem.at[k],
                                         recv_sem=crsem.at[k], device_id=(px, py, c),
                                         device_id_type=MESH).wait_recv()
        for k, (px, py) in enumerate(chips):
            jk = 2 * px + py
            for w in range(n_remote):
                reg = region(outs[w], _BIG[w], shs[w], jk, 1 - c)
                pltpu.make_async_remote_copy(src_ref=reg, dst_ref=reg, send_sem=fssem.at[k * n + w],
                                             recv_sem=frsem.at[k * n + w], device_id=(x, y, 1 - c),
                                             device_id_type=MESH).wait_recv()
        for cp in sends + fwd:
            cp.wait_send()
        for cp in local:
            cp.wait()

    nsem = 3 * n
    return pl.pallas_call(
        body, name="gather_first",
        in_specs=[VMEM_SPEC] * (n + 1), out_specs=[ANY] * (n + 1),
        out_shape=[SDS(fs, BF16) for fs in full_shapes] + [SDS((N_CHIPS, SUBLANES, LANES), F32)],
        scratch_shapes=[pltpu.VMEM(s_.shape, BF16) for s_ in shards] + [pltpu.VMEM((SUBLANES, LANES), F32)]
        + [pltpu.SemaphoreType.DMA((n + 1,))] + [pltpu.SemaphoreType.DMA((nsem,))] * 4
        + [pltpu.SemaphoreType.DMA((3,))] * 2,
        compiler_params=_cp())(*shards, conv_w)


def _start_all(make):
    def f(ins, outs, sems):
        for cp in make(ins, outs, sems):
            cp.start()
    return f


def _wait_all(make):
    def f(ins, outs, sems):
        for cp in make(ins, outs, sems):
            cp.wait()
    return f


def _ffn_gather_hosted(arrs):
    n = len(arrs)

    def make(outs, sems):
        ssem, rsem, fs, fr = sems
        x, y, c = _place()
        j = 2 * x + y

        def reg(w, jj, cc):
            hr = arrs[w].shape[1] // 2
            return outs[w].at[jj, pl.ds(pl.multiple_of(cc * hr, 16), hr), :]

        def rc(w, jj, cc, s_sem, r_sem, dev):
            return pltpu.make_async_remote_copy(src_ref=reg(w, jj, cc), dst_ref=reg(w, jj, cc), send_sem=s_sem,
                                                recv_sem=r_sem, device_id=dev, device_id_type=MESH)

        sends, recvs, fwds, frecvs = [], [], [], []
        for k, (px, py) in enumerate(_other_chips(x, y)):
            jk = 2 * px + py
            for w in range(n):
                q = k * n + w
                sends.append(rc(w, j, c, ssem.at[q], rsem.at[q], (px, py, c)))
                recvs.append(rc(w, jk, c, ssem.at[q], rsem.at[q], (px, py, c)))
                fwds.append(rc(w, jk, c, fs.at[q], fr.at[q], (x, y, 1 - c)))
                frecvs.append(rc(w, jk, 1 - c, fs.at[q], fr.at[q], (x, y, 1 - c)))
        return sends, recvs, fwds, frecvs

    def start(ins, outs, sems):
        for cp in make(outs, sems)[0]:
            cp.start()

    def mid(ins, outs, sems):
        _, recvs, fwds, _ = make(outs, sems)
        for r, f in zip(recvs, fwds):
            r.wait_recv()
            f.start()

    def finish(ins, outs, sems):
        sends, _, fwds, frecvs = make(outs, sems)
        for r in frecvs:
            r.wait_recv()
        for cp in sends + fwds:
            cp.wait_send()

    return _Hosted(arrs, [SDS(a.shape, a.dtype) for a in arrs], [3 * n] * 4, start, finish, mid=mid,
                   aliases={w: w for w in range(n)})


def _rs_sibling_hosted(arrs):
    n = len(arrs)

    def make(ins, outs, sems):
        x, y, c = _place()
        cps = []
        for w in range(n):
            hr = arrs[w].shape[1] // 2
            src = ins[w].at[:, pl.ds(pl.multiple_of((1 - c) * hr, SUBLANES), hr), :]
            cps.append(pltpu.make_async_remote_copy(src_ref=src, dst_ref=outs[w], send_sem=sems[0].at[w],
                                                    recv_sem=sems[1].at[w], device_id=(x, y, 1 - c),
                                                    device_id_type=MESH))
        return cps

    return _Hosted(arrs, [SDS((a.shape[0], a.shape[1] // 2, a.shape[2]), F32) for a in arrs], [n, n],
                   _start_all(make), _wait_all(make))


def _rs_chips_hosted(parts):
    n = len(parts)

    def make(ins, outs, sems):
        x, y, c = _place()
        j = 2 * x + y
        cps = []
        for k, (px, py) in enumerate(_other_chips(x, y)):
            jk = 2 * px + py
            for w in range(n):
                cps.append(pltpu.make_async_remote_copy(
                    src_ref=ins[w].at[jk], dst_ref=outs[w].at[j], send_sem=sems[0].at[k * n + w],
                    recv_sem=sems[1].at[k * n + w], device_id=(px, py, c), device_id_type=MESH))
        return cps

    return _Hosted(parts, [SDS(p.shape, p.dtype) for p in parts], [3 * n, 3 * n], _start_all(make), _wait_all(make))


def _rs_swap_hosted(halves):
    n = len(halves)

    def make(ins, outs, sems):
        x, y, c = _place()
        return [pltpu.make_async_remote_copy(src_ref=ins[w], dst_ref=outs[w], send_sem=sems[0].at[w],
                                             recv_sem=sems[1].at[w], device_id=(x, y, 1 - c), device_id_type=MESH)
                for w in range(n)]

    return _Hosted(halves, [SDS(h.shape, F32) for h in halves], [n, n], _start_all(make), _wait_all(make))


HBM_SPEC = pl.BlockSpec(memory_space=pltpu.HBM)
SEM_SPEC = pl.BlockSpec(memory_space=pltpu.SEMAPHORE)
_EFFECT = pltpu.SideEffectType.DATAFLOW_SIDE_EFFECTING


def _split_start(h, name):
    n_in, n_out, ns = len(h.ins), len(h.out_shapes), len(h.sems)
    ins = [pltpu.with_memory_space_constraint(a, pltpu.HBM) for a in h.ins]
    lands = [pltpu.with_memory_space_constraint(lax.empty(o.shape, o.dtype), pltpu.HBM) for o in h.out_shapes]

    def body(*refs):
        i_refs, l_refs = refs[:n_in], refs[n_in:n_in + n_out]
        s_refs = refs[n_in + n_out:n_in + n_out + ns]
        token = refs[-1]
        h.start(i_refs, l_refs, s_refs)
        token[...] = jnp.zeros_like(token)

    res = pl.pallas_call(
        body, name=name, in_specs=[HBM_SPEC] * (n_in + n_out),
        out_specs=[SEM_SPEC] * ns + [HBM_SPEC] * (n_in + n_out) + [VMEM_SPEC],
        out_shape=[pltpu.SemaphoreType.DMA((k,)) for k in h.sems]
        + [pltpu.HBM(a.shape, a.dtype) for a in h.ins] + [pltpu.HBM(o.shape, o.dtype) for o in h.out_shapes]
        + [SDS((SUBLANES, LANES), F32)],
        input_output_aliases={k: ns + k for k in range(n_in + n_out)},
        compiler_params=pltpu.CompilerParams(has_side_effects=_EFFECT))(*ins, *lands)
    return list(res[:-1]), res[-1]


def _split_wait(h, state, after, name):
    n_in, n_out, ns = len(h.ins), len(h.out_shapes), len(h.sems)
    sems, bufs = state[:ns], state[ns:]

    def body(*refs):
        i_refs, l_refs = refs[:n_in], refs[n_in:n_in + n_out]
        s_refs = refs[n_in + n_out:n_in + n_out + ns]
        h.finish(i_refs, l_refs, s_refs)

    res = pl.pallas_call(
        body, name=name, in_specs=[HBM_SPEC] * (n_in + n_out) + [SEM_SPEC] * ns + [ANY],
        out_specs=[HBM_SPEC] * (n_in + n_out),
        out_shape=[pltpu.HBM(b.shape, b.dtype) for b in bufs],
        input_output_aliases={k: k for k in range(n_in + n_out)},
        compiler_params=pltpu.CompilerParams(has_side_effects=_EFFECT))(*bufs, *sems, after)
    return list(res[n_in:])


def _run_comm(hosted, name):
    return _call(lambda: None, hosted, None, name=name, grid=(), in_specs=[], out_specs=[], out_shape=[],
                 scratch_shapes=[], args=(), sem=None)[1]


def _row_tile(rows, cols, n_arrays):
    budget = 24 * 1024 * 1024 // (2 * 4 * n_arrays * cols)
    best = SUBLANES
    for t in range(SUBLANES, rows + 1, SUBLANES):
        if rows % t == 0 and t <= budget:
            best = t
    return best


def _place_index(which):
    x, y, c = _place()
    v = c if which == "c" else 2 * x + y
    return jnp.reshape(v, (1,)).astype(jnp.int32)


def _add_own_half(full, recv, name, wire=BF16):
    nsh, rows, cols = full.shape
    hr = rows // 2
    t = _row_tile(hr, cols, 4)
    nt = hr // t

    def body(c_ref, a_ref, b_ref, o_ref, ob_ref):
        v = a_ref[...] + b_ref[...]
        o_ref[...] = v
        ob_ref[...] = v.astype(wire)

    half = pl.BlockSpec((1, t, cols), lambda s_, i, c_ref: (s_, i, 0))
    return pl.pallas_call(
        body, name=name,
        grid_spec=pltpu.PrefetchScalarGridSpec(
            num_scalar_prefetch=1, grid=(nsh, nt),
            in_specs=[pl.BlockSpec((1, t, cols), lambda s_, i, c_ref: (s_, c_ref[0] * nt + i, 0)), half],
            out_specs=[half, half]),
        out_shape=[SDS((nsh, hr, cols), F32), SDS((nsh, hr, cols), wire)],
        compiler_params=_cp(("parallel", "parallel")))(_place_index("c"), full, recv)


def _sum_chips(own, recv, name):
    nsh, hr, cols = own.shape
    t = _row_tile(hr, cols, 6)

    def body(j_ref, own_ref, *rest):
        r_refs, o_ref = rest[:nsh], rest[nsh]
        j = j_ref[0]
        mine = own_ref[0]
        parts = [jnp.where(j == k, mine, r_refs[k][0].astype(F32)) for k in range(nsh)]
        o_ref[...] = ((parts[0] + parts[1]) + parts[2]) + parts[3]

    def other(k):
        return pl.BlockSpec((1, t, cols), lambda i, j_ref: (jnp.where(j_ref[0] == k, (k + 1) % nsh, k), i, 0))

    return pl.pallas_call(
        body, name=name,
        grid_spec=pltpu.PrefetchScalarGridSpec(
            num_scalar_prefetch=1, grid=(hr // t,),
            in_specs=[pl.BlockSpec((1, t, cols), lambda i, j_ref: (j_ref[0], i, 0))]
            + [other(k) for k in range(nsh)],
            out_specs=pl.BlockSpec((t, cols), lambda i, j_ref: (i, 0))),
        out_shape=SDS((hr, cols), F32), compiler_params=_cp(("parallel",)))(_place_index("j"), own, *([recv] * nsh))


def _adamw_math(w, g, m, v):
    m = ADAM_B1 * m + (1.0 - ADAM_B1) * g
    v = ADAM_B2 * v + (1.0 - ADAM_B2) * (g * g)
    m_hat = m / (1.0 - ADAM_B1 ** ADAM_STEP)
    v_hat = v / (1.0 - ADAM_B2 ** ADAM_STEP)
    delta = -ADAM_LR * (m_hat / (jnp.sqrt(v_hat) + ADAM_EPS) + ADAM_WD * w)
    return delta, m, v


def _adamw_big(w, g_own, g_sib, m, v, name, token=None):
    _, rows, cols = w.shape
    hr = rows // 2
    t = _row_tile(hr, cols, 9)
    nth = hr // t
    if token is None:
        token = jnp.zeros((SUBLANES, LANES), F32)

    def body(c_ref, w_ref, go_ref, gs_ref, m_ref, v_ref, tok_ref, g_ref, d_ref, mo_ref, vo_ref):
        own = (pl.program_id(0) // nth) == c_ref[0]
        g = jnp.where(own, go_ref[...], gs_ref[...]) + tok_ref[0:1, 0:1]
        g_ref[0] = g
        d_ref[0], mo_ref[0], vo_ref[0] = _adamw_math(w_ref[0], g, m_ref[0], v_ref[0])

    spec = pl.BlockSpec((1, t, cols), lambda i, c_ref: (0, i, 0))
    hspec = pl.BlockSpec((t, cols), lambda i, c_ref: (i % nth, 0))
    tspec = pl.BlockSpec((SUBLANES, LANES), lambda i, c_ref: (0, 0))
    return pl.pallas_call(
        body, name=name,
        grid_spec=pltpu.PrefetchScalarGridSpec(
            num_scalar_prefetch=1, grid=(2 * nth,), in_specs=[spec, hspec, hspec, spec, spec, tspec],
            out_specs=[spec] * 4),
        out_shape=[SDS((1, rows, cols), F32)] * 4,
        compiler_params=_cp(("parallel",)))(_place_index("c"), w, g_own, g_sib, m, v, token)


def _build_slab(mix_slab, dg_mix, dg_ffn, dg_fin, loss8):
    def body(ms_ref, gm_ref, gf_ref, gn_ref, loss_ref, out_ref):
        rows = []
        for ref in (gm_ref, gf_ref, gn_ref):
            v = jnp.sum(ref[...], axis=0, keepdims=True)
            rows += [v[:, :SLAB_W], v[:, SLAB_W:]]
        rows.append(jnp.concatenate([loss_ref[0:1, :]] * (SLAB_W // LANES), axis=1))
        rows.append(jnp.zeros((SLAB_ROWS - ROW_LOSS - 1, SLAB_W), F32))
        tail = jnp.concatenate(rows, axis=0)
        for k in range(N_CHIPS):
            out_ref[k, 0:MIX_SLAB_ROWS, :] = ms_ref[...]
            out_ref[k, MIX_SLAB_ROWS:SLAB_ROWS, :] = tail

    return pl.pallas_call(
        body, name="build_slab", in_specs=[VMEM_SPEC] * 5, out_specs=VMEM_SPEC,
        out_shape=SDS((N_CHIPS, SLAB_ROWS, SLAB_W), F32),
        compiler_params=_cp())(mix_slab, dg_mix, dg_ffn, dg_fin, loss8)


_SMALL_ROWS = (("conv_b", ROW_CONV_B), ("gate_a_b", ROW_BA), ("gate_x_b", ROW_BX), ("lru_lambda", ROW_LAM),
               ("pool_b", ROW_PB), ("pool_scale", ROW_PS), ("norm_lru_g", ROW_GL), ("norm_pool_g", ROW_GP))
_WIDE_ROWS = (("norm_mix_g", ROW_MIX), ("norm_ffn_g", ROW_FFN), ("final_norm_g", ROW_FIN))
_BLOCK_ROWS = (("gate_a_w", ROW_GA), ("gate_x_w", ROW_GX), ("pool_w", ROW_PW))
_SMALL_ORDER = tuple(n for n, _ in _SMALL_ROWS) + tuple(n for n, _ in _WIDE_ROWS) + tuple(
    n for n, _ in _BLOCK_ROWS) + ("conv_w",)


def _adamw_small(slab_own, slab_sib, wmv):
    names = _SMALL_ORDER
    flat = [a for nme in names for a in wmv[nme]]
    nin = len(flat)

    def body(*refs):
        own_ref, sib_ref, j_ref = refs[0], refs[1], refs[2]
        ins = refs[3:3 + nin]
        outs = refs[3 + nin:-1]
        first = j_ref[1] == 0
        slab_ref = jnp.concatenate([jnp.where(first, own_ref[...], sib_ref[...]),
                                    jnp.where(first, sib_ref[...], own_ref[...])], axis=0)
        refs[-1][...] = jnp.broadcast_to(slab_ref[ROW_LOSS:ROW_LOSS + 1, 0:LANES], (SUBLANES, LANES))
        grads = {}
        for nme, row in _SMALL_ROWS:
            grads[nme] = slab_ref[row:row + 1, :]
        for nme, row in _WIDE_ROWS:
            grads[nme] = jnp.concatenate([slab_ref[row:row + 1, :], slab_ref[row + 1:row + 2, :]], axis=1)
        full = slab_ref[ROW_CONV_W:ROW_CONV_W + CONV_WIDTH, :]
        jv = j_ref[0]
        g = jnp.zeros((CONV_WIDTH, LANES), F32)
        for jj in range(N_CHIPS):
            g = jnp.where(jv == jj, full[:, jj * LANES:(jj + 1) * LANES], g)
        grads["conv_w"] = g
        block_rows = dict(_BLOCK_ROWS)
        for idx, nme in enumerate(names):
            w_ref, m_ref, v_ref = ins[3 * idx:3 * idx + 3]
            if nme in block_rows:
                nblk, r, c = w_ref.shape
                parts = [(b, slab_ref[block_rows[nme]:block_rows[nme] + r, b * c:(b + 1) * c]) for b in range(nblk)]
            else:
                parts = [(Ellipsis, grads[nme])]
            for b, g in parts:
                delta, m, v = _adamw_math(w_ref[b], g, m_ref[b], v_ref[b])
                outs[4 * idx][b] = g
                outs[4 * idx + 1][b] = delta
                outs[4 * idx + 2][b] = m
                outs[4 * idx + 3][b] = v

    place = jnp.concatenate([_place_index("j"), _place_index("c")])
    out_shape = [SDS(wmv[nme][0].shape, F32) for nme in names for _ in range(4)] + [SDS((SUBLANES, LANES), F32)]
    res = pl.pallas_call(
        body, name="adamw_small",
        in_specs=[VMEM_SPEC, VMEM_SPEC, pl.BlockSpec(memory_space=pltpu.SMEM)] + [VMEM_SPEC] * nin,
        out_specs=[VMEM_SPEC] * len(out_shape), out_shape=out_shape,
        compiler_params=_cp())(slab_own, slab_sib, place, *flat)
    return {nme: tuple(res[4 * idx:4 * idx + 4]) for idx, nme in enumerate(names)}, res[-1]


_FFN = ("ffn_w1", "ffn_w3", "ffn_w2")
_TRANSPOSED = ("ffn_w1", "ffn_w3")


def _local_step(x, target, full, sp_, distributed):
    d = x.shape[1]
    (u,), got = _inproj(x, sp_["norm_mix_g"], full["w_in"],
                        [_ffn_gather_hosted([full["w_out"]])] if distributed else None)
    w_out = (got[0][0] if distributed else full["w_out"]).reshape(d, d)
    gather = [_ffn_gather_hosted([full[n] for n in _FFN])] if distributed else None
    (h, yn, hres1, saved, pooled), got = _mixer_fwd(u, x, sp_, w_out, gather)
    w1, w3, w2 = got[0] if distributed else [full[n] for n in _FFN]
    h2, a1, a3, ff = _ffn_up(hres1, sp_["norm_ffn_g"], w1, w3)
    dh, dhb, loss8, dg_fin = _ffn_down(ff, hres1, target, sp_["final_norm_g"], w2)
    da1, da3 = _ffn_bwd_gate(dhb, a1, a3, w2)
    dws = list(_ffn_wgrad(h2, dhb, ff, da1, da3))
    rs1 = [_rs_sibling_hosted(dws)] if distributed else None
    (dhres1, dg_ffn), got = _ffn_bwd_down(da1, da3, dh, hres1, sp_["norm_ffn_g"], w1, w3, rs1)
    rs2 = None
    if distributed:
        pairs = [_add_own_half(a, r, "add_half_" + n) for n, a, r in zip(_FFN, dws, got[0])]
        rs2 = [_rs_chips_hosted([pb for _, pb in pairs])]
    (du, mix_slab), got = _mixer_bwd(u, saved, pooled, h, dhres1, sp_, w_out, rs2)
    g_mix = sp_["norm_mix_g"]
    if distributed:
        fin = [_sum_chips(pairs[k][0], got[0][k], "sum_chips_" + n) for k, n in enumerate(_FFN)]
        swap = _rs_swap_hosted(fin)
        state, token = _split_start(swap, "ffn_swap_start")
        g_mix = g_mix + token[0:1, 0:1]
    (gx, dwin, dwout, dg_mix), _ = _inproj_bwd(x, du, dhres1, yn, g_mix, full["w_in"])
    if distributed:
        sib = _split_wait(swap, state, dg_mix, "ffn_swap_wait")
    big = {"w_in": dwin, "w_out": dwout.reshape(N_CHIPS, d // N_CHIPS, d)}
    for k, n in enumerate(_FFN):
        big[n] = (fin[k], sib[k]) if distributed else dws[k]
    return gx, big, (mix_slab, dg_mix, dg_ffn, dg_fin, loss8)


_SMALL_LAYOUT = {
    "gate_a_w": (lambda a: a[0], lambda a: a[None]),
    "gate_x_w": (lambda a: a[0], lambda a: a[None]),
    "pool_w": (lambda a: a[0], lambda a: a[None]),
    "conv_w": (lambda a: a[0], lambda a: a[None]),
    "final_norm_g": (lambda a: a[None], lambda a: a[0]),
}

_WEIGHTS = ("norm_mix_g", "w_in", "conv_w", "conv_b", "gate_a_w", "gate_a_b", "gate_x_w", "gate_x_b", "lru_lambda",
            "pool_w", "pool_b", "pool_scale", "norm_lru_g", "norm_pool_g", "w_out", "norm_ffn_g", "ffn_w1",
            "ffn_w3", "ffn_w2", "final_norm_g")


def kernel(x, norm_mix_g, w_in, conv_w, conv_b, gate_a_w, gate_a_b, gate_x_w, gate_x_b, lru_lambda, pool_w, pool_b, pool_scale, norm_lru_g, norm_pool_g, w_out, norm_ffn_g, ffn_w1, ffn_w3, ffn_w2, final_norm_g, loss_target, m_norm_mix_g, m_w_in, m_conv_w, m_conv_b, m_gate_a_w, m_gate_a_b, m_gate_x_w, m_gate_x_b, m_lru_lambda, m_pool_w, m_pool_b, m_pool_scale, m_norm_lru_g, m_norm_pool_g, m_w_out, m_norm_ffn_g, m_ffn_w1, m_ffn_w3, m_ffn_w2, m_final_norm_g, v_norm_mix_g, v_w_in, v_conv_w, v_conv_b, v_gate_a_w, v_gate_a_b, v_gate_x_w, v_gate_x_b, v_lru_lambda, v_pool_w, v_pool_b, v_pool_scale, v_norm_lru_g, v_norm_pool_g, v_w_out, v_norm_ffn_g, v_ffn_w1, v_ffn_w3, v_ffn_w2, v_final_norm_g):
    loc = locals()
    w = {n: loc[n] for n in _WEIGHTS}
    m = {n: loc["m_" + n] for n in _WEIGHTS}
    v = {n: loc["v_" + n] for n in _WEIGHTS}

    def lay(nme, a):
        return _SMALL_LAYOUT[nme][0](a) if nme in _SMALL_LAYOUT else a

    def unlay(nme, a):
        return _SMALL_LAYOUT[nme][1](a) if nme in _SMALL_LAYOUT else a

    for group in (w, m, v):
        for n in _TRANSPOSED:
            group[n] = jnp.transpose(group[n], (0, 2, 1))

    gathered = _gather_weights([w[n][0] for n in _BIG], w["conv_w"][0], n_remote=1)
    full = dict(zip(_BIG, gathered[:-1]))
    cw_all = gathered[-1]
    sp_ = {n: lay(n, w[n]) for n in _SMALL_ORDER}
    sp_["conv_w"] = jnp.transpose(cw_all[:, :CONV_WIDTH, :], (1, 0, 2)).reshape(CONV_WIDTH, N_CHIPS * LANES)

    gx, big, small = _local_step(x[0], loss_target[0], full, sp_, distributed=True)

    late = ("w_in", "w_out", "slab")
    big["slab"] = _build_slab(*small)
    fin = {n: big[n][0] for n in _FFN}
    sib = {n: big[n][1] for n in _FFN}
    recv1, = _run_comm([_rs_sibling_hosted([big[n] for n in late])], "tail_sibling")
    pairs = [_add_own_half(big[n], r, "add_half_" + n, F32 if n == "slab" else BF16) for n, r in zip(late, recv1)]
    chips = _rs_chips_hosted([pb for _, pb in pairs])
    state, token = _split_start(chips, "tail_chips_start")
    out = {}
    for n in _FFN:
        out[n] = tuple(_adamw_big(w[n], fin[n], sib[n], m[n], v[n], "adamw_" + n, token))
    recv2 = _split_wait(chips, state, out[_FFN[-1]][1], "tail_chips_wait")
    for n, (p, _), r in zip(late, pairs, recv2):
        fin[n] = _sum_chips(p, r, "sum_chips_" + n)
    swapped, = _run_comm([_rs_swap_hosted([fin[n] for n in late])], "tail_swap")
    sib.update(zip(late, swapped))
    for n in late[:2]:
        out[n] = tuple(_adamw_big(w[n], fin[n], sib[n], m[n], v[n], "adamw_" + n))
    for n in _TRANSPOSED:
        out[n] = tuple(jnp.transpose(a, (0, 2, 1)) for a in out[n])
    wmv = {n: (lay(n, w[n]), lay(n, m[n]), lay(n, v[n])) for n in _SMALL_ORDER}
    res, loss = _adamw_small(fin["slab"], sib["slab"], wmv)
    for n in _SMALL_ORDER:
        out[n] = tuple(unlay(n, a) for a in res[n])
    return (loss[0, 0], gx[None]) + tuple(out[n][k] for k in range(4) for n in _WEIGHTS)
```

```python
import functools
import math

import jax
import jax.numpy as jnp
from jax import lax
from jax.experimental import pallas as pl
from jax.experimental.pallas import tpu as pltpu

F32 = jnp.float32
BF16 = jnp.bfloat16
SDS = jax.ShapeDtypeStruct
MESH = pl.DeviceIdType.MESH

EPS = 1e-6
LRU_C = 8.0
CONV_WIDTH = 4
POOL_WINDOWS = (2, 4, 8, 16)
HALO = 16
LANES = 128
SUBLANES = 8
GATE_BLOCK = 256
N_CHIPS = 4

ADAM_LR = 0.001
ADAM_B1 = 0.9
ADAM_B2 = 0.999
ADAM_EPS = 1e-08
ADAM_WD = 0.01
ADAM_STEP = 10

TM_PROJ = 512
TM_MIX = 512
TM_FFN = 512
TM_WGRAD = 2048
MIX_SAVED = ("xc", "r", "ig", "a", "m2raw", "ge", "dge")
FFN_ROW_CHUNKS = 2
VMEM_LIMIT = 56 * 1024 * 1024

SLAB_W = 512
ROW_CONV_B, ROW_CONV_W, ROW_BA, ROW_BX, ROW_LAM, ROW_PB, ROW_PS, ROW_GL, ROW_GP = 0, 1, 5, 6, 7, 8, 9, 10, 11
ROW_GA, ROW_GX, ROW_PW = 16, 80, 144
ROW_MIX, ROW_FFN, ROW_FIN, ROW_LOSS = 272, 274, 276, 278
MIX_SLAB_ROWS = 272
SLAB_ROWS = 288


def _cp(sem=None, **kw):
    if sem is not None:
        kw["dimension_semantics"] = sem
    return pltpu.CompilerParams(vmem_limit_bytes=VMEM_LIMIT, **kw)


def _const_spec(shape):
    nd = len(shape)
    return pl.BlockSpec(shape, lambda *_: (0,) * nd, pipeline_mode=pl.Buffered(1))


def _sigmoid(x):
    return 1.0 / (1.0 + jnp.exp(-x))


def _dot(a, b):
    return jnp.dot(a, b, preferred_element_type=F32)


def _dot_nt(a, b):
    return lax.dot_general(a, b, (((1,), (1,)), ((), ())), preferred_element_type=F32)


def _dot_tn(a, b):
    return lax.dot_general(a, b, (((0,), (0,)), ((), ())), preferred_element_type=F32)


def _colsum8(v):
    m, c = v.shape
    return v.reshape(m // SUBLANES, SUBLANES, c).sum(axis=0)


def _rowmean(v):
    return jnp.mean(v, axis=-1, keepdims=True)


def _rms_bwd(dy, xhat, r, g):
    dxh = dy * g
    return r * (dxh - xhat * _rowmean(dxh * xhat))


def _softplus_neg(lam):
    z = -lam
    e = jnp.exp(-jnp.abs(z))
    u = 1.0 + e
    d = u - 1.0
    log1p = jnp.where(d == 0.0, e, jnp.log(u) * (e / jnp.where(d == 0.0, 1.0, d)))
    return jnp.maximum(z, 0.0) + log1p


def _neg_expm1(z):
    series = -(z * (1.0 + z * (0.5 + z * (1.0 / 6.0 + z * (1.0 / 24.0)))))
    return jnp.where(z > -0.03, series, 1.0 - jnp.exp(z))


_GELU_C = math.sqrt(2.0 / math.pi)
_GELU_K = 0.044715


def _gelu_parts(x):
    x2 = x * x
    th = jnp.tanh(_GELU_C * (x + _GELU_K * x2 * x))
    ge = 0.5 * x * (1.0 + th)
    dge = 0.5 * (1.0 + th) + 0.5 * x * (1.0 - th * th) * (_GELU_C * (1.0 + 3.0 * _GELU_K * x2))
    return ge, dge


def _shift_down(halo, tile, k):
    if k == 0:
        return tile
    ext = jnp.concatenate([halo, tile], axis=0)
    n = tile.shape[0]
    h = halo.shape[0]
    return ext[h - k:h - k + n]


def _shift_up(tile, nxt, k):
    if k == 0:
        return tile
    ext = jnp.concatenate([tile, nxt], axis=0)
    return ext[k:k + tile.shape[0]]


def _build_gate_blocks(ga_ref, gx_ref, gw_ref):
    hd = ga_ref.shape[1]
    per = GATE_BLOCK // hd
    zero = jnp.zeros((hd, hd), F32)
    for b in range(gw_ref.shape[0]):
        for src, off in ((ga_ref, 0), (gx_ref, GATE_BLOCK)):
            for hh in range(per):
                row = jnp.concatenate([zero] * hh + [src[b * per + hh]] + [zero] * (per - 1 - hh), axis=1)
                gw_ref[b, hh * hd:(hh + 1) * hd, off:off + GATE_BLOCK] = row.astype(BF16)


def _scan_level1(a, b, reverse):
    m, c = a.shape
    a3 = a.reshape(m // SUBLANES, SUBLANES, c)
    b3 = b.reshape(m // SUBLANES, SUBLANES, c)
    row = lax.broadcasted_iota(jnp.int32, a3.shape, 1)
    for s in (1, 2, 4):
        sh = (SUBLANES - s) if reverse else s
        a_sh = pltpu.roll(a3, sh, 1)
        b_sh = pltpu.roll(b3, sh, 1)
        ok = (row < SUBLANES - s) if reverse else (row >= s)
        b3 = jnp.where(ok, a3 * b_sh + b3, b3)
        a3 = jnp.where(ok, a3 * a_sh, a3)
    return a3.reshape(m, c), b3.reshape(m, c)


def _scan_level2(a_ref, b_ref, out_ref, carry, reverse):
    m, c = a_ref.shape
    ng = m // SUBLANES

    def step(g, cr):
        gi = (ng - 1 - g) if reverse else g
        off = pl.multiple_of(gi * SUBLANES, SUBLANES)
        h = b_ref[pl.ds(off, SUBLANES), :] + a_ref[pl.ds(off, SUBLANES), :] * cr
        out_ref[pl.ds(off, SUBLANES), :] = h
        edge = h[0:1, :] if reverse else h[SUBLANES - 1:SUBLANES, :]
        return jnp.broadcast_to(edge, (SUBLANES, c))

    return lax.fori_loop(0, ng, step, carry, unroll=4)


def _mixer_recompute(u, hal, t0, cw, cb, gw_ref, ba, bx, lam, pw_ref, pb, ps):
    tm = u.shape[0]
    lw = cb.shape[1]
    u_l, u_g, u_p = u[:, :lw], u[:, lw:2 * lw], u[:, 2 * lw:]
    hal_l, hal_p = hal[:, :lw], hal[:, 2 * lw:]
    taps = [_shift_down(hal_l, u_l, CONV_WIDTH - 1 - k) for k in range(CONV_WIDTH)]
    xc = cb
    for k in range(CONV_WIDTH):
        xc = xc + taps[k] * cw[k:k + 1, :]
    xcb = xc.astype(BF16)
    nb = lw // GATE_BLOCK
    gs = [_dot(xcb[:, b * GATE_BLOCK:(b + 1) * GATE_BLOCK], gw_ref[b]) for b in range(nb)]
    r = _sigmoid(jnp.concatenate([g[:, :GATE_BLOCK] for g in gs], axis=1) + ba)
    ig = _sigmoid(jnp.concatenate([g[:, GATE_BLOCK:] for g in gs], axis=1) + bx)
    sp = _softplus_neg(lam)
    la = (-LRU_C * r) * sp
    a = jnp.exp(la)
    m2raw = _neg_expm1(2.0 * la)
    mult = jnp.sqrt(jnp.maximum(m2raw, 1e-12))
    ge, dge = _gelu_parts(u_g)
    row = lax.broadcasted_iota(jnp.int32, (tm, LANES), 0) + t0
    pooled, invs, zs = [], [], []
    for gi, w in enumerate(POOL_WINDOWS):
        e = jnp.concatenate([hal_p[:, gi * LANES:(gi + 1) * LANES], u_p[:, gi * LANES:(gi + 1) * LANES]], axis=0)
        s = e
        k = 1
        while k < w:
            s = s + pltpu.roll(s, k, 0)
            k *= 2
        inv = 1.0 / jnp.minimum(row + 1, w).astype(F32)
        pg = s[HALO:] * inv - e[HALO:]
        pooled.append(pg)
        invs.append(inv)
        zs.append(_dot(pg.astype(BF16), pw_ref[gi].astype(BF16)))
    z = jnp.concatenate(zs, axis=1) + pb
    y_pool = z * ps
    return dict(u_l=u_l, u_g=u_g, taps=taps, xc=xc, xcb=xcb, r=r, ig=ig, sp=sp, la=la, a=a, m2raw=m2raw,
                mult=mult, ge=ge, dge=dge, pooled=pooled, invs=invs, z=z, y_pool=y_pool)


ANY = pl.BlockSpec(memory_space=pl.ANY)
VMEM_SPEC = pl.BlockSpec(memory_space=pltpu.VMEM)


class _Hosted:
    def __init__(self, ins, out_shapes, sems, start, finish, mid=None, aliases=None):
        self.ins, self.out_shapes, self.sems = list(ins), list(out_shapes), list(sems)
        self.start, self.mid, self.finish = start, mid, finish
        self.aliases = dict(aliases or {})


def _call(body, hosted, stage_preds, *, name, grid, in_specs, out_specs, out_shape, scratch_shapes, args, sem):
    hosted = list(hosted or [])
    n_in, n_out, n_scr = len(in_specs), len(out_specs), len(scratch_shapes)
    c_in = [a for h in hosted for a in h.ins]
    c_out = [o for h in hosted for o in h.out_shapes]
    c_sem = [pltpu.SemaphoreType.DMA((k,)) for h in hosted for k in h.sems]

    def full(*refs):
        p = 0
        parts = []
        for cnt in (n_in, len(c_in), n_out, len(c_out), n_scr, len(c_sem)):
            parts.append(refs[p:p + cnt])
            p += cnt
        hi, ci, ho, co, hs, cs = parts
        per = []
        a = b = c_ = 0
        for h in hosted:
            per.append((h, ci[a:a + len(h.ins)], co[b:b + len(h.out_shapes)], cs[c_:c_ + len(h.sems)]))
            a, b, c_ = a + len(h.ins), b + len(h.out_shapes), c_ + len(h.sems)
        first = mid = last = None
        if hosted and grid:
            first, mid, last = stage_preds()

        def run(fn, pred, i_, o_, s_):
            if fn is None:
                return
            if pred is None:
                fn(i_, o_, s_)
            else:
                pl.when(pred)(functools.partial(fn, i_, o_, s_))

        for h, i_, o_, s_ in per:
            run(h.start, first, i_, o_, s_)
        body(*hi, *ho, *hs)
        for h, i_, o_, s_ in per:
            run(h.mid, mid, i_, o_, s_)
        for h, i_, o_, s_ in per:
            run(h.finish, last, i_, o_, s_)

    aliases = {}
    a = b = 0
    for h in hosted:
        for k, v in h.aliases.items():
            aliases[n_in + a + k] = n_out + b + v
        a, b = a + len(h.ins), b + len(h.out_shapes)
    res = pl.pallas_call(
        full, name=name, grid=grid, in_specs=list(in_specs) + [ANY] * len(c_in),
        out_specs=list(out_specs) + [ANY] * len(c_out), out_shape=list(out_shape) + c_out,
        scratch_shapes=list(scratch_shapes) + c_sem, input_output_aliases=aliases,
        compiler_params=_cp(sem))(*args, *c_in)
    res = list(res)
    outs = []
    p = n_out
    for h in hosted:
        outs.append(res[p:p + len(h.out_shapes)])
        p += len(h.out_shapes)
    return res[:n_out], outs


def _inproj(x, g_mix, w_in, hosted=None):
    s, d = x.shape
    n = w_in.shape[1]
    tm = min(TM_PROJ, s)
    nt = s // tm

    def body(x_ref, g_ref, w_ref, u_ref):
        xv = x_ref[...]
        r = lax.rsqrt(_rowmean(xv * xv) + EPS)
        u_ref[...] = _dot((xv * r * g_ref[...]).astype(BF16), w_ref[...])

    def stages():
        i = pl.program_id(0)
        return i == 0, i == max(nt - 3, 0), i == nt - 1

    return _call(
        body, hosted, stages, grid=(nt,), name="inproj",
        in_specs=[pl.BlockSpec((tm, d), lambda i: (i, 0)), _const_spec((1, d)), _const_spec((d, n))],
        out_specs=[pl.BlockSpec((tm, n), lambda i: (i, 0))], out_shape=[SDS((s, n), F32)], scratch_shapes=[],
        args=(x, g_mix, w_in), sem=("arbitrary",))


def _mixer_fwd(u, x, sp_, w_out, hosted=None):
    s, din = u.shape
    d = x.shape[1]
    lw = din // 3
    tm = min(TM_MIX, s)
    nb = lw // GATE_BLOCK

    def body(u_ref, halo_ref, x_ref, cw_ref, cb_ref, ga_ref, gx_ref, ba_ref, bx_ref, lam_ref, pw_ref, pb_ref,
             ps_ref, gl_ref, gp_ref, wout_ref, h_ref, yn_ref, hres_ref, saved_ref, pooled_ref,
             gw_s, a_s, b_s, carry_s):
        i = pl.program_id(0)

        @pl.when(i == 0)
        def _():
            _build_gate_blocks(ga_ref, gx_ref, gw_s)
            carry_s[...] = jnp.zeros_like(carry_s)

        uv = u_ref[...]
        hal = jnp.where(i > 0, halo_ref[...], 0.0)
        f = _mixer_recompute(uv, hal, i * tm, cw_ref[...], cb_ref[...], gw_s, ba_ref[...], bx_ref[...],
                             lam_ref[...], pw_ref, pb_ref[...], ps_ref[...])
        for k, name in enumerate(MIX_SAVED):
            saved_ref[k] = f[name]
        pooled_ref[...] = jnp.concatenate(f["pooled"], axis=1).astype(BF16)
        bb = f["mult"] * (f["ig"] * f["xc"])
        a1, b1 = _scan_level1(f["a"], bb, reverse=False)
        a_s[...] = a1
        b_s[...] = b1
        carry_s[...] = _scan_level2(a_s, b_s, h_ref, carry_s[...], reverse=False)
        y_lru = h_ref[...] * f["ge"]
        rl = lax.rsqrt(_rowmean(y_lru * y_lru) + EPS)
        yp = f["y_pool"]
        rp = lax.rsqrt(_rowmean(yp * yp) + EPS)
        yn = jnp.concatenate([y_lru * rl * gl_ref[...], yp * rp * gp_ref[...]], axis=1).astype(BF16)
        yn_ref[...] = yn
        hres_ref[...] = x_ref[...] + _dot(yn, wout_ref[...])

    small = [sp_[k] for k in ("conv_w", "conv_b", "gate_a_w", "gate_x_w", "gate_a_b", "gate_x_b", "lru_lambda",
                              "pool_w", "pool_b", "pool_scale", "norm_lru_g", "norm_pool_g")]
    nt = s // tm

    def stages():
        i = pl.program_id(0)
        return i == 0, i == max(nt - 3, 0), i == nt - 1

    return _call(
        body, hosted, stages, grid=(nt,), name="mixer_fwd",
        in_specs=[pl.BlockSpec((tm, din), lambda i: (i, 0)),
                  pl.BlockSpec((HALO, din), lambda i: (jnp.maximum(i * (tm // HALO) - 1, 0), 0)),
                  pl.BlockSpec((tm, d), lambda i: (i, 0))]
        + [_const_spec(a.shape) for a in small] + [_const_spec(w_out.shape)],
        out_specs=[pl.BlockSpec((tm, lw), lambda i: (i, 0)), pl.BlockSpec((tm, d), lambda i: (i, 0)),
                   pl.BlockSpec((tm, d), lambda i: (i, 0)),
                   pl.BlockSpec((len(MIX_SAVED), tm, lw), lambda i: (0, i, 0)),
                   pl.BlockSpec((tm, lw), lambda i: (i, 0))],
        out_shape=[SDS((s, lw), F32), SDS((s, d), BF16), SDS((s, d), F32), SDS((len(MIX_SAVED), s, lw), F32),
                   SDS((s, lw), BF16)],
        scratch_shapes=[pltpu.VMEM((nb, GATE_BLOCK, 2 * GATE_BLOCK), BF16), pltpu.VMEM((tm, lw), F32),
                        pltpu.VMEM((tm, lw), F32), pltpu.VMEM((SUBLANES, lw), F32)],
        args=(u, u, x, *small, w_out), sem=("arbitrary",))


def _row_chunks(tm):
    rc = tm // FFN_ROW_CHUNKS
    return [slice(q * rc, (q + 1) * rc) for q in range(FFN_ROW_CHUNKS)]


def _ffn_up(hres1, g_ffn, w1, w3):
    s, d = hres1.shape
    nj, fc, _ = w1.shape
    tm = min(TM_FFN, s)

    def body(h_ref, gf_ref, w1_ref, w3_ref, h2_ref, a1_ref, a3_ref, ff_ref):
        hv = h_ref[...]
        r = lax.rsqrt(_rowmean(hv * hv) + EPS)
        h2_ref[...] = (hv * r * gf_ref[...]).astype(BF16)
        h2 = h2_ref[...]
        for j in range(nj):
            a1 = _dot_nt(h2, w1_ref[j])
            a3 = _dot_nt(h2, w3_ref[j])
            a1_ref[j] = a1.astype(BF16)
            a3_ref[j] = a3.astype(BF16)
            ff_ref[j] = ((a1 * _sigmoid(a1)) * a3).astype(BF16)

    wspec = _const_spec(w1.shape)
    aspec = pl.BlockSpec((nj, tm, fc), lambda i: (0, i, 0))
    return pl.pallas_call(
        body, grid=(s // tm,), name="ffn_up",
        in_specs=[pl.BlockSpec((tm, d), lambda i: (i, 0)), _const_spec((1, d)), wspec, wspec],
        out_specs=[pl.BlockSpec((tm, d), lambda i: (i, 0)), aspec, aspec, aspec],
        out_shape=[SDS((s, d), BF16)] + [SDS((nj, s, fc), BF16)] * 3,
        compiler_params=_cp(("parallel",)))(hres1, g_ffn, w1, w3)


def _ffn_down(ff, hres1, target, g_fin, w2):
    s, d = hres1.shape
    nj, _, fc = ff.shape
    tm = min(TM_FFN, s)

    def body(ff_ref, h_ref, t_ref, gn_ref, w2_ref, dh_ref, dhb_ref, loss_ref, dgn_ref):
        @pl.when(pl.program_id(0) == 0)
        def _():
            loss_ref[...] = jnp.zeros_like(loss_ref)
            dgn_ref[...] = jnp.zeros_like(dgn_ref)

        gn = gn_ref[...]
        for rows in _row_chunks(tm):
            acc = _dot(ff_ref[0, rows, :], w2_ref[0])
            for j in range(1, nj):
                acc = acc + _dot(ff_ref[j, rows, :], w2_ref[j])
            hr2 = h_ref[rows, :] + acc
            r2 = lax.rsqrt(_rowmean(hr2 * hr2) + EPS)
            xh = hr2 * r2
            diff = xh * gn - t_ref[rows, :]
            tot = jnp.sum(jnp.sum(diff * diff, axis=1, keepdims=True), axis=0, keepdims=True)
            loss_ref[...] += tot * (0.5 / d)
            dout = diff * (1.0 / d)
            dgn_ref[...] += _colsum8(dout * xh)
            dh = _rms_bwd(dout, xh, r2, gn)
            dh_ref[rows, :] = dh
            dhb_ref[rows, :] = dh.astype(BF16)

    tile = pl.BlockSpec((tm, d), lambda i: (i, 0))
    return pl.pallas_call(
        body, grid=(s // tm,), name="ffn_down",
        in_specs=[pl.BlockSpec((nj, tm, fc), lambda i: (0, i, 0)), tile, tile, _const_spec((1, d)),
                  _const_spec(w2.shape)],
        out_specs=[tile, tile, pl.BlockSpec((SUBLANES, LANES), lambda i: (0, 0)),
                   pl.BlockSpec((SUBLANES, d), lambda i: (0, 0))],
        out_shape=[SDS((s, d), F32), SDS((s, d), BF16), SDS((SUBLANES, LANES), F32), SDS((SUBLANES, d), F32)],
        compiler_params=_cp(("arbitrary",)))(ff, hres1, target, g_fin, w2)


def _ffn_bwd_gate(dhb, a1, a3, w2):
    s, d = dhb.shape
    nj, _, fc = a1.shape
    tm = min(TM_FFN, s)

    def body(dhb_ref, a1_ref, a3_ref, w2_ref, da1_ref, da3_ref):
        for j in range(nj):
            for rows in _row_chunks(tm):
                dff = _dot_nt(dhb_ref[rows, :], w2_ref[j])
                a1v = a1_ref[j, rows, :].astype(F32)
                sg = _sigmoid(a1v)
                silu = a1v * sg
                da1_ref[j, rows, :] = (dff * a3_ref[j, rows, :].astype(F32)
                                       * (sg * (1.0 + (a1v - silu)))).astype(BF16)
                da3_ref[j, rows, :] = (dff * silu).astype(BF16)

    aspec = pl.BlockSpec((nj, tm, fc), lambda i: (0, i, 0))
    return pl.pallas_call(
        body, grid=(s // tm,), name="ffn_bwd_gate",
        in_specs=[pl.BlockSpec((tm, d), lambda i: (i, 0)), aspec, aspec, _const_spec(w2.shape)],
        out_specs=[aspec, aspec], out_shape=[SDS((nj, s, fc), BF16)] * 2,
        compiler_params=_cp(("parallel",)))(dhb, a1, a3, w2)


def _ffn_bwd_down(da1, da3, dh, hres1, g_ffn, w1, w3, hosted=None):
    s, d = hres1.shape
    nj, _, fc = da1.shape
    tm = min(TM_FFN, s)
    nt = s // tm

    def body(da1_ref, da3_ref, dh_ref, h_ref, gf_ref, w1_ref, w3_ref, dhr_ref, dgf_ref):
        @pl.when(pl.program_id(0) == 0)
        def _():
            dgf_ref[...] = jnp.zeros_like(dgf_ref)

        gf = gf_ref[...]
        for rows in _row_chunks(tm):
            dh2 = None
            for j in range(nj):
                part = _dot(da1_ref[j, rows, :], w1_ref[j]) + _dot(da3_ref[j, rows, :], w3_ref[j])
                dh2 = part if dh2 is None else dh2 + part
            hv = h_ref[rows, :]
            r = lax.rsqrt(_rowmean(hv * hv) + EPS)
            xh = hv * r
            dgf_ref[...] += _colsum8(dh2 * xh)
            dhr_ref[rows, :] = dh_ref[rows, :] + _rms_bwd(dh2, xh, r, gf)

    tile = pl.BlockSpec((tm, d), lambda i: (i, 0))
    aspec = pl.BlockSpec((nj, tm, fc), lambda i: (0, i, 0))
    wspec = _const_spec(w1.shape)

    def stages():
        i = pl.program_id(0)
        return i == 0, i == max(nt - 2, 0), i == nt - 1

    return _call(
        body, hosted, stages, grid=(nt,), name="ffn_bwd_down",
        in_specs=[aspec, aspec, tile, tile, _const_spec((1, d)), wspec, wspec],
        out_specs=[tile, pl.BlockSpec((SUBLANES, d), lambda i: (0, 0))],
        out_shape=[SDS((s, d), F32), SDS((SUBLANES, d), F32)],
        scratch_shapes=[], args=(da1, da3, dh, hres1, g_ffn, w1, w3), sem=("arbitrary",))


def _ffn_wgrad(h2, dhb, ff, da1, da3):
    s, d = h2.shape
    _, _, fc = ff.shape
    tm = min(TM_WGRAD, s)

    def body(h2_ref, dhb_ref, ff_ref, da1_ref, da3_ref, dw1_ref, dw3_ref, dw2_ref):
        @pl.when(pl.program_id(1) == 0)
        def _():
            dw1_ref[...] = jnp.zeros_like(dw1_ref)
            dw3_ref[...] = jnp.zeros_like(dw3_ref)
            dw2_ref[...] = jnp.zeros_like(dw2_ref)

        h2v = h2_ref[...]
        dw1_ref[0] += _dot_tn(da1_ref[0], h2v)
        dw3_ref[0] += _dot_tn(da3_ref[0], h2v)
        dw2_ref[0] += _dot_tn(ff_ref[0], dhb_ref[...])

    wspec = pl.BlockSpec((1, fc, d), lambda j, i: (j, 0, 0))
    return pl.pallas_call(
        body, grid=(N_CHIPS, s // tm), name="ffn_wgrad",
        in_specs=[pl.BlockSpec((tm, d), lambda j, i: (i, 0)), pl.BlockSpec((tm, d), lambda j, i: (i, 0))]
        + [pl.BlockSpec((1, tm, fc), lambda j, i: (j, i, 0))] * 3,
        out_specs=[wspec] * 3, out_shape=[SDS((N_CHIPS, fc, d), F32)] * 3,
        compiler_params=_cp(("parallel", "arbitrary")))(h2, dhb, ff, da1, da3)


def _mixer_bwd(u, saved, pooled, h, dhres1, sp_, w_out, hosted=None):
    s, din = u.shape
    d = dhres1.shape[1]
    lw = din // 3
    tm = min(TM_MIX, s)
    nt = s // tm
    nb = lw // GATE_BLOCK
    hd = sp_["gate_a_w"].shape[1]

    def body(ul_ref, saved_ref, pooled_ref, h_ref, hhalo_ref, dhr_ref, cw_ref, cb_ref, ga_ref, gx_ref, ba_ref,
             bx_ref, lam_ref, pw_ref, pb_ref, ps_ref, gl_ref, gp_ref, wout_ref, du_ref, slab_ref,
             gw_s, a_s, b_s, e_s, ecarry_s, dxc_s, q_s, vec_s, cwacc_s, dgw_s, dpw_s):
        i = pl.program_id(0)
        tile = nt - 1 - i

        @pl.when(i == 0)
        def _():
            _build_gate_blocks(ga_ref, gx_ref, gw_s)
            for ref in (ecarry_s, dxc_s, q_s, vec_s, cwacc_s, dgw_s, dpw_s):
                ref[...] = jnp.zeros_like(ref)

        cw = cw_ref[...]
        lam = lam_ref[...]
        ps = ps_ref[...]
        f = {name: saved_ref[k] for k, name in enumerate(MIX_SAVED)}
        f["mult"] = jnp.sqrt(jnp.maximum(f["m2raw"], 1e-12))
        f["sp"] = _softplus_neg(lam)
        f["xcb"] = f["xc"].astype(BF16)
        pooled = pooled_ref[...]
        row = lax.broadcasted_iota(jnp.int32, (tm, LANES), 0) + tile * tm
        f["invs"] = [1.0 / jnp.minimum(row + 1, w).astype(F32) for w in POOL_WINDOWS]
        f["z"] = jnp.concatenate(
            [_dot(pooled[:, g * LANES:(g + 1) * LANES], pw_ref[g].astype(BF16))
             for g in range(len(POOL_WINDOWS))], axis=1) + pb_ref[...]
        f["y_pool"] = f["z"] * ps
        u_l = ul_ref[...]
        hv = h_ref[...]
        h_prev = _shift_down(jnp.where(tile > 0, hhalo_ref[...], 0.0), hv, 1)
        y_lru = hv * f["ge"]
        rl = lax.rsqrt(_rowmean(y_lru * y_lru) + EPS)
        yp = f["y_pool"]
        rp = lax.rsqrt(_rowmean(yp * yp) + EPS)
        xh_l = y_lru * rl
        xh_p = yp * rp

        dyn = _dot_nt(dhr_ref[...].astype(BF16), wout_ref[...])
        d_nl, d_np = dyn[:, :lw], dyn[:, lw:]
        vec = {}
        vec[ROW_GL] = _colsum8(d_nl * xh_l)
        vec[ROW_GP] = _colsum8(d_np * xh_p)
        d_ylru = _rms_bwd(d_nl, xh_l, rl, gl_ref[...])
        d_ypool = _rms_bwd(d_np, xh_p, rp, gp_ref[...])

        vec[ROW_PS] = _colsum8(d_ypool * f["z"])
        dz = d_ypool * ps
        vec[ROW_PB] = _colsum8(dz)
        dzb = dz.astype(BF16)
        dup = []
        for gi, w in enumerate(POOL_WINDOWS):
            sl = slice(gi * LANES, (gi + 1) * LANES)
            dpw_s[:, sl] += _dot_tn(pooled[:, sl], dzb[:, sl])
            dpool = _dot_nt(dzb[:, sl], pw_ref[gi].astype(BF16))
            q = dpool * f["invs"][gi]
            e = jnp.concatenate([q, q_s[:, sl]], axis=0)
            k = 1
            while k < w:
                e = e + pltpu.roll(e, tm + HALO - k, 0)
                k *= 2
            dup.append(e[:tm] - dpool)
            q_s[:, sl] = q[:HALO]

        d_hout = d_ylru * f["ge"]
        d_ug = d_ylru * hv * f["dge"]
        a = f["a"]
        a1, b1 = _scan_level1(a, a * d_hout, reverse=True)
        a_s[...] = a1
        b_s[...] = b1
        e_next = ecarry_s[...]
        ecarry_s[...] = _scan_level2(a_s, b_s, e_s, e_next, reverse=True)
        sv = d_hout + _shift_up(e_s[...], e_next, 1)
        d_a = sv * h_prev
        mult, ig, xc, r = f["mult"], f["ig"], f["xc"], f["r"]
        d_mult = sv * (ig * xc)
        d_ig = sv * mult * xc
        d_xc = sv * mult * ig
        d_la = d_a * a + jnp.where(f["m2raw"] > 1e-12, d_mult * (-(a * a) / mult), 0.0)
        d_r = d_la * (-LRU_C * f["sp"])
        vec[ROW_LAM] = _colsum8(d_la * (-LRU_C * r))
        d_pr = d_r * r * (1.0 - r)
        d_pi = d_ig * ig * (1.0 - ig)
        vec[ROW_BA] = _colsum8(d_pr)
        vec[ROW_BX] = _colsum8(d_pi)
        dxc_parts = []
        for b in range(nb):
            sl = slice(b * GATE_BLOCK, (b + 1) * GATE_BLOCK)
            rhs = jnp.concatenate([d_pr[:, sl], d_pi[:, sl]], axis=1).astype(BF16)
            dgw_s[b] += _dot_tn(f["xcb"][:, sl], rhs)
            dxc_parts.append(_dot_nt(rhs, gw_s[b]))
        d_xc = d_xc + jnp.concatenate(dxc_parts, axis=1)
        vec[ROW_CONV_B] = _colsum8(d_xc)
        dxc_next = dxc_s[...]
        d_ul = None
        for k in range(CONV_WIDTH):
            ahead = _shift_up(d_xc, dxc_next, CONV_WIDTH - 1 - k)
            cwacc_s[k * SUBLANES:(k + 1) * SUBLANES, :] += _colsum8(ahead * u_l)
            term = ahead * cw[k:k + 1, :]
            d_ul = term if d_ul is None else d_ul + term
        dxc_s[...] = d_xc[:SUBLANES]
        for row, val in vec.items():
            vec_s[row * SUBLANES:(row + 1) * SUBLANES, :] += val
        du_ref[...] = jnp.concatenate([d_ul, d_ug] + dup, axis=1).astype(BF16)

        @pl.when(i == nt - 1)
        def _():
            rows = []
            for row in range(ROW_GA):
                if row in (ROW_CONV_W, ROW_CONV_W + 1, ROW_CONV_W + 2, ROW_CONV_W + 3):
                    k = row - ROW_CONV_W
                    v = jnp.sum(cwacc_s[k * SUBLANES:(k + 1) * SUBLANES, :], axis=0, keepdims=True)
                elif row <= ROW_GP:
                    v = jnp.sum(vec_s[row * SUBLANES:(row + 1) * SUBLANES, :], axis=0, keepdims=True)
                    if row == ROW_LAM:
                        v = v * (-1.0 / (1.0 + jnp.exp(lam)))
                else:
                    v = jnp.zeros((1, lw), F32)
                rows.append(v)
            slab_ref[0:ROW_GA, :] = jnp.concatenate(rows, axis=0)
            lane = lax.broadcasted_iota(jnp.int32, (hd, GATE_BLOCK), 1)
            for b in range(nb):
                for off, row0 in ((0, ROW_GA), (GATE_BLOCK, ROW_GX)):
                    acc = jnp.zeros((hd, GATE_BLOCK), F32)
                    for hh in range(GATE_BLOCK // hd):
                        m = (lane >= hh * hd) & (lane < (hh + 1) * hd)
                        acc = acc + jnp.where(m, dgw_s[b, hh * hd:(hh + 1) * hd, off:off + GATE_BLOCK], 0.0)
                    slab_ref[row0:row0 + hd, b * GATE_BLOCK:(b + 1) * GATE_BLOCK] = acc
            slab_ref[ROW_PW:ROW_PW + LANES, :] = dpw_s[...]

    small = [sp_[k] for k in ("conv_w", "conv_b", "gate_a_w", "gate_x_w", "gate_a_b", "gate_x_b", "lru_lambda",
                              "pool_w", "pool_b", "pool_scale", "norm_lru_g", "norm_pool_g")]
    rev = lambda i: nt - 1 - i

    def stages():
        i = pl.program_id(0)
        return i == 0, i == max(nt - 3, 0), i == nt - 1

    return _call(
        body, hosted, stages, grid=(nt,), name="mixer_bwd",
        in_specs=[pl.BlockSpec((tm, lw), lambda i: (rev(i), 0)),
                  pl.BlockSpec((len(MIX_SAVED), tm, lw), lambda i: (0, rev(i), 0)),
                  pl.BlockSpec((tm, lw), lambda i: (rev(i), 0)),
                  pl.BlockSpec((tm, lw), lambda i: (rev(i), 0)),
                  pl.BlockSpec((SUBLANES, lw), lambda i: (jnp.maximum(rev(i) * (tm // SUBLANES) - 1, 0), 0)),
                  pl.BlockSpec((tm, d), lambda i: (rev(i), 0))]
        + [_const_spec(a.shape) for a in small] + [_const_spec(w_out.shape)],
        out_specs=[pl.BlockSpec((tm, din), lambda i: (rev(i), 0)),
                   pl.BlockSpec((MIX_SLAB_ROWS, SLAB_W), lambda i: (0, 0))],
        out_shape=[SDS((s, din), BF16), SDS((MIX_SLAB_ROWS, SLAB_W), F32)],
        scratch_shapes=[pltpu.VMEM((nb, GATE_BLOCK, 2 * GATE_BLOCK), BF16),
                        pltpu.VMEM((tm, lw), F32), pltpu.VMEM((tm, lw), F32), pltpu.VMEM((tm, lw), F32),
                        pltpu.VMEM((SUBLANES, lw), F32), pltpu.VMEM((SUBLANES, lw), F32),
                        pltpu.VMEM((HALO, lw), F32), pltpu.VMEM((ROW_GA * SUBLANES, lw), F32),
                        pltpu.VMEM((CONV_WIDTH * SUBLANES, lw), F32),
                        pltpu.VMEM((nb, GATE_BLOCK, 2 * GATE_BLOCK), F32), pltpu.VMEM((LANES, lw), F32)],
        args=(u, saved, pooled, h, h, dhres1, *small, w_out), sem=("arbitrary",))


def _inproj_bwd(x, du, dhres1, yn, g_mix, w_in, hosted=None):
    s, d = x.shape
    n = w_in.shape[1]
    nc = n // N_CHIPS
    tm = min(TM_PROJ, s)
    nt = s // tm

    def body(x_ref, du_ref, dhr_ref, yn_ref, g_ref, w_ref, gx_ref, dwin_ref, dwout_ref, dg_ref):
        i = pl.program_id(0)

        @pl.when(i == 0)
        def _():
            dwin_ref[...] = jnp.zeros_like(dwin_ref)
            dwout_ref[...] = jnp.zeros_like(dwout_ref)
            dg_ref[...] = jnp.zeros_like(dg_ref)

        xv = x_ref[...]
        g = g_ref[...]
        r = lax.rsqrt(_rowmean(xv * xv) + EPS)
        xh = xv * r
        h1 = (xh * g).astype(BF16)
        duv = du_ref[...]
        dh1 = _dot_nt(duv, w_ref[...])
        dg_ref[...] += _colsum8(dh1 * xh)
        dhr = dhr_ref[...]
        gx_ref[...] = dhr + _rms_bwd(dh1, xh, r, g)
        for jj in range(N_CHIPS):
            dwin_ref[jj] += _dot_tn(h1, duv[:, jj * nc:(jj + 1) * nc])
        dwout_ref[...] += _dot_tn(yn_ref[...], dhr.astype(BF16))

    def stages():
        i = pl.program_id(0)
        return i == 0, i == max(nt - 3, 0), i == nt - 1

    return _call(
        body, hosted, stages, grid=(nt,), name="inproj_bwd",
        in_specs=[pl.BlockSpec((tm, d), lambda i: (i, 0)), pl.BlockSpec((tm, n), lambda i: (i, 0)),
                  pl.BlockSpec((tm, d), lambda i: (i, 0)), pl.BlockSpec((tm, d), lambda i: (i, 0)),
                  _const_spec((1, d)), _const_spec((d, n))],
        out_specs=[pl.BlockSpec((tm, d), lambda i: (i, 0)), pl.BlockSpec((N_CHIPS, d, nc), lambda i: (0, 0, 0)),
                   pl.BlockSpec((d, d), lambda i: (0, 0)), pl.BlockSpec((SUBLANES, d), lambda i: (0, 0))],
        out_shape=[SDS((s, d), F32), SDS((N_CHIPS, d, nc), F32), SDS((d, d), F32), SDS((SUBLANES, d), F32)],
        scratch_shapes=[], args=(x, du, dhres1, yn, g_mix, w_in), sem=("arbitrary",))


def _place():
    x, y, c = lax.axis_index("x"), lax.axis_index("y"), lax.axis_index("c")
    return x, y, c


def _other_chips(x, y):
    return [(1 - x, y), (x, 1 - y), (1 - x, 1 - y)]


ANY = pl.BlockSpec(memory_space=pl.ANY)
VMEM_SPEC = pl.BlockSpec(memory_space=pltpu.VMEM)

_GATHERED = {"w_in": "cols", "w_out": "major", "ffn_w1": "major", "ffn_w3": "major", "ffn_w2": "major"}
_BIG = ("w_in", "w_out", "ffn_w1", "ffn_w3", "ffn_w2")


def _gather_weights(shards, conv_w, n_remote):
    n = len(shards)
    full_shapes = []
    for name, sh in zip(_BIG, shards):
        r, cdim = sh.shape
        if _GATHERED[name] == "cols":
            assert cdim % LANES == 0
            full_shapes.append((r, cdim * N_CHIPS))
        else:
            full_shapes.append((N_CHIPS, r, cdim))

    def region(ref, name, sh, jj, cc):
        r, cdim = sh
        rows = pl.ds(0, r) if cc is None else pl.ds(pl.multiple_of(cc * (r // 2), 16), r // 2)
        if _GATHERED[name] == "cols":
            return ref.at[rows, pl.ds(pl.multiple_of(jj * cdim, LANES), cdim)]
        return ref.at[jj, rows, :]

    def staged(ref, sh, cc):
        r = sh[0]
        return ref.at[pl.ds(pl.multiple_of(cc * (r // 2), 16), r // 2), :]

    def body(*refs):
        ins, cw_in = refs[:n], refs[n]
        outs, cw_out = refs[n + 1:2 * n + 1], refs[2 * n + 1]
        stage = refs[2 * n + 2:3 * n + 2]
        cw_stage, lsem, ssem, rsem, fssem, frsem, cssem, crsem = refs[3 * n + 2:]
        x, y, c = _place()
        j = 2 * x + y
        chips = _other_chips(x, y)
        for w in range(n_remote):
            stage[w][...] = ins[w][...].astype(BF16)
        cw_stage[...] = jnp.zeros_like(cw_stage)
        cw_stage[0:CONV_WIDTH, :] = cw_in[...]
        shs = [s_.shape for s_ in shards]
        local = [pltpu.make_async_copy(stage[w], region(outs[w], _BIG[w], shs[w], j, None), lsem.at[w])
                 for w in range(n)]
        local.append(pltpu.make_async_copy(cw_stage, cw_out.at[j], lsem.at[n]))
        sends = []
        for k, (px, py) in enumerate(chips):
            for w in range(n_remote):
                sends.append(pltpu.make_async_remote_copy(
                    src_ref=staged(stage[w], shs[w], c), dst_ref=region(outs[w], _BIG[w], shs[w], j, c),
                    send_sem=ssem.at[k * n + w], recv_sem=rsem.at[k * n + w], device_id=(px, py, c),
                    device_id_type=MESH))
            sends.append(pltpu.make_async_remote_copy(
                src_ref=cw_stage, dst_ref=cw_out.at[j], send_sem=cssem.at[k], recv_sem=crsem.at[k],
                device_id=(px, py, c), device_id_type=MESH))
        for cp in sends:
            cp.start()
        for w in range(n_remote, n):
            stage[w][...] = ins[w][...].astype(BF16)
        for cp in local:
            cp.start()
        fwd = []
        for k, (px, py) in enumerate(chips):
            jk = 2 * px + py
            for w in range(n_remote):
                reg = region(outs[w], _BIG[w], shs[w], jk, c)
                pltpu.make_async_remote_copy(src_ref=reg, dst_ref=reg, send_sem=ssem.at[k * n + w],
                                             recv_sem=rsem.at[k * n + w], device_id=(px, py, c),
                                             device_id_type=MESH).wait_recv()
                cp = pltpu.make_async_remote_copy(src_ref=reg, dst_ref=reg, send_sem=fssem.at[k * n + w],
                                                  recv_sem=frsem.at[k * n + w], device_id=(x, y, 1 - c),
                                                  device_id_type=MESH)
                cp.start()
                fwd.append(cp)
            pltpu.make_async_remote_copy(src_ref=cw_stage, dst_ref=cw_out.at[jk], send_sem=cssem.at[k],
                                         recv_sem=crsem.at[k], device_id=(px, py, c),
                                         device_id_type=MESH).wait_recv()
        for k, (px, py) in enumerate(chips):
            jk = 2 * px + py
            for w in range(n_remote):
                reg = region(outs[w], _BIG[w], shs[w], jk, 1 - c)
                pltpu.make_async_remote_copy(src_ref=reg, dst_ref=reg, send_sem=fssem.at[k * n + w],
                                             recv_sem=frsem.at[k * n + w], device_id=(x, y, 1 - c),
                                             device_id_type=MESH).wait_recv()
        for cp in sends + fwd:
            cp.wait_send()
        for cp in local:
            cp.wait()

    nsem = 3 * n
    return pl.pallas_call(
        body, name="gather_first",
        in_specs=[VMEM_SPEC] * (n + 1), out_specs=[ANY] * (n + 1),
        out_shape=[SDS(fs, BF16) for fs in full_shapes] + [SDS((N_CHIPS, SUBLANES, LANES), F32)],
        scratch_shapes=[pltpu.VMEM(s_.shape, BF16) for s_ in shards] + [pltpu.VMEM((SUBLANES, LANES), F32)]
        + [pltpu.SemaphoreType.DMA((n + 1,))] + [pltpu.SemaphoreType.DMA((nsem,))] * 4
        + [pltpu.SemaphoreType.DMA((3,))] * 2,
        compiler_params=_cp())(*shards, conv_w)


def _start_all(make):
    def f(ins, outs, sems):
        for cp in make(ins, outs, sems):
            cp.start()
    return f


def _wait_all(make):
    def f(ins, outs, sems):
        for cp in make(ins, outs, sems):
            cp.wait()
    return f


def _ffn_gather_hosted(arrs):
    n = len(arrs)

    def make(outs, sems):
        ssem, rsem, fs, fr = sems
        x, y, c = _place()
        j = 2 * x + y

        def reg(w, jj, cc):
            hr = arrs[w].shape[1] // 2
            return outs[w].at[jj, pl.ds(pl.multiple_of(cc * hr, 16), hr), :]

        def rc(w, jj, cc, s_sem, r_sem, dev):
            return pltpu.make_async_remote_copy(src_ref=reg(w, jj, cc), dst_ref=reg(w, jj, cc), send_sem=s_sem,
                                                recv_sem=r_sem, device_id=dev, device_id_type=MESH)

        sends, recvs, fwds, frecvs = [], [], [], []
        for k, (px, py) in enumerate(_other_chips(x, y)):
            jk = 2 * px + py
            for w in range(n):
                q = k * n + w
                sends.append(rc(w, j, c, ssem.at[q], rsem.at[q], (px, py, c)))
                recvs.append(rc(w, jk, c, ssem.at[q], rsem.at[q], (px, py, c)))
                fwds.append(rc(w, jk, c, fs.at[q], fr.at[q], (x, y, 1 - c)))
                frecvs.append(rc(w, jk, 1 - c, fs.at[q], fr.at[q], (x, y, 1 - c)))
        return sends, recvs, fwds, frecvs

    def start(ins, outs, sems):
        for cp in make(outs, sems)[0]:
            cp.start()

    def mid(ins, outs, sems):
        _, recvs, fwds, _ = make(outs, sems)
        for r, f in zip(recvs, fwds):
            r.wait_recv()
            f.start()

    def finish(ins, outs, sems):
        sends, _, fwds, frecvs = make(outs, sems)
        for r in frecvs:
            r.wait_recv()
        for cp in sends + fwds:
            cp.wait_send()

    return _Hosted(arrs, [SDS(a.shape, a.dtype) for a in arrs], [3 * n] * 4, start, finish, mid=mid,
                   aliases={w: w for w in range(n)})


def _rs_sibling_hosted(arrs):
    n = len(arrs)

    def make(ins, outs, sems):
        x, y, c = _place()
        cps = []
        for w in range(n):
            hr = arrs[w].shape[1] // 2
            src = ins[w].at[:, pl.ds(pl.multiple_of((1 - c) * hr, SUBLANES), hr), :]
            cps.append(pltpu.make_async_remote_copy(src_ref=src, dst_ref=outs[w], send_sem=sems[0].at[w],
                                                    recv_sem=sems[1].at[w], device_id=(x, y, 1 - c),
                                                    device_id_type=MESH))
        return cps

    return _Hosted(arrs, [SDS((a.shape[0], a.shape[1] // 2, a.shape[2]), F32) for a in arrs], [n, n],
                   _start_all(make), _wait_all(make))


def _rs_chips_hosted(parts):
    n = len(parts)

    def make(ins, outs, sems):
        x, y, c = _place()
        j = 2 * x + y
        cps = []
        for k, (px, py) in enumerate(_other_chips(x, y)):
            jk = 2 * px + py
            for w in range(n):
                cps.append(pltpu.make_async_remote_copy(
                    src_ref=ins[w].at[jk], dst_ref=outs[w].at[j], send_sem=sems[0].at[k * n + w],
                    recv_sem=sems[1].at[k * n + w], device_id=(px, py, c), device_id_type=MESH))
        return cps

    return _Hosted(parts, [SDS(p.shape, p.dtype) for p in parts], [3 * n, 3 * n], _start_all(make), _wait_all(make))


def _rs_swap_hosted(halves):
    n = len(halves)

    def make(ins, outs, sems):
        x, y, c = _place()
        return [pltpu.make_async_remote_copy(src_ref=ins[w], dst_ref=outs[w], send_sem=sems[0].at[w],
                                             recv_sem=sems[1].at[w], device_id=(x, y, 1 - c), device_id_type=MESH)
                for w in range(n)]

    return _Hosted(halves, [SDS(h.shape, F32) for h in halves], [n, n], _start_all(make), _wait_all(make))


HBM_SPEC = pl.BlockSpec(memory_space=pltpu.HBM)
SEM_SPEC = pl.BlockSpec(memory_space=pltpu.SEMAPHORE)
_EFFECT = pltpu.SideEffectType.DATAFLOW_SIDE_EFFECTING


def _split_start(h, name):
    n_in, n_out, ns = len(h.ins), len(h.out_shapes), len(h.sems)
    ins = [pltpu.with_memory_space_constraint(a, pltpu.HBM) for a in h.ins]
    lands = [pltpu.with_memory_space_constraint(lax.empty(o.shape, o.dtype), pltpu.HBM) for o in h.out_shapes]

    def body(*refs):
        i_refs, l_refs = refs[:n_in], refs[n_in:n_in + n_out]
        s_refs = refs[n_in + n_out:n_in + n_out + ns]
        token = refs[-1]
        h.start(i_refs, l_refs, s_refs)
        token[...] = jnp.zeros_like(token)

    res = pl.pallas_call(
        body, name=name, in_specs=[HBM_SPEC] * (n_in + n_out),
        out_specs=[SEM_SPEC] * ns + [HBM_SPEC] * n_out + [VMEM_SPEC],
        out_shape=[pltpu.SemaphoreType.DMA((k,)) for k in h.sems]
        + [pltpu.HBM(o.shape, o.dtype) for o in h.out_shapes] + [SDS((SUBLANES, LANES), F32)],
        input_output_aliases={n_in + k: ns + k for k in range(n_out)},
        compiler_params=pltpu.CompilerParams(has_side_effects=_EFFECT))(*ins, *lands)
    return list(res[:ns]) + ins + list(res[ns:-1]), res[-1]


def _split_wait(h, state, after, name):
    n_in, n_out, ns = len(h.ins), len(h.out_shapes), len(h.sems)
    sems, bufs = state[:ns], state[ns:]

    def body(*refs):
        i_refs, l_refs = refs[:n_in], refs[n_in:n_in + n_out]
        s_refs = refs[n_in + n_out:n_in + n_out + ns]
        h.finish(i_refs, l_refs, s_refs)

    res = pl.pallas_call(
        body, name=name, in_specs=[HBM_SPEC] * (n_in + n_out) + [SEM_SPEC] * ns + [ANY],
        out_specs=[HBM_SPEC] * n_out,
        out_shape=[pltpu.HBM(b.shape, b.dtype) for b in bufs[n_in:]],
        input_output_aliases={n_in + k: k for k in range(n_out)},
        compiler_params=pltpu.CompilerParams(has_side_effects=_EFFECT))(*bufs, *sems, after)
    return list(res)


def _run_comm(hosted, name):
    return _call(lambda: None, hosted, None, name=name, grid=(), in_specs=[], out_specs=[], out_shape=[],
                 scratch_shapes=[], args=(), sem=None)[1]


def _row_tile(rows, cols, n_arrays):
    budget = 24 * 1024 * 1024 // (2 * 4 * n_arrays * cols)
    best = SUBLANES
    for t in range(SUBLANES, rows + 1, SUBLANES):
        if rows % t == 0 and t <= budget:
            best = t
    return best


def _place_index(which):
    x, y, c = _place()
    v = c if which == "c" else 2 * x + y
    return jnp.reshape(v, (1,)).astype(jnp.int32)


def _add_own_half(full, recv, name, wire=BF16):
    nsh, rows, cols = full.shape
    hr = rows // 2
    t = _row_tile(hr, cols, 4)
    nt = hr // t

    def body(c_ref, a_ref, b_ref, o_ref, ob_ref):
        v = a_ref[...] + b_ref[...]
        o_ref[...] = v
        ob_ref[...] = v.astype(wire)

    half = pl.BlockSpec((1, t, cols), lambda s_, i, c_ref: (s_, i, 0))
    return pl.pallas_call(
        body, name=name,
        grid_spec=pltpu.PrefetchScalarGridSpec(
            num_scalar_prefetch=1, grid=(nsh, nt),
            in_specs=[pl.BlockSpec((1, t, cols), lambda s_, i, c_ref: (s_, c_ref[0] * nt + i, 0)), half],
            out_specs=[half, half]),
        out_shape=[SDS((nsh, hr, cols), F32), SDS((nsh, hr, cols), wire)],
        compiler_params=_cp(("parallel", "parallel")))(_place_index("c"), full, recv)


def _sum_chips(own, recv, name):
    nsh, hr, cols = own.shape
    t = _row_tile(hr, cols, 6)

    def body(j_ref, own_ref, *rest):
        r_refs, o_ref = rest[:nsh], rest[nsh]
        j = j_ref[0]
        mine = own_ref[0]
        parts = [jnp.where(j == k, mine, r_refs[k][0].astype(F32)) for k in range(nsh)]
        o_ref[...] = ((parts[0] + parts[1]) + parts[2]) + parts[3]

    def other(k):
        return pl.BlockSpec((1, t, cols), lambda i, j_ref: (jnp.where(j_ref[0] == k, (k + 1) % nsh, k), i, 0))

    return pl.pallas_call(
        body, name=name,
        grid_spec=pltpu.PrefetchScalarGridSpec(
            num_scalar_prefetch=1, grid=(hr // t,),
            in_specs=[pl.BlockSpec((1, t, cols), lambda i, j_ref: (j_ref[0], i, 0))]
            + [other(k) for k in range(nsh)],
            out_specs=pl.BlockSpec((t, cols), lambda i, j_ref: (i, 0))),
        out_shape=SDS((hr, cols), F32), compiler_params=_cp(("parallel",)))(_place_index("j"), own, *([recv] * nsh))


def _adamw_math(w, g, m, v):
    m = ADAM_B1 * m + (1.0 - ADAM_B1) * g
    v = ADAM_B2 * v + (1.0 - ADAM_B2) * (g * g)
    m_hat = m / (1.0 - ADAM_B1 ** ADAM_STEP)
    v_hat = v / (1.0 - ADAM_B2 ** ADAM_STEP)
    delta = -ADAM_LR * (m_hat / (jnp.sqrt(v_hat) + ADAM_EPS) + ADAM_WD * w)
    return delta, m, v


def _adamw_big(w, g_own, g_sib, m, v, name, token=None):
    _, rows, cols = w.shape
    hr = rows // 2
    t = _row_tile(hr, cols, 9)
    nth = hr // t
    if token is None:
        token = jnp.zeros((SUBLANES, LANES), F32)

    def body(c_ref, w_ref, go_ref, gs_ref, m_ref, v_ref, tok_ref, g_ref, d_ref, mo_ref, vo_ref):
        own = (pl.program_id(0) // nth) == c_ref[0]
        g = jnp.where(own, go_ref[...], gs_ref[...]) + tok_ref[0:1, 0:1]
        g_ref[0] = g
        d_ref[0], mo_ref[0], vo_ref[0] = _adamw_math(w_ref[0], g, m_ref[0], v_ref[0])

    spec = pl.BlockSpec((1, t, cols), lambda i, c_ref: (0, i, 0))
    hspec = pl.BlockSpec((t, cols), lambda i, c_ref: (i % nth, 0))
    tspec = pl.BlockSpec((SUBLANES, LANES), lambda i, c_ref: (0, 0))
    return pl.pallas_call(
        body, name=name,
        grid_spec=pltpu.PrefetchScalarGridSpec(
            num_scalar_prefetch=1, grid=(2 * nth,), in_specs=[spec, hspec, hspec, spec, spec, tspec],
            out_specs=[spec] * 4),
        out_shape=[SDS((1, rows, cols), F32)] * 4,
        compiler_params=_cp(("parallel",)))(_place_index("c"), w, g_own, g_sib, m, v, token)


def _build_slab(mix_slab, dg_mix, dg_ffn, dg_fin, loss8):
    def body(ms_ref, gm_ref, gf_ref, gn_ref, loss_ref, out_ref):
        rows = []
        for ref in (gm_ref, gf_ref, gn_ref):
            v = jnp.sum(ref[...], axis=0, keepdims=True)
            rows += [v[:, :SLAB_W], v[:, SLAB_W:]]
        rows.append(jnp.concatenate([loss_ref[0:1, :]] * (SLAB_W // LANES), axis=1))
        rows.append(jnp.zeros((SLAB_ROWS - ROW_LOSS - 1, SLAB_W), F32))
        tail = jnp.concatenate(rows, axis=0)
        for k in range(N_CHIPS):
            out_ref[k, 0:MIX_SLAB_ROWS, :] = ms_ref[...]
            out_ref[k, MIX_SLAB_ROWS:SLAB_ROWS, :] = tail

    return pl.pallas_call(
        body, name="build_slab", in_specs=[VMEM_SPEC] * 5, out_specs=VMEM_SPEC,
        out_shape=SDS((N_CHIPS, SLAB_ROWS, SLAB_W), F32),
        compiler_params=_cp())(mix_slab, dg_mix, dg_ffn, dg_fin, loss8)


_SMALL_ROWS = (("conv_b", ROW_CONV_B), ("gate_a_b", ROW_BA), ("gate_x_b", ROW_BX), ("lru_lambda", ROW_LAM),
               ("pool_b", ROW_PB), ("pool_scale", ROW_PS), ("norm_lru_g", ROW_GL), ("norm_pool_g", ROW_GP))
_WIDE_ROWS = (("norm_mix_g", ROW_MIX), ("norm_ffn_g", ROW_FFN), ("final_norm_g", ROW_FIN))
_BLOCK_ROWS = (("gate_a_w", ROW_GA), ("gate_x_w", ROW_GX), ("pool_w", ROW_PW))
_SMALL_ORDER = tuple(n for n, _ in _SMALL_ROWS) + tuple(n for n, _ in _WIDE_ROWS) + tuple(
    n for n, _ in _BLOCK_ROWS) + ("conv_w",)


def _adamw_small(slab_own, slab_sib, wmv):
    names = _SMALL_ORDER
    flat = [a for nme in names for a in wmv[nme]]
    nin = len(flat)

    def body(*refs):
        own_ref, sib_ref, j_ref = refs[0], refs[1], refs[2]
        ins = refs[3:3 + nin]
        outs = refs[3 + nin:-1]
        first = j_ref[1] == 0
        slab_ref = jnp.concatenate([jnp.where(first, own_ref[...], sib_ref[...]),
                                    jnp.where(first, sib_ref[...], own_ref[...])], axis=0)
        refs[-1][...] = jnp.broadcast_to(slab_ref[ROW_LOSS:ROW_LOSS + 1, 0:LANES], (SUBLANES, LANES))
        grads = {}
        for nme, row in _SMALL_ROWS:
            grads[nme] = slab_ref[row:row + 1, :]
        for nme, row in _WIDE_ROWS:
            grads[nme] = jnp.concatenate([slab_ref[row:row + 1, :], slab_ref[row + 1:row + 2, :]], axis=1)
        full = slab_ref[ROW_CONV_W:ROW_CONV_W + CONV_WIDTH, :]
        jv = j_ref[0]
        g = jnp.zeros((CONV_WIDTH, LANES), F32)
        for jj in range(N_CHIPS):
            g = jnp.where(jv == jj, full[:, jj * LANES:(jj + 1) * LANES], g)
        grads["conv_w"] = g
        block_rows = dict(_BLOCK_ROWS)
        for idx, nme in enumerate(names):
            w_ref, m_ref, v_ref = ins[3 * idx:3 * idx + 3]
            if nme in block_rows:
                nblk, r, c = w_ref.shape
                parts = [(b, slab_ref[block_rows[nme]:block_rows[nme] + r, b * c:(b + 1) * c]) for b in range(nblk)]
            else:
                parts = [(Ellipsis, grads[nme])]
            for b, g in parts:
                delta, m, v = _adamw_math(w_ref[b], g, m_ref[b], v_ref[b])
                outs[4 * idx][b] = g
                outs[4 * idx + 1][b] = delta
                outs[4 * idx + 2][b] = m
                outs[4 * idx + 3][b] = v

    place = jnp.concatenate([_place_index("j"), _place_index("c")])
    out_shape = [SDS(wmv[nme][0].shape, F32) for nme in names for _ in range(4)] + [SDS((SUBLANES, LANES), F32)]
    res = pl.pallas_call(
        body, name="adamw_small",
        in_specs=[VMEM_SPEC, VMEM_SPEC, pl.BlockSpec(memory_space=pltpu.SMEM)] + [VMEM_SPEC] * nin,
        out_specs=[VMEM_SPEC] * len(out_shape), out_shape=out_shape,
        compiler_params=_cp())(slab_own, slab_sib, place, *flat)
    return {nme: tuple(res[4 * idx:4 * idx + 4]) for idx, nme in enumerate(names)}, res[-1]


_FFN = ("ffn_w1", "ffn_w3", "ffn_w2")
_TRANSPOSED = ("ffn_w1", "ffn_w3")


def _local_step(x, target, full, sp_, distributed):
    d = x.shape[1]
    (u,), got = _inproj(x, sp_["norm_mix_g"], full["w_in"],
                        [_ffn_gather_hosted([full["w_out"]])] if distributed else None)
    w_out = (got[0][0] if distributed else full["w_out"]).reshape(d, d)
    gather = [_ffn_gather_hosted([full[n] for n in _FFN])] if distributed else None
    (h, yn, hres1, saved, pooled), got = _mixer_fwd(u, x, sp_, w_out, gather)
    w1, w3, w2 = got[0] if distributed else [full[n] for n in _FFN]
    h2, a1, a3, ff = _ffn_up(hres1, sp_["norm_ffn_g"], w1, w3)
    dh, dhb, loss8, dg_fin = _ffn_down(ff, hres1, target, sp_["final_norm_g"], w2)
    da1, da3 = _ffn_bwd_gate(dhb, a1, a3, w2)
    dws = list(_ffn_wgrad(h2, dhb, ff, da1, da3))
    rs1 = [_rs_sibling_hosted(dws)] if distributed else None
    (dhres1, dg_ffn), got = _ffn_bwd_down(da1, da3, dh, hres1, sp_["norm_ffn_g"], w1, w3, rs1)
    rs2 = None
    if distributed:
        pairs = [_add_own_half(a, r, "add_half_" + n) for n, a, r in zip(_FFN, dws, got[0])]
        rs2 = [_rs_chips_hosted([pb for _, pb in pairs])]
    (du, mix_slab), got = _mixer_bwd(u, saved, pooled, h, dhres1, sp_, w_out, rs2)
    g_mix = sp_["norm_mix_g"]
    if distributed:
        fin = [_sum_chips(pairs[k][0], got[0][k], "sum_chips_" + n) for k, n in enumerate(_FFN)]
        swap = _rs_swap_hosted(fin)
        state, token = _split_start(swap, "ffn_swap_start")
        g_mix = g_mix + token[0:1, 0:1]
    (gx, dwin, dwout, dg_mix), _ = _inproj_bwd(x, du, dhres1, yn, g_mix, full["w_in"])
    if distributed:
        sib = _split_wait(swap, state, dg_mix, "ffn_swap_wait")
    big = {"w_in": dwin, "w_out": dwout.reshape(N_CHIPS, d // N_CHIPS, d)}
    for k, n in enumerate(_FFN):
        big[n] = (fin[k], sib[k]) if distributed else dws[k]
    return gx, big, (mix_slab, dg_mix, dg_ffn, dg_fin, loss8)


_SMALL_LAYOUT = {
    "gate_a_w": (lambda a: a[0], lambda a: a[None]),
    "gate_x_w": (lambda a: a[0], lambda a: a[None]),
    "pool_w": (lambda a: a[0], lambda a: a[None]),
    "conv_w": (lambda a: a[0], lambda a: a[None]),
    "final_norm_g": (lambda a: a[None], lambda a: a[0]),
}

_WEIGHTS = ("norm_mix_g", "w_in", "conv_w", "conv_b", "gate_a_w", "gate_a_b", "gate_x_w", "gate_x_b", "lru_lambda",
            "pool_w", "pool_b", "pool_scale", "norm_lru_g", "norm_pool_g", "w_out", "norm_ffn_g", "ffn_w1",
            "ffn_w3", "ffn_w2", "final_norm_g")


def kernel(x, norm_mix_g, w_in, conv_w, conv_b, gate_a_w, gate_a_b, gate_x_w, gate_x_b, lru_lambda, pool_w, pool_b, pool_scale, norm_lru_g, norm_pool_g, w_out, norm_ffn_g, ffn_w1, ffn_w3, ffn_w2, final_norm_g, loss_target, m_norm_mix_g, m_w_in, m_conv_w, m_conv_b, m_gate_a_w, m_gate_a_b, m_gate_x_w, m_gate_x_b, m_lru_lambda, m_pool_w, m_pool_b, m_pool_scale, m_norm_lru_g, m_norm_pool_g, m_w_out, m_norm_ffn_g, m_ffn_w1, m_ffn_w3, m_ffn_w2, m_final_norm_g, v_norm_mix_g, v_w_in, v_conv_w, v_conv_b, v_gate_a_w, v_gate_a_b, v_gate_x_w, v_gate_x_b, v_lru_lambda, v_pool_w, v_pool_b, v_pool_scale, v_norm_lru_g, v_norm_pool_g, v_w_out, v_norm_ffn_g, v_ffn_w1, v_ffn_w3, v_ffn_w2, v_final_norm_g):
    loc = locals()
    w = {n: loc[n] for n in _WEIGHTS}
    m = {n: loc["m_" + n] for n in _WEIGHTS}
    v = {n: loc["v_" + n] for n in _WEIGHTS}

    def lay(nme, a):
        return _SMALL_LAYOUT[nme][0](a) if nme in _SMALL_LAYOUT else a

    def unlay(nme, a):
        return _SMALL_LAYOUT[nme][1](a) if nme in _SMALL_LAYOUT else a

    for group in (w, m, v):
        for n in _TRANSPOSED:
            group[n] = jnp.transpose(group[n], (0, 2, 1))

    gathered = _gather_weights([w[n][0] for n in _BIG], w["conv_w"][0], n_remote=1)
    full = dict(zip(_BIG, gathered[:-1]))
    cw_all = gathered[-1]
    sp_ = {n: lay(n, w[n]) for n in _SMALL_ORDER}
    sp_["conv_w"] = jnp.transpose(cw_all[:, :CONV_WIDTH, :], (1, 0, 2)).reshape(CONV_WIDTH, N_CHIPS * LANES)

    gx, big, small = _local_step(x[0], loss_target[0], full, sp_, distributed=True)

    late = ("w_in", "w_out", "slab")
    big["slab"] = _build_slab(*small)
    fin = {n: big[n][0] for n in _FFN}
    sib = {n: big[n][1] for n in _FFN}
    recv1, = _run_comm([_rs_sibling_hosted([big[n] for n in late])], "tail_sibling")
    pairs = [_add_own_half(big[n], r, "add_half_" + n, F32 if n == "slab" else BF16) for n, r in zip(late, recv1)]
    chips = _rs_chips_hosted([pb for _, pb in pairs])
    state, token = _split_start(chips, "tail_chips_start")
    out = {}
    for n in _FFN:
        out[n] = tuple(_adamw_big(w[n], fin[n], sib[n], m[n], v[n], "adamw_" + n, token))
    recv2 = _split_wait(chips, state, out[_FFN[-1]][1], "tail_chips_wait")
    for n, (p, _), r in zip(late, pairs, recv2):
        fin[n] = _sum_chips(p, r, "sum_chips_" + n)
    swapped, = _run_comm([_rs_swap_hosted([fin[n] for n in late])], "tail_swap")
    sib.update(zip(late, swapped))
    for n in late[:2]:
        out[n] = tuple(_adamw_big(w[n], fin[n], sib[n], m[n], v[n], "adamw_" + n))
    for n in _TRANSPOSED:
        out[n] = tuple(jnp.transpose(a, (0, 2, 1)) for a in out[n])
    wmv = {n: (lay(n, w[n]), lay(n, m[n]), lay(n, v[n])) for n in _SMALL_ORDER}
    res, loss = _adamw_small(fin["slab"], sib["slab"], wmv)
    for n in _SMALL_ORDER:
        out[n] = tuple(unlay(n, a) for a in res[n])
    return (loss[0, 0], gx[None]) + tuple(out[n][k] for k in range(4) for n in _WEIGHTS)
```

```python
import functools
import math

import jax
import jax.numpy as jnp
from jax import lax
from jax.experimental import pallas as pl
from jax.experimental.pallas import tpu as pltpu

F32 = jnp.float32
BF16 = jnp.bfloat16
SDS = jax.ShapeDtypeStruct
MESH = pl.DeviceIdType.MESH

EPS = 1e-6
LRU_C = 8.0
CONV_WIDTH = 4
POOL_WINDOWS = (2, 4, 8, 16)
HALO = 16
LANES = 128
SUBLANES = 8
GATE_BLOCK = 256
N_CHIPS = 4

ADAM_LR = 0.001
ADAM_B1 = 0.9
ADAM_B2 = 0.999
ADAM_EPS = 1e-08
ADAM_WD = 0.01
ADAM_STEP = 10

TM_PROJ = 512
TM_MIX = 512
TM_FFN = 512
TM_WGRAD = 2048
MIX_SAVED = ("xc", "a", "m2raw")
MIX_SAVED_BF16 = ("r", "ig", "ge", "dge")
FFN_ROW_CHUNKS = 2
VMEM_LIMIT = 56 * 1024 * 1024

SLAB_W = 512
ROW_CONV_B, ROW_CONV_W, ROW_BA, ROW_BX, ROW_LAM, ROW_PB, ROW_PS, ROW_GL, ROW_GP = 0, 1, 5, 6, 7, 8, 9, 10, 11
ROW_GA, ROW_GX, ROW_PW = 16, 80, 144
ROW_MIX, ROW_FFN, ROW_FIN, ROW_LOSS = 272, 274, 276, 278
MIX_SLAB_ROWS = 272
SLAB_ROWS = 288


def _cp(sem=None, **kw):
    if sem is not None:
        kw["dimension_semantics"] = sem
    return pltpu.CompilerParams(vmem_limit_bytes=VMEM_LIMIT, **kw)


def _const_spec(shape):
    nd = len(shape)
    return pl.BlockSpec(shape, lambda *_: (0,) * nd, pipeline_mode=pl.Buffered(1))


def _sigmoid(x):
    return 1.0 / (1.0 + jnp.exp(-x))


def _dot(a, b):
    return jnp.dot(a, b, preferred_element_type=F32)


def _dot_nt(a, b):
    return lax.dot_general(a, b, (((1,), (1,)), ((), ())), preferred_element_type=F32)


def _dot_tn(a, b):
    return lax.dot_general(a, b, (((0,), (0,)), ((), ())), preferred_element_type=F32)


def _colsum8(v):
    m, c = v.shape
    return v.reshape(m // SUBLANES, SUBLANES, c).sum(axis=0)


def _rowmean(v):
    return jnp.mean(v, axis=-1, keepdims=True)


def _rms_bwd(dy, xhat, r, g):
    dxh = dy * g
    return r * (dxh - xhat * _rowmean(dxh * xhat))


def _softplus_neg(lam):
    z = -lam
    e = jnp.exp(-jnp.abs(z))
    u = 1.0 + e
    d = u - 1.0
    log1p = jnp.where(d == 0.0, e, jnp.log(u) * (e / jnp.where(d == 0.0, 1.0, d)))
    return jnp.maximum(z, 0.0) + log1p


def _neg_expm1(z):
    series = -(z * (1.0 + z * (0.5 + z * (1.0 / 6.0 + z * (1.0 / 24.0)))))
    return jnp.where(z > -0.03, series, 1.0 - jnp.exp(z))


_GELU_C = math.sqrt(2.0 / math.pi)
_GELU_K = 0.044715


def _gelu_parts(x):
    x2 = x * x
    th = jnp.tanh(_GELU_C * (x + _GELU_K * x2 * x))
    ge = 0.5 * x * (1.0 + th)
    dge = 0.5 * (1.0 + th) + 0.5 * x * (1.0 - th * th) * (_GELU_C * (1.0 + 3.0 * _GELU_K * x2))
    return ge, dge


def _shift_down(halo, tile, k):
    if k == 0:
        return tile
    ext = jnp.concatenate([halo, tile], axis=0)
    n = tile.shape[0]
    h = halo.shape[0]
    return ext[h - k:h - k + n]


def _shift_up(tile, nxt, k):
    if k == 0:
        return tile
    ext = jnp.concatenate([tile, nxt], axis=0)
    return ext[k:k + tile.shape[0]]


def _build_gate_blocks(ga_ref, gx_ref, gw_ref):
    hd = ga_ref.shape[1]
    per = GATE_BLOCK // hd
    zero = jnp.zeros((hd, hd), F32)
    for b in range(gw_ref.shape[0]):
        for src, off in ((ga_ref, 0), (gx_ref, GATE_BLOCK)):
            for hh in range(per):
                row = jnp.concatenate([zero] * hh + [src[b * per + hh]] + [zero] * (per - 1 - hh), axis=1)
                gw_ref[b, hh * hd:(hh + 1) * hd, off:off + GATE_BLOCK] = row.astype(BF16)


def _scan_level1(a, b, reverse):
    m, c = a.shape
    a3 = a.reshape(m // SUBLANES, SUBLANES, c)
    b3 = b.reshape(m // SUBLANES, SUBLANES, c)
    row = lax.broadcasted_iota(jnp.int32, a3.shape, 1)
    for s in (1, 2, 4):
        sh = (SUBLANES - s) if reverse else s
        a_sh = pltpu.roll(a3, sh, 1)
        b_sh = pltpu.roll(b3, sh, 1)
        ok = (row < SUBLANES - s) if reverse else (row >= s)
        b3 = jnp.where(ok, a3 * b_sh + b3, b3)
        a3 = jnp.where(ok, a3 * a_sh, a3)
    return a3.reshape(m, c), b3.reshape(m, c)


def _scan_level2(a_ref, b_ref, out_ref, carry, reverse):
    m, c = a_ref.shape
    ng = m // SUBLANES

    def step(g, cr):
        gi = (ng - 1 - g) if reverse else g
        off = pl.multiple_of(gi * SUBLANES, SUBLANES)
        h = b_ref[pl.ds(off, SUBLANES), :] + a_ref[pl.ds(off, SUBLANES), :] * cr
        out_ref[pl.ds(off, SUBLANES), :] = h
        edge = h[0:1, :] if reverse else h[SUBLANES - 1:SUBLANES, :]
        return jnp.broadcast_to(edge, (SUBLANES, c))

    return lax.fori_loop(0, ng, step, carry, unroll=4)


def _mixer_recompute(u, hal, t0, cw, cb, gw_ref, ba, bx, lam, pw_ref, pb, ps):
    tm = u.shape[0]
    lw = cb.shape[1]
    u_l, u_g, u_p = u[:, :lw], u[:, lw:2 * lw], u[:, 2 * lw:]
    hal_l, hal_p = hal[:, :lw], hal[:, 2 * lw:]
    taps = [_shift_down(hal_l, u_l, CONV_WIDTH - 1 - k) for k in range(CONV_WIDTH)]
    xc = cb
    for k in range(CONV_WIDTH):
        xc = xc + taps[k] * cw[k:k + 1, :]
    xcb = xc.astype(BF16)
    nb = lw // GATE_BLOCK
    gs = [_dot(xcb[:, b * GATE_BLOCK:(b + 1) * GATE_BLOCK], gw_ref[b]) for b in range(nb)]
    r = _sigmoid(jnp.concatenate([g[:, :GATE_BLOCK] for g in gs], axis=1) + ba)
    ig = _sigmoid(jnp.concatenate([g[:, GATE_BLOCK:] for g in gs], axis=1) + bx)
    sp = _softplus_neg(lam)
    la = (-LRU_C * r) * sp
    a = jnp.exp(la)
    m2raw = _neg_expm1(2.0 * la)
    mult = jnp.sqrt(jnp.maximum(m2raw, 1e-12))
    ge, dge = _gelu_parts(u_g)
    row = lax.broadcasted_iota(jnp.int32, (tm, LANES), 0) + t0
    pooled, invs, zs = [], [], []
    for gi, w in enumerate(POOL_WINDOWS):
        e = jnp.concatenate([hal_p[:, gi * LANES:(gi + 1) * LANES], u_p[:, gi * LANES:(gi + 1) * LANES]], axis=0)
        s = e
        k = 1
        while k < w:
            s = s + pltpu.roll(s, k, 0)
            k *= 2
        inv = 1.0 / jnp.minimum(row + 1, w).astype(F32)
        pg = s[HALO:] * inv - e[HALO:]
        pooled.append(pg)
        invs.append(inv)
        zs.append(_dot(pg.astype(BF16), pw_ref[gi].astype(BF16)))
    z = jnp.concatenate(zs, axis=1) + pb
    y_pool = z * ps
    return dict(u_l=u_l, u_g=u_g, taps=taps, xc=xc, xcb=xcb, r=r, ig=ig, sp=sp, la=la, a=a, m2raw=m2raw,
                mult=mult, ge=ge, dge=dge, pooled=pooled, invs=invs, z=z, y_pool=y_pool)


ANY = pl.BlockSpec(memory_space=pl.ANY)
VMEM_SPEC = pl.BlockSpec(memory_space=pltpu.VMEM)


class _Hosted:
    def __init__(self, ins, out_shapes, sems, start, finish, mid=None, aliases=None):
        self.ins, self.out_shapes, self.sems = list(ins), list(out_shapes), list(sems)
        self.start, self.mid, self.finish = start, mid, finish
        self.aliases = dict(aliases or {})


def _call(body, hosted, stage_preds, *, name, grid, in_specs, out_specs, out_shape, scratch_shapes, args, sem):
    hosted = list(hosted or [])
    n_in, n_out, n_scr = len(in_specs), len(out_specs), len(scratch_shapes)
    c_in = [a for h in hosted for a in h.ins]
    c_out = [o for h in hosted for o in h.out_shapes]
    c_sem = [pltpu.SemaphoreType.DMA((k,)) for h in hosted for k in h.sems]

    def full(*refs):
        p = 0
        parts = []
        for cnt in (n_in, len(c_in), n_out, len(c_out), n_scr, len(c_sem)):
            parts.append(refs[p:p + cnt])
            p += cnt
        hi, ci, ho, co, hs, cs = parts
        per = []
        a = b = c_ = 0
        for h in hosted:
            per.append((h, ci[a:a + len(h.ins)], co[b:b + len(h.out_shapes)], cs[c_:c_ + len(h.sems)]))
            a, b, c_ = a + len(h.ins), b + len(h.out_shapes), c_ + len(h.sems)
        first = mid = last = None
        if hosted and grid:
            first, mid, last = stage_preds()

        def run(fn, pred, i_, o_, s_):
            if fn is None:
                return
            if pred is None:
                fn(i_, o_, s_)
            else:
                pl.when(pred)(functools.partial(fn, i_, o_, s_))

        for h, i_, o_, s_ in per:
            run(h.start, first, i_, o_, s_)
        body(*hi, *ho, *hs)
        for h, i_, o_, s_ in per:
            run(h.mid, mid, i_, o_, s_)
        for h, i_, o_, s_ in per:
            run(h.finish, last, i_, o_, s_)

    aliases = {}
    a = b = 0
    for h in hosted:
        for k, v in h.aliases.items():
            aliases[n_in + a + k] = n_out + b + v
        a, b = a + len(h.ins), b + len(h.out_shapes)
    res = pl.pallas_call(
        full, name=name, grid=grid, in_specs=list(in_specs) + [ANY] * len(c_in),
        out_specs=list(out_specs) + [ANY] * len(c_out), out_shape=list(out_shape) + c_out,
        scratch_shapes=list(scratch_shapes) + c_sem, input_output_aliases=aliases,
        compiler_params=_cp(sem))(*args, *c_in)
    res = list(res)
    outs = []
    p = n_out
    for h in hosted:
        outs.append(res[p:p + len(h.out_shapes)])
        p += len(h.out_shapes)
    return res[:n_out], outs


def _inproj(x, g_mix, w_in, hosted=None):
    s, d = x.shape
    n = w_in.shape[1]
    tm = min(TM_PROJ, s)
    nt = s // tm

    def body(x_ref, g_ref, w_ref, u_ref):
        xv = x_ref[...]
        r = lax.rsqrt(_rowmean(xv * xv) + EPS)
        u_ref[...] = _dot((xv * r * g_ref[...]).astype(BF16), w_ref[...])

    def stages():
        i = pl.program_id(0)
        return i == 0, i == max(nt - 3, 0), i == nt - 1

    return _call(
        body, hosted, stages, grid=(nt,), name="inproj",
        in_specs=[pl.BlockSpec((tm, d), lambda i: (i, 0)), _const_spec((1, d)), _const_spec((d, n))],
        out_specs=[pl.BlockSpec((tm, n), lambda i: (i, 0))], out_shape=[SDS((s, n), F32)], scratch_shapes=[],
        args=(x, g_mix, w_in), sem=("arbitrary",))


def _mixer_fwd(u, x, sp_, w_out, hosted=None):
    s, din = u.shape
    d = x.shape[1]
    lw = din // 3
    tm = min(TM_MIX, s)
    nb = lw // GATE_BLOCK

    def body(u_ref, halo_ref, x_ref, cw_ref, cb_ref, ga_ref, gx_ref, ba_ref, bx_ref, lam_ref, pw_ref, pb_ref,
             ps_ref, gl_ref, gp_ref, wout_ref, h_ref, yn_ref, hres_ref, saved_ref, pooled_ref,
             gw_s, a_s, b_s, carry_s):
        i = pl.program_id(0)

        @pl.when(i == 0)
        def _():
            _build_gate_blocks(ga_ref, gx_ref, gw_s)
            carry_s[...] = jnp.zeros_like(carry_s)

        uv = u_ref[...]
        hal = jnp.where(i > 0, halo_ref[...], 0.0)
        f = _mixer_recompute(uv, hal, i * tm, cw_ref[...], cb_ref[...], gw_s, ba_ref[...], bx_ref[...],
                             lam_ref[...], pw_ref, pb_ref[...], ps_ref[...])
        for k, name in enumerate(MIX_SAVED):
            saved_ref[k] = f[name]
        for k, name in enumerate(MIX_SAVED_BF16):
            pooled_ref[k] = f[name].astype(BF16)
        pooled_ref[len(MIX_SAVED_BF16)] = jnp.concatenate(f["pooled"], axis=1).astype(BF16)
        bb = f["mult"] * (f["ig"] * f["xc"])
        a1, b1 = _scan_level1(f["a"], bb, reverse=False)
        a_s[...] = a1
        b_s[...] = b1
        carry_s[...] = _scan_level2(a_s, b_s, h_ref, carry_s[...], reverse=False)
        y_lru = h_ref[...] * f["ge"]
        rl = lax.rsqrt(_rowmean(y_lru * y_lru) + EPS)
        yp = f["y_pool"]
        rp = lax.rsqrt(_rowmean(yp * yp) + EPS)
        yn = jnp.concatenate([y_lru * rl * gl_ref[...], yp * rp * gp_ref[...]], axis=1).astype(BF16)
        yn_ref[...] = yn
        hres_ref[...] = x_ref[...] + _dot(yn, wout_ref[...])

    small = [sp_[k] for k in ("conv_w", "conv_b", "gate_a_w", "gate_x_w", "gate_a_b", "gate_x_b", "lru_lambda",
                              "pool_w", "pool_b", "pool_scale", "norm_lru_g", "norm_pool_g")]
    nt = s // tm

    def stages():
        i = pl.program_id(0)
        return i == 0, i == max(nt - 3, 0), i == nt - 1

    return _call(
        body, hosted, stages, grid=(nt,), name="mixer_fwd",
        in_specs=[pl.BlockSpec((tm, din), lambda i: (i, 0)),
                  pl.BlockSpec((HALO, din), lambda i: (jnp.maximum(i * (tm // HALO) - 1, 0), 0)),
                  pl.BlockSpec((tm, d), lambda i: (i, 0))]
        + [_const_spec(a.shape) for a in small] + [_const_spec(w_out.shape)],
        out_specs=[pl.BlockSpec((tm, lw), lambda i: (i, 0)), pl.BlockSpec((tm, d), lambda i: (i, 0)),
                   pl.BlockSpec((tm, d), lambda i: (i, 0)),
                   pl.BlockSpec((len(MIX_SAVED), tm, lw), lambda i: (0, i, 0)),
                   pl.BlockSpec((len(MIX_SAVED_BF16) + 1, tm, lw), lambda i: (0, i, 0))],
        out_shape=[SDS((s, lw), F32), SDS((s, d), BF16), SDS((s, d), F32), SDS((len(MIX_SAVED), s, lw), F32),
                   SDS((len(MIX_SAVED_BF16) + 1, s, lw), BF16)],
        scratch_shapes=[pltpu.VMEM((nb, GATE_BLOCK, 2 * GATE_BLOCK), BF16), pltpu.VMEM((tm, lw), F32),
                        pltpu.VMEM((tm, lw), F32), pltpu.VMEM((SUBLANES, lw), F32)],
        args=(u, u, x, *small, w_out), sem=("arbitrary",))


def _row_chunks(tm):
    rc = tm // FFN_ROW_CHUNKS
    return [slice(q * rc, (q + 1) * rc) for q in range(FFN_ROW_CHUNKS)]


def _ffn_up(hres1, g_ffn, w1, w3):
    s, d = hres1.shape
    nj, fc, _ = w1.shape
    tm = min(TM_FFN, s)

    def body(h_ref, gf_ref, w1_ref, w3_ref, h2_ref, a1_ref, a3_ref, ff_ref):
        hv = h_ref[...]
        r = lax.rsqrt(_rowmean(hv * hv) + EPS)
        h2_ref[...] = (hv * r * gf_ref[...]).astype(BF16)
        h2 = h2_ref[...]
        for j in range(nj):
            a1 = _dot_nt(h2, w1_ref[j])
            a3 = _dot_nt(h2, w3_ref[j])
            a1_ref[j] = a1.astype(BF16)
            a3_ref[j] = a3.astype(BF16)
            ff_ref[j] = ((a1 * _sigmoid(a1)) * a3).astype(BF16)

    wspec = _const_spec(w1.shape)
    aspec = pl.BlockSpec((nj, tm, fc), lambda i: (0, i, 0))
    return pl.pallas_call(
        body, grid=(s // tm,), name="ffn_up",
        in_specs=[pl.BlockSpec((tm, d), lambda i: (i, 0)), _const_spec((1, d)), wspec, wspec],
        out_specs=[pl.BlockSpec((tm, d), lambda i: (i, 0)), aspec, aspec, aspec],
        out_shape=[SDS((s, d), BF16)] + [SDS((nj, s, fc), BF16)] * 3,
        compiler_params=_cp(("parallel",)))(hres1, g_ffn, w1, w3)


def _ffn_down(ff, hres1, target, g_fin, w2):
    s, d = hres1.shape
    nj, _, fc = ff.shape
    tm = min(TM_FFN, s)

    def body(ff_ref, h_ref, t_ref, gn_ref, w2_ref, dh_ref, dhb_ref, loss_ref, dgn_ref):
        @pl.when(pl.program_id(0) == 0)
        def _():
            loss_ref[...] = jnp.zeros_like(loss_ref)
            dgn_ref[...] = jnp.zeros_like(dgn_ref)

        gn = gn_ref[...]
        for rows in _row_chunks(tm):
            acc = _dot(ff_ref[0, rows, :], w2_ref[0])
            for j in range(1, nj):
                acc = acc + _dot(ff_ref[j, rows, :], w2_ref[j])
            hr2 = h_ref[rows, :] + acc
            r2 = lax.rsqrt(_rowmean(hr2 * hr2) + EPS)
            xh = hr2 * r2
            diff = xh * gn - t_ref[rows, :]
            tot = jnp.sum(jnp.sum(diff * diff, axis=1, keepdims=True), axis=0, keepdims=True)
            loss_ref[...] += tot * (0.5 / d)
            dout = diff * (1.0 / d)
            dgn_ref[...] += _colsum8(dout * xh)
            dh = _rms_bwd(dout, xh, r2, gn)
            dh_ref[rows, :] = dh
            dhb_ref[rows, :] = dh.astype(BF16)

    tile = pl.BlockSpec((tm, d), lambda i: (i, 0))
    return pl.pallas_call(
        body, grid=(s // tm,), name="ffn_down",
        in_specs=[pl.BlockSpec((nj, tm, fc), lambda i: (0, i, 0)), tile, tile, _const_spec((1, d)),
                  _const_spec(w2.shape)],
        out_specs=[tile, tile, pl.BlockSpec((SUBLANES, LANES), lambda i: (0, 0)),
                   pl.BlockSpec((SUBLANES, d), lambda i: (0, 0))],
        out_shape=[SDS((s, d), F32), SDS((s, d), BF16), SDS((SUBLANES, LANES), F32), SDS((SUBLANES, d), F32)],
        compiler_params=_cp(("arbitrary",)))(ff, hres1, target, g_fin, w2)


def _ffn_bwd_gate(dhb, a1, a3, w2):
    s, d = dhb.shape
    nj, _, fc = a1.shape
    tm = min(TM_FFN, s)

    def body(dhb_ref, a1_ref, a3_ref, w2_ref, da1_ref, da3_ref):
        for j in range(nj):
            for rows in _row_chunks(tm):
                dff = _dot_nt(dhb_ref[rows, :], w2_ref[j])
                a1v = a1_ref[j, rows, :].astype(F32)
                sg = _sigmoid(a1v)
                silu = a1v * sg
                da1_ref[j, rows, :] = (dff * a3_ref[j, rows, :].astype(F32)
                                       * (sg * (1.0 + (a1v - silu)))).astype(BF16)
                da3_ref[j, rows, :] = (dff * silu).astype(BF16)

    aspec = pl.BlockSpec((nj, tm, fc), lambda i: (0, i, 0))
    return pl.pallas_call(
        body, grid=(s // tm,), name="ffn_bwd_gate",
        in_specs=[pl.BlockSpec((tm, d), lambda i: (i, 0)), aspec, aspec, _const_spec(w2.shape)],
        out_specs=[aspec, aspec], out_shape=[SDS((nj, s, fc), BF16)] * 2,
        compiler_params=_cp(("parallel",)))(dhb, a1, a3, w2)


def _ffn_bwd_down(da1, da3, dh, hres1, g_ffn, w1, w3, hosted=None):
    s, d = hres1.shape
    nj, _, fc = da1.shape
    tm = min(TM_FFN, s)
    nt = s // tm

    def body(da1_ref, da3_ref, dh_ref, h_ref, gf_ref, w1_ref, w3_ref, dhr_ref, dgf_ref):
        @pl.when(pl.program_id(0) == 0)
        def _():
            dgf_ref[...] = jnp.zeros_like(dgf_ref)

        gf = gf_ref[...]
        for rows in _row_chunks(tm):
            dh2 = None
            for j in range(nj):
                part = _dot(da1_ref[j, rows, :], w1_ref[j]) + _dot(da3_ref[j, rows, :], w3_ref[j])
                dh2 = part if dh2 is None else dh2 + part
            hv = h_ref[rows, :]
            r = lax.rsqrt(_rowmean(hv * hv) + EPS)
            xh = hv * r
            dgf_ref[...] += _colsum8(dh2 * xh)
            dhr_ref[rows, :] = dh_ref[rows, :] + _rms_bwd(dh2, xh, r, gf)

    tile = pl.BlockSpec((tm, d), lambda i: (i, 0))
    aspec = pl.BlockSpec((nj, tm, fc), lambda i: (0, i, 0))
    wspec = _const_spec(w1.shape)

    def stages():
        i = pl.program_id(0)
        return i == 0, i == max(nt - 2, 0), i == nt - 1

    return _call(
        body, hosted, stages, grid=(nt,), name="ffn_bwd_down",
        in_specs=[aspec, aspec, tile, tile, _const_spec((1, d)), wspec, wspec],
        out_specs=[tile, pl.BlockSpec((SUBLANES, d), lambda i: (0, 0))],
        out_shape=[SDS((s, d), F32), SDS((SUBLANES, d), F32)],
        scratch_shapes=[], args=(da1, da3, dh, hres1, g_ffn, w1, w3), sem=("arbitrary",))


def _ffn_wgrad(h2, dhb, ff, da1, da3):
    s, d = h2.shape
    _, _, fc = ff.shape
    tm = min(TM_WGRAD, s)

    def body(h2_ref, dhb_ref, ff_ref, da1_ref, da3_ref, dw1_ref, dw3_ref, dw2_ref):
        @pl.when(pl.program_id(1) == 0)
        def _():
            dw1_ref[...] = jnp.zeros_like(dw1_ref)
            dw3_ref[...] = jnp.zeros_like(dw3_ref)
            dw2_ref[...] = jnp.zeros_like(dw2_ref)

        h2v = h2_ref[...]
        dw1_ref[0] += _dot_tn(da1_ref[0], h2v)
        dw3_ref[0] += _dot_tn(da3_ref[0], h2v)
        dw2_ref[0] += _dot_tn(ff_ref[0], dhb_ref[...])

    wspec = pl.BlockSpec((1, fc, d), lambda j, i: (j, 0, 0))
    return pl.pallas_call(
        body, grid=(N_CHIPS, s // tm), name="ffn_wgrad",
        in_specs=[pl.BlockSpec((tm, d), lambda j, i: (i, 0)), pl.BlockSpec((tm, d), lambda j, i: (i, 0))]
        + [pl.BlockSpec((1, tm, fc), lambda j, i: (j, i, 0))] * 3,
        out_specs=[wspec] * 3, out_shape=[SDS((N_CHIPS, fc, d), F32)] * 3,
        compiler_params=_cp(("parallel", "arbitrary")))(h2, dhb, ff, da1, da3)


def _mixer_bwd(u, saved, pooled, h, dhres1, sp_, w_out, hosted=None):
    s, din = u.shape
    d = dhres1.shape[1]
    lw = din // 3
    tm = min(TM_MIX, s)
    nt = s // tm
    nb = lw // GATE_BLOCK
    hd = sp_["gate_a_w"].shape[1]

    def body(ul_ref, saved_ref, pooled_ref, h_ref, hhalo_ref, dhr_ref, cw_ref, cb_ref, ga_ref, gx_ref, ba_ref,
             bx_ref, lam_ref, pw_ref, pb_ref, ps_ref, gl_ref, gp_ref, wout_ref, du_ref, slab_ref,
             gw_s, a_s, b_s, e_s, ecarry_s, dxc_s, q_s, vec_s, cwacc_s, dgw_s, dpw_s):
        i = pl.program_id(0)
        tile = nt - 1 - i

        @pl.when(i == 0)
        def _():
            _build_gate_blocks(ga_ref, gx_ref, gw_s)
            for ref in (ecarry_s, dxc_s, q_s, vec_s, cwacc_s, dgw_s, dpw_s):
                ref[...] = jnp.zeros_like(ref)

        cw = cw_ref[...]
        lam = lam_ref[...]
        ps = ps_ref[...]
        f = {name: saved_ref[k] for k, name in enumerate(MIX_SAVED)}
        f.update({name: pooled_ref[k].astype(F32) for k, name in enumerate(MIX_SAVED_BF16)})
        f["mult"] = jnp.sqrt(jnp.maximum(f["m2raw"], 1e-12))
        f["sp"] = _softplus_neg(lam)
        f["xcb"] = f["xc"].astype(BF16)
        pooled = pooled_ref[len(MIX_SAVED_BF16)]
        row = lax.broadcasted_iota(jnp.int32, (tm, LANES), 0) + tile * tm
        f["invs"] = [1.0 / jnp.minimum(row + 1, w).astype(F32) for w in POOL_WINDOWS]
        f["z"] = jnp.concatenate(
            [_dot(pooled[:, g * LANES:(g + 1) * LANES], pw_ref[g].astype(BF16))
             for g in range(len(POOL_WINDOWS))], axis=1) + pb_ref[...]
        f["y_pool"] = f["z"] * ps
        u_l = ul_ref[...]
        hv = h_ref[...]
        h_prev = _shift_down(jnp.where(tile > 0, hhalo_ref[...], 0.0), hv, 1)
        y_lru = hv * f["ge"]
        rl = lax.rsqrt(_rowmean(y_lru * y_lru) + EPS)
        yp = f["y_pool"]
        rp = lax.rsqrt(_rowmean(yp * yp) + EPS)
        xh_l = y_lru * rl
        xh_p = yp * rp

        dyn = _dot_nt(dhr_ref[...].astype(BF16), wout_ref[...])
        d_nl, d_np = dyn[:, :lw], dyn[:, lw:]
        vec = {}
        vec[ROW_GL] = _colsum8(d_nl * xh_l)
        vec[ROW_GP] = _colsum8(d_np * xh_p)
        d_ylru = _rms_bwd(d_nl, xh_l, rl, gl_ref[...])
        d_ypool = _rms_bwd(d_np, xh_p, rp, gp_ref[...])

        vec[ROW_PS] = _colsum8(d_ypool * f["z"])
        dz = d_ypool * ps
        vec[ROW_PB] = _colsum8(dz)
        dzb = dz.astype(BF16)
        dup = []
        for gi, w in enumerate(POOL_WINDOWS):
            sl = slice(gi * LANES, (gi + 1) * LANES)
            dpw_s[:, sl] += _dot_tn(pooled[:, sl], dzb[:, sl])
            dpool = _dot_nt(dzb[:, sl], pw_ref[gi].astype(BF16))
            q = dpool * f["invs"][gi]
            e = jnp.concatenate([q, q_s[:, sl]], axis=0)
            k = 1
            while k < w:
                e = e + pltpu.roll(e, tm + HALO - k, 0)
                k *= 2
            dup.append(e[:tm] - dpool)
            q_s[:, sl] = q[:HALO]

        d_hout = d_ylru * f["ge"]
        d_ug = d_ylru * hv * f["dge"]
        a = f["a"]
        a1, b1 = _scan_level1(a, a * d_hout, reverse=True)
        a_s[...] = a1
        b_s[...] = b1
        e_next = ecarry_s[...]
        ecarry_s[...] = _scan_level2(a_s, b_s, e_s, e_next, reverse=True)
        sv = d_hout + _shift_up(e_s[...], e_next, 1)
        d_a = sv * h_prev
        mult, ig, xc, r = f["mult"], f["ig"], f["xc"], f["r"]
        d_mult = sv * (ig * xc)
        d_ig = sv * mult * xc
        d_xc = sv * mult * ig
        d_la = d_a * a + jnp.where(f["m2raw"] > 1e-12, d_mult * (-(a * a) / mult), 0.0)
        d_r = d_la * (-LRU_C * f["sp"])
        vec[ROW_LAM] = _colsum8(d_la * (-LRU_C * r))
        d_pr = d_r * r * (1.0 - r)
        d_pi = d_ig * ig * (1.0 - ig)
        vec[ROW_BA] = _colsum8(d_pr)
        vec[ROW_BX] = _colsum8(d_pi)
        dxc_parts = []
        for b in range(nb):
            sl = slice(b * GATE_BLOCK, (b + 1) * GATE_BLOCK)
            rhs = jnp.concatenate([d_pr[:, sl], d_pi[:, sl]], axis=1).astype(BF16)
            dgw_s[b] += _dot_tn(f["xcb"][:, sl], rhs)
            dxc_parts.append(_dot_nt(rhs, gw_s[b]))
        d_xc = d_xc + jnp.concatenate(dxc_parts, axis=1)
        vec[ROW_CONV_B] = _colsum8(d_xc)
        dxc_next = dxc_s[...]
        d_ul = None
        for k in range(CONV_WIDTH):
            ahead = _shift_up(d_xc, dxc_next, CONV_WIDTH - 1 - k)
            cwacc_s[k * SUBLANES:(k + 1) * SUBLANES, :] += _colsum8(ahead * u_l)
            term = ahead * cw[k:k + 1, :]
            d_ul = term if d_ul is None else d_ul + term
        dxc_s[...] = d_xc[:SUBLANES]
        for row, val in vec.items():
            vec_s[row * SUBLANES:(row + 1) * SUBLANES, :] += val
        du_ref[...] = jnp.concatenate([d_ul, d_ug] + dup, axis=1).astype(BF16)

        @pl.when(i == nt - 1)
        def _():
            rows = []
            for row in range(ROW_GA):
                if row in (ROW_CONV_W, ROW_CONV_W + 1, ROW_CONV_W + 2, ROW_CONV_W + 3):
                    k = row - ROW_CONV_W
                    v = jnp.sum(cwacc_s[k * SUBLANES:(k + 1) * SUBLANES, :], axis=0, keepdims=True)
                elif row <= ROW_GP:
                    v = jnp.sum(vec_s[row * SUBLANES:(row + 1) * SUBLANES, :], axis=0, keepdims=True)
                    if row == ROW_LAM:
                        v = v * (-1.0 / (1.0 + jnp.exp(lam)))
                else:
                    v = jnp.zeros((1, lw), F32)
                rows.append(v)
            slab_ref[0:ROW_GA, :] = jnp.concatenate(rows, axis=0)
            lane = lax.broadcasted_iota(jnp.int32, (hd, GATE_BLOCK), 1)
            for b in range(nb):
                for off, row0 in ((0, ROW_GA), (GATE_BLOCK, ROW_GX)):
                    acc = jnp.zeros((hd, GATE_BLOCK), F32)
                    for hh in range(GATE_BLOCK // hd):
                        m = (lane >= hh * hd) & (lane < (hh + 1) * hd)
                        acc = acc + jnp.where(m, dgw_s[b, hh * hd:(hh + 1) * hd, off:off + GATE_BLOCK], 0.0)
                    slab_ref[row0:row0 + hd, b * GATE_BLOCK:(b + 1) * GATE_BLOCK] = acc
            slab_ref[ROW_PW:ROW_PW + LANES, :] = dpw_s[...]

    small = [sp_[k] for k in ("conv_w", "conv_b", "gate_a_w", "gate_x_w", "gate_a_b", "gate_x_b", "lru_lambda",
                              "pool_w", "pool_b", "pool_scale", "norm_lru_g", "norm_pool_g")]
    rev = lambda i: nt - 1 - i

    def stages():
        i = pl.program_id(0)
        return i == 0, i == max(nt - 3, 0), i == nt - 1

    return _call(
        body, hosted, stages, grid=(nt,), name="mixer_bwd",
        in_specs=[pl.BlockSpec((tm, lw), lambda i: (rev(i), 0)),
                  pl.BlockSpec((len(MIX_SAVED), tm, lw), lambda i: (0, rev(i), 0)),
                  pl.BlockSpec((len(MIX_SAVED_BF16) + 1, tm, lw), lambda i: (0, rev(i), 0)),
                  pl.BlockSpec((tm, lw), lambda i: (rev(i), 0)),
                  pl.BlockSpec((SUBLANES, lw), lambda i: (jnp.maximum(rev(i) * (tm // SUBLANES) - 1, 0), 0)),
                  pl.BlockSpec((tm, d), lambda i: (rev(i), 0))]
        + [_const_spec(a.shape) for a in small] + [_const_spec(w_out.shape)],
        out_specs=[pl.BlockSpec((tm, din), lambda i: (rev(i), 0)),
                   pl.BlockSpec((MIX_SLAB_ROWS, SLAB_W), lambda i: (0, 0))],
        out_shape=[SDS((s, din), BF16), SDS((MIX_SLAB_ROWS, SLAB_W), F32)],
        scratch_shapes=[pltpu.VMEM((nb, GATE_BLOCK, 2 * GATE_BLOCK), BF16),
                        pltpu.VMEM((tm, lw), F32), pltpu.VMEM((tm, lw), F32), pltpu.VMEM((tm, lw), F32),
                        pltpu.VMEM((SUBLANES, lw), F32), pltpu.VMEM((SUBLANES, lw), F32),
                        pltpu.VMEM((HALO, lw), F32), pltpu.VMEM((ROW_GA * SUBLANES, lw), F32),
                        pltpu.VMEM((CONV_WIDTH * SUBLANES, lw), F32),
                        pltpu.VMEM((nb, GATE_BLOCK, 2 * GATE_BLOCK), F32), pltpu.VMEM((LANES, lw), F32)],
        args=(u, saved, pooled, h, h, dhres1, *small, w_out), sem=("arbitrary",))


def _inproj_bwd(x, du, dhres1, yn, g_mix, w_in, hosted=None):
    s, d = x.shape
    n = w_in.shape[1]
    nc = n // N_CHIPS
    tm = min(TM_PROJ, s)
    nt = s // tm

    def body(x_ref, du_ref, dhr_ref, yn_ref, g_ref, w_ref, gx_ref, dwin_ref, dwout_ref, dg_ref):
        i = pl.program_id(0)

        @pl.when(i == 0)
        def _():
            dwin_ref[...] = jnp.zeros_like(dwin_ref)
            dwout_ref[...] = jnp.zeros_like(dwout_ref)
            dg_ref[...] = jnp.zeros_like(dg_ref)

        xv = x_ref[...]
        g = g_ref[...]
        r = lax.rsqrt(_rowmean(xv * xv) + EPS)
        xh = xv * r
        h1 = (xh * g).astype(BF16)
        duv = du_ref[...]
        dh1 = _dot_nt(duv, w_ref[...])
        dg_ref[...] += _colsum8(dh1 * xh)
        dhr = dhr_ref[...]
        gx_ref[...] = dhr + _rms_bwd(dh1, xh, r, g)
        for jj in range(N_CHIPS):
            dwin_ref[jj] += _dot_tn(h1, duv[:, jj * nc:(jj + 1) * nc])
        dwout_ref[...] += _dot_tn(yn_ref[...], dhr.astype(BF16))

    def stages():
        i = pl.program_id(0)
        return i == 0, i == max(nt - 3, 0), i == nt - 1

    return _call(
        body, hosted, stages, grid=(nt,), name="inproj_bwd",
        in_specs=[pl.BlockSpec((tm, d), lambda i: (i, 0)), pl.BlockSpec((tm, n), lambda i: (i, 0)),
                  pl.BlockSpec((tm, d), lambda i: (i, 0)), pl.BlockSpec((tm, d), lambda i: (i, 0)),
                  _const_spec((1, d)), _const_spec((d, n))],
        out_specs=[pl.BlockSpec((tm, d), lambda i: (i, 0)), pl.BlockSpec((N_CHIPS, d, nc), lambda i: (0, 0, 0)),
                   pl.BlockSpec((d, d), lambda i: (0, 0)), pl.BlockSpec((SUBLANES, d), lambda i: (0, 0))],
        out_shape=[SDS((s, d), F32), SDS((N_CHIPS, d, nc), F32), SDS((d, d), F32), SDS((SUBLANES, d), F32)],
        scratch_shapes=[], args=(x, du, dhres1, yn, g_mix, w_in), sem=("arbitrary",))


def _place():
    x, y, c = lax.axis_index("x"), lax.axis_index("y"), lax.axis_index("c")
    return x, y, c


def _other_chips(x, y):
    return [(1 - x, y), (x, 1 - y), (1 - x, 1 - y)]


ANY = pl.BlockSpec(memory_space=pl.ANY)
VMEM_SPEC = pl.BlockSpec(memory_space=pltpu.VMEM)

_GATHERED = {"w_in": "cols", "w_out": "major", "ffn_w1": "major", "ffn_w3": "major", "ffn_w2": "major"}
_BIG = ("w_in", "w_out", "ffn_w1", "ffn_w3", "ffn_w2")


def _gather_weights(shards, conv_w, n_remote):
    n = len(shards)
    full_shapes = []
    for name, sh in zip(_BIG, shards):
        r, cdim = sh.shape
        if _GATHERED[name] == "cols":
            assert cdim % LANES == 0
            full_shapes.append((r, cdim * N_CHIPS))
        else:
            full_shapes.append((N_CHIPS, r, cdim))

    def region(ref, name, sh, jj, cc):
        r, cdim = sh
        rows = pl.ds(0, r) if cc is None else pl.ds(pl.multiple_of(cc * (r // 2), 16), r // 2)
        if _GATHERED[name] == "cols":
            return ref.at[rows, pl.ds(pl.multiple_of(jj * cdim, LANES), cdim)]
        return ref.at[jj, rows, :]

    def staged(ref, sh, cc):
        r = sh[0]
        return ref.at[pl.ds(pl.multiple_of(cc * (r // 2), 16), r // 2), :]

    def body(*refs):
        ins, cw_in = refs[:n], refs[n]
        outs, cw_out = refs[n + 1:2 * n + 1], refs[2 * n + 1]
        stage = refs[2 * n + 2:3 * n + 2]
        cw_stage, lsem, ssem, rsem, fssem, frsem, cssem, crsem = refs[3 * n + 2:]
        x, y, c = _place()
        j = 2 * x + y
        chips = _other_chips(x, y)
        for w in range(n_remote):
            stage[w][...] = ins[w][...].astype(BF16)
        cw_stage[...] = jnp.zeros_like(cw_stage)
        cw_stage[0:CONV_WIDTH, :] = cw_in[...]
        shs = [s_.shape for s_ in shards]
        local = [pltpu.make_async_copy(stage[w], region(outs[w], _BIG[w], shs[w], j, None), lsem.at[w])
                 for w in range(n)]
        local.append(pltpu.make_async_copy(cw_stage, cw_out.at[j], lsem.at[n]))
        sends = []
        for k, (px, py) in enumerate(chips):
            for w in range(n_remote):
                sends.append(pltpu.make_async_remote_copy(
                    src_ref=staged(stage[w], shs[w], c), dst_ref=region(outs[w], _BIG[w], shs[w], j, c),
                    send_sem=ssem.at[k * n + w], recv_sem=rsem.at[k * n + w], device_id=(px, py, c),
                    device_id_type=MESH))
            sends.append(pltpu.make_async_remote_copy(
                src_ref=cw_stage, dst_ref=cw_out.at[j], send_sem=cssem.at[k], recv_sem=crsem.at[k],
                device_id=(px, py, c), device_id_type=MESH))
        for cp in sends:
            cp.start()
        for w in range(n_remote, n):
            stage[w][...] = ins[w][...].astype(BF16)
        for cp in local:
            cp.start()
        fwd = []
        for k, (px, py) in enumerate(chips):
            jk = 2 * px + py
            for w in range(n_remote):
                reg = region(outs[w], _BIG[w], shs[w], jk, c)
                pltpu.make_async_remote_copy(src_ref=reg, dst_ref=reg, send_sem=ssem.at[k * n + w],
                                             recv_sem=rsem.at[k * n + w], device_id=(px, py, c),
                                             device_id_type=MESH).wait_recv()
                cp = pltpu.make_async_remote_copy(src_ref=reg, dst_ref=reg, send_sem=fssem.at[k * n + w],
                                                  recv_sem=frsem.at[k * n + w], device_id=(x, y, 1 - c),
                                                  device_id_type=MESH)
                cp.start()
                fwd.append(cp)
            pltpu.make_async_remote_copy(src_ref=cw_stage, dst_ref=cw_out.at[jk], send_sem=cssem.at[k],
                                         recv_sem=crsem.at[k], device_id=(px, py, c),
                                         device_id_type=MESH).wait_recv()
        for k, (px, py) in enumerate(chips):
            jk = 2 * px + py
            for w in range(n_remote):
                reg = region(outs[w], _BIG[w], shs[w], jk, 1 - c)
                pltpu.make_async_remote_copy(src_ref=reg, dst_ref=reg, send_sem=fssem.at[k * n + w],
                                             recv_sem=frsem.at[k * n + w], device_id=(x, y, 1 - c),
                                             device_id_type=MESH).wait_recv()
        for cp in sends + fwd:
            cp.wait_send()
        for cp in local:
            cp.wait()

    nsem = 3 * n
    return pl.pallas_call(
        body, name="gather_first",
        in_specs=[VMEM_SPEC] * (n + 1), out_specs=[ANY] * (n + 1),
        out_shape=[SDS(fs, BF16) for fs in full_shapes] + [SDS((N_CHIPS, SUBLANES, LANES), F32)],
        scratch_shapes=[pltpu.VMEM(s_.shape, BF16) for s_ in shards] + [pltpu.VMEM((SUBLANES, LANES), F32)]
        + [pltpu.SemaphoreType.DMA((n + 1,))] + [pltpu.SemaphoreType.DMA((nsem,))] * 4
        + [pltpu.SemaphoreType.DMA((3,))] * 2,
        compiler_params=_cp())(*shards, conv_w)


def _start_all(make):
    def f(ins, outs, sems):
        for cp in make(ins, outs, sems):
            cp.start()
    return f


def _wait_all(make):
    def f(ins, outs, sems):
        for cp in make(ins, outs, sems):
            cp.wait()
    return f


def _ffn_gather_hosted(arrs):
    n = len(arrs)

    def make(outs, sems):
        ssem, rsem, fs, fr = sems
        x, y, c = _place()
        j = 2 * x + y

        def reg(w, jj, cc):
            hr = arrs[w].shape[1] // 2
            return outs[w].at[jj, pl.ds(pl.multiple_of(cc * hr, 16), hr), :]

        def rc(w, jj, cc, s_sem, r_sem, dev):
            return pltpu.make_async_remote_copy(src_ref=reg(w, jj, cc), dst_ref=reg(w, jj, cc), send_sem=s_sem,
                                                recv_sem=r_sem, device_id=dev, device_id_type=MESH)

        sends, recvs, fwds, frecvs = [], [], [], []
        for k, (px, py) in enumerate(_other_chips(x, y)):
            jk = 2 * px + py
            for w in range(n):
                q = k * n + w
                sends.append(rc(w, j, c, ssem.at[q], rsem.at[q], (px, py, c)))
                recvs.append(rc(w, jk, c, ssem.at[q], rsem.at[q], (px, py, c)))
                fwds.append(rc(w, jk, c, fs.at[q], fr.at[q], (x, y, 1 - c)))
                frecvs.append(rc(w, jk, 1 - c, fs.at[q], fr.at[q], (x, y, 1 - c)))
        return sends, recvs, fwds, frecvs

    def start(ins, outs, sems):
        for cp in make(outs, sems)[0]:
            cp.start()

    def mid(ins, outs, sems):
        _, recvs, fwds, _ = make(outs, sems)
        for r, f in zip(recvs, fwds):
            r.wait_recv()
            f.start()

    def finish(ins, outs, sems):
        sends, _, fwds, frecvs = make(outs, sems)
        for r in frecvs:
            r.wait_recv()
        for cp in sends + fwds:
            cp.wait_send()

    return _Hosted(arrs, [SDS(a.shape, a.dtype) for a in arrs], [3 * n] * 4, start, finish, mid=mid,
                   aliases={w: w for w in range(n)})


def _rs_sibling_hosted(arrs):
    n = len(arrs)

    def make(ins, outs, sems):
        x, y, c = _place()
        cps = []
        for w in range(n):
            hr = arrs[w].shape[1] // 2
            src = ins[w].at[:, pl.ds(pl.multiple_of((1 - c) * hr, SUBLANES), hr), :]
            cps.append(pltpu.make_async_remote_copy(src_ref=src, dst_ref=outs[w], send_sem=sems[0].at[w],
                                                    recv_sem=sems[1].at[w], device_id=(x, y, 1 - c),
                                                    device_id_type=MESH))
        return cps

    return _Hosted(arrs, [SDS((a.shape[0], a.shape[1] // 2, a.shape[2]), F32) for a in arrs], [n, n],
                   _start_all(make), _wait_all(make))


def _rs_chips_hosted(parts):
    n = len(parts)

    def make(ins, outs, sems):
        x, y, c = _place()
        j = 2 * x + y
        cps = []
        for k, (px, py) in enumerate(_other_chips(x, y)):
            jk = 2 * px + py
            for w in range(n):
                cps.append(pltpu.make_async_remote_copy(
                    src_ref=ins[w].at[jk], dst_ref=outs[w].at[j], send_sem=sems[0].at[k * n + w],
                    recv_sem=sems[1].at[k * n + w], device_id=(px, py, c), device_id_type=MESH))
        return cps

    return _Hosted(parts, [SDS(p.shape, p.dtype) for p in parts], [3 * n, 3 * n], _start_all(make), _wait_all(make))


def _rs_swap_hosted(halves):
    n = len(halves)

    def make(ins, outs, sems):
        x, y, c = _place()
        return [pltpu.make_async_remote_copy(src_ref=ins[w], dst_ref=outs[w], send_sem=sems[0].at[w],
                                             recv_sem=sems[1].at[w], device_id=(x, y, 1 - c), device_id_type=MESH)
                for w in range(n)]

    return _Hosted(halves, [SDS(h.shape, F32) for h in halves], [n, n], _start_all(make), _wait_all(make))


HBM_SPEC = pl.BlockSpec(memory_space=pltpu.HBM)
SEM_SPEC = pl.BlockSpec(memory_space=pltpu.SEMAPHORE)
_EFFECT = pltpu.SideEffectType.DATAFLOW_SIDE_EFFECTING


def _split_start(h, name):
    n_in, n_out, ns = len(h.ins), len(h.out_shapes), len(h.sems)
    ins = [pltpu.with_memory_space_constraint(a, pltpu.HBM) for a in h.ins]
    lands = [pltpu.with_memory_space_constraint(lax.empty(o.shape, o.dtype), pltpu.HBM) for o in h.out_shapes]

    def body(*refs):
        i_refs, l_refs = refs[:n_in], refs[n_in:n_in + n_out]
        s_refs = refs[n_in + n_out:n_in + n_out + ns]
        token = refs[-1]
        h.start(i_refs, l_refs, s_refs)
        token[...] = jnp.zeros_like(token)

    res = pl.pallas_call(
        body, name=name, in_specs=[HBM_SPEC] * (n_in + n_out),
        out_specs=[SEM_SPEC] * ns + [HBM_SPEC] * n_out + [VMEM_SPEC],
        out_shape=[pltpu.SemaphoreType.DMA((k,)) for k in h.sems]
        + [pltpu.HBM(o.shape, o.dtype) for o in h.out_shapes] + [SDS((SUBLANES, LANES), F32)],
        input_output_aliases={n_in + k: ns + k for k in range(n_out)},
        compiler_params=pltpu.CompilerParams(has_side_effects=_EFFECT))(*ins, *lands)
    return list(res[:ns]) + ins + list(res[ns:-1]), res[-1]


def _split_wait(h, state, after, name):
    n_in, n_out, ns = len(h.ins), len(h.out_shapes), len(h.sems)
    sems, bufs = state[:ns], state[ns:]

    def body(*refs):
        i_refs, l_refs = refs[:n_in], refs[n_in:n_in + n_out]
        s_refs = refs[n_in + n_out:n_in + n_out + ns]
        h.finish(i_refs, l_refs, s_refs)

    res = pl.pallas_call(
        body, name=name, in_specs=[HBM_SPEC] * (n_in + n_out) + [SEM_SPEC] * ns + [ANY],
        out_specs=[HBM_SPEC] * n_out,
        out_shape=[pltpu.HBM(b.shape, b.dtype) for b in bufs[n_in:]],
        input_output_aliases={n_in + k: k for k in range(n_out)},
        compiler_params=pltpu.CompilerParams(has_side_effects=_EFFECT))(*bufs, *sems, after)
    return list(res)


def _run_comm(hosted, name):
    return _call(lambda: None, hosted, None, name=name, grid=(), in_specs=[], out_specs=[], out_shape=[],
                 scratch_shapes=[], args=(), sem=None)[1]


def _row_tile(rows, cols, n_arrays):
    budget = 24 * 1024 * 1024 // (2 * 4 * n_arrays * cols)
    best = SUBLANES
    for t in range(SUBLANES, rows + 1, SUBLANES):
        if rows % t == 0 and t <= budget:
            best = t
    return best


def _place_index(which):
    x, y, c = _place()
    v = c if which == "c" else 2 * x + y
    return jnp.reshape(v, (1,)).astype(jnp.int32)


def _add_own_half(full, recv, name, wire=BF16):
    nsh, rows, cols = full.shape
    hr = rows // 2
    t = _row_tile(hr, cols, 4)
    nt = hr // t

    def body(c_ref, a_ref, b_ref, o_ref, ob_ref):
        v = a_ref[...] + b_ref[...]
        o_ref[...] = v
        ob_ref[...] = v.astype(wire)

    half = pl.BlockSpec((1, t, cols), lambda s_, i, c_ref: (s_, i, 0))
    return pl.pallas_call(
        body, name=name,
        grid_spec=pltpu.PrefetchScalarGridSpec(
            num_scalar_prefetch=1, grid=(nsh, nt),
            in_specs=[pl.BlockSpec((1, t, cols), lambda s_, i, c_ref: (s_, c_ref[0] * nt + i, 0)), half],
            out_specs=[half, half]),
        out_shape=[SDS((nsh, hr, cols), F32), SDS((nsh, hr, cols), wire)],
        compiler_params=_cp(("parallel", "parallel")))(_place_index("c"), full, recv)


def _sum_chips(own, recv, name):
    nsh, hr, cols = own.shape
    t = _row_tile(hr, cols, 6)

    def body(j_ref, own_ref, *rest):
        r_refs, o_ref = rest[:nsh], rest[nsh]
        j = j_ref[0]
        mine = own_ref[0]
        parts = [jnp.where(j == k, mine, r_refs[k][0].astype(F32)) for k in range(nsh)]
        o_ref[...] = ((parts[0] + parts[1]) + parts[2]) + parts[3]

    def other(k):
        return pl.BlockSpec((1, t, cols), lambda i, j_ref: (jnp.where(j_ref[0] == k, (k + 1) % nsh, k), i, 0))

    return pl.pallas_call(
        body, name=name,
        grid_spec=pltpu.PrefetchScalarGridSpec(
            num_scalar_prefetch=1, grid=(hr // t,),
            in_specs=[pl.BlockSpec((1, t, cols), lambda i, j_ref: (j_ref[0], i, 0))]
            + [other(k) for k in range(nsh)],
            out_specs=pl.BlockSpec((t, cols), lambda i, j_ref: (i, 0))),
        out_shape=SDS((hr, cols), F32), compiler_params=_cp(("parallel",)))(_place_index("j"), own, *([recv] * nsh))


def _adamw_math(w, g, m, v):
    m = ADAM_B1 * m + (1.0 - ADAM_B1) * g
    v = ADAM_B2 * v + (1.0 - ADAM_B2) * (g * g)
    m_hat = m / (1.0 - ADAM_B1 ** ADAM_STEP)
    v_hat = v / (1.0 - ADAM_B2 ** ADAM_STEP)
    delta = -ADAM_LR * (m_hat / (jnp.sqrt(v_hat) + ADAM_EPS) + ADAM_WD * w)
    return delta, m, v


def _adamw_big(w, g_own, g_sib, m, v, name, token=None):
    _, rows, cols = w.shape
    hr = rows // 2
    t = _row_tile(hr, cols, 9)
    nth = hr // t
    if token is None:
        token = jnp.zeros((SUBLANES, LANES), F32)

    def body(c_ref, w_ref, go_ref, gs_ref, m_ref, v_ref, tok_ref, g_ref, d_ref, mo_ref, vo_ref):
        own = (pl.program_id(0) // nth) == c_ref[0]
        g = jnp.where(own, go_ref[...], gs_ref[...]) + tok_ref[0:1, 0:1]
        g_ref[0] = g
        d_ref[0], mo_ref[0], vo_ref[0] = _adamw_math(w_ref[0], g, m_ref[0], v_ref[0])

    spec = pl.BlockSpec((1, t, cols), lambda i, c_ref: (0, i, 0))
    hspec = pl.BlockSpec((t, cols), lambda i, c_ref: (i % nth, 0))
    tspec = pl.BlockSpec((SUBLANES, LANES), lambda i, c_ref: (0, 0))
    return pl.pallas_call(
        body, name=name,
        grid_spec=pltpu.PrefetchScalarGridSpec(
            num_scalar_prefetch=1, grid=(2 * nth,), in_specs=[spec, hspec, hspec, spec, spec, tspec],
            out_specs=[spec] * 4),
        out_shape=[SDS((1, rows, cols), F32)] * 4,
        compiler_params=_cp(("parallel",)))(_place_index("c"), w, g_own, g_sib, m, v, token)


def _build_slab(mix_slab, dg_mix, dg_ffn, dg_fin, loss8):
    def body(ms_ref, gm_ref, gf_ref, gn_ref, loss_ref, out_ref):
        rows = []
        for ref in (gm_ref, gf_ref, gn_ref):
            v = jnp.sum(ref[...], axis=0, keepdims=True)
            rows += [v[:, :SLAB_W], v[:, SLAB_W:]]
        rows.append(jnp.concatenate([loss_ref[0:1, :]] * (SLAB_W // LANES), axis=1))
        rows.append(jnp.zeros((SLAB_ROWS - ROW_LOSS - 1, SLAB_W), F32))
        tail = jnp.concatenate(rows, axis=0)
        for k in range(N_CHIPS):
            out_ref[k, 0:MIX_SLAB_ROWS, :] = ms_ref[...]
            out_ref[k, MIX_SLAB_ROWS:SLAB_ROWS, :] = tail

    return pl.pallas_call(
        body, name="build_slab", in_specs=[VMEM_SPEC] * 5, out_specs=VMEM_SPEC,
        out_shape=SDS((N_CHIPS, SLAB_ROWS, SLAB_W), F32),
        compiler_params=_cp())(mix_slab, dg_mix, dg_ffn, dg_fin, loss8)


_SMALL_ROWS = (("conv_b", ROW_CONV_B), ("gate_a_b", ROW_BA), ("gate_x_b", ROW_BX), ("lru_lambda", ROW_LAM),
               ("pool_b", ROW_PB), ("pool_scale", ROW_PS), ("norm_lru_g", ROW_GL), ("norm_pool_g", ROW_GP))
_WIDE_ROWS = (("norm_mix_g", ROW_MIX), ("norm_ffn_g", ROW_FFN), ("final_norm_g", ROW_FIN))
_BLOCK_ROWS = (("gate_a_w", ROW_GA), ("gate_x_w", ROW_GX), ("pool_w", ROW_PW))
_SMALL_ORDER = tuple(n for n, _ in _SMALL_ROWS) + tuple(n for n, _ in _WIDE_ROWS) + tuple(
    n for n, _ in _BLOCK_ROWS) + ("conv_w",)


def _adamw_small(slab_own, slab_sib, wmv):
    names = _SMALL_ORDER
    flat = [a for nme in names for a in wmv[nme]]
    nin = len(flat)

    def body(*refs):
        own_ref, sib_ref, j_ref = refs[0], refs[1], refs[2]
        ins = refs[3:3 + nin]
        outs = refs[3 + nin:-1]
        first = j_ref[1] == 0
        slab_ref = jnp.concatenate([jnp.where(first, own_ref[...], sib_ref[...]),
                                    jnp.where(first, sib_ref[...], own_ref[...])], axis=0)
        refs[-1][...] = jnp.broadcast_to(slab_ref[ROW_LOSS:ROW_LOSS + 1, 0:LANES], (SUBLANES, LANES))
        grads = {}
        for nme, row in _SMALL_ROWS:
            grads[nme] = slab_ref[row:row + 1, :]
        for nme, row in _WIDE_ROWS:
            grads[nme] = jnp.concatenate([slab_ref[row:row + 1, :], slab_ref[row + 1:row + 2, :]], axis=1)
        full = slab_ref[ROW_CONV_W:ROW_CONV_W + CONV_WIDTH, :]
        jv = j_ref[0]
        g = jnp.zeros((CONV_WIDTH, LANES), F32)
        for jj in range(N_CHIPS):
            g = jnp.where(jv == jj, full[:, jj * LANES:(jj + 1) * LANES], g)
        grads["conv_w"] = g
        block_rows = dict(_BLOCK_ROWS)
        for idx, nme in enumerate(names):
            w_ref, m_ref, v_ref = ins[3 * idx:3 * idx + 3]
            if nme in block_rows:
                nblk, r, c = w_ref.shape
                parts = [(b, slab_ref[block_rows[nme]:block_rows[nme] + r, b * c:(b + 1) * c]) for b in range(nblk)]
            else:
                parts = [(Ellipsis, grads[nme])]
            for b, g in parts:
                delta, m, v = _adamw_math(w_ref[b], g, m_ref[b], v_ref[b])
                outs[4 * idx][b] = g
                outs[4 * idx + 1][b] = delta
                outs[4 * idx + 2][b] = m
                outs[4 * idx + 3][b] = v

    place = jnp.concatenate([_place_index("j"), _place_index("c")])
    out_shape = [SDS(wmv[nme][0].shape, F32) for nme in names for _ in range(4)] + [SDS((SUBLANES, LANES), F32)]
    res = pl.pallas_call(
        body, name="adamw_small",
        in_specs=[VMEM_SPEC, VMEM_SPEC, pl.BlockSpec(memory_space=pltpu.SMEM)] + [VMEM_SPEC] * nin,
        out_specs=[VMEM_SPEC] * len(out_shape), out_shape=out_shape,
        compiler_params=_cp())(slab_own, slab_sib, place, *flat)
    return {nme: tuple(res[4 * idx:4 * idx + 4]) for idx, nme in enumerate(names)}, res[-1]


_FFN = ("ffn_w1", "ffn_w3", "ffn_w2")
_TRANSPOSED = ("ffn_w1", "ffn_w3")


def _local_step(x, target, full, sp_, distributed):
    d = x.shape[1]
    (u,), got = _inproj(x, sp_["norm_mix_g"], full["w_in"],
                        [_ffn_gather_hosted([full["w_out"]])] if distributed else None)
    w_out = (got[0][0] if distributed else full["w_out"]).reshape(d, d)
    gather = [_ffn_gather_hosted([full[n] for n in _FFN])] if distributed else None
    (h, yn, hres1, saved, pooled), got = _mixer_fwd(u, x, sp_, w_out, gather)
    w1, w3, w2 = got[0] if distributed else [full[n] for n in _FFN]
    h2, a1, a3, ff = _ffn_up(hres1, sp_["norm_ffn_g"], w1, w3)
    dh, dhb, loss8, dg_fin = _ffn_down(ff, hres1, target, sp_["final_norm_g"], w2)
    da1, da3 = _ffn_bwd_gate(dhb, a1, a3, w2)
    dws = list(_ffn_wgrad(h2, dhb, ff, da1, da3))
    rs1 = [_rs_sibling_hosted(dws)] if distributed else None
    (dhres1, dg_ffn), got = _ffn_bwd_down(da1, da3, dh, hres1, sp_["norm_ffn_g"], w1, w3, rs1)
    rs2 = None
    if distributed:
        pairs = [_add_own_half(a, r, "add_half_" + n) for n, a, r in zip(_FFN, dws, got[0])]
        rs2 = [_rs_chips_hosted([pb for _, pb in pairs])]
    (du, mix_slab), got = _mixer_bwd(u, saved, pooled, h, dhres1, sp_, w_out, rs2)
    g_mix = sp_["norm_mix_g"]
    if distributed:
        fin = [_sum_chips(pairs[k][0], got[0][k], "sum_chips_" + n) for k, n in enumerate(_FFN)]
        swap = _rs_swap_hosted(fin)
        state, token = _split_start(swap, "ffn_swap_start")
        g_mix = g_mix + token[0:1, 0:1]
    (gx, dwin, dwout, dg_mix), _ = _inproj_bwd(x, du, dhres1, yn, g_mix, full["w_in"])
    if distributed:
        sib = _split_wait(swap, state, dg_mix, "ffn_swap_wait")
    big = {"w_in": dwin, "w_out": dwout.reshape(N_CHIPS, d // N_CHIPS, d)}
    for k, n in enumerate(_FFN):
        big[n] = (fin[k], sib[k]) if distributed else dws[k]
    return gx, big, (mix_slab, dg_mix, dg_ffn, dg_fin, loss8)


_SMALL_LAYOUT = {
    "gate_a_w": (lambda a: a[0], lambda a: a[None]),
    "gate_x_w": (lambda a: a[0], lambda a: a[None]),
    "pool_w": (lambda a: a[0], lambda a: a[None]),
    "conv_w": (lambda a: a[0], lambda a: a[None]),
    "final_norm_g": (lambda a: a[None], lambda a: a[0]),
}

_WEIGHTS = ("norm_mix_g", "w_in", "conv_w", "conv_b", "gate_a_w", "gate_a_b", "gate_x_w", "gate_x_b", "lru_lambda",
            "pool_w", "pool_b", "pool_scale", "norm_lru_g", "norm_pool_g", "w_out", "norm_ffn_g", "ffn_w1",
            "ffn_w3", "ffn_w2", "final_norm_g")


def kernel(x, norm_mix_g, w_in, conv_w, conv_b, gate_a_w, gate_a_b, gate_x_w, gate_x_b, lru_lambda, pool_w, pool_b, pool_scale, norm_lru_g, norm_pool_g, w_out, norm_ffn_g, ffn_w1, ffn_w3, ffn_w2, final_norm_g, loss_target, m_norm_mix_g, m_w_in, m_conv_w, m_conv_b, m_gate_a_w, m_gate_a_b, m_gate_x_w, m_gate_x_b, m_lru_lambda, m_pool_w, m_pool_b, m_pool_scale, m_norm_lru_g, m_norm_pool_g, m_w_out, m_norm_ffn_g, m_ffn_w1, m_ffn_w3, m_ffn_w2, m_final_norm_g, v_norm_mix_g, v_w_in, v_conv_w, v_conv_b, v_gate_a_w, v_gate_a_b, v_gate_x_w, v_gate_x_b, v_lru_lambda, v_pool_w, v_pool_b, v_pool_scale, v_norm_lru_g, v_norm_pool_g, v_w_out, v_norm_ffn_g, v_ffn_w1, v_ffn_w3, v_ffn_w2, v_final_norm_g):
    loc = locals()
    w = {n: loc[n] for n in _WEIGHTS}
    m = {n: loc["m_" + n] for n in _WEIGHTS}
    v = {n: loc["v_" + n] for n in _WEIGHTS}

    def lay(nme, a):
        return _SMALL_LAYOUT[nme][0](a) if nme in _SMALL_LAYOUT else a

    def unlay(nme, a):
        return _SMALL_LAYOUT[nme][1](a) if nme in _SMALL_LAYOUT else a

    for group in (w, m, v):
        for n in _TRANSPOSED:
            group[n] = jnp.transpose(group[n], (0, 2, 1))

    gathered = _gather_weights([w[n][0] for n in _BIG], w["conv_w"][0], n_remote=1)
    full = dict(zip(_BIG, gathered[:-1]))
    cw_all = gathered[-1]
    sp_ = {n: lay(n, w[n]) for n in _SMALL_ORDER}
    sp_["conv_w"] = jnp.transpose(cw_all[:, :CONV_WIDTH, :], (1, 0, 2)).reshape(CONV_WIDTH, N_CHIPS * LANES)

    gx, big, small = _local_step(x[0], loss_target[0], full, sp_, distributed=True)

    late = ("w_in", "w_out", "slab")
    big["slab"] = _build_slab(*small)
    fin = {n: big[n][0] for n in _FFN}
    sib = {n: big[n][1] for n in _FFN}
    recv1, = _run_comm([_rs_sibling_hosted([big[n] for n in late])], "tail_sibling")
    pairs = [_add_own_half(big[n], r, "add_half_" + n, F32 if n == "slab" else BF16) for n, r in zip(late, recv1)]
    chips = _rs_chips_hosted([pb for _, pb in pairs])
    state, token = _split_start(chips, "tail_chips_start")
    out = {}
    for n in _FFN:
        out[n] = tuple(_adamw_big(w[n], fin[n], sib[n], m[n], v[n], "adamw_" + n, token))
    recv2 = _split_wait(chips, state, out[_FFN[-1]][1], "tail_chips_wait")
    for n, (p, _), r in zip(late, pairs, recv2):
        fin[n] = _sum_chips(p, r, "sum_chips_" + n)
    swapped, = _run_comm([_rs_swap_hosted([fin[n] for n in late])], "tail_swap")
    sib.update(zip(late, swapped))
    for n in late[:2]:
        out[n] = tuple(_adamw_big(w[n], fin[n], sib[n], m[n], v[n], "adamw_" + n))
    for n in _TRANSPOSED:
        out[n] = tuple(jnp.transpose(a, (0, 2, 1)) for a in out[n])
    wmv = {n: (lay(n, w[n]), lay(n, m[n]), lay(n, v[n])) for n in _SMALL_ORDER}
    res, loss = _adamw_small(fin["slab"], sib["slab"], wmv)
    for n in _SMALL_ORDER:
        out[n] = tuple(unlay(n, a) for a in res[n])
    return (loss[0, 0], gx[None]) + tuple(out[n][k] for k in range(4) for n in _WEIGHTS)
```

```python
import functools
import math

import jax
import jax.numpy as jnp
from jax import lax
from jax.experimental import pallas as pl
from jax.experimental.pallas import tpu as pltpu

F32 = jnp.float32
BF16 = jnp.bfloat16
SDS = jax.ShapeDtypeStruct
MESH = pl.DeviceIdType.MESH

EPS = 1e-6
LRU_C = 8.0
CONV_WIDTH = 4
POOL_WINDOWS = (2, 4, 8, 16)
HALO = 16
LANES = 128
SUBLANES = 8
GATE_BLOCK = 256
N_CHIPS = 4

ADAM_LR = 0.001
ADAM_B1 = 0.9
ADAM_B2 = 0.999
ADAM_EPS = 1e-08
ADAM_WD = 0.01
ADAM_STEP = 10

TM_PROJ = 512
TM_MIX = 512
TM_FFN = 512
TM_WGRAD = 2048
MIX_SAVED = ("xc", "r", "ig", "a", "m2raw", "ge", "dge")
FFN_ROW_CHUNKS = 2
VMEM_LIMIT = 56 * 1024 * 1024

SLAB_W = 512
ROW_CONV_B, ROW_CONV_W, ROW_BA, ROW_BX, ROW_LAM, ROW_PB, ROW_PS, ROW_GL, ROW_GP = 0, 1, 5, 6, 7, 8, 9, 10, 11
ROW_GA, ROW_GX, ROW_PW = 16, 80, 144
ROW_MIX, ROW_FFN, ROW_FIN, ROW_LOSS = 272, 274, 276, 278
MIX_SLAB_ROWS = 272
SLAB_ROWS = 288


def _cp(sem=None, **kw):
    if sem is not None:
        kw["dimension_semantics"] = sem
    return pltpu.CompilerParams(vmem_limit_bytes=VMEM_LIMIT, **kw)


def _const_spec(shape):
    nd = len(shape)
    return pl.BlockSpec(shape, lambda *_: (0,) * nd, pipeline_mode=pl.Buffered(1))


def _sigmoid(x):
    return 1.0 / (1.0 + jnp.exp(-x))


def _dot(a, b):
    return jnp.dot(a, b, preferred_element_type=F32)


def _dot_nt(a, b):
    return lax.dot_general(a, b, (((1,), (1,)), ((), ())), preferred_element_type=F32)


def _dot_tn(a, b):
    return lax.dot_general(a, b, (((0,), (0,)), ((), ())), preferred_element_type=F32)


def _colsum8(v):
    m, c = v.shape
    return v.reshape(m // SUBLANES, SUBLANES, c).sum(axis=0)


def _rowmean(v):
    return jnp.mean(v, axis=-1, keepdims=True)


def _rms_bwd(dy, xhat, r, g):
    dxh = dy * g
    return r * (dxh - xhat * _rowmean(dxh * xhat))


def _softplus_neg(lam):
    z = -lam
    e = jnp.exp(-jnp.abs(z))
    u = 1.0 + e
    d = u - 1.0
    log1p = jnp.where(d == 0.0, e, jnp.log(u) * (e / jnp.where(d == 0.0, 1.0, d)))
    return jnp.maximum(z, 0.0) + log1p


def _neg_expm1(z):
    series = -(z * (1.0 + z * (0.5 + z * (1.0 / 6.0 + z * (1.0 / 24.0)))))
    return jnp.where(z > -0.03, series, 1.0 - jnp.exp(z))


_GELU_C = math.sqrt(2.0 / math.pi)
_GELU_K = 0.044715


def _gelu_parts(x):
    x2 = x * x
    th = jnp.tanh(_GELU_C * (x + _GELU_K * x2 * x))
    ge = 0.5 * x * (1.0 + th)
    dge = 0.5 * (1.0 + th) + 0.5 * x * (1.0 - th * th) * (_GELU_C * (1.0 + 3.0 * _GELU_K * x2))
    return ge, dge


def _shift_down(halo, tile, k):
    if k == 0:
        return tile
    ext = jnp.concatenate([halo, tile], axis=0)
    n = tile.shape[0]
    h = halo.shape[0]
    return ext[h - k:h - k + n]


def _shift_up(tile, nxt, k):
    if k == 0:
        return tile
    ext = jnp.concatenate([tile, nxt], axis=0)
    return ext[k:k + tile.shape[0]]


def _build_gate_blocks(ga_ref, gx_ref, gw_ref):
    hd = ga_ref.shape[1]
    per = GATE_BLOCK // hd
    zero = jnp.zeros((hd, hd), F32)
    for b in range(gw_ref.shape[0]):
        for src, off in ((ga_ref, 0), (gx_ref, GATE_BLOCK)):
            for hh in range(per):
                row = jnp.concatenate([zero] * hh + [src[b * per + hh]] + [zero] * (per - 1 - hh), axis=1)
                gw_ref[b, hh * hd:(hh + 1) * hd, off:off + GATE_BLOCK] = row.astype(BF16)


def _scan_level1(a, b, reverse):
    m, c = a.shape
    a3 = a.reshape(m // SUBLANES, SUBLANES, c)
    b3 = b.reshape(m // SUBLANES, SUBLANES, c)
    row = lax.broadcasted_iota(jnp.int32, a3.shape, 1)
    for s in (1, 2, 4):
        sh = (SUBLANES - s) if reverse else s
        a_sh = pltpu.roll(a3, sh, 1)
        b_sh = pltpu.roll(b3, sh, 1)
        ok = (row < SUBLANES - s) if reverse else (row >= s)
        b3 = jnp.where(ok, a3 * b_sh + b3, b3)
        a3 = jnp.where(ok, a3 * a_sh, a3)
    return a3.reshape(m, c), b3.reshape(m, c)


def _scan_level2(a_ref, b_ref, out_ref, carry, reverse):
    m, c = a_ref.shape
    ng = m // SUBLANES

    def step(g, cr):
        gi = (ng - 1 - g) if reverse else g
        off = pl.multiple_of(gi * SUBLANES, SUBLANES)
        h = b_ref[pl.ds(off, SUBLANES), :] + a_ref[pl.ds(off, SUBLANES), :] * cr
        out_ref[pl.ds(off, SUBLANES), :] = h
        edge = h[0:1, :] if reverse else h[SUBLANES - 1:SUBLANES, :]
        return jnp.broadcast_to(edge, (SUBLANES, c))

    return lax.fori_loop(0, ng, step, carry, unroll=4)


def _mixer_recompute(u, hal, t0, cw, cb, gw_ref, ba, bx, lam, pw_ref, pb, ps):
    tm = u.shape[0]
    lw = cb.shape[1]
    u_l, u_g, u_p = u[:, :lw], u[:, lw:2 * lw], u[:, 2 * lw:]
    hal_l, hal_p = hal[:, :lw], hal[:, 2 * lw:]
    taps = [_shift_down(hal_l, u_l, CONV_WIDTH - 1 - k) for k in range(CONV_WIDTH)]
    xc = cb
    for k in range(CONV_WIDTH):
        xc = xc + taps[k] * cw[k:k + 1, :]
    xcb = xc.astype(BF16)
    nb = lw // GATE_BLOCK
    gs = [_dot(xcb[:, b * GATE_BLOCK:(b + 1) * GATE_BLOCK], gw_ref[b]) for b in range(nb)]
    r = _sigmoid(jnp.concatenate([g[:, :GATE_BLOCK] for g in gs], axis=1) + ba)
    ig = _sigmoid(jnp.concatenate([g[:, GATE_BLOCK:] for g in gs], axis=1) + bx)
    sp = _softplus_neg(lam)
    la = (-LRU_C * r) * sp
    a = jnp.exp(la)
    m2raw = _neg_expm1(2.0 * la)
    mult = jnp.sqrt(jnp.maximum(m2raw, 1e-12))
    ge, dge = _gelu_parts(u_g)
    row = lax.broadcasted_iota(jnp.int32, (tm, LANES), 0) + t0
    pooled, invs, zs = [], [], []
    for gi, w in enumerate(POOL_WINDOWS):
        e = jnp.concatenate([hal_p[:, gi * LANES:(gi + 1) * LANES], u_p[:, gi * LANES:(gi + 1) * LANES]], axis=0)
        s = e
        k = 1
        while k < w:
            s = s + pltpu.roll(s, k, 0)
            k *= 2
        inv = 1.0 / jnp.minimum(row + 1, w).astype(F32)
        pg = s[HALO:] * inv - e[HALO:]
        pooled.append(pg)
        invs.append(inv)
        zs.append(_dot(pg.astype(BF16), pw_ref[gi].astype(BF16)))
    z = jnp.concatenate(zs, axis=1) + pb
    y_pool = z * ps
    return dict(u_l=u_l, u_g=u_g, taps=taps, xc=xc, xcb=xcb, r=r, ig=ig, sp=sp, la=la, a=a, m2raw=m2raw,
                mult=mult, ge=ge, dge=dge, pooled=pooled, invs=invs, z=z, y_pool=y_pool)


ANY = pl.BlockSpec(memory_space=pl.ANY)
VMEM_SPEC = pl.BlockSpec(memory_space=pltpu.VMEM)


class _Hosted:
    def __init__(self, ins, out_shapes, sems, start, finish, mid=None, aliases=None):
        self.ins, self.out_shapes, self.sems = list(ins), list(out_shapes), list(sems)
        self.start, self.mid, self.finish = start, mid, finish
        self.aliases = dict(aliases or {})


def _call(body, hosted, stage_preds, *, name, grid, in_specs, out_specs, out_shape, scratch_shapes, args, sem):
    hosted = list(hosted or [])
    n_in, n_out, n_scr = len(in_specs), len(out_specs), len(scratch_shapes)
    c_in = [a for h in hosted for a in h.ins]
    c_out = [o for h in hosted for o in h.out_shapes]
    c_sem = [pltpu.SemaphoreType.DMA((k,)) for h in hosted for k in h.sems]

    def full(*refs):
        p = 0
        parts = []
        for cnt in (n_in, len(c_in), n_out, len(c_out), n_scr, len(c_sem)):
            parts.append(refs[p:p + cnt])
            p += cnt
        hi, ci, ho, co, hs, cs = parts
        per = []
        a = b = c_ = 0
        for h in hosted:
            per.append((h, ci[a:a + len(h.ins)], co[b:b + len(h.out_shapes)], cs[c_:c_ + len(h.sems)]))
            a, b, c_ = a + len(h.ins), b + len(h.out_shapes), c_ + len(h.sems)
        first = mid = last = None
        if hosted and grid:
            first, mid, last = stage_preds()

        def run(fn, pred, i_, o_, s_):
            if fn is None:
                return
            if pred is None:
                fn(i_, o_, s_)
            else:
                pl.when(pred)(functools.partial(fn, i_, o_, s_))

        for h, i_, o_, s_ in per:
            run(h.start, first, i_, o_, s_)
        body(*hi, *ho, *hs)
        for h, i_, o_, s_ in per:
            run(h.mid, mid, i_, o_, s_)
        for h, i_, o_, s_ in per:
            run(h.finish, last, i_, o_, s_)

    aliases = {}
    a = b = 0
    for h in hosted:
        for k, v in h.aliases.items():
            aliases[n_in + a + k] = n_out + b + v
        a, b = a + len(h.ins), b + len(h.out_shapes)
    res = pl.pallas_call(
        full, name=name, grid=grid, in_specs=list(in_specs) + [ANY] * len(c_in),
        out_specs=list(out_specs) + [ANY] * len(c_out), out_shape=list(out_shape) + c_out,
        scratch_shapes=list(scratch_shapes) + c_sem, input_output_aliases=aliases,
        compiler_params=_cp(sem))(*args, *c_in)
    res = list(res)
    outs = []
    p = n_out
    for h in hosted:
        outs.append(res[p:p + len(h.out_shapes)])
        p += len(h.out_shapes)
    return res[:n_out], outs


def _inproj(x, g_mix, w_in, hosted=None):
    s, d = x.shape
    n = w_in.shape[1]
    tm = min(TM_PROJ, s)
    nt = s // tm

    def body(x_ref, g_ref, w_ref, u_ref):
        xv = x_ref[...]
        r = lax.rsqrt(_rowmean(xv * xv) + EPS)
        u_ref[...] = _dot((xv * r * g_ref[...]).astype(BF16), w_ref[...])

    def stages():
        i = pl.program_id(0)
        return i == 0, i == max(nt - 3, 0), i == nt - 1

    return _call(
        body, hosted, stages, grid=(nt,), name="inproj",
        in_specs=[pl.BlockSpec((tm, d), lambda i: (i, 0)), _const_spec((1, d)), _const_spec((d, n))],
        out_specs=[pl.BlockSpec((tm, n), lambda i: (i, 0))], out_shape=[SDS((s, n), F32)], scratch_shapes=[],
        args=(x, g_mix, w_in), sem=("arbitrary",))


def _mixer_fwd(u, x, sp_, w_out, hosted=None):
    s, din = u.shape
    d = x.shape[1]
    lw = din // 3
    tm = min(TM_MIX, s)
    nb = lw // GATE_BLOCK

    def body(u_ref, halo_ref, x_ref, cw_ref, cb_ref, ga_ref, gx_ref, ba_ref, bx_ref, lam_ref, pw_ref, pb_ref,
             ps_ref, gl_ref, gp_ref, wout_ref, h_ref, yn_ref, hres_ref, saved_ref, pooled_ref,
             gw_s, a_s, b_s, carry_s):
        i = pl.program_id(0)

        @pl.when(i == 0)
        def _():
            _build_gate_blocks(ga_ref, gx_ref, gw_s)
            carry_s[...] = jnp.zeros_like(carry_s)

        uv = u_ref[...]
        hal = jnp.where(i > 0, halo_ref[...], 0.0)
        f = _mixer_recompute(uv, hal, i * tm, cw_ref[...], cb_ref[...], gw_s, ba_ref[...], bx_ref[...],
                             lam_ref[...], pw_ref, pb_ref[...], ps_ref[...])
        for k, name in enumerate(MIX_SAVED):
            saved_ref[k] = f[name]
        pooled_ref[...] = jnp.concatenate(f["pooled"], axis=1).astype(BF16)
        bb = f["mult"] * (f["ig"] * f["xc"])
        a1, b1 = _scan_level1(f["a"], bb, reverse=False)
        a_s[...] = a1
        b_s[...] = b1
        carry_s[...] = _scan_level2(a_s, b_s, h_ref, carry_s[...], reverse=False)
        y_lru = h_ref[...] * f["ge"]
        rl = lax.rsqrt(_rowmean(y_lru * y_lru) + EPS)
        yp = f["y_pool"]
        rp = lax.rsqrt(_rowmean(yp * yp) + EPS)
        yn = jnp.concatenate([y_lru * rl * gl_ref[...], yp * rp * gp_ref[...]], axis=1).astype(BF16)
        yn_ref[...] = yn
        hres_ref[...] = x_ref[...] + _dot(yn, wout_ref[...])

    small = [sp_[k] for k in ("conv_w", "conv_b", "gate_a_w", "gate_x_w", "gate_a_b", "gate_x_b", "lru_lambda",
                              "pool_w", "pool_b", "pool_scale", "norm_lru_g", "norm_pool_g")]
    nt = s // tm

    def stages():
        i = pl.program_id(0)
        return i == 0, i == max(nt - 2, 0), i == nt - 1

    return _call(
        body, hosted, stages, grid=(nt,), name="mixer_fwd",
        in_specs=[pl.BlockSpec((tm, din), lambda i: (i, 0)),
                  pl.BlockSpec((HALO, din), lambda i: (jnp.maximum(i * (tm // HALO) - 1, 0), 0)),
                  pl.BlockSpec((tm, d), lambda i: (i, 0))]
        + [_const_spec(a.shape) for a in small] + [_const_spec(w_out.shape)],
        out_specs=[pl.BlockSpec((tm, lw), lambda i: (i, 0)), pl.BlockSpec((tm, d), lambda i: (i, 0)),
                   pl.BlockSpec((tm, d), lambda i: (i, 0)),
                   pl.BlockSpec((len(MIX_SAVED), tm, lw), lambda i: (0, i, 0)),
                   pl.BlockSpec((tm, lw), lambda i: (i, 0))],
        out_shape=[SDS((s, lw), F32), SDS((s, d), BF16), SDS((s, d), F32), SDS((len(MIX_SAVED), s, lw), F32),
                   SDS((s, lw), BF16)],
        scratch_shapes=[pltpu.VMEM((nb, GATE_BLOCK, 2 * GATE_BLOCK), BF16), pltpu.VMEM((tm, lw), F32),
                        pltpu.VMEM((tm, lw), F32), pltpu.VMEM((SUBLANES, lw), F32)],
        args=(u, u, x, *small, w_out), sem=("arbitrary",))


def _row_chunks(tm):
    rc = tm // FFN_ROW_CHUNKS
    return [slice(q * rc, (q + 1) * rc) for q in range(FFN_ROW_CHUNKS)]


def _ffn_up(hres1, g_ffn, w1, w3):
    s, d = hres1.shape
    nj, fc, _ = w1.shape
    tm = min(TM_FFN, s)

    def body(h_ref, gf_ref, w1_ref, w3_ref, h2_ref, a1_ref, a3_ref, ff_ref):
        hv = h_ref[...]
        r = lax.rsqrt(_rowmean(hv * hv) + EPS)
        h2_ref[...] = (hv * r * gf_ref[...]).astype(BF16)
        h2 = h2_ref[...]
        for j in range(nj):
            a1 = _dot_nt(h2, w1_ref[j])
            a3 = _dot_nt(h2, w3_ref[j])
            a1_ref[j] = a1.astype(BF16)
            a3_ref[j] = a3.astype(BF16)
            ff_ref[j] = ((a1 * _sigmoid(a1)) * a3).astype(BF16)

    wspec = _const_spec(w1.shape)
    aspec = pl.BlockSpec((nj, tm, fc), lambda i: (0, i, 0))
    return pl.pallas_call(
        body, grid=(s // tm,), name="ffn_up",
        in_specs=[pl.BlockSpec((tm, d), lambda i: (i, 0)), _const_spec((1, d)), wspec, wspec],
        out_specs=[pl.BlockSpec((tm, d), lambda i: (i, 0)), aspec, aspec, aspec],
        out_shape=[SDS((s, d), BF16)] + [SDS((nj, s, fc), BF16)] * 3,
        compiler_params=_cp(("parallel",)))(hres1, g_ffn, w1, w3)


def _ffn_down(ff, hres1, target, g_fin, w2):
    s, d = hres1.shape
    nj, _, fc = ff.shape
    tm = min(TM_FFN, s)

    def body(ff_ref, h_ref, t_ref, gn_ref, w2_ref, dh_ref, dhb_ref, loss_ref, dgn_ref):
        @pl.when(pl.program_id(0) == 0)
        def _():
            loss_ref[...] = jnp.zeros_like(loss_ref)
            dgn_ref[...] = jnp.zeros_like(dgn_ref)

        gn = gn_ref[...]
        for rows in _row_chunks(tm):
            acc = _dot(ff_ref[0, rows, :], w2_ref[0])
            for j in range(1, nj):
                acc = acc + _dot(ff_ref[j, rows, :], w2_ref[j])
            hr2 = h_ref[rows, :] + acc
            r2 = lax.rsqrt(_rowmean(hr2 * hr2) + EPS)
            xh = hr2 * r2
            diff = xh * gn - t_ref[rows, :]
            tot = jnp.sum(jnp.sum(diff * diff, axis=1, keepdims=True), axis=0, keepdims=True)
            loss_ref[...] += tot * (0.5 / d)
            dout = diff * (1.0 / d)
            dgn_ref[...] += _colsum8(dout * xh)
            dh = _rms_bwd(dout, xh, r2, gn)
            dh_ref[rows, :] = dh
            dhb_ref[rows, :] = dh.astype(BF16)

    tile = pl.BlockSpec((tm, d), lambda i: (i, 0))
    return pl.pallas_call(
        body, grid=(s // tm,), name="ffn_down",
        in_specs=[pl.BlockSpec((nj, tm, fc), lambda i: (0, i, 0)), tile, tile, _const_spec((1, d)),
                  _const_spec(w2.shape)],
        out_specs=[tile, tile, pl.BlockSpec((SUBLANES, LANES), lambda i: (0, 0)),
                   pl.BlockSpec((SUBLANES, d), lambda i: (0, 0))],
        out_shape=[SDS((s, d), F32), SDS((s, d), BF16), SDS((SUBLANES, LANES), F32), SDS((SUBLANES, d), F32)],
        compiler_params=_cp(("arbitrary",)))(ff, hres1, target, g_fin, w2)


def _ffn_bwd_gate(dhb, a1, a3, w2):
    s, d = dhb.shape
    nj, _, fc = a1.shape
    tm = min(TM_FFN, s)

    def body(dhb_ref, a1_ref, a3_ref, w2_ref, da1_ref, da3_ref):
        for j in range(nj):
            for rows in _row_chunks(tm):
                dff = _dot_nt(dhb_ref[rows, :], w2_ref[j])
                a1v = a1_ref[j, rows, :].astype(F32)
                sg = _sigmoid(a1v)
                silu = a1v * sg
                da1_ref[j, rows, :] = (dff * a3_ref[j, rows, :].astype(F32)
                                       * (sg * (1.0 + (a1v - silu)))).astype(BF16)
                da3_ref[j, rows, :] = (dff * silu).astype(BF16)

    aspec = pl.BlockSpec((nj, tm, fc), lambda i: (0, i, 0))
    return pl.pallas_call(
        body, grid=(s // tm,), name="ffn_bwd_gate",
        in_specs=[pl.BlockSpec((tm, d), lambda i: (i, 0)), aspec, aspec, _const_spec(w2.shape)],
        out_specs=[aspec, aspec], out_shape=[SDS((nj, s, fc), BF16)] * 2,
        compiler_params=_cp(("parallel",)))(dhb, a1, a3, w2)


def _ffn_bwd_down(da1, da3, dh, hres1, g_ffn, w1, w3, hosted=None):
    s, d = hres1.shape
    nj, _, fc = da1.shape
    tm = min(TM_FFN, s)
    nt = s // tm

    def body(da1_ref, da3_ref, dh_ref, h_ref, gf_ref, w1_ref, w3_ref, dhr_ref, dgf_ref):
        @pl.when(pl.program_id(0) == 0)
        def _():
            dgf_ref[...] = jnp.zeros_like(dgf_ref)

        gf = gf_ref[...]
        for rows in _row_chunks(tm):
            dh2 = None
            for j in range(nj):
                part = _dot(da1_ref[j, rows, :], w1_ref[j]) + _dot(da3_ref[j, rows, :], w3_ref[j])
                dh2 = part if dh2 is None else dh2 + part
            hv = h_ref[rows, :]
            r = lax.rsqrt(_rowmean(hv * hv) + EPS)
            xh = hv * r
            dgf_ref[...] += _colsum8(dh2 * xh)
            dhr_ref[rows, :] = dh_ref[rows, :] + _rms_bwd(dh2, xh, r, gf)

    tile = pl.BlockSpec((tm, d), lambda i: (i, 0))
    aspec = pl.BlockSpec((nj, tm, fc), lambda i: (0, i, 0))
    wspec = _const_spec(w1.shape)

    def stages():
        i = pl.program_id(0)
        return i == 0, i == max(nt - 2, 0), i == nt - 1

    return _call(
        body, hosted, stages, grid=(nt,), name="ffn_bwd_down",
        in_specs=[aspec, aspec, tile, tile, _const_spec((1, d)), wspec, wspec],
        out_specs=[tile, pl.BlockSpec((SUBLANES, d), lambda i: (0, 0))],
        out_shape=[SDS((s, d), F32), SDS((SUBLANES, d), F32)],
        scratch_shapes=[], args=(da1, da3, dh, hres1, g_ffn, w1, w3), sem=("arbitrary",))


def _ffn_wgrad(h2, dhb, ff, da1, da3):
    s, d = h2.shape
    _, _, fc = ff.shape
    tm = min(TM_WGRAD, s)

    def body(h2_ref, dhb_ref, ff_ref, da1_ref, da3_ref, dw1_ref, dw3_ref, dw2_ref):
        @pl.when(pl.program_id(1) == 0)
        def _():
            dw1_ref[...] = jnp.zeros_like(dw1_ref)
            dw3_ref[...] = jnp.zeros_like(dw3_ref)
            dw2_ref[...] = jnp.zeros_like(dw2_ref)

        h2v = h2_ref[...]
        dw1_ref[0] += _dot_tn(da1_ref[0], h2v)
        dw3_ref[0] += _dot_tn(da3_ref[0], h2v)
        dw2_ref[0] += _dot_tn(ff_ref[0], dhb_ref[...])

    wspec = pl.BlockSpec((1, fc, d), lambda j, i: (j, 0, 0))
    return pl.pallas_call(
        body, grid=(N_CHIPS, s // tm), name="ffn_wgrad",
        in_specs=[pl.BlockSpec((tm, d), lambda j, i: (i, 0)), pl.BlockSpec((tm, d), lambda j, i: (i, 0))]
        + [pl.BlockSpec((1, tm, fc), lambda j, i: (j, i, 0))] * 3,
        out_specs=[wspec] * 3, out_shape=[SDS((N_CHIPS, fc, d), F32)] * 3,
        compiler_params=_cp(("parallel", "arbitrary")))(h2, dhb, ff, da1, da3)


def _mixer_bwd(u, saved, pooled, h, dhres1, sp_, w_out, hosted=None):
    s, din = u.shape
    d = dhres1.shape[1]
    lw = din // 3
    tm = min(TM_MIX, s)
    nt = s // tm
    nb = lw // GATE_BLOCK
    hd = sp_["gate_a_w"].shape[1]

    def body(ul_ref, saved_ref, pooled_ref, h_ref, hhalo_ref, dhr_ref, cw_ref, cb_ref, ga_ref, gx_ref, ba_ref,
             bx_ref, lam_ref, pw_ref, pb_ref, ps_ref, gl_ref, gp_ref, wout_ref, du_ref, slab_ref,
             gw_s, a_s, b_s, e_s, ecarry_s, dxc_s, q_s, vec_s, cwacc_s, dgw_s, dpw_s):
        i = pl.program_id(0)
        tile = nt - 1 - i

        @pl.when(i == 0)
        def _():
            _build_gate_blocks(ga_ref, gx_ref, gw_s)
            for ref in (ecarry_s, dxc_s, q_s, vec_s, cwacc_s, dgw_s, dpw_s):
                ref[...] = jnp.zeros_like(ref)

        cw = cw_ref[...]
        lam = lam_ref[...]
        ps = ps_ref[...]
        f = {name: saved_ref[k] for k, name in enumerate(MIX_SAVED)}
        f["mult"] = jnp.sqrt(jnp.maximum(f["m2raw"], 1e-12))
        f["sp"] = _softplus_neg(lam)
        f["xcb"] = f["xc"].astype(BF16)
        pooled = pooled_ref[...]
        row = lax.broadcasted_iota(jnp.int32, (tm, LANES), 0) + tile * tm
        f["invs"] = [1.0 / jnp.minimum(row + 1, w).astype(F32) for w in POOL_WINDOWS]
        f["z"] = jnp.concatenate(
            [_dot(pooled[:, g * LANES:(g + 1) * LANES], pw_ref[g].astype(BF16))
             for g in range(len(POOL_WINDOWS))], axis=1) + pb_ref[...]
        f["y_pool"] = f["z"] * ps
        u_l = ul_ref[...]
        hv = h_ref[...]
        h_prev = _shift_down(jnp.where(tile > 0, hhalo_ref[...], 0.0), hv, 1)
        y_lru = hv * f["ge"]
        rl = lax.rsqrt(_rowmean(y_lru * y_lru) + EPS)
        yp = f["y_pool"]
        rp = lax.rsqrt(_rowmean(yp * yp) + EPS)
        xh_l = y_lru * rl
        xh_p = yp * rp

        dyn = _dot_nt(dhr_ref[...].astype(BF16), wout_ref[...])
        d_nl, d_np = dyn[:, :lw], dyn[:, lw:]
        vec = {}
        vec[ROW_GL] = _colsum8(d_nl * xh_l)
        vec[ROW_GP] = _colsum8(d_np * xh_p)
        d_ylru = _rms_bwd(d_nl, xh_l, rl, gl_ref[...])
        d_ypool = _rms_bwd(d_np, xh_p, rp, gp_ref[...])

        vec[ROW_PS] = _colsum8(d_ypool * f["z"])
        dz = d_ypool * ps
        vec[ROW_PB] = _colsum8(dz)
        dzb = dz.astype(BF16)
        dup = []
        for gi, w in enumerate(POOL_WINDOWS):
            sl = slice(gi * LANES, (gi + 1) * LANES)
            dpw_s[:, sl] += _dot_tn(pooled[:, sl], dzb[:, sl])
            dpool = _dot_nt(dzb[:, sl], pw_ref[gi].astype(BF16))
            q = dpool * f["invs"][gi]
            e = jnp.concatenate([q, q_s[:, sl]], axis=0)
            k = 1
            while k < w:
                e = e + pltpu.roll(e, tm + HALO - k, 0)
                k *= 2
            dup.append(e[:tm] - dpool)
            q_s[:, sl] = q[:HALO]

        d_hout = d_ylru * f["ge"]
        d_ug = d_ylru * hv * f["dge"]
        a = f["a"]
        a1, b1 = _scan_level1(a, a * d_hout, reverse=True)
        a_s[...] = a1
        b_s[...] = b1
        e_next = ecarry_s[...]
        ecarry_s[...] = _scan_level2(a_s, b_s, e_s, e_next, reverse=True)
        sv = d_hout + _shift_up(e_s[...], e_next, 1)
        d_a = sv * h_prev
        mult, ig, xc, r = f["mult"], f["ig"], f["xc"], f["r"]
        d_mult = sv * (ig * xc)
        d_ig = sv * mult * xc
        d_xc = sv * mult * ig
        d_la = d_a * a + jnp.where(f["m2raw"] > 1e-12, d_mult * (-(a * a) / mult), 0.0)
        d_r = d_la * (-LRU_C * f["sp"])
        vec[ROW_LAM] = _colsum8(d_la * (-LRU_C * r))
        d_pr = d_r * r * (1.0 - r)
        d_pi = d_ig * ig * (1.0 - ig)
        vec[ROW_BA] = _colsum8(d_pr)
        vec[ROW_BX] = _colsum8(d_pi)
        dxc_parts = []
        for b in range(nb):
            sl = slice(b * GATE_BLOCK, (b + 1) * GATE_BLOCK)
            rhs = jnp.concatenate([d_pr[:, sl], d_pi[:, sl]], axis=1).astype(BF16)
            dgw_s[b] += _dot_tn(f["xcb"][:, sl], rhs)
            dxc_parts.append(_dot_nt(rhs, gw_s[b]))
        d_xc = d_xc + jnp.concatenate(dxc_parts, axis=1)
        vec[ROW_CONV_B] = _colsum8(d_xc)
        dxc_next = dxc_s[...]
        d_ul = None
        for k in range(CONV_WIDTH):
            ahead = _shift_up(d_xc, dxc_next, CONV_WIDTH - 1 - k)
            cwacc_s[k * SUBLANES:(k + 1) * SUBLANES, :] += _colsum8(ahead * u_l)
            term = ahead * cw[k:k + 1, :]
            d_ul = term if d_ul is None else d_ul + term
        dxc_s[...] = d_xc[:SUBLANES]
        for row, val in vec.items():
            vec_s[row * SUBLANES:(row + 1) * SUBLANES, :] += val
        du_ref[...] = jnp.concatenate([d_ul, d_ug] + dup, axis=1).astype(BF16)

        @pl.when(i == nt - 1)
        def _():
            rows = []
            for row in range(ROW_GA):
                if row in (ROW_CONV_W, ROW_CONV_W + 1, ROW_CONV_W + 2, ROW_CONV_W + 3):
                    k = row - ROW_CONV_W
                    v = jnp.sum(cwacc_s[k * SUBLANES:(k + 1) * SUBLANES, :], axis=0, keepdims=True)
                elif row <= ROW_GP:
                    v = jnp.sum(vec_s[row * SUBLANES:(row + 1) * SUBLANES, :], axis=0, keepdims=True)
                    if row == ROW_LAM:
                        v = v * (-1.0 / (1.0 + jnp.exp(lam)))
                else:
                    v = jnp.zeros((1, lw), F32)
                rows.append(v)
            slab_ref[0:ROW_GA, :] = jnp.concatenate(rows, axis=0)
            lane = lax.broadcasted_iota(jnp.int32, (hd, GATE_BLOCK), 1)
            for b in range(nb):
                for off, row0 in ((0, ROW_GA), (GATE_BLOCK, ROW_GX)):
                    acc = jnp.zeros((hd, GATE_BLOCK), F32)
                    for hh in range(GATE_BLOCK // hd):
                        m = (lane >= hh * hd) & (lane < (hh + 1) * hd)
                        acc = acc + jnp.where(m, dgw_s[b, hh * hd:(hh + 1) * hd, off:off + GATE_BLOCK], 0.0)
                    slab_ref[row0:row0 + hd, b * GATE_BLOCK:(b + 1) * GATE_BLOCK] = acc
            slab_ref[ROW_PW:ROW_PW + LANES, :] = dpw_s[...]

    small = [sp_[k] for k in ("conv_w", "conv_b", "gate_a_w", "gate_x_w", "gate_a_b", "gate_x_b", "lru_lambda",
                              "pool_w", "pool_b", "pool_scale", "norm_lru_g", "norm_pool_g")]
    rev = lambda i: nt - 1 - i

    def stages():
        i = pl.program_id(0)
        return i == 0, i == max(nt - 3, 0), i == nt - 1

    return _call(
        body, hosted, stages, grid=(nt,), name="mixer_bwd",
        in_specs=[pl.BlockSpec((tm, lw), lambda i: (rev(i), 0)),
                  pl.BlockSpec((len(MIX_SAVED), tm, lw), lambda i: (0, rev(i), 0)),
                  pl.BlockSpec((tm, lw), lambda i: (rev(i), 0)),
                  pl.BlockSpec((tm, lw), lambda i: (rev(i), 0)),
                  pl.BlockSpec((SUBLANES, lw), lambda i: (jnp.maximum(rev(i) * (tm // SUBLANES) - 1, 0), 0)),
                  pl.BlockSpec((tm, d), lambda i: (rev(i), 0))]
        + [_const_spec(a.shape) for a in small] + [_const_spec(w_out.shape)],
        out_specs=[pl.BlockSpec((tm, din), lambda i: (rev(i), 0)),
                   pl.BlockSpec((MIX_SLAB_ROWS, SLAB_W), lambda i: (0, 0))],
        out_shape=[SDS((s, din), BF16), SDS((MIX_SLAB_ROWS, SLAB_W), F32)],
        scratch_shapes=[pltpu.VMEM((nb, GATE_BLOCK, 2 * GATE_BLOCK), BF16),
                        pltpu.VMEM((tm, lw), F32), pltpu.VMEM((tm, lw), F32), pltpu.VMEM((tm, lw), F32),
                        pltpu.VMEM((SUBLANES, lw), F32), pltpu.VMEM((SUBLANES, lw), F32),
                        pltpu.VMEM((HALO, lw), F32), pltpu.VMEM((ROW_GA * SUBLANES, lw), F32),
                        pltpu.VMEM((CONV_WIDTH * SUBLANES, lw), F32),
                        pltpu.VMEM((nb, GATE_BLOCK, 2 * GATE_BLOCK), F32), pltpu.VMEM((LANES, lw), F32)],
        args=(u, saved, pooled, h, h, dhres1, *small, w_out), sem=("arbitrary",))


def _inproj_bwd(x, du, dhres1, yn, g_mix, w_in, hosted=None):
    s, d = x.shape
    n = w_in.shape[1]
    nc = n // N_CHIPS
    tm = min(TM_PROJ, s)
    nt = s // tm

    def body(x_ref, du_ref, dhr_ref, yn_ref, g_ref, w_ref, gx_ref, dwin_ref, dwout_ref, dg_ref):
        i = pl.program_id(0)

        @pl.when(i == 0)
        def _():
            dwin_ref[...] = jnp.zeros_like(dwin_ref)
            dwout_ref[...] = jnp.zeros_like(dwout_ref)
            dg_ref[...] = jnp.zeros_like(dg_ref)

        xv = x_ref[...]
        g = g_ref[...]
        r = lax.rsqrt(_rowmean(xv * xv) + EPS)
        xh = xv * r
        h1 = (xh * g).astype(BF16)
        duv = du_ref[...]
        dh1 = _dot_nt(duv, w_ref[...])
        dg_ref[...] += _colsum8(dh1 * xh)
        dhr = dhr_ref[...]
        gx_ref[...] = dhr + _rms_bwd(dh1, xh, r, g)
        for jj in range(N_CHIPS):
            dwin_ref[jj] += _dot_tn(h1, duv[:, jj * nc:(jj + 1) * nc])
        dwout_ref[...] += _dot_tn(yn_ref[...], dhr.astype(BF16))

    def stages():
        i = pl.program_id(0)
        return i == 0, i == max(nt - 3, 0), i == nt - 1

    return _call(
        body, hosted, stages, grid=(nt,), name="inproj_bwd",
        in_specs=[pl.BlockSpec((tm, d), lambda i: (i, 0)), pl.BlockSpec((tm, n), lambda i: (i, 0)),
                  pl.BlockSpec((tm, d), lambda i: (i, 0)), pl.BlockSpec((tm, d), lambda i: (i, 0)),
                  _const_spec((1, d)), _const_spec((d, n))],
        out_specs=[pl.BlockSpec((tm, d), lambda i: (i, 0)), pl.BlockSpec((N_CHIPS, d, nc), lambda i: (0, 0, 0)),
                   pl.BlockSpec((d, d), lambda i: (0, 0)), pl.BlockSpec((SUBLANES, d), lambda i: (0, 0))],
        out_shape=[SDS((s, d), F32), SDS((N_CHIPS, d, nc), F32), SDS((d, d), F32), SDS((SUBLANES, d), F32)],
        scratch_shapes=[], args=(x, du, dhres1, yn, g_mix, w_in), sem=("arbitrary",))


def _place():
    x, y, c = lax.axis_index("x"), lax.axis_index("y"), lax.axis_index("c")
    return x, y, c


def _other_chips(x, y):
    return [(1 - x, y), (x, 1 - y), (1 - x, 1 - y)]


ANY = pl.BlockSpec(memory_space=pl.ANY)
VMEM_SPEC = pl.BlockSpec(memory_space=pltpu.VMEM)

_GATHERED = {"w_in": "cols", "w_out": "major", "ffn_w1": "major", "ffn_w3": "major", "ffn_w2": "major"}
_BIG = ("w_in", "w_out", "ffn_w1", "ffn_w3", "ffn_w2")


def _gather_weights(shards, conv_w, n_remote):
    n = len(shards)
    full_shapes = []
    for name, sh in zip(_BIG, shards):
        r, cdim = sh.shape
        if _GATHERED[name] == "cols":
            assert cdim % LANES == 0
            full_shapes.append((r, cdim * N_CHIPS))
        else:
            full_shapes.append((N_CHIPS, r, cdim))

    def region(ref, name, sh, jj, cc):
        r, cdim = sh
        rows = pl.ds(0, r) if cc is None else pl.ds(pl.multiple_of(cc * (r // 2), 16), r // 2)
        if _GATHERED[name] == "cols":
            return ref.at[rows, pl.ds(pl.multiple_of(jj * cdim, LANES), cdim)]
        return ref.at[jj, rows, :]

    def staged(ref, sh, cc):
        r = sh[0]
        return ref.at[pl.ds(pl.multiple_of(cc * (r // 2), 16), r // 2), :]

    def body(*refs):
        ins, cw_in = refs[:n], refs[n]
        outs, cw_out = refs[n + 1:2 * n + 1], refs[2 * n + 1]
        stage = refs[2 * n + 2:3 * n + 2]
        cw_stage, lsem, ssem, rsem, fssem, frsem, cssem, crsem = refs[3 * n + 2:]
        x, y, c = _place()
        j = 2 * x + y
        chips = _other_chips(x, y)
        for w in range(n_remote):
            stage[w][...] = ins[w][...].astype(BF16)
        cw_stage[...] = jnp.zeros_like(cw_stage)
        cw_stage[0:CONV_WIDTH, :] = cw_in[...]
        shs = [s_.shape for s_ in shards]
        local = [pltpu.make_async_copy(stage[w], region(outs[w], _BIG[w], shs[w], j, None), lsem.at[w])
                 for w in range(n)]
        local.append(pltpu.make_async_copy(cw_stage, cw_out.at[j], lsem.at[n]))
        sends = []
        for k, (px, py) in enumerate(chips):
            for w in range(n_remote):
                sends.append(pltpu.make_async_remote_copy(
                    src_ref=staged(stage[w], shs[w], c), dst_ref=region(outs[w], _BIG[w], shs[w], j, c),
                    send_sem=ssem.at[k * n + w], recv_sem=rsem.at[k * n + w], device_id=(px, py, c),
                    device_id_type=MESH))
            sends.append(pltpu.make_async_remote_copy(
                src_ref=cw_stage, dst_ref=cw_out.at[j], send_sem=cssem.at[k], recv_sem=crsem.at[k],
                device_id=(px, py, c), device_id_type=MESH))
        for cp in sends:
            cp.start()
        for w in range(n_remote, n):
            stage[w][...] = ins[w][...].astype(BF16)
        for cp in local:
            cp.start()
        fwd = []
        for k, (px, py) in enumerate(chips):
            jk = 2 * px + py
            for w in range(n_remote):
                reg = region(outs[w], _BIG[w], shs[w], jk, c)
                pltpu.make_async_remote_copy(src_ref=reg, dst_ref=reg, send_sem=ssem.at[k * n + w],
                                             recv_sem=rsem.at[k * n + w], device_id=(px, py, c),
                                             device_id_type=MESH).wait_recv()
                cp = pltpu.make_async_remote_copy(src_ref=reg, dst_ref=reg, send_sem=fssem.at[k * n + w],
                                                  recv_sem=frsem.at[k * n + w], device_id=(x, y, 1 - c),
                                                  device_id_type=MESH)
                cp.start()
                fwd.append(cp)
            pltpu.make_async_remote_copy(src_ref=cw_stage, dst_ref=cw_out.at[jk], send_sem=cssem.at[k],
                                         recv_sem=crsem.at[k], device_id=(px, py, c),
                                         device_id_type=MESH).wait_recv()
        for k, (px, py) in enumerate(chips):
            jk = 2 * px + py
            for w in range(n_remote):
                reg = region(outs[w], _BIG[w], shs[w], jk, 1 - c)
                pltpu.make_async_remote_copy(src_ref=reg, dst_ref=reg, send_sem=fssem.at[k * n + w],
                                             recv_sem=frsem.at[k * n + w], device_id=(x, y, 1 - c),
                                             device_id_type=MESH).wait_recv()
        for cp in sends + fwd:
            cp.wait_send()
        for cp in local:
            cp.wait()

    nsem = 3 * n
    return pl.pallas_call(
        body, name="gather_first",
        in_specs=[VMEM_SPEC] * (n + 1), out_specs=[ANY] * (n + 1),
        out_shape=[SDS(fs, BF16) for fs in full_shapes] + [SDS((N_CHIPS, SUBLANES, LANES), F32)],
        scratch_shapes=[pltpu.VMEM(s_.shape, BF16) for s_ in shards] + [pltpu.VMEM((SUBLANES, LANES), F32)]
        + [pltpu.SemaphoreType.DMA((n + 1,))] + [pltpu.SemaphoreType.DMA((nsem,))] * 4
        + [pltpu.SemaphoreType.DMA((3,))] * 2,
        compiler_params=_cp())(*shards, conv_w)


def _start_all(make):
    def f(ins, outs, sems):
        for cp in make(ins, outs, sems):
            cp.start()
    return f


def _wait_all(make):
    def f(ins, outs, sems):
        for cp in make(ins, outs, sems):
            cp.wait()
    return f


def _ffn_gather_hosted(arrs):
    n = len(arrs)

    def make(outs, sems):
        ssem, rsem, fs, fr = sems
        x, y, c = _place()
        j = 2 * x + y

        def reg(w, jj, cc):
            hr = arrs[w].shape[1] // 2
            return outs[w].at[jj, pl.ds(pl.multiple_of(cc * hr, 16), hr), :]

        def rc(w, jj, cc, s_sem, r_sem, dev):
            return pltpu.make_async_remote_copy(src_ref=reg(w, jj, cc), dst_ref=reg(w, jj, cc), send_sem=s_sem,
                                                recv_sem=r_sem, device_id=dev, device_id_type=MESH)

        sends, recvs, fwds, frecvs = [], [], [], []
        for k, (px, py) in enumerate(_other_chips(x, y)):
            jk = 2 * px + py
            for w in range(n):
                q = k * n + w
                sends.append(rc(w, j, c, ssem.at[q], rsem.at[q], (px, py, c)))
                recvs.append(rc(w, jk, c, ssem.at[q], rsem.at[q], (px, py, c)))
                fwds.append(rc(w, jk, c, fs.at[q], fr.at[q], (x, y, 1 - c)))
                frecvs.append(rc(w, jk, 1 - c, fs.at[q], fr.at[q], (x, y, 1 - c)))
        return sends, recvs, fwds, frecvs

    def start(ins, outs, sems):
        for cp in make(outs, sems)[0]:
            cp.start()

    def mid(ins, outs, sems):
        _, recvs, fwds, _ = make(outs, sems)
        for r, f in zip(recvs, fwds):
            r.wait_recv()
            f.start()

    def finish(ins, outs, sems):
        sends, _, fwds, frecvs = make(outs, sems)
        for r in frecvs:
            r.wait_recv()
        for cp in sends + fwds:
            cp.wait_send()

    return _Hosted(arrs, [SDS(a.shape, a.dtype) for a in arrs], [3 * n] * 4, start, finish, mid=mid,
                   aliases={w: w for w in range(n)})


def _rs_sibling_hosted(arrs):
    n = len(arrs)

    def make(ins, outs, sems):
        x, y, c = _place()
        cps = []
        for w in range(n):
            hr = arrs[w].shape[1] // 2
            src = ins[w].at[:, pl.ds(pl.multiple_of((1 - c) * hr, SUBLANES), hr), :]
            cps.append(pltpu.make_async_remote_copy(src_ref=src, dst_ref=outs[w], send_sem=sems[0].at[w],
                                                    recv_sem=sems[1].at[w], device_id=(x, y, 1 - c),
                                                    device_id_type=MESH))
        return cps

    return _Hosted(arrs, [SDS((a.shape[0], a.shape[1] // 2, a.shape[2]), F32) for a in arrs], [n, n],
                   _start_all(make), _wait_all(make))


def _rs_chips_hosted(parts):
    n = len(parts)

    def make(ins, outs, sems):
        x, y, c = _place()
        j = 2 * x + y
        cps = []
        for k, (px, py) in enumerate(_other_chips(x, y)):
            jk = 2 * px + py
            for w in range(n):
                cps.append(pltpu.make_async_remote_copy(
                    src_ref=ins[w].at[jk], dst_ref=outs[w].at[j], send_sem=sems[0].at[k * n + w],
                    recv_sem=sems[1].at[k * n + w], device_id=(px, py, c), device_id_type=MESH))
        return cps

    return _Hosted(parts, [SDS(p.shape, p.dtype) for p in parts], [3 * n, 3 * n], _start_all(make), _wait_all(make))


def _rs_swap_hosted(halves):
    n = len(halves)

    def make(ins, outs, sems):
        x, y, c = _place()
        return [pltpu.make_async_remote_copy(src_ref=ins[w], dst_ref=outs[w], send_sem=sems[0].at[w],
                                             recv_sem=sems[1].at[w], device_id=(x, y, 1 - c), device_id_type=MESH)
                for w in range(n)]

    return _Hosted(halves, [SDS(h.shape, F32) for h in halves], [n, n], _start_all(make), _wait_all(make))


HBM_SPEC = pl.BlockSpec(memory_space=pltpu.HBM)
SEM_SPEC = pl.BlockSpec(memory_space=pltpu.SEMAPHORE)
_EFFECT = pltpu.SideEffectType.DATAFLOW_SIDE_EFFECTING


def _split_start(h, name):
    n_in, n_out, ns = len(h.ins), len(h.out_shapes), len(h.sems)
    ins = [pltpu.with_memory_space_constraint(a, pltpu.HBM) for a in h.ins]
    lands = [pltpu.with_memory_space_constraint(lax.empty(o.shape, o.dtype), pltpu.HBM) for o in h.out_shapes]

    def body(*refs):
        i_refs, l_refs = refs[:n_in], refs[n_in:n_in + n_out]
        s_refs = refs[n_in + n_out:n_in + n_out + ns]
        token = refs[-1]
        h.start(i_refs, l_refs, s_refs)
        token[...] = jnp.zeros_like(token)

    res = pl.pallas_call(
        body, name=name, in_specs=[HBM_SPEC] * (n_in + n_out),
        out_specs=[SEM_SPEC] * ns + [HBM_SPEC] * n_out + [VMEM_SPEC],
        out_shape=[pltpu.SemaphoreType.DMA((k,)) for k in h.sems]
        + [pltpu.HBM(o.shape, o.dtype) for o in h.out_shapes] + [SDS((SUBLANES, LANES), F32)],
        input_output_aliases={n_in + k: ns + k for k in range(n_out)},
        compiler_params=pltpu.CompilerParams(has_side_effects=_EFFECT))(*ins, *lands)
    return list(res[:ns]) + ins + list(res[ns:-1]), res[-1]


def _split_wait(h, state, after, name):
    n_in, n_out, ns = len(h.ins), len(h.out_shapes), len(h.sems)
    sems, bufs = state[:ns], state[ns:]

    def body(*refs):
        i_refs, l_refs = refs[:n_in], refs[n_in:n_in + n_out]
        s_refs = refs[n_in + n_out:n_in + n_out + ns]
        h.finish(i_refs, l_refs, s_refs)

    res = pl.pallas_call(
        body, name=name, in_specs=[HBM_SPEC] * (n_in + n_out) + [SEM_SPEC] * ns + [ANY],
        out_specs=[HBM_SPEC] * n_out,
        out_shape=[pltpu.HBM(b.shape, b.dtype) for b in bufs[n_in:]],
        input_output_aliases={n_in + k: k for k in range(n_out)},
        compiler_params=pltpu.CompilerParams(has_side_effects=_EFFECT))(*bufs, *sems, after)
    return list(res)


def _run_comm(hosted, name):
    return _call(lambda: None, hosted, None, name=name, grid=(), in_specs=[], out_specs=[], out_shape=[],
                 scratch_shapes=[], args=(), sem=None)[1]


def _row_tile(rows, cols, n_arrays):
    budget = 24 * 1024 * 1024 // (2 * 4 * n_arrays * cols)
    best = SUBLANES
    for t in range(SUBLANES, rows + 1, SUBLANES):
        if rows % t == 0 and t <= budget:
            best = t
    return best


def _place_index(which):
    x, y, c = _place()
    v = c if which == "c" else 2 * x + y
    return jnp.reshape(v, (1,)).astype(jnp.int32)


def _add_own_half(full, recv, name, wire=BF16):
    nsh, rows, cols = full.shape
    hr = rows // 2
    t = _row_tile(hr, cols, 4)
    nt = hr // t

    def body(c_ref, a_ref, b_ref, o_ref, ob_ref):
        v = a_ref[...] + b_ref[...]
        o_ref[...] = v
        ob_ref[...] = v.astype(wire)

    half = pl.BlockSpec((1, t, cols), lambda s_, i, c_ref: (s_, i, 0))
    return pl.pallas_call(
        body, name=name,
        grid_spec=pltpu.PrefetchScalarGridSpec(
            num_scalar_prefetch=1, grid=(nsh, nt),
            in_specs=[pl.BlockSpec((1, t, cols), lambda s_, i, c_ref: (s_, c_ref[0] * nt + i, 0)), half],
            out_specs=[half, half]),
        out_shape=[SDS((nsh, hr, cols), F32), SDS((nsh, hr, cols), wire)],
        compiler_params=_cp(("parallel", "parallel")))(_place_index("c"), full, recv)


def _sum_chips(own, recv, name):
    nsh, hr, cols = own.shape
    t = _row_tile(hr, cols, 6)

    def body(j_ref, own_ref, *rest):
        r_refs, o_ref = rest[:nsh], rest[nsh]
        j = j_ref[0]
        mine = own_ref[0]
        parts = [jnp.where(j == k, mine, r_refs[k][0].astype(F32)) for k in range(nsh)]
        o_ref[...] = ((parts[0] + parts[1]) + parts[2]) + parts[3]

    def other(k):
        return pl.BlockSpec((1, t, cols), lambda i, j_ref: (jnp.where(j_ref[0] == k, (k + 1) % nsh, k), i, 0))

    return pl.pallas_call(
        body, name=name,
        grid_spec=pltpu.PrefetchScalarGridSpec(
            num_scalar_prefetch=1, grid=(hr // t,),
            in_specs=[pl.BlockSpec((1, t, cols), lambda i, j_ref: (j_ref[0], i, 0))]
            + [other(k) for k in range(nsh)],
            out_specs=pl.BlockSpec((t, cols), lambda i, j_ref: (i, 0))),
        out_shape=SDS((hr, cols), F32), compiler_params=_cp(("parallel",)))(_place_index("j"), own, *([recv] * nsh))


def _adamw_math(w, g, m, v):
    m = ADAM_B1 * m + (1.0 - ADAM_B1) * g
    v = ADAM_B2 * v + (1.0 - ADAM_B2) * (g * g)
    m_hat = m / (1.0 - ADAM_B1 ** ADAM_STEP)
    v_hat = v / (1.0 - ADAM_B2 ** ADAM_STEP)
    delta = -ADAM_LR * (m_hat / (jnp.sqrt(v_hat) + ADAM_EPS) + ADAM_WD * w)
    return delta, m, v


def _adamw_big(w, g_own, g_sib, m, v, name, token=None):
    _, rows, cols = w.shape
    hr = rows // 2
    t = _row_tile(hr, cols, 9)
    nth = hr // t
    if token is None:
        token = jnp.zeros((SUBLANES, LANES), F32)

    def body(c_ref, w_ref, go_ref, gs_ref, m_ref, v_ref, tok_ref, g_ref, d_ref, mo_ref, vo_ref):
        own = (pl.program_id(0) // nth) == c_ref[0]
        g = jnp.where(own, go_ref[...], gs_ref[...]) + tok_ref[0:1, 0:1]
        g_ref[0] = g
        d_ref[0], mo_ref[0], vo_ref[0] = _adamw_math(w_ref[0], g, m_ref[0], v_ref[0])

    spec = pl.BlockSpec((1, t, cols), lambda i, c_ref: (0, i, 0))
    hspec = pl.BlockSpec((t, cols), lambda i, c_ref: (i % nth, 0))
    tspec = pl.BlockSpec((SUBLANES, LANES), lambda i, c_ref: (0, 0))
    return pl.pallas_call(
        body, name=name,
        grid_spec=pltpu.PrefetchScalarGridSpec(
            num_scalar_prefetch=1, grid=(2 * nth,), in_specs=[spec, hspec, hspec, spec, spec, tspec],
            out_specs=[spec] * 4),
        out_shape=[SDS((1, rows, cols), F32)] * 4,
        compiler_params=_cp(("parallel",)))(_place_index("c"), w, g_own, g_sib, m, v, token)


def _build_slab(mix_slab, dg_mix, dg_ffn, dg_fin, loss8):
    def body(ms_ref, gm_ref, gf_ref, gn_ref, loss_ref, out_ref):
        rows = []
        for ref in (gm_ref, gf_ref, gn_ref):
            v = jnp.sum(ref[...], axis=0, keepdims=True)
            rows += [v[:, :SLAB_W], v[:, SLAB_W:]]
        rows.append(jnp.concatenate([loss_ref[0:1, :]] * (SLAB_W // LANES), axis=1))
        rows.append(jnp.zeros((SLAB_ROWS - ROW_LOSS - 1, SLAB_W), F32))
        tail = jnp.concatenate(rows, axis=0)
        for k in range(N_CHIPS):
            out_ref[k, 0:MIX_SLAB_ROWS, :] = ms_ref[...]
            out_ref[k, MIX_SLAB_ROWS:SLAB_ROWS, :] = tail

    return pl.pallas_call(
        body, name="build_slab", in_specs=[VMEM_SPEC] * 5, out_specs=VMEM_SPEC,
        out_shape=SDS((N_CHIPS, SLAB_ROWS, SLAB_W), F32),
        compiler_params=_cp())(mix_slab, dg_mix, dg_ffn, dg_fin, loss8)


_SMALL_ROWS = (("conv_b", ROW_CONV_B), ("gate_a_b", ROW_BA), ("gate_x_b", ROW_BX), ("lru_lambda", ROW_LAM),
               ("pool_b", ROW_PB), ("pool_scale", ROW_PS), ("norm_lru_g", ROW_GL), ("norm_pool_g", ROW_GP))
_WIDE_ROWS = (("norm_mix_g", ROW_MIX), ("norm_ffn_g", ROW_FFN), ("final_norm_g", ROW_FIN))
_BLOCK_ROWS = (("gate_a_w", ROW_GA), ("gate_x_w", ROW_GX), ("pool_w", ROW_PW))
_SMALL_ORDER = tuple(n for n, _ in _SMALL_ROWS) + tuple(n for n, _ in _WIDE_ROWS) + tuple(
    n for n, _ in _BLOCK_ROWS) + ("conv_w",)


def _adamw_small(slab_own, slab_sib, wmv):
    names = _SMALL_ORDER
    flat = [a for nme in names for a in wmv[nme]]
    nin = len(flat)

    def body(*refs):
        own_ref, sib_ref, j_ref = refs[0], refs[1], refs[2]
        ins = refs[3:3 + nin]
        outs = refs[3 + nin:-1]
        first = j_ref[1] == 0
        slab_ref = jnp.concatenate([jnp.where(first, own_ref[...], sib_ref[...]),
                                    jnp.where(first, sib_ref[...], own_ref[...])], axis=0)
        refs[-1][...] = jnp.broadcast_to(slab_ref[ROW_LOSS:ROW_LOSS + 1, 0:LANES], (SUBLANES, LANES))
        grads = {}
        for nme, row in _SMALL_ROWS:
            grads[nme] = slab_ref[row:row + 1, :]
        for nme, row in _WIDE_ROWS:
            grads[nme] = jnp.concatenate([slab_ref[row:row + 1, :], slab_ref[row + 1:row + 2, :]], axis=1)
        full = slab_ref[ROW_CONV_W:ROW_CONV_W + CONV_WIDTH, :]
        jv = j_ref[0]
        g = jnp.zeros((CONV_WIDTH, LANES), F32)
        for jj in range(N_CHIPS):
            g = jnp.where(jv == jj, full[:, jj * LANES:(jj + 1) * LANES], g)
        grads["conv_w"] = g
        block_rows = dict(_BLOCK_ROWS)
        for idx, nme in enumerate(names):
            w_ref, m_ref, v_ref = ins[3 * idx:3 * idx + 3]
            if nme in block_rows:
                nblk, r, c = w_ref.shape
                parts = [(b, slab_ref[block_rows[nme]:block_rows[nme] + r, b * c:(b + 1) * c]) for b in range(nblk)]
            else:
                parts = [(Ellipsis, grads[nme])]
            for b, g in parts:
                delta, m, v = _adamw_math(w_ref[b], g, m_ref[b], v_ref[b])
                outs[4 * idx][b] = g
                outs[4 * idx + 1][b] = delta
                outs[4 * idx + 2][b] = m
                outs[4 * idx + 3][b] = v

    place = jnp.concatenate([_place_index("j"), _place_index("c")])
    out_shape = [SDS(wmv[nme][0].shape, F32) for nme in names for _ in range(4)] + [SDS((SUBLANES, LANES), F32)]
    res = pl.pallas_call(
        body, name="adamw_small",
        in_specs=[VMEM_SPEC, VMEM_SPEC, pl.BlockSpec(memory_space=pltpu.SMEM)] + [VMEM_SPEC] * nin,
        out_specs=[VMEM_SPEC] * len(out_shape), out_shape=out_shape,
        compiler_params=_cp())(slab_own, slab_sib, place, *flat)
    return {nme: tuple(res[4 * idx:4 * idx + 4]) for idx, nme in enumerate(names)}, res[-1]


_FFN = ("ffn_w1", "ffn_w3", "ffn_w2")
_TRANSPOSED = ("ffn_w1", "ffn_w3")


def _local_step(x, target, full, sp_, distributed):
    d = x.shape[1]
    (u,), got = _inproj(x, sp_["norm_mix_g"], full["w_in"],
                        [_ffn_gather_hosted([full["w_out"]])] if distributed else None)
    w_out = (got[0][0] if distributed else full["w_out"]).reshape(d, d)
    gather = [_ffn_gather_hosted([full[n] for n in _FFN])] if distributed else None
    (h, yn, hres1, saved, pooled), got = _mixer_fwd(u, x, sp_, w_out, gather)
    w1, w3, w2 = got[0] if distributed else [full[n] for n in _FFN]
    h2, a1, a3, ff = _ffn_up(hres1, sp_["norm_ffn_g"], w1, w3)
    dh, dhb, loss8, dg_fin = _ffn_down(ff, hres1, target, sp_["final_norm_g"], w2)
    da1, da3 = _ffn_bwd_gate(dhb, a1, a3, w2)
    dws = list(_ffn_wgrad(h2, dhb, ff, da1, da3))
    rs1 = [_rs_sibling_hosted(dws)] if distributed else None
    (dhres1, dg_ffn), got = _ffn_bwd_down(da1, da3, dh, hres1, sp_["norm_ffn_g"], w1, w3, rs1)
    rs2 = None
    if distributed:
        pairs = [_add_own_half(a, r, "add_half_" + n) for n, a, r in zip(_FFN, dws, got[0])]
        rs2 = [_rs_chips_hosted([pb for _, pb in pairs])]
    (du, mix_slab), got = _mixer_bwd(u, saved, pooled, h, dhres1, sp_, w_out, rs2)
    g_mix = sp_["norm_mix_g"]
    if distributed:
        fin = [_sum_chips(pairs[k][0], got[0][k], "sum_chips_" + n) for k, n in enumerate(_FFN)]
        swap = _rs_swap_hosted(fin)
        state, token = _split_start(swap, "ffn_swap_start")
        g_mix = g_mix + token[0:1, 0:1]
    (gx, dwin, dwout, dg_mix), _ = _inproj_bwd(x, du, dhres1, yn, g_mix, full["w_in"])
    if distributed:
        sib = _split_wait(swap, state, dg_mix, "ffn_swap_wait")
    big = {"w_in": dwin, "w_out": dwout.reshape(N_CHIPS, d // N_CHIPS, d)}
    for k, n in enumerate(_FFN):
        big[n] = (fin[k], sib[k]) if distributed else dws[k]
    return gx, big, (mix_slab, dg_mix, dg_ffn, dg_fin, loss8)


_SMALL_LAYOUT = {
    "gate_a_w": (lambda a: a[0], lambda a: a[None]),
    "gate_x_w": (lambda a: a[0], lambda a: a[None]),
    "pool_w": (lambda a: a[0], lambda a: a[None]),
    "conv_w": (lambda a: a[0], lambda a: a[None]),
    "final_norm_g": (lambda a: a[None], lambda a: a[0]),
}

_WEIGHTS = ("norm_mix_g", "w_in", "conv_w", "conv_b", "gate_a_w", "gate_a_b", "gate_x_w", "gate_x_b", "lru_lambda",
            "pool_w", "pool_b", "pool_scale", "norm_lru_g", "norm_pool_g", "w_out", "norm_ffn_g", "ffn_w1",
            "ffn_w3", "ffn_w2", "final_norm_g")


def kernel(x, norm_mix_g, w_in, conv_w, conv_b, gate_a_w, gate_a_b, gate_x_w, gate_x_b, lru_lambda, pool_w, pool_b, pool_scale, norm_lru_g, norm_pool_g, w_out, norm_ffn_g, ffn_w1, ffn_w3, ffn_w2, final_norm_g, loss_target, m_norm_mix_g, m_w_in, m_conv_w, m_conv_b, m_gate_a_w, m_gate_a_b, m_gate_x_w, m_gate_x_b, m_lru_lambda, m_pool_w, m_pool_b, m_pool_scale, m_norm_lru_g, m_norm_pool_g, m_w_out, m_norm_ffn_g, m_ffn_w1, m_ffn_w3, m_ffn_w2, m_final_norm_g, v_norm_mix_g, v_w_in, v_conv_w, v_conv_b, v_gate_a_w, v_gate_a_b, v_gate_x_w, v_gate_x_b, v_lru_lambda, v_pool_w, v_pool_b, v_pool_scale, v_norm_lru_g, v_norm_pool_g, v_w_out, v_norm_ffn_g, v_ffn_w1, v_ffn_w3, v_ffn_w2, v_final_norm_g):
    loc = locals()
    w = {n: loc[n] for n in _WEIGHTS}
    m = {n: loc["m_" + n] for n in _WEIGHTS}
    v = {n: loc["v_" + n] for n in _WEIGHTS}

    def lay(nme, a):
        return _SMALL_LAYOUT[nme][0](a) if nme in _SMALL_LAYOUT else a

    def unlay(nme, a):
        return _SMALL_LAYOUT[nme][1](a) if nme in _SMALL_LAYOUT else a

    for group in (w, m, v):
        for n in _TRANSPOSED:
            group[n] = jnp.transpose(group[n], (0, 2, 1))

    gathered = _gather_weights([w[n][0] for n in _BIG], w["conv_w"][0], n_remote=1)
    full = dict(zip(_BIG, gathered[:-1]))
    cw_all = gathered[-1]
    sp_ = {n: lay(n, w[n]) for n in _SMALL_ORDER}
    sp_["conv_w"] = jnp.transpose(cw_all[:, :CONV_WIDTH, :], (1, 0, 2)).reshape(CONV_WIDTH, N_CHIPS * LANES)

    gx, big, small = _local_step(x[0], loss_target[0], full, sp_, distributed=True)

    late = ("w_in", "w_out", "slab")
    big["slab"] = _build_slab(*small)
    fin = {n: big[n][0] for n in _FFN}
    sib = {n: big[n][1] for n in _FFN}
    recv1, = _run_comm([_rs_sibling_hosted([big[n] for n in late])], "tail_sibling")
    pairs = [_add_own_half(big[n], r, "add_half_" + n, F32 if n == "slab" else BF16) for n, r in zip(late, recv1)]
    chips = _rs_chips_hosted([pb for _, pb in pairs])
    state, token = _split_start(chips, "tail_chips_start")
    out = {}
    for n in _FFN:
        out[n] = tuple(_adamw_big(w[n], fin[n], sib[n], m[n], v[n], "adamw_" + n, token))
    recv2 = _split_wait(chips, state, out[_FFN[-1]][1], "tail_chips_wait")
    for n, (p, _), r in zip(late, pairs, recv2):
        fin[n] = _sum_chips(p, r, "sum_chips_" + n)
    swapped, = _run_comm([_rs_swap_hosted([fin[n] for n in late])], "tail_swap")
    sib.update(zip(late, swapped))
    for n in late[:2]:
        out[n] = tuple(_adamw_big(w[n], fin[n], sib[n], m[n], v[n], "adamw_" + n))
    for n in _TRANSPOSED:
        out[n] = tuple(jnp.transpose(a, (0, 2, 1)) for a in out[n])
    wmv = {n: (lay(n, w[n]), lay(n, m[n]), lay(n, v[n])) for n in _SMALL_ORDER}
    res, loss = _adamw_small(fin["slab"], sib["slab"], wmv)
    for n in _SMALL_ORDER:
        out[n] = tuple(unlay(n, a) for a in res[n])
    return (loss[0, 0], gx[None]) + tuple(out[n][k] for k in range(4) for n in _WEIGHTS)
```

```python
import functools
import math

import jax
import jax.numpy as jnp
from jax import lax
from jax.experimental import pallas as pl
from jax.experimental.pallas import tpu as pltpu

F32 = jnp.float32
BF16 = jnp.bfloat16
SDS = jax.ShapeDtypeStruct
MESH = pl.DeviceIdType.MESH

EPS = 1e-6
LRU_C = 8.0
CONV_WIDTH = 4
POOL_WINDOWS = (2, 4, 8, 16)
HALO = 16
LANES = 128
SUBLANES = 8
GATE_BLOCK = 256
N_CHIPS = 4

ADAM_LR = 0.001
ADAM_B1 = 0.9
ADAM_B2 = 0.999
ADAM_EPS = 1e-08
ADAM_WD = 0.01
ADAM_STEP = 10

TM_PROJ = 512
TM_MIX = 512
TM_FFN = 512
TM_FFN_BWD = 256
TM_WGRAD = 2048
MIX_SAVED = ("xc", "r", "ig", "a", "m2raw", "ge", "dge")
FFN_ROW_CHUNKS = 2
VMEM_LIMIT = 56 * 1024 * 1024

SLAB_W = 512
ROW_CONV_B, ROW_CONV_W, ROW_BA, ROW_BX, ROW_LAM, ROW_PB, ROW_PS, ROW_GL, ROW_GP = 0, 1, 5, 6, 7, 8, 9, 10, 11
ROW_GA, ROW_GX, ROW_PW = 16, 80, 144
ROW_MIX, ROW_FFN, ROW_FIN, ROW_LOSS = 272, 274, 276, 278
MIX_SLAB_ROWS = 272
SLAB_ROWS = 288


def _cp(sem=None, **kw):
    if sem is not None:
        kw["dimension_semantics"] = sem
    return pltpu.CompilerParams(vmem_limit_bytes=VMEM_LIMIT, **kw)


def _const_spec(shape):
    nd = len(shape)
    return pl.BlockSpec(shape, lambda *_: (0,) * nd, pipeline_mode=pl.Buffered(1))


def _sigmoid(x):
    return 1.0 / (1.0 + jnp.exp(-x))


def _dot(a, b):
    return jnp.dot(a, b, preferred_element_type=F32)


def _dot_nt(a, b):
    return lax.dot_general(a, b, (((1,), (1,)), ((), ())), preferred_element_type=F32)


def _dot_tn(a, b):
    return lax.dot_general(a, b, (((0,), (0,)), ((), ())), preferred_element_type=F32)


def _colsum8(v):
    m, c = v.shape
    return v.reshape(m // SUBLANES, SUBLANES, c).sum(axis=0)


def _rowmean(v):
    return jnp.mean(v, axis=-1, keepdims=True)


def _rms_bwd(dy, xhat, r, g):
    dxh = dy * g
    return r * (dxh - xhat * _rowmean(dxh * xhat))


def _softplus_neg(lam):
    z = -lam
    e = jnp.exp(-jnp.abs(z))
    u = 1.0 + e
    d = u - 1.0
    log1p = jnp.where(d == 0.0, e, jnp.log(u) * (e / jnp.where(d == 0.0, 1.0, d)))
    return jnp.maximum(z, 0.0) + log1p


def _neg_expm1(z):
    series = -(z * (1.0 + z * (0.5 + z * (1.0 / 6.0 + z * (1.0 / 24.0)))))
    return jnp.where(z > -0.03, series, 1.0 - jnp.exp(z))


_GELU_C = math.sqrt(2.0 / math.pi)
_GELU_K = 0.044715


def _gelu_parts(x):
    x2 = x * x
    th = jnp.tanh(_GELU_C * (x + _GELU_K * x2 * x))
    ge = 0.5 * x * (1.0 + th)
    dge = 0.5 * (1.0 + th) + 0.5 * x * (1.0 - th * th) * (_GELU_C * (1.0 + 3.0 * _GELU_K * x2))
    return ge, dge


def _shift_down(halo, tile, k):
    if k == 0:
        return tile
    ext = jnp.concatenate([halo, tile], axis=0)
    n = tile.shape[0]
    h = halo.shape[0]
    return ext[h - k:h - k + n]


def _shift_up(tile, nxt, k):
    if k == 0:
        return tile
    ext = jnp.concatenate([tile, nxt], axis=0)
    return ext[k:k + tile.shape[0]]


def _build_gate_blocks(ga_ref, gx_ref, gw_ref):
    hd = ga_ref.shape[1]
    per = GATE_BLOCK // hd
    zero = jnp.zeros((hd, hd), F32)
    for b in range(gw_ref.shape[0]):
        for src, off in ((ga_ref, 0), (gx_ref, GATE_BLOCK)):
            for hh in range(per):
                row = jnp.concatenate([zero] * hh + [src[b * per + hh]] + [zero] * (per - 1 - hh), axis=1)
                gw_ref[b, hh * hd:(hh + 1) * hd, off:off + GATE_BLOCK] = row.astype(BF16)


def _scan_level1(a, b, reverse):
    m, c = a.shape
    a3 = a.reshape(m // SUBLANES, SUBLANES, c)
    b3 = b.reshape(m // SUBLANES, SUBLANES, c)
    row = lax.broadcasted_iota(jnp.int32, a3.shape, 1)
    for s in (1, 2, 4):
        sh = (SUBLANES - s) if reverse else s
        a_sh = pltpu.roll(a3, sh, 1)
        b_sh = pltpu.roll(b3, sh, 1)
        ok = (row < SUBLANES - s) if reverse else (row >= s)
        b3 = jnp.where(ok, a3 * b_sh + b3, b3)
        a3 = jnp.where(ok, a3 * a_sh, a3)
    return a3.reshape(m, c), b3.reshape(m, c)


def _scan_level2(a_ref, b_ref, out_ref, carry, reverse):
    m, c = a_ref.shape
    ng = m // SUBLANES

    def step(g, cr):
        gi = (ng - 1 - g) if reverse else g
        off = pl.multiple_of(gi * SUBLANES, SUBLANES)
        h = b_ref[pl.ds(off, SUBLANES), :] + a_ref[pl.ds(off, SUBLANES), :] * cr
        out_ref[pl.ds(off, SUBLANES), :] = h
        edge = h[0:1, :] if reverse else h[SUBLANES - 1:SUBLANES, :]
        return jnp.broadcast_to(edge, (SUBLANES, c))

    return lax.fori_loop(0, ng, step, carry, unroll=4)


def _mixer_recompute(u, hal, t0, cw, cb, gw_ref, ba, bx, lam, pw_ref, pb, ps):
    tm = u.shape[0]
    lw = cb.shape[1]
    u_l, u_g, u_p = u[:, :lw], u[:, lw:2 * lw], u[:, 2 * lw:]
    hal_l, hal_p = hal[:, :lw], hal[:, 2 * lw:]
    taps = [_shift_down(hal_l, u_l, CONV_WIDTH - 1 - k) for k in range(CONV_WIDTH)]
    xc = cb
    for k in range(CONV_WIDTH):
        xc = xc + taps[k] * cw[k:k + 1, :]
    xcb = xc.astype(BF16)
    nb = lw // GATE_BLOCK
    gs = [_dot(xcb[:, b * GATE_BLOCK:(b + 1) * GATE_BLOCK], gw_ref[b]) for b in range(nb)]
    r = _sigmoid(jnp.concatenate([g[:, :GATE_BLOCK] for g in gs], axis=1) + ba)
    ig = _sigmoid(jnp.concatenate([g[:, GATE_BLOCK:] for g in gs], axis=1) + bx)
    sp = _softplus_neg(lam)
    la = (-LRU_C * r) * sp
    a = jnp.exp(la)
    m2raw = _neg_expm1(2.0 * la)
    mult = jnp.sqrt(jnp.maximum(m2raw, 1e-12))
    ge, dge = _gelu_parts(u_g)
    row = lax.broadcasted_iota(jnp.int32, (tm, LANES), 0) + t0
    pooled, invs, zs = [], [], []
    for gi, w in enumerate(POOL_WINDOWS):
        e = jnp.concatenate([hal_p[:, gi * LANES:(gi + 1) * LANES], u_p[:, gi * LANES:(gi + 1) * LANES]], axis=0)
        s = e
        k = 1
        while k < w:
            s = s + pltpu.roll(s, k, 0)
            k *= 2
        inv = 1.0 / jnp.minimum(row + 1, w).astype(F32)
        pg = s[HALO:] * inv - e[HALO:]
        pooled.append(pg)
        invs.append(inv)
        zs.append(_dot(pg.astype(BF16), pw_ref[gi].astype(BF16)))
    z = jnp.concatenate(zs, axis=1) + pb
    y_pool = z * ps
    return dict(u_l=u_l, u_g=u_g, taps=taps, xc=xc, xcb=xcb, r=r, ig=ig, sp=sp, la=la, a=a, m2raw=m2raw,
                mult=mult, ge=ge, dge=dge, pooled=pooled, invs=invs, z=z, y_pool=y_pool)


ANY = pl.BlockSpec(memory_space=pl.ANY)
VMEM_SPEC = pl.BlockSpec(memory_space=pltpu.VMEM)


class _Hosted:
    def __init__(self, ins, out_shapes, sems, start, finish, mid=None, aliases=None):
        self.ins, self.out_shapes, self.sems = list(ins), list(out_shapes), list(sems)
        self.start, self.mid, self.finish = start, mid, finish
        self.aliases = dict(aliases or {})


def _call(body, hosted, stage_preds, *, name, grid, in_specs, out_specs, out_shape, scratch_shapes, args, sem):
    hosted = list(hosted or [])
    n_in, n_out, n_scr = len(in_specs), len(out_specs), len(scratch_shapes)
    c_in = [a for h in hosted for a in h.ins]
    c_out = [o for h in hosted for o in h.out_shapes]
    c_sem = [pltpu.SemaphoreType.DMA((k,)) for h in hosted for k in h.sems]

    def full(*refs):
        p = 0
        parts = []
        for cnt in (n_in, len(c_in), n_out, len(c_out), n_scr, len(c_sem)):
            parts.append(refs[p:p + cnt])
            p += cnt
        hi, ci, ho, co, hs, cs = parts
        per = []
        a = b = c_ = 0
        for h in hosted:
            per.append((h, ci[a:a + len(h.ins)], co[b:b + len(h.out_shapes)], cs[c_:c_ + len(h.sems)]))
            a, b, c_ = a + len(h.ins), b + len(h.out_shapes), c_ + len(h.sems)
        first = mid = last = None
        if hosted and grid:
            first, mid, last = stage_preds()

        def run(fn, pred, i_, o_, s_):
            if fn is None:
                return
            if pred is None:
                fn(i_, o_, s_)
            else:
                pl.when(pred)(functools.partial(fn, i_, o_, s_))

        for h, i_, o_, s_ in per:
            run(h.start, first, i_, o_, s_)
        body(*hi, *ho, *hs)
        for h, i_, o_, s_ in per:
            run(h.mid, mid, i_, o_, s_)
        for h, i_, o_, s_ in per:
            run(h.finish, last, i_, o_, s_)

    aliases = {}
    a = b = 0
    for h in hosted:
        for k, v in h.aliases.items():
            aliases[n_in + a + k] = n_out + b + v
        a, b = a + len(h.ins), b + len(h.out_shapes)
    res = pl.pallas_call(
        full, name=name, grid=grid, in_specs=list(in_specs) + [ANY] * len(c_in),
        out_specs=list(out_specs) + [ANY] * len(c_out), out_shape=list(out_shape) + c_out,
        scratch_shapes=list(scratch_shapes) + c_sem, input_output_aliases=aliases,
        compiler_params=_cp(sem))(*args, *c_in)
    res = list(res)
    outs = []
    p = n_out
    for h in hosted:
        outs.append(res[p:p + len(h.out_shapes)])
        p += len(h.out_shapes)
    return res[:n_out], outs


def _inproj(x, g_mix, w_in, hosted=None):
    s, d = x.shape
    n = w_in.shape[1]
    tm = min(TM_PROJ, s)
    nt = s // tm

    def body(x_ref, g_ref, w_ref, u_ref):
        xv = x_ref[...]
        r = lax.rsqrt(_rowmean(xv * xv) + EPS)
        u_ref[...] = _dot((xv * r * g_ref[...]).astype(BF16), w_ref[...])

    def stages():
        i = pl.program_id(0)
        return i == 0, i == max(nt - 3, 0), i == nt - 1

    return _call(
        body, hosted, stages, grid=(nt,), name="inproj",
        in_specs=[pl.BlockSpec((tm, d), lambda i: (i, 0)), _const_spec((1, d)), _const_spec((d, n))],
        out_specs=[pl.BlockSpec((tm, n), lambda i: (i, 0))], out_shape=[SDS((s, n), F32)], scratch_shapes=[],
        args=(x, g_mix, w_in), sem=("arbitrary",))


def _mixer_fwd(u, x, sp_, w_out, hosted=None):
    s, din = u.shape
    d = x.shape[1]
    lw = din // 3
    tm = min(TM_MIX, s)
    nb = lw // GATE_BLOCK

    def body(u_ref, halo_ref, x_ref, cw_ref, cb_ref, ga_ref, gx_ref, ba_ref, bx_ref, lam_ref, pw_ref, pb_ref,
             ps_ref, gl_ref, gp_ref, wout_ref, h_ref, yn_ref, hres_ref, saved_ref, pooled_ref,
             gw_s, a_s, b_s, carry_s):
        i = pl.program_id(0)

        @pl.when(i == 0)
        def _():
            _build_gate_blocks(ga_ref, gx_ref, gw_s)
            carry_s[...] = jnp.zeros_like(carry_s)

        uv = u_ref[...]
        hal = jnp.where(i > 0, halo_ref[...], 0.0)
        f = _mixer_recompute(uv, hal, i * tm, cw_ref[...], cb_ref[...], gw_s, ba_ref[...], bx_ref[...],
                             lam_ref[...], pw_ref, pb_ref[...], ps_ref[...])
        for k, name in enumerate(MIX_SAVED):
            saved_ref[k] = f[name]
        pooled_ref[...] = jnp.concatenate(f["pooled"], axis=1).astype(BF16)
        bb = f["mult"] * (f["ig"] * f["xc"])
        a1, b1 = _scan_level1(f["a"], bb, reverse=False)
        a_s[...] = a1
        b_s[...] = b1
        carry_s[...] = _scan_level2(a_s, b_s, h_ref, carry_s[...], reverse=False)
        y_lru = h_ref[...] * f["ge"]
        rl = lax.rsqrt(_rowmean(y_lru * y_lru) + EPS)
        yp = f["y_pool"]
        rp = lax.rsqrt(_rowmean(yp * yp) + EPS)
        yn = jnp.concatenate([y_lru * rl * gl_ref[...], yp * rp * gp_ref[...]], axis=1).astype(BF16)
        yn_ref[...] = yn
        hres_ref[...] = x_ref[...] + _dot(yn, wout_ref[...])

    small = [sp_[k] for k in ("conv_w", "conv_b", "gate_a_w", "gate_x_w", "gate_a_b", "gate_x_b", "lru_lambda",
                              "pool_w", "pool_b", "pool_scale", "norm_lru_g", "norm_pool_g")]
    nt = s // tm

    def stages():
        i = pl.program_id(0)
        return i == 0, i == max(nt - 3, 0), i == nt - 1

    return _call(
        body, hosted, stages, grid=(nt,), name="mixer_fwd",
        in_specs=[pl.BlockSpec((tm, din), lambda i: (i, 0)),
                  pl.BlockSpec((HALO, din), lambda i: (jnp.maximum(i * (tm // HALO) - 1, 0), 0)),
                  pl.BlockSpec((tm, d), lambda i: (i, 0))]
        + [_const_spec(a.shape) for a in small] + [_const_spec(w_out.shape)],
        out_specs=[pl.BlockSpec((tm, lw), lambda i: (i, 0)), pl.BlockSpec((tm, d), lambda i: (i, 0)),
                   pl.BlockSpec((tm, d), lambda i: (i, 0)),
                   pl.BlockSpec((len(MIX_SAVED), tm, lw), lambda i: (0, i, 0)),
                   pl.BlockSpec((tm, lw), lambda i: (i, 0))],
        out_shape=[SDS((s, lw), F32), SDS((s, d), BF16), SDS((s, d), F32), SDS((len(MIX_SAVED), s, lw), F32),
                   SDS((s, lw), BF16)],
        scratch_shapes=[pltpu.VMEM((nb, GATE_BLOCK, 2 * GATE_BLOCK), BF16), pltpu.VMEM((tm, lw), F32),
                        pltpu.VMEM((tm, lw), F32), pltpu.VMEM((SUBLANES, lw), F32)],
        args=(u, u, x, *small, w_out), sem=("arbitrary",))


def _row_chunks(tm):
    rc = tm // FFN_ROW_CHUNKS
    return [slice(q * rc, (q + 1) * rc) for q in range(FFN_ROW_CHUNKS)]


def _ffn_up(hres1, g_ffn, w1, w3):
    s, d = hres1.shape
    nj, fc, _ = w1.shape
    tm = min(TM_FFN, s)

    def body(h_ref, gf_ref, w1_ref, w3_ref, h2_ref, a1_ref, a3_ref, ff_ref):
        hv = h_ref[...]
        r = lax.rsqrt(_rowmean(hv * hv) + EPS)
        h2_ref[...] = (hv * r * gf_ref[...]).astype(BF16)
        h2 = h2_ref[...]
        for j in range(nj):
            a1 = _dot_nt(h2, w1_ref[j])
            a3 = _dot_nt(h2, w3_ref[j])
            a1_ref[j] = a1.astype(BF16)
            a3_ref[j] = a3.astype(BF16)
            ff_ref[j] = ((a1 * _sigmoid(a1)) * a3).astype(BF16)

    wspec = _const_spec(w1.shape)
    aspec = pl.BlockSpec((nj, tm, fc), lambda i: (0, i, 0))
    return pl.pallas_call(
        body, grid=(s // tm,), name="ffn_up",
        in_specs=[pl.BlockSpec((tm, d), lambda i: (i, 0)), _const_spec((1, d)), wspec, wspec],
        out_specs=[pl.BlockSpec((tm, d), lambda i: (i, 0)), aspec, aspec, aspec],
        out_shape=[SDS((s, d), BF16)] + [SDS((nj, s, fc), BF16)] * 3,
        compiler_params=_cp(("parallel",)))(hres1, g_ffn, w1, w3)


def _ffn_down(ff, hres1, target, g_fin, w2):
    s, d = hres1.shape
    nj, _, fc = ff.shape
    tm = min(TM_FFN, s)

    def body(ff_ref, h_ref, t_ref, gn_ref, w2_ref, dh_ref, dhb_ref, loss_ref, dgn_ref):
        @pl.when(pl.program_id(0) == 0)
        def _():
            loss_ref[...] = jnp.zeros_like(loss_ref)
            dgn_ref[...] = jnp.zeros_like(dgn_ref)

        gn = gn_ref[...]
        for rows in _row_chunks(tm):
            acc = _dot(ff_ref[0, rows, :], w2_ref[0])
            for j in range(1, nj):
                acc = acc + _dot(ff_ref[j, rows, :], w2_ref[j])
            hr2 = h_ref[rows, :] + acc
            r2 = lax.rsqrt(_rowmean(hr2 * hr2) + EPS)
            xh = hr2 * r2
            diff = xh * gn - t_ref[rows, :]
            tot = jnp.sum(jnp.sum(diff * diff, axis=1, keepdims=True), axis=0, keepdims=True)
            loss_ref[...] += tot * (0.5 / d)
            dout = diff * (1.0 / d)
            dgn_ref[...] += _colsum8(dout * xh)
            dh = _rms_bwd(dout, xh, r2, gn)
            dh_ref[rows, :] = dh
            dhb_ref[rows, :] = dh.astype(BF16)

    tile = pl.BlockSpec((tm, d), lambda i: (i, 0))
    return pl.pallas_call(
        body, grid=(s // tm,), name="ffn_down",
        in_specs=[pl.BlockSpec((nj, tm, fc), lambda i: (0, i, 0)), tile, tile, _const_spec((1, d)),
                  _const_spec(w2.shape)],
        out_specs=[tile, tile, pl.BlockSpec((SUBLANES, LANES), lambda i: (0, 0)),
                   pl.BlockSpec((SUBLANES, d), lambda i: (0, 0))],
        out_shape=[SDS((s, d), F32), SDS((s, d), BF16), SDS((SUBLANES, LANES), F32), SDS((SUBLANES, d), F32)],
        compiler_params=_cp(("arbitrary",)))(ff, hres1, target, g_fin, w2)


def _ffn_bwd_gate(dhb, a1, a3, w2):
    s, d = dhb.shape
    nj, _, fc = a1.shape
    tm = min(TM_FFN, s)

    def body(dhb_ref, a1_ref, a3_ref, w2_ref, da1_ref, da3_ref):
        for j in range(nj):
            for rows in _row_chunks(tm):
                dff = _dot_nt(dhb_ref[rows, :], w2_ref[j])
                a1v = a1_ref[j, rows, :].astype(F32)
                sg = _sigmoid(a1v)
                silu = a1v * sg
                da1_ref[j, rows, :] = (dff * a3_ref[j, rows, :].astype(F32)
                                       * (sg * (1.0 + (a1v - silu)))).astype(BF16)
                da3_ref[j, rows, :] = (dff * silu).astype(BF16)

    aspec = pl.BlockSpec((nj, tm, fc), lambda i: (0, i, 0))
    return pl.pallas_call(
        body, grid=(s // tm,), name="ffn_bwd_gate",
        in_specs=[pl.BlockSpec((tm, d), lambda i: (i, 0)), aspec, aspec, _const_spec(w2.shape)],
        out_specs=[aspec, aspec], out_shape=[SDS((nj, s, fc), BF16)] * 2,
        compiler_params=_cp(("parallel",)))(dhb, a1, a3, w2)


def _ffn_bwd_down(da1, da3, dh, hres1, g_ffn, w1, w3, hosted=None):
    s, d = hres1.shape
    nj, _, fc = da1.shape
    tm = min(TM_FFN, s)
    nt = s // tm

    def body(da1_ref, da3_ref, dh_ref, h_ref, gf_ref, w1_ref, w3_ref, dhr_ref, dgf_ref):
        @pl.when(pl.program_id(0) == 0)
        def _():
            dgf_ref[...] = jnp.zeros_like(dgf_ref)

        gf = gf_ref[...]
        for rows in _row_chunks(tm):
            dh2 = None
            for j in range(nj):
                part = _dot(da1_ref[j, rows, :], w1_ref[j]) + _dot(da3_ref[j, rows, :], w3_ref[j])
                dh2 = part if dh2 is None else dh2 + part
            hv = h_ref[rows, :]
            r = lax.rsqrt(_rowmean(hv * hv) + EPS)
            xh = hv * r
            dgf_ref[...] += _colsum8(dh2 * xh)
            dhr_ref[rows, :] = dh_ref[rows, :] + _rms_bwd(dh2, xh, r, gf)

    tile = pl.BlockSpec((tm, d), lambda i: (i, 0))
    aspec = pl.BlockSpec((nj, tm, fc), lambda i: (0, i, 0))
    wspec = _const_spec(w1.shape)

    def stages():
        i = pl.program_id(0)
        return i == 0, i == max(nt - 2, 0), i == nt - 1

    return _call(
        body, hosted, stages, grid=(nt,), name="ffn_bwd_down",
        in_specs=[aspec, aspec, tile, tile, _const_spec((1, d)), wspec, wspec],
        out_specs=[tile, pl.BlockSpec((SUBLANES, d), lambda i: (0, 0))],
        out_shape=[SDS((s, d), F32), SDS((SUBLANES, d), F32)],
        scratch_shapes=[], args=(da1, da3, dh, hres1, g_ffn, w1, w3), sem=("arbitrary",))


def _ffn_bwd(dhb, dh, a1, a3, hres1, g_ffn, w1, w3, w2):
    s, d = hres1.shape
    nj, _, fc = a1.shape
    tm = min(TM_FFN_BWD, s)

    def body(dhb_ref, dh_ref, a1_ref, a3_ref, h_ref, gf_ref, w1_ref, w3_ref, w2_ref,
             da1_ref, da3_ref, dhr_ref, dgf_ref):
        @pl.when(pl.program_id(0) == 0)
        def _():
            dgf_ref[...] = jnp.zeros_like(dgf_ref)

        dhb_v = dhb_ref[...]
        dh2 = None
        for j in range(nj):
            dff = _dot_nt(dhb_v, w2_ref[j])
            a1v = a1_ref[j].astype(F32)
            sg = _sigmoid(a1v)
            silu = a1v * sg
            da1 = (dff * a3_ref[j].astype(F32) * (sg * (1.0 + (a1v - silu)))).astype(BF16)
            da3 = (dff * silu).astype(BF16)
            da1_ref[j] = da1
            da3_ref[j] = da3
            part = _dot(da1, w1_ref[j]) + _dot(da3, w3_ref[j])
            dh2 = part if dh2 is None else dh2 + part
        hv = h_ref[...]
        r = lax.rsqrt(_rowmean(hv * hv) + EPS)
        xh = hv * r
        dgf_ref[...] += _colsum8(dh2 * xh)
        dhr_ref[...] = dh_ref[...] + _rms_bwd(dh2, xh, r, gf_ref[...])

    tile = pl.BlockSpec((tm, d), lambda i: (i, 0))
    aspec = pl.BlockSpec((nj, tm, fc), lambda i: (0, i, 0))
    wspec = _const_spec(w1.shape)
    return pl.pallas_call(
        body, grid=(s // tm,), name="ffn_bwd",
        in_specs=[tile, tile, aspec, aspec, tile, _const_spec((1, d)), wspec, wspec, wspec],
        out_specs=[aspec, aspec, tile, pl.BlockSpec((SUBLANES, d), lambda i: (0, 0))],
        out_shape=[SDS((nj, s, fc), BF16)] * 2 + [SDS((s, d), F32), SDS((SUBLANES, d), F32)],
        compiler_params=_cp(("arbitrary",)))(dhb, dh, a1, a3, hres1, g_ffn, w1, w3, w2)


def _ffn_wgrad13(h2, da1, da3):
    s, d = h2.shape
    _, _, fc = da1.shape
    tm = min(TM_WGRAD, s)

    def body(h2_ref, da1_ref, da3_ref, dw1_ref, dw3_ref):
        @pl.when(pl.program_id(1) == 0)
        def _():
            dw1_ref[...] = jnp.zeros_like(dw1_ref)
            dw3_ref[...] = jnp.zeros_like(dw3_ref)

        h2v = h2_ref[...]
        dw1_ref[0] += _dot_tn(da1_ref[0], h2v)
        dw3_ref[0] += _dot_tn(da3_ref[0], h2v)

    wspec = pl.BlockSpec((1, fc, d), lambda j, i: (j, 0, 0))
    return pl.pallas_call(
        body, grid=(N_CHIPS, s // tm), name="ffn_wgrad13",
        in_specs=[pl.BlockSpec((tm, d), lambda j, i: (i, 0))] + [pl.BlockSpec((1, tm, fc), lambda j, i: (j, i, 0))] * 2,
        out_specs=[wspec] * 2, out_shape=[SDS((N_CHIPS, fc, d), F32)] * 2,
        compiler_params=_cp(("parallel", "arbitrary")))(h2, da1, da3)


def _ffn_wgrad2(ff, dhb, hosted=None):
    nj, s, fc = ff.shape
    d = dhb.shape[1]
    tm = min(TM_WGRAD, s)
    nt = s // tm

    def body(ff_ref, dhb_ref, dw2_ref):
        @pl.when(pl.program_id(1) == 0)
        def _():
            dw2_ref[...] = jnp.zeros_like(dw2_ref)

        dw2_ref[0] += _dot_tn(ff_ref[0], dhb_ref[...])

    def stages():
        j, i = pl.program_id(0), pl.program_id(1)
        return (j == 0) & (i == 0), None, (j == nj - 1) & (i == nt - 1)

    return _call(
        body, hosted, stages, grid=(nj, nt), name="ffn_wgrad2",
        in_specs=[pl.BlockSpec((1, tm, fc), lambda j, i: (j, i, 0)), pl.BlockSpec((tm, d), lambda j, i: (i, 0))],
        out_specs=[pl.BlockSpec((1, fc, d), lambda j, i: (j, 0, 0))], out_shape=[SDS((nj, fc, d), F32)],
        scratch_shapes=[], args=(ff, dhb), sem=("arbitrary", "arbitrary"))


def _mixer_bwd(u, saved, pooled, h, dhres1, sp_, w_out, hosted=None):
    s, din = u.shape
    d = dhres1.shape[1]
    lw = din // 3
    tm = min(TM_MIX, s)
    nt = s // tm
    nb = lw // GATE_BLOCK
    hd = sp_["gate_a_w"].shape[1]

    def body(ul_ref, saved_ref, pooled_ref, h_ref, hhalo_ref, dhr_ref, cw_ref, cb_ref, ga_ref, gx_ref, ba_ref,
             bx_ref, lam_ref, pw_ref, pb_ref, ps_ref, gl_ref, gp_ref, wout_ref, du_ref, slab_ref,
             gw_s, a_s, b_s, e_s, ecarry_s, dxc_s, q_s, vec_s, cwacc_s, dgw_s, dpw_s):
        i = pl.program_id(0)
        tile = nt - 1 - i

        @pl.when(i == 0)
        def _():
            _build_gate_blocks(ga_ref, gx_ref, gw_s)
            for ref in (ecarry_s, dxc_s, q_s, vec_s, cwacc_s, dgw_s, dpw_s):
                ref[...] = jnp.zeros_like(ref)

        cw = cw_ref[...]
        lam = lam_ref[...]
        ps = ps_ref[...]
        f = {name: saved_ref[k] for k, name in enumerate(MIX_SAVED)}
        f["mult"] = jnp.sqrt(jnp.maximum(f["m2raw"], 1e-12))
        f["sp"] = _softplus_neg(lam)
        f["xcb"] = f["xc"].astype(BF16)
        pooled = pooled_ref[...]
        row = lax.broadcasted_iota(jnp.int32, (tm, LANES), 0) + tile * tm
        f["invs"] = [1.0 / jnp.minimum(row + 1, w).astype(F32) for w in POOL_WINDOWS]
        f["z"] = jnp.concatenate(
            [_dot(pooled[:, g * LANES:(g + 1) * LANES], pw_ref[g].astype(BF16))
             for g in range(len(POOL_WINDOWS))], axis=1) + pb_ref[...]
        f["y_pool"] = f["z"] * ps
        u_l = ul_ref[...]
        hv = h_ref[...]
        h_prev = _shift_down(jnp.where(tile > 0, hhalo_ref[...], 0.0), hv, 1)
        y_lru = hv * f["ge"]
        rl = lax.rsqrt(_rowmean(y_lru * y_lru) + EPS)
        yp = f["y_pool"]
        rp = lax.rsqrt(_rowmean(yp * yp) + EPS)
        xh_l = y_lru * rl
        xh_p = yp * rp

        dyn = _dot_nt(dhr_ref[...].astype(BF16), wout_ref[...])
        d_nl, d_np = dyn[:, :lw], dyn[:, lw:]
        vec = {}
        vec[ROW_GL] = _colsum8(d_nl * xh_l)
        vec[ROW_GP] = _colsum8(d_np * xh_p)
        d_ylru = _rms_bwd(d_nl, xh_l, rl, gl_ref[...])
        d_ypool = _rms_bwd(d_np, xh_p, rp, gp_ref[...])

        vec[ROW_PS] = _colsum8(d_ypool * f["z"])
        dz = d_ypool * ps
        vec[ROW_PB] = _colsum8(dz)
        dzb = dz.astype(BF16)
        dup = []
        for gi, w in enumerate(POOL_WINDOWS):
            sl = slice(gi * LANES, (gi + 1) * LANES)
            dpw_s[:, sl] += _dot_tn(pooled[:, sl], dzb[:, sl])
            dpool = _dot_nt(dzb[:, sl], pw_ref[gi].astype(BF16))
            q = dpool * f["invs"][gi]
            e = jnp.concatenate([q, q_s[:, sl]], axis=0)
            k = 1
            while k < w:
                e = e + pltpu.roll(e, tm + HALO - k, 0)
                k *= 2
            dup.append(e[:tm] - dpool)
            q_s[:, sl] = q[:HALO]

        d_hout = d_ylru * f["ge"]
        d_ug = d_ylru * hv * f["dge"]
        a = f["a"]
        a1, b1 = _scan_level1(a, a * d_hout, reverse=True)
        a_s[...] = a1
        b_s[...] = b1
        e_next = ecarry_s[...]
        ecarry_s[...] = _scan_level2(a_s, b_s, e_s, e_next, reverse=True)
        sv = d_hout + _shift_up(e_s[...], e_next, 1)
        d_a = sv * h_prev
        mult, ig, xc, r = f["mult"], f["ig"], f["xc"], f["r"]
        d_mult = sv * (ig * xc)
        d_ig = sv * mult * xc
        d_xc = sv * mult * ig
        d_la = d_a * a + jnp.where(f["m2raw"] > 1e-12, d_mult * (-(a * a) / mult), 0.0)
        d_r = d_la * (-LRU_C * f["sp"])
        vec[ROW_LAM] = _colsum8(d_la * (-LRU_C * r))
        d_pr = d_r * r * (1.0 - r)
        d_pi = d_ig * ig * (1.0 - ig)
        vec[ROW_BA] = _colsum8(d_pr)
        vec[ROW_BX] = _colsum8(d_pi)
        dxc_parts = []
        for b in range(nb):
            sl = slice(b * GATE_BLOCK, (b + 1) * GATE_BLOCK)
            rhs = jnp.concatenate([d_pr[:, sl], d_pi[:, sl]], axis=1).astype(BF16)
            dgw_s[b] += _dot_tn(f["xcb"][:, sl], rhs)
            dxc_parts.append(_dot_nt(rhs, gw_s[b]))
        d_xc = d_xc + jnp.concatenate(dxc_parts, axis=1)
        vec[ROW_CONV_B] = _colsum8(d_xc)
        dxc_next = dxc_s[...]
        d_ul = None
        for k in range(CONV_WIDTH):
            ahead = _shift_up(d_xc, dxc_next, CONV_WIDTH - 1 - k)
            cwacc_s[k * SUBLANES:(k + 1) * SUBLANES, :] += _colsum8(ahead * u_l)
            term = ahead * cw[k:k + 1, :]
            d_ul = term if d_ul is None else d_ul + term
        dxc_s[...] = d_xc[:SUBLANES]
        for row, val in vec.items():
            vec_s[row * SUBLANES:(row + 1) * SUBLANES, :] += val
        du_ref[...] = jnp.concatenate([d_ul, d_ug] + dup, axis=1).astype(BF16)

        @pl.when(i == nt - 1)
        def _():
            rows = []
            for row in range(ROW_GA):
                if row in (ROW_CONV_W, ROW_CONV_W + 1, ROW_CONV_W + 2, ROW_CONV_W + 3):
                    k = row - ROW_CONV_W
                    v = jnp.sum(cwacc_s[k * SUBLANES:(k + 1) * SUBLANES, :], axis=0, keepdims=True)
                elif row <= ROW_GP:
                    v = jnp.sum(vec_s[row * SUBLANES:(row + 1) * SUBLANES, :], axis=0, keepdims=True)
                    if row == ROW_LAM:
                        v = v * (-1.0 / (1.0 + jnp.exp(lam)))
                else:
                    v = jnp.zeros((1, lw), F32)
                rows.append(v)
            slab_ref[0:ROW_GA, :] = jnp.concatenate(rows, axis=0)
            lane = lax.broadcasted_iota(jnp.int32, (hd, GATE_BLOCK), 1)
            for b in range(nb):
                for off, row0 in ((0, ROW_GA), (GATE_BLOCK, ROW_GX)):
                    acc = jnp.zeros((hd, GATE_BLOCK), F32)
                    for hh in range(GATE_BLOCK // hd):
                        m = (lane >= hh * hd) & (lane < (hh + 1) * hd)
                        acc = acc + jnp.where(m, dgw_s[b, hh * hd:(hh + 1) * hd, off:off + GATE_BLOCK], 0.0)
                    slab_ref[row0:row0 + hd, b * GATE_BLOCK:(b + 1) * GATE_BLOCK] = acc
            slab_ref[ROW_PW:ROW_PW + LANES, :] = dpw_s[...]

    small = [sp_[k] for k in ("conv_w", "conv_b", "gate_a_w", "gate_x_w", "gate_a_b", "gate_x_b", "lru_lambda",
                              "pool_w", "pool_b", "pool_scale", "norm_lru_g", "norm_pool_g")]
    rev = lambda i: nt - 1 - i

    def stages():
        i = pl.program_id(0)
        return i == 0, i == max(nt - 3, 0), i == nt - 1

    return _call(
        body, hosted, stages, grid=(nt,), name="mixer_bwd",
        in_specs=[pl.BlockSpec((tm, lw), lambda i: (rev(i), 0)),
                  pl.BlockSpec((len(MIX_SAVED), tm, lw), lambda i: (0, rev(i), 0)),
                  pl.BlockSpec((tm, lw), lambda i: (rev(i), 0)),
                  pl.BlockSpec((tm, lw), lambda i: (rev(i), 0)),
                  pl.BlockSpec((SUBLANES, lw), lambda i: (jnp.maximum(rev(i) * (tm // SUBLANES) - 1, 0), 0)),
                  pl.BlockSpec((tm, d), lambda i: (rev(i), 0))]
        + [_const_spec(a.shape) for a in small] + [_const_spec(w_out.shape)],
        out_specs=[pl.BlockSpec((tm, din), lambda i: (rev(i), 0)),
                   pl.BlockSpec((MIX_SLAB_ROWS, SLAB_W), lambda i: (0, 0))],
        out_shape=[SDS((s, din), BF16), SDS((MIX_SLAB_ROWS, SLAB_W), F32)],
        scratch_shapes=[pltpu.VMEM((nb, GATE_BLOCK, 2 * GATE_BLOCK), BF16),
                        pltpu.VMEM((tm, lw), F32), pltpu.VMEM((tm, lw), F32), pltpu.VMEM((tm, lw), F32),
                        pltpu.VMEM((SUBLANES, lw), F32), pltpu.VMEM((SUBLANES, lw), F32),
                        pltpu.VMEM((HALO, lw), F32), pltpu.VMEM((ROW_GA * SUBLANES, lw), F32),
                        pltpu.VMEM((CONV_WIDTH * SUBLANES, lw), F32),
                        pltpu.VMEM((nb, GATE_BLOCK, 2 * GATE_BLOCK), F32), pltpu.VMEM((LANES, lw), F32)],
        args=(u, saved, pooled, h, h, dhres1, *small, w_out), sem=("arbitrary",))


def _inproj_bwd(x, du, dhres1, yn, g_mix, w_in, hosted=None):
    s, d = x.shape
    n = w_in.shape[1]
    nc = n // N_CHIPS
    tm = min(TM_PROJ, s)
    nt = s // tm

    def body(x_ref, du_ref, dhr_ref, yn_ref, g_ref, w_ref, gx_ref, dwin_ref, dwout_ref, dg_ref):
        i = pl.program_id(0)

        @pl.when(i == 0)
        def _():
            dwin_ref[...] = jnp.zeros_like(dwin_ref)
            dwout_ref[...] = jnp.zeros_like(dwout_ref)
            dg_ref[...] = jnp.zeros_like(dg_ref)

        xv = x_ref[...]
        g = g_ref[...]
        r = lax.rsqrt(_rowmean(xv * xv) + EPS)
        xh = xv * r
        h1 = (xh * g).astype(BF16)
        duv = du_ref[...]
        dh1 = _dot_nt(duv, w_ref[...])
        dg_ref[...] += _colsum8(dh1 * xh)
        dhr = dhr_ref[...]
        gx_ref[...] = dhr + _rms_bwd(dh1, xh, r, g)
        for jj in range(N_CHIPS):
            dwin_ref[jj] += _dot_tn(h1, duv[:, jj * nc:(jj + 1) * nc])
        dwout_ref[...] += _dot_tn(yn_ref[...], dhr.astype(BF16))

    def stages():
        i = pl.program_id(0)
        return i == 0, i == max(nt - 3, 0), i == nt - 1

    return _call(
        body, hosted, stages, grid=(nt,), name="inproj_bwd",
        in_specs=[pl.BlockSpec((tm, d), lambda i: (i, 0)), pl.BlockSpec((tm, n), lambda i: (i, 0)),
                  pl.BlockSpec((tm, d), lambda i: (i, 0)), pl.BlockSpec((tm, d), lambda i: (i, 0)),
                  _const_spec((1, d)), _const_spec((d, n))],
        out_specs=[pl.BlockSpec((tm, d), lambda i: (i, 0)), pl.BlockSpec((N_CHIPS, d, nc), lambda i: (0, 0, 0)),
                   pl.BlockSpec((d, d), lambda i: (0, 0)), pl.BlockSpec((SUBLANES, d), lambda i: (0, 0))],
        out_shape=[SDS((s, d), F32), SDS((N_CHIPS, d, nc), F32), SDS((d, d), F32), SDS((SUBLANES, d), F32)],
        scratch_shapes=[], args=(x, du, dhres1, yn, g_mix, w_in), sem=("arbitrary",))


def _place():
    x, y, c = lax.axis_index("x"), lax.axis_index("y"), lax.axis_index("c")
    return x, y, c


def _other_chips(x, y):
    return [(1 - x, y), (x, 1 - y), (1 - x, 1 - y)]


ANY = pl.BlockSpec(memory_space=pl.ANY)
VMEM_SPEC = pl.BlockSpec(memory_space=pltpu.VMEM)

_GATHERED = {"w_in": "cols", "w_out": "major", "ffn_w1": "major", "ffn_w3": "major", "ffn_w2": "major"}
_BIG = ("w_in", "w_out", "ffn_w1", "ffn_w3", "ffn_w2")


def _gather_weights(shards, conv_w, n_remote):
    n = len(shards)
    full_shapes = []
    for name, sh in zip(_BIG, shards):
        r, cdim = sh.shape
        if _GATHERED[name] == "cols":
            assert cdim % LANES == 0
            full_shapes.append((r, cdim * N_CHIPS))
        else:
            full_shapes.append((N_CHIPS, r, cdim))

    def region(ref, name, sh, jj, cc):
        r, cdim = sh
        rows = pl.ds(0, r) if cc is None else pl.ds(pl.multiple_of(cc * (r // 2), 16), r // 2)
        if _GATHERED[name] == "cols":
            return ref.at[rows, pl.ds(pl.multiple_of(jj * cdim, LANES), cdim)]
        return ref.at[jj, rows, :]

    def staged(ref, sh, cc):
        r = sh[0]
        return ref.at[pl.ds(pl.multiple_of(cc * (r // 2), 16), r // 2), :]

    def body(*refs):
        ins, cw_in = refs[:n], refs[n]
        outs, cw_out = refs[n + 1:2 * n + 1], refs[2 * n + 1]
        stage = refs[2 * n + 2:3 * n + 2]
        cw_stage, lsem, ssem, rsem, fssem, frsem, cssem, crsem = refs[3 * n + 2:]
        x, y, c = _place()
        j = 2 * x + y
        chips = _other_chips(x, y)
        for w in range(n_remote):
            stage[w][...] = ins[w][...].astype(BF16)
        cw_stage[...] = jnp.zeros_like(cw_stage)
        cw_stage[0:CONV_WIDTH, :] = cw_in[...]
        shs = [s_.shape for s_ in shards]
        local = [pltpu.make_async_copy(stage[w], region(outs[w], _BIG[w], shs[w], j, None), lsem.at[w])
                 for w in range(n)]
        local.append(pltpu.make_async_copy(cw_stage, cw_out.at[j], lsem.at[n]))
        sends = []
        for k, (px, py) in enumerate(chips):
            for w in range(n_remote):
                sends.append(pltpu.make_async_remote_copy(
                    src_ref=staged(stage[w], shs[w], c), dst_ref=region(outs[w], _BIG[w], shs[w], j, c),
                    send_sem=ssem.at[k * n + w], recv_sem=rsem.at[k * n + w], device_id=(px, py, c),
                    device_id_type=MESH))
            sends.append(pltpu.make_async_remote_copy(
                src_ref=cw_stage, dst_ref=cw_out.at[j], send_sem=cssem.at[k], recv_sem=crsem.at[k],
                device_id=(px, py, c), device_id_type=MESH))
        for cp in sends:
            cp.start()
        for w in range(n_remote, n):
            stage[w][...] = ins[w][...].astype(BF16)
        for cp in local:
            cp.start()
        fwd = []
        for k, (px, py) in enumerate(chips):
            jk = 2 * px + py
            for w in range(n_remote):
                reg = region(outs[w], _BIG[w], shs[w], jk, c)
                pltpu.make_async_remote_copy(src_ref=reg, dst_ref=reg, send_sem=ssem.at[k * n + w],
                                             recv_sem=rsem.at[k * n + w], device_id=(px, py, c),
                                             device_id_type=MESH).wait_recv()
                cp = pltpu.make_async_remote_copy(src_ref=reg, dst_ref=reg, send_sem=fssem.at[k * n + w],
                                                  recv_sem=frsem.at[k * n + w], device_id=(x, y, 1 - c),
                                                  device_id_type=MESH)
                cp.start()
                fwd.append(cp)
            pltpu.make_async_remote_copy(src_ref=cw_stage, dst_ref=cw_out.at[jk], send_sem=cssem.at[k],
                                         recv_sem=crsem.at[k], device_id=(px, py, c),
                                         device_id_type=MESH).wait_recv()
        for k, (px, py) in enumerate(chips):
            jk = 2 * px + py
            for w in range(n_remote):
                reg = region(outs[w], _BIG[w], shs[w], jk, 1 - c)
                pltpu.make_async_remote_copy(src_ref=reg, dst_ref=reg, send_sem=fssem.at[k * n + w],
                                             recv_sem=frsem.at[k * n + w], device_id=(x, y, 1 - c),
                                             device_id_type=MESH).wait_recv()
        for cp in sends + fwd:
            cp.wait_send()
        for cp in local:
            cp.wait()

    nsem = 3 * n
    return pl.pallas_call(
        body, name="gather_first",
        in_specs=[VMEM_SPEC] * (n + 1), out_specs=[ANY] * (n + 1),
        out_shape=[SDS(fs, BF16) for fs in full_shapes] + [SDS((N_CHIPS, SUBLANES, LANES), F32)],
        scratch_shapes=[pltpu.VMEM(s_.shape, BF16) for s_ in shards] + [pltpu.VMEM((SUBLANES, LANES), F32)]
        + [pltpu.SemaphoreType.DMA((n + 1,))] + [pltpu.SemaphoreType.DMA((nsem,))] * 4
        + [pltpu.SemaphoreType.DMA((3,))] * 2,
        compiler_params=_cp())(*shards, conv_w)


def _start_all(make):
    def f(ins, outs, sems):
        for cp in make(ins, outs, sems):
            cp.start()
    return f


def _wait_all(make):
    def f(ins, outs, sems):
        for cp in make(ins, outs, sems):
            cp.wait()
    return f


def _ffn_gather_hosted(arrs):
    n = len(arrs)

    def make(outs, sems):
        ssem, rsem, fs, fr = sems
        x, y, c = _place()
        j = 2 * x + y

        def reg(w, jj, cc):
            hr = arrs[w].shape[1] // 2
            return outs[w].at[jj, pl.ds(pl.multiple_of(cc * hr, 16), hr), :]

        def rc(w, jj, cc, s_sem, r_sem, dev):
            return pltpu.make_async_remote_copy(src_ref=reg(w, jj, cc), dst_ref=reg(w, jj, cc), send_sem=s_sem,
                                                recv_sem=r_sem, device_id=dev, device_id_type=MESH)

        sends, recvs, fwds, frecvs = [], [], [], []
        for k, (px, py) in enumerate(_other_chips(x, y)):
            jk = 2 * px + py
            for w in range(n):
                q = k * n + w
                sends.append(rc(w, j, c, ssem.at[q], rsem.at[q], (px, py, c)))
                recvs.append(rc(w, jk, c, ssem.at[q], rsem.at[q], (px, py, c)))
                fwds.append(rc(w, jk, c, fs.at[q], fr.at[q], (x, y, 1 - c)))
                frecvs.append(rc(w, jk, 1 - c, fs.at[q], fr.at[q], (x, y, 1 - c)))
        return sends, recvs, fwds, frecvs

    def start(ins, outs, sems):
        for cp in make(outs, sems)[0]:
            cp.start()

    def mid(ins, outs, sems):
        _, recvs, fwds, _ = make(outs, sems)
        for r, f in zip(recvs, fwds):
            r.wait_recv()
            f.start()

    def finish(ins, outs, sems):
        sends, _, fwds, frecvs = make(outs, sems)
        for r in frecvs:
            r.wait_recv()
        for cp in sends + fwds:
            cp.wait_send()

    return _Hosted(arrs, [SDS(a.shape, a.dtype) for a in arrs], [3 * n] * 4, start, finish, mid=mid,
                   aliases={w: w for w in range(n)})


def _rs_sibling_hosted(arrs):
    n = len(arrs)

    def make(ins, outs, sems):
        x, y, c = _place()
        cps = []
        for w in range(n):
            hr = arrs[w].shape[1] // 2
            src = ins[w].at[:, pl.ds(pl.multiple_of((1 - c) * hr, SUBLANES), hr), :]
            cps.append(pltpu.make_async_remote_copy(src_ref=src, dst_ref=outs[w], send_sem=sems[0].at[w],
                                                    recv_sem=sems[1].at[w], device_id=(x, y, 1 - c),
                                                    device_id_type=MESH))
        return cps

    return _Hosted(arrs, [SDS((a.shape[0], a.shape[1] // 2, a.shape[2]), F32) for a in arrs], [n, n],
                   _start_all(make), _wait_all(make))


def _rs_chips_hosted(parts):
    n = len(parts)

    def make(ins, outs, sems):
        x, y, c = _place()
        j = 2 * x + y
        cps = []
        for k, (px, py) in enumerate(_other_chips(x, y)):
            jk = 2 * px + py
            for w in range(n):
                cps.append(pltpu.make_async_remote_copy(
                    src_ref=ins[w].at[jk], dst_ref=outs[w].at[j], send_sem=sems[0].at[k * n + w],
                    recv_sem=sems[1].at[k * n + w], device_id=(px, py, c), device_id_type=MESH))
        return cps

    return _Hosted(parts, [SDS(p.shape, p.dtype) for p in parts], [3 * n, 3 * n], _start_all(make), _wait_all(make))


def _rs_swap_hosted(halves):
    n = len(halves)

    def make(ins, outs, sems):
        x, y, c = _place()
        return [pltpu.make_async_remote_copy(src_ref=ins[w], dst_ref=outs[w], send_sem=sems[0].at[w],
                                             recv_sem=sems[1].at[w], device_id=(x, y, 1 - c), device_id_type=MESH)
                for w in range(n)]

    return _Hosted(halves, [SDS(h.shape, F32) for h in halves], [n, n], _start_all(make), _wait_all(make))


HBM_SPEC = pl.BlockSpec(memory_space=pltpu.HBM)
SEM_SPEC = pl.BlockSpec(memory_space=pltpu.SEMAPHORE)
_EFFECT = pltpu.SideEffectType.DATAFLOW_SIDE_EFFECTING


def _split_start(h, name):
    n_in, n_out, ns = len(h.ins), len(h.out_shapes), len(h.sems)
    ins = [pltpu.with_memory_space_constraint(a, pltpu.HBM) for a in h.ins]
    lands = [pltpu.with_memory_space_constraint(lax.empty(o.shape, o.dtype), pltpu.HBM) for o in h.out_shapes]

    def body(*refs):
        i_refs, l_refs = refs[:n_in], refs[n_in:n_in + n_out]
        s_refs = refs[n_in + n_out:n_in + n_out + ns]
        token = refs[-1]
        h.start(i_refs, l_refs, s_refs)
        token[...] = jnp.zeros_like(token)

    res = pl.pallas_call(
        body, name=name, in_specs=[HBM_SPEC] * (n_in + n_out),
        out_specs=[SEM_SPEC] * ns + [HBM_SPEC] * n_out + [VMEM_SPEC],
        out_shape=[pltpu.SemaphoreType.DMA((k,)) for k in h.sems]
        + [pltpu.HBM(o.shape, o.dtype) for o in h.out_shapes] + [SDS((SUBLANES, LANES), F32)],
        input_output_aliases={n_in + k: ns + k for k in range(n_out)},
        compiler_params=pltpu.CompilerParams(has_side_effects=_EFFECT))(*ins, *lands)
    return list(res[:ns]) + ins + list(res[ns:-1]), res[-1]


def _split_wait(h, state, after, name):
    n_in, n_out, ns = len(h.ins), len(h.out_shapes), len(h.sems)
    sems, bufs = state[:ns], state[ns:]

    def body(*refs):
        i_refs, l_refs = refs[:n_in], refs[n_in:n_in + n_out]
        s_refs = refs[n_in + n_out:n_in + n_out + ns]
        h.finish(i_refs, l_refs, s_refs)

    res = pl.pallas_call(
        body, name=name, in_specs=[HBM_SPEC] * (n_in + n_out) + [SEM_SPEC] * ns + [ANY],
        out_specs=[HBM_SPEC] * n_out,
        out_shape=[pltpu.HBM(b.shape, b.dtype) for b in bufs[n_in:]],
        input_output_aliases={n_in + k: k for k in range(n_out)},
        compiler_params=pltpu.CompilerParams(has_side_effects=_EFFECT))(*bufs, *sems, after)
    return list(res)


def _run_comm(hosted, name):
    return _call(lambda: None, hosted, None, name=name, grid=(), in_specs=[], out_specs=[], out_shape=[],
                 scratch_shapes=[], args=(), sem=None)[1]


def _row_tile(rows, cols, n_arrays):
    budget = 24 * 1024 * 1024 // (2 * 4 * n_arrays * cols)
    best = SUBLANES
    for t in range(SUBLANES, rows + 1, SUBLANES):
        if rows % t == 0 and t <= budget:
            best = t
    return best


def _place_index(which):
    x, y, c = _place()
    v = c if which == "c" else 2 * x + y
    return jnp.reshape(v, (1,)).astype(jnp.int32)


def _add_own_half(full, recv, name, wire=BF16, token=None):
    nsh, rows, cols = full.shape
    hr = rows // 2
    t = _row_tile(hr, cols, 4)
    nt = hr // t
    if token is None:
        token = jnp.zeros((SUBLANES, LANES), F32)

    def body(c_ref, a_ref, b_ref, tok_ref, o_ref, ob_ref):
        v = a_ref[...] + b_ref[...] + tok_ref[0:1, 0:1]
        o_ref[...] = v
        ob_ref[...] = v.astype(wire)

    half = pl.BlockSpec((1, t, cols), lambda s_, i, c_ref: (s_, i, 0))
    return pl.pallas_call(
        body, name=name,
        grid_spec=pltpu.PrefetchScalarGridSpec(
            num_scalar_prefetch=1, grid=(nsh, nt),
            in_specs=[pl.BlockSpec((1, t, cols), lambda s_, i, c_ref: (s_, c_ref[0] * nt + i, 0)), half,
                      pl.BlockSpec((SUBLANES, LANES), lambda s_, i, c_ref: (0, 0))],
            out_specs=[half, half]),
        out_shape=[SDS((nsh, hr, cols), F32), SDS((nsh, hr, cols), wire)],
        compiler_params=_cp(("parallel", "parallel")))(_place_index("c"), full, recv, token)


def _sum_chips(own, recv, name):
    nsh, hr, cols = own.shape
    t = _row_tile(hr, cols, 6)

    def body(j_ref, own_ref, *rest):
        r_refs, o_ref = rest[:nsh], rest[nsh]
        j = j_ref[0]
        mine = own_ref[0]
        parts = [jnp.where(j == k, mine, r_refs[k][0].astype(F32)) for k in range(nsh)]
        o_ref[...] = ((parts[0] + parts[1]) + parts[2]) + parts[3]

    def other(k):
        return pl.BlockSpec((1, t, cols), lambda i, j_ref: (jnp.where(j_ref[0] == k, (k + 1) % nsh, k), i, 0))

    return pl.pallas_call(
        body, name=name,
        grid_spec=pltpu.PrefetchScalarGridSpec(
            num_scalar_prefetch=1, grid=(hr // t,),
            in_specs=[pl.BlockSpec((1, t, cols), lambda i, j_ref: (j_ref[0], i, 0))]
            + [other(k) for k in range(nsh)],
            out_specs=pl.BlockSpec((t, cols), lambda i, j_ref: (i, 0))),
        out_shape=SDS((hr, cols), F32), compiler_params=_cp(("parallel",)))(_place_index("j"), own, *([recv] * nsh))


def _adamw_math(w, g, m, v):
    m = ADAM_B1 * m + (1.0 - ADAM_B1) * g
    v = ADAM_B2 * v + (1.0 - ADAM_B2) * (g * g)
    m_hat = m / (1.0 - ADAM_B1 ** ADAM_STEP)
    v_hat = v / (1.0 - ADAM_B2 ** ADAM_STEP)
    delta = -ADAM_LR * (m_hat / (jnp.sqrt(v_hat) + ADAM_EPS) + ADAM_WD * w)
    return delta, m, v


def _adamw_big(w, g_own, g_sib, m, v, name, token=None):
    _, rows, cols = w.shape
    hr = rows // 2
    t = _row_tile(hr, cols, 9)
    nth = hr // t
    if token is None:
        token = jnp.zeros((SUBLANES, LANES), F32)

    def body(c_ref, w_ref, go_ref, gs_ref, m_ref, v_ref, tok_ref, g_ref, d_ref, mo_ref, vo_ref):
        own = (pl.program_id(0) // nth) == c_ref[0]
        g = jnp.where(own, go_ref[...], gs_ref[...]) + tok_ref[0:1, 0:1]
        g_ref[0] = g
        d_ref[0], mo_ref[0], vo_ref[0] = _adamw_math(w_ref[0], g, m_ref[0], v_ref[0])

    spec = pl.BlockSpec((1, t, cols), lambda i, c_ref: (0, i, 0))
    hspec = pl.BlockSpec((t, cols), lambda i, c_ref: (i % nth, 0))
    tspec = pl.BlockSpec((SUBLANES, LANES), lambda i, c_ref: (0, 0))
    return pl.pallas_call(
        body, name=name,
        grid_spec=pltpu.PrefetchScalarGridSpec(
            num_scalar_prefetch=1, grid=(2 * nth,), in_specs=[spec, hspec, hspec, spec, spec, tspec],
            out_specs=[spec] * 4),
        out_shape=[SDS((1, rows, cols), F32)] * 4,
        compiler_params=_cp(("parallel",)))(_place_index("c"), w, g_own, g_sib, m, v, token)


def _build_slab(mix_slab, dg_mix, dg_ffn, dg_fin, loss8):
    def body(ms_ref, gm_ref, gf_ref, gn_ref, loss_ref, out_ref):
        rows = []
        for ref in (gm_ref, gf_ref, gn_ref):
            v = jnp.sum(ref[...], axis=0, keepdims=True)
            rows += [v[:, :SLAB_W], v[:, SLAB_W:]]
        rows.append(jnp.concatenate([loss_ref[0:1, :]] * (SLAB_W // LANES), axis=1))
        rows.append(jnp.zeros((SLAB_ROWS - ROW_LOSS - 1, SLAB_W), F32))
        tail = jnp.concatenate(rows, axis=0)
        for k in range(N_CHIPS):
            out_ref[k, 0:MIX_SLAB_ROWS, :] = ms_ref[...]
            out_ref[k, MIX_SLAB_ROWS:SLAB_ROWS, :] = tail

    return pl.pallas_call(
        body, name="build_slab", in_specs=[VMEM_SPEC] * 5, out_specs=VMEM_SPEC,
        out_shape=SDS((N_CHIPS, SLAB_ROWS, SLAB_W), F32),
        compiler_params=_cp())(mix_slab, dg_mix, dg_ffn, dg_fin, loss8)


_SMALL_ROWS = (("conv_b", ROW_CONV_B), ("gate_a_b", ROW_BA), ("gate_x_b", ROW_BX), ("lru_lambda", ROW_LAM),
               ("pool_b", ROW_PB), ("pool_scale", ROW_PS), ("norm_lru_g", ROW_GL), ("norm_pool_g", ROW_GP))
_WIDE_ROWS = (("norm_mix_g", ROW_MIX), ("norm_ffn_g", ROW_FFN), ("final_norm_g", ROW_FIN))
_BLOCK_ROWS = (("gate_a_w", ROW_GA), ("gate_x_w", ROW_GX), ("pool_w", ROW_PW))
_SMALL_ORDER = tuple(n for n, _ in _SMALL_ROWS) + tuple(n for n, _ in _WIDE_ROWS) + tuple(
    n for n, _ in _BLOCK_ROWS) + ("conv_w",)


def _adamw_small(slab_own, slab_sib, wmv):
    names = _SMALL_ORDER
    flat = [a for nme in names for a in wmv[nme]]
    nin = len(flat)

    def body(*refs):
        own_ref, sib_ref, j_ref = refs[0], refs[1], refs[2]
        ins = refs[3:3 + nin]
        outs = refs[3 + nin:-1]
        first = j_ref[1] == 0
        slab_ref = jnp.concatenate([jnp.where(first, own_ref[...], sib_ref[...]),
                                    jnp.where(first, sib_ref[...], own_ref[...])], axis=0)
        refs[-1][...] = jnp.broadcast_to(slab_ref[ROW_LOSS:ROW_LOSS + 1, 0:LANES], (SUBLANES, LANES))
        grads = {}
        for nme, row in _SMALL_ROWS:
            grads[nme] = slab_ref[row:row + 1, :]
        for nme, row in _WIDE_ROWS:
            grads[nme] = jnp.concatenate([slab_ref[row:row + 1, :], slab_ref[row + 1:row + 2, :]], axis=1)
        full = slab_ref[ROW_CONV_W:ROW_CONV_W + CONV_WIDTH, :]
        jv = j_ref[0]
        g = jnp.zeros((CONV_WIDTH, LANES), F32)
        for jj in range(N_CHIPS):
            g = jnp.where(jv == jj, full[:, jj * LANES:(jj + 1) * LANES], g)
        grads["conv_w"] = g
        block_rows = dict(_BLOCK_ROWS)
        for idx, nme in enumerate(names):
            w_ref, m_ref, v_ref = ins[3 * idx:3 * idx + 3]
            if nme in block_rows:
                nblk, r, c = w_ref.shape
                parts = [(b, slab_ref[block_rows[nme]:block_rows[nme] + r, b * c:(b + 1) * c]) for b in range(nblk)]
            else:
                parts = [(Ellipsis, grads[nme])]
            for b, g in parts:
                delta, m, v = _adamw_math(w_ref[b], g, m_ref[b], v_ref[b])
                outs[4 * idx][b] = g
                outs[4 * idx + 1][b] = delta
                outs[4 * idx + 2][b] = m
                outs[4 * idx + 3][b] = v

    place = jnp.concatenate([_place_index("j"), _place_index("c")])
    out_shape = [SDS(wmv[nme][0].shape, F32) for nme in names for _ in range(4)] + [SDS((SUBLANES, LANES), F32)]
    res = pl.pallas_call(
        body, name="adamw_small",
        in_specs=[VMEM_SPEC, VMEM_SPEC, pl.BlockSpec(memory_space=pltpu.SMEM)] + [VMEM_SPEC] * nin,
        out_specs=[VMEM_SPEC] * len(out_shape), out_shape=out_shape,
        compiler_params=_cp())(slab_own, slab_sib, place, *flat)
    return {nme: tuple(res[4 * idx:4 * idx + 4]) for idx, nme in enumerate(names)}, res[-1]


_FFN = ("ffn_w1", "ffn_w3", "ffn_w2")
_TRANSPOSED = ("ffn_w1", "ffn_w3")


def _local_step(x, target, full, sp_, distributed):
    d = x.shape[1]
    (u,), got = _inproj(x, sp_["norm_mix_g"], full["w_in"],
                        [_ffn_gather_hosted([full["w_out"]])] if distributed else None)
    w_out = (got[0][0] if distributed else full["w_out"]).reshape(d, d)
    gather = [_ffn_gather_hosted([full[n] for n in _FFN])] if distributed else None
    (h, yn, hres1, saved, pooled), got = _mixer_fwd(u, x, sp_, w_out, gather)
    w1, w3, w2 = got[0] if distributed else [full[n] for n in _FFN]
    h2, a1, a3, ff = _ffn_up(hres1, sp_["norm_ffn_g"], w1, w3)
    dh, dhb, loss8, dg_fin = _ffn_down(ff, hres1, target, sp_["final_norm_g"], w2)
    da1, da3, dhres1, dg_ffn = _ffn_bwd(dhb, dh, a1, a3, hres1, sp_["norm_ffn_g"], w1, w3, w2)
    dw1, dw3 = _ffn_wgrad13(h2, da1, da3)
    (dw2,), got = _ffn_wgrad2(ff, dhb, [_rs_sibling_hosted([dw1, dw3])] if distributed else None)
    dws = [dw1, dw3, dw2]
    rs2 = None
    if distributed:
        late = _rs_sibling_hosted([dw2])
        state, token = _split_start(late, "ffn_sibling_start")
        pairs = [_add_own_half(a, r, "add_half_" + n, token=token) for n, a, r in zip(_FFN[:2], dws[:2], got[0])]
        recv, = _split_wait(late, state, pairs[1][1], "ffn_sibling_wait")
        pairs.append(_add_own_half(dw2, recv, "add_half_" + _FFN[2]))
        rs2 = [_rs_chips_hosted([pb for _, pb in pairs])]
    (du, mix_slab), got = _mixer_bwd(u, saved, pooled, h, dhres1, sp_, w_out, rs2)
    g_mix = sp_["norm_mix_g"]
    if distributed:
        fin = [_sum_chips(pairs[k][0], got[0][k], "sum_chips_" + n) for k, n in enumerate(_FFN)]
        swap = _rs_swap_hosted(fin)
        state, token = _split_start(swap, "ffn_swap_start")
        g_mix = g_mix + token[0:1, 0:1]
    (gx, dwin, dwout, dg_mix), _ = _inproj_bwd(x, du, dhres1, yn, g_mix, full["w_in"])
    if distributed:
        sib = _split_wait(swap, state, dg_mix, "ffn_swap_wait")
    big = {"w_in": dwin, "w_out": dwout.reshape(N_CHIPS, d // N_CHIPS, d)}
    for k, n in enumerate(_FFN):
        big[n] = (fin[k], sib[k]) if distributed else dws[k]
    return gx, big, (mix_slab, dg_mix, dg_ffn, dg_fin, loss8)


_SMALL_LAYOUT = {
    "gate_a_w": (lambda a: a[0], lambda a: a[None]),
    "gate_x_w": (lambda a: a[0], lambda a: a[None]),
    "pool_w": (lambda a: a[0], lambda a: a[None]),
    "conv_w": (lambda a: a[0], lambda a: a[None]),
    "final_norm_g": (lambda a: a[None], lambda a: a[0]),
}

_WEIGHTS = ("norm_mix_g", "w_in", "conv_w", "conv_b", "gate_a_w", "gate_a_b", "gate_x_w", "gate_x_b", "lru_lambda",
            "pool_w", "pool_b", "pool_scale", "norm_lru_g", "norm_pool_g", "w_out", "norm_ffn_g", "ffn_w1",
            "ffn_w3", "ffn_w2", "final_norm_g")


def kernel(x, norm_mix_g, w_in, conv_w, conv_b, gate_a_w, gate_a_b, gate_x_w, gate_x_b, lru_lambda, pool_w, pool_b, pool_scale, norm_lru_g, norm_pool_g, w_out, norm_ffn_g, ffn_w1, ffn_w3, ffn_w2, final_norm_g, loss_target, m_norm_mix_g, m_w_in, m_conv_w, m_conv_b, m_gate_a_w, m_gate_a_b, m_gate_x_w, m_gate_x_b, m_lru_lambda, m_pool_w, m_pool_b, m_pool_scale, m_norm_lru_g, m_norm_pool_g, m_w_out, m_norm_ffn_g, m_ffn_w1, m_ffn_w3, m_ffn_w2, m_final_norm_g, v_norm_mix_g, v_w_in, v_conv_w, v_conv_b, v_gate_a_w, v_gate_a_b, v_gate_x_w, v_gate_x_b, v_lru_lambda, v_pool_w, v_pool_b, v_pool_scale, v_norm_lru_g, v_norm_pool_g, v_w_out, v_norm_ffn_g, v_ffn_w1, v_ffn_w3, v_ffn_w2, v_final_norm_g):
    loc = locals()
    w = {n: loc[n] for n in _WEIGHTS}
    m = {n: loc["m_" + n] for n in _WEIGHTS}
    v = {n: loc["v_" + n] for n in _WEIGHTS}

    def lay(nme, a):
        return _SMALL_LAYOUT[nme][0](a) if nme in _SMALL_LAYOUT else a

    def unlay(nme, a):
        return _SMALL_LAYOUT[nme][1](a) if nme in _SMALL_LAYOUT else a

    for group in (w, m, v):
        for n in _TRANSPOSED:
            group[n] = jnp.transpose(group[n], (0, 2, 1))

    gathered = _gather_weights([w[n][0] for n in _BIG], w["conv_w"][0], n_remote=1)
    full = dict(zip(_BIG, gathered[:-1]))
    cw_all = gathered[-1]
    sp_ = {n: lay(n, w[n]) for n in _SMALL_ORDER}
    sp_["conv_w"] = jnp.transpose(cw_all[:, :CONV_WIDTH, :], (1, 0, 2)).reshape(CONV_WIDTH, N_CHIPS * LANES)

    gx, big, small = _local_step(x[0], loss_target[0], full, sp_, distributed=True)

    late = ("w_in", "w_out", "slab")
    big["slab"] = _build_slab(*small)
    fin = {n: big[n][0] for n in _FFN}
    sib = {n: big[n][1] for n in _FFN}
    recv1, = _run_comm([_rs_sibling_hosted([big[n] for n in late])], "tail_sibling")
    pairs = [_add_own_half(big[n], r, "add_half_" + n, F32 if n == "slab" else BF16) for n, r in zip(late, recv1)]
    chips = _rs_chips_hosted([pb for _, pb in pairs])
    state, token = _split_start(chips, "tail_chips_start")
    out = {}
    for n in _FFN:
        out[n] = tuple(_adamw_big(w[n], fin[n], sib[n], m[n], v[n], "adamw_" + n, token))
    recv2 = _split_wait(chips, state, out[_FFN[-1]][1], "tail_chips_wait")
    for n, (p, _), r in zip(late, pairs, recv2):
        fin[n] = _sum_chips(p, r, "sum_chips_" + n)
    swapped, = _run_comm([_rs_swap_hosted([fin[n] for n in late])], "tail_swap")
    sib.update(zip(late, swapped))
    for n in late[:2]:
        out[n] = tuple(_adamw_big(w[n], fin[n], sib[n], m[n], v[n], "adamw_" + n))
    for n in _TRANSPOSED:
        out[n] = tuple(jnp.transpose(a, (0, 2, 1)) for a in out[n])
    wmv = {n: (lay(n, w[n]), lay(n, m[n]), lay(n, v[n])) for n in _SMALL_ORDER}
    res, loss = _adamw_small(fin["slab"], sib["slab"], wmv)
    for n in _SMALL_ORDER:
        out[n] = tuple(unlay(n, a) for a in res[n])
    return (loss[0, 0], gx[None]) + tuple(out[n][k] for k in range(4) for n in _WEIGHTS)
```

```python
import functools
import math

import jax
import jax.numpy as jnp
from jax import lax
from jax.experimental import pallas as pl
from jax.experimental.pallas import tpu as pltpu

F32 = jnp.float32
BF16 = jnp.bfloat16
SDS = jax.ShapeDtypeStruct
MESH = pl.DeviceIdType.MESH

EPS = 1e-6
LRU_C = 8.0
CONV_WIDTH = 4
POOL_WINDOWS = (2, 4, 8, 16)
HALO = 16
LANES = 128
SUBLANES = 8
GATE_BLOCK = 256
N_CHIPS = 4

ADAM_LR = 0.001
ADAM_B1 = 0.9
ADAM_B2 = 0.999
ADAM_EPS = 1e-08
ADAM_WD = 0.01
ADAM_STEP = 10

TM_PROJ = 512
TM_MIX = 512
TM_FFN = 512
TM_WGRAD = 2048
MIX_SAVED = ("xc", "r", "ig", "a", "m2raw", "ge", "dge")
FFN_ROW_CHUNKS = 2
VMEM_LIMIT = 56 * 1024 * 1024

SLAB_W = 512
ROW_CONV_B, ROW_CONV_W, ROW_BA, ROW_BX, ROW_LAM, ROW_PB, ROW_PS, ROW_GL, ROW_GP = 0, 1, 5, 6, 7, 8, 9, 10, 11
ROW_GA, ROW_GX, ROW_PW = 16, 80, 144
ROW_MIX, ROW_FFN, ROW_FIN, ROW_LOSS = 272, 274, 276, 278
MIX_SLAB_ROWS = 272
SLAB_ROWS = 288


def _cp(sem=None, **kw):
    if sem is not None:
        kw["dimension_semantics"] = sem
    return pltpu.CompilerParams(vmem_limit_bytes=VMEM_LIMIT, **kw)


def _const_spec(shape):
    nd = len(shape)
    return pl.BlockSpec(shape, lambda *_: (0,) * nd, pipeline_mode=pl.Buffered(1))


def _sigmoid(x):
    return 1.0 / (1.0 + jnp.exp(-x))


def _dot(a, b):
    return jnp.dot(a, b, preferred_element_type=F32)


def _dot_nt(a, b):
    return lax.dot_general(a, b, (((1,), (1,)), ((), ())), preferred_element_type=F32)


def _dot_tn(a, b):
    return lax.dot_general(a, b, (((0,), (0,)), ((), ())), preferred_element_type=F32)


def _colsum8(v):
    m, c = v.shape
    return v.reshape(m // SUBLANES, SUBLANES, c).sum(axis=0)


def _rowmean(v):
    return jnp.mean(v, axis=-1, keepdims=True)


def _rms_bwd(dy, xhat, r, g):
    dxh = dy * g
    return r * (dxh - xhat * _rowmean(dxh * xhat))


def _softplus_neg(lam):
    z = -lam
    e = jnp.exp(-jnp.abs(z))
    u = 1.0 + e
    d = u - 1.0
    log1p = jnp.where(d == 0.0, e, jnp.log(u) * (e / jnp.where(d == 0.0, 1.0, d)))
    return jnp.maximum(z, 0.0) + log1p


def _neg_expm1(z):
    series = -(z * (1.0 + z * (0.5 + z * (1.0 / 6.0 + z * (1.0 / 24.0)))))
    return jnp.where(z > -0.03, series, 1.0 - jnp.exp(z))


_GELU_C = math.sqrt(2.0 / math.pi)
_GELU_K = 0.044715


def _gelu_parts(x):
    x2 = x * x
    th = jnp.tanh(_GELU_C * (x + _GELU_K * x2 * x))
    ge = 0.5 * x * (1.0 + th)
    dge = 0.5 * (1.0 + th) + 0.5 * x * (1.0 - th * th) * (_GELU_C * (1.0 + 3.0 * _GELU_K * x2))
    return ge, dge


def _shift_down(halo, tile, k):
    if k == 0:
        return tile
    ext = jnp.concatenate([halo, tile], axis=0)
    n = tile.shape[0]
    h = halo.shape[0]
    return ext[h - k:h - k + n]


def _shift_up(tile, nxt, k):
    if k == 0:
        return tile
    ext = jnp.concatenate([tile, nxt], axis=0)
    return ext[k:k + tile.shape[0]]


def _build_gate_blocks(ga_ref, gx_ref, gw_ref):
    hd = ga_ref.shape[1]
    per = GATE_BLOCK // hd
    zero = jnp.zeros((hd, hd), F32)
    for b in range(gw_ref.shape[0]):
        for src, off in ((ga_ref, 0), (gx_ref, GATE_BLOCK)):
            for hh in range(per):
                row = jnp.concatenate([zero] * hh + [src[b * per + hh]] + [zero] * (per - 1 - hh), axis=1)
                gw_ref[b, hh * hd:(hh + 1) * hd, off:off + GATE_BLOCK] = row.astype(BF16)


def _scan_level1(a, b, reverse):
    m, c = a.shape
    a3 = a.reshape(m // SUBLANES, SUBLANES, c)
    b3 = b.reshape(m // SUBLANES, SUBLANES, c)
    row = lax.broadcasted_iota(jnp.int32, a3.shape, 1)
    for s in (1, 2, 4):
        sh = (SUBLANES - s) if reverse else s
        a_sh = pltpu.roll(a3, sh, 1)
        b_sh = pltpu.roll(b3, sh, 1)
        ok = (row < SUBLANES - s) if reverse else (row >= s)
        b3 = jnp.where(ok, a3 * b_sh + b3, b3)
        a3 = jnp.where(ok, a3 * a_sh, a3)
    return a3.reshape(m, c), b3.reshape(m, c)


def _scan_level2(a_ref, b_ref, out_ref, carry, reverse):
    m, c = a_ref.shape
    ng = m // SUBLANES

    def step(g, cr):
        gi = (ng - 1 - g) if reverse else g
        off = pl.multiple_of(gi * SUBLANES, SUBLANES)
        h = b_ref[pl.ds(off, SUBLANES), :] + a_ref[pl.ds(off, SUBLANES), :] * cr
        out_ref[pl.ds(off, SUBLANES), :] = h
        edge = h[0:1, :] if reverse else h[SUBLANES - 1:SUBLANES, :]
        return jnp.broadcast_to(edge, (SUBLANES, c))

    return lax.fori_loop(0, ng, step, carry, unroll=4)


def _mixer_recompute(u, hal, t0, cw, cb, gw_ref, ba, bx, lam, pw_ref, pb, ps):
    tm = u.shape[0]
    lw = cb.shape[1]
    u_l, u_g, u_p = u[:, :lw], u[:, lw:2 * lw], u[:, 2 * lw:]
    hal_l, hal_p = hal[:, :lw], hal[:, 2 * lw:]
    taps = [_shift_down(hal_l, u_l, CONV_WIDTH - 1 - k) for k in range(CONV_WIDTH)]
    xc = cb
    for k in range(CONV_WIDTH):
        xc = xc + taps[k] * cw[k:k + 1, :]
    xcb = xc.astype(BF16)
    nb = lw // GATE_BLOCK
    gs = [_dot(xcb[:, b * GATE_BLOCK:(b + 1) * GATE_BLOCK], gw_ref[b]) for b in range(nb)]
    r = _sigmoid(jnp.concatenate([g[:, :GATE_BLOCK] for g in gs], axis=1) + ba)
    ig = _sigmoid(jnp.concatenate([g[:, GATE_BLOCK:] for g in gs], axis=1) + bx)
    sp = _softplus_neg(lam)
    la = (-LRU_C * r) * sp
    a = jnp.exp(la)
    m2raw = _neg_expm1(2.0 * la)
    mult = jnp.sqrt(jnp.maximum(m2raw, 1e-12))
    ge, dge = _gelu_parts(u_g)
    row = lax.broadcasted_iota(jnp.int32, (tm, LANES), 0) + t0
    pooled, invs, zs = [], [], []
    for gi, w in enumerate(POOL_WINDOWS):
        e = jnp.concatenate([hal_p[:, gi * LANES:(gi + 1) * LANES], u_p[:, gi * LANES:(gi + 1) * LANES]], axis=0)
        s = e
        k = 1
        while k < w:
            s = s + pltpu.roll(s, k, 0)
            k *= 2
        inv = 1.0 / jnp.minimum(row + 1, w).astype(F32)
        pg = s[HALO:] * inv - e[HALO:]
        pooled.append(pg)
        invs.append(inv)
        zs.append(_dot(pg.astype(BF16), pw_ref[gi].astype(BF16)))
    z = jnp.concatenate(zs, axis=1) + pb
    y_pool = z * ps
    return dict(u_l=u_l, u_g=u_g, taps=taps, xc=xc, xcb=xcb, r=r, ig=ig, sp=sp, la=la, a=a, m2raw=m2raw,
                mult=mult, ge=ge, dge=dge, pooled=pooled, invs=invs, z=z, y_pool=y_pool)


ANY = pl.BlockSpec(memory_space=pl.ANY)
VMEM_SPEC = pl.BlockSpec(memory_space=pltpu.VMEM)


class _Hosted:
    def __init__(self, ins, out_shapes, sems, start, finish, mid=None, aliases=None):
        self.ins, self.out_shapes, self.sems = list(ins), list(out_shapes), list(sems)
        self.start, self.mid, self.finish = start, mid, finish
        self.aliases = dict(aliases or {})


def _call(body, hosted, stage_preds, *, name, grid, in_specs, out_specs, out_shape, scratch_shapes, args, sem):
    hosted = list(hosted or [])
    n_in, n_out, n_scr = len(in_specs), len(out_specs), len(scratch_shapes)
    c_in = [a for h in hosted for a in h.ins]
    c_out = [o for h in hosted for o in h.out_shapes]
    c_sem = [pltpu.SemaphoreType.DMA((k,)) for h in hosted for k in h.sems]

    def full(*refs):
        p = 0
        parts = []
        for cnt in (n_in, len(c_in), n_out, len(c_out), n_scr, len(c_sem)):
            parts.append(refs[p:p + cnt])
            p += cnt
        hi, ci, ho, co, hs, cs = parts
        per = []
        a = b = c_ = 0
        for h in hosted:
            per.append((h, ci[a:a + len(h.ins)], co[b:b + len(h.out_shapes)], cs[c_:c_ + len(h.sems)]))
            a, b, c_ = a + len(h.ins), b + len(h.out_shapes), c_ + len(h.sems)
        first = mid = last = None
        if hosted and grid:
            first, mid, last = stage_preds()

        def run(fn, pred, i_, o_, s_):
            if fn is None:
                return
            if pred is None:
                fn(i_, o_, s_)
            else:
                pl.when(pred)(functools.partial(fn, i_, o_, s_))

        for h, i_, o_, s_ in per:
            run(h.start, first, i_, o_, s_)
        body(*hi, *ho, *hs)
        for h, i_, o_, s_ in per:
            run(h.mid, mid, i_, o_, s_)
        for h, i_, o_, s_ in per:
            run(h.finish, last, i_, o_, s_)

    aliases = {}
    a = b = 0
    for h in hosted:
        for k, v in h.aliases.items():
            aliases[n_in + a + k] = n_out + b + v
        a, b = a + len(h.ins), b + len(h.out_shapes)
    res = pl.pallas_call(
        full, name=name, grid=grid, in_specs=list(in_specs) + [ANY] * len(c_in),
        out_specs=list(out_specs) + [ANY] * len(c_out), out_shape=list(out_shape) + c_out,
        scratch_shapes=list(scratch_shapes) + c_sem, input_output_aliases=aliases,
        compiler_params=_cp(sem))(*args, *c_in)
    res = list(res)
    outs = []
    p = n_out
    for h in hosted:
        outs.append(res[p:p + len(h.out_shapes)])
        p += len(h.out_shapes)
    return res[:n_out], outs


def _inproj(x, g_mix, w_in, hosted=None):
    s, d = x.shape
    n = w_in.shape[1]
    tm = min(TM_PROJ, s)
    nt = s // tm

    def body(x_ref, g_ref, w_ref, u_ref):
        xv = x_ref[...]
        r = lax.rsqrt(_rowmean(xv * xv) + EPS)
        u_ref[...] = _dot((xv * r * g_ref[...]).astype(BF16), w_ref[...])

    def stages():
        i = pl.program_id(0)
        return i == 0, i == max(nt - 3, 0), i == nt - 1

    return _call(
        body, hosted, stages, grid=(nt,), name="inproj",
        in_specs=[pl.BlockSpec((tm, d), lambda i: (i, 0)), _const_spec((1, d)), _const_spec((d, n))],
        out_specs=[pl.BlockSpec((tm, n), lambda i: (i, 0))], out_shape=[SDS((s, n), F32)], scratch_shapes=[],
        args=(x, g_mix, w_in), sem=("arbitrary",))


def _mixer_fwd(u, x, sp_, w_out, hosted=None):
    s, din = u.shape
    d = x.shape[1]
    lw = din // 3
    tm = min(TM_MIX, s)
    nb = lw // GATE_BLOCK

    def body(u_ref, halo_ref, x_ref, cw_ref, cb_ref, ga_ref, gx_ref, ba_ref, bx_ref, lam_ref, pw_ref, pb_ref,
             ps_ref, gl_ref, gp_ref, wout_ref, h_ref, hres_ref, saved_ref, pooled_ref,
             gw_s, a_s, b_s, carry_s):
        i = pl.program_id(0)

        @pl.when(i == 0)
        def _():
            _build_gate_blocks(ga_ref, gx_ref, gw_s)
            carry_s[...] = jnp.zeros_like(carry_s)

        uv = u_ref[...]
        hal = jnp.where(i > 0, halo_ref[...], 0.0)
        f = _mixer_recompute(uv, hal, i * tm, cw_ref[...], cb_ref[...], gw_s, ba_ref[...], bx_ref[...],
                             lam_ref[...], pw_ref, pb_ref[...], ps_ref[...])
        for k, name in enumerate(MIX_SAVED):
            saved_ref[k] = f[name]
        pooled_ref[...] = jnp.concatenate(f["pooled"], axis=1).astype(BF16)
        bb = f["mult"] * (f["ig"] * f["xc"])
        a1, b1 = _scan_level1(f["a"], bb, reverse=False)
        a_s[...] = a1
        b_s[...] = b1
        carry_s[...] = _scan_level2(a_s, b_s, h_ref, carry_s[...], reverse=False)
        y_lru = h_ref[...] * f["ge"]
        rl = lax.rsqrt(_rowmean(y_lru * y_lru) + EPS)
        yp = f["y_pool"]
        rp = lax.rsqrt(_rowmean(yp * yp) + EPS)
        yn = jnp.concatenate([y_lru * rl * gl_ref[...], yp * rp * gp_ref[...]], axis=1).astype(BF16)
        hres_ref[...] = x_ref[...] + _dot(yn, wout_ref[...])

    small = [sp_[k] for k in ("conv_w", "conv_b", "gate_a_w", "gate_x_w", "gate_a_b", "gate_x_b", "lru_lambda",
                              "pool_w", "pool_b", "pool_scale", "norm_lru_g", "norm_pool_g")]
    nt = s // tm

    def stages():
        i = pl.program_id(0)
        return i == 0, i == max(nt - 3, 0), i == nt - 1

    return _call(
        body, hosted, stages, grid=(nt,), name="mixer_fwd",
        in_specs=[pl.BlockSpec((tm, din), lambda i: (i, 0)),
                  pl.BlockSpec((HALO, din), lambda i: (jnp.maximum(i * (tm // HALO) - 1, 0), 0)),
                  pl.BlockSpec((tm, d), lambda i: (i, 0))]
        + [_const_spec(a.shape) for a in small] + [_const_spec(w_out.shape)],
        out_specs=[pl.BlockSpec((tm, lw), lambda i: (i, 0)), pl.BlockSpec((tm, d), lambda i: (i, 0)),
                   pl.BlockSpec((len(MIX_SAVED), tm, lw), lambda i: (0, i, 0)),
                   pl.BlockSpec((tm, lw), lambda i: (i, 0))],
        out_shape=[SDS((s, lw), F32), SDS((s, d), F32), SDS((len(MIX_SAVED), s, lw), F32),
                   SDS((s, lw), BF16)],
        scratch_shapes=[pltpu.VMEM((nb, GATE_BLOCK, 2 * GATE_BLOCK), BF16), pltpu.VMEM((tm, lw), F32),
                        pltpu.VMEM((tm, lw), F32), pltpu.VMEM((SUBLANES, lw), F32)],
        args=(u, u, x, *small, w_out), sem=("arbitrary",))


def _row_chunks(tm):
    rc = tm // FFN_ROW_CHUNKS
    return [slice(q * rc, (q + 1) * rc) for q in range(FFN_ROW_CHUNKS)]


def _ffn_up(hres1, g_ffn, w1, w3):
    s, d = hres1.shape
    nj, fc, _ = w1.shape
    tm = min(TM_FFN, s)

    def body(h_ref, gf_ref, w1_ref, w3_ref, h2_ref, a1_ref, a3_ref, ff_ref):
        hv = h_ref[...]
        r = lax.rsqrt(_rowmean(hv * hv) + EPS)
        h2_ref[...] = (hv * r * gf_ref[...]).astype(BF16)
        h2 = h2_ref[...]
        for j in range(nj):
            a1 = _dot_nt(h2, w1_ref[j])
            a3 = _dot_nt(h2, w3_ref[j])
            a1_ref[j] = a1.astype(BF16)
            a3_ref[j] = a3.astype(BF16)
            ff_ref[j] = ((a1 * _sigmoid(a1)) * a3).astype(BF16)

    wspec = _const_spec(w1.shape)
    aspec = pl.BlockSpec((nj, tm, fc), lambda i: (0, i, 0))
    return pl.pallas_call(
        body, grid=(s // tm,), name="ffn_up",
        in_specs=[pl.BlockSpec((tm, d), lambda i: (i, 0)), _const_spec((1, d)), wspec, wspec],
        out_specs=[pl.BlockSpec((tm, d), lambda i: (i, 0)), aspec, aspec, aspec],
        out_shape=[SDS((s, d), BF16)] + [SDS((nj, s, fc), BF16)] * 3,
        compiler_params=_cp(("parallel",)))(hres1, g_ffn, w1, w3)


def _ffn_down(ff, hres1, target, g_fin, w2):
    s, d = hres1.shape
    nj, _, fc = ff.shape
    tm = min(TM_FFN, s)

    def body(ff_ref, h_ref, t_ref, gn_ref, w2_ref, dh_ref, dhb_ref, loss_ref, dgn_ref):
        @pl.when(pl.program_id(0) == 0)
        def _():
            loss_ref[...] = jnp.zeros_like(loss_ref)
            dgn_ref[...] = jnp.zeros_like(dgn_ref)

        gn = gn_ref[...]
        for rows in _row_chunks(tm):
            acc = _dot(ff_ref[0, rows, :], w2_ref[0])
            for j in range(1, nj):
                acc = acc + _dot(ff_ref[j, rows, :], w2_ref[j])
            hr2 = h_ref[rows, :] + acc
            r2 = lax.rsqrt(_rowmean(hr2 * hr2) + EPS)
            xh = hr2 * r2
            diff = xh * gn - t_ref[rows, :]
            tot = jnp.sum(jnp.sum(diff * diff, axis=1, keepdims=True), axis=0, keepdims=True)
            loss_ref[...] += tot * (0.5 / d)
            dout = diff * (1.0 / d)
            dgn_ref[...] += _colsum8(dout * xh)
            dh = _rms_bwd(dout, xh, r2, gn)
            dh_ref[rows, :] = dh
            dhb_ref[rows, :] = dh.astype(BF16)

    tile = pl.BlockSpec((tm, d), lambda i: (i, 0))
    return pl.pallas_call(
        body, grid=(s // tm,), name="ffn_down",
        in_specs=[pl.BlockSpec((nj, tm, fc), lambda i: (0, i, 0)), tile, tile, _const_spec((1, d)),
                  _const_spec(w2.shape)],
        out_specs=[tile, tile, pl.BlockSpec((SUBLANES, LANES), lambda i: (0, 0)),
                   pl.BlockSpec((SUBLANES, d), lambda i: (0, 0))],
        out_shape=[SDS((s, d), F32), SDS((s, d), BF16), SDS((SUBLANES, LANES), F32), SDS((SUBLANES, d), F32)],
        compiler_params=_cp(("arbitrary",)))(ff, hres1, target, g_fin, w2)


def _ffn_bwd_gate(dhb, a1, a3, w2):
    s, d = dhb.shape
    nj, _, fc = a1.shape
    tm = min(TM_FFN, s)

    def body(dhb_ref, a1_ref, a3_ref, w2_ref, da1_ref, da3_ref):
        for j in range(nj):
            for rows in _row_chunks(tm):
                dff = _dot_nt(dhb_ref[rows, :], w2_ref[j])
                a1v = a1_ref[j, rows, :].astype(F32)
                sg = _sigmoid(a1v)
                silu = a1v * sg
                da1_ref[j, rows, :] = (dff * a3_ref[j, rows, :].astype(F32)
                                       * (sg * (1.0 + (a1v - silu)))).astype(BF16)
                da3_ref[j, rows, :] = (dff * silu).astype(BF16)

    aspec = pl.BlockSpec((nj, tm, fc), lambda i: (0, i, 0))
    return pl.pallas_call(
        body, grid=(s // tm,), name="ffn_bwd_gate",
        in_specs=[pl.BlockSpec((tm, d), lambda i: (i, 0)), aspec, aspec, _const_spec(w2.shape)],
        out_specs=[aspec, aspec], out_shape=[SDS((nj, s, fc), BF16)] * 2,
        compiler_params=_cp(("parallel",)))(dhb, a1, a3, w2)


def _ffn_bwd_down(da1, da3, dh, hres1, g_ffn, w1, w3, hosted=None):
    s, d = hres1.shape
    nj, _, fc = da1.shape
    tm = min(TM_FFN, s)
    nt = s // tm

    def body(da1_ref, da3_ref, dh_ref, h_ref, gf_ref, w1_ref, w3_ref, dhr_ref, dgf_ref):
        @pl.when(pl.program_id(0) == 0)
        def _():
            dgf_ref[...] = jnp.zeros_like(dgf_ref)

        gf = gf_ref[...]
        for rows in _row_chunks(tm):
            dh2 = None
            for j in range(nj):
                part = _dot(da1_ref[j, rows, :], w1_ref[j]) + _dot(da3_ref[j, rows, :], w3_ref[j])
                dh2 = part if dh2 is None else dh2 + part
            hv = h_ref[rows, :]
            r = lax.rsqrt(_rowmean(hv * hv) + EPS)
            xh = hv * r
            dgf_ref[...] += _colsum8(dh2 * xh)
            dhr_ref[rows, :] = dh_ref[rows, :] + _rms_bwd(dh2, xh, r, gf)

    tile = pl.BlockSpec((tm, d), lambda i: (i, 0))
    aspec = pl.BlockSpec((nj, tm, fc), lambda i: (0, i, 0))
    wspec = _const_spec(w1.shape)

    def stages():
        i = pl.program_id(0)
        return i == 0, i == max(nt - 2, 0), i == nt - 1

    return _call(
        body, hosted, stages, grid=(nt,), name="ffn_bwd_down",
        in_specs=[aspec, aspec, tile, tile, _const_spec((1, d)), wspec, wspec],
        out_specs=[tile, pl.BlockSpec((SUBLANES, d), lambda i: (0, 0))],
        out_shape=[SDS((s, d), F32), SDS((SUBLANES, d), F32)],
        scratch_shapes=[], args=(da1, da3, dh, hres1, g_ffn, w1, w3), sem=("arbitrary",))


def _ffn_wgrad(h2, dhb, ff, da1, da3):
    s, d = h2.shape
    _, _, fc = ff.shape
    tm = min(TM_WGRAD, s)

    def body(h2_ref, dhb_ref, ff_ref, da1_ref, da3_ref, dw1_ref, dw3_ref, dw2_ref):
        @pl.when(pl.program_id(1) == 0)
        def _():
            dw1_ref[...] = jnp.zeros_like(dw1_ref)
            dw3_ref[...] = jnp.zeros_like(dw3_ref)
            dw2_ref[...] = jnp.zeros_like(dw2_ref)

        h2v = h2_ref[...]
        dw1_ref[0] += _dot_tn(da1_ref[0], h2v)
        dw3_ref[0] += _dot_tn(da3_ref[0], h2v)
        dw2_ref[0] += _dot_tn(ff_ref[0], dhb_ref[...])

    wspec = pl.BlockSpec((1, fc, d), lambda j, i: (j, 0, 0))
    return pl.pallas_call(
        body, grid=(N_CHIPS, s // tm), name="ffn_wgrad",
        in_specs=[pl.BlockSpec((tm, d), lambda j, i: (i, 0)), pl.BlockSpec((tm, d), lambda j, i: (i, 0))]
        + [pl.BlockSpec((1, tm, fc), lambda j, i: (j, i, 0))] * 3,
        out_specs=[wspec] * 3, out_shape=[SDS((N_CHIPS, fc, d), F32)] * 3,
        compiler_params=_cp(("parallel", "arbitrary")))(h2, dhb, ff, da1, da3)


def _mixer_bwd(u, saved, pooled, h, dhres1, sp_, w_out, hosted=None):
    s, din = u.shape
    d = dhres1.shape[1]
    lw = din // 3
    tm = min(TM_MIX, s)
    nt = s // tm
    nb = lw // GATE_BLOCK
    hd = sp_["gate_a_w"].shape[1]

    def body(ul_ref, saved_ref, pooled_ref, h_ref, hhalo_ref, dhr_ref, cw_ref, cb_ref, ga_ref, gx_ref, ba_ref,
             bx_ref, lam_ref, pw_ref, pb_ref, ps_ref, gl_ref, gp_ref, wout_ref, du_ref, slab_ref, dwout_ref,
             gw_s, a_s, b_s, e_s, ecarry_s, dxc_s, q_s, vec_s, cwacc_s, dgw_s, dpw_s):
        i = pl.program_id(0)
        tile = nt - 1 - i

        @pl.when(i == 0)
        def _():
            _build_gate_blocks(ga_ref, gx_ref, gw_s)
            for ref in (ecarry_s, dxc_s, q_s, vec_s, cwacc_s, dgw_s, dpw_s, dwout_ref):
                ref[...] = jnp.zeros_like(ref)

        cw = cw_ref[...]
        lam = lam_ref[...]
        ps = ps_ref[...]
        f = {name: saved_ref[k] for k, name in enumerate(MIX_SAVED)}
        f["mult"] = jnp.sqrt(jnp.maximum(f["m2raw"], 1e-12))
        f["sp"] = _softplus_neg(lam)
        f["xcb"] = f["xc"].astype(BF16)
        pooled = pooled_ref[...]
        row = lax.broadcasted_iota(jnp.int32, (tm, LANES), 0) + tile * tm
        f["invs"] = [1.0 / jnp.minimum(row + 1, w).astype(F32) for w in POOL_WINDOWS]
        f["z"] = jnp.concatenate(
            [_dot(pooled[:, g * LANES:(g + 1) * LANES], pw_ref[g].astype(BF16))
             for g in range(len(POOL_WINDOWS))], axis=1) + pb_ref[...]
        f["y_pool"] = f["z"] * ps
        u_l = ul_ref[...]
        hv = h_ref[...]
        h_prev = _shift_down(jnp.where(tile > 0, hhalo_ref[...], 0.0), hv, 1)
        y_lru = hv * f["ge"]
        rl = lax.rsqrt(_rowmean(y_lru * y_lru) + EPS)
        yp = f["y_pool"]
        rp = lax.rsqrt(_rowmean(yp * yp) + EPS)
        xh_l = y_lru * rl
        xh_p = yp * rp

        dhrb = dhr_ref[...].astype(BF16)
        dyn = _dot_nt(dhrb, wout_ref[...])
        yn = jnp.concatenate([xh_l * gl_ref[...], xh_p * gp_ref[...]], axis=1).astype(BF16)
        dwout_ref[...] += _dot_tn(yn, dhrb)
        d_nl, d_np = dyn[:, :lw], dyn[:, lw:]
        vec = {}
        vec[ROW_GL] = _colsum8(d_nl * xh_l)
        vec[ROW_GP] = _colsum8(d_np * xh_p)
        d_ylru = _rms_bwd(d_nl, xh_l, rl, gl_ref[...])
        d_ypool = _rms_bwd(d_np, xh_p, rp, gp_ref[...])

        vec[ROW_PS] = _colsum8(d_ypool * f["z"])
        dz = d_ypool * ps
        vec[ROW_PB] = _colsum8(dz)
        dzb = dz.astype(BF16)
        dup = []
        for gi, w in enumerate(POOL_WINDOWS):
            sl = slice(gi * LANES, (gi + 1) * LANES)
            dpw_s[:, sl] += _dot_tn(pooled[:, sl], dzb[:, sl])
            dpool = _dot_nt(dzb[:, sl], pw_ref[gi].astype(BF16))
            q = dpool * f["invs"][gi]
            e = jnp.concatenate([q, q_s[:, sl]], axis=0)
            k = 1
            while k < w:
                e = e + pltpu.roll(e, tm + HALO - k, 0)
                k *= 2
            dup.append(e[:tm] - dpool)
            q_s[:, sl] = q[:HALO]

        d_hout = d_ylru * f["ge"]
        d_ug = d_ylru * hv * f["dge"]
        a = f["a"]
        a1, b1 = _scan_level1(a, a * d_hout, reverse=True)
        a_s[...] = a1
        b_s[...] = b1
        e_next = ecarry_s[...]
        ecarry_s[...] = _scan_level2(a_s, b_s, e_s, e_next, reverse=True)
        sv = d_hout + _shift_up(e_s[...], e_next, 1)
        d_a = sv * h_prev
        mult, ig, xc, r = f["mult"], f["ig"], f["xc"], f["r"]
        d_mult = sv * (ig * xc)
        d_ig = sv * mult * xc
        d_xc = sv * mult * ig
        d_la = d_a * a + jnp.where(f["m2raw"] > 1e-12, d_mult * (-(a * a) / mult), 0.0)
        d_r = d_la * (-LRU_C * f["sp"])
        vec[ROW_LAM] = _colsum8(d_la * (-LRU_C * r))
        d_pr = d_r * r * (1.0 - r)
        d_pi = d_ig * ig * (1.0 - ig)
        vec[ROW_BA] = _colsum8(d_pr)
        vec[ROW_BX] = _colsum8(d_pi)
        dxc_parts = []
        for b in range(nb):
            sl = slice(b * GATE_BLOCK, (b + 1) * GATE_BLOCK)
            rhs = jnp.concatenate([d_pr[:, sl], d_pi[:, sl]], axis=1).astype(BF16)
            dgw_s[b] += _dot_tn(f["xcb"][:, sl], rhs)
            dxc_parts.append(_dot_nt(rhs, gw_s[b]))
        d_xc = d_xc + jnp.concatenate(dxc_parts, axis=1)
        vec[ROW_CONV_B] = _colsum8(d_xc)
        dxc_next = dxc_s[...]
        d_ul = None
        for k in range(CONV_WIDTH):
            ahead = _shift_up(d_xc, dxc_next, CONV_WIDTH - 1 - k)
            cwacc_s[k * SUBLANES:(k + 1) * SUBLANES, :] += _colsum8(ahead * u_l)
            term = ahead * cw[k:k + 1, :]
            d_ul = term if d_ul is None else d_ul + term
        dxc_s[...] = d_xc[:SUBLANES]
        for row, val in vec.items():
            vec_s[row * SUBLANES:(row + 1) * SUBLANES, :] += val
        du_ref[...] = jnp.concatenate([d_ul, d_ug] + dup, axis=1).astype(BF16)

        @pl.when(i == nt - 1)
        def _():
            rows = []
            for row in range(ROW_GA):
                if row in (ROW_CONV_W, ROW_CONV_W + 1, ROW_CONV_W + 2, ROW_CONV_W + 3):
                    k = row - ROW_CONV_W
                    v = jnp.sum(cwacc_s[k * SUBLANES:(k + 1) * SUBLANES, :], axis=0, keepdims=True)
                elif row <= ROW_GP:
                    v = jnp.sum(vec_s[row * SUBLANES:(row + 1) * SUBLANES, :], axis=0, keepdims=True)
                    if row == ROW_LAM:
                        v = v * (-1.0 / (1.0 + jnp.exp(lam)))
                else:
                    v = jnp.zeros((1, lw), F32)
                rows.append(v)
            slab_ref[0:ROW_GA, :] = jnp.concatenate(rows, axis=0)
            lane = lax.broadcasted_iota(jnp.int32, (hd, GATE_BLOCK), 1)
            for b in range(nb):
                for off, row0 in ((0, ROW_GA), (GATE_BLOCK, ROW_GX)):
                    acc = jnp.zeros((hd, GATE_BLOCK), F32)
                    for hh in range(GATE_BLOCK // hd):
                        m = (lane >= hh * hd) & (lane < (hh + 1) * hd)
                        acc = acc + jnp.where(m, dgw_s[b, hh * hd:(hh + 1) * hd, off:off + GATE_BLOCK], 0.0)
                    slab_ref[row0:row0 + hd, b * GATE_BLOCK:(b + 1) * GATE_BLOCK] = acc
            slab_ref[ROW_PW:ROW_PW + LANES, :] = dpw_s[...]

    small = [sp_[k] for k in ("conv_w", "conv_b", "gate_a_w", "gate_x_w", "gate_a_b", "gate_x_b", "lru_lambda",
                              "pool_w", "pool_b", "pool_scale", "norm_lru_g", "norm_pool_g")]
    rev = lambda i: nt - 1 - i

    def stages():
        i = pl.program_id(0)
        return i == 0, i == max(nt - 3, 0), i == nt - 1

    return _call(
        body, hosted, stages, grid=(nt,), name="mixer_bwd",
        in_specs=[pl.BlockSpec((tm, lw), lambda i: (rev(i), 0)),
                  pl.BlockSpec((len(MIX_SAVED), tm, lw), lambda i: (0, rev(i), 0)),
                  pl.BlockSpec((tm, lw), lambda i: (rev(i), 0)),
                  pl.BlockSpec((tm, lw), lambda i: (rev(i), 0)),
                  pl.BlockSpec((SUBLANES, lw), lambda i: (jnp.maximum(rev(i) * (tm // SUBLANES) - 1, 0), 0)),
                  pl.BlockSpec((tm, d), lambda i: (rev(i), 0))]
        + [_const_spec(a.shape) for a in small] + [_const_spec(w_out.shape)],
        out_specs=[pl.BlockSpec((tm, din), lambda i: (rev(i), 0)),
                   pl.BlockSpec((MIX_SLAB_ROWS, SLAB_W), lambda i: (0, 0)), pl.BlockSpec((d, d), lambda i: (0, 0))],
        out_shape=[SDS((s, din), BF16), SDS((MIX_SLAB_ROWS, SLAB_W), F32), SDS((d, d), F32)],
        scratch_shapes=[pltpu.VMEM((nb, GATE_BLOCK, 2 * GATE_BLOCK), BF16),
                        pltpu.VMEM((tm, lw), F32), pltpu.VMEM((tm, lw), F32), pltpu.VMEM((tm, lw), F32),
                        pltpu.VMEM((SUBLANES, lw), F32), pltpu.VMEM((SUBLANES, lw), F32),
                        pltpu.VMEM((HALO, lw), F32), pltpu.VMEM((ROW_GA * SUBLANES, lw), F32),
                        pltpu.VMEM((CONV_WIDTH * SUBLANES, lw), F32),
                        pltpu.VMEM((nb, GATE_BLOCK, 2 * GATE_BLOCK), F32), pltpu.VMEM((LANES, lw), F32)],
        args=(u, saved, pooled, h, h, dhres1, *small, w_out), sem=("arbitrary",))


def _inproj_bwd(x, du, dhres1, g_mix, w_in, hosted=None):
    s, d = x.shape
    n = w_in.shape[1]
    nc = n // N_CHIPS
    tm = min(TM_PROJ, s)
    nt = s // tm

    def body(x_ref, du_ref, dhr_ref, g_ref, w_ref, gx_ref, dwin_ref, dg_ref):
        i = pl.program_id(0)

        @pl.when(i == 0)
        def _():
            dwin_ref[...] = jnp.zeros_like(dwin_ref)
            dg_ref[...] = jnp.zeros_like(dg_ref)

        xv = x_ref[...]
        g = g_ref[...]
        r = lax.rsqrt(_rowmean(xv * xv) + EPS)
        xh = xv * r
        h1 = (xh * g).astype(BF16)
        duv = du_ref[...]
        dh1 = _dot_nt(duv, w_ref[...])
        dg_ref[...] += _colsum8(dh1 * xh)
        gx_ref[...] = dhr_ref[...] + _rms_bwd(dh1, xh, r, g)
        for jj in range(N_CHIPS):
            dwin_ref[jj] += _dot_tn(h1, duv[:, jj * nc:(jj + 1) * nc])

    def stages():
        i = pl.program_id(0)
        return i == 0, i == max(nt - 3, 0), i == nt - 1

    return _call(
        body, hosted, stages, grid=(nt,), name="inproj_bwd",
        in_specs=[pl.BlockSpec((tm, d), lambda i: (i, 0)), pl.BlockSpec((tm, n), lambda i: (i, 0)),
                  pl.BlockSpec((tm, d), lambda i: (i, 0)), _const_spec((1, d)), _const_spec((d, n))],
        out_specs=[pl.BlockSpec((tm, d), lambda i: (i, 0)), pl.BlockSpec((N_CHIPS, d, nc), lambda i: (0, 0, 0)),
                   pl.BlockSpec((SUBLANES, d), lambda i: (0, 0))],
        out_shape=[SDS((s, d), F32), SDS((N_CHIPS, d, nc), F32), SDS((SUBLANES, d), F32)],
        scratch_shapes=[], args=(x, du, dhres1, g_mix, w_in), sem=("arbitrary",))


def _place():
    x, y, c = lax.axis_index("x"), lax.axis_index("y"), lax.axis_index("c")
    return x, y, c


def _other_chips(x, y):
    return [(1 - x, y), (x, 1 - y), (1 - x, 1 - y)]


ANY = pl.BlockSpec(memory_space=pl.ANY)
VMEM_SPEC = pl.BlockSpec(memory_space=pltpu.VMEM)

_GATHERED = {"w_in": "cols", "w_out": "major", "ffn_w1": "major", "ffn_w3": "major", "ffn_w2": "major"}
_BIG = ("w_in", "w_out", "ffn_w1", "ffn_w3", "ffn_w2")


def _gather_weights(shards, conv_w, n_remote):
    n = len(shards)
    full_shapes = []
    for name, sh in zip(_BIG, shards):
        r, cdim = sh.shape
        if _GATHERED[name] == "cols":
            assert cdim % LANES == 0
            full_shapes.append((r, cdim * N_CHIPS))
        else:
            full_shapes.append((N_CHIPS, r, cdim))

    def region(ref, name, sh, jj, cc):
        r, cdim = sh
        rows = pl.ds(0, r) if cc is None else pl.ds(pl.multiple_of(cc * (r // 2), 16), r // 2)
        if _GATHERED[name] == "cols":
            return ref.at[rows, pl.ds(pl.multiple_of(jj * cdim, LANES), cdim)]
        return ref.at[jj, rows, :]

    def staged(ref, sh, cc):
        r = sh[0]
        return ref.at[pl.ds(pl.multiple_of(cc * (r // 2), 16), r // 2), :]

    def body(*refs):
        ins, cw_in = refs[:n], refs[n]
        outs, cw_out = refs[n + 1:2 * n + 1], refs[2 * n + 1]
        stage = refs[2 * n + 2:3 * n + 2]
        cw_stage, lsem, ssem, rsem, fssem, frsem, cssem, crsem = refs[3 * n + 2:]
        x, y, c = _place()
        j = 2 * x + y
        chips = _other_chips(x, y)
        for w in range(n_remote):
            stage[w][...] = ins[w][...].astype(BF16)
        cw_stage[...] = jnp.zeros_like(cw_stage)
        cw_stage[0:CONV_WIDTH, :] = cw_in[...]
        shs = [s_.shape for s_ in shards]
        local = [pltpu.make_async_copy(stage[w], region(outs[w], _BIG[w], shs[w], j, None), lsem.at[w])
                 for w in range(n)]
        local.append(pltpu.make_async_copy(cw_stage, cw_out.at[j], lsem.at[n]))
        sends = []
        for k, (px, py) in enumerate(chips):
            for w in range(n_remote):
                sends.append(pltpu.make_async_remote_copy(
                    src_ref=staged(stage[w], shs[w], c), dst_ref=region(outs[w], _BIG[w], shs[w], j, c),
                    send_sem=ssem.at[k * n + w], recv_sem=rsem.at[k * n + w], device_id=(px, py, c),
                    device_id_type=MESH))
            sends.append(pltpu.make_async_remote_copy(
                src_ref=cw_stage, dst_ref=cw_out.at[j], send_sem=cssem.at[k], recv_sem=crsem.at[k],
                device_id=(px, py, c), device_id_type=MESH))
        for cp in sends:
            cp.start()
        for w in range(n_remote, n):
            stage[w][...] = ins[w][...].astype(BF16)
        for cp in local:
            cp.start()
        fwd = []
        for k, (px, py) in enumerate(chips):
            jk = 2 * px + py
            for w in range(n_remote):
                reg = region(outs[w], _BIG[w], shs[w], jk, c)
                pltpu.make_async_remote_copy(src_ref=reg, dst_ref=reg, send_sem=ssem.at[k * n + w],
                                             recv_sem=rsem.at[k * n + w], device_id=(px, py, c),
                                             device_id_type=MESH).wait_recv()
                cp = pltpu.make_async_remote_copy(src_ref=reg, dst_ref=reg, send_sem=fssem.at[k * n + w],
                                                  recv_sem=frsem.at[k * n + w], device_id=(x, y, 1 - c),
                                                  device_id_type=MESH)
                cp.start()
                fwd.append(cp)
            pltpu.make_async_remote_copy(src_ref=cw_stage, dst_ref=cw_out.at[jk], send_sem=cssem.at[k],
                                         recv_sem=crsem.at[k], device_id=(px, py, c),
                                         device_id_type=MESH).wait_recv()
        for k, (px, py) in enumerate(chips):
            jk = 2 * px + py
            for w in range(n_remote):
                reg = region(outs[w], _BIG[w], shs[w], jk, 1 - c)
                pltpu.make_async_remote_copy(src_ref=reg, dst_ref=reg, send_sem=fssem.at[k * n + w],
                                             recv_sem=frsem.at[k * n + w], device_id=(x, y, 1 - c),
                                             device_id_type=MESH).wait_recv()
        for cp in sends + fwd:
            cp.wait_send()
        for cp in local:
            cp.wait()

    nsem = 3 * n
    return pl.pallas_call(
        body, name="gather_first",
        in_specs=[VMEM_SPEC] * (n + 1), out_specs=[ANY] * (n + 1),
        out_shape=[SDS(fs, BF16) for fs in full_shapes] + [SDS((N_CHIPS, SUBLANES, LANES), F32)],
        scratch_shapes=[pltpu.VMEM(s_.shape, BF16) for s_ in shards] + [pltpu.VMEM((SUBLANES, LANES), F32)]
        + [pltpu.SemaphoreType.DMA((n + 1,))] + [pltpu.SemaphoreType.DMA((nsem,))] * 4
        + [pltpu.SemaphoreType.DMA((3,))] * 2,
        compiler_params=_cp())(*shards, conv_w)


def _start_all(make):
    def f(ins, outs, sems):
        for cp in make(ins, outs, sems):
            cp.start()
    return f


def _wait_all(make):
    def f(ins, outs, sems):
        for cp in make(ins, outs, sems):
            cp.wait()
    return f


def _ffn_gather_hosted(arrs):
    n = len(arrs)

    def make(outs, sems):
        ssem, rsem, fs, fr = sems
        x, y, c = _place()
        j = 2 * x + y

        def reg(w, jj, cc):
            hr = arrs[w].shape[1] // 2
            return outs[w].at[jj, pl.ds(pl.multiple_of(cc * hr, 16), hr), :]

        def rc(w, jj, cc, s_sem, r_sem, dev):
            return pltpu.make_async_remote_copy(src_ref=reg(w, jj, cc), dst_ref=reg(w, jj, cc), send_sem=s_sem,
                                                recv_sem=r_sem, device_id=dev, device_id_type=MESH)

        sends, recvs, fwds, frecvs = [], [], [], []
        for k, (px, py) in enumerate(_other_chips(x, y)):
            jk = 2 * px + py
            for w in range(n):
                q = k * n + w
                sends.append(rc(w, j, c, ssem.at[q], rsem.at[q], (px, py, c)))
                recvs.append(rc(w, jk, c, ssem.at[q], rsem.at[q], (px, py, c)))
                fwds.append(rc(w, jk, c, fs.at[q], fr.at[q], (x, y, 1 - c)))
                frecvs.append(rc(w, jk, 1 - c, fs.at[q], fr.at[q], (x, y, 1 - c)))
        return sends, recvs, fwds, frecvs

    def start(ins, outs, sems):
        for cp in make(outs, sems)[0]:
            cp.start()

    def mid(ins, outs, sems):
        _, recvs, fwds, _ = make(outs, sems)
        for r, f in zip(recvs, fwds):
            r.wait_recv()
            f.start()

    def finish(ins, outs, sems):
        sends, _, fwds, frecvs = make(outs, sems)
        for r in frecvs:
            r.wait_recv()
        for cp in sends + fwds:
            cp.wait_send()

    return _Hosted(arrs, [SDS(a.shape, a.dtype) for a in arrs], [3 * n] * 4, start, finish, mid=mid,
                   aliases={w: w for w in range(n)})


def _rs_sibling_hosted(arrs):
    n = len(arrs)

    def make(ins, outs, sems):
        x, y, c = _place()
        cps = []
        for w in range(n):
            hr = arrs[w].shape[1] // 2
            src = ins[w].at[:, pl.ds(pl.multiple_of((1 - c) * hr, SUBLANES), hr), :]
            cps.append(pltpu.make_async_remote_copy(src_ref=src, dst_ref=outs[w], send_sem=sems[0].at[w],
                                                    recv_sem=sems[1].at[w], device_id=(x, y, 1 - c),
                                                    device_id_type=MESH))
        return cps

    return _Hosted(arrs, [SDS((a.shape[0], a.shape[1] // 2, a.shape[2]), F32) for a in arrs], [n, n],
                   _start_all(make), _wait_all(make))


def _rs_chips_hosted(parts):
    n = len(parts)

    def make(ins, outs, sems):
        x, y, c = _place()
        j = 2 * x + y
        cps = []
        for k, (px, py) in enumerate(_other_chips(x, y)):
            jk = 2 * px + py
            for w in range(n):
                cps.append(pltpu.make_async_remote_copy(
                    src_ref=ins[w].at[jk], dst_ref=outs[w].at[j], send_sem=sems[0].at[k * n + w],
                    recv_sem=sems[1].at[k * n + w], device_id=(px, py, c), device_id_type=MESH))
        return cps

    return _Hosted(parts, [SDS(p.shape, p.dtype) for p in parts], [3 * n, 3 * n], _start_all(make), _wait_all(make))


def _rs_swap_hosted(halves):
    n = len(halves)

    def make(ins, outs, sems):
        x, y, c = _place()
        return [pltpu.make_async_remote_copy(src_ref=ins[w], dst_ref=outs[w], send_sem=sems[0].at[w],
                                             recv_sem=sems[1].at[w], device_id=(x, y, 1 - c), device_id_type=MESH)
                for w in range(n)]

    return _Hosted(halves, [SDS(h.shape, F32) for h in halves], [n, n], _start_all(make), _wait_all(make))


HBM_SPEC = pl.BlockSpec(memory_space=pltpu.HBM)
SEM_SPEC = pl.BlockSpec(memory_space=pltpu.SEMAPHORE)
_EFFECT = pltpu.SideEffectType.DATAFLOW_SIDE_EFFECTING


def _split_start(h, name):
    n_in, n_out, ns = len(h.ins), len(h.out_shapes), len(h.sems)
    ins = [pltpu.with_memory_space_constraint(a, pltpu.HBM) for a in h.ins]
    lands = [pltpu.with_memory_space_constraint(lax.empty(o.shape, o.dtype), pltpu.HBM) for o in h.out_shapes]

    def body(*refs):
        i_refs, l_refs = refs[:n_in], refs[n_in:n_in + n_out]
        s_refs = refs[n_in + n_out:n_in + n_out + ns]
        token = refs[-1]
        h.start(i_refs, l_refs, s_refs)
        token[...] = jnp.zeros_like(token)

    res = pl.pallas_call(
        body, name=name, in_specs=[HBM_SPEC] * (n_in + n_out),
        out_specs=[SEM_SPEC] * ns + [HBM_SPEC] * n_out + [VMEM_SPEC],
        out_shape=[pltpu.SemaphoreType.DMA((k,)) for k in h.sems]
        + [pltpu.HBM(o.shape, o.dtype) for o in h.out_shapes] + [SDS((SUBLANES, LANES), F32)],
        input_output_aliases={n_in + k: ns + k for k in range(n_out)},
        compiler_params=pltpu.CompilerParams(has_side_effects=_EFFECT))(*ins, *lands)
    return list(res[:ns]) + ins + list(res[ns:-1]), res[-1]


def _split_wait(h, state, after, name):
    n_in, n_out, ns = len(h.ins), len(h.out_shapes), len(h.sems)
    sems, bufs = state[:ns], state[ns:]

    def body(*refs):
        i_refs, l_refs = refs[:n_in], refs[n_in:n_in + n_out]
        s_refs = refs[n_in + n_out:n_in + n_out + ns]
        h.finish(i_refs, l_refs, s_refs)

    res = pl.pallas_call(
        body, name=name, in_specs=[HBM_SPEC] * (n_in + n_out) + [SEM_SPEC] * ns + [ANY],
        out_specs=[HBM_SPEC] * n_out,
        out_shape=[pltpu.HBM(b.shape, b.dtype) for b in bufs[n_in:]],
        input_output_aliases={n_in + k: k for k in range(n_out)},
        compiler_params=pltpu.CompilerParams(has_side_effects=_EFFECT))(*bufs, *sems, after)
    return list(res)


def _run_comm(hosted, name):
    return _call(lambda: None, hosted, None, name=name, grid=(), in_specs=[], out_specs=[], out_shape=[],
                 scratch_shapes=[], args=(), sem=None)[1]


def _row_tile(rows, cols, n_arrays):
    budget = 24 * 1024 * 1024 // (2 * 4 * n_arrays * cols)
    best = SUBLANES
    for t in range(SUBLANES, rows + 1, SUBLANES):
        if rows % t == 0 and t <= budget:
            best = t
    return best


def _place_index(which):
    x, y, c = _place()
    v = c if which == "c" else 2 * x + y
    return jnp.reshape(v, (1,)).astype(jnp.int32)


def _add_own_half(full, recv, name, wire=BF16):
    nsh, rows, cols = full.shape
    hr = rows // 2
    t = _row_tile(hr, cols, 4)
    nt = hr // t

    def body(c_ref, a_ref, b_ref, o_ref, ob_ref):
        v = a_ref[...] + b_ref[...]
        o_ref[...] = v
        ob_ref[...] = v.astype(wire)

    half = pl.BlockSpec((1, t, cols), lambda s_, i, c_ref: (s_, i, 0))
    return pl.pallas_call(
        body, name=name,
        grid_spec=pltpu.PrefetchScalarGridSpec(
            num_scalar_prefetch=1, grid=(nsh, nt),
            in_specs=[pl.BlockSpec((1, t, cols), lambda s_, i, c_ref: (s_, c_ref[0] * nt + i, 0)), half],
            out_specs=[half, half]),
        out_shape=[SDS((nsh, hr, cols), F32), SDS((nsh, hr, cols), wire)],
        compiler_params=_cp(("parallel", "parallel")))(_place_index("c"), full, recv)


def _sum_chips(own, recv, name):
    nsh, hr, cols = own.shape
    t = _row_tile(hr, cols, 6)

    def body(j_ref, own_ref, *rest):
        r_refs, o_ref = rest[:nsh], rest[nsh]
        j = j_ref[0]
        mine = own_ref[0]
        parts = [jnp.where(j == k, mine, r_refs[k][0].astype(F32)) for k in range(nsh)]
        o_ref[...] = ((parts[0] + parts[1]) + parts[2]) + parts[3]

    def other(k):
        return pl.BlockSpec((1, t, cols), lambda i, j_ref: (jnp.where(j_ref[0] == k, (k + 1) % nsh, k), i, 0))

    return pl.pallas_call(
        body, name=name,
        grid_spec=pltpu.PrefetchScalarGridSpec(
            num_scalar_prefetch=1, grid=(hr // t,),
            in_specs=[pl.BlockSpec((1, t, cols), lambda i, j_ref: (j_ref[0], i, 0))]
            + [other(k) for k in range(nsh)],
            out_specs=pl.BlockSpec((t, cols), lambda i, j_ref: (i, 0))),
        out_shape=SDS((hr, cols), F32), compiler_params=_cp(("parallel",)))(_place_index("j"), own, *([recv] * nsh))


def _adamw_math(w, g, m, v):
    m = ADAM_B1 * m + (1.0 - ADAM_B1) * g
    v = ADAM_B2 * v + (1.0 - ADAM_B2) * (g * g)
    m_hat = m / (1.0 - ADAM_B1 ** ADAM_STEP)
    v_hat = v / (1.0 - ADAM_B2 ** ADAM_STEP)
    delta = -ADAM_LR * (m_hat / (jnp.sqrt(v_hat) + ADAM_EPS) + ADAM_WD * w)
    return delta, m, v


def _adamw_big(w, g_own, g_sib, m, v, name, token=None):
    _, rows, cols = w.shape
    hr = rows // 2
    t = _row_tile(hr, cols, 9)
    nth = hr // t
    if token is None:
        token = jnp.zeros((SUBLANES, LANES), F32)

    def body(c_ref, w_ref, go_ref, gs_ref, m_ref, v_ref, tok_ref, g_ref, d_ref, mo_ref, vo_ref):
        own = (pl.program_id(0) // nth) == c_ref[0]
        g = jnp.where(own, go_ref[...], gs_ref[...]) + tok_ref[0:1, 0:1]
        g_ref[0] = g
        d_ref[0], mo_ref[0], vo_ref[0] = _adamw_math(w_ref[0], g, m_ref[0], v_ref[0])

    spec = pl.BlockSpec((1, t, cols), lambda i, c_ref: (0, i, 0))
    hspec = pl.BlockSpec((t, cols), lambda i, c_ref: (i % nth, 0))
    tspec = pl.BlockSpec((SUBLANES, LANES), lambda i, c_ref: (0, 0))
    return pl.pallas_call(
        body, name=name,
        grid_spec=pltpu.PrefetchScalarGridSpec(
            num_scalar_prefetch=1, grid=(2 * nth,), in_specs=[spec, hspec, hspec, spec, spec, tspec],
            out_specs=[spec] * 4),
        out_shape=[SDS((1, rows, cols), F32)] * 4,
        compiler_params=_cp(("parallel",)))(_place_index("c"), w, g_own, g_sib, m, v, token)


def _build_slab(mix_slab, dg_mix, dg_ffn, dg_fin, loss8):
    def body(ms_ref, gm_ref, gf_ref, gn_ref, loss_ref, out_ref):
        rows = []
        for ref in (gm_ref, gf_ref, gn_ref):
            v = jnp.sum(ref[...], axis=0, keepdims=True)
            rows += [v[:, :SLAB_W], v[:, SLAB_W:]]
        rows.append(jnp.concatenate([loss_ref[0:1, :]] * (SLAB_W // LANES), axis=1))
        rows.append(jnp.zeros((SLAB_ROWS - ROW_LOSS - 1, SLAB_W), F32))
        tail = jnp.concatenate(rows, axis=0)
        for k in range(N_CHIPS):
            out_ref[k, 0:MIX_SLAB_ROWS, :] = ms_ref[...]
            out_ref[k, MIX_SLAB_ROWS:SLAB_ROWS, :] = tail

    return pl.pallas_call(
        body, name="build_slab", in_specs=[VMEM_SPEC] * 5, out_specs=VMEM_SPEC,
        out_shape=SDS((N_CHIPS, SLAB_ROWS, SLAB_W), F32),
        compiler_params=_cp())(mix_slab, dg_mix, dg_ffn, dg_fin, loss8)


_SMALL_ROWS = (("conv_b", ROW_CONV_B), ("gate_a_b", ROW_BA), ("gate_x_b", ROW_BX), ("lru_lambda", ROW_LAM),
               ("pool_b", ROW_PB), ("pool_scale", ROW_PS), ("norm_lru_g", ROW_GL), ("norm_pool_g", ROW_GP))
_WIDE_ROWS = (("norm_mix_g", ROW_MIX), ("norm_ffn_g", ROW_FFN), ("final_norm_g", ROW_FIN))
_BLOCK_ROWS = (("gate_a_w", ROW_GA), ("gate_x_w", ROW_GX), ("pool_w", ROW_PW))
_SMALL_ORDER = tuple(n for n, _ in _SMALL_ROWS) + tuple(n for n, _ in _WIDE_ROWS) + tuple(
    n for n, _ in _BLOCK_ROWS) + ("conv_w",)


def _adamw_small(slab_own, slab_sib, wmv):
    names = _SMALL_ORDER
    flat = [a for nme in names for a in wmv[nme]]
    nin = len(flat)

    def body(*refs):
        own_ref, sib_ref, j_ref = refs[0], refs[1], refs[2]
        ins = refs[3:3 + nin]
        outs = refs[3 + nin:-1]
        first = j_ref[1] == 0
        slab_ref = jnp.concatenate([jnp.where(first, own_ref[...], sib_ref[...]),
                                    jnp.where(first, sib_ref[...], own_ref[...])], axis=0)
        refs[-1][...] = jnp.broadcast_to(slab_ref[ROW_LOSS:ROW_LOSS + 1, 0:LANES], (SUBLANES, LANES))
        grads = {}
        for nme, row in _SMALL_ROWS:
            grads[nme] = slab_ref[row:row + 1, :]
        for nme, row in _WIDE_ROWS:
            grads[nme] = jnp.concatenate([slab_ref[row:row + 1, :], slab_ref[row + 1:row + 2, :]], axis=1)
        full = slab_ref[ROW_CONV_W:ROW_CONV_W + CONV_WIDTH, :]
        jv = j_ref[0]
        g = jnp.zeros((CONV_WIDTH, LANES), F32)
        for jj in range(N_CHIPS):
            g = jnp.where(jv == jj, full[:, jj * LANES:(jj + 1) * LANES], g)
        grads["conv_w"] = g
        block_rows = dict(_BLOCK_ROWS)
        for idx, nme in enumerate(names):
            w_ref, m_ref, v_ref = ins[3 * idx:3 * idx + 3]
            if nme in block_rows:
                nblk, r, c = w_ref.shape
                parts = [(b, slab_ref[block_rows[nme]:block_rows[nme] + r, b * c:(b + 1) * c]) for b in range(nblk)]
            else:
                parts = [(Ellipsis, grads[nme])]
            for b, g in parts:
                delta, m, v = _adamw_math(w_ref[b], g, m_ref[b], v_ref[b])
                outs[4 * idx][b] = g
                outs[4 * idx + 1][b] = delta
                outs[4 * idx + 2][b] = m
                outs[4 * idx + 3][b] = v

    place = jnp.concatenate([_place_index("j"), _place_index("c")])
    out_shape = [SDS(wmv[nme][0].shape, F32) for nme in names for _ in range(4)] + [SDS((SUBLANES, LANES), F32)]
    res = pl.pallas_call(
        body, name="adamw_small",
        in_specs=[VMEM_SPEC, VMEM_SPEC, pl.BlockSpec(memory_space=pltpu.SMEM)] + [VMEM_SPEC] * nin,
        out_specs=[VMEM_SPEC] * len(out_shape), out_shape=out_shape,
        compiler_params=_cp())(slab_own, slab_sib, place, *flat)
    return {nme: tuple(res[4 * idx:4 * idx + 4]) for idx, nme in enumerate(names)}, res[-1]


_FFN = ("ffn_w1", "ffn_w3", "ffn_w2")
_TRANSPOSED = ("ffn_w1", "ffn_w3")


def _local_step(x, target, full, sp_, distributed):
    d = x.shape[1]
    (u,), got = _inproj(x, sp_["norm_mix_g"], full["w_in"],
                        [_ffn_gather_hosted([full["w_out"]])] if distributed else None)
    w_out = (got[0][0] if distributed else full["w_out"]).reshape(d, d)
    gather = [_ffn_gather_hosted([full[n] for n in _FFN])] if distributed else None
    (h, hres1, saved, pooled), got = _mixer_fwd(u, x, sp_, w_out, gather)
    w1, w3, w2 = got[0] if distributed else [full[n] for n in _FFN]
    h2, a1, a3, ff = _ffn_up(hres1, sp_["norm_ffn_g"], w1, w3)
    dh, dhb, loss8, dg_fin = _ffn_down(ff, hres1, target, sp_["final_norm_g"], w2)
    da1, da3 = _ffn_bwd_gate(dhb, a1, a3, w2)
    dws = list(_ffn_wgrad(h2, dhb, ff, da1, da3))
    rs1 = [_rs_sibling_hosted(dws)] if distributed else None
    (dhres1, dg_ffn), got = _ffn_bwd_down(da1, da3, dh, hres1, sp_["norm_ffn_g"], w1, w3, rs1)
    rs2 = None
    if distributed:
        pairs = [_add_own_half(a, r, "add_half_" + n) for n, a, r in zip(_FFN, dws, got[0])]
        rs2 = [_rs_chips_hosted([pb for _, pb in pairs])]
    (du, mix_slab, dwout), got = _mixer_bwd(u, saved, pooled, h, dhres1, sp_, w_out, rs2)
    g_mix = sp_["norm_mix_g"]
    if distributed:
        fin = [_sum_chips(pairs[k][0], got[0][k], "sum_chips_" + n) for k, n in enumerate(_FFN)]
        swap = _rs_swap_hosted(fin)
        state, token = _split_start(swap, "ffn_swap_start")
        g_mix = g_mix + token[0:1, 0:1]
    (gx, dwin, dg_mix), _ = _inproj_bwd(x, du, dhres1, g_mix, full["w_in"])
    if distributed:
        sib = _split_wait(swap, state, dg_mix, "ffn_swap_wait")
    big = {"w_in": dwin, "w_out": dwout.reshape(N_CHIPS, d // N_CHIPS, d)}
    for k, n in enumerate(_FFN):
        big[n] = (fin[k], sib[k]) if distributed else dws[k]
    return gx, big, (mix_slab, dg_mix, dg_ffn, dg_fin, loss8)


_SMALL_LAYOUT = {
    "gate_a_w": (lambda a: a[0], lambda a: a[None]),
    "gate_x_w": (lambda a: a[0], lambda a: a[None]),
    "pool_w": (lambda a: a[0], lambda a: a[None]),
    "conv_w": (lambda a: a[0], lambda a: a[None]),
    "final_norm_g": (lambda a: a[None], lambda a: a[0]),
}

_WEIGHTS = ("norm_mix_g", "w_in", "conv_w", "conv_b", "gate_a_w", "gate_a_b", "gate_x_w", "gate_x_b", "lru_lambda",
            "pool_w", "pool_b", "pool_scale", "norm_lru_g", "norm_pool_g", "w_out", "norm_ffn_g", "ffn_w1",
            "ffn_w3", "ffn_w2", "final_norm_g")


def kernel(x, norm_mix_g, w_in, conv_w, conv_b, gate_a_w, gate_a_b, gate_x_w, gate_x_b, lru_lambda, pool_w, pool_b, pool_scale, norm_lru_g, norm_pool_g, w_out, norm_ffn_g, ffn_w1, ffn_w3, ffn_w2, final_norm_g, loss_target, m_norm_mix_g, m_w_in, m_conv_w, m_conv_b, m_gate_a_w, m_gate_a_b, m_gate_x_w, m_gate_x_b, m_lru_lambda, m_pool_w, m_pool_b, m_pool_scale, m_norm_lru_g, m_norm_pool_g, m_w_out, m_norm_ffn_g, m_ffn_w1, m_ffn_w3, m_ffn_w2, m_final_norm_g, v_norm_mix_g, v_w_in, v_conv_w, v_conv_b, v_gate_a_w, v_gate_a_b, v_gate_x_w, v_gate_x_b, v_lru_lambda, v_pool_w, v_pool_b, v_pool_scale, v_norm_lru_g, v_norm_pool_g, v_w_out, v_norm_ffn_g, v_ffn_w1, v_ffn_w3, v_ffn_w2, v_final_norm_g):
    loc = locals()
    w = {n: loc[n] for n in _WEIGHTS}
    m = {n: loc["m_" + n] for n in _WEIGHTS}
    v = {n: loc["v_" + n] for n in _WEIGHTS}

    def lay(nme, a):
        return _SMALL_LAYOUT[nme][0](a) if nme in _SMALL_LAYOUT else a

    def unlay(nme, a):
        return _SMALL_LAYOUT[nme][1](a) if nme in _SMALL_LAYOUT else a

    for group in (w, m, v):
        for n in _TRANSPOSED:
            group[n] = jnp.transpose(group[n], (0, 2, 1))

    gathered = _gather_weights([w[n][0] for n in _BIG], w["conv_w"][0], n_remote=1)
    full = dict(zip(_BIG, gathered[:-1]))
    cw_all = gathered[-1]
    sp_ = {n: lay(n, w[n]) for n in _SMALL_ORDER}
    sp_["conv_w"] = jnp.transpose(cw_all[:, :CONV_WIDTH, :], (1, 0, 2)).reshape(CONV_WIDTH, N_CHIPS * LANES)

    gx, big, small = _local_step(x[0], loss_target[0], full, sp_, distributed=True)

    late = ("w_in", "w_out", "slab")
    big["slab"] = _build_slab(*small)
    fin = {n: big[n][0] for n in _FFN}
    sib = {n: big[n][1] for n in _FFN}
    recv1, = _run_comm([_rs_sibling_hosted([big[n] for n in late])], "tail_sibling")
    pairs = [_add_own_half(big[n], r, "add_half_" + n, F32 if n == "slab" else BF16) for n, r in zip(late, recv1)]
    chips = _rs_chips_hosted([pb for _, pb in pairs])
    state, token = _split_start(chips, "tail_chips_start")
    out = {}
    for n in _FFN:
        out[n] = tuple(_adamw_big(w[n], fin[n], sib[n], m[n], v[n], "adamw_" + n, token))
    recv2 = _split_wait(chips, state, out[_FFN[-1]][1], "tail_chips_wait")
    for n, (p, _), r in zip(late, pairs, recv2):
        fin[n] = _sum_chips(p, r, "sum_chips_" + n)
    swapped, = _run_comm([_rs_swap_hosted([fin[n] for n in late])], "tail_swap")
    sib.update(zip(late, swapped))
    for n in late[:2]:
        out[n] = tuple(_adamw_big(w[n], fin[n], sib[n], m[n], v[n], "adamw_" + n))
    for n in _TRANSPOSED:
        out[n] = tuple(jnp.transpose(a, (0, 2, 1)) for a in out[n])
    wmv = {n: (lay(n, w[n]), lay(n, m[n]), lay(n, v[n])) for n in _SMALL_ORDER}
    res, loss = _adamw_small(fin["slab"], sib["slab"], wmv)
    for n in _SMALL_ORDER:
        out[n] = tuple(unlay(n, a) for a in res[n])
    return (loss[0, 0], gx[None]) + tuple(out[n][k] for k in range(4) for n in _WEIGHTS)
```

```python
import functools
import math

import jax
import jax.numpy as jnp
from jax import lax
from jax.experimental import pallas as pl
from jax.experimental.pallas import tpu as pltpu

F32 = jnp.float32
BF16 = jnp.bfloat16
SDS = jax.ShapeDtypeStruct
MESH = pl.DeviceIdType.MESH

EPS = 1e-6
LRU_C = 8.0
CONV_WIDTH = 4
POOL_WINDOWS = (2, 4, 8, 16)
HALO = 16
LANES = 128
SUBLANES = 8
GATE_BLOCK = 256
N_CHIPS = 4

ADAM_LR = 0.001
ADAM_B1 = 0.9
ADAM_B2 = 0.999
ADAM_EPS = 1e-08
ADAM_WD = 0.01
ADAM_STEP = 10

TM_PROJ = 512
TM_MIX = 512
TM_FFN = 512
TM_WGRAD = 2048
MIX_SAVED = ("xc", "r", "ig", "a", "m2raw", "ge", "dge")
FFN_ROW_CHUNKS = 2
VMEM_LIMIT = 56 * 1024 * 1024

SLAB_W = 512
ROW_CONV_B, ROW_CONV_W, ROW_BA, ROW_BX, ROW_LAM, ROW_PB, ROW_PS, ROW_GL, ROW_GP = 0, 1, 5, 6, 7, 8, 9, 10, 11
ROW_GA, ROW_GX, ROW_PW = 16, 80, 144
ROW_MIX, ROW_FFN, ROW_FIN, ROW_LOSS = 272, 274, 276, 278
MIX_SLAB_ROWS = 272
SLAB_ROWS = 288


def _cp(sem=None, **kw):
    if sem is not None:
        kw["dimension_semantics"] = sem
    return pltpu.CompilerParams(vmem_limit_bytes=VMEM_LIMIT, **kw)


def _const_spec(shape):
    nd = len(shape)
    return pl.BlockSpec(shape, lambda *_: (0,) * nd, pipeline_mode=pl.Buffered(1))


def _sigmoid(x):
    return 1.0 / (1.0 + jnp.exp(-x))


def _dot(a, b):
    return jnp.dot(a, b, preferred_element_type=F32)


def _dot_nt(a, b):
    return lax.dot_general(a, b, (((1,), (1,)), ((), ())), preferred_element_type=F32)


def _dot_tn(a, b):
    return lax.dot_general(a, b, (((0,), (0,)), ((), ())), preferred_element_type=F32)


def _colsum8(v):
    m, c = v.shape
    return v.reshape(m // SUBLANES, SUBLANES, c).sum(axis=0)


def _rowmean(v):
    return jnp.mean(v, axis=-1, keepdims=True)


def _rms_bwd(dy, xhat, r, g):
    dxh = dy * g
    return r * (dxh - xhat * _rowmean(dxh * xhat))


def _softplus_neg(lam):
    z = -lam
    e = jnp.exp(-jnp.abs(z))
    u = 1.0 + e
    d = u - 1.0
    log1p = jnp.where(d == 0.0, e, jnp.log(u) * (e / jnp.where(d == 0.0, 1.0, d)))
    return jnp.maximum(z, 0.0) + log1p


def _neg_expm1(z):
    series = -(z * (1.0 + z * (0.5 + z * (1.0 / 6.0 + z * (1.0 / 24.0)))))
    return jnp.where(z > -0.03, series, 1.0 - jnp.exp(z))


_GELU_C = math.sqrt(2.0 / math.pi)
_GELU_K = 0.044715


def _gelu_parts(x):
    x2 = x * x
    th = jnp.tanh(_GELU_C * (x + _GELU_K * x2 * x))
    ge = 0.5 * x * (1.0 + th)
    dge = 0.5 * (1.0 + th) + 0.5 * x * (1.0 - th * th) * (_GELU_C * (1.0 + 3.0 * _GELU_K * x2))
    return ge, dge


def _shift_down(halo, tile, k):
    if k == 0:
        return tile
    ext = jnp.concatenate([halo, tile], axis=0)
    n = tile.shape[0]
    h = halo.shape[0]
    return ext[h - k:h - k + n]


def _shift_up(tile, nxt, k):
    if k == 0:
        return tile
    ext = jnp.concatenate([tile, nxt], axis=0)
    return ext[k:k + tile.shape[0]]


def _build_gate_blocks(ga_ref, gx_ref, gw_ref):
    hd = ga_ref.shape[1]
    per = GATE_BLOCK // hd
    zero = jnp.zeros((hd, hd), F32)
    for b in range(gw_ref.shape[0]):
        for src, off in ((ga_ref, 0), (gx_ref, GATE_BLOCK)):
            for hh in range(per):
                row = jnp.concatenate([zero] * hh + [src[b * per + hh]] + [zero] * (per - 1 - hh), axis=1)
                gw_ref[b, hh * hd:(hh + 1) * hd, off:off + GATE_BLOCK] = row.astype(BF16)


def _scan_level1(a, b, reverse):
    m, c = a.shape
    a3 = a.reshape(m // SUBLANES, SUBLANES, c)
    b3 = b.reshape(m // SUBLANES, SUBLANES, c)
    row = lax.broadcasted_iota(jnp.int32, a3.shape, 1)
    for s in (1, 2, 4):
        sh = (SUBLANES - s) if reverse else s
        a_sh = pltpu.roll(a3, sh, 1)
        b_sh = pltpu.roll(b3, sh, 1)
        ok = (row < SUBLANES - s) if reverse else (row >= s)
        b3 = jnp.where(ok, a3 * b_sh + b3, b3)
        a3 = jnp.where(ok, a3 * a_sh, a3)
    return a3.reshape(m, c), b3.reshape(m, c)


def _scan_level2(a_ref, b_ref, out_ref, carry, reverse):
    m, c = a_ref.shape
    ng = m // SUBLANES

    def step(g, cr):
        gi = (ng - 1 - g) if reverse else g
        off = pl.multiple_of(gi * SUBLANES, SUBLANES)
        h = b_ref[pl.ds(off, SUBLANES), :] + a_ref[pl.ds(off, SUBLANES), :] * cr
        out_ref[pl.ds(off, SUBLANES), :] = h
        edge = h[0:1, :] if reverse else h[SUBLANES - 1:SUBLANES, :]
        return jnp.broadcast_to(edge, (SUBLANES, c))

    return lax.fori_loop(0, ng, step, carry, unroll=4)


def _mixer_recompute(u, hal, t0, cw, cb, gw_ref, ba, bx, lam, pw_ref, pb, ps):
    tm = u.shape[0]
    lw = cb.shape[1]
    u_l, u_g, u_p = u[:, :lw], u[:, lw:2 * lw], u[:, 2 * lw:]
    hal_l, hal_p = hal[:, :lw], hal[:, 2 * lw:]
    taps = [_shift_down(hal_l, u_l, CONV_WIDTH - 1 - k) for k in range(CONV_WIDTH)]
    xc = cb
    for k in range(CONV_WIDTH):
        xc = xc + taps[k] * cw[k:k + 1, :]
    xcb = xc.astype(BF16)
    nb = lw // GATE_BLOCK
    gs = [_dot(xcb[:, b * GATE_BLOCK:(b + 1) * GATE_BLOCK], gw_ref[b]) for b in range(nb)]
    r = _sigmoid(jnp.concatenate([g[:, :GATE_BLOCK] for g in gs], axis=1) + ba)
    ig = _sigmoid(jnp.concatenate([g[:, GATE_BLOCK:] for g in gs], axis=1) + bx)
    sp = _softplus_neg(lam)
    la = (-LRU_C * r) * sp
    a = jnp.exp(la)
    m2raw = _neg_expm1(2.0 * la)
    mult = jnp.sqrt(jnp.maximum(m2raw, 1e-12))
    ge, dge = _gelu_parts(u_g)
    row = lax.broadcasted_iota(jnp.int32, (tm, LANES), 0) + t0
    pooled, invs, zs = [], [], []
    for gi, w in enumerate(POOL_WINDOWS):
        e = jnp.concatenate([hal_p[:, gi * LANES:(gi + 1) * LANES], u_p[:, gi * LANES:(gi + 1) * LANES]], axis=0)
        s = e
        k = 1
        while k < w:
            s = s + pltpu.roll(s, k, 0)
            k *= 2
        inv = 1.0 / jnp.minimum(row + 1, w).astype(F32)
        pg = s[HALO:] * inv - e[HALO:]
        pooled.append(pg)
        invs.append(inv)
        zs.append(_dot(pg.astype(BF16), pw_ref[gi].astype(BF16)))
    z = jnp.concatenate(zs, axis=1) + pb
    y_pool = z * ps
    return dict(u_l=u_l, u_g=u_g, taps=taps, xc=xc, xcb=xcb, r=r, ig=ig, sp=sp, la=la, a=a, m2raw=m2raw,
                mult=mult, ge=ge, dge=dge, pooled=pooled, invs=invs, z=z, y_pool=y_pool)


ANY = pl.BlockSpec(memory_space=pl.ANY)
VMEM_SPEC = pl.BlockSpec(memory_space=pltpu.VMEM)


class _Hosted:
    def __init__(self, ins, out_shapes, sems, start, finish, mid=None, aliases=None):
        self.ins, self.out_shapes, self.sems = list(ins), list(out_shapes), list(sems)
        self.start, self.mid, self.finish = start, mid, finish
        self.aliases = dict(aliases or {})


def _call(body, hosted, stage_preds, *, name, grid, in_specs, out_specs, out_shape, scratch_shapes, args, sem):
    hosted = list(hosted or [])
    n_in, n_out, n_scr = len(in_specs), len(out_specs), len(scratch_shapes)
    c_in = [a for h in hosted for a in h.ins]
    c_out = [o for h in hosted for o in h.out_shapes]
    c_sem = [pltpu.SemaphoreType.DMA((k,)) for h in hosted for k in h.sems]

    def full(*refs):
        p = 0
        parts = []
        for cnt in (n_in, len(c_in), n_out, len(c_out), n_scr, len(c_sem)):
            parts.append(refs[p:p + cnt])
            p += cnt
        hi, ci, ho, co, hs, cs = parts
        per = []
        a = b = c_ = 0
        for h in hosted:
            per.append((h, ci[a:a + len(h.ins)], co[b:b + len(h.out_shapes)], cs[c_:c_ + len(h.sems)]))
            a, b, c_ = a + len(h.ins), b + len(h.out_shapes), c_ + len(h.sems)
        first = mid = last = None
        if hosted and grid:
            first, mid, last = stage_preds()

        def run(fn, pred, i_, o_, s_):
            if fn is None:
                return
            if pred is None:
                fn(i_, o_, s_)
            else:
                pl.when(pred)(functools.partial(fn, i_, o_, s_))

        for h, i_, o_, s_ in per:
            run(h.start, first, i_, o_, s_)
        body(*hi, *ho, *hs)
        for h, i_, o_, s_ in per:
            run(h.mid, mid, i_, o_, s_)
        for h, i_, o_, s_ in per:
            run(h.finish, last, i_, o_, s_)

    aliases = {}
    a = b = 0
    for h in hosted:
        for k, v in h.aliases.items():
            aliases[n_in + a + k] = n_out + b + v
        a, b = a + len(h.ins), b + len(h.out_shapes)
    res = pl.pallas_call(
        full, name=name, grid=grid, in_specs=list(in_specs) + [ANY] * len(c_in),
        out_specs=list(out_specs) + [ANY] * len(c_out), out_shape=list(out_shape) + c_out,
        scratch_shapes=list(scratch_shapes) + c_sem, input_output_aliases=aliases,
        compiler_params=_cp(sem))(*args, *c_in)
    res = list(res)
    outs = []
    p = n_out
    for h in hosted:
        outs.append(res[p:p + len(h.out_shapes)])
        p += len(h.out_shapes)
    return res[:n_out], outs


def _inproj(x, g_mix, w_in, hosted=None):
    s, d = x.shape
    n = w_in.shape[1]
    tm = min(TM_PROJ, s)
    nt = s // tm

    def body(x_ref, g_ref, w_ref, u_ref):
        xv = x_ref[...]
        r = lax.rsqrt(_rowmean(xv * xv) + EPS)
        u_ref[...] = _dot((xv * r * g_ref[...]).astype(BF16), w_ref[...])

    def stages():
        i = pl.program_id(0)
        return i == 0, i == max(nt - 3, 0), i == nt - 1

    return _call(
        body, hosted, stages, grid=(nt,), name="inproj",
        in_specs=[pl.BlockSpec((tm, d), lambda i: (i, 0)), _const_spec((1, d)), _const_spec((d, n))],
        out_specs=[pl.BlockSpec((tm, n), lambda i: (i, 0))], out_shape=[SDS((s, n), F32)], scratch_shapes=[],
        args=(x, g_mix, w_in), sem=("arbitrary",))


def _mixer_fwd(u, x, sp_, w_out, hosted=None):
    s, din = u.shape
    d = x.shape[1]
    lw = din // 3
    tm = min(TM_MIX, s)
    nb = lw // GATE_BLOCK

    def body(u_ref, halo_ref, x_ref, cw_ref, cb_ref, ga_ref, gx_ref, ba_ref, bx_ref, lam_ref, pw_ref, pb_ref,
             ps_ref, gl_ref, gp_ref, wout_ref, h_ref, hres_ref, saved_ref, pooled_ref,
             gw_s, a_s, b_s, carry_s):
        i = pl.program_id(0)

        @pl.when(i == 0)
        def _():
            _build_gate_blocks(ga_ref, gx_ref, gw_s)
            carry_s[...] = jnp.zeros_like(carry_s)

        uv = u_ref[...]
        hal = jnp.where(i > 0, halo_ref[...], 0.0)
        f = _mixer_recompute(uv, hal, i * tm, cw_ref[...], cb_ref[...], gw_s, ba_ref[...], bx_ref[...],
                             lam_ref[...], pw_ref, pb_ref[...], ps_ref[...])
        for k, name in enumerate(MIX_SAVED):
            saved_ref[k] = f[name]
        pooled_ref[...] = jnp.concatenate(f["pooled"], axis=1).astype(BF16)
        bb = f["mult"] * (f["ig"] * f["xc"])
        a1, b1 = _scan_level1(f["a"], bb, reverse=False)
        a_s[...] = a1
        b_s[...] = b1
        carry_s[...] = _scan_level2(a_s, b_s, h_ref, carry_s[...], reverse=False)
        y_lru = h_ref[...] * f["ge"]
        rl = lax.rsqrt(_rowmean(y_lru * y_lru) + EPS)
        yp = f["y_pool"]
        rp = lax.rsqrt(_rowmean(yp * yp) + EPS)
        yn = jnp.concatenate([y_lru * rl * gl_ref[...], yp * rp * gp_ref[...]], axis=1).astype(BF16)
        hres_ref[...] = x_ref[...] + _dot(yn, wout_ref[...])

    small = [sp_[k] for k in ("conv_w", "conv_b", "gate_a_w", "gate_x_w", "gate_a_b", "gate_x_b", "lru_lambda",
                              "pool_w", "pool_b", "pool_scale", "norm_lru_g", "norm_pool_g")]
    nt = s // tm

    def stages():
        i = pl.program_id(0)
        return i == 0, i == max(nt - 3, 0), i == nt - 1

    return _call(
        body, hosted, stages, grid=(nt,), name="mixer_fwd",
        in_specs=[pl.BlockSpec((tm, din), lambda i: (i, 0)),
                  pl.BlockSpec((HALO, din), lambda i: (jnp.maximum(i * (tm // HALO) - 1, 0), 0)),
                  pl.BlockSpec((tm, d), lambda i: (i, 0))]
        + [_const_spec(a.shape) for a in small] + [_const_spec(w_out.shape)],
        out_specs=[pl.BlockSpec((tm, lw), lambda i: (i, 0)), pl.BlockSpec((tm, d), lambda i: (i, 0)),
                   pl.BlockSpec((len(MIX_SAVED), tm, lw), lambda i: (0, i, 0)),
                   pl.BlockSpec((tm, lw), lambda i: (i, 0))],
        out_shape=[SDS((s, lw), F32), SDS((s, d), F32), SDS((len(MIX_SAVED), s, lw), F32),
                   SDS((s, lw), BF16)],
        scratch_shapes=[pltpu.VMEM((nb, GATE_BLOCK, 2 * GATE_BLOCK), BF16), pltpu.VMEM((tm, lw), F32),
                        pltpu.VMEM((tm, lw), F32), pltpu.VMEM((SUBLANES, lw), F32)],
        args=(u, u, x, *small, w_out), sem=("arbitrary",))


def _row_chunks(tm):
    rc = tm // FFN_ROW_CHUNKS
    return [slice(q * rc, (q + 1) * rc) for q in range(FFN_ROW_CHUNKS)]


def _ffn_up(hres1, g_ffn, w1, w3):
    s, d = hres1.shape
    nj, fc, _ = w1.shape
    tm = min(TM_FFN, s)

    def body(h_ref, gf_ref, w1_ref, w3_ref, h2_ref, a1_ref, a3_ref, ff_ref):
        hv = h_ref[...]
        r = lax.rsqrt(_rowmean(hv * hv) + EPS)
        h2_ref[...] = (hv * r * gf_ref[...]).astype(BF16)
        h2 = h2_ref[...]
        for j in range(nj):
            a1 = _dot_nt(h2, w1_ref[j])
            a3 = _dot_nt(h2, w3_ref[j])
            a1_ref[j] = a1.astype(BF16)
            a3_ref[j] = a3.astype(BF16)
            ff_ref[j] = ((a1 * _sigmoid(a1)) * a3).astype(BF16)

    wspec = _const_spec(w1.shape)
    aspec = pl.BlockSpec((nj, tm, fc), lambda i: (0, i, 0))
    return pl.pallas_call(
        body, grid=(s // tm,), name="ffn_up",
        in_specs=[pl.BlockSpec((tm, d), lambda i: (i, 0)), _const_spec((1, d)), wspec, wspec],
        out_specs=[pl.BlockSpec((tm, d), lambda i: (i, 0)), aspec, aspec, aspec],
        out_shape=[SDS((s, d), BF16)] + [SDS((nj, s, fc), BF16)] * 3,
        compiler_params=_cp(("parallel",)))(hres1, g_ffn, w1, w3)


def _ffn_down(ff, hres1, target, g_fin, w2):
    s, d = hres1.shape
    nj, _, fc = ff.shape
    tm = min(TM_FFN, s)

    def body(ff_ref, h_ref, t_ref, gn_ref, w2_ref, dh_ref, dhb_ref, loss_ref, dgn_ref):
        @pl.when(pl.program_id(0) == 0)
        def _():
            loss_ref[...] = jnp.zeros_like(loss_ref)
            dgn_ref[...] = jnp.zeros_like(dgn_ref)

        gn = gn_ref[...]
        for rows in _row_chunks(tm):
            acc = _dot(ff_ref[0, rows, :], w2_ref[0])
            for j in range(1, nj):
                acc = acc + _dot(ff_ref[j, rows, :], w2_ref[j])
            hr2 = h_ref[rows, :] + acc
            r2 = lax.rsqrt(_rowmean(hr2 * hr2) + EPS)
            xh = hr2 * r2
            diff = xh * gn - t_ref[rows, :]
            tot = jnp.sum(jnp.sum(diff * diff, axis=1, keepdims=True), axis=0, keepdims=True)
            loss_ref[...] += tot * (0.5 / d)
            dout = diff * (1.0 / d)
            dgn_ref[...] += _colsum8(dout * xh)
            dh = _rms_bwd(dout, xh, r2, gn)
            dh_ref[rows, :] = dh
            dhb_ref[rows, :] = dh.astype(BF16)

    tile = pl.BlockSpec((tm, d), lambda i: (i, 0))
    return pl.pallas_call(
        body, grid=(s // tm,), name="ffn_down",
        in_specs=[pl.BlockSpec((nj, tm, fc), lambda i: (0, i, 0)), tile, tile, _const_spec((1, d)),
                  _const_spec(w2.shape)],
        out_specs=[tile, tile, pl.BlockSpec((SUBLANES, LANES), lambda i: (0, 0)),
                   pl.BlockSpec((SUBLANES, d), lambda i: (0, 0))],
        out_shape=[SDS((s, d), F32), SDS((s, d), BF16), SDS((SUBLANES, LANES), F32), SDS((SUBLANES, d), F32)],
        compiler_params=_cp(("arbitrary",)))(ff, hres1, target, g_fin, w2)


def _ffn_bwd_gate(dhb, a1, a3, w2):
    s, d = dhb.shape
    nj, _, fc = a1.shape
    tm = min(TM_FFN, s)

    def body(dhb_ref, a1_ref, a3_ref, w2_ref, da1_ref, da3_ref):
        for j in range(nj):
            for rows in _row_chunks(tm):
                dff = _dot_nt(dhb_ref[rows, :], w2_ref[j])
                a1v = a1_ref[j, rows, :].astype(F32)
                sg = _sigmoid(a1v)
                silu = a1v * sg
                da1_ref[j, rows, :] = (dff * a3_ref[j, rows, :].astype(F32)
                                       * (sg * (1.0 + (a1v - silu)))).astype(BF16)
                da3_ref[j, rows, :] = (dff * silu).astype(BF16)

    aspec = pl.BlockSpec((nj, tm, fc), lambda i: (0, i, 0))
    return pl.pallas_call(
        body, grid=(s // tm,), name="ffn_bwd_gate",
        in_specs=[pl.BlockSpec((tm, d), lambda i: (i, 0)), aspec, aspec, _const_spec(w2.shape)],
        out_specs=[aspec, aspec], out_shape=[SDS((nj, s, fc), BF16)] * 2,
        compiler_params=_cp(("parallel",)))(dhb, a1, a3, w2)


def _ffn_bwd_down(da1, da3, dh, hres1, g_ffn, w1, w3, hosted=None):
    s, d = hres1.shape
    nj, _, fc = da1.shape
    tm = min(TM_FFN, s)
    nt = s // tm

    def body(da1_ref, da3_ref, dh_ref, h_ref, gf_ref, w1_ref, w3_ref, dhr_ref, dgf_ref):
        @pl.when(pl.program_id(0) == 0)
        def _():
            dgf_ref[...] = jnp.zeros_like(dgf_ref)

        gf = gf_ref[...]
        for rows in _row_chunks(tm):
            dh2 = None
            for j in range(nj):
                part = _dot(da1_ref[j, rows, :], w1_ref[j]) + _dot(da3_ref[j, rows, :], w3_ref[j])
                dh2 = part if dh2 is None else dh2 + part
            hv = h_ref[rows, :]
            r = lax.rsqrt(_rowmean(hv * hv) + EPS)
            xh = hv * r
            dgf_ref[...] += _colsum8(dh2 * xh)
            dhr_ref[rows, :] = dh_ref[rows, :] + _rms_bwd(dh2, xh, r, gf)

    tile = pl.BlockSpec((tm, d), lambda i: (i, 0))
    aspec = pl.BlockSpec((nj, tm, fc), lambda i: (0, i, 0))
    wspec = _const_spec(w1.shape)

    def stages():
        i = pl.program_id(0)
        return i == 0, i == max(nt - 2, 0), i == nt - 1

    return _call(
        body, hosted, stages, grid=(nt,), name="ffn_bwd_down",
        in_specs=[aspec, aspec, tile, tile, _const_spec((1, d)), wspec, wspec],
        out_specs=[tile, pl.BlockSpec((SUBLANES, d), lambda i: (0, 0))],
        out_shape=[SDS((s, d), F32), SDS((SUBLANES, d), F32)],
        scratch_shapes=[], args=(da1, da3, dh, hres1, g_ffn, w1, w3), sem=("arbitrary",))


def _ffn_wgrad(h2, dhb, ff, da1, da3):
    s, d = h2.shape
    _, _, fc = ff.shape
    tm = min(TM_WGRAD, s)

    def body(h2_ref, dhb_ref, ff_ref, da1_ref, da3_ref, dw1_ref, dw3_ref, dw2_ref):
        @pl.when(pl.program_id(1) == 0)
        def _():
            dw1_ref[...] = jnp.zeros_like(dw1_ref)
            dw3_ref[...] = jnp.zeros_like(dw3_ref)
            dw2_ref[...] = jnp.zeros_like(dw2_ref)

        h2v = h2_ref[...]
        dw1_ref[0] += _dot_tn(da1_ref[0], h2v)
        dw3_ref[0] += _dot_tn(da3_ref[0], h2v)
        dw2_ref[0] += _dot_tn(ff_ref[0], dhb_ref[...])

    wspec = pl.BlockSpec((1, fc, d), lambda j, i: (j, 0, 0))
    return pl.pallas_call(
        body, grid=(N_CHIPS, s // tm), name="ffn_wgrad",
        in_specs=[pl.BlockSpec((tm, d), lambda j, i: (i, 0)), pl.BlockSpec((tm, d), lambda j, i: (i, 0))]
        + [pl.BlockSpec((1, tm, fc), lambda j, i: (j, i, 0))] * 3,
        out_specs=[wspec] * 3, out_shape=[SDS((N_CHIPS, fc, d), F32)] * 3,
        compiler_params=_cp(("parallel", "arbitrary")))(h2, dhb, ff, da1, da3)


def _mixer_bwd(u, saved, pooled, h, dhres1, sp_, w_out, hosted=None):
    s, din = u.shape
    d = dhres1.shape[1]
    lw = din // 3
    tm = min(TM_MIX, s)
    nt = s // tm
    nb = lw // GATE_BLOCK
    hd = sp_["gate_a_w"].shape[1]

    def body(ul_ref, saved_ref, pooled_ref, h_ref, hhalo_ref, dhr_ref, cw_ref, cb_ref, ga_ref, gx_ref, ba_ref,
             bx_ref, lam_ref, pw_ref, pb_ref, ps_ref, gl_ref, gp_ref, wout_ref, du_ref, slab_ref, dwout_ref,
             gw_s, a_s, b_s, e_s, ecarry_s, dxc_s, q_s, vec_s, cwacc_s, dgw_s, dpw_s):
        i = pl.program_id(0)
        tile = nt - 1 - i

        @pl.when(i == 0)
        def _():
            _build_gate_blocks(ga_ref, gx_ref, gw_s)
            for ref in (ecarry_s, dxc_s, q_s, vec_s, cwacc_s, dgw_s, dpw_s, dwout_ref):
                ref[...] = jnp.zeros_like(ref)

        cw = cw_ref[...]
        lam = lam_ref[...]
        ps = ps_ref[...]
        f = {name: saved_ref[k] for k, name in enumerate(MIX_SAVED)}
        f["mult"] = jnp.sqrt(jnp.maximum(f["m2raw"], 1e-12))
        f["sp"] = _softplus_neg(lam)
        f["xcb"] = f["xc"].astype(BF16)
        pooled = pooled_ref[...]
        row = lax.broadcasted_iota(jnp.int32, (tm, LANES), 0) + tile * tm
        f["invs"] = [1.0 / jnp.minimum(row + 1, w).astype(F32) for w in POOL_WINDOWS]
        f["z"] = jnp.concatenate(
            [_dot(pooled[:, g * LANES:(g + 1) * LANES], pw_ref[g].astype(BF16))
             for g in range(len(POOL_WINDOWS))], axis=1) + pb_ref[...]
        f["y_pool"] = f["z"] * ps
        u_l = ul_ref[...]
        hv = h_ref[...]
        h_prev = _shift_down(jnp.where(tile > 0, hhalo_ref[...], 0.0), hv, 1)
        y_lru = hv * f["ge"]
        rl = lax.rsqrt(_rowmean(y_lru * y_lru) + EPS)
        yp = f["y_pool"]
        rp = lax.rsqrt(_rowmean(yp * yp) + EPS)
        xh_l = y_lru * rl
        xh_p = yp * rp

        dhrb = dhr_ref[...].astype(BF16)
        dyn = _dot_nt(dhrb, wout_ref[...])
        yn = jnp.concatenate([xh_l * gl_ref[...], xh_p * gp_ref[...]], axis=1).astype(BF16)
        dwout_ref[...] += _dot_tn(yn, dhrb)
        d_nl, d_np = dyn[:, :lw], dyn[:, lw:]
        vec = {}
        vec[ROW_GL] = _colsum8(d_nl * xh_l)
        vec[ROW_GP] = _colsum8(d_np * xh_p)
        d_ylru = _rms_bwd(d_nl, xh_l, rl, gl_ref[...])
        d_ypool = _rms_bwd(d_np, xh_p, rp, gp_ref[...])

        vec[ROW_PS] = _colsum8(d_ypool * f["z"])
        dz = d_ypool * ps
        vec[ROW_PB] = _colsum8(dz)
        dzb = dz.astype(BF16)
        dup = []
        for gi, w in enumerate(POOL_WINDOWS):
            sl = slice(gi * LANES, (gi + 1) * LANES)
            dpw_s[:, sl] += _dot_tn(pooled[:, sl], dzb[:, sl])
            dpool = _dot_nt(dzb[:, sl], pw_ref[gi].astype(BF16))
            q = dpool * f["invs"][gi]
            e = jnp.concatenate([q, q_s[:, sl]], axis=0)
            k = 1
            while k < w:
                e = e + pltpu.roll(e, tm + HALO - k, 0)
                k *= 2
            dup.append(e[:tm] - dpool)
            q_s[:, sl] = q[:HALO]

        d_hout = d_ylru * f["ge"]
        d_ug = d_ylru * hv * f["dge"]
        a = f["a"]
        a1, b1 = _scan_level1(a, a * d_hout, reverse=True)
        a_s[...] = a1
        b_s[...] = b1
        e_next = ecarry_s[...]
        ecarry_s[...] = _scan_level2(a_s, b_s, e_s, e_next, reverse=True)
        sv = d_hout + _shift_up(e_s[...], e_next, 1)
        d_a = sv * h_prev
        mult, ig, xc, r = f["mult"], f["ig"], f["xc"], f["r"]
        d_mult = sv * (ig * xc)
        d_ig = sv * mult * xc
        d_xc = sv * mult * ig
        d_la = d_a * a + jnp.where(f["m2raw"] > 1e-12, d_mult * (-(a * a) / mult), 0.0)
        d_r = d_la * (-LRU_C * f["sp"])
        vec[ROW_LAM] = _colsum8(d_la * (-LRU_C * r))
        d_pr = d_r * r * (1.0 - r)
        d_pi = d_ig * ig * (1.0 - ig)
        vec[ROW_BA] = _colsum8(d_pr)
        vec[ROW_BX] = _colsum8(d_pi)
        dxc_parts = []
        for b in range(nb):
            sl = slice(b * GATE_BLOCK, (b + 1) * GATE_BLOCK)
            rhs = jnp.concatenate([d_pr[:, sl], d_pi[:, sl]], axis=1).astype(BF16)
            dgw_s[b] += _dot_tn(f["xcb"][:, sl], rhs)
            dxc_parts.append(_dot_nt(rhs, gw_s[b]))
        d_xc = d_xc + jnp.concatenate(dxc_parts, axis=1)
        vec[ROW_CONV_B] = _colsum8(d_xc)
        dxc_next = dxc_s[...]
        d_ul = None
        for k in range(CONV_WIDTH):
            ahead = _shift_up(d_xc, dxc_next, CONV_WIDTH - 1 - k)
            cwacc_s[k * SUBLANES:(k + 1) * SUBLANES, :] += _colsum8(ahead * u_l)
            term = ahead * cw[k:k + 1, :]
            d_ul = term if d_ul is None else d_ul + term
        dxc_s[...] = d_xc[:SUBLANES]
        for row, val in vec.items():
            vec_s[row * SUBLANES:(row + 1) * SUBLANES, :] += val
        du_ref[...] = jnp.concatenate([d_ul, d_ug] + dup, axis=1).astype(BF16)

        @pl.when(i == nt - 1)
        def _():
            rows = []
            for row in range(ROW_GA):
                if row in (ROW_CONV_W, ROW_CONV_W + 1, ROW_CONV_W + 2, ROW_CONV_W + 3):
                    k = row - ROW_CONV_W
                    v = jnp.sum(cwacc_s[k * SUBLANES:(k + 1) * SUBLANES, :], axis=0, keepdims=True)
                elif row <= ROW_GP:
                    v = jnp.sum(vec_s[row * SUBLANES:(row + 1) * SUBLANES, :], axis=0, keepdims=True)
                    if row == ROW_LAM:
                        v = v * (-1.0 / (1.0 + jnp.exp(lam)))
                else:
                    v = jnp.zeros((1, lw), F32)
                rows.append(v)
            slab_ref[0:ROW_GA, :] = jnp.concatenate(rows, axis=0)
            lane = lax.broadcasted_iota(jnp.int32, (hd, GATE_BLOCK), 1)
            for b in range(nb):
                for off, row0 in ((0, ROW_GA), (GATE_BLOCK, ROW_GX)):
                    acc = jnp.zeros((hd, GATE_BLOCK), F32)
                    for hh in range(GATE_BLOCK // hd):
                        m = (lane >= hh * hd) & (lane < (hh + 1) * hd)
                        acc = acc + jnp.where(m, dgw_s[b, hh * hd:(hh + 1) * hd, off:off + GATE_BLOCK], 0.0)
                    slab_ref[row0:row0 + hd, b * GATE_BLOCK:(b + 1) * GATE_BLOCK] = acc
            slab_ref[ROW_PW:ROW_PW + LANES, :] = dpw_s[...]

    small = [sp_[k] for k in ("conv_w", "conv_b", "gate_a_w", "gate_x_w", "gate_a_b", "gate_x_b", "lru_lambda",
                              "pool_w", "pool_b", "pool_scale", "norm_lru_g", "norm_pool_g")]
    rev = lambda i: nt - 1 - i

    def stages():
        i = pl.program_id(0)
        return i == 0, i == max(nt - 3, 0), i == nt - 1

    return _call(
        body, hosted, stages, grid=(nt,), name="mixer_bwd",
        in_specs=[pl.BlockSpec((tm, lw), lambda i: (rev(i), 0)),
                  pl.BlockSpec((len(MIX_SAVED), tm, lw), lambda i: (0, rev(i), 0)),
                  pl.BlockSpec((tm, lw), lambda i: (rev(i), 0)),
                  pl.BlockSpec((tm, lw), lambda i: (rev(i), 0)),
                  pl.BlockSpec((SUBLANES, lw), lambda i: (jnp.maximum(rev(i) * (tm // SUBLANES) - 1, 0), 0)),
                  pl.BlockSpec((tm, d), lambda i: (rev(i), 0))]
        + [_const_spec(a.shape) for a in small] + [_const_spec(w_out.shape)],
        out_specs=[pl.BlockSpec((tm, din), lambda i: (rev(i), 0)),
                   pl.BlockSpec((MIX_SLAB_ROWS, SLAB_W), lambda i: (0, 0)), pl.BlockSpec((d, d), lambda i: (0, 0))],
        out_shape=[SDS((s, din), BF16), SDS((MIX_SLAB_ROWS, SLAB_W), F32), SDS((d, d), F32)],
        scratch_shapes=[pltpu.VMEM((nb, GATE_BLOCK, 2 * GATE_BLOCK), BF16),
                        pltpu.VMEM((tm, lw), F32), pltpu.VMEM((tm, lw), F32), pltpu.VMEM((tm, lw), F32),
                        pltpu.VMEM((SUBLANES, lw), F32), pltpu.VMEM((SUBLANES, lw), F32),
                        pltpu.VMEM((HALO, lw), F32), pltpu.VMEM((ROW_GA * SUBLANES, lw), F32),
                        pltpu.VMEM((CONV_WIDTH * SUBLANES, lw), F32),
                        pltpu.VMEM((nb, GATE_BLOCK, 2 * GATE_BLOCK), F32), pltpu.VMEM((LANES, lw), F32)],
        args=(u, saved, pooled, h, h, dhres1, *small, w_out), sem=("arbitrary",))


def _inproj_bwd(x, du, dhres1, g_mix, w_in, hosted=None):
    s, d = x.shape
    n = w_in.shape[1]
    nc = n // N_CHIPS
    tm = min(TM_PROJ, s)
    nt = s // tm

    def body(x_ref, du_ref, dhr_ref, g_ref, w_ref, gx_ref, dwin_ref, dg_ref):
        i = pl.program_id(0)

        @pl.when(i == 0)
        def _():
            dwin_ref[...] = jnp.zeros_like(dwin_ref)
            dg_ref[...] = jnp.zeros_like(dg_ref)

        xv = x_ref[...]
        g = g_ref[...]
        r = lax.rsqrt(_rowmean(xv * xv) + EPS)
        xh = xv * r
        h1 = (xh * g).astype(BF16)
        duv = du_ref[...]
        dh1 = _dot_nt(duv, w_ref[...])
        dg_ref[...] += _colsum8(dh1 * xh)
        gx_ref[...] = dhr_ref[...] + _rms_bwd(dh1, xh, r, g)
        for jj in range(N_CHIPS):
            dwin_ref[jj] += _dot_tn(h1, duv[:, jj * nc:(jj + 1) * nc])

    def stages():
        i = pl.program_id(0)
        return i == 0, i == max(nt - 3, 0), i == nt - 1

    return _call(
        body, hosted, stages, grid=(nt,), name="inproj_bwd",
        in_specs=[pl.BlockSpec((tm, d), lambda i: (i, 0)), pl.BlockSpec((tm, n), lambda i: (i, 0)),
                  pl.BlockSpec((tm, d), lambda i: (i, 0)), _const_spec((1, d)), _const_spec((d, n))],
        out_specs=[pl.BlockSpec((tm, d), lambda i: (i, 0)), pl.BlockSpec((N_CHIPS, d, nc), lambda i: (0, 0, 0)),
                   pl.BlockSpec((SUBLANES, d), lambda i: (0, 0))],
        out_shape=[SDS((s, d), F32), SDS((N_CHIPS, d, nc), F32), SDS((SUBLANES, d), F32)],
        scratch_shapes=[], args=(x, du, dhres1, g_mix, w_in), sem=("arbitrary",))


def _place():
    x, y, c = lax.axis_index("x"), lax.axis_index("y"), lax.axis_index("c")
    return x, y, c


def _other_chips(x, y):
    return [(1 - x, y), (x, 1 - y), (1 - x, 1 - y)]


ANY = pl.BlockSpec(memory_space=pl.ANY)
VMEM_SPEC = pl.BlockSpec(memory_space=pltpu.VMEM)

_GATHERED = {"w_in": "cols", "w_out": "major", "ffn_w1": "major", "ffn_w3": "major", "ffn_w2": "major"}
_BIG = ("w_in", "w_out", "ffn_w1", "ffn_w3", "ffn_w2")


def _gather_weights(shards, conv_w, n_remote):
    n = len(shards)
    full_shapes = []
    for name, sh in zip(_BIG, shards):
        r, cdim = sh.shape
        if _GATHERED[name] == "cols":
            assert cdim % LANES == 0
            full_shapes.append((r, cdim * N_CHIPS))
        else:
            full_shapes.append((N_CHIPS, r, cdim))

    def region(ref, name, sh, jj, cc):
        r, cdim = sh
        rows = pl.ds(0, r) if cc is None else pl.ds(pl.multiple_of(cc * (r // 2), 16), r // 2)
        if _GATHERED[name] == "cols":
            return ref.at[rows, pl.ds(pl.multiple_of(jj * cdim, LANES), cdim)]
        return ref.at[jj, rows, :]

    def staged(ref, sh, cc):
        r = sh[0]
        return ref.at[pl.ds(pl.multiple_of(cc * (r // 2), 16), r // 2), :]

    def body(*refs):
        ins, cw_in = refs[:n], refs[n]
        outs, cw_out = refs[n + 1:2 * n + 1], refs[2 * n + 1]
        stage = refs[2 * n + 2:3 * n + 2]
        cw_stage, lsem, ssem, rsem, fssem, frsem, cssem, crsem = refs[3 * n + 2:]
        x, y, c = _place()
        j = 2 * x + y
        chips = _other_chips(x, y)
        for w in range(n_remote):
            stage[w][...] = ins[w][...].astype(BF16)
        cw_stage[...] = jnp.zeros_like(cw_stage)
        cw_stage[0:CONV_WIDTH, :] = cw_in[...]
        shs = [s_.shape for s_ in shards]
        local = [pltpu.make_async_copy(stage[w], region(outs[w], _BIG[w], shs[w], j, None), lsem.at[w])
                 for w in range(n)]
        local.append(pltpu.make_async_copy(cw_stage, cw_out.at[j], lsem.at[n]))
        sends = []
        for k, (px, py) in enumerate(chips):
            for w in range(n_remote):
                sends.append(pltpu.make_async_remote_copy(
                    src_ref=staged(stage[w], shs[w], c), dst_ref=region(outs[w], _BIG[w], shs[w], j, c),
                    send_sem=ssem.at[k * n + w], recv_sem=rsem.at[k * n + w], device_id=(px, py, c),
                    device_id_type=MESH))
            sends.append(pltpu.make_async_remote_copy(
                src_ref=cw_stage, dst_ref=cw_out.at[j], send_sem=cssem.at[k], recv_sem=crsem.at[k],
                device_id=(px, py, c), device_id_type=MESH))
        for cp in sends:
            cp.start()
        for w in range(n_remote, n):
            stage[w][...] = ins[w][...].astype(BF16)
        for cp in local:
            cp.start()
        fwd = []
        for k, (px, py) in enumerate(chips):
            jk = 2 * px + py
            for w in range(n_remote):
                reg = region(outs[w], _BIG[w], shs[w], jk, c)
                pltpu.make_async_remote_copy(src_ref=reg, dst_ref=reg, send_sem=ssem.at[k * n + w],
                                             recv_sem=rsem.at[k * n + w], device_id=(px, py, c),
                                             device_id_type=MESH).wait_recv()
                cp = pltpu.make_async_remote_copy(src_ref=reg, dst_ref=reg, send_sem=fssem.at[k * n + w],
                                                  recv_sem=frsem.at[k * n + w], device_id=(x, y, 1 - c),
                                                  device_id_type=MESH)
                cp.start()
                fwd.append(cp)
            pltpu.make_async_remote_copy(src_ref=cw_stage, dst_ref=cw_out.at[jk], send_sem=cssem.at[k],
                                         recv_sem=crsem.at[k], device_id=(px, py, c),
                                         device_id_type=MESH).wait_recv()
        for k, (px, py) in enumerate(chips):
            jk = 2 * px + py
            for w in range(n_remote):
                reg = region(outs[w], _BIG[w], shs[w], jk, 1 - c)
                pltpu.make_async_remote_copy(src_ref=reg, dst_ref=reg, send_sem=fssem.at[k * n + w],
                                             recv_sem=frsem.at[k * n + w], device_id=(x, y, 1 - c),
                                             device_id_type=MESH).wait_recv()
        for cp in sends + fwd:
            cp.wait_send()
        for cp in local:
            cp.wait()

    nsem = 3 * n
    return pl.pallas_call(
        body, name="gather_first",
        in_specs=[VMEM_SPEC] * (n + 1), out_specs=[ANY] * (n + 1),
        out_shape=[SDS(fs, BF16) for fs in full_shapes] + [SDS((N_CHIPS, SUBLANES, LANES), F32)],
        scratch_shapes=[pltpu.VMEM(s_.shape, BF16) for s_ in shards] + [pltpu.VMEM((SUBLANES, LANES), F32)]
        + [pltpu.SemaphoreType.DMA((n + 1,))] + [pltpu.SemaphoreType.DMA((nsem,))] * 4
        + [pltpu.SemaphoreType.DMA((3,))] * 2,
        compiler_params=_cp())(*shards, conv_w)


def _start_all(make):
    def f(ins, outs, sems):
        for cp in make(ins, outs, sems):
            cp.start()
    return f


def _wait_all(make):
    def f(ins, outs, sems):
        for cp in make(ins, outs, sems):
            cp.wait()
    return f


def _ffn_gather_hosted(arrs):
    n = len(arrs)

    def make(outs, sems):
        ssem, rsem, fs, fr = sems
        x, y, c = _place()
        j = 2 * x + y

        def reg(w, jj, cc):
            hr = arrs[w].shape[1] // 2
            return outs[w].at[jj, pl.ds(pl.multiple_of(cc * hr, 16), hr), :]

        def rc(w, jj, cc, s_sem, r_sem, dev):
            return pltpu.make_async_remote_copy(src_ref=reg(w, jj, cc), dst_ref=reg(w, jj, cc), send_sem=s_sem,
                                                recv_sem=r_sem, device_id=dev, device_id_type=MESH)

        sends, recvs, fwds, frecvs = [], [], [], []
        for k, (px, py) in enumerate(_other_chips(x, y)):
            jk = 2 * px + py
            for w in range(n):
                q = k * n + w
                sends.append(rc(w, j, c, ssem.at[q], rsem.at[q], (px, py, c)))
                recvs.append(rc(w, jk, c, ssem.at[q], rsem.at[q], (px, py, c)))
                fwds.append(rc(w, jk, c, fs.at[q], fr.at[q], (x, y, 1 - c)))
                frecvs.append(rc(w, jk, 1 - c, fs.at[q], fr.at[q], (x, y, 1 - c)))
        return sends, recvs, fwds, frecvs

    def start(ins, outs, sems):
        for cp in make(outs, sems)[0]:
            cp.start()

    def mid(ins, outs, sems):
        _, recvs, fwds, _ = make(outs, sems)
        for r, f in zip(recvs, fwds):
            r.wait_recv()
            f.start()

    def finish(ins, outs, sems):
        sends, _, fwds, frecvs = make(outs, sems)
        for r in frecvs:
            r.wait_recv()
        for cp in sends + fwds:
            cp.wait_send()

    return _Hosted(arrs, [SDS(a.shape, a.dtype) for a in arrs], [3 * n] * 4, start, finish, mid=mid,
                   aliases={w: w for w in range(n)})


def _rs_sibling_hosted(arrs):
    n = len(arrs)

    def make(ins, outs, sems):
        x, y, c = _place()
        cps = []
        for w in range(n):
            hr = arrs[w].shape[1] // 2
            src = ins[w].at[:, pl.ds(pl.multiple_of((1 - c) * hr, SUBLANES), hr), :]
            cps.append(pltpu.make_async_remote_copy(src_ref=src, dst_ref=outs[w], send_sem=sems[0].at[w],
                                                    recv_sem=sems[1].at[w], device_id=(x, y, 1 - c),
                                                    device_id_type=MESH))
        return cps

    return _Hosted(arrs, [SDS((a.shape[0], a.shape[1] // 2, a.shape[2]), F32) for a in arrs], [n, n],
                   _start_all(make), _wait_all(make))


def _rs_chips_hosted(parts):
    n = len(parts)

    def make(ins, outs, sems):
        x, y, c = _place()
        j = 2 * x + y
        cps = []
        for k, (px, py) in enumerate(_other_chips(x, y)):
            jk = 2 * px + py
            for w in range(n):
                cps.append(pltpu.make_async_remote_copy(
                    src_ref=ins[w].at[jk], dst_ref=outs[w].at[j], send_sem=sems[0].at[k * n + w],
                    recv_sem=sems[1].at[k * n + w], device_id=(px, py, c), device_id_type=MESH))
        return cps

    return _Hosted(parts, [SDS(p.shape, p.dtype) for p in parts], [3 * n, 3 * n], _start_all(make), _wait_all(make))


def _rs_swap_hosted(halves):
    n = len(halves)

    def make(ins, outs, sems):
        x, y, c = _place()
        return [pltpu.make_async_remote_copy(src_ref=ins[w], dst_ref=outs[w], send_sem=sems[0].at[w],
                                             recv_sem=sems[1].at[w], device_id=(x, y, 1 - c), device_id_type=MESH)
                for w in range(n)]

    return _Hosted(halves, [SDS(h.shape, F32) for h in halves], [n, n], _start_all(make), _wait_all(make))


HBM_SPEC = pl.BlockSpec(memory_space=pltpu.HBM)
SEM_SPEC = pl.BlockSpec(memory_space=pltpu.SEMAPHORE)
_EFFECT = pltpu.SideEffectType.DATAFLOW_SIDE_EFFECTING


def _split_start(h, name):
    n_in, n_out, ns = len(h.ins), len(h.out_shapes), len(h.sems)
    ins = [pltpu.with_memory_space_constraint(a, pltpu.HBM) for a in h.ins]
    lands = [pltpu.with_memory_space_constraint(lax.empty(o.shape, o.dtype), pltpu.HBM) for o in h.out_shapes]

    def body(*refs):
        i_refs, l_refs = refs[:n_in], refs[n_in:n_in + n_out]
        s_refs = refs[n_in + n_out:n_in + n_out + ns]
        token = refs[-1]
        h.start(i_refs, l_refs, s_refs)
        token[...] = jnp.zeros_like(token)

    res = pl.pallas_call(
        body, name=name, in_specs=[HBM_SPEC] * (n_in + n_out),
        out_specs=[SEM_SPEC] * ns + [HBM_SPEC] * n_out + [VMEM_SPEC],
        out_shape=[pltpu.SemaphoreType.DMA((k,)) for k in h.sems]
        + [pltpu.HBM(o.shape, o.dtype) for o in h.out_shapes] + [SDS((SUBLANES, LANES), F32)],
        input_output_aliases={n_in + k: ns + k for k in range(n_out)},
        compiler_params=pltpu.CompilerParams(has_side_effects=_EFFECT))(*ins, *lands)
    return list(res[:ns]) + ins + list(res[ns:-1]), res[-1]


def _split_wait(h, state, after, name):
    n_in, n_out, ns = len(h.ins), len(h.out_shapes), len(h.sems)
    sems, bufs = state[:ns], state[ns:]

    def body(*refs):
        i_refs, l_refs = refs[:n_in], refs[n_in:n_in + n_out]
        s_refs = refs[n_in + n_out:n_in + n_out + ns]
        h.finish(i_refs, l_refs, s_refs)

    res = pl.pallas_call(
        body, name=name, in_specs=[HBM_SPEC] * (n_in + n_out) + [SEM_SPEC] * ns + [ANY],
        out_specs=[HBM_SPEC] * n_out,
        out_shape=[pltpu.HBM(b.shape, b.dtype) for b in bufs[n_in:]],
        input_output_aliases={n_in + k: k for k in range(n_out)},
        compiler_params=pltpu.CompilerParams(has_side_effects=_EFFECT))(*bufs, *sems, after)
    return list(res)


def _run_comm(hosted, name):
    return _call(lambda: None, hosted, None, name=name, grid=(), in_specs=[], out_specs=[], out_shape=[],
                 scratch_shapes=[], args=(), sem=None)[1]


def _row_tile(rows, cols, n_arrays):
    budget = 24 * 1024 * 1024 // (2 * 4 * n_arrays * cols)
    best = SUBLANES
    for t in range(SUBLANES, rows + 1, SUBLANES):
        if rows % t == 0 and t <= budget:
            best = t
    return best


def _place_index(which):
    x, y, c = _place()
    v = c if which == "c" else 2 * x + y
    return jnp.reshape(v, (1,)).astype(jnp.int32)


def _add_own_half(full, recv, name, wire=BF16):
    nsh, rows, cols = full.shape
    hr = rows // 2
    t = _row_tile(hr, cols, 4)
    nt = hr // t

    def body(p_ref, a_ref, b_ref, o_ref, ob_ref):
        v = a_ref[0] + b_ref[0]
        ob_ref[0] = v.astype(wire)

        @pl.when(pl.program_id(1) == nsh - 1)
        def _():
            o_ref[...] = v

    def shard(k, p_ref):
        return (p_ref[1] + 1 + k) % nsh

    half = pl.BlockSpec((1, t, cols), lambda i, k, p_ref: (shard(k, p_ref), i, 0))
    return pl.pallas_call(
        body, name=name,
        grid_spec=pltpu.PrefetchScalarGridSpec(
            num_scalar_prefetch=1, grid=(nt, nsh),
            in_specs=[pl.BlockSpec((1, t, cols), lambda i, k, p_ref: (shard(k, p_ref), p_ref[0] * nt + i, 0)), half],
            out_specs=[pl.BlockSpec((t, cols), lambda i, k, p_ref: (i, 0)), half]),
        out_shape=[SDS((hr, cols), F32), SDS((nsh, hr, cols), wire)],
        compiler_params=_cp(("parallel", "arbitrary")))(
            jnp.concatenate([_place_index("c"), _place_index("j")]), full, recv)


def _sum_chips(own, recv, name):
    nsh, hr, cols = recv.shape
    t = _row_tile(hr, cols, 6)

    def body(j_ref, own_ref, *rest):
        r_refs, o_ref = rest[:nsh], rest[nsh]
        j = j_ref[0]
        mine = own_ref[...]
        parts = [jnp.where(j == k, mine, r_refs[k][0].astype(F32)) for k in range(nsh)]
        o_ref[...] = ((parts[0] + parts[1]) + parts[2]) + parts[3]

    def other(k):
        return pl.BlockSpec((1, t, cols), lambda i, j_ref: (jnp.where(j_ref[0] == k, (k + 1) % nsh, k), i, 0))

    return pl.pallas_call(
        body, name=name,
        grid_spec=pltpu.PrefetchScalarGridSpec(
            num_scalar_prefetch=1, grid=(hr // t,),
            in_specs=[pl.BlockSpec((t, cols), lambda i, j_ref: (i, 0))] + [other(k) for k in range(nsh)],
            out_specs=pl.BlockSpec((t, cols), lambda i, j_ref: (i, 0))),
        out_shape=SDS((hr, cols), F32), compiler_params=_cp(("parallel",)))(_place_index("j"), own, *([recv] * nsh))


def _adamw_math(w, g, m, v):
    m = ADAM_B1 * m + (1.0 - ADAM_B1) * g
    v = ADAM_B2 * v + (1.0 - ADAM_B2) * (g * g)
    m_hat = m / (1.0 - ADAM_B1 ** ADAM_STEP)
    v_hat = v / (1.0 - ADAM_B2 ** ADAM_STEP)
    delta = -ADAM_LR * (m_hat / (jnp.sqrt(v_hat) + ADAM_EPS) + ADAM_WD * w)
    return delta, m, v


def _adamw_big(w, g_own, g_sib, m, v, name, token=None):
    _, rows, cols = w.shape
    hr = rows // 2
    t = _row_tile(hr, cols, 9)
    nth = hr // t
    if token is None:
        token = jnp.zeros((SUBLANES, LANES), F32)

    def body(c_ref, w_ref, go_ref, gs_ref, m_ref, v_ref, tok_ref, g_ref, d_ref, mo_ref, vo_ref):
        own = (pl.program_id(0) // nth) == c_ref[0]
        g = jnp.where(own, go_ref[...], gs_ref[...]) + tok_ref[0:1, 0:1]
        g_ref[0] = g
        d_ref[0], mo_ref[0], vo_ref[0] = _adamw_math(w_ref[0], g, m_ref[0], v_ref[0])

    spec = pl.BlockSpec((1, t, cols), lambda i, c_ref: (0, i, 0))
    hspec = pl.BlockSpec((t, cols), lambda i, c_ref: (i % nth, 0))
    tspec = pl.BlockSpec((SUBLANES, LANES), lambda i, c_ref: (0, 0))
    return pl.pallas_call(
        body, name=name,
        grid_spec=pltpu.PrefetchScalarGridSpec(
            num_scalar_prefetch=1, grid=(2 * nth,), in_specs=[spec, hspec, hspec, spec, spec, tspec],
            out_specs=[spec] * 4),
        out_shape=[SDS((1, rows, cols), F32)] * 4,
        compiler_params=_cp(("parallel",)))(_place_index("c"), w, g_own, g_sib, m, v, token)


def _build_slab(mix_slab, dg_mix, dg_ffn, dg_fin, loss8):
    def body(ms_ref, gm_ref, gf_ref, gn_ref, loss_ref, out_ref):
        rows = []
        for ref in (gm_ref, gf_ref, gn_ref):
            v = jnp.sum(ref[...], axis=0, keepdims=True)
            rows += [v[:, :SLAB_W], v[:, SLAB_W:]]
        rows.append(jnp.concatenate([loss_ref[0:1, :]] * (SLAB_W // LANES), axis=1))
        rows.append(jnp.zeros((SLAB_ROWS - ROW_LOSS - 1, SLAB_W), F32))
        tail = jnp.concatenate(rows, axis=0)
        for k in range(N_CHIPS):
            out_ref[k, 0:MIX_SLAB_ROWS, :] = ms_ref[...]
            out_ref[k, MIX_SLAB_ROWS:SLAB_ROWS, :] = tail

    return pl.pallas_call(
        body, name="build_slab", in_specs=[VMEM_SPEC] * 5, out_specs=VMEM_SPEC,
        out_shape=SDS((N_CHIPS, SLAB_ROWS, SLAB_W), F32),
        compiler_params=_cp())(mix_slab, dg_mix, dg_ffn, dg_fin, loss8)


_SMALL_ROWS = (("conv_b", ROW_CONV_B), ("gate_a_b", ROW_BA), ("gate_x_b", ROW_BX), ("lru_lambda", ROW_LAM),
               ("pool_b", ROW_PB), ("pool_scale", ROW_PS), ("norm_lru_g", ROW_GL), ("norm_pool_g", ROW_GP))
_WIDE_ROWS = (("norm_mix_g", ROW_MIX), ("norm_ffn_g", ROW_FFN), ("final_norm_g", ROW_FIN))
_BLOCK_ROWS = (("gate_a_w", ROW_GA), ("gate_x_w", ROW_GX), ("pool_w", ROW_PW))
_SMALL_ORDER = tuple(n for n, _ in _SMALL_ROWS) + tuple(n for n, _ in _WIDE_ROWS) + tuple(
    n for n, _ in _BLOCK_ROWS) + ("conv_w",)


def _adamw_small(slab_own, slab_sib, wmv):
    names = _SMALL_ORDER
    flat = [a for nme in names for a in wmv[nme]]
    nin = len(flat)

    def body(*refs):
        own_ref, sib_ref, j_ref = refs[0], refs[1], refs[2]
        ins = refs[3:3 + nin]
        outs = refs[3 + nin:-1]
        first = j_ref[1] == 0
        slab_ref = jnp.concatenate([jnp.where(first, own_ref[...], sib_ref[...]),
                                    jnp.where(first, sib_ref[...], own_ref[...])], axis=0)
        refs[-1][...] = jnp.broadcast_to(slab_ref[ROW_LOSS:ROW_LOSS + 1, 0:LANES], (SUBLANES, LANES))
        grads = {}
        for nme, row in _SMALL_ROWS:
            grads[nme] = slab_ref[row:row + 1, :]
        for nme, row in _WIDE_ROWS:
            grads[nme] = jnp.concatenate([slab_ref[row:row + 1, :], slab_ref[row + 1:row + 2, :]], axis=1)
        full = slab_ref[ROW_CONV_W:ROW_CONV_W + CONV_WIDTH, :]
        jv = j_ref[0]
        g = jnp.zeros((CONV_WIDTH, LANES), F32)
        for jj in range(N_CHIPS):
            g = jnp.where(jv == jj, full[:, jj * LANES:(jj + 1) * LANES], g)
        grads["conv_w"] = g
        block_rows = dict(_BLOCK_ROWS)
        for idx, nme in enumerate(names):
            w_ref, m_ref, v_ref = ins[3 * idx:3 * idx + 3]
            if nme in block_rows:
                nblk, r, c = w_ref.shape
                parts = [(b, slab_ref[block_rows[nme]:block_rows[nme] + r, b * c:(b + 1) * c]) for b in range(nblk)]
            else:
                parts = [(Ellipsis, grads[nme])]
            for b, g in parts:
                delta, m, v = _adamw_math(w_ref[b], g, m_ref[b], v_ref[b])
                outs[4 * idx][b] = g
                outs[4 * idx + 1][b] = delta
                outs[4 * idx + 2][b] = m
                outs[4 * idx + 3][b] = v

    place = jnp.concatenate([_place_index("j"), _place_index("c")])
    out_shape = [SDS(wmv[nme][0].shape, F32) for nme in names for _ in range(4)] + [SDS((SUBLANES, LANES), F32)]
    res = pl.pallas_call(
        body, name="adamw_small",
        in_specs=[VMEM_SPEC, VMEM_SPEC, pl.BlockSpec(memory_space=pltpu.SMEM)] + [VMEM_SPEC] * nin,
        out_specs=[VMEM_SPEC] * len(out_shape), out_shape=out_shape,
        compiler_params=_cp())(slab_own, slab_sib, place, *flat)
    return {nme: tuple(res[4 * idx:4 * idx + 4]) for idx, nme in enumerate(names)}, res[-1]


_FFN = ("ffn_w1", "ffn_w3", "ffn_w2")
_TRANSPOSED = ("ffn_w1", "ffn_w3")


def _local_step(x, target, full, sp_, distributed):
    d = x.shape[1]
    (u,), got = _inproj(x, sp_["norm_mix_g"], full["w_in"],
                        [_ffn_gather_hosted([full["w_out"]])] if distributed else None)
    w_out = (got[0][0] if distributed else full["w_out"]).reshape(d, d)
    gather = [_ffn_gather_hosted([full[n] for n in _FFN])] if distributed else None
    (h, hres1, saved, pooled), got = _mixer_fwd(u, x, sp_, w_out, gather)
    w1, w3, w2 = got[0] if distributed else [full[n] for n in _FFN]
    h2, a1, a3, ff = _ffn_up(hres1, sp_["norm_ffn_g"], w1, w3)
    dh, dhb, loss8, dg_fin = _ffn_down(ff, hres1, target, sp_["final_norm_g"], w2)
    da1, da3 = _ffn_bwd_gate(dhb, a1, a3, w2)
    dws = list(_ffn_wgrad(h2, dhb, ff, da1, da3))
    rs1 = [_rs_sibling_hosted(dws)] if distributed else None
    (dhres1, dg_ffn), got = _ffn_bwd_down(da1, da3, dh, hres1, sp_["norm_ffn_g"], w1, w3, rs1)
    rs2 = None
    if distributed:
        pairs = [_add_own_half(a, r, "add_half_" + n) for n, a, r in zip(_FFN, dws, got[0])]
        rs2 = [_rs_chips_hosted([pb for _, pb in pairs])]
    (du, mix_slab, dwout), got = _mixer_bwd(u, saved, pooled, h, dhres1, sp_, w_out, rs2)
    g_mix = sp_["norm_mix_g"]
    if distributed:
        fin = [_sum_chips(pairs[k][0], got[0][k], "sum_chips_" + n) for k, n in enumerate(_FFN)]
        swap = _rs_swap_hosted(fin)
        state, token = _split_start(swap, "ffn_swap_start")
        g_mix = g_mix + token[0:1, 0:1]
    (gx, dwin, dg_mix), _ = _inproj_bwd(x, du, dhres1, g_mix, full["w_in"])
    if distributed:
        sib = _split_wait(swap, state, dg_mix, "ffn_swap_wait")
    big = {"w_in": dwin, "w_out": dwout.reshape(N_CHIPS, d // N_CHIPS, d)}
    for k, n in enumerate(_FFN):
        big[n] = (fin[k], sib[k]) if distributed else dws[k]
    return gx, big, (mix_slab, dg_mix, dg_ffn, dg_fin, loss8)


_SMALL_LAYOUT = {
    "gate_a_w": (lambda a: a[0], lambda a: a[None]),
    "gate_x_w": (lambda a: a[0], lambda a: a[None]),
    "pool_w": (lambda a: a[0], lambda a: a[None]),
    "conv_w": (lambda a: a[0], lambda a: a[None]),
    "final_norm_g": (lambda a: a[None], lambda a: a[0]),
}

_WEIGHTS = ("norm_mix_g", "w_in", "conv_w", "conv_b", "gate_a_w", "gate_a_b", "gate_x_w", "gate_x_b", "lru_lambda",
            "pool_w", "pool_b", "pool_scale", "norm_lru_g", "norm_pool_g", "w_out", "norm_ffn_g", "ffn_w1",
            "ffn_w3", "ffn_w2", "final_norm_g")


def kernel(x, norm_mix_g, w_in, conv_w, conv_b, gate_a_w, gate_a_b, gate_x_w, gate_x_b, lru_lambda, pool_w, pool_b, pool_scale, norm_lru_g, norm_pool_g, w_out, norm_ffn_g, ffn_w1, ffn_w3, ffn_w2, final_norm_g, loss_target, m_norm_mix_g, m_w_in, m_conv_w, m_conv_b, m_gate_a_w, m_gate_a_b, m_gate_x_w, m_gate_x_b, m_lru_lambda, m_pool_w, m_pool_b, m_pool_scale, m_norm_lru_g, m_norm_pool_g, m_w_out, m_norm_ffn_g, m_ffn_w1, m_ffn_w3, m_ffn_w2, m_final_norm_g, v_norm_mix_g, v_w_in, v_conv_w, v_conv_b, v_gate_a_w, v_gate_a_b, v_gate_x_w, v_gate_x_b, v_lru_lambda, v_pool_w, v_pool_b, v_pool_scale, v_norm_lru_g, v_norm_pool_g, v_w_out, v_norm_ffn_g, v_ffn_w1, v_ffn_w3, v_ffn_w2, v_final_norm_g):
    loc = locals()
    w = {n: loc[n] for n in _WEIGHTS}
    m = {n: loc["m_" + n] for n in _WEIGHTS}
    v = {n: loc["v_" + n] for n in _WEIGHTS}

    def lay(nme, a):
        return _SMALL_LAYOUT[nme][0](a) if nme in _SMALL_LAYOUT else a

    def unlay(nme, a):
        return _SMALL_LAYOUT[nme][1](a) if nme in _SMALL_LAYOUT else a

    for group in (w, m, v):
        for n in _TRANSPOSED:
            group[n] = jnp.transpose(group[n], (0, 2, 1))

    gathered = _gather_weights([w[n][0] for n in _BIG], w["conv_w"][0], n_remote=1)
    full = dict(zip(_BIG, gathered[:-1]))
    cw_all = gathered[-1]
    sp_ = {n: lay(n, w[n]) for n in _SMALL_ORDER}
    sp_["conv_w"] = jnp.transpose(cw_all[:, :CONV_WIDTH, :], (1, 0, 2)).reshape(CONV_WIDTH, N_CHIPS * LANES)

    gx, big, small = _local_step(x[0], loss_target[0], full, sp_, distributed=True)

    late = ("w_in", "w_out", "slab")
    big["slab"] = _build_slab(*small)
    fin = {n: big[n][0] for n in _FFN}
    sib = {n: big[n][1] for n in _FFN}
    recv1, = _run_comm([_rs_sibling_hosted([big[n] for n in late])], "tail_sibling")
    pairs = [_add_own_half(big[n], r, "add_half_" + n, F32 if n == "slab" else BF16) for n, r in zip(late, recv1)]
    chips = _rs_chips_hosted([pb for _, pb in pairs])
    state, token = _split_start(chips, "tail_chips_start")
    out = {}
    for n in _FFN:
        out[n] = tuple(_adamw_big(w[n], fin[n], sib[n], m[n], v[n], "adamw_" + n, token))
    recv2 = _split_wait(chips, state, out[_FFN[-1]][1], "tail_chips_wait")
    for n, (p, _), r in zip(late, pairs, recv2):
        fin[n] = _sum_chips(p, r, "sum_chips_" + n)
    swapped, = _run_comm([_rs_swap_hosted([fin[n] for n in late])], "tail_swap")
    sib.update(zip(late, swapped))
    for n in late[:2]:
        out[n] = tuple(_adamw_big(w[n], fin[n], sib[n], m[n], v[n], "adamw_" + n))
    for n in _TRANSPOSED:
        out[n] = tuple(jnp.transpose(a, (0, 2, 1)) for a in out[n])
    wmv = {n: (lay(n, w[n]), lay(n, m[n]), lay(n, v[n])) for n in _SMALL_ORDER}
    res, loss = _adamw_small(fin["slab"], sib["slab"], wmv)
    for n in _SMALL_ORDER:
        out[n] = tuple(unlay(n, a) for a in res[n])
    return (loss[0, 0], gx[None]) + tuple(out[n][k] for k in range(4) for n in _WEIGHTS)
```

```python
import functools
import math

import jax
import jax.numpy as jnp
from jax import lax
from jax.experimental import pallas as pl
from jax.experimental.pallas import tpu as pltpu

F32 = jnp.float32
BF16 = jnp.bfloat16
SDS = jax.ShapeDtypeStruct
MESH = pl.DeviceIdType.MESH

EPS = 1e-6
LRU_C = 8.0
CONV_WIDTH = 4
POOL_WINDOWS = (2, 4, 8, 16)
HALO = 16
LANES = 128
SUBLANES = 8
GATE_BLOCK = 256
N_CHIPS = 4

ADAM_LR = 0.001
ADAM_B1 = 0.9
ADAM_B2 = 0.999
ADAM_EPS = 1e-08
ADAM_WD = 0.01
ADAM_STEP = 10

TM_PROJ = 512
TM_MIX = 512
TM_FFN = 512
TM_WGRAD = 2048
MIX_SAVED = ("xc", "r", "ig", "a", "m2raw", "ge", "dge")
FFN_ROW_CHUNKS = 2
VMEM_LIMIT = 56 * 1024 * 1024

SLAB_W = 512
ROW_CONV_B, ROW_CONV_W, ROW_BA, ROW_BX, ROW_LAM, ROW_PB, ROW_PS, ROW_GL, ROW_GP = 0, 1, 5, 6, 7, 8, 9, 10, 11
ROW_GA, ROW_GX, ROW_PW = 16, 80, 144
ROW_MIX, ROW_FFN, ROW_FIN, ROW_LOSS = 272, 274, 276, 278
MIX_SLAB_ROWS = 272
SLAB_ROWS = 288


def _cp(sem=None, **kw):
    if sem is not None:
        kw["dimension_semantics"] = sem
    return pltpu.CompilerParams(vmem_limit_bytes=VMEM_LIMIT, **kw)


def _const_spec(shape):
    nd = len(shape)
    return pl.BlockSpec(shape, lambda *_: (0,) * nd, pipeline_mode=pl.Buffered(1))


def _sigmoid(x):
    return 1.0 / (1.0 + jnp.exp(-x))


def _dot(a, b):
    return jnp.dot(a, b, preferred_element_type=F32)


def _dot_nt(a, b):
    return lax.dot_general(a, b, (((1,), (1,)), ((), ())), preferred_element_type=F32)


def _dot_tn(a, b):
    return lax.dot_general(a, b, (((0,), (0,)), ((), ())), preferred_element_type=F32)


def _colsum8(v):
    m, c = v.shape
    return v.reshape(m // SUBLANES, SUBLANES, c).sum(axis=0)


def _rowmean(v):
    return jnp.mean(v, axis=-1, keepdims=True)


def _rms_bwd(dy, xhat, r, g):
    dxh = dy * g
    return r * (dxh - xhat * _rowmean(dxh * xhat))


def _softplus_neg(lam):
    z = -lam
    e = jnp.exp(-jnp.abs(z))
    u = 1.0 + e
    d = u - 1.0
    log1p = jnp.where(d == 0.0, e, jnp.log(u) * (e / jnp.where(d == 0.0, 1.0, d)))
    return jnp.maximum(z, 0.0) + log1p


def _neg_expm1(z):
    series = -(z * (1.0 + z * (0.5 + z * (1.0 / 6.0 + z * (1.0 / 24.0)))))
    return jnp.where(z > -0.03, series, 1.0 - jnp.exp(z))


_GELU_C = math.sqrt(2.0 / math.pi)
_GELU_K = 0.044715


def _gelu_parts(x):
    x2 = x * x
    th = jnp.tanh(_GELU_C * (x + _GELU_K * x2 * x))
    ge = 0.5 * x * (1.0 + th)
    dge = 0.5 * (1.0 + th) + 0.5 * x * (1.0 - th * th) * (_GELU_C * (1.0 + 3.0 * _GELU_K * x2))
    return ge, dge


def _shift_down(halo, tile, k):
    if k == 0:
        return tile
    ext = jnp.concatenate([halo, tile], axis=0)
    n = tile.shape[0]
    h = halo.shape[0]
    return ext[h - k:h - k + n]


def _shift_up(tile, nxt, k):
    if k == 0:
        return tile
    ext = jnp.concatenate([tile, nxt], axis=0)
    return ext[k:k + tile.shape[0]]


def _build_gate_blocks(ga_ref, gx_ref, gw_ref):
    hd = ga_ref.shape[1]
    per = GATE_BLOCK // hd
    zero = jnp.zeros((hd, hd), F32)
    for b in range(gw_ref.shape[0]):
        for src, off in ((ga_ref, 0), (gx_ref, GATE_BLOCK)):
            for hh in range(per):
                row = jnp.concatenate([zero] * hh + [src[b * per + hh]] + [zero] * (per - 1 - hh), axis=1)
                gw_ref[b, hh * hd:(hh + 1) * hd, off:off + GATE_BLOCK] = row.astype(BF16)


def _scan_level1(a, b, reverse):
    m, c = a.shape
    a3 = a.reshape(m // SUBLANES, SUBLANES, c)
    b3 = b.reshape(m // SUBLANES, SUBLANES, c)
    row = lax.broadcasted_iota(jnp.int32, a3.shape, 1)
    for s in (1, 2, 4):
        sh = (SUBLANES - s) if reverse else s
        a_sh = pltpu.roll(a3, sh, 1)
        b_sh = pltpu.roll(b3, sh, 1)
        ok = (row < SUBLANES - s) if reverse else (row >= s)
        b3 = jnp.where(ok, a3 * b_sh + b3, b3)
        a3 = jnp.where(ok, a3 * a_sh, a3)
    return a3.reshape(m, c), b3.reshape(m, c)


def _scan_level2(a_ref, b_ref, out_ref, carry, reverse):
    m, c = a_ref.shape
    ng = m // SUBLANES

    def step(g, cr):
        gi = (ng - 1 - g) if reverse else g
        off = pl.multiple_of(gi * SUBLANES, SUBLANES)
        h = b_ref[pl.ds(off, SUBLANES), :] + a_ref[pl.ds(off, SUBLANES), :] * cr
        out_ref[pl.ds(off, SUBLANES), :] = h
        edge = h[0:1, :] if reverse else h[SUBLANES - 1:SUBLANES, :]
        return jnp.broadcast_to(edge, (SUBLANES, c))

    return lax.fori_loop(0, ng, step, carry, unroll=4)


def _mixer_recompute(u, hal, t0, cw, cb, gw_ref, ba, bx, lam, pw_ref, pb, ps):
    tm = u.shape[0]
    lw = cb.shape[1]
    u_l, u_g, u_p = u[:, :lw], u[:, lw:2 * lw], u[:, 2 * lw:]
    hal_l, hal_p = hal[:, :lw], hal[:, 2 * lw:]
    taps = [_shift_down(hal_l, u_l, CONV_WIDTH - 1 - k) for k in range(CONV_WIDTH)]
    xc = cb
    for k in range(CONV_WIDTH):
        xc = xc + taps[k] * cw[k:k + 1, :]
    xcb = xc.astype(BF16)
    nb = lw // GATE_BLOCK
    gs = [_dot(xcb[:, b * GATE_BLOCK:(b + 1) * GATE_BLOCK], gw_ref[b]) for b in range(nb)]
    r = _sigmoid(jnp.concatenate([g[:, :GATE_BLOCK] for g in gs], axis=1) + ba)
    ig = _sigmoid(jnp.concatenate([g[:, GATE_BLOCK:] for g in gs], axis=1) + bx)
    sp = _softplus_neg(lam)
    la = (-LRU_C * r) * sp
    a = jnp.exp(la)
    m2raw = _neg_expm1(2.0 * la)
    mult = jnp.sqrt(jnp.maximum(m2raw, 1e-12))
    ge, dge = _gelu_parts(u_g)
    row = lax.broadcasted_iota(jnp.int32, (tm, LANES), 0) + t0
    pooled, invs, zs = [], [], []
    for gi, w in enumerate(POOL_WINDOWS):
        e = jnp.concatenate([hal_p[:, gi * LANES:(gi + 1) * LANES], u_p[:, gi * LANES:(gi + 1) * LANES]], axis=0)
        s = e
        k = 1
        while k < w:
            s = s + pltpu.roll(s, k, 0)
            k *= 2
        inv = 1.0 / jnp.minimum(row + 1, w).astype(F32)
        pg = s[HALO:] * inv - e[HALO:]
        pooled.append(pg)
        invs.append(inv)
        zs.append(_dot(pg.astype(BF16), pw_ref[gi].astype(BF16)))
    z = jnp.concatenate(zs, axis=1) + pb
    y_pool = z * ps
    return dict(u_l=u_l, u_g=u_g, taps=taps, xc=xc, xcb=xcb, r=r, ig=ig, sp=sp, la=la, a=a, m2raw=m2raw,
                mult=mult, ge=ge, dge=dge, pooled=pooled, invs=invs, z=z, y_pool=y_pool)


ANY = pl.BlockSpec(memory_space=pl.ANY)
VMEM_SPEC = pl.BlockSpec(memory_space=pltpu.VMEM)


class _Hosted:
    def __init__(self, ins, out_shapes, sems, start, finish, mid=None, aliases=None):
        self.ins, self.out_shapes, self.sems = list(ins), list(out_shapes), list(sems)
        self.start, self.mid, self.finish = start, mid, finish
        self.aliases = dict(aliases or {})


def _call(body, hosted, stage_preds, *, name, grid, in_specs, out_specs, out_shape, scratch_shapes, args, sem):
    hosted = list(hosted or [])
    n_in, n_out, n_scr = len(in_specs), len(out_specs), len(scratch_shapes)
    c_in = [a for h in hosted for a in h.ins]
    c_out = [o for h in hosted for o in h.out_shapes]
    c_sem = [pltpu.SemaphoreType.DMA((k,)) for h in hosted for k in h.sems]

    def full(*refs):
        p = 0
        parts = []
        for cnt in (n_in, len(c_in), n_out, len(c_out), n_scr, len(c_sem)):
            parts.append(refs[p:p + cnt])
            p += cnt
        hi, ci, ho, co, hs, cs = parts
        per = []
        a = b = c_ = 0
        for h in hosted:
            per.append((h, ci[a:a + len(h.ins)], co[b:b + len(h.out_shapes)], cs[c_:c_ + len(h.sems)]))
            a, b, c_ = a + len(h.ins), b + len(h.out_shapes), c_ + len(h.sems)
        first = mid = last = None
        if hosted and grid:
            first, mid, last = stage_preds()

        def run(fn, pred, i_, o_, s_):
            if fn is None:
                return
            if pred is None:
                fn(i_, o_, s_)
            else:
                pl.when(pred)(functools.partial(fn, i_, o_, s_))

        for h, i_, o_, s_ in per:
            run(h.start, first, i_, o_, s_)
        body(*hi, *ho, *hs)
        for h, i_, o_, s_ in per:
            run(h.mid, mid, i_, o_, s_)
        for h, i_, o_, s_ in per:
            run(h.finish, last, i_, o_, s_)

    aliases = {}
    a = b = 0
    for h in hosted:
        for k, v in h.aliases.items():
            aliases[n_in + a + k] = n_out + b + v
        a, b = a + len(h.ins), b + len(h.out_shapes)
    res = pl.pallas_call(
        full, name=name, grid=grid, in_specs=list(in_specs) + [ANY] * len(c_in),
        out_specs=list(out_specs) + [ANY] * len(c_out), out_shape=list(out_shape) + c_out,
        scratch_shapes=list(scratch_shapes) + c_sem, input_output_aliases=aliases,
        compiler_params=_cp(sem))(*args, *c_in)
    res = list(res)
    outs = []
    p = n_out
    for h in hosted:
        outs.append(res[p:p + len(h.out_shapes)])
        p += len(h.out_shapes)
    return res[:n_out], outs


def _inproj(x, g_mix, w_in, hosted=None):
    s, d = x.shape
    n = w_in.shape[1]
    tm = min(TM_PROJ, s)
    nt = s // tm

    def body(x_ref, g_ref, w_ref, u_ref):
        xv = x_ref[...]
        r = lax.rsqrt(_rowmean(xv * xv) + EPS)
        u_ref[...] = _dot((xv * r * g_ref[...]).astype(BF16), w_ref[...])

    def stages():
        i = pl.program_id(0)
        return i == 0, i == max(nt - 3, 0), i == nt - 1

    return _call(
        body, hosted, stages, grid=(nt,), name="inproj",
        in_specs=[pl.BlockSpec((tm, d), lambda i: (i, 0)), _const_spec((1, d)), _const_spec((d, n))],
        out_specs=[pl.BlockSpec((tm, n), lambda i: (i, 0))], out_shape=[SDS((s, n), F32)], scratch_shapes=[],
        args=(x, g_mix, w_in), sem=("arbitrary",))


def _mixer_fwd(u, x, sp_, w_out, hosted=None):
    s, din = u.shape
    d = x.shape[1]
    lw = din // 3
    tm = min(TM_MIX, s)
    nb = lw // GATE_BLOCK

    def body(u_ref, halo_ref, x_ref, cw_ref, cb_ref, ga_ref, gx_ref, ba_ref, bx_ref, lam_ref, pw_ref, pb_ref,
             ps_ref, gl_ref, gp_ref, wout_ref, h_ref, hres_ref, saved_ref, pooled_ref,
             gw_s, a_s, b_s, carry_s):
        i = pl.program_id(0)

        @pl.when(i == 0)
        def _():
            _build_gate_blocks(ga_ref, gx_ref, gw_s)
            carry_s[...] = jnp.zeros_like(carry_s)

        uv = u_ref[...]
        hal = jnp.where(i > 0, halo_ref[...], 0.0)
        f = _mixer_recompute(uv, hal, i * tm, cw_ref[...], cb_ref[...], gw_s, ba_ref[...], bx_ref[...],
                             lam_ref[...], pw_ref, pb_ref[...], ps_ref[...])
        for k, name in enumerate(MIX_SAVED):
            saved_ref[k] = f[name]
        pooled_ref[...] = jnp.concatenate(f["pooled"], axis=1).astype(BF16)
        bb = f["mult"] * (f["ig"] * f["xc"])
        a1, b1 = _scan_level1(f["a"], bb, reverse=False)
        a_s[...] = a1
        b_s[...] = b1
        carry_s[...] = _scan_level2(a_s, b_s, h_ref, carry_s[...], reverse=False)
        y_lru = h_ref[...] * f["ge"]
        rl = lax.rsqrt(_rowmean(y_lru * y_lru) + EPS)
        yp = f["y_pool"]
        rp = lax.rsqrt(_rowmean(yp * yp) + EPS)
        yn = jnp.concatenate([y_lru * rl * gl_ref[...], yp * rp * gp_ref[...]], axis=1).astype(BF16)
        hres_ref[...] = x_ref[...] + _dot(yn, wout_ref[...])

    small = [sp_[k] for k in ("conv_w", "conv_b", "gate_a_w", "gate_x_w", "gate_a_b", "gate_x_b", "lru_lambda",
                              "pool_w", "pool_b", "pool_scale", "norm_lru_g", "norm_pool_g")]
    nt = s // tm

    def stages():
        i = pl.program_id(0)
        return i == 0, i == max(nt - 3, 0), i == nt - 1

    return _call(
        body, hosted, stages, grid=(nt,), name="mixer_fwd",
        in_specs=[pl.BlockSpec((tm, din), lambda i: (i, 0)),
                  pl.BlockSpec((HALO, din), lambda i: (jnp.maximum(i * (tm // HALO) - 1, 0), 0)),
                  pl.BlockSpec((tm, d), lambda i: (i, 0))]
        + [_const_spec(a.shape) for a in small] + [_const_spec(w_out.shape)],
        out_specs=[pl.BlockSpec((tm, lw), lambda i: (i, 0)), pl.BlockSpec((tm, d), lambda i: (i, 0)),
                   pl.BlockSpec((len(MIX_SAVED), tm, lw), lambda i: (0, i, 0)),
                   pl.BlockSpec((tm, lw), lambda i: (i, 0))],
        out_shape=[SDS((s, lw), F32), SDS((s, d), F32), SDS((len(MIX_SAVED), s, lw), F32),
                   SDS((s, lw), BF16)],
        scratch_shapes=[pltpu.VMEM((nb, GATE_BLOCK, 2 * GATE_BLOCK), BF16), pltpu.VMEM((tm, lw), F32),
                        pltpu.VMEM((tm, lw), F32), pltpu.VMEM((SUBLANES, lw), F32)],
        args=(u, u, x, *small, w_out), sem=("arbitrary",))


def _row_chunks(tm):
    rc = tm // FFN_ROW_CHUNKS
    return [slice(q * rc, (q + 1) * rc) for q in range(FFN_ROW_CHUNKS)]


def _ffn_up(hres1, g_ffn, w1, w3):
    s, d = hres1.shape
    nj, fc, _ = w1.shape
    tm = min(TM_FFN, s)

    def body(h_ref, gf_ref, w1_ref, w3_ref, h2_ref, a1_ref, a3_ref, ff_ref):
        hv = h_ref[...]
        r = lax.rsqrt(_rowmean(hv * hv) + EPS)
        h2_ref[...] = (hv * r * gf_ref[...]).astype(BF16)
        h2 = h2_ref[...]
        for j in range(nj):
            a1 = _dot_nt(h2, w1_ref[j])
            a3 = _dot_nt(h2, w3_ref[j])
            a1_ref[j] = a1.astype(BF16)
            a3_ref[j] = a3.astype(BF16)
            ff_ref[j] = ((a1 * _sigmoid(a1)) * a3).astype(BF16)

    wspec = _const_spec(w1.shape)
    aspec = pl.BlockSpec((nj, tm, fc), lambda i: (0, i, 0))
    return pl.pallas_call(
        body, grid=(s // tm,), name="ffn_up",
        in_specs=[pl.BlockSpec((tm, d), lambda i: (i, 0)), _const_spec((1, d)), wspec, wspec],
        out_specs=[pl.BlockSpec((tm, d), lambda i: (i, 0)), aspec, aspec, aspec],
        out_shape=[SDS((s, d), BF16)] + [SDS((nj, s, fc), BF16)] * 3,
        compiler_params=_cp(("parallel",)))(hres1, g_ffn, w1, w3)


def _ffn_down(ff, hres1, target, g_fin, w2):
    s, d = hres1.shape
    nj, _, fc = ff.shape
    tm = min(TM_FFN, s)

    def body(ff_ref, h_ref, t_ref, gn_ref, w2_ref, dh_ref, dhb_ref, loss_ref, dgn_ref):
        @pl.when(pl.program_id(0) == 0)
        def _():
            loss_ref[...] = jnp.zeros_like(loss_ref)
            dgn_ref[...] = jnp.zeros_like(dgn_ref)

        gn = gn_ref[...]
        for rows in _row_chunks(tm):
            acc = _dot(ff_ref[0, rows, :], w2_ref[0])
            for j in range(1, nj):
                acc = acc + _dot(ff_ref[j, rows, :], w2_ref[j])
            hr2 = h_ref[rows, :] + acc
            r2 = lax.rsqrt(_rowmean(hr2 * hr2) + EPS)
            xh = hr2 * r2
            diff = xh * gn - t_ref[rows, :]
            tot = jnp.sum(jnp.sum(diff * diff, axis=1, keepdims=True), axis=0, keepdims=True)
            loss_ref[...] += tot * (0.5 / d)
            dout = diff * (1.0 / d)
            dgn_ref[...] += _colsum8(dout * xh)
            dh = _rms_bwd(dout, xh, r2, gn)
            dh_ref[rows, :] = dh
            dhb_ref[rows, :] = dh.astype(BF16)

    tile = pl.BlockSpec((tm, d), lambda i: (i, 0))
    return pl.pallas_call(
        body, grid=(s // tm,), name="ffn_down",
        in_specs=[pl.BlockSpec((nj, tm, fc), lambda i: (0, i, 0)), tile, tile, _const_spec((1, d)),
                  _const_spec(w2.shape)],
        out_specs=[tile, tile, pl.BlockSpec((SUBLANES, LANES), lambda i: (0, 0)),
                   pl.BlockSpec((SUBLANES, d), lambda i: (0, 0))],
        out_shape=[SDS((s, d), F32), SDS((s, d), BF16), SDS((SUBLANES, LANES), F32), SDS((SUBLANES, d), F32)],
        compiler_params=_cp(("arbitrary",)))(ff, hres1, target, g_fin, w2)


def _ffn_bwd_gate(dhb, a1, a3, w2):
    s, d = dhb.shape
    nj, _, fc = a1.shape
    tm = min(TM_FFN, s)

    def body(dhb_ref, a1_ref, a3_ref, w2_ref, da1_ref, da3_ref):
        for j in range(nj):
            for rows in _row_chunks(tm):
                dff = _dot_nt(dhb_ref[rows, :], w2_ref[j])
                a1v = a1_ref[j, rows, :].astype(F32)
                sg = _sigmoid(a1v)
                silu = a1v * sg
                da1_ref[j, rows, :] = (dff * a3_ref[j, rows, :].astype(F32)
                                       * (sg * (1.0 + (a1v - silu)))).astype(BF16)
                da3_ref[j, rows, :] = (dff * silu).astype(BF16)

    aspec = pl.BlockSpec((nj, tm, fc), lambda i: (0, i, 0))
    return pl.pallas_call(
        body, grid=(s // tm,), name="ffn_bwd_gate",
        in_specs=[pl.BlockSpec((tm, d), lambda i: (i, 0)), aspec, aspec, _const_spec(w2.shape)],
        out_specs=[aspec, aspec], out_shape=[SDS((nj, s, fc), BF16)] * 2,
        compiler_params=_cp(("parallel",)))(dhb, a1, a3, w2)


def _ffn_bwd_down(da1, da3, dh, hres1, g_ffn, w1, w3, hosted=None):
    s, d = hres1.shape
    nj, _, fc = da1.shape
    tm = min(TM_FFN, s)
    nt = s // tm

    def body(da1_ref, da3_ref, dh_ref, h_ref, gf_ref, w1_ref, w3_ref, dhr_ref, dgf_ref):
        @pl.when(pl.program_id(0) == 0)
        def _():
            dgf_ref[...] = jnp.zeros_like(dgf_ref)

        gf = gf_ref[...]
        for rows in _row_chunks(tm):
            dh2 = None
            for j in range(nj):
                part = _dot(da1_ref[j, rows, :], w1_ref[j]) + _dot(da3_ref[j, rows, :], w3_ref[j])
                dh2 = part if dh2 is None else dh2 + part
            hv = h_ref[rows, :]
            r = lax.rsqrt(_rowmean(hv * hv) + EPS)
            xh = hv * r
            dgf_ref[...] += _colsum8(dh2 * xh)
            dhr_ref[rows, :] = dh_ref[rows, :] + _rms_bwd(dh2, xh, r, gf)

    tile = pl.BlockSpec((tm, d), lambda i: (i, 0))
    aspec = pl.BlockSpec((nj, tm, fc), lambda i: (0, i, 0))
    wspec = _const_spec(w1.shape)

    def stages():
        i = pl.program_id(0)
        return i == 0, i == max(nt - 2, 0), i == nt - 1

    return _call(
        body, hosted, stages, grid=(nt,), name="ffn_bwd_down",
        in_specs=[aspec, aspec, tile, tile, _const_spec((1, d)), wspec, wspec],
        out_specs=[tile, pl.BlockSpec((SUBLANES, d), lambda i: (0, 0))],
        out_shape=[SDS((s, d), F32), SDS((SUBLANES, d), F32)],
        scratch_shapes=[], args=(da1, da3, dh, hres1, g_ffn, w1, w3), sem=("arbitrary",))


def _ffn_wgrad(h2, dhb, ff, da1, da3):
    s, d = h2.shape
    _, _, fc = ff.shape
    tm = min(TM_WGRAD, s)

    def body(h2_ref, dhb_ref, ff_ref, da1_ref, da3_ref, dw1_ref, dw3_ref, dw2_ref):
        @pl.when(pl.program_id(1) == 0)
        def _():
            dw1_ref[...] = jnp.zeros_like(dw1_ref)
            dw3_ref[...] = jnp.zeros_like(dw3_ref)
            dw2_ref[...] = jnp.zeros_like(dw2_ref)

        h2v = h2_ref[...]
        dw1_ref[0] += _dot_tn(da1_ref[0], h2v)
        dw3_ref[0] += _dot_tn(da3_ref[0], h2v)
        dw2_ref[0] += _dot_tn(ff_ref[0], dhb_ref[...])

    wspec = pl.BlockSpec((1, fc, d), lambda j, i: (j, 0, 0))
    return pl.pallas_call(
        body, grid=(N_CHIPS, s // tm), name="ffn_wgrad",
        in_specs=[pl.BlockSpec((tm, d), lambda j, i: (i, 0)), pl.BlockSpec((tm, d), lambda j, i: (i, 0))]
        + [pl.BlockSpec((1, tm, fc), lambda j, i: (j, i, 0))] * 3,
        out_specs=[wspec] * 3, out_shape=[SDS((N_CHIPS, fc, d), F32)] * 3,
        compiler_params=_cp(("parallel", "arbitrary")))(h2, dhb, ff, da1, da3)


def _mixer_bwd(u, saved, pooled, h, dhres1, sp_, w_out, hosted=None):
    s, din = u.shape
    d = dhres1.shape[1]
    lw = din // 3
    tm = min(TM_MIX, s)
    nt = s // tm
    nb = lw // GATE_BLOCK
    hd = sp_["gate_a_w"].shape[1]

    def body(ul_ref, saved_ref, pooled_ref, h_ref, hhalo_ref, dhr_ref, cw_ref, cb_ref, ga_ref, gx_ref, ba_ref,
             bx_ref, lam_ref, pw_ref, pb_ref, ps_ref, gl_ref, gp_ref, wout_ref, du_ref, slab_ref, dwout_ref,
             gw_s, a_s, b_s, e_s, ecarry_s, dxc_s, q_s, vec_s, cwacc_s, dgw_s, dpw_s):
        i = pl.program_id(0)
        tile = nt - 1 - i

        @pl.when(i == 0)
        def _():
            _build_gate_blocks(ga_ref, gx_ref, gw_s)
            for ref in (ecarry_s, dxc_s, q_s, vec_s, cwacc_s, dgw_s, dpw_s, dwout_ref):
                ref[...] = jnp.zeros_like(ref)

        cw = cw_ref[...]
        lam = lam_ref[...]
        ps = ps_ref[...]
        def sv_(name):
            return saved_ref[MIX_SAVED.index(name)]

        f = {"sp": _softplus_neg(lam)}
        row = lax.broadcasted_iota(jnp.int32, (tm, LANES), 0) + tile * tm
        f["z"] = jnp.concatenate(
            [_dot(pooled_ref[:, g * LANES:(g + 1) * LANES], pw_ref[g].astype(BF16))
             for g in range(len(POOL_WINDOWS))], axis=1) + pb_ref[...]
        f["y_pool"] = f["z"] * ps
        y_lru = h_ref[...] * sv_("ge")
        rl = lax.rsqrt(_rowmean(y_lru * y_lru) + EPS)
        yp = f["y_pool"]
        rp = lax.rsqrt(_rowmean(yp * yp) + EPS)
        xh_l = y_lru * rl
        xh_p = yp * rp

        dhrb = dhr_ref[...].astype(BF16)
        dyn = _dot_nt(dhrb, wout_ref[...])
        yn = jnp.concatenate([xh_l * gl_ref[...], xh_p * gp_ref[...]], axis=1).astype(BF16)
        dwout_ref[...] += _dot_tn(yn, dhrb)
        d_nl, d_np = dyn[:, :lw], dyn[:, lw:]
        vec = {}
        vec[ROW_GL] = _colsum8(d_nl * xh_l)
        vec[ROW_GP] = _colsum8(d_np * xh_p)
        d_ylru = _rms_bwd(d_nl, xh_l, rl, gl_ref[...])
        d_ypool = _rms_bwd(d_np, xh_p, rp, gp_ref[...])

        vec[ROW_PS] = _colsum8(d_ypool * f["z"])
        dz = d_ypool * ps
        vec[ROW_PB] = _colsum8(dz)
        dzb = dz.astype(BF16)
        dup = []
        for gi, w in enumerate(POOL_WINDOWS):
            sl = slice(gi * LANES, (gi + 1) * LANES)
            dpw_s[:, sl] += _dot_tn(pooled_ref[:, sl], dzb[:, sl])
            dpool = _dot_nt(dzb[:, sl], pw_ref[gi].astype(BF16))
            q = dpool * (1.0 / jnp.minimum(row + 1, w).astype(F32))
            e = jnp.concatenate([q, q_s[:, sl]], axis=0)
            k = 1
            while k < w:
                e = e + pltpu.roll(e, tm + HALO - k, 0)
                k *= 2
            dup.append(e[:tm] - dpool)
            q_s[:, sl] = q[:HALO]

        d_hout = d_ylru * sv_("ge")
        d_ug = d_ylru * h_ref[...] * sv_("dge")
        a1, b1 = _scan_level1(sv_("a"), sv_("a") * d_hout, reverse=True)
        a_s[...] = a1
        b_s[...] = b1
        e_next = ecarry_s[...]
        ecarry_s[...] = _scan_level2(a_s, b_s, e_s, e_next, reverse=True)
        sv = d_hout + _shift_up(e_s[...], e_next, 1)
        d_a = sv * _shift_down(jnp.where(tile > 0, hhalo_ref[...], 0.0), h_ref[...], 1)
        mult = jnp.sqrt(jnp.maximum(sv_("m2raw"), 1e-12))
        d_mult = sv * (sv_("ig") * sv_("xc"))
        d_ig = sv * mult * sv_("xc")
        d_xc = sv * mult * sv_("ig")
        a = sv_("a")
        d_la = d_a * a + jnp.where(sv_("m2raw") > 1e-12, d_mult * (-(a * a) / mult), 0.0)
        d_r = d_la * (-LRU_C * f["sp"])
        vec[ROW_LAM] = _colsum8(d_la * (-LRU_C * sv_("r")))
        d_pr = d_r * sv_("r") * (1.0 - sv_("r"))
        d_pi = d_ig * sv_("ig") * (1.0 - sv_("ig"))
        vec[ROW_BA] = _colsum8(d_pr)
        vec[ROW_BX] = _colsum8(d_pi)
        dxc_parts = []
        for b in range(nb):
            sl = slice(b * GATE_BLOCK, (b + 1) * GATE_BLOCK)
            rhs = jnp.concatenate([d_pr[:, sl], d_pi[:, sl]], axis=1).astype(BF16)
            dgw_s[b] += _dot_tn(saved_ref[MIX_SAVED.index("xc"), :, sl].astype(BF16), rhs)
            dxc_parts.append(_dot_nt(rhs, gw_s[b]))
        d_xc = d_xc + jnp.concatenate(dxc_parts, axis=1)
        vec[ROW_CONV_B] = _colsum8(d_xc)
        dxc_next = dxc_s[...]
        d_ul = None
        for k in range(CONV_WIDTH):
            ahead = _shift_up(d_xc, dxc_next, CONV_WIDTH - 1 - k)
            cwacc_s[k * SUBLANES:(k + 1) * SUBLANES, :] += _colsum8(ahead * ul_ref[...])
            term = ahead * cw[k:k + 1, :]
            d_ul = term if d_ul is None else d_ul + term
        dxc_s[...] = d_xc[:SUBLANES]
        for row, val in vec.items():
            vec_s[row * SUBLANES:(row + 1) * SUBLANES, :] += val
        du_ref[...] = jnp.concatenate([d_ul, d_ug] + dup, axis=1).astype(BF16)

        @pl.when(i == nt - 1)
        def _():
            rows = []
            for row in range(ROW_GA):
                if row in (ROW_CONV_W, ROW_CONV_W + 1, ROW_CONV_W + 2, ROW_CONV_W + 3):
                    k = row - ROW_CONV_W
                    v = jnp.sum(cwacc_s[k * SUBLANES:(k + 1) * SUBLANES, :], axis=0, keepdims=True)
                elif row <= ROW_GP:
                    v = jnp.sum(vec_s[row * SUBLANES:(row + 1) * SUBLANES, :], axis=0, keepdims=True)
                    if row == ROW_LAM:
                        v = v * (-1.0 / (1.0 + jnp.exp(lam)))
                else:
                    v = jnp.zeros((1, lw), F32)
                rows.append(v)
            slab_ref[0:ROW_GA, :] = jnp.concatenate(rows, axis=0)
            lane = lax.broadcasted_iota(jnp.int32, (hd, GATE_BLOCK), 1)
            for b in range(nb):
                for off, row0 in ((0, ROW_GA), (GATE_BLOCK, ROW_GX)):
                    acc = jnp.zeros((hd, GATE_BLOCK), F32)
                    for hh in range(GATE_BLOCK // hd):
                        m = (lane >= hh * hd) & (lane < (hh + 1) * hd)
                        acc = acc + jnp.where(m, dgw_s[b, hh * hd:(hh + 1) * hd, off:off + GATE_BLOCK], 0.0)
                    slab_ref[row0:row0 + hd, b * GATE_BLOCK:(b + 1) * GATE_BLOCK] = acc
            slab_ref[ROW_PW:ROW_PW + LANES, :] = dpw_s[...]

    small = [sp_[k] for k in ("conv_w", "conv_b", "gate_a_w", "gate_x_w", "gate_a_b", "gate_x_b", "lru_lambda",
                              "pool_w", "pool_b", "pool_scale", "norm_lru_g", "norm_pool_g")]
    rev = lambda i: nt - 1 - i

    def stages():
        i = pl.program_id(0)
        return i == 0, i == max(nt - 3, 0), i == nt - 1

    return _call(
        body, hosted, stages, grid=(nt,), name="mixer_bwd",
        in_specs=[pl.BlockSpec((tm, lw), lambda i: (rev(i), 0)),
                  pl.BlockSpec((len(MIX_SAVED), tm, lw), lambda i: (0, rev(i), 0)),
                  pl.BlockSpec((tm, lw), lambda i: (rev(i), 0)),
                  pl.BlockSpec((tm, lw), lambda i: (rev(i), 0)),
                  pl.BlockSpec((SUBLANES, lw), lambda i: (jnp.maximum(rev(i) * (tm // SUBLANES) - 1, 0), 0)),
                  pl.BlockSpec((tm, d), lambda i: (rev(i), 0))]
        + [_const_spec(a.shape) for a in small] + [_const_spec(w_out.shape)],
        out_specs=[pl.BlockSpec((tm, din), lambda i: (rev(i), 0)),
                   pl.BlockSpec((MIX_SLAB_ROWS, SLAB_W), lambda i: (0, 0)), pl.BlockSpec((d, d), lambda i: (0, 0))],
        out_shape=[SDS((s, din), BF16), SDS((MIX_SLAB_ROWS, SLAB_W), F32), SDS((d, d), F32)],
        scratch_shapes=[pltpu.VMEM((nb, GATE_BLOCK, 2 * GATE_BLOCK), BF16),
                        pltpu.VMEM((tm, lw), F32), pltpu.VMEM((tm, lw), F32), pltpu.VMEM((tm, lw), F32),
                        pltpu.VMEM((SUBLANES, lw), F32), pltpu.VMEM((SUBLANES, lw), F32),
                        pltpu.VMEM((HALO, lw), F32), pltpu.VMEM((ROW_GA * SUBLANES, lw), F32),
                        pltpu.VMEM((CONV_WIDTH * SUBLANES, lw), F32),
                        pltpu.VMEM((nb, GATE_BLOCK, 2 * GATE_BLOCK), F32), pltpu.VMEM((LANES, lw), F32)],
        args=(u, saved, pooled, h, h, dhres1, *small, w_out), sem=("arbitrary",))


def _inproj_bwd(x, du, dhres1, g_mix, w_in, hosted=None):
    s, d = x.shape
    n = w_in.shape[1]
    nc = n // N_CHIPS
    tm = min(TM_PROJ, s)
    nt = s // tm

    def body(x_ref, du_ref, dhr_ref, g_ref, w_ref, gx_ref, dwin_ref, dg_ref):
        i = pl.program_id(0)

        @pl.when(i == 0)
        def _():
            dwin_ref[...] = jnp.zeros_like(dwin_ref)
            dg_ref[...] = jnp.zeros_like(dg_ref)

        xv = x_ref[...]
        g = g_ref[...]
        r = lax.rsqrt(_rowmean(xv * xv) + EPS)
        xh = xv * r
        h1 = (xh * g).astype(BF16)
        duv = du_ref[...]
        dh1 = _dot_nt(duv, w_ref[...])
        dg_ref[...] += _colsum8(dh1 * xh)
        gx_ref[...] = dhr_ref[...] + _rms_bwd(dh1, xh, r, g)
        for jj in range(N_CHIPS):
            dwin_ref[jj] += _dot_tn(h1, duv[:, jj * nc:(jj + 1) * nc])

    def stages():
        i = pl.program_id(0)
        return i == 0, i == max(nt - 3, 0), i == nt - 1

    return _call(
        body, hosted, stages, grid=(nt,), name="inproj_bwd",
        in_specs=[pl.BlockSpec((tm, d), lambda i: (i, 0)), pl.BlockSpec((tm, n), lambda i: (i, 0)),
                  pl.BlockSpec((tm, d), lambda i: (i, 0)), _const_spec((1, d)), _const_spec((d, n))],
        out_specs=[pl.BlockSpec((tm, d), lambda i: (i, 0)), pl.BlockSpec((N_CHIPS, d, nc), lambda i: (0, 0, 0)),
                   pl.BlockSpec((SUBLANES, d), lambda i: (0, 0))],
        out_shape=[SDS((s, d), F32), SDS((N_CHIPS, d, nc), F32), SDS((SUBLANES, d), F32)],
        scratch_shapes=[], args=(x, du, dhres1, g_mix, w_in), sem=("arbitrary",))


def _place():
    x, y, c = lax.axis_index("x"), lax.axis_index("y"), lax.axis_index("c")
    return x, y, c


def _other_chips(x, y):
    return [(1 - x, y), (x, 1 - y), (1 - x, 1 - y)]


ANY = pl.BlockSpec(memory_space=pl.ANY)
VMEM_SPEC = pl.BlockSpec(memory_space=pltpu.VMEM)

_GATHERED = {"w_in": "cols", "w_out": "major", "ffn_w1": "major", "ffn_w3": "major", "ffn_w2": "major"}
_BIG = ("w_in", "w_out", "ffn_w1", "ffn_w3", "ffn_w2")


def _gather_weights(shards, conv_w, n_remote):
    n = len(shards)
    full_shapes = []
    for name, sh in zip(_BIG, shards):
        r, cdim = sh.shape
        if _GATHERED[name] == "cols":
            assert cdim % LANES == 0
            full_shapes.append((r, cdim * N_CHIPS))
        else:
            full_shapes.append((N_CHIPS, r, cdim))

    def region(ref, name, sh, jj, cc):
        r, cdim = sh
        rows = pl.ds(0, r) if cc is None else pl.ds(pl.multiple_of(cc * (r // 2), 16), r // 2)
        if _GATHERED[name] == "cols":
            return ref.at[rows, pl.ds(pl.multiple_of(jj * cdim, LANES), cdim)]
        return ref.at[jj, rows, :]

    def staged(ref, sh, cc):
        r = sh[0]
        return ref.at[pl.ds(pl.multiple_of(cc * (r // 2), 16), r // 2), :]

    def body(*refs):
        ins, cw_in = refs[:n], refs[n]
        outs, cw_out = refs[n + 1:2 * n + 1], refs[2 * n + 1]
        stage = refs[2 * n + 2:3 * n + 2]
        cw_stage, lsem, ssem, rsem, fssem, frsem, cssem, crsem = refs[3 * n + 2:]
        x, y, c = _place()
        j = 2 * x + y
        chips = _other_chips(x, y)
        for w in range(n_remote):
            stage[w][...] = ins[w][...].astype(BF16)
        cw_stage[...] = jnp.zeros_like(cw_stage)
        cw_stage[0:CONV_WIDTH, :] = cw_in[...]
        shs = [s_.shape for s_ in shards]
        local = [pltpu.make_async_copy(stage[w], region(outs[w], _BIG[w], shs[w], j, None), lsem.at[w])
                 for w in range(n)]
        local.append(pltpu.make_async_copy(cw_stage, cw_out.at[j], lsem.at[n]))
        sends = []
        for k, (px, py) in enumerate(chips):
            for w in range(n_remote):
                sends.append(pltpu.make_async_remote_copy(
                    src_ref=staged(stage[w], shs[w], c), dst_ref=region(outs[w], _BIG[w], shs[w], j, c),
                    send_sem=ssem.at[k * n + w], recv_sem=rsem.at[k * n + w], device_id=(px, py, c),
                    device_id_type=MESH))
            sends.append(pltpu.make_async_remote_copy(
                src_ref=cw_stage, dst_ref=cw_out.at[j], send_sem=cssem.at[k], recv_sem=crsem.at[k],
                device_id=(px, py, c), device_id_type=MESH))
        for cp in sends:
            cp.start()
        for w in range(n_remote, n):
            stage[w][...] = ins[w][...].astype(BF16)
        for cp in local:
            cp.start()
        fwd = []
        for k, (px, py) in enumerate(chips):
            jk = 2 * px + py
            for w in range(n_remote):
                reg = region(outs[w], _BIG[w], shs[w], jk, c)
                pltpu.make_async_remote_copy(src_ref=reg, dst_ref=reg, send_sem=ssem.at[k * n + w],
                                             recv_sem=rsem.at[k * n + w], device_id=(px, py, c),
                                             device_id_type=MESH).wait_recv()
                cp = pltpu.make_async_remote_copy(src_ref=reg, dst_ref=reg, send_sem=fssem.at[k * n + w],
                                                  recv_sem=frsem.at[k * n + w], device_id=(x, y, 1 - c),
                                                  device_id_type=MESH)
                cp.start()
                fwd.append(cp)
            pltpu.make_async_remote_copy(src_ref=cw_stage, dst_ref=cw_out.at[jk], send_sem=cssem.at[k],
                                         recv_sem=crsem.at[k], device_id=(px, py, c),
                                         device_id_type=MESH).wait_recv()
        for k, (px, py) in enumerate(chips):
            jk = 2 * px + py
            for w in range(n_remote):
                reg = region(outs[w], _BIG[w], shs[w], jk, 1 - c)
                pltpu.make_async_remote_copy(src_ref=reg, dst_ref=reg, send_sem=fssem.at[k * n + w],
                                             recv_sem=frsem.at[k * n + w], device_id=(x, y, 1 - c),
                                             device_id_type=MESH).wait_recv()
        for cp in sends + fwd:
            cp.wait_send()
        for cp in local:
            cp.wait()

    nsem = 3 * n
    return pl.pallas_call(
        body, name="gather_first",
        in_specs=[VMEM_SPEC] * (n + 1), out_specs=[ANY] * (n + 1),
        out_shape=[SDS(fs, BF16) for fs in full_shapes] + [SDS((N_CHIPS, SUBLANES, LANES), F32)],
        scratch_shapes=[pltpu.VMEM(s_.shape, BF16) for s_ in shards] + [pltpu.VMEM((SUBLANES, LANES), F32)]
        + [pltpu.SemaphoreType.DMA((n + 1,))] + [pltpu.SemaphoreType.DMA((nsem,))] * 4
        + [pltpu.SemaphoreType.DMA((3,))] * 2,
        compiler_params=_cp())(*shards, conv_w)


def _start_all(make):
    def f(ins, outs, sems):
        for cp in make(ins, outs, sems):
            cp.start()
    return f


def _wait_all(make):
    def f(ins, outs, sems):
        for cp in make(ins, outs, sems):
            cp.wait()
    return f


def _ffn_gather_hosted(arrs):
    n = len(arrs)

    def make(outs, sems):
        ssem, rsem, fs, fr = sems
        x, y, c = _place()
        j = 2 * x + y

        def reg(w, jj, cc):
            hr = arrs[w].shape[1] // 2
            return outs[w].at[jj, pl.ds(pl.multiple_of(cc * hr, 16), hr), :]

        def rc(w, jj, cc, s_sem, r_sem, dev):
            return pltpu.make_async_remote_copy(src_ref=reg(w, jj, cc), dst_ref=reg(w, jj, cc), send_sem=s_sem,
                                                recv_sem=r_sem, device_id=dev, device_id_type=MESH)

        sends, recvs, fwds, frecvs = [], [], [], []
        for k, (px, py) in enumerate(_other_chips(x, y)):
            jk = 2 * px + py
            for w in range(n):
                q = k * n + w
                sends.append(rc(w, j, c, ssem.at[q], rsem.at[q], (px, py, c)))
                recvs.append(rc(w, jk, c, ssem.at[q], rsem.at[q], (px, py, c)))
                fwds.append(rc(w, jk, c, fs.at[q], fr.at[q], (x, y, 1 - c)))
                frecvs.append(rc(w, jk, 1 - c, fs.at[q], fr.at[q], (x, y, 1 - c)))
        return sends, recvs, fwds, frecvs

    def start(ins, outs, sems):
        for cp in make(outs, sems)[0]:
            cp.start()

    def mid(ins, outs, sems):
        _, recvs, fwds, _ = make(outs, sems)
        for r, f in zip(recvs, fwds):
            r.wait_recv()
            f.start()

    def finish(ins, outs, sems):
        sends, _, fwds, frecvs = make(outs, sems)
        for r in frecvs:
            r.wait_recv()
        for cp in sends + fwds:
            cp.wait_send()

    return _Hosted(arrs, [SDS(a.shape, a.dtype) for a in arrs], [3 * n] * 4, start, finish, mid=mid,
                   aliases={w: w for w in range(n)})


def _rs_sibling_hosted(arrs):
    n = len(arrs)

    def make(ins, outs, sems):
        x, y, c = _place()
        cps = []
        for w in range(n):
            hr = arrs[w].shape[1] // 2
            src = ins[w].at[:, pl.ds(pl.multiple_of((1 - c) * hr, SUBLANES), hr), :]
            cps.append(pltpu.make_async_remote_copy(src_ref=src, dst_ref=outs[w], send_sem=sems[0].at[w],
                                                    recv_sem=sems[1].at[w], device_id=(x, y, 1 - c),
                                                    device_id_type=MESH))
        return cps

    return _Hosted(arrs, [SDS((a.shape[0], a.shape[1] // 2, a.shape[2]), F32) for a in arrs], [n, n],
                   _start_all(make), _wait_all(make))


def _rs_chips_hosted(parts):
    n = len(parts)

    def make(ins, outs, sems):
        x, y, c = _place()
        j = 2 * x + y
        cps = []
        for k, (px, py) in enumerate(_other_chips(x, y)):
            jk = 2 * px + py
            for w in range(n):
                cps.append(pltpu.make_async_remote_copy(
                    src_ref=ins[w].at[jk], dst_ref=outs[w].at[j], send_sem=sems[0].at[k * n + w],
                    recv_sem=sems[1].at[k * n + w], device_id=(px, py, c), device_id_type=MESH))
        return cps

    return _Hosted(parts, [SDS(p.shape, p.dtype) for p in parts], [3 * n, 3 * n], _start_all(make), _wait_all(make))


def _rs_swap_hosted(halves):
    n = len(halves)

    def make(ins, outs, sems):
        x, y, c = _place()
        return [pltpu.make_async_remote_copy(src_ref=ins[w], dst_ref=outs[w], send_sem=sems[0].at[w],
                                             recv_sem=sems[1].at[w], device_id=(x, y, 1 - c), device_id_type=MESH)
                for w in range(n)]

    return _Hosted(halves, [SDS(h.shape, F32) for h in halves], [n, n], _start_all(make), _wait_all(make))


HBM_SPEC = pl.BlockSpec(memory_space=pltpu.HBM)
SEM_SPEC = pl.BlockSpec(memory_space=pltpu.SEMAPHORE)
_EFFECT = pltpu.SideEffectType.DATAFLOW_SIDE_EFFECTING


def _split_start(h, name):
    n_in, n_out, ns = len(h.ins), len(h.out_shapes), len(h.sems)
    ins = [pltpu.with_memory_space_constraint(a, pltpu.HBM) for a in h.ins]
    lands = [pltpu.with_memory_space_constraint(lax.empty(o.shape, o.dtype), pltpu.HBM) for o in h.out_shapes]

    def body(*refs):
        i_refs, l_refs = refs[:n_in], refs[n_in:n_in + n_out]
        s_refs = refs[n_in + n_out:n_in + n_out + ns]
        token = refs[-1]
        h.start(i_refs, l_refs, s_refs)
        token[...] = jnp.zeros_like(token)

    res = pl.pallas_call(
        body, name=name, in_specs=[HBM_SPEC] * (n_in + n_out),
        out_specs=[SEM_SPEC] * ns + [HBM_SPEC] * n_out + [VMEM_SPEC],
        out_shape=[pltpu.SemaphoreType.DMA((k,)) for k in h.sems]
        + [pltpu.HBM(o.shape, o.dtype) for o in h.out_shapes] + [SDS((SUBLANES, LANES), F32)],
        input_output_aliases={n_in + k: ns + k for k in range(n_out)},
        compiler_params=pltpu.CompilerParams(has_side_effects=_EFFECT))(*ins, *lands)
    return list(res[:ns]) + ins + list(res[ns:-1]), res[-1]


def _split_wait(h, state, after, name):
    n_in, n_out, ns = len(h.ins), len(h.out_shapes), len(h.sems)
    sems, bufs = state[:ns], state[ns:]

    def body(*refs):
        i_refs, l_refs = refs[:n_in], refs[n_in:n_in + n_out]
        s_refs = refs[n_in + n_out:n_in + n_out + ns]
        h.finish(i_refs, l_refs, s_refs)

    res = pl.pallas_call(
        body, name=name, in_specs=[HBM_SPEC] * (n_in + n_out) + [SEM_SPEC] * ns + [ANY],
        out_specs=[HBM_SPEC] * n_out,
        out_shape=[pltpu.HBM(b.shape, b.dtype) for b in bufs[n_in:]],
        input_output_aliases={n_in + k: k for k in range(n_out)},
        compiler_params=pltpu.CompilerParams(has_side_effects=_EFFECT))(*bufs, *sems, after)
    return list(res)


def _run_comm(hosted, name):
    return _call(lambda: None, hosted, None, name=name, grid=(), in_specs=[], out_specs=[], out_shape=[],
                 scratch_shapes=[], args=(), sem=None)[1]


def _row_tile(rows, cols, n_arrays):
    budget = 24 * 1024 * 1024 // (2 * 4 * n_arrays * cols)
    best = SUBLANES
    for t in range(SUBLANES, rows + 1, SUBLANES):
        if rows % t == 0 and t <= budget:
            best = t
    return best


def _place_index(which):
    x, y, c = _place()
    v = c if which == "c" else 2 * x + y
    return jnp.reshape(v, (1,)).astype(jnp.int32)


def _add_own_half(full, recv, name, wire=BF16):
    nsh, rows, cols = full.shape
    hr = rows // 2
    t = _row_tile(hr, cols, 4)
    nt = hr // t

    def body(c_ref, a_ref, b_ref, o_ref, ob_ref):
        v = a_ref[...] + b_ref[...]
        o_ref[...] = v
        ob_ref[...] = v.astype(wire)

    half = pl.BlockSpec((1, t, cols), lambda s_, i, c_ref: (s_, i, 0))
    return pl.pallas_call(
        body, name=name,
        grid_spec=pltpu.PrefetchScalarGridSpec(
            num_scalar_prefetch=1, grid=(nsh, nt),
            in_specs=[pl.BlockSpec((1, t, cols), lambda s_, i, c_ref: (s_, c_ref[0] * nt + i, 0)), half],
            out_specs=[half, half]),
        out_shape=[SDS((nsh, hr, cols), F32), SDS((nsh, hr, cols), wire)],
        compiler_params=_cp(("parallel", "parallel")))(_place_index("c"), full, recv)


def _sum_chips(own, recv, name):
    nsh, hr, cols = own.shape
    t = _row_tile(hr, cols, 6)

    def body(j_ref, own_ref, *rest):
        r_refs, o_ref = rest[:nsh], rest[nsh]
        j = j_ref[0]
        mine = own_ref[0]
        parts = [jnp.where(j == k, mine, r_refs[k][0].astype(F32)) for k in range(nsh)]
        o_ref[...] = ((parts[0] + parts[1]) + parts[2]) + parts[3]

    def other(k):
        return pl.BlockSpec((1, t, cols), lambda i, j_ref: (jnp.where(j_ref[0] == k, (k + 1) % nsh, k), i, 0))

    return pl.pallas_call(
        body, name=name,
        grid_spec=pltpu.PrefetchScalarGridSpec(
            num_scalar_prefetch=1, grid=(hr // t,),
            in_specs=[pl.BlockSpec((1, t, cols), lambda i, j_ref: (j_ref[0], i, 0))]
            + [other(k) for k in range(nsh)],
            out_specs=pl.BlockSpec((t, cols), lambda i, j_ref: (i, 0))),
        out_shape=SDS((hr, cols), F32), compiler_params=_cp(("parallel",)))(_place_index("j"), own, *([recv] * nsh))


def _adamw_math(w, g, m, v):
    m = ADAM_B1 * m + (1.0 - ADAM_B1) * g
    v = ADAM_B2 * v + (1.0 - ADAM_B2) * (g * g)
    m_hat = m / (1.0 - ADAM_B1 ** ADAM_STEP)
    v_hat = v / (1.0 - ADAM_B2 ** ADAM_STEP)
    delta = -ADAM_LR * (m_hat / (jnp.sqrt(v_hat) + ADAM_EPS) + ADAM_WD * w)
    return delta, m, v


def _adamw_big(w, g_own, g_sib, m, v, name, token=None):
    _, rows, cols = w.shape
    hr = rows // 2
    t = _row_tile(hr, cols, 9)
    nth = hr // t
    if token is None:
        token = jnp.zeros((SUBLANES, LANES), F32)

    def body(c_ref, w_ref, go_ref, gs_ref, m_ref, v_ref, tok_ref, g_ref, d_ref, mo_ref, vo_ref):
        own = (pl.program_id(0) // nth) == c_ref[0]
        g = jnp.where(own, go_ref[...], gs_ref[...]) + tok_ref[0:1, 0:1]
        g_ref[0] = g
        d_ref[0], mo_ref[0], vo_ref[0] = _adamw_math(w_ref[0], g, m_ref[0], v_ref[0])

    spec = pl.BlockSpec((1, t, cols), lambda i, c_ref: (0, i, 0))
    hspec = pl.BlockSpec((t, cols), lambda i, c_ref: (i % nth, 0))
    tspec = pl.BlockSpec((SUBLANES, LANES), lambda i, c_ref: (0, 0))
    return pl.pallas_call(
        body, name=name,
        grid_spec=pltpu.PrefetchScalarGridSpec(
            num_scalar_prefetch=1, grid=(2 * nth,), in_specs=[spec, hspec, hspec, spec, spec, tspec],
            out_specs=[spec] * 4),
        out_shape=[SDS((1, rows, cols), F32)] * 4,
        compiler_params=_cp(("parallel",)))(_place_index("c"), w, g_own, g_sib, m, v, token)


def _build_slab(mix_slab, dg_mix, dg_ffn, dg_fin, loss8):
    def body(ms_ref, gm_ref, gf_ref, gn_ref, loss_ref, out_ref):
        rows = []
        for ref in (gm_ref, gf_ref, gn_ref):
            v = jnp.sum(ref[...], axis=0, keepdims=True)
            rows += [v[:, :SLAB_W], v[:, SLAB_W:]]
        rows.append(jnp.concatenate([loss_ref[0:1, :]] * (SLAB_W // LANES), axis=1))
        rows.append(jnp.zeros((SLAB_ROWS - ROW_LOSS - 1, SLAB_W), F32))
        tail = jnp.concatenate(rows, axis=0)
        for k in range(N_CHIPS):
            out_ref[k, 0:MIX_SLAB_ROWS, :] = ms_ref[...]
            out_ref[k, MIX_SLAB_ROWS:SLAB_ROWS, :] = tail

    return pl.pallas_call(
        body, name="build_slab", in_specs=[VMEM_SPEC] * 5, out_specs=VMEM_SPEC,
        out_shape=SDS((N_CHIPS, SLAB_ROWS, SLAB_W), F32),
        compiler_params=_cp())(mix_slab, dg_mix, dg_ffn, dg_fin, loss8)


_SMALL_ROWS = (("conv_b", ROW_CONV_B), ("gate_a_b", ROW_BA), ("gate_x_b", ROW_BX), ("lru_lambda", ROW_LAM),
               ("pool_b", ROW_PB), ("pool_scale", ROW_PS), ("norm_lru_g", ROW_GL), ("norm_pool_g", ROW_GP))
_WIDE_ROWS = (("norm_mix_g", ROW_MIX), ("norm_ffn_g", ROW_FFN), ("final_norm_g", ROW_FIN))
_BLOCK_ROWS = (("gate_a_w", ROW_GA), ("gate_x_w", ROW_GX), ("pool_w", ROW_PW))
_SMALL_ORDER = tuple(n for n, _ in _SMALL_ROWS) + tuple(n for n, _ in _WIDE_ROWS) + tuple(
    n for n, _ in _BLOCK_ROWS) + ("conv_w",)


def _adamw_small(slab_own, slab_sib, wmv):
    names = _SMALL_ORDER
    flat = [a for nme in names for a in wmv[nme]]
    nin = len(flat)

    def body(*refs):
        own_ref, sib_ref, j_ref = refs[0], refs[1], refs[2]
        ins = refs[3:3 + nin]
        outs = refs[3 + nin:-1]
        first = j_ref[1] == 0
        slab_ref = jnp.concatenate([jnp.where(first, own_ref[...], sib_ref[...]),
                                    jnp.where(first, sib_ref[...], own_ref[...])], axis=0)
        refs[-1][...] = jnp.broadcast_to(slab_ref[ROW_LOSS:ROW_LOSS + 1, 0:LANES], (SUBLANES, LANES))
        grads = {}
        for nme, row in _SMALL_ROWS:
            grads[nme] = slab_ref[row:row + 1, :]
        for nme, row in _WIDE_ROWS:
            grads[nme] = jnp.concatenate([slab_ref[row:row + 1, :], slab_ref[row + 1:row + 2, :]], axis=1)
        full = slab_ref[ROW_CONV_W:ROW_CONV_W + CONV_WIDTH, :]
        jv = j_ref[0]
        g = jnp.zeros((CONV_WIDTH, LANES), F32)
        for jj in range(N_CHIPS):
            g = jnp.where(jv == jj, full[:, jj * LANES:(jj + 1) * LANES], g)
        grads["conv_w"] = g
        block_rows = dict(_BLOCK_ROWS)
        for idx, nme in enumerate(names):
            w_ref, m_ref, v_ref = ins[3 * idx:3 * idx + 3]
            if nme in block_rows:
                nblk, r, c = w_ref.shape
                parts = [(b, slab_ref[block_rows[nme]:block_rows[nme] + r, b * c:(b + 1) * c]) for b in range(nblk)]
            else:
                parts = [(Ellipsis, grads[nme])]
            for b, g in parts:
                delta, m, v = _adamw_math(w_ref[b], g, m_ref[b], v_ref[b])
                outs[4 * idx][b] = g
                outs[4 * idx + 1][b] = delta
                outs[4 * idx + 2][b] = m
                outs[4 * idx + 3][b] = v

    place = jnp.concatenate([_place_index("j"), _place_index("c")])
    out_shape = [SDS(wmv[nme][0].shape, F32) for nme in names for _ in range(4)] + [SDS((SUBLANES, LANES), F32)]
    res = pl.pallas_call(
        body, name="adamw_small",
        in_specs=[VMEM_SPEC, VMEM_SPEC, pl.BlockSpec(memory_space=pltpu.SMEM)] + [VMEM_SPEC] * nin,
        out_specs=[VMEM_SPEC] * len(out_shape), out_shape=out_shape,
        compiler_params=_cp())(slab_own, slab_sib, place, *flat)
    return {nme: tuple(res[4 * idx:4 * idx + 4]) for idx, nme in enumerate(names)}, res[-1]


_FFN = ("ffn_w1", "ffn_w3", "ffn_w2")
_TRANSPOSED = ("ffn_w1", "ffn_w3")


def _local_step(x, target, full, sp_, distributed):
    d = x.shape[1]
    (u,), got = _inproj(x, sp_["norm_mix_g"], full["w_in"],
                        [_ffn_gather_hosted([full["w_out"]])] if distributed else None)
    w_out = (got[0][0] if distributed else full["w_out"]).reshape(d, d)
    gather = [_ffn_gather_hosted([full[n] for n in _FFN])] if distributed else None
    (h, hres1, saved, pooled), got = _mixer_fwd(u, x, sp_, w_out, gather)
    w1, w3, w2 = got[0] if distributed else [full[n] for n in _FFN]
    h2, a1, a3, ff = _ffn_up(hres1, sp_["norm_ffn_g"], w1, w3)
    dh, dhb, loss8, dg_fin = _ffn_down(ff, hres1, target, sp_["final_norm_g"], w2)
    da1, da3 = _ffn_bwd_gate(dhb, a1, a3, w2)
    dws = list(_ffn_wgrad(h2, dhb, ff, da1, da3))
    rs1 = [_rs_sibling_hosted(dws)] if distributed else None
    (dhres1, dg_ffn), got = _ffn_bwd_down(da1, da3, dh, hres1, sp_["norm_ffn_g"], w1, w3, rs1)
    rs2 = None
    if distributed:
        pairs = [_add_own_half(a, r, "add_half_" + n) for n, a, r in zip(_FFN, dws, got[0])]
        rs2 = [_rs_chips_hosted([pb for _, pb in pairs])]
    (du, mix_slab, dwout), got = _mixer_bwd(u, saved, pooled, h, dhres1, sp_, w_out, rs2)
    g_mix = sp_["norm_mix_g"]
    if distributed:
        fin = [_sum_chips(pairs[k][0], got[0][k], "sum_chips_" + n) for k, n in enumerate(_FFN)]
        swap = _rs_swap_hosted(fin)
        state, token = _split_start(swap, "ffn_swap_start")
        g_mix = g_mix + token[0:1, 0:1]
    (gx, dwin, dg_mix), _ = _inproj_bwd(x, du, dhres1, g_mix, full["w_in"])
    if distributed:
        sib = _split_wait(swap, state, dg_mix, "ffn_swap_wait")
    big = {"w_in": dwin, "w_out": dwout.reshape(N_CHIPS, d // N_CHIPS, d)}
    for k, n in enumerate(_FFN):
        big[n] = (fin[k], sib[k]) if distributed else dws[k]
    return gx, big, (mix_slab, dg_mix, dg_ffn, dg_fin, loss8)


_SMALL_LAYOUT = {
    "gate_a_w": (lambda a: a[0], lambda a: a[None]),
    "gate_x_w": (lambda a: a[0], lambda a: a[None]),
    "pool_w": (lambda a: a[0], lambda a: a[None]),
    "conv_w": (lambda a: a[0], lambda a: a[None]),
    "final_norm_g": (lambda a: a[None], lambda a: a[0]),
}

_WEIGHTS = ("norm_mix_g", "w_in", "conv_w", "conv_b", "gate_a_w", "gate_a_b", "gate_x_w", "gate_x_b", "lru_lambda",
            "pool_w", "pool_b", "pool_scale", "norm_lru_g", "norm_pool_g", "w_out", "norm_ffn_g", "ffn_w1",
            "ffn_w3", "ffn_w2", "final_norm_g")


def kernel(x, norm_mix_g, w_in, conv_w, conv_b, gate_a_w, gate_a_b, gate_x_w, gate_x_b, lru_lambda, pool_w, pool_b, pool_scale, norm_lru_g, norm_pool_g, w_out, norm_ffn_g, ffn_w1, ffn_w3, ffn_w2, final_norm_g, loss_target, m_norm_mix_g, m_w_in, m_conv_w, m_conv_b, m_gate_a_w, m_gate_a_b, m_gate_x_w, m_gate_x_b, m_lru_lambda, m_pool_w, m_pool_b, m_pool_scale, m_norm_lru_g, m_norm_pool_g, m_w_out, m_norm_ffn_g, m_ffn_w1, m_ffn_w3, m_ffn_w2, m_final_norm_g, v_norm_mix_g, v_w_in, v_conv_w, v_conv_b, v_gate_a_w, v_gate_a_b, v_gate_x_w, v_gate_x_b, v_lru_lambda, v_pool_w, v_pool_b, v_pool_scale, v_norm_lru_g, v_norm_pool_g, v_w_out, v_norm_ffn_g, v_ffn_w1, v_ffn_w3, v_ffn_w2, v_final_norm_g):
    loc = locals()
    w = {n: loc[n] for n in _WEIGHTS}
    m = {n: loc["m_" + n] for n in _WEIGHTS}
    v = {n: loc["v_" + n] for n in _WEIGHTS}

    def lay(nme, a):
        return _SMALL_LAYOUT[nme][0](a) if nme in _SMALL_LAYOUT else a

    def unlay(nme, a):
        return _SMALL_LAYOUT[nme][1](a) if nme in _SMALL_LAYOUT else a

    for group in (w, m, v):
        for n in _TRANSPOSED:
            group[n] = jnp.transpose(group[n], (0, 2, 1))

    gathered = _gather_weights([w[n][0] for n in _BIG], w["conv_w"][0], n_remote=1)
    full = dict(zip(_BIG, gathered[:-1]))
    cw_all = gathered[-1]
    sp_ = {n: lay(n, w[n]) for n in _SMALL_ORDER}
    sp_["conv_w"] = jnp.transpose(cw_all[:, :CONV_WIDTH, :], (1, 0, 2)).reshape(CONV_WIDTH, N_CHIPS * LANES)

    gx, big, small = _local_step(x[0], loss_target[0], full, sp_, distributed=True)

    late = ("w_in", "w_out", "slab")
    big["slab"] = _build_slab(*small)
    fin = {n: big[n][0] for n in _FFN}
    sib = {n: big[n][1] for n in _FFN}
    recv1, = _run_comm([_rs_sibling_hosted([big[n] for n in late])], "tail_sibling")
    pairs = [_add_own_half(big[n], r, "add_half_" + n, F32 if n == "slab" else BF16) for n, r in zip(late, recv1)]
    chips = _rs_chips_hosted([pb for _, pb in pairs])
    state, token = _split_start(chips, "tail_chips_start")
    out = {}
    for n in _FFN:
        out[n] = tuple(_adamw_big(w[n], fin[n], sib[n], m[n], v[n], "adamw_" + n, token))
    recv2 = _split_wait(chips, state, out[_FFN[-1]][1], "tail_chips_wait")
    for n, (p, _), r in zip(late, pairs, recv2):
        fin[n] = _sum_chips(p, r, "sum_chips_" + n)
    swapped, = _run_comm([_rs_swap_hosted([fin[n] for n in late])], "tail_swap")
    sib.update(zip(late, swapped))
    for n in late[:2]:
        out[n] = tuple(_adamw_big(w[n], fin[n], sib[n], m[n], v[n], "adamw_" + n))
    for n in _TRANSPOSED:
        out[n] = tuple(jnp.transpose(a, (0, 2, 1)) for a in out[n])
    wmv = {n: (lay(n, w[n]), lay(n, m[n]), lay(n, v[n])) for n in _SMALL_ORDER}
    res, loss = _adamw_small(fin["slab"], sib["slab"], wmv)
    for n in _SMALL_ORDER:
        out[n] = tuple(unlay(n, a) for a in res[n])
    return (loss[0, 0], gx[None]) + tuple(out[n][k] for k in range(4) for n in _WEIGHTS)
```

```python
import functools
import math

import jax
import jax.numpy as jnp
from jax import lax
from jax.experimental import pallas as pl
from jax.experimental.pallas import tpu as pltpu

F32 = jnp.float32
BF16 = jnp.bfloat16
SDS = jax.ShapeDtypeStruct
MESH = pl.DeviceIdType.MESH

EPS = 1e-6
LRU_C = 8.0
CONV_WIDTH = 4
POOL_WINDOWS = (2, 4, 8, 16)
HALO = 16
LANES = 128
SUBLANES = 8
GATE_BLOCK = 256
N_CHIPS = 4

ADAM_LR = 0.001
ADAM_B1 = 0.9
ADAM_B2 = 0.999
ADAM_EPS = 1e-08
ADAM_WD = 0.01
ADAM_STEP = 10

TM_PROJ = 512
TM_MIX = 512
TM_FFN = 512
TM_WGRAD = 2048
MIX_SAVED = ("xc", "r", "ig", "a", "m2raw", "ge", "dge")
FFN_ROW_CHUNKS = 2
VMEM_LIMIT = 56 * 1024 * 1024

SLAB_W = 512
ROW_CONV_B, ROW_CONV_W, ROW_BA, ROW_BX, ROW_LAM, ROW_PB, ROW_PS, ROW_GL, ROW_GP = 0, 1, 5, 6, 7, 8, 9, 10, 11
ROW_GA, ROW_GX, ROW_PW = 16, 80, 144
ROW_MIX, ROW_FFN, ROW_FIN, ROW_LOSS = 272, 274, 276, 278
MIX_SLAB_ROWS = 272
SLAB_ROWS = 288


def _cp(sem=None, **kw):
    if sem is not None:
        kw["dimension_semantics"] = sem
    return pltpu.CompilerParams(vmem_limit_bytes=VMEM_LIMIT, **kw)


def _const_spec(shape):
    nd = len(shape)
    return pl.BlockSpec(shape, lambda *_: (0,) * nd, pipeline_mode=pl.Buffered(1))


def _sigmoid(x):
    return 1.0 / (1.0 + jnp.exp(-x))


def _dot(a, b):
    return jnp.dot(a, b, preferred_element_type=F32)


def _dot_nt(a, b):
    return lax.dot_general(a, b, (((1,), (1,)), ((), ())), preferred_element_type=F32)


def _dot_tn(a, b):
    return lax.dot_general(a, b, (((0,), (0,)), ((), ())), preferred_element_type=F32)


def _colsum8(v):
    m, c = v.shape
    return v.reshape(m // SUBLANES, SUBLANES, c).sum(axis=0)


def _rowmean(v):
    return jnp.mean(v, axis=-1, keepdims=True)


def _rms_bwd(dy, xhat, r, g):
    dxh = dy * g
    return r * (dxh - xhat * _rowmean(dxh * xhat))


def _softplus_neg(lam):
    z = -lam
    e = jnp.exp(-jnp.abs(z))
    u = 1.0 + e
    d = u - 1.0
    log1p = jnp.where(d == 0.0, e, jnp.log(u) * (e / jnp.where(d == 0.0, 1.0, d)))
    return jnp.maximum(z, 0.0) + log1p


def _neg_expm1(z):
    series = -(z * (1.0 + z * (0.5 + z * (1.0 / 6.0 + z * (1.0 / 24.0)))))
    return jnp.where(z > -0.03, series, 1.0 - jnp.exp(z))


_GELU_C = math.sqrt(2.0 / math.pi)
_GELU_K = 0.044715


def _gelu_parts(x):
    x2 = x * x
    th = jnp.tanh(_GELU_C * (x + _GELU_K * x2 * x))
    ge = 0.5 * x * (1.0 + th)
    dge = 0.5 * (1.0 + th) + 0.5 * x * (1.0 - th * th) * (_GELU_C * (1.0 + 3.0 * _GELU_K * x2))
    return ge, dge


def _shift_down(halo, tile, k):
    if k == 0:
        return tile
    ext = jnp.concatenate([halo, tile], axis=0)
    h = halo.shape[0]
    return pltpu.roll(ext, k, 0)[h:]


def _shift_up(tile, nxt, k):
    if k == 0:
        return tile
    ext = jnp.concatenate([tile, nxt], axis=0)
    return pltpu.roll(ext, ext.shape[0] - k, 0)[:tile.shape[0]]


def _build_gate_blocks(ga_ref, gx_ref, gw_ref):
    hd = ga_ref.shape[1]
    per = GATE_BLOCK // hd
    zero = jnp.zeros((hd, hd), F32)
    for b in range(gw_ref.shape[0]):
        for src, off in ((ga_ref, 0), (gx_ref, GATE_BLOCK)):
            for hh in range(per):
                row = jnp.concatenate([zero] * hh + [src[b * per + hh]] + [zero] * (per - 1 - hh), axis=1)
                gw_ref[b, hh * hd:(hh + 1) * hd, off:off + GATE_BLOCK] = row.astype(BF16)


def _scan_level1(a, b, reverse):
    m, c = a.shape
    a3 = a.reshape(m // SUBLANES, SUBLANES, c)
    b3 = b.reshape(m // SUBLANES, SUBLANES, c)
    row = lax.broadcasted_iota(jnp.int32, a3.shape, 1)
    for s in (1, 2, 4):
        sh = (SUBLANES - s) if reverse else s
        a_sh = pltpu.roll(a3, sh, 1)
        b_sh = pltpu.roll(b3, sh, 1)
        ok = (row < SUBLANES - s) if reverse else (row >= s)
        b3 = jnp.where(ok, a3 * b_sh + b3, b3)
        a3 = jnp.where(ok, a3 * a_sh, a3)
    return a3.reshape(m, c), b3.reshape(m, c)


def _scan_level2(a_ref, b_ref, out_ref, carry, reverse):
    m, c = a_ref.shape
    ng = m // SUBLANES

    def step(g, cr):
        gi = (ng - 1 - g) if reverse else g
        off = pl.multiple_of(gi * SUBLANES, SUBLANES)
        h = b_ref[pl.ds(off, SUBLANES), :] + a_ref[pl.ds(off, SUBLANES), :] * cr
        out_ref[pl.ds(off, SUBLANES), :] = h
        edge = h[0:1, :] if reverse else h[SUBLANES - 1:SUBLANES, :]
        return jnp.broadcast_to(edge, (SUBLANES, c))

    return lax.fori_loop(0, ng, step, carry, unroll=4)


def _mixer_recompute(u_ref, hal, t0, cw, cb, gw_ref, ba, bx, lam, pw_ref, pb, ps, saved_ref, pooled_ref):
    tm = u_ref.shape[0]
    lw = cb.shape[1]
    keep = {name: k for k, name in enumerate(MIX_SAVED)}
    hal_l, hal_p = hal[:, :lw], hal[:, 2 * lw:]
    xc = cb
    for k in range(CONV_WIDTH):
        xc = xc + _shift_down(hal_l, u_ref[:, :lw], CONV_WIDTH - 1 - k) * cw[k:k + 1, :]
    saved_ref[keep["xc"]] = xc
    xcb = xc.astype(BF16)
    nb = lw // GATE_BLOCK
    gs = [_dot(xcb[:, b * GATE_BLOCK:(b + 1) * GATE_BLOCK], gw_ref[b]) for b in range(nb)]
    r = _sigmoid(jnp.concatenate([g[:, :GATE_BLOCK] for g in gs], axis=1) + ba)
    saved_ref[keep["r"]] = r
    ig = _sigmoid(jnp.concatenate([g[:, GATE_BLOCK:] for g in gs], axis=1) + bx)
    saved_ref[keep["ig"]] = ig
    la = (-LRU_C * r) * _softplus_neg(lam)
    a = jnp.exp(la)
    saved_ref[keep["a"]] = a
    m2raw = _neg_expm1(2.0 * la)
    saved_ref[keep["m2raw"]] = m2raw
    bb = jnp.sqrt(jnp.maximum(m2raw, 1e-12)) * (ig * saved_ref[keep["xc"]])
    ge, dge = _gelu_parts(u_ref[:, lw:2 * lw])
    saved_ref[keep["ge"]] = ge
    saved_ref[keep["dge"]] = dge
    row = lax.broadcasted_iota(jnp.int32, (tm, LANES), 0) + t0
    zs = []
    for gi, w in enumerate(POOL_WINDOWS):
        sl = slice(gi * LANES, (gi + 1) * LANES)
        e = jnp.concatenate([hal_p[:, sl], u_ref[:, 2 * lw + gi * LANES:2 * lw + (gi + 1) * LANES]], axis=0)
        s = e
        k = 1
        while k < w:
            s = s + pltpu.roll(s, k, 0)
            k *= 2
        pg = (s[HALO:] * (1.0 / jnp.minimum(row + 1, w).astype(F32)) - e[HALO:]).astype(BF16)
        pooled_ref[:, sl] = pg
        zs.append(_dot(pg, pw_ref[gi].astype(BF16)))
    y_pool = (jnp.concatenate(zs, axis=1) + pb) * ps
    return a, bb, y_pool


ANY = pl.BlockSpec(memory_space=pl.ANY)
VMEM_SPEC = pl.BlockSpec(memory_space=pltpu.VMEM)


class _Hosted:
    def __init__(self, ins, out_shapes, sems, start, finish, mid=None, aliases=None):
        self.ins, self.out_shapes, self.sems = list(ins), list(out_shapes), list(sems)
        self.start, self.mid, self.finish = start, mid, finish
        self.aliases = dict(aliases or {})


def _call(body, hosted, stage_preds, *, name, grid, in_specs, out_specs, out_shape, scratch_shapes, args, sem):
    hosted = list(hosted or [])
    n_in, n_out, n_scr = len(in_specs), len(out_specs), len(scratch_shapes)
    c_in = [a for h in hosted for a in h.ins]
    c_out = [o for h in hosted for o in h.out_shapes]
    c_sem = [pltpu.SemaphoreType.DMA((k,)) for h in hosted for k in h.sems]

    def full(*refs):
        p = 0
        parts = []
        for cnt in (n_in, len(c_in), n_out, len(c_out), n_scr, len(c_sem)):
            parts.append(refs[p:p + cnt])
            p += cnt
        hi, ci, ho, co, hs, cs = parts
        per = []
        a = b = c_ = 0
        for h in hosted:
            per.append((h, ci[a:a + len(h.ins)], co[b:b + len(h.out_shapes)], cs[c_:c_ + len(h.sems)]))
            a, b, c_ = a + len(h.ins), b + len(h.out_shapes), c_ + len(h.sems)
        first = mid = last = None
        if hosted and grid:
            first, mid, last = stage_preds()

        def run(fn, pred, i_, o_, s_):
            if fn is None:
                return
            if pred is None:
                fn(i_, o_, s_)
            else:
                pl.when(pred)(functools.partial(fn, i_, o_, s_))

        for h, i_, o_, s_ in per:
            run(h.start, first, i_, o_, s_)
        body(*hi, *ho, *hs)
        for h, i_, o_, s_ in per:
            run(h.mid, mid, i_, o_, s_)
        for h, i_, o_, s_ in per:
            run(h.finish, last, i_, o_, s_)

    aliases = {}
    a = b = 0
    for h in hosted:
        for k, v in h.aliases.items():
            aliases[n_in + a + k] = n_out + b + v
        a, b = a + len(h.ins), b + len(h.out_shapes)
    res = pl.pallas_call(
        full, name=name, grid=grid, in_specs=list(in_specs) + [ANY] * len(c_in),
        out_specs=list(out_specs) + [ANY] * len(c_out), out_shape=list(out_shape) + c_out,
        scratch_shapes=list(scratch_shapes) + c_sem, input_output_aliases=aliases,
        compiler_params=_cp(sem))(*args, *c_in)
    res = list(res)
    outs = []
    p = n_out
    for h in hosted:
        outs.append(res[p:p + len(h.out_shapes)])
        p += len(h.out_shapes)
    return res[:n_out], outs


def _inproj(x, g_mix, w_in, hosted=None):
    s, d = x.shape
    n = w_in.shape[1]
    tm = min(TM_PROJ, s)
    nt = s // tm

    def body(x_ref, g_ref, w_ref, u_ref):
        xv = x_ref[...]
        r = lax.rsqrt(_rowmean(xv * xv) + EPS)
        u_ref[...] = _dot((xv * r * g_ref[...]).astype(BF16), w_ref[...])

    def stages():
        i = pl.program_id(0)
        return i == 0, i == max(nt - 3, 0), i == nt - 1

    return _call(
        body, hosted, stages, grid=(nt,), name="inproj",
        in_specs=[pl.BlockSpec((tm, d), lambda i: (i, 0)), _const_spec((1, d)), _const_spec((d, n))],
        out_specs=[pl.BlockSpec((tm, n), lambda i: (i, 0))], out_shape=[SDS((s, n), F32)], scratch_shapes=[],
        args=(x, g_mix, w_in), sem=("arbitrary",))


def _mixer_fwd(u, x, sp_, w_out, hosted=None):
    s, din = u.shape
    d = x.shape[1]
    lw = din // 3
    tm = min(TM_MIX, s)
    nb = lw // GATE_BLOCK

    def body(u_ref, halo_ref, x_ref, cw_ref, cb_ref, ga_ref, gx_ref, ba_ref, bx_ref, lam_ref, pw_ref, pb_ref,
             ps_ref, gl_ref, gp_ref, wout_ref, h_ref, hres_ref, saved_ref, pooled_ref,
             gw_s, a_s, b_s, carry_s):
        i = pl.program_id(0)

        @pl.when(i == 0)
        def _():
            _build_gate_blocks(ga_ref, gx_ref, gw_s)
            carry_s[...] = jnp.zeros_like(carry_s)

        hal = jnp.where(i > 0, halo_ref[...], 0.0)
        a, bb, yp = _mixer_recompute(u_ref, hal, i * tm, cw_ref[...], cb_ref[...], gw_s, ba_ref[...], bx_ref[...],
                                     lam_ref[...], pw_ref, pb_ref[...], ps_ref[...], saved_ref, pooled_ref)
        a1, b1 = _scan_level1(a, bb, reverse=False)
        a_s[...] = a1
        b_s[...] = b1
        carry_s[...] = _scan_level2(a_s, b_s, h_ref, carry_s[...], reverse=False)
        y_lru = h_ref[...] * saved_ref[MIX_SAVED.index("ge")]
        rl = lax.rsqrt(_rowmean(y_lru * y_lru) + EPS)
        rp = lax.rsqrt(_rowmean(yp * yp) + EPS)
        yn = jnp.concatenate([y_lru * rl * gl_ref[...], yp * rp * gp_ref[...]], axis=1).astype(BF16)
        hres_ref[...] = x_ref[...] + _dot(yn, wout_ref[...])

    small = [sp_[k] for k in ("conv_w", "conv_b", "gate_a_w", "gate_x_w", "gate_a_b", "gate_x_b", "lru_lambda",
                              "pool_w", "pool_b", "pool_scale", "norm_lru_g", "norm_pool_g")]
    nt = s // tm

    def stages():
        i = pl.program_id(0)
        return i == 0, i == max(nt - 3, 0), i == nt - 1

    return _call(
        body, hosted, stages, grid=(nt,), name="mixer_fwd",
        in_specs=[pl.BlockSpec((tm, din), lambda i: (i, 0)),
                  pl.BlockSpec((HALO, din), lambda i: (jnp.maximum(i * (tm // HALO) - 1, 0), 0)),
                  pl.BlockSpec((tm, d), lambda i: (i, 0))]
        + [_const_spec(a.shape) for a in small] + [_const_spec(w_out.shape)],
        out_specs=[pl.BlockSpec((tm, lw), lambda i: (i, 0)), pl.BlockSpec((tm, d), lambda i: (i, 0)),
                   pl.BlockSpec((len(MIX_SAVED), tm, lw), lambda i: (0, i, 0)),
                   pl.BlockSpec((tm, lw), lambda i: (i, 0))],
        out_shape=[SDS((s, lw), F32), SDS((s, d), F32), SDS((len(MIX_SAVED), s, lw), F32),
                   SDS((s, lw), BF16)],
        scratch_shapes=[pltpu.VMEM((nb, GATE_BLOCK, 2 * GATE_BLOCK), BF16), pltpu.VMEM((tm, lw), F32),
                        pltpu.VMEM((tm, lw), F32), pltpu.VMEM((SUBLANES, lw), F32)],
        args=(u, u, x, *small, w_out), sem=("arbitrary",))


def _row_chunks(tm):
    rc = tm // FFN_ROW_CHUNKS
    return [slice(q * rc, (q + 1) * rc) for q in range(FFN_ROW_CHUNKS)]


def _ffn_up(hres1, g_ffn, w1, w3):
    s, d = hres1.shape
    nj, fc, _ = w1.shape
    tm = min(TM_FFN, s)

    def body(h_ref, gf_ref, w1_ref, w3_ref, h2_ref, a1_ref, a3_ref, ff_ref):
        hv = h_ref[...]
        r = lax.rsqrt(_rowmean(hv * hv) + EPS)
        h2_ref[...] = (hv * r * gf_ref[...]).astype(BF16)
        h2 = h2_ref[...]
        for j in range(nj):
            a1 = _dot_nt(h2, w1_ref[j])
            a3 = _dot_nt(h2, w3_ref[j])
            a1_ref[j] = a1.astype(BF16)
            a3_ref[j] = a3.astype(BF16)
            ff_ref[j] = ((a1 * _sigmoid(a1)) * a3).astype(BF16)

    wspec = _const_spec(w1.shape)
    aspec = pl.BlockSpec((nj, tm, fc), lambda i: (0, i, 0))
    return pl.pallas_call(
        body, grid=(s // tm,), name="ffn_up",
        in_specs=[pl.BlockSpec((tm, d), lambda i: (i, 0)), _const_spec((1, d)), wspec, wspec],
        out_specs=[pl.BlockSpec((tm, d), lambda i: (i, 0)), aspec, aspec, aspec],
        out_shape=[SDS((s, d), BF16)] + [SDS((nj, s, fc), BF16)] * 3,
        compiler_params=_cp(("parallel",)))(hres1, g_ffn, w1, w3)


def _ffn_down(ff, hres1, target, g_fin, w2):
    s, d = hres1.shape
    nj, _, fc = ff.shape
    tm = min(TM_FFN, s)

    def body(ff_ref, h_ref, t_ref, gn_ref, w2_ref, dh_ref, dhb_ref, loss_ref, dgn_ref):
        @pl.when(pl.program_id(0) == 0)
        def _():
            loss_ref[...] = jnp.zeros_like(loss_ref)
            dgn_ref[...] = jnp.zeros_like(dgn_ref)

        gn = gn_ref[...]
        for rows in _row_chunks(tm):
            acc = _dot(ff_ref[0, rows, :], w2_ref[0])
            for j in range(1, nj):
                acc = acc + _dot(ff_ref[j, rows, :], w2_ref[j])
            hr2 = h_ref[rows, :] + acc
            r2 = lax.rsqrt(_rowmean(hr2 * hr2) + EPS)
            xh = hr2 * r2
            diff = xh * gn - t_ref[rows, :]
            tot = jnp.sum(jnp.sum(diff * diff, axis=1, keepdims=True), axis=0, keepdims=True)
            loss_ref[...] += tot * (0.5 / d)
            dout = diff * (1.0 / d)
            dgn_ref[...] += _colsum8(dout * xh)
            dh = _rms_bwd(dout, xh, r2, gn)
            dh_ref[rows, :] = dh
            dhb_ref[rows, :] = dh.astype(BF16)

    tile = pl.BlockSpec((tm, d), lambda i: (i, 0))
    return pl.pallas_call(
        body, grid=(s // tm,), name="ffn_down",
        in_specs=[pl.BlockSpec((nj, tm, fc), lambda i: (0, i, 0)), tile, tile, _const_spec((1, d)),
                  _const_spec(w2.shape)],
        out_specs=[tile, tile, pl.BlockSpec((SUBLANES, LANES), lambda i: (0, 0)),
                   pl.BlockSpec((SUBLANES, d), lambda i: (0, 0))],
        out_shape=[SDS((s, d), F32), SDS((s, d), BF16), SDS((SUBLANES, LANES), F32), SDS((SUBLANES, d), F32)],
        compiler_params=_cp(("arbitrary",)))(ff, hres1, target, g_fin, w2)


def _ffn_bwd_gate(dhb, a1, a3, w2):
    s, d = dhb.shape
    nj, _, fc = a1.shape
    tm = min(TM_FFN, s)

    def body(dhb_ref, a1_ref, a3_ref, w2_ref, da1_ref, da3_ref):
        for j in range(nj):
            for rows in _row_chunks(tm):
                dff = _dot_nt(dhb_ref[rows, :], w2_ref[j])
                a1v = a1_ref[j, rows, :].astype(F32)
                sg = _sigmoid(a1v)
                silu = a1v * sg
                da1_ref[j, rows, :] = (dff * a3_ref[j, rows, :].astype(F32)
                                       * (sg * (1.0 + (a1v - silu)))).astype(BF16)
                da3_ref[j, rows, :] = (dff * silu).astype(BF16)

    aspec = pl.BlockSpec((nj, tm, fc), lambda i: (0, i, 0))
    return pl.pallas_call(
        body, grid=(s // tm,), name="ffn_bwd_gate",
        in_specs=[pl.BlockSpec((tm, d), lambda i: (i, 0)), aspec, aspec, _const_spec(w2.shape)],
        out_specs=[aspec, aspec], out_shape=[SDS((nj, s, fc), BF16)] * 2,
        compiler_params=_cp(("parallel",)))(dhb, a1, a3, w2)


def _ffn_bwd_down(da1, da3, dh, hres1, g_ffn, w1, w3, hosted=None):
    s, d = hres1.shape
    nj, _, fc = da1.shape
    tm = min(TM_FFN, s)
    nt = s // tm

    def body(da1_ref, da3_ref, dh_ref, h_ref, gf_ref, w1_ref, w3_ref, dhr_ref, dgf_ref):
        @pl.when(pl.program_id(0) == 0)
        def _():
            dgf_ref[...] = jnp.zeros_like(dgf_ref)

        gf = gf_ref[...]
        for rows in _row_chunks(tm):
            dh2 = None
            for j in range(nj):
                part = _dot(da1_ref[j, rows, :], w1_ref[j]) + _dot(da3_ref[j, rows, :], w3_ref[j])
                dh2 = part if dh2 is None else dh2 + part
            hv = h_ref[rows, :]
            r = lax.rsqrt(_rowmean(hv * hv) + EPS)
            xh = hv * r
            dgf_ref[...] += _colsum8(dh2 * xh)
            dhr_ref[rows, :] = dh_ref[rows, :] + _rms_bwd(dh2, xh, r, gf)

    tile = pl.BlockSpec((tm, d), lambda i: (i, 0))
    aspec = pl.BlockSpec((nj, tm, fc), lambda i: (0, i, 0))
    wspec = _const_spec(w1.shape)

    def stages():
        i = pl.program_id(0)
        return i == 0, i == max(nt - 2, 0), i == nt - 1

    return _call(
        body, hosted, stages, grid=(nt,), name="ffn_bwd_down",
        in_specs=[aspec, aspec, tile, tile, _const_spec((1, d)), wspec, wspec],
        out_specs=[tile, pl.BlockSpec((SUBLANES, d), lambda i: (0, 0))],
        out_shape=[SDS((s, d), F32), SDS((SUBLANES, d), F32)],
        scratch_shapes=[], args=(da1, da3, dh, hres1, g_ffn, w1, w3), sem=("arbitrary",))


def _ffn_wgrad(h2, dhb, ff, da1, da3):
    s, d = h2.shape
    _, _, fc = ff.shape
    tm = min(TM_WGRAD, s)

    def body(h2_ref, dhb_ref, ff_ref, da1_ref, da3_ref, dw1_ref, dw3_ref, dw2_ref):
        @pl.when(pl.program_id(1) == 0)
        def _():
            dw1_ref[...] = jnp.zeros_like(dw1_ref)
            dw3_ref[...] = jnp.zeros_like(dw3_ref)
            dw2_ref[...] = jnp.zeros_like(dw2_ref)

        h2v = h2_ref[...]
        dw1_ref[0] += _dot_tn(da1_ref[0], h2v)
        dw3_ref[0] += _dot_tn(da3_ref[0], h2v)
        dw2_ref[0] += _dot_tn(ff_ref[0], dhb_ref[...])

    wspec = pl.BlockSpec((1, fc, d), lambda j, i: (j, 0, 0))
    return pl.pallas_call(
        body, grid=(N_CHIPS, s // tm), name="ffn_wgrad",
        in_specs=[pl.BlockSpec((tm, d), lambda j, i: (i, 0)), pl.BlockSpec((tm, d), lambda j, i: (i, 0))]
        + [pl.BlockSpec((1, tm, fc), lambda j, i: (j, i, 0))] * 3,
        out_specs=[wspec] * 3, out_shape=[SDS((N_CHIPS, fc, d), F32)] * 3,
        compiler_params=_cp(("parallel", "arbitrary")))(h2, dhb, ff, da1, da3)


def _mixer_bwd(u, saved, pooled, h, dhres1, sp_, w_out, hosted=None):
    s, din = u.shape
    d = dhres1.shape[1]
    lw = din // 3
    tm = min(TM_MIX, s)
    nt = s // tm
    nb = lw // GATE_BLOCK
    hd = sp_["gate_a_w"].shape[1]

    def body(ul_ref, saved_ref, pooled_ref, h_ref, hhalo_ref, dhr_ref, cw_ref, cb_ref, ga_ref, gx_ref, ba_ref,
             bx_ref, lam_ref, pw_ref, pb_ref, ps_ref, gl_ref, gp_ref, wout_ref, du_ref, slab_ref, dwout_ref,
             gw_s, a_s, b_s, e_s, ecarry_s, dxc_s, q_s, vec_s, cwacc_s, dgw_s, dpw_s):
        i = pl.program_id(0)
        tile = nt - 1 - i

        @pl.when(i == 0)
        def _():
            _build_gate_blocks(ga_ref, gx_ref, gw_s)
            for ref in (ecarry_s, dxc_s, q_s, vec_s, cwacc_s, dgw_s, dpw_s, dwout_ref):
                ref[...] = jnp.zeros_like(ref)

        cw = cw_ref[...]
        lam = lam_ref[...]
        ps = ps_ref[...]
        def sv_(name):
            return saved_ref[MIX_SAVED.index(name)]

        f = {"sp": _softplus_neg(lam)}
        row = lax.broadcasted_iota(jnp.int32, (tm, LANES), 0) + tile * tm
        f["z"] = jnp.concatenate(
            [_dot(pooled_ref[:, g * LANES:(g + 1) * LANES], pw_ref[g].astype(BF16))
             for g in range(len(POOL_WINDOWS))], axis=1) + pb_ref[...]
        f["y_pool"] = f["z"] * ps
        y_lru = h_ref[...] * sv_("ge")
        rl = lax.rsqrt(_rowmean(y_lru * y_lru) + EPS)
        yp = f["y_pool"]
        rp = lax.rsqrt(_rowmean(yp * yp) + EPS)
        xh_l = y_lru * rl
        xh_p = yp * rp

        dhrb = dhr_ref[...].astype(BF16)
        dyn = _dot_nt(dhrb, wout_ref[...])
        yn = jnp.concatenate([xh_l * gl_ref[...], xh_p * gp_ref[...]], axis=1).astype(BF16)
        dwout_ref[...] += _dot_tn(yn, dhrb)
        d_nl, d_np = dyn[:, :lw], dyn[:, lw:]
        vec = {}
        vec[ROW_GL] = _colsum8(d_nl * xh_l)
        vec[ROW_GP] = _colsum8(d_np * xh_p)
        d_ylru = _rms_bwd(d_nl, xh_l, rl, gl_ref[...])
        d_ypool = _rms_bwd(d_np, xh_p, rp, gp_ref[...])

        vec[ROW_PS] = _colsum8(d_ypool * f["z"])
        dz = d_ypool * ps
        vec[ROW_PB] = _colsum8(dz)
        dzb = dz.astype(BF16)
        dup = []
        for gi, w in enumerate(POOL_WINDOWS):
            sl = slice(gi * LANES, (gi + 1) * LANES)
            dpw_s[:, sl] += _dot_tn(pooled_ref[:, sl], dzb[:, sl])
            dpool = _dot_nt(dzb[:, sl], pw_ref[gi].astype(BF16))
            q = dpool * (1.0 / jnp.minimum(row + 1, w).astype(F32))
            e = jnp.concatenate([q, q_s[:, sl]], axis=0)
            k = 1
            while k < w:
                e = e + pltpu.roll(e, tm + HALO - k, 0)
                k *= 2
            dup.append(e[:tm] - dpool)
            q_s[:, sl] = q[:HALO]

        d_hout = d_ylru * sv_("ge")
        d_ug = d_ylru * h_ref[...] * sv_("dge")
        a1, b1 = _scan_level1(sv_("a"), sv_("a") * d_hout, reverse=True)
        a_s[...] = a1
        b_s[...] = b1
        e_next = ecarry_s[...]
        ecarry_s[...] = _scan_level2(a_s, b_s, e_s, e_next, reverse=True)
        sv = d_hout + _shift_up(e_s[...], e_next, 1)
        d_a = sv * _shift_down(jnp.where(tile > 0, hhalo_ref[...], 0.0), h_ref[...], 1)
        mult = jnp.sqrt(jnp.maximum(sv_("m2raw"), 1e-12))
        d_mult = sv * (sv_("ig") * sv_("xc"))
        d_ig = sv * mult * sv_("xc")
        d_xc = sv * mult * sv_("ig")
        a = sv_("a")
        d_la = d_a * a + jnp.where(sv_("m2raw") > 1e-12, d_mult * (-(a * a) / mult), 0.0)
        d_r = d_la * (-LRU_C * f["sp"])
        vec[ROW_LAM] = _colsum8(d_la * (-LRU_C * sv_("r")))
        d_pr = d_r * sv_("r") * (1.0 - sv_("r"))
        d_pi = d_ig * sv_("ig") * (1.0 - sv_("ig"))
        vec[ROW_BA] = _colsum8(d_pr)
        vec[ROW_BX] = _colsum8(d_pi)
        dxc_parts = []
        for b in range(nb):
            sl = slice(b * GATE_BLOCK, (b + 1) * GATE_BLOCK)
            rhs = jnp.concatenate([d_pr[:, sl], d_pi[:, sl]], axis=1).astype(BF16)
            dgw_s[b] += _dot_tn(saved_ref[MIX_SAVED.index("xc"), :, sl].astype(BF16), rhs)
            dxc_parts.append(_dot_nt(rhs, gw_s[b]))
        d_xc = d_xc + jnp.concatenate(dxc_parts, axis=1)
        vec[ROW_CONV_B] = _colsum8(d_xc)
        dxc_next = dxc_s[...]
        d_ul = None
        for k in range(CONV_WIDTH):
            ahead = _shift_up(d_xc, dxc_next, CONV_WIDTH - 1 - k)
            cwacc_s[k * SUBLANES:(k + 1) * SUBLANES, :] += _colsum8(ahead * ul_ref[...])
            term = ahead * cw[k:k + 1, :]
            d_ul = term if d_ul is None else d_ul + term
        dxc_s[...] = d_xc[:SUBLANES]
        for row, val in vec.items():
            vec_s[row * SUBLANES:(row + 1) * SUBLANES, :] += val
        du_ref[...] = jnp.concatenate([d_ul, d_ug] + dup, axis=1).astype(BF16)

        @pl.when(i == nt - 1)
        def _():
            rows = []
            for row in range(ROW_GA):
                if row in (ROW_CONV_W, ROW_CONV_W + 1, ROW_CONV_W + 2, ROW_CONV_W + 3):
                    k = row - ROW_CONV_W
                    v = jnp.sum(cwacc_s[k * SUBLANES:(k + 1) * SUBLANES, :], axis=0, keepdims=True)
                elif row <= ROW_GP:
                    v = jnp.sum(vec_s[row * SUBLANES:(row + 1) * SUBLANES, :], axis=0, keepdims=True)
                    if row == ROW_LAM:
                        v = v * (-1.0 / (1.0 + jnp.exp(lam)))
                else:
                    v = jnp.zeros((1, lw), F32)
                rows.append(v)
            slab_ref[0:ROW_GA, :] = jnp.concatenate(rows, axis=0)
            lane = lax.broadcasted_iota(jnp.int32, (hd, GATE_BLOCK), 1)
            for b in range(nb):
                for off, row0 in ((0, ROW_GA), (GATE_BLOCK, ROW_GX)):
                    acc = jnp.zeros((hd, GATE_BLOCK), F32)
                    for hh in range(GATE_BLOCK // hd):
                        m = (lane >= hh * hd) & (lane < (hh + 1) * hd)
                        acc = acc + jnp.where(m, dgw_s[b, hh * hd:(hh + 1) * hd, off:off + GATE_BLOCK], 0.0)
                    slab_ref[row0:row0 + hd, b * GATE_BLOCK:(b + 1) * GATE_BLOCK] = acc
            slab_ref[ROW_PW:ROW_PW + LANES, :] = dpw_s[...]

    small = [sp_[k] for k in ("conv_w", "conv_b", "gate_a_w", "gate_x_w", "gate_a_b", "gate_x_b", "lru_lambda",
                              "pool_w", "pool_b", "pool_scale", "norm_lru_g", "norm_pool_g")]
    rev = lambda i: nt - 1 - i

    def stages():
        i = pl.program_id(0)
        return i == 0, i == max(nt - 3, 0), i == nt - 1

    return _call(
        body, hosted, stages, grid=(nt,), name="mixer_bwd",
        in_specs=[pl.BlockSpec((tm, lw), lambda i: (rev(i), 0)),
                  pl.BlockSpec((len(MIX_SAVED), tm, lw), lambda i: (0, rev(i), 0)),
                  pl.BlockSpec((tm, lw), lambda i: (rev(i), 0)),
                  pl.BlockSpec((tm, lw), lambda i: (rev(i), 0)),
                  pl.BlockSpec((SUBLANES, lw), lambda i: (jnp.maximum(rev(i) * (tm // SUBLANES) - 1, 0), 0)),
                  pl.BlockSpec((tm, d), lambda i: (rev(i), 0))]
        + [_const_spec(a.shape) for a in small] + [_const_spec(w_out.shape)],
        out_specs=[pl.BlockSpec((tm, din), lambda i: (rev(i), 0)),
                   pl.BlockSpec((MIX_SLAB_ROWS, SLAB_W), lambda i: (0, 0)), pl.BlockSpec((d, d), lambda i: (0, 0))],
        out_shape=[SDS((s, din), BF16), SDS((MIX_SLAB_ROWS, SLAB_W), F32), SDS((d, d), F32)],
        scratch_shapes=[pltpu.VMEM((nb, GATE_BLOCK, 2 * GATE_BLOCK), BF16),
                        pltpu.VMEM((tm, lw), F32), pltpu.VMEM((tm, lw), F32), pltpu.VMEM((tm, lw), F32),
                        pltpu.VMEM((SUBLANES, lw), F32), pltpu.VMEM((SUBLANES, lw), F32),
                        pltpu.VMEM((HALO, lw), F32), pltpu.VMEM((ROW_GA * SUBLANES, lw), F32),
                        pltpu.VMEM((CONV_WIDTH * SUBLANES, lw), F32),
                        pltpu.VMEM((nb, GATE_BLOCK, 2 * GATE_BLOCK), F32), pltpu.VMEM((LANES, lw), F32)],
        args=(u, saved, pooled, h, h, dhres1, *small, w_out), sem=("arbitrary",))


def _inproj_bwd(x, du, dhres1, g_mix, w_in, hosted=None):
    s, d = x.shape
    n = w_in.shape[1]
    nc = n // N_CHIPS
    tm = min(TM_PROJ, s)
    nt = s // tm

    def body(x_ref, du_ref, dhr_ref, g_ref, w_ref, gx_ref, dwin_ref, dg_ref):
        i = pl.program_id(0)

        @pl.when(i == 0)
        def _():
            dwin_ref[...] = jnp.zeros_like(dwin_ref)
            dg_ref[...] = jnp.zeros_like(dg_ref)

        xv = x_ref[...]
        g = g_ref[...]
        r = lax.rsqrt(_rowmean(xv * xv) + EPS)
        xh = xv * r
        h1 = (xh * g).astype(BF16)
        duv = du_ref[...]
        dh1 = _dot_nt(duv, w_ref[...])
        dg_ref[...] += _colsum8(dh1 * xh)
        gx_ref[...] = dhr_ref[...] + _rms_bwd(dh1, xh, r, g)
        for jj in range(N_CHIPS):
            dwin_ref[jj] += _dot_tn(h1, duv[:, jj * nc:(jj + 1) * nc])

    def stages():
        i = pl.program_id(0)
        return i == 0, i == max(nt - 3, 0), i == nt - 1

    return _call(
        body, hosted, stages, grid=(nt,), name="inproj_bwd",
        in_specs=[pl.BlockSpec((tm, d), lambda i: (i, 0)), pl.BlockSpec((tm, n), lambda i: (i, 0)),
                  pl.BlockSpec((tm, d), lambda i: (i, 0)), _const_spec((1, d)), _const_spec((d, n))],
        out_specs=[pl.BlockSpec((tm, d), lambda i: (i, 0)), pl.BlockSpec((N_CHIPS, d, nc), lambda i: (0, 0, 0)),
                   pl.BlockSpec((SUBLANES, d), lambda i: (0, 0))],
        out_shape=[SDS((s, d), F32), SDS((N_CHIPS, d, nc), F32), SDS((SUBLANES, d), F32)],
        scratch_shapes=[], args=(x, du, dhres1, g_mix, w_in), sem=("arbitrary",))


def _place():
    x, y, c = lax.axis_index("x"), lax.axis_index("y"), lax.axis_index("c")
    return x, y, c


def _other_chips(x, y):
    return [(1 - x, y), (x, 1 - y), (1 - x, 1 - y)]


ANY = pl.BlockSpec(memory_space=pl.ANY)
VMEM_SPEC = pl.BlockSpec(memory_space=pltpu.VMEM)

_GATHERED = {"w_in": "cols", "w_out": "major", "ffn_w1": "major", "ffn_w3": "major", "ffn_w2": "major"}
_BIG = ("w_in", "w_out", "ffn_w1", "ffn_w3", "ffn_w2")


def _gather_weights(shards, conv_w, n_remote):
    n = len(shards)
    full_shapes = []
    for name, sh in zip(_BIG, shards):
        r, cdim = sh.shape
        if _GATHERED[name] == "cols":
            assert cdim % LANES == 0
            full_shapes.append((r, cdim * N_CHIPS))
        else:
            full_shapes.append((N_CHIPS, r, cdim))

    def region(ref, name, sh, jj, cc):
        r, cdim = sh
        rows = pl.ds(0, r) if cc is None else pl.ds(pl.multiple_of(cc * (r // 2), 16), r // 2)
        if _GATHERED[name] == "cols":
            return ref.at[rows, pl.ds(pl.multiple_of(jj * cdim, LANES), cdim)]
        return ref.at[jj, rows, :]

    def staged(ref, sh, cc):
        r = sh[0]
        return ref.at[pl.ds(pl.multiple_of(cc * (r // 2), 16), r // 2), :]

    def body(*refs):
        ins, cw_in = refs[:n], refs[n]
        outs, cw_out = refs[n + 1:2 * n + 1], refs[2 * n + 1]
        stage = refs[2 * n + 2:3 * n + 2]
        cw_stage, lsem, ssem, rsem, fssem, frsem, cssem, crsem = refs[3 * n + 2:]
        x, y, c = _place()
        j = 2 * x + y
        chips = _other_chips(x, y)
        for w in range(n_remote):
            stage[w][...] = ins[w][...].astype(BF16)
        cw_stage[...] = jnp.zeros_like(cw_stage)
        cw_stage[0:CONV_WIDTH, :] = cw_in[...]
        shs = [s_.shape for s_ in shards]
        local = [pltpu.make_async_copy(stage[w], region(outs[w], _BIG[w], shs[w], j, None), lsem.at[w])
                 for w in range(n)]
        local.append(pltpu.make_async_copy(cw_stage, cw_out.at[j], lsem.at[n]))
        sends = []
        for k, (px, py) in enumerate(chips):
            for w in range(n_remote):
                sends.append(pltpu.make_async_remote_copy(
                    src_ref=staged(stage[w], shs[w], c), dst_ref=region(outs[w], _BIG[w], shs[w], j, c),
                    send_sem=ssem.at[k * n + w], recv_sem=rsem.at[k * n + w], device_id=(px, py, c),
                    device_id_type=MESH))
            sends.append(pltpu.make_async_remote_copy(
                src_ref=cw_stage, dst_ref=cw_out.at[j], send_sem=cssem.at[k], recv_sem=crsem.at[k],
                device_id=(px, py, c), device_id_type=MESH))
        for cp in sends:
            cp.start()
        for w in range(n_remote, n):
            stage[w][...] = ins[w][...].astype(BF16)
        for cp in local:
            cp.start()
        fwd = []
        for k, (px, py) in enumerate(chips):
            jk = 2 * px + py
            for w in range(n_remote):
                reg = region(outs[w], _BIG[w], shs[w], jk, c)
                pltpu.make_async_remote_copy(src_ref=reg, dst_ref=reg, send_sem=ssem.at[k * n + w],
                                             recv_sem=rsem.at[k * n + w], device_id=(px, py, c),
                                             device_id_type=MESH).wait_recv()
                cp = pltpu.make_async_remote_copy(src_ref=reg, dst_ref=reg, send_sem=fssem.at[k * n + w],
                                                  recv_sem=frsem.at[k * n + w], device_id=(x, y, 1 - c),
                                                  device_id_type=MESH)
                cp.start()
                fwd.append(cp)
            pltpu.make_async_remote_copy(src_ref=cw_stage, dst_ref=cw_out.at[jk], send_sem=cssem.at[k],
                                         recv_sem=crsem.at[k], device_id=(px, py, c),
                                         device_id_type=MESH).wait_recv()
        for k, (px, py) in enumerate(chips):
            jk = 2 * px + py
            for w in range(n_remote):
                reg = region(outs[w], _BIG[w], shs[w], jk, 1 - c)
                pltpu.make_async_remote_copy(src_ref=reg, dst_ref=reg, send_sem=fssem.at[k * n + w],
                                             recv_sem=frsem.at[k * n + w], device_id=(x, y, 1 - c),
                                             device_id_type=MESH).wait_recv()
        for cp in sends + fwd:
            cp.wait_send()
        for cp in local:
            cp.wait()

    nsem = 3 * n
    return pl.pallas_call(
        body, name="gather_first",
        in_specs=[VMEM_SPEC] * (n + 1), out_specs=[ANY] * (n + 1),
        out_shape=[SDS(fs, BF16) for fs in full_shapes] + [SDS((N_CHIPS, SUBLANES, LANES), F32)],
        scratch_shapes=[pltpu.VMEM(s_.shape, BF16) for s_ in shards] + [pltpu.VMEM((SUBLANES, LANES), F32)]
        + [pltpu.SemaphoreType.DMA((n + 1,))] + [pltpu.SemaphoreType.DMA((nsem,))] * 4
        + [pltpu.SemaphoreType.DMA((3,))] * 2,
        compiler_params=_cp())(*shards, conv_w)


def _start_all(make):
    def f(ins, outs, sems):
        for cp in make(ins, outs, sems):
            cp.start()
    return f


def _wait_all(make):
    def f(ins, outs, sems):
        for cp in make(ins, outs, sems):
            cp.wait()
    return f


def _ffn_gather_hosted(arrs):
    n = len(arrs)

    def make(outs, sems):
        ssem, rsem, fs, fr = sems
        x, y, c = _place()
        j = 2 * x + y

        def reg(w, jj, cc):
            hr = arrs[w].shape[1] // 2
            return outs[w].at[jj, pl.ds(pl.multiple_of(cc * hr, 16), hr), :]

        def rc(w, jj, cc, s_sem, r_sem, dev):
            return pltpu.make_async_remote_copy(src_ref=reg(w, jj, cc), dst_ref=reg(w, jj, cc), send_sem=s_sem,
                                                recv_sem=r_sem, device_id=dev, device_id_type=MESH)

        sends, recvs, fwds, frecvs = [], [], [], []
        for k, (px, py) in enumerate(_other_chips(x, y)):
            jk = 2 * px + py
            for w in range(n):
                q = k * n + w
                sends.append(rc(w, j, c, ssem.at[q], rsem.at[q], (px, py, c)))
                recvs.append(rc(w, jk, c, ssem.at[q], rsem.at[q], (px, py, c)))
                fwds.append(rc(w, jk, c, fs.at[q], fr.at[q], (x, y, 1 - c)))
                frecvs.append(rc(w, jk, 1 - c, fs.at[q], fr.at[q], (x, y, 1 - c)))
        return sends, recvs, fwds, frecvs

    def start(ins, outs, sems):
        for cp in make(outs, sems)[0]:
            cp.start()

    def mid(ins, outs, sems):
        _, recvs, fwds, _ = make(outs, sems)
        for r, f in zip(recvs, fwds):
            r.wait_recv()
            f.start()

    def finish(ins, outs, sems):
        sends, _, fwds, frecvs = make(outs, sems)
        for r in frecvs:
            r.wait_recv()
        for cp in sends + fwds:
            cp.wait_send()

    return _Hosted(arrs, [SDS(a.shape, a.dtype) for a in arrs], [3 * n] * 4, start, finish, mid=mid,
                   aliases={w: w for w in range(n)})


def _rs_sibling_hosted(arrs):
    n = len(arrs)

    def make(ins, outs, sems):
        x, y, c = _place()
        cps = []
        for w in range(n):
            hr = arrs[w].shape[1] // 2
            src = ins[w].at[:, pl.ds(pl.multiple_of((1 - c) * hr, SUBLANES), hr), :]
            cps.append(pltpu.make_async_remote_copy(src_ref=src, dst_ref=outs[w], send_sem=sems[0].at[w],
                                                    recv_sem=sems[1].at[w], device_id=(x, y, 1 - c),
                                                    device_id_type=MESH))
        return cps

    return _Hosted(arrs, [SDS((a.shape[0], a.shape[1] // 2, a.shape[2]), F32) for a in arrs], [n, n],
                   _start_all(make), _wait_all(make))


def _rs_chips_hosted(parts):
    n = len(parts)

    def make(ins, outs, sems):
        x, y, c = _place()
        j = 2 * x + y
        cps = []
        for k, (px, py) in enumerate(_other_chips(x, y)):
            jk = 2 * px + py
            for w in range(n):
                cps.append(pltpu.make_async_remote_copy(
                    src_ref=ins[w].at[jk], dst_ref=outs[w].at[j], send_sem=sems[0].at[k * n + w],
                    recv_sem=sems[1].at[k * n + w], device_id=(px, py, c), device_id_type=MESH))
        return cps

    return _Hosted(parts, [SDS(p.shape, p.dtype) for p in parts], [3 * n, 3 * n], _start_all(make), _wait_all(make))


def _rs_swap_hosted(halves):
    n = len(halves)

    def make(ins, outs, sems):
        x, y, c = _place()
        return [pltpu.make_async_remote_copy(src_ref=ins[w], dst_ref=outs[w], send_sem=sems[0].at[w],
                                             recv_sem=sems[1].at[w], device_id=(x, y, 1 - c), device_id_type=MESH)
                for w in range(n)]

    return _Hosted(halves, [SDS(h.shape, F32) for h in halves], [n, n], _start_all(make), _wait_all(make))


HBM_SPEC = pl.BlockSpec(memory_space=pltpu.HBM)
SEM_SPEC = pl.BlockSpec(memory_space=pltpu.SEMAPHORE)
_EFFECT = pltpu.SideEffectType.DATAFLOW_SIDE_EFFECTING


def _split_start(h, name):
    n_in, n_out, ns = len(h.ins), len(h.out_shapes), len(h.sems)
    ins = [pltpu.with_memory_space_constraint(a, pltpu.HBM) for a in h.ins]
    lands = [pltpu.with_memory_space_constraint(lax.empty(o.shape, o.dtype), pltpu.HBM) for o in h.out_shapes]

    def body(*refs):
        i_refs, l_refs = refs[:n_in], refs[n_in:n_in + n_out]
        s_refs = refs[n_in + n_out:n_in + n_out + ns]
        token = refs[-1]
        h.start(i_refs, l_refs, s_refs)
        token[...] = jnp.zeros_like(token)

    res = pl.pallas_call(
        body, name=name, in_specs=[HBM_SPEC] * (n_in + n_out),
        out_specs=[SEM_SPEC] * ns + [HBM_SPEC] * n_out + [VMEM_SPEC],
        out_shape=[pltpu.SemaphoreType.DMA((k,)) for k in h.sems]
        + [pltpu.HBM(o.shape, o.dtype) for o in h.out_shapes] + [SDS((SUBLANES, LANES), F32)],
        input_output_aliases={n_in + k: ns + k for k in range(n_out)},
        compiler_params=pltpu.CompilerParams(has_side_effects=_EFFECT))(*ins, *lands)
    return list(res[:ns]) + ins + list(res[ns:-1]), res[-1]


def _split_wait(h, state, after, name):
    n_in, n_out, ns = len(h.ins), len(h.out_shapes), len(h.sems)
    sems, bufs = state[:ns], state[ns:]

    def body(*refs):
        i_refs, l_refs = refs[:n_in], refs[n_in:n_in + n_out]
        s_refs = refs[n_in + n_out:n_in + n_out + ns]
        h.finish(i_refs, l_refs, s_refs)

    res = pl.pallas_call(
        body, name=name, in_specs=[HBM_SPEC] * (n_in + n_out) + [SEM_SPEC] * ns + [ANY],
        out_specs=[HBM_SPEC] * n_out,
        out_shape=[pltpu.HBM(b.shape, b.dtype) for b in bufs[n_in:]],
        input_output_aliases={n_in + k: k for k in range(n_out)},
        compiler_params=pltpu.CompilerParams(has_side_effects=_EFFECT))(*bufs, *sems, after)
    return list(res)


def _run_comm(hosted, name):
    return _call(lambda: None, hosted, None, name=name, grid=(), in_specs=[], out_specs=[], out_shape=[],
                 scratch_shapes=[], args=(), sem=None)[1]


def _row_tile(rows, cols, n_arrays):
    budget = 24 * 1024 * 1024 // (2 * 4 * n_arrays * cols)
    best = SUBLANES
    for t in range(SUBLANES, rows + 1, SUBLANES):
        if rows % t == 0 and t <= budget:
            best = t
    return best


def _place_index(which):
    x, y, c = _place()
    v = c if which == "c" else 2 * x + y
    return jnp.reshape(v, (1,)).astype(jnp.int32)


def _add_own_half(full, recv, name, wire=BF16):
    nsh, rows, cols = full.shape
    hr = rows // 2
    t = _row_tile(hr, cols, 4)
    nt = hr // t

    def body(c_ref, a_ref, b_ref, o_ref, ob_ref):
        v = a_ref[...] + b_ref[...]
        o_ref[...] = v
        ob_ref[...] = v.astype(wire)

    half = pl.BlockSpec((1, t, cols), lambda s_, i, c_ref: (s_, i, 0))
    return pl.pallas_call(
        body, name=name,
        grid_spec=pltpu.PrefetchScalarGridSpec(
            num_scalar_prefetch=1, grid=(nsh, nt),
            in_specs=[pl.BlockSpec((1, t, cols), lambda s_, i, c_ref: (s_, c_ref[0] * nt + i, 0)), half],
            out_specs=[half, half]),
        out_shape=[SDS((nsh, hr, cols), F32), SDS((nsh, hr, cols), wire)],
        compiler_params=_cp(("parallel", "parallel")))(_place_index("c"), full, recv)


def _sum_chips(own, recv, name):
    nsh, hr, cols = own.shape
    t = _row_tile(hr, cols, 6)

    def body(j_ref, own_ref, *rest):
        r_refs, o_ref = rest[:nsh], rest[nsh]
        j = j_ref[0]
        mine = own_ref[0]
        parts = [jnp.where(j == k, mine, r_refs[k][0].astype(F32)) for k in range(nsh)]
        o_ref[...] = ((parts[0] + parts[1]) + parts[2]) + parts[3]

    def other(k):
        return pl.BlockSpec((1, t, cols), lambda i, j_ref: (jnp.where(j_ref[0] == k, (k + 1) % nsh, k), i, 0))

    return pl.pallas_call(
        body, name=name,
        grid_spec=pltpu.PrefetchScalarGridSpec(
            num_scalar_prefetch=1, grid=(hr // t,),
            in_specs=[pl.BlockSpec((1, t, cols), lambda i, j_ref: (j_ref[0], i, 0))]
            + [other(k) for k in range(nsh)],
            out_specs=pl.BlockSpec((t, cols), lambda i, j_ref: (i, 0))),
        out_shape=SDS((hr, cols), F32), compiler_params=_cp(("parallel",)))(_place_index("j"), own, *([recv] * nsh))


def _adamw_math(w, g, m, v):
    m = ADAM_B1 * m + (1.0 - ADAM_B1) * g
    v = ADAM_B2 * v + (1.0 - ADAM_B2) * (g * g)
    m_hat = m / (1.0 - ADAM_B1 ** ADAM_STEP)
    v_hat = v / (1.0 - ADAM_B2 ** ADAM_STEP)
    delta = -ADAM_LR * (m_hat / (jnp.sqrt(v_hat) + ADAM_EPS) + ADAM_WD * w)
    return delta, m, v


def _adamw_big(w, g_own, g_sib, m, v, name, token=None):
    _, rows, cols = w.shape
    hr = rows // 2
    t = _row_tile(hr, cols, 9)
    nth = hr // t
    if token is None:
        token = jnp.zeros((SUBLANES, LANES), F32)

    def body(c_ref, w_ref, go_ref, gs_ref, m_ref, v_ref, tok_ref, g_ref, d_ref, mo_ref, vo_ref):
        own = (pl.program_id(0) // nth) == c_ref[0]
        g = jnp.where(own, go_ref[...], gs_ref[...]) + tok_ref[0:1, 0:1]
        g_ref[0] = g
        d_ref[0], mo_ref[0], vo_ref[0] = _adamw_math(w_ref[0], g, m_ref[0], v_ref[0])

    spec = pl.BlockSpec((1, t, cols), lambda i, c_ref: (0, i, 0))
    hspec = pl.BlockSpec((t, cols), lambda i, c_ref: (i % nth, 0))
    tspec = pl.BlockSpec((SUBLANES, LANES), lambda i, c_ref: (0, 0))
    return pl.pallas_call(
        body, name=name,
        grid_spec=pltpu.PrefetchScalarGridSpec(
            num_scalar_prefetch=1, grid=(2 * nth,), in_specs=[spec, hspec, hspec, spec, spec, tspec],
            out_specs=[spec] * 4),
        out_shape=[SDS((1, rows, cols), F32)] * 4,
        compiler_params=_cp(("parallel",)))(_place_index("c"), w, g_own, g_sib, m, v, token)


def _build_slab(mix_slab, dg_mix, dg_ffn, dg_fin, loss8):
    def body(ms_ref, gm_ref, gf_ref, gn_ref, loss_ref, out_ref):
        rows = []
        for ref in (gm_ref, gf_ref, gn_ref):
            v = jnp.sum(ref[...], axis=0, keepdims=True)
            rows += [v[:, :SLAB_W], v[:, SLAB_W:]]
        rows.append(jnp.concatenate([loss_ref[0:1, :]] * (SLAB_W // LANES), axis=1))
        rows.append(jnp.zeros((SLAB_ROWS - ROW_LOSS - 1, SLAB_W), F32))
        tail = jnp.concatenate(rows, axis=0)
        for k in range(N_CHIPS):
            out_ref[k, 0:MIX_SLAB_ROWS, :] = ms_ref[...]
            out_ref[k, MIX_SLAB_ROWS:SLAB_ROWS, :] = tail

    return pl.pallas_call(
        body, name="build_slab", in_specs=[VMEM_SPEC] * 5, out_specs=VMEM_SPEC,
        out_shape=SDS((N_CHIPS, SLAB_ROWS, SLAB_W), F32),
        compiler_params=_cp())(mix_slab, dg_mix, dg_ffn, dg_fin, loss8)


_SMALL_ROWS = (("conv_b", ROW_CONV_B), ("gate_a_b", ROW_BA), ("gate_x_b", ROW_BX), ("lru_lambda", ROW_LAM),
               ("pool_b", ROW_PB), ("pool_scale", ROW_PS), ("norm_lru_g", ROW_GL), ("norm_pool_g", ROW_GP))
_WIDE_ROWS = (("norm_mix_g", ROW_MIX), ("norm_ffn_g", ROW_FFN), ("final_norm_g", ROW_FIN))
_BLOCK_ROWS = (("gate_a_w", ROW_GA), ("gate_x_w", ROW_GX), ("pool_w", ROW_PW))
_SMALL_ORDER = tuple(n for n, _ in _SMALL_ROWS) + tuple(n for n, _ in _WIDE_ROWS) + tuple(
    n for n, _ in _BLOCK_ROWS) + ("conv_w",)


def _adamw_small(slab_own, slab_sib, wmv):
    names = _SMALL_ORDER
    flat = [a for nme in names for a in wmv[nme]]
    nin = len(flat)

    def body(*refs):
        own_ref, sib_ref, j_ref = refs[0], refs[1], refs[2]
        ins = refs[3:3 + nin]
        outs = refs[3 + nin:-1]
        first = j_ref[1] == 0
        slab_ref = jnp.concatenate([jnp.where(first, own_ref[...], sib_ref[...]),
                                    jnp.where(first, sib_ref[...], own_ref[...])], axis=0)
        refs[-1][...] = jnp.broadcast_to(slab_ref[ROW_LOSS:ROW_LOSS + 1, 0:LANES], (SUBLANES, LANES))
        grads = {}
        for nme, row in _SMALL_ROWS:
            grads[nme] = slab_ref[row:row + 1, :]
        for nme, row in _WIDE_ROWS:
            grads[nme] = jnp.concatenate([slab_ref[row:row + 1, :], slab_ref[row + 1:row + 2, :]], axis=1)
        full = slab_ref[ROW_CONV_W:ROW_CONV_W + CONV_WIDTH, :]
        jv = j_ref[0]
        g = jnp.zeros((CONV_WIDTH, LANES), F32)
        for jj in range(N_CHIPS):
            g = jnp.where(jv == jj, full[:, jj * LANES:(jj + 1) * LANES], g)
        grads["conv_w"] = g
        block_rows = dict(_BLOCK_ROWS)
        for idx, nme in enumerate(names):
            w_ref, m_ref, v_ref = ins[3 * idx:3 * idx + 3]
            if nme in block_rows:
                nblk, r, c = w_ref.shape
                parts = [(b, slab_ref[block_rows[nme]:block_rows[nme] + r, b * c:(b + 1) * c]) for b in range(nblk)]
            else:
                parts = [(Ellipsis, grads[nme])]
            for b, g in parts:
                delta, m, v = _adamw_math(w_ref[b], g, m_ref[b], v_ref[b])
                outs[4 * idx][b] = g
                outs[4 * idx + 1][b] = delta
                outs[4 * idx + 2][b] = m
                outs[4 * idx + 3][b] = v

    place = jnp.concatenate([_place_index("j"), _place_index("c")])
    out_shape = [SDS(wmv[nme][0].shape, F32) for nme in names for _ in range(4)] + [SDS((SUBLANES, LANES), F32)]
    res = pl.pallas_call(
        body, name="adamw_small",
        in_specs=[VMEM_SPEC, VMEM_SPEC, pl.BlockSpec(memory_space=pltpu.SMEM)] + [VMEM_SPEC] * nin,
        out_specs=[VMEM_SPEC] * len(out_shape), out_shape=out_shape,
        compiler_params=_cp())(slab_own, slab_sib, place, *flat)
    return {nme: tuple(res[4 * idx:4 * idx + 4]) for idx, nme in enumerate(names)}, res[-1]


_FFN = ("ffn_w1", "ffn_w3", "ffn_w2")
_TRANSPOSED = ("ffn_w1", "ffn_w3")


def _local_step(x, target, full, sp_, distributed):
    d = x.shape[1]
    (u,), got = _inproj(x, sp_["norm_mix_g"], full["w_in"],
                        [_ffn_gather_hosted([full["w_out"]])] if distributed else None)
    w_out = (got[0][0] if distributed else full["w_out"]).reshape(d, d)
    gather = [_ffn_gather_hosted([full[n] for n in _FFN])] if distributed else None
    (h, hres1, saved, pooled), got = _mixer_fwd(u, x, sp_, w_out, gather)
    w1, w3, w2 = got[0] if distributed else [full[n] for n in _FFN]
    h2, a1, a3, ff = _ffn_up(hres1, sp_["norm_ffn_g"], w1, w3)
    dh, dhb, loss8, dg_fin = _ffn_down(ff, hres1, target, sp_["final_norm_g"], w2)
    da1, da3 = _ffn_bwd_gate(dhb, a1, a3, w2)
    dws = list(_ffn_wgrad(h2, dhb, ff, da1, da3))
    rs1 = [_rs_sibling_hosted(dws)] if distributed else None
    (dhres1, dg_ffn), got = _ffn_bwd_down(da1, da3, dh, hres1, sp_["norm_ffn_g"], w1, w3, rs1)
    rs2 = None
    if distributed:
        pairs = [_add_own_half(a, r, "add_half_" + n) for n, a, r in zip(_FFN, dws, got[0])]
        rs2 = [_rs_chips_hosted([pb for _, pb in pairs])]
    (du, mix_slab, dwout), got = _mixer_bwd(u, saved, pooled, h, dhres1, sp_, w_out, rs2)
    g_mix = sp_["norm_mix_g"]
    if distributed:
        fin = [_sum_chips(pairs[k][0], got[0][k], "sum_chips_" + n) for k, n in enumerate(_FFN)]
        swap = _rs_swap_hosted(fin)
        state, token = _split_start(swap, "ffn_swap_start")
        g_mix = g_mix + token[0:1, 0:1]
    (gx, dwin, dg_mix), _ = _inproj_bwd(x, du, dhres1, g_mix, full["w_in"])
    if distributed:
        sib = _split_wait(swap, state, dg_mix, "ffn_swap_wait")
    big = {"w_in": dwin, "w_out": dwout.reshape(N_CHIPS, d // N_CHIPS, d)}
    for k, n in enumerate(_FFN):
        big[n] = (fin[k], sib[k]) if distributed else dws[k]
    return gx, big, (mix_slab, dg_mix, dg_ffn, dg_fin, loss8)


_SMALL_LAYOUT = {
    "gate_a_w": (lambda a: a[0], lambda a: a[None]),
    "gate_x_w": (lambda a: a[0], lambda a: a[None]),
    "pool_w": (lambda a: a[0], lambda a: a[None]),
    "conv_w": (lambda a: a[0], lambda a: a[None]),
    "final_norm_g": (lambda a: a[None], lambda a: a[0]),
}

_WEIGHTS = ("norm_mix_g", "w_in", "conv_w", "conv_b", "gate_a_w", "gate_a_b", "gate_x_w", "gate_x_b", "lru_lambda",
            "pool_w", "pool_b", "pool_scale", "norm_lru_g", "norm_pool_g", "w_out", "norm_ffn_g", "ffn_w1",
            "ffn_w3", "ffn_w2", "final_norm_g")


def kernel(x, norm_mix_g, w_in, conv_w, conv_b, gate_a_w, gate_a_b, gate_x_w, gate_x_b, lru_lambda, pool_w, pool_b, pool_scale, norm_lru_g, norm_pool_g, w_out, norm_ffn_g, ffn_w1, ffn_w3, ffn_w2, final_norm_g, loss_target, m_norm_mix_g, m_w_in, m_conv_w, m_conv_b, m_gate_a_w, m_gate_a_b, m_gate_x_w, m_gate_x_b, m_lru_lambda, m_pool_w, m_pool_b, m_pool_scale, m_norm_lru_g, m_norm_pool_g, m_w_out, m_norm_ffn_g, m_ffn_w1, m_ffn_w3, m_ffn_w2, m_final_norm_g, v_norm_mix_g, v_w_in, v_conv_w, v_conv_b, v_gate_a_w, v_gate_a_b, v_gate_x_w, v_gate_x_b, v_lru_lambda, v_pool_w, v_pool_b, v_pool_scale, v_norm_lru_g, v_norm_pool_g, v_w_out, v_norm_ffn_g, v_ffn_w1, v_ffn_w3, v_ffn_w2, v_final_norm_g):
    loc = locals()
    w = {n: loc[n] for n in _WEIGHTS}
    m = {n: loc["m_" + n] for n in _WEIGHTS}
    v = {n: loc["v_" + n] for n in _WEIGHTS}

    def lay(nme, a):
        return _SMALL_LAYOUT[nme][0](a) if nme in _SMALL_LAYOUT else a

    def unlay(nme, a):
        return _SMALL_LAYOUT[nme][1](a) if nme in _SMALL_LAYOUT else a

    for group in (w, m, v):
        for n in _TRANSPOSED:
            group[n] = jnp.transpose(group[n], (0, 2, 1))

    gathered = _gather_weights([w[n][0] for n in _BIG], w["conv_w"][0], n_remote=1)
    full = dict(zip(_BIG, gathered[:-1]))
    cw_all = gathered[-1]
    sp_ = {n: lay(n, w[n]) for n in _SMALL_ORDER}
    sp_["conv_w"] = jnp.transpose(cw_all[:, :CONV_WIDTH, :], (1, 0, 2)).reshape(CONV_WIDTH, N_CHIPS * LANES)

    gx, big, small = _local_step(x[0], loss_target[0], full, sp_, distributed=True)

    late = ("w_in", "w_out", "slab")
    big["slab"] = _build_slab(*small)
    fin = {n: big[n][0] for n in _FFN}
    sib = {n: big[n][1] for n in _FFN}
    recv1, = _run_comm([_rs_sibling_hosted([big[n] for n in late])], "tail_sibling")
    pairs = [_add_own_half(big[n], r, "add_half_" + n, F32 if n == "slab" else BF16) for n, r in zip(late, recv1)]
    chips = _rs_chips_hosted([pb for _, pb in pairs])
    state, token = _split_start(chips, "tail_chips_start")
    out = {}
    for n in _FFN:
        out[n] = tuple(_adamw_big(w[n], fin[n], sib[n], m[n], v[n], "adamw_" + n, token))
    recv2 = _split_wait(chips, state, out[_FFN[-1]][1], "tail_chips_wait")
    for n, (p, _), r in zip(late, pairs, recv2):
        fin[n] = _sum_chips(p, r, "sum_chips_" + n)
    swapped, = _run_comm([_rs_swap_hosted([fin[n] for n in late])], "tail_swap")
    sib.update(zip(late, swapped))
    for n in late[:2]:
        out[n] = tuple(_adamw_big(w[n], fin[n], sib[n], m[n], v[n], "adamw_" + n))
    for n in _TRANSPOSED:
        out[n] = tuple(jnp.transpose(a, (0, 2, 1)) for a in out[n])
    wmv = {n: (lay(n, w[n]), lay(n, m[n]), lay(n, v[n])) for n in _SMALL_ORDER}
    res, loss = _adamw_small(fin["slab"], sib["slab"], wmv)
    for n in _SMALL_ORDER:
        out[n] = tuple(unlay(n, a) for a in res[n])
    return (loss[0, 0], gx[None]) + tuple(out[n][k] for k in range(4) for n in _WEIGHTS)
```

```python
import functools
import math

import jax
import jax.numpy as jnp
from jax import lax
from jax.experimental import pallas as pl
from jax.experimental.pallas import tpu as pltpu

F32 = jnp.float32
BF16 = jnp.bfloat16
SDS = jax.ShapeDtypeStruct
MESH = pl.DeviceIdType.MESH

EPS = 1e-6
LRU_C = 8.0
CONV_WIDTH = 4
POOL_WINDOWS = (2, 4, 8, 16)
HALO = 16
LANES = 128
SUBLANES = 8
GATE_BLOCK = 256
N_CHIPS = 4

ADAM_LR = 0.001
ADAM_B1 = 0.9
ADAM_B2 = 0.999
ADAM_EPS = 1e-08
ADAM_WD = 0.01
ADAM_STEP = 10

TM_PROJ = 512
TM_MIX = 512
TM_FFN = 512
TM_WGRAD = 2048
MIX_SAVED = ("xc", "r", "ig", "a", "m2raw", "ge", "dge")
FFN_ROW_CHUNKS = 2
VMEM_LIMIT = 56 * 1024 * 1024

SLAB_W = 512
ROW_CONV_B, ROW_CONV_W, ROW_BA, ROW_BX, ROW_LAM, ROW_PB, ROW_PS, ROW_GL, ROW_GP = 0, 1, 5, 6, 7, 8, 9, 10, 11
ROW_GA, ROW_GX, ROW_PW = 16, 80, 144
ROW_MIX, ROW_FFN, ROW_FIN, ROW_LOSS = 272, 274, 276, 278
MIX_SLAB_ROWS = 272
SLAB_ROWS = 288


def _cp(sem=None, **kw):
    if sem is not None:
        kw["dimension_semantics"] = sem
    return pltpu.CompilerParams(vmem_limit_bytes=VMEM_LIMIT, **kw)


def _const_spec(shape):
    nd = len(shape)
    return pl.BlockSpec(shape, lambda *_: (0,) * nd, pipeline_mode=pl.Buffered(1))


def _sigmoid(x):
    return 1.0 / (1.0 + jnp.exp(-x))


def _dot(a, b):
    return jnp.dot(a, b, preferred_element_type=F32)


def _dot_nt(a, b):
    return lax.dot_general(a, b, (((1,), (1,)), ((), ())), preferred_element_type=F32)


def _dot_tn(a, b):
    return lax.dot_general(a, b, (((0,), (0,)), ((), ())), preferred_element_type=F32)


def _colsum8(v):
    m, c = v.shape
    return v.reshape(m // SUBLANES, SUBLANES, c).sum(axis=0)


def _rowmean(v):
    return jnp.mean(v, axis=-1, keepdims=True)


def _rms_bwd(dy, xhat, r, g):
    dxh = dy * g
    return r * (dxh - xhat * _rowmean(dxh * xhat))


def _softplus_neg(lam):
    z = -lam
    e = jnp.exp(-jnp.abs(z))
    u = 1.0 + e
    d = u - 1.0
    log1p = jnp.where(d == 0.0, e, jnp.log(u) * (e / jnp.where(d == 0.0, 1.0, d)))
    return jnp.maximum(z, 0.0) + log1p


def _neg_expm1(z):
    series = -(z * (1.0 + z * (0.5 + z * (1.0 / 6.0 + z * (1.0 / 24.0)))))
    return jnp.where(z > -0.03, series, 1.0 - jnp.exp(z))


_GELU_C = math.sqrt(2.0 / math.pi)
_GELU_K = 0.044715


def _gelu_parts(x):
    x2 = x * x
    th = jnp.tanh(_GELU_C * (x + _GELU_K * x2 * x))
    ge = 0.5 * x * (1.0 + th)
    dge = 0.5 * (1.0 + th) + 0.5 * x * (1.0 - th * th) * (_GELU_C * (1.0 + 3.0 * _GELU_K * x2))
    return ge, dge


def _shift_down(halo, tile, k):
    if k == 0:
        return tile
    ext = jnp.concatenate([halo, tile], axis=0)
    h = halo.shape[0]
    return pltpu.roll(ext, k, 0)[h:]


def _shift_up(tile, nxt, k):
    if k == 0:
        return tile
    ext = jnp.concatenate([tile, nxt], axis=0)
    return pltpu.roll(ext, ext.shape[0] - k, 0)[:tile.shape[0]]


def _build_gate_blocks(ga_ref, gx_ref, gw_ref):
    hd = ga_ref.shape[1]
    per = GATE_BLOCK // hd
    zero = jnp.zeros((hd, hd), F32)
    for b in range(gw_ref.shape[0]):
        for src, off in ((ga_ref, 0), (gx_ref, GATE_BLOCK)):
            for hh in range(per):
                row = jnp.concatenate([zero] * hh + [src[b * per + hh]] + [zero] * (per - 1 - hh), axis=1)
                gw_ref[b, hh * hd:(hh + 1) * hd, off:off + GATE_BLOCK] = row.astype(BF16)


def _scan_level1(a, b, reverse):
    m, c = a.shape
    a3 = a.reshape(m // SUBLANES, SUBLANES, c)
    b3 = b.reshape(m // SUBLANES, SUBLANES, c)
    row = lax.broadcasted_iota(jnp.int32, a3.shape, 1)
    for s in (1, 2, 4):
        sh = (SUBLANES - s) if reverse else s
        a_sh = pltpu.roll(a3, sh, 1)
        b_sh = pltpu.roll(b3, sh, 1)
        ok = (row < SUBLANES - s) if reverse else (row >= s)
        b3 = jnp.where(ok, a3 * b_sh + b3, b3)
        a3 = jnp.where(ok, a3 * a_sh, a3)
    return a3.reshape(m, c), b3.reshape(m, c)


def _scan_level2(a_ref, b_ref, out_ref, carry, reverse):
    m, c = a_ref.shape
    ng = m // SUBLANES

    def step(g, cr):
        gi = (ng - 1 - g) if reverse else g
        off = pl.multiple_of(gi * SUBLANES, SUBLANES)
        h = b_ref[pl.ds(off, SUBLANES), :] + a_ref[pl.ds(off, SUBLANES), :] * cr
        out_ref[pl.ds(off, SUBLANES), :] = h
        edge = h[0:1, :] if reverse else h[SUBLANES - 1:SUBLANES, :]
        return jnp.broadcast_to(edge, (SUBLANES, c))

    return lax.fori_loop(0, ng, step, carry, unroll=4)


def _mixer_recompute(u_ref, hal, t0, cw, cb, gw_ref, ba, bx, lam, pw_ref, pb, ps, saved_ref, pooled_ref):
    tm = u_ref.shape[0]
    lw = cb.shape[1]
    keep = {name: k for k, name in enumerate(MIX_SAVED)}
    hal_l, hal_p = hal[:, :lw], hal[:, 2 * lw:]
    xc = cb
    for k in range(CONV_WIDTH):
        xc = xc + _shift_down(hal_l, u_ref[:, :lw], CONV_WIDTH - 1 - k) * cw[k:k + 1, :]
    saved_ref[keep["xc"]] = xc
    xcb = xc.astype(BF16)
    nb = lw // GATE_BLOCK
    gs = [_dot(xcb[:, b * GATE_BLOCK:(b + 1) * GATE_BLOCK], gw_ref[b]) for b in range(nb)]
    r = _sigmoid(jnp.concatenate([g[:, :GATE_BLOCK] for g in gs], axis=1) + ba)
    saved_ref[keep["r"]] = r
    ig = _sigmoid(jnp.concatenate([g[:, GATE_BLOCK:] for g in gs], axis=1) + bx)
    saved_ref[keep["ig"]] = ig
    la = (-LRU_C * r) * _softplus_neg(lam)
    a = jnp.exp(la)
    saved_ref[keep["a"]] = a
    m2raw = _neg_expm1(2.0 * la)
    saved_ref[keep["m2raw"]] = m2raw
    bb = jnp.sqrt(jnp.maximum(m2raw, 1e-12)) * (ig * saved_ref[keep["xc"]])
    ge, dge = _gelu_parts(u_ref[:, lw:2 * lw])
    saved_ref[keep["ge"]] = ge
    saved_ref[keep["dge"]] = dge
    row = lax.broadcasted_iota(jnp.int32, (HALO, LANES), 0) + t0
    zs = []
    for gi, w in enumerate(POOL_WINDOWS):
        sl = slice(gi * LANES, (gi + 1) * LANES)
        e = jnp.concatenate([hal_p[:, sl], u_ref[:, 2 * lw + gi * LANES:2 * lw + (gi + 1) * LANES]], axis=0)
        s = e
        k = 1
        while k < w:
            s = s + pltpu.roll(s, k, 0)
            k *= 2
        mean = jnp.concatenate([s[HALO:2 * HALO] * (1.0 / jnp.minimum(row + 1, w).astype(F32)),
                                s[2 * HALO:] * (1.0 / w)], axis=0)
        pg = (mean - e[HALO:]).astype(BF16)
        pooled_ref[:, sl] = pg
        zs.append(_dot(pg, pw_ref[gi].astype(BF16)))
    y_pool = (jnp.concatenate(zs, axis=1) + pb) * ps
    return a, bb, y_pool


ANY = pl.BlockSpec(memory_space=pl.ANY)
VMEM_SPEC = pl.BlockSpec(memory_space=pltpu.VMEM)


class _Hosted:
    def __init__(self, ins, out_shapes, sems, start, finish, mid=None, aliases=None):
        self.ins, self.out_shapes, self.sems = list(ins), list(out_shapes), list(sems)
        self.start, self.mid, self.finish = start, mid, finish
        self.aliases = dict(aliases or {})


def _call(body, hosted, stage_preds, *, name, grid, in_specs, out_specs, out_shape, scratch_shapes, args, sem):
    hosted = list(hosted or [])
    n_in, n_out, n_scr = len(in_specs), len(out_specs), len(scratch_shapes)
    c_in = [a for h in hosted for a in h.ins]
    c_out = [o for h in hosted for o in h.out_shapes]
    c_sem = [pltpu.SemaphoreType.DMA((k,)) for h in hosted for k in h.sems]

    def full(*refs):
        p = 0
        parts = []
        for cnt in (n_in, len(c_in), n_out, len(c_out), n_scr, len(c_sem)):
            parts.append(refs[p:p + cnt])
            p += cnt
        hi, ci, ho, co, hs, cs = parts
        per = []
        a = b = c_ = 0
        for h in hosted:
            per.append((h, ci[a:a + len(h.ins)], co[b:b + len(h.out_shapes)], cs[c_:c_ + len(h.sems)]))
            a, b, c_ = a + len(h.ins), b + len(h.out_shapes), c_ + len(h.sems)
        first = mid = last = None
        if hosted and grid:
            first, mid, last = stage_preds()

        def run(fn, pred, i_, o_, s_):
            if fn is None:
                return
            if pred is None:
                fn(i_, o_, s_)
            else:
                pl.when(pred)(functools.partial(fn, i_, o_, s_))

        for h, i_, o_, s_ in per:
            run(h.start, first, i_, o_, s_)
        body(*hi, *ho, *hs)
        for h, i_, o_, s_ in per:
            run(h.mid, mid, i_, o_, s_)
        for h, i_, o_, s_ in per:
            run(h.finish, last, i_, o_, s_)

    aliases = {}
    a = b = 0
    for h in hosted:
        for k, v in h.aliases.items():
            aliases[n_in + a + k] = n_out + b + v
        a, b = a + len(h.ins), b + len(h.out_shapes)
    res = pl.pallas_call(
        full, name=name, grid=grid, in_specs=list(in_specs) + [ANY] * len(c_in),
        out_specs=list(out_specs) + [ANY] * len(c_out), out_shape=list(out_shape) + c_out,
        scratch_shapes=list(scratch_shapes) + c_sem, input_output_aliases=aliases,
        compiler_params=_cp(sem))(*args, *c_in)
    res = list(res)
    outs = []
    p = n_out
    for h in hosted:
        outs.append(res[p:p + len(h.out_shapes)])
        p += len(h.out_shapes)
    return res[:n_out], outs


def _inproj(x, g_mix, w_in, hosted=None):
    s, d = x.shape
    n = w_in.shape[1]
    tm = min(TM_PROJ, s)
    nt = s // tm

    def body(x_ref, g_ref, w_ref, u_ref):
        xv = x_ref[...]
        r = lax.rsqrt(_rowmean(xv * xv) + EPS)
        u_ref[...] = _dot((xv * r * g_ref[...]).astype(BF16), w_ref[...])

    def stages():
        i = pl.program_id(0)
        return i == 0, i == max(nt - 3, 0), i == nt - 1

    return _call(
        body, hosted, stages, grid=(nt,), name="inproj",
        in_specs=[pl.BlockSpec((tm, d), lambda i: (i, 0)), _const_spec((1, d)), _const_spec((d, n))],
        out_specs=[pl.BlockSpec((tm, n), lambda i: (i, 0))], out_shape=[SDS((s, n), F32)], scratch_shapes=[],
        args=(x, g_mix, w_in), sem=("arbitrary",))


def _mixer_fwd(u, x, sp_, w_out, hosted=None):
    s, din = u.shape
    d = x.shape[1]
    lw = din // 3
    tm = min(TM_MIX, s)
    nb = lw // GATE_BLOCK

    def body(u_ref, halo_ref, x_ref, cw_ref, cb_ref, ga_ref, gx_ref, ba_ref, bx_ref, lam_ref, pw_ref, pb_ref,
             ps_ref, gl_ref, gp_ref, wout_ref, h_ref, hres_ref, saved_ref, pooled_ref,
             gw_s, a_s, b_s, carry_s):
        i = pl.program_id(0)

        @pl.when(i == 0)
        def _():
            _build_gate_blocks(ga_ref, gx_ref, gw_s)
            carry_s[...] = jnp.zeros_like(carry_s)

        hal = jnp.where(i > 0, halo_ref[...], 0.0)
        a, bb, yp = _mixer_recompute(u_ref, hal, i * tm, cw_ref[...], cb_ref[...], gw_s, ba_ref[...], bx_ref[...],
                                     lam_ref[...], pw_ref, pb_ref[...], ps_ref[...], saved_ref, pooled_ref)
        a1, b1 = _scan_level1(a, bb, reverse=False)
        a_s[...] = a1
        b_s[...] = b1
        carry_s[...] = _scan_level2(a_s, b_s, h_ref, carry_s[...], reverse=False)
        y_lru = h_ref[...] * saved_ref[MIX_SAVED.index("ge")]
        rl = lax.rsqrt(_rowmean(y_lru * y_lru) + EPS)
        rp = lax.rsqrt(_rowmean(yp * yp) + EPS)
        yn = jnp.concatenate([y_lru * rl * gl_ref[...], yp * rp * gp_ref[...]], axis=1).astype(BF16)
        hres_ref[...] = x_ref[...] + _dot(yn, wout_ref[...])

    small = [sp_[k] for k in ("conv_w", "conv_b", "gate_a_w", "gate_x_w", "gate_a_b", "gate_x_b", "lru_lambda",
                              "pool_w", "pool_b", "pool_scale", "norm_lru_g", "norm_pool_g")]
    nt = s // tm

    def stages():
        i = pl.program_id(0)
        return i == 0, i == max(nt - 3, 0), i == nt - 1

    return _call(
        body, hosted, stages, grid=(nt,), name="mixer_fwd",
        in_specs=[pl.BlockSpec((tm, din), lambda i: (i, 0)),
                  pl.BlockSpec((HALO, din), lambda i: (jnp.maximum(i * (tm // HALO) - 1, 0), 0)),
                  pl.BlockSpec((tm, d), lambda i: (i, 0))]
        + [_const_spec(a.shape) for a in small] + [_const_spec(w_out.shape)],
        out_specs=[pl.BlockSpec((tm, lw), lambda i: (i, 0)), pl.BlockSpec((tm, d), lambda i: (i, 0)),
                   pl.BlockSpec((len(MIX_SAVED), tm, lw), lambda i: (0, i, 0)),
                   pl.BlockSpec((tm, lw), lambda i: (i, 0))],
        out_shape=[SDS((s, lw), F32), SDS((s, d), F32), SDS((len(MIX_SAVED), s, lw), F32),
                   SDS((s, lw), BF16)],
        scratch_shapes=[pltpu.VMEM((nb, GATE_BLOCK, 2 * GATE_BLOCK), BF16), pltpu.VMEM((tm, lw), F32),
                        pltpu.VMEM((tm, lw), F32), pltpu.VMEM((SUBLANES, lw), F32)],
        args=(u, u, x, *small, w_out), sem=("arbitrary",))


def _row_chunks(tm):
    rc = tm // FFN_ROW_CHUNKS
    return [slice(q * rc, (q + 1) * rc) for q in range(FFN_ROW_CHUNKS)]


def _ffn_up(hres1, g_ffn, w1, w3):
    s, d = hres1.shape
    nj, fc, _ = w1.shape
    tm = min(TM_FFN, s)

    def body(h_ref, gf_ref, w1_ref, w3_ref, h2_ref, a1_ref, a3_ref, ff_ref):
        hv = h_ref[...]
        r = lax.rsqrt(_rowmean(hv * hv) + EPS)
        h2_ref[...] = (hv * r * gf_ref[...]).astype(BF16)
        h2 = h2_ref[...]
        for j in range(nj):
            a1 = _dot_nt(h2, w1_ref[j])
            a3 = _dot_nt(h2, w3_ref[j])
            a1_ref[j] = a1.astype(BF16)
            a3_ref[j] = a3.astype(BF16)
            ff_ref[j] = ((a1 * _sigmoid(a1)) * a3).astype(BF16)

    wspec = _const_spec(w1.shape)
    aspec = pl.BlockSpec((nj, tm, fc), lambda i: (0, i, 0))
    return pl.pallas_call(
        body, grid=(s // tm,), name="ffn_up",
        in_specs=[pl.BlockSpec((tm, d), lambda i: (i, 0)), _const_spec((1, d)), wspec, wspec],
        out_specs=[pl.BlockSpec((tm, d), lambda i: (i, 0)), aspec, aspec, aspec],
        out_shape=[SDS((s, d), BF16)] + [SDS((nj, s, fc), BF16)] * 3,
        compiler_params=_cp(("parallel",)))(hres1, g_ffn, w1, w3)


def _ffn_down(ff, hres1, target, g_fin, w2):
    s, d = hres1.shape
    nj, _, fc = ff.shape
    tm = min(TM_FFN, s)

    def body(ff_ref, h_ref, t_ref, gn_ref, w2_ref, dh_ref, dhb_ref, loss_ref, dgn_ref):
        @pl.when(pl.program_id(0) == 0)
        def _():
            loss_ref[...] = jnp.zeros_like(loss_ref)
            dgn_ref[...] = jnp.zeros_like(dgn_ref)

        gn = gn_ref[...]
        for rows in _row_chunks(tm):
            acc = _dot(ff_ref[0, rows, :], w2_ref[0])
            for j in range(1, nj):
                acc = acc + _dot(ff_ref[j, rows, :], w2_ref[j])
            hr2 = h_ref[rows, :] + acc
            r2 = lax.rsqrt(_rowmean(hr2 * hr2) + EPS)
            xh = hr2 * r2
            diff = xh * gn - t_ref[rows, :]
            tot = jnp.sum(jnp.sum(diff * diff, axis=1, keepdims=True), axis=0, keepdims=True)
            loss_ref[...] += tot * (0.5 / d)
            dout = diff * (1.0 / d)
            dgn_ref[...] += _colsum8(dout * xh)
            dh = _rms_bwd(dout, xh, r2, gn)
            dh_ref[rows, :] = dh
            dhb_ref[rows, :] = dh.astype(BF16)

    tile = pl.BlockSpec((tm, d), lambda i: (i, 0))
    return pl.pallas_call(
        body, grid=(s // tm,), name="ffn_down",
        in_specs=[pl.BlockSpec((nj, tm, fc), lambda i: (0, i, 0)), tile, tile, _const_spec((1, d)),
                  _const_spec(w2.shape)],
        out_specs=[tile, tile, pl.BlockSpec((SUBLANES, LANES), lambda i: (0, 0)),
                   pl.BlockSpec((SUBLANES, d), lambda i: (0, 0))],
        out_shape=[SDS((s, d), F32), SDS((s, d), BF16), SDS((SUBLANES, LANES), F32), SDS((SUBLANES, d), F32)],
        compiler_params=_cp(("arbitrary",)))(ff, hres1, target, g_fin, w2)


def _ffn_bwd_gate(dhb, a1, a3, w2):
    s, d = dhb.shape
    nj, _, fc = a1.shape
    tm = min(TM_FFN, s)

    def body(dhb_ref, a1_ref, a3_ref, w2_ref, da1_ref, da3_ref):
        for j in range(nj):
            for rows in _row_chunks(tm):
                dff = _dot_nt(dhb_ref[rows, :], w2_ref[j])
                a1v = a1_ref[j, rows, :].astype(F32)
                sg = _sigmoid(a1v)
                silu = a1v * sg
                da1_ref[j, rows, :] = (dff * a3_ref[j, rows, :].astype(F32)
                                       * (sg * (1.0 + (a1v - silu)))).astype(BF16)
                da3_ref[j, rows, :] = (dff * silu).astype(BF16)

    aspec = pl.BlockSpec((nj, tm, fc), lambda i: (0, i, 0))
    return pl.pallas_call(
        body, grid=(s // tm,), name="ffn_bwd_gate",
        in_specs=[pl.BlockSpec((tm, d), lambda i: (i, 0)), aspec, aspec, _const_spec(w2.shape)],
        out_specs=[aspec, aspec], out_shape=[SDS((nj, s, fc), BF16)] * 2,
        compiler_params=_cp(("parallel",)))(dhb, a1, a3, w2)


def _ffn_bwd_down(da1, da3, dh, hres1, g_ffn, w1, w3, hosted=None):
    s, d = hres1.shape
    nj, _, fc = da1.shape
    tm = min(TM_FFN, s)
    nt = s // tm

    def body(da1_ref, da3_ref, dh_ref, h_ref, gf_ref, w1_ref, w3_ref, dhr_ref, dgf_ref):
        @pl.when(pl.program_id(0) == 0)
        def _():
            dgf_ref[...] = jnp.zeros_like(dgf_ref)

        gf = gf_ref[...]
        for rows in _row_chunks(tm):
            dh2 = None
            for j in range(nj):
                part = _dot(da1_ref[j, rows, :], w1_ref[j]) + _dot(da3_ref[j, rows, :], w3_ref[j])
                dh2 = part if dh2 is None else dh2 + part
            hv = h_ref[rows, :]
            r = lax.rsqrt(_rowmean(hv * hv) + EPS)
            xh = hv * r
            dgf_ref[...] += _colsum8(dh2 * xh)
            dhr_ref[rows, :] = dh_ref[rows, :] + _rms_bwd(dh2, xh, r, gf)

    tile = pl.BlockSpec((tm, d), lambda i: (i, 0))
    aspec = pl.BlockSpec((nj, tm, fc), lambda i: (0, i, 0))
    wspec = _const_spec(w1.shape)

    def stages():
        i = pl.program_id(0)
        return i == 0, i == max(nt - 2, 0), i == nt - 1

    return _call(
        body, hosted, stages, grid=(nt,), name="ffn_bwd_down",
        in_specs=[aspec, aspec, tile, tile, _const_spec((1, d)), wspec, wspec],
        out_specs=[tile, pl.BlockSpec((SUBLANES, d), lambda i: (0, 0))],
        out_shape=[SDS((s, d), F32), SDS((SUBLANES, d), F32)],
        scratch_shapes=[], args=(da1, da3, dh, hres1, g_ffn, w1, w3), sem=("arbitrary",))


def _ffn_wgrad(h2, dhb, ff, da1, da3):
    s, d = h2.shape
    _, _, fc = ff.shape
    tm = min(TM_WGRAD, s)

    def body(h2_ref, dhb_ref, ff_ref, da1_ref, da3_ref, dw1_ref, dw3_ref, dw2_ref):
        @pl.when(pl.program_id(1) == 0)
        def _():
            dw1_ref[...] = jnp.zeros_like(dw1_ref)
            dw3_ref[...] = jnp.zeros_like(dw3_ref)
            dw2_ref[...] = jnp.zeros_like(dw2_ref)

        h2v = h2_ref[...]
        dw1_ref[0] += _dot_tn(da1_ref[0], h2v)
        dw3_ref[0] += _dot_tn(da3_ref[0], h2v)
        dw2_ref[0] += _dot_tn(ff_ref[0], dhb_ref[...])

    wspec = pl.BlockSpec((1, fc, d), lambda j, i: (j, 0, 0))
    return pl.pallas_call(
        body, grid=(N_CHIPS, s // tm), name="ffn_wgrad",
        in_specs=[pl.BlockSpec((tm, d), lambda j, i: (i, 0)), pl.BlockSpec((tm, d), lambda j, i: (i, 0))]
        + [pl.BlockSpec((1, tm, fc), lambda j, i: (j, i, 0))] * 3,
        out_specs=[wspec] * 3, out_shape=[SDS((N_CHIPS, fc, d), F32)] * 3,
        compiler_params=_cp(("parallel", "arbitrary")))(h2, dhb, ff, da1, da3)


def _mixer_bwd(u, saved, pooled, h, dhres1, sp_, w_out, hosted=None):
    s, din = u.shape
    d = dhres1.shape[1]
    lw = din // 3
    tm = min(TM_MIX, s)
    nt = s // tm
    nb = lw // GATE_BLOCK
    hd = sp_["gate_a_w"].shape[1]

    def body(ul_ref, saved_ref, pooled_ref, h_ref, hhalo_ref, dhr_ref, cw_ref, cb_ref, ga_ref, gx_ref, ba_ref,
             bx_ref, lam_ref, pw_ref, pb_ref, ps_ref, gl_ref, gp_ref, wout_ref, du_ref, slab_ref, dwout_ref,
             gw_s, a_s, b_s, e_s, ecarry_s, dxc_s, q_s, vec_s, cwacc_s, dgw_s, dpw_s):
        i = pl.program_id(0)
        tile = nt - 1 - i

        @pl.when(i == 0)
        def _():
            _build_gate_blocks(ga_ref, gx_ref, gw_s)
            for ref in (ecarry_s, dxc_s, q_s, vec_s, cwacc_s, dgw_s, dpw_s, dwout_ref):
                ref[...] = jnp.zeros_like(ref)

        cw = cw_ref[...]
        lam = lam_ref[...]
        ps = ps_ref[...]
        def sv_(name):
            return saved_ref[MIX_SAVED.index(name)]

        f = {"sp": _softplus_neg(lam)}
        row = lax.broadcasted_iota(jnp.int32, (HALO, LANES), 0) + tile * tm
        f["z"] = jnp.concatenate(
            [_dot(pooled_ref[:, g * LANES:(g + 1) * LANES], pw_ref[g].astype(BF16))
             for g in range(len(POOL_WINDOWS))], axis=1) + pb_ref[...]
        f["y_pool"] = f["z"] * ps
        y_lru = h_ref[...] * sv_("ge")
        rl = lax.rsqrt(_rowmean(y_lru * y_lru) + EPS)
        yp = f["y_pool"]
        rp = lax.rsqrt(_rowmean(yp * yp) + EPS)
        xh_l = y_lru * rl
        xh_p = yp * rp

        dhrb = dhr_ref[...].astype(BF16)
        dyn = _dot_nt(dhrb, wout_ref[...])
        yn = jnp.concatenate([xh_l * gl_ref[...], xh_p * gp_ref[...]], axis=1).astype(BF16)
        dwout_ref[...] += _dot_tn(yn, dhrb)
        d_nl, d_np = dyn[:, :lw], dyn[:, lw:]
        vec = {}
        vec[ROW_GL] = _colsum8(d_nl * xh_l)
        vec[ROW_GP] = _colsum8(d_np * xh_p)
        d_ylru = _rms_bwd(d_nl, xh_l, rl, gl_ref[...])
        d_ypool = _rms_bwd(d_np, xh_p, rp, gp_ref[...])

        vec[ROW_PS] = _colsum8(d_ypool * f["z"])
        dz = d_ypool * ps
        vec[ROW_PB] = _colsum8(dz)
        dzb = dz.astype(BF16)
        dup = []
        for gi, w in enumerate(POOL_WINDOWS):
            sl = slice(gi * LANES, (gi + 1) * LANES)
            dpw_s[:, sl] += _dot_tn(pooled_ref[:, sl], dzb[:, sl])
            dpool = _dot_nt(dzb[:, sl], pw_ref[gi].astype(BF16))
            q = jnp.concatenate([dpool[:HALO] * (1.0 / jnp.minimum(row + 1, w).astype(F32)),
                                 dpool[HALO:] * (1.0 / w)], axis=0)
            e = jnp.concatenate([q, q_s[:, sl]], axis=0)
            k = 1
            while k < w:
                e = e + pltpu.roll(e, tm + HALO - k, 0)
                k *= 2
            dup.append(e[:tm] - dpool)
            q_s[:, sl] = q[:HALO]

        d_hout = d_ylru * sv_("ge")
        d_ug = d_ylru * h_ref[...] * sv_("dge")
        a1, b1 = _scan_level1(sv_("a"), sv_("a") * d_hout, reverse=True)
        a_s[...] = a1
        b_s[...] = b1
        e_next = ecarry_s[...]
        ecarry_s[...] = _scan_level2(a_s, b_s, e_s, e_next, reverse=True)
        sv = d_hout + _shift_up(e_s[...], e_next, 1)
        d_a = sv * _shift_down(jnp.where(tile > 0, hhalo_ref[...], 0.0), h_ref[...], 1)
        mult = jnp.sqrt(jnp.maximum(sv_("m2raw"), 1e-12))
        d_mult = sv * (sv_("ig") * sv_("xc"))
        d_ig = sv * mult * sv_("xc")
        d_xc = sv * mult * sv_("ig")
        a = sv_("a")
        d_la = d_a * a + jnp.where(sv_("m2raw") > 1e-12, d_mult * (-(a * a) / mult), 0.0)
        d_r = d_la * (-LRU_C * f["sp"])
        vec[ROW_LAM] = _colsum8(d_la * (-LRU_C * sv_("r")))
        d_pr = d_r * sv_("r") * (1.0 - sv_("r"))
        d_pi = d_ig * sv_("ig") * (1.0 - sv_("ig"))
        vec[ROW_BA] = _colsum8(d_pr)
        vec[ROW_BX] = _colsum8(d_pi)
        dxc_parts = []
        for b in range(nb):
            sl = slice(b * GATE_BLOCK, (b + 1) * GATE_BLOCK)
            rhs = jnp.concatenate([d_pr[:, sl], d_pi[:, sl]], axis=1).astype(BF16)
            dgw_s[b] += _dot_tn(saved_ref[MIX_SAVED.index("xc"), :, sl].astype(BF16), rhs)
            dxc_parts.append(_dot_nt(rhs, gw_s[b]))
        d_xc = d_xc + jnp.concatenate(dxc_parts, axis=1)
        vec[ROW_CONV_B] = _colsum8(d_xc)
        dxc_next = dxc_s[...]
        d_ul = None
        for k in range(CONV_WIDTH):
            ahead = _shift_up(d_xc, dxc_next, CONV_WIDTH - 1 - k)
            cwacc_s[k * SUBLANES:(k + 1) * SUBLANES, :] += _colsum8(ahead * ul_ref[...])
            term = ahead * cw[k:k + 1, :]
            d_ul = term if d_ul is None else d_ul + term
        dxc_s[...] = d_xc[:SUBLANES]
        for row, val in vec.items():
            vec_s[row * SUBLANES:(row + 1) * SUBLANES, :] += val
        du_ref[...] = jnp.concatenate([d_ul, d_ug] + dup, axis=1).astype(BF16)

        @pl.when(i == nt - 1)
        def _():
            rows = []
            for row in range(ROW_GA):
                if row in (ROW_CONV_W, ROW_CONV_W + 1, ROW_CONV_W + 2, ROW_CONV_W + 3):
                    k = row - ROW_CONV_W
                    v = jnp.sum(cwacc_s[k * SUBLANES:(k + 1) * SUBLANES, :], axis=0, keepdims=True)
                elif row <= ROW_GP:
                    v = jnp.sum(vec_s[row * SUBLANES:(row + 1) * SUBLANES, :], axis=0, keepdims=True)
                    if row == ROW_LAM:
                        v = v * (-1.0 / (1.0 + jnp.exp(lam)))
                else:
                    v = jnp.zeros((1, lw), F32)
                rows.append(v)
            slab_ref[0:ROW_GA, :] = jnp.concatenate(rows, axis=0)
            lane = lax.broadcasted_iota(jnp.int32, (hd, GATE_BLOCK), 1)
            for b in range(nb):
                for off, row0 in ((0, ROW_GA), (GATE_BLOCK, ROW_GX)):
                    acc = jnp.zeros((hd, GATE_BLOCK), F32)
                    for hh in range(GATE_BLOCK // hd):
                        m = (lane >= hh * hd) & (lane < (hh + 1) * hd)
                        acc = acc + jnp.where(m, dgw_s[b, hh * hd:(hh + 1) * hd, off:off + GATE_BLOCK], 0.0)
                    slab_ref[row0:row0 + hd, b * GATE_BLOCK:(b + 1) * GATE_BLOCK] = acc
            slab_ref[ROW_PW:ROW_PW + LANES, :] = dpw_s[...]

    small = [sp_[k] for k in ("conv_w", "conv_b", "gate_a_w", "gate_x_w", "gate_a_b", "gate_x_b", "lru_lambda",
                              "pool_w", "pool_b", "pool_scale", "norm_lru_g", "norm_pool_g")]
    rev = lambda i: nt - 1 - i

    def stages():
        i = pl.program_id(0)
        return i == 0, i == max(nt - 3, 0), i == nt - 1

    return _call(
        body, hosted, stages, grid=(nt,), name="mixer_bwd",
        in_specs=[pl.BlockSpec((tm, lw), lambda i: (rev(i), 0)),
                  pl.BlockSpec((len(MIX_SAVED), tm, lw), lambda i: (0, rev(i), 0)),
                  pl.BlockSpec((tm, lw), lambda i: (rev(i), 0)),
                  pl.BlockSpec((tm, lw), lambda i: (rev(i), 0)),
                  pl.BlockSpec((SUBLANES, lw), lambda i: (jnp.maximum(rev(i) * (tm // SUBLANES) - 1, 0), 0)),
                  pl.BlockSpec((tm, d), lambda i: (rev(i), 0))]
        + [_const_spec(a.shape) for a in small] + [_const_spec(w_out.shape)],
        out_specs=[pl.BlockSpec((tm, din), lambda i: (rev(i), 0)),
                   pl.BlockSpec((MIX_SLAB_ROWS, SLAB_W), lambda i: (0, 0)), pl.BlockSpec((d, d), lambda i: (0, 0))],
        out_shape=[SDS((s, din), BF16), SDS((MIX_SLAB_ROWS, SLAB_W), F32), SDS((d, d), F32)],
        scratch_shapes=[pltpu.VMEM((nb, GATE_BLOCK, 2 * GATE_BLOCK), BF16),
                        pltpu.VMEM((tm, lw), F32), pltpu.VMEM((tm, lw), F32), pltpu.VMEM((tm, lw), F32),
                        pltpu.VMEM((SUBLANES, lw), F32), pltpu.VMEM((SUBLANES, lw), F32),
                        pltpu.VMEM((HALO, lw), F32), pltpu.VMEM((ROW_GA * SUBLANES, lw), F32),
                        pltpu.VMEM((CONV_WIDTH * SUBLANES, lw), F32),
                        pltpu.VMEM((nb, GATE_BLOCK, 2 * GATE_BLOCK), F32), pltpu.VMEM((LANES, lw), F32)],
        args=(u, saved, pooled, h, h, dhres1, *small, w_out), sem=("arbitrary",))


def _inproj_bwd(x, du, dhres1, g_mix, w_in, hosted=None):
    s, d = x.shape
    n = w_in.shape[1]
    nc = n // N_CHIPS
    tm = min(TM_PROJ, s)
    nt = s // tm

    def body(x_ref, du_ref, dhr_ref, g_ref, w_ref, gx_ref, dwin_ref, dg_ref):
        i = pl.program_id(0)

        @pl.when(i == 0)
        def _():
            dwin_ref[...] = jnp.zeros_like(dwin_ref)
            dg_ref[...] = jnp.zeros_like(dg_ref)

        xv = x_ref[...]
        g = g_ref[...]
        r = lax.rsqrt(_rowmean(xv * xv) + EPS)
        xh = xv * r
        h1 = (xh * g).astype(BF16)
        duv = du_ref[...]
        dh1 = _dot_nt(duv, w_ref[...])
        dg_ref[...] += _colsum8(dh1 * xh)
        gx_ref[...] = dhr_ref[...] + _rms_bwd(dh1, xh, r, g)
        for jj in range(N_CHIPS):
            dwin_ref[jj] += _dot_tn(h1, duv[:, jj * nc:(jj + 1) * nc])

    def stages():
        i = pl.program_id(0)
        return i == 0, i == max(nt - 3, 0), i == nt - 1

    return _call(
        body, hosted, stages, grid=(nt,), name="inproj_bwd",
        in_specs=[pl.BlockSpec((tm, d), lambda i: (i, 0)), pl.BlockSpec((tm, n), lambda i: (i, 0)),
                  pl.BlockSpec((tm, d), lambda i: (i, 0)), _const_spec((1, d)), _const_spec((d, n))],
        out_specs=[pl.BlockSpec((tm, d), lambda i: (i, 0)), pl.BlockSpec((N_CHIPS, d, nc), lambda i: (0, 0, 0)),
                   pl.BlockSpec((SUBLANES, d), lambda i: (0, 0))],
        out_shape=[SDS((s, d), F32), SDS((N_CHIPS, d, nc), F32), SDS((SUBLANES, d), F32)],
        scratch_shapes=[], args=(x, du, dhres1, g_mix, w_in), sem=("arbitrary",))


def _place():
    x, y, c = lax.axis_index("x"), lax.axis_index("y"), lax.axis_index("c")
    return x, y, c


def _other_chips(x, y):
    return [(1 - x, y), (x, 1 - y), (1 - x, 1 - y)]


ANY = pl.BlockSpec(memory_space=pl.ANY)
VMEM_SPEC = pl.BlockSpec(memory_space=pltpu.VMEM)

_GATHERED = {"w_in": "cols", "w_out": "major", "ffn_w1": "major", "ffn_w3": "major", "ffn_w2": "major"}
_BIG = ("w_in", "w_out", "ffn_w1", "ffn_w3", "ffn_w2")


def _gather_weights(shards, conv_w, n_remote):
    n = len(shards)
    full_shapes = []
    for name, sh in zip(_BIG, shards):
        r, cdim = sh.shape
        if _GATHERED[name] == "cols":
            assert cdim % LANES == 0
            full_shapes.append((r, cdim * N_CHIPS))
        else:
            full_shapes.append((N_CHIPS, r, cdim))

    def region(ref, name, sh, jj, cc):
        r, cdim = sh
        rows = pl.ds(0, r) if cc is None else pl.ds(pl.multiple_of(cc * (r // 2), 16), r // 2)
        if _GATHERED[name] == "cols":
            return ref.at[rows, pl.ds(pl.multiple_of(jj * cdim, LANES), cdim)]
        return ref.at[jj, rows, :]

    def staged(ref, sh, cc):
        r = sh[0]
        return ref.at[pl.ds(pl.multiple_of(cc * (r // 2), 16), r // 2), :]

    def body(*refs):
        ins, cw_in = refs[:n], refs[n]
        outs, cw_out = refs[n + 1:2 * n + 1], refs[2 * n + 1]
        stage = refs[2 * n + 2:3 * n + 2]
        cw_stage, lsem, ssem, rsem, fssem, frsem, cssem, crsem = refs[3 * n + 2:]
        x, y, c = _place()
        j = 2 * x + y
        chips = _other_chips(x, y)
        for w in range(n_remote):
            stage[w][...] = ins[w][...].astype(BF16)
        cw_stage[...] = jnp.zeros_like(cw_stage)
        cw_stage[0:CONV_WIDTH, :] = cw_in[...]
        shs = [s_.shape for s_ in shards]
        local = [pltpu.make_async_copy(stage[w], region(outs[w], _BIG[w], shs[w], j, None), lsem.at[w])
                 for w in range(n)]
        local.append(pltpu.make_async_copy(cw_stage, cw_out.at[j], lsem.at[n]))
        sends = []
        for k, (px, py) in enumerate(chips):
            for w in range(n_remote):
                sends.append(pltpu.make_async_remote_copy(
                    src_ref=staged(stage[w], shs[w], c), dst_ref=region(outs[w], _BIG[w], shs[w], j, c),
                    send_sem=ssem.at[k * n + w], recv_sem=rsem.at[k * n + w], device_id=(px, py, c),
                    device_id_type=MESH))
            sends.append(pltpu.make_async_remote_copy(
                src_ref=cw_stage, dst_ref=cw_out.at[j], send_sem=cssem.at[k], recv_sem=crsem.at[k],
                device_id=(px, py, c), device_id_type=MESH))
        for cp in sends:
            cp.start()
        for w in range(n_remote, n):
            stage[w][...] = ins[w][...].astype(BF16)
        for cp in local:
            cp.start()
        fwd = []
        for k, (px, py) in enumerate(chips):
            jk = 2 * px + py
            for w in range(n_remote):
                reg = region(outs[w], _BIG[w], shs[w], jk, c)
                pltpu.make_async_remote_copy(src_ref=reg, dst_ref=reg, send_sem=ssem.at[k * n + w],
                                             recv_sem=rsem.at[k * n + w], device_id=(px, py, c),
                                             device_id_type=MESH).wait_recv()
                cp = pltpu.make_async_remote_copy(src_ref=reg, dst_ref=reg, send_sem=fssem.at[k * n + w],
                                                  recv_sem=frsem.at[k * n + w], device_id=(x, y, 1 - c),
                                                  device_id_type=MESH)
                cp.start()
                fwd.append(cp)
            pltpu.make_async_remote_copy(src_ref=cw_stage, dst_ref=cw_out.at[jk], send_sem=cssem.at[k],
                                         recv_sem=crsem.at[k], device_id=(px, py, c),
                                         device_id_type=MESH).wait_recv()
        for k, (px, py) in enumerate(chips):
            jk = 2 * px + py
            for w in range(n_remote):
                reg = region(outs[w], _BIG[w], shs[w], jk, 1 - c)
                pltpu.make_async_remote_copy(src_ref=reg, dst_ref=reg, send_sem=fssem.at[k * n + w],
                                             recv_sem=frsem.at[k * n + w], device_id=(x, y, 1 - c),
                                             device_id_type=MESH).wait_recv()
        for cp in sends + fwd:
            cp.wait_send()
        for cp in local:
            cp.wait()

    nsem = 3 * n
    return pl.pallas_call(
        body, name="gather_first",
        in_specs=[VMEM_SPEC] * (n + 1), out_specs=[ANY] * (n + 1),
        out_shape=[SDS(fs, BF16) for fs in full_shapes] + [SDS((N_CHIPS, SUBLANES, LANES), F32)],
        scratch_shapes=[pltpu.VMEM(s_.shape, BF16) for s_ in shards] + [pltpu.VMEM((SUBLANES, LANES), F32)]
        + [pltpu.SemaphoreType.DMA((n + 1,))] + [pltpu.SemaphoreType.DMA((nsem,))] * 4
        + [pltpu.SemaphoreType.DMA((3,))] * 2,
        compiler_params=_cp())(*shards, conv_w)


def _start_all(make):
    def f(ins, outs, sems):
        for cp in make(ins, outs, sems):
            cp.start()
    return f


def _wait_all(make):
    def f(ins, outs, sems):
        for cp in make(ins, outs, sems):
            cp.wait()
    return f


def _ffn_gather_hosted(arrs):
    n = len(arrs)

    def make(outs, sems):
        ssem, rsem, fs, fr = sems
        x, y, c = _place()
        j = 2 * x + y

        def reg(w, jj, cc):
            hr = arrs[w].shape[1] // 2
            return outs[w].at[jj, pl.ds(pl.multiple_of(cc * hr, 16), hr), :]

        def rc(w, jj, cc, s_sem, r_sem, dev):
            return pltpu.make_async_remote_copy(src_ref=reg(w, jj, cc), dst_ref=reg(w, jj, cc), send_sem=s_sem,
                                                recv_sem=r_sem, device_id=dev, device_id_type=MESH)

        sends, recvs, fwds, frecvs = [], [], [], []
        for k, (px, py) in enumerate(_other_chips(x, y)):
            jk = 2 * px + py
            for w in range(n):
                q = k * n + w
                sends.append(rc(w, j, c, ssem.at[q], rsem.at[q], (px, py, c)))
                recvs.append(rc(w, jk, c, ssem.at[q], rsem.at[q], (px, py, c)))
                fwds.append(rc(w, jk, c, fs.at[q], fr.at[q], (x, y, 1 - c)))
                frecvs.append(rc(w, jk, 1 - c, fs.at[q], fr.at[q], (x, y, 1 - c)))
        return sends, recvs, fwds, frecvs

    def start(ins, outs, sems):
        for cp in make(outs, sems)[0]:
            cp.start()

    def mid(ins, outs, sems):
        _, recvs, fwds, _ = make(outs, sems)
        for r, f in zip(recvs, fwds):
            r.wait_recv()
            f.start()

    def finish(ins, outs, sems):
        sends, _, fwds, frecvs = make(outs, sems)
        for r in frecvs:
            r.wait_recv()
        for cp in sends + fwds:
            cp.wait_send()

    return _Hosted(arrs, [SDS(a.shape, a.dtype) for a in arrs], [3 * n] * 4, start, finish, mid=mid,
                   aliases={w: w for w in range(n)})


def _rs_sibling_hosted(arrs):
    n = len(arrs)

    def make(ins, outs, sems):
        x, y, c = _place()
        cps = []
        for w in range(n):
            hr = arrs[w].shape[1] // 2
            src = ins[w].at[:, pl.ds(pl.multiple_of((1 - c) * hr, SUBLANES), hr), :]
            cps.append(pltpu.make_async_remote_copy(src_ref=src, dst_ref=outs[w], send_sem=sems[0].at[w],
                                                    recv_sem=sems[1].at[w], device_id=(x, y, 1 - c),
                                                    device_id_type=MESH))
        return cps

    return _Hosted(arrs, [SDS((a.shape[0], a.shape[1] // 2, a.shape[2]), F32) for a in arrs], [n, n],
                   _start_all(make), _wait_all(make))


def _rs_chips_hosted(parts):
    n = len(parts)

    def make(ins, outs, sems):
        x, y, c = _place()
        j = 2 * x + y
        cps = []
        for k, (px, py) in enumerate(_other_chips(x, y)):
            jk = 2 * px + py
            for w in range(n):
                cps.append(pltpu.make_async_remote_copy(
                    src_ref=ins[w].at[jk], dst_ref=outs[w].at[j], send_sem=sems[0].at[k * n + w],
                    recv_sem=sems[1].at[k * n + w], device_id=(px, py, c), device_id_type=MESH))
        return cps

    return _Hosted(parts, [SDS(p.shape, p.dtype) for p in parts], [3 * n, 3 * n], _start_all(make), _wait_all(make))


def _rs_swap_hosted(halves):
    n = len(halves)

    def make(ins, outs, sems):
        x, y, c = _place()
        return [pltpu.make_async_remote_copy(src_ref=ins[w], dst_ref=outs[w], send_sem=sems[0].at[w],
                                             recv_sem=sems[1].at[w], device_id=(x, y, 1 - c), device_id_type=MESH)
                for w in range(n)]

    return _Hosted(halves, [SDS(h.shape, F32) for h in halves], [n, n], _start_all(make), _wait_all(make))


HBM_SPEC = pl.BlockSpec(memory_space=pltpu.HBM)
SEM_SPEC = pl.BlockSpec(memory_space=pltpu.SEMAPHORE)
_EFFECT = pltpu.SideEffectType.DATAFLOW_SIDE_EFFECTING


def _split_start(h, name):
    n_in, n_out, ns = len(h.ins), len(h.out_shapes), len(h.sems)
    ins = [pltpu.with_memory_space_constraint(a, pltpu.HBM) for a in h.ins]
    lands = [pltpu.with_memory_space_constraint(lax.empty(o.shape, o.dtype), pltpu.HBM) for o in h.out_shapes]

    def body(*refs):
        i_refs, l_refs = refs[:n_in], refs[n_in:n_in + n_out]
        s_refs = refs[n_in + n_out:n_in + n_out + ns]
        token = refs[-1]
        h.start(i_refs, l_refs, s_refs)
        token[...] = jnp.zeros_like(token)

    res = pl.pallas_call(
        body, name=name, in_specs=[HBM_SPEC] * (n_in + n_out),
        out_specs=[SEM_SPEC] * ns + [HBM_SPEC] * n_out + [VMEM_SPEC],
        out_shape=[pltpu.SemaphoreType.DMA((k,)) for k in h.sems]
        + [pltpu.HBM(o.shape, o.dtype) for o in h.out_shapes] + [SDS((SUBLANES, LANES), F32)],
        input_output_aliases={n_in + k: ns + k for k in range(n_out)},
        compiler_params=pltpu.CompilerParams(has_side_effects=_EFFECT))(*ins, *lands)
    return list(res[:ns]) + ins + list(res[ns:-1]), res[-1]


def _split_wait(h, state, after, name):
    n_in, n_out, ns = len(h.ins), len(h.out_shapes), len(h.sems)
    sems, bufs = state[:ns], state[ns:]

    def body(*refs):
        i_refs, l_refs = refs[:n_in], refs[n_in:n_in + n_out]
        s_refs = refs[n_in + n_out:n_in + n_out + ns]
        h.finish(i_refs, l_refs, s_refs)

    res = pl.pallas_call(
        body, name=name, in_specs=[HBM_SPEC] * (n_in + n_out) + [SEM_SPEC] * ns + [ANY],
        out_specs=[HBM_SPEC] * n_out,
        out_shape=[pltpu.HBM(b.shape, b.dtype) for b in bufs[n_in:]],
        input_output_aliases={n_in + k: k for k in range(n_out)},
        compiler_params=pltpu.CompilerParams(has_side_effects=_EFFECT))(*bufs, *sems, after)
    return list(res)


def _run_comm(hosted, name):
    return _call(lambda: None, hosted, None, name=name, grid=(), in_specs=[], out_specs=[], out_shape=[],
                 scratch_shapes=[], args=(), sem=None)[1]


def _row_tile(rows, cols, n_arrays):
    budget = 24 * 1024 * 1024 // (2 * 4 * n_arrays * cols)
    best = SUBLANES
    for t in range(SUBLANES, rows + 1, SUBLANES):
        if rows % t == 0 and t <= budget:
            best = t
    return best


def _place_index(which):
    x, y, c = _place()
    v = c if which == "c" else 2 * x + y
    return jnp.reshape(v, (1,)).astype(jnp.int32)


def _add_own_half(full, recv, name, wire=BF16):
    nsh, rows, cols = full.shape
    hr = rows // 2
    t = _row_tile(hr, cols, 4)
    nt = hr // t

    def body(c_ref, a_ref, b_ref, o_ref, ob_ref):
        v = a_ref[...] + b_ref[...]
        o_ref[...] = v
        ob_ref[...] = v.astype(wire)

    half = pl.BlockSpec((1, t, cols), lambda s_, i, c_ref: (s_, i, 0))
    return pl.pallas_call(
        body, name=name,
        grid_spec=pltpu.PrefetchScalarGridSpec(
            num_scalar_prefetch=1, grid=(nsh, nt),
            in_specs=[pl.BlockSpec((1, t, cols), lambda s_, i, c_ref: (s_, c_ref[0] * nt + i, 0)), half],
            out_specs=[half, half]),
        out_shape=[SDS((nsh, hr, cols), F32), SDS((nsh, hr, cols), wire)],
        compiler_params=_cp(("parallel", "parallel")))(_place_index("c"), full, recv)


def _sum_chips(own, recv, name):
    nsh, hr, cols = own.shape
    t = _row_tile(hr, cols, 6)

    def body(j_ref, own_ref, *rest):
        r_refs, o_ref = rest[:nsh], rest[nsh]
        j = j_ref[0]
        mine = own_ref[0]
        parts = [jnp.where(j == k, mine, r_refs[k][0].astype(F32)) for k in range(nsh)]
        o_ref[...] = ((parts[0] + parts[1]) + parts[2]) + parts[3]

    def other(k):
        return pl.BlockSpec((1, t, cols), lambda i, j_ref: (jnp.where(j_ref[0] == k, (k + 1) % nsh, k), i, 0))

    return pl.pallas_call(
        body, name=name,
        grid_spec=pltpu.PrefetchScalarGridSpec(
            num_scalar_prefetch=1, grid=(hr // t,),
            in_specs=[pl.BlockSpec((1, t, cols), lambda i, j_ref: (j_ref[0], i, 0))]
            + [other(k) for k in range(nsh)],
            out_specs=pl.BlockSpec((t, cols), lambda i, j_ref: (i, 0))),
        out_shape=SDS((hr, cols), F32), compiler_params=_cp(("parallel",)))(_place_index("j"), own, *([recv] * nsh))


def _adamw_math(w, g, m, v):
    m = ADAM_B1 * m + (1.0 - ADAM_B1) * g
    v = ADAM_B2 * v + (1.0 - ADAM_B2) * (g * g)
    m_hat = m / (1.0 - ADAM_B1 ** ADAM_STEP)
    v_hat = v / (1.0 - ADAM_B2 ** ADAM_STEP)
    delta = -ADAM_LR * (m_hat / (jnp.sqrt(v_hat) + ADAM_EPS) + ADAM_WD * w)
    return delta, m, v


def _adamw_big(w, g_own, g_sib, m, v, name, token=None):
    _, rows, cols = w.shape
    hr = rows // 2
    t = _row_tile(hr, cols, 9)
    nth = hr // t
    if token is None:
        token = jnp.zeros((SUBLANES, LANES), F32)

    def body(c_ref, w_ref, go_ref, gs_ref, m_ref, v_ref, tok_ref, g_ref, d_ref, mo_ref, vo_ref):
        own = (pl.program_id(0) // nth) == c_ref[0]
        g = jnp.where(own, go_ref[...], gs_ref[...]) + tok_ref[0:1, 0:1]
        g_ref[0] = g
        d_ref[0], mo_ref[0], vo_ref[0] = _adamw_math(w_ref[0], g, m_ref[0], v_ref[0])

    spec = pl.BlockSpec((1, t, cols), lambda i, c_ref: (0, i, 0))
    hspec = pl.BlockSpec((t, cols), lambda i, c_ref: (i % nth, 0))
    tspec = pl.BlockSpec((SUBLANES, LANES), lambda i, c_ref: (0, 0))
    return pl.pallas_call(
        body, name=name,
        grid_spec=pltpu.PrefetchScalarGridSpec(
            num_scalar_prefetch=1, grid=(2 * nth,), in_specs=[spec, hspec, hspec, spec, spec, tspec],
            out_specs=[spec] * 4),
        out_shape=[SDS((1, rows, cols), F32)] * 4,
        compiler_params=_cp(("parallel",)))(_place_index("c"), w, g_own, g_sib, m, v, token)


def _build_slab(mix_slab, dg_mix, dg_ffn, dg_fin, loss8):
    def body(ms_ref, gm_ref, gf_ref, gn_ref, loss_ref, out_ref):
        rows = []
        for ref in (gm_ref, gf_ref, gn_ref):
            v = jnp.sum(ref[...], axis=0, keepdims=True)
            rows += [v[:, :SLAB_W], v[:, SLAB_W:]]
        rows.append(jnp.concatenate([loss_ref[0:1, :]] * (SLAB_W // LANES), axis=1))
        rows.append(jnp.zeros((SLAB_ROWS - ROW_LOSS - 1, SLAB_W), F32))
        tail = jnp.concatenate(rows, axis=0)
        for k in range(N_CHIPS):
            out_ref[k, 0:MIX_SLAB_ROWS, :] = ms_ref[...]
            out_ref[k, MIX_SLAB_ROWS:SLAB_ROWS, :] = tail

    return pl.pallas_call(
        body, name="build_slab", in_specs=[VMEM_SPEC] * 5, out_specs=VMEM_SPEC,
        out_shape=SDS((N_CHIPS, SLAB_ROWS, SLAB_W), F32),
        compiler_params=_cp())(mix_slab, dg_mix, dg_ffn, dg_fin, loss8)


_SMALL_ROWS = (("conv_b", ROW_CONV_B), ("gate_a_b", ROW_BA), ("gate_x_b", ROW_BX), ("lru_lambda", ROW_LAM),
               ("pool_b", ROW_PB), ("pool_scale", ROW_PS), ("norm_lru_g", ROW_GL), ("norm_pool_g", ROW_GP))
_WIDE_ROWS = (("norm_mix_g", ROW_MIX), ("norm_ffn_g", ROW_FFN), ("final_norm_g", ROW_FIN))
_BLOCK_ROWS = (("gate_a_w", ROW_GA), ("gate_x_w", ROW_GX), ("pool_w", ROW_PW))
_SMALL_ORDER = tuple(n for n, _ in _SMALL_ROWS) + tuple(n for n, _ in _WIDE_ROWS) + tuple(
    n for n, _ in _BLOCK_ROWS) + ("conv_w",)


def _adamw_small(slab_own, slab_sib, wmv):
    names = _SMALL_ORDER
    flat = [a for nme in names for a in wmv[nme]]
    nin = len(flat)

    def body(*refs):
        own_ref, sib_ref, j_ref = refs[0], refs[1], refs[2]
        ins = refs[3:3 + nin]
        outs = refs[3 + nin:-1]
        first = j_ref[1] == 0
        slab_ref = jnp.concatenate([jnp.where(first, own_ref[...], sib_ref[...]),
                                    jnp.where(first, sib_ref[...], own_ref[...])], axis=0)
        refs[-1][...] = jnp.broadcast_to(slab_ref[ROW_LOSS:ROW_LOSS + 1, 0:LANES], (SUBLANES, LANES))
        grads = {}
        for nme, row in _SMALL_ROWS:
            grads[nme] = slab_ref[row:row + 1, :]
        for nme, row in _WIDE_ROWS:
            grads[nme] = jnp.concatenate([slab_ref[row:row + 1, :], slab_ref[row + 1:row + 2, :]], axis=1)
        full = slab_ref[ROW_CONV_W:ROW_CONV_W + CONV_WIDTH, :]
        jv = j_ref[0]
        g = jnp.zeros((CONV_WIDTH, LANES), F32)
        for jj in range(N_CHIPS):
            g = jnp.where(jv == jj, full[:, jj * LANES:(jj + 1) * LANES], g)
        grads["conv_w"] = g
        block_rows = dict(_BLOCK_ROWS)
        for idx, nme in enumerate(names):
            w_ref, m_ref, v_ref = ins[3 * idx:3 * idx + 3]
            if nme in block_rows:
                nblk, r, c = w_ref.shape
                parts = [(b, slab_ref[block_rows[nme]:block_rows[nme] + r, b * c:(b + 1) * c]) for b in range(nblk)]
            else:
                parts = [(Ellipsis, grads[nme])]
            for b, g in parts:
                delta, m, v = _adamw_math(w_ref[b], g, m_ref[b], v_ref[b])
                outs[4 * idx][b] = g
                outs[4 * idx + 1][b] = delta
                outs[4 * idx + 2][b] = m
                outs[4 * idx + 3][b] = v

    place = jnp.concatenate([_place_index("j"), _place_index("c")])
    out_shape = [SDS(wmv[nme][0].shape, F32) for nme in names for _ in range(4)] + [SDS((SUBLANES, LANES), F32)]
    res = pl.pallas_call(
        body, name="adamw_small",
        in_specs=[VMEM_SPEC, VMEM_SPEC, pl.BlockSpec(memory_space=pltpu.SMEM)] + [VMEM_SPEC] * nin,
        out_specs=[VMEM_SPEC] * len(out_shape), out_shape=out_shape,
        compiler_params=_cp())(slab_own, slab_sib, place, *flat)
    return {nme: tuple(res[4 * idx:4 * idx + 4]) for idx, nme in enumerate(names)}, res[-1]


_FFN = ("ffn_w1", "ffn_w3", "ffn_w2")
_TRANSPOSED = ("ffn_w1", "ffn_w3")


def _local_step(x, target, full, sp_, distributed):
    d = x.shape[1]
    (u,), got = _inproj(x, sp_["norm_mix_g"], full["w_in"],
                        [_ffn_gather_hosted([full["w_out"]])] if distributed else None)
    w_out = (got[0][0] if distributed else full["w_out"]).reshape(d, d)
    gather = [_ffn_gather_hosted([full[n] for n in _FFN])] if distributed else None
    (h, hres1, saved, pooled), got = _mixer_fwd(u, x, sp_, w_out, gather)
    w1, w3, w2 = got[0] if distributed else [full[n] for n in _FFN]
    h2, a1, a3, ff = _ffn_up(hres1, sp_["norm_ffn_g"], w1, w3)
    dh, dhb, loss8, dg_fin = _ffn_down(ff, hres1, target, sp_["final_norm_g"], w2)
    da1, da3 = _ffn_bwd_gate(dhb, a1, a3, w2)
    dws = list(_ffn_wgrad(h2, dhb, ff, da1, da3))
    rs1 = [_rs_sibling_hosted(dws)] if distributed else None
    (dhres1, dg_ffn), got = _ffn_bwd_down(da1, da3, dh, hres1, sp_["norm_ffn_g"], w1, w3, rs1)
    rs2 = None
    if distributed:
        pairs = [_add_own_half(a, r, "add_half_" + n) for n, a, r in zip(_FFN, dws, got[0])]
        rs2 = [_rs_chips_hosted([pb for _, pb in pairs])]
    (du, mix_slab, dwout), got = _mixer_bwd(u, saved, pooled, h, dhres1, sp_, w_out, rs2)
    g_mix = sp_["norm_mix_g"]
    if distributed:
        fin = [_sum_chips(pairs[k][0], got[0][k], "sum_chips_" + n) for k, n in enumerate(_FFN)]
        swap = _rs_swap_hosted(fin)
        state, token = _split_start(swap, "ffn_swap_start")
        g_mix = g_mix + token[0:1, 0:1]
    (gx, dwin, dg_mix), _ = _inproj_bwd(x, du, dhres1, g_mix, full["w_in"])
    if distributed:
        sib = _split_wait(swap, state, dg_mix, "ffn_swap_wait")
    big = {"w_in": dwin, "w_out": dwout.reshape(N_CHIPS, d // N_CHIPS, d)}
    for k, n in enumerate(_FFN):
        big[n] = (fin[k], sib[k]) if distributed else dws[k]
    return gx, big, (mix_slab, dg_mix, dg_ffn, dg_fin, loss8)


_SMALL_LAYOUT = {
    "gate_a_w": (lambda a: a[0], lambda a: a[None]),
    "gate_x_w": (lambda a: a[0], lambda a: a[None]),
    "pool_w": (lambda a: a[0], lambda a: a[None]),
    "conv_w": (lambda a: a[0], lambda a: a[None]),
    "final_norm_g": (lambda a: a[None], lambda a: a[0]),
}

_WEIGHTS = ("norm_mix_g", "w_in", "conv_w", "conv_b", "gate_a_w", "gate_a_b", "gate_x_w", "gate_x_b", "lru_lambda",
            "pool_w", "pool_b", "pool_scale", "norm_lru_g", "norm_pool_g", "w_out", "norm_ffn_g", "ffn_w1",
            "ffn_w3", "ffn_w2", "final_norm_g")


def kernel(x, norm_mix_g, w_in, conv_w, conv_b, gate_a_w, gate_a_b, gate_x_w, gate_x_b, lru_lambda, pool_w, pool_b, pool_scale, norm_lru_g, norm_pool_g, w_out, norm_ffn_g, ffn_w1, ffn_w3, ffn_w2, final_norm_g, loss_target, m_norm_mix_g, m_w_in, m_conv_w, m_conv_b, m_gate_a_w, m_gate_a_b, m_gate_x_w, m_gate_x_b, m_lru_lambda, m_pool_w, m_pool_b, m_pool_scale, m_norm_lru_g, m_norm_pool_g, m_w_out, m_norm_ffn_g, m_ffn_w1, m_ffn_w3, m_ffn_w2, m_final_norm_g, v_norm_mix_g, v_w_in, v_conv_w, v_conv_b, v_gate_a_w, v_gate_a_b, v_gate_x_w, v_gate_x_b, v_lru_lambda, v_pool_w, v_pool_b, v_pool_scale, v_norm_lru_g, v_norm_pool_g, v_w_out, v_norm_ffn_g, v_ffn_w1, v_ffn_w3, v_ffn_w2, v_final_norm_g):
    loc = locals()
    w = {n: loc[n] for n in _WEIGHTS}
    m = {n: loc["m_" + n] for n in _WEIGHTS}
    v = {n: loc["v_" + n] for n in _WEIGHTS}

    def lay(nme, a):
        return _SMALL_LAYOUT[nme][0](a) if nme in _SMALL_LAYOUT else a

    def unlay(nme, a):
        return _SMALL_LAYOUT[nme][1](a) if nme in _SMALL_LAYOUT else a

    for group in (w, m, v):
        for n in _TRANSPOSED:
            group[n] = jnp.transpose(group[n], (0, 2, 1))

    gathered = _gather_weights([w[n][0] for n in _BIG], w["conv_w"][0], n_remote=1)
    full = dict(zip(_BIG, gathered[:-1]))
    cw_all = gathered[-1]
    sp_ = {n: lay(n, w[n]) for n in _SMALL_ORDER}
    sp_["conv_w"] = jnp.transpose(cw_all[:, :CONV_WIDTH, :], (1, 0, 2)).reshape(CONV_WIDTH, N_CHIPS * LANES)

    gx, big, small = _local_step(x[0], loss_target[0], full, sp_, distributed=True)

    late = ("w_in", "w_out", "slab")
    big["slab"] = _build_slab(*small)
    fin = {n: big[n][0] for n in _FFN}
    sib = {n: big[n][1] for n in _FFN}
    recv1, = _run_comm([_rs_sibling_hosted([big[n] for n in late])], "tail_sibling")
    pairs = [_add_own_half(big[n], r, "add_half_" + n, F32 if n == "slab" else BF16) for n, r in zip(late, recv1)]
    chips = _rs_chips_hosted([pb for _, pb in pairs])
    state, token = _split_start(chips, "tail_chips_start")
    out = {}
    for n in _FFN:
        out[n] = tuple(_adamw_big(w[n], fin[n], sib[n], m[n], v[n], "adamw_" + n, token))
    recv2 = _split_wait(chips, state, out[_FFN[-1]][1], "tail_chips_wait")
    for n, (p, _), r in zip(late, pairs, recv2):
        fin[n] = _sum_chips(p, r, "sum_chips_" + n)
    swapped, = _run_comm([_rs_swap_hosted([fin[n] for n in late])], "tail_swap")
    sib.update(zip(late, swapped))
    for n in late[:2]:
        out[n] = tuple(_adamw_big(w[n], fin[n], sib[n], m[n], v[n], "adamw_" + n))
    for n in _TRANSPOSED:
        out[n] = tuple(jnp.transpose(a, (0, 2, 1)) for a in out[n])
    wmv = {n: (lay(n, w[n]), lay(n, m[n]), lay(n, v[n])) for n in _SMALL_ORDER}
    res, loss = _adamw_small(fin["slab"], sib["slab"], wmv)
    for n in _SMALL_ORDER:
        out[n] = tuple(unlay(n, a) for a in res[n])
    return (loss[0, 0], gx[None]) + tuple(out[n][k] for k in range(4) for n in _WEIGHTS)
```

```python
import functools
import math

import jax
import jax.numpy as jnp
from jax import lax
from jax.experimental import pallas as pl
from jax.experimental.pallas import tpu as pltpu

F32 = jnp.float32
BF16 = jnp.bfloat16
SDS = jax.ShapeDtypeStruct
MESH = pl.DeviceIdType.MESH

EPS = 1e-6
LRU_C = 8.0
CONV_WIDTH = 4
POOL_WINDOWS = (2, 4, 8, 16)
HALO = 16
LANES = 128
SUBLANES = 8
GATE_BLOCK = 256
N_CHIPS = 4

ADAM_LR = 0.001
ADAM_B1 = 0.9
ADAM_B2 = 0.999
ADAM_EPS = 1e-08
ADAM_WD = 0.01
ADAM_STEP = 10

TM_PROJ = 512
TM_MIX = 512
TM_FFN = 512
TM_WGRAD = 2048
MIX_SAVED = ("xc", "a", "m2raw", "ge")
MIX_SAVED_BF16 = ("r", "ig", "dge")
FFN_ROW_CHUNKS = 2
VMEM_LIMIT = 56 * 1024 * 1024

SLAB_W = 512
ROW_CONV_B, ROW_CONV_W, ROW_BA, ROW_BX, ROW_LAM, ROW_PB, ROW_PS, ROW_GL, ROW_GP = 0, 1, 5, 6, 7, 8, 9, 10, 11
ROW_GA, ROW_GX, ROW_PW = 16, 80, 144
ROW_MIX, ROW_FFN, ROW_FIN, ROW_LOSS = 272, 274, 276, 278
MIX_SLAB_ROWS = 272
SLAB_ROWS = 288


def _cp(sem=None, **kw):
    if sem is not None:
        kw["dimension_semantics"] = sem
    return pltpu.CompilerParams(vmem_limit_bytes=VMEM_LIMIT, **kw)


def _const_spec(shape):
    nd = len(shape)
    return pl.BlockSpec(shape, lambda *_: (0,) * nd, pipeline_mode=pl.Buffered(1))


def _sigmoid(x):
    return 1.0 / (1.0 + jnp.exp(-x))


def _dot(a, b):
    return jnp.dot(a, b, preferred_element_type=F32)


def _dot_nt(a, b):
    return lax.dot_general(a, b, (((1,), (1,)), ((), ())), preferred_element_type=F32)


def _dot_tn(a, b):
    return lax.dot_general(a, b, (((0,), (0,)), ((), ())), preferred_element_type=F32)


def _colsum8(v):
    m, c = v.shape
    return v.reshape(m // SUBLANES, SUBLANES, c).sum(axis=0)


def _rowmean(v):
    return jnp.mean(v, axis=-1, keepdims=True)


def _rms_bwd(dy, xhat, r, g):
    dxh = dy * g
    return r * (dxh - xhat * _rowmean(dxh * xhat))


def _softplus_neg(lam):
    z = -lam
    e = jnp.exp(-jnp.abs(z))
    u = 1.0 + e
    d = u - 1.0
    log1p = jnp.where(d == 0.0, e, jnp.log(u) * (e / jnp.where(d == 0.0, 1.0, d)))
    return jnp.maximum(z, 0.0) + log1p


def _neg_expm1(z):
    series = -(z * (1.0 + z * (0.5 + z * (1.0 / 6.0 + z * (1.0 / 24.0)))))
    return jnp.where(z > -0.03, series, 1.0 - jnp.exp(z))


_GELU_C = math.sqrt(2.0 / math.pi)
_GELU_K = 0.044715


def _gelu_parts(x):
    x2 = x * x
    th = jnp.tanh(_GELU_C * (x + _GELU_K * x2 * x))
    ge = 0.5 * x * (1.0 + th)
    dge = 0.5 * (1.0 + th) + 0.5 * x * (1.0 - th * th) * (_GELU_C * (1.0 + 3.0 * _GELU_K * x2))
    return ge, dge


def _shift_down(halo, tile, k):
    if k == 0:
        return tile
    ext = jnp.concatenate([halo, tile], axis=0)
    h = halo.shape[0]
    return pltpu.roll(ext, k, 0)[h:]


def _shift_up(tile, nxt, k):
    if k == 0:
        return tile
    ext = jnp.concatenate([tile, nxt], axis=0)
    return pltpu.roll(ext, ext.shape[0] - k, 0)[:tile.shape[0]]


def _build_gate_blocks(ga_ref, gx_ref, gw_ref):
    hd = ga_ref.shape[1]
    per = GATE_BLOCK // hd
    zero = jnp.zeros((hd, hd), F32)
    for b in range(gw_ref.shape[0]):
        for src, off in ((ga_ref, 0), (gx_ref, GATE_BLOCK)):
            for hh in range(per):
                row = jnp.concatenate([zero] * hh + [src[b * per + hh]] + [zero] * (per - 1 - hh), axis=1)
                gw_ref[b, hh * hd:(hh + 1) * hd, off:off + GATE_BLOCK] = row.astype(BF16)


def _scan_level1(a, b, reverse):
    m, c = a.shape
    a3 = a.reshape(m // SUBLANES, SUBLANES, c)
    b3 = b.reshape(m // SUBLANES, SUBLANES, c)
    row = lax.broadcasted_iota(jnp.int32, a3.shape, 1)
    for s in (1, 2, 4):
        sh = (SUBLANES - s) if reverse else s
        a_sh = pltpu.roll(a3, sh, 1)
        b_sh = pltpu.roll(b3, sh, 1)
        ok = (row < SUBLANES - s) if reverse else (row >= s)
        b3 = jnp.where(ok, a3 * b_sh + b3, b3)
        a3 = jnp.where(ok, a3 * a_sh, a3)
    return a3.reshape(m, c), b3.reshape(m, c)


def _scan_level2(a_ref, b_ref, out_ref, carry, reverse):
    m, c = a_ref.shape
    ng = m // SUBLANES

    def step(g, cr):
        gi = (ng - 1 - g) if reverse else g
        off = pl.multiple_of(gi * SUBLANES, SUBLANES)
        h = b_ref[pl.ds(off, SUBLANES), :] + a_ref[pl.ds(off, SUBLANES), :] * cr
        out_ref[pl.ds(off, SUBLANES), :] = h
        edge = h[0:1, :] if reverse else h[SUBLANES - 1:SUBLANES, :]
        return jnp.broadcast_to(edge, (SUBLANES, c))

    return lax.fori_loop(0, ng, step, carry, unroll=4)


def _mixer_recompute(u_ref, hal, t0, cw, cb, gw_ref, ba, bx, lam, pw_ref, pb, ps, saved_ref, pooled_ref):
    tm = u_ref.shape[0]
    lw = cb.shape[1]
    keep = {name: k for k, name in enumerate(MIX_SAVED)}
    keep16 = {name: k for k, name in enumerate(MIX_SAVED_BF16)}
    hal_l, hal_p = hal[:, :lw], hal[:, 2 * lw:]
    xc = cb
    for k in range(CONV_WIDTH):
        xc = xc + _shift_down(hal_l, u_ref[:, :lw], CONV_WIDTH - 1 - k) * cw[k:k + 1, :]
    saved_ref[keep["xc"]] = xc
    xcb = xc.astype(BF16)
    nb = lw // GATE_BLOCK
    gs = [_dot(xcb[:, b * GATE_BLOCK:(b + 1) * GATE_BLOCK], gw_ref[b]) for b in range(nb)]
    r = _sigmoid(jnp.concatenate([g[:, :GATE_BLOCK] for g in gs], axis=1) + ba)
    pooled_ref[keep16["r"]] = r.astype(BF16)
    ig = _sigmoid(jnp.concatenate([g[:, GATE_BLOCK:] for g in gs], axis=1) + bx)
    pooled_ref[keep16["ig"]] = ig.astype(BF16)
    la = (-LRU_C * r) * _softplus_neg(lam)
    a = jnp.exp(la)
    saved_ref[keep["a"]] = a
    m2raw = _neg_expm1(2.0 * la)
    saved_ref[keep["m2raw"]] = m2raw
    bb = jnp.sqrt(jnp.maximum(m2raw, 1e-12)) * (ig * saved_ref[keep["xc"]])
    ge, dge = _gelu_parts(u_ref[:, lw:2 * lw])
    saved_ref[keep["ge"]] = ge
    pooled_ref[keep16["dge"]] = dge.astype(BF16)
    row = lax.broadcasted_iota(jnp.int32, (HALO, LANES), 0) + t0
    zs = []
    for gi, w in enumerate(POOL_WINDOWS):
        sl = slice(gi * LANES, (gi + 1) * LANES)
        e = jnp.concatenate([hal_p[:, sl], u_ref[:, 2 * lw + gi * LANES:2 * lw + (gi + 1) * LANES]], axis=0)
        s = e
        k = 1
        while k < w:
            s = s + pltpu.roll(s, k, 0)
            k *= 2
        mean = jnp.concatenate([s[HALO:2 * HALO] * (1.0 / jnp.minimum(row + 1, w).astype(F32)),
                                s[2 * HALO:] * (1.0 / w)], axis=0)
        pg = (mean - e[HALO:]).astype(BF16)
        pooled_ref[len(MIX_SAVED_BF16), :, sl] = pg
        zs.append(_dot(pg, pw_ref[gi].astype(BF16)))
    y_pool = (jnp.concatenate(zs, axis=1) + pb) * ps
    return a, bb, y_pool


ANY = pl.BlockSpec(memory_space=pl.ANY)
VMEM_SPEC = pl.BlockSpec(memory_space=pltpu.VMEM)


class _Hosted:
    def __init__(self, ins, out_shapes, sems, start, finish, mid=None, aliases=None):
        self.ins, self.out_shapes, self.sems = list(ins), list(out_shapes), list(sems)
        self.start, self.mid, self.finish = start, mid, finish
        self.aliases = dict(aliases or {})


def _call(body, hosted, stage_preds, *, name, grid, in_specs, out_specs, out_shape, scratch_shapes, args, sem):
    hosted = list(hosted or [])
    n_in, n_out, n_scr = len(in_specs), len(out_specs), len(scratch_shapes)
    c_in = [a for h in hosted for a in h.ins]
    c_out = [o for h in hosted for o in h.out_shapes]
    c_sem = [pltpu.SemaphoreType.DMA((k,)) for h in hosted for k in h.sems]

    def full(*refs):
        p = 0
        parts = []
        for cnt in (n_in, len(c_in), n_out, len(c_out), n_scr, len(c_sem)):
            parts.append(refs[p:p + cnt])
            p += cnt
        hi, ci, ho, co, hs, cs = parts
        per = []
        a = b = c_ = 0
        for h in hosted:
            per.append((h, ci[a:a + len(h.ins)], co[b:b + len(h.out_shapes)], cs[c_:c_ + len(h.sems)]))
            a, b, c_ = a + len(h.ins), b + len(h.out_shapes), c_ + len(h.sems)
        first = mid = last = None
        if hosted and grid:
            first, mid, last = stage_preds()

        def run(fn, pred, i_, o_, s_):
            if fn is None:
                return
            if pred is None:
                fn(i_, o_, s_)
            else:
                pl.when(pred)(functools.partial(fn, i_, o_, s_))

        for h, i_, o_, s_ in per:
            run(h.start, first, i_, o_, s_)
        body(*hi, *ho, *hs)
        for h, i_, o_, s_ in per:
            run(h.mid, mid, i_, o_, s_)
        for h, i_, o_, s_ in per:
            run(h.finish, last, i_, o_, s_)

    aliases = {}
    a = b = 0
    for h in hosted:
        for k, v in h.aliases.items():
            aliases[n_in + a + k] = n_out + b + v
        a, b = a + len(h.ins), b + len(h.out_shapes)
    res = pl.pallas_call(
        full, name=name, grid=grid, in_specs=list(in_specs) + [ANY] * len(c_in),
        out_specs=list(out_specs) + [ANY] * len(c_out), out_shape=list(out_shape) + c_out,
        scratch_shapes=list(scratch_shapes) + c_sem, input_output_aliases=aliases,
        compiler_params=_cp(sem))(*args, *c_in)
    res = list(res)
    outs = []
    p = n_out
    for h in hosted:
        outs.append(res[p:p + len(h.out_shapes)])
        p += len(h.out_shapes)
    return res[:n_out], outs


def _inproj(x, g_mix, w_in, hosted=None):
    s, d = x.shape
    n = w_in.shape[1]
    tm = min(TM_PROJ, s)
    nt = s // tm

    def body(x_ref, g_ref, w_ref, u_ref):
        xv = x_ref[...]
        r = lax.rsqrt(_rowmean(xv * xv) + EPS)
        u_ref[...] = _dot((xv * r * g_ref[...]).astype(BF16), w_ref[...])

    def stages():
        i = pl.program_id(0)
        return i == 0, i == max(nt - 3, 0), i == nt - 1

    return _call(
        body, hosted, stages, grid=(nt,), name="inproj",
        in_specs=[pl.BlockSpec((tm, d), lambda i: (i, 0)), _const_spec((1, d)), _const_spec((d, n))],
        out_specs=[pl.BlockSpec((tm, n), lambda i: (i, 0))], out_shape=[SDS((s, n), F32)], scratch_shapes=[],
        args=(x, g_mix, w_in), sem=("arbitrary",))


def _mixer_fwd(u, x, sp_, w_out, hosted=None):
    s, din = u.shape
    d = x.shape[1]
    lw = din // 3
    tm = min(TM_MIX, s)
    nb = lw // GATE_BLOCK

    def body(u_ref, halo_ref, x_ref, cw_ref, cb_ref, ga_ref, gx_ref, ba_ref, bx_ref, lam_ref, pw_ref, pb_ref,
             ps_ref, gl_ref, gp_ref, wout_ref, h_ref, hres_ref, saved_ref, pooled_ref,
             gw_s, a_s, b_s, carry_s):
        i = pl.program_id(0)

        @pl.when(i == 0)
        def _():
            _build_gate_blocks(ga_ref, gx_ref, gw_s)
            carry_s[...] = jnp.zeros_like(carry_s)

        hal = jnp.where(i > 0, halo_ref[...], 0.0)
        a, bb, yp = _mixer_recompute(u_ref, hal, i * tm, cw_ref[...], cb_ref[...], gw_s, ba_ref[...], bx_ref[...],
                                     lam_ref[...], pw_ref, pb_ref[...], ps_ref[...], saved_ref, pooled_ref)
        a1, b1 = _scan_level1(a, bb, reverse=False)
        a_s[...] = a1
        b_s[...] = b1
        carry_s[...] = _scan_level2(a_s, b_s, h_ref, carry_s[...], reverse=False)
        y_lru = h_ref[...] * saved_ref[MIX_SAVED.index("ge")]
        rl = lax.rsqrt(_rowmean(y_lru * y_lru) + EPS)
        rp = lax.rsqrt(_rowmean(yp * yp) + EPS)
        yn = jnp.concatenate([y_lru * rl * gl_ref[...], yp * rp * gp_ref[...]], axis=1).astype(BF16)
        hres_ref[...] = x_ref[...] + _dot(yn, wout_ref[...])

    small = [sp_[k] for k in ("conv_w", "conv_b", "gate_a_w", "gate_x_w", "gate_a_b", "gate_x_b", "lru_lambda",
                              "pool_w", "pool_b", "pool_scale", "norm_lru_g", "norm_pool_g")]
    nt = s // tm

    def stages():
        i = pl.program_id(0)
        return i == 0, i == max(nt - 3, 0), i == nt - 1

    return _call(
        body, hosted, stages, grid=(nt,), name="mixer_fwd",
        in_specs=[pl.BlockSpec((tm, din), lambda i: (i, 0)),
                  pl.BlockSpec((HALO, din), lambda i: (jnp.maximum(i * (tm // HALO) - 1, 0), 0)),
                  pl.BlockSpec((tm, d), lambda i: (i, 0))]
        + [_const_spec(a.shape) for a in small] + [_const_spec(w_out.shape)],
        out_specs=[pl.BlockSpec((tm, lw), lambda i: (i, 0)), pl.BlockSpec((tm, d), lambda i: (i, 0)),
                   pl.BlockSpec((len(MIX_SAVED), tm, lw), lambda i: (0, i, 0)),
                   pl.BlockSpec((len(MIX_SAVED_BF16) + 1, tm, lw), lambda i: (0, i, 0))],
        out_shape=[SDS((s, lw), F32), SDS((s, d), F32), SDS((len(MIX_SAVED), s, lw), F32),
                   SDS((len(MIX_SAVED_BF16) + 1, s, lw), BF16)],
        scratch_shapes=[pltpu.VMEM((nb, GATE_BLOCK, 2 * GATE_BLOCK), BF16), pltpu.VMEM((tm, lw), F32),
                        pltpu.VMEM((tm, lw), F32), pltpu.VMEM((SUBLANES, lw), F32)],
        args=(u, u, x, *small, w_out), sem=("arbitrary",))


def _row_chunks(tm):
    rc = tm // FFN_ROW_CHUNKS
    return [slice(q * rc, (q + 1) * rc) for q in range(FFN_ROW_CHUNKS)]


def _ffn_up(hres1, g_ffn, w1, w3):
    s, d = hres1.shape
    nj, fc, _ = w1.shape
    tm = min(TM_FFN, s)

    def body(h_ref, gf_ref, w1_ref, w3_ref, h2_ref, a1_ref, a3_ref, ff_ref):
        hv = h_ref[...]
        r = lax.rsqrt(_rowmean(hv * hv) + EPS)
        h2_ref[...] = (hv * r * gf_ref[...]).astype(BF16)
        h2 = h2_ref[...]
        for j in range(nj):
            a1 = _dot_nt(h2, w1_ref[j])
            a3 = _dot_nt(h2, w3_ref[j])
            a1_ref[j] = a1.astype(BF16)
            a3_ref[j] = a3.astype(BF16)
            ff_ref[j] = ((a1 * _sigmoid(a1)) * a3).astype(BF16)

    wspec = _const_spec(w1.shape)
    aspec = pl.BlockSpec((nj, tm, fc), lambda i: (0, i, 0))
    return pl.pallas_call(
        body, grid=(s // tm,), name="ffn_up",
        in_specs=[pl.BlockSpec((tm, d), lambda i: (i, 0)), _const_spec((1, d)), wspec, wspec],
        out_specs=[pl.BlockSpec((tm, d), lambda i: (i, 0)), aspec, aspec, aspec],
        out_shape=[SDS((s, d), BF16)] + [SDS((nj, s, fc), BF16)] * 3,
        compiler_params=_cp(("parallel",)))(hres1, g_ffn, w1, w3)


def _ffn_down(ff, hres1, target, g_fin, w2):
    s, d = hres1.shape
    nj, _, fc = ff.shape
    tm = min(TM_FFN, s)

    def body(ff_ref, h_ref, t_ref, gn_ref, w2_ref, dh_ref, dhb_ref, loss_ref, dgn_ref):
        @pl.when(pl.program_id(0) == 0)
        def _():
            loss_ref[...] = jnp.zeros_like(loss_ref)
            dgn_ref[...] = jnp.zeros_like(dgn_ref)

        gn = gn_ref[...]
        for rows in _row_chunks(tm):
            acc = _dot(ff_ref[0, rows, :], w2_ref[0])
            for j in range(1, nj):
                acc = acc + _dot(ff_ref[j, rows, :], w2_ref[j])
            hr2 = h_ref[rows, :] + acc
            r2 = lax.rsqrt(_rowmean(hr2 * hr2) + EPS)
            xh = hr2 * r2
            diff = xh * gn - t_ref[rows, :]
            tot = jnp.sum(jnp.sum(diff * diff, axis=1, keepdims=True), axis=0, keepdims=True)
            loss_ref[...] += tot * (0.5 / d)
            dout = diff * (1.0 / d)
            dgn_ref[...] += _colsum8(dout * xh)
            dh = _rms_bwd(dout, xh, r2, gn)
            dh_ref[rows, :] = dh
            dhb_ref[rows, :] = dh.astype(BF16)

    tile = pl.BlockSpec((tm, d), lambda i: (i, 0))
    return pl.pallas_call(
        body, grid=(s // tm,), name="ffn_down",
        in_specs=[pl.BlockSpec((nj, tm, fc), lambda i: (0, i, 0)), tile, tile, _const_spec((1, d)),
                  _const_spec(w2.shape)],
        out_specs=[tile, tile, pl.BlockSpec((SUBLANES, LANES), lambda i: (0, 0)),
                   pl.BlockSpec((SUBLANES, d), lambda i: (0, 0))],
        out_shape=[SDS((s, d), F32), SDS((s, d), BF16), SDS((SUBLANES, LANES), F32), SDS((SUBLANES, d), F32)],
        compiler_params=_cp(("arbitrary",)))(ff, hres1, target, g_fin, w2)


def _ffn_bwd_gate(dhb, a1, a3, w2):
    s, d = dhb.shape
    nj, _, fc = a1.shape
    tm = min(TM_FFN, s)

    def body(dhb_ref, a1_ref, a3_ref, w2_ref, da1_ref, da3_ref):
        for j in range(nj):
            for rows in _row_chunks(tm):
                dff = _dot_nt(dhb_ref[rows, :], w2_ref[j])
                a1v = a1_ref[j, rows, :].astype(F32)
                sg = _sigmoid(a1v)
                silu = a1v * sg
                da1_ref[j, rows, :] = (dff * a3_ref[j, rows, :].astype(F32)
                                       * (sg * (1.0 + (a1v - silu)))).astype(BF16)
                da3_ref[j, rows, :] = (dff * silu).astype(BF16)

    aspec = pl.BlockSpec((nj, tm, fc), lambda i: (0, i, 0))
    return pl.pallas_call(
        body, grid=(s // tm,), name="ffn_bwd_gate",
        in_specs=[pl.BlockSpec((tm, d), lambda i: (i, 0)), aspec, aspec, _const_spec(w2.shape)],
        out_specs=[aspec, aspec], out_shape=[SDS((nj, s, fc), BF16)] * 2,
        compiler_params=_cp(("parallel",)))(dhb, a1, a3, w2)


def _ffn_bwd_down(da1, da3, dh, hres1, g_ffn, w1, w3, hosted=None):
    s, d = hres1.shape
    nj, _, fc = da1.shape
    tm = min(TM_FFN, s)
    nt = s // tm

    def body(da1_ref, da3_ref, dh_ref, h_ref, gf_ref, w1_ref, w3_ref, dhr_ref, dgf_ref):
        @pl.when(pl.program_id(0) == 0)
        def _():
            dgf_ref[...] = jnp.zeros_like(dgf_ref)

        gf = gf_ref[...]
        for rows in _row_chunks(tm):
            dh2 = None
            for j in range(nj):
                part = _dot(da1_ref[j, rows, :], w1_ref[j]) + _dot(da3_ref[j, rows, :], w3_ref[j])
                dh2 = part if dh2 is None else dh2 + part
            hv = h_ref[rows, :]
            r = lax.rsqrt(_rowmean(hv * hv) + EPS)
            xh = hv * r
            dgf_ref[...] += _colsum8(dh2 * xh)
            dhr_ref[rows, :] = dh_ref[rows, :] + _rms_bwd(dh2, xh, r, gf)

    tile = pl.BlockSpec((tm, d), lambda i: (i, 0))
    aspec = pl.BlockSpec((nj, tm, fc), lambda i: (0, i, 0))
    wspec = _const_spec(w1.shape)

    def stages():
        i = pl.program_id(0)
        return i == 0, i == max(nt - 2, 0), i == nt - 1

    return _call(
        body, hosted, stages, grid=(nt,), name="ffn_bwd_down",
        in_specs=[aspec, aspec, tile, tile, _const_spec((1, d)), wspec, wspec],
        out_specs=[tile, pl.BlockSpec((SUBLANES, d), lambda i: (0, 0))],
        out_shape=[SDS((s, d), F32), SDS((SUBLANES, d), F32)],
        scratch_shapes=[], args=(da1, da3, dh, hres1, g_ffn, w1, w3), sem=("arbitrary",))


def _ffn_wgrad(h2, dhb, ff, da1, da3):
    s, d = h2.shape
    _, _, fc = ff.shape
    tm = min(TM_WGRAD, s)

    def body(h2_ref, dhb_ref, ff_ref, da1_ref, da3_ref, dw1_ref, dw3_ref, dw2_ref):
        @pl.when(pl.program_id(1) == 0)
        def _():
            dw1_ref[...] = jnp.zeros_like(dw1_ref)
            dw3_ref[...] = jnp.zeros_like(dw3_ref)
            dw2_ref[...] = jnp.zeros_like(dw2_ref)

        h2v = h2_ref[...]
        dw1_ref[0] += _dot_tn(da1_ref[0], h2v)
        dw3_ref[0] += _dot_tn(da3_ref[0], h2v)
        dw2_ref[0] += _dot_tn(ff_ref[0], dhb_ref[...])

    wspec = pl.BlockSpec((1, fc, d), lambda j, i: (j, 0, 0))
    return pl.pallas_call(
        body, grid=(N_CHIPS, s // tm), name="ffn_wgrad",
        in_specs=[pl.BlockSpec((tm, d), lambda j, i: (i, 0)), pl.BlockSpec((tm, d), lambda j, i: (i, 0))]
        + [pl.BlockSpec((1, tm, fc), lambda j, i: (j, i, 0))] * 3,
        out_specs=[wspec] * 3, out_shape=[SDS((N_CHIPS, fc, d), F32)] * 3,
        compiler_params=_cp(("parallel", "arbitrary")))(h2, dhb, ff, da1, da3)


def _mixer_bwd(u, saved, pooled, h, dhres1, sp_, w_out, hosted=None):
    s, din = u.shape
    d = dhres1.shape[1]
    lw = din // 3
    tm = min(TM_MIX, s)
    nt = s // tm
    nb = lw // GATE_BLOCK
    hd = sp_["gate_a_w"].shape[1]

    def body(ul_ref, saved_ref, pooled_ref, h_ref, hhalo_ref, dhr_ref, cw_ref, cb_ref, ga_ref, gx_ref, ba_ref,
             bx_ref, lam_ref, pw_ref, pb_ref, ps_ref, gl_ref, gp_ref, wout_ref, du_ref, slab_ref, dwout_ref,
             gw_s, a_s, b_s, e_s, ecarry_s, dxc_s, q_s, vec_s, cwacc_s, dgw_s, dpw_s):
        i = pl.program_id(0)
        tile = nt - 1 - i

        @pl.when(i == 0)
        def _():
            _build_gate_blocks(ga_ref, gx_ref, gw_s)
            for ref in (ecarry_s, dxc_s, q_s, vec_s, cwacc_s, dgw_s, dpw_s, dwout_ref):
                ref[...] = jnp.zeros_like(ref)

        cw = cw_ref[...]
        lam = lam_ref[...]
        ps = ps_ref[...]
        def sv_(name):
            if name in MIX_SAVED_BF16:
                return pooled_ref[MIX_SAVED_BF16.index(name)].astype(F32)
            return saved_ref[MIX_SAVED.index(name)]

        npool = len(MIX_SAVED_BF16)

        f = {"sp": _softplus_neg(lam)}
        row = lax.broadcasted_iota(jnp.int32, (HALO, LANES), 0) + tile * tm
        f["z"] = jnp.concatenate(
            [_dot(pooled_ref[npool, :, g * LANES:(g + 1) * LANES], pw_ref[g].astype(BF16))
             for g in range(len(POOL_WINDOWS))], axis=1) + pb_ref[...]
        f["y_pool"] = f["z"] * ps
        y_lru = h_ref[...] * sv_("ge")
        rl = lax.rsqrt(_rowmean(y_lru * y_lru) + EPS)
        yp = f["y_pool"]
        rp = lax.rsqrt(_rowmean(yp * yp) + EPS)
        xh_l = y_lru * rl
        xh_p = yp * rp

        dhrb = dhr_ref[...].astype(BF16)
        dyn = _dot_nt(dhrb, wout_ref[...])
        yn = jnp.concatenate([xh_l * gl_ref[...], xh_p * gp_ref[...]], axis=1).astype(BF16)
        dwout_ref[...] += _dot_tn(yn, dhrb)
        d_nl, d_np = dyn[:, :lw], dyn[:, lw:]
        vec = {}
        vec[ROW_GL] = _colsum8(d_nl * xh_l)
        vec[ROW_GP] = _colsum8(d_np * xh_p)
        d_ylru = _rms_bwd(d_nl, xh_l, rl, gl_ref[...])
        d_ypool = _rms_bwd(d_np, xh_p, rp, gp_ref[...])

        vec[ROW_PS] = _colsum8(d_ypool * f["z"])
        dz = d_ypool * ps
        vec[ROW_PB] = _colsum8(dz)
        dzb = dz.astype(BF16)
        dup = []
        for gi, w in enumerate(POOL_WINDOWS):
            sl = slice(gi * LANES, (gi + 1) * LANES)
            dpw_s[:, sl] += _dot_tn(pooled_ref[npool, :, sl], dzb[:, sl])
            dpool = _dot_nt(dzb[:, sl], pw_ref[gi].astype(BF16))
            q = jnp.concatenate([dpool[:HALO] * (1.0 / jnp.minimum(row + 1, w).astype(F32)),
                                 dpool[HALO:] * (1.0 / w)], axis=0)
            e = jnp.concatenate([q, q_s[:, sl]], axis=0)
            k = 1
            while k < w:
                e = e + pltpu.roll(e, tm + HALO - k, 0)
                k *= 2
            dup.append(e[:tm] - dpool)
            q_s[:, sl] = q[:HALO]

        d_hout = d_ylru * sv_("ge")
        d_ug = d_ylru * h_ref[...] * sv_("dge")
        a1, b1 = _scan_level1(sv_("a"), sv_("a") * d_hout, reverse=True)
        a_s[...] = a1
        b_s[...] = b1
        e_next = ecarry_s[...]
        ecarry_s[...] = _scan_level2(a_s, b_s, e_s, e_next, reverse=True)
        sv = d_hout + _shift_up(e_s[...], e_next, 1)
        d_a = sv * _shift_down(jnp.where(tile > 0, hhalo_ref[...], 0.0), h_ref[...], 1)
        mult = jnp.sqrt(jnp.maximum(sv_("m2raw"), 1e-12))
        ig = sv_("ig")
        d_mult = sv * (ig * sv_("xc"))
        d_ig = sv * mult * sv_("xc")
        d_xc = sv * mult * ig
        a = sv_("a")
        d_la = d_a * a + jnp.where(sv_("m2raw") > 1e-12, d_mult * (-(a * a) / mult), 0.0)
        d_r = d_la * (-LRU_C * f["sp"])
        r = sv_("r")
        vec[ROW_LAM] = _colsum8(d_la * (-LRU_C * r))
        d_pr = d_r * r * (1.0 - r)
        d_pi = d_ig * ig * (1.0 - ig)
        vec[ROW_BA] = _colsum8(d_pr)
        vec[ROW_BX] = _colsum8(d_pi)
        dxc_parts = []
        for b in range(nb):
            sl = slice(b * GATE_BLOCK, (b + 1) * GATE_BLOCK)
            rhs = jnp.concatenate([d_pr[:, sl], d_pi[:, sl]], axis=1).astype(BF16)
            dgw_s[b] += _dot_tn(saved_ref[MIX_SAVED.index("xc"), :, sl].astype(BF16), rhs)
            dxc_parts.append(_dot_nt(rhs, gw_s[b]))
        d_xc = d_xc + jnp.concatenate(dxc_parts, axis=1)
        vec[ROW_CONV_B] = _colsum8(d_xc)
        dxc_next = dxc_s[...]
        d_ul = None
        for k in range(CONV_WIDTH):
            ahead = _shift_up(d_xc, dxc_next, CONV_WIDTH - 1 - k)
            cwacc_s[k * SUBLANES:(k + 1) * SUBLANES, :] += _colsum8(ahead * ul_ref[...])
            term = ahead * cw[k:k + 1, :]
            d_ul = term if d_ul is None else d_ul + term
        dxc_s[...] = d_xc[:SUBLANES]
        for row, val in vec.items():
            vec_s[row * SUBLANES:(row + 1) * SUBLANES, :] += val
        du_ref[...] = jnp.concatenate([d_ul, d_ug] + dup, axis=1).astype(BF16)

        @pl.when(i == nt - 1)
        def _():
            rows = []
            for row in range(ROW_GA):
                if row in (ROW_CONV_W, ROW_CONV_W + 1, ROW_CONV_W + 2, ROW_CONV_W + 3):
                    k = row - ROW_CONV_W
                    v = jnp.sum(cwacc_s[k * SUBLANES:(k + 1) * SUBLANES, :], axis=0, keepdims=True)
                elif row <= ROW_GP:
                    v = jnp.sum(vec_s[row * SUBLANES:(row + 1) * SUBLANES, :], axis=0, keepdims=True)
                    if row == ROW_LAM:
                        v = v * (-1.0 / (1.0 + jnp.exp(lam)))
                else:
                    v = jnp.zeros((1, lw), F32)
                rows.append(v)
            slab_ref[0:ROW_GA, :] = jnp.concatenate(rows, axis=0)
            lane = lax.broadcasted_iota(jnp.int32, (hd, GATE_BLOCK), 1)
            for b in range(nb):
                for off, row0 in ((0, ROW_GA), (GATE_BLOCK, ROW_GX)):
                    acc = jnp.zeros((hd, GATE_BLOCK), F32)
                    for hh in range(GATE_BLOCK // hd):
                        m = (lane >= hh * hd) & (lane < (hh + 1) * hd)
                        acc = acc + jnp.where(m, dgw_s[b, hh * hd:(hh + 1) * hd, off:off + GATE_BLOCK], 0.0)
                    slab_ref[row0:row0 + hd, b * GATE_BLOCK:(b + 1) * GATE_BLOCK] = acc
            slab_ref[ROW_PW:ROW_PW + LANES, :] = dpw_s[...]

    small = [sp_[k] for k in ("conv_w", "conv_b", "gate_a_w", "gate_x_w", "gate_a_b", "gate_x_b", "lru_lambda",
                              "pool_w", "pool_b", "pool_scale", "norm_lru_g", "norm_pool_g")]
    rev = lambda i: nt - 1 - i

    def stages():
        i = pl.program_id(0)
        return i == 0, i == max(nt - 3, 0), i == nt - 1

    return _call(
        body, hosted, stages, grid=(nt,), name="mixer_bwd",
        in_specs=[pl.BlockSpec((tm, lw), lambda i: (rev(i), 0)),
                  pl.BlockSpec((len(MIX_SAVED), tm, lw), lambda i: (0, rev(i), 0)),
                  pl.BlockSpec((len(MIX_SAVED_BF16) + 1, tm, lw), lambda i: (0, rev(i), 0)),
                  pl.BlockSpec((tm, lw), lambda i: (rev(i), 0)),
                  pl.BlockSpec((SUBLANES, lw), lambda i: (jnp.maximum(rev(i) * (tm // SUBLANES) - 1, 0), 0)),
                  pl.BlockSpec((tm, d), lambda i: (rev(i), 0))]
        + [_const_spec(a.shape) for a in small] + [_const_spec(w_out.shape)],
        out_specs=[pl.BlockSpec((tm, din), lambda i: (rev(i), 0)),
                   pl.BlockSpec((MIX_SLAB_ROWS, SLAB_W), lambda i: (0, 0)), pl.BlockSpec((d, d), lambda i: (0, 0))],
        out_shape=[SDS((s, din), BF16), SDS((MIX_SLAB_ROWS, SLAB_W), F32), SDS((d, d), F32)],
        scratch_shapes=[pltpu.VMEM((nb, GATE_BLOCK, 2 * GATE_BLOCK), BF16),
                        pltpu.VMEM((tm, lw), F32), pltpu.VMEM((tm, lw), F32), pltpu.VMEM((tm, lw), F32),
                        pltpu.VMEM((SUBLANES, lw), F32), pltpu.VMEM((SUBLANES, lw), F32),
                        pltpu.VMEM((HALO, lw), F32), pltpu.VMEM((ROW_GA * SUBLANES, lw), F32),
                        pltpu.VMEM((CONV_WIDTH * SUBLANES, lw), F32),
                        pltpu.VMEM((nb, GATE_BLOCK, 2 * GATE_BLOCK), F32), pltpu.VMEM((LANES, lw), F32)],
        args=(u, saved, pooled, h, h, dhres1, *small, w_out), sem=("arbitrary",))


def _inproj_bwd(x, du, dhres1, g_mix, w_in, hosted=None):
    s, d = x.shape
    n = w_in.shape[1]
    nc = n // N_CHIPS
    tm = min(TM_PROJ, s)
    nt = s // tm

    def body(x_ref, du_ref, dhr_ref, g_ref, w_ref, gx_ref, dwin_ref, dg_ref):
        i = pl.program_id(0)

        @pl.when(i == 0)
        def _():
            dwin_ref[...] = jnp.zeros_like(dwin_ref)
            dg_ref[...] = jnp.zeros_like(dg_ref)

        xv = x_ref[...]
        g = g_ref[...]
        r = lax.rsqrt(_rowmean(xv * xv) + EPS)
        xh = xv * r
        h1 = (xh * g).astype(BF16)
        duv = du_ref[...]
        dh1 = _dot_nt(duv, w_ref[...])
        dg_ref[...] += _colsum8(dh1 * xh)
        gx_ref[...] = dhr_ref[...] + _rms_bwd(dh1, xh, r, g)
        for jj in range(N_CHIPS):
            dwin_ref[jj] += _dot_tn(h1, duv[:, jj * nc:(jj + 1) * nc])

    def stages():
        i = pl.program_id(0)
        return i == 0, i == max(nt - 3, 0), i == nt - 1

    return _call(
        body, hosted, stages, grid=(nt,), name="inproj_bwd",
        in_specs=[pl.BlockSpec((tm, d), lambda i: (i, 0)), pl.BlockSpec((tm, n), lambda i: (i, 0)),
                  pl.BlockSpec((tm, d), lambda i: (i, 0)), _const_spec((1, d)), _const_spec((d, n))],
        out_specs=[pl.BlockSpec((tm, d), lambda i: (i, 0)), pl.BlockSpec((N_CHIPS, d, nc), lambda i: (0, 0, 0)),
                   pl.BlockSpec((SUBLANES, d), lambda i: (0, 0))],
        out_shape=[SDS((s, d), F32), SDS((N_CHIPS, d, nc), F32), SDS((SUBLANES, d), F32)],
        scratch_shapes=[], args=(x, du, dhres1, g_mix, w_in), sem=("arbitrary",))


def _place():
    x, y, c = lax.axis_index("x"), lax.axis_index("y"), lax.axis_index("c")
    return x, y, c


def _other_chips(x, y):
    return [(1 - x, y), (x, 1 - y), (1 - x, 1 - y)]


ANY = pl.BlockSpec(memory_space=pl.ANY)
VMEM_SPEC = pl.BlockSpec(memory_space=pltpu.VMEM)

_GATHERED = {"w_in": "cols", "w_out": "major", "ffn_w1": "major", "ffn_w3": "major", "ffn_w2": "major"}
_BIG = ("w_in", "w_out", "ffn_w1", "ffn_w3", "ffn_w2")


def _gather_weights(shards, conv_w, n_remote):
    n = len(shards)
    full_shapes = []
    for name, sh in zip(_BIG, shards):
        r, cdim = sh.shape
        if _GATHERED[name] == "cols":
            assert cdim % LANES == 0
            full_shapes.append((r, cdim * N_CHIPS))
        else:
            full_shapes.append((N_CHIPS, r, cdim))

    def region(ref, name, sh, jj, cc):
        r, cdim = sh
        rows = pl.ds(0, r) if cc is None else pl.ds(pl.multiple_of(cc * (r // 2), 16), r // 2)
        if _GATHERED[name] == "cols":
            return ref.at[rows, pl.ds(pl.multiple_of(jj * cdim, LANES), cdim)]
        return ref.at[jj, rows, :]

    def staged(ref, sh, cc):
        r = sh[0]
        return ref.at[pl.ds(pl.multiple_of(cc * (r // 2), 16), r // 2), :]

    def body(*refs):
        ins, cw_in = refs[:n], refs[n]
        outs, cw_out = refs[n + 1:2 * n + 1], refs[2 * n + 1]
        stage = refs[2 * n + 2:3 * n + 2]
        cw_stage, lsem, ssem, rsem, fssem, frsem, cssem, crsem = refs[3 * n + 2:]
        x, y, c = _place()
        j = 2 * x + y
        chips = _other_chips(x, y)
        for w in range(n_remote):
            stage[w][...] = ins[w][...].astype(BF16)
        cw_stage[...] = jnp.zeros_like(cw_stage)
        cw_stage[0:CONV_WIDTH, :] = cw_in[...]
        shs = [s_.shape for s_ in shards]
        local = [pltpu.make_async_copy(stage[w], region(outs[w], _BIG[w], shs[w], j, None), lsem.at[w])
                 for w in range(n)]
        local.append(pltpu.make_async_copy(cw_stage, cw_out.at[j], lsem.at[n]))
        sends = []
        for k, (px, py) in enumerate(chips):
            for w in range(n_remote):
                sends.append(pltpu.make_async_remote_copy(
                    src_ref=staged(stage[w], shs[w], c), dst_ref=region(outs[w], _BIG[w], shs[w], j, c),
                    send_sem=ssem.at[k * n + w], recv_sem=rsem.at[k * n + w], device_id=(px, py, c),
                    device_id_type=MESH))
            sends.append(pltpu.make_async_remote_copy(
                src_ref=cw_stage, dst_ref=cw_out.at[j], send_sem=cssem.at[k], recv_sem=crsem.at[k],
                device_id=(px, py, c), device_id_type=MESH))
        for cp in sends:
            cp.start()
        for w in range(n_remote, n):
            stage[w][...] = ins[w][...].astype(BF16)
        for cp in local:
            cp.start()
        fwd = []
        for k, (px, py) in enumerate(chips):
            jk = 2 * px + py
            for w in range(n_remote):
                reg = region(outs[w], _BIG[w], shs[w], jk, c)
                pltpu.make_async_remote_copy(src_ref=reg, dst_ref=reg, send_sem=ssem.at[k * n + w],
                                             recv_sem=rsem.at[k * n + w], device_id=(px, py, c),
                                             device_id_type=MESH).wait_recv()
                cp = pltpu.make_async_remote_copy(src_ref=reg, dst_ref=reg, send_sem=fssem.at[k * n + w],
                                                  recv_sem=frsem.at[k * n + w], device_id=(x, y, 1 - c),
                                                  device_id_type=MESH)
                cp.start()
                fwd.append(cp)
            pltpu.make_async_remote_copy(src_ref=cw_stage, dst_ref=cw_out.at[jk], send_sem=cssem.at[k],
                                         recv_sem=crsem.at[k], device_id=(px, py, c),
                                         device_id_type=MESH).wait_recv()
        for k, (px, py) in enumerate(chips):
            jk = 2 * px + py
            for w in range(n_remote):
                reg = region(outs[w], _BIG[w], shs[w], jk, 1 - c)
                pltpu.make_async_remote_copy(src_ref=reg, dst_ref=reg, send_sem=fssem.at[k * n + w],
                                             recv_sem=frsem.at[k * n + w], device_id=(x, y, 1 - c),
                                             device_id_type=MESH).wait_recv()
        for cp in sends + fwd:
            cp.wait_send()
        for cp in local:
            cp.wait()

    nsem = 3 * n
    return pl.pallas_call(
        body, name="gather_first",
        in_specs=[VMEM_SPEC] * (n + 1), out_specs=[ANY] * (n + 1),
        out_shape=[SDS(fs, BF16) for fs in full_shapes] + [SDS((N_CHIPS, SUBLANES, LANES), F32)],
        scratch_shapes=[pltpu.VMEM(s_.shape, BF16) for s_ in shards] + [pltpu.VMEM((SUBLANES, LANES), F32)]
        + [pltpu.SemaphoreType.DMA((n + 1,))] + [pltpu.SemaphoreType.DMA((nsem,))] * 4
        + [pltpu.SemaphoreType.DMA((3,))] * 2,
        compiler_params=_cp())(*shards, conv_w)


def _start_all(make):
    def f(ins, outs, sems):
        for cp in make(ins, outs, sems):
            cp.start()
    return f


def _wait_all(make):
    def f(ins, outs, sems):
        for cp in make(ins, outs, sems):
            cp.wait()
    return f


def _ffn_gather_hosted(arrs):
    n = len(arrs)

    def make(outs, sems):
        ssem, rsem, fs, fr = sems
        x, y, c = _place()
        j = 2 * x + y

        def reg(w, jj, cc):
            hr = arrs[w].shape[1] // 2
            return outs[w].at[jj, pl.ds(pl.multiple_of(cc * hr, 16), hr), :]

        def rc(w, jj, cc, s_sem, r_sem, dev):
            return pltpu.make_async_remote_copy(src_ref=reg(w, jj, cc), dst_ref=reg(w, jj, cc), send_sem=s_sem,
                                                recv_sem=r_sem, device_id=dev, device_id_type=MESH)

        sends, recvs, fwds, frecvs = [], [], [], []
        for k, (px, py) in enumerate(_other_chips(x, y)):
            jk = 2 * px + py
            for w in range(n):
                q = k * n + w
                sends.append(rc(w, j, c, ssem.at[q], rsem.at[q], (px, py, c)))
                recvs.append(rc(w, jk, c, ssem.at[q], rsem.at[q], (px, py, c)))
                fwds.append(rc(w, jk, c, fs.at[q], fr.at[q], (x, y, 1 - c)))
                frecvs.append(rc(w, jk, 1 - c, fs.at[q], fr.at[q], (x, y, 1 - c)))
        return sends, recvs, fwds, frecvs

    def start(ins, outs, sems):
        for cp in make(outs, sems)[0]:
            cp.start()

    def mid(ins, outs, sems):
        _, recvs, fwds, _ = make(outs, sems)
        for r, f in zip(recvs, fwds):
            r.wait_recv()
            f.start()

    def finish(ins, outs, sems):
        sends, _, fwds, frecvs = make(outs, sems)
        for r in frecvs:
            r.wait_recv()
        for cp in sends + fwds:
            cp.wait_send()

    return _Hosted(arrs, [SDS(a.shape, a.dtype) for a in arrs], [3 * n] * 4, start, finish, mid=mid,
                   aliases={w: w for w in range(n)})


def _rs_sibling_hosted(arrs):
    n = len(arrs)

    def make(ins, outs, sems):
        x, y, c = _place()
        cps = []
        for w in range(n):
            hr = arrs[w].shape[1] // 2
            src = ins[w].at[:, pl.ds(pl.multiple_of((1 - c) * hr, SUBLANES), hr), :]
            cps.append(pltpu.make_async_remote_copy(src_ref=src, dst_ref=outs[w], send_sem=sems[0].at[w],
                                                    recv_sem=sems[1].at[w], device_id=(x, y, 1 - c),
                                                    device_id_type=MESH))
        return cps

    return _Hosted(arrs, [SDS((a.shape[0], a.shape[1] // 2, a.shape[2]), F32) for a in arrs], [n, n],
                   _start_all(make), _wait_all(make))


def _rs_chips_hosted(parts):
    n = len(parts)

    def make(ins, outs, sems):
        x, y, c = _place()
        j = 2 * x + y
        cps = []
        for k, (px, py) in enumerate(_other_chips(x, y)):
            jk = 2 * px + py
            for w in range(n):
                cps.append(pltpu.make_async_remote_copy(
                    src_ref=ins[w].at[jk], dst_ref=outs[w].at[j], send_sem=sems[0].at[k * n + w],
                    recv_sem=sems[1].at[k * n + w], device_id=(px, py, c), device_id_type=MESH))
        return cps

    return _Hosted(parts, [SDS(p.shape, p.dtype) for p in parts], [3 * n, 3 * n], _start_all(make), _wait_all(make))


def _rs_swap_hosted(halves):
    n = len(halves)

    def make(ins, outs, sems):
        x, y, c = _place()
        return [pltpu.make_async_remote_copy(src_ref=ins[w], dst_ref=outs[w], send_sem=sems[0].at[w],
                                             recv_sem=sems[1].at[w], device_id=(x, y, 1 - c), device_id_type=MESH)
                for w in range(n)]

    return _Hosted(halves, [SDS(h.shape, F32) for h in halves], [n, n], _start_all(make), _wait_all(make))


HBM_SPEC = pl.BlockSpec(memory_space=pltpu.HBM)
SEM_SPEC = pl.BlockSpec(memory_space=pltpu.SEMAPHORE)
_EFFECT = pltpu.SideEffectType.DATAFLOW_SIDE_EFFECTING


def _split_start(h, name):
    n_in, n_out, ns = len(h.ins), len(h.out_shapes), len(h.sems)
    ins = [pltpu.with_memory_space_constraint(a, pltpu.HBM) for a in h.ins]
    lands = [pltpu.with_memory_space_constraint(lax.empty(o.shape, o.dtype), pltpu.HBM) for o in h.out_shapes]

    def body(*refs):
        i_refs, l_refs = refs[:n_in], refs[n_in:n_in + n_out]
        s_refs = refs[n_in + n_out:n_in + n_out + ns]
        token = refs[-1]
        h.start(i_refs, l_refs, s_refs)
        token[...] = jnp.zeros_like(token)

    res = pl.pallas_call(
        body, name=name, in_specs=[HBM_SPEC] * (n_in + n_out),
        out_specs=[SEM_SPEC] * ns + [HBM_SPEC] * n_out + [VMEM_SPEC],
        out_shape=[pltpu.SemaphoreType.DMA((k,)) for k in h.sems]
        + [pltpu.HBM(o.shape, o.dtype) for o in h.out_shapes] + [SDS((SUBLANES, LANES), F32)],
        input_output_aliases={n_in + k: ns + k for k in range(n_out)},
        compiler_params=pltpu.CompilerParams(has_side_effects=_EFFECT))(*ins, *lands)
    return list(res[:ns]) + ins + list(res[ns:-1]), res[-1]


def _split_wait(h, state, after, name):
    n_in, n_out, ns = len(h.ins), len(h.out_shapes), len(h.sems)
    sems, bufs = state[:ns], state[ns:]

    def body(*refs):
        i_refs, l_refs = refs[:n_in], refs[n_in:n_in + n_out]
        s_refs = refs[n_in + n_out:n_in + n_out + ns]
        h.finish(i_refs, l_refs, s_refs)

    res = pl.pallas_call(
        body, name=name, in_specs=[HBM_SPEC] * (n_in + n_out) + [SEM_SPEC] * ns + [ANY],
        out_specs=[HBM_SPEC] * n_out,
        out_shape=[pltpu.HBM(b.shape, b.dtype) for b in bufs[n_in:]],
        input_output_aliases={n_in + k: k for k in range(n_out)},
        compiler_params=pltpu.CompilerParams(has_side_effects=_EFFECT))(*bufs, *sems, after)
    return list(res)


def _run_comm(hosted, name):
    return _call(lambda: None, hosted, None, name=name, grid=(), in_specs=[], out_specs=[], out_shape=[],
                 scratch_shapes=[], args=(), sem=None)[1]


def _row_tile(rows, cols, n_arrays):
    budget = 24 * 1024 * 1024 // (2 * 4 * n_arrays * cols)
    best = SUBLANES
    for t in range(SUBLANES, rows + 1, SUBLANES):
        if rows % t == 0 and t <= budget:
            best = t
    return best


def _place_index(which):
    x, y, c = _place()
    v = c if which == "c" else 2 * x + y
    return jnp.reshape(v, (1,)).astype(jnp.int32)


def _add_own_half(full, recv, name, wire=BF16):
    nsh, rows, cols = full.shape
    hr = rows // 2
    t = _row_tile(hr, cols, 4)
    nt = hr // t

    def body(c_ref, a_ref, b_ref, o_ref, ob_ref):
        v = a_ref[...] + b_ref[...]
        o_ref[...] = v
        ob_ref[...] = v.astype(wire)

    half = pl.BlockSpec((1, t, cols), lambda s_, i, c_ref: (s_, i, 0))
    return pl.pallas_call(
        body, name=name,
        grid_spec=pltpu.PrefetchScalarGridSpec(
            num_scalar_prefetch=1, grid=(nsh, nt),
            in_specs=[pl.BlockSpec((1, t, cols), lambda s_, i, c_ref: (s_, c_ref[0] * nt + i, 0)), half],
            out_specs=[half, half]),
        out_shape=[SDS((nsh, hr, cols), F32), SDS((nsh, hr, cols), wire)],
        compiler_params=_cp(("parallel", "parallel")))(_place_index("c"), full, recv)


def _sum_chips(own, recv, name):
    nsh, hr, cols = own.shape
    t = _row_tile(hr, cols, 6)

    def body(j_ref, own_ref, *rest):
        r_refs, o_ref = rest[:nsh], rest[nsh]
        j = j_ref[0]
        mine = own_ref[0]
        parts = [jnp.where(j == k, mine, r_refs[k][0].astype(F32)) for k in range(nsh)]
        o_ref[...] = ((parts[0] + parts[1]) + parts[2]) + parts[3]

    def other(k):
        return pl.BlockSpec((1, t, cols), lambda i, j_ref: (jnp.where(j_ref[0] == k, (k + 1) % nsh, k), i, 0))

    return pl.pallas_call(
        body, name=name,
        grid_spec=pltpu.PrefetchScalarGridSpec(
            num_scalar_prefetch=1, grid=(hr // t,),
            in_specs=[pl.BlockSpec((1, t, cols), lambda i, j_ref: (j_ref[0], i, 0))]
            + [other(k) for k in range(nsh)],
            out_specs=pl.BlockSpec((t, cols), lambda i, j_ref: (i, 0))),
        out_shape=SDS((hr, cols), F32), compiler_params=_cp(("parallel",)))(_place_index("j"), own, *([recv] * nsh))


def _adamw_math(w, g, m, v):
    m = ADAM_B1 * m + (1.0 - ADAM_B1) * g
    v = ADAM_B2 * v + (1.0 - ADAM_B2) * (g * g)
    m_hat = m / (1.0 - ADAM_B1 ** ADAM_STEP)
    v_hat = v / (1.0 - ADAM_B2 ** ADAM_STEP)
    delta = -ADAM_LR * (m_hat / (jnp.sqrt(v_hat) + ADAM_EPS) + ADAM_WD * w)
    return delta, m, v


def _adamw_big(w, g_own, g_sib, m, v, name, token=None):
    _, rows, cols = w.shape
    hr = rows // 2
    t = _row_tile(hr, cols, 9)
    nth = hr // t
    if token is None:
        token = jnp.zeros((SUBLANES, LANES), F32)

    def body(c_ref, w_ref, go_ref, gs_ref, m_ref, v_ref, tok_ref, g_ref, d_ref, mo_ref, vo_ref):
        own = (pl.program_id(0) // nth) == c_ref[0]
        g = jnp.where(own, go_ref[...], gs_ref[...]) + tok_ref[0:1, 0:1]
        g_ref[0] = g
        d_ref[0], mo_ref[0], vo_ref[0] = _adamw_math(w_ref[0], g, m_ref[0], v_ref[0])

    spec = pl.BlockSpec((1, t, cols), lambda i, c_ref: (0, i, 0))
    hspec = pl.BlockSpec((t, cols), lambda i, c_ref: (i % nth, 0))
    tspec = pl.BlockSpec((SUBLANES, LANES), lambda i, c_ref: (0, 0))
    return pl.pallas_call(
        body, name=name,
        grid_spec=pltpu.PrefetchScalarGridSpec(
            num_scalar_prefetch=1, grid=(2 * nth,), in_specs=[spec, hspec, hspec, spec, spec, tspec],
            out_specs=[spec] * 4),
        out_shape=[SDS((1, rows, cols), F32)] * 4,
        compiler_params=_cp(("parallel",)))(_place_index("c"), w, g_own, g_sib, m, v, token)


def _build_slab(mix_slab, dg_mix, dg_ffn, dg_fin, loss8):
    def body(ms_ref, gm_ref, gf_ref, gn_ref, loss_ref, out_ref):
        rows = []
        for ref in (gm_ref, gf_ref, gn_ref):
            v = jnp.sum(ref[...], axis=0, keepdims=True)
            rows += [v[:, :SLAB_W], v[:, SLAB_W:]]
        rows.append(jnp.concatenate([loss_ref[0:1, :]] * (SLAB_W // LANES), axis=1))
        rows.append(jnp.zeros((SLAB_ROWS - ROW_LOSS - 1, SLAB_W), F32))
        tail = jnp.concatenate(rows, axis=0)
        for k in range(N_CHIPS):
            out_ref[k, 0:MIX_SLAB_ROWS, :] = ms_ref[...]
            out_ref[k, MIX_SLAB_ROWS:SLAB_ROWS, :] = tail

    return pl.pallas_call(
        body, name="build_slab", in_specs=[VMEM_SPEC] * 5, out_specs=VMEM_SPEC,
        out_shape=SDS((N_CHIPS, SLAB_ROWS, SLAB_W), F32),
        compiler_params=_cp())(mix_slab, dg_mix, dg_ffn, dg_fin, loss8)


_SMALL_ROWS = (("conv_b", ROW_CONV_B), ("gate_a_b", ROW_BA), ("gate_x_b", ROW_BX), ("lru_lambda", ROW_LAM),
               ("pool_b", ROW_PB), ("pool_scale", ROW_PS), ("norm_lru_g", ROW_GL), ("norm_pool_g", ROW_GP))
_WIDE_ROWS = (("norm_mix_g", ROW_MIX), ("norm_ffn_g", ROW_FFN), ("final_norm_g", ROW_FIN))
_BLOCK_ROWS = (("gate_a_w", ROW_GA), ("gate_x_w", ROW_GX), ("pool_w", ROW_PW))
_SMALL_ORDER = tuple(n for n, _ in _SMALL_ROWS) + tuple(n for n, _ in _WIDE_ROWS) + tuple(
    n for n, _ in _BLOCK_ROWS) + ("conv_w",)


def _adamw_small(slab_own, slab_sib, wmv):
    names = _SMALL_ORDER
    flat = [a for nme in names for a in wmv[nme]]
    nin = len(flat)

    def body(*refs):
        own_ref, sib_ref, j_ref = refs[0], refs[1], refs[2]
        ins = refs[3:3 + nin]
        outs = refs[3 + nin:-1]
        first = j_ref[1] == 0
        slab_ref = jnp.concatenate([jnp.where(first, own_ref[...], sib_ref[...]),
                                    jnp.where(first, sib_ref[...], own_ref[...])], axis=0)
        refs[-1][...] = jnp.broadcast_to(slab_ref[ROW_LOSS:ROW_LOSS + 1, 0:LANES], (SUBLANES, LANES))
        grads = {}
        for nme, row in _SMALL_ROWS:
            grads[nme] = slab_ref[row:row + 1, :]
        for nme, row in _WIDE_ROWS:
            grads[nme] = jnp.concatenate([slab_ref[row:row + 1, :], slab_ref[row + 1:row + 2, :]], axis=1)
        full = slab_ref[ROW_CONV_W:ROW_CONV_W + CONV_WIDTH, :]
        jv = j_ref[0]
        g = jnp.zeros((CONV_WIDTH, LANES), F32)
        for jj in range(N_CHIPS):
            g = jnp.where(jv == jj, full[:, jj * LANES:(jj + 1) * LANES], g)
        grads["conv_w"] = g
        block_rows = dict(_BLOCK_ROWS)
        for idx, nme in enumerate(names):
            w_ref, m_ref, v_ref = ins[3 * idx:3 * idx + 3]
            if nme in block_rows:
                nblk, r, c = w_ref.shape
                parts = [(b, slab_ref[block_rows[nme]:block_rows[nme] + r, b * c:(b + 1) * c]) for b in range(nblk)]
            else:
                parts = [(Ellipsis, grads[nme])]
            for b, g in parts:
                delta, m, v = _adamw_math(w_ref[b], g, m_ref[b], v_ref[b])
                outs[4 * idx][b] = g
                outs[4 * idx + 1][b] = delta
                outs[4 * idx + 2][b] = m
                outs[4 * idx + 3][b] = v

    place = jnp.concatenate([_place_index("j"), _place_index("c")])
    out_shape = [SDS(wmv[nme][0].shape, F32) for nme in names for _ in range(4)] + [SDS((SUBLANES, LANES), F32)]
    res = pl.pallas_call(
        body, name="adamw_small",
        in_specs=[VMEM_SPEC, VMEM_SPEC, pl.BlockSpec(memory_space=pltpu.SMEM)] + [VMEM_SPEC] * nin,
        out_specs=[VMEM_SPEC] * len(out_shape), out_shape=out_shape,
        compiler_params=_cp())(slab_own, slab_sib, place, *flat)
    return {nme: tuple(res[4 * idx:4 * idx + 4]) for idx, nme in enumerate(names)}, res[-1]


_FFN = ("ffn_w1", "ffn_w3", "ffn_w2")
_TRANSPOSED = ("ffn_w1", "ffn_w3")


def _local_step(x, target, full, sp_, distributed):
    d = x.shape[1]
    (u,), got = _inproj(x, sp_["norm_mix_g"], full["w_in"],
                        [_ffn_gather_hosted([full["w_out"]])] if distributed else None)
    w_out = (got[0][0] if distributed else full["w_out"]).reshape(d, d)
    gather = [_ffn_gather_hosted([full[n] for n in _FFN])] if distributed else None
    (h, hres1, saved, pooled), got = _mixer_fwd(u, x, sp_, w_out, gather)
    w1, w3, w2 = got[0] if distributed else [full[n] for n in _FFN]
    h2, a1, a3, ff = _ffn_up(hres1, sp_["norm_ffn_g"], w1, w3)
    dh, dhb, loss8, dg_fin = _ffn_down(ff, hres1, target, sp_["final_norm_g"], w2)
    da1, da3 = _ffn_bwd_gate(dhb, a1, a3, w2)
    dws = list(_ffn_wgrad(h2, dhb, ff, da1, da3))
    rs1 = [_rs_sibling_hosted(dws)] if distributed else None
    (dhres1, dg_ffn), got = _ffn_bwd_down(da1, da3, dh, hres1, sp_["norm_ffn_g"], w1, w3, rs1)
    rs2 = None
    if distributed:
        pairs = [_add_own_half(a, r, "add_half_" + n) for n, a, r in zip(_FFN, dws, got[0])]
        rs2 = [_rs_chips_hosted([pb for _, pb in pairs])]
    (du, mix_slab, dwout), got = _mixer_bwd(u, saved, pooled, h, dhres1, sp_, w_out, rs2)
    g_mix = sp_["norm_mix_g"]
    if distributed:
        fin = [_sum_chips(pairs[k][0], got[0][k], "sum_chips_" + n) for k, n in enumerate(_FFN)]
        swap = _rs_swap_hosted(fin)
        state, token = _split_start(swap, "ffn_swap_start")
        g_mix = g_mix + token[0:1, 0:1]
    (gx, dwin, dg_mix), _ = _inproj_bwd(x, du, dhres1, g_mix, full["w_in"])
    if distributed:
        sib = _split_wait(swap, state, dg_mix, "ffn_swap_wait")
    big = {"w_in": dwin, "w_out": dwout.reshape(N_CHIPS, d // N_CHIPS, d)}
    for k, n in enumerate(_FFN):
        big[n] = (fin[k], sib[k]) if distributed else dws[k]
    return gx, big, (mix_slab, dg_mix, dg_ffn, dg_fin, loss8)


_SMALL_LAYOUT = {
    "gate_a_w": (lambda a: a[0], lambda a: a[None]),
    "gate_x_w": (lambda a: a[0], lambda a: a[None]),
    "pool_w": (lambda a: a[0], lambda a: a[None]),
    "conv_w": (lambda a: a[0], lambda a: a[None]),
    "final_norm_g": (lambda a: a[None], lambda a: a[0]),
}

_WEIGHTS = ("norm_mix_g", "w_in", "conv_w", "conv_b", "gate_a_w", "gate_a_b", "gate_x_w", "gate_x_b", "lru_lambda",
            "pool_w", "pool_b", "pool_scale", "norm_lru_g", "norm_pool_g", "w_out", "norm_ffn_g", "ffn_w1",
            "ffn_w3", "ffn_w2", "final_norm_g")


def kernel(x, norm_mix_g, w_in, conv_w, conv_b, gate_a_w, gate_a_b, gate_x_w, gate_x_b, lru_lambda, pool_w, pool_b, pool_scale, norm_lru_g, norm_pool_g, w_out, norm_ffn_g, ffn_w1, ffn_w3, ffn_w2, final_norm_g, loss_target, m_norm_mix_g, m_w_in, m_conv_w, m_conv_b, m_gate_a_w, m_gate_a_b, m_gate_x_w, m_gate_x_b, m_lru_lambda, m_pool_w, m_pool_b, m_pool_scale, m_norm_lru_g, m_norm_pool_g, m_w_out, m_norm_ffn_g, m_ffn_w1, m_ffn_w3, m_ffn_w2, m_final_norm_g, v_norm_mix_g, v_w_in, v_conv_w, v_conv_b, v_gate_a_w, v_gate_a_b, v_gate_x_w, v_gate_x_b, v_lru_lambda, v_pool_w, v_pool_b, v_pool_scale, v_norm_lru_g, v_norm_pool_g, v_w_out, v_norm_ffn_g, v_ffn_w1, v_ffn_w3, v_ffn_w2, v_final_norm_g):
    loc = locals()
    w = {n: loc[n] for n in _WEIGHTS}
    m = {n: loc["m_" + n] for n in _WEIGHTS}
    v = {n: loc["v_" + n] for n in _WEIGHTS}

    def lay(nme, a):
        return _SMALL_LAYOUT[nme][0](a) if nme in _SMALL_LAYOUT else a

    def unlay(nme, a):
        return _SMALL_LAYOUT[nme][1](a) if nme in _SMALL_LAYOUT else a

    for group in (w, m, v):
        for n in _TRANSPOSED:
            group[n] = jnp.transpose(group[n], (0, 2, 1))

    gathered = _gather_weights([w[n][0] for n in _BIG], w["conv_w"][0], n_remote=1)
    full = dict(zip(_BIG, gathered[:-1]))
    cw_all = gathered[-1]
    sp_ = {n: lay(n, w[n]) for n in _SMALL_ORDER}
    sp_["conv_w"] = jnp.transpose(cw_all[:, :CONV_WIDTH, :], (1, 0, 2)).reshape(CONV_WIDTH, N_CHIPS * LANES)

    gx, big, small = _local_step(x[0], loss_target[0], full, sp_, distributed=True)

    late = ("w_in", "w_out", "slab")
    big["slab"] = _build_slab(*small)
    fin = {n: big[n][0] for n in _FFN}
    sib = {n: big[n][1] for n in _FFN}
    recv1, = _run_comm([_rs_sibling_hosted([big[n] for n in late])], "tail_sibling")
    pairs = [_add_own_half(big[n], r, "add_half_" + n, F32 if n == "slab" else BF16) for n, r in zip(late, recv1)]
    chips = _rs_chips_hosted([pb for _, pb in pairs])
    state, token = _split_start(chips, "tail_chips_start")
    out = {}
    for n in _FFN:
        out[n] = tuple(_adamw_big(w[n], fin[n], sib[n], m[n], v[n], "adamw_" + n, token))
    recv2 = _split_wait(chips, state, out[_FFN[-1]][1], "tail_chips_wait")
    for n, (p, _), r in zip(late, pairs, recv2):
        fin[n] = _sum_chips(p, r, "sum_chips_" + n)
    swapped, = _run_comm([_rs_swap_hosted([fin[n] for n in late])], "tail_swap")
    sib.update(zip(late, swapped))
    for n in late[:2]:
        out[n] = tuple(_adamw_big(w[n], fin[n], sib[n], m[n], v[n], "adamw_" + n))
    for n in _TRANSPOSED:
        out[n] = tuple(jnp.transpose(a, (0, 2, 1)) for a in out[n])
    wmv = {n: (lay(n, w[n]), lay(n, m[n]), lay(n, v[n])) for n in _SMALL_ORDER}
    res, loss = _adamw_small(fin["slab"], sib["slab"], wmv)
    for n in _SMALL_ORDER:
        out[n] = tuple(unlay(n, a) for a in res[n])
    return (loss[0, 0], gx[None]) + tuple(out[n][k] for k in range(4) for n in _WEIGHTS)
```

```python
import functools
import math

import jax
import jax.numpy as jnp
from jax import lax
from jax.experimental import pallas as pl
from jax.experimental.pallas import tpu as pltpu

F32 = jnp.float32
BF16 = jnp.bfloat16
SDS = jax.ShapeDtypeStruct
MESH = pl.DeviceIdType.MESH

EPS = 1e-6
LRU_C = 8.0
CONV_WIDTH = 4
POOL_WINDOWS = (2, 4, 8, 16)
HALO = 16
LANES = 128
SUBLANES = 8
GATE_BLOCK = 256
N_CHIPS = 4

ADAM_LR = 0.001
ADAM_B1 = 0.9
ADAM_B2 = 0.999
ADAM_EPS = 1e-08
ADAM_WD = 0.01
ADAM_STEP = 10

TM_PROJ = 512
TM_MIX = 512
TM_FFN = 512
TM_WGRAD = 2048
MIX_SAVED = ("xc", "a", "mult", "ge")
MIX_SAVED_BF16 = ("r", "ig", "dge")
FFN_ROW_CHUNKS = 2
VMEM_LIMIT = 56 * 1024 * 1024

SLAB_W = 512
ROW_CONV_B, ROW_CONV_W, ROW_BA, ROW_BX, ROW_LAM, ROW_PB, ROW_PS, ROW_GL, ROW_GP = 0, 1, 5, 6, 7, 8, 9, 10, 11
ROW_GA, ROW_GX, ROW_PW = 16, 80, 144
ROW_MIX, ROW_FFN, ROW_FIN, ROW_LOSS = 272, 274, 276, 278
MIX_SLAB_ROWS = 272
SLAB_ROWS = 288


def _cp(sem=None, **kw):
    if sem is not None:
        kw["dimension_semantics"] = sem
    return pltpu.CompilerParams(vmem_limit_bytes=VMEM_LIMIT, **kw)


def _const_spec(shape):
    nd = len(shape)
    return pl.BlockSpec(shape, lambda *_: (0,) * nd, pipeline_mode=pl.Buffered(1))


def _sigmoid(x):
    return 1.0 / (1.0 + jnp.exp(-x))


def _dot(a, b):
    return jnp.dot(a, b, preferred_element_type=F32)


def _dot_nt(a, b):
    return lax.dot_general(a, b, (((1,), (1,)), ((), ())), preferred_element_type=F32)


def _dot_tn(a, b):
    return lax.dot_general(a, b, (((0,), (0,)), ((), ())), preferred_element_type=F32)


def _colsum8(v):
    m, c = v.shape
    return v.reshape(m // SUBLANES, SUBLANES, c).sum(axis=0)


def _rowmean(v):
    return jnp.mean(v, axis=-1, keepdims=True)


def _rms_bwd(dy, xhat, r, g):
    dxh = dy * g
    return r * (dxh - xhat * _rowmean(dxh * xhat))


def _softplus_neg(lam):
    z = -lam
    e = jnp.exp(-jnp.abs(z))
    u = 1.0 + e
    d = u - 1.0
    log1p = jnp.where(d == 0.0, e, jnp.log(u) * (e / jnp.where(d == 0.0, 1.0, d)))
    return jnp.maximum(z, 0.0) + log1p


def _neg_expm1(z):
    series = -(z * (1.0 + z * (0.5 + z * (1.0 / 6.0 + z * (1.0 / 24.0)))))
    return jnp.where(z > -0.03, series, 1.0 - jnp.exp(z))


_GELU_C = math.sqrt(2.0 / math.pi)
_GELU_K = 0.044715


def _gelu_parts(x):
    x2 = x * x
    th = jnp.tanh(_GELU_C * (x + _GELU_K * x2 * x))
    ge = 0.5 * x * (1.0 + th)
    dge = 0.5 * (1.0 + th) + 0.5 * x * (1.0 - th * th) * (_GELU_C * (1.0 + 3.0 * _GELU_K * x2))
    return ge, dge


def _shift_down(halo, tile, k):
    if k == 0:
        return tile
    ext = jnp.concatenate([halo, tile], axis=0)
    h = halo.shape[0]
    return pltpu.roll(ext, k, 0)[h:]


def _shift_up(tile, nxt, k):
    if k == 0:
        return tile
    ext = jnp.concatenate([tile, nxt], axis=0)
    return pltpu.roll(ext, ext.shape[0] - k, 0)[:tile.shape[0]]


def _build_gate_blocks(ga_ref, gx_ref, gw_ref):
    hd = ga_ref.shape[1]
    per = GATE_BLOCK // hd
    zero = jnp.zeros((hd, hd), F32)
    for b in range(gw_ref.shape[0]):
        for src, off in ((ga_ref, 0), (gx_ref, GATE_BLOCK)):
            for hh in range(per):
                row = jnp.concatenate([zero] * hh + [src[b * per + hh]] + [zero] * (per - 1 - hh), axis=1)
                gw_ref[b, hh * hd:(hh + 1) * hd, off:off + GATE_BLOCK] = row.astype(BF16)


def _scan_level1(a, b, reverse):
    m, c = a.shape
    a3 = a.reshape(m // SUBLANES, SUBLANES, c)
    b3 = b.reshape(m // SUBLANES, SUBLANES, c)
    row = lax.broadcasted_iota(jnp.int32, a3.shape, 1)
    for s in (1, 2, 4):
        sh = (SUBLANES - s) if reverse else s
        a_sh = pltpu.roll(a3, sh, 1)
        b_sh = pltpu.roll(b3, sh, 1)
        ok = (row < SUBLANES - s) if reverse else (row >= s)
        b3 = jnp.where(ok, a3 * b_sh + b3, b3)
        a3 = jnp.where(ok, a3 * a_sh, a3)
    return a3.reshape(m, c), b3.reshape(m, c)


def _scan_level2(a_ref, b_ref, out_ref, carry, reverse):
    m, c = a_ref.shape
    ng = m // SUBLANES

    def step(g, cr):
        gi = (ng - 1 - g) if reverse else g
        off = pl.multiple_of(gi * SUBLANES, SUBLANES)
        h = b_ref[pl.ds(off, SUBLANES), :] + a_ref[pl.ds(off, SUBLANES), :] * cr
        out_ref[pl.ds(off, SUBLANES), :] = h
        edge = h[0:1, :] if reverse else h[SUBLANES - 1:SUBLANES, :]
        return jnp.broadcast_to(edge, (SUBLANES, c))

    return lax.fori_loop(0, ng, step, carry, unroll=4)


def _mixer_recompute(u_ref, hal, t0, cw, cb, gw_ref, ba, bx, lam, pw_ref, pb, ps, saved_ref, pooled_ref):
    tm = u_ref.shape[0]
    lw = cb.shape[1]
    keep = {name: k for k, name in enumerate(MIX_SAVED)}
    keep16 = {name: k for k, name in enumerate(MIX_SAVED_BF16)}
    hal_l, hal_p = hal[:, :lw], hal[:, 2 * lw:]
    xc = cb
    for k in range(CONV_WIDTH):
        xc = xc + _shift_down(hal_l, u_ref[:, :lw], CONV_WIDTH - 1 - k) * cw[k:k + 1, :]
    saved_ref[keep["xc"]] = xc
    xcb = xc.astype(BF16)
    nb = lw // GATE_BLOCK
    gs = [_dot(xcb[:, b * GATE_BLOCK:(b + 1) * GATE_BLOCK], gw_ref[b]) for b in range(nb)]
    r = _sigmoid(jnp.concatenate([g[:, :GATE_BLOCK] for g in gs], axis=1) + ba)
    pooled_ref[keep16["r"]] = r.astype(BF16)
    ig = _sigmoid(jnp.concatenate([g[:, GATE_BLOCK:] for g in gs], axis=1) + bx)
    pooled_ref[keep16["ig"]] = ig.astype(BF16)
    la = (-LRU_C * r) * _softplus_neg(lam)
    a = jnp.exp(la)
    saved_ref[keep["a"]] = a
    mult = jnp.sqrt(jnp.maximum(_neg_expm1(2.0 * la), 1e-12))
    saved_ref[keep["mult"]] = mult
    bb = mult * (ig * saved_ref[keep["xc"]])
    ge, dge = _gelu_parts(u_ref[:, lw:2 * lw])
    saved_ref[keep["ge"]] = ge
    pooled_ref[keep16["dge"]] = dge.astype(BF16)
    row = lax.broadcasted_iota(jnp.int32, (HALO, LANES), 0) + t0
    zs = []
    for gi, w in enumerate(POOL_WINDOWS):
        sl = slice(gi * LANES, (gi + 1) * LANES)
        e = jnp.concatenate([hal_p[:, sl], u_ref[:, 2 * lw + gi * LANES:2 * lw + (gi + 1) * LANES]], axis=0)
        s = e
        k = 1
        while k < w:
            s = s + pltpu.roll(s, k, 0)
            k *= 2
        mean = jnp.concatenate([s[HALO:2 * HALO] * (1.0 / jnp.minimum(row + 1, w).astype(F32)),
                                s[2 * HALO:] * (1.0 / w)], axis=0)
        pg = (mean - e[HALO:]).astype(BF16)
        pooled_ref[len(MIX_SAVED_BF16), :, sl] = pg
        zs.append(_dot(pg, pw_ref[gi].astype(BF16)))
    y_pool = (jnp.concatenate(zs, axis=1) + pb) * ps
    return a, bb, y_pool


ANY = pl.BlockSpec(memory_space=pl.ANY)
VMEM_SPEC = pl.BlockSpec(memory_space=pltpu.VMEM)


class _Hosted:
    def __init__(self, ins, out_shapes, sems, start, finish, mid=None, aliases=None):
        self.ins, self.out_shapes, self.sems = list(ins), list(out_shapes), list(sems)
        self.start, self.mid, self.finish = start, mid, finish
        self.aliases = dict(aliases or {})


def _call(body, hosted, stage_preds, *, name, grid, in_specs, out_specs, out_shape, scratch_shapes, args, sem):
    hosted = list(hosted or [])
    n_in, n_out, n_scr = len(in_specs), len(out_specs), len(scratch_shapes)
    c_in = [a for h in hosted for a in h.ins]
    c_out = [o for h in hosted for o in h.out_shapes]
    c_sem = [pltpu.SemaphoreType.DMA((k,)) for h in hosted for k in h.sems]

    def full(*refs):
        p = 0
        parts = []
        for cnt in (n_in, len(c_in), n_out, len(c_out), n_scr, len(c_sem)):
            parts.append(refs[p:p + cnt])
            p += cnt
        hi, ci, ho, co, hs, cs = parts
        per = []
        a = b = c_ = 0
        for h in hosted:
            per.append((h, ci[a:a + len(h.ins)], co[b:b + len(h.out_shapes)], cs[c_:c_ + len(h.sems)]))
            a, b, c_ = a + len(h.ins), b + len(h.out_shapes), c_ + len(h.sems)
        first = mid = last = None
        if hosted and grid:
            first, mid, last = stage_preds()

        def run(fn, pred, i_, o_, s_):
            if fn is None:
                return
            if pred is None:
                fn(i_, o_, s_)
            else:
                pl.when(pred)(functools.partial(fn, i_, o_, s_))

        for h, i_, o_, s_ in per:
            run(h.start, first, i_, o_, s_)
        body(*hi, *ho, *hs)
        for h, i_, o_, s_ in per:
            run(h.mid, mid, i_, o_, s_)
        for h, i_, o_, s_ in per:
            run(h.finish, last, i_, o_, s_)

    aliases = {}
    a = b = 0
    for h in hosted:
        for k, v in h.aliases.items():
            aliases[n_in + a + k] = n_out + b + v
        a, b = a + len(h.ins), b + len(h.out_shapes)
    res = pl.pallas_call(
        full, name=name, grid=grid, in_specs=list(in_specs) + [ANY] * len(c_in),
        out_specs=list(out_specs) + [ANY] * len(c_out), out_shape=list(out_shape) + c_out,
        scratch_shapes=list(scratch_shapes) + c_sem, input_output_aliases=aliases,
        compiler_params=_cp(sem))(*args, *c_in)
    res = list(res)
    outs = []
    p = n_out
    for h in hosted:
        outs.append(res[p:p + len(h.out_shapes)])
        p += len(h.out_shapes)
    return res[:n_out], outs


def _inproj(x, g_mix, w_in, hosted=None):
    s, d = x.shape
    n = w_in.shape[1]
    tm = min(TM_PROJ, s)
    nt = s // tm

    def body(x_ref, g_ref, w_ref, u_ref):
        xv = x_ref[...]
        r = lax.rsqrt(_rowmean(xv * xv) + EPS)
        u_ref[...] = _dot((xv * r * g_ref[...]).astype(BF16), w_ref[...])

    def stages():
        i = pl.program_id(0)
        return i == 0, i == max(nt - 3, 0), i == nt - 1

    return _call(
        body, hosted, stages, grid=(nt,), name="inproj",
        in_specs=[pl.BlockSpec((tm, d), lambda i: (i, 0)), _const_spec((1, d)), _const_spec((d, n))],
        out_specs=[pl.BlockSpec((tm, n), lambda i: (i, 0))], out_shape=[SDS((s, n), F32)], scratch_shapes=[],
        args=(x, g_mix, w_in), sem=("arbitrary",))


def _mixer_fwd(u, x, sp_, w_out, hosted=None):
    s, din = u.shape
    d = x.shape[1]
    lw = din // 3
    tm = min(TM_MIX, s)
    nb = lw // GATE_BLOCK

    def body(u_ref, halo_ref, x_ref, cw_ref, cb_ref, ga_ref, gx_ref, ba_ref, bx_ref, lam_ref, pw_ref, pb_ref,
             ps_ref, gl_ref, gp_ref, wout_ref, h_ref, hres_ref, saved_ref, pooled_ref,
             gw_s, a_s, b_s, carry_s):
        i = pl.program_id(0)

        @pl.when(i == 0)
        def _():
            _build_gate_blocks(ga_ref, gx_ref, gw_s)
            carry_s[...] = jnp.zeros_like(carry_s)

        hal = jnp.where(i > 0, halo_ref[...], 0.0)
        a, bb, yp = _mixer_recompute(u_ref, hal, i * tm, cw_ref[...], cb_ref[...], gw_s, ba_ref[...], bx_ref[...],
                                     lam_ref[...], pw_ref, pb_ref[...], ps_ref[...], saved_ref, pooled_ref)
        a1, b1 = _scan_level1(a, bb, reverse=False)
        a_s[...] = a1
        b_s[...] = b1
        carry_s[...] = _scan_level2(a_s, b_s, h_ref, carry_s[...], reverse=False)
        y_lru = h_ref[...] * saved_ref[MIX_SAVED.index("ge")]
        rl = lax.rsqrt(_rowmean(y_lru * y_lru) + EPS)
        rp = lax.rsqrt(_rowmean(yp * yp) + EPS)
        yn = jnp.concatenate([y_lru * rl * gl_ref[...], yp * rp * gp_ref[...]], axis=1).astype(BF16)
        hres_ref[...] = x_ref[...] + _dot(yn, wout_ref[...])

    small = [sp_[k] for k in ("conv_w", "conv_b", "gate_a_w", "gate_x_w", "gate_a_b", "gate_x_b", "lru_lambda",
                              "pool_w", "pool_b", "pool_scale", "norm_lru_g", "norm_pool_g")]
    nt = s // tm

    def stages():
        i = pl.program_id(0)
        return i == 0, i == max(nt - 3, 0), i == nt - 1

    return _call(
        body, hosted, stages, grid=(nt,), name="mixer_fwd",
        in_specs=[pl.BlockSpec((tm, din), lambda i: (i, 0)),
                  pl.BlockSpec((HALO, din), lambda i: (jnp.maximum(i * (tm // HALO) - 1, 0), 0)),
                  pl.BlockSpec((tm, d), lambda i: (i, 0))]
        + [_const_spec(a.shape) for a in small] + [_const_spec(w_out.shape)],
        out_specs=[pl.BlockSpec((tm, lw), lambda i: (i, 0)), pl.BlockSpec((tm, d), lambda i: (i, 0)),
                   pl.BlockSpec((len(MIX_SAVED), tm, lw), lambda i: (0, i, 0)),
                   pl.BlockSpec((len(MIX_SAVED_BF16) + 1, tm, lw), lambda i: (0, i, 0))],
        out_shape=[SDS((s, lw), F32), SDS((s, d), F32), SDS((len(MIX_SAVED), s, lw), F32),
                   SDS((len(MIX_SAVED_BF16) + 1, s, lw), BF16)],
        scratch_shapes=[pltpu.VMEM((nb, GATE_BLOCK, 2 * GATE_BLOCK), BF16), pltpu.VMEM((tm, lw), F32),
                        pltpu.VMEM((tm, lw), F32), pltpu.VMEM((SUBLANES, lw), F32)],
        args=(u, u, x, *small, w_out), sem=("arbitrary",))


def _row_chunks(tm):
    rc = tm // FFN_ROW_CHUNKS
    return [slice(q * rc, (q + 1) * rc) for q in range(FFN_ROW_CHUNKS)]


def _ffn_up(hres1, g_ffn, w1, w3):
    s, d = hres1.shape
    nj, fc, _ = w1.shape
    tm = min(TM_FFN, s)

    def body(h_ref, gf_ref, w1_ref, w3_ref, h2_ref, a1_ref, a3_ref, ff_ref):
        hv = h_ref[...]
        r = lax.rsqrt(_rowmean(hv * hv) + EPS)
        h2_ref[...] = (hv * r * gf_ref[...]).astype(BF16)
        h2 = h2_ref[...]
        for j in range(nj):
            a1 = _dot_nt(h2, w1_ref[j])
            a3 = _dot_nt(h2, w3_ref[j])
            a1_ref[j] = a1.astype(BF16)
            a3_ref[j] = a3.astype(BF16)
            ff_ref[j] = ((a1 * _sigmoid(a1)) * a3).astype(BF16)

    wspec = _const_spec(w1.shape)
    aspec = pl.BlockSpec((nj, tm, fc), lambda i: (0, i, 0))
    return pl.pallas_call(
        body, grid=(s // tm,), name="ffn_up",
        in_specs=[pl.BlockSpec((tm, d), lambda i: (i, 0)), _const_spec((1, d)), wspec, wspec],
        out_specs=[pl.BlockSpec((tm, d), lambda i: (i, 0)), aspec, aspec, aspec],
        out_shape=[SDS((s, d), BF16)] + [SDS((nj, s, fc), BF16)] * 3,
        compiler_params=_cp(("parallel",)))(hres1, g_ffn, w1, w3)


def _ffn_down(ff, hres1, target, g_fin, w2):
    s, d = hres1.shape
    nj, _, fc = ff.shape
    tm = min(TM_FFN, s)

    def body(ff_ref, h_ref, t_ref, gn_ref, w2_ref, dh_ref, dhb_ref, loss_ref, dgn_ref):
        @pl.when(pl.program_id(0) == 0)
        def _():
            loss_ref[...] = jnp.zeros_like(loss_ref)
            dgn_ref[...] = jnp.zeros_like(dgn_ref)

        gn = gn_ref[...]
        for rows in _row_chunks(tm):
            acc = _dot(ff_ref[0, rows, :], w2_ref[0])
            for j in range(1, nj):
                acc = acc + _dot(ff_ref[j, rows, :], w2_ref[j])
            hr2 = h_ref[rows, :] + acc
            r2 = lax.rsqrt(_rowmean(hr2 * hr2) + EPS)
            xh = hr2 * r2
            diff = xh * gn - t_ref[rows, :]
            tot = jnp.sum(jnp.sum(diff * diff, axis=1, keepdims=True), axis=0, keepdims=True)
            loss_ref[...] += tot * (0.5 / d)
            dout = diff * (1.0 / d)
            dgn_ref[...] += _colsum8(dout * xh)
            dh = _rms_bwd(dout, xh, r2, gn)
            dh_ref[rows, :] = dh
            dhb_ref[rows, :] = dh.astype(BF16)

    tile = pl.BlockSpec((tm, d), lambda i: (i, 0))
    return pl.pallas_call(
        body, grid=(s // tm,), name="ffn_down",
        in_specs=[pl.BlockSpec((nj, tm, fc), lambda i: (0, i, 0)), tile, tile, _const_spec((1, d)),
                  _const_spec(w2.shape)],
        out_specs=[tile, tile, pl.BlockSpec((SUBLANES, LANES), lambda i: (0, 0)),
                   pl.BlockSpec((SUBLANES, d), lambda i: (0, 0))],
        out_shape=[SDS((s, d), F32), SDS((s, d), BF16), SDS((SUBLANES, LANES), F32), SDS((SUBLANES, d), F32)],
        compiler_params=_cp(("arbitrary",)))(ff, hres1, target, g_fin, w2)


def _ffn_bwd_gate(dhb, a1, a3, w2):
    s, d = dhb.shape
    nj, _, fc = a1.shape
    tm = min(TM_FFN, s)

    def body(dhb_ref, a1_ref, a3_ref, w2_ref, da1_ref, da3_ref):
        for j in range(nj):
            for rows in _row_chunks(tm):
                dff = _dot_nt(dhb_ref[rows, :], w2_ref[j])
                a1v = a1_ref[j, rows, :].astype(F32)
                sg = _sigmoid(a1v)
                silu = a1v * sg
                da1_ref[j, rows, :] = (dff * a3_ref[j, rows, :].astype(F32)
                                       * (sg * (1.0 + (a1v - silu)))).astype(BF16)
                da3_ref[j, rows, :] = (dff * silu).astype(BF16)

    aspec = pl.BlockSpec((nj, tm, fc), lambda i: (0, i, 0))
    return pl.pallas_call(
        body, grid=(s // tm,), name="ffn_bwd_gate",
        in_specs=[pl.BlockSpec((tm, d), lambda i: (i, 0)), aspec, aspec, _const_spec(w2.shape)],
        out_specs=[aspec, aspec], out_shape=[SDS((nj, s, fc), BF16)] * 2,
        compiler_params=_cp(("parallel",)))(dhb, a1, a3, w2)


def _ffn_bwd_down(da1, da3, dh, hres1, g_ffn, w1, w3, hosted=None):
    s, d = hres1.shape
    nj, _, fc = da1.shape
    tm = min(TM_FFN, s)
    nt = s // tm

    def body(da1_ref, da3_ref, dh_ref, h_ref, gf_ref, w1_ref, w3_ref, dhr_ref, dgf_ref):
        @pl.when(pl.program_id(0) == 0)
        def _():
            dgf_ref[...] = jnp.zeros_like(dgf_ref)

        gf = gf_ref[...]
        for rows in _row_chunks(tm):
            dh2 = None
            for j in range(nj):
                part = _dot(da1_ref[j, rows, :], w1_ref[j]) + _dot(da3_ref[j, rows, :], w3_ref[j])
                dh2 = part if dh2 is None else dh2 + part
            hv = h_ref[rows, :]
            r = lax.rsqrt(_rowmean(hv * hv) + EPS)
            xh = hv * r
            dgf_ref[...] += _colsum8(dh2 * xh)
            dhr_ref[rows, :] = dh_ref[rows, :] + _rms_bwd(dh2, xh, r, gf)

    tile = pl.BlockSpec((tm, d), lambda i: (i, 0))
    aspec = pl.BlockSpec((nj, tm, fc), lambda i: (0, i, 0))
    wspec = _const_spec(w1.shape)

    def stages():
        i = pl.program_id(0)
        return i == 0, i == max(nt - 2, 0), i == nt - 1

    return _call(
        body, hosted, stages, grid=(nt,), name="ffn_bwd_down",
        in_specs=[aspec, aspec, tile, tile, _const_spec((1, d)), wspec, wspec],
        out_specs=[tile, pl.BlockSpec((SUBLANES, d), lambda i: (0, 0))],
        out_shape=[SDS((s, d), F32), SDS((SUBLANES, d), F32)],
        scratch_shapes=[], args=(da1, da3, dh, hres1, g_ffn, w1, w3), sem=("arbitrary",))


def _ffn_wgrad(h2, dhb, ff, da1, da3):
    s, d = h2.shape
    _, _, fc = ff.shape
    tm = min(TM_WGRAD, s)

    def body(h2_ref, dhb_ref, ff_ref, da1_ref, da3_ref, dw1_ref, dw3_ref, dw2_ref):
        @pl.when(pl.program_id(1) == 0)
        def _():
            dw1_ref[...] = jnp.zeros_like(dw1_ref)
            dw3_ref[...] = jnp.zeros_like(dw3_ref)
            dw2_ref[...] = jnp.zeros_like(dw2_ref)

        h2v = h2_ref[...]
        dw1_ref[0] += _dot_tn(da1_ref[0], h2v)
        dw3_ref[0] += _dot_tn(da3_ref[0], h2v)
        dw2_ref[0] += _dot_tn(ff_ref[0], dhb_ref[...])

    wspec = pl.BlockSpec((1, fc, d), lambda j, i: (j, 0, 0))
    return pl.pallas_call(
        body, grid=(N_CHIPS, s // tm), name="ffn_wgrad",
        in_specs=[pl.BlockSpec((tm, d), lambda j, i: (i, 0)), pl.BlockSpec((tm, d), lambda j, i: (i, 0))]
        + [pl.BlockSpec((1, tm, fc), lambda j, i: (j, i, 0))] * 3,
        out_specs=[wspec] * 3, out_shape=[SDS((N_CHIPS, fc, d), F32)] * 3,
        compiler_params=_cp(("parallel", "arbitrary")))(h2, dhb, ff, da1, da3)


def _mixer_bwd(u, saved, pooled, h, dhres1, sp_, w_out, hosted=None):
    s, din = u.shape
    d = dhres1.shape[1]
    lw = din // 3
    tm = min(TM_MIX, s)
    nt = s // tm
    nb = lw // GATE_BLOCK
    hd = sp_["gate_a_w"].shape[1]

    def body(ul_ref, saved_ref, pooled_ref, h_ref, hhalo_ref, dhr_ref, cw_ref, cb_ref, ga_ref, gx_ref, ba_ref,
             bx_ref, lam_ref, pw_ref, pb_ref, ps_ref, gl_ref, gp_ref, wout_ref, du_ref, slab_ref, dwout_ref,
             gw_s, a_s, b_s, e_s, ecarry_s, dxc_s, q_s, vec_s, cwacc_s, dgw_s, dpw_s):
        i = pl.program_id(0)
        tile = nt - 1 - i

        @pl.when(i == 0)
        def _():
            _build_gate_blocks(ga_ref, gx_ref, gw_s)
            for ref in (ecarry_s, dxc_s, q_s, vec_s, cwacc_s, dgw_s, dpw_s, dwout_ref):
                ref[...] = jnp.zeros_like(ref)

        cw = cw_ref[...]
        lam = lam_ref[...]
        ps = ps_ref[...]
        def sv_(name):
            if name in MIX_SAVED_BF16:
                return pooled_ref[MIX_SAVED_BF16.index(name)].astype(F32)
            return saved_ref[MIX_SAVED.index(name)]

        npool = len(MIX_SAVED_BF16)

        f = {"sp": _softplus_neg(lam)}
        row = lax.broadcasted_iota(jnp.int32, (HALO, LANES), 0) + tile * tm
        f["z"] = jnp.concatenate(
            [_dot(pooled_ref[npool, :, g * LANES:(g + 1) * LANES], pw_ref[g].astype(BF16))
             for g in range(len(POOL_WINDOWS))], axis=1) + pb_ref[...]
        f["y_pool"] = f["z"] * ps
        y_lru = h_ref[...] * sv_("ge")
        rl = lax.rsqrt(_rowmean(y_lru * y_lru) + EPS)
        yp = f["y_pool"]
        rp = lax.rsqrt(_rowmean(yp * yp) + EPS)
        xh_l = y_lru * rl
        xh_p = yp * rp

        dhrb = dhr_ref[...].astype(BF16)
        dyn = _dot_nt(dhrb, wout_ref[...])
        yn = jnp.concatenate([xh_l * gl_ref[...], xh_p * gp_ref[...]], axis=1).astype(BF16)
        dwout_ref[...] += _dot_tn(yn, dhrb)
        d_nl, d_np = dyn[:, :lw], dyn[:, lw:]
        vec = {}
        vec[ROW_GL] = _colsum8(d_nl * xh_l)
        vec[ROW_GP] = _colsum8(d_np * xh_p)
        d_ylru = _rms_bwd(d_nl, xh_l, rl, gl_ref[...])
        d_ypool = _rms_bwd(d_np, xh_p, rp, gp_ref[...])

        vec[ROW_PS] = _colsum8(d_ypool * f["z"])
        dz = d_ypool * ps
        vec[ROW_PB] = _colsum8(dz)
        dzb = dz.astype(BF16)
        dup = []
        for gi, w in enumerate(POOL_WINDOWS):
            sl = slice(gi * LANES, (gi + 1) * LANES)
            dpw_s[:, sl] += _dot_tn(pooled_ref[npool, :, sl], dzb[:, sl])
            dpool = _dot_nt(dzb[:, sl], pw_ref[gi].astype(BF16))
            q = jnp.concatenate([dpool[:HALO] * (1.0 / jnp.minimum(row + 1, w).astype(F32)),
                                 dpool[HALO:] * (1.0 / w)], axis=0)
            e = jnp.concatenate([q, q_s[:, sl]], axis=0)
            k = 1
            while k < w:
                e = e + pltpu.roll(e, tm + HALO - k, 0)
                k *= 2
            dup.append(e[:tm] - dpool)
            q_s[:, sl] = q[:HALO]

        d_hout = d_ylru * sv_("ge")
        d_ug = d_ylru * h_ref[...] * sv_("dge")
        a1, b1 = _scan_level1(sv_("a"), sv_("a") * d_hout, reverse=True)
        a_s[...] = a1
        b_s[...] = b1
        e_next = ecarry_s[...]
        ecarry_s[...] = _scan_level2(a_s, b_s, e_s, e_next, reverse=True)
        sv = d_hout + _shift_up(e_s[...], e_next, 1)
        d_a = sv * _shift_down(jnp.where(tile > 0, hhalo_ref[...], 0.0), h_ref[...], 1)
        mult = sv_("mult")
        ig = sv_("ig")
        d_mult = sv * (ig * sv_("xc"))
        d_ig = sv * mult * sv_("xc")
        d_xc = sv * mult * ig
        a = sv_("a")
        d_la = d_a * a + jnp.where(mult > 1.000001e-6, d_mult * (-(a * a) / mult), 0.0)
        d_r = d_la * (-LRU_C * f["sp"])
        r = sv_("r")
        vec[ROW_LAM] = _colsum8(d_la * (-LRU_C * r))
        d_pr = d_r * r * (1.0 - r)
        d_pi = d_ig * ig * (1.0 - ig)
        vec[ROW_BA] = _colsum8(d_pr)
        vec[ROW_BX] = _colsum8(d_pi)
        dxc_parts = []
        for b in range(nb):
            sl = slice(b * GATE_BLOCK, (b + 1) * GATE_BLOCK)
            rhs = jnp.concatenate([d_pr[:, sl], d_pi[:, sl]], axis=1).astype(BF16)
            dgw_s[b] += _dot_tn(saved_ref[MIX_SAVED.index("xc"), :, sl].astype(BF16), rhs)
            dxc_parts.append(_dot_nt(rhs, gw_s[b]))
        d_xc = d_xc + jnp.concatenate(dxc_parts, axis=1)
        vec[ROW_CONV_B] = _colsum8(d_xc)
        dxc_next = dxc_s[...]
        d_ul = None
        for k in range(CONV_WIDTH):
            ahead = _shift_up(d_xc, dxc_next, CONV_WIDTH - 1 - k)
            cwacc_s[k * SUBLANES:(k + 1) * SUBLANES, :] += _colsum8(ahead * ul_ref[...])
            term = ahead * cw[k:k + 1, :]
            d_ul = term if d_ul is None else d_ul + term
        dxc_s[...] = d_xc[:SUBLANES]
        for row, val in vec.items():
            vec_s[row * SUBLANES:(row + 1) * SUBLANES, :] += val
        du_ref[...] = jnp.concatenate([d_ul, d_ug] + dup, axis=1).astype(BF16)

        @pl.when(i == nt - 1)
        def _():
            rows = []
            for row in range(ROW_GA):
                if row in (ROW_CONV_W, ROW_CONV_W + 1, ROW_CONV_W + 2, ROW_CONV_W + 3):
                    k = row - ROW_CONV_W
                    v = jnp.sum(cwacc_s[k * SUBLANES:(k + 1) * SUBLANES, :], axis=0, keepdims=True)
                elif row <= ROW_GP:
                    v = jnp.sum(vec_s[row * SUBLANES:(row + 1) * SUBLANES, :], axis=0, keepdims=True)
                    if row == ROW_LAM:
                        v = v * (-1.0 / (1.0 + jnp.exp(lam)))
                else:
                    v = jnp.zeros((1, lw), F32)
                rows.append(v)
            slab_ref[0:ROW_GA, :] = jnp.concatenate(rows, axis=0)
            lane = lax.broadcasted_iota(jnp.int32, (hd, GATE_BLOCK), 1)
            for b in range(nb):
                for off, row0 in ((0, ROW_GA), (GATE_BLOCK, ROW_GX)):
                    acc = jnp.zeros((hd, GATE_BLOCK), F32)
                    for hh in range(GATE_BLOCK // hd):
                        m = (lane >= hh * hd) & (lane < (hh + 1) * hd)
                        acc = acc + jnp.where(m, dgw_s[b, hh * hd:(hh + 1) * hd, off:off + GATE_BLOCK], 0.0)
                    slab_ref[row0:row0 + hd, b * GATE_BLOCK:(b + 1) * GATE_BLOCK] = acc
            slab_ref[ROW_PW:ROW_PW + LANES, :] = dpw_s[...]

    small = [sp_[k] for k in ("conv_w", "conv_b", "gate_a_w", "gate_x_w", "gate_a_b", "gate_x_b", "lru_lambda",
                              "pool_w", "pool_b", "pool_scale", "norm_lru_g", "norm_pool_g")]
    rev = lambda i: nt - 1 - i

    def stages():
        i = pl.program_id(0)
        return i == 0, i == max(nt - 3, 0), i == nt - 1

    return _call(
        body, hosted, stages, grid=(nt,), name="mixer_bwd",
        in_specs=[pl.BlockSpec((tm, lw), lambda i: (rev(i), 0)),
                  pl.BlockSpec((len(MIX_SAVED), tm, lw), lambda i: (0, rev(i), 0)),
                  pl.BlockSpec((len(MIX_SAVED_BF16) + 1, tm, lw), lambda i: (0, rev(i), 0)),
                  pl.BlockSpec((tm, lw), lambda i: (rev(i), 0)),
                  pl.BlockSpec((SUBLANES, lw), lambda i: (jnp.maximum(rev(i) * (tm // SUBLANES) - 1, 0), 0)),
                  pl.BlockSpec((tm, d), lambda i: (rev(i), 0))]
        + [_const_spec(a.shape) for a in small] + [_const_spec(w_out.shape)],
        out_specs=[pl.BlockSpec((tm, din), lambda i: (rev(i), 0)),
                   pl.BlockSpec((MIX_SLAB_ROWS, SLAB_W), lambda i: (0, 0)), pl.BlockSpec((d, d), lambda i: (0, 0))],
        out_shape=[SDS((s, din), BF16), SDS((MIX_SLAB_ROWS, SLAB_W), F32), SDS((d, d), F32)],
        scratch_shapes=[pltpu.VMEM((nb, GATE_BLOCK, 2 * GATE_BLOCK), BF16),
                        pltpu.VMEM((tm, lw), F32), pltpu.VMEM((tm, lw), F32), pltpu.VMEM((tm, lw), F32),
                        pltpu.VMEM((SUBLANES, lw), F32), pltpu.VMEM((SUBLANES, lw), F32),
                        pltpu.VMEM((HALO, lw), F32), pltpu.VMEM((ROW_GA * SUBLANES, lw), F32),
                        pltpu.VMEM((CONV_WIDTH * SUBLANES, lw), F32),
                        pltpu.VMEM((nb, GATE_BLOCK, 2 * GATE_BLOCK), F32), pltpu.VMEM((LANES, lw), F32)],
        args=(u, saved, pooled, h, h, dhres1, *small, w_out), sem=("arbitrary",))


def _inproj_bwd(x, du, dhres1, g_mix, w_in, hosted=None):
    s, d = x.shape
    n = w_in.shape[1]
    nc = n // N_CHIPS
    tm = min(TM_PROJ, s)
    nt = s // tm

    def body(x_ref, du_ref, dhr_ref, g_ref, w_ref, gx_ref, dwin_ref, dg_ref):
        i = pl.program_id(0)

        @pl.when(i == 0)
        def _():
            dwin_ref[...] = jnp.zeros_like(dwin_ref)
            dg_ref[...] = jnp.zeros_like(dg_ref)

        xv = x_ref[...]
        g = g_ref[...]
        r = lax.rsqrt(_rowmean(xv * xv) + EPS)
        xh = xv * r
        h1 = (xh * g).astype(BF16)
        duv = du_ref[...]
        dh1 = _dot_nt(duv, w_ref[...])
        dg_ref[...] += _colsum8(dh1 * xh)
        gx_ref[...] = dhr_ref[...] + _rms_bwd(dh1, xh, r, g)
        for jj in range(N_CHIPS):
            dwin_ref[jj] += _dot_tn(h1, duv[:, jj * nc:(jj + 1) * nc])

    def stages():
        i = pl.program_id(0)
        return i == 0, i == max(nt - 3, 0), i == nt - 1

    return _call(
        body, hosted, stages, grid=(nt,), name="inproj_bwd",
        in_specs=[pl.BlockSpec((tm, d), lambda i: (i, 0)), pl.BlockSpec((tm, n), lambda i: (i, 0)),
                  pl.BlockSpec((tm, d), lambda i: (i, 0)), _const_spec((1, d)), _const_spec((d, n))],
        out_specs=[pl.BlockSpec((tm, d), lambda i: (i, 0)), pl.BlockSpec((N_CHIPS, d, nc), lambda i: (0, 0, 0)),
                   pl.BlockSpec((SUBLANES, d), lambda i: (0, 0))],
        out_shape=[SDS((s, d), F32), SDS((N_CHIPS, d, nc), F32), SDS((SUBLANES, d), F32)],
        scratch_shapes=[], args=(x, du, dhres1, g_mix, w_in), sem=("arbitrary",))


def _place():
    x, y, c = lax.axis_index("x"), lax.axis_index("y"), lax.axis_index("c")
    return x, y, c


def _other_chips(x, y):
    return [(1 - x, y), (x, 1 - y), (1 - x, 1 - y)]


ANY = pl.BlockSpec(memory_space=pl.ANY)
VMEM_SPEC = pl.BlockSpec(memory_space=pltpu.VMEM)

_GATHERED = {"w_in": "cols", "w_out": "major", "ffn_w1": "major", "ffn_w3": "major", "ffn_w2": "major"}
_BIG = ("w_in", "w_out", "ffn_w1", "ffn_w3", "ffn_w2")


def _gather_weights(shards, conv_w, n_remote):
    n = len(shards)
    full_shapes = []
    for name, sh in zip(_BIG, shards):
        r, cdim = sh.shape
        if _GATHERED[name] == "cols":
            assert cdim % LANES == 0
            full_shapes.append((r, cdim * N_CHIPS))
        else:
            full_shapes.append((N_CHIPS, r, cdim))

    def region(ref, name, sh, jj, cc):
        r, cdim = sh
        rows = pl.ds(0, r) if cc is None else pl.ds(pl.multiple_of(cc * (r // 2), 16), r // 2)
        if _GATHERED[name] == "cols":
            return ref.at[rows, pl.ds(pl.multiple_of(jj * cdim, LANES), cdim)]
        return ref.at[jj, rows, :]

    def staged(ref, sh, cc):
        r = sh[0]
        return ref.at[pl.ds(pl.multiple_of(cc * (r // 2), 16), r // 2), :]

    def body(*refs):
        ins, cw_in = refs[:n], refs[n]
        outs, cw_out = refs[n + 1:2 * n + 1], refs[2 * n + 1]
        stage = refs[2 * n + 2:3 * n + 2]
        cw_stage, lsem, ssem, rsem, fssem, frsem, cssem, crsem = refs[3 * n + 2:]
        x, y, c = _place()
        j = 2 * x + y
        chips = _other_chips(x, y)
        for w in range(n_remote):
            stage[w][...] = ins[w][...].astype(BF16)
        cw_stage[...] = jnp.zeros_like(cw_stage)
        cw_stage[0:CONV_WIDTH, :] = cw_in[...]
        shs = [s_.shape for s_ in shards]
        local = [pltpu.make_async_copy(stage[w], region(outs[w], _BIG[w], shs[w], j, None), lsem.at[w])
                 for w in range(n)]
        local.append(pltpu.make_async_copy(cw_stage, cw_out.at[j], lsem.at[n]))
        sends = []
        for k, (px, py) in enumerate(chips):
            for w in range(n_remote):
                sends.append(pltpu.make_async_remote_copy(
                    src_ref=staged(stage[w], shs[w], c), dst_ref=region(outs[w], _BIG[w], shs[w], j, c),
                    send_sem=ssem.at[k * n + w], recv_sem=rsem.at[k * n + w], device_id=(px, py, c),
                    device_id_type=MESH))
            sends.append(pltpu.make_async_remote_copy(
                src_ref=cw_stage, dst_ref=cw_out.at[j], send_sem=cssem.at[k], recv_sem=crsem.at[k],
                device_id=(px, py, c), device_id_type=MESH))
        for cp in sends:
            cp.start()
        for w in range(n_remote, n):
            stage[w][...] = ins[w][...].astype(BF16)
        for cp in local:
            cp.start()
        fwd = []
        for k, (px, py) in enumerate(chips):
            jk = 2 * px + py
            for w in range(n_remote):
                reg = region(outs[w], _BIG[w], shs[w], jk, c)
                pltpu.make_async_remote_copy(src_ref=reg, dst_ref=reg, send_sem=ssem.at[k * n + w],
                                             recv_sem=rsem.at[k * n + w], device_id=(px, py, c),
                                             device_id_type=MESH).wait_recv()
                cp = pltpu.make_async_remote_copy(src_ref=reg, dst_ref=reg, send_sem=fssem.at[k * n + w],
                                                  recv_sem=frsem.at[k * n + w], device_id=(x, y, 1 - c),
                                                  device_id_type=MESH)
                cp.start()
                fwd.append(cp)
            pltpu.make_async_remote_copy(src_ref=cw_stage, dst_ref=cw_out.at[jk], send_sem=cssem.at[k],
                                         recv_sem=crsem.at[k], device_id=(px, py, c),
                                         device_id_type=MESH).wait_recv()
        for k, (px, py) in enumerate(chips):
            jk = 2 * px + py
            for w in range(n_remote):
                reg = region(outs[w], _BIG[w], shs[w], jk, 1 - c)
                pltpu.make_async_remote_copy(src_ref=reg, dst_ref=reg, send_sem=fssem.at[k * n + w],
                                             recv_sem=frsem.at[k * n + w], device_id=(x, y, 1 - c),
                                             device_id_type=MESH).wait_recv()
        for cp in sends + fwd:
            cp.wait_send()
        for cp in local:
            cp.wait()

    nsem = 3 * n
    return pl.pallas_call(
        body, name="gather_first",
        in_specs=[VMEM_SPEC] * (n + 1), out_specs=[ANY] * (n + 1),
        out_shape=[SDS(fs, BF16) for fs in full_shapes] + [SDS((N_CHIPS, SUBLANES, LANES), F32)],
        scratch_shapes=[pltpu.VMEM(s_.shape, BF16) for s_ in shards] + [pltpu.VMEM((SUBLANES, LANES), F32)]
        + [pltpu.SemaphoreType.DMA((n + 1,))] + [pltpu.SemaphoreType.DMA((nsem,))] * 4
        + [pltpu.SemaphoreType.DMA((3,))] * 2,
        compiler_params=_cp())(*shards, conv_w)


def _start_all(make):
    def f(ins, outs, sems):
        for cp in make(ins, outs, sems):
            cp.start()
    return f


def _wait_all(make):
    def f(ins, outs, sems):
        for cp in make(ins, outs, sems):
            cp.wait()
    return f


def _ffn_gather_hosted(arrs):
    n = len(arrs)

    def make(outs, sems):
        ssem, rsem, fs, fr = sems
        x, y, c = _place()
        j = 2 * x + y

        def reg(w, jj, cc):
            hr = arrs[w].shape[1] // 2
            return outs[w].at[jj, pl.ds(pl.multiple_of(cc * hr, 16), hr), :]

        def rc(w, jj, cc, s_sem, r_sem, dev):
            return pltpu.make_async_remote_copy(src_ref=reg(w, jj, cc), dst_ref=reg(w, jj, cc), send_sem=s_sem,
                                                recv_sem=r_sem, device_id=dev, device_id_type=MESH)

        sends, recvs, fwds, frecvs = [], [], [], []
        for k, (px, py) in enumerate(_other_chips(x, y)):
            jk = 2 * px + py
            for w in range(n):
                q = k * n + w
                sends.append(rc(w, j, c, ssem.at[q], rsem.at[q], (px, py, c)))
                recvs.append(rc(w, jk, c, ssem.at[q], rsem.at[q], (px, py, c)))
                fwds.append(rc(w, jk, c, fs.at[q], fr.at[q], (x, y, 1 - c)))
                frecvs.append(rc(w, jk, 1 - c, fs.at[q], fr.at[q], (x, y, 1 - c)))
        return sends, recvs, fwds, frecvs

    def start(ins, outs, sems):
        for cp in make(outs, sems)[0]:
            cp.start()

    def mid(ins, outs, sems):
        _, recvs, fwds, _ = make(outs, sems)
        for r, f in zip(recvs, fwds):
            r.wait_recv()
            f.start()

    def finish(ins, outs, sems):
        sends, _, fwds, frecvs = make(outs, sems)
        for r in frecvs:
            r.wait_recv()
        for cp in sends + fwds:
            cp.wait_send()

    return _Hosted(arrs, [SDS(a.shape, a.dtype) for a in arrs], [3 * n] * 4, start, finish, mid=mid,
                   aliases={w: w for w in range(n)})


def _rs_sibling_hosted(arrs):
    n = len(arrs)

    def make(ins, outs, sems):
        x, y, c = _place()
        cps = []
        for w in range(n):
            hr = arrs[w].shape[1] // 2
            src = ins[w].at[:, pl.ds(pl.multiple_of((1 - c) * hr, SUBLANES), hr), :]
            cps.append(pltpu.make_async_remote_copy(src_ref=src, dst_ref=outs[w], send_sem=sems[0].at[w],
                                                    recv_sem=sems[1].at[w], device_id=(x, y, 1 - c),
                                                    device_id_type=MESH))
        return cps

    return _Hosted(arrs, [SDS((a.shape[0], a.shape[1] // 2, a.shape[2]), F32) for a in arrs], [n, n],
                   _start_all(make), _wait_all(make))


def _rs_chips_hosted(parts):
    n = len(parts)

    def make(ins, outs, sems):
        x, y, c = _place()
        j = 2 * x + y
        cps = []
        for k, (px, py) in enumerate(_other_chips(x, y)):
            jk = 2 * px + py
            for w in range(n):
                cps.append(pltpu.make_async_remote_copy(
                    src_ref=ins[w].at[jk], dst_ref=outs[w].at[j], send_sem=sems[0].at[k * n + w],
                    recv_sem=sems[1].at[k * n + w], device_id=(px, py, c), device_id_type=MESH))
        return cps

    return _Hosted(parts, [SDS(p.shape, p.dtype) for p in parts], [3 * n, 3 * n], _start_all(make), _wait_all(make))


def _rs_swap_hosted(halves):
    n = len(halves)

    def make(ins, outs, sems):
        x, y, c = _place()
        return [pltpu.make_async_remote_copy(src_ref=ins[w], dst_ref=outs[w], send_sem=sems[0].at[w],
                                             recv_sem=sems[1].at[w], device_id=(x, y, 1 - c), device_id_type=MESH)
                for w in range(n)]

    return _Hosted(halves, [SDS(h.shape, F32) for h in halves], [n, n], _start_all(make), _wait_all(make))


HBM_SPEC = pl.BlockSpec(memory_space=pltpu.HBM)
SEM_SPEC = pl.BlockSpec(memory_space=pltpu.SEMAPHORE)
_EFFECT = pltpu.SideEffectType.DATAFLOW_SIDE_EFFECTING


def _split_start(h, name):
    n_in, n_out, ns = len(h.ins), len(h.out_shapes), len(h.sems)
    ins = [pltpu.with_memory_space_constraint(a, pltpu.HBM) for a in h.ins]
    lands = [pltpu.with_memory_space_constraint(lax.empty(o.shape, o.dtype), pltpu.HBM) for o in h.out_shapes]

    def body(*refs):
        i_refs, l_refs = refs[:n_in], refs[n_in:n_in + n_out]
        s_refs = refs[n_in + n_out:n_in + n_out + ns]
        token = refs[-1]
        h.start(i_refs, l_refs, s_refs)
        token[...] = jnp.zeros_like(token)

    res = pl.pallas_call(
        body, name=name, in_specs=[HBM_SPEC] * (n_in + n_out),
        out_specs=[SEM_SPEC] * ns + [HBM_SPEC] * n_out + [VMEM_SPEC],
        out_shape=[pltpu.SemaphoreType.DMA((k,)) for k in h.sems]
        + [pltpu.HBM(o.shape, o.dtype) for o in h.out_shapes] + [SDS((SUBLANES, LANES), F32)],
        input_output_aliases={n_in + k: ns + k for k in range(n_out)},
        compiler_params=pltpu.CompilerParams(has_side_effects=_EFFECT))(*ins, *lands)
    return list(res[:ns]) + ins + list(res[ns:-1]), res[-1]


def _split_wait(h, state, after, name):
    n_in, n_out, ns = len(h.ins), len(h.out_shapes), len(h.sems)
    sems, bufs = state[:ns], state[ns:]

    def body(*refs):
        i_refs, l_refs = refs[:n_in], refs[n_in:n_in + n_out]
        s_refs = refs[n_in + n_out:n_in + n_out + ns]
        h.finish(i_refs, l_refs, s_refs)

    res = pl.pallas_call(
        body, name=name, in_specs=[HBM_SPEC] * (n_in + n_out) + [SEM_SPEC] * ns + [ANY],
        out_specs=[HBM_SPEC] * n_out,
        out_shape=[pltpu.HBM(b.shape, b.dtype) for b in bufs[n_in:]],
        input_output_aliases={n_in + k: k for k in range(n_out)},
        compiler_params=pltpu.CompilerParams(has_side_effects=_EFFECT))(*bufs, *sems, after)
    return list(res)


def _run_comm(hosted, name):
    return _call(lambda: None, hosted, None, name=name, grid=(), in_specs=[], out_specs=[], out_shape=[],
                 scratch_shapes=[], args=(), sem=None)[1]


def _row_tile(rows, cols, n_arrays):
    budget = 24 * 1024 * 1024 // (2 * 4 * n_arrays * cols)
    best = SUBLANES
    for t in range(SUBLANES, rows + 1, SUBLANES):
        if rows % t == 0 and t <= budget:
            best = t
    return best


def _place_index(which):
    x, y, c = _place()
    v = c if which == "c" else 2 * x + y
    return jnp.reshape(v, (1,)).astype(jnp.int32)


def _add_own_half(full, recv, name, wire=BF16):
    nsh, rows, cols = full.shape
    hr = rows // 2
    t = _row_tile(hr, cols, 4)
    nt = hr // t

    def body(c_ref, a_ref, b_ref, o_ref, ob_ref):
        v = a_ref[...] + b_ref[...]
        o_ref[...] = v
        ob_ref[...] = v.astype(wire)

    half = pl.BlockSpec((1, t, cols), lambda s_, i, c_ref: (s_, i, 0))
    return pl.pallas_call(
        body, name=name,
        grid_spec=pltpu.PrefetchScalarGridSpec(
            num_scalar_prefetch=1, grid=(nsh, nt),
            in_specs=[pl.BlockSpec((1, t, cols), lambda s_, i, c_ref: (s_, c_ref[0] * nt + i, 0)), half],
            out_specs=[half, half]),
        out_shape=[SDS((nsh, hr, cols), F32), SDS((nsh, hr, cols), wire)],
        compiler_params=_cp(("parallel", "parallel")))(_place_index("c"), full, recv)


def _sum_chips(own, recv, name):
    nsh, hr, cols = own.shape
    t = _row_tile(hr, cols, 6)

    def body(j_ref, own_ref, *rest):
        r_refs, o_ref = rest[:nsh], rest[nsh]
        j = j_ref[0]
        mine = own_ref[0]
        parts = [jnp.where(j == k, mine, r_refs[k][0].astype(F32)) for k in range(nsh)]
        o_ref[...] = ((parts[0] + parts[1]) + parts[2]) + parts[3]

    def other(k):
        return pl.BlockSpec((1, t, cols), lambda i, j_ref: (jnp.where(j_ref[0] == k, (k + 1) % nsh, k), i, 0))

    return pl.pallas_call(
        body, name=name,
        grid_spec=pltpu.PrefetchScalarGridSpec(
            num_scalar_prefetch=1, grid=(hr // t,),
            in_specs=[pl.BlockSpec((1, t, cols), lambda i, j_ref: (j_ref[0], i, 0))]
            + [other(k) for k in range(nsh)],
            out_specs=pl.BlockSpec((t, cols), lambda i, j_ref: (i, 0))),
        out_shape=SDS((hr, cols), F32), compiler_params=_cp(("parallel",)))(_place_index("j"), own, *([recv] * nsh))


def _adamw_math(w, g, m, v):
    m = ADAM_B1 * m + (1.0 - ADAM_B1) * g
    v = ADAM_B2 * v + (1.0 - ADAM_B2) * (g * g)
    m_hat = m / (1.0 - ADAM_B1 ** ADAM_STEP)
    v_hat = v / (1.0 - ADAM_B2 ** ADAM_STEP)
    delta = -ADAM_LR * (m_hat / (jnp.sqrt(v_hat) + ADAM_EPS) + ADAM_WD * w)
    return delta, m, v


def _adamw_big(w, g_own, g_sib, m, v, name, token=None):
    _, rows, cols = w.shape
    hr = rows // 2
    t = _row_tile(hr, cols, 9)
    nth = hr // t
    if token is None:
        token = jnp.zeros((SUBLANES, LANES), F32)

    def body(c_ref, w_ref, go_ref, gs_ref, m_ref, v_ref, tok_ref, g_ref, d_ref, mo_ref, vo_ref):
        own = (pl.program_id(0) // nth) == c_ref[0]
        g = jnp.where(own, go_ref[...], gs_ref[...]) + tok_ref[0:1, 0:1]
        g_ref[0] = g
        d_ref[0], mo_ref[0], vo_ref[0] = _adamw_math(w_ref[0], g, m_ref[0], v_ref[0])

    spec = pl.BlockSpec((1, t, cols), lambda i, c_ref: (0, i, 0))
    hspec = pl.BlockSpec((t, cols), lambda i, c_ref: (i % nth, 0))
    tspec = pl.BlockSpec((SUBLANES, LANES), lambda i, c_ref: (0, 0))
    return pl.pallas_call(
        body, name=name,
        grid_spec=pltpu.PrefetchScalarGridSpec(
            num_scalar_prefetch=1, grid=(2 * nth,), in_specs=[spec, hspec, hspec, spec, spec, tspec],
            out_specs=[spec] * 4),
        out_shape=[SDS((1, rows, cols), F32)] * 4,
        compiler_params=_cp(("parallel",)))(_place_index("c"), w, g_own, g_sib, m, v, token)


def _build_slab(mix_slab, dg_mix, dg_ffn, dg_fin, loss8):
    def body(ms_ref, gm_ref, gf_ref, gn_ref, loss_ref, out_ref):
        rows = []
        for ref in (gm_ref, gf_ref, gn_ref):
            v = jnp.sum(ref[...], axis=0, keepdims=True)
            rows += [v[:, :SLAB_W], v[:, SLAB_W:]]
        rows.append(jnp.concatenate([loss_ref[0:1, :]] * (SLAB_W // LANES), axis=1))
        rows.append(jnp.zeros((SLAB_ROWS - ROW_LOSS - 1, SLAB_W), F32))
        tail = jnp.concatenate(rows, axis=0)
        for k in range(N_CHIPS):
            out_ref[k, 0:MIX_SLAB_ROWS, :] = ms_ref[...]
            out_ref[k, MIX_SLAB_ROWS:SLAB_ROWS, :] = tail

    return pl.pallas_call(
        body, name="build_slab", in_specs=[VMEM_SPEC] * 5, out_specs=VMEM_SPEC,
        out_shape=SDS((N_CHIPS, SLAB_ROWS, SLAB_W), F32),
        compiler_params=_cp())(mix_slab, dg_mix, dg_ffn, dg_fin, loss8)


_SMALL_ROWS = (("conv_b", ROW_CONV_B), ("gate_a_b", ROW_BA), ("gate_x_b", ROW_BX), ("lru_lambda", ROW_LAM),
               ("pool_b", ROW_PB), ("pool_scale", ROW_PS), ("norm_lru_g", ROW_GL), ("norm_pool_g", ROW_GP))
_WIDE_ROWS = (("norm_mix_g", ROW_MIX), ("norm_ffn_g", ROW_FFN), ("final_norm_g", ROW_FIN))
_BLOCK_ROWS = (("gate_a_w", ROW_GA), ("gate_x_w", ROW_GX), ("pool_w", ROW_PW))
_SMALL_ORDER = tuple(n for n, _ in _SMALL_ROWS) + tuple(n for n, _ in _WIDE_ROWS) + tuple(
    n for n, _ in _BLOCK_ROWS) + ("conv_w",)


def _adamw_small(slab_own, slab_sib, wmv):
    names = _SMALL_ORDER
    flat = [a for nme in names for a in wmv[nme]]
    nin = len(flat)

    def body(*refs):
        own_ref, sib_ref, j_ref = refs[0], refs[1], refs[2]
        ins = refs[3:3 + nin]
        outs = refs[3 + nin:-1]
        first = j_ref[1] == 0
        slab_ref = jnp.concatenate([jnp.where(first, own_ref[...], sib_ref[...]),
                                    jnp.where(first, sib_ref[...], own_ref[...])], axis=0)
        refs[-1][...] = jnp.broadcast_to(slab_ref[ROW_LOSS:ROW_LOSS + 1, 0:LANES], (SUBLANES, LANES))
        grads = {}
        for nme, row in _SMALL_ROWS:
            grads[nme] = slab_ref[row:row + 1, :]
        for nme, row in _WIDE_ROWS:
            grads[nme] = jnp.concatenate([slab_ref[row:row + 1, :], slab_ref[row + 1:row + 2, :]], axis=1)
        full = slab_ref[ROW_CONV_W:ROW_CONV_W + CONV_WIDTH, :]
        jv = j_ref[0]
        g = jnp.zeros((CONV_WIDTH, LANES), F32)
        for jj in range(N_CHIPS):
            g = jnp.where(jv == jj, full[:, jj * LANES:(jj + 1) * LANES], g)
        grads["conv_w"] = g
        block_rows = dict(_BLOCK_ROWS)
        for idx, nme in enumerate(names):
            w_ref, m_ref, v_ref = ins[3 * idx:3 * idx + 3]
            if nme in block_rows:
                nblk, r, c = w_ref.shape
                parts = [(b, slab_ref[block_rows[nme]:block_rows[nme] + r, b * c:(b + 1) * c]) for b in range(nblk)]
            else:
                parts = [(Ellipsis, grads[nme])]
            for b, g in parts:
                delta, m, v = _adamw_math(w_ref[b], g, m_ref[b], v_ref[b])
                outs[4 * idx][b] = g
                outs[4 * idx + 1][b] = delta
                outs[4 * idx + 2][b] = m
                outs[4 * idx + 3][b] = v

    place = jnp.concatenate([_place_index("j"), _place_index("c")])
    out_shape = [SDS(wmv[nme][0].shape, F32) for nme in names for _ in range(4)] + [SDS((SUBLANES, LANES), F32)]
    res = pl.pallas_call(
        body, name="adamw_small",
        in_specs=[VMEM_SPEC, VMEM_SPEC, pl.BlockSpec(memory_space=pltpu.SMEM)] + [VMEM_SPEC] * nin,
        out_specs=[VMEM_SPEC] * len(out_shape), out_shape=out_shape,
        compiler_params=_cp())(slab_own, slab_sib, place, *flat)
    return {nme: tuple(res[4 * idx:4 * idx + 4]) for idx, nme in enumerate(names)}, res[-1]


_FFN = ("ffn_w1", "ffn_w3", "ffn_w2")
_TRANSPOSED = ("ffn_w1", "ffn_w3")


def _local_step(x, target, full, sp_, distributed):
    d = x.shape[1]
    (u,), got = _inproj(x, sp_["norm_mix_g"], full["w_in"],
                        [_ffn_gather_hosted([full["w_out"]])] if distributed else None)
    w_out = (got[0][0] if distributed else full["w_out"]).reshape(d, d)
    gather = [_ffn_gather_hosted([full[n] for n in _FFN])] if distributed else None
    (h, hres1, saved, pooled), got = _mixer_fwd(u, x, sp_, w_out, gather)
    w1, w3, w2 = got[0] if distributed else [full[n] for n in _FFN]
    h2, a1, a3, ff = _ffn_up(hres1, sp_["norm_ffn_g"], w1, w3)
    dh, dhb, loss8, dg_fin = _ffn_down(ff, hres1, target, sp_["final_norm_g"], w2)
    da1, da3 = _ffn_bwd_gate(dhb, a1, a3, w2)
    dws = list(_ffn_wgrad(h2, dhb, ff, da1, da3))
    rs1 = [_rs_sibling_hosted(dws)] if distributed else None
    (dhres1, dg_ffn), got = _ffn_bwd_down(da1, da3, dh, hres1, sp_["norm_ffn_g"], w1, w3, rs1)
    rs2 = None
    if distributed:
        pairs = [_add_own_half(a, r, "add_half_" + n) for n, a, r in zip(_FFN, dws, got[0])]
        rs2 = [_rs_chips_hosted([pb for _, pb in pairs])]
    (du, mix_slab, dwout), got = _mixer_bwd(u, saved, pooled, h, dhres1, sp_, w_out, rs2)
    g_mix = sp_["norm_mix_g"]
    if distributed:
        fin = [_sum_chips(pairs[k][0], got[0][k], "sum_chips_" + n) for k, n in enumerate(_FFN)]
        swap = _rs_swap_hosted(fin)
        state, token = _split_start(swap, "ffn_swap_start")
        g_mix = g_mix + token[0:1, 0:1]
    (gx, dwin, dg_mix), _ = _inproj_bwd(x, du, dhres1, g_mix, full["w_in"])
    if distributed:
        sib = _split_wait(swap, state, dg_mix, "ffn_swap_wait")
    big = {"w_in": dwin, "w_out": dwout.reshape(N_CHIPS, d // N_CHIPS, d)}
    for k, n in enumerate(_FFN):
        big[n] = (fin[k], sib[k]) if distributed else dws[k]
    return gx, big, (mix_slab, dg_mix, dg_ffn, dg_fin, loss8)


_SMALL_LAYOUT = {
    "gate_a_w": (lambda a: a[0], lambda a: a[None]),
    "gate_x_w": (lambda a: a[0], lambda a: a[None]),
    "pool_w": (lambda a: a[0], lambda a: a[None]),
    "conv_w": (lambda a: a[0], lambda a: a[None]),
    "final_norm_g": (lambda a: a[None], lambda a: a[0]),
}

_WEIGHTS = ("norm_mix_g", "w_in", "conv_w", "conv_b", "gate_a_w", "gate_a_b", "gate_x_w", "gate_x_b", "lru_lambda",
            "pool_w", "pool_b", "pool_scale", "norm_lru_g", "norm_pool_g", "w_out", "norm_ffn_g", "ffn_w1",
            "ffn_w3", "ffn_w2", "final_norm_g")


def kernel(x, norm_mix_g, w_in, conv_w, conv_b, gate_a_w, gate_a_b, gate_x_w, gate_x_b, lru_lambda, pool_w, pool_b, pool_scale, norm_lru_g, norm_pool_g, w_out, norm_ffn_g, ffn_w1, ffn_w3, ffn_w2, final_norm_g, loss_target, m_norm_mix_g, m_w_in, m_conv_w, m_conv_b, m_gate_a_w, m_gate_a_b, m_gate_x_w, m_gate_x_b, m_lru_lambda, m_pool_w, m_pool_b, m_pool_scale, m_norm_lru_g, m_norm_pool_g, m_w_out, m_norm_ffn_g, m_ffn_w1, m_ffn_w3, m_ffn_w2, m_final_norm_g, v_norm_mix_g, v_w_in, v_conv_w, v_conv_b, v_gate_a_w, v_gate_a_b, v_gate_x_w, v_gate_x_b, v_lru_lambda, v_pool_w, v_pool_b, v_pool_scale, v_norm_lru_g, v_norm_pool_g, v_w_out, v_norm_ffn_g, v_ffn_w1, v_ffn_w3, v_ffn_w2, v_final_norm_g):
    loc = locals()
    w = {n: loc[n] for n in _WEIGHTS}
    m = {n: loc["m_" + n] for n in _WEIGHTS}
    v = {n: loc["v_" + n] for n in _WEIGHTS}

    def lay(nme, a):
        return _SMALL_LAYOUT[nme][0](a) if nme in _SMALL_LAYOUT else a

    def unlay(nme, a):
        return _SMALL_LAYOUT[nme][1](a) if nme in _SMALL_LAYOUT else a

    for group in (w, m, v):
        for n in _TRANSPOSED:
            group[n] = jnp.transpose(group[n], (0, 2, 1))

    gathered = _gather_weights([w[n][0] for n in _BIG], w["conv_w"][0], n_remote=1)
    full = dict(zip(_BIG, gathered[:-1]))
    cw_all = gathered[-1]
    sp_ = {n: lay(n, w[n]) for n in _SMALL_ORDER}
    sp_["conv_w"] = jnp.transpose(cw_all[:, :CONV_WIDTH, :], (1, 0, 2)).reshape(CONV_WIDTH, N_CHIPS * LANES)

    gx, big, small = _local_step(x[0], loss_target[0], full, sp_, distributed=True)

    late = ("w_in", "w_out", "slab")
    big["slab"] = _build_slab(*small)
    fin = {n: big[n][0] for n in _FFN}
    sib = {n: big[n][1] for n in _FFN}
    recv1, = _run_comm([_rs_sibling_hosted([big[n] for n in late])], "tail_sibling")
    pairs = [_add_own_half(big[n], r, "add_half_" + n, F32 if n == "slab" else BF16) for n, r in zip(late, recv1)]
    chips = _rs_chips_hosted([pb for _, pb in pairs])
    state, token = _split_start(chips, "tail_chips_start")
    out = {}
    for n in _FFN:
        out[n] = tuple(_adamw_big(w[n], fin[n], sib[n], m[n], v[n], "adamw_" + n, token))
    recv2 = _split_wait(chips, state, out[_FFN[-1]][1], "tail_chips_wait")
    for n, (p, _), r in zip(late, pairs, recv2):
        fin[n] = _sum_chips(p, r, "sum_chips_" + n)
    swapped, = _run_comm([_rs_swap_hosted([fin[n] for n in late])], "tail_swap")
    sib.update(zip(late, swapped))
    for n in late[:2]:
        out[n] = tuple(_adamw_big(w[n], fin[n], sib[n], m[n], v[n], "adamw_" + n))
    for n in _TRANSPOSED:
        out[n] = tuple(jnp.transpose(a, (0, 2, 1)) for a in out[n])
    wmv = {n: (lay(n, w[n]), lay(n, m[n]), lay(n, v[n])) for n in _SMALL_ORDER}
    res, loss = _adamw_small(fin["slab"], sib["slab"], wmv)
    for n in _SMALL_ORDER:
        out[n] = tuple(unlay(n, a) for a in res[n])
    return (loss[0, 0], gx[None]) + tuple(out[n][k] for k in range(4) for n in _WEIGHTS)
```

```python
import functools
import math

import jax
import jax.numpy as jnp
from jax import lax
from jax.experimental import pallas as pl
from jax.experimental.pallas import tpu as pltpu

F32 = jnp.float32
BF16 = jnp.bfloat16
SDS = jax.ShapeDtypeStruct
MESH = pl.DeviceIdType.MESH

EPS = 1e-6
LRU_C = 8.0
CONV_WIDTH = 4
POOL_WINDOWS = (2, 4, 8, 16)
HALO = 16
LANES = 128
SUBLANES = 8
GATE_BLOCK = 256
N_CHIPS = 4

ADAM_LR = 0.001
ADAM_B1 = 0.9
ADAM_B2 = 0.999
ADAM_EPS = 1e-08
ADAM_WD = 0.01
ADAM_STEP = 10

TM_PROJ = 512
TM_MIX = 512
TM_FFN = 512
TM_WGRAD = 2048
MIX_SAVED = ("xc", "a", "mult", "ge")
MIX_SAVED_BF16 = ("r", "ig", "dge")
FFN_ROW_CHUNKS = 2
VMEM_LIMIT = 56 * 1024 * 1024

SLAB_W = 512
ROW_CONV_B, ROW_CONV_W, ROW_BA, ROW_BX, ROW_LAM, ROW_PB, ROW_PS, ROW_GL, ROW_GP = 0, 1, 5, 6, 7, 8, 9, 10, 11
ROW_GA, ROW_GX, ROW_PW = 16, 80, 144
ROW_MIX, ROW_FFN, ROW_FIN, ROW_LOSS = 272, 274, 276, 278
MIX_SLAB_ROWS = 272
SLAB_ROWS = 288


def _cp(sem=None, **kw):
    if sem is not None:
        kw["dimension_semantics"] = sem
    return pltpu.CompilerParams(vmem_limit_bytes=VMEM_LIMIT, **kw)


def _const_spec(shape):
    nd = len(shape)
    return pl.BlockSpec(shape, lambda *_: (0,) * nd, pipeline_mode=pl.Buffered(1))


def _sigmoid(x):
    return 1.0 / (1.0 + jnp.exp(-x))


def _dot(a, b):
    return jnp.dot(a, b, preferred_element_type=F32)


def _dot_nt(a, b):
    return lax.dot_general(a, b, (((1,), (1,)), ((), ())), preferred_element_type=F32)


def _dot_tn(a, b):
    return lax.dot_general(a, b, (((0,), (0,)), ((), ())), preferred_element_type=F32)


def _colsum8(v):
    m, c = v.shape
    return v.reshape(m // SUBLANES, SUBLANES, c).sum(axis=0)


def _rowmean(v):
    return jnp.mean(v, axis=-1, keepdims=True)


def _rms_bwd(dy, xhat, r, g):
    dxh = dy * g
    return r * (dxh - xhat * _rowmean(dxh * xhat))


def _softplus_neg(lam):
    z = -lam
    e = jnp.exp(-jnp.abs(z))
    u = 1.0 + e
    d = u - 1.0
    log1p = jnp.where(d == 0.0, e, jnp.log(u) * (e / jnp.where(d == 0.0, 1.0, d)))
    return jnp.maximum(z, 0.0) + log1p


def _neg_expm1(z):
    series = -(z * (1.0 + z * (0.5 + z * (1.0 / 6.0 + z * (1.0 / 24.0)))))
    return jnp.where(z > -0.03, series, 1.0 - jnp.exp(z))


_GELU_C = math.sqrt(2.0 / math.pi)
_GELU_K = 0.044715


def _gelu_parts(x):
    x2 = x * x
    th = jnp.tanh(_GELU_C * (x + _GELU_K * x2 * x))
    ge = 0.5 * x * (1.0 + th)
    dge = 0.5 * (1.0 + th) + 0.5 * x * (1.0 - th * th) * (_GELU_C * (1.0 + 3.0 * _GELU_K * x2))
    return ge, dge


def _shift_down(halo, tile, k):
    if k == 0:
        return tile
    ext = jnp.concatenate([halo, tile], axis=0)
    h = halo.shape[0]
    return pltpu.roll(ext, k, 0)[h:]


def _shift_up(tile, nxt, k):
    if k == 0:
        return tile
    ext = jnp.concatenate([tile, nxt], axis=0)
    return pltpu.roll(ext, ext.shape[0] - k, 0)[:tile.shape[0]]


def _build_gate_blocks(ga_ref, gx_ref, gw_ref):
    hd = ga_ref.shape[1]
    per = GATE_BLOCK // hd
    zero = jnp.zeros((hd, hd), F32)
    for b in range(gw_ref.shape[0]):
        for src, off in ((ga_ref, 0), (gx_ref, GATE_BLOCK)):
            for hh in range(per):
                row = jnp.concatenate([zero] * hh + [src[b * per + hh]] + [zero] * (per - 1 - hh), axis=1)
                gw_ref[b, hh * hd:(hh + 1) * hd, off:off + GATE_BLOCK] = row.astype(BF16)


def _scan_level1(a, b, reverse):
    m, c = a.shape
    a3 = a.reshape(m // SUBLANES, SUBLANES, c)
    b3 = b.reshape(m // SUBLANES, SUBLANES, c)
    row = lax.broadcasted_iota(jnp.int32, a3.shape, 1)
    for s in (1, 2, 4):
        sh = (SUBLANES - s) if reverse else s
        a_sh = pltpu.roll(a3, sh, 1)
        b_sh = pltpu.roll(b3, sh, 1)
        ok = (row < SUBLANES - s) if reverse else (row >= s)
        b3 = jnp.where(ok, a3 * b_sh + b3, b3)
        a3 = jnp.where(ok, a3 * a_sh, a3)
    return a3.reshape(m, c), b3.reshape(m, c)


def _scan_level2(a_ref, b_ref, out_ref, carry, reverse):
    m, c = a_ref.shape
    ng = m // SUBLANES

    def step(g, cr):
        gi = (ng - 1 - g) if reverse else g
        off = pl.multiple_of(gi * SUBLANES, SUBLANES)
        h = b_ref[pl.ds(off, SUBLANES), :] + a_ref[pl.ds(off, SUBLANES), :] * cr
        out_ref[pl.ds(off, SUBLANES), :] = h
        edge = h[0:1, :] if reverse else h[SUBLANES - 1:SUBLANES, :]
        return jnp.broadcast_to(edge, (SUBLANES, c))

    return lax.fori_loop(0, ng, step, carry, unroll=4)


def _mixer_recompute(u_ref, hal, t0, cw, cb, gw_ref, ba, bx, lam, pw_ref, pb, ps, saved_ref, pooled_ref):
    tm = u_ref.shape[0]
    lw = cb.shape[1]
    keep = {name: k for k, name in enumerate(MIX_SAVED)}
    keep16 = {name: k for k, name in enumerate(MIX_SAVED_BF16)}
    hal_l, hal_p = hal[:, :lw], hal[:, 2 * lw:]
    xc = cb
    for k in range(CONV_WIDTH):
        xc = xc + _shift_down(hal_l, u_ref[:, :lw], CONV_WIDTH - 1 - k) * cw[k:k + 1, :]
    saved_ref[keep["xc"]] = xc
    xcb = xc.astype(BF16)
    nb = lw // GATE_BLOCK
    gs = [_dot(xcb[:, b * GATE_BLOCK:(b + 1) * GATE_BLOCK], gw_ref[b]) for b in range(nb)]
    r = _sigmoid(jnp.concatenate([g[:, :GATE_BLOCK] for g in gs], axis=1) + ba)
    pooled_ref[keep16["r"]] = r.astype(BF16)
    ig = _sigmoid(jnp.concatenate([g[:, GATE_BLOCK:] for g in gs], axis=1) + bx)
    pooled_ref[keep16["ig"]] = ig.astype(BF16)
    la = (-LRU_C * r) * _softplus_neg(lam)
    a = jnp.exp(la)
    saved_ref[keep["a"]] = a
    mult = jnp.sqrt(jnp.maximum(_neg_expm1(2.0 * la), 1e-12))
    saved_ref[keep["mult"]] = mult
    bb = mult * (ig * saved_ref[keep["xc"]])
    ge, dge = _gelu_parts(u_ref[:, lw:2 * lw])
    saved_ref[keep["ge"]] = ge
    pooled_ref[keep16["dge"]] = dge.astype(BF16)
    row = lax.broadcasted_iota(jnp.int32, (HALO, LANES), 0) + t0
    zs = []
    for gi, w in enumerate(POOL_WINDOWS):
        sl = slice(gi * LANES, (gi + 1) * LANES)
        e = jnp.concatenate([hal_p[:, sl], u_ref[:, 2 * lw + gi * LANES:2 * lw + (gi + 1) * LANES]], axis=0)
        s = e
        k = 1
        while k < w:
            s = s + pltpu.roll(s, k, 0)
            k *= 2
        mean = jnp.concatenate([s[HALO:2 * HALO] * (1.0 / jnp.minimum(row + 1, w).astype(F32)),
                                s[2 * HALO:] * (1.0 / w)], axis=0)
        pg = (mean - e[HALO:]).astype(BF16)
        pooled_ref[len(MIX_SAVED_BF16), :, sl] = pg
        zs.append(_dot(pg, pw_ref[gi].astype(BF16)))
    y_pool = (jnp.concatenate(zs, axis=1) + pb) * ps
    return a, bb, y_pool


ANY = pl.BlockSpec(memory_space=pl.ANY)
VMEM_SPEC = pl.BlockSpec(memory_space=pltpu.VMEM)


class _Hosted:
    def __init__(self, ins, out_shapes, sems, start, finish, mid=None, aliases=None):
        self.ins, self.out_shapes, self.sems = list(ins), list(out_shapes), list(sems)
        self.start, self.mid, self.finish = start, mid, finish
        self.aliases = dict(aliases or {})


def _call(body, hosted, stage_preds, *, name, grid, in_specs, out_specs, out_shape, scratch_shapes, args, sem):
    hosted = list(hosted or [])
    n_in, n_out, n_scr = len(in_specs), len(out_specs), len(scratch_shapes)
    c_in = [a for h in hosted for a in h.ins]
    c_out = [o for h in hosted for o in h.out_shapes]
    c_sem = [pltpu.SemaphoreType.DMA((k,)) for h in hosted for k in h.sems]

    def full(*refs):
        p = 0
        parts = []
        for cnt in (n_in, len(c_in), n_out, len(c_out), n_scr, len(c_sem)):
            parts.append(refs[p:p + cnt])
            p += cnt
        hi, ci, ho, co, hs, cs = parts
        per = []
        a = b = c_ = 0
        for h in hosted:
            per.append((h, ci[a:a + len(h.ins)], co[b:b + len(h.out_shapes)], cs[c_:c_ + len(h.sems)]))
            a, b, c_ = a + len(h.ins), b + len(h.out_shapes), c_ + len(h.sems)
        first = mid = last = None
        if hosted and grid:
            first, mid, last = stage_preds()

        def run(fn, pred, i_, o_, s_):
            if fn is None:
                return
            if pred is None:
                fn(i_, o_, s_)
            else:
                pl.when(pred)(functools.partial(fn, i_, o_, s_))

        for h, i_, o_, s_ in per:
            run(h.start, first, i_, o_, s_)
        body(*hi, *ho, *hs)
        for h, i_, o_, s_ in per:
            run(h.mid, mid, i_, o_, s_)
        for h, i_, o_, s_ in per:
            run(h.finish, last, i_, o_, s_)

    aliases = {}
    a = b = 0
    for h in hosted:
        for k, v in h.aliases.items():
            aliases[n_in + a + k] = n_out + b + v
        a, b = a + len(h.ins), b + len(h.out_shapes)
    res = pl.pallas_call(
        full, name=name, grid=grid, in_specs=list(in_specs) + [ANY] * len(c_in),
        out_specs=list(out_specs) + [ANY] * len(c_out), out_shape=list(out_shape) + c_out,
        scratch_shapes=list(scratch_shapes) + c_sem, input_output_aliases=aliases,
        compiler_params=_cp(sem))(*args, *c_in)
    res = list(res)
    outs = []
    p = n_out
    for h in hosted:
        outs.append(res[p:p + len(h.out_shapes)])
        p += len(h.out_shapes)
    return res[:n_out], outs


def _inproj(x, g_mix, w_in, hosted=None):
    s, d = x.shape
    n = w_in.shape[1]
    tm = min(TM_PROJ, s)
    nt = s // tm

    def body(x_ref, g_ref, w_ref, u_ref):
        xv = x_ref[...]
        r = lax.rsqrt(_rowmean(xv * xv) + EPS)
        u_ref[...] = _dot((xv * r * g_ref[...]).astype(BF16), w_ref[...])

    def stages():
        i = pl.program_id(0)
        return i == 0, i == max(nt - 3, 0), i == nt - 1

    return _call(
        body, hosted, stages, grid=(nt,), name="inproj",
        in_specs=[pl.BlockSpec((tm, d), lambda i: (i, 0)), _const_spec((1, d)), _const_spec((d, n))],
        out_specs=[pl.BlockSpec((tm, n), lambda i: (i, 0))], out_shape=[SDS((s, n), F32)], scratch_shapes=[],
        args=(x, g_mix, w_in), sem=("arbitrary",))


def _mixer_fwd(u, x, sp_, w_out, hosted=None):
    s, din = u.shape
    d = x.shape[1]
    lw = din // 3
    tm = min(TM_MIX, s)
    nb = lw // GATE_BLOCK

    def body(u_ref, halo_ref, x_ref, cw_ref, cb_ref, ga_ref, gx_ref, ba_ref, bx_ref, lam_ref, pw_ref, pb_ref,
             ps_ref, gl_ref, gp_ref, wout_ref, h_ref, hres_ref, saved_ref, pooled_ref,
             gw_s, a_s, b_s, carry_s):
        i = pl.program_id(0)

        @pl.when(i == 0)
        def _():
            _build_gate_blocks(ga_ref, gx_ref, gw_s)
            carry_s[...] = jnp.zeros_like(carry_s)

        hal = jnp.where(i > 0, halo_ref[...], 0.0)
        a, bb, yp = _mixer_recompute(u_ref, hal, i * tm, cw_ref[...], cb_ref[...], gw_s, ba_ref[...], bx_ref[...],
                                     lam_ref[...], pw_ref, pb_ref[...], ps_ref[...], saved_ref, pooled_ref)
        a1, b1 = _scan_level1(a, bb, reverse=False)
        a_s[...] = a1
        b_s[...] = b1
        carry_s[...] = _scan_level2(a_s, b_s, h_ref, carry_s[...], reverse=False)
        y_lru = h_ref[...] * saved_ref[MIX_SAVED.index("ge")]
        rl = lax.rsqrt(_rowmean(y_lru * y_lru) + EPS)
        rp = lax.rsqrt(_rowmean(yp * yp) + EPS)
        yn = jnp.concatenate([y_lru * rl * gl_ref[...], yp * rp * gp_ref[...]], axis=1).astype(BF16)
        hres_ref[...] = x_ref[...] + _dot(yn, wout_ref[...])

    small = [sp_[k] for k in ("conv_w", "conv_b", "gate_a_w", "gate_x_w", "gate_a_b", "gate_x_b", "lru_lambda",
                              "pool_w", "pool_b", "pool_scale", "norm_lru_g", "norm_pool_g")]
    nt = s // tm

    def stages():
        i = pl.program_id(0)
        return i == 0, i == max(nt - 3, 0), i == nt - 1

    return _call(
        body, hosted, stages, grid=(nt,), name="mixer_fwd",
        in_specs=[pl.BlockSpec((tm, din), lambda i: (i, 0)),
                  pl.BlockSpec((HALO, din), lambda i: (jnp.maximum(i * (tm // HALO) - 1, 0), 0)),
                  pl.BlockSpec((tm, d), lambda i: (i, 0))]
        + [_const_spec(a.shape) for a in small] + [_const_spec(w_out.shape)],
        out_specs=[pl.BlockSpec((tm, lw), lambda i: (i, 0)), pl.BlockSpec((tm, d), lambda i: (i, 0)),
                   pl.BlockSpec((len(MIX_SAVED), tm, lw), lambda i: (0, i, 0)),
                   pl.BlockSpec((len(MIX_SAVED_BF16) + 1, tm, lw), lambda i: (0, i, 0))],
        out_shape=[SDS((s, lw), F32), SDS((s, d), F32), SDS((len(MIX_SAVED), s, lw), F32),
                   SDS((len(MIX_SAVED_BF16) + 1, s, lw), BF16)],
        scratch_shapes=[pltpu.VMEM((nb, GATE_BLOCK, 2 * GATE_BLOCK), BF16), pltpu.VMEM((tm, lw), F32),
                        pltpu.VMEM((tm, lw), F32), pltpu.VMEM((SUBLANES, lw), F32)],
        args=(u, u, x, *small, w_out), sem=("arbitrary",))


def _row_chunks(tm):
    rc = tm // FFN_ROW_CHUNKS
    return [slice(q * rc, (q + 1) * rc) for q in range(FFN_ROW_CHUNKS)]


def _ffn_up(hres1, g_ffn, w1, w3):
    s, d = hres1.shape
    nj, fc, _ = w1.shape
    tm = min(TM_FFN, s)

    def body(h_ref, gf_ref, w1_ref, w3_ref, h2_ref, a1_ref, a3_ref, ff_ref):
        hv = h_ref[...]
        r = lax.rsqrt(_rowmean(hv * hv) + EPS)
        h2_ref[...] = (hv * r * gf_ref[...]).astype(BF16)
        h2 = h2_ref[...]
        for j in range(nj):
            a1 = _dot_nt(h2, w1_ref[j])
            a3 = _dot_nt(h2, w3_ref[j])
            a1_ref[j] = a1.astype(BF16)
            a3_ref[j] = a3.astype(BF16)
            ff_ref[j] = ((a1 * _sigmoid(a1)) * a3).astype(BF16)

    wspec = _const_spec(w1.shape)
    aspec = pl.BlockSpec((nj, tm, fc), lambda i: (0, i, 0))
    return pl.pallas_call(
        body, grid=(s // tm,), name="ffn_up",
        in_specs=[pl.BlockSpec((tm, d), lambda i: (i, 0)), _const_spec((1, d)), wspec, wspec],
        out_specs=[pl.BlockSpec((tm, d), lambda i: (i, 0)), aspec, aspec, aspec],
        out_shape=[SDS((s, d), BF16)] + [SDS((nj, s, fc), BF16)] * 3,
        compiler_params=_cp(("parallel",)))(hres1, g_ffn, w1, w3)


def _ffn_down(ff, hres1, target, g_fin, w2):
    s, d = hres1.shape
    nj, _, fc = ff.shape
    tm = min(TM_FFN, s)

    def body(ff_ref, h_ref, t_ref, gn_ref, w2_ref, dh_ref, dhb_ref, loss_ref, dgn_ref):
        @pl.when(pl.program_id(0) == 0)
        def _():
            loss_ref[...] = jnp.zeros_like(loss_ref)
            dgn_ref[...] = jnp.zeros_like(dgn_ref)

        gn = gn_ref[...]
        for rows in _row_chunks(tm):
            acc = _dot(ff_ref[0, rows, :], w2_ref[0])
            for j in range(1, nj):
                acc = acc + _dot(ff_ref[j, rows, :], w2_ref[j])
            hr2 = h_ref[rows, :] + acc
            r2 = lax.rsqrt(_rowmean(hr2 * hr2) + EPS)
            xh = hr2 * r2
            diff = xh * gn - t_ref[rows, :]
            tot = jnp.sum(jnp.sum(diff * diff, axis=1, keepdims=True), axis=0, keepdims=True)
            loss_ref[...] += tot * (0.5 / d)
            dout = diff * (1.0 / d)
            dgn_ref[...] += _colsum8(dout * xh)
            dh = _rms_bwd(dout, xh, r2, gn)
            dh_ref[rows, :] = dh
            dhb_ref[rows, :] = dh.astype(BF16)

    tile = pl.BlockSpec((tm, d), lambda i: (i, 0))
    return pl.pallas_call(
        body, grid=(s // tm,), name="ffn_down",
        in_specs=[pl.BlockSpec((nj, tm, fc), lambda i: (0, i, 0)), tile, tile, _const_spec((1, d)),
                  _const_spec(w2.shape)],
        out_specs=[tile, tile, pl.BlockSpec((SUBLANES, LANES), lambda i: (0, 0)),
                   pl.BlockSpec((SUBLANES, d), lambda i: (0, 0))],
        out_shape=[SDS((s, d), F32), SDS((s, d), BF16), SDS((SUBLANES, LANES), F32), SDS((SUBLANES, d), F32)],
        compiler_params=_cp(("arbitrary",)))(ff, hres1, target, g_fin, w2)


def _ffn_bwd_gate(dhb, a1, a3, w2):
    s, d = dhb.shape
    nj, _, fc = a1.shape
    tm = min(TM_FFN, s)

    def body(dhb_ref, a1_ref, a3_ref, w2_ref, da1_ref, da3_ref):
        for j in range(nj):
            for rows in _row_chunks(tm):
                dff = _dot_nt(dhb_ref[rows, :], w2_ref[j])
                a1v = a1_ref[j, rows, :].astype(F32)
                sg = _sigmoid(a1v)
                silu = a1v * sg
                da1_ref[j, rows, :] = (dff * a3_ref[j, rows, :].astype(F32)
                                       * (sg * (1.0 + (a1v - silu)))).astype(BF16)
                da3_ref[j, rows, :] = (dff * silu).astype(BF16)

    aspec = pl.BlockSpec((nj, tm, fc), lambda i: (0, i, 0))
    return pl.pallas_call(
        body, grid=(s // tm,), name="ffn_bwd_gate",
        in_specs=[pl.BlockSpec((tm, d), lambda i: (i, 0)), aspec, aspec, _const_spec(w2.shape)],
        out_specs=[aspec, aspec], out_shape=[SDS((nj, s, fc), BF16)] * 2,
        compiler_params=_cp(("parallel",)))(dhb, a1, a3, w2)


def _ffn_bwd_down(da1, da3, dh, hres1, g_ffn, w1, w3, hosted=None):
    s, d = hres1.shape
    nj, _, fc = da1.shape
    tm = min(TM_FFN, s)
    nt = s // tm

    def body(da1_ref, da3_ref, dh_ref, h_ref, gf_ref, w1_ref, w3_ref, dhr_ref, dgf_ref):
        @pl.when(pl.program_id(0) == 0)
        def _():
            dgf_ref[...] = jnp.zeros_like(dgf_ref)

        gf = gf_ref[...]
        for rows in _row_chunks(tm):
            dh2 = None
            for j in range(nj):
                part = _dot(da1_ref[j, rows, :], w1_ref[j]) + _dot(da3_ref[j, rows, :], w3_ref[j])
                dh2 = part if dh2 is None else dh2 + part
            hv = h_ref[rows, :]
            r = lax.rsqrt(_rowmean(hv * hv) + EPS)
            xh = hv * r
            dgf_ref[...] += _colsum8(dh2 * xh)
            dhr_ref[rows, :] = dh_ref[rows, :] + _rms_bwd(dh2, xh, r, gf)

    tile = pl.BlockSpec((tm, d), lambda i: (i, 0))
    aspec = pl.BlockSpec((nj, tm, fc), lambda i: (0, i, 0))
    wspec = _const_spec(w1.shape)

    def stages():
        i = pl.program_id(0)
        return i == 0, i == max(nt - 2, 0), i == nt - 1

    return _call(
        body, hosted, stages, grid=(nt,), name="ffn_bwd_down",
        in_specs=[aspec, aspec, tile, tile, _const_spec((1, d)), wspec, wspec],
        out_specs=[tile, pl.BlockSpec((SUBLANES, d), lambda i: (0, 0))],
        out_shape=[SDS((s, d), F32), SDS((SUBLANES, d), F32)],
        scratch_shapes=[], args=(da1, da3, dh, hres1, g_ffn, w1, w3), sem=("arbitrary",))


def _ffn_wgrad(h2, dhb, ff, da1, da3):
    s, d = h2.shape
    _, _, fc = ff.shape
    tm = min(TM_WGRAD, s)

    def body(h2_ref, dhb_ref, ff_ref, da1_ref, da3_ref, dw1_ref, dw3_ref, dw2_ref):
        @pl.when(pl.program_id(1) == 0)
        def _():
            dw1_ref[...] = jnp.zeros_like(dw1_ref)
            dw3_ref[...] = jnp.zeros_like(dw3_ref)
            dw2_ref[...] = jnp.zeros_like(dw2_ref)

        h2v = h2_ref[...]
        dw1_ref[0] += _dot_tn(da1_ref[0], h2v)
        dw3_ref[0] += _dot_tn(da3_ref[0], h2v)
        dw2_ref[0] += _dot_tn(ff_ref[0], dhb_ref[...])

    wspec = pl.BlockSpec((1, fc, d), lambda j, i: (j, 0, 0))
    return pl.pallas_call(
        body, grid=(N_CHIPS, s // tm), name="ffn_wgrad",
        in_specs=[pl.BlockSpec((tm, d), lambda j, i: (i, 0)), pl.BlockSpec((tm, d), lambda j, i: (i, 0))]
        + [pl.BlockSpec((1, tm, fc), lambda j, i: (j, i, 0))] * 3,
        out_specs=[wspec] * 3, out_shape=[SDS((N_CHIPS, fc, d), F32)] * 3,
        compiler_params=_cp(("parallel", "arbitrary")))(h2, dhb, ff, da1, da3)


def _mixer_bwd(u, saved, pooled, h, dhres1, sp_, w_out, hosted=None):
    s, din = u.shape
    d = dhres1.shape[1]
    lw = din // 3
    tm = min(TM_MIX, s)
    nt = s // tm
    nb = lw // GATE_BLOCK
    hd = sp_["gate_a_w"].shape[1]

    def body(ul_ref, saved_ref, pooled_ref, h_ref, hhalo_ref, dhr_ref, cw_ref, cb_ref, ga_ref, gx_ref, ba_ref,
             bx_ref, lam_ref, pw_ref, pb_ref, ps_ref, gl_ref, gp_ref, wout_ref, du_ref, slab_ref, dwout_ref,
             gw_s, a_s, b_s, e_s, ecarry_s, dxc_s, q_s, vec_s, cwacc_s, dgw_s, dpw_s):
        i = pl.program_id(0)
        tile = nt - 1 - i

        @pl.when(i == 0)
        def _():
            _build_gate_blocks(ga_ref, gx_ref, gw_s)
            for ref in (ecarry_s, dxc_s, q_s, vec_s, cwacc_s, dgw_s, dpw_s, dwout_ref):
                ref[...] = jnp.zeros_like(ref)

        cw = cw_ref[...]
        lam = lam_ref[...]
        ps = ps_ref[...]
        def sv_(name):
            if name in MIX_SAVED_BF16:
                return pooled_ref[MIX_SAVED_BF16.index(name)].astype(F32)
            return saved_ref[MIX_SAVED.index(name)]

        npool = len(MIX_SAVED_BF16)

        f = {"sp": _softplus_neg(lam)}
        row = lax.broadcasted_iota(jnp.int32, (HALO, LANES), 0) + tile * tm
        f["z"] = jnp.concatenate(
            [_dot(pooled_ref[npool, :, g * LANES:(g + 1) * LANES], pw_ref[g].astype(BF16))
             for g in range(len(POOL_WINDOWS))], axis=1) + pb_ref[...]
        f["y_pool"] = f["z"] * ps
        y_lru = h_ref[...] * sv_("ge")
        rl = lax.rsqrt(_rowmean(y_lru * y_lru) + EPS)
        yp = f["y_pool"]
        rp = lax.rsqrt(_rowmean(yp * yp) + EPS)
        xh_l = y_lru * rl
        xh_p = yp * rp

        dhrb = dhr_ref[...].astype(BF16)
        dyn = _dot_nt(dhrb, wout_ref[...])
        yn = jnp.concatenate([xh_l * gl_ref[...], xh_p * gp_ref[...]], axis=1).astype(BF16)
        dwout_ref[...] += _dot_tn(yn, dhrb)
        d_nl, d_np = dyn[:, :lw], dyn[:, lw:]
        vec = {}
        vec[ROW_GL] = _colsum8(d_nl * xh_l)
        vec[ROW_GP] = _colsum8(d_np * xh_p)
        d_ylru = _rms_bwd(d_nl, xh_l, rl, gl_ref[...])
        d_ypool = _rms_bwd(d_np, xh_p, rp, gp_ref[...])

        vec[ROW_PS] = _colsum8(d_ypool * f["z"])
        dz = d_ypool * ps
        vec[ROW_PB] = _colsum8(dz)
        dzb = dz.astype(BF16)
        dup = []
        for gi, w in enumerate(POOL_WINDOWS):
            sl = slice(gi * LANES, (gi + 1) * LANES)
            dpw_s[:, sl] += _dot_tn(pooled_ref[npool, :, sl], dzb[:, sl])
            dpool = _dot_nt(dzb[:, sl], pw_ref[gi].astype(BF16))
            q = jnp.concatenate([dpool[:HALO] * (1.0 / jnp.minimum(row + 1, w).astype(F32)),
                                 dpool[HALO:] * (1.0 / w)], axis=0)
            e = jnp.concatenate([q, q_s[:, sl]], axis=0)
            k = 1
            while k < w:
                e = e + pltpu.roll(e, tm + HALO - k, 0)
                k *= 2
            dup.append(e[:tm] - dpool)
            q_s[:, sl] = q[:HALO]

        d_hout = d_ylru * sv_("ge")
        d_ug = d_ylru * h_ref[...] * sv_("dge")
        a1, b1 = _scan_level1(sv_("a"), sv_("a") * d_hout, reverse=True)
        a_s[...] = a1
        b_s[...] = b1
        e_next = ecarry_s[...]
        ecarry_s[...] = _scan_level2(a_s, b_s, e_s, e_next, reverse=True)
        sv = d_hout + _shift_up(e_s[...], e_next, 1)
        d_a = sv * _shift_down(jnp.where(tile > 0, hhalo_ref[...], 0.0), h_ref[...], 1)
        mult = sv_("mult")
        ig = sv_("ig")
        d_mult = sv * (ig * sv_("xc"))
        d_ig = sv * mult * sv_("xc")
        d_xc = sv * mult * ig
        a = sv_("a")
        d_la = d_a * a + jnp.where(mult > 1.000001e-6, d_mult * (-(a * a) / mult), 0.0)
        d_r = d_la * (-LRU_C * f["sp"])
        r = sv_("r")
        vec[ROW_LAM] = _colsum8(d_la * (-LRU_C * r))
        d_pr = d_r * r * (1.0 - r)
        d_pi = d_ig * ig * (1.0 - ig)
        vec[ROW_BA] = _colsum8(d_pr)
        vec[ROW_BX] = _colsum8(d_pi)
        dxc_parts = []
        for b in range(nb):
            sl = slice(b * GATE_BLOCK, (b + 1) * GATE_BLOCK)
            rhs = jnp.concatenate([d_pr[:, sl], d_pi[:, sl]], axis=1).astype(BF16)
            dgw_s[b] += _dot_tn(saved_ref[MIX_SAVED.index("xc"), :, sl].astype(BF16), rhs)
            dxc_parts.append(_dot_nt(rhs, gw_s[b]))
        d_xc = d_xc + jnp.concatenate(dxc_parts, axis=1)
        vec[ROW_CONV_B] = _colsum8(d_xc)
        dxc_next = dxc_s[...]
        d_ul = None
        for k in range(CONV_WIDTH):
            ahead = _shift_up(d_xc, dxc_next, CONV_WIDTH - 1 - k)
            cwacc_s[k * SUBLANES:(k + 1) * SUBLANES, :] += _colsum8(ahead * ul_ref[...])
            term = ahead * cw[k:k + 1, :]
            d_ul = term if d_ul is None else d_ul + term
        dxc_s[...] = d_xc[:SUBLANES]
        for row, val in vec.items():
            vec_s[row * SUBLANES:(row + 1) * SUBLANES, :] += val
        du_ref[...] = jnp.concatenate([d_ul, d_ug] + dup, axis=1).astype(BF16)

        @pl.when(i == nt - 1)
        def _():
            rows = []
            for row in range(ROW_GA):
                if row in (ROW_CONV_W, ROW_CONV_W + 1, ROW_CONV_W + 2, ROW_CONV_W + 3):
                    k = row - ROW_CONV_W
                    v = jnp.sum(cwacc_s[k * SUBLANES:(k + 1) * SUBLANES, :], axis=0, keepdims=True)
                elif row <= ROW_GP:
                    v = jnp.sum(vec_s[row * SUBLANES:(row + 1) * SUBLANES, :], axis=0, keepdims=True)
                    if row == ROW_LAM:
                        v = v * (-1.0 / (1.0 + jnp.exp(lam)))
                else:
                    v = jnp.zeros((1, lw), F32)
                rows.append(v)
            slab_ref[0:ROW_GA, :] = jnp.concatenate(rows, axis=0)
            lane = lax.broadcasted_iota(jnp.int32, (hd, GATE_BLOCK), 1)
            for b in range(nb):
                for off, row0 in ((0, ROW_GA), (GATE_BLOCK, ROW_GX)):
                    acc = jnp.zeros((hd, GATE_BLOCK), F32)
                    for hh in range(GATE_BLOCK // hd):
                        m = (lane >= hh * hd) & (lane < (hh + 1) * hd)
                        acc = acc + jnp.where(m, dgw_s[b, hh * hd:(hh + 1) * hd, off:off + GATE_BLOCK], 0.0)
                    slab_ref[row0:row0 + hd, b * GATE_BLOCK:(b + 1) * GATE_BLOCK] = acc
            slab_ref[ROW_PW:ROW_PW + LANES, :] = dpw_s[...]

    small = [sp_[k] for k in ("conv_w", "conv_b", "gate_a_w", "gate_x_w", "gate_a_b", "gate_x_b", "lru_lambda",
                              "pool_w", "pool_b", "pool_scale", "norm_lru_g", "norm_pool_g")]
    rev = lambda i: nt - 1 - i

    def stages():
        i = pl.program_id(0)
        return i == 0, i == max(nt - 3, 0), i == nt - 1

    return _call(
        body, hosted, stages, grid=(nt,), name="mixer_bwd",
        in_specs=[pl.BlockSpec((tm, lw), lambda i: (rev(i), 0)),
                  pl.BlockSpec((len(MIX_SAVED), tm, lw), lambda i: (0, rev(i), 0)),
                  pl.BlockSpec((len(MIX_SAVED_BF16) + 1, tm, lw), lambda i: (0, rev(i), 0)),
                  pl.BlockSpec((tm, lw), lambda i: (rev(i), 0)),
                  pl.BlockSpec((SUBLANES, lw), lambda i: (jnp.maximum(rev(i) * (tm // SUBLANES) - 1, 0), 0)),
                  pl.BlockSpec((tm, d), lambda i: (rev(i), 0))]
        + [_const_spec(a.shape) for a in small] + [_const_spec(w_out.shape)],
        out_specs=[pl.BlockSpec((tm, din), lambda i: (rev(i), 0)),
                   pl.BlockSpec((MIX_SLAB_ROWS, SLAB_W), lambda i: (0, 0)), pl.BlockSpec((d, d), lambda i: (0, 0))],
        out_shape=[SDS((s, din), BF16), SDS((MIX_SLAB_ROWS, SLAB_W), F32), SDS((d, d), F32)],
        scratch_shapes=[pltpu.VMEM((nb, GATE_BLOCK, 2 * GATE_BLOCK), BF16),
                        pltpu.VMEM((tm, lw), F32), pltpu.VMEM((tm, lw), F32), pltpu.VMEM((tm, lw), F32),
                        pltpu.VMEM((SUBLANES, lw), F32), pltpu.VMEM((SUBLANES, lw), F32),
                        pltpu.VMEM((HALO, lw), F32), pltpu.VMEM((ROW_GA * SUBLANES, lw), F32),
                        pltpu.VMEM((CONV_WIDTH * SUBLANES, lw), F32),
                        pltpu.VMEM((nb, GATE_BLOCK, 2 * GATE_BLOCK), F32), pltpu.VMEM((LANES, lw), F32)],
        args=(u, saved, pooled, h, h, dhres1, *small, w_out), sem=("arbitrary",))


def _inproj_bwd(x, du, dhres1, g_mix, w_in, hosted=None):
    s, d = x.shape
    n = w_in.shape[1]
    nc = n // N_CHIPS
    tm = min(TM_PROJ, s)
    nt = s // tm

    def body(x_ref, du_ref, dhr_ref, g_ref, w_ref, gx_ref, dwin_ref, dg_ref):
        i = pl.program_id(0)

        @pl.when(i == 0)
        def _():
            dwin_ref[...] = jnp.zeros_like(dwin_ref)
            dg_ref[...] = jnp.zeros_like(dg_ref)

        xv = x_ref[...]
        g = g_ref[...]
        r = lax.rsqrt(_rowmean(xv * xv) + EPS)
        xh = xv * r
        h1 = (xh * g).astype(BF16)
        duv = du_ref[...]
        dh1 = _dot_nt(duv, w_ref[...])
        dg_ref[...] += _colsum8(dh1 * xh)
        gx_ref[...] = dhr_ref[...] + _rms_bwd(dh1, xh, r, g)
        for jj in range(N_CHIPS):
            dwin_ref[jj] += _dot_tn(h1, duv[:, jj * nc:(jj + 1) * nc])

    def stages():
        i = pl.program_id(0)
        return i == 0, i == max(nt - 3, 0), i == nt - 1

    return _call(
        body, hosted, stages, grid=(nt,), name="inproj_bwd",
        in_specs=[pl.BlockSpec((tm, d), lambda i: (i, 0)), pl.BlockSpec((tm, n), lambda i: (i, 0)),
                  pl.BlockSpec((tm, d), lambda i: (i, 0)), _const_spec((1, d)), _const_spec((d, n))],
        out_specs=[pl.BlockSpec((tm, d), lambda i: (i, 0)), pl.BlockSpec((N_CHIPS, d, nc), lambda i: (0, 0, 0)),
                   pl.BlockSpec((SUBLANES, d), lambda i: (0, 0))],
        out_shape=[SDS((s, d), F32), SDS((N_CHIPS, d, nc), F32), SDS((SUBLANES, d), F32)],
        scratch_shapes=[], args=(x, du, dhres1, g_mix, w_in), sem=("arbitrary",))


def _place():
    x, y, c = lax.axis_index("x"), lax.axis_index("y"), lax.axis_index("c")
    return x, y, c


def _other_chips(x, y):
    return [(1 - x, y), (x, 1 - y), (1 - x, 1 - y)]


ANY = pl.BlockSpec(memory_space=pl.ANY)
VMEM_SPEC = pl.BlockSpec(memory_space=pltpu.VMEM)

_GATHERED = {"w_in": "cols", "w_out": "major", "ffn_w1": "major", "ffn_w3": "major", "ffn_w2": "major"}
_BIG = ("w_in", "w_out", "ffn_w1", "ffn_w3", "ffn_w2")


def _gather_weights(shards, conv_w, n_remote):
    n = len(shards)
    full_shapes = []
    for name, sh in zip(_BIG, shards):
        r, cdim = sh.shape
        if _GATHERED[name] == "cols":
            assert cdim % LANES == 0
            full_shapes.append((r, cdim * N_CHIPS))
        else:
            full_shapes.append((N_CHIPS, r, cdim))

    def region(ref, name, sh, jj, cc):
        r, cdim = sh
        rows = pl.ds(0, r) if cc is None else pl.ds(pl.multiple_of(cc * (r // 2), 16), r // 2)
        if _GATHERED[name] == "cols":
            return ref.at[rows, pl.ds(pl.multiple_of(jj * cdim, LANES), cdim)]
        return ref.at[jj, rows, :]

    def staged(ref, sh, cc):
        r = sh[0]
        return ref.at[pl.ds(pl.multiple_of(cc * (r // 2), 16), r // 2), :]

    def body(*refs):
        ins, cw_in = refs[:n], refs[n]
        outs, cw_out = refs[n + 1:2 * n + 1], refs[2 * n + 1]
        stage = refs[2 * n + 2:3 * n + 2]
        cw_stage, lsem, ssem, rsem, fssem, frsem, cssem, crsem = refs[3 * n + 2:]
        x, y, c = _place()
        j = 2 * x + y
        chips = _other_chips(x, y)
        for w in range(n_remote):
            stage[w][...] = ins[w][...].astype(BF16)
        cw_stage[...] = jnp.zeros_like(cw_stage)
        cw_stage[0:CONV_WIDTH, :] = cw_in[...]
        shs = [s_.shape for s_ in shards]
        local = [pltpu.make_async_copy(stage[w], region(outs[w], _BIG[w], shs[w], j, None), lsem.at[w])
                 for w in range(n)]
        local.append(pltpu.make_async_copy(cw_stage, cw_out.at[j], lsem.at[n]))
        sends = []
        for k, (px, py) in enumerate(chips):
            for w in range(n_remote):
                sends.append(pltpu.make_async_remote_copy(
                    src_ref=staged(stage[w], shs[w], c), dst_ref=region(outs[w], _BIG[w], shs[w], j, c),
                    send_sem=ssem.at[k * n + w], recv_sem=rsem.at[k * n + w], device_id=(px, py, c),
                    device_id_type=MESH))
            sends.append(pltpu.make_async_remote_copy(
                src_ref=cw_stage, dst_ref=cw_out.at[j], send_sem=cssem.at[k], recv_sem=crsem.at[k],
                device_id=(px, py, c), device_id_type=MESH))
        for cp in sends:
            cp.start()
        for w in range(n_remote, n):
            stage[w][...] = ins[w][...].astype(BF16)
        for cp in local:
            cp.start()
        fwd = []
        for k, (px, py) in enumerate(chips):
            jk = 2 * px + py
            for w in range(n_remote):
                reg = region(outs[w], _BIG[w], shs[w], jk, c)
                pltpu.make_async_remote_copy(src_ref=reg, dst_ref=reg, send_sem=ssem.at[k * n + w],
                                             recv_sem=rsem.at[k * n + w], device_id=(px, py, c),
                                             device_id_type=MESH).wait_recv()
                cp = pltpu.make_async_remote_copy(src_ref=reg, dst_ref=reg, send_sem=fssem.at[k * n + w],
                                                  recv_sem=frsem.at[k * n + w], device_id=(x, y, 1 - c),
                                                  device_id_type=MESH)
                cp.start()
                fwd.append(cp)
            pltpu.make_async_remote_copy(src_ref=cw_stage, dst_ref=cw_out.at[jk], send_sem=cssem.at[k],
                                         recv_sem=crsem.at[k], device_id=(px, py, c),
                                         device_id_type=MESH).wait_recv()
        for k, (px, py) in enumerate(chips):
            jk = 2 * px + py
            for w in range(n_remote):
                reg = region(outs[w], _BIG[w], shs[w], jk, 1 - c)
                pltpu.make_async_remote_copy(src_ref=reg, dst_ref=reg, send_sem=fssem.at[k * n + w],
                                             recv_sem=frsem.at[k * n + w], device_id=(x, y, 1 - c),
                                             device_id_type=MESH).wait_recv()
        for cp in sends + fwd:
            cp.wait_send()
        for cp in local:
            cp.wait()

    nsem = 3 * n
    return pl.pallas_call(
        body, name="gather_first",
        in_specs=[VMEM_SPEC] * (n + 1), out_specs=[ANY] * (n + 1),
        out_shape=[SDS(fs, BF16) for fs in full_shapes] + [SDS((N_CHIPS, SUBLANES, LANES), F32)],
        scratch_shapes=[pltpu.VMEM(s_.shape, BF16) for s_ in shards] + [pltpu.VMEM((SUBLANES, LANES), F32)]
        + [pltpu.SemaphoreType.DMA((n + 1,))] + [pltpu.SemaphoreType.DMA((nsem,))] * 4
        + [pltpu.SemaphoreType.DMA((3,))] * 2,
        compiler_params=_cp())(*shards, conv_w)


def _start_all(make):
    def f(ins, outs, sems):
        for cp in make(ins, outs, sems):
            cp.start()
    return f


def _wait_all(make):
    def f(ins, outs, sems):
        for cp in make(ins, outs, sems):
            cp.wait()
    return f


def _ffn_gather_hosted(arrs):
    n = len(arrs)

    def make(outs, sems):
        ssem, rsem, fs, fr = sems
        x, y, c = _place()
        j = 2 * x + y

        def reg(w, jj, cc):
            hr = arrs[w].shape[1] // 2
            return outs[w].at[jj, pl.ds(pl.multiple_of(cc * hr, 16), hr), :]

        def rc(w, jj, cc, s_sem, r_sem, dev):
            return pltpu.make_async_remote_copy(src_ref=reg(w, jj, cc), dst_ref=reg(w, jj, cc), send_sem=s_sem,
                                                recv_sem=r_sem, device_id=dev, device_id_type=MESH)

        sends, recvs, fwds, frecvs = [], [], [], []
        for k, (px, py) in enumerate(_other_chips(x, y)):
            jk = 2 * px + py
            for w in range(n):
                q = k * n + w
                sends.append(rc(w, j, c, ssem.at[q], rsem.at[q], (px, py, c)))
                recvs.append(rc(w, jk, c, ssem.at[q], rsem.at[q], (px, py, c)))
                fwds.append(rc(w, jk, c, fs.at[q], fr.at[q], (x, y, 1 - c)))
                frecvs.append(rc(w, jk, 1 - c, fs.at[q], fr.at[q], (x, y, 1 - c)))
        return sends, recvs, fwds, frecvs

    def start(ins, outs, sems):
        for cp in make(outs, sems)[0]:
            cp.start()

    def mid(ins, outs, sems):
        _, recvs, fwds, _ = make(outs, sems)
        for r, f in zip(recvs, fwds):
            r.wait_recv()
            f.start()

    def finish(ins, outs, sems):
        sends, _, fwds, frecvs = make(outs, sems)
        for r in frecvs:
            r.wait_recv()
        for cp in sends + fwds:
            cp.wait_send()

    return _Hosted(arrs, [SDS(a.shape, a.dtype) for a in arrs], [3 * n] * 4, start, finish, mid=mid,
                   aliases={w: w for w in range(n)})


def _rs_sibling_hosted(arrs):
    n = len(arrs)

    def make(ins, outs, sems):
        x, y, c = _place()
        cps = []
        for w in range(n):
            hr = arrs[w].shape[1] // 2
            src = ins[w].at[:, pl.ds(pl.multiple_of((1 - c) * hr, SUBLANES), hr), :]
            cps.append(pltpu.make_async_remote_copy(src_ref=src, dst_ref=outs[w], send_sem=sems[0].at[w],
                                                    recv_sem=sems[1].at[w], device_id=(x, y, 1 - c),
                                                    device_id_type=MESH))
        return cps

    return _Hosted(arrs, [SDS((a.shape[0], a.shape[1] // 2, a.shape[2]), F32) for a in arrs], [n, n],
                   _start_all(make), _wait_all(make))


def _rs_chips_hosted(parts):
    n = len(parts)

    def make(ins, outs, sems):
        x, y, c = _place()
        j = 2 * x + y
        cps = []
        for k, (px, py) in enumerate(_other_chips(x, y)):
            jk = 2 * px + py
            for w in range(n):
                cps.append(pltpu.make_async_remote_copy(
                    src_ref=ins[w].at[jk], dst_ref=outs[w].at[j], send_sem=sems[0].at[k * n + w],
                    recv_sem=sems[1].at[k * n + w], device_id=(px, py, c), device_id_type=MESH))
        return cps

    return _Hosted(parts, [SDS(p.shape, p.dtype) for p in parts], [3 * n, 3 * n], _start_all(make), _wait_all(make))


def _rs_swap_hosted(halves):
    n = len(halves)

    def make(ins, outs, sems):
        x, y, c = _place()
        return [pltpu.make_async_remote_copy(src_ref=ins[w], dst_ref=outs[w], send_sem=sems[0].at[w],
                                             recv_sem=sems[1].at[w], device_id=(x, y, 1 - c), device_id_type=MESH)
                for w in range(n)]

    return _Hosted(halves, [SDS(h.shape, F32) for h in halves], [n, n], _start_all(make), _wait_all(make))


HBM_SPEC = pl.BlockSpec(memory_space=pltpu.HBM)
SEM_SPEC = pl.BlockSpec(memory_space=pltpu.SEMAPHORE)
_EFFECT = pltpu.SideEffectType.DATAFLOW_SIDE_EFFECTING


def _split_start(h, name):
    n_in, n_out, ns = len(h.ins), len(h.out_shapes), len(h.sems)
    ins = [pltpu.with_memory_space_constraint(a, pltpu.HBM) for a in h.ins]
    lands = [pltpu.with_memory_space_constraint(lax.empty(o.shape, o.dtype), pltpu.HBM) for o in h.out_shapes]

    def body(*refs):
        i_refs, l_refs = refs[:n_in], refs[n_in:n_in + n_out]
        s_refs = refs[n_in + n_out:n_in + n_out + ns]
        token = refs[-1]
        h.start(i_refs, l_refs, s_refs)
        token[...] = jnp.zeros_like(token)

    res = pl.pallas_call(
        body, name=name, in_specs=[HBM_SPEC] * (n_in + n_out),
        out_specs=[SEM_SPEC] * ns + [HBM_SPEC] * n_out + [VMEM_SPEC],
        out_shape=[pltpu.SemaphoreType.DMA((k,)) for k in h.sems]
        + [pltpu.HBM(o.shape, o.dtype) for o in h.out_shapes] + [SDS((SUBLANES, LANES), F32)],
        input_output_aliases={n_in + k: ns + k for k in range(n_out)},
        compiler_params=pltpu.CompilerParams(has_side_effects=_EFFECT))(*ins, *lands)
    return list(res[:ns]) + ins + list(res[ns:-1]), res[-1]


def _split_wait(h, state, after, name):
    n_in, n_out, ns = len(h.ins), len(h.out_shapes), len(h.sems)
    sems, bufs = state[:ns], state[ns:]

    def body(*refs):
        i_refs, l_refs = refs[:n_in], refs[n_in:n_in + n_out]
        s_refs = refs[n_in + n_out:n_in + n_out + ns]
        h.finish(i_refs, l_refs, s_refs)

    res = pl.pallas_call(
        body, name=name, in_specs=[HBM_SPEC] * (n_in + n_out) + [SEM_SPEC] * ns + [ANY],
        out_specs=[HBM_SPEC] * n_out,
        out_shape=[pltpu.HBM(b.shape, b.dtype) for b in bufs[n_in:]],
        input_output_aliases={n_in + k: k for k in range(n_out)},
        compiler_params=pltpu.CompilerParams(has_side_effects=_EFFECT))(*bufs, *sems, after)
    return list(res)


def _run_comm(hosted, name):
    return _call(lambda: None, hosted, None, name=name, grid=(), in_specs=[], out_specs=[], out_shape=[],
                 scratch_shapes=[], args=(), sem=None)[1]


def _row_tile(rows, cols, n_arrays):
    budget = 24 * 1024 * 1024 // (2 * 4 * n_arrays * cols)
    best = SUBLANES
    for t in range(SUBLANES, rows + 1, SUBLANES):
        if rows % t == 0 and t <= budget:
            best = t
    return best


def _place_index(which):
    x, y, c = _place()
    v = c if which == "c" else 2 * x + y
    return jnp.reshape(v, (1,)).astype(jnp.int32)


def _add_own_half(full, recv, name, wire=BF16):
    nsh, rows, cols = full.shape
    hr = rows // 2
    t = _row_tile(hr, cols, 4)
    nt = hr // t

    def body(p_ref, a_ref, b_ref, o_ref, ob_ref):
        v = a_ref[0] + b_ref[0]
        ob_ref[0] = v.astype(wire)

        @pl.when(pl.program_id(1) == nsh - 1)
        def _():
            o_ref[...] = v

    def shard(k, p_ref):
        return (p_ref[1] + 1 + k) % nsh

    half = pl.BlockSpec((1, t, cols), lambda i, k, p_ref: (shard(k, p_ref), i, 0))
    return pl.pallas_call(
        body, name=name,
        grid_spec=pltpu.PrefetchScalarGridSpec(
            num_scalar_prefetch=1, grid=(nt, nsh),
            in_specs=[pl.BlockSpec((1, t, cols), lambda i, k, p_ref: (shard(k, p_ref), p_ref[0] * nt + i, 0)), half],
            out_specs=[pl.BlockSpec((t, cols), lambda i, k, p_ref: (i, 0)), half]),
        out_shape=[SDS((hr, cols), F32), SDS((nsh, hr, cols), wire)],
        compiler_params=_cp(("parallel", "arbitrary")))(
            jnp.concatenate([_place_index("c"), _place_index("j")]), full, recv)


def _sum_chips(own, recv, name):
    nsh, hr, cols = recv.shape
    t = _row_tile(hr, cols, 6)

    def body(j_ref, own_ref, *rest):
        r_refs, o_ref = rest[:nsh], rest[nsh]
        j = j_ref[0]
        mine = own_ref[...]
        parts = [jnp.where(j == k, mine, r_refs[k][0].astype(F32)) for k in range(nsh)]
        o_ref[...] = ((parts[0] + parts[1]) + parts[2]) + parts[3]

    def other(k):
        return pl.BlockSpec((1, t, cols), lambda i, j_ref: (jnp.where(j_ref[0] == k, (k + 1) % nsh, k), i, 0))

    return pl.pallas_call(
        body, name=name,
        grid_spec=pltpu.PrefetchScalarGridSpec(
            num_scalar_prefetch=1, grid=(hr // t,),
            in_specs=[pl.BlockSpec((t, cols), lambda i, j_ref: (i, 0))] + [other(k) for k in range(nsh)],
            out_specs=pl.BlockSpec((t, cols), lambda i, j_ref: (i, 0))),
        out_shape=SDS((hr, cols), F32), compiler_params=_cp(("parallel",)))(_place_index("j"), own, *([recv] * nsh))


def _adamw_math(w, g, m, v):
    m = ADAM_B1 * m + (1.0 - ADAM_B1) * g
    v = ADAM_B2 * v + (1.0 - ADAM_B2) * (g * g)
    m_hat = m / (1.0 - ADAM_B1 ** ADAM_STEP)
    v_hat = v / (1.0 - ADAM_B2 ** ADAM_STEP)
    delta = -ADAM_LR * (m_hat / (jnp.sqrt(v_hat) + ADAM_EPS) + ADAM_WD * w)
    return delta, m, v


def _adamw_big(w, g_own, g_sib, m, v, name, token=None):
    _, rows, cols = w.shape
    hr = rows // 2
    t = _row_tile(hr, cols, 9)
    nth = hr // t
    if token is None:
        token = jnp.zeros((SUBLANES, LANES), F32)

    def body(c_ref, w_ref, go_ref, gs_ref, m_ref, v_ref, tok_ref, g_ref, d_ref, mo_ref, vo_ref):
        own = (pl.program_id(0) // nth) == c_ref[0]
        g = jnp.where(own, go_ref[...], gs_ref[...]) + tok_ref[0:1, 0:1]
        g_ref[0] = g
        d_ref[0], mo_ref[0], vo_ref[0] = _adamw_math(w_ref[0], g, m_ref[0], v_ref[0])

    spec = pl.BlockSpec((1, t, cols), lambda i, c_ref: (0, i, 0))
    hspec = pl.BlockSpec((t, cols), lambda i, c_ref: (i % nth, 0))
    tspec = pl.BlockSpec((SUBLANES, LANES), lambda i, c_ref: (0, 0))
    return pl.pallas_call(
        body, name=name,
        grid_spec=pltpu.PrefetchScalarGridSpec(
            num_scalar_prefetch=1, grid=(2 * nth,), in_specs=[spec, hspec, hspec, spec, spec, tspec],
            out_specs=[spec] * 4),
        out_shape=[SDS((1, rows, cols), F32)] * 4,
        compiler_params=_cp(("parallel",)))(_place_index("c"), w, g_own, g_sib, m, v, token)


def _build_slab(mix_slab, dg_mix, dg_ffn, dg_fin, loss8):
    def body(ms_ref, gm_ref, gf_ref, gn_ref, loss_ref, out_ref):
        rows = []
        for ref in (gm_ref, gf_ref, gn_ref):
            v = jnp.sum(ref[...], axis=0, keepdims=True)
            rows += [v[:, :SLAB_W], v[:, SLAB_W:]]
        rows.append(jnp.concatenate([loss_ref[0:1, :]] * (SLAB_W // LANES), axis=1))
        rows.append(jnp.zeros((SLAB_ROWS - ROW_LOSS - 1, SLAB_W), F32))
        tail = jnp.concatenate(rows, axis=0)
        for k in range(N_CHIPS):
            out_ref[k, 0:MIX_SLAB_ROWS, :] = ms_ref[...]
            out_ref[k, MIX_SLAB_ROWS:SLAB_ROWS, :] = tail

    return pl.pallas_call(
        body, name="build_slab", in_specs=[VMEM_SPEC] * 5, out_specs=VMEM_SPEC,
        out_shape=SDS((N_CHIPS, SLAB_ROWS, SLAB_W), F32),
        compiler_params=_cp())(mix_slab, dg_mix, dg_ffn, dg_fin, loss8)


_SMALL_ROWS = (("conv_b", ROW_CONV_B), ("gate_a_b", ROW_BA), ("gate_x_b", ROW_BX), ("lru_lambda", ROW_LAM),
               ("pool_b", ROW_PB), ("pool_scale", ROW_PS), ("norm_lru_g", ROW_GL), ("norm_pool_g", ROW_GP))
_WIDE_ROWS = (("norm_mix_g", ROW_MIX), ("norm_ffn_g", ROW_FFN), ("final_norm_g", ROW_FIN))
_BLOCK_ROWS = (("gate_a_w", ROW_GA), ("gate_x_w", ROW_GX), ("pool_w", ROW_PW))
_SMALL_ORDER = tuple(n for n, _ in _SMALL_ROWS) + tuple(n for n, _ in _WIDE_ROWS) + tuple(
    n for n, _ in _BLOCK_ROWS) + ("conv_w",)


def _adamw_small(slab_own, slab_sib, wmv):
    names = _SMALL_ORDER
    flat = [a for nme in names for a in wmv[nme]]
    nin = len(flat)

    def body(*refs):
        own_ref, sib_ref, j_ref = refs[0], refs[1], refs[2]
        ins = refs[3:3 + nin]
        outs = refs[3 + nin:-1]
        first = j_ref[1] == 0
        slab_ref = jnp.concatenate([jnp.where(first, own_ref[...], sib_ref[...]),
                                    jnp.where(first, sib_ref[...], own_ref[...])], axis=0)
        refs[-1][...] = jnp.broadcast_to(slab_ref[ROW_LOSS:ROW_LOSS + 1, 0:LANES], (SUBLANES, LANES))
        grads = {}
        for nme, row in _SMALL_ROWS:
            grads[nme] = slab_ref[row:row + 1, :]
        for nme, row in _WIDE_ROWS:
            grads[nme] = jnp.concatenate([slab_ref[row:row + 1, :], slab_ref[row + 1:row + 2, :]], axis=1)
        full = slab_ref[ROW_CONV_W:ROW_CONV_W + CONV_WIDTH, :]
        jv = j_ref[0]
        g = jnp.zeros((CONV_WIDTH, LANES), F32)
        for jj in range(N_CHIPS):
            g = jnp.where(jv == jj, full[:, jj * LANES:(jj + 1) * LANES], g)
        grads["conv_w"] = g
        block_rows = dict(_BLOCK_ROWS)
        for idx, nme in enumerate(names):
            w_ref, m_ref, v_ref = ins[3 * idx:3 * idx + 3]
            if nme in block_rows:
                nblk, r, c = w_ref.shape
                parts = [(b, slab_ref[block_rows[nme]:block_rows[nme] + r, b * c:(b + 1) * c]) for b in range(nblk)]
            else:
                parts = [(Ellipsis, grads[nme])]
            for b, g in parts:
                delta, m, v = _adamw_math(w_ref[b], g, m_ref[b], v_ref[b])
                outs[4 * idx][b] = g
                outs[4 * idx + 1][b] = delta
                outs[4 * idx + 2][b] = m
                outs[4 * idx + 3][b] = v

    place = jnp.concatenate([_place_index("j"), _place_index("c")])
    out_shape = [SDS(wmv[nme][0].shape, F32) for nme in names for _ in range(4)] + [SDS((SUBLANES, LANES), F32)]
    res = pl.pallas_call(
        body, name="adamw_small",
        in_specs=[VMEM_SPEC, VMEM_SPEC, pl.BlockSpec(memory_space=pltpu.SMEM)] + [VMEM_SPEC] * nin,
        out_specs=[VMEM_SPEC] * len(out_shape), out_shape=out_shape,
        compiler_params=_cp())(slab_own, slab_sib, place, *flat)
    return {nme: tuple(res[4 * idx:4 * idx + 4]) for idx, nme in enumerate(names)}, res[-1]


_FFN = ("ffn_w1", "ffn_w3", "ffn_w2")
_TRANSPOSED = ("ffn_w1", "ffn_w3")


def _local_step(x, target, full, sp_, distributed):
    d = x.shape[1]
    (u,), got = _inproj(x, sp_["norm_mix_g"], full["w_in"],
                        [_ffn_gather_hosted([full["w_out"]])] if distributed else None)
    w_out = (got[0][0] if distributed else full["w_out"]).reshape(d, d)
    gather = [_ffn_gather_hosted([full[n] for n in _FFN])] if distributed else None
    (h, hres1, saved, pooled), got = _mixer_fwd(u, x, sp_, w_out, gather)
    w1, w3, w2 = got[0] if distributed else [full[n] for n in _FFN]
    h2, a1, a3, ff = _ffn_up(hres1, sp_["norm_ffn_g"], w1, w3)
    dh, dhb, loss8, dg_fin = _ffn_down(ff, hres1, target, sp_["final_norm_g"], w2)
    da1, da3 = _ffn_bwd_gate(dhb, a1, a3, w2)
    dws = list(_ffn_wgrad(h2, dhb, ff, da1, da3))
    rs1 = [_rs_sibling_hosted(dws)] if distributed else None
    (dhres1, dg_ffn), got = _ffn_bwd_down(da1, da3, dh, hres1, sp_["norm_ffn_g"], w1, w3, rs1)
    rs2 = None
    if distributed:
        pairs = [_add_own_half(a, r, "add_half_" + n) for n, a, r in zip(_FFN, dws, got[0])]
        rs2 = [_rs_chips_hosted([pb for _, pb in pairs])]
    (du, mix_slab, dwout), got = _mixer_bwd(u, saved, pooled, h, dhres1, sp_, w_out, rs2)
    g_mix = sp_["norm_mix_g"]
    if distributed:
        fin = [_sum_chips(pairs[k][0], got[0][k], "sum_chips_" + n) for k, n in enumerate(_FFN)]
        swap = _rs_swap_hosted(fin)
        state, token = _split_start(swap, "ffn_swap_start")
        g_mix = g_mix + token[0:1, 0:1]
    (gx, dwin, dg_mix), _ = _inproj_bwd(x, du, dhres1, g_mix, full["w_in"])
    if distributed:
        sib = _split_wait(swap, state, dg_mix, "ffn_swap_wait")
    big = {"w_in": dwin, "w_out": dwout.reshape(N_CHIPS, d // N_CHIPS, d)}
    for k, n in enumerate(_FFN):
        big[n] = (fin[k], sib[k]) if distributed else dws[k]
    return gx, big, (mix_slab, dg_mix, dg_ffn, dg_fin, loss8)


_SMALL_LAYOUT = {
    "gate_a_w": (lambda a: a[0], lambda a: a[None]),
    "gate_x_w": (lambda a: a[0], lambda a: a[None]),
    "pool_w": (lambda a: a[0], lambda a: a[None]),
    "conv_w": (lambda a: a[0], lambda a: a[None]),
    "final_norm_g": (lambda a: a[None], lambda a: a[0]),
}

_WEIGHTS = ("norm_mix_g", "w_in", "conv_w", "conv_b", "gate_a_w", "gate_a_b", "gate_x_w", "gate_x_b", "lru_lambda",
            "pool_w", "pool_b", "pool_scale", "norm_lru_g", "norm_pool_g", "w_out", "norm_ffn_g", "ffn_w1",
            "ffn_w3", "ffn_w2", "final_norm_g")


def kernel(x, norm_mix_g, w_in, conv_w, conv_b, gate_a_w, gate_a_b, gate_x_w, gate_x_b, lru_lambda, pool_w, pool_b, pool_scale, norm_lru_g, norm_pool_g, w_out, norm_ffn_g, ffn_w1, ffn_w3, ffn_w2, final_norm_g, loss_target, m_norm_mix_g, m_w_in, m_conv_w, m_conv_b, m_gate_a_w, m_gate_a_b, m_gate_x_w, m_gate_x_b, m_lru_lambda, m_pool_w, m_pool_b, m_pool_scale, m_norm_lru_g, m_norm_pool_g, m_w_out, m_norm_ffn_g, m_ffn_w1, m_ffn_w3, m_ffn_w2, m_final_norm_g, v_norm_mix_g, v_w_in, v_conv_w, v_conv_b, v_gate_a_w, v_gate_a_b, v_gate_x_w, v_gate_x_b, v_lru_lambda, v_pool_w, v_pool_b, v_pool_scale, v_norm_lru_g, v_norm_pool_g, v_w_out, v_norm_ffn_g, v_ffn_w1, v_ffn_w3, v_ffn_w2, v_final_norm_g):
    loc = locals()
    w = {n: loc[n] for n in _WEIGHTS}
    m = {n: loc["m_" + n] for n in _WEIGHTS}
    v = {n: loc["v_" + n] for n in _WEIGHTS}

    def lay(nme, a):
        return _SMALL_LAYOUT[nme][0](a) if nme in _SMALL_LAYOUT else a

    def unlay(nme, a):
        return _SMALL_LAYOUT[nme][1](a) if nme in _SMALL_LAYOUT else a

    for group in (w, m, v):
        for n in _TRANSPOSED:
            group[n] = jnp.transpose(group[n], (0, 2, 1))

    gathered = _gather_weights([w[n][0] for n in _BIG], w["conv_w"][0], n_remote=1)
    full = dict(zip(_BIG, gathered[:-1]))
    cw_all = gathered[-1]
    sp_ = {n: lay(n, w[n]) for n in _SMALL_ORDER}
    sp_["conv_w"] = jnp.transpose(cw_all[:, :CONV_WIDTH, :], (1, 0, 2)).reshape(CONV_WIDTH, N_CHIPS * LANES)

    gx, big, small = _local_step(x[0], loss_target[0], full, sp_, distributed=True)

    late = ("w_in", "w_out", "slab")
    big["slab"] = _build_slab(*small)
    fin = {n: big[n][0] for n in _FFN}
    sib = {n: big[n][1] for n in _FFN}
    recv1, = _run_comm([_rs_sibling_hosted([big[n] for n in late])], "tail_sibling")
    pairs = [_add_own_half(big[n], r, "add_half_" + n, F32 if n == "slab" else BF16) for n, r in zip(late, recv1)]
    chips = _rs_chips_hosted([pb for _, pb in pairs])
    state, token = _split_start(chips, "tail_chips_start")
    out = {}
    for n in _FFN:
        out[n] = tuple(_adamw_big(w[n], fin[n], sib[n], m[n], v[n], "adamw_" + n, token))
    recv2 = _split_wait(chips, state, out[_FFN[-1]][1], "tail_chips_wait")
    for n, (p, _), r in zip(late, pairs, recv2):
        fin[n] = _sum_chips(p, r, "sum_chips_" + n)
    swapped, = _run_comm([_rs_swap_hosted([fin[n] for n in late])], "tail_swap")
    sib.update(zip(late, swapped))
    for n in late[:2]:
        out[n] = tuple(_adamw_big(w[n], fin[n], sib[n], m[n], v[n], "adamw_" + n))
    for n in _TRANSPOSED:
        out[n] = tuple(jnp.transpose(a, (0, 2, 1)) for a in out[n])
    wmv = {n: (lay(n, w[n]), lay(n, m[n]), lay(n, v[n])) for n in _SMALL_ORDER}
    res, loss = _adamw_small(fin["slab"], sib["slab"], wmv)
    for n in _SMALL_ORDER:
        out[n] = tuple(unlay(n, a) for a in res[n])
    return (loss[0, 0], gx[None]) + tuple(out[n][k] for k in range(4) for n in _WEIGHTS)
```
